```python
import jax, jax.numpy as jnp
from jax import lax
import numpy as np

D_MODEL = 1024
BATCH = 32
SEQ = 2048
DEPTH = 1

GRID_W = 64
CTX_LEN = 256
MLA_HEADS = 8
QK_NOPE = 64
QK_ROPE = 32
QK_HEAD = QK_NOPE + QK_ROPE
V_HEAD = 64
Q_LORA = 256
KV_LORA = 128
AXIS_DIM = QK_ROPE // 2
ROPE_BASE = 10000.0
Q_BLOCK = 128
MLA_WIDTH = MLA_HEADS * V_HEAD
GMLP_GROUPS = 8
GMLP_GROUP_DIM = 64
GMLP_WIDTH = GMLP_GROUPS * GMLP_GROUP_DIM
CHUNK = 128
D_MIX = MLA_WIDTH + GMLP_WIDTH
KV_COLS = KV_LORA + QK_ROPE
Q_START = KV_COLS
U_START = KV_COLS + Q_LORA
V_START = U_START + GMLP_WIDTH
IN_COLS = V_START + GMLP_WIDTH
D_FF = 2816
N_MOD = 9
EPS = 1e-6

kernel_name = "hymba_mla_gmlp_macaron_dit_layer"


def rms_norm(x, w):
    xf = x.astype(jnp.float32)
    y = xf * lax.rsqrt(jnp.mean(xf * xf, axis=-1, keepdims=True) + EPS)
    return (y * w.astype(jnp.float32)).astype(x.dtype)


def modulate(h, shift, scale):
    return h * (1 + scale) + shift


def swiglu(h, w1, w3, w2):
    return (jax.nn.silu(h @ w1) * (h @ w3)) @ w2


def ffn_sublayer(h_in, shift, scale, gate, norm_w, w1, w3, w2):
    h = modulate(rms_norm(h_in, norm_w), shift, scale)
    return h_in + 0.5 * gate * swiglu(h, w1, w3, w2)


def axial_rope(x, cos, sin):
    xr = x.reshape(x.shape[:-1] + (2, 2, AXIS_DIM // 2))
    rot = jnp.stack([-xr[..., 1, :], xr[..., 0, :]], axis=-2).reshape(x.shape)
    return x * cos[:, None, :] + rot * sin[:, None, :]


def rope_part(x, rope):
    if rope is None:
        return x
    return jnp.concatenate([x[..., :QK_NOPE], axial_rope(x[..., QK_NOPE:], *rope)], axis=-1)


def mla_keys_values(kv_proj, kv_a_norm_w, w_ukv, k_norm_w, rope):
    B, S, _ = kv_proj.shape
    c_kv = rms_norm(kv_proj[..., :KV_LORA], kv_a_norm_w)
    k_pe = kv_proj[..., KV_LORA:]
    kv = (c_kv @ w_ukv).reshape(B, S, MLA_HEADS, QK_NOPE + V_HEAD)
    k_nope, v = kv[..., :QK_NOPE], kv[..., QK_NOPE:]
    k_pe = jnp.broadcast_to(k_pe[:, :, None, :], (B, S, MLA_HEADS, QK_ROPE))
    k = rms_norm(jnp.concatenate([k_nope, k_pe], axis=-1), k_norm_w)
    return rope_part(k, rope), v


def mla_queries(q_proj, q_a_norm_w, w_uq, q_norm_w, rope):
    B, S, _ = q_proj.shape
    c_q = rms_norm(q_proj, q_a_norm_w)
    q = (c_q @ w_uq).reshape(B, S, MLA_HEADS, QK_HEAD)
    return rope_part(rms_norm(q, q_norm_w), rope)


def block_attention(q, k_all, v_all):
    B, S, H, Dk = q.shape
    nb = S // Q_BLOCK
    scale = Dk ** -0.5
    qb = jnp.moveaxis(q.reshape(B, nb, Q_BLOCK, H, Dk), 1, 0)

    def one_block(q_blk):
        s = jnp.einsum('bqhd,bkhd->bhqk', q_blk, k_all).astype(jnp.float32) * scale
        p = jax.nn.softmax(s, axis=-1).astype(v_all.dtype)
        return jnp.einsum('bhqk,bkhd->bqhd', p, v_all)

    out = lax.map(one_block, qb)
    return jnp.moveaxis(out, 0, 1).reshape(B, S, H * V_HEAD)


def chunk_gmlp(u, v, v_norm_w, w_s, b_s):
    B, S, _ = u.shape
    n = S // CHUNK
    u = jax.nn.gelu(u).reshape(B, n, CHUNK, GMLP_GROUPS, GMLP_GROUP_DIM)
    v = rms_norm(jax.nn.gelu(v).reshape(B, n, CHUNK, GMLP_GROUPS, GMLP_GROUP_DIM), v_norm_w)
    s = jnp.einsum('gpq,bnqgc->bnpgc', w_s, v) + b_s.T[:, :, None]
    return (u * s).reshape(B, S, GMLP_WIDTH)


def token_mix(proj, k_all, v_all, rope, q_a_norm_w, w_uq, q_norm_w, v_norm_w, w_s, b_s, w_out):
    q = mla_queries(proj[..., Q_START:U_START], q_a_norm_w, w_uq, q_norm_w, rope)
    attn = block_attention(q, k_all, v_all)
    sg = chunk_gmlp(proj[..., U_START:V_START], proj[..., V_START:], v_norm_w, w_s, b_s)
    return jnp.concatenate([attn, sg], axis=-1) @ w_out


def hybrid_layer(x, ctx, c, c_ctx, cos, sin,
                 w_ada, b_ada, norm1_w, ffn1_w1, ffn1_w3, ffn1_w2,
                 norm2_w, w_in, q_a_norm_w, w_uq, kv_a_norm_w, w_ukv, q_norm_w, k_norm_w,
                 v_norm_w, w_s, b_s, w_out,
                 norm3_w, ffn2_w1, ffn2_w3, ffn2_w2, update_ctx):
    mx = jnp.split((jax.nn.silu(c) @ w_ada + b_ada)[:, None, :], N_MOD, axis=-1)
    mc = jnp.split((jax.nn.silu(c_ctx) @ w_ada + b_ada)[None, None, :], N_MOD, axis=-1)
    rope = (cos, sin)

    x = ffn_sublayer(x, mx[0], mx[1], mx[2], norm1_w, ffn1_w1, ffn1_w3, ffn1_w2)
    ctx = ffn_sublayer(ctx, mc[0], mc[1], mc[2], norm1_w, ffn1_w1, ffn1_w3, ffn1_w2)

    proj = modulate(rms_norm(x, norm2_w), mx[3], mx[4]) @ w_in
    hc = modulate(rms_norm(ctx, norm2_w), mc[3], mc[4])
    proj_c = hc @ (w_in if update_ctx else w_in[:, :KV_COLS])
    k_lat, v_lat = mla_keys_values(proj[..., :KV_COLS], kv_a_norm_w, w_ukv, k_norm_w, rope)
    k_ctx, v_ctx = mla_keys_values(proj_c[..., :KV_COLS], kv_a_norm_w, w_ukv, k_norm_w, None)
    k_all = jnp.concatenate([k_lat, k_ctx], axis=1)
    v_all = jnp.concatenate([v_lat, v_ctx], axis=1)
    x = x + mx[5] * token_mix(proj, k_all, v_all, rope, q_a_norm_w, w_uq, q_norm_w,
                              v_norm_w, w_s, b_s, w_out)
    if update_ctx:
        ctx = ctx + mc[5] * token_mix(proj_c, k_ctx, v_ctx, None, q_a_norm_w, w_uq, q_norm_w,
                                      v_norm_w, w_s, b_s, w_out)
        ctx = ffn_sublayer(ctx, mc[6], mc[7], mc[8], norm3_w, ffn2_w1, ffn2_w3, ffn2_w2)

    x = ffn_sublayer(x, mx[6], mx[7], mx[8], norm3_w, ffn2_w1, ffn2_w3, ffn2_w2)
    return x, ctx


def _fwd_setup_inputs(seed: int = 0) -> dict:
    key = jax.random.key(seed)
    ks = jax.random.split(key, 26)
    f32 = jnp.float32

    def dense(k, shape, fan_in, gain=1.0):
        return jax.random.normal(k, shape, f32) * (gain * fan_in ** -0.5)

    def gain_vec(k, shape):
        return 1.0 + 0.02 * jax.random.normal(k, shape, f32)

    L = DEPTH
    return {
        "x": jax.random.normal(ks[0], (BATCH, SEQ, D_MODEL), f32),
        "c": jax.random.normal(ks[1], (BATCH, D_MODEL), f32),
        "ctx": jax.random.normal(ks[2], (BATCH, CTX_LEN, D_MODEL), f32),
        "c_ctx": jax.random.normal(ks[3], (D_MODEL,), f32),
        "w_ada": dense(ks[4], (L, D_MODEL, N_MOD * D_MODEL), D_MODEL, 0.5),
        "b_ada": 0.02 * jax.random.normal(ks[5], (L, N_MOD * D_MODEL), f32),
        "norm1_w": gain_vec(ks[6], (L, D_MODEL)),
        "ffn1_w1": dense(ks[7], (L, D_MODEL, D_FF), D_MODEL),
        "ffn1_w3": dense(ks[8], (L, D_MODEL, D_FF), D_MODEL),
        "ffn1_w2": dense(ks[9], (L, D_FF, D_MODEL), D_FF),
        "norm2_w": gain_vec(ks[10], (L, D_MODEL)),
        "w_in": dense(ks[11], (L, D_MODEL, IN_COLS), D_MODEL),
        "q_a_norm_w": gain_vec(ks[12], (L, Q_LORA)),
        "w_uq": dense(ks[13], (L, Q_LORA, MLA_HEADS * QK_HEAD), Q_LORA),
        "kv_a_norm_w": gain_vec(ks[14], (L, KV_LORA)),
        "w_ukv": dense(ks[15], (L, KV_LORA, MLA_HEADS * (QK_NOPE + V_HEAD)), KV_LORA),
        "q_norm_w": gain_vec(ks[16], (L, QK_HEAD)),
        "k_norm_w": gain_vec(ks[17], (L, QK_HEAD)),
        "v_norm_w": gain_vec(ks[18], (L, GMLP_GROUPS, GMLP_GROUP_DIM)),
        "w_s": dense(ks[19], (L, GMLP_GROUPS, CHUNK, CHUNK), CHUNK),
        "b_s": gain_vec(ks[20], (L, GMLP_GROUPS, CHUNK)),
        "w_out": dense(ks[21], (L, D_MIX, D_MODEL), D_MIX),
        "norm3_w": gain_vec(ks[22], (L, D_MODEL)),
        "ffn2_w1": dense(ks[23], (L, D_MODEL, D_FF), D_MODEL),
        "ffn2_w3": dense(ks[24], (L, D_MODEL, D_FF), D_MODEL),
        "ffn2_w2": dense(ks[25], (L, D_FF, D_MODEL), D_FF),
    }


def _fwd_reference(x, c, ctx, c_ctx, w_ada, b_ada, norm1_w, ffn1_w1, ffn1_w3, ffn1_w2,
              norm2_w, w_in, q_a_norm_w, w_uq, kv_a_norm_w, w_ukv, q_norm_w, k_norm_w,
              v_norm_w, w_s, b_s, w_out, norm3_w, ffn2_w1, ffn2_w3, ffn2_w2):
    S = x.shape[1]
    ROWS = S // GRID_W
    f32 = jnp.float32
    rows = jnp.repeat(jnp.arange(ROWS, dtype=f32), GRID_W)
    cols = jnp.tile(jnp.arange(GRID_W, dtype=f32), ROWS)
    inv = ROPE_BASE ** (-jnp.arange(0, AXIS_DIM, 2, dtype=f32) / AXIS_DIM)
    ang_r = rows[:, None] * inv
    ang_c = cols[:, None] * inv
    ang = jnp.concatenate([ang_r, ang_r, ang_c, ang_c], axis=-1)
    cos = jnp.cos(ang).astype(x.dtype)
    sin = jnp.sin(ang).astype(x.dtype)

    layer_weights = (w_ada, b_ada, norm1_w, ffn1_w1, ffn1_w3, ffn1_w2,
                     norm2_w, w_in, q_a_norm_w, w_uq, kv_a_norm_w, w_ukv, q_norm_w, k_norm_w,
                     v_norm_w, w_s, b_s, w_out, norm3_w, ffn2_w1, ffn2_w3, ffn2_w2)
    for i in range(DEPTH):
        x, ctx = hybrid_layer(x, ctx, c, c_ctx, cos, sin, *[w[i] for w in layer_weights],
                              update_ctx=i < DEPTH - 1)
    return x


import jax as _jax
import jax.numpy as _jnp

TWIN_FORMAT = 'train_step'
FWD_PARAMS = ['x', 'c', 'ctx', 'c_ctx', 'w_ada', 'b_ada', 'norm1_w', 'ffn1_w1', 'ffn1_w3', 'ffn1_w2', 'norm2_w', 'w_in', 'q_a_norm_w', 'w_uq', 'kv_a_norm_w', 'w_ukv', 'q_norm_w', 'k_norm_w', 'v_norm_w', 'w_s', 'b_s', 'w_out', 'norm3_w', 'ffn2_w1', 'ffn2_w3', 'ffn2_w2']
TWIN_WEIGHTS = ['c_ctx', 'w_ada', 'b_ada', 'norm1_w', 'ffn1_w1', 'ffn1_w3', 'ffn1_w2', 'norm2_w', 'w_in', 'q_a_norm_w', 'w_uq', 'kv_a_norm_w', 'w_ukv', 'q_norm_w', 'k_norm_w', 'v_norm_w', 'w_s', 'b_s', 'w_out', 'norm3_w', 'ffn2_w1', 'ffn2_w3', 'ffn2_w2']
TWIN_DIFF_INPUT = 'x'
TWIN_INPUTS = ['x', 'c', 'ctx', 'c_ctx', 'w_ada', 'b_ada', 'norm1_w', 'ffn1_w1', 'ffn1_w3', 'ffn1_w2', 'norm2_w', 'w_in', 'q_a_norm_w', 'w_uq', 'kv_a_norm_w', 'w_ukv', 'q_norm_w', 'k_norm_w', 'v_norm_w', 'w_s', 'b_s', 'w_out', 'norm3_w', 'ffn2_w1', 'ffn2_w3', 'ffn2_w2', 'loss_target', 'm_c_ctx', 'm_w_ada', 'm_b_ada', 'm_norm1_w', 'm_ffn1_w1', 'm_ffn1_w3', 'm_ffn1_w2', 'm_norm2_w', 'm_w_in', 'm_q_a_norm_w', 'm_w_uq', 'm_kv_a_norm_w', 'm_w_ukv', 'm_q_norm_w', 'm_k_norm_w', 'm_v_norm_w', 'm_w_s', 'm_b_s', 'm_w_out', 'm_norm3_w', 'm_ffn2_w1', 'm_ffn2_w3', 'm_ffn2_w2', 'v_c_ctx', 'v_w_ada', 'v_b_ada', 'v_norm1_w', 'v_ffn1_w1', 'v_ffn1_w3', 'v_ffn1_w2', 'v_norm2_w', 'v_w_in', 'v_q_a_norm_w', 'v_w_uq', 'v_kv_a_norm_w', 'v_w_ukv', 'v_q_norm_w', 'v_k_norm_w', 'v_v_norm_w', 'v_w_s', 'v_b_s', 'v_w_out', 'v_norm3_w', 'v_ffn2_w1', 'v_ffn2_w3', 'v_ffn2_w2']
TWIN_OUTPUTS = ['loss', 'grad_x', 'grad_c_ctx', 'grad_w_ada', 'grad_b_ada', 'grad_norm1_w', 'grad_ffn1_w1', 'grad_ffn1_w3', 'grad_ffn1_w2', 'grad_norm2_w', 'grad_w_in', 'grad_q_a_norm_w', 'grad_w_uq', 'grad_kv_a_norm_w', 'grad_w_ukv', 'grad_q_norm_w', 'grad_k_norm_w', 'grad_v_norm_w', 'grad_w_s', 'grad_b_s', 'grad_w_out', 'grad_norm3_w', 'grad_ffn2_w1', 'grad_ffn2_w3', 'grad_ffn2_w2', 'delta_c_ctx', 'delta_w_ada', 'delta_b_ada', 'delta_norm1_w', 'delta_ffn1_w1', 'delta_ffn1_w3', 'delta_ffn1_w2', 'delta_norm2_w', 'delta_w_in', 'delta_q_a_norm_w', 'delta_w_uq', 'delta_kv_a_norm_w', 'delta_w_ukv', 'delta_q_norm_w', 'delta_k_norm_w', 'delta_v_norm_w', 'delta_w_s', 'delta_b_s', 'delta_w_out', 'delta_norm3_w', 'delta_ffn2_w1', 'delta_ffn2_w3', 'delta_ffn2_w2', 'new_m_c_ctx', 'new_m_w_ada', 'new_m_b_ada', 'new_m_norm1_w', 'new_m_ffn1_w1', 'new_m_ffn1_w3', 'new_m_ffn1_w2', 'new_m_norm2_w', 'new_m_w_in', 'new_m_q_a_norm_w', 'new_m_w_uq', 'new_m_kv_a_norm_w', 'new_m_w_ukv', 'new_m_q_norm_w', 'new_m_k_norm_w', 'new_m_v_norm_w', 'new_m_w_s', 'new_m_b_s', 'new_m_w_out', 'new_m_norm3_w', 'new_m_ffn2_w1', 'new_m_ffn2_w3', 'new_m_ffn2_w2', 'new_v_c_ctx', 'new_v_w_ada', 'new_v_b_ada', 'new_v_norm1_w', 'new_v_ffn1_w1', 'new_v_ffn1_w3', 'new_v_ffn1_w2', 'new_v_norm2_w', 'new_v_w_in', 'new_v_q_a_norm_w', 'new_v_w_uq', 'new_v_kv_a_norm_w', 'new_v_w_ukv', 'new_v_q_norm_w', 'new_v_k_norm_w', 'new_v_v_norm_w', 'new_v_w_s', 'new_v_b_s', 'new_v_w_out', 'new_v_norm3_w', 'new_v_ffn2_w1', 'new_v_ffn2_w3', 'new_v_ffn2_w2']
TWIN_LEAF_KINDS = {'loss': 'loss', 'grad_x': 'grad_x', 'grad_c_ctx': 'grad_w', 'grad_w_ada': 'grad_w', 'grad_b_ada': 'grad_w', 'grad_norm1_w': 'grad_w', 'grad_ffn1_w1': 'grad_w', 'grad_ffn1_w3': 'grad_w', 'grad_ffn1_w2': 'grad_w', 'grad_norm2_w': 'grad_w', 'grad_w_in': 'grad_w', 'grad_q_a_norm_w': 'grad_w', 'grad_w_uq': 'grad_w', 'grad_kv_a_norm_w': 'grad_w', 'grad_w_ukv': 'grad_w', 'grad_q_norm_w': 'grad_w', 'grad_k_norm_w': 'grad_w', 'grad_v_norm_w': 'grad_w', 'grad_w_s': 'grad_w', 'grad_b_s': 'grad_w', 'grad_w_out': 'grad_w', 'grad_norm3_w': 'grad_w', 'grad_ffn2_w1': 'grad_w', 'grad_ffn2_w3': 'grad_w', 'grad_ffn2_w2': 'grad_w', 'delta_c_ctx': 'delta_w', 'delta_w_ada': 'delta_w', 'delta_b_ada': 'delta_w', 'delta_norm1_w': 'delta_w', 'delta_ffn1_w1': 'delta_w', 'delta_ffn1_w3': 'delta_w', 'delta_ffn1_w2': 'delta_w', 'delta_norm2_w': 'delta_w', 'delta_w_in': 'delta_w', 'delta_q_a_norm_w': 'delta_w', 'delta_w_uq': 'delta_w', 'delta_kv_a_norm_w': 'delta_w', 'delta_w_ukv': 'delta_w', 'delta_q_norm_w': 'delta_w', 'delta_k_norm_w': 'delta_w', 'delta_v_norm_w': 'delta_w', 'delta_w_s': 'delta_w', 'delta_b_s': 'delta_w', 'delta_w_out': 'delta_w', 'delta_norm3_w': 'delta_w', 'delta_ffn2_w1': 'delta_w', 'delta_ffn2_w3': 'delta_w', 'delta_ffn2_w2': 'delta_w', 'new_m_c_ctx': 'new_m', 'new_m_w_ada': 'new_m', 'new_m_b_ada': 'new_m', 'new_m_norm1_w': 'new_m', 'new_m_ffn1_w1': 'new_m', 'new_m_ffn1_w3': 'new_m', 'new_m_ffn1_w2': 'new_m', 'new_m_norm2_w': 'new_m', 'new_m_w_in': 'new_m', 'new_m_q_a_norm_w': 'new_m', 'new_m_w_uq': 'new_m', 'new_m_kv_a_norm_w': 'new_m', 'new_m_w_ukv': 'new_m', 'new_m_q_norm_w': 'new_m', 'new_m_k_norm_w': 'new_m', 'new_m_v_norm_w': 'new_m', 'new_m_w_s': 'new_m', 'new_m_b_s': 'new_m', 'new_m_w_out': 'new_m', 'new_m_norm3_w': 'new_m', 'new_m_ffn2_w1': 'new_m', 'new_m_ffn2_w3': 'new_m', 'new_m_ffn2_w2': 'new_m', 'new_v_c_ctx': 'new_v', 'new_v_w_ada': 'new_v', 'new_v_b_ada': 'new_v', 'new_v_norm1_w': 'new_v', 'new_v_ffn1_w1': 'new_v', 'new_v_ffn1_w3': 'new_v', 'new_v_ffn1_w2': 'new_v', 'new_v_norm2_w': 'new_v', 'new_v_w_in': 'new_v', 'new_v_q_a_norm_w': 'new_v', 'new_v_w_uq': 'new_v', 'new_v_kv_a_norm_w': 'new_v', 'new_v_w_ukv': 'new_v', 'new_v_q_norm_w': 'new_v', 'new_v_k_norm_w': 'new_v', 'new_v_v_norm_w': 'new_v', 'new_v_w_s': 'new_v', 'new_v_b_s': 'new_v', 'new_v_w_out': 'new_v', 'new_v_norm3_w': 'new_v', 'new_v_ffn2_w1': 'new_v', 'new_v_ffn2_w3': 'new_v', 'new_v_ffn2_w2': 'new_v'}


def _forward(args):
    return _fwd_reference(*[args[k] for k in FWD_PARAMS])


def _output_shape():
    out = _jax.eval_shape(lambda: _forward(_fwd_setup_inputs(0)))
    return out.shape, out.dtype

N_MICROBATCH = 1
ADAM_LR = 0.001
ADAM_B1 = 0.9
ADAM_B2 = 0.999
ADAM_EPS = 1e-08
ADAM_WD = 0.01
ADAM_STEP = 10
PER_EXAMPLE_BATCH_AXIS = {'x': 0, 'c': 0, 'ctx': 0, 'loss_target': 0}
SHARED_INPUTS = []
_WEIGHT_DTYPES = {'c_ctx': _jnp.float32, 'w_ada': _jnp.float32, 'b_ada': _jnp.float32, 'norm1_w': _jnp.float32, 'ffn1_w1': _jnp.float32, 'ffn1_w3': _jnp.float32, 'ffn1_w2': _jnp.float32, 'norm2_w': _jnp.float32, 'w_in': _jnp.float32, 'q_a_norm_w': _jnp.float32, 'w_uq': _jnp.float32, 'kv_a_norm_w': _jnp.float32, 'w_ukv': _jnp.float32, 'q_norm_w': _jnp.float32, 'k_norm_w': _jnp.float32, 'v_norm_w': _jnp.float32, 'w_s': _jnp.float32, 'b_s': _jnp.float32, 'w_out': _jnp.float32, 'norm3_w': _jnp.float32, 'ffn2_w1': _jnp.float32, 'ffn2_w3': _jnp.float32, 'ffn2_w2': _jnp.float32}
MOMENT_SCALE = {'c_ctx': 3.906546e-02, 'w_ada': 1.325355e+00, 'b_ada': 2.731763e+00, 'norm1_w': 1.522412e+00, 'ffn1_w1': 3.446185e-02, 'ffn1_w3': 3.436980e-02, 'ffn1_w2': 5.605575e-02, 'norm2_w': 3.045196e+00, 'w_in': 4.787838e-01, 'q_a_norm_w': 2.546640e-02, 'w_uq': 2.030007e-02, 'kv_a_norm_w': 1.873267e+00, 'w_ukv': 3.250427e-01, 'q_norm_w': 1.269566e-01, 'k_norm_w': 1.278490e-01, 'v_norm_w': 3.368897e+00, 'w_s': 6.399797e-01, 'b_s': 1.602178e+00, 'w_out': 4.278462e-01, 'norm3_w': 1.613852e+00, 'ffn2_w1': 3.434199e-02, 'ffn2_w3': 3.306735e-02, 'ffn2_w2': 5.359900e-02}


def _to_microbatches(a, axis):
    t = _jnp.moveaxis(a, axis, 0)
    t = t.reshape((N_MICROBATCH, t.shape[0] // N_MICROBATCH) + t.shape[1:])
    return _jnp.moveaxis(t, 1, axis + 1)


def setup_inputs(seed: int = 0) -> dict:
    inp = _fwd_setup_inputs(seed)
    key = _jax.random.fold_in(_jax.random.key(seed), 7919)
    shape, _ = _output_shape()
    out = dict(inp)
    out["loss_target"] = _jax.random.normal(_jax.random.fold_in(key, 0), shape, _jnp.float32)
    for i, name in enumerate(TWIN_WEIGHTS):
        w = inp[name].astype(_jnp.float32)
        if MOMENT_SCALE is None:
            s = _jnp.sqrt(_jnp.mean(_jnp.square(w)) + 1e-30)
        else:
            s = MOMENT_SCALE[name]
        km, kv = _jax.random.split(_jax.random.fold_in(key, i + 1))
        out[name] = w
        out["m_" + name] = s * _jax.random.normal(km, w.shape, _jnp.float32)
        out["v_" + name] = (s * s) * _jax.random.uniform(kv, w.shape, _jnp.float32, 0.5, 1.5)
    if N_MICROBATCH > 1:
        for name, axis in PER_EXAMPLE_BATCH_AXIS.items():
            out[name] = _to_microbatches(out[name], axis)
    return {'x': out['x'], 'c': out['c'], 'ctx': out['ctx'], 'c_ctx': out['c_ctx'], 'w_ada': out['w_ada'], 'b_ada': out['b_ada'], 'norm1_w': out['norm1_w'], 'ffn1_w1': out['ffn1_w1'], 'ffn1_w3': out['ffn1_w3'], 'ffn1_w2': out['ffn1_w2'], 'norm2_w': out['norm2_w'], 'w_in': out['w_in'], 'q_a_norm_w': out['q_a_norm_w'], 'w_uq': out['w_uq'], 'kv_a_norm_w': out['kv_a_norm_w'], 'w_ukv': out['w_ukv'], 'q_norm_w': out['q_norm_w'], 'k_norm_w': out['k_norm_w'], 'v_norm_w': out['v_norm_w'], 'w_s': out['w_s'], 'b_s': out['b_s'], 'w_out': out['w_out'], 'norm3_w': out['norm3_w'], 'ffn2_w1': out['ffn2_w1'], 'ffn2_w3': out['ffn2_w3'], 'ffn2_w2': out['ffn2_w2'], 'loss_target': out['loss_target'], 'm_c_ctx': out['m_c_ctx'], 'm_w_ada': out['m_w_ada'], 'm_b_ada': out['m_b_ada'], 'm_norm1_w': out['m_norm1_w'], 'm_ffn1_w1': out['m_ffn1_w1'], 'm_ffn1_w3': out['m_ffn1_w3'], 'm_ffn1_w2': out['m_ffn1_w2'], 'm_norm2_w': out['m_norm2_w'], 'm_w_in': out['m_w_in'], 'm_q_a_norm_w': out['m_q_a_norm_w'], 'm_w_uq': out['m_w_uq'], 'm_kv_a_norm_w': out['m_kv_a_norm_w'], 'm_w_ukv': out['m_w_ukv'], 'm_q_norm_w': out['m_q_norm_w'], 'm_k_norm_w': out['m_k_norm_w'], 'm_v_norm_w': out['m_v_norm_w'], 'm_w_s': out['m_w_s'], 'm_b_s': out['m_b_s'], 'm_w_out': out['m_w_out'], 'm_norm3_w': out['m_norm3_w'], 'm_ffn2_w1': out['m_ffn2_w1'], 'm_ffn2_w3': out['m_ffn2_w3'], 'm_ffn2_w2': out['m_ffn2_w2'], 'v_c_ctx': out['v_c_ctx'], 'v_w_ada': out['v_w_ada'], 'v_b_ada': out['v_b_ada'], 'v_norm1_w': out['v_norm1_w'], 'v_ffn1_w1': out['v_ffn1_w1'], 'v_ffn1_w3': out['v_ffn1_w3'], 'v_ffn1_w2': out['v_ffn1_w2'], 'v_norm2_w': out['v_norm2_w'], 'v_w_in': out['v_w_in'], 'v_q_a_norm_w': out['v_q_a_norm_w'], 'v_w_uq': out['v_w_uq'], 'v_kv_a_norm_w': out['v_kv_a_norm_w'], 'v_w_ukv': out['v_w_ukv'], 'v_q_norm_w': out['v_q_norm_w'], 'v_k_norm_w': out['v_k_norm_w'], 'v_v_norm_w': out['v_v_norm_w'], 'v_w_s': out['v_w_s'], 'v_b_s': out['v_b_s'], 'v_w_out': out['v_w_out'], 'v_norm3_w': out['v_norm3_w'], 'v_ffn2_w1': out['v_ffn2_w1'], 'v_ffn2_w3': out['v_ffn2_w3'], 'v_ffn2_w2': out['v_ffn2_w2']}


def _loss(weights, diff, rest, loss_target):
    with _jax.named_scope("forward"):
        args = {**rest, TWIN_DIFF_INPUT: diff, **{k: w.astype(_WEIGHT_DTYPES[k]) for k, w in weights.items()}}
        y = _forward(args)
    with _jax.named_scope("loss_head"):
        err = _jnp.square(y.astype(_jnp.float32) - loss_target)
        return 0.5 * _jnp.sum(_jnp.mean(err, axis=-1)) if err.ndim else 0.5 * err


def _adamw(w, g, m, v):
    m = ADAM_B1 * m + (1.0 - ADAM_B1) * g
    v = ADAM_B2 * v + (1.0 - ADAM_B2) * _jnp.square(g)
    m_hat = m / (1.0 - ADAM_B1 ** ADAM_STEP)
    v_hat = v / (1.0 - ADAM_B2 ** ADAM_STEP)
    delta = -ADAM_LR * (m_hat / (_jnp.sqrt(v_hat) + ADAM_EPS) + ADAM_WD * w)
    return delta, m, v


def reference(x, c, ctx, c_ctx, w_ada, b_ada, norm1_w, ffn1_w1, ffn1_w3, ffn1_w2, norm2_w, w_in, q_a_norm_w, w_uq, kv_a_norm_w, w_ukv, q_norm_w, k_norm_w, v_norm_w, w_s, b_s, w_out, norm3_w, ffn2_w1, ffn2_w3, ffn2_w2, loss_target, m_c_ctx, m_w_ada, m_b_ada, m_norm1_w, m_ffn1_w1, m_ffn1_w3, m_ffn1_w2, m_norm2_w, m_w_in, m_q_a_norm_w, m_w_uq, m_kv_a_norm_w, m_w_ukv, m_q_norm_w, m_k_norm_w, m_v_norm_w, m_w_s, m_b_s, m_w_out, m_norm3_w, m_ffn2_w1, m_ffn2_w3, m_ffn2_w2, v_c_ctx, v_w_ada, v_b_ada, v_norm1_w, v_ffn1_w1, v_ffn1_w3, v_ffn1_w2, v_norm2_w, v_w_in, v_q_a_norm_w, v_w_uq, v_kv_a_norm_w, v_w_ukv, v_q_norm_w, v_k_norm_w, v_v_norm_w, v_w_s, v_b_s, v_w_out, v_norm3_w, v_ffn2_w1, v_ffn2_w3, v_ffn2_w2):
    given = dict(x=x, c=c, ctx=ctx, c_ctx=c_ctx, w_ada=w_ada, b_ada=b_ada, norm1_w=norm1_w, ffn1_w1=ffn1_w1, ffn1_w3=ffn1_w3, ffn1_w2=ffn1_w2, norm2_w=norm2_w, w_in=w_in, q_a_norm_w=q_a_norm_w, w_uq=w_uq, kv_a_norm_w=kv_a_norm_w, w_ukv=w_ukv, q_norm_w=q_norm_w, k_norm_w=k_norm_w, v_norm_w=v_norm_w, w_s=w_s, b_s=b_s, w_out=w_out, norm3_w=norm3_w, ffn2_w1=ffn2_w1, ffn2_w3=ffn2_w3, ffn2_w2=ffn2_w2, loss_target=loss_target, m_c_ctx=m_c_ctx, m_w_ada=m_w_ada, m_b_ada=m_b_ada, m_norm1_w=m_norm1_w, m_ffn1_w1=m_ffn1_w1, m_ffn1_w3=m_ffn1_w3, m_ffn1_w2=m_ffn1_w2, m_norm2_w=m_norm2_w, m_w_in=m_w_in, m_q_a_norm_w=m_q_a_norm_w, m_w_uq=m_w_uq, m_kv_a_norm_w=m_kv_a_norm_w, m_w_ukv=m_w_ukv, m_q_norm_w=m_q_norm_w, m_k_norm_w=m_k_norm_w, m_v_norm_w=m_v_norm_w, m_w_s=m_w_s, m_b_s=m_b_s, m_w_out=m_w_out, m_norm3_w=m_norm3_w, m_ffn2_w1=m_ffn2_w1, m_ffn2_w3=m_ffn2_w3, m_ffn2_w2=m_ffn2_w2, v_c_ctx=v_c_ctx, v_w_ada=v_w_ada, v_b_ada=v_b_ada, v_norm1_w=v_norm1_w, v_ffn1_w1=v_ffn1_w1, v_ffn1_w3=v_ffn1_w3, v_ffn1_w2=v_ffn1_w2, v_norm2_w=v_norm2_w, v_w_in=v_w_in, v_q_a_norm_w=v_q_a_norm_w, v_w_uq=v_w_uq, v_kv_a_norm_w=v_kv_a_norm_w, v_w_ukv=v_w_ukv, v_q_norm_w=v_q_norm_w, v_k_norm_w=v_k_norm_w, v_v_norm_w=v_v_norm_w, v_w_s=v_w_s, v_b_s=v_b_s, v_w_out=v_w_out, v_norm3_w=v_norm3_w, v_ffn2_w1=v_ffn2_w1, v_ffn2_w3=v_ffn2_w3, v_ffn2_w2=v_ffn2_w2)
    weights = {n: given[n] for n in TWIN_WEIGHTS}
    shared = {n: given[n] for n in SHARED_INPUTS}
    per_example = {n: given[n] for n in ['x', 'c', 'ctx']}
    grad_fn = _jax.value_and_grad(_loss, argnums=(0, 1))

    def one_microbatch(ex, loss_target):
        ex = dict(ex)
        diff = ex.pop(TWIN_DIFF_INPUT)
        return grad_fn(weights, diff, {**shared, **ex}, loss_target)

    if N_MICROBATCH == 1:
        loss, (grad_w, grad_x) = one_microbatch(per_example, given["loss_target"])
    else:
        def body(carry, xs):
            loss_sum, grad_sum = carry
            l_k, (gw_k, gx_k) = one_microbatch(xs[0], xs[1])
            with _jax.named_scope("update"):
                return (loss_sum + l_k, _jax.tree.map(_jnp.add, grad_sum, gw_k)), gx_k

        init = (_jnp.zeros((), _jnp.float32), _jax.tree.map(_jnp.zeros_like, weights))
        (loss, grad_w), grad_x = _jax.lax.scan(body, init, (per_example, given["loss_target"]))
    with _jax.named_scope("update"):
        delta_w, new_m, new_v = {}, {}, {}
        for n in TWIN_WEIGHTS:
            delta_w[n], new_m[n], new_v[n] = _adamw(weights[n], grad_w[n], given["m_" + n], given["v_" + n])
    return (loss, grad_x, *[grad_w[n] for n in TWIN_WEIGHTS], *[delta_w[n] for n in TWIN_WEIGHTS],
            *[new_m[n] for n in TWIN_WEIGHTS], *[new_v[n] for n in TWIN_WEIGHTS])
```

```python
import functools
import math

import jax
import jax.numpy as jnp
from jax import lax
from jax.experimental import pallas as pl
from jax.experimental.pallas import tpu as pltpu

F32 = jnp.float32
BF16 = jnp.bfloat16

EPS = 1e-6
N_MOD = 9
HEADS = 8
QK_NOPE, QK_ROPE, V_HEAD = 64, 32, 64
QK_HEAD = QK_NOPE + QK_ROPE
HEAD_PAD = 128
Q_LORA, KV_LORA = 256, 128
GROUPS, GROUP_DIM, CHUNK = 8, 64, 128
GMLP_W = GROUPS * GROUP_DIM
MLA_W = HEADS * V_HEAD
IN_COLS = 1440
PROJ_COLS = 1536
GRID_W = 64
ROPE_BASE = 10000.0
MOD_ROWS = 16
ADAM_LR, ADAM_B1, ADAM_B2, ADAM_EPS, ADAM_WD, ADAM_STEP = 0.001, 0.9, 0.999, 1e-08, 0.01, 10
N_CHIPS = 4
LANES = 128
V7X_VMEM_LIMIT = 56 * 1024 * 1024
GELU_C = math.sqrt(2.0 / math.pi)

SHARDED = ("w_ada", "ffn1_w1", "ffn1_w3", "ffn1_w2", "w_in", "w_uq", "w_ukv", "w_out", "ffn2_w1", "ffn2_w3", "ffn2_w2")
ROW_SHARDED = ("ffn1_w2", "w_out", "ffn2_w2")
SMALL = ("c_ctx", "b_ada", "norm1_w", "norm2_w", "q_a_norm_w", "kv_a_norm_w", "q_norm_w", "k_norm_w", "v_norm_w",
         "w_s", "b_s", "norm3_w")
WEIGHTS = ("c_ctx", "w_ada", "b_ada", "norm1_w", "ffn1_w1", "ffn1_w3", "ffn1_w2", "norm2_w", "w_in", "q_a_norm_w",
           "w_uq", "kv_a_norm_w", "w_ukv", "q_norm_w", "k_norm_w", "v_norm_w", "w_s", "b_s", "w_out", "norm3_w",
           "ffn2_w1", "ffn2_w3", "ffn2_w2")


def _round_up(n, m):
    return (n + m - 1) // m * m


def _div_tile(n, target, mult):
    best = None
    for t in range(mult, min(n, target) + 1, mult):
        if n % t == 0:
            best = t
    return n if best is None else best


def _dot(a, b):
    return lax.dot_general(a, b, (((1,), (0,)), ((), ())), preferred_element_type=F32)


def _dot_nt(a, b):
    return lax.dot_general(a, b, (((1,), (1,)), ((), ())), preferred_element_type=F32)


def _dot_tn(a, b):
    return lax.dot_general(a, b, (((0,), (0,)), ((), ())), preferred_element_type=F32)


def _sigmoid(x):
    return 1.0 / (1.0 + jnp.exp(-x))


def _gelu(x):
    return 0.5 * x * (1.0 + jnp.tanh(GELU_C * (x + 0.044715 * x * x * x)))


def _gelu_grad(x):
    t = jnp.tanh(GELU_C * (x + 0.044715 * x * x * x))
    return 0.5 * (1.0 + t) + 0.5 * x * (1.0 - t * t) * (GELU_C * (1.0 + 3 * 0.044715 * x * x))


def _rope(x, cos, sin_a, sin_b):
    return x * cos + pltpu.roll(x, 8, 1) * sin_a + pltpu.roll(x, HEAD_PAD - 8, 1) * sin_b


def _rope_t(d, cos, sin_a, sin_b):
    return d * cos + pltpu.roll(d * sin_a, HEAD_PAD - 8, 1) + pltpu.roll(d * sin_b, 8, 1)


def _group_sum(x, ones_ref):
    hi = x.astype(BF16)
    lo = (x - hi.astype(F32)).astype(BF16)
    return _dot(hi, ones_ref[...]) + _dot(lo, ones_ref[...])


def _params(n_axes):
    return pltpu.CompilerParams(dimension_semantics=("arbitrary",) * n_axes, vmem_limit_bytes=V7X_VMEM_LIMIT)


def _whole(shape):
    nd = len(shape)
    return pl.BlockSpec(shape, lambda *_: (0,) * nd, pipeline_mode=pl.Buffered(1))


def _sds(shape, dtype):
    return jax.ShapeDtypeStruct(shape, dtype)


def _token_tile(s, ctx):
    return _div_tile(math.gcd(s, ctx), 256, CHUNK)


def _ada_fwd(cc, w_ada, b_ada):
    d = cc.shape[1]
    n = w_ada.shape[1]
    tn = _div_tile(n, 2304, LANES)

    def body(cc_ref, w_ref, b_ref, o_ref):
        c = cc_ref[...]
        sc = (c * _sigmoid(c)).astype(BF16)
        o_ref[...] = _dot(sc, w_ref[...]) + b_ref[...]

    return pl.pallas_call(
        body, name="ada_fwd", grid=(n // tn,),
        in_specs=[_whole(cc.shape), pl.BlockSpec((d, tn), lambda j: (0, j)), pl.BlockSpec((1, tn), lambda j: (0, j))],
        out_specs=pl.BlockSpec((MOD_ROWS, tn), lambda j: (0, j)),
        out_shape=_sds((MOD_ROWS, n), F32), compiler_params=_params(1),
    )(cc, w_ada, b_ada)


def _ada_bwd(cc, dmods, w_ada):
    d = cc.shape[1]
    n = w_ada.shape[1]
    tn = _div_tile(n, 1152, LANES)
    nsteps = n // tn

    def body(cc_ref, m0, m1, m2, m3, w_ref, dw_ref, db_ref, dcc_ref, acc_ref):
        j = pl.program_id(0)

        @pl.when(j == 0)
        def _():
            acc_ref[...] = jnp.zeros_like(acc_ref)

        c = cc_ref[...]
        sig = _sigmoid(c)
        dm = m0[...] + m1[...] + m2[...] + m3[...]
        dmb = dm.astype(BF16)
        db_ref[...] = jnp.sum(dm, axis=0, keepdims=True)
        dw_ref[...] = _dot_tn((c * sig).astype(BF16), dmb)
        acc_ref[...] += _dot_nt(dmb, w_ref[...])

        @pl.when(j == nsteps - 1)
        def _():
            dcc_ref[...] = acc_ref[...] * (sig * (1.0 + c * (1.0 - sig)))

    mspec = pl.BlockSpec((MOD_ROWS, tn), lambda j: (0, j))
    return pl.pallas_call(
        body, name="ada_bwd", grid=(nsteps,),
        in_specs=[_whole(cc.shape), mspec, mspec, mspec, mspec, pl.BlockSpec((d, tn), lambda j: (0, j))],
        out_specs=[pl.BlockSpec((d, tn), lambda j: (0, j)), pl.BlockSpec((1, tn), lambda j: (0, j)),
                   pl.BlockSpec((MOD_ROWS, d), lambda j: (0, 0))],
        out_shape=[_sds((d, n), F32), _sds((1, n), F32), _sds((MOD_ROWS, d), F32)],
        scratch_shapes=[pltpu.VMEM((MOD_ROWS, d), F32)], compiler_params=_params(1),
    )(cc, *dmods, w_ada)


def _ffn_fwd(xs, mod, nw, w1, w3, w2, k0, s, nb, tm, name):
    t, d = xs.shape
    f = w1.shape[1]

    def body(x_ref, mod_ref, nw_ref, w1_ref, w3_ref, w2_ref, o_ref, a_ref, b_ref, y_ref):
        g = jnp.minimum((pl.program_id(0) * tm) // s, nb)
        shift = mod_ref[g, pl.ds(k0, 1), :]
        scale = mod_ref[g, pl.ds(k0 + 1, 1), :]
        gate = mod_ref[g, pl.ds(k0 + 2, 1), :]
        x = x_ref[...]
        r = lax.rsqrt(jnp.mean(x * x, axis=-1, keepdims=True) + EPS)
        hb = ((x * r * nw_ref[...]) * (1.0 + scale) + shift).astype(BF16)
        a = _dot(hb, w1_ref[...])
        b = _dot(hb, w3_ref[...])
        gb = (a * _sigmoid(a) * b).astype(BF16)
        y = _dot(gb, w2_ref[...])
        o_ref[...] = x + (0.5 * gate) * y
        a_ref[...] = a.astype(BF16)
        b_ref[...] = b.astype(BF16)
        y_ref[...] = y.astype(BF16)

    row = lambda i: (i, 0)
    return pl.pallas_call(
        body, name=name, grid=(t // tm,),
        in_specs=[pl.BlockSpec((tm, d), row), _whole(mod.shape), _whole(nw.shape), _whole(w1.shape), _whole(w3.shape),
                  _whole(w2.shape)],
        out_specs=[pl.BlockSpec((tm, d), row), pl.BlockSpec((tm, f), row), pl.BlockSpec((tm, f), row),
                   pl.BlockSpec((tm, d), row)],
        out_shape=[_sds((t, d), F32), _sds((t, f), BF16), _sds((t, f), BF16), _sds((t, d), BF16)],
        compiler_params=_params(1),
    )(xs, mod, nw, w1, w3, w2)


def _ffn_bwd(dout, xs, a, b, y, mod, nw, w1, w3, w2, k0, s, nb, tm, name):
    t, d = xs.shape
    f = w1.shape[1]
    nch = 2 if (f // 2) % LANES == 0 and f % 2 == 0 else 1
    fc = f // nch

    def body(do_ref, x_ref, a_ref, b_ref, y_ref, mod_ref, nw_ref, w1_ref, w3_ref, w2_ref,
             dx_ref, h_ref, g_ref, da_ref, db_ref, dy_ref, dmod_ref, dnw_ref):
        i = pl.program_id(0)

        @pl.when(i == 0)
        def _():
            dmod_ref[...] = jnp.zeros_like(dmod_ref)
            dnw_ref[...] = jnp.zeros_like(dnw_ref)

        g = jnp.minimum((i * tm) // s, nb)
        shift = mod_ref[g, pl.ds(k0, 1), :]
        scale = mod_ref[g, pl.ds(k0 + 1, 1), :]
        gate = mod_ref[g, pl.ds(k0 + 2, 1), :]
        x = x_ref[...]
        dout_v = do_ref[...]
        r = lax.rsqrt(jnp.mean(x * x, axis=-1, keepdims=True) + EPS)
        xh = x * r
        n = xh * nw_ref[...]
        h_ref[...] = (n * (1.0 + scale) + shift).astype(BF16)
        dyb = ((0.5 * gate) * dout_v).astype(BF16)
        dy_ref[...] = dyb
        dmod_ref[g, pl.ds(k0 + 2, 1), :] += 0.5 * jnp.sum(dout_v * y_ref[...].astype(F32), axis=0, keepdims=True)
        dh = jnp.zeros((tm, d), F32)
        for c in range(nch):
            sl = slice(c * fc, (c + 1) * fc)
            dg = _dot_nt(dyb, w2_ref[sl, :])
            av = a_ref[:, sl].astype(F32)
            bv = b_ref[:, sl].astype(F32)
            sig = _sigmoid(av)
            sa = av * sig
            g_ref[:, sl] = (sa * bv).astype(BF16)
            dab = (dg * bv * (sig * (1.0 + av * (1.0 - sig)))).astype(BF16)
            dbb = (dg * sa).astype(BF16)
            da_ref[:, sl] = dab
            db_ref[:, sl] = dbb
            dh = dh + _dot_nt(dab, w1_ref[:, sl]) + _dot_nt(dbb, w3_ref[:, sl])
        dmod_ref[g, pl.ds(k0, 1), :] += jnp.sum(dh, axis=0, keepdims=True)
        dmod_ref[g, pl.ds(k0 + 1, 1), :] += jnp.sum(dh * n, axis=0, keepdims=True)
        dn = dh * (1.0 + scale)
        dnw_ref[...] += jnp.sum(dn * xh, axis=0, keepdims=True)
        dxh = dn * nw_ref[...]
        dx_ref[...] = dout_v + r * (dxh - xh * jnp.mean(dxh * xh, axis=-1, keepdims=True))

    row = lambda i: (i, 0)
    td = pl.BlockSpec((tm, d), row)
    tf = pl.BlockSpec((tm, f), row)
    return pl.pallas_call(
        body, name=name, grid=(t // tm,),
        in_specs=[td, td, tf, tf, td, _whole(mod.shape), _whole(nw.shape), _whole(w1.shape), _whole(w3.shape),
                  _whole(w2.shape)],
        out_specs=[td, td, tf, tf, tf, td, pl.BlockSpec(mod.shape, lambda i: (0, 0, 0)),
                   pl.BlockSpec((1, d), lambda i: (0, 0))],
        out_shape=[_sds((t, d), F32), _sds((t, d), BF16), _sds((t, f), BF16), _sds((t, f), BF16), _sds((t, f), BF16),
                   _sds((t, d), BF16), _sds(mod.shape, F32), _sds((1, d), F32)],
        compiler_params=_params(1),
    )(dout, xs, a, b, y, mod, nw, w1, w3, w2)


def _mm_tn(a, b, rows, name):
    m = a.shape[1]
    n = b.shape[1]
    bm = _div_tile(m, 1408, LANES)
    bn = _div_tile(n, 1408, LANES)
    bk = _div_tile(rows, 512, LANES)

    def body(a_ref, b_ref, o_ref):
        @pl.when(pl.program_id(2) == 0)
        def _():
            o_ref[...] = jnp.zeros_like(o_ref)

        o_ref[...] += _dot_tn(a_ref[...], b_ref[...])

    return pl.pallas_call(
        body, name=name, grid=(m // bm, n // bn, rows // bk),
        in_specs=[pl.BlockSpec((bk, bm), lambda i, j, k: (k, i)), pl.BlockSpec((bk, bn), lambda i, j, k: (k, j))],
        out_specs=pl.BlockSpec((bm, bn), lambda i, j, k: (i, j)),
        out_shape=_sds((m, n), F32), compiler_params=_params(3),
    )(a, b)


def _mixin_fwd(xs, mod, nw, wp, s, nb, tm):
    t, d = xs.shape

    def body(x_ref, mod_ref, nw_ref, wp_ref, h_ref, p_ref):
        g = jnp.minimum((pl.program_id(0) * tm) // s, nb)
        shift = mod_ref[g, pl.ds(3, 1), :]
        scale = mod_ref[g, pl.ds(4, 1), :]
        x = x_ref[...]
        r = lax.rsqrt(jnp.mean(x * x, axis=-1, keepdims=True) + EPS)
        hb = ((x * r * nw_ref[...]) * (1.0 + scale) + shift).astype(BF16)
        h_ref[...] = hb
        p_ref[...] = _dot(hb, wp_ref[...])

    row = lambda i: (i, 0)
    return pl.pallas_call(
        body, name="mixin_fwd", grid=(t // tm,),
        in_specs=[pl.BlockSpec((tm, d), row), _whole(mod.shape), _whole(nw.shape), _whole(wp.shape)],
        out_specs=[pl.BlockSpec((tm, d), row), pl.BlockSpec((tm, PROJ_COLS), row)],
        out_shape=[_sds((t, d), BF16), _sds((t, PROJ_COLS), F32)], compiler_params=_params(1),
    )(xs, mod, nw, wp)


def _mixin_bwd(dp0, dpu, dpv, xs, dres, mod, nw, wp, s, nb, tm):
    t_all, d = xs.shape
    nlat = dres.shape[0] // tm

    def body(p0_ref, pu_ref, pv_ref, x_ref, dr_ref, mod_ref, nw_ref, wp_ref, dx_ref, dmod_ref, dnw_ref):
        i = pl.program_id(0)

        @pl.when(i == 0)
        def _():
            dmod_ref[...] = jnp.zeros_like(dmod_ref)
            dnw_ref[...] = jnp.zeros_like(dnw_ref)

        lat = i < nlat
        g = jnp.minimum((i * tm) // s, nb)
        scale = mod_ref[g, pl.ds(4, 1), :]
        dh = _dot_nt(p0_ref[...], wp_ref[:, 0:512])
        extra = _dot_nt(pu_ref[...], wp_ref[:, 512:1024]) + _dot_nt(pv_ref[...], wp_ref[:, 1024:1536])
        dh = dh + jnp.where(lat, extra, 0.0)
        x = x_ref[...]
        r = lax.rsqrt(jnp.mean(x * x, axis=-1, keepdims=True) + EPS)
        xh = x * r
        n = xh * nw_ref[...]
        dmod_ref[g, pl.ds(3, 1), :] += jnp.sum(dh, axis=0, keepdims=True)
        dmod_ref[g, pl.ds(4, 1), :] += jnp.sum(dh * n, axis=0, keepdims=True)
        dn = dh * (1.0 + scale)
        dnw_ref[...] += jnp.sum(dn * xh, axis=0, keepdims=True)
        dxh = dn * nw_ref[...]
        dx_ref[...] = jnp.where(lat, dr_ref[...], 0.0) + r * (dxh - xh * jnp.mean(dxh * xh, axis=-1, keepdims=True))

    row = lambda i: (i, 0)
    lrow = lambda i: (jnp.minimum(i, nlat - 1), 0)
    return pl.pallas_call(
        body, name="mixin_bwd", grid=(t_all // tm,),
        in_specs=[pl.BlockSpec((tm, 512), row), pl.BlockSpec((tm, 512), lrow), pl.BlockSpec((tm, 512), lrow),
                  pl.BlockSpec((tm, d), row), pl.BlockSpec((tm, d), lrow), _whole(mod.shape), _whole(nw.shape),
                  _whole(wp.shape)],
        out_specs=[pl.BlockSpec((tm, d), row), pl.BlockSpec(mod.shape, lambda i: (0, 0, 0)),
                   pl.BlockSpec((1, d), lambda i: (0, 0))],
        out_shape=[_sds((t_all, d), F32), _sds(mod.shape, F32), _sds((1, d), F32)], compiler_params=_params(1),
    )(dp0, dpu, dpv, xs, dres, mod, nw, wp)


def _prep_fwd(proj, row0, nb, s, pos0, tabs, wq, wk, wv, kvaw, qaw, qnw, knw, tm, with_q, name):
    nblk = s // tm

    def body(p_ref, cos_ref, sa_ref, sb_ref, wq_ref, wk_ref, wv_ref, kvaw_ref, qaw_ref, qnw_ref, knw_ref, *outs):
        q_ref, k_ref, v_ref = outs if with_q else (None,) + outs
        cos, sin_a, sin_b = cos_ref[...], sa_ref[...], sb_ref[...]
        ckv = p_ref[:, 0:128]
        kpe = p_ref[:, 128:256]
        rkv = lax.rsqrt(jnp.mean(ckv * ckv, axis=-1, keepdims=True) + EPS)
        ckvb = (ckv * rkv * kvaw_ref[...]).astype(BF16)
        for h in range(HEADS):
            kp = _dot(ckvb, wk_ref[h]) + kpe
            rk = lax.rsqrt(jnp.sum(kp * kp, axis=-1, keepdims=True) * (1.0 / QK_HEAD) + EPS)
            k_ref[h] = _rope(kp * rk * knw_ref[...], cos, sin_a, sin_b).astype(BF16)
        for j in range(HEADS // 2):
            v_ref[j] = _dot(ckvb, wv_ref[j]).astype(BF16)
        if with_q:
            cq = p_ref[:, 256:512]
            rq = lax.rsqrt(jnp.mean(cq * cq, axis=-1, keepdims=True) + EPS)
            cqb = (cq * rq * qaw_ref[...]).astype(BF16)
            for h in range(HEADS):
                qp = _dot(cqb, wq_ref[h])
                rh = lax.rsqrt(jnp.sum(qp * qp, axis=-1, keepdims=True) * (1.0 / QK_HEAD) + EPS)
                q_ref[h] = _rope(qp * rh * qnw_ref[...], cos, sin_a, sin_b).astype(BF16)

    tab = pl.BlockSpec((tm, HEAD_PAD), lambda i: (pos0 + i % nblk, 0))
    hspec = pl.BlockSpec((None, HEADS, tm, HEAD_PAD), lambda i: (i // nblk, 0, i % nblk, 0))
    vspec = pl.BlockSpec((None, HEADS // 2, tm, HEAD_PAD), lambda i: (i // nblk, 0, i % nblk, 0))
    hshape = _sds((nb, HEADS, s, HEAD_PAD), BF16)
    vshape = _sds((nb, HEADS // 2, s, HEAD_PAD), BF16)
    return pl.pallas_call(
        body, name=name, grid=(nb * nblk,),
        in_specs=[pl.BlockSpec((tm, 512), lambda i: (row0 + i, 0)), tab, tab, tab, _whole(wq.shape), _whole(wk.shape),
                  _whole(wv.shape), _whole(kvaw.shape), _whole(qaw.shape), _whole(qnw.shape), _whole(knw.shape)],
        out_specs=([hspec] if with_q else []) + [hspec, vspec],
        out_shape=([hshape] if with_q else []) + [hshape, vshape], compiler_params=_params(1),
    )(proj, *tabs, wq, wk, wv, kvaw, qaw, qnw, knw)


def _prep_bwd(proj, row0, nb, s, pos0, key0, tabs, wq, wk, wv, kvaw, qaw, qnw, knw, dq, dk, dv, init, tm, name):
    nblk = s // tm
    with_q = dq is not None
    n_init = 0 if init is None else len(init)

    def body(*refs):
        p_ref, cos_ref, sa_ref, sb_ref, wq_ref, wk_ref, wv_ref, kvaw_ref, qaw_ref, qnw_ref, knw_ref = refs[:11]
        rest = list(refs[11:])
        dq_ref = rest.pop(0) if with_q else None
        dk_ref, dv_ref = rest.pop(0), rest.pop(0)
        init_refs = [rest.pop(0) for _ in range(n_init)]
        dp_ref = rest.pop(0)
        if with_q:
            dwq_ref, dqaw_ref, dqnw_ref = rest.pop(0), rest.pop(0), rest.pop(0)
        dwk_ref, dwv_ref, dkvaw_ref, dknw_ref = rest
        accs = [dwk_ref, dwv_ref, dkvaw_ref, dknw_ref]

        @pl.when(pl.program_id(0) == 0)
        def _():
            for k, acc in enumerate(accs):
                acc[...] = init_refs[k][...] if n_init else jnp.zeros_like(acc)
            if with_q:
                dwq_ref[...] = jnp.zeros_like(dwq_ref)
                dqaw_ref[...] = jnp.zeros_like(dqaw_ref)
                dqnw_ref[...] = jnp.zeros_like(dqnw_ref)

        cos, sin_a, sin_b = cos_ref[...], sa_ref[...], sb_ref[...]
        lane = lax.broadcasted_iota(jnp.int32, (tm, HEAD_PAD), 1)
        rope_lanes = (lane >= QK_NOPE) & (lane < QK_HEAD)
        ckv = p_ref[:, 0:128]
        kpe = p_ref[:, 128:256]
        rkv = lax.rsqrt(jnp.mean(ckv * ckv, axis=-1, keepdims=True) + EPS)
        ckvh = ckv * rkv
        ckvb = (ckvh * kvaw_ref[...]).astype(BF16)
        dckv = jnp.zeros((tm, KV_LORA), F32)
        dkpe = jnp.zeros((tm, HEAD_PAD), F32)
        dknw = jnp.zeros((1, HEAD_PAD), F32)
        for h in range(HEADS):
            kp = _dot(ckvb, wk_ref[h]) + kpe
            rk = lax.rsqrt(jnp.sum(kp * kp, axis=-1, keepdims=True) * (1.0 / QK_HEAD) + EPS)
            kh = kp * rk
            dkn = _rope_t(dk_ref[h], cos, sin_a, sin_b)
            dknw = dknw + jnp.sum(dkn * kh, axis=0, keepdims=True)
            dkh = dkn * knw_ref[...]
            dkp = rk * (dkh - kh * (jnp.sum(dkh * kh, axis=-1, keepdims=True) * (1.0 / QK_HEAD)))
            dkpe = dkpe + jnp.where(rope_lanes, dkp, 0.0)
            dkpb = dkp.astype(BF16)
            dckv = dckv + _dot_nt(dkpb, wk_ref[h])
            dwk_ref[h] += _dot_tn(ckvb, dkpb)
        dknw_ref[...] += dknw
        for j in range(HEADS // 2):
            dvb = dv_ref[j].astype(BF16)
            dckv = dckv + _dot_nt(dvb, wv_ref[j])
            dwv_ref[j] += _dot_tn(ckvb, dvb)
        dkvaw_ref[...] += jnp.sum(dckv * ckvh, axis=0, keepdims=True)
        dch = dckv * kvaw_ref[...]
        dp_ref[:, 0:128] = (rkv * (dch - ckvh * jnp.mean(dch * ckvh, axis=-1, keepdims=True))).astype(BF16)
        dp_ref[:, 128:256] = dkpe.astype(BF16)
        if with_q:
            cq = p_ref[:, 256:512]
            rq = lax.rsqrt(jnp.mean(cq * cq, axis=-1, keepdims=True) + EPS)
            cqh = cq * rq
            cqb = (cqh * qaw_ref[...]).astype(BF16)
            dcq = jnp.zeros((tm, Q_LORA), F32)
            dqnw = jnp.zeros((1, HEAD_PAD), F32)
            for h in range(HEADS):
                qp = _dot(cqb, wq_ref[h])
                rh = lax.rsqrt(jnp.sum(qp * qp, axis=-1, keepdims=True) * (1.0 / QK_HEAD) + EPS)
                qh = qp * rh
                dqn = _rope_t(dq_ref[h], cos, sin_a, sin_b)
                dqnw = dqnw + jnp.sum(dqn * qh, axis=0, keepdims=True)
                dqh = dqn * qnw_ref[...]
                dqp = (rh * (dqh - qh * (jnp.sum(dqh * qh, axis=-1, keepdims=True) * (1.0 / QK_HEAD)))).astype(BF16)
                dcq = dcq + _dot_nt(dqp, wq_ref[h])
                dwq_ref[h] += _dot_tn(cqb, dqp)
            dqnw_ref[...] += dqnw
            dqaw_ref[...] += jnp.sum(dcq * cqh, axis=0, keepdims=True)
            dqc = dcq * qaw_ref[...]
            dp_ref[:, 256:512] = (rq * (dqc - cqh * jnp.mean(dqc * cqh, axis=-1, keepdims=True))).astype(BF16)
        else:
            dp_ref[:, 256:512] = jnp.zeros((tm, Q_LORA), BF16)

    tab = pl.BlockSpec((tm, HEAD_PAD), lambda i: (pos0 + i % nblk, 0))
    qspec = pl.BlockSpec((None, HEADS, tm, HEAD_PAD), lambda i: (i // nblk, 0, i % nblk, 0))
    kspec = pl.BlockSpec((None, HEADS, tm, HEAD_PAD), lambda i: (i // nblk, 0, key0 + i % nblk, 0))
    vspec = pl.BlockSpec((None, HEADS // 2, tm, HEAD_PAD), lambda i: (i // nblk, 0, key0 + i % nblk, 0))

    def acc_spec(shape):
        nd = len(shape)
        return pl.BlockSpec(shape, lambda i: (0,) * nd)

    acc_shapes = [(HEADS, KV_LORA, HEAD_PAD), (HEADS // 2, KV_LORA, HEAD_PAD), (1, KV_LORA), (1, HEAD_PAD)]
    q_shapes = [(HEADS, Q_LORA, HEAD_PAD), (1, Q_LORA), (1, HEAD_PAD)] if with_q else []
    out_shapes = [(nb * s, 512)] + q_shapes + acc_shapes
    return pl.pallas_call(
        body, name=name, grid=(nb * nblk,),
        in_specs=[pl.BlockSpec((tm, 512), lambda i: (row0 + i, 0)), tab, tab, tab, _whole(wq.shape), _whole(wk.shape),
                  _whole(wv.shape), _whole(kvaw.shape), _whole(qaw.shape), _whole(qnw.shape), _whole(knw.shape)]
        + ([qspec] if with_q else []) + [kspec, vspec] + [_whole(a.shape) for a in (init or [])],
        out_specs=[pl.BlockSpec((tm, 512), lambda i: (i, 0))] + [acc_spec(sh) for sh in q_shapes + acc_shapes],
        out_shape=[_sds(out_shapes[0], BF16)] + [_sds(sh, F32) for sh in out_shapes[1:]],
        compiler_params=_params(1),
    )(proj, *tabs, wq, wk, wv, kvaw, qaw, qnw, knw, *([dq] if with_q else []), dk, dv, *(init or []))


def _attn_fwd(q, k, v, tq):
    nb, _, s, _ = q.shape
    sk = k.shape[2]
    nq = s // tq
    scale = QK_HEAD ** -0.5

    def body(q_ref, k_ref, v_ref, o_ref):
        lane = lax.broadcasted_iota(jnp.int32, (tq, HEAD_PAD), 1)
        vv = v_ref[...]
        outs = []
        for hh in range(2):
            sc = _dot_nt(q_ref[hh], k_ref[hh]) * scale
            p = jnp.exp(sc - jnp.max(sc, axis=-1, keepdims=True))
            l = jnp.sum(p, axis=-1, keepdims=True)
            outs.append(_dot(p.astype(BF16), vv) / l)
        o_ref[...] = jnp.where(lane < V_HEAD, outs[0], outs[1]).astype(BF16)

    return pl.pallas_call(
        body, name="attn_fwd", grid=(nb, HEADS // 2, nq),
        in_specs=[pl.BlockSpec((None, 2, tq, HEAD_PAD), lambda b, j, i: (b, j, i, 0)),
                  pl.BlockSpec((None, 2, sk, HEAD_PAD), lambda b, j, i: (b, j, 0, 0)),
                  pl.BlockSpec((None, None, sk, HEAD_PAD), lambda b, j, i: (b, j, 0, 0))],
        out_specs=pl.BlockSpec((tq, HEAD_PAD), lambda b, j, i: (b * nq + i, j)),
        out_shape=_sds((nb * s, MLA_W), BF16), compiler_params=_params(3),
    )(q, k, v)


def _attn_bwd(q, k, v, do, tq):
    nb, _, s, _ = q.shape
    sk = k.shape[2]
    nq = s // tq
    scale = QK_HEAD ** -0.5

    def body(q_ref, k_ref, v_ref, do_ref, dq_ref, dk_ref, dv_ref):
        @pl.when(pl.program_id(2) == 0)
        def _():
            dk_ref[...] = jnp.zeros_like(dk_ref)
            dv_ref[...] = jnp.zeros_like(dv_ref)

        lane = lax.broadcasted_iota(jnp.int32, (tq, HEAD_PAD), 1)
        vv = v_ref[...]
        dov = do_ref[...]
        for hh in range(2):
            mine = (lane < V_HEAD) if hh == 0 else (lane >= V_HEAD)
            doh = jnp.where(mine, dov, jnp.zeros_like(dov))
            sc = _dot_nt(q_ref[hh], k_ref[hh]) * scale
            p = jnp.exp(sc - jnp.max(sc, axis=-1, keepdims=True))
            p = p / jnp.sum(p, axis=-1, keepdims=True)
            dp = _dot_nt(doh, vv)
            ds = (p * (dp - jnp.sum(p * dp, axis=-1, keepdims=True)) * scale).astype(BF16)
            dq_ref[hh] = _dot(ds, k_ref[hh])
            dk_ref[hh] += _dot_tn(ds, q_ref[hh])
            dv_ref[...] += _dot_tn(p.astype(BF16), doh)

    return pl.pallas_call(
        body, name="attn_bwd", grid=(nb, HEADS // 2, nq),
        in_specs=[pl.BlockSpec((None, 2, tq, HEAD_PAD), lambda b, j, i: (b, j, i, 0)),
                  pl.BlockSpec((None, 2, sk, HEAD_PAD), lambda b, j, i: (b, j, 0, 0)),
                  pl.BlockSpec((None, None, sk, HEAD_PAD), lambda b, j, i: (b, j, 0, 0)),
                  pl.BlockSpec((tq, HEAD_PAD), lambda b, j, i: (b * nq + i, j))],
        out_specs=[pl.BlockSpec((None, 2, tq, HEAD_PAD), lambda b, j, i: (b, j, i, 0)),
                   pl.BlockSpec((None, 2, sk, HEAD_PAD), lambda b, j, i: (b, j, 0, 0)),
                   pl.BlockSpec((None, None, sk, HEAD_PAD), lambda b, j, i: (b, j, 0, 0))],
        out_shape=[_sds(q.shape, F32), _sds(k.shape, F32), _sds(v.shape, F32)], compiler_params=_params(3),
    )(q, k, v, do)


def _group_masks(rows):
    lane = lax.broadcasted_iota(jnp.int32, (rows, GMLP_W), 1)
    return [(lane >= g * GROUP_DIM) & (lane < (g + 1) * GROUP_DIM) for g in range(GROUPS)]


def _gmlp_fwd(proj, t, wcat, bias, vnw, ones, tm):
    def body(u_ref, v_ref, wcat_ref, bias_ref, vnw_ref, ones_ref, o_ref):
        masks = _group_masks(CHUNK)
        gv = _gelu(v_ref[...])
        rv = lax.rsqrt(_group_sum(gv * gv, ones_ref) * (1.0 / GROUP_DIM) + EPS)
        vnb = (gv * rv * vnw_ref[...]).astype(BF16)
        for c in range(tm // CHUNK):
            rows = slice(c * CHUNK, (c + 1) * CHUNK)
            vc = vnb[rows]
            stack = jnp.concatenate([jnp.where(m, vc, jnp.zeros_like(vc)) for m in masks], axis=0)
            sp = _dot(wcat_ref[...], stack) + bias_ref[...]
            o_ref[rows, :] = (_gelu(u_ref[rows, :]) * sp).astype(BF16)

    return pl.pallas_call(
        body, name="gmlp_fwd", grid=(t // tm,),
        in_specs=[pl.BlockSpec((tm, GMLP_W), lambda i: (i, 1)), pl.BlockSpec((tm, GMLP_W), lambda i: (i, 2)),
                  _whole(wcat.shape), _whole(bias.shape), _whole(vnw.shape), _whole(ones.shape)],
        out_specs=pl.BlockSpec((tm, GMLP_W), lambda i: (i, 0)),
        out_shape=_sds((t, GMLP_W), BF16), compiler_params=_params(1),
    )(proj, proj, wcat, bias, vnw, ones)


def _gmlp_bwd(proj, dsg, wcat, wcat_t, bias, vnw, ones, tm):
    t = dsg.shape[0]

    def body(u_ref, v_ref, dsg_ref, wcat_ref, wcatt_ref, bias_ref, vnw_ref, ones_ref,
             du_ref, dv_ref, dws_ref, dbs_ref, dvnw_ref):
        @pl.when(pl.program_id(0) == 0)
        def _():
            dws_ref[...] = jnp.zeros_like(dws_ref)
            dbs_ref[...] = jnp.zeros_like(dbs_ref)
            dvnw_ref[...] = jnp.zeros_like(dvnw_ref)

        masks = _group_masks(CHUNK)
        v = v_ref[...]
        gv = _gelu(v)
        rv = lax.rsqrt(_group_sum(gv * gv, ones_ref) * (1.0 / GROUP_DIM) + EPS)
        xh = gv * rv
        vnb = (xh * vnw_ref[...]).astype(BF16)
        dvn_parts = []
        for c in range(tm // CHUNK):
            rows = slice(c * CHUNK, (c + 1) * CHUNK)
            vc = vnb[rows]
            stack = jnp.concatenate([jnp.where(m, vc, jnp.zeros_like(vc)) for m in masks], axis=0)
            sp = _dot(wcat_ref[...], stack) + bias_ref[...]
            u = u_ref[rows, :]
            dsg_c = dsg_ref[rows, :]
            du_ref[rows, :] = (dsg_c * sp * _gelu_grad(u)).astype(BF16)
            ds = dsg_c * _gelu(u)
            dstack = jnp.concatenate([jnp.where(m, ds, 0.0) for m in masks], axis=0)
            dbs_ref[...] += jnp.broadcast_to(jnp.sum(dstack, axis=-1, keepdims=True), dbs_ref.shape)
            dstb = dstack.astype(BF16)
            dvn_parts.append(_dot(wcatt_ref[...], dstb))
            dws_ref[...] += _dot_nt(dstb, vc)
        dvn = jnp.concatenate(dvn_parts, axis=0) if len(dvn_parts) > 1 else dvn_parts[0]
        dvnw_ref[...] += jnp.sum(dvn * xh, axis=0, keepdims=True)
        dxh = dvn * vnw_ref[...]
        gm = _group_sum(dxh * xh, ones_ref) * (1.0 / GROUP_DIM)
        dv_ref[...] = (rv * (dxh - xh * gm) * _gelu_grad(v)).astype(BF16)

    row = pl.BlockSpec((tm, GMLP_W), lambda i: (i, 0))
    return pl.pallas_call(
        body, name="gmlp_bwd", grid=(t // tm,),
        in_specs=[pl.BlockSpec((tm, GMLP_W), lambda i: (i, 1)), pl.BlockSpec((tm, GMLP_W), lambda i: (i, 2)), row,
                  _whole(wcat.shape), _whole(wcat_t.shape), _whole(bias.shape), _whole(vnw.shape), _whole(ones.shape)],
        out_specs=[row, row, pl.BlockSpec((GROUPS * CHUNK, CHUNK), lambda i: (0, 0)),
                   pl.BlockSpec((GROUPS * CHUNK, CHUNK), lambda i: (0, 0)), pl.BlockSpec((1, GMLP_W), lambda i: (0, 0))],
        out_shape=[_sds((t, GMLP_W), BF16), _sds((t, GMLP_W), BF16), _sds((GROUPS * CHUNK, CHUNK), F32),
                   _sds((GROUPS * CHUNK, CHUNK), F32), _sds((1, GMLP_W), F32)],
        compiler_params=_params(1),
    )(proj, proj, dsg, wcat, wcat_t, bias, vnw, ones)


def _mixout_fwd(o, sg, xs, mod, wout, s, tm):
    t = o.shape[0]
    d = xs.shape[1]

    def body(o_ref, sg_ref, x_ref, mod_ref, w_ref, x2_ref, mix_ref):
        g = (pl.program_id(0) * tm) // s
        gate = mod_ref[g, pl.ds(5, 1), :]
        mix = _dot(o_ref[...], w_ref[0:MLA_W, :]) + _dot(sg_ref[...], w_ref[MLA_W:MLA_W + GMLP_W, :])
        x2_ref[...] = x_ref[...] + gate * mix
        mix_ref[...] = mix.astype(BF16)

    row = lambda i: (i, 0)
    return pl.pallas_call(
        body, name="mixout_fwd", grid=(t // tm,),
        in_specs=[pl.BlockSpec((tm, MLA_W), row), pl.BlockSpec((tm, GMLP_W), row), pl.BlockSpec((tm, d), row),
                  _whole(mod.shape), _whole(wout.shape)],
        out_specs=[pl.BlockSpec((tm, d), row), pl.BlockSpec((tm, d), row)],
        out_shape=[_sds((t, d), F32), _sds((t, d), BF16)], compiler_params=_params(1),
    )(o, sg, xs, mod, wout)


def _mixout_bwd(dx2, mix, mod, wout, s, tm):
    t, d = dx2.shape

    def body(dx_ref, mix_ref, mod_ref, w_ref, dmix_ref, do_ref, dsg_ref, dmod_ref):
        i = pl.program_id(0)

        @pl.when(i == 0)
        def _():
            dmod_ref[...] = jnp.zeros_like(dmod_ref)

        g = (i * tm) // s
        gate = mod_ref[g, pl.ds(5, 1), :]
        dx = dx_ref[...]
        dmod_ref[g, pl.ds(5, 1), :] += jnp.sum(dx * mix_ref[...].astype(F32), axis=0, keepdims=True)
        dmb = (gate * dx).astype(BF16)
        dmix_ref[...] = dmb
        do_ref[...] = _dot_nt(dmb, w_ref[0:MLA_W, :]).astype(BF16)
        dsg_ref[...] = _dot_nt(dmb, w_ref[MLA_W:MLA_W + GMLP_W, :])

    row = lambda i: (i, 0)
    return pl.pallas_call(
        body, name="mixout_bwd", grid=(t // tm,),
        in_specs=[pl.BlockSpec((tm, d), row), pl.BlockSpec((tm, d), row), _whole(mod.shape), _whole(wout.shape)],
        out_specs=[pl.BlockSpec((tm, d), row), pl.BlockSpec((tm, MLA_W), row), pl.BlockSpec((tm, GMLP_W), row),
                   pl.BlockSpec(mod.shape, lambda i: (0, 0, 0))],
        out_shape=[_sds((t, d), BF16), _sds((t, MLA_W), BF16), _sds((t, GMLP_W), F32), _sds(mod.shape, F32)],
        compiler_params=_params(1),
    )(dx2, mix, mod, wout)


def _loss_head(yv, target, tm):
    t, d = yv.shape
    nsteps = t // tm

    def body(y_ref, t_ref, dy_ref, loss_ref, acc_ref):
        i = pl.program_id(0)

        @pl.when(i == 0)
        def _():
            acc_ref[...] = jnp.zeros_like(acc_ref)

        e = y_ref[...] - t_ref[...]
        dy_ref[...] = e * (1.0 / d)
        acc_ref[...] += jnp.sum(e * e, axis=0, keepdims=True)

        @pl.when(i == nsteps - 1)
        def _():
            loss_ref[...] = (0.5 / d) * jnp.sum(acc_ref[...], axis=-1, keepdims=True)

    row = lambda i: (i, 0)
    return pl.pallas_call(
        body, name="loss_head", grid=(nsteps,),
        in_specs=[pl.BlockSpec((tm, d), row), pl.BlockSpec((tm, d), row)],
        out_specs=[pl.BlockSpec((tm, d), row), pl.BlockSpec((1, 1), lambda i: (0, 0))],
        out_shape=[_sds((t, d), F32), _sds((1, 1), F32)],
        scratch_shapes=[pltpu.VMEM((1, d), F32)], compiler_params=_params(1),
    )(yv, target)


def _other_chips(x, y):
    return [(1 - x, y), (x, 1 - y), (1 - x, 1 - y)]


def _gather_chips(pack):
    def body(src_ref, out_ref, send_sems, recv_sems, local_sem):
        x, y, c = lax.axis_index("x"), lax.axis_index("y"), lax.axis_index("c")
        me = 2 * x + y
        mine = pltpu.make_async_copy(src_ref, out_ref.at[me], local_sem)
        mine.start()
        sends = []
        for k, (px, py) in enumerate(_other_chips(x, y)):
            sends.append(pltpu.make_async_remote_copy(
                src_ref=src_ref, dst_ref=out_ref.at[me], send_sem=send_sems.at[k], recv_sem=recv_sems.at[k],
                device_id=(px, py, c), device_id_type=pl.DeviceIdType.MESH))
            sends[-1].start()
        for k, (px, py) in enumerate(_other_chips(x, y)):
            pltpu.make_async_remote_copy(
                src_ref=src_ref, dst_ref=out_ref.at[2 * px + py], send_sem=send_sems.at[k], recv_sem=recv_sems.at[k],
                device_id=(px, py, c), device_id_type=pl.DeviceIdType.MESH).wait_recv()
        for cp in sends:
            cp.wait_send()
        mine.wait()

    return pl.pallas_call(
        body, name="gather_chips",
        in_specs=[pl.BlockSpec(memory_space=pl.ANY)], out_specs=pl.BlockSpec(memory_space=pl.ANY),
        out_shape=_sds((N_CHIPS,) + pack.shape, pack.dtype),
        scratch_shapes=[pltpu.SemaphoreType.DMA((3,)), pltpu.SemaphoreType.DMA((3,)), pltpu.SemaphoreType.DMA(())],
    )(pack)


def _scatter_chips(pack):
    def body(src_ref, out_ref, send_sems, recv_sems, local_sem):
        x, y, c = lax.axis_index("x"), lax.axis_index("y"), lax.axis_index("c")
        me = 2 * x + y
        mine = pltpu.make_async_copy(src_ref.at[me], out_ref.at[me], local_sem)
        mine.start()
        sends = []
        for k, (px, py) in enumerate(_other_chips(x, y)):
            sends.append(pltpu.make_async_remote_copy(
                src_ref=src_ref.at[2 * px + py], dst_ref=out_ref.at[me], send_sem=send_sems.at[k],
                recv_sem=recv_sems.at[k], device_id=(px, py, c), device_id_type=pl.DeviceIdType.MESH))
            sends[-1].start()
        for k, (px, py) in enumerate(_other_chips(x, y)):
            pltpu.make_async_remote_copy(
                src_ref=src_ref.at[me], dst_ref=out_ref.at[2 * px + py], send_sem=send_sems.at[k],
                recv_sem=recv_sems.at[k], device_id=(px, py, c), device_id_type=pl.DeviceIdType.MESH).wait_recv()
        for cp in sends:
            cp.wait_send()
        mine.wait()

    return pl.pallas_call(
        body, name="scatter_chips",
        in_specs=[pl.BlockSpec(memory_space=pl.ANY)], out_specs=pl.BlockSpec(memory_space=pl.ANY),
        out_shape=_sds(pack.shape, pack.dtype),
        scratch_shapes=[pltpu.SemaphoreType.DMA((3,)), pltpu.SemaphoreType.DMA((3,)), pltpu.SemaphoreType.DMA(())],
    )(pack)


def _swap_cores(part):
    def body(src_ref, out_ref, send_sem, recv_sem):
        x, y, c = lax.axis_index("x"), lax.axis_index("y"), lax.axis_index("c")
        cp = pltpu.make_async_remote_copy(
            src_ref=src_ref, dst_ref=out_ref, send_sem=send_sem, recv_sem=recv_sem,
            device_id=(x, y, 1 - c), device_id_type=pl.DeviceIdType.MESH)
        cp.start()
        cp.wait()

    return pl.pallas_call(
        body, name="swap_cores",
        in_specs=[pl.BlockSpec(memory_space=pl.ANY)], out_specs=pl.BlockSpec(memory_space=pl.ANY),
        out_shape=_sds(part.shape, part.dtype),
        scratch_shapes=[pltpu.SemaphoreType.DMA(()), pltpu.SemaphoreType.DMA(())],
    )(part)


def _sum_slots(recv, tr):
    _, r, w = recv.shape

    def body(r_ref, o_ref):
        o_ref[...] = ((r_ref[0] + r_ref[1]) + r_ref[2]) + r_ref[3]

    return pl.pallas_call(
        body, name="sum_slots", grid=(r // tr,),
        in_specs=[pl.BlockSpec((N_CHIPS, tr, w), lambda i: (0, i, 0))],
        out_specs=pl.BlockSpec((tr, w), lambda i: (i, 0)),
        out_shape=_sds((r, w), F32), compiler_params=_params(1),
    )(recv)


def _adamw(p0, p1, w, m, v, tr):
    r, wd = w.shape
    c1 = 1.0 / (1.0 - ADAM_B1 ** ADAM_STEP)
    c2 = 1.0 / (1.0 - ADAM_B2 ** ADAM_STEP)

    def body(p0_ref, p1_ref, w_ref, m_ref, v_ref, g_ref, d_ref, nm_ref, nv_ref):
        g = p0_ref[...] + p1_ref[...]
        nm = ADAM_B1 * m_ref[...] + (1.0 - ADAM_B1) * g
        nv = ADAM_B2 * v_ref[...] + (1.0 - ADAM_B2) * (g * g)
        g_ref[...] = g
        nm_ref[...] = nm
        nv_ref[...] = nv
        d_ref[...] = -ADAM_LR * ((nm * c1) / (jnp.sqrt(nv * c2) + ADAM_EPS) + ADAM_WD * w_ref[...])

    spec = pl.BlockSpec((tr, wd), lambda i: (i, 0))
    return pl.pallas_call(
        body, name="adamw", grid=(r // tr,),
        in_specs=[spec] * 5, out_specs=[spec] * 4, out_shape=[_sds((r, wd), F32)] * 4, compiler_params=_params(1),
    )(p0, p1, w, m, v)


def _rope_tables(s, ctx):
    pos = jnp.arange(s, dtype=F32)
    inv = ROPE_BASE ** (-jnp.arange(0, QK_ROPE // 2, 2, dtype=F32) / (QK_ROPE // 2))
    ang_r = jnp.floor(pos / GRID_W)[:, None] * inv
    ang_c = (pos - GRID_W * jnp.floor(pos / GRID_W))[:, None] * inv
    ang = jnp.concatenate([ang_r, ang_r, ang_c, ang_c], axis=-1)
    cos, sin = jnp.cos(ang), jnp.sin(ang)
    half_b = (jnp.arange(QK_ROPE) // 8) % 2 == 1
    sin_a = jnp.where(half_b, sin, 0.0)
    sin_b = jnp.where(half_b, 0.0, -sin)

    def place(tab, fill):
        full = jnp.full((s + ctx, HEAD_PAD), fill, F32)
        return full.at[:s, QK_NOPE:QK_HEAD].set(tab)

    return place(cos, 1.0), place(sin_a, 0.0), place(sin_b, 0.0)


def _pad_last(a, n):
    return jnp.pad(a, [(0, 0)] * (a.ndim - 1) + [(0, n - a.shape[-1])])


def _flat_rows(parts, rows, width):
    flat = jnp.concatenate([p.reshape(-1) for p in parts])
    return jnp.pad(flat, (0, rows * width - flat.shape[0])).reshape(rows, width)


def kernel(x, c, ctx, c_ctx, w_ada, b_ada, norm1_w, ffn1_w1, ffn1_w3, ffn1_w2, norm2_w, w_in, q_a_norm_w, w_uq, kv_a_norm_w, w_ukv, q_norm_w, k_norm_w, v_norm_w, w_s, b_s, w_out, norm3_w, ffn2_w1, ffn2_w3, ffn2_w2, loss_target, m_c_ctx, m_w_ada, m_b_ada, m_norm1_w, m_ffn1_w1, m_ffn1_w3, m_ffn1_w2, m_norm2_w, m_w_in, m_q_a_norm_w, m_w_uq, m_kv_a_norm_w, m_w_ukv, m_q_norm_w, m_k_norm_w, m_v_norm_w, m_w_s, m_b_s, m_w_out, m_norm3_w, m_ffn2_w1, m_ffn2_w3, m_ffn2_w2, v_c_ctx, v_w_ada, v_b_ada, v_norm1_w, v_ffn1_w1, v_ffn1_w3, v_ffn1_w2, v_norm2_w, v_w_in, v_q_a_norm_w, v_w_uq, v_kv_a_norm_w, v_w_ukv, v_q_norm_w, v_k_norm_w, v_v_norm_w, v_w_s, v_b_s, v_w_out, v_norm3_w, v_ffn2_w1, v_ffn2_w3, v_ffn2_w2):
    wts = dict(c_ctx=c_ctx, w_ada=w_ada, b_ada=b_ada, norm1_w=norm1_w, ffn1_w1=ffn1_w1, ffn1_w3=ffn1_w3, ffn1_w2=ffn1_w2,
               norm2_w=norm2_w, w_in=w_in, q_a_norm_w=q_a_norm_w, w_uq=w_uq, kv_a_norm_w=kv_a_norm_w, w_ukv=w_ukv,
               q_norm_w=q_norm_w, k_norm_w=k_norm_w, v_norm_w=v_norm_w, w_s=w_s, b_s=b_s, w_out=w_out, norm3_w=norm3_w,
               ffn2_w1=ffn2_w1, ffn2_w3=ffn2_w3, ffn2_w2=ffn2_w2)
    moms = dict(c_ctx=m_c_ctx, w_ada=m_w_ada, b_ada=m_b_ada, norm1_w=m_norm1_w, ffn1_w1=m_ffn1_w1, ffn1_w3=m_ffn1_w3,
                ffn1_w2=m_ffn1_w2, norm2_w=m_norm2_w, w_in=m_w_in, q_a_norm_w=m_q_a_norm_w, w_uq=m_w_uq,
                kv_a_norm_w=m_kv_a_norm_w, w_ukv=m_w_ukv, q_norm_w=m_q_norm_w, k_norm_w=m_k_norm_w, v_norm_w=m_v_norm_w,
                w_s=m_w_s, b_s=m_b_s, w_out=m_w_out, norm3_w=m_norm3_w, ffn2_w1=m_ffn2_w1, ffn2_w3=m_ffn2_w3,
                ffn2_w2=m_ffn2_w2)
    vars_ = dict(c_ctx=v_c_ctx, w_ada=v_w_ada, b_ada=v_b_ada, norm1_w=v_norm1_w, ffn1_w1=v_ffn1_w1, ffn1_w3=v_ffn1_w3,
                 ffn1_w2=v_ffn1_w2, norm2_w=v_norm2_w, w_in=v_w_in, q_a_norm_w=v_q_a_norm_w, w_uq=v_w_uq,
                 kv_a_norm_w=v_kv_a_norm_w, w_ukv=v_w_ukv, q_norm_w=v_q_norm_w, k_norm_w=v_k_norm_w, v_norm_w=v_v_norm_w,
                 w_s=v_w_s, b_s=v_b_s, w_out=v_w_out, norm3_w=v_norm3_w, ffn2_w1=v_ffn2_w1, ffn2_w3=v_ffn2_w3,
                 ffn2_w2=v_ffn2_w2)

    nb, s, d = x.shape
    nctx = ctx.shape[1]
    t, tc = nb * s, nb * nctx
    t_all = t + tc
    sk = s + nctx
    assert nb + 1 <= MOD_ROWS and d % LANES == 0
    tm = _token_tile(s, nctx)
    pw = d

    shard_sizes = [wts[n].size for n in SHARDED]
    rows_sharded = sum(shard_sizes) // pw
    assert sum(shard_sizes) % pw == 0
    rows_g = _round_up(rows_sharded, 16)
    gathered = _gather_chips(_flat_rows([wts[n].astype(BF16) for n in SHARDED], rows_g, pw))
    gathered = gathered.reshape(N_CHIPS, rows_g * pw)
    full = {}
    off = 0
    for n, size in zip(SHARDED, shard_sizes):
        r_, c_ = wts[n].shape[1:]
        piece = gathered[:, off:off + size].reshape(N_CHIPS, r_, c_)
        off += size
        if n in ROW_SHARDED:
            full[n] = piece.reshape(N_CHIPS * r_, c_)
        else:
            full[n] = piece.transpose(1, 0, 2).reshape(r_, N_CHIPS * c_)

    f = full["ffn1_w1"].shape[1]
    wi = full["w_in"]
    wp = jnp.concatenate([wi[:, 0:KV_LORA], jnp.zeros((d, QK_NOPE), BF16), wi[:, KV_LORA:KV_LORA + QK_ROPE],
                          jnp.zeros((d, HEAD_PAD - QK_HEAD), BF16), wi[:, KV_LORA + QK_ROPE:]], axis=1)
    wq = _pad_last(full["w_uq"].reshape(Q_LORA, HEADS, QK_HEAD).transpose(1, 0, 2), HEAD_PAD)
    wkv = full["w_ukv"].reshape(KV_LORA, HEADS, QK_NOPE + V_HEAD)
    wk = _pad_last(wkv[:, :, :QK_NOPE].transpose(1, 0, 2), HEAD_PAD)
    wv = wkv[:, :, QK_NOPE:].reshape(KV_LORA, HEADS // 2, 2 * V_HEAD).transpose(1, 0, 2)
    wsb = w_s[0].astype(BF16)
    wcat = wsb.transpose(1, 0, 2).reshape(CHUNK, GROUPS * CHUNK)
    wcat_t = wsb.transpose(2, 0, 1).reshape(CHUNK, GROUPS * CHUNK)
    bias = jnp.repeat(b_s[0].T, GROUP_DIM, axis=1)
    vnw = v_norm_w.reshape(1, GMLP_W)
    lane = jnp.arange(GMLP_W)
    ones = (lane[:, None] // GROUP_DIM == lane[None, :] // GROUP_DIM).astype(BF16)
    qnw = _pad_last(q_norm_w, HEAD_PAD)
    knw = _pad_last(k_norm_w, HEAD_PAD)
    tabs = _rope_tables(s, nctx)

    cc = jnp.concatenate([c, c_ctx[None, :], jnp.zeros((MOD_ROWS - nb - 1, d), F32)], axis=0)
    mod = _ada_fwd(cc, full["w_ada"], b_ada).reshape(MOD_ROWS, N_MOD, d)
    xs0 = jnp.concatenate([x.reshape(t, d), ctx.reshape(tc, d)], axis=0)
    xs1, a1, b1, y1 = _ffn_fwd(xs0, mod, norm1_w, full["ffn1_w1"], full["ffn1_w3"], full["ffn1_w2"], 0, s, nb, tm,
                               "ffn1_fwd")
    h2, proj = _mixin_fwd(xs1, mod, norm2_w, wp, s, nb, tm)
    prep_w = (wq, wk, wv, kv_a_norm_w, q_a_norm_w, qnw, knw)
    q, k_lat, v_lat = _prep_fwd(proj, 0, nb, s, 0, tabs, *prep_w, tm, True, "prep_fwd")
    k_ctx, v_ctx = _prep_fwd(proj, t // tm, nb, nctx, s // tm, tabs, *prep_w, tm, False, "prep_ctx_fwd")
    k_all = jnp.concatenate([k_lat, k_ctx], axis=2)
    v_all = jnp.concatenate([v_lat, v_ctx], axis=2)
    o = _attn_fwd(q, k_all, v_all, tm)
    sg = _gmlp_fwd(proj, t, wcat, bias, vnw, ones, tm)
    x2, mix = _mixout_fwd(o, sg, xs1, mod, full["w_out"], s, tm)
    yv, a2, b2, y2 = _ffn_fwd(x2, mod, norm3_w, full["ffn2_w1"], full["ffn2_w3"], full["ffn2_w2"], 6, s, nb, tm,
                              "ffn2_fwd")
    dy, loss_part = _loss_head(yv, loss_target.reshape(t, d), tm)
    loss = lax.psum(loss_part[0, 0], ("x", "y", "c"))

    grads = {}
    dx2, h3, g2, da2, db2, dyb2, dmod_c, grads["norm3_w"] = _ffn_bwd(
        dy, x2, a2, b2, y2, mod, norm3_w, full["ffn2_w1"], full["ffn2_w3"], full["ffn2_w2"], 6, s, nb, tm, "ffn2_bwd")
    grads["ffn2_w1"] = _mm_tn(h3, da2, t, "ffn2_dw1")
    grads["ffn2_w3"] = _mm_tn(h3, db2, t, "ffn2_dw3")
    grads["ffn2_w2"] = _mm_tn(g2, dyb2, t, "ffn2_dw2")
    dmix, do, dsg, dmod_b = _mixout_bwd(dx2, mix, mod, full["w_out"], s, tm)
    grads["w_out"] = jnp.concatenate([_mm_tn(o, dmix, t, "wout_dw_attn"), _mm_tn(sg, dmix, t, "wout_dw_gmlp")], axis=0)
    dpu, dpv, dws, dbs, dvnw = _gmlp_bwd(proj, dsg, wcat, wcat_t, bias, vnw, ones, tm)
    dq, dk, dv = _attn_bwd(q, k_all, v_all, do, tm)
    dp0_c, dwk_c, dwv_c, dkvaw_c, dknw_c = _prep_bwd(
        proj, t // tm, nb, nctx, s // tm, s // tm, tabs, *prep_w, None, dk, dv, None, tm, "prep_ctx_bwd")
    dp0_l, dwq, dqaw, dqnw, dwk, dwv, dkvaw, dknw = _prep_bwd(
        proj, 0, nb, s, 0, 0, tabs, *prep_w, dq, dk, dv, [dwk_c, dwv_c, dkvaw_c, dknw_c], tm, "prep_bwd")
    dp0 = jnp.concatenate([dp0_l, dp0_c], axis=0)
    dxs1, dmod_a, grads["norm2_w"] = _mixin_bwd(dp0, dpu, dpv, xs1, dx2, mod, norm2_w, wp, s, nb, tm)
    dwp = jnp.concatenate([_mm_tn(h2, dp0, t_all, "win_dw_kvq"), _mm_tn(h2, dpu, t, "win_dw_u"),
                           _mm_tn(h2, dpv, t, "win_dw_v")], axis=1)
    grads["w_in"] = jnp.concatenate([dwp[:, 0:KV_LORA], dwp[:, KV_LORA + QK_NOPE:KV_LORA + QK_HEAD], dwp[:, 256:]], axis=1)
    dxs0, h1, g1, da1, db1, dyb1, dmod_0, grads["norm1_w"] = _ffn_bwd(
        dxs1, xs0, a1, b1, y1, mod, norm1_w, full["ffn1_w1"], full["ffn1_w3"], full["ffn1_w2"], 0, s, nb, tm, "ffn1_bwd")
    grads["ffn1_w1"] = _mm_tn(h1, da1, t_all, "ffn1_dw1")
    grads["ffn1_w3"] = _mm_tn(h1, db1, t_all, "ffn1_dw3")
    grads["ffn1_w2"] = _mm_tn(g1, dyb1, t_all, "ffn1_dw2")
    dmods = [m_.reshape(MOD_ROWS, N_MOD * d) for m_ in (dmod_0, dmod_a, dmod_b, dmod_c)]
    grads["w_ada"], grads["b_ada"], dcc = _ada_bwd(cc, dmods, full["w_ada"])
    grads["c_ctx"] = dcc[nb]
    grads["w_uq"] = dwq[:, :, :QK_HEAD].transpose(1, 0, 2).reshape(Q_LORA, HEADS * QK_HEAD)
    grads["w_ukv"] = jnp.concatenate(
        [dwk[:, :, :QK_NOPE].transpose(1, 0, 2),
         dwv.transpose(1, 0, 2).reshape(KV_LORA, HEADS, V_HEAD)], axis=2).reshape(KV_LORA, HEADS * (QK_NOPE + V_HEAD))
    grads["q_a_norm_w"], grads["kv_a_norm_w"] = dqaw, dkvaw
    grads["q_norm_w"], grads["k_norm_w"] = dqnw[:, :QK_HEAD], dknw[:, :QK_HEAD]
    grads["v_norm_w"], grads["w_s"], grads["b_s"] = dvnw, dws, dbs[:, 0]
    grad_x = dxs0[:t].reshape(nb, s, d)

    small_sizes = [wts[n].size for n in SMALL]
    rows_t = _round_up(rows_sharded + -(-sum(small_sizes) // pw), 256)
    per_chip = []
    for n in SHARDED:
        g_ = grads[n]
        if n in ROW_SHARDED:
            per_chip.append(g_.reshape(N_CHIPS, -1))
        else:
            r_, cols = g_.shape
            per_chip.append(g_.reshape(r_, N_CHIPS, cols // N_CHIPS).transpose(1, 0, 2).reshape(N_CHIPS, -1))
    small_flat = jnp.concatenate([grads[n].reshape(-1) for n in SMALL])
    small_flat = jnp.pad(small_flat, (0, (rows_t - rows_sharded) * pw - small_flat.shape[0]))
    gpack = jnp.concatenate(per_chip + [jnp.broadcast_to(small_flat, (N_CHIPS, small_flat.shape[0]))], axis=1)
    part = _sum_slots(_scatter_chips(gpack.reshape(N_CHIPS, rows_t, pw)), 256)
    order = SHARDED + SMALL
    g_p, d_p, m_p, v_p = _adamw(part, _swap_cores(part), _flat_rows([wts[n] for n in order], rows_t, pw),
                                _flat_rows([moms[n] for n in order], rows_t, pw),
                                _flat_rows([vars_[n] for n in order], rows_t, pw), 256)
    outs = []
    for packed in (g_p, d_p, m_p, v_p):
        flat = packed.reshape(-1)
        got = {}
        off = 0
        for n in order:
            got[n] = flat[off:off + wts[n].size].reshape(wts[n].shape)
            off += wts[n].size
        outs.append([got[n] for n in WEIGHTS])
    return (loss, grad_x, *outs[0], *outs[1], *outs[2], *outs[3])
```

```python
import functools
import math

import jax
import jax.numpy as jnp
import numpy as np
from jax import lax
from jax.experimental import pallas as pl
from jax.experimental.pallas import tpu as pltpu

F32 = jnp.float32
BF16 = jnp.bfloat16

EPS = 1e-6
N_MOD = 9
HEADS = 8
QK_NOPE, QK_ROPE, V_HEAD = 64, 32, 64
QK_HEAD = QK_NOPE + QK_ROPE
HEAD_PAD = 128
Q_LORA, KV_LORA = 256, 128
GROUPS, GROUP_DIM, CHUNK = 8, 64, 128
GMLP_W = GROUPS * GROUP_DIM
MLA_W = HEADS * V_HEAD
IN_COLS = 1440
PROJ_COLS = 1536
GRID_W = 64
ROPE_BASE = 10000.0
MOD_ROWS = 16
ADAM_LR, ADAM_B1, ADAM_B2, ADAM_EPS, ADAM_WD, ADAM_STEP = 0.001, 0.9, 0.999, 1e-08, 0.01, 10
N_CHIPS = 4
LANES = 128
V7X_VMEM_LIMIT = 56 * 1024 * 1024
GELU_C = math.sqrt(2.0 / math.pi)

SHARDED = ("w_ada", "ffn1_w1", "ffn1_w3", "ffn1_w2", "w_in", "w_uq", "w_ukv", "w_out", "ffn2_w1", "ffn2_w3", "ffn2_w2")
ROW_SHARDED = ("ffn1_w2", "w_out", "ffn2_w2")
FIRST_WEIGHTS = ("w_ada", "ffn1_w1", "ffn1_w3", "ffn1_w2")
MIX_WEIGHTS = ("w_in", "w_uq", "w_ukv", "w_out")
LAST_WEIGHTS = ("ffn2_w1", "ffn2_w3", "ffn2_w2")
SMALL = ("c_ctx", "b_ada", "norm1_w", "norm2_w", "q_a_norm_w", "kv_a_norm_w", "q_norm_w", "k_norm_w", "v_norm_w",
         "w_s", "b_s", "norm3_w")
WEIGHTS = ("c_ctx", "w_ada", "b_ada", "norm1_w", "ffn1_w1", "ffn1_w3", "ffn1_w2", "norm2_w", "w_in", "q_a_norm_w",
           "w_uq", "kv_a_norm_w", "w_ukv", "q_norm_w", "k_norm_w", "v_norm_w", "w_s", "b_s", "w_out", "norm3_w",
           "ffn2_w1", "ffn2_w3", "ffn2_w2")


def _round_up(n, m):
    return (n + m - 1) // m * m


def _div_tile(n, target, mult):
    best = None
    for t in range(mult, min(n, target) + 1, mult):
        if n % t == 0:
            best = t
    return n if best is None else best


def _dot(a, b):
    return lax.dot_general(a, b, (((1,), (0,)), ((), ())), preferred_element_type=F32)


def _dot_nt(a, b):
    return lax.dot_general(a, b, (((1,), (1,)), ((), ())), preferred_element_type=F32)


def _dot_tn(a, b):
    return lax.dot_general(a, b, (((0,), (0,)), ((), ())), preferred_element_type=F32)


def _sigmoid(x):
    return 1.0 / (1.0 + jnp.exp(-x))


def _gelu(x):
    return 0.5 * x * (1.0 + jnp.tanh(GELU_C * (x + 0.044715 * x * x * x)))


def _gelu_grad(x):
    t = jnp.tanh(GELU_C * (x + 0.044715 * x * x * x))
    return 0.5 * (1.0 + t) + 0.5 * x * (1.0 - t * t) * (GELU_C * (1.0 + 3 * 0.044715 * x * x))


def _rope(x, cos, sin_a, sin_b):
    return x * cos + pltpu.roll(x, 8, 1) * sin_a + pltpu.roll(x, HEAD_PAD - 8, 1) * sin_b


def _rope_t(d, cos, sin_a, sin_b):
    return d * cos + pltpu.roll(d * sin_a, HEAD_PAD - 8, 1) + pltpu.roll(d * sin_b, 8, 1)


def _group_sum(x, ones_ref):
    hi = x.astype(BF16)
    lo = (x - hi.astype(F32)).astype(BF16)
    return _dot(hi, ones_ref[...]) + _dot(lo, ones_ref[...])


def _params(n_axes):
    return pltpu.CompilerParams(dimension_semantics=("arbitrary",) * n_axes, vmem_limit_bytes=V7X_VMEM_LIMIT)


def _whole(shape):
    nd = len(shape)
    return pl.BlockSpec(shape, lambda *_: (0,) * nd, pipeline_mode=pl.Buffered(1))


def _sds(shape, dtype):
    return jax.ShapeDtypeStruct(shape, dtype)


def _token_tile(s, ctx):
    return _div_tile(math.gcd(s, ctx), 256, CHUNK)


def _other_chips(x, y):
    return [(1 - x, y), (x, 1 - y), (1 - x, 1 - y)]


def _exch_copies(kind, srcs, dsts, send_sems, recv_sems, local_sems):
    x, y, c = lax.axis_index("x"), lax.axis_index("y"), lax.axis_index("c")
    me = 2 * x + y
    local, sends, arrivals = [], [], []
    for w, (src, dst) in enumerate(zip(srcs, dsts)):
        own = src if kind == "gather" else src.at[me]
        local.append(pltpu.make_async_copy(own, dst.at[me], local_sems.at[w]))
        for k, (px, py) in enumerate(_other_chips(x, y)):
            sem = dict(send_sem=send_sems.at[3 * w + k], recv_sem=recv_sems.at[3 * w + k], device_id=(px, py, c),
                       device_id_type=pl.DeviceIdType.MESH)
            out = src if kind == "gather" else src.at[2 * px + py]
            sends.append(pltpu.make_async_remote_copy(src_ref=out, dst_ref=dst.at[me], **sem))
            arrivals.append(pltpu.make_async_remote_copy(src_ref=own, dst_ref=dst.at[2 * px + py], **sem))
    return local, sends, arrivals


def _exch_start(kind, srcs, dsts, sems):
    local, sends, _ = _exch_copies(kind, srcs, dsts, *sems)
    for cp in local + sends:
        cp.start()


def _exch_wait(kind, srcs, dsts, sems):
    local, sends, arrivals = _exch_copies(kind, srcs, dsts, *sems)
    for cp in arrivals:
        cp.wait_recv()
    for cp in sends:
        cp.wait_send()
    for cp in local:
        cp.wait()


def _exch_scratch(n):
    return [pltpu.SemaphoreType.DMA((3 * n,)), pltpu.SemaphoreType.DMA((3 * n,)), pltpu.SemaphoreType.DMA((n,))]


def _exch_shapes(kind, arrays):
    return [_sds((N_CHIPS,) + a.shape if kind == "gather" else a.shape, a.dtype) for a in arrays]


def _hosted_call(body, name, grid, in_specs, out_specs, out_shape, operands, scratch=(), exch=None):
    n_axes = len(grid)
    if exch is None:
        outs = pl.pallas_call(body, name=name, grid=grid, in_specs=list(in_specs), out_specs=list(out_specs),
                              out_shape=list(out_shape), scratch_shapes=list(scratch),
                              compiler_params=_params(n_axes))(*operands)
        return list(outs), []
    kind, arrays = exch
    n_in, n_out, n_sc, n_ex = len(in_specs), len(out_specs), len(scratch), len(arrays)

    def hosted(*refs):
        cin, ein = refs[:n_in], refs[n_in:n_in + n_ex]
        o0 = n_in + n_ex
        cout, eout = refs[o0:o0 + n_out], refs[o0 + n_out:o0 + n_out + n_ex]
        rest = refs[o0 + n_out + n_ex:]
        csc, sems = rest[:n_sc], rest[n_sc:]
        first = functools.reduce(jnp.logical_and, [pl.program_id(a) == 0 for a in range(n_axes)])
        last = functools.reduce(jnp.logical_and, [pl.program_id(a) == grid[a] - 1 for a in range(n_axes)])

        @pl.when(first)
        def _():
            _exch_start(kind, ein, eout, sems)

        body(*cin, *cout, *csc)

        @pl.when(last)
        def _():
            _exch_wait(kind, ein, eout, sems)

    any_spec = pl.BlockSpec(memory_space=pl.ANY)
    outs = pl.pallas_call(
        hosted, name=name, grid=grid, in_specs=list(in_specs) + [any_spec] * n_ex,
        out_specs=list(out_specs) + [any_spec] * n_ex, out_shape=list(out_shape) + _exch_shapes(kind, arrays),
        scratch_shapes=list(scratch) + _exch_scratch(n_ex), compiler_params=_params(n_axes),
    )(*operands, *arrays)
    return list(outs[:n_out]), list(outs[n_out:])


def _ada_fwd(cc, w_ada, b_ada):
    d = cc.shape[1]
    n = w_ada.shape[1]
    tn = _div_tile(n, 2304, LANES)

    def body(cc_ref, w_ref, b_ref, o_ref):
        c = cc_ref[...]
        sc = (c * _sigmoid(c)).astype(BF16)
        o_ref[...] = _dot(sc, w_ref[...]) + b_ref[...]

    return pl.pallas_call(
        body, name="ada_fwd", grid=(n // tn,),
        in_specs=[_whole(cc.shape), pl.BlockSpec((d, tn), lambda j: (0, j)), pl.BlockSpec((1, tn), lambda j: (0, j))],
        out_specs=pl.BlockSpec((MOD_ROWS, tn), lambda j: (0, j)),
        out_shape=_sds((MOD_ROWS, n), F32), compiler_params=_params(1),
    )(cc, w_ada, b_ada)


def _ada_bwd(cc, dmods, w_ada, exch=None):
    d = cc.shape[1]
    n = w_ada.shape[1]
    tn = _div_tile(n, 1152, LANES)
    nsteps = n // tn

    def body(cc_ref, m0, m1, m2, m3, w_ref, dw_ref, db_ref, dcc_ref, acc_ref):
        j = pl.program_id(0)

        @pl.when(j == 0)
        def _():
            acc_ref[...] = jnp.zeros_like(acc_ref)

        c = cc_ref[...]
        sig = _sigmoid(c)
        dm = m0[...] + m1[...] + m2[...] + m3[...]
        dmb = dm.astype(BF16)
        db_ref[...] = jnp.sum(dm, axis=0, keepdims=True)
        dw_ref[...] = _dot_tn((c * sig).astype(BF16), dmb)
        acc_ref[...] += _dot_nt(dmb, w_ref[...])

        @pl.when(j == nsteps - 1)
        def _():
            dcc_ref[...] = acc_ref[...] * (sig * (1.0 + c * (1.0 - sig)))

    mspec = pl.BlockSpec((MOD_ROWS, tn), lambda j: (0, j))
    return _hosted_call(
        body, "ada_bwd", (nsteps,),
        [_whole(cc.shape), mspec, mspec, mspec, mspec, pl.BlockSpec((d, tn), lambda j: (0, j))],
        [pl.BlockSpec((d, tn), lambda j: (0, j)), pl.BlockSpec((1, tn), lambda j: (0, j)),
         pl.BlockSpec((MOD_ROWS, d), lambda j: (0, 0))],
        [_sds((d, n), F32), _sds((1, n), F32), _sds((MOD_ROWS, d), F32)],
        (cc, *dmods, w_ada), scratch=[pltpu.VMEM((MOD_ROWS, d), F32)], exch=exch)


def _ffn_fwd(xs, mod, nw, w1, w3, w2, k0, s, nb, tm, name, exch=None):
    t, d = xs.shape
    f = w1.shape[1]

    def body(x_ref, mod_ref, nw_ref, w1_ref, w3_ref, w2_ref, o_ref, a_ref, b_ref, y_ref):
        g = jnp.minimum((pl.program_id(0) * tm) // s, nb)
        shift = mod_ref[g, pl.ds(k0, 1), :]
        scale = mod_ref[g, pl.ds(k0 + 1, 1), :]
        gate = mod_ref[g, pl.ds(k0 + 2, 1), :]
        x = x_ref[...]
        r = lax.rsqrt(jnp.mean(x * x, axis=-1, keepdims=True) + EPS)
        hb = ((x * r * nw_ref[...]) * (1.0 + scale) + shift).astype(BF16)
        a = _dot(hb, w1_ref[...])
        b = _dot(hb, w3_ref[...])
        gb = (a * _sigmoid(a) * b).astype(BF16)
        y = _dot(gb, w2_ref[...])
        o_ref[...] = x + (0.5 * gate) * y
        a_ref[...] = a.astype(BF16)
        b_ref[...] = b.astype(BF16)
        y_ref[...] = y.astype(BF16)

    row = lambda i: (i, 0)
    return _hosted_call(
        body, name, (t // tm,),
        [pl.BlockSpec((tm, d), row), _whole(mod.shape), _whole(nw.shape), _whole(w1.shape), _whole(w3.shape),
         _whole(w2.shape)],
        [pl.BlockSpec((tm, d), row), pl.BlockSpec((tm, f), row), pl.BlockSpec((tm, f), row), pl.BlockSpec((tm, d), row)],
        [_sds((t, d), F32), _sds((t, f), BF16), _sds((t, f), BF16), _sds((t, d), BF16)],
        (xs, mod, nw, w1, w3, w2), exch=exch)


def _ffn_bwd(dout, xs, a, b, y, mod, nw, w1, w3, w2, k0, s, nb, tm, name, exch=None):
    t, d = xs.shape
    f = w1.shape[1]
    nch = 2 if (f // 2) % LANES == 0 and f % 2 == 0 else 1
    fc = f // nch

    def body(do_ref, x_ref, a_ref, b_ref, y_ref, mod_ref, nw_ref, w1_ref, w3_ref, w2_ref,
             dx_ref, h_ref, g_ref, da_ref, db_ref, dy_ref, dmod_ref, dnw_ref):
        i = pl.program_id(0)

        @pl.when(i == 0)
        def _():
            dmod_ref[...] = jnp.zeros_like(dmod_ref)
            dnw_ref[...] = jnp.zeros_like(dnw_ref)

        g = jnp.minimum((i * tm) // s, nb)
        shift = mod_ref[g, pl.ds(k0, 1), :]
        scale = mod_ref[g, pl.ds(k0 + 1, 1), :]
        gate = mod_ref[g, pl.ds(k0 + 2, 1), :]
        x = x_ref[...]
        dout_v = do_ref[...]
        r = lax.rsqrt(jnp.mean(x * x, axis=-1, keepdims=True) + EPS)
        xh = x * r
        n = xh * nw_ref[...]
        h_ref[...] = (n * (1.0 + scale) + shift).astype(BF16)
        dyb = ((0.5 * gate) * dout_v).astype(BF16)
        dy_ref[...] = dyb
        dmod_ref[g, pl.ds(k0 + 2, 1), :] += 0.5 * jnp.sum(dout_v * y_ref[...].astype(F32), axis=0, keepdims=True)
        dh = jnp.zeros((tm, d), F32)
        for c in range(nch):
            sl = slice(c * fc, (c + 1) * fc)
            dg = _dot_nt(dyb, w2_ref[sl, :])
            av = a_ref[:, sl].astype(F32)
            bv = b_ref[:, sl].astype(F32)
            sig = _sigmoid(av)
            sa = av * sig
            g_ref[:, sl] = (sa * bv).astype(BF16)
            dab = (dg * bv * (sig * (1.0 + av * (1.0 - sig)))).astype(BF16)
            dbb = (dg * sa).astype(BF16)
            da_ref[:, sl] = dab
            db_ref[:, sl] = dbb
            dh = dh + _dot_nt(dab, w1_ref[:, sl]) + _dot_nt(dbb, w3_ref[:, sl])
        dmod_ref[g, pl.ds(k0, 1), :] += jnp.sum(dh, axis=0, keepdims=True)
        dmod_ref[g, pl.ds(k0 + 1, 1), :] += jnp.sum(dh * n, axis=0, keepdims=True)
        dn = dh * (1.0 + scale)
        dnw_ref[...] += jnp.sum(dn * xh, axis=0, keepdims=True)
        dxh = dn * nw_ref[...]
        dx_ref[...] = dout_v + r * (dxh - xh * jnp.mean(dxh * xh, axis=-1, keepdims=True))

    row = lambda i: (i, 0)
    td = pl.BlockSpec((tm, d), row)
    tf = pl.BlockSpec((tm, f), row)
    return _hosted_call(
        body, name, (t // tm,),
        [td, td, tf, tf, td, _whole(mod.shape), _whole(nw.shape), _whole(w1.shape), _whole(w3.shape), _whole(w2.shape)],
        [td, td, tf, tf, tf, td, pl.BlockSpec(mod.shape, lambda i: (0, 0, 0)), pl.BlockSpec((1, d), lambda i: (0, 0))],
        [_sds((t, d), F32), _sds((t, d), BF16), _sds((t, f), BF16), _sds((t, f), BF16), _sds((t, f), BF16),
         _sds((t, d), BF16), _sds(mod.shape, F32), _sds((1, d), F32)],
        (dout, xs, a, b, y, mod, nw, w1, w3, w2), exch=exch)


def _mm_tn(a, b, rows, name, exch=None):
    m = a.shape[1]
    n = b.shape[1]
    bm = _div_tile(m, 1408, LANES)
    bn = _div_tile(n, 1408, LANES)
    bk = _div_tile(rows, 512, LANES)

    def body(a_ref, b_ref, o_ref):
        @pl.when(pl.program_id(2) == 0)
        def _():
            o_ref[...] = jnp.zeros_like(o_ref)

        o_ref[...] += _dot_tn(a_ref[...], b_ref[...])

    (out,), got = _hosted_call(
        body, name, (m // bm, n // bn, rows // bk),
        [pl.BlockSpec((bk, bm), lambda i, j, k: (k, i)), pl.BlockSpec((bk, bn), lambda i, j, k: (k, j))],
        [pl.BlockSpec((bm, bn), lambda i, j, k: (i, j))], [_sds((m, n), F32)], (a, b), exch=exch)
    return out if exch is None else (out, got)


def _mixin_fwd(xs, mod, nw, wp, s, nb, tm):
    t, d = xs.shape

    def body(x_ref, mod_ref, nw_ref, wp_ref, h_ref, p_ref):
        g = jnp.minimum((pl.program_id(0) * tm) // s, nb)
        shift = mod_ref[g, pl.ds(3, 1), :]
        scale = mod_ref[g, pl.ds(4, 1), :]
        x = x_ref[...]
        r = lax.rsqrt(jnp.mean(x * x, axis=-1, keepdims=True) + EPS)
        hb = ((x * r * nw_ref[...]) * (1.0 + scale) + shift).astype(BF16)
        h_ref[...] = hb
        p_ref[...] = _dot(hb, wp_ref[...])

    row = lambda i: (i, 0)
    return pl.pallas_call(
        body, name="mixin_fwd", grid=(t // tm,),
        in_specs=[pl.BlockSpec((tm, d), row), _whole(mod.shape), _whole(nw.shape), _whole(wp.shape)],
        out_specs=[pl.BlockSpec((tm, d), row), pl.BlockSpec((tm, PROJ_COLS), row)],
        out_shape=[_sds((t, d), BF16), _sds((t, PROJ_COLS), F32)], compiler_params=_params(1),
    )(xs, mod, nw, wp)


def _mixin_bwd(dp0, dpu, dpv, xs, dres, mod, nw, wp, s, nb, tm):
    t_all, d = xs.shape
    nlat = dres.shape[0] // tm

    def body(p0_ref, pu_ref, pv_ref, x_ref, dr_ref, mod_ref, nw_ref, wp_ref, dx_ref, dmod_ref, dnw_ref):
        i = pl.program_id(0)

        @pl.when(i == 0)
        def _():
            dmod_ref[...] = jnp.zeros_like(dmod_ref)
            dnw_ref[...] = jnp.zeros_like(dnw_ref)

        lat = i < nlat
        g = jnp.minimum((i * tm) // s, nb)
        scale = mod_ref[g, pl.ds(4, 1), :]
        dh = _dot_nt(p0_ref[...], wp_ref[:, 0:512])
        extra = _dot_nt(pu_ref[...], wp_ref[:, 512:1024]) + _dot_nt(pv_ref[...], wp_ref[:, 1024:1536])
        dh = dh + jnp.where(lat, extra, 0.0)
        x = x_ref[...]
        r = lax.rsqrt(jnp.mean(x * x, axis=-1, keepdims=True) + EPS)
        xh = x * r
        n = xh * nw_ref[...]
        dmod_ref[g, pl.ds(3, 1), :] += jnp.sum(dh, axis=0, keepdims=True)
        dmod_ref[g, pl.ds(4, 1), :] += jnp.sum(dh * n, axis=0, keepdims=True)
        dn = dh * (1.0 + scale)
        dnw_ref[...] += jnp.sum(dn * xh, axis=0, keepdims=True)
        dxh = dn * nw_ref[...]
        dx_ref[...] = jnp.where(lat, dr_ref[...], 0.0) + r * (dxh - xh * jnp.mean(dxh * xh, axis=-1, keepdims=True))

    row = lambda i: (i, 0)
    lrow = lambda i: (jnp.minimum(i, nlat - 1), 0)
    return pl.pallas_call(
        body, name="mixin_bwd", grid=(t_all // tm,),
        in_specs=[pl.BlockSpec((tm, 512), row), pl.BlockSpec((tm, 512), lrow), pl.BlockSpec((tm, 512), lrow),
                  pl.BlockSpec((tm, d), row), pl.BlockSpec((tm, d), lrow), _whole(mod.shape), _whole(nw.shape),
                  _whole(wp.shape)],
        out_specs=[pl.BlockSpec((tm, d), row), pl.BlockSpec(mod.shape, lambda i: (0, 0, 0)),
                   pl.BlockSpec((1, d), lambda i: (0, 0))],
        out_shape=[_sds((t_all, d), F32), _sds(mod.shape, F32), _sds((1, d), F32)], compiler_params=_params(1),
    )(dp0, dpu, dpv, xs, dres, mod, nw, wp)


def _prep_fwd(proj, row0, nb, s, pos0, sk, key0, into, tabs, wq, wk, wv, kvaw, qaw, qnw, knw, tm, with_q, name):
    nblk = s // tm
    n_into = 0 if into is None else 2

    def body(p_ref, cos_ref, sa_ref, sb_ref, wq_ref, wk_ref, wv_ref, kvaw_ref, qaw_ref, qnw_ref, knw_ref, *outs):
        outs = outs[n_into:]
        q_ref, k_ref, v_ref = outs if with_q else (None,) + outs
        cos, sin_a, sin_b = cos_ref[...], sa_ref[...], sb_ref[...]
        ckv = p_ref[:, 0:128]
        kpe = p_ref[:, 128:256]
        rkv = lax.rsqrt(jnp.mean(ckv * ckv, axis=-1, keepdims=True) + EPS)
        ckvb = (ckv * rkv * kvaw_ref[...]).astype(BF16)
        for h in range(HEADS):
            kp = _dot(ckvb, wk_ref[h]) + kpe
            rk = lax.rsqrt(jnp.sum(kp * kp, axis=-1, keepdims=True) * (1.0 / QK_HEAD) + EPS)
            k_ref[h] = _rope(kp * rk * knw_ref[...], cos, sin_a, sin_b).astype(BF16)
        for j in range(HEADS // 2):
            v_ref[j] = _dot(ckvb, wv_ref[j]).astype(BF16)
        if with_q:
            cq = p_ref[:, 256:512]
            rq = lax.rsqrt(jnp.mean(cq * cq, axis=-1, keepdims=True) + EPS)
            cqb = (cq * rq * qaw_ref[...]).astype(BF16)
            for h in range(HEADS):
                qp = _dot(cqb, wq_ref[h])
                rh = lax.rsqrt(jnp.sum(qp * qp, axis=-1, keepdims=True) * (1.0 / QK_HEAD) + EPS)
                q_ref[h] = _rope(qp * rh * qnw_ref[...], cos, sin_a, sin_b).astype(BF16)

    tab = pl.BlockSpec((tm, HEAD_PAD), lambda i: (pos0 + i % nblk, 0))
    qspec = pl.BlockSpec((None, HEADS, tm, HEAD_PAD), lambda i: (i // nblk, 0, i % nblk, 0))
    kspec = pl.BlockSpec((None, HEADS, tm, HEAD_PAD), lambda i: (i // nblk, 0, key0 + i % nblk, 0))
    vspec = pl.BlockSpec((None, HEADS // 2, tm, HEAD_PAD), lambda i: (i // nblk, 0, key0 + i % nblk, 0))
    qshape = _sds((nb, HEADS, s, HEAD_PAD), BF16)
    kshape = _sds((nb, HEADS, sk, HEAD_PAD), BF16)
    vshape = _sds((nb, HEADS // 2, sk, HEAD_PAD), BF16)
    n_q = 1 if with_q else 0
    return pl.pallas_call(
        body, name=name, grid=(nb * nblk,),
        in_specs=[pl.BlockSpec((tm, 512), lambda i: (row0 + i, 0)), tab, tab, tab, _whole(wq.shape), _whole(wk.shape),
                  _whole(wv.shape), _whole(kvaw.shape), _whole(qaw.shape), _whole(qnw.shape), _whole(knw.shape)]
        + [pl.BlockSpec(memory_space=pl.ANY)] * n_into,
        out_specs=([qspec] if with_q else []) + [kspec, vspec],
        out_shape=([qshape] if with_q else []) + [kshape, vshape],
        input_output_aliases={11: n_q, 12: n_q + 1} if n_into else {}, compiler_params=_params(1),
    )(proj, *tabs, wq, wk, wv, kvaw, qaw, qnw, knw, *(into or ()))


def _prep_bwd(proj, row0, nb, s, pos0, key0, dp_rows, dp_into, tabs, wq, wk, wv, kvaw, qaw, qnw, knw, dq, dk, dv, init, tm,
              name):
    nblk = s // tm
    with_q = dq is not None
    n_init = 0 if init is None else len(init)
    n_into = 0 if dp_into is None else 1

    def body(*refs):
        p_ref, cos_ref, sa_ref, sb_ref, wq_ref, wk_ref, wv_ref, kvaw_ref, qaw_ref, qnw_ref, knw_ref = refs[:11]
        rest = list(refs[11:])
        dq_ref = rest.pop(0) if with_q else None
        dk_ref, dv_ref = rest.pop(0), rest.pop(0)
        init_refs = [rest.pop(0) for _ in range(n_init)]
        if n_into:
            rest.pop(0)
        dp_ref = rest.pop(0)
        if with_q:
            dwq_ref, dqaw_ref, dqnw_ref = rest.pop(0), rest.pop(0), rest.pop(0)
        dwk_ref, dwv_ref, dkvaw_ref, dknw_ref = rest
        accs = [dwk_ref, dwv_ref, dkvaw_ref, dknw_ref]

        @pl.when(pl.program_id(0) == 0)
        def _():
            for k, acc in enumerate(accs):
                acc[...] = init_refs[k][...] if n_init else jnp.zeros_like(acc)
            if with_q:
                dwq_ref[...] = jnp.zeros_like(dwq_ref)
                dqaw_ref[...] = jnp.zeros_like(dqaw_ref)
                dqnw_ref[...] = jnp.zeros_like(dqnw_ref)

        cos, sin_a, sin_b = cos_ref[...], sa_ref[...], sb_ref[...]
        lane = lax.broadcasted_iota(jnp.int32, (tm, HEAD_PAD), 1)
        rope_lanes = (lane >= QK_NOPE) & (lane < QK_HEAD)
        ckv = p_ref[:, 0:128]
        kpe = p_ref[:, 128:256]
        rkv = lax.rsqrt(jnp.mean(ckv * ckv, axis=-1, keepdims=True) + EPS)
        ckvh = ckv * rkv
        ckvb = (ckvh * kvaw_ref[...]).astype(BF16)
        dckv = jnp.zeros((tm, KV_LORA), F32)
        dkpe = jnp.zeros((tm, HEAD_PAD), F32)
        dknw = jnp.zeros((1, HEAD_PAD), F32)
        for h in range(HEADS):
            kp = _dot(ckvb, wk_ref[h]) + kpe
            rk = lax.rsqrt(jnp.sum(kp * kp, axis=-1, keepdims=True) * (1.0 / QK_HEAD) + EPS)
            kh = kp * rk
            dkn = _rope_t(dk_ref[h], cos, sin_a, sin_b)
            dknw = dknw + jnp.sum(dkn * kh, axis=0, keepdims=True)
            dkh = dkn * knw_ref[...]
            dkp = rk * (dkh - kh * (jnp.sum(dkh * kh, axis=-1, keepdims=True) * (1.0 / QK_HEAD)))
            dkpe = dkpe + jnp.where(rope_lanes, dkp, 0.0)
            dkpb = dkp.astype(BF16)
            dckv = dckv + _dot_nt(dkpb, wk_ref[h])
            dwk_ref[h] += _dot_tn(ckvb, dkpb)
        dknw_ref[...] += dknw
        for j in range(HEADS // 2):
            dvb = dv_ref[j].astype(BF16)
            dckv = dckv + _dot_nt(dvb, wv_ref[j])
            dwv_ref[j] += _dot_tn(ckvb, dvb)
        dkvaw_ref[...] += jnp.sum(dckv * ckvh, axis=0, keepdims=True)
        dch = dckv * kvaw_ref[...]
        dp_ref[:, 0:128] = (rkv * (dch - ckvh * jnp.mean(dch * ckvh, axis=-1, keepdims=True))).astype(BF16)
        dp_ref[:, 128:256] = dkpe.astype(BF16)
        if with_q:
            cq = p_ref[:, 256:512]
            rq = lax.rsqrt(jnp.mean(cq * cq, axis=-1, keepdims=True) + EPS)
            cqh = cq * rq
            cqb = (cqh * qaw_ref[...]).astype(BF16)
            dcq = jnp.zeros((tm, Q_LORA), F32)
            dqnw = jnp.zeros((1, HEAD_PAD), F32)
            for h in range(HEADS):
                qp = _dot(cqb, wq_ref[h])
                rh = lax.rsqrt(jnp.sum(qp * qp, axis=-1, keepdims=True) * (1.0 / QK_HEAD) + EPS)
                qh = qp * rh
                dqn = _rope_t(dq_ref[h], cos, sin_a, sin_b)
                dqnw = dqnw + jnp.sum(dqn * qh, axis=0, keepdims=True)
                dqh = dqn * qnw_ref[...]
                dqp = (rh * (dqh - qh * (jnp.sum(dqh * qh, axis=-1, keepdims=True) * (1.0 / QK_HEAD)))).astype(BF16)
                dcq = dcq + _dot_nt(dqp, wq_ref[h])
                dwq_ref[h] += _dot_tn(cqb, dqp)
            dqnw_ref[...] += dqnw
            dqaw_ref[...] += jnp.sum(dcq * cqh, axis=0, keepdims=True)
            dqc = dcq * qaw_ref[...]
            dp_ref[:, 256:512] = (rq * (dqc - cqh * jnp.mean(dqc * cqh, axis=-1, keepdims=True))).astype(BF16)
        else:
            dp_ref[:, 256:512] = jnp.zeros((tm, Q_LORA), BF16)

    tab = pl.BlockSpec((tm, HEAD_PAD), lambda i: (pos0 + i % nblk, 0))
    qspec = pl.BlockSpec((None, HEADS, tm, HEAD_PAD), lambda i: (i // nblk, 0, i % nblk, 0))
    kspec = pl.BlockSpec((None, HEADS, tm, HEAD_PAD), lambda i: (i // nblk, 0, key0 + i % nblk, 0))
    vspec = pl.BlockSpec((None, HEADS // 2, tm, HEAD_PAD), lambda i: (i // nblk, 0, key0 + i % nblk, 0))

    def acc_spec(shape):
        nd = len(shape)
        return pl.BlockSpec(shape, lambda i: (0,) * nd)

    acc_shapes = [(HEADS, KV_LORA, HEAD_PAD), (HEADS // 2, KV_LORA, HEAD_PAD), (1, KV_LORA), (1, HEAD_PAD)]
    q_shapes = [(HEADS, Q_LORA, HEAD_PAD), (1, Q_LORA), (1, HEAD_PAD)] if with_q else []
    out_shapes = [(dp_rows, 512)] + q_shapes + acc_shapes
    n_before = 11 + (1 if with_q else 0) + 2 + n_init
    return pl.pallas_call(
        body, name=name, grid=(nb * nblk,),
        in_specs=[pl.BlockSpec((tm, 512), lambda i: (row0 + i, 0)), tab, tab, tab, _whole(wq.shape), _whole(wk.shape),
                  _whole(wv.shape), _whole(kvaw.shape), _whole(qaw.shape), _whole(qnw.shape), _whole(knw.shape)]
        + ([qspec] if with_q else []) + [kspec, vspec] + [_whole(a.shape) for a in (init or [])]
        + [pl.BlockSpec(memory_space=pl.ANY)] * n_into,
        out_specs=[pl.BlockSpec((tm, 512), lambda i: (row0 + i, 0))] + [acc_spec(sh) for sh in q_shapes + acc_shapes],
        out_shape=[_sds(out_shapes[0], BF16)] + [_sds(sh, F32) for sh in out_shapes[1:]],
        input_output_aliases={n_before: 0} if n_into else {}, compiler_params=_params(1),
    )(proj, *tabs, wq, wk, wv, kvaw, qaw, qnw, knw, *([dq] if with_q else []), dk, dv, *(init or []),
      *([dp_into] if n_into else []))


def _attn_fwd(q, k, v, tq, exch=None):
    nb, _, s, _ = q.shape
    sk = k.shape[2]
    nq = s // tq
    scale = QK_HEAD ** -0.5

    def body(q_ref, k_ref, v_ref, o_ref):
        lane = lax.broadcasted_iota(jnp.int32, (tq, HEAD_PAD), 1)
        vv = v_ref[...]
        outs = []
        for hh in range(2):
            sc = _dot_nt(q_ref[hh], k_ref[hh]) * scale
            p = jnp.exp(sc - jnp.max(sc, axis=-1, keepdims=True))
            l = jnp.sum(p, axis=-1, keepdims=True)
            outs.append(_dot(p.astype(BF16), vv) / l)
        o_ref[...] = jnp.where(lane < V_HEAD, outs[0], outs[1]).astype(BF16)

    (o,), got = _hosted_call(
        body, "attn_fwd", (nb, HEADS // 2, nq),
        [pl.BlockSpec((None, 2, tq, HEAD_PAD), lambda b, j, i: (b, j, i, 0)),
         pl.BlockSpec((None, 2, sk, HEAD_PAD), lambda b, j, i: (b, j, 0, 0)),
         pl.BlockSpec((None, None, sk, HEAD_PAD), lambda b, j, i: (b, j, 0, 0))],
        [pl.BlockSpec((tq, HEAD_PAD), lambda b, j, i: (b * nq + i, j))], [_sds((nb * s, MLA_W), BF16)], (q, k, v),
        exch=exch)
    return o, got


def _attn_bwd(q, k, v, do, tq, exch=None):
    nb, _, s, _ = q.shape
    sk = k.shape[2]
    nq = s // tq
    scale = QK_HEAD ** -0.5

    def body(q_ref, k_ref, v_ref, do_ref, dq_ref, dk_ref, dv_ref):
        @pl.when(pl.program_id(2) == 0)
        def _():
            dk_ref[...] = jnp.zeros_like(dk_ref)
            dv_ref[...] = jnp.zeros_like(dv_ref)

        lane = lax.broadcasted_iota(jnp.int32, (tq, HEAD_PAD), 1)
        vv = v_ref[...]
        dov = do_ref[...]
        for hh in range(2):
            mine = (lane < V_HEAD) if hh == 0 else (lane >= V_HEAD)
            doh = jnp.where(mine, dov, jnp.zeros_like(dov))
            sc = _dot_nt(q_ref[hh], k_ref[hh]) * scale
            p = jnp.exp(sc - jnp.max(sc, axis=-1, keepdims=True))
            p = p / jnp.sum(p, axis=-1, keepdims=True)
            dp = _dot_nt(doh, vv)
            ds = (p * (dp - jnp.sum(p * dp, axis=-1, keepdims=True)) * scale).astype(BF16)
            dq_ref[hh] = _dot(ds, k_ref[hh])
            dk_ref[hh] += _dot_tn(ds, q_ref[hh])
            dv_ref[...] += _dot_tn(p.astype(BF16), doh)

    return _hosted_call(
        body, "attn_bwd", (nb, HEADS // 2, nq),
        [pl.BlockSpec((None, 2, tq, HEAD_PAD), lambda b, j, i: (b, j, i, 0)),
         pl.BlockSpec((None, 2, sk, HEAD_PAD), lambda b, j, i: (b, j, 0, 0)),
         pl.BlockSpec((None, None, sk, HEAD_PAD), lambda b, j, i: (b, j, 0, 0)),
         pl.BlockSpec((tq, HEAD_PAD), lambda b, j, i: (b * nq + i, j))],
        [pl.BlockSpec((None, 2, tq, HEAD_PAD), lambda b, j, i: (b, j, i, 0)),
         pl.BlockSpec((None, 2, sk, HEAD_PAD), lambda b, j, i: (b, j, 0, 0)),
         pl.BlockSpec((None, None, sk, HEAD_PAD), lambda b, j, i: (b, j, 0, 0))],
        [_sds(q.shape, F32), _sds(k.shape, F32), _sds(v.shape, F32)], (q, k, v, do), exch=exch)


def _group_masks(rows):
    lane = lax.broadcasted_iota(jnp.int32, (rows, GMLP_W), 1)
    return [(lane >= g * GROUP_DIM) & (lane < (g + 1) * GROUP_DIM) for g in range(GROUPS)]


def _gmlp_fwd(proj, t, wcat, bias, vnw, ones, tm):
    def body(u_ref, v_ref, wcat_ref, bias_ref, vnw_ref, ones_ref, o_ref):
        masks = _group_masks(CHUNK)
        gv = _gelu(v_ref[...])
        rv = lax.rsqrt(_group_sum(gv * gv, ones_ref) * (1.0 / GROUP_DIM) + EPS)
        vnb = (gv * rv * vnw_ref[...]).astype(BF16)
        for c in range(tm // CHUNK):
            rows = slice(c * CHUNK, (c + 1) * CHUNK)
            vc = vnb[rows]
            stack = jnp.concatenate([jnp.where(m, vc, jnp.zeros_like(vc)) for m in masks], axis=0)
            sp = _dot(wcat_ref[...], stack) + bias_ref[...]
            o_ref[rows, :] = (_gelu(u_ref[rows, :]) * sp).astype(BF16)

    return pl.pallas_call(
        body, name="gmlp_fwd", grid=(t // tm,),
        in_specs=[pl.BlockSpec((tm, GMLP_W), lambda i: (i, 1)), pl.BlockSpec((tm, GMLP_W), lambda i: (i, 2)),
                  _whole(wcat.shape), _whole(bias.shape), _whole(vnw.shape), _whole(ones.shape)],
        out_specs=pl.BlockSpec((tm, GMLP_W), lambda i: (i, 0)),
        out_shape=_sds((t, GMLP_W), BF16), compiler_params=_params(1),
    )(proj, proj, wcat, bias, vnw, ones)


def _gmlp_bwd(proj, dsg, wcat, wcat_t, bias, vnw, ones, tm):
    t = dsg.shape[0]

    def body(u_ref, v_ref, dsg_ref, wcat_ref, wcatt_ref, bias_ref, vnw_ref, ones_ref,
             du_ref, dv_ref, dws_ref, dbs_ref, dvnw_ref):
        @pl.when(pl.program_id(0) == 0)
        def _():
            dws_ref[...] = jnp.zeros_like(dws_ref)
            dbs_ref[...] = jnp.zeros_like(dbs_ref)
            dvnw_ref[...] = jnp.zeros_like(dvnw_ref)

        masks = _group_masks(CHUNK)
        v = v_ref[...]
        gv = _gelu(v)
        rv = lax.rsqrt(_group_sum(gv * gv, ones_ref) * (1.0 / GROUP_DIM) + EPS)
        xh = gv * rv
        vnb = (xh * vnw_ref[...]).astype(BF16)
        dvn_parts = []
        for c in range(tm // CHUNK):
            rows = slice(c * CHUNK, (c + 1) * CHUNK)
            vc = vnb[rows]
            stack = jnp.concatenate([jnp.where(m, vc, jnp.zeros_like(vc)) for m in masks], axis=0)
            sp = _dot(wcat_ref[...], stack) + bias_ref[...]
            u = u_ref[rows, :]
            dsg_c = dsg_ref[rows, :]
            du_ref[rows, :] = (dsg_c * sp * _gelu_grad(u)).astype(BF16)
            ds = dsg_c * _gelu(u)
            dstack = jnp.concatenate([jnp.where(m, ds, 0.0) for m in masks], axis=0)
            dbs_ref[...] += jnp.broadcast_to(jnp.sum(dstack, axis=-1, keepdims=True), dbs_ref.shape)
            dstb = dstack.astype(BF16)
            dvn_parts.append(_dot(wcatt_ref[...], dstb))
            dws_ref[...] += _dot_nt(dstb, vc)
        dvn = jnp.concatenate(dvn_parts, axis=0) if len(dvn_parts) > 1 else dvn_parts[0]
        dvnw_ref[...] += jnp.sum(dvn * xh, axis=0, keepdims=True)
        dxh = dvn * vnw_ref[...]
        gm = _group_sum(dxh * xh, ones_ref) * (1.0 / GROUP_DIM)
        dv_ref[...] = (rv * (dxh - xh * gm) * _gelu_grad(v)).astype(BF16)

    row = pl.BlockSpec((tm, GMLP_W), lambda i: (i, 0))
    return pl.pallas_call(
        body, name="gmlp_bwd", grid=(t // tm,),
        in_specs=[pl.BlockSpec((tm, GMLP_W), lambda i: (i, 1)), pl.BlockSpec((tm, GMLP_W), lambda i: (i, 2)), row,
                  _whole(wcat.shape), _whole(wcat_t.shape), _whole(bias.shape), _whole(vnw.shape), _whole(ones.shape)],
        out_specs=[row, row, pl.BlockSpec((GROUPS * CHUNK, CHUNK), lambda i: (0, 0)),
                   pl.BlockSpec((GROUPS * CHUNK, CHUNK), lambda i: (0, 0)), pl.BlockSpec((1, GMLP_W), lambda i: (0, 0))],
        out_shape=[_sds((t, GMLP_W), BF16), _sds((t, GMLP_W), BF16), _sds((GROUPS * CHUNK, CHUNK), F32),
                   _sds((GROUPS * CHUNK, CHUNK), F32), _sds((1, GMLP_W), F32)],
        compiler_params=_params(1),
    )(proj, proj, dsg, wcat, wcat_t, bias, vnw, ones)


def _mixout_fwd(o, sg, xs, mod, wout, s, tm):
    t = o.shape[0]
    d = xs.shape[1]

    def body(o_ref, sg_ref, x_ref, mod_ref, w_ref, x2_ref, mix_ref):
        g = (pl.program_id(0) * tm) // s
        gate = mod_ref[g, pl.ds(5, 1), :]
        mix = _dot(o_ref[...], w_ref[0:MLA_W, :]) + _dot(sg_ref[...], w_ref[MLA_W:MLA_W + GMLP_W, :])
        x2_ref[...] = x_ref[...] + gate * mix
        mix_ref[...] = mix.astype(BF16)

    row = lambda i: (i, 0)
    return pl.pallas_call(
        body, name="mixout_fwd", grid=(t // tm,),
        in_specs=[pl.BlockSpec((tm, MLA_W), row), pl.BlockSpec((tm, GMLP_W), row), pl.BlockSpec((tm, d), row),
                  _whole(mod.shape), _whole(wout.shape)],
        out_specs=[pl.BlockSpec((tm, d), row), pl.BlockSpec((tm, d), row)],
        out_shape=[_sds((t, d), F32), _sds((t, d), BF16)], compiler_params=_params(1),
    )(o, sg, xs, mod, wout)


def _mixout_bwd(dx2, mix, mod, wout, s, tm):
    t, d = dx2.shape

    def body(dx_ref, mix_ref, mod_ref, w_ref, dmix_ref, do_ref, dsg_ref, dmod_ref):
        i = pl.program_id(0)

        @pl.when(i == 0)
        def _():
            dmod_ref[...] = jnp.zeros_like(dmod_ref)

        g = (i * tm) // s
        gate = mod_ref[g, pl.ds(5, 1), :]
        dx = dx_ref[...]
        dmod_ref[g, pl.ds(5, 1), :] += jnp.sum(dx * mix_ref[...].astype(F32), axis=0, keepdims=True)
        dmb = (gate * dx).astype(BF16)
        dmix_ref[...] = dmb
        do_ref[...] = _dot_nt(dmb, w_ref[0:MLA_W, :]).astype(BF16)
        dsg_ref[...] = _dot_nt(dmb, w_ref[MLA_W:MLA_W + GMLP_W, :])

    row = lambda i: (i, 0)
    return pl.pallas_call(
        body, name="mixout_bwd", grid=(t // tm,),
        in_specs=[pl.BlockSpec((tm, d), row), pl.BlockSpec((tm, d), row), _whole(mod.shape), _whole(wout.shape)],
        out_specs=[pl.BlockSpec((tm, d), row), pl.BlockSpec((tm, MLA_W), row), pl.BlockSpec((tm, GMLP_W), row),
                   pl.BlockSpec(mod.shape, lambda i: (0, 0, 0))],
        out_shape=[_sds((t, d), BF16), _sds((t, MLA_W), BF16), _sds((t, GMLP_W), F32), _sds(mod.shape, F32)],
        compiler_params=_params(1),
    )(dx2, mix, mod, wout)


def _loss_head(yv, target, tm):
    t, d = yv.shape
    nsteps = t // tm

    def body(y_ref, t_ref, dy_ref, loss_ref, acc_ref):
        i = pl.program_id(0)

        @pl.when(i == 0)
        def _():
            acc_ref[...] = jnp.zeros_like(acc_ref)

        e = y_ref[...] - t_ref[...]
        dy_ref[...] = e * (1.0 / d)
        acc_ref[...] += jnp.sum(e * e, axis=0, keepdims=True)

        @pl.when(i == nsteps - 1)
        def _():
            loss_ref[...] = (0.5 / d) * jnp.sum(acc_ref[...], axis=-1, keepdims=True)

    row = lambda i: (i, 0)
    return pl.pallas_call(
        body, name="loss_head", grid=(nsteps,),
        in_specs=[pl.BlockSpec((tm, d), row), pl.BlockSpec((tm, d), row)],
        out_specs=[pl.BlockSpec((tm, d), row), pl.BlockSpec((1, 1), lambda i: (0, 0))],
        out_shape=[_sds((t, d), F32), _sds((1, 1), F32)],
        scratch_shapes=[pltpu.VMEM((1, d), F32)], compiler_params=_params(1),
    )(yv, target)


def _gather_first(shards):
    n = len(shards)

    def body(*refs):
        srcs, outs = refs[:n], refs[n:2 * n]
        ici_send, ici_recv, d2d_send, d2d_recv, local_sems = refs[2 * n:]
        x, y, c = lax.axis_index("x"), lax.axis_index("y"), lax.axis_index("c")
        me = 2 * x + y
        chips = _other_chips(x, y)

        def half(w, which):
            hr = shards[w].shape[0] // 2
            return pl.ds(pl.multiple_of(which * hr, 16), hr)

        def over_ici(w, k, arriving):
            px, py = chips[k]
            slot = 2 * px + py if arriving else me
            return pltpu.make_async_remote_copy(
                src_ref=srcs[w].at[half(w, c)], dst_ref=outs[w].at[slot, half(w, c)], send_sem=ici_send.at[3 * w + k],
                recv_sem=ici_recv.at[3 * w + k], device_id=(px, py, c), device_id_type=pl.DeviceIdType.MESH)

        def to_sibling(w, k, arriving):
            px, py = chips[k]
            rows = half(w, 1 - c if arriving else c)
            return pltpu.make_async_remote_copy(
                src_ref=outs[w].at[2 * px + py, rows], dst_ref=outs[w].at[2 * px + py, rows],
                send_sem=d2d_send.at[3 * w + k], recv_sem=d2d_recv.at[3 * w + k], device_id=(x, y, 1 - c),
                device_id_type=pl.DeviceIdType.MESH)

        local = [pltpu.make_async_copy(srcs[w], outs[w].at[me], local_sems.at[w]) for w in range(n)]
        for cp in local:
            cp.start()
        pairs = [(w, k) for w in range(n) for k in range(3)]
        for w, k in pairs:
            over_ici(w, k, False).start()
        for w, k in pairs:
            over_ici(w, k, True).wait_recv()
            to_sibling(w, k, False).start()
        for w, k in pairs:
            to_sibling(w, k, True).wait_recv()
        for w, k in pairs:
            over_ici(w, k, False).wait_send()
            to_sibling(w, k, False).wait_send()
        for cp in local:
            cp.wait()

    any_spec = pl.BlockSpec(memory_space=pl.ANY)
    sems = pltpu.SemaphoreType.DMA((3 * n,))
    return pl.pallas_call(
        body, name="gather_first", in_specs=[any_spec] * n, out_specs=[any_spec] * n,
        out_shape=_exch_shapes("gather", shards),
        scratch_shapes=[sems, sems, sems, sems, pltpu.SemaphoreType.DMA((n,))],
    )(*shards)


def _scatter_chips(arrays):
    n = len(arrays)

    def body(*refs):
        srcs, outs, sems = refs[:n], refs[n:2 * n], refs[2 * n:]
        _exch_start("scatter", srcs, outs, sems)
        _exch_wait("scatter", srcs, outs, sems)

    any_spec = pl.BlockSpec(memory_space=pl.ANY)
    return pl.pallas_call(
        body, name="scatter_chips", in_specs=[any_spec] * n, out_specs=[any_spec] * n,
        out_shape=_exch_shapes("scatter", arrays), scratch_shapes=_exch_scratch(n),
    )(*arrays)


def _swap_cores(parts, name):
    n = len(parts)

    def body(*refs):
        srcs, outs, send_sems, recv_sems = refs[:n], refs[n:2 * n], refs[2 * n], refs[2 * n + 1]
        x, y, c = lax.axis_index("x"), lax.axis_index("y"), lax.axis_index("c")
        copies = [pltpu.make_async_remote_copy(
            src_ref=srcs[w], dst_ref=outs[w], send_sem=send_sems.at[w], recv_sem=recv_sems.at[w],
            device_id=(x, y, 1 - c), device_id_type=pl.DeviceIdType.MESH) for w in range(n)]
        for cp in copies:
            cp.start()
        for cp in copies:
            cp.wait()

    any_spec = pl.BlockSpec(memory_space=pl.ANY)
    return pl.pallas_call(
        body, name=name, in_specs=[any_spec] * n, out_specs=[any_spec] * n,
        out_shape=[_sds(p.shape, p.dtype) for p in parts],
        scratch_shapes=[pltpu.SemaphoreType.DMA((n,)), pltpu.SemaphoreType.DMA((n,))],
    )(*parts)


def _row_tile(r, c, mult):
    return _div_tile(r, max(mult, (1 << 16) // c), mult)


def _sum_slots(recv, name):
    _, r, c = recv.shape
    tr = _row_tile(r, c, 16)

    def body(r_ref, o_ref):
        f = lambda k: r_ref[k].astype(F32)
        o_ref[...] = ((f(0) + f(1)) + f(2)) + f(3)

    return pl.pallas_call(
        body, name=name, grid=(r // tr,),
        in_specs=[pl.BlockSpec((N_CHIPS, tr, c), lambda i: (0, i, 0))],
        out_specs=pl.BlockSpec((tr, c), lambda i: (i, 0)),
        out_shape=_sds((r, c), F32), compiler_params=_params(1),
    )(recv)


def _adamw(p0, p1, w, m, v, name):
    r, wd = w.shape
    tr = _row_tile(r, wd, 8)
    c1 = 1.0 / (1.0 - ADAM_B1 ** ADAM_STEP)
    c2 = 1.0 / (1.0 - ADAM_B2 ** ADAM_STEP)

    def body(p0_ref, p1_ref, w_ref, m_ref, v_ref, g_ref, d_ref, nm_ref, nv_ref):
        g = p0_ref[...] + p1_ref[...]
        nm = ADAM_B1 * m_ref[...] + (1.0 - ADAM_B1) * g
        nv = ADAM_B2 * v_ref[...] + (1.0 - ADAM_B2) * (g * g)
        g_ref[...] = g
        nm_ref[...] = nm
        nv_ref[...] = nv
        d_ref[...] = -ADAM_LR * ((nm * c1) / (jnp.sqrt(nv * c2) + ADAM_EPS) + ADAM_WD * w_ref[...])

    spec = pl.BlockSpec((tr, wd), lambda i: (i, 0))
    return pl.pallas_call(
        body, name=name, grid=(r // tr,),
        in_specs=[spec] * 5, out_specs=[spec] * 4, out_shape=[_sds((r, wd), F32)] * 4, compiler_params=_params(1),
    )(p0, p1, w, m, v)


def _rope_tables(s, ctx):
    pos = np.arange(s, dtype=np.float32)
    inv = (np.float32(ROPE_BASE) ** (-np.arange(0, QK_ROPE // 2, 2, dtype=np.float32) / np.float32(QK_ROPE // 2)))
    ang_r = np.floor(pos / GRID_W)[:, None] * inv
    ang_c = (pos - GRID_W * np.floor(pos / GRID_W))[:, None] * inv
    ang = np.concatenate([ang_r, ang_r, ang_c, ang_c], axis=-1).astype(np.float32)
    cos, sin = np.cos(ang), np.sin(ang)
    half_b = (np.arange(QK_ROPE) // 8) % 2 == 1
    sin_a = np.where(half_b, sin, 0.0)
    sin_b = np.where(half_b, 0.0, -sin)

    def place(tab, fill):
        full = np.full((s + ctx, HEAD_PAD), fill, np.float32)
        full[:s, QK_NOPE:QK_HEAD] = tab
        return jnp.asarray(full)

    return place(cos, 1.0), place(sin_a, 0.0), place(sin_b, 0.0)


def _pad_last(a, n):
    return jnp.pad(a, [(0, 0)] * (a.ndim - 1) + [(0, n - a.shape[-1])])


def _flat_rows(parts, rows, width):
    flat = jnp.concatenate([p.reshape(-1) for p in parts])
    return jnp.pad(flat, (0, rows * width - flat.shape[0])).reshape(rows, width)


def kernel(x, c, ctx, c_ctx, w_ada, b_ada, norm1_w, ffn1_w1, ffn1_w3, ffn1_w2, norm2_w, w_in, q_a_norm_w, w_uq, kv_a_norm_w, w_ukv, q_norm_w, k_norm_w, v_norm_w, w_s, b_s, w_out, norm3_w, ffn2_w1, ffn2_w3, ffn2_w2, loss_target, m_c_ctx, m_w_ada, m_b_ada, m_norm1_w, m_ffn1_w1, m_ffn1_w3, m_ffn1_w2, m_norm2_w, m_w_in, m_q_a_norm_w, m_w_uq, m_kv_a_norm_w, m_w_ukv, m_q_norm_w, m_k_norm_w, m_v_norm_w, m_w_s, m_b_s, m_w_out, m_norm3_w, m_ffn2_w1, m_ffn2_w3, m_ffn2_w2, v_c_ctx, v_w_ada, v_b_ada, v_norm1_w, v_ffn1_w1, v_ffn1_w3, v_ffn1_w2, v_norm2_w, v_w_in, v_q_a_norm_w, v_w_uq, v_kv_a_norm_w, v_w_ukv, v_q_norm_w, v_k_norm_w, v_v_norm_w, v_w_s, v_b_s, v_w_out, v_norm3_w, v_ffn2_w1, v_ffn2_w3, v_ffn2_w2):
    wts = dict(c_ctx=c_ctx, w_ada=w_ada, b_ada=b_ada, norm1_w=norm1_w, ffn1_w1=ffn1_w1, ffn1_w3=ffn1_w3, ffn1_w2=ffn1_w2,
               norm2_w=norm2_w, w_in=w_in, q_a_norm_w=q_a_norm_w, w_uq=w_uq, kv_a_norm_w=kv_a_norm_w, w_ukv=w_ukv,
               q_norm_w=q_norm_w, k_norm_w=k_norm_w, v_norm_w=v_norm_w, w_s=w_s, b_s=b_s, w_out=w_out, norm3_w=norm3_w,
               ffn2_w1=ffn2_w1, ffn2_w3=ffn2_w3, ffn2_w2=ffn2_w2)
    moms = dict(c_ctx=m_c_ctx, w_ada=m_w_ada, b_ada=m_b_ada, norm1_w=m_norm1_w, ffn1_w1=m_ffn1_w1, ffn1_w3=m_ffn1_w3,
                ffn1_w2=m_ffn1_w2, norm2_w=m_norm2_w, w_in=m_w_in, q_a_norm_w=m_q_a_norm_w, w_uq=m_w_uq,
                kv_a_norm_w=m_kv_a_norm_w, w_ukv=m_w_ukv, q_norm_w=m_q_norm_w, k_norm_w=m_k_norm_w, v_norm_w=m_v_norm_w,
                w_s=m_w_s, b_s=m_b_s, w_out=m_w_out, norm3_w=m_norm3_w, ffn2_w1=m_ffn2_w1, ffn2_w3=m_ffn2_w3,
                ffn2_w2=m_ffn2_w2)
    vars_ = dict(c_ctx=v_c_ctx, w_ada=v_w_ada, b_ada=v_b_ada, norm1_w=v_norm1_w, ffn1_w1=v_ffn1_w1, ffn1_w3=v_ffn1_w3,
                 ffn1_w2=v_ffn1_w2, norm2_w=v_norm2_w, w_in=v_w_in, q_a_norm_w=v_q_a_norm_w, w_uq=v_w_uq,
                 kv_a_norm_w=v_kv_a_norm_w, w_ukv=v_w_ukv, q_norm_w=v_q_norm_w, k_norm_w=v_k_norm_w, v_norm_w=v_v_norm_w,
                 w_s=v_w_s, b_s=v_b_s, w_out=v_w_out, norm3_w=v_norm3_w, ffn2_w1=v_ffn2_w1, ffn2_w3=v_ffn2_w3,
                 ffn2_w2=v_ffn2_w2)

    nb, s, d = x.shape
    nctx = ctx.shape[1]
    t, tc = nb * s, nb * nctx
    t_all = t + tc
    sk = s + nctx
    assert nb + 1 <= MOD_ROWS and d % LANES == 0
    tm = _token_tile(s, nctx)

    shard = {n: wts[n][0].astype(BF16) for n in SHARDED}
    full = {}

    def unshard(names, blocks):
        for n, g4 in zip(names, blocks):
            _, r_, c_ = g4.shape
            full[n] = g4.reshape(N_CHIPS * r_, c_) if n in ROW_SHARDED else g4.transpose(1, 0, 2).reshape(r_, N_CHIPS * c_)

    def chip_major(n, g_):
        if n in ROW_SHARDED:
            return g_.reshape(N_CHIPS, g_.shape[0] // N_CHIPS, g_.shape[1]).astype(BF16)
        r_, cols = g_.shape
        return g_.reshape(r_, N_CHIPS, cols // N_CHIPS).transpose(1, 0, 2).astype(BF16)

    unshard(FIRST_WEIGHTS, _gather_first([shard[n] for n in FIRST_WEIGHTS]))
    wsb = w_s[0].astype(BF16)
    wcat = wsb.transpose(1, 0, 2).reshape(CHUNK, GROUPS * CHUNK)
    wcat_t = wsb.transpose(2, 0, 1).reshape(CHUNK, GROUPS * CHUNK)
    bias = jnp.repeat(b_s[0].T, GROUP_DIM, axis=1)
    vnw = v_norm_w.reshape(1, GMLP_W)
    lane = jnp.arange(GMLP_W)
    ones = (lane[:, None] // GROUP_DIM == lane[None, :] // GROUP_DIM).astype(BF16)
    qnw = _pad_last(q_norm_w, HEAD_PAD)
    knw = _pad_last(k_norm_w, HEAD_PAD)
    tabs = _rope_tables(s, nctx)

    cc = jnp.concatenate([c, c_ctx[None, :], jnp.zeros((MOD_ROWS - nb - 1, d), F32)], axis=0)
    mod = _ada_fwd(cc, full["w_ada"], b_ada).reshape(MOD_ROWS, N_MOD, d)
    xs0 = jnp.concatenate([x.reshape(t, d), ctx.reshape(tc, d)], axis=0)
    (xs1, a1, b1, y1), got = _ffn_fwd(xs0, mod, norm1_w, full["ffn1_w1"], full["ffn1_w3"], full["ffn1_w2"], 0, s, nb, tm,
                                      "ffn1_fwd", exch=("gather", [shard[n] for n in MIX_WEIGHTS]))
    unshard(MIX_WEIGHTS, got)
    wi = full["w_in"]
    wp = jnp.concatenate([wi[:, 0:KV_LORA], jnp.zeros((d, QK_NOPE), BF16), wi[:, KV_LORA:KV_LORA + QK_ROPE],
                          jnp.zeros((d, HEAD_PAD - QK_HEAD), BF16), wi[:, KV_LORA + QK_ROPE:]], axis=1)
    wq = _pad_last(full["w_uq"].reshape(Q_LORA, HEADS, QK_HEAD).transpose(1, 0, 2), HEAD_PAD)
    wkv = full["w_ukv"].reshape(KV_LORA, HEADS, QK_NOPE + V_HEAD)
    wk = _pad_last(wkv[:, :, :QK_NOPE].transpose(1, 0, 2), HEAD_PAD)
    wv = wkv[:, :, QK_NOPE:].reshape(KV_LORA, HEADS // 2, 2 * V_HEAD).transpose(1, 0, 2)
    h2, proj = _mixin_fwd(xs1, mod, norm2_w, wp, s, nb, tm)
    prep_w = (wq, wk, wv, kv_a_norm_w, q_a_norm_w, qnw, knw)
    q, k_all, v_all = _prep_fwd(proj, 0, nb, s, 0, sk, 0, None, tabs, *prep_w, tm, True, "prep_fwd")
    k_all, v_all = _prep_fwd(proj, t // tm, nb, nctx, s // tm, sk, s // tm, (k_all, v_all), tabs, *prep_w, tm, False,
                             "prep_ctx_fwd")
    o, got = _attn_fwd(q, k_all, v_all, tm, exch=("gather", [shard[n] for n in LAST_WEIGHTS]))
    unshard(LAST_WEIGHTS, got)
    sg = _gmlp_fwd(proj, t, wcat, bias, vnw, ones, tm)
    x2, mix = _mixout_fwd(o, sg, xs1, mod, full["w_out"], s, tm)
    (yv, a2, b2, y2), _ = _ffn_fwd(x2, mod, norm3_w, full["ffn2_w1"], full["ffn2_w3"], full["ffn2_w2"], 6, s, nb, tm,
                                   "ffn2_fwd")
    dy, loss_part = _loss_head(yv, loss_target.reshape(t, d), tm)
    loss = lax.psum(loss_part[0, 0], ("x", "y", "c"))

    grads, cm, recv = {}, {}, {}

    def scatter_of(names):
        return ("scatter", [cm[n] for n in names])

    (dx2, h3, g2, da2, db2, dyb2, dmod_c, grads["norm3_w"]), _ = _ffn_bwd(
        dy, x2, a2, b2, y2, mod, norm3_w, full["ffn2_w1"], full["ffn2_w3"], full["ffn2_w2"], 6, s, nb, tm, "ffn2_bwd")
    cm["ffn2_w1"] = chip_major("ffn2_w1", _mm_tn(h3, da2, t, "ffn2_dw1"))
    cm["ffn2_w3"] = chip_major("ffn2_w3", _mm_tn(h3, db2, t, "ffn2_dw3"))
    cm["ffn2_w2"] = chip_major("ffn2_w2", _mm_tn(g2, dyb2, t, "ffn2_dw2"))
    dmix, do, dsg, dmod_b = _mixout_bwd(dx2, mix, mod, full["w_out"], s, tm)
    cm["w_out"] = chip_major("w_out", jnp.concatenate([_mm_tn(o, dmix, t, "wout_dw_attn"),
                                                       _mm_tn(sg, dmix, t, "wout_dw_gmlp")], axis=0))
    dpu, dpv, dws, dbs, dvnw = _gmlp_bwd(proj, dsg, wcat, wcat_t, bias, vnw, ones, tm)
    group = LAST_WEIGHTS + ("w_out",)
    (dq, dk, dv), got = _attn_bwd(q, k_all, v_all, do, tm, exch=scatter_of(group))
    recv.update(zip(group, got))
    dp0, dwk_c, dwv_c, dkvaw_c, dknw_c = _prep_bwd(
        proj, t // tm, nb, nctx, s // tm, s // tm, t_all, None, tabs, *prep_w, None, dk, dv, None, tm, "prep_ctx_bwd")
    dp0, dwq, dqaw, dqnw, dwk, dwv, dkvaw, dknw = _prep_bwd(
        proj, 0, nb, s, 0, 0, t_all, dp0, tabs, *prep_w, dq, dk, dv, [dwk_c, dwv_c, dkvaw_c, dknw_c], tm, "prep_bwd")
    dxs1, dmod_a, grads["norm2_w"] = _mixin_bwd(dp0, dpu, dpv, xs1, dx2, mod, norm2_w, wp, s, nb, tm)
    dwp = jnp.concatenate([_mm_tn(h2, dp0, t_all, "win_dw_kvq"), _mm_tn(h2, dpu, t, "win_dw_u"),
                           _mm_tn(h2, dpv, t, "win_dw_v")], axis=1)
    cm["w_in"] = chip_major("w_in", jnp.concatenate(
        [dwp[:, 0:KV_LORA], dwp[:, KV_LORA + QK_NOPE:KV_LORA + QK_HEAD], dwp[:, 256:]], axis=1))
    cm["w_uq"] = chip_major("w_uq", dwq[:, :, :QK_HEAD].transpose(1, 0, 2).reshape(Q_LORA, HEADS * QK_HEAD))
    cm["w_ukv"] = chip_major("w_ukv", jnp.concatenate(
        [dwk[:, :, :QK_NOPE].transpose(1, 0, 2),
         dwv.transpose(1, 0, 2).reshape(KV_LORA, HEADS, V_HEAD)], axis=2).reshape(KV_LORA, HEADS * (QK_NOPE + V_HEAD)))
    group = ("w_in", "w_uq", "w_ukv")
    (dxs0, h1, g1, da1, db1, dyb1, dmod_0, grads["norm1_w"]), got = _ffn_bwd(
        dxs1, xs0, a1, b1, y1, mod, norm1_w, full["ffn1_w1"], full["ffn1_w3"], full["ffn1_w2"], 0, s, nb, tm, "ffn1_bwd",
        exch=scatter_of(group))
    recv.update(zip(group, got))
    cm["ffn1_w2"] = chip_major("ffn1_w2", _mm_tn(g1, dyb1, t_all, "ffn1_dw2"))
    dw1, got = _mm_tn(h1, da1, t_all, "ffn1_dw1", exch=scatter_of(("ffn1_w2",)))
    recv["ffn1_w2"] = got[0]
    cm["ffn1_w1"] = chip_major("ffn1_w1", dw1)
    dw3, got = _mm_tn(h1, db1, t_all, "ffn1_dw3", exch=scatter_of(("ffn1_w1",)))
    recv["ffn1_w1"] = got[0]
    cm["ffn1_w3"] = chip_major("ffn1_w3", dw3)
    dmods = [m_.reshape(MOD_ROWS, N_MOD * d) for m_ in (dmod_0, dmod_a, dmod_b, dmod_c)]
    (dw_ada, grads["b_ada"], dcc), got = _ada_bwd(cc, dmods, full["w_ada"], exch=scatter_of(("ffn1_w3",)))
    recv["ffn1_w3"] = got[0]
    cm["w_ada"] = chip_major("w_ada", dw_ada)
    grads["c_ctx"] = dcc[nb]
    grads["q_a_norm_w"], grads["kv_a_norm_w"] = dqaw, dkvaw
    grads["q_norm_w"], grads["k_norm_w"] = dqnw[:, :QK_HEAD], dknw[:, :QK_HEAD]
    grads["v_norm_w"], grads["w_s"], grads["b_s"] = dvnw, dws, dbs[:, 0]
    grad_x = dxs0[:t].reshape(nb, s, d)
    rows_s = _round_up(-(-sum(wts[n].size for n in SMALL) // d), 16)
    small = _flat_rows([grads[n] for n in SMALL], rows_s, d)
    recv["w_ada"], recv["small"] = _scatter_chips([cm["w_ada"], jnp.broadcast_to(small, (N_CHIPS, rows_s, d))])

    part = {n: _sum_slots(recv[n], "sum_" + n) for n in SHARDED + ("small",)}
    early = LAST_WEIGHTS + ("w_out",)
    late = tuple(n for n in SHARDED + ("small",) if n not in early)
    sib = dict(zip(early, _swap_cores([part[n] for n in early], "swap_early")))
    sib.update(zip(late, _swap_cores([part[n] for n in late], "swap_late")))
    stepped = {}
    for n in SHARDED:
        stepped[n] = [a_.reshape(wts[n].shape) for a_ in
                      _adamw(part[n], sib[n], wts[n][0], moms[n][0], vars_[n][0], "adamw_" + n)]
    packed = _adamw(part["small"], sib["small"], _flat_rows([wts[n] for n in SMALL], rows_s, d),
                    _flat_rows([moms[n] for n in SMALL], rows_s, d), _flat_rows([vars_[n] for n in SMALL], rows_s, d),
                    "adamw_small")
    for n in SMALL:
        stepped[n] = []
    for a_ in packed:
        flat = a_.reshape(-1)
        off = 0
        for n in SMALL:
            stepped[n].append(flat[off:off + wts[n].size].reshape(wts[n].shape))
            off += wts[n].size
    return (loss, grad_x, *[stepped[n][0] for n in WEIGHTS], *[stepped[n][1] for n in WEIGHTS],
            *[stepped[n][2] for n in WEIGHTS], *[stepped[n][3] for n in WEIGHTS])
```

```python
import functools
import math

import jax
import jax.numpy as jnp
import numpy as np
from jax import lax
from jax.experimental import pallas as pl
from jax.experimental.pallas import tpu as pltpu

F32 = jnp.float32
BF16 = jnp.bfloat16

EPS = 1e-6
N_MOD = 9
HEADS = 8
QK_NOPE, QK_ROPE, V_HEAD = 64, 32, 64
QK_HEAD = QK_NOPE + QK_ROPE
HEAD_PAD = 128
Q_LORA, KV_LORA = 256, 128
GROUPS, GROUP_DIM, CHUNK = 8, 64, 128
GMLP_W = GROUPS * GROUP_DIM
MLA_W = HEADS * V_HEAD
IN_COLS = 1440
PROJ_COLS = 1536
GRID_W = 64
ROPE_BASE = 10000.0
MOD_ROWS = 16
ADAM_LR, ADAM_B1, ADAM_B2, ADAM_EPS, ADAM_WD, ADAM_STEP = 0.001, 0.9, 0.999, 1e-08, 0.01, 10
N_CHIPS = 4
LANES = 128
V7X_VMEM_LIMIT = 56 * 1024 * 1024
GELU_C = math.sqrt(2.0 / math.pi)

SHARDED = ("w_ada", "ffn1_w1", "ffn1_w3", "ffn1_w2", "w_in", "w_uq", "w_ukv", "w_out", "ffn2_w1", "ffn2_w3", "ffn2_w2")
ROW_SHARDED = ("ffn1_w2", "w_out", "ffn2_w2")
FIRST_WEIGHTS = ("ffn1_w1", "ffn1_w3", "ffn1_w2")
MIX_WEIGHTS = ("w_in", "w_uq", "w_ukv", "w_out")
LAST_WEIGHTS = ("ffn2_w1", "ffn2_w3", "ffn2_w2")
SMALL = ("c_ctx", "b_ada", "norm1_w", "norm2_w", "q_a_norm_w", "kv_a_norm_w", "q_norm_w", "k_norm_w", "v_norm_w",
         "w_s", "b_s", "norm3_w")
WEIGHTS = ("c_ctx", "w_ada", "b_ada", "norm1_w", "ffn1_w1", "ffn1_w3", "ffn1_w2", "norm2_w", "w_in", "q_a_norm_w",
           "w_uq", "kv_a_norm_w", "w_ukv", "q_norm_w", "k_norm_w", "v_norm_w", "w_s", "b_s", "w_out", "norm3_w",
           "ffn2_w1", "ffn2_w3", "ffn2_w2")


def _round_up(n, m):
    return (n + m - 1) // m * m


def _div_tile(n, target, mult):
    best = None
    for t in range(mult, min(n, target) + 1, mult):
        if n % t == 0:
            best = t
    return n if best is None else best


def _dot(a, b):
    return lax.dot_general(a, b, (((1,), (0,)), ((), ())), preferred_element_type=F32)


def _dot_nt(a, b):
    return lax.dot_general(a, b, (((1,), (1,)), ((), ())), preferred_element_type=F32)


def _dot_tn(a, b):
    return lax.dot_general(a, b, (((0,), (0,)), ((), ())), preferred_element_type=F32)


def _sigmoid(x):
    return 1.0 / (1.0 + jnp.exp(-x))


def _gelu(x):
    return 0.5 * x * (1.0 + jnp.tanh(GELU_C * (x + 0.044715 * x * x * x)))


def _gelu_grad(x):
    t = jnp.tanh(GELU_C * (x + 0.044715 * x * x * x))
    return 0.5 * (1.0 + t) + 0.5 * x * (1.0 - t * t) * (GELU_C * (1.0 + 3 * 0.044715 * x * x))


def _rope(x, cos, sin_a, sin_b):
    return x * cos + pltpu.roll(x, 8, 1) * sin_a + pltpu.roll(x, HEAD_PAD - 8, 1) * sin_b


def _rope_t(d, cos, sin_a, sin_b):
    return d * cos + pltpu.roll(d * sin_a, HEAD_PAD - 8, 1) + pltpu.roll(d * sin_b, 8, 1)


def _group_sum(x, ones_ref):
    hi = x.astype(BF16)
    lo = (x - hi.astype(F32)).astype(BF16)
    return _dot(hi, ones_ref[...]) + _dot(lo, ones_ref[...])


def _params(n_axes):
    return pltpu.CompilerParams(dimension_semantics=("arbitrary",) * n_axes, vmem_limit_bytes=V7X_VMEM_LIMIT)


def _whole(shape):
    nd = len(shape)
    return pl.BlockSpec(shape, lambda *_: (0,) * nd, pipeline_mode=pl.Buffered(1))


def _sds(shape, dtype):
    return jax.ShapeDtypeStruct(shape, dtype)


def _token_tile(s, ctx):
    return _div_tile(math.gcd(s, ctx), 256, CHUNK)


def _other_chips(x, y):
    return [(1 - x, y), (x, 1 - y), (1 - x, 1 - y)]


def _exch_copies(kind, srcs, dsts, send_sems, recv_sems, local_sems):
    x, y, c = lax.axis_index("x"), lax.axis_index("y"), lax.axis_index("c")
    me = 2 * x + y
    local, sends, arrivals = [], [], []
    for w, (src, dst) in enumerate(zip(srcs, dsts)):
        own = src if kind == "gather" else src.at[me]
        local.append(pltpu.make_async_copy(own, dst.at[me], local_sems.at[w]))
        for k, (px, py) in enumerate(_other_chips(x, y)):
            sem = dict(send_sem=send_sems.at[3 * w + k], recv_sem=recv_sems.at[3 * w + k], device_id=(px, py, c),
                       device_id_type=pl.DeviceIdType.MESH)
            out = src if kind == "gather" else src.at[2 * px + py]
            sends.append(pltpu.make_async_remote_copy(src_ref=out, dst_ref=dst.at[me], **sem))
            arrivals.append(pltpu.make_async_remote_copy(src_ref=own, dst_ref=dst.at[2 * px + py], **sem))
    return local, sends, arrivals


def _exch_start(kind, srcs, dsts, sems):
    local, sends, _ = _exch_copies(kind, srcs, dsts, *sems)
    for cp in local + sends:
        cp.start()


def _exch_wait(kind, srcs, dsts, sems):
    local, sends, arrivals = _exch_copies(kind, srcs, dsts, *sems)
    for cp in arrivals:
        cp.wait_recv()
    for cp in sends:
        cp.wait_send()
    for cp in local:
        cp.wait()


def _exch_scratch(n):
    return [pltpu.SemaphoreType.DMA((3 * n,)), pltpu.SemaphoreType.DMA((3 * n,)), pltpu.SemaphoreType.DMA((n,))]


def _exch_shapes(kind, arrays):
    return [_sds((N_CHIPS,) + a.shape if kind == "gather" else a.shape, a.dtype) for a in arrays]


def _hosted_call(body, name, grid, in_specs, out_specs, out_shape, operands, scratch=(), exch=None):
    n_axes = len(grid)
    if exch is None:
        outs = pl.pallas_call(body, name=name, grid=grid, in_specs=list(in_specs), out_specs=list(out_specs),
                              out_shape=list(out_shape), scratch_shapes=list(scratch),
                              compiler_params=_params(n_axes))(*operands)
        return list(outs), []
    kind, arrays = exch
    n_in, n_out, n_sc, n_ex = len(in_specs), len(out_specs), len(scratch), len(arrays)

    def hosted(*refs):
        cin, ein = refs[:n_in], refs[n_in:n_in + n_ex]
        o0 = n_in + n_ex
        cout, eout = refs[o0:o0 + n_out], refs[o0 + n_out:o0 + n_out + n_ex]
        rest = refs[o0 + n_out + n_ex:]
        csc, sems = rest[:n_sc], rest[n_sc:]
        first = functools.reduce(jnp.logical_and, [pl.program_id(a) == 0 for a in range(n_axes)])
        last = functools.reduce(jnp.logical_and, [pl.program_id(a) == grid[a] - 1 for a in range(n_axes)])

        @pl.when(first)
        def _():
            _exch_start(kind, ein, eout, sems)

        body(*cin, *cout, *csc)

        @pl.when(last)
        def _():
            _exch_wait(kind, ein, eout, sems)

    any_spec = pl.BlockSpec(memory_space=pl.ANY)
    outs = pl.pallas_call(
        hosted, name=name, grid=grid, in_specs=list(in_specs) + [any_spec] * n_ex,
        out_specs=list(out_specs) + [any_spec] * n_ex, out_shape=list(out_shape) + _exch_shapes(kind, arrays),
        scratch_shapes=list(scratch) + _exch_scratch(n_ex), compiler_params=_params(n_axes),
    )(*operands, *arrays)
    return list(outs[:n_out]), list(outs[n_out:])


def _ffn_fwd(xs, mod, nw, w1, w3, w2, k0, s, nb, tm, name, exch=None):
    t, d = xs.shape
    f = w1.shape[1]

    def body(x_ref, mod_ref, nw_ref, w1_ref, w3_ref, w2_ref, o_ref, a_ref, b_ref, y_ref):
        g = jnp.minimum((pl.program_id(0) * tm) // s, nb)
        shift = mod_ref[g, pl.ds(k0, 1), :]
        scale = mod_ref[g, pl.ds(k0 + 1, 1), :]
        gate = mod_ref[g, pl.ds(k0 + 2, 1), :]
        x = x_ref[...]
        r = lax.rsqrt(jnp.mean(x * x, axis=-1, keepdims=True) + EPS)
        hb = ((x * r * nw_ref[...]) * (1.0 + scale) + shift).astype(BF16)
        a = _dot(hb, w1_ref[...])
        b = _dot(hb, w3_ref[...])
        gb = (a * _sigmoid(a) * b).astype(BF16)
        y = _dot(gb, w2_ref[...])
        o_ref[...] = x + (0.5 * gate) * y
        a_ref[...] = a.astype(BF16)
        b_ref[...] = b.astype(BF16)
        y_ref[...] = y.astype(BF16)

    row = lambda i: (i, 0)
    return _hosted_call(
        body, name, (t // tm,),
        [pl.BlockSpec((tm, d), row), _whole(mod.shape), _whole(nw.shape), _whole(w1.shape), _whole(w3.shape),
         _whole(w2.shape)],
        [pl.BlockSpec((tm, d), row), pl.BlockSpec((tm, f), row), pl.BlockSpec((tm, f), row), pl.BlockSpec((tm, d), row)],
        [_sds((t, d), F32), _sds((t, f), BF16), _sds((t, f), BF16), _sds((t, d), BF16)],
        (xs, mod, nw, w1, w3, w2), exch=exch)


def _ffn_bwd(dout, xs, a, b, y, mod, nw, w1, w3, w2, k0, s, nb, tm, name, exch=None):
    t, d = xs.shape
    f = w1.shape[1]
    nch = 2 if (f // 2) % LANES == 0 and f % 2 == 0 else 1
    fc = f // nch

    def body(do_ref, x_ref, a_ref, b_ref, y_ref, mod_ref, nw_ref, w1_ref, w3_ref, w2_ref,
             dx_ref, h_ref, g_ref, da_ref, db_ref, dy_ref, dmod_ref, dnw_ref):
        i = pl.program_id(0)

        @pl.when(i == 0)
        def _():
            dmod_ref[...] = jnp.zeros_like(dmod_ref)
            dnw_ref[...] = jnp.zeros_like(dnw_ref)

        g = jnp.minimum((i * tm) // s, nb)
        shift = mod_ref[g, pl.ds(k0, 1), :]
        scale = mod_ref[g, pl.ds(k0 + 1, 1), :]
        gate = mod_ref[g, pl.ds(k0 + 2, 1), :]
        x = x_ref[...]
        dout_v = do_ref[...]
        r = lax.rsqrt(jnp.mean(x * x, axis=-1, keepdims=True) + EPS)
        xh = x * r
        n = xh * nw_ref[...]
        h_ref[...] = (n * (1.0 + scale) + shift).astype(BF16)
        dyb = ((0.5 * gate) * dout_v).astype(BF16)
        dy_ref[...] = dyb
        dmod_ref[g, pl.ds(k0 + 2, 1), :] += 0.5 * jnp.sum(dout_v * y_ref[...].astype(F32), axis=0, keepdims=True)
        dh = jnp.zeros((tm, d), F32)
        for c in range(nch):
            sl = slice(c * fc, (c + 1) * fc)
            dg = _dot_nt(dyb, w2_ref[sl, :])
            av = a_ref[:, sl].astype(F32)
            bv = b_ref[:, sl].astype(F32)
            sig = _sigmoid(av)
            sa = av * sig
            g_ref[:, sl] = (sa * bv).astype(BF16)
            dab = (dg * bv * (sig * (1.0 + av * (1.0 - sig)))).astype(BF16)
            dbb = (dg * sa).astype(BF16)
            da_ref[:, sl] = dab
            db_ref[:, sl] = dbb
            dh = dh + _dot_nt(dab, w1_ref[:, sl]) + _dot_nt(dbb, w3_ref[:, sl])
        dmod_ref[g, pl.ds(k0, 1), :] += jnp.sum(dh, axis=0, keepdims=True)
        dmod_ref[g, pl.ds(k0 + 1, 1), :] += jnp.sum(dh * n, axis=0, keepdims=True)
        dn = dh * (1.0 + scale)
        dnw_ref[...] += jnp.sum(dn * xh, axis=0, keepdims=True)
        dxh = dn * nw_ref[...]
        dx_ref[...] = dout_v + r * (dxh - xh * jnp.mean(dxh * xh, axis=-1, keepdims=True))

    row = lambda i: (i, 0)
    td = pl.BlockSpec((tm, d), row)
    tf = pl.BlockSpec((tm, f), row)
    return _hosted_call(
        body, name, (t // tm,),
        [td, td, tf, tf, td, _whole(mod.shape), _whole(nw.shape), _whole(w1.shape), _whole(w3.shape), _whole(w2.shape)],
        [td, td, tf, tf, tf, td, pl.BlockSpec(mod.shape, lambda i: (0, 0, 0)), pl.BlockSpec((1, d), lambda i: (0, 0))],
        [_sds((t, d), F32), _sds((t, d), BF16), _sds((t, f), BF16), _sds((t, f), BF16), _sds((t, f), BF16),
         _sds((t, d), BF16), _sds(mod.shape, F32), _sds((1, d), F32)],
        (dout, xs, a, b, y, mod, nw, w1, w3, w2), exch=exch)


def _mm_tn(a, b, rows, name, exch=None):
    m = a.shape[1]
    n = b.shape[1]
    bm = _div_tile(m, 1408, LANES)
    bn = _div_tile(n, 1408, LANES)
    bk = _div_tile(rows, 512, LANES)

    def body(a_ref, b_ref, o_ref):
        @pl.when(pl.program_id(2) == 0)
        def _():
            o_ref[...] = jnp.zeros_like(o_ref)

        o_ref[...] += _dot_tn(a_ref[...], b_ref[...])

    (out,), got = _hosted_call(
        body, name, (m // bm, n // bn, rows // bk),
        [pl.BlockSpec((bk, bm), lambda i, j, k: (k, i)), pl.BlockSpec((bk, bn), lambda i, j, k: (k, j))],
        [pl.BlockSpec((bm, bn), lambda i, j, k: (i, j))], [_sds((m, n), F32)], (a, b), exch=exch)
    return out if exch is None else (out, got)


def _mixin_fwd(xs, mod, nw, wp, s, nb, tm):
    t, d = xs.shape

    def body(x_ref, mod_ref, nw_ref, wp_ref, h_ref, p_ref):
        g = jnp.minimum((pl.program_id(0) * tm) // s, nb)
        shift = mod_ref[g, pl.ds(3, 1), :]
        scale = mod_ref[g, pl.ds(4, 1), :]
        x = x_ref[...]
        r = lax.rsqrt(jnp.mean(x * x, axis=-1, keepdims=True) + EPS)
        hb = ((x * r * nw_ref[...]) * (1.0 + scale) + shift).astype(BF16)
        h_ref[...] = hb
        p_ref[...] = _dot(hb, wp_ref[...])

    row = lambda i: (i, 0)
    return pl.pallas_call(
        body, name="mixin_fwd", grid=(t // tm,),
        in_specs=[pl.BlockSpec((tm, d), row), _whole(mod.shape), _whole(nw.shape), _whole(wp.shape)],
        out_specs=[pl.BlockSpec((tm, d), row), pl.BlockSpec((tm, PROJ_COLS), row)],
        out_shape=[_sds((t, d), BF16), _sds((t, PROJ_COLS), F32)], compiler_params=_params(1),
    )(xs, mod, nw, wp)


def _mixin_bwd(dp0, dpu, dpv, xs, dres, mod, nw, wp, s, nb, tm):
    t_all, d = xs.shape
    nlat = dres.shape[0] // tm

    def body(p0_ref, pu_ref, pv_ref, x_ref, dr_ref, mod_ref, nw_ref, wp_ref, dx_ref, dmod_ref, dnw_ref):
        i = pl.program_id(0)

        @pl.when(i == 0)
        def _():
            dmod_ref[...] = jnp.zeros_like(dmod_ref)
            dnw_ref[...] = jnp.zeros_like(dnw_ref)

        lat = i < nlat
        g = jnp.minimum((i * tm) // s, nb)
        scale = mod_ref[g, pl.ds(4, 1), :]
        dh = _dot_nt(p0_ref[...], wp_ref[:, 0:512])
        extra = _dot_nt(pu_ref[...], wp_ref[:, 512:1024]) + _dot_nt(pv_ref[...], wp_ref[:, 1024:1536])
        dh = dh + jnp.where(lat, extra, 0.0)
        x = x_ref[...]
        r = lax.rsqrt(jnp.mean(x * x, axis=-1, keepdims=True) + EPS)
        xh = x * r
        n = xh * nw_ref[...]
        dmod_ref[g, pl.ds(3, 1), :] += jnp.sum(dh, axis=0, keepdims=True)
        dmod_ref[g, pl.ds(4, 1), :] += jnp.sum(dh * n, axis=0, keepdims=True)
        dn = dh * (1.0 + scale)
        dnw_ref[...] += jnp.sum(dn * xh, axis=0, keepdims=True)
        dxh = dn * nw_ref[...]
        dx_ref[...] = jnp.where(lat, dr_ref[...], 0.0) + r * (dxh - xh * jnp.mean(dxh * xh, axis=-1, keepdims=True))

    row = lambda i: (i, 0)
    lrow = lambda i: (jnp.minimum(i, nlat - 1), 0)
    return pl.pallas_call(
        body, name="mixin_bwd", grid=(t_all // tm,),
        in_specs=[pl.BlockSpec((tm, 512), row), pl.BlockSpec((tm, 512), lrow), pl.BlockSpec((tm, 512), lrow),
                  pl.BlockSpec((tm, d), row), pl.BlockSpec((tm, d), lrow), _whole(mod.shape), _whole(nw.shape),
                  _whole(wp.shape)],
        out_specs=[pl.BlockSpec((tm, d), row), pl.BlockSpec(mod.shape, lambda i: (0, 0, 0)),
                   pl.BlockSpec((1, d), lambda i: (0, 0))],
        out_shape=[_sds((t_all, d), F32), _sds(mod.shape, F32), _sds((1, d), F32)], compiler_params=_params(1),
    )(dp0, dpu, dpv, xs, dres, mod, nw, wp)


def _prep_fwd(proj, row0, nb, s, pos0, sk, key0, into, tabs, wq, wk, wv, kvaw, qaw, qnw, knw, tm, with_q, name):
    nblk = s // tm
    n_into = 0 if into is None else 2

    def body(p_ref, cos_ref, sa_ref, sb_ref, wq_ref, wk_ref, wv_ref, kvaw_ref, qaw_ref, qnw_ref, knw_ref, *outs):
        outs = outs[n_into:]
        q_ref, k_ref, v_ref = outs if with_q else (None,) + outs
        cos, sin_a, sin_b = cos_ref[...], sa_ref[...], sb_ref[...]
        ckv = p_ref[:, 0:128]
        kpe = p_ref[:, 128:256]
        rkv = lax.rsqrt(jnp.mean(ckv * ckv, axis=-1, keepdims=True) + EPS)
        ckvb = (ckv * rkv * kvaw_ref[...]).astype(BF16)
        for h in range(HEADS):
            kp = _dot(ckvb, wk_ref[h]) + kpe
            rk = lax.rsqrt(jnp.sum(kp * kp, axis=-1, keepdims=True) * (1.0 / QK_HEAD) + EPS)
            k_ref[h] = _rope(kp * rk * knw_ref[...], cos, sin_a, sin_b).astype(BF16)
        for j in range(HEADS // 2):
            v_ref[j] = _dot(ckvb, wv_ref[j]).astype(BF16)
        if with_q:
            cq = p_ref[:, 256:512]
            rq = lax.rsqrt(jnp.mean(cq * cq, axis=-1, keepdims=True) + EPS)
            cqb = (cq * rq * qaw_ref[...]).astype(BF16)
            for h in range(HEADS):
                qp = _dot(cqb, wq_ref[h])
                rh = lax.rsqrt(jnp.sum(qp * qp, axis=-1, keepdims=True) * (1.0 / QK_HEAD) + EPS)
                q_ref[h] = _rope(qp * rh * qnw_ref[...], cos, sin_a, sin_b).astype(BF16)

    tab = pl.BlockSpec((tm, HEAD_PAD), lambda i: (pos0 + i % nblk, 0))
    qspec = pl.BlockSpec((None, HEADS, tm, HEAD_PAD), lambda i: (i // nblk, 0, i % nblk, 0))
    kspec = pl.BlockSpec((None, HEADS, tm, HEAD_PAD), lambda i: (i // nblk, 0, key0 + i % nblk, 0))
    vspec = pl.BlockSpec((None, HEADS // 2, tm, HEAD_PAD), lambda i: (i // nblk, 0, key0 + i % nblk, 0))
    qshape = _sds((nb, HEADS, s, HEAD_PAD), BF16)
    kshape = _sds((nb, HEADS, sk, HEAD_PAD), BF16)
    vshape = _sds((nb, HEADS // 2, sk, HEAD_PAD), BF16)
    n_q = 1 if with_q else 0
    return pl.pallas_call(
        body, name=name, grid=(nb * nblk,),
        in_specs=[pl.BlockSpec((tm, 512), lambda i: (row0 + i, 0)), tab, tab, tab, _whole(wq.shape), _whole(wk.shape),
                  _whole(wv.shape), _whole(kvaw.shape), _whole(qaw.shape), _whole(qnw.shape), _whole(knw.shape)]
        + [pl.BlockSpec(memory_space=pl.ANY)] * n_into,
        out_specs=([qspec] if with_q else []) + [kspec, vspec],
        out_shape=([qshape] if with_q else []) + [kshape, vshape],
        input_output_aliases={11: n_q, 12: n_q + 1} if n_into else {}, compiler_params=_params(1),
    )(proj, *tabs, wq, wk, wv, kvaw, qaw, qnw, knw, *(into or ()))


def _prep_bwd(proj, row0, nb, s, pos0, key0, dp_rows, dp_into, tabs, wq, wk, wv, kvaw, qaw, qnw, knw, dq, dk, dv, init, tm,
              name):
    nblk = s // tm
    with_q = dq is not None
    n_init = 0 if init is None else len(init)
    n_into = 0 if dp_into is None else 1

    def body(*refs):
        p_ref, cos_ref, sa_ref, sb_ref, wq_ref, wk_ref, wv_ref, kvaw_ref, qaw_ref, qnw_ref, knw_ref = refs[:11]
        rest = list(refs[11:])
        dq_ref = rest.pop(0) if with_q else None
        dk_ref, dv_ref = rest.pop(0), rest.pop(0)
        init_refs = [rest.pop(0) for _ in range(n_init)]
        if n_into:
            rest.pop(0)
        dp_ref = rest.pop(0)
        if with_q:
            dwq_ref, dqaw_ref, dqnw_ref = rest.pop(0), rest.pop(0), rest.pop(0)
        dwk_ref, dwv_ref, dkvaw_ref, dknw_ref = rest
        accs = [dwk_ref, dwv_ref, dkvaw_ref, dknw_ref]

        @pl.when(pl.program_id(0) == 0)
        def _():
            for k, acc in enumerate(accs):
                acc[...] = init_refs[k][...] if n_init else jnp.zeros_like(acc)
            if with_q:
                dwq_ref[...] = jnp.zeros_like(dwq_ref)
                dqaw_ref[...] = jnp.zeros_like(dqaw_ref)
                dqnw_ref[...] = jnp.zeros_like(dqnw_ref)

        cos, sin_a, sin_b = cos_ref[...], sa_ref[...], sb_ref[...]
        lane = lax.broadcasted_iota(jnp.int32, (tm, HEAD_PAD), 1)
        rope_lanes = (lane >= QK_NOPE) & (lane < QK_HEAD)
        ckv = p_ref[:, 0:128]
        kpe = p_ref[:, 128:256]
        rkv = lax.rsqrt(jnp.mean(ckv * ckv, axis=-1, keepdims=True) + EPS)
        ckvh = ckv * rkv
        ckvb = (ckvh * kvaw_ref[...]).astype(BF16)
        dckv = jnp.zeros((tm, KV_LORA), F32)
        dkpe = jnp.zeros((tm, HEAD_PAD), F32)
        dknw = jnp.zeros((1, HEAD_PAD), F32)
        for h in range(HEADS):
            kp = _dot(ckvb, wk_ref[h]) + kpe
            rk = lax.rsqrt(jnp.sum(kp * kp, axis=-1, keepdims=True) * (1.0 / QK_HEAD) + EPS)
            kh = kp * rk
            dkn = _rope_t(dk_ref[h], cos, sin_a, sin_b)
            dknw = dknw + jnp.sum(dkn * kh, axis=0, keepdims=True)
            dkh = dkn * knw_ref[...]
            dkp = rk * (dkh - kh * (jnp.sum(dkh * kh, axis=-1, keepdims=True) * (1.0 / QK_HEAD)))
            dkpe = dkpe + jnp.where(rope_lanes, dkp, 0.0)
            dkpb = dkp.astype(BF16)
            dckv = dckv + _dot_nt(dkpb, wk_ref[h])
            dwk_ref[h] += _dot_tn(ckvb, dkpb)
        dknw_ref[...] += dknw
        for j in range(HEADS // 2):
            dvb = dv_ref[j].astype(BF16)
            dckv = dckv + _dot_nt(dvb, wv_ref[j])
            dwv_ref[j] += _dot_tn(ckvb, dvb)
        dkvaw_ref[...] += jnp.sum(dckv * ckvh, axis=0, keepdims=True)
        dch = dckv * kvaw_ref[...]
        dp_ref[:, 0:128] = (rkv * (dch - ckvh * jnp.mean(dch * ckvh, axis=-1, keepdims=True))).astype(BF16)
        dp_ref[:, 128:256] = dkpe.astype(BF16)
        if with_q:
            cq = p_ref[:, 256:512]
            rq = lax.rsqrt(jnp.mean(cq * cq, axis=-1, keepdims=True) + EPS)
            cqh = cq * rq
            cqb = (cqh * qaw_ref[...]).astype(BF16)
            dcq = jnp.zeros((tm, Q_LORA), F32)
            dqnw = jnp.zeros((1, HEAD_PAD), F32)
            for h in range(HEADS):
                qp = _dot(cqb, wq_ref[h])
                rh = lax.rsqrt(jnp.sum(qp * qp, axis=-1, keepdims=True) * (1.0 / QK_HEAD) + EPS)
                qh = qp * rh
                dqn = _rope_t(dq_ref[h], cos, sin_a, sin_b)
                dqnw = dqnw + jnp.sum(dqn * qh, axis=0, keepdims=True)
                dqh = dqn * qnw_ref[...]
                dqp = (rh * (dqh - qh * (jnp.sum(dqh * qh, axis=-1, keepdims=True) * (1.0 / QK_HEAD)))).astype(BF16)
                dcq = dcq + _dot_nt(dqp, wq_ref[h])
                dwq_ref[h] += _dot_tn(cqb, dqp)
            dqnw_ref[...] += dqnw
            dqaw_ref[...] += jnp.sum(dcq * cqh, axis=0, keepdims=True)
            dqc = dcq * qaw_ref[...]
            dp_ref[:, 256:512] = (rq * (dqc - cqh * jnp.mean(dqc * cqh, axis=-1, keepdims=True))).astype(BF16)
        else:
            dp_ref[:, 256:512] = jnp.zeros((tm, Q_LORA), BF16)

    tab = pl.BlockSpec((tm, HEAD_PAD), lambda i: (pos0 + i % nblk, 0))
    qspec = pl.BlockSpec((None, HEADS, tm, HEAD_PAD), lambda i: (i // nblk, 0, i % nblk, 0))
    kspec = pl.BlockSpec((None, HEADS, tm, HEAD_PAD), lambda i: (i // nblk, 0, key0 + i % nblk, 0))
    vspec = pl.BlockSpec((None, HEADS // 2, tm, HEAD_PAD), lambda i: (i // nblk, 0, key0 + i % nblk, 0))

    def acc_spec(shape):
        nd = len(shape)
        return pl.BlockSpec(shape, lambda i: (0,) * nd)

    acc_shapes = [(HEADS, KV_LORA, HEAD_PAD), (HEADS // 2, KV_LORA, HEAD_PAD), (1, KV_LORA), (1, HEAD_PAD)]
    q_shapes = [(HEADS, Q_LORA, HEAD_PAD), (1, Q_LORA), (1, HEAD_PAD)] if with_q else []
    out_shapes = [(dp_rows, 512)] + q_shapes + acc_shapes
    n_before = 11 + (1 if with_q else 0) + 2 + n_init
    return pl.pallas_call(
        body, name=name, grid=(nb * nblk,),
        in_specs=[pl.BlockSpec((tm, 512), lambda i: (row0 + i, 0)), tab, tab, tab, _whole(wq.shape), _whole(wk.shape),
                  _whole(wv.shape), _whole(kvaw.shape), _whole(qaw.shape), _whole(qnw.shape), _whole(knw.shape)]
        + ([qspec] if with_q else []) + [kspec, vspec] + [_whole(a.shape) for a in (init or [])]
        + [pl.BlockSpec(memory_space=pl.ANY)] * n_into,
        out_specs=[pl.BlockSpec((tm, 512), lambda i: (row0 + i, 0))] + [acc_spec(sh) for sh in q_shapes + acc_shapes],
        out_shape=[_sds(out_shapes[0], BF16)] + [_sds(sh, F32) for sh in out_shapes[1:]],
        input_output_aliases={n_before: 0} if n_into else {}, compiler_params=_params(1),
    )(proj, *tabs, wq, wk, wv, kvaw, qaw, qnw, knw, *([dq] if with_q else []), dk, dv, *(init or []),
      *([dp_into] if n_into else []))


def _attn_fwd(q, k, v, tq, exch=None):
    nb, _, s, _ = q.shape
    sk = k.shape[2]
    nq = s // tq
    scale = QK_HEAD ** -0.5

    def body(q_ref, k_ref, v_ref, o_ref):
        lane = lax.broadcasted_iota(jnp.int32, (tq, HEAD_PAD), 1)
        vv = v_ref[...]
        outs = []
        for hh in range(2):
            sc = _dot_nt(q_ref[hh], k_ref[hh]) * scale
            p = jnp.exp(sc - jnp.max(sc, axis=-1, keepdims=True))
            l = jnp.sum(p, axis=-1, keepdims=True)
            outs.append(_dot(p.astype(BF16), vv) / l)
        o_ref[...] = jnp.where(lane < V_HEAD, outs[0], outs[1]).astype(BF16)

    (o,), got = _hosted_call(
        body, "attn_fwd", (nb, HEADS // 2, nq),
        [pl.BlockSpec((None, 2, tq, HEAD_PAD), lambda b, j, i: (b, j, i, 0)),
         pl.BlockSpec((None, 2, sk, HEAD_PAD), lambda b, j, i: (b, j, 0, 0)),
         pl.BlockSpec((None, None, sk, HEAD_PAD), lambda b, j, i: (b, j, 0, 0))],
        [pl.BlockSpec((tq, HEAD_PAD), lambda b, j, i: (b * nq + i, j))], [_sds((nb * s, MLA_W), BF16)], (q, k, v),
        exch=exch)
    return o, got


def _attn_bwd(q, k, v, do, tq, exch=None):
    nb, _, s, _ = q.shape
    sk = k.shape[2]
    nq = s // tq
    scale = QK_HEAD ** -0.5

    def body(q_ref, k_ref, v_ref, do_ref, dq_ref, dk_ref, dv_ref):
        @pl.when(pl.program_id(2) == 0)
        def _():
            dk_ref[...] = jnp.zeros_like(dk_ref)
            dv_ref[...] = jnp.zeros_like(dv_ref)

        lane = lax.broadcasted_iota(jnp.int32, (tq, HEAD_PAD), 1)
        vv = v_ref[...]
        dov = do_ref[...]
        for hh in range(2):
            mine = (lane < V_HEAD) if hh == 0 else (lane >= V_HEAD)
            doh = jnp.where(mine, dov, jnp.zeros_like(dov))
            sc = _dot_nt(q_ref[hh], k_ref[hh]) * scale
            p = jnp.exp(sc - jnp.max(sc, axis=-1, keepdims=True))
            p = p / jnp.sum(p, axis=-1, keepdims=True)
            dp = _dot_nt(doh, vv)
            ds = (p * (dp - jnp.sum(p * dp, axis=-1, keepdims=True)) * scale).astype(BF16)
            dq_ref[hh] = _dot(ds, k_ref[hh])
            dk_ref[hh] += _dot_tn(ds, q_ref[hh])
            dv_ref[...] += _dot_tn(p.astype(BF16), doh)

    return _hosted_call(
        body, "attn_bwd", (nb, HEADS // 2, nq),
        [pl.BlockSpec((None, 2, tq, HEAD_PAD), lambda b, j, i: (b, j, i, 0)),
         pl.BlockSpec((None, 2, sk, HEAD_PAD), lambda b, j, i: (b, j, 0, 0)),
         pl.BlockSpec((None, None, sk, HEAD_PAD), lambda b, j, i: (b, j, 0, 0)),
         pl.BlockSpec((tq, HEAD_PAD), lambda b, j, i: (b * nq + i, j))],
        [pl.BlockSpec((None, 2, tq, HEAD_PAD), lambda b, j, i: (b, j, i, 0)),
         pl.BlockSpec((None, 2, sk, HEAD_PAD), lambda b, j, i: (b, j, 0, 0)),
         pl.BlockSpec((None, None, sk, HEAD_PAD), lambda b, j, i: (b, j, 0, 0))],
        [_sds(q.shape, F32), _sds(k.shape, F32), _sds(v.shape, F32)], (q, k, v, do), exch=exch)


def _group_masks(rows):
    lane = lax.broadcasted_iota(jnp.int32, (rows, GMLP_W), 1)
    return [(lane >= g * GROUP_DIM) & (lane < (g + 1) * GROUP_DIM) for g in range(GROUPS)]


def _gmlp_fwd(proj, t, wcat, bias, vnw, ones, tm):
    def body(u_ref, v_ref, wcat_ref, bias_ref, vnw_ref, ones_ref, o_ref):
        masks = _group_masks(CHUNK)
        gv = _gelu(v_ref[...])
        rv = lax.rsqrt(_group_sum(gv * gv, ones_ref) * (1.0 / GROUP_DIM) + EPS)
        vnb = (gv * rv * vnw_ref[...]).astype(BF16)
        for c in range(tm // CHUNK):
            rows = slice(c * CHUNK, (c + 1) * CHUNK)
            vc = vnb[rows]
            stack = jnp.concatenate([jnp.where(m, vc, jnp.zeros_like(vc)) for m in masks], axis=0)
            sp = _dot(wcat_ref[...], stack) + bias_ref[...]
            o_ref[rows, :] = (_gelu(u_ref[rows, :]) * sp).astype(BF16)

    return pl.pallas_call(
        body, name="gmlp_fwd", grid=(t // tm,),
        in_specs=[pl.BlockSpec((tm, GMLP_W), lambda i: (i, 1)), pl.BlockSpec((tm, GMLP_W), lambda i: (i, 2)),
                  _whole(wcat.shape), _whole(bias.shape), _whole(vnw.shape), _whole(ones.shape)],
        out_specs=pl.BlockSpec((tm, GMLP_W), lambda i: (i, 0)),
        out_shape=_sds((t, GMLP_W), BF16), compiler_params=_params(1),
    )(proj, proj, wcat, bias, vnw, ones)


def _gmlp_bwd(proj, dsg, wcat, wcat_t, bias, vnw, ones, tm):
    t = dsg.shape[0]

    def body(u_ref, v_ref, dsg_ref, wcat_ref, wcatt_ref, bias_ref, vnw_ref, ones_ref,
             du_ref, dv_ref, dws_ref, dbs_ref, dvnw_ref):
        @pl.when(pl.program_id(0) == 0)
        def _():
            dws_ref[...] = jnp.zeros_like(dws_ref)
            dbs_ref[...] = jnp.zeros_like(dbs_ref)
            dvnw_ref[...] = jnp.zeros_like(dvnw_ref)

        masks = _group_masks(CHUNK)
        v = v_ref[...]
        gv = _gelu(v)
        rv = lax.rsqrt(_group_sum(gv * gv, ones_ref) * (1.0 / GROUP_DIM) + EPS)
        xh = gv * rv
        vnb = (xh * vnw_ref[...]).astype(BF16)
        dvn_parts = []
        for c in range(tm // CHUNK):
            rows = slice(c * CHUNK, (c + 1) * CHUNK)
            vc = vnb[rows]
            stack = jnp.concatenate([jnp.where(m, vc, jnp.zeros_like(vc)) for m in masks], axis=0)
            sp = _dot(wcat_ref[...], stack) + bias_ref[...]
            u = u_ref[rows, :]
            dsg_c = dsg_ref[rows, :]
            du_ref[rows, :] = (dsg_c * sp * _gelu_grad(u)).astype(BF16)
            ds = dsg_c * _gelu(u)
            dstack = jnp.concatenate([jnp.where(m, ds, 0.0) for m in masks], axis=0)
            dbs_ref[...] += jnp.broadcast_to(jnp.sum(dstack, axis=-1, keepdims=True), dbs_ref.shape)
            dstb = dstack.astype(BF16)
            dvn_parts.append(_dot(wcatt_ref[...], dstb))
            dws_ref[...] += _dot_nt(dstb, vc)
        dvn = jnp.concatenate(dvn_parts, axis=0) if len(dvn_parts) > 1 else dvn_parts[0]
        dvnw_ref[...] += jnp.sum(dvn * xh, axis=0, keepdims=True)
        dxh = dvn * vnw_ref[...]
        gm = _group_sum(dxh * xh, ones_ref) * (1.0 / GROUP_DIM)
        dv_ref[...] = (rv * (dxh - xh * gm) * _gelu_grad(v)).astype(BF16)

    row = pl.BlockSpec((tm, GMLP_W), lambda i: (i, 0))
    return pl.pallas_call(
        body, name="gmlp_bwd", grid=(t // tm,),
        in_specs=[pl.BlockSpec((tm, GMLP_W), lambda i: (i, 1)), pl.BlockSpec((tm, GMLP_W), lambda i: (i, 2)), row,
                  _whole(wcat.shape), _whole(wcat_t.shape), _whole(bias.shape), _whole(vnw.shape), _whole(ones.shape)],
        out_specs=[row, row, pl.BlockSpec((GROUPS * CHUNK, CHUNK), lambda i: (0, 0)),
                   pl.BlockSpec((GROUPS * CHUNK, CHUNK), lambda i: (0, 0)), pl.BlockSpec((1, GMLP_W), lambda i: (0, 0))],
        out_shape=[_sds((t, GMLP_W), BF16), _sds((t, GMLP_W), BF16), _sds((GROUPS * CHUNK, CHUNK), F32),
                   _sds((GROUPS * CHUNK, CHUNK), F32), _sds((1, GMLP_W), F32)],
        compiler_params=_params(1),
    )(proj, proj, dsg, wcat, wcat_t, bias, vnw, ones)


def _mixout_fwd(o, sg, xs, mod, wout, s, tm):
    t = o.shape[0]
    d = xs.shape[1]

    def body(o_ref, sg_ref, x_ref, mod_ref, w_ref, x2_ref, mix_ref):
        g = (pl.program_id(0) * tm) // s
        gate = mod_ref[g, pl.ds(5, 1), :]
        mix = _dot(o_ref[...], w_ref[0:MLA_W, :]) + _dot(sg_ref[...], w_ref[MLA_W:MLA_W + GMLP_W, :])
        x2_ref[...] = x_ref[...] + gate * mix
        mix_ref[...] = mix.astype(BF16)

    row = lambda i: (i, 0)
    return pl.pallas_call(
        body, name="mixout_fwd", grid=(t // tm,),
        in_specs=[pl.BlockSpec((tm, MLA_W), row), pl.BlockSpec((tm, GMLP_W), row), pl.BlockSpec((tm, d), row),
                  _whole(mod.shape), _whole(wout.shape)],
        out_specs=[pl.BlockSpec((tm, d), row), pl.BlockSpec((tm, d), row)],
        out_shape=[_sds((t, d), F32), _sds((t, d), BF16)], compiler_params=_params(1),
    )(o, sg, xs, mod, wout)


def _mixout_bwd(dx2, mix, mod, wout, s, tm):
    t, d = dx2.shape

    def body(dx_ref, mix_ref, mod_ref, w_ref, dmix_ref, do_ref, dsg_ref, dmod_ref):
        i = pl.program_id(0)

        @pl.when(i == 0)
        def _():
            dmod_ref[...] = jnp.zeros_like(dmod_ref)

        g = (i * tm) // s
        gate = mod_ref[g, pl.ds(5, 1), :]
        dx = dx_ref[...]
        dmod_ref[g, pl.ds(5, 1), :] += jnp.sum(dx * mix_ref[...].astype(F32), axis=0, keepdims=True)
        dmb = (gate * dx).astype(BF16)
        dmix_ref[...] = dmb
        do_ref[...] = _dot_nt(dmb, w_ref[0:MLA_W, :]).astype(BF16)
        dsg_ref[...] = _dot_nt(dmb, w_ref[MLA_W:MLA_W + GMLP_W, :])

    row = lambda i: (i, 0)
    return pl.pallas_call(
        body, name="mixout_bwd", grid=(t // tm,),
        in_specs=[pl.BlockSpec((tm, d), row), pl.BlockSpec((tm, d), row), _whole(mod.shape), _whole(wout.shape)],
        out_specs=[pl.BlockSpec((tm, d), row), pl.BlockSpec((tm, MLA_W), row), pl.BlockSpec((tm, GMLP_W), row),
                   pl.BlockSpec(mod.shape, lambda i: (0, 0, 0))],
        out_shape=[_sds((t, d), BF16), _sds((t, MLA_W), BF16), _sds((t, GMLP_W), F32), _sds(mod.shape, F32)],
        compiler_params=_params(1),
    )(dx2, mix, mod, wout)


def _loss_head(yv, target, tm):
    t, d = yv.shape
    nsteps = t // tm

    def body(y_ref, t_ref, dy_ref, loss_ref, acc_ref):
        i = pl.program_id(0)

        @pl.when(i == 0)
        def _():
            acc_ref[...] = jnp.zeros_like(acc_ref)

        e = y_ref[...] - t_ref[...]
        dy_ref[...] = e * (1.0 / d)
        acc_ref[...] += jnp.sum(e * e, axis=0, keepdims=True)

        @pl.when(i == nsteps - 1)
        def _():
            loss_ref[...] = (0.5 / d) * jnp.sum(acc_ref[...], axis=-1, keepdims=True)

    row = lambda i: (i, 0)
    return pl.pallas_call(
        body, name="loss_head", grid=(nsteps,),
        in_specs=[pl.BlockSpec((tm, d), row), pl.BlockSpec((tm, d), row)],
        out_specs=[pl.BlockSpec((tm, d), row), pl.BlockSpec((1, 1), lambda i: (0, 0))],
        out_shape=[_sds((t, d), F32), _sds((1, 1), F32)],
        scratch_shapes=[pltpu.VMEM((1, d), F32)], compiler_params=_params(1),
    )(yv, target)


def _gather_first(shards):
    n = len(shards)

    def body(*refs):
        srcs, outs = refs[:n], refs[n:2 * n]
        ici_send, ici_recv, d2d_send, d2d_recv, local_sems = refs[2 * n:]
        x, y, c = lax.axis_index("x"), lax.axis_index("y"), lax.axis_index("c")
        me = 2 * x + y
        chips = _other_chips(x, y)

        def half(w, which):
            hr = shards[w].shape[0] // 2
            return pl.ds(pl.multiple_of(which * hr, 16), hr)

        def over_ici(w, k, arriving):
            px, py = chips[k]
            slot = 2 * px + py if arriving else me
            return pltpu.make_async_remote_copy(
                src_ref=srcs[w].at[half(w, c)], dst_ref=outs[w].at[slot, half(w, c)], send_sem=ici_send.at[3 * w + k],
                recv_sem=ici_recv.at[3 * w + k], device_id=(px, py, c), device_id_type=pl.DeviceIdType.MESH)

        def to_sibling(w, k, arriving):
            px, py = chips[k]
            rows = half(w, 1 - c if arriving else c)
            return pltpu.make_async_remote_copy(
                src_ref=outs[w].at[2 * px + py, rows], dst_ref=outs[w].at[2 * px + py, rows],
                send_sem=d2d_send.at[3 * w + k], recv_sem=d2d_recv.at[3 * w + k], device_id=(x, y, 1 - c),
                device_id_type=pl.DeviceIdType.MESH)

        local = [pltpu.make_async_copy(srcs[w], outs[w].at[me], local_sems.at[w]) for w in range(n)]
        for cp in local:
            cp.start()
        pairs = [(w, k) for w in range(n) for k in range(3)]
        for w, k in pairs:
            over_ici(w, k, False).start()
        for w, k in pairs:
            over_ici(w, k, True).wait_recv()
            to_sibling(w, k, False).start()
        for w, k in pairs:
            to_sibling(w, k, True).wait_recv()
        for w, k in pairs:
            over_ici(w, k, False).wait_send()
            to_sibling(w, k, False).wait_send()
        for cp in local:
            cp.wait()

    any_spec = pl.BlockSpec(memory_space=pl.ANY)
    sems = pltpu.SemaphoreType.DMA((3 * n,))
    return pl.pallas_call(
        body, name="gather_first", in_specs=[any_spec] * n, out_specs=[any_spec] * n,
        out_shape=_exch_shapes("gather", shards),
        scratch_shapes=[sems, sems, sems, sems, pltpu.SemaphoreType.DMA((n,))],
    )(*shards)


def _swap_cores(parts, name):
    n = len(parts)

    def body(*refs):
        srcs, outs, send_sems, recv_sems = refs[:n], refs[n:2 * n], refs[2 * n], refs[2 * n + 1]
        x, y, c = lax.axis_index("x"), lax.axis_index("y"), lax.axis_index("c")
        copies = [pltpu.make_async_remote_copy(
            src_ref=srcs[w], dst_ref=outs[w], send_sem=send_sems.at[w], recv_sem=recv_sems.at[w],
            device_id=(x, y, 1 - c), device_id_type=pl.DeviceIdType.MESH) for w in range(n)]
        for cp in copies:
            cp.start()
        for cp in copies:
            cp.wait()

    any_spec = pl.BlockSpec(memory_space=pl.ANY)
    return pl.pallas_call(
        body, name=name, in_specs=[any_spec] * n, out_specs=[any_spec] * n,
        out_shape=[_sds(p.shape, p.dtype) for p in parts],
        scratch_shapes=[pltpu.SemaphoreType.DMA((n,)), pltpu.SemaphoreType.DMA((n,))],
    )(*parts)


def _row_tile(r, c, mult):
    return _div_tile(r, max(mult, (1 << 16) // c), mult)


def _sum_slots(recv, name):
    _, r, c = recv.shape
    tr = _row_tile(r, c, 16)

    def body(r_ref, o_ref):
        f = lambda k: r_ref[k].astype(F32)
        o_ref[...] = ((f(0) + f(1)) + f(2)) + f(3)

    return pl.pallas_call(
        body, name=name, grid=(r // tr,),
        in_specs=[pl.BlockSpec((N_CHIPS, tr, c), lambda i: (0, i, 0))],
        out_specs=pl.BlockSpec((tr, c), lambda i: (i, 0)),
        out_shape=_sds((r, c), F32), compiler_params=_params(1),
    )(recv)


def _adamw(parts, w, m, v, name, exch=None):
    r, wd = w.shape
    tr = _row_tile(r, wd, 8)
    c1 = 1.0 / (1.0 - ADAM_B1 ** ADAM_STEP)
    c2 = 1.0 / (1.0 - ADAM_B2 ** ADAM_STEP)
    n_p = len(parts)

    def body(*refs):
        p_refs = refs[:n_p]
        w_ref, m_ref, v_ref, g_ref, d_ref, nm_ref, nv_ref = refs[n_p:]
        g = p_refs[0][...]
        for p_ref in p_refs[1:]:
            g = g + p_ref[...]
        nm = ADAM_B1 * m_ref[...] + (1.0 - ADAM_B1) * g
        nv = ADAM_B2 * v_ref[...] + (1.0 - ADAM_B2) * (g * g)
        g_ref[...] = g
        nm_ref[...] = nm
        nv_ref[...] = nv
        d_ref[...] = -ADAM_LR * ((nm * c1) / (jnp.sqrt(nv * c2) + ADAM_EPS) + ADAM_WD * w_ref[...])

    spec = pl.BlockSpec((tr, wd), lambda i: (i, 0))
    return _hosted_call(body, name, (r // tr,), [spec] * (n_p + 3), [spec] * 4, [_sds((r, wd), F32)] * 4,
                        (*parts, w, m, v), exch=exch)


def _all_peers(x, y, c):
    flips = [(dx, dy, dc) for dx in (0, 1) for dy in (0, 1) for dc in (0, 1)][1:]
    return [(1 - x if dx else x, 1 - y if dy else y, 1 - c if dc else c) for dx, dy, dc in flips]


def _gather_devices(block):
    def body(src_ref, out_ref, send_sems, recv_sems, local_sem):
        x, y, c = lax.axis_index("x"), lax.axis_index("y"), lax.axis_index("c")
        me = 4 * x + 2 * y + c
        mine = pltpu.make_async_copy(src_ref, out_ref.at[me], local_sem)
        mine.start()

        def copy(k, peer, slot):
            return pltpu.make_async_remote_copy(
                src_ref=src_ref, dst_ref=out_ref.at[slot], send_sem=send_sems.at[k], recv_sem=recv_sems.at[k],
                device_id=peer, device_id_type=pl.DeviceIdType.MESH)

        peers = _all_peers(x, y, c)
        for k, peer in enumerate(peers):
            copy(k, peer, me).start()
        for k, (px, py, pc) in enumerate(peers):
            copy(k, (px, py, pc), 4 * px + 2 * py + pc).wait_recv()
        for k, peer in enumerate(peers):
            copy(k, peer, me).wait_send()
        mine.wait()

    return pl.pallas_call(
        body, name="gather_devices", in_specs=[pl.BlockSpec(memory_space=pl.ANY)],
        out_specs=pl.BlockSpec(memory_space=pl.ANY), out_shape=_sds((8,) + block.shape, block.dtype),
        scratch_shapes=[pltpu.SemaphoreType.DMA((7,)), pltpu.SemaphoreType.DMA((7,)), pltpu.SemaphoreType.DMA(())],
    )(block)


def _ada_fwd_tp(cc_all, w, b):
    n = w.shape[1]

    def body(cc_ref, w_ref, b_ref, out_ref, part_ref, send_sems, recv_sems):
        x, y, c = lax.axis_index("x"), lax.axis_index("y"), lax.axis_index("c")
        me = 2 * x + y
        cv = cc_ref[...]
        part_ref[...] = _dot((cv * _sigmoid(cv)).astype(BF16), w_ref[...]) + b_ref[...]

        def rows_of(px, py):
            return part_ref.at[pl.ds(pl.multiple_of((4 * px + 2 * py + c) * MOD_ROWS, MOD_ROWS), MOD_ROWS)]

        def copy(k, px, py, slot):
            return pltpu.make_async_remote_copy(
                src_ref=rows_of(px, py), dst_ref=out_ref.at[slot], send_sem=send_sems.at[k], recv_sem=recv_sems.at[k],
                device_id=(px, py, c), device_id_type=pl.DeviceIdType.MESH)

        chips = _other_chips(x, y)
        for k, (px, py) in enumerate(chips):
            copy(k, px, py, me).start()
        out_ref[me] = rows_of(x, y)[...]
        for k, (px, py) in enumerate(chips):
            copy(k, px, py, 2 * px + py).wait_recv()
        for k, (px, py) in enumerate(chips):
            copy(k, px, py, me).wait_send()

    vmem = pl.BlockSpec(memory_space=pltpu.VMEM)
    return pl.pallas_call(
        body, name="ada_fwd_tp", in_specs=[vmem, vmem, vmem], out_specs=vmem,
        out_shape=_sds((N_CHIPS, MOD_ROWS, n), F32),
        scratch_shapes=[pltpu.VMEM((8 * MOD_ROWS, n), F32), pltpu.SemaphoreType.DMA((3,)), pltpu.SemaphoreType.DMA((3,))],
        compiler_params=pltpu.CompilerParams(vmem_limit_bytes=V7X_VMEM_LIMIT),
    )(cc_all, w, b)


def _ada_bwd_tp(cc_all, dmods, w, ctx_row):
    d, n = w.shape

    def body(cc_ref, m0, m1, m2, m3, w_ref, dw_ref, db_ref, dctx_ref, stage_ref, all_ref, send_sems, recv_sems):
        x, y, c = lax.axis_index("x"), lax.axis_index("y"), lax.axis_index("c")
        me = 4 * x + 2 * y + c
        dsum = m0[...] + m1[...] + m2[...] + m3[...]
        db_ref[...] = jnp.sum(dsum, axis=0, keepdims=True)
        for j in range(N_CHIPS):
            stage_ref[j] = dsum[:, j * n:(j + 1) * n]

        def copy(k, peer, slot):
            px, py, _ = peer
            return pltpu.make_async_remote_copy(
                src_ref=stage_ref.at[2 * px + py], dst_ref=all_ref.at[slot], send_sem=send_sems.at[k],
                recv_sem=recv_sems.at[k], device_id=peer, device_id_type=pl.DeviceIdType.MESH)

        peers = _all_peers(x, y, c)
        for k, peer in enumerate(peers):
            copy(k, peer, me).start()
        all_ref[me] = stage_ref[2 * x + y]
        for k, (px, py, pc) in enumerate(peers):
            copy(k, (px, py, pc), 4 * px + 2 * py + pc).wait_recv()
        for k, peer in enumerate(peers):
            copy(k, peer, me).wait_send()
        cv = cc_ref[...]
        sig = _sigmoid(cv)
        dmb = all_ref[...].reshape(8 * MOD_ROWS, n).astype(BF16)
        dw_ref[...] = _dot_tn((cv * sig).astype(BF16), dmb)
        dsc = _dot_nt(dmb, w_ref[...])
        dctx = dsc[ctx_row:ctx_row + 1, :]
        for dev in range(1, 8):
            dctx = dctx + dsc[dev * MOD_ROWS + ctx_row:dev * MOD_ROWS + ctx_row + 1, :]
        cx = cv[ctx_row:ctx_row + 1, :]
        sx = sig[ctx_row:ctx_row + 1, :]
        dctx_ref[...] = dctx * (sx * (1.0 + cx * (1.0 - sx))) * jnp.where(c == 0, 1.0, 0.0)

    vmem = pl.BlockSpec(memory_space=pltpu.VMEM)
    return pl.pallas_call(
        body, name="ada_bwd_tp", in_specs=[vmem] * 6, out_specs=[vmem] * 3,
        out_shape=[_sds((d, n), F32), _sds((1, N_MOD * d), F32), _sds((1, d), F32)],
        scratch_shapes=[pltpu.VMEM((N_CHIPS, MOD_ROWS, n), F32), pltpu.VMEM((8, MOD_ROWS, n), F32),
                        pltpu.SemaphoreType.DMA((7,)), pltpu.SemaphoreType.DMA((7,))],
        compiler_params=pltpu.CompilerParams(vmem_limit_bytes=V7X_VMEM_LIMIT),
    )(cc_all, *dmods, w)


def _rope_tables(s, ctx):
    pos = np.arange(s, dtype=np.float32)
    inv = (np.float32(ROPE_BASE) ** (-np.arange(0, QK_ROPE // 2, 2, dtype=np.float32) / np.float32(QK_ROPE // 2)))
    ang_r = np.floor(pos / GRID_W)[:, None] * inv
    ang_c = (pos - GRID_W * np.floor(pos / GRID_W))[:, None] * inv
    ang = np.concatenate([ang_r, ang_r, ang_c, ang_c], axis=-1).astype(np.float32)
    cos, sin = np.cos(ang), np.sin(ang)
    half_b = (np.arange(QK_ROPE) // 8) % 2 == 1
    sin_a = np.where(half_b, sin, 0.0)
    sin_b = np.where(half_b, 0.0, -sin)

    def place(tab, fill):
        full = np.full((s + ctx, HEAD_PAD), fill, np.float32)
        full[:s, QK_NOPE:QK_HEAD] = tab
        return jnp.asarray(full)

    return place(cos, 1.0), place(sin_a, 0.0), place(sin_b, 0.0)


def _pad_last(a, n):
    return jnp.pad(a, [(0, 0)] * (a.ndim - 1) + [(0, n - a.shape[-1])])


def _flat_rows(parts, rows, width):
    flat = jnp.concatenate([p.reshape(-1) for p in parts])
    return jnp.pad(flat, (0, rows * width - flat.shape[0])).reshape(rows, width)


def kernel(x, c, ctx, c_ctx, w_ada, b_ada, norm1_w, ffn1_w1, ffn1_w3, ffn1_w2, norm2_w, w_in, q_a_norm_w, w_uq, kv_a_norm_w, w_ukv, q_norm_w, k_norm_w, v_norm_w, w_s, b_s, w_out, norm3_w, ffn2_w1, ffn2_w3, ffn2_w2, loss_target, m_c_ctx, m_w_ada, m_b_ada, m_norm1_w, m_ffn1_w1, m_ffn1_w3, m_ffn1_w2, m_norm2_w, m_w_in, m_q_a_norm_w, m_w_uq, m_kv_a_norm_w, m_w_ukv, m_q_norm_w, m_k_norm_w, m_v_norm_w, m_w_s, m_b_s, m_w_out, m_norm3_w, m_ffn2_w1, m_ffn2_w3, m_ffn2_w2, v_c_ctx, v_w_ada, v_b_ada, v_norm1_w, v_ffn1_w1, v_ffn1_w3, v_ffn1_w2, v_norm2_w, v_w_in, v_q_a_norm_w, v_w_uq, v_kv_a_norm_w, v_w_ukv, v_q_norm_w, v_k_norm_w, v_v_norm_w, v_w_s, v_b_s, v_w_out, v_norm3_w, v_ffn2_w1, v_ffn2_w3, v_ffn2_w2):
    wts = dict(c_ctx=c_ctx, w_ada=w_ada, b_ada=b_ada, norm1_w=norm1_w, ffn1_w1=ffn1_w1, ffn1_w3=ffn1_w3, ffn1_w2=ffn1_w2,
               norm2_w=norm2_w, w_in=w_in, q_a_norm_w=q_a_norm_w, w_uq=w_uq, kv_a_norm_w=kv_a_norm_w, w_ukv=w_ukv,
               q_norm_w=q_norm_w, k_norm_w=k_norm_w, v_norm_w=v_norm_w, w_s=w_s, b_s=b_s, w_out=w_out, norm3_w=norm3_w,
               ffn2_w1=ffn2_w1, ffn2_w3=ffn2_w3, ffn2_w2=ffn2_w2)
    moms = dict(c_ctx=m_c_ctx, w_ada=m_w_ada, b_ada=m_b_ada, norm1_w=m_norm1_w, ffn1_w1=m_ffn1_w1, ffn1_w3=m_ffn1_w3,
                ffn1_w2=m_ffn1_w2, norm2_w=m_norm2_w, w_in=m_w_in, q_a_norm_w=m_q_a_norm_w, w_uq=m_w_uq,
                kv_a_norm_w=m_kv_a_norm_w, w_ukv=m_w_ukv, q_norm_w=m_q_norm_w, k_norm_w=m_k_norm_w, v_norm_w=m_v_norm_w,
                w_s=m_w_s, b_s=m_b_s, w_out=m_w_out, norm3_w=m_norm3_w, ffn2_w1=m_ffn2_w1, ffn2_w3=m_ffn2_w3,
                ffn2_w2=m_ffn2_w2)
    vars_ = dict(c_ctx=v_c_ctx, w_ada=v_w_ada, b_ada=v_b_ada, norm1_w=v_norm1_w, ffn1_w1=v_ffn1_w1, ffn1_w3=v_ffn1_w3,
                 ffn1_w2=v_ffn1_w2, norm2_w=v_norm2_w, w_in=v_w_in, q_a_norm_w=v_q_a_norm_w, w_uq=v_w_uq,
                 kv_a_norm_w=v_kv_a_norm_w, w_ukv=v_w_ukv, q_norm_w=v_q_norm_w, k_norm_w=v_k_norm_w, v_norm_w=v_v_norm_w,
                 w_s=v_w_s, b_s=v_b_s, w_out=v_w_out, norm3_w=v_norm3_w, ffn2_w1=v_ffn2_w1, ffn2_w3=v_ffn2_w3,
                 ffn2_w2=v_ffn2_w2)

    nb, s, d = x.shape
    nctx = ctx.shape[1]
    t, tc = nb * s, nb * nctx
    t_all = t + tc
    sk = s + nctx
    assert nb + 1 <= MOD_ROWS and d % LANES == 0
    tm = _token_tile(s, nctx)

    shard = {n: wts[n][0].astype(BF16) for n in SHARDED}
    full = {}

    def unshard(names, blocks):
        for n, g4 in zip(names, blocks):
            _, r_, c_ = g4.shape
            full[n] = g4.reshape(N_CHIPS * r_, c_) if n in ROW_SHARDED else g4.transpose(1, 0, 2).reshape(r_, N_CHIPS * c_)

    def chip_major(n, g_):
        if n in ROW_SHARDED:
            return g_.reshape(N_CHIPS, g_.shape[0] // N_CHIPS, g_.shape[1]).astype(BF16)
        r_, cols = g_.shape
        return g_.reshape(r_, N_CHIPS, cols // N_CHIPS).transpose(1, 0, 2).astype(BF16)

    unshard(FIRST_WEIGHTS, _gather_first([shard[n] for n in FIRST_WEIGHTS]))
    wsb = w_s[0].astype(BF16)
    wcat = wsb.transpose(1, 0, 2).reshape(CHUNK, GROUPS * CHUNK)
    wcat_t = wsb.transpose(2, 0, 1).reshape(CHUNK, GROUPS * CHUNK)
    bias = jnp.repeat(b_s[0].T, GROUP_DIM, axis=1)
    vnw = v_norm_w.reshape(1, GMLP_W)
    lane = jnp.arange(GMLP_W)
    ones = (lane[:, None] // GROUP_DIM == lane[None, :] // GROUP_DIM).astype(BF16)
    qnw = _pad_last(q_norm_w, HEAD_PAD)
    knw = _pad_last(k_norm_w, HEAD_PAD)
    tabs = _rope_tables(s, nctx)

    cc = jnp.concatenate([c, c_ctx[None, :], jnp.zeros((MOD_ROWS - nb - 1, d), F32)], axis=0)
    cc_all = _gather_devices(cc).reshape(8 * MOD_ROWS, d)
    n_ada = shard["w_ada"].shape[1]
    assert n_ada % LANES == 0
    my_chip = 2 * lax.axis_index("x") + lax.axis_index("y")
    b_cols = lax.dynamic_slice_in_dim(b_ada, my_chip * n_ada, n_ada, axis=1)
    mod = _ada_fwd_tp(cc_all, shard["w_ada"], b_cols).transpose(1, 0, 2).reshape(MOD_ROWS, N_MOD, d)
    xs0 = jnp.concatenate([x.reshape(t, d), ctx.reshape(tc, d)], axis=0)
    (xs1, a1, b1, y1), got = _ffn_fwd(xs0, mod, norm1_w, full["ffn1_w1"], full["ffn1_w3"], full["ffn1_w2"], 0, s, nb, tm,
                                      "ffn1_fwd", exch=("gather", [shard[n] for n in MIX_WEIGHTS]))
    unshard(MIX_WEIGHTS, got)
    wi = full["w_in"]
    wp = jnp.concatenate([wi[:, 0:KV_LORA], jnp.zeros((d, QK_NOPE), BF16), wi[:, KV_LORA:KV_LORA + QK_ROPE],
                          jnp.zeros((d, HEAD_PAD - QK_HEAD), BF16), wi[:, KV_LORA + QK_ROPE:]], axis=1)
    wq = _pad_last(full["w_uq"].reshape(Q_LORA, HEADS, QK_HEAD).transpose(1, 0, 2), HEAD_PAD)
    wkv = full["w_ukv"].reshape(KV_LORA, HEADS, QK_NOPE + V_HEAD)
    wk = _pad_last(wkv[:, :, :QK_NOPE].transpose(1, 0, 2), HEAD_PAD)
    wv = wkv[:, :, QK_NOPE:].reshape(KV_LORA, HEADS // 2, 2 * V_HEAD).transpose(1, 0, 2)
    h2, proj = _mixin_fwd(xs1, mod, norm2_w, wp, s, nb, tm)
    prep_w = (wq, wk, wv, kv_a_norm_w, q_a_norm_w, qnw, knw)
    q, k_all, v_all = _prep_fwd(proj, 0, nb, s, 0, sk, 0, None, tabs, *prep_w, tm, True, "prep_fwd")
    k_all, v_all = _prep_fwd(proj, t // tm, nb, nctx, s // tm, sk, s // tm, (k_all, v_all), tabs, *prep_w, tm, False,
                             "prep_ctx_fwd")
    o, got = _attn_fwd(q, k_all, v_all, tm, exch=("gather", [shard[n] for n in LAST_WEIGHTS]))
    unshard(LAST_WEIGHTS, got)
    sg = _gmlp_fwd(proj, t, wcat, bias, vnw, ones, tm)
    x2, mix = _mixout_fwd(o, sg, xs1, mod, full["w_out"], s, tm)
    (yv, a2, b2, y2), _ = _ffn_fwd(x2, mod, norm3_w, full["ffn2_w1"], full["ffn2_w3"], full["ffn2_w2"], 6, s, nb, tm,
                                   "ffn2_fwd")
    dy, loss_part = _loss_head(yv, loss_target.reshape(t, d), tm)
    loss = lax.psum(loss_part[0, 0], ("x", "y", "c"))

    grads, cm, recv = {}, {}, {}

    def scatter_of(names):
        return ("scatter", [cm[n] for n in names])

    (dx2, h3, g2, da2, db2, dyb2, dmod_c, grads["norm3_w"]), _ = _ffn_bwd(
        dy, x2, a2, b2, y2, mod, norm3_w, full["ffn2_w1"], full["ffn2_w3"], full["ffn2_w2"], 6, s, nb, tm, "ffn2_bwd")
    cm["ffn2_w1"] = chip_major("ffn2_w1", _mm_tn(h3, da2, t, "ffn2_dw1"))
    cm["ffn2_w3"] = chip_major("ffn2_w3", _mm_tn(h3, db2, t, "ffn2_dw3"))
    cm["ffn2_w2"] = chip_major("ffn2_w2", _mm_tn(g2, dyb2, t, "ffn2_dw2"))
    dmix, do, dsg, dmod_b = _mixout_bwd(dx2, mix, mod, full["w_out"], s, tm)
    cm["w_out"] = chip_major("w_out", jnp.concatenate([_mm_tn(o, dmix, t, "wout_dw_attn"),
                                                       _mm_tn(sg, dmix, t, "wout_dw_gmlp")], axis=0))
    dpu, dpv, dws, dbs, dvnw = _gmlp_bwd(proj, dsg, wcat, wcat_t, bias, vnw, ones, tm)
    group = LAST_WEIGHTS + ("w_out",)
    (dq, dk, dv), got = _attn_bwd(q, k_all, v_all, do, tm, exch=scatter_of(group))
    recv.update(zip(group, got))
    dp0, dwk_c, dwv_c, dkvaw_c, dknw_c = _prep_bwd(
        proj, t // tm, nb, nctx, s // tm, s // tm, t_all, None, tabs, *prep_w, None, dk, dv, None, tm, "prep_ctx_bwd")
    dp0, dwq, dqaw, dqnw, dwk, dwv, dkvaw, dknw = _prep_bwd(
        proj, 0, nb, s, 0, 0, t_all, dp0, tabs, *prep_w, dq, dk, dv, [dwk_c, dwv_c, dkvaw_c, dknw_c], tm, "prep_bwd")
    dxs1, dmod_a, grads["norm2_w"] = _mixin_bwd(dp0, dpu, dpv, xs1, dx2, mod, norm2_w, wp, s, nb, tm)
    dwp = jnp.concatenate([_mm_tn(h2, dp0, t_all, "win_dw_kvq"), _mm_tn(h2, dpu, t, "win_dw_u"),
                           _mm_tn(h2, dpv, t, "win_dw_v")], axis=1)
    cm["w_in"] = chip_major("w_in", jnp.concatenate(
        [dwp[:, 0:KV_LORA], dwp[:, KV_LORA + QK_NOPE:KV_LORA + QK_HEAD], dwp[:, 256:]], axis=1))
    cm["w_uq"] = chip_major("w_uq", dwq[:, :, :QK_HEAD].transpose(1, 0, 2).reshape(Q_LORA, HEADS * QK_HEAD))
    cm["w_ukv"] = chip_major("w_ukv", jnp.concatenate(
        [dwk[:, :, :QK_NOPE].transpose(1, 0, 2),
         dwv.transpose(1, 0, 2).reshape(KV_LORA, HEADS, V_HEAD)], axis=2).reshape(KV_LORA, HEADS * (QK_NOPE + V_HEAD)))
    group = ("w_in", "w_uq", "w_ukv")
    (dxs0, h1, g1, da1, db1, dyb1, dmod_0, grads["norm1_w"]), got = _ffn_bwd(
        dxs1, xs0, a1, b1, y1, mod, norm1_w, full["ffn1_w1"], full["ffn1_w3"], full["ffn1_w2"], 0, s, nb, tm, "ffn1_bwd",
        exch=scatter_of(group))
    recv.update(zip(group, got))
    cm["ffn1_w2"] = chip_major("ffn1_w2", _mm_tn(g1, dyb1, t_all, "ffn1_dw2"))
    dw1, got = _mm_tn(h1, da1, t_all, "ffn1_dw1", exch=scatter_of(("ffn1_w2",)))
    recv["ffn1_w2"] = got[0]
    cm["ffn1_w1"] = chip_major("ffn1_w1", dw1)
    dw3, got = _mm_tn(h1, db1, t_all, "ffn1_dw3", exch=scatter_of(("ffn1_w1",)))
    recv["ffn1_w1"] = got[0]
    cm["ffn1_w3"] = chip_major("ffn1_w3", dw3)
    dmods = [m_.reshape(MOD_ROWS, N_MOD * d) for m_ in (dmod_0, dmod_a, dmod_b, dmod_c)]
    dw_ada, grads["b_ada"], dctx = _ada_bwd_tp(cc_all, dmods, shard["w_ada"], nb)
    grads["c_ctx"] = dctx[0]
    grads["q_a_norm_w"], grads["kv_a_norm_w"] = dqaw, dkvaw
    grads["q_norm_w"], grads["k_norm_w"] = dqnw[:, :QK_HEAD], dknw[:, :QK_HEAD]
    grads["v_norm_w"], grads["w_s"], grads["b_s"] = dvnw, dws, dbs[:, 0]
    grad_x = dxs0[:t].reshape(nb, s, d)
    rows_s = _round_up(-(-sum(wts[n].size for n in SMALL) // d), 16)
    small = _flat_rows([grads[n] for n in SMALL], rows_s, d)
    stepped = {}
    stepped["w_ada"], got = _adamw([dw_ada], wts["w_ada"][0], moms["w_ada"][0], vars_["w_ada"][0], "adamw_w_ada",
                                   exch=("scatter", [cm["ffn1_w3"], jnp.broadcast_to(small, (N_CHIPS, rows_s, d))]))
    recv["ffn1_w3"], recv["small"] = got

    reduced = tuple(n for n in SHARDED if n != "w_ada") + ("small",)
    part = {n: _sum_slots(recv[n], "sum_" + n) for n in reduced}
    early = LAST_WEIGHTS + ("w_out",)
    late = tuple(n for n in reduced if n not in early)
    sib = dict(zip(early, _swap_cores([part[n] for n in early], "swap_early")))
    sib.update(zip(late, _swap_cores([part[n] for n in late], "swap_late")))
    for n in reduced[:-1]:
        stepped[n], _ = _adamw([part[n], sib[n]], wts[n][0], moms[n][0], vars_[n][0], "adamw_" + n)
    for n in SHARDED:
        stepped[n] = [a_.reshape(wts[n].shape) for a_ in stepped[n]]
    packed, _ = _adamw([part["small"], sib["small"]], _flat_rows([wts[n] for n in SMALL], rows_s, d),
                       _flat_rows([moms[n] for n in SMALL], rows_s, d), _flat_rows([vars_[n] for n in SMALL], rows_s, d),
                       "adamw_small")
    for n in SMALL:
        stepped[n] = []
    for a_ in packed:
        flat = a_.reshape(-1)
        off = 0
        for n in SMALL:
            stepped[n].append(flat[off:off + wts[n].size].reshape(wts[n].shape))
            off += wts[n].size
    return (loss, grad_x, *[stepped[n][0] for n in WEIGHTS], *[stepped[n][1] for n in WEIGHTS],
            *[stepped[n][2] for n in WEIGHTS], *[stepped[n][3] for n in WEIGHTS])
```

```python
import functools
import math

import jax
import jax.numpy as jnp
import numpy as np
from jax import lax
from jax.experimental import pallas as pl
from jax.experimental.pallas import tpu as pltpu

F32 = jnp.float32
BF16 = jnp.bfloat16

EPS = 1e-6
N_MOD = 9
HEADS = 8
QK_NOPE, QK_ROPE, V_HEAD = 64, 32, 64
QK_HEAD = QK_NOPE + QK_ROPE
HEAD_PAD = 128
SOFTMAX_SCALE = QK_HEAD ** -0.5
Q_LORA, KV_LORA = 256, 128
GROUPS, GROUP_DIM, CHUNK = 8, 64, 128
GMLP_W = GROUPS * GROUP_DIM
MLA_W = HEADS * V_HEAD
IN_COLS = 1440
PROJ_COLS = 1536
GRID_W = 64
ROPE_BASE = 10000.0
MOD_ROWS = 16
ADAM_LR, ADAM_B1, ADAM_B2, ADAM_EPS, ADAM_WD, ADAM_STEP = 0.001, 0.9, 0.999, 1e-08, 0.01, 10
N_CHIPS = 4
LANES = 128
V7X_VMEM_LIMIT = 56 * 1024 * 1024
GELU_C = math.sqrt(2.0 / math.pi)

SHARDED = ("w_ada", "ffn1_w1", "ffn1_w3", "ffn1_w2", "w_in", "w_uq", "w_ukv", "w_out", "ffn2_w1", "ffn2_w3", "ffn2_w2")
ROW_SHARDED = ("ffn1_w2", "w_out", "ffn2_w2")
FIRST_WEIGHTS = ("ffn1_w1", "ffn1_w3", "ffn1_w2")
MIX_WEIGHTS = ("w_in", "w_uq", "w_ukv", "w_out")
LAST_WEIGHTS = ("ffn2_w1", "ffn2_w3", "ffn2_w2")
SMALL = ("c_ctx", "b_ada", "norm1_w", "norm2_w", "q_a_norm_w", "kv_a_norm_w", "q_norm_w", "k_norm_w", "v_norm_w",
         "w_s", "b_s", "norm3_w")
WEIGHTS = ("c_ctx", "w_ada", "b_ada", "norm1_w", "ffn1_w1", "ffn1_w3", "ffn1_w2", "norm2_w", "w_in", "q_a_norm_w",
           "w_uq", "kv_a_norm_w", "w_ukv", "q_norm_w", "k_norm_w", "v_norm_w", "w_s", "b_s", "w_out", "norm3_w",
           "ffn2_w1", "ffn2_w3", "ffn2_w2")


def _round_up(n, m):
    return (n + m - 1) // m * m


def _div_tile(n, target, mult):
    best = None
    for t in range(mult, min(n, target) + 1, mult):
        if n % t == 0:
            best = t
    return n if best is None else best


def _dot(a, b):
    return lax.dot_general(a, b, (((1,), (0,)), ((), ())), preferred_element_type=F32)


def _dot_nt(a, b):
    return lax.dot_general(a, b, (((1,), (1,)), ((), ())), preferred_element_type=F32)


def _dot_tn(a, b):
    return lax.dot_general(a, b, (((0,), (0,)), ((), ())), preferred_element_type=F32)


def _sigmoid(x):
    return 1.0 / (1.0 + jnp.exp(-x))


def _gelu(x):
    return 0.5 * x * (1.0 + jnp.tanh(GELU_C * (x + 0.044715 * x * x * x)))


def _gelu_grad(x):
    t = jnp.tanh(GELU_C * (x + 0.044715 * x * x * x))
    return 0.5 * (1.0 + t) + 0.5 * x * (1.0 - t * t) * (GELU_C * (1.0 + 3 * 0.044715 * x * x))


def _rope(x, cos, sin_a, sin_b):
    return x * cos + pltpu.roll(x, 8, 1) * sin_a + pltpu.roll(x, HEAD_PAD - 8, 1) * sin_b


def _rope_t(d, cos, sin_a, sin_b):
    return d * cos + pltpu.roll(d * sin_a, HEAD_PAD - 8, 1) + pltpu.roll(d * sin_b, 8, 1)


def _rope3(x, cos, sin_a, sin_b):
    return x * cos + pltpu.roll(x, 8, 2) * sin_a + pltpu.roll(x, HEAD_PAD - 8, 2) * sin_b


def _rope3_t(d, cos, sin_a, sin_b):
    return d * cos + pltpu.roll(d * sin_a, HEAD_PAD - 8, 2) + pltpu.roll(d * sin_b, 8, 2)


def _group_sum(x, ones_ref):
    hi = x.astype(BF16)
    lo = (x - hi.astype(F32)).astype(BF16)
    return _dot(hi, ones_ref[...]) + _dot(lo, ones_ref[...])


def _params(n_axes):
    return pltpu.CompilerParams(dimension_semantics=("arbitrary",) * n_axes, vmem_limit_bytes=V7X_VMEM_LIMIT)


def _whole(shape):
    nd = len(shape)
    return pl.BlockSpec(shape, lambda *_: (0,) * nd, pipeline_mode=pl.Buffered(1))


def _sds(shape, dtype):
    return jax.ShapeDtypeStruct(shape, dtype)


def _token_tile(s, ctx):
    return _div_tile(math.gcd(s, ctx), 256, CHUNK)


def _other_chips(x, y):
    return [(1 - x, y), (x, 1 - y), (1 - x, 1 - y)]


def _exch_copies(kind, srcs, dsts, send_sems, recv_sems, local_sems):
    x, y, c = lax.axis_index("x"), lax.axis_index("y"), lax.axis_index("c")
    me = 2 * x + y
    local, sends, arrivals = [], [], []
    for w, (src, dst) in enumerate(zip(srcs, dsts)):
        own = src if kind == "gather" else src.at[me]
        local.append(pltpu.make_async_copy(own, dst.at[me], local_sems.at[w]))
        for k, (px, py) in enumerate(_other_chips(x, y)):
            sem = dict(send_sem=send_sems.at[3 * w + k], recv_sem=recv_sems.at[3 * w + k], device_id=(px, py, c),
                       device_id_type=pl.DeviceIdType.MESH)
            out = src if kind == "gather" else src.at[2 * px + py]
            sends.append(pltpu.make_async_remote_copy(src_ref=out, dst_ref=dst.at[me], **sem))
            arrivals.append(pltpu.make_async_remote_copy(src_ref=own, dst_ref=dst.at[2 * px + py], **sem))
    return local, sends, arrivals


def _exch_start(kind, srcs, dsts, sems):
    local, sends, _ = _exch_copies(kind, srcs, dsts, *sems)
    for cp in local + sends:
        cp.start()


def _exch_wait(kind, srcs, dsts, sems):
    local, sends, arrivals = _exch_copies(kind, srcs, dsts, *sems)
    for cp in arrivals:
        cp.wait_recv()
    for cp in sends:
        cp.wait_send()
    for cp in local:
        cp.wait()


def _exch_scratch(n):
    return [pltpu.SemaphoreType.DMA((3 * n,)), pltpu.SemaphoreType.DMA((3 * n,)), pltpu.SemaphoreType.DMA((n,))]


def _exch_shapes(kind, arrays):
    return [_sds((N_CHIPS,) + a.shape if kind == "gather" else a.shape, a.dtype) for a in arrays]


def _hosted_call(body, name, grid, in_specs, out_specs, out_shape, operands, scratch=(), exch=None):
    n_axes = len(grid)
    if exch is None:
        outs = pl.pallas_call(body, name=name, grid=grid, in_specs=list(in_specs), out_specs=list(out_specs),
                              out_shape=list(out_shape), scratch_shapes=list(scratch),
                              compiler_params=_params(n_axes))(*operands)
        return list(outs), []
    kind, arrays = exch
    n_in, n_out, n_sc, n_ex = len(in_specs), len(out_specs), len(scratch), len(arrays)

    def hosted(*refs):
        cin, ein = refs[:n_in], refs[n_in:n_in + n_ex]
        o0 = n_in + n_ex
        cout, eout = refs[o0:o0 + n_out], refs[o0 + n_out:o0 + n_out + n_ex]
        rest = refs[o0 + n_out + n_ex:]
        csc, sems = rest[:n_sc], rest[n_sc:]
        first = functools.reduce(jnp.logical_and, [pl.program_id(a) == 0 for a in range(n_axes)])
        last = functools.reduce(jnp.logical_and, [pl.program_id(a) == grid[a] - 1 for a in range(n_axes)])

        @pl.when(first)
        def _():
            _exch_start(kind, ein, eout, sems)

        body(*cin, *cout, *csc)

        @pl.when(last)
        def _():
            _exch_wait(kind, ein, eout, sems)

    any_spec = pl.BlockSpec(memory_space=pl.ANY)
    outs = pl.pallas_call(
        hosted, name=name, grid=grid, in_specs=list(in_specs) + [any_spec] * n_ex,
        out_specs=list(out_specs) + [any_spec] * n_ex, out_shape=list(out_shape) + _exch_shapes(kind, arrays),
        scratch_shapes=list(scratch) + _exch_scratch(n_ex), compiler_params=_params(n_axes),
    )(*operands, *arrays)
    return list(outs[:n_out]), list(outs[n_out:])


class _TokenTiles:
    def __init__(self, t, tc, tm):
        self.n_lat, self.n_ctx = t // tm, tc // tm
        self.n_all = self.n_lat + self.n_ctx

    def tile(self, i):
        return (i + self.n_lat) % self.n_all if self.n_ctx else i

    def is_lat(self, i):
        return self.tile(i) < self.n_lat

    def row(self, i):
        return (self.tile(i), 0)

    def lat_row(self, i):
        return (jnp.where(self.is_lat(i), self.tile(i), 0), 0) if self.n_ctx else (i, 0)

    def ctx_row(self, i):
        return (jnp.where(self.is_lat(i), self.n_ctx - 1, self.tile(i) - self.n_lat), 0)


def _ffn_fwd(x_lat, x_ctx, mod, nw, w1, w3, w2, k0, s, nb, tm, name, target=None, exch=None):
    t, d = x_lat.shape
    tc = 0 if x_ctx is None else x_ctx.shape[0]
    f = w1.shape[1]
    tiles = _TokenTiles(t, tc, tm)
    n_x = 2 if tc else 1
    n_t = 0 if target is None else 1
    assert not (tc and n_t)

    def body(*refs):
        x_ref = refs[0]
        t_ref = refs[n_x] if n_t else None
        mod_ref, nw_ref, w1_ref, w3_ref, w2_ref, o_ref, a_ref, b_ref, y_ref = refs[n_x + n_t:n_x + n_t + 9]
        i = pl.program_id(0)
        g = jnp.minimum((tiles.tile(i) * tm) // s, nb)
        shift = mod_ref[g, pl.ds(k0, 1), :]
        scale = mod_ref[g, pl.ds(k0 + 1, 1), :]
        gate = mod_ref[g, pl.ds(k0 + 2, 1), :]
        x = jnp.where(tiles.is_lat(i), x_ref[...], refs[1][...]) if tc else x_ref[...]
        r = lax.rsqrt(jnp.mean(x * x, axis=-1, keepdims=True) + EPS)
        hb = ((x * r * nw_ref[...]) * (1.0 + scale) + shift).astype(BF16)
        a = _dot(hb, w1_ref[...])
        b = _dot(hb, w3_ref[...])
        gb = (a * _sigmoid(a) * b).astype(BF16)
        y = _dot(gb, w2_ref[...])
        out = x + (0.5 * gate) * y
        a_ref[...] = a.astype(BF16)
        b_ref[...] = b.astype(BF16)
        y_ref[...] = y.astype(BF16)
        if n_t:
            loss_ref, acc_ref = refs[-2:]

            @pl.when(i == 0)
            def _():
                acc_ref[...] = jnp.zeros_like(acc_ref)

            e = out - t_ref[...]
            o_ref[...] = e * (1.0 / d)
            acc_ref[...] += jnp.sum(e * e, axis=0, keepdims=True)

            @pl.when(i == tiles.n_all - 1)
            def _():
                loss_ref[...] = (0.5 / d) * jnp.sum(acc_ref[...], axis=-1, keepdims=True)
        else:
            o_ref[...] = out

    td = pl.BlockSpec((tm, d), tiles.row)
    tf = pl.BlockSpec((tm, f), tiles.row)
    return _hosted_call(
        body, name, (tiles.n_all,),
        [pl.BlockSpec((tm, d), tiles.lat_row)] + ([pl.BlockSpec((tm, d), tiles.ctx_row)] if tc else []) + [td] * n_t
        + [_whole(mod.shape), _whole(nw.shape), _whole(w1.shape), _whole(w3.shape), _whole(w2.shape)],
        [td, tf, tf, td] + [pl.BlockSpec((1, 1), lambda i: (0, 0))] * n_t,
        [_sds((t + tc, d), F32), _sds((t + tc, f), BF16), _sds((t + tc, f), BF16), _sds((t + tc, d), BF16)]
        + [_sds((1, 1), F32)] * n_t,
        (x_lat,) + ((x_ctx,) if tc else ()) + ((target,) if n_t else ()) + (mod, nw, w1, w3, w2),
        scratch=[pltpu.VMEM((1, d), F32)] * n_t, exch=exch)


def _ffn_bwd(dout, x_lat, x_ctx, a, b, y, mod, nw, w1, w3, w2, k0, s, nb, tm, name, exch=None):
    t, d = x_lat.shape
    tc = 0 if x_ctx is None else x_ctx.shape[0]
    f = w1.shape[1]
    nch = 2 if (f // 2) % LANES == 0 and f % 2 == 0 else 1
    fc = f // nch
    tiles = _TokenTiles(t, tc, tm)
    n_x = 2 if tc else 1

    def body(*refs):
        do_ref, x_ref = refs[0], refs[1]
        (a_ref, b_ref, y_ref, mod_ref, nw_ref, w1_ref, w3_ref, w2_ref,
         dx_ref, h_ref, g_ref, da_ref, db_ref, dy_ref, dmod_ref, dnw_ref) = refs[1 + n_x:]
        i = pl.program_id(0)

        @pl.when(i == 0)
        def _():
            dmod_ref[...] = jnp.zeros_like(dmod_ref)
            dnw_ref[...] = jnp.zeros_like(dnw_ref)

        g = jnp.minimum((tiles.tile(i) * tm) // s, nb)
        shift = mod_ref[g, pl.ds(k0, 1), :]
        scale = mod_ref[g, pl.ds(k0 + 1, 1), :]
        gate = mod_ref[g, pl.ds(k0 + 2, 1), :]
        x = jnp.where(tiles.is_lat(i), x_ref[...], refs[2][...]) if tc else x_ref[...]
        dout_v = do_ref[...]
        r = lax.rsqrt(jnp.mean(x * x, axis=-1, keepdims=True) + EPS)
        xh = x * r
        n = xh * nw_ref[...]
        h_ref[...] = (n * (1.0 + scale) + shift).astype(BF16)
        dyb = ((0.5 * gate) * dout_v).astype(BF16)
        dy_ref[...] = dyb
        dmod_ref[g, pl.ds(k0 + 2, 1), :] += 0.5 * jnp.sum(dout_v * y_ref[...].astype(F32), axis=0, keepdims=True)
        dh = jnp.zeros((tm, d), F32)
        for c in range(nch):
            sl = slice(c * fc, (c + 1) * fc)
            dg = _dot_nt(dyb, w2_ref[sl, :])
            av = a_ref[:, sl].astype(F32)
            bv = b_ref[:, sl].astype(F32)
            sig = _sigmoid(av)
            sa = av * sig
            g_ref[:, sl] = (sa * bv).astype(BF16)
            dab = (dg * bv * (sig * (1.0 + av * (1.0 - sig)))).astype(BF16)
            dbb = (dg * sa).astype(BF16)
            da_ref[:, sl] = dab
            db_ref[:, sl] = dbb
            dh = dh + _dot_nt(dab, w1_ref[:, sl]) + _dot_nt(dbb, w3_ref[:, sl])
        dmod_ref[g, pl.ds(k0, 1), :] += jnp.sum(dh, axis=0, keepdims=True)
        dmod_ref[g, pl.ds(k0 + 1, 1), :] += jnp.sum(dh * n, axis=0, keepdims=True)
        dn = dh * (1.0 + scale)
        dnw_ref[...] += jnp.sum(dn * xh, axis=0, keepdims=True)
        dxh = dn * nw_ref[...]
        dx_ref[...] = dout_v + r * (dxh - xh * jnp.mean(dxh * xh, axis=-1, keepdims=True))

    td = pl.BlockSpec((tm, d), tiles.row)
    tf = pl.BlockSpec((tm, f), tiles.row)
    lat = pl.BlockSpec((tm, d), tiles.lat_row)
    ta = t + tc
    return _hosted_call(
        body, name, (tiles.n_all,),
        [td, lat] + ([pl.BlockSpec((tm, d), tiles.ctx_row)] if tc else [])
        + [tf, tf, td, _whole(mod.shape), _whole(nw.shape), _whole(w1.shape), _whole(w3.shape), _whole(w2.shape)],
        [lat, td, tf, tf, tf, td, pl.BlockSpec(mod.shape, lambda i: (0, 0, 0)), pl.BlockSpec((1, d), lambda i: (0, 0))],
        [_sds((t, d), F32), _sds((ta, d), BF16), _sds((ta, f), BF16), _sds((ta, f), BF16), _sds((ta, f), BF16),
         _sds((ta, d), BF16), _sds(mod.shape, F32), _sds((1, d), F32)],
        (dout, x_lat) + ((x_ctx,) if tc else ()) + (a, b, y, mod, nw, w1, w3, w2), exch=exch)


def _mm_tn(a, b, rows, name, exch=None):
    m = a.shape[1]
    n = b.shape[1]
    bm = _div_tile(m, 1408, LANES)
    bn = _div_tile(n, 1408, LANES)
    bk = _div_tile(rows, 512, LANES)

    def body(a_ref, b_ref, o_ref):
        @pl.when(pl.program_id(2) == 0)
        def _():
            o_ref[...] = jnp.zeros_like(o_ref)

        o_ref[...] += _dot_tn(a_ref[...], b_ref[...])

    (out,), got = _hosted_call(
        body, name, (m // bm, n // bn, rows // bk),
        [pl.BlockSpec((bk, bm), lambda i, j, k: (k, i)), pl.BlockSpec((bk, bn), lambda i, j, k: (k, j))],
        [pl.BlockSpec((bm, bn), lambda i, j, k: (i, j))], [_sds((m, n), F32)], (a, b), exch=exch)
    return out if exch is None else (out, got)


def _mixin_fwd(xs, mod, nw, wp, s, nb, tm):
    t, d = xs.shape

    def body(x_ref, mod_ref, nw_ref, wp_ref, h_ref, p_ref):
        g = jnp.minimum((pl.program_id(0) * tm) // s, nb)
        shift = mod_ref[g, pl.ds(3, 1), :]
        scale = mod_ref[g, pl.ds(4, 1), :]
        x = x_ref[...]
        r = lax.rsqrt(jnp.mean(x * x, axis=-1, keepdims=True) + EPS)
        hb = ((x * r * nw_ref[...]) * (1.0 + scale) + shift).astype(BF16)
        h_ref[...] = hb
        p_ref[...] = _dot(hb, wp_ref[...])

    row = lambda i: (i, 0)
    return pl.pallas_call(
        body, name="mixin_fwd", grid=(t // tm,),
        in_specs=[pl.BlockSpec((tm, d), row), _whole(mod.shape), _whole(nw.shape), _whole(wp.shape)],
        out_specs=[pl.BlockSpec((tm, d), row), pl.BlockSpec((tm, PROJ_COLS), row)],
        out_shape=[_sds((t, d), BF16), _sds((t, PROJ_COLS), F32)], compiler_params=_params(1),
    )(xs, mod, nw, wp)


def _mixin_bwd(dp0, dpu, dpv, xs, dres, mod, nw, wp, s, nb, tm):
    t_all, d = xs.shape
    nlat = dres.shape[0] // tm

    def body(p0_ref, pu_ref, pv_ref, x_ref, dr_ref, mod_ref, nw_ref, wp_ref, dx_ref, dmod_ref, dnw_ref):
        i = pl.program_id(0)

        @pl.when(i == 0)
        def _():
            dmod_ref[...] = jnp.zeros_like(dmod_ref)
            dnw_ref[...] = jnp.zeros_like(dnw_ref)

        lat = i < nlat
        g = jnp.minimum((i * tm) // s, nb)
        scale = mod_ref[g, pl.ds(4, 1), :]
        dh = _dot_nt(p0_ref[...], wp_ref[:, 0:512])
        extra = _dot_nt(pu_ref[...], wp_ref[:, 512:1024]) + _dot_nt(pv_ref[...], wp_ref[:, 1024:1536])
        dh = dh + jnp.where(lat, extra, 0.0)
        x = x_ref[...]
        r = lax.rsqrt(jnp.mean(x * x, axis=-1, keepdims=True) + EPS)
        xh = x * r
        n = xh * nw_ref[...]
        dmod_ref[g, pl.ds(3, 1), :] += jnp.sum(dh, axis=0, keepdims=True)
        dmod_ref[g, pl.ds(4, 1), :] += jnp.sum(dh * n, axis=0, keepdims=True)
        dn = dh * (1.0 + scale)
        dnw_ref[...] += jnp.sum(dn * xh, axis=0, keepdims=True)
        dxh = dn * nw_ref[...]
        dx_ref[...] = jnp.where(lat, dr_ref[...], 0.0) + r * (dxh - xh * jnp.mean(dxh * xh, axis=-1, keepdims=True))

    row = lambda i: (i, 0)
    lrow = lambda i: (jnp.minimum(i, nlat - 1), 0)
    return pl.pallas_call(
        body, name="mixin_bwd", grid=(t_all // tm,),
        in_specs=[pl.BlockSpec((tm, 512), row), pl.BlockSpec((tm, 512), lrow), pl.BlockSpec((tm, 512), lrow),
                  pl.BlockSpec((tm, d), row), pl.BlockSpec((tm, d), lrow), _whole(mod.shape), _whole(nw.shape),
                  _whole(wp.shape)],
        out_specs=[pl.BlockSpec((tm, d), row), pl.BlockSpec(mod.shape, lambda i: (0, 0, 0)),
                   pl.BlockSpec((1, d), lambda i: (0, 0))],
        out_shape=[_sds((t_all, d), F32), _sds(mod.shape, F32), _sds((1, d), F32)], compiler_params=_params(1),
    )(dp0, dpu, dpv, xs, dres, mod, nw, wp)


def _prep_fwd(proj, row0, nb, s, pos0, sk, key0, into, tabs, wq, wk, wv, kvaw, qaw, qnw, knw, tm, with_q, name):
    nblk = s // tm
    n_into = 0 if into is None else 2

    def body(p_ref, cos_ref, sa_ref, sb_ref, wq_ref, wk_ref, wv_ref, kvaw_ref, qaw_ref, qnw_ref, knw_ref, *rest):
        outs, heads_ref = rest[n_into:-1], rest[-1]
        q_ref, k_ref, v_ref = outs if with_q else (None,) + outs
        cos, sin_a, sin_b = cos_ref[...][None], sa_ref[...][None], sb_ref[...][None]

        def normed_roped(w_ref, src, extra, nw_ref, o_ref, post):
            for h in range(HEADS):
                heads_ref[h] = _dot(src, w_ref[h])
            xp = heads_ref[...] if extra is None else heads_ref[...] + extra[None]
            r = lax.rsqrt(jnp.sum(xp * xp, axis=-1, keepdims=True) * (1.0 / QK_HEAD) + EPS)
            o_ref[...] = _rope3(xp * r * (nw_ref[...] * post)[None], cos, sin_a, sin_b).astype(BF16)

        ckv = p_ref[:, 0:128]
        rkv = lax.rsqrt(jnp.mean(ckv * ckv, axis=-1, keepdims=True) + EPS)
        ckvb = (ckv * rkv * kvaw_ref[...]).astype(BF16)
        normed_roped(wk_ref, ckvb, p_ref[:, 128:256], knw_ref, k_ref, 1.0)
        for j in range(HEADS // 2):
            v_ref[j] = _dot(ckvb, wv_ref[j]).astype(BF16)
        if with_q:
            cq = p_ref[:, 256:512]
            rq = lax.rsqrt(jnp.mean(cq * cq, axis=-1, keepdims=True) + EPS)
            normed_roped(wq_ref, (cq * rq * qaw_ref[...]).astype(BF16), None, qnw_ref, q_ref, SOFTMAX_SCALE)

    tab = pl.BlockSpec((tm, HEAD_PAD), lambda i: (pos0 + i % nblk, 0))
    qspec = pl.BlockSpec((None, HEADS, tm, HEAD_PAD), lambda i: (i // nblk, 0, i % nblk, 0))
    kspec = pl.BlockSpec((None, HEADS, tm, HEAD_PAD), lambda i: (i // nblk, 0, key0 + i % nblk, 0))
    vspec = pl.BlockSpec((None, HEADS // 2, tm, HEAD_PAD), lambda i: (i // nblk, 0, key0 + i % nblk, 0))
    qshape = _sds((nb, HEADS, s, HEAD_PAD), BF16)
    kshape = _sds((nb, HEADS, sk, HEAD_PAD), BF16)
    vshape = _sds((nb, HEADS // 2, sk, HEAD_PAD), BF16)
    n_q = 1 if with_q else 0
    return pl.pallas_call(
        body, name=name, grid=(nb * nblk,),
        in_specs=[pl.BlockSpec((tm, 512), lambda i: (row0 + i, 0)), tab, tab, tab, _whole(wq.shape), _whole(wk.shape),
                  _whole(wv.shape), _whole(kvaw.shape), _whole(qaw.shape), _whole(qnw.shape), _whole(knw.shape)]
        + [pl.BlockSpec(memory_space=pl.ANY)] * n_into,
        out_specs=([qspec] if with_q else []) + [kspec, vspec],
        out_shape=([qshape] if with_q else []) + [kshape, vshape],
        scratch_shapes=[pltpu.VMEM((HEADS, tm, HEAD_PAD), F32)],
        input_output_aliases={11: n_q, 12: n_q + 1} if n_into else {}, compiler_params=_params(1),
    )(proj, *tabs, wq, wk, wv, kvaw, qaw, qnw, knw, *(into or ()))


def _prep_bwd(proj, row0, nb, s, pos0, key0, dp_rows, dp_into, tabs, wq, wk, wv, kvaw, qaw, qnw, knw, dq, dk, dv, init, tm,
              name):
    nblk = s // tm
    with_q = dq is not None
    n_init = 0 if init is None else len(init)
    n_into = 0 if dp_into is None else 1

    def body(*refs):
        p_ref, cos_ref, sa_ref, sb_ref, wq_ref, wk_ref, wv_ref, kvaw_ref, qaw_ref, qnw_ref, knw_ref = refs[:11]
        rest = list(refs[11:])
        dq_ref = rest.pop(0) if with_q else None
        dk_ref, dv_ref = rest.pop(0), rest.pop(0)
        init_refs = [rest.pop(0) for _ in range(n_init)]
        if n_into:
            rest.pop(0)
        dp_ref = rest.pop(0)
        if with_q:
            dwq_ref, dqaw_ref, dqnw_ref = rest.pop(0), rest.pop(0), rest.pop(0)
        dwk_ref, dwv_ref, dkvaw_ref, dknw_ref, heads_ref, dhb_ref = rest
        accs = [dwk_ref, dwv_ref, dkvaw_ref, dknw_ref]

        @pl.when(pl.program_id(0) == 0)
        def _():
            for k, acc in enumerate(accs):
                acc[...] = init_refs[k][...] if n_init else jnp.zeros_like(acc)
            if with_q:
                dwq_ref[...] = jnp.zeros_like(dwq_ref)
                dqaw_ref[...] = jnp.zeros_like(dqaw_ref)
                dqnw_ref[...] = jnp.zeros_like(dqnw_ref)

        cos, sin_a, sin_b = cos_ref[...][None], sa_ref[...][None], sb_ref[...][None]
        lane = lax.broadcasted_iota(jnp.int32, (tm, HEAD_PAD), 1)
        rope_lanes = (lane >= QK_NOPE) & (lane < QK_HEAD)

        def heads_bwd(w_ref, src, extra, nw_ref, d_ref, dnw_ref, dw_ref, post):
            for h in range(HEADS):
                heads_ref[h] = _dot(src, w_ref[h])
            xp = heads_ref[...] if extra is None else heads_ref[...] + extra[None]
            r = lax.rsqrt(jnp.sum(xp * xp, axis=-1, keepdims=True) * (1.0 / QK_HEAD) + EPS)
            xh = xp * r
            dn = _rope3_t(d_ref[...], cos, sin_a, sin_b)
            dnw_ref[...] += post * jnp.sum(jnp.sum(dn * xh, axis=0), axis=0, keepdims=True)
            dxh = dn * (nw_ref[...] * post)[None]
            dxp = r * (dxh - xh * (jnp.sum(dxh * xh, axis=-1, keepdims=True) * (1.0 / QK_HEAD)))
            dhb_ref[...] = dxp.astype(BF16)
            dsrc = jnp.zeros((tm, src.shape[1]), F32)
            for h in range(HEADS):
                dsrc = dsrc + _dot_nt(dhb_ref[h], w_ref[h])
                dw_ref[h] += _dot_tn(src, dhb_ref[h])
            return dsrc, jnp.sum(dxp, axis=0)

        ckv = p_ref[:, 0:128]
        rkv = lax.rsqrt(jnp.mean(ckv * ckv, axis=-1, keepdims=True) + EPS)
        ckvh = ckv * rkv
        ckvb = (ckvh * kvaw_ref[...]).astype(BF16)
        dckv, dkp_sum = heads_bwd(wk_ref, ckvb, p_ref[:, 128:256], knw_ref, dk_ref, dknw_ref, dwk_ref, 1.0)
        for j in range(HEADS // 2):
            dvb = dv_ref[j].astype(BF16)
            dckv = dckv + _dot_nt(dvb, wv_ref[j])
            dwv_ref[j] += _dot_tn(ckvb, dvb)
        dkvaw_ref[...] += jnp.sum(dckv * ckvh, axis=0, keepdims=True)
        dch = dckv * kvaw_ref[...]
        dp_ref[:, 0:128] = (rkv * (dch - ckvh * jnp.mean(dch * ckvh, axis=-1, keepdims=True))).astype(BF16)
        dp_ref[:, 128:256] = jnp.where(rope_lanes, dkp_sum, 0.0).astype(BF16)
        if with_q:
            cq = p_ref[:, 256:512]
            rq = lax.rsqrt(jnp.mean(cq * cq, axis=-1, keepdims=True) + EPS)
            cqh = cq * rq
            cqb = (cqh * qaw_ref[...]).astype(BF16)
            dcq, _ = heads_bwd(wq_ref, cqb, None, qnw_ref, dq_ref, dqnw_ref, dwq_ref, SOFTMAX_SCALE)
            dqaw_ref[...] += jnp.sum(dcq * cqh, axis=0, keepdims=True)
            dqc = dcq * qaw_ref[...]
            dp_ref[:, 256:512] = (rq * (dqc - cqh * jnp.mean(dqc * cqh, axis=-1, keepdims=True))).astype(BF16)
        else:
            dp_ref[:, 256:512] = jnp.zeros((tm, Q_LORA), BF16)

    tab = pl.BlockSpec((tm, HEAD_PAD), lambda i: (pos0 + i % nblk, 0))
    qspec = pl.BlockSpec((None, HEADS, tm, HEAD_PAD), lambda i: (i // nblk, 0, i % nblk, 0))
    kspec = pl.BlockSpec((None, HEADS, tm, HEAD_PAD), lambda i: (i // nblk, 0, key0 + i % nblk, 0))
    vspec = pl.BlockSpec((None, HEADS // 2, tm, HEAD_PAD), lambda i: (i // nblk, 0, key0 + i % nblk, 0))

    def acc_spec(shape):
        nd = len(shape)
        return pl.BlockSpec(shape, lambda i: (0,) * nd)

    acc_shapes = [(HEADS, KV_LORA, HEAD_PAD), (HEADS // 2, KV_LORA, HEAD_PAD), (1, KV_LORA), (1, HEAD_PAD)]
    q_shapes = [(HEADS, Q_LORA, HEAD_PAD), (1, Q_LORA), (1, HEAD_PAD)] if with_q else []
    out_shapes = [(dp_rows, 512)] + q_shapes + acc_shapes
    n_before = 11 + (1 if with_q else 0) + 2 + n_init
    return pl.pallas_call(
        body, name=name, grid=(nb * nblk,),
        in_specs=[pl.BlockSpec((tm, 512), lambda i: (row0 + i, 0)), tab, tab, tab, _whole(wq.shape), _whole(wk.shape),
                  _whole(wv.shape), _whole(kvaw.shape), _whole(qaw.shape), _whole(qnw.shape), _whole(knw.shape)]
        + ([qspec] if with_q else []) + [kspec, vspec] + [_whole(a.shape) for a in (init or [])]
        + [pl.BlockSpec(memory_space=pl.ANY)] * n_into,
        out_specs=[pl.BlockSpec((tm, 512), lambda i: (row0 + i, 0))] + [acc_spec(sh) for sh in q_shapes + acc_shapes],
        out_shape=[_sds(out_shapes[0], BF16)] + [_sds(sh, F32) for sh in out_shapes[1:]],
        scratch_shapes=[pltpu.VMEM((HEADS, tm, HEAD_PAD), F32), pltpu.VMEM((HEADS, tm, HEAD_PAD), BF16)],
        input_output_aliases={n_before: 0} if n_into else {}, compiler_params=_params(1),
    )(proj, *tabs, wq, wk, wv, kvaw, qaw, qnw, knw, *([dq] if with_q else []), dk, dv, *(init or []),
      *([dp_into] if n_into else []))


def _attn_fwd(q, k, v, tq, exch=None):
    nb, _, s, _ = q.shape
    sk = k.shape[2]
    nq = s // tq
    scale = QK_HEAD ** -0.5

    def body(q_ref, k_ref, v_ref, o_ref):
        lane = lax.broadcasted_iota(jnp.int32, (tq, HEAD_PAD), 1)
        vv = v_ref[...]
        outs = []
        for hh in range(2):
            sc = _dot_nt(q_ref[hh], k_ref[hh])
            p = jnp.exp(sc - jnp.max(sc, axis=-1, keepdims=True))
            l = jnp.sum(p, axis=-1, keepdims=True)
            outs.append(_dot(p.astype(BF16), vv) / l)
        o_ref[...] = jnp.where(lane < V_HEAD, outs[0], outs[1]).astype(BF16)

    (o,), got = _hosted_call(
        body, "attn_fwd", (nb, HEADS // 2, nq),
        [pl.BlockSpec((None, 2, tq, HEAD_PAD), lambda b, j, i: (b, j, i, 0)),
         pl.BlockSpec((None, 2, sk, HEAD_PAD), lambda b, j, i: (b, j, 0, 0)),
         pl.BlockSpec((None, None, sk, HEAD_PAD), lambda b, j, i: (b, j, 0, 0))],
        [pl.BlockSpec((tq, HEAD_PAD), lambda b, j, i: (b * nq + i, j))], [_sds((nb * s, MLA_W), BF16)], (q, k, v),
        exch=exch)
    return o, got


def _attn_bwd(q, k, v, do, tq, exch=None):
    nb, _, s, _ = q.shape
    sk = k.shape[2]
    nq = s // tq
    scale = QK_HEAD ** -0.5

    def body(q_ref, k_ref, v_ref, do_ref, dq_ref, dk_ref, dv_ref):
        @pl.when(pl.program_id(2) == 0)
        def _():
            dk_ref[...] = jnp.zeros_like(dk_ref)
            dv_ref[...] = jnp.zeros_like(dv_ref)

        lane = lax.broadcasted_iota(jnp.int32, (tq, HEAD_PAD), 1)
        vv = v_ref[...]
        dov = do_ref[...]
        for hh in range(2):
            mine = (lane < V_HEAD) if hh == 0 else (lane >= V_HEAD)
            doh = jnp.where(mine, dov, jnp.zeros_like(dov))
            qh = q_ref[hh]
            sc = _dot_nt(qh, k_ref[hh])
            p = jnp.exp(sc - jnp.max(sc, axis=-1, keepdims=True))
            inv = 1.0 / jnp.sum(p, axis=-1, keepdims=True)
            dp = _dot_nt(doh, vv)
            delta = jnp.sum(p * dp, axis=-1, keepdims=True) * inv
            u = (p * (dp - delta)).astype(BF16)
            dq_ref[hh] = _dot(u, k_ref[hh]) * inv
            dk_ref[hh] += _dot_tn(u, (qh.astype(F32) * inv).astype(BF16))
            dv_ref[...] += _dot_tn(p.astype(BF16), (doh.astype(F32) * inv).astype(BF16))

    return _hosted_call(
        body, "attn_bwd", (nb, HEADS // 2, nq),
        [pl.BlockSpec((None, 2, tq, HEAD_PAD), lambda b, j, i: (b, j, i, 0)),
         pl.BlockSpec((None, 2, sk, HEAD_PAD), lambda b, j, i: (b, j, 0, 0)),
         pl.BlockSpec((None, None, sk, HEAD_PAD), lambda b, j, i: (b, j, 0, 0)),
         pl.BlockSpec((tq, HEAD_PAD), lambda b, j, i: (b * nq + i, j))],
        [pl.BlockSpec((None, 2, tq, HEAD_PAD), lambda b, j, i: (b, j, i, 0)),
         pl.BlockSpec((None, 2, sk, HEAD_PAD), lambda b, j, i: (b, j, 0, 0)),
         pl.BlockSpec((None, None, sk, HEAD_PAD), lambda b, j, i: (b, j, 0, 0))],
        [_sds(q.shape, F32), _sds(k.shape, F32), _sds(v.shape, F32)], (q, k, v, do), exch=exch)


def _group_masks(rows):
    lane = lax.broadcasted_iota(jnp.int32, (rows, GMLP_W), 1)
    return [(lane >= g * GROUP_DIM) & (lane < (g + 1) * GROUP_DIM) for g in range(GROUPS)]


def _gmlp_fwd(proj, t, wcat, bias, vnw, ones, tm):
    def body(u_ref, v_ref, wcat_ref, bias_ref, vnw_ref, ones_ref, o_ref):
        masks = _group_masks(CHUNK)
        gv = _gelu(v_ref[...])
        rv = lax.rsqrt(_group_sum(gv * gv, ones_ref) * (1.0 / GROUP_DIM) + EPS)
        vnb = (gv * rv * vnw_ref[...]).astype(BF16)
        for c in range(tm // CHUNK):
            rows = slice(c * CHUNK, (c + 1) * CHUNK)
            vc = vnb[rows]
            stack = jnp.concatenate([jnp.where(m, vc, jnp.zeros_like(vc)) for m in masks], axis=0)
            sp = _dot(wcat_ref[...], stack) + bias_ref[...]
            o_ref[rows, :] = (_gelu(u_ref[rows, :]) * sp).astype(BF16)

    return pl.pallas_call(
        body, name="gmlp_fwd", grid=(t // tm,),
        in_specs=[pl.BlockSpec((tm, GMLP_W), lambda i: (i, 1)), pl.BlockSpec((tm, GMLP_W), lambda i: (i, 2)),
                  _whole(wcat.shape), _whole(bias.shape), _whole(vnw.shape), _whole(ones.shape)],
        out_specs=pl.BlockSpec((tm, GMLP_W), lambda i: (i, 0)),
        out_shape=_sds((t, GMLP_W), BF16), compiler_params=_params(1),
    )(proj, proj, wcat, bias, vnw, ones)


def _gmlp_bwd(proj, dsg, wcat, wcat_t, bias, vnw, ones, tm):
    t = dsg.shape[0]

    def body(u_ref, v_ref, dsg_ref, wcat_ref, wcatt_ref, bias_ref, vnw_ref, ones_ref,
             du_ref, dv_ref, dws_ref, dbs_ref, dvnw_ref):
        @pl.when(pl.program_id(0) == 0)
        def _():
            dws_ref[...] = jnp.zeros_like(dws_ref)
            dbs_ref[...] = jnp.zeros_like(dbs_ref)
            dvnw_ref[...] = jnp.zeros_like(dvnw_ref)

        masks = _group_masks(CHUNK)
        v = v_ref[...]
        gv = _gelu(v)
        rv = lax.rsqrt(_group_sum(gv * gv, ones_ref) * (1.0 / GROUP_DIM) + EPS)
        xh = gv * rv
        vnb = (xh * vnw_ref[...]).astype(BF16)
        dvn_parts = []
        for c in range(tm // CHUNK):
            rows = slice(c * CHUNK, (c + 1) * CHUNK)
            vc = vnb[rows]
            stack = jnp.concatenate([jnp.where(m, vc, jnp.zeros_like(vc)) for m in masks], axis=0)
            sp = _dot(wcat_ref[...], stack) + bias_ref[...]
            u = u_ref[rows, :]
            dsg_c = dsg_ref[rows, :]
            du_ref[rows, :] = (dsg_c * sp * _gelu_grad(u)).astype(BF16)
            ds = dsg_c * _gelu(u)
            dstack = jnp.concatenate([jnp.where(m, ds, 0.0) for m in masks], axis=0)
            dbs_ref[...] += jnp.broadcast_to(jnp.sum(dstack, axis=-1, keepdims=True), dbs_ref.shape)
            dstb = dstack.astype(BF16)
            dvn_parts.append(_dot(wcatt_ref[...], dstb))
            dws_ref[...] += _dot_nt(dstb, vc)
        dvn = jnp.concatenate(dvn_parts, axis=0) if len(dvn_parts) > 1 else dvn_parts[0]
        dvnw_ref[...] += jnp.sum(dvn * xh, axis=0, keepdims=True)
        dxh = dvn * vnw_ref[...]
        gm = _group_sum(dxh * xh, ones_ref) * (1.0 / GROUP_DIM)
        dv_ref[...] = (rv * (dxh - xh * gm) * _gelu_grad(v)).astype(BF16)

    row = pl.BlockSpec((tm, GMLP_W), lambda i: (i, 0))
    return pl.pallas_call(
        body, name="gmlp_bwd", grid=(t // tm,),
        in_specs=[pl.BlockSpec((tm, GMLP_W), lambda i: (i, 1)), pl.BlockSpec((tm, GMLP_W), lambda i: (i, 2)), row,
                  _whole(wcat.shape), _whole(wcat_t.shape), _whole(bias.shape), _whole(vnw.shape), _whole(ones.shape)],
        out_specs=[row, row, pl.BlockSpec((GROUPS * CHUNK, CHUNK), lambda i: (0, 0)),
                   pl.BlockSpec((GROUPS * CHUNK, CHUNK), lambda i: (0, 0)), pl.BlockSpec((1, GMLP_W), lambda i: (0, 0))],
        out_shape=[_sds((t, GMLP_W), BF16), _sds((t, GMLP_W), BF16), _sds((GROUPS * CHUNK, CHUNK), F32),
                   _sds((GROUPS * CHUNK, CHUNK), F32), _sds((1, GMLP_W), F32)],
        compiler_params=_params(1),
    )(proj, proj, dsg, wcat, wcat_t, bias, vnw, ones)


def _mixout_fwd(o, sg, xs, mod, wout, s, tm):
    t = o.shape[0]
    d = xs.shape[1]

    def body(o_ref, sg_ref, x_ref, mod_ref, w_ref, x2_ref, mix_ref):
        g = (pl.program_id(0) * tm) // s
        gate = mod_ref[g, pl.ds(5, 1), :]
        mix = _dot(o_ref[...], w_ref[0:MLA_W, :]) + _dot(sg_ref[...], w_ref[MLA_W:MLA_W + GMLP_W, :])
        x2_ref[...] = x_ref[...] + gate * mix
        mix_ref[...] = mix.astype(BF16)

    row = lambda i: (i, 0)
    return pl.pallas_call(
        body, name="mixout_fwd", grid=(t // tm,),
        in_specs=[pl.BlockSpec((tm, MLA_W), row), pl.BlockSpec((tm, GMLP_W), row), pl.BlockSpec((tm, d), row),
                  _whole(mod.shape), _whole(wout.shape)],
        out_specs=[pl.BlockSpec((tm, d), row), pl.BlockSpec((tm, d), row)],
        out_shape=[_sds((t, d), F32), _sds((t, d), BF16)], compiler_params=_params(1),
    )(o, sg, xs, mod, wout)


def _mixout_bwd(dx2, mix, mod, wout, s, tm):
    t, d = dx2.shape

    def body(dx_ref, mix_ref, mod_ref, w_ref, dmix_ref, do_ref, dsg_ref, dmod_ref):
        i = pl.program_id(0)

        @pl.when(i == 0)
        def _():
            dmod_ref[...] = jnp.zeros_like(dmod_ref)

        g = (i * tm) // s
        gate = mod_ref[g, pl.ds(5, 1), :]
        dx = dx_ref[...]
        dmod_ref[g, pl.ds(5, 1), :] += jnp.sum(dx * mix_ref[...].astype(F32), axis=0, keepdims=True)
        dmb = (gate * dx).astype(BF16)
        dmix_ref[...] = dmb
        do_ref[...] = _dot_nt(dmb, w_ref[0:MLA_W, :]).astype(BF16)
        dsg_ref[...] = _dot_nt(dmb, w_ref[MLA_W:MLA_W + GMLP_W, :])

    row = lambda i: (i, 0)
    return pl.pallas_call(
        body, name="mixout_bwd", grid=(t // tm,),
        in_specs=[pl.BlockSpec((tm, d), row), pl.BlockSpec((tm, d), row), _whole(mod.shape), _whole(wout.shape)],
        out_specs=[pl.BlockSpec((tm, d), row), pl.BlockSpec((tm, MLA_W), row), pl.BlockSpec((tm, GMLP_W), row),
                   pl.BlockSpec(mod.shape, lambda i: (0, 0, 0))],
        out_shape=[_sds((t, d), BF16), _sds((t, MLA_W), BF16), _sds((t, GMLP_W), F32), _sds(mod.shape, F32)],
        compiler_params=_params(1),
    )(dx2, mix, mod, wout)


def _gather_first(shards):
    n = len(shards)

    def body(*refs):
        srcs, outs = refs[:n], refs[n:2 * n]
        ici_send, ici_recv, d2d_send, d2d_recv, local_sems = refs[2 * n:]
        x, y, c = lax.axis_index("x"), lax.axis_index("y"), lax.axis_index("c")
        me = 2 * x + y
        chips = _other_chips(x, y)

        def half(w, which):
            hr = shards[w].shape[0] // 2
            return pl.ds(pl.multiple_of(which * hr, 16), hr)

        def over_ici(w, k, arriving):
            px, py = chips[k]
            slot = 2 * px + py if arriving else me
            return pltpu.make_async_remote_copy(
                src_ref=srcs[w].at[half(w, c)], dst_ref=outs[w].at[slot, half(w, c)], send_sem=ici_send.at[3 * w + k],
                recv_sem=ici_recv.at[3 * w + k], device_id=(px, py, c), device_id_type=pl.DeviceIdType.MESH)

        def to_sibling(w, k, arriving):
            px, py = chips[k]
            rows = half(w, 1 - c if arriving else c)
            return pltpu.make_async_remote_copy(
                src_ref=outs[w].at[2 * px + py, rows], dst_ref=outs[w].at[2 * px + py, rows],
                send_sem=d2d_send.at[3 * w + k], recv_sem=d2d_recv.at[3 * w + k], device_id=(x, y, 1 - c),
                device_id_type=pl.DeviceIdType.MESH)

        local = [pltpu.make_async_copy(srcs[w], outs[w].at[me], local_sems.at[w]) for w in range(n)]
        for cp in local:
            cp.start()
        pairs = [(w, k) for w in range(n) for k in range(3)]
        for w, k in pairs:
            over_ici(w, k, False).start()
        for w, k in pairs:
            over_ici(w, k, True).wait_recv()
            to_sibling(w, k, False).start()
        for w, k in pairs:
            to_sibling(w, k, True).wait_recv()
        for w, k in pairs:
            over_ici(w, k, False).wait_send()
            to_sibling(w, k, False).wait_send()
        for cp in local:
            cp.wait()

    any_spec = pl.BlockSpec(memory_space=pl.ANY)
    sems = pltpu.SemaphoreType.DMA((3 * n,))
    return pl.pallas_call(
        body, name="gather_first", in_specs=[any_spec] * n, out_specs=[any_spec] * n,
        out_shape=_exch_shapes("gather", shards),
        scratch_shapes=[sems, sems, sems, sems, pltpu.SemaphoreType.DMA((n,))],
    )(*shards)


def _swap_cores(parts, name):
    n = len(parts)

    def body(*refs):
        srcs, outs, send_sems, recv_sems = refs[:n], refs[n:2 * n], refs[2 * n], refs[2 * n + 1]
        x, y, c = lax.axis_index("x"), lax.axis_index("y"), lax.axis_index("c")
        copies = [pltpu.make_async_remote_copy(
            src_ref=srcs[w], dst_ref=outs[w], send_sem=send_sems.at[w], recv_sem=recv_sems.at[w],
            device_id=(x, y, 1 - c), device_id_type=pl.DeviceIdType.MESH) for w in range(n)]
        for cp in copies:
            cp.start()
        for cp in copies:
            cp.wait()

    any_spec = pl.BlockSpec(memory_space=pl.ANY)
    return pl.pallas_call(
        body, name=name, in_specs=[any_spec] * n, out_specs=[any_spec] * n,
        out_shape=[_sds(p.shape, p.dtype) for p in parts],
        scratch_shapes=[pltpu.SemaphoreType.DMA((n,)), pltpu.SemaphoreType.DMA((n,))],
    )(*parts)


def _row_tile(r, c, mult):
    return _div_tile(r, max(mult, (1 << 16) // c), mult)


def _sum_slots(recv, name):
    _, r, c = recv.shape
    tr = _row_tile(r, c, 16)

    def body(r_ref, o_ref):
        f = lambda k: r_ref[k].astype(F32)
        o_ref[...] = ((f(0) + f(1)) + f(2)) + f(3)

    return pl.pallas_call(
        body, name=name, grid=(r // tr,),
        in_specs=[pl.BlockSpec((N_CHIPS, tr, c), lambda i: (0, i, 0))],
        out_specs=pl.BlockSpec((tr, c), lambda i: (i, 0)),
        out_shape=_sds((r, c), F32), compiler_params=_params(1),
    )(recv)


def _adamw(parts, w, m, v, name, exch=None):
    r, wd = w.shape
    tr = _row_tile(r, wd, 8)
    c1 = 1.0 / (1.0 - ADAM_B1 ** ADAM_STEP)
    c2 = 1.0 / (1.0 - ADAM_B2 ** ADAM_STEP)
    n_p = len(parts)

    def body(*refs):
        p_refs = refs[:n_p]
        w_ref, m_ref, v_ref, g_ref, d_ref, nm_ref, nv_ref = refs[n_p:]
        g = p_refs[0][...]
        for p_ref in p_refs[1:]:
            g = g + p_ref[...]
        nm = ADAM_B1 * m_ref[...] + (1.0 - ADAM_B1) * g
        nv = ADAM_B2 * v_ref[...] + (1.0 - ADAM_B2) * (g * g)
        g_ref[...] = g
        nm_ref[...] = nm
        nv_ref[...] = nv
        d_ref[...] = -ADAM_LR * ((nm * c1) / (jnp.sqrt(nv * c2) + ADAM_EPS) + ADAM_WD * w_ref[...])

    spec = pl.BlockSpec((tr, wd), lambda i: (i, 0))
    return _hosted_call(body, name, (r // tr,), [spec] * (n_p + 3), [spec] * 4, [_sds((r, wd), F32)] * 4,
                        (*parts, w, m, v), exch=exch)


def _all_peers(x, y, c):
    flips = [(dx, dy, dc) for dx in (0, 1) for dy in (0, 1) for dc in (0, 1)][1:]
    return [(1 - x if dx else x, 1 - y if dy else y, 1 - c if dc else c) for dx, dy, dc in flips]


def _gather_devices(block):
    def body(src_ref, out_ref, send_sems, recv_sems, local_sem):
        x, y, c = lax.axis_index("x"), lax.axis_index("y"), lax.axis_index("c")
        me = 4 * x + 2 * y + c
        mine = pltpu.make_async_copy(src_ref, out_ref.at[me], local_sem)
        mine.start()

        def copy(k, peer, slot):
            return pltpu.make_async_remote_copy(
                src_ref=src_ref, dst_ref=out_ref.at[slot], send_sem=send_sems.at[k], recv_sem=recv_sems.at[k],
                device_id=peer, device_id_type=pl.DeviceIdType.MESH)

        peers = _all_peers(x, y, c)
        for k, peer in enumerate(peers):
            copy(k, peer, me).start()
        for k, (px, py, pc) in enumerate(peers):
            copy(k, (px, py, pc), 4 * px + 2 * py + pc).wait_recv()
        for k, peer in enumerate(peers):
            copy(k, peer, me).wait_send()
        mine.wait()

    return pl.pallas_call(
        body, name="gather_devices", in_specs=[pl.BlockSpec(memory_space=pl.ANY)],
        out_specs=pl.BlockSpec(memory_space=pl.ANY), out_shape=_sds((8,) + block.shape, block.dtype),
        scratch_shapes=[pltpu.SemaphoreType.DMA((7,)), pltpu.SemaphoreType.DMA((7,)), pltpu.SemaphoreType.DMA(())],
    )(block)


def _ada_fwd_tp(cc_all, w, b):
    n = w.shape[1]

    def body(cc_ref, w_ref, b_ref, out_ref, part_ref, send_sems, recv_sems):
        x, y, c = lax.axis_index("x"), lax.axis_index("y"), lax.axis_index("c")
        me = 2 * x + y
        cv = cc_ref[...]
        part_ref[...] = _dot((cv * _sigmoid(cv)).astype(BF16), w_ref[...]) + b_ref[...]

        def rows_of(px, py):
            return part_ref.at[pl.ds(pl.multiple_of((4 * px + 2 * py + c) * MOD_ROWS, MOD_ROWS), MOD_ROWS)]

        def copy(k, px, py, slot):
            return pltpu.make_async_remote_copy(
                src_ref=rows_of(px, py), dst_ref=out_ref.at[slot], send_sem=send_sems.at[k], recv_sem=recv_sems.at[k],
                device_id=(px, py, c), device_id_type=pl.DeviceIdType.MESH)

        chips = _other_chips(x, y)
        for k, (px, py) in enumerate(chips):
            copy(k, px, py, me).start()
        out_ref[me] = rows_of(x, y)[...]
        for k, (px, py) in enumerate(chips):
            copy(k, px, py, 2 * px + py).wait_recv()
        for k, (px, py) in enumerate(chips):
            copy(k, px, py, me).wait_send()

    vmem = pl.BlockSpec(memory_space=pltpu.VMEM)
    return pl.pallas_call(
        body, name="ada_fwd_tp", in_specs=[vmem, vmem, vmem], out_specs=vmem,
        out_shape=_sds((N_CHIPS, MOD_ROWS, n), F32),
        scratch_shapes=[pltpu.VMEM((8 * MOD_ROWS, n), F32), pltpu.SemaphoreType.DMA((3,)), pltpu.SemaphoreType.DMA((3,))],
        compiler_params=pltpu.CompilerParams(vmem_limit_bytes=V7X_VMEM_LIMIT),
    )(cc_all, w, b)


def _ada_bwd_tp(cc_all, dmods, w, ctx_row):
    d, n = w.shape

    def body(cc_ref, m0, m1, m2, m3, w_ref, dw_ref, db_ref, dctx_ref, stage_ref, all_ref, send_sems, recv_sems):
        x, y, c = lax.axis_index("x"), lax.axis_index("y"), lax.axis_index("c")
        me = 4 * x + 2 * y + c
        dsum = m0[...] + m1[...] + m2[...] + m3[...]
        db_ref[...] = jnp.sum(dsum, axis=0, keepdims=True)
        for j in range(N_CHIPS):
            stage_ref[j] = dsum[:, j * n:(j + 1) * n]

        def copy(k, peer, slot):
            px, py, _ = peer
            return pltpu.make_async_remote_copy(
                src_ref=stage_ref.at[2 * px + py], dst_ref=all_ref.at[slot], send_sem=send_sems.at[k],
                recv_sem=recv_sems.at[k], device_id=peer, device_id_type=pl.DeviceIdType.MESH)

        peers = _all_peers(x, y, c)
        for k, peer in enumerate(peers):
            copy(k, peer, me).start()
        all_ref[me] = stage_ref[2 * x + y]
        for k, (px, py, pc) in enumerate(peers):
            copy(k, (px, py, pc), 4 * px + 2 * py + pc).wait_recv()
        for k, peer in enumerate(peers):
            copy(k, peer, me).wait_send()
        cv = cc_ref[...]
        sig = _sigmoid(cv)
        dmb = all_ref[...].reshape(8 * MOD_ROWS, n).astype(BF16)
        dw_ref[...] = _dot_tn((cv * sig).astype(BF16), dmb)
        dsc = _dot_nt(dmb, w_ref[...])
        dctx = dsc[ctx_row:ctx_row + 1, :]
        for dev in range(1, 8):
            dctx = dctx + dsc[dev * MOD_ROWS + ctx_row:dev * MOD_ROWS + ctx_row + 1, :]
        cx = cv[ctx_row:ctx_row + 1, :]
        sx = sig[ctx_row:ctx_row + 1, :]
        dctx_ref[...] = dctx * (sx * (1.0 + cx * (1.0 - sx))) * jnp.where(c == 0, 1.0, 0.0)

    vmem = pl.BlockSpec(memory_space=pltpu.VMEM)
    return pl.pallas_call(
        body, name="ada_bwd_tp", in_specs=[vmem] * 6, out_specs=[vmem] * 3,
        out_shape=[_sds((d, n), F32), _sds((1, N_MOD * d), F32), _sds((1, d), F32)],
        scratch_shapes=[pltpu.VMEM((N_CHIPS, MOD_ROWS, n), F32), pltpu.VMEM((8, MOD_ROWS, n), F32),
                        pltpu.SemaphoreType.DMA((7,)), pltpu.SemaphoreType.DMA((7,))],
        compiler_params=pltpu.CompilerParams(vmem_limit_bytes=V7X_VMEM_LIMIT),
    )(cc_all, *dmods, w)


def _rope_tables(s, ctx):
    pos = np.arange(s, dtype=np.float32)
    inv = (np.float32(ROPE_BASE) ** (-np.arange(0, QK_ROPE // 2, 2, dtype=np.float32) / np.float32(QK_ROPE // 2)))
    ang_r = np.floor(pos / GRID_W)[:, None] * inv
    ang_c = (pos - GRID_W * np.floor(pos / GRID_W))[:, None] * inv
    ang = np.concatenate([ang_r, ang_r, ang_c, ang_c], axis=-1).astype(np.float32)
    cos, sin = np.cos(ang), np.sin(ang)
    half_b = (np.arange(QK_ROPE) // 8) % 2 == 1
    sin_a = np.where(half_b, sin, 0.0)
    sin_b = np.where(half_b, 0.0, -sin)

    def place(tab, fill):
        full = np.full((s + ctx, HEAD_PAD), fill, np.float32)
        full[:s, QK_NOPE:QK_HEAD] = tab
        return jnp.asarray(full)

    return place(cos, 1.0), place(sin_a, 0.0), place(sin_b, 0.0)


def _pad_last(a, n):
    return jnp.pad(a, [(0, 0)] * (a.ndim - 1) + [(0, n - a.shape[-1])])


def _flat_rows(parts, rows, width):
    flat = jnp.concatenate([p.reshape(-1) for p in parts])
    return jnp.pad(flat, (0, rows * width - flat.shape[0])).reshape(rows, width)


def kernel(x, c, ctx, c_ctx, w_ada, b_ada, norm1_w, ffn1_w1, ffn1_w3, ffn1_w2, norm2_w, w_in, q_a_norm_w, w_uq, kv_a_norm_w, w_ukv, q_norm_w, k_norm_w, v_norm_w, w_s, b_s, w_out, norm3_w, ffn2_w1, ffn2_w3, ffn2_w2, loss_target, m_c_ctx, m_w_ada, m_b_ada, m_norm1_w, m_ffn1_w1, m_ffn1_w3, m_ffn1_w2, m_norm2_w, m_w_in, m_q_a_norm_w, m_w_uq, m_kv_a_norm_w, m_w_ukv, m_q_norm_w, m_k_norm_w, m_v_norm_w, m_w_s, m_b_s, m_w_out, m_norm3_w, m_ffn2_w1, m_ffn2_w3, m_ffn2_w2, v_c_ctx, v_w_ada, v_b_ada, v_norm1_w, v_ffn1_w1, v_ffn1_w3, v_ffn1_w2, v_norm2_w, v_w_in, v_q_a_norm_w, v_w_uq, v_kv_a_norm_w, v_w_ukv, v_q_norm_w, v_k_norm_w, v_v_norm_w, v_w_s, v_b_s, v_w_out, v_norm3_w, v_ffn2_w1, v_ffn2_w3, v_ffn2_w2):
    wts = dict(c_ctx=c_ctx, w_ada=w_ada, b_ada=b_ada, norm1_w=norm1_w, ffn1_w1=ffn1_w1, ffn1_w3=ffn1_w3, ffn1_w2=ffn1_w2,
               norm2_w=norm2_w, w_in=w_in, q_a_norm_w=q_a_norm_w, w_uq=w_uq, kv_a_norm_w=kv_a_norm_w, w_ukv=w_ukv,
               q_norm_w=q_norm_w, k_norm_w=k_norm_w, v_norm_w=v_norm_w, w_s=w_s, b_s=b_s, w_out=w_out, norm3_w=norm3_w,
               ffn2_w1=ffn2_w1, ffn2_w3=ffn2_w3, ffn2_w2=ffn2_w2)
    moms = dict(c_ctx=m_c_ctx, w_ada=m_w_ada, b_ada=m_b_ada, norm1_w=m_norm1_w, ffn1_w1=m_ffn1_w1, ffn1_w3=m_ffn1_w3,
                ffn1_w2=m_ffn1_w2, norm2_w=m_norm2_w, w_in=m_w_in, q_a_norm_w=m_q_a_norm_w, w_uq=m_w_uq,
                kv_a_norm_w=m_kv_a_norm_w, w_ukv=m_w_ukv, q_norm_w=m_q_norm_w, k_norm_w=m_k_norm_w, v_norm_w=m_v_norm_w,
                w_s=m_w_s, b_s=m_b_s, w_out=m_w_out, norm3_w=m_norm3_w, ffn2_w1=m_ffn2_w1, ffn2_w3=m_ffn2_w3,
                ffn2_w2=m_ffn2_w2)
    vars_ = dict(c_ctx=v_c_ctx, w_ada=v_w_ada, b_ada=v_b_ada, norm1_w=v_norm1_w, ffn1_w1=v_ffn1_w1, ffn1_w3=v_ffn1_w3,
                 ffn1_w2=v_ffn1_w2, norm2_w=v_norm2_w, w_in=v_w_in, q_a_norm_w=v_q_a_norm_w, w_uq=v_w_uq,
                 kv_a_norm_w=v_kv_a_norm_w, w_ukv=v_w_ukv, q_norm_w=v_q_norm_w, k_norm_w=v_k_norm_w, v_norm_w=v_v_norm_w,
                 w_s=v_w_s, b_s=v_b_s, w_out=v_w_out, norm3_w=v_norm3_w, ffn2_w1=v_ffn2_w1, ffn2_w3=v_ffn2_w3,
                 ffn2_w2=v_ffn2_w2)

    nb, s, d = x.shape
    nctx = ctx.shape[1]
    t, tc = nb * s, nb * nctx
    t_all = t + tc
    sk = s + nctx
    assert nb + 1 <= MOD_ROWS and d % LANES == 0
    tm = _token_tile(s, nctx)

    shard = {n: wts[n][0].astype(BF16) for n in SHARDED}
    full = {}

    def unshard(names, blocks):
        for n, g4 in zip(names, blocks):
            _, r_, c_ = g4.shape
            full[n] = g4.reshape(N_CHIPS * r_, c_) if n in ROW_SHARDED else g4.transpose(1, 0, 2).reshape(r_, N_CHIPS * c_)

    def chip_major(n, g_):
        if n in ROW_SHARDED:
            return g_.reshape(N_CHIPS, g_.shape[0] // N_CHIPS, g_.shape[1]).astype(BF16)
        r_, cols = g_.shape
        return g_.reshape(r_, N_CHIPS, cols // N_CHIPS).transpose(1, 0, 2).astype(BF16)

    unshard(FIRST_WEIGHTS, _gather_first([shard[n] for n in FIRST_WEIGHTS]))
    wsb = w_s[0].astype(BF16)
    wcat = wsb.transpose(1, 0, 2).reshape(CHUNK, GROUPS * CHUNK)
    wcat_t = wsb.transpose(2, 0, 1).reshape(CHUNK, GROUPS * CHUNK)
    bias = jnp.repeat(b_s[0].T, GROUP_DIM, axis=1)
    vnw = v_norm_w.reshape(1, GMLP_W)
    lane = jnp.arange(GMLP_W)
    ones = (lane[:, None] // GROUP_DIM == lane[None, :] // GROUP_DIM).astype(BF16)
    qnw = _pad_last(q_norm_w, HEAD_PAD)
    knw = _pad_last(k_norm_w, HEAD_PAD)
    tabs = _rope_tables(s, nctx)

    cc = jnp.concatenate([c, c_ctx[None, :], jnp.zeros((MOD_ROWS - nb - 1, d), F32)], axis=0)
    cc_all = _gather_devices(cc).reshape(8 * MOD_ROWS, d)
    n_ada = shard["w_ada"].shape[1]
    assert n_ada % LANES == 0
    my_chip = 2 * lax.axis_index("x") + lax.axis_index("y")
    b_cols = lax.dynamic_slice_in_dim(b_ada, my_chip * n_ada, n_ada, axis=1)
    mod = _ada_fwd_tp(cc_all, shard["w_ada"], b_cols).transpose(1, 0, 2).reshape(MOD_ROWS, N_MOD, d)
    x_lat, x_ctx = x.reshape(t, d), ctx.reshape(tc, d)
    (xs1, a1, b1, y1), got = _ffn_fwd(x_lat, x_ctx, mod, norm1_w, full["ffn1_w1"], full["ffn1_w3"], full["ffn1_w2"], 0, s,
                                      nb, tm, "ffn1_fwd", exch=("gather", [shard[n] for n in MIX_WEIGHTS]))
    unshard(MIX_WEIGHTS, got)
    wi = full["w_in"]
    wp = jnp.concatenate([wi[:, 0:KV_LORA], jnp.zeros((d, QK_NOPE), BF16), wi[:, KV_LORA:KV_LORA + QK_ROPE],
                          jnp.zeros((d, HEAD_PAD - QK_HEAD), BF16), wi[:, KV_LORA + QK_ROPE:]], axis=1)
    wq = _pad_last(full["w_uq"].reshape(Q_LORA, HEADS, QK_HEAD).transpose(1, 0, 2), HEAD_PAD)
    wkv = full["w_ukv"].reshape(KV_LORA, HEADS, QK_NOPE + V_HEAD)
    wk = _pad_last(wkv[:, :, :QK_NOPE].transpose(1, 0, 2), HEAD_PAD)
    wv = wkv[:, :, QK_NOPE:].reshape(KV_LORA, HEADS // 2, 2 * V_HEAD).transpose(1, 0, 2)
    h2, proj = _mixin_fwd(xs1, mod, norm2_w, wp, s, nb, tm)
    prep_w = (wq, wk, wv, kv_a_norm_w, q_a_norm_w, qnw, knw)
    q, k_all, v_all = _prep_fwd(proj, 0, nb, s, 0, sk, 0, None, tabs, *prep_w, tm, True, "prep_fwd")
    k_all, v_all = _prep_fwd(proj, t // tm, nb, nctx, s // tm, sk, s // tm, (k_all, v_all), tabs, *prep_w, tm, False,
                             "prep_ctx_fwd")
    o, got = _attn_fwd(q, k_all, v_all, tm, exch=("gather", [shard[n] for n in LAST_WEIGHTS]))
    unshard(LAST_WEIGHTS, got)
    sg = _gmlp_fwd(proj, t, wcat, bias, vnw, ones, tm)
    x2, mix = _mixout_fwd(o, sg, xs1, mod, full["w_out"], s, tm)
    (dy, a2, b2, y2, loss_part), _ = _ffn_fwd(x2, None, mod, norm3_w, full["ffn2_w1"], full["ffn2_w3"], full["ffn2_w2"], 6,
                                              s, nb, tm, "ffn2_fwd", target=loss_target.reshape(t, d))
    loss = lax.psum(loss_part[0, 0], ("x", "y", "c"))

    grads, cm, recv = {}, {}, {}

    def scatter_of(names):
        return ("scatter", [cm[n] for n in names])

    (dx2, h3, g2, da2, db2, dyb2, dmod_c, grads["norm3_w"]), _ = _ffn_bwd(
        dy, x2, None, a2, b2, y2, mod, norm3_w, full["ffn2_w1"], full["ffn2_w3"], full["ffn2_w2"], 6, s, nb, tm,
        "ffn2_bwd")
    cm["ffn2_w1"] = chip_major("ffn2_w1", _mm_tn(h3, da2, t, "ffn2_dw1"))
    cm["ffn2_w3"] = chip_major("ffn2_w3", _mm_tn(h3, db2, t, "ffn2_dw3"))
    cm["ffn2_w2"] = chip_major("ffn2_w2", _mm_tn(g2, dyb2, t, "ffn2_dw2"))
    dmix, do, dsg, dmod_b = _mixout_bwd(dx2, mix, mod, full["w_out"], s, tm)
    cm["w_out"] = chip_major("w_out", jnp.concatenate([_mm_tn(o, dmix, t, "wout_dw_attn"),
                                                       _mm_tn(sg, dmix, t, "wout_dw_gmlp")], axis=0))
    dpu, dpv, dws, dbs, dvnw = _gmlp_bwd(proj, dsg, wcat, wcat_t, bias, vnw, ones, tm)
    group = LAST_WEIGHTS + ("w_out",)
    (dq, dk, dv), got = _attn_bwd(q, k_all, v_all, do, tm, exch=scatter_of(group))
    recv.update(zip(group, got))
    dp0, dwk_c, dwv_c, dkvaw_c, dknw_c = _prep_bwd(
        proj, t // tm, nb, nctx, s // tm, s // tm, t_all, None, tabs, *prep_w, None, dk, dv, None, tm, "prep_ctx_bwd")
    dp0, dwq, dqaw, dqnw, dwk, dwv, dkvaw, dknw = _prep_bwd(
        proj, 0, nb, s, 0, 0, t_all, dp0, tabs, *prep_w, dq, dk, dv, [dwk_c, dwv_c, dkvaw_c, dknw_c], tm, "prep_bwd")
    dxs1, dmod_a, grads["norm2_w"] = _mixin_bwd(dp0, dpu, dpv, xs1, dx2, mod, norm2_w, wp, s, nb, tm)
    dwp = jnp.concatenate([_mm_tn(h2, dp0, t_all, "win_dw_kvq"), _mm_tn(h2, dpu, t, "win_dw_u"),
                           _mm_tn(h2, dpv, t, "win_dw_v")], axis=1)
    cm["w_in"] = chip_major("w_in", jnp.concatenate(
        [dwp[:, 0:KV_LORA], dwp[:, KV_LORA + QK_NOPE:KV_LORA + QK_HEAD], dwp[:, 256:]], axis=1))
    cm["w_uq"] = chip_major("w_uq", dwq[:, :, :QK_HEAD].transpose(1, 0, 2).reshape(Q_LORA, HEADS * QK_HEAD))
    cm["w_ukv"] = chip_major("w_ukv", jnp.concatenate(
        [dwk[:, :, :QK_NOPE].transpose(1, 0, 2),
         dwv.transpose(1, 0, 2).reshape(KV_LORA, HEADS, V_HEAD)], axis=2).reshape(KV_LORA, HEADS * (QK_NOPE + V_HEAD)))
    group = ("w_in", "w_uq", "w_ukv")
    (dx_lat, h1, g1, da1, db1, dyb1, dmod_0, grads["norm1_w"]), got = _ffn_bwd(
        dxs1, x_lat, x_ctx, a1, b1, y1, mod, norm1_w, full["ffn1_w1"], full["ffn1_w3"], full["ffn1_w2"], 0, s, nb, tm,
        "ffn1_bwd", exch=scatter_of(group))
    recv.update(zip(group, got))
    cm["ffn1_w2"] = chip_major("ffn1_w2", _mm_tn(g1, dyb1, t_all, "ffn1_dw2"))
    dw1, got = _mm_tn(h1, da1, t_all, "ffn1_dw1", exch=scatter_of(("ffn1_w2",)))
    recv["ffn1_w2"] = got[0]
    cm["ffn1_w1"] = chip_major("ffn1_w1", dw1)
    dw3, got = _mm_tn(h1, db1, t_all, "ffn1_dw3", exch=scatter_of(("ffn1_w1",)))
    recv["ffn1_w1"] = got[0]
    cm["ffn1_w3"] = chip_major("ffn1_w3", dw3)
    dmods = [m_.reshape(MOD_ROWS, N_MOD * d) for m_ in (dmod_0, dmod_a, dmod_b, dmod_c)]
    dw_ada, grads["b_ada"], dctx = _ada_bwd_tp(cc_all, dmods, shard["w_ada"], nb)
    grads["c_ctx"] = dctx[0]
    grads["q_a_norm_w"], grads["kv_a_norm_w"] = dqaw, dkvaw
    grads["q_norm_w"], grads["k_norm_w"] = dqnw[:, :QK_HEAD], dknw[:, :QK_HEAD]
    grads["v_norm_w"], grads["w_s"], grads["b_s"] = dvnw, dws, dbs[:, 0]
    grad_x = dx_lat.reshape(nb, s, d)
    rows_s = _round_up(-(-sum(wts[n].size for n in SMALL) // d), 16)
    small = _flat_rows([grads[n] for n in SMALL], rows_s, d)
    stepped = {}
    stepped["w_ada"], got = _adamw([dw_ada], wts["w_ada"][0], moms["w_ada"][0], vars_["w_ada"][0], "adamw_w_ada",
                                   exch=("scatter", [cm["ffn1_w3"], jnp.broadcast_to(small, (N_CHIPS, rows_s, d))]))
    recv["ffn1_w3"], recv["small"] = got

    reduced = tuple(n for n in SHARDED if n != "w_ada") + ("small",)
    part = {n: _sum_slots(recv[n], "sum_" + n) for n in reduced}
    early = LAST_WEIGHTS + ("w_out",)
    late = tuple(n for n in reduced if n not in early)
    sib = dict(zip(early, _swap_cores([part[n] for n in early], "swap_early")))
    sib.update(zip(late, _swap_cores([part[n] for n in late], "swap_late")))
    for n in reduced[:-1]:
        stepped[n], _ = _adamw([part[n], sib[n]], wts[n][0], moms[n][0], vars_[n][0], "adamw_" + n)
    for n in SHARDED:
        stepped[n] = [a_.reshape(wts[n].shape) for a_ in stepped[n]]
    packed, _ = _adamw([part["small"], sib["small"]], _flat_rows([wts[n] for n in SMALL], rows_s, d),
                       _flat_rows([moms[n] for n in SMALL], rows_s, d), _flat_rows([vars_[n] for n in SMALL], rows_s, d),
                       "adamw_small")
    for n in SMALL:
        stepped[n] = []
    for a_ in packed:
        flat = a_.reshape(-1)
        off = 0
        for n in SMALL:
            stepped[n].append(flat[off:off + wts[n].size].reshape(wts[n].shape))
            off += wts[n].size
    return (loss, grad_x, *[stepped[n][0] for n in WEIGHTS], *[stepped[n][1] for n in WEIGHTS],
            *[stepped[n][2] for n in WEIGHTS], *[stepped[n][3] for n in WEIGHTS])
```

```python
import functools
import math

import jax
import jax.numpy as jnp
import numpy as np
from jax import lax
from jax.experimental import pallas as pl
from jax.experimental.pallas import tpu as pltpu

F32 = jnp.float32
BF16 = jnp.bfloat16

EPS = 1e-6
N_MOD = 9
HEADS = 8
QK_NOPE, QK_ROPE, V_HEAD = 64, 32, 64
QK_HEAD = QK_NOPE + QK_ROPE
HEAD_PAD = 128
SOFTMAX_SCALE = QK_HEAD ** -0.5
Q_LORA, KV_LORA = 256, 128
GROUPS, GROUP_DIM, CHUNK = 8, 64, 128
GMLP_W = GROUPS * GROUP_DIM
MLA_W = HEADS * V_HEAD
IN_COLS = 1440
PROJ_COLS = 1536
GRID_W = 64
ROPE_BASE = 10000.0
MOD_ROWS = 16
ADAM_LR, ADAM_B1, ADAM_B2, ADAM_EPS, ADAM_WD, ADAM_STEP = 0.001, 0.9, 0.999, 1e-08, 0.01, 10
N_CHIPS = 4
LANES = 128
V7X_VMEM_LIMIT = 56 * 1024 * 1024
GELU_C = math.sqrt(2.0 / math.pi)

SHARDED = ("w_ada", "ffn1_w1", "ffn1_w3", "ffn1_w2", "w_in", "w_uq", "w_ukv", "w_out", "ffn2_w1", "ffn2_w3", "ffn2_w2")
ROW_SHARDED = ("ffn1_w2", "w_out", "ffn2_w2")
T_WEIGHTS = ("ffn1_w1", "ffn1_w3", "ffn2_w1", "ffn2_w3", "w_in", "w_uq")
FIRST_WEIGHTS = ("ffn1_w1", "ffn1_w3", "ffn1_w2")
MIX_WEIGHTS = ("w_in", "w_uq", "w_ukv", "w_out")
LAST_WEIGHTS = ("ffn2_w1", "ffn2_w3", "ffn2_w2")
SMALL = ("c_ctx", "b_ada", "norm1_w", "norm2_w", "q_a_norm_w", "kv_a_norm_w", "q_norm_w", "k_norm_w", "v_norm_w",
         "w_s", "b_s", "norm3_w")
WEIGHTS = ("c_ctx", "w_ada", "b_ada", "norm1_w", "ffn1_w1", "ffn1_w3", "ffn1_w2", "norm2_w", "w_in", "q_a_norm_w",
           "w_uq", "kv_a_norm_w", "w_ukv", "q_norm_w", "k_norm_w", "v_norm_w", "w_s", "b_s", "w_out", "norm3_w",
           "ffn2_w1", "ffn2_w3", "ffn2_w2")


def _round_up(n, m):
    return (n + m - 1) // m * m


def _div_tile(n, target, mult):
    best = None
    for t in range(mult, min(n, target) + 1, mult):
        if n % t == 0:
            best = t
    return n if best is None else best


def _dot(a, b):
    return lax.dot_general(a, b, (((1,), (0,)), ((), ())), preferred_element_type=F32)


def _dot_nt(a, b):
    return lax.dot_general(a, b, (((1,), (1,)), ((), ())), preferred_element_type=F32)


def _dot_tn(a, b):
    return lax.dot_general(a, b, (((0,), (0,)), ((), ())), preferred_element_type=F32)


def _sigmoid(x):
    return 1.0 / (1.0 + jnp.exp(-x))


def _gelu(x):
    return 0.5 * x * (1.0 + jnp.tanh(GELU_C * (x + 0.044715 * x * x * x)))


def _gelu_grad(x):
    t = jnp.tanh(GELU_C * (x + 0.044715 * x * x * x))
    return 0.5 * (1.0 + t) + 0.5 * x * (1.0 - t * t) * (GELU_C * (1.0 + 3 * 0.044715 * x * x))


def _rope(x, cos, sin_a, sin_b):
    return x * cos + pltpu.roll(x, 8, 1) * sin_a + pltpu.roll(x, HEAD_PAD - 8, 1) * sin_b


def _rope_t(d, cos, sin_a, sin_b):
    return d * cos + pltpu.roll(d * sin_a, HEAD_PAD - 8, 1) + pltpu.roll(d * sin_b, 8, 1)


def _rope3(x, cos, sin_a, sin_b):
    return x * cos + pltpu.roll(x, 8, 2) * sin_a + pltpu.roll(x, HEAD_PAD - 8, 2) * sin_b


def _rope3_t(d, cos, sin_a, sin_b):
    return d * cos + pltpu.roll(d * sin_a, HEAD_PAD - 8, 2) + pltpu.roll(d * sin_b, 8, 2)


def _group_sum(x, ones_ref):
    hi = x.astype(BF16)
    lo = (x - hi.astype(F32)).astype(BF16)
    return _dot(hi, ones_ref[...]) + _dot(lo, ones_ref[...])


def _params(n_axes):
    return pltpu.CompilerParams(dimension_semantics=("arbitrary",) * n_axes, vmem_limit_bytes=V7X_VMEM_LIMIT)


def _whole(shape):
    nd = len(shape)
    return pl.BlockSpec(shape, lambda *_: (0,) * nd, pipeline_mode=pl.Buffered(1))


def _sds(shape, dtype):
    return jax.ShapeDtypeStruct(shape, dtype)


def _token_tile(s, ctx):
    return _div_tile(math.gcd(s, ctx), 256, CHUNK)


def _other_chips(x, y):
    return [(1 - x, y), (x, 1 - y), (1 - x, 1 - y)]


def _exch_copies(kind, srcs, dsts, send_sems, recv_sems, local_sems):
    x, y, c = lax.axis_index("x"), lax.axis_index("y"), lax.axis_index("c")
    me = 2 * x + y
    local, sends, arrivals = [], [], []
    for w, (src, dst) in enumerate(zip(srcs, dsts)):
        own = src if kind == "gather" else src.at[me]
        local.append(pltpu.make_async_copy(own, dst.at[me], local_sems.at[w]))
        for k, (px, py) in enumerate(_other_chips(x, y)):
            sem = dict(send_sem=send_sems.at[3 * w + k], recv_sem=recv_sems.at[3 * w + k], device_id=(px, py, c),
                       device_id_type=pl.DeviceIdType.MESH)
            out = src if kind == "gather" else src.at[2 * px + py]
            sends.append(pltpu.make_async_remote_copy(src_ref=out, dst_ref=dst.at[me], **sem))
            arrivals.append(pltpu.make_async_remote_copy(src_ref=own, dst_ref=dst.at[2 * px + py], **sem))
    return local, sends, arrivals


def _exch_start(kind, srcs, dsts, sems):
    local, sends, _ = _exch_copies(kind, srcs, dsts, *sems)
    for cp in local + sends:
        cp.start()


def _exch_wait(kind, srcs, dsts, sems):
    local, sends, arrivals = _exch_copies(kind, srcs, dsts, *sems)
    for cp in arrivals:
        cp.wait_recv()
    for cp in sends:
        cp.wait_send()
    for cp in local:
        cp.wait()


def _exch_scratch(n):
    return [pltpu.SemaphoreType.DMA((3 * n,)), pltpu.SemaphoreType.DMA((3 * n,)), pltpu.SemaphoreType.DMA((n,))]


def _exch_shapes(kind, arrays):
    return [_sds((N_CHIPS,) + a.shape if kind == "gather" else a.shape, a.dtype) for a in arrays]


def _hosted_call(body, name, grid, in_specs, out_specs, out_shape, operands, scratch=(), exch=None):
    n_axes = len(grid)
    if exch is None:
        outs = pl.pallas_call(body, name=name, grid=grid, in_specs=list(in_specs), out_specs=list(out_specs),
                              out_shape=list(out_shape), scratch_shapes=list(scratch),
                              compiler_params=_params(n_axes))(*operands)
        return list(outs), []
    kind, arrays = exch
    n_in, n_out, n_sc, n_ex = len(in_specs), len(out_specs), len(scratch), len(arrays)

    def hosted(*refs):
        cin, ein = refs[:n_in], refs[n_in:n_in + n_ex]
        o0 = n_in + n_ex
        cout, eout = refs[o0:o0 + n_out], refs[o0 + n_out:o0 + n_out + n_ex]
        rest = refs[o0 + n_out + n_ex:]
        csc, sems = rest[:n_sc], rest[n_sc:]
        first = functools.reduce(jnp.logical_and, [pl.program_id(a) == 0 for a in range(n_axes)])
        last = functools.reduce(jnp.logical_and, [pl.program_id(a) == grid[a] - 1 for a in range(n_axes)])

        @pl.when(first)
        def _():
            _exch_start(kind, ein, eout, sems)

        body(*cin, *cout, *csc)

        @pl.when(last)
        def _():
            _exch_wait(kind, ein, eout, sems)

    any_spec = pl.BlockSpec(memory_space=pl.ANY)
    outs = pl.pallas_call(
        hosted, name=name, grid=grid, in_specs=list(in_specs) + [any_spec] * n_ex,
        out_specs=list(out_specs) + [any_spec] * n_ex, out_shape=list(out_shape) + _exch_shapes(kind, arrays),
        scratch_shapes=list(scratch) + _exch_scratch(n_ex), compiler_params=_params(n_axes),
    )(*operands, *arrays)
    return list(outs[:n_out]), list(outs[n_out:])


class _TokenTiles:
    def __init__(self, t, tc, tm):
        self.n_lat, self.n_ctx = t // tm, tc // tm
        self.n_all = self.n_lat + self.n_ctx

    def tile(self, i):
        return (i + self.n_lat) % self.n_all if self.n_ctx else i

    def is_lat(self, i):
        return self.tile(i) < self.n_lat

    def row(self, i):
        return (self.tile(i), 0)

    def lat_row(self, i):
        return (jnp.where(self.is_lat(i), self.tile(i), 0), 0) if self.n_ctx else (i, 0)

    def ctx_row(self, i):
        return (jnp.where(self.is_lat(i), self.n_ctx - 1, self.tile(i) - self.n_lat), 0)


def _ffn_fwd(x_lat, x_ctx, mod, nw, w1, w3, w2, k0, s, nb, tm, name, target=None, exch=None):
    t, d = x_lat.shape
    tc = 0 if x_ctx is None else x_ctx.shape[0]
    f = w1.shape[0]
    tiles = _TokenTiles(t, tc, tm)
    n_x = 2 if tc else 1
    n_t = 0 if target is None else 1
    assert not (tc and n_t)

    def body(*refs):
        x_ref = refs[0]
        t_ref = refs[n_x] if n_t else None
        mod_ref, nw_ref, w1_ref, w3_ref, w2_ref, o_ref, a_ref, b_ref, y_ref = refs[n_x + n_t:n_x + n_t + 9]
        i = pl.program_id(0)
        g = jnp.minimum((tiles.tile(i) * tm) // s, nb)
        shift = mod_ref[g, pl.ds(k0, 1), :]
        scale = mod_ref[g, pl.ds(k0 + 1, 1), :]
        gate = mod_ref[g, pl.ds(k0 + 2, 1), :]
        x = jnp.where(tiles.is_lat(i), x_ref[...], refs[1][...]) if tc else x_ref[...]
        r = lax.rsqrt(jnp.mean(x * x, axis=-1, keepdims=True) + EPS)
        hb = ((x * r * nw_ref[...]) * (1.0 + scale) + shift).astype(BF16)
        a = _dot_nt(hb, w1_ref[...])
        b = _dot_nt(hb, w3_ref[...])
        gb = (a * _sigmoid(a) * b).astype(BF16)
        y = _dot(gb, w2_ref[...])
        out = x + (0.5 * gate) * y
        a_ref[...] = a.astype(BF16)
        b_ref[...] = b.astype(BF16)
        y_ref[...] = y.astype(BF16)
        if n_t:
            loss_ref, acc_ref = refs[-2:]

            @pl.when(i == 0)
            def _():
                acc_ref[...] = jnp.zeros_like(acc_ref)

            e = out - t_ref[...]
            o_ref[...] = e * (1.0 / d)
            acc_ref[...] += jnp.sum(e * e, axis=0, keepdims=True)

            @pl.when(i == tiles.n_all - 1)
            def _():
                loss_ref[...] = (0.5 / d) * jnp.sum(acc_ref[...], axis=-1, keepdims=True)
        else:
            o_ref[...] = out

    td = pl.BlockSpec((tm, d), tiles.row)
    tf = pl.BlockSpec((tm, f), tiles.row)
    return _hosted_call(
        body, name, (tiles.n_all,),
        [pl.BlockSpec((tm, d), tiles.lat_row)] + ([pl.BlockSpec((tm, d), tiles.ctx_row)] if tc else []) + [td] * n_t
        + [_whole(mod.shape), _whole(nw.shape), _whole(w1.shape), _whole(w3.shape), _whole(w2.shape)],
        [td, tf, tf, td] + [pl.BlockSpec((1, 1), lambda i: (0, 0))] * n_t,
        [_sds((t + tc, d), F32), _sds((t + tc, f), BF16), _sds((t + tc, f), BF16), _sds((t + tc, d), BF16)]
        + [_sds((1, 1), F32)] * n_t,
        (x_lat,) + ((x_ctx,) if tc else ()) + ((target,) if n_t else ()) + (mod, nw, w1, w3, w2),
        scratch=[pltpu.VMEM((1, d), F32)] * n_t, exch=exch)


def _ffn_bwd(dout, x_lat, x_ctx, a, b, y, mod, nw, w1, w3, w2, k0, s, nb, tm, name, exch=None):
    t, d = x_lat.shape
    tc = 0 if x_ctx is None else x_ctx.shape[0]
    f = w1.shape[0]
    nch = 2 if (f // 2) % LANES == 0 and f % 2 == 0 else 1
    fc = f // nch
    tiles = _TokenTiles(t, tc, tm)
    n_x = 2 if tc else 1

    def body(*refs):
        do_ref, x_ref = refs[0], refs[1]
        (a_ref, b_ref, y_ref, mod_ref, nw_ref, w1_ref, w3_ref, w2_ref,
         dx_ref, h_ref, g_ref, da_ref, db_ref, dy_ref, dmod_ref, dnw_ref) = refs[1 + n_x:]
        i = pl.program_id(0)

        @pl.when(i == 0)
        def _():
            dmod_ref[...] = jnp.zeros_like(dmod_ref)
            dnw_ref[...] = jnp.zeros_like(dnw_ref)

        g = jnp.minimum((tiles.tile(i) * tm) // s, nb)
        shift = mod_ref[g, pl.ds(k0, 1), :]
        scale = mod_ref[g, pl.ds(k0 + 1, 1), :]
        gate = mod_ref[g, pl.ds(k0 + 2, 1), :]
        x = jnp.where(tiles.is_lat(i), x_ref[...], refs[2][...]) if tc else x_ref[...]
        dout_v = do_ref[...]
        r = lax.rsqrt(jnp.mean(x * x, axis=-1, keepdims=True) + EPS)
        xh = x * r
        n = xh * nw_ref[...]
        h_ref[...] = (n * (1.0 + scale) + shift).astype(BF16)
        dyb = ((0.5 * gate) * dout_v).astype(BF16)
        dy_ref[...] = dyb
        dmod_ref[g, pl.ds(k0 + 2, 1), :] += 0.5 * jnp.sum(dout_v * y_ref[...].astype(F32), axis=0, keepdims=True)
        dh = jnp.zeros((tm, d), F32)
        for c in range(nch):
            sl = slice(c * fc, (c + 1) * fc)
            dg = _dot_nt(dyb, w2_ref[sl, :])
            av = a_ref[:, sl].astype(F32)
            bv = b_ref[:, sl].astype(F32)
            sig = _sigmoid(av)
            sa = av * sig
            g_ref[:, sl] = (sa * bv).astype(BF16)
            dab = (dg * bv * (sig * (1.0 + av * (1.0 - sig)))).astype(BF16)
            dbb = (dg * sa).astype(BF16)
            da_ref[:, sl] = dab
            db_ref[:, sl] = dbb
            dh = dh + _dot(dab, w1_ref[sl, :]) + _dot(dbb, w3_ref[sl, :])
        dmod_ref[g, pl.ds(k0, 1), :] += jnp.sum(dh, axis=0, keepdims=True)
        dmod_ref[g, pl.ds(k0 + 1, 1), :] += jnp.sum(dh * n, axis=0, keepdims=True)
        dn = dh * (1.0 + scale)
        dnw_ref[...] += jnp.sum(dn * xh, axis=0, keepdims=True)
        dxh = dn * nw_ref[...]
        dx_ref[...] = dout_v + r * (dxh - xh * jnp.mean(dxh * xh, axis=-1, keepdims=True))

    td = pl.BlockSpec((tm, d), tiles.row)
    tf = pl.BlockSpec((tm, f), tiles.row)
    lat = pl.BlockSpec((tm, d), tiles.lat_row)
    ta = t + tc
    return _hosted_call(
        body, name, (tiles.n_all,),
        [td, lat] + ([pl.BlockSpec((tm, d), tiles.ctx_row)] if tc else [])
        + [tf, tf, td, _whole(mod.shape), _whole(nw.shape), _whole(w1.shape), _whole(w3.shape), _whole(w2.shape)],
        [lat, td, tf, tf, tf, td, pl.BlockSpec(mod.shape, lambda i: (0, 0, 0)), pl.BlockSpec((1, d), lambda i: (0, 0))],
        [_sds((t, d), F32), _sds((ta, d), BF16), _sds((ta, f), BF16), _sds((ta, f), BF16), _sds((ta, f), BF16),
         _sds((ta, d), BF16), _sds(mod.shape, F32), _sds((1, d), F32)],
        (dout, x_lat) + ((x_ctx,) if tc else ()) + (a, b, y, mod, nw, w1, w3, w2), exch=exch)


def _mm_tn(a, b, rows, name, exch=None):
    m = a.shape[1]
    n = b.shape[1]
    bm = _div_tile(m, 1408, LANES)
    bn = _div_tile(n, 1408, LANES)
    bk = _div_tile(rows, 512, LANES)
    nk = rows // bk

    def body(a_ref, b_ref, o_ref, acc_ref):
        k = pl.program_id(2)

        @pl.when(k == 0)
        def _():
            acc_ref[...] = jnp.zeros_like(acc_ref)

        acc_ref[...] += _dot_tn(a_ref[...], b_ref[...])

        @pl.when(k == nk - 1)
        def _():
            o_ref[...] = acc_ref[...].astype(BF16)

    (out,), got = _hosted_call(
        body, name, (m // bm, n // bn, nk),
        [pl.BlockSpec((bk, bm), lambda i, j, k: (k, i)), pl.BlockSpec((bk, bn), lambda i, j, k: (k, j))],
        [pl.BlockSpec((bm, bn), lambda i, j, k: (i, j))], [_sds((m, n), BF16)], (a, b),
        scratch=[pltpu.VMEM((bm, bn), F32)], exch=exch)
    return out if exch is None else (out, got)


def _mixin_fwd(xs, mod, nw, wp, s, nb, tm):
    t, d = xs.shape

    def body(x_ref, mod_ref, nw_ref, wp_ref, h_ref, p_ref):
        g = jnp.minimum((pl.program_id(0) * tm) // s, nb)
        shift = mod_ref[g, pl.ds(3, 1), :]
        scale = mod_ref[g, pl.ds(4, 1), :]
        x = x_ref[...]
        r = lax.rsqrt(jnp.mean(x * x, axis=-1, keepdims=True) + EPS)
        hb = ((x * r * nw_ref[...]) * (1.0 + scale) + shift).astype(BF16)
        h_ref[...] = hb
        p_ref[...] = _dot_nt(hb, wp_ref[...])

    row = lambda i: (i, 0)
    return pl.pallas_call(
        body, name="mixin_fwd", grid=(t // tm,),
        in_specs=[pl.BlockSpec((tm, d), row), _whole(mod.shape), _whole(nw.shape), _whole(wp.shape)],
        out_specs=[pl.BlockSpec((tm, d), row), pl.BlockSpec((tm, PROJ_COLS), row)],
        out_shape=[_sds((t, d), BF16), _sds((t, PROJ_COLS), F32)], compiler_params=_params(1),
    )(xs, mod, nw, wp)


def _mixin_bwd(dp0, dpu, dpv, xs, dres, mod, nw, wp, s, nb, tm):
    t_all, d = xs.shape
    nlat = dres.shape[0] // tm

    def body(p0_ref, pu_ref, pv_ref, x_ref, dr_ref, mod_ref, nw_ref, wp_ref, dx_ref, dmod_ref, dnw_ref):
        i = pl.program_id(0)

        @pl.when(i == 0)
        def _():
            dmod_ref[...] = jnp.zeros_like(dmod_ref)
            dnw_ref[...] = jnp.zeros_like(dnw_ref)

        lat = i < nlat
        g = jnp.minimum((i * tm) // s, nb)
        scale = mod_ref[g, pl.ds(4, 1), :]
        dh = _dot(p0_ref[...], wp_ref[0:512, :])
        extra = _dot(pu_ref[...], wp_ref[512:1024, :]) + _dot(pv_ref[...], wp_ref[1024:1536, :])
        dh = dh + jnp.where(lat, extra, 0.0)
        x = x_ref[...]
        r = lax.rsqrt(jnp.mean(x * x, axis=-1, keepdims=True) + EPS)
        xh = x * r
        n = xh * nw_ref[...]
        dmod_ref[g, pl.ds(3, 1), :] += jnp.sum(dh, axis=0, keepdims=True)
        dmod_ref[g, pl.ds(4, 1), :] += jnp.sum(dh * n, axis=0, keepdims=True)
        dn = dh * (1.0 + scale)
        dnw_ref[...] += jnp.sum(dn * xh, axis=0, keepdims=True)
        dxh = dn * nw_ref[...]
        dx_ref[...] = jnp.where(lat, dr_ref[...], 0.0) + r * (dxh - xh * jnp.mean(dxh * xh, axis=-1, keepdims=True))

    row = lambda i: (i, 0)
    lrow = lambda i: (jnp.minimum(i, nlat - 1), 0)
    return pl.pallas_call(
        body, name="mixin_bwd", grid=(t_all // tm,),
        in_specs=[pl.BlockSpec((tm, 512), row), pl.BlockSpec((tm, 512), lrow), pl.BlockSpec((tm, 512), lrow),
                  pl.BlockSpec((tm, d), row), pl.BlockSpec((tm, d), lrow), _whole(mod.shape), _whole(nw.shape),
                  _whole(wp.shape)],
        out_specs=[pl.BlockSpec((tm, d), row), pl.BlockSpec(mod.shape, lambda i: (0, 0, 0)),
                   pl.BlockSpec((1, d), lambda i: (0, 0))],
        out_shape=[_sds((t_all, d), F32), _sds(mod.shape, F32), _sds((1, d), F32)], compiler_params=_params(1),
    )(dp0, dpu, dpv, xs, dres, mod, nw, wp)


def _prep_fwd(proj, row0, nb, s, pos0, sk, key0, into, tabs, wq, wk, wv, kvaw, qaw, qnw, knw, tm, with_q, name):
    nblk = s // tm
    n_into = 0 if into is None else 2

    def body(p_ref, cos_ref, sa_ref, sb_ref, wq_ref, wk_ref, wv_ref, kvaw_ref, qaw_ref, qnw_ref, knw_ref, *rest):
        outs, heads_ref = rest[n_into:-1], rest[-1]
        q_ref, k_ref, v_ref = outs if with_q else (None,) + outs
        cos, sin_a, sin_b = cos_ref[...][None], sa_ref[...][None], sb_ref[...][None]

        def normed_roped(w_ref, src, extra, nw_ref, o_ref, post):
            for h in range(HEADS):
                heads_ref[h] = _dot_nt(src, w_ref[h]) if extra is None else _dot(src, w_ref[h])
            xp = heads_ref[...] if extra is None else heads_ref[...] + extra[None]
            r = lax.rsqrt(jnp.sum(xp * xp, axis=-1, keepdims=True) * (1.0 / QK_HEAD) + EPS)
            o_ref[...] = _rope3(xp * r * (nw_ref[...] * post)[None], cos, sin_a, sin_b).astype(BF16)

        ckv = p_ref[:, 0:128]
        rkv = lax.rsqrt(jnp.mean(ckv * ckv, axis=-1, keepdims=True) + EPS)
        ckvb = (ckv * rkv * kvaw_ref[...]).astype(BF16)
        normed_roped(wk_ref, ckvb, p_ref[:, 128:256], knw_ref, k_ref, 1.0)
        for j in range(HEADS // 2):
            v_ref[j] = _dot(ckvb, wv_ref[j]).astype(BF16)
        if with_q:
            cq = p_ref[:, 256:512]
            rq = lax.rsqrt(jnp.mean(cq * cq, axis=-1, keepdims=True) + EPS)
            normed_roped(wq_ref, (cq * rq * qaw_ref[...]).astype(BF16), None, qnw_ref, q_ref, SOFTMAX_SCALE)

    tab = pl.BlockSpec((tm, HEAD_PAD), lambda i: (pos0 + i % nblk, 0))
    qspec = pl.BlockSpec((None, HEADS, tm, HEAD_PAD), lambda i: (i // nblk, 0, i % nblk, 0))
    kspec = pl.BlockSpec((None, HEADS, tm, HEAD_PAD), lambda i: (i // nblk, 0, key0 + i % nblk, 0))
    vspec = pl.BlockSpec((None, HEADS // 2, tm, HEAD_PAD), lambda i: (i // nblk, 0, key0 + i % nblk, 0))
    qshape = _sds((nb, HEADS, s, HEAD_PAD), BF16)
    kshape = _sds((nb, HEADS, sk, HEAD_PAD), BF16)
    vshape = _sds((nb, HEADS // 2, sk, HEAD_PAD), BF16)
    n_q = 1 if with_q else 0
    return pl.pallas_call(
        body, name=name, grid=(nb * nblk,),
        in_specs=[pl.BlockSpec((tm, 512), lambda i: (row0 + i, 0)), tab, tab, tab, _whole(wq.shape), _whole(wk.shape),
                  _whole(wv.shape), _whole(kvaw.shape), _whole(qaw.shape), _whole(qnw.shape), _whole(knw.shape)]
        + [pl.BlockSpec(memory_space=pl.ANY)] * n_into,
        out_specs=([qspec] if with_q else []) + [kspec, vspec],
        out_shape=([qshape] if with_q else []) + [kshape, vshape],
        scratch_shapes=[pltpu.VMEM((HEADS, tm, HEAD_PAD), F32)],
        input_output_aliases={11: n_q, 12: n_q + 1} if n_into else {}, compiler_params=_params(1),
    )(proj, *tabs, wq, wk, wv, kvaw, qaw, qnw, knw, *(into or ()))


def _prep_bwd(proj, row0, nb, s, pos0, key0, dp_rows, dp_into, tabs, wq, wk, wv, kvaw, qaw, qnw, knw, dq, dk, dv, init, tm,
              name):
    nblk = s // tm
    with_q = dq is not None
    n_init = 0 if init is None else len(init)
    n_into = 0 if dp_into is None else 1

    def body(*refs):
        p_ref, cos_ref, sa_ref, sb_ref, wq_ref, wk_ref, wv_ref, kvaw_ref, qaw_ref, qnw_ref, knw_ref = refs[:11]
        rest = list(refs[11:])
        dq_ref = rest.pop(0) if with_q else None
        dk_ref, dv_ref = rest.pop(0), rest.pop(0)
        init_refs = [rest.pop(0) for _ in range(n_init)]
        if n_into:
            rest.pop(0)
        dp_ref = rest.pop(0)
        if with_q:
            dwq_ref, dqaw_ref, dqnw_ref = rest.pop(0), rest.pop(0), rest.pop(0)
        dwk_ref, dwv_ref, dkvaw_ref, dknw_ref, heads_ref, dhb_ref = rest
        accs = [dwk_ref, dwv_ref, dkvaw_ref, dknw_ref]

        @pl.when(pl.program_id(0) == 0)
        def _():
            for k, acc in enumerate(accs):
                acc[...] = init_refs[k][...] if n_init else jnp.zeros_like(acc)
            if with_q:
                dwq_ref[...] = jnp.zeros_like(dwq_ref)
                dqaw_ref[...] = jnp.zeros_like(dqaw_ref)
                dqnw_ref[...] = jnp.zeros_like(dqnw_ref)

        cos, sin_a, sin_b = cos_ref[...][None], sa_ref[...][None], sb_ref[...][None]
        lane = lax.broadcasted_iota(jnp.int32, (tm, HEAD_PAD), 1)
        rope_lanes = (lane >= QK_NOPE) & (lane < QK_HEAD)

        def heads_bwd(w_ref, src, extra, nw_ref, d_ref, dnw_ref, dw_ref, post):
            w_t = extra is None
            for h in range(HEADS):
                heads_ref[h] = _dot_nt(src, w_ref[h]) if w_t else _dot(src, w_ref[h])
            xp = heads_ref[...] if extra is None else heads_ref[...] + extra[None]
            r = lax.rsqrt(jnp.sum(xp * xp, axis=-1, keepdims=True) * (1.0 / QK_HEAD) + EPS)
            xh = xp * r
            dn = _rope3_t(d_ref[...], cos, sin_a, sin_b)
            dnw_ref[...] += post * jnp.sum(jnp.sum(dn * xh, axis=0), axis=0, keepdims=True)
            dxh = dn * (nw_ref[...] * post)[None]
            dxp = r * (dxh - xh * (jnp.sum(dxh * xh, axis=-1, keepdims=True) * (1.0 / QK_HEAD)))
            dhb_ref[...] = dxp.astype(BF16)
            dsrc = jnp.zeros((tm, src.shape[1]), F32)
            for h in range(HEADS):
                if w_t:
                    dsrc = dsrc + _dot(dhb_ref[h], w_ref[h])
                    dw_ref[h] += _dot_tn(dhb_ref[h], src)
                else:
                    dsrc = dsrc + _dot_nt(dhb_ref[h], w_ref[h])
                    dw_ref[h] += _dot_tn(src, dhb_ref[h])
            return dsrc, jnp.sum(dxp, axis=0)

        ckv = p_ref[:, 0:128]
        rkv = lax.rsqrt(jnp.mean(ckv * ckv, axis=-1, keepdims=True) + EPS)
        ckvh = ckv * rkv
        ckvb = (ckvh * kvaw_ref[...]).astype(BF16)
        dckv, dkp_sum = heads_bwd(wk_ref, ckvb, p_ref[:, 128:256], knw_ref, dk_ref, dknw_ref, dwk_ref, 1.0)
        for j in range(HEADS // 2):
            dvb = dv_ref[j].astype(BF16)
            dckv = dckv + _dot_nt(dvb, wv_ref[j])
            dwv_ref[j] += _dot_tn(ckvb, dvb)
        dkvaw_ref[...] += jnp.sum(dckv * ckvh, axis=0, keepdims=True)
        dch = dckv * kvaw_ref[...]
        dp_ref[:, 0:128] = (rkv * (dch - ckvh * jnp.mean(dch * ckvh, axis=-1, keepdims=True))).astype(BF16)
        dp_ref[:, 128:256] = jnp.where(rope_lanes, dkp_sum, 0.0).astype(BF16)
        if with_q:
            cq = p_ref[:, 256:512]
            rq = lax.rsqrt(jnp.mean(cq * cq, axis=-1, keepdims=True) + EPS)
            cqh = cq * rq
            cqb = (cqh * qaw_ref[...]).astype(BF16)
            dcq, _ = heads_bwd(wq_ref, cqb, None, qnw_ref, dq_ref, dqnw_ref, dwq_ref, SOFTMAX_SCALE)
            dqaw_ref[...] += jnp.sum(dcq * cqh, axis=0, keepdims=True)
            dqc = dcq * qaw_ref[...]
            dp_ref[:, 256:512] = (rq * (dqc - cqh * jnp.mean(dqc * cqh, axis=-1, keepdims=True))).astype(BF16)
        else:
            dp_ref[:, 256:512] = jnp.zeros((tm, Q_LORA), BF16)

    tab = pl.BlockSpec((tm, HEAD_PAD), lambda i: (pos0 + i % nblk, 0))
    qspec = pl.BlockSpec((None, HEADS, tm, HEAD_PAD), lambda i: (i // nblk, 0, i % nblk, 0))
    kspec = pl.BlockSpec((None, HEADS, tm, HEAD_PAD), lambda i: (i // nblk, 0, key0 + i % nblk, 0))
    vspec = pl.BlockSpec((None, HEADS // 2, tm, HEAD_PAD), lambda i: (i // nblk, 0, key0 + i % nblk, 0))

    def acc_spec(shape):
        nd = len(shape)
        return pl.BlockSpec(shape, lambda i: (0,) * nd)

    acc_shapes = [(HEADS, KV_LORA, HEAD_PAD), (HEADS // 2, KV_LORA, HEAD_PAD), (1, KV_LORA), (1, HEAD_PAD)]
    q_shapes = [(HEADS, HEAD_PAD, Q_LORA), (1, Q_LORA), (1, HEAD_PAD)] if with_q else []
    out_shapes = [(dp_rows, 512)] + q_shapes + acc_shapes
    n_before = 11 + (1 if with_q else 0) + 2 + n_init
    return pl.pallas_call(
        body, name=name, grid=(nb * nblk,),
        in_specs=[pl.BlockSpec((tm, 512), lambda i: (row0 + i, 0)), tab, tab, tab, _whole(wq.shape), _whole(wk.shape),
                  _whole(wv.shape), _whole(kvaw.shape), _whole(qaw.shape), _whole(qnw.shape), _whole(knw.shape)]
        + ([qspec] if with_q else []) + [kspec, vspec] + [_whole(a.shape) for a in (init or [])]
        + [pl.BlockSpec(memory_space=pl.ANY)] * n_into,
        out_specs=[pl.BlockSpec((tm, 512), lambda i: (row0 + i, 0))] + [acc_spec(sh) for sh in q_shapes + acc_shapes],
        out_shape=[_sds(out_shapes[0], BF16)] + [_sds(sh, F32) for sh in out_shapes[1:]],
        scratch_shapes=[pltpu.VMEM((HEADS, tm, HEAD_PAD), F32), pltpu.VMEM((HEADS, tm, HEAD_PAD), BF16)],
        input_output_aliases={n_before: 0} if n_into else {}, compiler_params=_params(1),
    )(proj, *tabs, wq, wk, wv, kvaw, qaw, qnw, knw, *([dq] if with_q else []), dk, dv, *(init or []),
      *([dp_into] if n_into else []))


def _attn_fwd(q, k, v, tq, exch=None):
    nb, _, s, _ = q.shape
    sk = k.shape[2]
    nq = s // tq
    scale = QK_HEAD ** -0.5

    def body(q_ref, k_ref, v_ref, o_ref):
        lane = lax.broadcasted_iota(jnp.int32, (tq, HEAD_PAD), 1)
        vv = v_ref[...]
        outs = []
        for hh in range(2):
            sc = _dot_nt(q_ref[hh], k_ref[hh])
            p = jnp.exp(sc - jnp.max(sc, axis=-1, keepdims=True))
            l = jnp.sum(p, axis=-1, keepdims=True)
            outs.append(_dot(p.astype(BF16), vv) / l)
        o_ref[...] = jnp.where(lane < V_HEAD, outs[0], outs[1]).astype(BF16)

    (o,), got = _hosted_call(
        body, "attn_fwd", (nb, HEADS // 2, nq),
        [pl.BlockSpec((None, 2, tq, HEAD_PAD), lambda b, j, i: (b, j, i, 0)),
         pl.BlockSpec((None, 2, sk, HEAD_PAD), lambda b, j, i: (b, j, 0, 0)),
         pl.BlockSpec((None, None, sk, HEAD_PAD), lambda b, j, i: (b, j, 0, 0))],
        [pl.BlockSpec((tq, HEAD_PAD), lambda b, j, i: (b * nq + i, j))], [_sds((nb * s, MLA_W), BF16)], (q, k, v),
        exch=exch)
    return o, got


def _attn_bwd(q, k, v, do, tq, exch=None):
    nb, _, s, _ = q.shape
    sk = k.shape[2]
    nq = s // tq
    scale = QK_HEAD ** -0.5

    def body(q_ref, k_ref, v_ref, do_ref, dq_ref, dk_ref, dv_ref):
        @pl.when(pl.program_id(2) == 0)
        def _():
            dk_ref[...] = jnp.zeros_like(dk_ref)
            dv_ref[...] = jnp.zeros_like(dv_ref)

        lane = lax.broadcasted_iota(jnp.int32, (tq, HEAD_PAD), 1)
        vv = v_ref[...]
        dov = do_ref[...]
        for hh in range(2):
            mine = (lane < V_HEAD) if hh == 0 else (lane >= V_HEAD)
            doh = jnp.where(mine, dov, jnp.zeros_like(dov))
            qh = q_ref[hh]
            sc = _dot_nt(qh, k_ref[hh])
            p = jnp.exp(sc - jnp.max(sc, axis=-1, keepdims=True))
            inv = 1.0 / jnp.sum(p, axis=-1, keepdims=True)
            dp = _dot_nt(doh, vv)
            delta = jnp.sum(p * dp, axis=-1, keepdims=True) * inv
            u = (p * (dp - delta)).astype(BF16)
            dq_ref[hh] = _dot(u, k_ref[hh]) * inv
            dk_ref[hh] += _dot_tn(u, (qh.astype(F32) * inv).astype(BF16))
            dv_ref[...] += _dot_tn(p.astype(BF16), (doh.astype(F32) * inv).astype(BF16))

    return _hosted_call(
        body, "attn_bwd", (nb, HEADS // 2, nq),
        [pl.BlockSpec((None, 2, tq, HEAD_PAD), lambda b, j, i: (b, j, i, 0)),
         pl.BlockSpec((None, 2, sk, HEAD_PAD), lambda b, j, i: (b, j, 0, 0)),
         pl.BlockSpec((None, None, sk, HEAD_PAD), lambda b, j, i: (b, j, 0, 0)),
         pl.BlockSpec((tq, HEAD_PAD), lambda b, j, i: (b * nq + i, j))],
        [pl.BlockSpec((None, 2, tq, HEAD_PAD), lambda b, j, i: (b, j, i, 0)),
         pl.BlockSpec((None, 2, sk, HEAD_PAD), lambda b, j, i: (b, j, 0, 0)),
         pl.BlockSpec((None, None, sk, HEAD_PAD), lambda b, j, i: (b, j, 0, 0))],
        [_sds(q.shape, F32), _sds(k.shape, F32), _sds(v.shape, F32)], (q, k, v, do), exch=exch)


def _group_masks(rows):
    lane = lax.broadcasted_iota(jnp.int32, (rows, GMLP_W), 1)
    return [(lane >= g * GROUP_DIM) & (lane < (g + 1) * GROUP_DIM) for g in range(GROUPS)]


def _gmlp_fwd(proj, t, wcat, bias, vnw, ones, tm):
    def body(u_ref, v_ref, wcat_ref, bias_ref, vnw_ref, ones_ref, o_ref):
        masks = _group_masks(CHUNK)
        gv = _gelu(v_ref[...])
        rv = lax.rsqrt(_group_sum(gv * gv, ones_ref) * (1.0 / GROUP_DIM) + EPS)
        vnb = (gv * rv * vnw_ref[...]).astype(BF16)
        for c in range(tm // CHUNK):
            rows = slice(c * CHUNK, (c + 1) * CHUNK)
            vc = vnb[rows]
            stack = jnp.concatenate([jnp.where(m, vc, jnp.zeros_like(vc)) for m in masks], axis=0)
            sp = _dot(wcat_ref[...], stack) + bias_ref[...]
            o_ref[rows, :] = (_gelu(u_ref[rows, :]) * sp).astype(BF16)

    return pl.pallas_call(
        body, name="gmlp_fwd", grid=(t // tm,),
        in_specs=[pl.BlockSpec((tm, GMLP_W), lambda i: (i, 1)), pl.BlockSpec((tm, GMLP_W), lambda i: (i, 2)),
                  _whole(wcat.shape), _whole(bias.shape), _whole(vnw.shape), _whole(ones.shape)],
        out_specs=pl.BlockSpec((tm, GMLP_W), lambda i: (i, 0)),
        out_shape=_sds((t, GMLP_W), BF16), compiler_params=_params(1),
    )(proj, proj, wcat, bias, vnw, ones)


def _gmlp_bwd(proj, dsg, wcat, wcat_t, bias, vnw, ones, tm):
    t = dsg.shape[0]

    def body(u_ref, v_ref, dsg_ref, wcat_ref, wcatt_ref, bias_ref, vnw_ref, ones_ref,
             du_ref, dv_ref, dws_ref, dbs_ref, dvnw_ref):
        @pl.when(pl.program_id(0) == 0)
        def _():
            dws_ref[...] = jnp.zeros_like(dws_ref)
            dbs_ref[...] = jnp.zeros_like(dbs_ref)
            dvnw_ref[...] = jnp.zeros_like(dvnw_ref)

        masks = _group_masks(CHUNK)
        v = v_ref[...]
        gv = _gelu(v)
        rv = lax.rsqrt(_group_sum(gv * gv, ones_ref) * (1.0 / GROUP_DIM) + EPS)
        xh = gv * rv
        vnb = (xh * vnw_ref[...]).astype(BF16)
        dvn_parts = []
        for c in range(tm // CHUNK):
            rows = slice(c * CHUNK, (c + 1) * CHUNK)
            vc = vnb[rows]
            stack = jnp.concatenate([jnp.where(m, vc, jnp.zeros_like(vc)) for m in masks], axis=0)
            sp = _dot(wcat_ref[...], stack) + bias_ref[...]
            u = u_ref[rows, :]
            dsg_c = dsg_ref[rows, :]
            du_ref[rows, :] = (dsg_c * sp * _gelu_grad(u)).astype(BF16)
            ds = dsg_c * _gelu(u)
            dstack = jnp.concatenate([jnp.where(m, ds, 0.0) for m in masks], axis=0)
            dbs_ref[...] += jnp.broadcast_to(jnp.sum(dstack, axis=-1, keepdims=True), dbs_ref.shape)
            dstb = dstack.astype(BF16)
            dvn_parts.append(_dot(wcatt_ref[...], dstb))
            dws_ref[...] += _dot_nt(dstb, vc)
        dvn = jnp.concatenate(dvn_parts, axis=0) if len(dvn_parts) > 1 else dvn_parts[0]
        dvnw_ref[...] += jnp.sum(dvn * xh, axis=0, keepdims=True)
        dxh = dvn * vnw_ref[...]
        gm = _group_sum(dxh * xh, ones_ref) * (1.0 / GROUP_DIM)
        dv_ref[...] = (rv * (dxh - xh * gm) * _gelu_grad(v)).astype(BF16)

    row = pl.BlockSpec((tm, GMLP_W), lambda i: (i, 0))
    return pl.pallas_call(
        body, name="gmlp_bwd", grid=(t // tm,),
        in_specs=[pl.BlockSpec((tm, GMLP_W), lambda i: (i, 1)), pl.BlockSpec((tm, GMLP_W), lambda i: (i, 2)), row,
                  _whole(wcat.shape), _whole(wcat_t.shape), _whole(bias.shape), _whole(vnw.shape), _whole(ones.shape)],
        out_specs=[row, row, pl.BlockSpec((GROUPS * CHUNK, CHUNK), lambda i: (0, 0)),
                   pl.BlockSpec((GROUPS * CHUNK, CHUNK), lambda i: (0, 0)), pl.BlockSpec((1, GMLP_W), lambda i: (0, 0))],
        out_shape=[_sds((t, GMLP_W), BF16), _sds((t, GMLP_W), BF16), _sds((GROUPS * CHUNK, CHUNK), F32),
                   _sds((GROUPS * CHUNK, CHUNK), F32), _sds((1, GMLP_W), F32)],
        compiler_params=_params(1),
    )(proj, proj, dsg, wcat, wcat_t, bias, vnw, ones)


def _mixout_fwd(o, sg, xs, mod, wout, s, tm):
    t = o.shape[0]
    d = xs.shape[1]

    def body(o_ref, sg_ref, x_ref, mod_ref, w_ref, x2_ref, mix_ref):
        g = (pl.program_id(0) * tm) // s
        gate = mod_ref[g, pl.ds(5, 1), :]
        mix = _dot(o_ref[...], w_ref[0:MLA_W, :]) + _dot(sg_ref[...], w_ref[MLA_W:MLA_W + GMLP_W, :])
        x2_ref[...] = x_ref[...] + gate * mix
        mix_ref[...] = mix.astype(BF16)

    row = lambda i: (i, 0)
    return pl.pallas_call(
        body, name="mixout_fwd", grid=(t // tm,),
        in_specs=[pl.BlockSpec((tm, MLA_W), row), pl.BlockSpec((tm, GMLP_W), row), pl.BlockSpec((tm, d), row),
                  _whole(mod.shape), _whole(wout.shape)],
        out_specs=[pl.BlockSpec((tm, d), row), pl.BlockSpec((tm, d), row)],
        out_shape=[_sds((t, d), F32), _sds((t, d), BF16)], compiler_params=_params(1),
    )(o, sg, xs, mod, wout)


def _mixout_bwd(dx2, mix, mod, wout, s, tm):
    t, d = dx2.shape

    def body(dx_ref, mix_ref, mod_ref, w_ref, dmix_ref, do_ref, dsg_ref, dmod_ref):
        i = pl.program_id(0)

        @pl.when(i == 0)
        def _():
            dmod_ref[...] = jnp.zeros_like(dmod_ref)

        g = (i * tm) // s
        gate = mod_ref[g, pl.ds(5, 1), :]
        dx = dx_ref[...]
        dmod_ref[g, pl.ds(5, 1), :] += jnp.sum(dx * mix_ref[...].astype(F32), axis=0, keepdims=True)
        dmb = (gate * dx).astype(BF16)
        dmix_ref[...] = dmb
        do_ref[...] = _dot_nt(dmb, w_ref[0:MLA_W, :]).astype(BF16)
        dsg_ref[...] = _dot_nt(dmb, w_ref[MLA_W:MLA_W + GMLP_W, :])

    row = lambda i: (i, 0)
    return pl.pallas_call(
        body, name="mixout_bwd", grid=(t // tm,),
        in_specs=[pl.BlockSpec((tm, d), row), pl.BlockSpec((tm, d), row), _whole(mod.shape), _whole(wout.shape)],
        out_specs=[pl.BlockSpec((tm, d), row), pl.BlockSpec((tm, MLA_W), row), pl.BlockSpec((tm, GMLP_W), row),
                   pl.BlockSpec(mod.shape, lambda i: (0, 0, 0))],
        out_shape=[_sds((t, d), BF16), _sds((t, MLA_W), BF16), _sds((t, GMLP_W), F32), _sds(mod.shape, F32)],
        compiler_params=_params(1),
    )(dx2, mix, mod, wout)


def _gather_first(shards):
    n = len(shards)

    def body(*refs):
        srcs, outs = refs[:n], refs[n:2 * n]
        ici_send, ici_recv, d2d_send, d2d_recv, local_sems = refs[2 * n:]
        x, y, c = lax.axis_index("x"), lax.axis_index("y"), lax.axis_index("c")
        me = 2 * x + y
        chips = _other_chips(x, y)

        def half(w, which):
            hr = shards[w].shape[0] // 2
            return pl.ds(pl.multiple_of(which * hr, 16), hr)

        def over_ici(w, k, arriving):
            px, py = chips[k]
            slot = 2 * px + py if arriving else me
            return pltpu.make_async_remote_copy(
                src_ref=srcs[w].at[half(w, c)], dst_ref=outs[w].at[slot, half(w, c)], send_sem=ici_send.at[3 * w + k],
                recv_sem=ici_recv.at[3 * w + k], device_id=(px, py, c), device_id_type=pl.DeviceIdType.MESH)

        def to_sibling(w, k, arriving):
            px, py = chips[k]
            rows = half(w, 1 - c if arriving else c)
            return pltpu.make_async_remote_copy(
                src_ref=outs[w].at[2 * px + py, rows], dst_ref=outs[w].at[2 * px + py, rows],
                send_sem=d2d_send.at[3 * w + k], recv_sem=d2d_recv.at[3 * w + k], device_id=(x, y, 1 - c),
                device_id_type=pl.DeviceIdType.MESH)

        local = [pltpu.make_async_copy(srcs[w], outs[w].at[me], local_sems.at[w]) for w in range(n)]
        for cp in local:
            cp.start()
        pairs = [(w, k) for w in range(n) for k in range(3)]
        for w, k in pairs:
            over_ici(w, k, False).start()
        for w, k in pairs:
            over_ici(w, k, True).wait_recv()
            to_sibling(w, k, False).start()
        for w, k in pairs:
            to_sibling(w, k, True).wait_recv()
        for w, k in pairs:
            over_ici(w, k, False).wait_send()
            to_sibling(w, k, False).wait_send()
        for cp in local:
            cp.wait()

    any_spec = pl.BlockSpec(memory_space=pl.ANY)
    sems = pltpu.SemaphoreType.DMA((3 * n,))
    return pl.pallas_call(
        body, name="gather_first", in_specs=[any_spec] * n, out_specs=[any_spec] * n,
        out_shape=_exch_shapes("gather", shards),
        scratch_shapes=[sems, sems, sems, sems, pltpu.SemaphoreType.DMA((n,))],
    )(*shards)


def _swap_cores(parts, name):
    n = len(parts)

    def body(*refs):
        srcs, outs, send_sems, recv_sems = refs[:n], refs[n:2 * n], refs[2 * n], refs[2 * n + 1]
        x, y, c = lax.axis_index("x"), lax.axis_index("y"), lax.axis_index("c")
        copies = [pltpu.make_async_remote_copy(
            src_ref=srcs[w], dst_ref=outs[w], send_sem=send_sems.at[w], recv_sem=recv_sems.at[w],
            device_id=(x, y, 1 - c), device_id_type=pl.DeviceIdType.MESH) for w in range(n)]
        for cp in copies:
            cp.start()
        for cp in copies:
            cp.wait()

    any_spec = pl.BlockSpec(memory_space=pl.ANY)
    return pl.pallas_call(
        body, name=name, in_specs=[any_spec] * n, out_specs=[any_spec] * n,
        out_shape=[_sds(p.shape, p.dtype) for p in parts],
        scratch_shapes=[pltpu.SemaphoreType.DMA((n,)), pltpu.SemaphoreType.DMA((n,))],
    )(*parts)


def _row_tile(r, c, mult):
    return _div_tile(r, max(mult, (1 << 16) // c), mult)


def _sum_slots(recv, name):
    _, r, c = recv.shape
    tr = _row_tile(r, c, 16)

    def body(r_ref, o_ref):
        f = lambda k: r_ref[k].astype(F32)
        o_ref[...] = ((f(0) + f(1)) + f(2)) + f(3)

    return pl.pallas_call(
        body, name=name, grid=(r // tr,),
        in_specs=[pl.BlockSpec((N_CHIPS, tr, c), lambda i: (0, i, 0))],
        out_specs=pl.BlockSpec((tr, c), lambda i: (i, 0)),
        out_shape=_sds((r, c), F32), compiler_params=_params(1),
    )(recv)


def _adamw(parts, w, m, v, name, exch=None):
    r, wd = w.shape
    tr = _row_tile(r, wd, 8)
    c1 = 1.0 / (1.0 - ADAM_B1 ** ADAM_STEP)
    c2 = 1.0 / (1.0 - ADAM_B2 ** ADAM_STEP)
    n_p = len(parts)

    def body(*refs):
        p_refs = refs[:n_p]
        w_ref, m_ref, v_ref, g_ref, d_ref, nm_ref, nv_ref = refs[n_p:]
        g = p_refs[0][...]
        for p_ref in p_refs[1:]:
            g = g + p_ref[...]
        nm = ADAM_B1 * m_ref[...] + (1.0 - ADAM_B1) * g
        nv = ADAM_B2 * v_ref[...] + (1.0 - ADAM_B2) * (g * g)
        g_ref[...] = g
        nm_ref[...] = nm
        nv_ref[...] = nv
        d_ref[...] = -ADAM_LR * ((nm * c1) / (jnp.sqrt(nv * c2) + ADAM_EPS) + ADAM_WD * w_ref[...])

    spec = pl.BlockSpec((tr, wd), lambda i: (i, 0))
    return _hosted_call(body, name, (r // tr,), [spec] * (n_p + 3), [spec] * 4, [_sds((r, wd), F32)] * 4,
                        (*parts, w, m, v), exch=exch)


def _all_peers(x, y, c):
    flips = [(dx, dy, dc) for dx in (0, 1) for dy in (0, 1) for dc in (0, 1)][1:]
    return [(1 - x if dx else x, 1 - y if dy else y, 1 - c if dc else c) for dx, dy, dc in flips]


def _gather_devices(block):
    def body(src_ref, out_ref, send_sems, recv_sems, local_sem):
        x, y, c = lax.axis_index("x"), lax.axis_index("y"), lax.axis_index("c")
        me = 4 * x + 2 * y + c
        mine = pltpu.make_async_copy(src_ref, out_ref.at[me], local_sem)
        mine.start()

        def copy(k, peer, slot):
            return pltpu.make_async_remote_copy(
                src_ref=src_ref, dst_ref=out_ref.at[slot], send_sem=send_sems.at[k], recv_sem=recv_sems.at[k],
                device_id=peer, device_id_type=pl.DeviceIdType.MESH)

        peers = _all_peers(x, y, c)
        for k, peer in enumerate(peers):
            copy(k, peer, me).start()
        for k, (px, py, pc) in enumerate(peers):
            copy(k, (px, py, pc), 4 * px + 2 * py + pc).wait_recv()
        for k, peer in enumerate(peers):
            copy(k, peer, me).wait_send()
        mine.wait()

    return pl.pallas_call(
        body, name="gather_devices", in_specs=[pl.BlockSpec(memory_space=pl.ANY)],
        out_specs=pl.BlockSpec(memory_space=pl.ANY), out_shape=_sds((8,) + block.shape, block.dtype),
        scratch_shapes=[pltpu.SemaphoreType.DMA((7,)), pltpu.SemaphoreType.DMA((7,)), pltpu.SemaphoreType.DMA(())],
    )(block)


def _ada_fwd_tp(cc_all, w, b):
    n = w.shape[1]

    def body(cc_ref, w_ref, b_ref, out_ref, part_ref, send_sems, recv_sems):
        x, y, c = lax.axis_index("x"), lax.axis_index("y"), lax.axis_index("c")
        me = 2 * x + y
        cv = cc_ref[...]
        part_ref[...] = _dot((cv * _sigmoid(cv)).astype(BF16), w_ref[...]) + b_ref[...]

        def rows_of(px, py):
            return part_ref.at[pl.ds(pl.multiple_of((4 * px + 2 * py + c) * MOD_ROWS, MOD_ROWS), MOD_ROWS)]

        def copy(k, px, py, slot):
            return pltpu.make_async_remote_copy(
                src_ref=rows_of(px, py), dst_ref=out_ref.at[slot], send_sem=send_sems.at[k], recv_sem=recv_sems.at[k],
                device_id=(px, py, c), device_id_type=pl.DeviceIdType.MESH)

        chips = _other_chips(x, y)
        for k, (px, py) in enumerate(chips):
            copy(k, px, py, me).start()
        out_ref[me] = rows_of(x, y)[...]
        for k, (px, py) in enumerate(chips):
            copy(k, px, py, 2 * px + py).wait_recv()
        for k, (px, py) in enumerate(chips):
            copy(k, px, py, me).wait_send()

    vmem = pl.BlockSpec(memory_space=pltpu.VMEM)
    return pl.pallas_call(
        body, name="ada_fwd_tp", in_specs=[vmem, vmem, vmem], out_specs=vmem,
        out_shape=_sds((N_CHIPS, MOD_ROWS, n), F32),
        scratch_shapes=[pltpu.VMEM((8 * MOD_ROWS, n), F32), pltpu.SemaphoreType.DMA((3,)), pltpu.SemaphoreType.DMA((3,))],
        compiler_params=pltpu.CompilerParams(vmem_limit_bytes=V7X_VMEM_LIMIT),
    )(cc_all, w, b)


def _ada_bwd_tp(cc_all, dmods, w, ctx_row):
    d, n = w.shape

    def body(cc_ref, m0, m1, m2, m3, w_ref, dw_ref, db_ref, dctx_ref, stage_ref, all_ref, send_sems, recv_sems):
        x, y, c = lax.axis_index("x"), lax.axis_index("y"), lax.axis_index("c")
        me = 4 * x + 2 * y + c
        dsum = m0[...] + m1[...] + m2[...] + m3[...]
        db_ref[...] = jnp.sum(dsum, axis=0, keepdims=True)
        for j in range(N_CHIPS):
            stage_ref[j] = dsum[:, j * n:(j + 1) * n]

        def copy(k, peer, slot):
            px, py, _ = peer
            return pltpu.make_async_remote_copy(
                src_ref=stage_ref.at[2 * px + py], dst_ref=all_ref.at[slot], send_sem=send_sems.at[k],
                recv_sem=recv_sems.at[k], device_id=peer, device_id_type=pl.DeviceIdType.MESH)

        peers = _all_peers(x, y, c)
        for k, peer in enumerate(peers):
            copy(k, peer, me).start()
        all_ref[me] = stage_ref[2 * x + y]
        for k, (px, py, pc) in enumerate(peers):
            copy(k, (px, py, pc), 4 * px + 2 * py + pc).wait_recv()
        for k, peer in enumerate(peers):
            copy(k, peer, me).wait_send()
        cv = cc_ref[...]
        sig = _sigmoid(cv)
        dmb = all_ref[...].reshape(8 * MOD_ROWS, n).astype(BF16)
        dw_ref[...] = _dot_tn((cv * sig).astype(BF16), dmb)
        dsc = _dot_nt(dmb, w_ref[...])
        dctx = dsc[ctx_row:ctx_row + 1, :]
        for dev in range(1, 8):
            dctx = dctx + dsc[dev * MOD_ROWS + ctx_row:dev * MOD_ROWS + ctx_row + 1, :]
        cx = cv[ctx_row:ctx_row + 1, :]
        sx = sig[ctx_row:ctx_row + 1, :]
        dctx_ref[...] = dctx * (sx * (1.0 + cx * (1.0 - sx))) * jnp.where(c == 0, 1.0, 0.0)

    vmem = pl.BlockSpec(memory_space=pltpu.VMEM)
    return pl.pallas_call(
        body, name="ada_bwd_tp", in_specs=[vmem] * 6, out_specs=[vmem] * 3,
        out_shape=[_sds((d, n), F32), _sds((1, N_MOD * d), F32), _sds((1, d), F32)],
        scratch_shapes=[pltpu.VMEM((N_CHIPS, MOD_ROWS, n), F32), pltpu.VMEM((8, MOD_ROWS, n), F32),
                        pltpu.SemaphoreType.DMA((7,)), pltpu.SemaphoreType.DMA((7,))],
        compiler_params=pltpu.CompilerParams(vmem_limit_bytes=V7X_VMEM_LIMIT),
    )(cc_all, *dmods, w)


def _rope_tables(s, ctx):
    pos = np.arange(s, dtype=np.float32)
    inv = (np.float32(ROPE_BASE) ** (-np.arange(0, QK_ROPE // 2, 2, dtype=np.float32) / np.float32(QK_ROPE // 2)))
    ang_r = np.floor(pos / GRID_W)[:, None] * inv
    ang_c = (pos - GRID_W * np.floor(pos / GRID_W))[:, None] * inv
    ang = np.concatenate([ang_r, ang_r, ang_c, ang_c], axis=-1).astype(np.float32)
    cos, sin = np.cos(ang), np.sin(ang)
    half_b = (np.arange(QK_ROPE) // 8) % 2 == 1
    sin_a = np.where(half_b, sin, 0.0)
    sin_b = np.where(half_b, 0.0, -sin)

    def place(tab, fill):
        full = np.full((s + ctx, HEAD_PAD), fill, np.float32)
        full[:s, QK_NOPE:QK_HEAD] = tab
        return jnp.asarray(full)

    return place(cos, 1.0), place(sin_a, 0.0), place(sin_b, 0.0)


def _pad_last(a, n):
    return jnp.pad(a, [(0, 0)] * (a.ndim - 1) + [(0, n - a.shape[-1])])


def _flat_rows(parts, rows, width):
    flat = jnp.concatenate([p.reshape(-1) for p in parts])
    return jnp.pad(flat, (0, rows * width - flat.shape[0])).reshape(rows, width)


def kernel(x, c, ctx, c_ctx, w_ada, b_ada, norm1_w, ffn1_w1, ffn1_w3, ffn1_w2, norm2_w, w_in, q_a_norm_w, w_uq, kv_a_norm_w, w_ukv, q_norm_w, k_norm_w, v_norm_w, w_s, b_s, w_out, norm3_w, ffn2_w1, ffn2_w3, ffn2_w2, loss_target, m_c_ctx, m_w_ada, m_b_ada, m_norm1_w, m_ffn1_w1, m_ffn1_w3, m_ffn1_w2, m_norm2_w, m_w_in, m_q_a_norm_w, m_w_uq, m_kv_a_norm_w, m_w_ukv, m_q_norm_w, m_k_norm_w, m_v_norm_w, m_w_s, m_b_s, m_w_out, m_norm3_w, m_ffn2_w1, m_ffn2_w3, m_ffn2_w2, v_c_ctx, v_w_ada, v_b_ada, v_norm1_w, v_ffn1_w1, v_ffn1_w3, v_ffn1_w2, v_norm2_w, v_w_in, v_q_a_norm_w, v_w_uq, v_kv_a_norm_w, v_w_ukv, v_q_norm_w, v_k_norm_w, v_v_norm_w, v_w_s, v_b_s, v_w_out, v_norm3_w, v_ffn2_w1, v_ffn2_w3, v_ffn2_w2):
    wts = dict(c_ctx=c_ctx, w_ada=w_ada, b_ada=b_ada, norm1_w=norm1_w, ffn1_w1=ffn1_w1, ffn1_w3=ffn1_w3, ffn1_w2=ffn1_w2,
               norm2_w=norm2_w, w_in=w_in, q_a_norm_w=q_a_norm_w, w_uq=w_uq, kv_a_norm_w=kv_a_norm_w, w_ukv=w_ukv,
               q_norm_w=q_norm_w, k_norm_w=k_norm_w, v_norm_w=v_norm_w, w_s=w_s, b_s=b_s, w_out=w_out, norm3_w=norm3_w,
               ffn2_w1=ffn2_w1, ffn2_w3=ffn2_w3, ffn2_w2=ffn2_w2)
    moms = dict(c_ctx=m_c_ctx, w_ada=m_w_ada, b_ada=m_b_ada, norm1_w=m_norm1_w, ffn1_w1=m_ffn1_w1, ffn1_w3=m_ffn1_w3,
                ffn1_w2=m_ffn1_w2, norm2_w=m_norm2_w, w_in=m_w_in, q_a_norm_w=m_q_a_norm_w, w_uq=m_w_uq,
                kv_a_norm_w=m_kv_a_norm_w, w_ukv=m_w_ukv, q_norm_w=m_q_norm_w, k_norm_w=m_k_norm_w, v_norm_w=m_v_norm_w,
                w_s=m_w_s, b_s=m_b_s, w_out=m_w_out, norm3_w=m_norm3_w, ffn2_w1=m_ffn2_w1, ffn2_w3=m_ffn2_w3,
                ffn2_w2=m_ffn2_w2)
    vars_ = dict(c_ctx=v_c_ctx, w_ada=v_w_ada, b_ada=v_b_ada, norm1_w=v_norm1_w, ffn1_w1=v_ffn1_w1, ffn1_w3=v_ffn1_w3,
                 ffn1_w2=v_ffn1_w2, norm2_w=v_norm2_w, w_in=v_w_in, q_a_norm_w=v_q_a_norm_w, w_uq=v_w_uq,
                 kv_a_norm_w=v_kv_a_norm_w, w_ukv=v_w_ukv, q_norm_w=v_q_norm_w, k_norm_w=v_k_norm_w, v_norm_w=v_v_norm_w,
                 w_s=v_w_s, b_s=v_b_s, w_out=v_w_out, norm3_w=v_norm3_w, ffn2_w1=v_ffn2_w1, ffn2_w3=v_ffn2_w3,
                 ffn2_w2=v_ffn2_w2)

    nb, s, d = x.shape
    nctx = ctx.shape[1]
    t, tc = nb * s, nb * nctx
    t_all = t + tc
    sk = s + nctx
    assert nb + 1 <= MOD_ROWS and d % LANES == 0
    tm = _token_tile(s, nctx)

    def held(n, a_):
        return jnp.swapaxes(a_[0], 0, 1) if n in T_WEIGHTS else a_[0]

    def unheld(n, a_):
        return (jnp.swapaxes(a_, 0, 1) if n in T_WEIGHTS else a_)[None]

    shard = {n: held(n, wts[n]).astype(BF16) for n in SHARDED}
    full = {}

    def unshard(names, blocks):
        for n, g4 in zip(names, blocks):
            _, r_, c_ = g4.shape
            if n in ROW_SHARDED or n in T_WEIGHTS:
                full[n] = g4.reshape(N_CHIPS * r_, c_)
            else:
                full[n] = g4.transpose(1, 0, 2).reshape(r_, N_CHIPS * c_)

    def chip_major(n, g_):
        if n in ROW_SHARDED or n in T_WEIGHTS:
            return g_.reshape(N_CHIPS, g_.shape[0] // N_CHIPS, g_.shape[1]).astype(BF16)
        r_, cols = g_.shape
        return g_.reshape(r_, N_CHIPS, cols // N_CHIPS).transpose(1, 0, 2).astype(BF16)

    unshard(FIRST_WEIGHTS, _gather_first([shard[n] for n in FIRST_WEIGHTS]))
    wsb = w_s[0].astype(BF16)
    wcat = wsb.transpose(1, 0, 2).reshape(CHUNK, GROUPS * CHUNK)
    wcat_t = wsb.transpose(2, 0, 1).reshape(CHUNK, GROUPS * CHUNK)
    bias = jnp.repeat(b_s[0].T, GROUP_DIM, axis=1)
    vnw = v_norm_w.reshape(1, GMLP_W)
    lane = jnp.arange(GMLP_W)
    ones = (lane[:, None] // GROUP_DIM == lane[None, :] // GROUP_DIM).astype(BF16)
    qnw = _pad_last(q_norm_w, HEAD_PAD)
    knw = _pad_last(k_norm_w, HEAD_PAD)
    tabs = _rope_tables(s, nctx)

    cc = jnp.concatenate([c, c_ctx[None, :], jnp.zeros((MOD_ROWS - nb - 1, d), F32)], axis=0)
    cc_all = _gather_devices(cc).reshape(8 * MOD_ROWS, d)
    n_ada = shard["w_ada"].shape[1]
    assert n_ada % LANES == 0
    my_chip = 2 * lax.axis_index("x") + lax.axis_index("y")
    b_cols = lax.dynamic_slice_in_dim(b_ada, my_chip * n_ada, n_ada, axis=1)
    mod = _ada_fwd_tp(cc_all, shard["w_ada"], b_cols).transpose(1, 0, 2).reshape(MOD_ROWS, N_MOD, d)
    x_lat, x_ctx = x.reshape(t, d), ctx.reshape(tc, d)
    (xs1, a1, b1, y1), got = _ffn_fwd(x_lat, x_ctx, mod, norm1_w, full["ffn1_w1"], full["ffn1_w3"], full["ffn1_w2"], 0, s,
                                      nb, tm, "ffn1_fwd", exch=("gather", [shard[n] for n in MIX_WEIGHTS]))
    unshard(MIX_WEIGHTS, got)
    wi = full["w_in"]
    wp = jnp.concatenate([wi[0:KV_LORA], jnp.zeros((QK_NOPE, d), BF16), wi[KV_LORA:KV_LORA + QK_ROPE],
                          jnp.zeros((HEAD_PAD - QK_HEAD, d), BF16), wi[KV_LORA + QK_ROPE:]], axis=0)
    wq = jnp.pad(full["w_uq"].reshape(HEADS, QK_HEAD, Q_LORA), ((0, 0), (0, HEAD_PAD - QK_HEAD), (0, 0)))
    wkv = full["w_ukv"].reshape(KV_LORA, HEADS, QK_NOPE + V_HEAD)
    wk = _pad_last(wkv[:, :, :QK_NOPE].transpose(1, 0, 2), HEAD_PAD)
    wv = wkv[:, :, QK_NOPE:].reshape(KV_LORA, HEADS // 2, 2 * V_HEAD).transpose(1, 0, 2)
    h2, proj = _mixin_fwd(xs1, mod, norm2_w, wp, s, nb, tm)
    prep_w = (wq, wk, wv, kv_a_norm_w, q_a_norm_w, qnw, knw)
    q, k_all, v_all = _prep_fwd(proj, 0, nb, s, 0, sk, 0, None, tabs, *prep_w, tm, True, "prep_fwd")
    k_all, v_all = _prep_fwd(proj, t // tm, nb, nctx, s // tm, sk, s // tm, (k_all, v_all), tabs, *prep_w, tm, False,
                             "prep_ctx_fwd")
    o, got = _attn_fwd(q, k_all, v_all, tm, exch=("gather", [shard[n] for n in LAST_WEIGHTS]))
    unshard(LAST_WEIGHTS, got)
    sg = _gmlp_fwd(proj, t, wcat, bias, vnw, ones, tm)
    x2, mix = _mixout_fwd(o, sg, xs1, mod, full["w_out"], s, tm)
    (dy, a2, b2, y2, loss_part), _ = _ffn_fwd(x2, None, mod, norm3_w, full["ffn2_w1"], full["ffn2_w3"], full["ffn2_w2"], 6,
                                              s, nb, tm, "ffn2_fwd", target=loss_target.reshape(t, d))
    loss = lax.psum(loss_part[0, 0], ("x", "y", "c"))

    grads, cm, recv = {}, {}, {}

    def scatter_of(names):
        return ("scatter", [cm[n] for n in names])

    (dx2, h3, g2, da2, db2, dyb2, dmod_c, grads["norm3_w"]), _ = _ffn_bwd(
        dy, x2, None, a2, b2, y2, mod, norm3_w, full["ffn2_w1"], full["ffn2_w3"], full["ffn2_w2"], 6, s, nb, tm,
        "ffn2_bwd")
    cm["ffn2_w1"] = chip_major("ffn2_w1", _mm_tn(da2, h3, t, "ffn2_dw1"))
    cm["ffn2_w3"] = chip_major("ffn2_w3", _mm_tn(db2, h3, t, "ffn2_dw3"))
    cm["ffn2_w2"] = chip_major("ffn2_w2", _mm_tn(g2, dyb2, t, "ffn2_dw2"))
    dmix, do, dsg, dmod_b = _mixout_bwd(dx2, mix, mod, full["w_out"], s, tm)
    cm["w_out"] = chip_major("w_out", jnp.concatenate([_mm_tn(o, dmix, t, "wout_dw_attn"),
                                                       _mm_tn(sg, dmix, t, "wout_dw_gmlp")], axis=0))
    dpu, dpv, dws, dbs, dvnw = _gmlp_bwd(proj, dsg, wcat, wcat_t, bias, vnw, ones, tm)
    group = LAST_WEIGHTS + ("w_out",)
    (dq, dk, dv), got = _attn_bwd(q, k_all, v_all, do, tm, exch=scatter_of(group))
    recv.update(zip(group, got))
    dp0, dwk_c, dwv_c, dkvaw_c, dknw_c = _prep_bwd(
        proj, t // tm, nb, nctx, s // tm, s // tm, t_all, None, tabs, *prep_w, None, dk, dv, None, tm, "prep_ctx_bwd")
    dp0, dwq, dqaw, dqnw, dwk, dwv, dkvaw, dknw = _prep_bwd(
        proj, 0, nb, s, 0, 0, t_all, dp0, tabs, *prep_w, dq, dk, dv, [dwk_c, dwv_c, dkvaw_c, dknw_c], tm, "prep_bwd")
    dxs1, dmod_a, grads["norm2_w"] = _mixin_bwd(dp0, dpu, dpv, xs1, dx2, mod, norm2_w, wp, s, nb, tm)
    dwp = jnp.concatenate([_mm_tn(dp0, h2, t_all, "win_dw_kvq"), _mm_tn(dpu, h2, t, "win_dw_u"),
                           _mm_tn(dpv, h2, t, "win_dw_v")], axis=0)
    cm["w_in"] = chip_major("w_in", jnp.concatenate(
        [dwp[0:KV_LORA], dwp[KV_LORA + QK_NOPE:KV_LORA + QK_HEAD], dwp[256:]], axis=0))
    cm["w_uq"] = chip_major("w_uq", dwq[:, :QK_HEAD, :].reshape(HEADS * QK_HEAD, Q_LORA))
    cm["w_ukv"] = chip_major("w_ukv", jnp.concatenate(
        [dwk[:, :, :QK_NOPE].transpose(1, 0, 2),
         dwv.transpose(1, 0, 2).reshape(KV_LORA, HEADS, V_HEAD)], axis=2).reshape(KV_LORA, HEADS * (QK_NOPE + V_HEAD)))
    (dx_lat, h1, g1, da1, db1, dyb1, dmod_0, grads["norm1_w"]), _ = _ffn_bwd(
        dxs1, x_lat, x_ctx, a1, b1, y1, mod, norm1_w, full["ffn1_w1"], full["ffn1_w3"], full["ffn1_w2"], 0, s, nb, tm,
        "ffn1_bwd")
    dmods = [m_.reshape(MOD_ROWS, N_MOD * d) for m_ in (dmod_0, dmod_a, dmod_b, dmod_c)]
    dw_ada, grads["b_ada"], dctx = _ada_bwd_tp(cc_all, dmods, shard["w_ada"], nb)
    grads["c_ctx"] = dctx[0]
    grads["q_a_norm_w"], grads["kv_a_norm_w"] = dqaw, dkvaw
    grads["q_norm_w"], grads["k_norm_w"] = dqnw[:, :QK_HEAD], dknw[:, :QK_HEAD]
    grads["v_norm_w"], grads["w_s"], grads["b_s"] = dvnw, dws, dbs[:, 0]
    grad_x = dx_lat.reshape(nb, s, d)
    rows_s = _round_up(-(-sum(wts[n].size for n in SMALL) // d), 16)
    cm["small"] = jnp.broadcast_to(_flat_rows([grads[n] for n in SMALL], rows_s, d), (N_CHIPS, rows_s, d))
    group = ("w_in", "w_uq", "w_ukv")
    dw2, got = _mm_tn(g1, dyb1, t_all, "ffn1_dw2", exch=scatter_of(group))
    recv.update(zip(group, got))
    cm["ffn1_w2"] = chip_major("ffn1_w2", dw2)
    dw1, got = _mm_tn(da1, h1, t_all, "ffn1_dw1", exch=scatter_of(("ffn1_w2",)))
    recv["ffn1_w2"] = got[0]
    cm["ffn1_w1"] = chip_major("ffn1_w1", dw1)
    group = ("ffn1_w1", "small")
    dw3, got = _mm_tn(db1, h1, t_all, "ffn1_dw3", exch=scatter_of(group))
    recv.update(zip(group, got))
    cm["ffn1_w3"] = chip_major("ffn1_w3", dw3)
    stepped = {}
    stepped["w_ada"], got = _adamw([dw_ada], wts["w_ada"][0], moms["w_ada"][0], vars_["w_ada"][0], "adamw_w_ada",
                                   exch=scatter_of(("ffn1_w3",)))
    recv["ffn1_w3"] = got[0]

    reduced = tuple(n for n in SHARDED if n != "w_ada") + ("small",)
    part = {n: _sum_slots(recv[n], "sum_" + n) for n in reduced}
    early = LAST_WEIGHTS + ("w_out",)
    late = tuple(n for n in reduced if n not in early)
    sib = dict(zip(early, _swap_cores([part[n] for n in early], "swap_early")))
    sib.update(zip(late, _swap_cores([part[n] for n in late], "swap_late")))
    for n in reduced[:-1]:
        stepped[n], _ = _adamw([part[n], sib[n]], held(n, wts[n]), held(n, moms[n]), held(n, vars_[n]), "adamw_" + n)
    for n in SHARDED:
        stepped[n] = [unheld(n, a_) for a_ in stepped[n]]
    packed, _ = _adamw([part["small"], sib["small"]], _flat_rows([wts[n] for n in SMALL], rows_s, d),
                       _flat_rows([moms[n] for n in SMALL], rows_s, d), _flat_rows([vars_[n] for n in SMALL], rows_s, d),
                       "adamw_small")
    for n in SMALL:
        stepped[n] = []
    for a_ in packed:
        flat = a_.reshape(-1)
        off = 0
        for n in SMALL:
            stepped[n].append(flat[off:off + wts[n].size].reshape(wts[n].shape))
            off += wts[n].size
    return (loss, grad_x, *[stepped[n][0] for n in WEIGHTS], *[stepped[n][1] for n in WEIGHTS],
            *[stepped[n][2] for n in WEIGHTS], *[stepped[n][3] for n in WEIGHTS])
```

```python
import functools
import math

import jax
import jax.numpy as jnp
import numpy as np
from jax import lax
from jax.experimental import pallas as pl
from jax.experimental.pallas import tpu as pltpu

F32 = jnp.float32
BF16 = jnp.bfloat16

EPS = 1e-6
N_MOD = 9
HEADS = 8
QK_NOPE, QK_ROPE, V_HEAD = 64, 32, 64
QK_HEAD = QK_NOPE + QK_ROPE
HEAD_PAD = 128
SOFTMAX_SCALE = QK_HEAD ** -0.5
Q_LORA, KV_LORA = 256, 128
GROUPS, GROUP_DIM, CHUNK = 8, 64, 128
GMLP_W = GROUPS * GROUP_DIM
MLA_W = HEADS * V_HEAD
IN_COLS = 1440
PROJ_COLS = 1536
GRID_W = 64
ROPE_BASE = 10000.0
MOD_ROWS = 16
ADAM_LR, ADAM_B1, ADAM_B2, ADAM_EPS, ADAM_WD, ADAM_STEP = 0.001, 0.9, 0.999, 1e-08, 0.01, 10
N_CHIPS = 4
LANES = 128
V7X_VMEM_LIMIT = 56 * 1024 * 1024
GELU_C = math.sqrt(2.0 / math.pi)

SHARDED = ("w_ada", "ffn1_w1", "ffn1_w3", "ffn1_w2", "w_in", "w_uq", "w_ukv", "w_out", "ffn2_w1", "ffn2_w3", "ffn2_w2")
ROW_SHARDED = ("ffn1_w2", "w_out", "ffn2_w2")
T_WEIGHTS = ("ffn1_w1", "ffn1_w3", "ffn2_w1", "ffn2_w3", "w_in", "w_uq")
FIRST_WEIGHTS = ("ffn1_w1", "ffn1_w3", "ffn1_w2")
MIX_WEIGHTS = ("w_in", "w_uq", "w_ukv", "w_out")
LAST_WEIGHTS = ("ffn2_w1", "ffn2_w3", "ffn2_w2")
SMALL = ("c_ctx", "b_ada", "norm1_w", "norm2_w", "q_a_norm_w", "kv_a_norm_w", "q_norm_w", "k_norm_w", "v_norm_w",
         "w_s", "b_s", "norm3_w")
WEIGHTS = ("c_ctx", "w_ada", "b_ada", "norm1_w", "ffn1_w1", "ffn1_w3", "ffn1_w2", "norm2_w", "w_in", "q_a_norm_w",
           "w_uq", "kv_a_norm_w", "w_ukv", "q_norm_w", "k_norm_w", "v_norm_w", "w_s", "b_s", "w_out", "norm3_w",
           "ffn2_w1", "ffn2_w3", "ffn2_w2")


def _round_up(n, m):
    return (n + m - 1) // m * m


def _div_tile(n, target, mult):
    best = None
    for t in range(mult, min(n, target) + 1, mult):
        if n % t == 0:
            best = t
    return n if best is None else best


def _dot(a, b):
    return lax.dot_general(a, b, (((1,), (0,)), ((), ())), preferred_element_type=F32)


def _dot_nt(a, b):
    return lax.dot_general(a, b, (((1,), (1,)), ((), ())), preferred_element_type=F32)


def _dot_tn(a, b):
    return lax.dot_general(a, b, (((0,), (0,)), ((), ())), preferred_element_type=F32)


def _sigmoid(x):
    return 1.0 / (1.0 + jnp.exp(-x))


def _gelu(x):
    return 0.5 * x * (1.0 + jnp.tanh(GELU_C * (x + 0.044715 * x * x * x)))


def _gelu_grad(x):
    t = jnp.tanh(GELU_C * (x + 0.044715 * x * x * x))
    return 0.5 * (1.0 + t) + 0.5 * x * (1.0 - t * t) * (GELU_C * (1.0 + 3 * 0.044715 * x * x))


def _rope(x, cos, sin_a, sin_b):
    return x * cos + pltpu.roll(x, 8, 1) * sin_a + pltpu.roll(x, HEAD_PAD - 8, 1) * sin_b


def _rope_t(d, cos, sin_a, sin_b):
    return d * cos + pltpu.roll(d * sin_a, HEAD_PAD - 8, 1) + pltpu.roll(d * sin_b, 8, 1)


def _rope3(x, cos, sin_a, sin_b):
    return x * cos + pltpu.roll(x, 8, 2) * sin_a + pltpu.roll(x, HEAD_PAD - 8, 2) * sin_b


def _rope3_t(d, cos, sin_a, sin_b):
    return d * cos + pltpu.roll(d * sin_a, HEAD_PAD - 8, 2) + pltpu.roll(d * sin_b, 8, 2)


def _group_sum(x, ones_ref):
    hi = x.astype(BF16)
    lo = (x - hi.astype(F32)).astype(BF16)
    return _dot(hi, ones_ref[...]) + _dot(lo, ones_ref[...])


def _params(n_axes):
    return pltpu.CompilerParams(dimension_semantics=("arbitrary",) * n_axes, vmem_limit_bytes=V7X_VMEM_LIMIT)


def _whole(shape):
    nd = len(shape)
    return pl.BlockSpec(shape, lambda *_: (0,) * nd, pipeline_mode=pl.Buffered(1))


def _sds(shape, dtype):
    return jax.ShapeDtypeStruct(shape, dtype)


def _token_tile(s, ctx):
    return _div_tile(math.gcd(s, ctx), 256, CHUNK)


def _other_chips(x, y):
    return [(1 - x, y), (x, 1 - y), (1 - x, 1 - y)]


def _exch_copies(kind, srcs, dsts, send_sems, recv_sems, local_sems):
    x, y, c = lax.axis_index("x"), lax.axis_index("y"), lax.axis_index("c")
    me = 2 * x + y
    local, sends, arrivals = [], [], []
    for w, (src, dst) in enumerate(zip(srcs, dsts)):
        own = src if kind == "gather" else src.at[me]
        local.append(pltpu.make_async_copy(own, dst.at[me], local_sems.at[w]))
        for k, (px, py) in enumerate(_other_chips(x, y)):
            sem = dict(send_sem=send_sems.at[3 * w + k], recv_sem=recv_sems.at[3 * w + k], device_id=(px, py, c),
                       device_id_type=pl.DeviceIdType.MESH)
            out = src if kind == "gather" else src.at[2 * px + py]
            sends.append(pltpu.make_async_remote_copy(src_ref=out, dst_ref=dst.at[me], **sem))
            arrivals.append(pltpu.make_async_remote_copy(src_ref=own, dst_ref=dst.at[2 * px + py], **sem))
    return local, sends, arrivals


def _exch_start(kind, srcs, dsts, sems):
    local, sends, _ = _exch_copies(kind, srcs, dsts, *sems)
    for cp in local + sends:
        cp.start()


def _exch_wait(kind, srcs, dsts, sems):
    local, sends, arrivals = _exch_copies(kind, srcs, dsts, *sems)
    for cp in arrivals:
        cp.wait_recv()
    for cp in sends:
        cp.wait_send()
    for cp in local:
        cp.wait()


def _exch_scratch(n):
    return [pltpu.SemaphoreType.DMA((3 * n,)), pltpu.SemaphoreType.DMA((3 * n,)), pltpu.SemaphoreType.DMA((n,))]


def _exch_shapes(kind, arrays):
    return [_sds((N_CHIPS,) + a.shape if kind == "gather" else a.shape, a.dtype) for a in arrays]


def _hosted_call(body, name, grid, in_specs, out_specs, out_shape, operands, scratch=(), exch=None):
    n_axes = len(grid)
    if exch is None:
        outs = pl.pallas_call(body, name=name, grid=grid, in_specs=list(in_specs), out_specs=list(out_specs),
                              out_shape=list(out_shape), scratch_shapes=list(scratch),
                              compiler_params=_params(n_axes))(*operands)
        return list(outs), []
    kind, arrays = exch
    n_in, n_out, n_sc, n_ex = len(in_specs), len(out_specs), len(scratch), len(arrays)

    def hosted(*refs):
        cin, ein = refs[:n_in], refs[n_in:n_in + n_ex]
        o0 = n_in + n_ex
        cout, eout = refs[o0:o0 + n_out], refs[o0 + n_out:o0 + n_out + n_ex]
        rest = refs[o0 + n_out + n_ex:]
        csc, sems = rest[:n_sc], rest[n_sc:]
        first = functools.reduce(jnp.logical_and, [pl.program_id(a) == 0 for a in range(n_axes)])
        last = functools.reduce(jnp.logical_and, [pl.program_id(a) == grid[a] - 1 for a in range(n_axes)])

        @pl.when(first)
        def _():
            _exch_start(kind, ein, eout, sems)

        body(*cin, *cout, *csc)

        @pl.when(last)
        def _():
            _exch_wait(kind, ein, eout, sems)

    any_spec = pl.BlockSpec(memory_space=pl.ANY)
    outs = pl.pallas_call(
        hosted, name=name, grid=grid, in_specs=list(in_specs) + [any_spec] * n_ex,
        out_specs=list(out_specs) + [any_spec] * n_ex, out_shape=list(out_shape) + _exch_shapes(kind, arrays),
        scratch_shapes=list(scratch) + _exch_scratch(n_ex), compiler_params=_params(n_axes),
    )(*operands, *arrays)
    return list(outs[:n_out]), list(outs[n_out:])


class _TokenTiles:
    def __init__(self, t, tc, tm):
        self.n_lat, self.n_ctx = t // tm, tc // tm
        self.n_all = self.n_lat + self.n_ctx

    def tile(self, i):
        return (i + self.n_lat) % self.n_all if self.n_ctx else i

    def is_lat(self, i):
        return self.tile(i) < self.n_lat

    def row(self, i):
        return (self.tile(i), 0)

    def lat_row(self, i):
        return (jnp.where(self.is_lat(i), self.tile(i), 0), 0) if self.n_ctx else (i, 0)

    def ctx_row(self, i):
        return (jnp.where(self.is_lat(i), self.n_ctx - 1, self.tile(i) - self.n_lat), 0)


def _ffn_fwd(x_lat, x_ctx, mod, nw, w1, w3, w2, k0, s, nb, tm, name, target=None, exch=None):
    t, d = x_lat.shape
    tc = 0 if x_ctx is None else x_ctx.shape[0]
    f = w1.shape[0]
    tiles = _TokenTiles(t, tc, tm)
    n_x = 2 if tc else 1
    n_t = 0 if target is None else 1
    assert not (tc and n_t)

    def body(*refs):
        x_ref = refs[0]
        t_ref = refs[n_x] if n_t else None
        mod_ref, nw_ref, w1_ref, w3_ref, w2_ref, o_ref, a_ref, b_ref, y_ref = refs[n_x + n_t:n_x + n_t + 9]
        i = pl.program_id(0)
        g = jnp.minimum((tiles.tile(i) * tm) // s, nb)
        shift = mod_ref[g, pl.ds(k0, 1), :]
        scale = mod_ref[g, pl.ds(k0 + 1, 1), :]
        gate = mod_ref[g, pl.ds(k0 + 2, 1), :]
        x = jnp.where(tiles.is_lat(i), x_ref[...], refs[1][...]) if tc else x_ref[...]
        r = lax.rsqrt(jnp.mean(x * x, axis=-1, keepdims=True) + EPS)
        hb = ((x * r * nw_ref[...]) * (1.0 + scale) + shift).astype(BF16)
        a = _dot_nt(hb, w1_ref[...])
        b = _dot_nt(hb, w3_ref[...])
        gb = (a * _sigmoid(a) * b).astype(BF16)
        y = _dot(gb, w2_ref[...])
        out = x + (0.5 * gate) * y
        a_ref[...] = a.astype(BF16)
        b_ref[...] = b.astype(BF16)
        y_ref[...] = y.astype(BF16)
        if n_t:
            loss_ref, acc_ref = refs[-2:]

            @pl.when(i == 0)
            def _():
                acc_ref[...] = jnp.zeros_like(acc_ref)

            e = out - t_ref[...]
            o_ref[...] = e * (1.0 / d)
            acc_ref[...] += jnp.sum(e * e, axis=0, keepdims=True)

            @pl.when(i == tiles.n_all - 1)
            def _():
                loss_ref[...] = (0.5 / d) * jnp.sum(acc_ref[...], axis=-1, keepdims=True)
        else:
            o_ref[...] = out

    td = pl.BlockSpec((tm, d), tiles.row)
    tf = pl.BlockSpec((tm, f), tiles.row)
    return _hosted_call(
        body, name, (tiles.n_all,),
        [pl.BlockSpec((tm, d), tiles.lat_row)] + ([pl.BlockSpec((tm, d), tiles.ctx_row)] if tc else []) + [td] * n_t
        + [_whole(mod.shape), _whole(nw.shape), _whole(w1.shape), _whole(w3.shape), _whole(w2.shape)],
        [td, tf, tf, td] + [pl.BlockSpec((1, 1), lambda i: (0, 0))] * n_t,
        [_sds((t + tc, d), F32), _sds((t + tc, f), BF16), _sds((t + tc, f), BF16), _sds((t + tc, d), BF16)]
        + [_sds((1, 1), F32)] * n_t,
        (x_lat,) + ((x_ctx,) if tc else ()) + ((target,) if n_t else ()) + (mod, nw, w1, w3, w2),
        scratch=[pltpu.VMEM((1, d), F32)] * n_t, exch=exch)


def _ffn_bwd(dout, x_lat, x_ctx, a, b, y, mod, nw, w1, w3, w2, k0, s, nb, tm, name, exch=None):
    t, d = x_lat.shape
    tc = 0 if x_ctx is None else x_ctx.shape[0]
    f = w1.shape[0]
    nch = 2 if (f // 2) % LANES == 0 and f % 2 == 0 else 1
    fc = f // nch
    tiles = _TokenTiles(t, tc, tm)
    n_x = 2 if tc else 1

    def body(*refs):
        do_ref, x_ref = refs[0], refs[1]
        (a_ref, b_ref, y_ref, mod_ref, nw_ref, w1_ref, w3_ref, w2_ref,
         dx_ref, h_ref, g_ref, da_ref, db_ref, dy_ref, dmod_ref, dnw_ref) = refs[1 + n_x:]
        i = pl.program_id(0)

        @pl.when(i == 0)
        def _():
            dmod_ref[...] = jnp.zeros_like(dmod_ref)
            dnw_ref[...] = jnp.zeros_like(dnw_ref)

        g = jnp.minimum((tiles.tile(i) * tm) // s, nb)
        shift = mod_ref[g, pl.ds(k0, 1), :]
        scale = mod_ref[g, pl.ds(k0 + 1, 1), :]
        gate = mod_ref[g, pl.ds(k0 + 2, 1), :]
        x = jnp.where(tiles.is_lat(i), x_ref[...], refs[2][...]) if tc else x_ref[...]
        dout_v = do_ref[...]
        r = lax.rsqrt(jnp.mean(x * x, axis=-1, keepdims=True) + EPS)
        xh = x * r
        n = xh * nw_ref[...]
        h_ref[...] = (n * (1.0 + scale) + shift).astype(BF16)
        dyb = ((0.5 * gate) * dout_v).astype(BF16)
        dy_ref[...] = dyb
        dmod_ref[g, pl.ds(k0 + 2, 1), :] += 0.5 * jnp.sum(dout_v * y_ref[...].astype(F32), axis=0, keepdims=True)
        dh = jnp.zeros((tm, d), F32)
        for c in range(nch):
            sl = slice(c * fc, (c + 1) * fc)
            dg = _dot_nt(dyb, w2_ref[sl, :])
            av = a_ref[:, sl].astype(F32)
            bv = b_ref[:, sl].astype(F32)
            sig = _sigmoid(av)
            sa = av * sig
            g_ref[:, sl] = (sa * bv).astype(BF16)
            dab = (dg * bv * (sig * (1.0 + av * (1.0 - sig)))).astype(BF16)
            dbb = (dg * sa).astype(BF16)
            da_ref[:, sl] = dab
            db_ref[:, sl] = dbb
            dh = dh + _dot(dab, w1_ref[sl, :]) + _dot(dbb, w3_ref[sl, :])
        dmod_ref[g, pl.ds(k0, 1), :] += jnp.sum(dh, axis=0, keepdims=True)
        dmod_ref[g, pl.ds(k0 + 1, 1), :] += jnp.sum(dh * n, axis=0, keepdims=True)
        dn = dh * (1.0 + scale)
        dnw_ref[...] += jnp.sum(dn * xh, axis=0, keepdims=True)
        dxh = dn * nw_ref[...]
        dx_ref[...] = dout_v + r * (dxh - xh * jnp.mean(dxh * xh, axis=-1, keepdims=True))

    td = pl.BlockSpec((tm, d), tiles.row)
    tf = pl.BlockSpec((tm, f), tiles.row)
    lat = pl.BlockSpec((tm, d), tiles.lat_row)
    ta = t + tc
    return _hosted_call(
        body, name, (tiles.n_all,),
        [td, lat] + ([pl.BlockSpec((tm, d), tiles.ctx_row)] if tc else [])
        + [tf, tf, td, _whole(mod.shape), _whole(nw.shape), _whole(w1.shape), _whole(w3.shape), _whole(w2.shape)],
        [lat, td, tf, tf, tf, td, pl.BlockSpec(mod.shape, lambda i: (0, 0, 0)), pl.BlockSpec((1, d), lambda i: (0, 0))],
        [_sds((t, d), F32), _sds((ta, d), BF16), _sds((ta, f), BF16), _sds((ta, f), BF16), _sds((ta, f), BF16),
         _sds((ta, d), BF16), _sds(mod.shape, F32), _sds((1, d), F32)],
        (dout, x_lat) + ((x_ctx,) if tc else ()) + (a, b, y, mod, nw, w1, w3, w2), exch=exch)


def _mm_tn(a, b, rows, name, exch=None):
    m = a.shape[1]
    n = b.shape[1]
    bm = _div_tile(m, 1408, LANES)
    bn = _div_tile(n, 1408, LANES)
    bk = _div_tile(rows, 2304, LANES)
    nk = rows // bk

    def body(a_ref, b_ref, o_ref, acc_ref):
        k = pl.program_id(2)

        @pl.when(k == 0)
        def _():
            acc_ref[...] = jnp.zeros_like(acc_ref)

        acc_ref[...] += _dot_tn(a_ref[...], b_ref[...])

        @pl.when(k == nk - 1)
        def _():
            o_ref[...] = acc_ref[...].astype(BF16)

    (out,), got = _hosted_call(
        body, name, (m // bm, n // bn, nk),
        [pl.BlockSpec((bk, bm), lambda i, j, k: (k, i)), pl.BlockSpec((bk, bn), lambda i, j, k: (k, j))],
        [pl.BlockSpec((bm, bn), lambda i, j, k: (i, j))], [_sds((m, n), BF16)], (a, b),
        scratch=[pltpu.VMEM((bm, bn), F32)], exch=exch)
    return out if exch is None else (out, got)


def _mixin_fwd(xs, mod, nw, wp, s, nb, tm):
    t, d = xs.shape

    def body(x_ref, mod_ref, nw_ref, wp_ref, h_ref, p_ref):
        g = jnp.minimum((pl.program_id(0) * tm) // s, nb)
        shift = mod_ref[g, pl.ds(3, 1), :]
        scale = mod_ref[g, pl.ds(4, 1), :]
        x = x_ref[...]
        r = lax.rsqrt(jnp.mean(x * x, axis=-1, keepdims=True) + EPS)
        hb = ((x * r * nw_ref[...]) * (1.0 + scale) + shift).astype(BF16)
        h_ref[...] = hb
        p_ref[...] = _dot_nt(hb, wp_ref[...])

    row = lambda i: (i, 0)
    return pl.pallas_call(
        body, name="mixin_fwd", grid=(t // tm,),
        in_specs=[pl.BlockSpec((tm, d), row), _whole(mod.shape), _whole(nw.shape), _whole(wp.shape)],
        out_specs=[pl.BlockSpec((tm, d), row), pl.BlockSpec((tm, PROJ_COLS), row)],
        out_shape=[_sds((t, d), BF16), _sds((t, PROJ_COLS), F32)], compiler_params=_params(1),
    )(xs, mod, nw, wp)


def _mixin_bwd(dp0, dpu, dpv, xs, dres, mod, nw, wp, s, nb, tm):
    t_all, d = xs.shape
    nlat = dres.shape[0] // tm

    def body(p0_ref, pu_ref, pv_ref, x_ref, dr_ref, mod_ref, nw_ref, wp_ref, dx_ref, dmod_ref, dnw_ref):
        i = pl.program_id(0)

        @pl.when(i == 0)
        def _():
            dmod_ref[...] = jnp.zeros_like(dmod_ref)
            dnw_ref[...] = jnp.zeros_like(dnw_ref)

        lat = i < nlat
        g = jnp.minimum((i * tm) // s, nb)
        scale = mod_ref[g, pl.ds(4, 1), :]
        dh = _dot(p0_ref[...], wp_ref[0:512, :])
        extra = _dot(pu_ref[...], wp_ref[512:1024, :]) + _dot(pv_ref[...], wp_ref[1024:1536, :])
        dh = dh + jnp.where(lat, extra, 0.0)
        x = x_ref[...]
        r = lax.rsqrt(jnp.mean(x * x, axis=-1, keepdims=True) + EPS)
        xh = x * r
        n = xh * nw_ref[...]
        dmod_ref[g, pl.ds(3, 1), :] += jnp.sum(dh, axis=0, keepdims=True)
        dmod_ref[g, pl.ds(4, 1), :] += jnp.sum(dh * n, axis=0, keepdims=True)
        dn = dh * (1.0 + scale)
        dnw_ref[...] += jnp.sum(dn * xh, axis=0, keepdims=True)
        dxh = dn * nw_ref[...]
        dx_ref[...] = jnp.where(lat, dr_ref[...], 0.0) + r * (dxh - xh * jnp.mean(dxh * xh, axis=-1, keepdims=True))

    row = lambda i: (i, 0)
    lrow = lambda i: (jnp.minimum(i, nlat - 1), 0)
    return pl.pallas_call(
        body, name="mixin_bwd", grid=(t_all // tm,),
        in_specs=[pl.BlockSpec((tm, 512), row), pl.BlockSpec((tm, 512), lrow), pl.BlockSpec((tm, 512), lrow),
                  pl.BlockSpec((tm, d), row), pl.BlockSpec((tm, d), lrow), _whole(mod.shape), _whole(nw.shape),
                  _whole(wp.shape)],
        out_specs=[pl.BlockSpec((tm, d), row), pl.BlockSpec(mod.shape, lambda i: (0, 0, 0)),
                   pl.BlockSpec((1, d), lambda i: (0, 0))],
        out_shape=[_sds((t_all, d), F32), _sds(mod.shape, F32), _sds((1, d), F32)], compiler_params=_params(1),
    )(dp0, dpu, dpv, xs, dres, mod, nw, wp)


def _prep_fwd(proj, row0, nb, s, pos0, sk, key0, into, tabs, wq, wk, wv, kvaw, qaw, qnw, knw, tm, with_q, name):
    nblk = s // tm
    n_into = 0 if into is None else 2

    def body(p_ref, cos_ref, sa_ref, sb_ref, wq_ref, wk_ref, wv_ref, kvaw_ref, qaw_ref, qnw_ref, knw_ref, *rest):
        outs, heads_ref = rest[n_into:-1], rest[-1]
        q_ref, k_ref, v_ref = outs if with_q else (None,) + outs
        cos, sin_a, sin_b = cos_ref[...][None], sa_ref[...][None], sb_ref[...][None]

        def normed_roped(w_ref, src, extra, nw_ref, o_ref, post):
            for h in range(HEADS):
                heads_ref[h] = _dot_nt(src, w_ref[h]) if extra is None else _dot(src, w_ref[h])
            xp = heads_ref[...] if extra is None else heads_ref[...] + extra[None]
            r = lax.rsqrt(jnp.sum(xp * xp, axis=-1, keepdims=True) * (1.0 / QK_HEAD) + EPS)
            o_ref[...] = _rope3(xp * r * (nw_ref[...] * post)[None], cos, sin_a, sin_b).astype(BF16)

        ckv = p_ref[:, 0:128]
        rkv = lax.rsqrt(jnp.mean(ckv * ckv, axis=-1, keepdims=True) + EPS)
        ckvb = (ckv * rkv * kvaw_ref[...]).astype(BF16)
        normed_roped(wk_ref, ckvb, p_ref[:, 128:256], knw_ref, k_ref, 1.0)
        for j in range(HEADS // 2):
            v_ref[j] = _dot(ckvb, wv_ref[j]).astype(BF16)
        if with_q:
            cq = p_ref[:, 256:512]
            rq = lax.rsqrt(jnp.mean(cq * cq, axis=-1, keepdims=True) + EPS)
            normed_roped(wq_ref, (cq * rq * qaw_ref[...]).astype(BF16), None, qnw_ref, q_ref, SOFTMAX_SCALE)

    tab = pl.BlockSpec((tm, HEAD_PAD), lambda i: (pos0 + i % nblk, 0))
    qspec = pl.BlockSpec((None, HEADS, tm, HEAD_PAD), lambda i: (i // nblk, 0, i % nblk, 0))
    kspec = pl.BlockSpec((None, HEADS, tm, HEAD_PAD), lambda i: (i // nblk, 0, key0 + i % nblk, 0))
    vspec = pl.BlockSpec((None, HEADS // 2, tm, HEAD_PAD), lambda i: (i // nblk, 0, key0 + i % nblk, 0))
    qshape = _sds((nb, HEADS, s, HEAD_PAD), BF16)
    kshape = _sds((nb, HEADS, sk, HEAD_PAD), BF16)
    vshape = _sds((nb, HEADS // 2, sk, HEAD_PAD), BF16)
    n_q = 1 if with_q else 0
    return pl.pallas_call(
        body, name=name, grid=(nb * nblk,),
        in_specs=[pl.BlockSpec((tm, 512), lambda i: (row0 + i, 0)), tab, tab, tab, _whole(wq.shape), _whole(wk.shape),
                  _whole(wv.shape), _whole(kvaw.shape), _whole(qaw.shape), _whole(qnw.shape), _whole(knw.shape)]
        + [pl.BlockSpec(memory_space=pl.ANY)] * n_into,
        out_specs=([qspec] if with_q else []) + [kspec, vspec],
        out_shape=([qshape] if with_q else []) + [kshape, vshape],
        scratch_shapes=[pltpu.VMEM((HEADS, tm, HEAD_PAD), F32)],
        input_output_aliases={11: n_q, 12: n_q + 1} if n_into else {}, compiler_params=_params(1),
    )(proj, *tabs, wq, wk, wv, kvaw, qaw, qnw, knw, *(into or ()))


def _prep_bwd(proj, row0, nb, s, pos0, key0, dp_rows, dp_into, tabs, wq, wk, wv, kvaw, qaw, qnw, knw, dq, dk, dv, init, tm,
              name):
    nblk = s // tm
    with_q = dq is not None
    n_init = 0 if init is None else len(init)
    n_into = 0 if dp_into is None else 1

    def body(*refs):
        p_ref, cos_ref, sa_ref, sb_ref, wq_ref, wk_ref, wv_ref, kvaw_ref, qaw_ref, qnw_ref, knw_ref = refs[:11]
        rest = list(refs[11:])
        dq_ref = rest.pop(0) if with_q else None
        dk_ref, dv_ref = rest.pop(0), rest.pop(0)
        init_refs = [rest.pop(0) for _ in range(n_init)]
        if n_into:
            rest.pop(0)
        dp_ref = rest.pop(0)
        if with_q:
            dwq_ref, dqaw_ref, dqnw_ref = rest.pop(0), rest.pop(0), rest.pop(0)
        dwk_ref, dwv_ref, dkvaw_ref, dknw_ref, heads_ref, dhb_ref = rest
        accs = [dwk_ref, dwv_ref, dkvaw_ref, dknw_ref]

        @pl.when(pl.program_id(0) == 0)
        def _():
            for k, acc in enumerate(accs):
                acc[...] = init_refs[k][...] if n_init else jnp.zeros_like(acc)
            if with_q:
                dwq_ref[...] = jnp.zeros_like(dwq_ref)
                dqaw_ref[...] = jnp.zeros_like(dqaw_ref)
                dqnw_ref[...] = jnp.zeros_like(dqnw_ref)

        cos, sin_a, sin_b = cos_ref[...][None], sa_ref[...][None], sb_ref[...][None]
        lane = lax.broadcasted_iota(jnp.int32, (tm, HEAD_PAD), 1)
        rope_lanes = (lane >= QK_NOPE) & (lane < QK_HEAD)

        def heads_bwd(w_ref, src, extra, nw_ref, d_ref, dnw_ref, dw_ref, post):
            w_t = extra is None
            for h in range(HEADS):
                heads_ref[h] = _dot_nt(src, w_ref[h]) if w_t else _dot(src, w_ref[h])
            xp = heads_ref[...] if extra is None else heads_ref[...] + extra[None]
            r = lax.rsqrt(jnp.sum(xp * xp, axis=-1, keepdims=True) * (1.0 / QK_HEAD) + EPS)
            xh = xp * r
            dn = _rope3_t(d_ref[...], cos, sin_a, sin_b)
            dnw_ref[...] += post * jnp.sum(jnp.sum(dn * xh, axis=0), axis=0, keepdims=True)
            dxh = dn * (nw_ref[...] * post)[None]
            dxp = r * (dxh - xh * (jnp.sum(dxh * xh, axis=-1, keepdims=True) * (1.0 / QK_HEAD)))
            dhb_ref[...] = dxp.astype(BF16)
            dsrc = jnp.zeros((tm, src.shape[1]), F32)
            for h in range(HEADS):
                dsrc = dsrc + (_dot(dhb_ref[h], w_ref[h]) if w_t else _dot_nt(dhb_ref[h], w_ref[h]))
                dw_ref[h] += _dot_tn(src, dhb_ref[h])
            return dsrc, jnp.sum(dxp, axis=0)

        ckv = p_ref[:, 0:128]
        rkv = lax.rsqrt(jnp.mean(ckv * ckv, axis=-1, keepdims=True) + EPS)
        ckvh = ckv * rkv
        ckvb = (ckvh * kvaw_ref[...]).astype(BF16)
        dckv, dkp_sum = heads_bwd(wk_ref, ckvb, p_ref[:, 128:256], knw_ref, dk_ref, dknw_ref, dwk_ref, 1.0)
        for j in range(HEADS // 2):
            dvb = dv_ref[j].astype(BF16)
            dckv = dckv + _dot_nt(dvb, wv_ref[j])
            dwv_ref[j] += _dot_tn(ckvb, dvb)
        dkvaw_ref[...] += jnp.sum(dckv * ckvh, axis=0, keepdims=True)
        dch = dckv * kvaw_ref[...]
        dp_ref[:, 0:128] = (rkv * (dch - ckvh * jnp.mean(dch * ckvh, axis=-1, keepdims=True))).astype(BF16)
        dp_ref[:, 128:256] = jnp.where(rope_lanes, dkp_sum, 0.0).astype(BF16)
        if with_q:
            cq = p_ref[:, 256:512]
            rq = lax.rsqrt(jnp.mean(cq * cq, axis=-1, keepdims=True) + EPS)
            cqh = cq * rq
            cqb = (cqh * qaw_ref[...]).astype(BF16)
            dcq, _ = heads_bwd(wq_ref, cqb, None, qnw_ref, dq_ref, dqnw_ref, dwq_ref, SOFTMAX_SCALE)
            dqaw_ref[...] += jnp.sum(dcq * cqh, axis=0, keepdims=True)
            dqc = dcq * qaw_ref[...]
            dp_ref[:, 256:512] = (rq * (dqc - cqh * jnp.mean(dqc * cqh, axis=-1, keepdims=True))).astype(BF16)
        else:
            dp_ref[:, 256:512] = jnp.zeros((tm, Q_LORA), BF16)

    tab = pl.BlockSpec((tm, HEAD_PAD), lambda i: (pos0 + i % nblk, 0))
    qspec = pl.BlockSpec((None, HEADS, tm, HEAD_PAD), lambda i: (i // nblk, 0, i % nblk, 0))
    kspec = pl.BlockSpec((None, HEADS, tm, HEAD_PAD), lambda i: (i // nblk, 0, key0 + i % nblk, 0))
    vspec = pl.BlockSpec((None, HEADS // 2, tm, HEAD_PAD), lambda i: (i // nblk, 0, key0 + i % nblk, 0))

    def acc_spec(shape):
        nd = len(shape)
        return pl.BlockSpec(shape, lambda i: (0,) * nd)

    acc_shapes = [(HEADS, KV_LORA, HEAD_PAD), (HEADS // 2, KV_LORA, HEAD_PAD), (1, KV_LORA), (1, HEAD_PAD)]
    q_shapes = [(HEADS, Q_LORA, HEAD_PAD), (1, Q_LORA), (1, HEAD_PAD)] if with_q else []
    out_shapes = [(dp_rows, 512)] + q_shapes + acc_shapes
    n_before = 11 + (1 if with_q else 0) + 2 + n_init
    return pl.pallas_call(
        body, name=name, grid=(nb * nblk,),
        in_specs=[pl.BlockSpec((tm, 512), lambda i: (row0 + i, 0)), tab, tab, tab, _whole(wq.shape), _whole(wk.shape),
                  _whole(wv.shape), _whole(kvaw.shape), _whole(qaw.shape), _whole(qnw.shape), _whole(knw.shape)]
        + ([qspec] if with_q else []) + [kspec, vspec] + [_whole(a.shape) for a in (init or [])]
        + [pl.BlockSpec(memory_space=pl.ANY)] * n_into,
        out_specs=[pl.BlockSpec((tm, 512), lambda i: (row0 + i, 0))] + [acc_spec(sh) for sh in q_shapes + acc_shapes],
        out_shape=[_sds(out_shapes[0], BF16)] + [_sds(sh, F32) for sh in out_shapes[1:]],
        scratch_shapes=[pltpu.VMEM((HEADS, tm, HEAD_PAD), F32), pltpu.VMEM((HEADS, tm, HEAD_PAD), BF16)],
        input_output_aliases={n_before: 0} if n_into else {}, compiler_params=_params(1),
    )(proj, *tabs, wq, wk, wv, kvaw, qaw, qnw, knw, *([dq] if with_q else []), dk, dv, *(init or []),
      *([dp_into] if n_into else []))


def _attn_fwd(q, k, v, tq, exch=None):
    nb, _, s, _ = q.shape
    sk = k.shape[2]
    nq = s // tq

    def body(q_ref, k_ref, v_ref, o_ref, lse_ref):
        lane = lax.broadcasted_iota(jnp.int32, (tq, HEAD_PAD), 1)
        vv = v_ref[...]
        outs = []
        for hh in range(2):
            sc = _dot_nt(q_ref[hh], k_ref[hh])
            m = jnp.max(sc, axis=-1, keepdims=True)
            p = jnp.exp(sc - m)
            l = jnp.sum(p, axis=-1, keepdims=True)
            outs.append(_dot(p.astype(BF16), vv) / l)
            lse_ref[hh] = m + jnp.log(l)
        o_ref[...] = jnp.where(lane < V_HEAD, outs[0], outs[1]).astype(BF16)

    (o, lse), got = _hosted_call(
        body, "attn_fwd", (nb, HEADS // 2, nq),
        [pl.BlockSpec((None, 2, tq, HEAD_PAD), lambda b, j, i: (b, j, i, 0)),
         pl.BlockSpec((None, 2, sk, HEAD_PAD), lambda b, j, i: (b, j, 0, 0)),
         pl.BlockSpec((None, None, sk, HEAD_PAD), lambda b, j, i: (b, j, 0, 0))],
        [pl.BlockSpec((tq, HEAD_PAD), lambda b, j, i: (b * nq + i, j)),
         pl.BlockSpec((None, 2, tq, 1), lambda b, j, i: (b, j, i, 0))],
        [_sds((nb * s, MLA_W), BF16), _sds((nb, HEADS, s, 1), F32)], (q, k, v), exch=exch)
    return o, lse, got


def _attn_bwd(q, k, v, do, o, lse, tq, exch=None):
    nb, _, s, _ = q.shape
    sk = k.shape[2]
    nq = s // tq
    kc = _div_tile(sk, 2304, LANES)

    def body(q_ref, k_ref, v_ref, do_ref, o_ref, lse_ref, dq_ref, dk_ref, dv_ref):
        @pl.when(pl.program_id(2) == 0)
        def _():
            dk_ref[...] = jnp.zeros_like(dk_ref)
            dv_ref[...] = jnp.zeros_like(dv_ref)

        lane = lax.broadcasted_iota(jnp.int32, (tq, HEAD_PAD), 1)
        dov = do_ref[...]
        prod = dov.astype(F32) * o_ref[...].astype(F32)
        for hh in range(2):
            mine = (lane < V_HEAD) if hh == 0 else (lane >= V_HEAD)
            doh = jnp.where(mine, dov, jnp.zeros_like(dov))
            delta = jnp.sum(jnp.where(mine, prod, 0.0), axis=-1, keepdims=True)
            qh = q_ref[hh]
            lse_h = lse_ref[hh]
            dq = jnp.zeros((tq, HEAD_PAD), F32)
            for c in range(sk // kc):
                rows = slice(c * kc, (c + 1) * kc)
                kv = k_ref[hh, rows, :]
                p = jnp.exp(_dot_nt(qh, kv) - lse_h)
                dp = _dot_nt(doh, v_ref[rows, :])
                u = (p * (dp - delta)).astype(BF16)
                dq = dq + _dot(u, kv)
                dk_ref[hh, rows, :] += _dot_tn(u, qh)
                dv_ref[rows, :] += _dot_tn(p.astype(BF16), doh)
            dq_ref[hh] = dq

    qspec = pl.BlockSpec((None, 2, tq, HEAD_PAD), lambda b, j, i: (b, j, i, 0))
    kspec = pl.BlockSpec((None, 2, sk, HEAD_PAD), lambda b, j, i: (b, j, 0, 0))
    vspec = pl.BlockSpec((None, None, sk, HEAD_PAD), lambda b, j, i: (b, j, 0, 0))
    ospec = pl.BlockSpec((tq, HEAD_PAD), lambda b, j, i: (b * nq + i, j))
    return _hosted_call(
        body, "attn_bwd", (nb, HEADS // 2, nq),
        [qspec, kspec, vspec, ospec, ospec, pl.BlockSpec((None, 2, tq, 1), lambda b, j, i: (b, j, i, 0))],
        [qspec, kspec, vspec], [_sds(q.shape, F32), _sds(k.shape, F32), _sds(v.shape, F32)], (q, k, v, do, o, lse),
        exch=exch)


def _group_masks(rows):
    lane = lax.broadcasted_iota(jnp.int32, (rows, GMLP_W), 1)
    return [(lane >= g * GROUP_DIM) & (lane < (g + 1) * GROUP_DIM) for g in range(GROUPS)]


def _gmlp_fwd(proj, t, wcat, bias, vnw, ones, tm):
    def body(u_ref, v_ref, wcat_ref, bias_ref, vnw_ref, ones_ref, o_ref):
        masks = _group_masks(CHUNK)
        gv = _gelu(v_ref[...])
        rv = lax.rsqrt(_group_sum(gv * gv, ones_ref) * (1.0 / GROUP_DIM) + EPS)
        vnb = (gv * rv * vnw_ref[...]).astype(BF16)
        for c in range(tm // CHUNK):
            rows = slice(c * CHUNK, (c + 1) * CHUNK)
            vc = vnb[rows]
            stack = jnp.concatenate([jnp.where(m, vc, jnp.zeros_like(vc)) for m in masks], axis=0)
            sp = _dot(wcat_ref[...], stack) + bias_ref[...]
            o_ref[rows, :] = (_gelu(u_ref[rows, :]) * sp).astype(BF16)

    return pl.pallas_call(
        body, name="gmlp_fwd", grid=(t // tm,),
        in_specs=[pl.BlockSpec((tm, GMLP_W), lambda i: (i, 1)), pl.BlockSpec((tm, GMLP_W), lambda i: (i, 2)),
                  _whole(wcat.shape), _whole(bias.shape), _whole(vnw.shape), _whole(ones.shape)],
        out_specs=pl.BlockSpec((tm, GMLP_W), lambda i: (i, 0)),
        out_shape=_sds((t, GMLP_W), BF16), compiler_params=_params(1),
    )(proj, proj, wcat, bias, vnw, ones)


def _gmlp_bwd(proj, dsg, wcat, wcat_t, bias, vnw, ones, tm):
    t = dsg.shape[0]

    def body(u_ref, v_ref, dsg_ref, wcat_ref, wcatt_ref, bias_ref, vnw_ref, ones_ref,
             du_ref, dv_ref, dws_ref, dbs_ref, dvnw_ref):
        @pl.when(pl.program_id(0) == 0)
        def _():
            dws_ref[...] = jnp.zeros_like(dws_ref)
            dbs_ref[...] = jnp.zeros_like(dbs_ref)
            dvnw_ref[...] = jnp.zeros_like(dvnw_ref)

        masks = _group_masks(CHUNK)
        v = v_ref[...]
        gv = _gelu(v)
        rv = lax.rsqrt(_group_sum(gv * gv, ones_ref) * (1.0 / GROUP_DIM) + EPS)
        xh = gv * rv
        vnb = (xh * vnw_ref[...]).astype(BF16)
        dvn_parts = []
        for c in range(tm // CHUNK):
            rows = slice(c * CHUNK, (c + 1) * CHUNK)
            vc = vnb[rows]
            stack = jnp.concatenate([jnp.where(m, vc, jnp.zeros_like(vc)) for m in masks], axis=0)
            sp = _dot(wcat_ref[...], stack) + bias_ref[...]
            u = u_ref[rows, :]
            dsg_c = dsg_ref[rows, :]
            du_ref[rows, :] = (dsg_c * sp * _gelu_grad(u)).astype(BF16)
            ds = dsg_c * _gelu(u)
            dstack = jnp.concatenate([jnp.where(m, ds, 0.0) for m in masks], axis=0)
            dbs_ref[...] += jnp.broadcast_to(jnp.sum(dstack, axis=-1, keepdims=True), dbs_ref.shape)
            dstb = dstack.astype(BF16)
            dvn_parts.append(_dot(wcatt_ref[...], dstb))
            dws_ref[...] += _dot_nt(dstb, vc)
        dvn = jnp.concatenate(dvn_parts, axis=0) if len(dvn_parts) > 1 else dvn_parts[0]
        dvnw_ref[...] += jnp.sum(dvn * xh, axis=0, keepdims=True)
        dxh = dvn * vnw_ref[...]
        gm = _group_sum(dxh * xh, ones_ref) * (1.0 / GROUP_DIM)
        dv_ref[...] = (rv * (dxh - xh * gm) * _gelu_grad(v)).astype(BF16)

    row = pl.BlockSpec((tm, GMLP_W), lambda i: (i, 0))
    return pl.pallas_call(
        body, name="gmlp_bwd", grid=(t // tm,),
        in_specs=[pl.BlockSpec((tm, GMLP_W), lambda i: (i, 1)), pl.BlockSpec((tm, GMLP_W), lambda i: (i, 2)), row,
                  _whole(wcat.shape), _whole(wcat_t.shape), _whole(bias.shape), _whole(vnw.shape), _whole(ones.shape)],
        out_specs=[row, row, pl.BlockSpec((GROUPS * CHUNK, CHUNK), lambda i: (0, 0)),
                   pl.BlockSpec((GROUPS * CHUNK, CHUNK), lambda i: (0, 0)), pl.BlockSpec((1, GMLP_W), lambda i: (0, 0))],
        out_shape=[_sds((t, GMLP_W), BF16), _sds((t, GMLP_W), BF16), _sds((GROUPS * CHUNK, CHUNK), F32),
                   _sds((GROUPS * CHUNK, CHUNK), F32), _sds((1, GMLP_W), F32)],
        compiler_params=_params(1),
    )(proj, proj, dsg, wcat, wcat_t, bias, vnw, ones)


def _mixout_fwd(o, sg, xs, mod, wout, s, tm):
    t = o.shape[0]
    d = xs.shape[1]

    def body(o_ref, sg_ref, x_ref, mod_ref, w_ref, x2_ref, mix_ref):
        g = (pl.program_id(0) * tm) // s
        gate = mod_ref[g, pl.ds(5, 1), :]
        mix = _dot(o_ref[...], w_ref[0:MLA_W, :]) + _dot(sg_ref[...], w_ref[MLA_W:MLA_W + GMLP_W, :])
        x2_ref[...] = x_ref[...] + gate * mix
        mix_ref[...] = mix.astype(BF16)

    row = lambda i: (i, 0)
    return pl.pallas_call(
        body, name="mixout_fwd", grid=(t // tm,),
        in_specs=[pl.BlockSpec((tm, MLA_W), row), pl.BlockSpec((tm, GMLP_W), row), pl.BlockSpec((tm, d), row),
                  _whole(mod.shape), _whole(wout.shape)],
        out_specs=[pl.BlockSpec((tm, d), row), pl.BlockSpec((tm, d), row)],
        out_shape=[_sds((t, d), F32), _sds((t, d), BF16)], compiler_params=_params(1),
    )(o, sg, xs, mod, wout)


def _mixout_bwd(dx2, mix, mod, wout, s, tm):
    t, d = dx2.shape

    def body(dx_ref, mix_ref, mod_ref, w_ref, dmix_ref, do_ref, dsg_ref, dmod_ref):
        i = pl.program_id(0)

        @pl.when(i == 0)
        def _():
            dmod_ref[...] = jnp.zeros_like(dmod_ref)

        g = (i * tm) // s
        gate = mod_ref[g, pl.ds(5, 1), :]
        dx = dx_ref[...]
        dmod_ref[g, pl.ds(5, 1), :] += jnp.sum(dx * mix_ref[...].astype(F32), axis=0, keepdims=True)
        dmb = (gate * dx).astype(BF16)
        dmix_ref[...] = dmb
        do_ref[...] = _dot_nt(dmb, w_ref[0:MLA_W, :]).astype(BF16)
        dsg_ref[...] = _dot_nt(dmb, w_ref[MLA_W:MLA_W + GMLP_W, :])

    row = lambda i: (i, 0)
    return pl.pallas_call(
        body, name="mixout_bwd", grid=(t // tm,),
        in_specs=[pl.BlockSpec((tm, d), row), pl.BlockSpec((tm, d), row), _whole(mod.shape), _whole(wout.shape)],
        out_specs=[pl.BlockSpec((tm, d), row), pl.BlockSpec((tm, MLA_W), row), pl.BlockSpec((tm, GMLP_W), row),
                   pl.BlockSpec(mod.shape, lambda i: (0, 0, 0))],
        out_shape=[_sds((t, d), BF16), _sds((t, MLA_W), BF16), _sds((t, GMLP_W), F32), _sds(mod.shape, F32)],
        compiler_params=_params(1),
    )(dx2, mix, mod, wout)


def _gather_first(shards):
    n = len(shards)

    def body(*refs):
        srcs, outs = refs[:n], refs[n:2 * n]
        ici_send, ici_recv, d2d_send, d2d_recv, local_sems = refs[2 * n:]
        x, y, c = lax.axis_index("x"), lax.axis_index("y"), lax.axis_index("c")
        me = 2 * x + y
        chips = _other_chips(x, y)

        def half(w, which):
            hr = shards[w].shape[0] // 2
            return pl.ds(pl.multiple_of(which * hr, 16), hr)

        def over_ici(w, k, arriving):
            px, py = chips[k]
            slot = 2 * px + py if arriving else me
            return pltpu.make_async_remote_copy(
                src_ref=srcs[w].at[half(w, c)], dst_ref=outs[w].at[slot, half(w, c)], send_sem=ici_send.at[3 * w + k],
                recv_sem=ici_recv.at[3 * w + k], device_id=(px, py, c), device_id_type=pl.DeviceIdType.MESH)

        def to_sibling(w, k, arriving):
            px, py = chips[k]
            rows = half(w, 1 - c if arriving else c)
            return pltpu.make_async_remote_copy(
                src_ref=outs[w].at[2 * px + py, rows], dst_ref=outs[w].at[2 * px + py, rows],
                send_sem=d2d_send.at[3 * w + k], recv_sem=d2d_recv.at[3 * w + k], device_id=(x, y, 1 - c),
                device_id_type=pl.DeviceIdType.MESH)

        local = [pltpu.make_async_copy(srcs[w], outs[w].at[me], local_sems.at[w]) for w in range(n)]
        for cp in local:
            cp.start()
        pairs = [(w, k) for w in range(n) for k in range(3)]
        for w, k in pairs:
            over_ici(w, k, False).start()
        for w, k in pairs:
            over_ici(w, k, True).wait_recv()
            to_sibling(w, k, False).start()
        for w, k in pairs:
            to_sibling(w, k, True).wait_recv()
        for w, k in pairs:
            over_ici(w, k, False).wait_send()
            to_sibling(w, k, False).wait_send()
        for cp in local:
            cp.wait()

    any_spec = pl.BlockSpec(memory_space=pl.ANY)
    sems = pltpu.SemaphoreType.DMA((3 * n,))
    return pl.pallas_call(
        body, name="gather_first", in_specs=[any_spec] * n, out_specs=[any_spec] * n,
        out_shape=_exch_shapes("gather", shards),
        scratch_shapes=[sems, sems, sems, sems, pltpu.SemaphoreType.DMA((n,))],
    )(*shards)


def _swap_cores(parts, name):
    n = len(parts)

    def body(*refs):
        srcs, outs, send_sems, recv_sems = refs[:n], refs[n:2 * n], refs[2 * n], refs[2 * n + 1]
        x, y, c = lax.axis_index("x"), lax.axis_index("y"), lax.axis_index("c")
        copies = [pltpu.make_async_remote_copy(
            src_ref=srcs[w], dst_ref=outs[w], send_sem=send_sems.at[w], recv_sem=recv_sems.at[w],
            device_id=(x, y, 1 - c), device_id_type=pl.DeviceIdType.MESH) for w in range(n)]
        for cp in copies:
            cp.start()
        for cp in copies:
            cp.wait()

    any_spec = pl.BlockSpec(memory_space=pl.ANY)
    return pl.pallas_call(
        body, name=name, in_specs=[any_spec] * n, out_specs=[any_spec] * n,
        out_shape=[_sds(p.shape, p.dtype) for p in parts],
        scratch_shapes=[pltpu.SemaphoreType.DMA((n,)), pltpu.SemaphoreType.DMA((n,))],
    )(*parts)


def _row_tile(r, c, mult):
    return _div_tile(r, max(mult, (1 << 16) // c), mult)


def _sum_slots(recv, name):
    _, r, c = recv.shape
    tr = _row_tile(r, c, 16)

    def body(r_ref, o_ref):
        f = lambda k: r_ref[k].astype(F32)
        o_ref[...] = ((f(0) + f(1)) + f(2)) + f(3)

    return pl.pallas_call(
        body, name=name, grid=(r // tr,),
        in_specs=[pl.BlockSpec((N_CHIPS, tr, c), lambda i: (0, i, 0))],
        out_specs=pl.BlockSpec((tr, c), lambda i: (i, 0)),
        out_shape=_sds((r, c), F32), compiler_params=_params(1),
    )(recv)


def _adamw(parts, w, m, v, name, exch=None):
    r, wd = w.shape
    tr = _row_tile(r, wd, 8)
    c1 = 1.0 / (1.0 - ADAM_B1 ** ADAM_STEP)
    c2 = 1.0 / (1.0 - ADAM_B2 ** ADAM_STEP)
    n_p = len(parts)

    def body(*refs):
        p_refs = refs[:n_p]
        w_ref, m_ref, v_ref, g_ref, d_ref, nm_ref, nv_ref = refs[n_p:]
        g = p_refs[0][...]
        for p_ref in p_refs[1:]:
            g = g + p_ref[...]
        nm = ADAM_B1 * m_ref[...] + (1.0 - ADAM_B1) * g
        nv = ADAM_B2 * v_ref[...] + (1.0 - ADAM_B2) * (g * g)
        g_ref[...] = g
        nm_ref[...] = nm
        nv_ref[...] = nv
        d_ref[...] = -ADAM_LR * ((nm * c1) / (jnp.sqrt(nv * c2) + ADAM_EPS) + ADAM_WD * w_ref[...])

    spec = pl.BlockSpec((tr, wd), lambda i: (i, 0))
    return _hosted_call(body, name, (r // tr,), [spec] * (n_p + 3), [spec] * 4, [_sds((r, wd), F32)] * 4,
                        (*parts, w, m, v), exch=exch)


def _all_peers(x, y, c):
    flips = [(dx, dy, dc) for dx in (0, 1) for dy in (0, 1) for dc in (0, 1)][1:]
    return [(1 - x if dx else x, 1 - y if dy else y, 1 - c if dc else c) for dx, dy, dc in flips]


def _gather_devices(block):
    def body(src_ref, out_ref, send_sems, recv_sems, local_sem):
        x, y, c = lax.axis_index("x"), lax.axis_index("y"), lax.axis_index("c")
        me = 4 * x + 2 * y + c
        mine = pltpu.make_async_copy(src_ref, out_ref.at[me], local_sem)
        mine.start()

        def copy(k, peer, slot):
            return pltpu.make_async_remote_copy(
                src_ref=src_ref, dst_ref=out_ref.at[slot], send_sem=send_sems.at[k], recv_sem=recv_sems.at[k],
                device_id=peer, device_id_type=pl.DeviceIdType.MESH)

        peers = _all_peers(x, y, c)
        for k, peer in enumerate(peers):
            copy(k, peer, me).start()
        for k, (px, py, pc) in enumerate(peers):
            copy(k, (px, py, pc), 4 * px + 2 * py + pc).wait_recv()
        for k, peer in enumerate(peers):
            copy(k, peer, me).wait_send()
        mine.wait()

    return pl.pallas_call(
        body, name="gather_devices", in_specs=[pl.BlockSpec(memory_space=pl.ANY)],
        out_specs=pl.BlockSpec(memory_space=pl.ANY), out_shape=_sds((8,) + block.shape, block.dtype),
        scratch_shapes=[pltpu.SemaphoreType.DMA((7,)), pltpu.SemaphoreType.DMA((7,)), pltpu.SemaphoreType.DMA(())],
    )(block)


def _ada_fwd_tp(cc_all, w, b):
    n = w.shape[1]

    def body(cc_ref, w_ref, b_ref, out_ref, part_ref, send_sems, recv_sems):
        x, y, c = lax.axis_index("x"), lax.axis_index("y"), lax.axis_index("c")
        me = 2 * x + y
        cv = cc_ref[...]
        part_ref[...] = _dot((cv * _sigmoid(cv)).astype(BF16), w_ref[...]) + b_ref[...]

        def rows_of(px, py):
            return part_ref.at[pl.ds(pl.multiple_of((4 * px + 2 * py + c) * MOD_ROWS, MOD_ROWS), MOD_ROWS)]

        def copy(k, px, py, slot):
            return pltpu.make_async_remote_copy(
                src_ref=rows_of(px, py), dst_ref=out_ref.at[slot], send_sem=send_sems.at[k], recv_sem=recv_sems.at[k],
                device_id=(px, py, c), device_id_type=pl.DeviceIdType.MESH)

        chips = _other_chips(x, y)
        for k, (px, py) in enumerate(chips):
            copy(k, px, py, me).start()
        out_ref[me] = rows_of(x, y)[...]
        for k, (px, py) in enumerate(chips):
            copy(k, px, py, 2 * px + py).wait_recv()
        for k, (px, py) in enumerate(chips):
            copy(k, px, py, me).wait_send()

    vmem = pl.BlockSpec(memory_space=pltpu.VMEM)
    return pl.pallas_call(
        body, name="ada_fwd_tp", in_specs=[vmem, vmem, vmem], out_specs=vmem,
        out_shape=_sds((N_CHIPS, MOD_ROWS, n), F32),
        scratch_shapes=[pltpu.VMEM((8 * MOD_ROWS, n), F32), pltpu.SemaphoreType.DMA((3,)), pltpu.SemaphoreType.DMA((3,))],
        compiler_params=pltpu.CompilerParams(vmem_limit_bytes=V7X_VMEM_LIMIT),
    )(cc_all, w, b)


def _ada_bwd_tp(cc_all, dmods, w, ctx_row):
    d, n = w.shape

    def body(cc_ref, m0, m1, m2, m3, w_ref, dw_ref, db_ref, dctx_ref, stage_ref, all_ref, send_sems, recv_sems):
        x, y, c = lax.axis_index("x"), lax.axis_index("y"), lax.axis_index("c")
        me = 4 * x + 2 * y + c
        dsum = m0[...] + m1[...] + m2[...] + m3[...]
        db_ref[...] = jnp.sum(dsum, axis=0, keepdims=True)
        for j in range(N_CHIPS):
            stage_ref[j] = dsum[:, j * n:(j + 1) * n]

        def copy(k, peer, slot):
            px, py, _ = peer
            return pltpu.make_async_remote_copy(
                src_ref=stage_ref.at[2 * px + py], dst_ref=all_ref.at[slot], send_sem=send_sems.at[k],
                recv_sem=recv_sems.at[k], device_id=peer, device_id_type=pl.DeviceIdType.MESH)

        peers = _all_peers(x, y, c)
        for k, peer in enumerate(peers):
            copy(k, peer, me).start()
        all_ref[me] = stage_ref[2 * x + y]
        for k, (px, py, pc) in enumerate(peers):
            copy(k, (px, py, pc), 4 * px + 2 * py + pc).wait_recv()
        for k, peer in enumerate(peers):
            copy(k, peer, me).wait_send()
        cv = cc_ref[...]
        sig = _sigmoid(cv)
        dmb = all_ref[...].reshape(8 * MOD_ROWS, n).astype(BF16)
        dw_ref[...] = _dot_tn((cv * sig).astype(BF16), dmb)
        dsc = _dot_nt(dmb, w_ref[...])
        dctx = dsc[ctx_row:ctx_row + 1, :]
        for dev in range(1, 8):
            dctx = dctx + dsc[dev * MOD_ROWS + ctx_row:dev * MOD_ROWS + ctx_row + 1, :]
        cx = cv[ctx_row:ctx_row + 1, :]
        sx = sig[ctx_row:ctx_row + 1, :]
        dctx_ref[...] = dctx * (sx * (1.0 + cx * (1.0 - sx))) * jnp.where(c == 0, 1.0, 0.0)

    vmem = pl.BlockSpec(memory_space=pltpu.VMEM)
    return pl.pallas_call(
        body, name="ada_bwd_tp", in_specs=[vmem] * 6, out_specs=[vmem] * 3,
        out_shape=[_sds((d, n), F32), _sds((1, N_MOD * d), F32), _sds((1, d), F32)],
        scratch_shapes=[pltpu.VMEM((N_CHIPS, MOD_ROWS, n), F32), pltpu.VMEM((8, MOD_ROWS, n), F32),
                        pltpu.SemaphoreType.DMA((7,)), pltpu.SemaphoreType.DMA((7,))],
        compiler_params=pltpu.CompilerParams(vmem_limit_bytes=V7X_VMEM_LIMIT),
    )(cc_all, *dmods, w)


def _rope_tables(s, ctx):
    pos = np.arange(s, dtype=np.float32)
    inv = (np.float32(ROPE_BASE) ** (-np.arange(0, QK_ROPE // 2, 2, dtype=np.float32) / np.float32(QK_ROPE // 2)))
    ang_r = np.floor(pos / GRID_W)[:, None] * inv
    ang_c = (pos - GRID_W * np.floor(pos / GRID_W))[:, None] * inv
    ang = np.concatenate([ang_r, ang_r, ang_c, ang_c], axis=-1).astype(np.float32)
    cos, sin = np.cos(ang), np.sin(ang)
    half_b = (np.arange(QK_ROPE) // 8) % 2 == 1
    sin_a = np.where(half_b, sin, 0.0)
    sin_b = np.where(half_b, 0.0, -sin)

    def place(tab, fill):
        full = np.full((s + ctx, HEAD_PAD), fill, np.float32)
        full[:s, QK_NOPE:QK_HEAD] = tab
        return jnp.asarray(full)

    return place(cos, 1.0), place(sin_a, 0.0), place(sin_b, 0.0)


def _pad_last(a, n):
    return jnp.pad(a, [(0, 0)] * (a.ndim - 1) + [(0, n - a.shape[-1])])


def _flat_rows(parts, rows, width):
    flat = jnp.concatenate([p.reshape(-1) for p in parts])
    return jnp.pad(flat, (0, rows * width - flat.shape[0])).reshape(rows, width)


def kernel(x, c, ctx, c_ctx, w_ada, b_ada, norm1_w, ffn1_w1, ffn1_w3, ffn1_w2, norm2_w, w_in, q_a_norm_w, w_uq, kv_a_norm_w, w_ukv, q_norm_w, k_norm_w, v_norm_w, w_s, b_s, w_out, norm3_w, ffn2_w1, ffn2_w3, ffn2_w2, loss_target, m_c_ctx, m_w_ada, m_b_ada, m_norm1_w, m_ffn1_w1, m_ffn1_w3, m_ffn1_w2, m_norm2_w, m_w_in, m_q_a_norm_w, m_w_uq, m_kv_a_norm_w, m_w_ukv, m_q_norm_w, m_k_norm_w, m_v_norm_w, m_w_s, m_b_s, m_w_out, m_norm3_w, m_ffn2_w1, m_ffn2_w3, m_ffn2_w2, v_c_ctx, v_w_ada, v_b_ada, v_norm1_w, v_ffn1_w1, v_ffn1_w3, v_ffn1_w2, v_norm2_w, v_w_in, v_q_a_norm_w, v_w_uq, v_kv_a_norm_w, v_w_ukv, v_q_norm_w, v_k_norm_w, v_v_norm_w, v_w_s, v_b_s, v_w_out, v_norm3_w, v_ffn2_w1, v_ffn2_w3, v_ffn2_w2):
    wts = dict(c_ctx=c_ctx, w_ada=w_ada, b_ada=b_ada, norm1_w=norm1_w, ffn1_w1=ffn1_w1, ffn1_w3=ffn1_w3, ffn1_w2=ffn1_w2,
               norm2_w=norm2_w, w_in=w_in, q_a_norm_w=q_a_norm_w, w_uq=w_uq, kv_a_norm_w=kv_a_norm_w, w_ukv=w_ukv,
               q_norm_w=q_norm_w, k_norm_w=k_norm_w, v_norm_w=v_norm_w, w_s=w_s, b_s=b_s, w_out=w_out, norm3_w=norm3_w,
               ffn2_w1=ffn2_w1, ffn2_w3=ffn2_w3, ffn2_w2=ffn2_w2)
    moms = dict(c_ctx=m_c_ctx, w_ada=m_w_ada, b_ada=m_b_ada, norm1_w=m_norm1_w, ffn1_w1=m_ffn1_w1, ffn1_w3=m_ffn1_w3,
                ffn1_w2=m_ffn1_w2, norm2_w=m_norm2_w, w_in=m_w_in, q_a_norm_w=m_q_a_norm_w, w_uq=m_w_uq,
                kv_a_norm_w=m_kv_a_norm_w, w_ukv=m_w_ukv, q_norm_w=m_q_norm_w, k_norm_w=m_k_norm_w, v_norm_w=m_v_norm_w,
                w_s=m_w_s, b_s=m_b_s, w_out=m_w_out, norm3_w=m_norm3_w, ffn2_w1=m_ffn2_w1, ffn2_w3=m_ffn2_w3,
                ffn2_w2=m_ffn2_w2)
    vars_ = dict(c_ctx=v_c_ctx, w_ada=v_w_ada, b_ada=v_b_ada, norm1_w=v_norm1_w, ffn1_w1=v_ffn1_w1, ffn1_w3=v_ffn1_w3,
                 ffn1_w2=v_ffn1_w2, norm2_w=v_norm2_w, w_in=v_w_in, q_a_norm_w=v_q_a_norm_w, w_uq=v_w_uq,
                 kv_a_norm_w=v_kv_a_norm_w, w_ukv=v_w_ukv, q_norm_w=v_q_norm_w, k_norm_w=v_k_norm_w, v_norm_w=v_v_norm_w,
                 w_s=v_w_s, b_s=v_b_s, w_out=v_w_out, norm3_w=v_norm3_w, ffn2_w1=v_ffn2_w1, ffn2_w3=v_ffn2_w3,
                 ffn2_w2=v_ffn2_w2)

    nb, s, d = x.shape
    nctx = ctx.shape[1]
    t, tc = nb * s, nb * nctx
    t_all = t + tc
    sk = s + nctx
    assert nb + 1 <= MOD_ROWS and d % LANES == 0
    tm = _token_tile(s, nctx)

    def held(n, a_):
        return jnp.swapaxes(a_[0], 0, 1) if n in T_WEIGHTS else a_[0]

    def unheld(n, a_):
        return (jnp.swapaxes(a_, 0, 1) if n in T_WEIGHTS else a_)[None]

    shard = {n: held(n, wts[n]).astype(BF16) for n in SHARDED}
    full = {}

    def unshard(names, blocks):
        for n, g4 in zip(names, blocks):
            _, r_, c_ = g4.shape
            if n in ROW_SHARDED or n in T_WEIGHTS:
                full[n] = g4.reshape(N_CHIPS * r_, c_)
            else:
                full[n] = g4.transpose(1, 0, 2).reshape(r_, N_CHIPS * c_)

    def chip_major(n, g_):
        if n in ROW_SHARDED or n in T_WEIGHTS:
            return g_.reshape(N_CHIPS, g_.shape[0] // N_CHIPS, g_.shape[1]).astype(BF16)
        r_, cols = g_.shape
        return g_.reshape(r_, N_CHIPS, cols // N_CHIPS).transpose(1, 0, 2).astype(BF16)

    unshard(FIRST_WEIGHTS, _gather_first([shard[n] for n in FIRST_WEIGHTS]))
    wsb = w_s[0].astype(BF16)
    wcat = wsb.transpose(1, 0, 2).reshape(CHUNK, GROUPS * CHUNK)
    wcat_t = wsb.transpose(2, 0, 1).reshape(CHUNK, GROUPS * CHUNK)
    bias = jnp.repeat(b_s[0].T, GROUP_DIM, axis=1)
    vnw = v_norm_w.reshape(1, GMLP_W)
    lane = jnp.arange(GMLP_W)
    ones = (lane[:, None] // GROUP_DIM == lane[None, :] // GROUP_DIM).astype(BF16)
    qnw = _pad_last(q_norm_w, HEAD_PAD)
    knw = _pad_last(k_norm_w, HEAD_PAD)
    tabs = _rope_tables(s, nctx)

    cc = jnp.concatenate([c, c_ctx[None, :], jnp.zeros((MOD_ROWS - nb - 1, d), F32)], axis=0)
    cc_all = _gather_devices(cc).reshape(8 * MOD_ROWS, d)
    n_ada = shard["w_ada"].shape[1]
    assert n_ada % LANES == 0
    my_chip = 2 * lax.axis_index("x") + lax.axis_index("y")
    b_cols = lax.dynamic_slice_in_dim(b_ada, my_chip * n_ada, n_ada, axis=1)
    mod = _ada_fwd_tp(cc_all, shard["w_ada"], b_cols).transpose(1, 0, 2).reshape(MOD_ROWS, N_MOD, d)
    x_lat, x_ctx = x.reshape(t, d), ctx.reshape(tc, d)
    (xs1, a1, b1, y1), got = _ffn_fwd(x_lat, x_ctx, mod, norm1_w, full["ffn1_w1"], full["ffn1_w3"], full["ffn1_w2"], 0, s,
                                      nb, tm, "ffn1_fwd", exch=("gather", [shard[n] for n in MIX_WEIGHTS]))
    unshard(MIX_WEIGHTS, got)
    wi = full["w_in"]
    wp = jnp.concatenate([wi[0:KV_LORA], jnp.zeros((QK_NOPE, d), BF16), wi[KV_LORA:KV_LORA + QK_ROPE],
                          jnp.zeros((HEAD_PAD - QK_HEAD, d), BF16), wi[KV_LORA + QK_ROPE:]], axis=0)
    wq = jnp.pad(full["w_uq"].reshape(HEADS, QK_HEAD, Q_LORA), ((0, 0), (0, HEAD_PAD - QK_HEAD), (0, 0)))
    wkv = full["w_ukv"].reshape(KV_LORA, HEADS, QK_NOPE + V_HEAD)
    wk = _pad_last(wkv[:, :, :QK_NOPE].transpose(1, 0, 2), HEAD_PAD)
    wv = wkv[:, :, QK_NOPE:].reshape(KV_LORA, HEADS // 2, 2 * V_HEAD).transpose(1, 0, 2)
    h2, proj = _mixin_fwd(xs1, mod, norm2_w, wp, s, nb, tm)
    prep_w = (wq, wk, wv, kv_a_norm_w, q_a_norm_w, qnw, knw)
    q, k_all, v_all = _prep_fwd(proj, 0, nb, s, 0, sk, 0, None, tabs, *prep_w, tm, True, "prep_fwd")
    k_all, v_all = _prep_fwd(proj, t // tm, nb, nctx, s // tm, sk, s // tm, (k_all, v_all), tabs, *prep_w, tm, False,
                             "prep_ctx_fwd")
    tq = _div_tile(s, 512, tm)
    o, lse, got = _attn_fwd(q, k_all, v_all, tq, exch=("gather", [shard[n] for n in LAST_WEIGHTS]))
    unshard(LAST_WEIGHTS, got)
    sg = _gmlp_fwd(proj, t, wcat, bias, vnw, ones, tm)
    x2, mix = _mixout_fwd(o, sg, xs1, mod, full["w_out"], s, tm)
    (dy, a2, b2, y2, loss_part), _ = _ffn_fwd(x2, None, mod, norm3_w, full["ffn2_w1"], full["ffn2_w3"], full["ffn2_w2"], 6,
                                              s, nb, tm, "ffn2_fwd", target=loss_target.reshape(t, d))
    loss = lax.psum(loss_part[0, 0], ("x", "y", "c"))

    grads, cm, recv = {}, {}, {}

    def scatter_of(names):
        return ("scatter", [cm[n] for n in names])

    (dx2, h3, g2, da2, db2, dyb2, dmod_c, grads["norm3_w"]), _ = _ffn_bwd(
        dy, x2, None, a2, b2, y2, mod, norm3_w, full["ffn2_w1"], full["ffn2_w3"], full["ffn2_w2"], 6, s, nb, tm,
        "ffn2_bwd")
    cm["ffn2_w1"] = chip_major("ffn2_w1", _mm_tn(da2, h3, t, "ffn2_dw1"))
    cm["ffn2_w3"] = chip_major("ffn2_w3", _mm_tn(db2, h3, t, "ffn2_dw3"))
    cm["ffn2_w2"] = chip_major("ffn2_w2", _mm_tn(g2, dyb2, t, "ffn2_dw2"))
    dmix, do, dsg, dmod_b = _mixout_bwd(dx2, mix, mod, full["w_out"], s, tm)
    cm["w_out"] = chip_major("w_out", jnp.concatenate([_mm_tn(o, dmix, t, "wout_dw_attn"),
                                                       _mm_tn(sg, dmix, t, "wout_dw_gmlp")], axis=0))
    dpu, dpv, dws, dbs, dvnw = _gmlp_bwd(proj, dsg, wcat, wcat_t, bias, vnw, ones, tm)
    group = LAST_WEIGHTS + ("w_out",)
    (dq, dk, dv), got = _attn_bwd(q, k_all, v_all, do, o, lse, tq, exch=scatter_of(group))
    recv.update(zip(group, got))
    dp0, dwk_c, dwv_c, dkvaw_c, dknw_c = _prep_bwd(
        proj, t // tm, nb, nctx, s // tm, s // tm, t_all, None, tabs, *prep_w, None, dk, dv, None, tm, "prep_ctx_bwd")
    dp0, dwq, dqaw, dqnw, dwk, dwv, dkvaw, dknw = _prep_bwd(
        proj, 0, nb, s, 0, 0, t_all, dp0, tabs, *prep_w, dq, dk, dv, [dwk_c, dwv_c, dkvaw_c, dknw_c], tm, "prep_bwd")
    dxs1, dmod_a, grads["norm2_w"] = _mixin_bwd(dp0, dpu, dpv, xs1, dx2, mod, norm2_w, wp, s, nb, tm)
    dwp = jnp.concatenate([_mm_tn(dp0, h2, t_all, "win_dw_kvq"), _mm_tn(dpu, h2, t, "win_dw_u"),
                           _mm_tn(dpv, h2, t, "win_dw_v")], axis=0)
    cm["w_in"] = chip_major("w_in", jnp.concatenate(
        [dwp[0:KV_LORA], dwp[KV_LORA + QK_NOPE:KV_LORA + QK_HEAD], dwp[256:]], axis=0))
    cm["w_uq"] = chip_major("w_uq", dwq[:, :, :QK_HEAD].transpose(0, 2, 1).reshape(HEADS * QK_HEAD, Q_LORA))
    cm["w_ukv"] = chip_major("w_ukv", jnp.concatenate(
        [dwk[:, :, :QK_NOPE].transpose(1, 0, 2),
         dwv.transpose(1, 0, 2).reshape(KV_LORA, HEADS, V_HEAD)], axis=2).reshape(KV_LORA, HEADS * (QK_NOPE + V_HEAD)))
    (dx_lat, h1, g1, da1, db1, dyb1, dmod_0, grads["norm1_w"]), _ = _ffn_bwd(
        dxs1, x_lat, x_ctx, a1, b1, y1, mod, norm1_w, full["ffn1_w1"], full["ffn1_w3"], full["ffn1_w2"], 0, s, nb, tm,
        "ffn1_bwd")
    dmods = [m_.reshape(MOD_ROWS, N_MOD * d) for m_ in (dmod_0, dmod_a, dmod_b, dmod_c)]
    dw_ada, grads["b_ada"], dctx = _ada_bwd_tp(cc_all, dmods, shard["w_ada"], nb)
    grads["c_ctx"] = dctx[0]
    grads["q_a_norm_w"], grads["kv_a_norm_w"] = dqaw, dkvaw
    grads["q_norm_w"], grads["k_norm_w"] = dqnw[:, :QK_HEAD], dknw[:, :QK_HEAD]
    grads["v_norm_w"], grads["w_s"], grads["b_s"] = dvnw, dws, dbs[:, 0]
    grad_x = dx_lat.reshape(nb, s, d)
    rows_s = _round_up(-(-sum(wts[n].size for n in SMALL) // d), 16)
    cm["small"] = jnp.broadcast_to(_flat_rows([grads[n] for n in SMALL], rows_s, d), (N_CHIPS, rows_s, d))
    group = ("w_in", "w_uq", "w_ukv")
    dw2, got = _mm_tn(g1, dyb1, t_all, "ffn1_dw2", exch=scatter_of(group))
    recv.update(zip(group, got))
    cm["ffn1_w2"] = chip_major("ffn1_w2", dw2)
    dw1, got = _mm_tn(da1, h1, t_all, "ffn1_dw1", exch=scatter_of(("ffn1_w2",)))
    recv["ffn1_w2"] = got[0]
    cm["ffn1_w1"] = chip_major("ffn1_w1", dw1)
    group = ("ffn1_w1", "small")
    dw3, got = _mm_tn(db1, h1, t_all, "ffn1_dw3", exch=scatter_of(group))
    recv.update(zip(group, got))
    cm["ffn1_w3"] = chip_major("ffn1_w3", dw3)
    stepped = {}
    stepped["w_ada"], got = _adamw([dw_ada], wts["w_ada"][0], moms["w_ada"][0], vars_["w_ada"][0], "adamw_w_ada",
                                   exch=scatter_of(("ffn1_w3",)))
    recv["ffn1_w3"] = got[0]

    reduced = tuple(n for n in SHARDED if n != "w_ada") + ("small",)
    part = {n: _sum_slots(recv[n], "sum_" + n) for n in reduced}
    early = LAST_WEIGHTS + ("w_out",)
    late = tuple(n for n in reduced if n not in early)
    sib = dict(zip(early, _swap_cores([part[n] for n in early], "swap_early")))
    sib.update(zip(late, _swap_cores([part[n] for n in late], "swap_late")))
    for n in reduced[:-1]:
        stepped[n], _ = _adamw([part[n], sib[n]], held(n, wts[n]), held(n, moms[n]), held(n, vars_[n]), "adamw_" + n)
    for n in SHARDED:
        stepped[n] = [unheld(n, a_) for a_ in stepped[n]]
    packed, _ = _adamw([part["small"], sib["small"]], _flat_rows([wts[n] for n in SMALL], rows_s, d),
                       _flat_rows([moms[n] for n in SMALL], rows_s, d), _flat_rows([vars_[n] for n in SMALL], rows_s, d),
                       "adamw_small")
    for n in SMALL:
        stepped[n] = []
    for a_ in packed:
        flat = a_.reshape(-1)
        off = 0
        for n in SMALL:
            stepped[n].append(flat[off:off + wts[n].size].reshape(wts[n].shape))
            off += wts[n].size
    return (loss, grad_x, *[stepped[n][0] for n in WEIGHTS], *[stepped[n][1] for n in WEIGHTS],
            *[stepped[n][2] for n in WEIGHTS], *[stepped[n][3] for n in WEIGHTS])
```

```python
import functools
import math

import jax
import jax.numpy as jnp
import numpy as np
from jax import lax
from jax.experimental import pallas as pl
from jax.experimental.pallas import tpu as pltpu

F32 = jnp.float32
BF16 = jnp.bfloat16

EPS = 1e-6
N_MOD = 9
HEADS = 8
QK_NOPE, QK_ROPE, V_HEAD = 64, 32, 64
QK_HEAD = QK_NOPE + QK_ROPE
HEAD_PAD = 128
LN2 = math.log(2.0)
SOFTMAX_SCALE = QK_HEAD ** -0.5 / LN2
Q_LORA, KV_LORA = 256, 128
GROUPS, GROUP_DIM, CHUNK = 8, 64, 128
GMLP_W = GROUPS * GROUP_DIM
MLA_W = HEADS * V_HEAD
IN_COLS = 1440
PROJ_COLS = 1536
GRID_W = 64
ROPE_BASE = 10000.0
MOD_ROWS = 16
ADAM_LR, ADAM_B1, ADAM_B2, ADAM_EPS, ADAM_WD, ADAM_STEP = 0.001, 0.9, 0.999, 1e-08, 0.01, 10
N_CHIPS = 4
LANES = 128
V7X_VMEM_LIMIT = 56 * 1024 * 1024
GELU_C = math.sqrt(2.0 / math.pi)

SHARDED = ("w_ada", "ffn1_w1", "ffn1_w3", "ffn1_w2", "w_in", "w_uq", "w_ukv", "w_out", "ffn2_w1", "ffn2_w3", "ffn2_w2")
ROW_SHARDED = ("ffn1_w2", "w_out", "ffn2_w2")
T_WEIGHTS = ("ffn1_w1", "ffn1_w3", "ffn2_w1", "ffn2_w3", "w_in", "w_uq")
FIRST_WEIGHTS = ("ffn1_w1", "ffn1_w3", "ffn1_w2")
MIX_WEIGHTS = ("w_in", "w_uq", "w_ukv", "w_out")
LAST_WEIGHTS = ("ffn2_w1", "ffn2_w3", "ffn2_w2")
SMALL = ("c_ctx", "b_ada", "norm1_w", "norm2_w", "q_a_norm_w", "kv_a_norm_w", "q_norm_w", "k_norm_w", "v_norm_w",
         "w_s", "b_s", "norm3_w")
WEIGHTS = ("c_ctx", "w_ada", "b_ada", "norm1_w", "ffn1_w1", "ffn1_w3", "ffn1_w2", "norm2_w", "w_in", "q_a_norm_w",
           "w_uq", "kv_a_norm_w", "w_ukv", "q_norm_w", "k_norm_w", "v_norm_w", "w_s", "b_s", "w_out", "norm3_w",
           "ffn2_w1", "ffn2_w3", "ffn2_w2")


def _round_up(n, m):
    return (n + m - 1) // m * m


def _div_tile(n, target, mult):
    best = None
    for t in range(mult, min(n, target) + 1, mult):
        if n % t == 0:
            best = t
    return n if best is None else best


def _dot(a, b):
    return lax.dot_general(a, b, (((1,), (0,)), ((), ())), preferred_element_type=F32)


def _dot_nt(a, b):
    return lax.dot_general(a, b, (((1,), (1,)), ((), ())), preferred_element_type=F32)


def _dot_tn(a, b):
    return lax.dot_general(a, b, (((0,), (0,)), ((), ())), preferred_element_type=F32)


def _sigmoid(x):
    return 1.0 / (1.0 + jnp.exp(-x))


def _gelu(x):
    return 0.5 * x * (1.0 + jnp.tanh(GELU_C * (x + 0.044715 * x * x * x)))


def _gelu_grad(x):
    t = jnp.tanh(GELU_C * (x + 0.044715 * x * x * x))
    return 0.5 * (1.0 + t) + 0.5 * x * (1.0 - t * t) * (GELU_C * (1.0 + 3 * 0.044715 * x * x))


def _rope(x, cos, sin_a, sin_b):
    return x * cos + pltpu.roll(x, 8, 1) * sin_a + pltpu.roll(x, HEAD_PAD - 8, 1) * sin_b


def _rope_t(d, cos, sin_a, sin_b):
    return d * cos + pltpu.roll(d * sin_a, HEAD_PAD - 8, 1) + pltpu.roll(d * sin_b, 8, 1)


def _rope3(x, cos, sin_a, sin_b):
    return x * cos + pltpu.roll(x, 8, 2) * sin_a + pltpu.roll(x, HEAD_PAD - 8, 2) * sin_b


def _rope3_t(d, cos, sin_a, sin_b):
    return d * cos + pltpu.roll(d * sin_a, HEAD_PAD - 8, 2) + pltpu.roll(d * sin_b, 8, 2)


def _group_sum(x, ones_ref):
    hi = x.astype(BF16)
    lo = (x - hi.astype(F32)).astype(BF16)
    return _dot(hi, ones_ref[...]) + _dot(lo, ones_ref[...])


def _params(n_axes):
    return pltpu.CompilerParams(dimension_semantics=("arbitrary",) * n_axes, vmem_limit_bytes=V7X_VMEM_LIMIT)


def _whole(shape):
    nd = len(shape)
    return pl.BlockSpec(shape, lambda *_: (0,) * nd, pipeline_mode=pl.Buffered(1))


def _sds(shape, dtype):
    return jax.ShapeDtypeStruct(shape, dtype)


def _token_tile(s, ctx):
    return _div_tile(math.gcd(s, ctx), 256, CHUNK)


def _other_chips(x, y):
    return [(1 - x, y), (x, 1 - y), (1 - x, 1 - y)]


def _exch_copies(kind, srcs, dsts, send_sems, recv_sems, local_sems):
    x, y, c = lax.axis_index("x"), lax.axis_index("y"), lax.axis_index("c")
    me = 2 * x + y
    local, sends, arrivals = [], [], []
    for w, (src, dst) in enumerate(zip(srcs, dsts)):
        own = src if kind == "gather" else src.at[me]
        local.append(pltpu.make_async_copy(own, dst.at[me], local_sems.at[w]))
        for k, (px, py) in enumerate(_other_chips(x, y)):
            sem = dict(send_sem=send_sems.at[3 * w + k], recv_sem=recv_sems.at[3 * w + k], device_id=(px, py, c),
                       device_id_type=pl.DeviceIdType.MESH)
            out = src if kind == "gather" else src.at[2 * px + py]
            sends.append(pltpu.make_async_remote_copy(src_ref=out, dst_ref=dst.at[me], **sem))
            arrivals.append(pltpu.make_async_remote_copy(src_ref=own, dst_ref=dst.at[2 * px + py], **sem))
    return local, sends, arrivals


def _exch_start(kind, srcs, dsts, sems):
    local, sends, _ = _exch_copies(kind, srcs, dsts, *sems)
    for cp in local + sends:
        cp.start()


def _exch_wait(kind, srcs, dsts, sems):
    local, sends, arrivals = _exch_copies(kind, srcs, dsts, *sems)
    for cp in arrivals:
        cp.wait_recv()
    for cp in sends:
        cp.wait_send()
    for cp in local:
        cp.wait()


def _exch_scratch(n):
    return [pltpu.SemaphoreType.DMA((3 * n,)), pltpu.SemaphoreType.DMA((3 * n,)), pltpu.SemaphoreType.DMA((n,))]


def _exch_shapes(kind, arrays):
    return [_sds((N_CHIPS,) + a.shape if kind == "gather" else a.shape, a.dtype) for a in arrays]


def _hosted_call(body, name, grid, in_specs, out_specs, out_shape, operands, scratch=(), exch=None):
    n_axes = len(grid)
    if exch is None:
        outs = pl.pallas_call(body, name=name, grid=grid, in_specs=list(in_specs), out_specs=list(out_specs),
                              out_shape=list(out_shape), scratch_shapes=list(scratch),
                              compiler_params=_params(n_axes))(*operands)
        return list(outs), []
    kind, arrays = exch
    n_in, n_out, n_sc, n_ex = len(in_specs), len(out_specs), len(scratch), len(arrays)

    def hosted(*refs):
        cin, ein = refs[:n_in], refs[n_in:n_in + n_ex]
        o0 = n_in + n_ex
        cout, eout = refs[o0:o0 + n_out], refs[o0 + n_out:o0 + n_out + n_ex]
        rest = refs[o0 + n_out + n_ex:]
        csc, sems = rest[:n_sc], rest[n_sc:]
        first = functools.reduce(jnp.logical_and, [pl.program_id(a) == 0 for a in range(n_axes)])
        last = functools.reduce(jnp.logical_and, [pl.program_id(a) == grid[a] - 1 for a in range(n_axes)])

        @pl.when(first)
        def _():
            _exch_start(kind, ein, eout, sems)

        body(*cin, *cout, *csc)

        @pl.when(last)
        def _():
            _exch_wait(kind, ein, eout, sems)

    any_spec = pl.BlockSpec(memory_space=pl.ANY)
    outs = pl.pallas_call(
        hosted, name=name, grid=grid, in_specs=list(in_specs) + [any_spec] * n_ex,
        out_specs=list(out_specs) + [any_spec] * n_ex, out_shape=list(out_shape) + _exch_shapes(kind, arrays),
        scratch_shapes=list(scratch) + _exch_scratch(n_ex), compiler_params=_params(n_axes),
    )(*operands, *arrays)
    return list(outs[:n_out]), list(outs[n_out:])


class _TokenTiles:
    def __init__(self, t, tc, tm):
        self.n_lat, self.n_ctx = t // tm, tc // tm
        self.n_all = self.n_lat + self.n_ctx

    def tile(self, i):
        return (i + self.n_lat) % self.n_all if self.n_ctx else i

    def is_lat(self, i):
        return self.tile(i) < self.n_lat

    def row(self, i):
        return (self.tile(i), 0)

    def lat_row(self, i):
        return (jnp.where(self.is_lat(i), self.tile(i), 0), 0) if self.n_ctx else (i, 0)

    def ctx_row(self, i):
        return (jnp.where(self.is_lat(i), self.n_ctx - 1, self.tile(i) - self.n_lat), 0)


def _ffn_fwd(x_lat, x_ctx, mod, nw, w1, w3, w2, k0, s, nb, tm, name, target=None, exch=None):
    t, d = x_lat.shape
    tc = 0 if x_ctx is None else x_ctx.shape[0]
    f = w1.shape[0]
    tiles = _TokenTiles(t, tc, tm)
    n_x = 2 if tc else 1
    n_t = 0 if target is None else 1
    assert not (tc and n_t)

    def body(*refs):
        x_ref = refs[0]
        t_ref = refs[n_x] if n_t else None
        mod_ref, nw_ref, w1_ref, w3_ref, w2_ref, o_ref, a_ref, b_ref, y_ref = refs[n_x + n_t:n_x + n_t + 9]
        i = pl.program_id(0)
        g = jnp.minimum((tiles.tile(i) * tm) // s, nb)
        shift = mod_ref[g, pl.ds(k0, 1), :]
        scale = mod_ref[g, pl.ds(k0 + 1, 1), :]
        gate = mod_ref[g, pl.ds(k0 + 2, 1), :]
        x = jnp.where(tiles.is_lat(i), x_ref[...], refs[1][...]) if tc else x_ref[...]
        r = lax.rsqrt(jnp.mean(x * x, axis=-1, keepdims=True) + EPS)
        hb = ((x * r * nw_ref[...]) * (1.0 + scale) + shift).astype(BF16)
        a = _dot_nt(hb, w1_ref[...])
        b = _dot_nt(hb, w3_ref[...])
        gb = (a * _sigmoid(a) * b).astype(BF16)
        y = _dot(gb, w2_ref[...])
        out = x + (0.5 * gate) * y
        a_ref[...] = a.astype(BF16)
        b_ref[...] = b.astype(BF16)
        y_ref[...] = y.astype(BF16)
        if n_t:
            loss_ref, acc_ref = refs[-2:]

            @pl.when(i == 0)
            def _():
                acc_ref[...] = jnp.zeros_like(acc_ref)

            e = out - t_ref[...]
            o_ref[...] = e * (1.0 / d)
            acc_ref[...] += jnp.sum(e * e, axis=0, keepdims=True)

            @pl.when(i == tiles.n_all - 1)
            def _():
                loss_ref[...] = (0.5 / d) * jnp.sum(acc_ref[...], axis=-1, keepdims=True)
        else:
            o_ref[...] = out

    td = pl.BlockSpec((tm, d), tiles.row)
    tf = pl.BlockSpec((tm, f), tiles.row)
    return _hosted_call(
        body, name, (tiles.n_all,),
        [pl.BlockSpec((tm, d), tiles.lat_row)] + ([pl.BlockSpec((tm, d), tiles.ctx_row)] if tc else []) + [td] * n_t
        + [_whole(mod.shape), _whole(nw.shape), _whole(w1.shape), _whole(w3.shape), _whole(w2.shape)],
        [td, tf, tf, td] + [pl.BlockSpec((1, 1), lambda i: (0, 0))] * n_t,
        [_sds((t + tc, d), F32), _sds((t + tc, f), BF16), _sds((t + tc, f), BF16), _sds((t + tc, d), BF16)]
        + [_sds((1, 1), F32)] * n_t,
        (x_lat,) + ((x_ctx,) if tc else ()) + ((target,) if n_t else ()) + (mod, nw, w1, w3, w2),
        scratch=[pltpu.VMEM((1, d), F32)] * n_t, exch=exch)


def _ffn_bwd(dout, x_lat, x_ctx, a, b, y, mod, nw, w1, w3, w2, k0, s, nb, tm, name, exch=None):
    t, d = x_lat.shape
    tc = 0 if x_ctx is None else x_ctx.shape[0]
    f = w1.shape[0]
    nch = 2 if (f // 2) % LANES == 0 and f % 2 == 0 else 1
    fc = f // nch
    tiles = _TokenTiles(t, tc, tm)
    n_x = 2 if tc else 1

    def body(*refs):
        do_ref, x_ref = refs[0], refs[1]
        (a_ref, b_ref, y_ref, mod_ref, nw_ref, w1_ref, w3_ref, w2_ref,
         dx_ref, h_ref, g_ref, da_ref, db_ref, dy_ref, dmod_ref, dnw_ref) = refs[1 + n_x:]
        i = pl.program_id(0)

        @pl.when(i == 0)
        def _():
            dmod_ref[...] = jnp.zeros_like(dmod_ref)
            dnw_ref[...] = jnp.zeros_like(dnw_ref)

        g = jnp.minimum((tiles.tile(i) * tm) // s, nb)
        shift = mod_ref[g, pl.ds(k0, 1), :]
        scale = mod_ref[g, pl.ds(k0 + 1, 1), :]
        gate = mod_ref[g, pl.ds(k0 + 2, 1), :]
        x = jnp.where(tiles.is_lat(i), x_ref[...], refs[2][...]) if tc else x_ref[...]
        dout_v = do_ref[...]
        r = lax.rsqrt(jnp.mean(x * x, axis=-1, keepdims=True) + EPS)
        xh = x * r
        n = xh * nw_ref[...]
        h_ref[...] = (n * (1.0 + scale) + shift).astype(BF16)
        dyb = ((0.5 * gate) * dout_v).astype(BF16)
        dy_ref[...] = dyb
        dmod_ref[g, pl.ds(k0 + 2, 1), :] += 0.5 * jnp.sum(dout_v * y_ref[...].astype(F32), axis=0, keepdims=True)
        dh = jnp.zeros((tm, d), F32)
        for c in range(nch):
            sl = slice(c * fc, (c + 1) * fc)
            dg = _dot_nt(dyb, w2_ref[sl, :])
            av = a_ref[:, sl].astype(F32)
            bv = b_ref[:, sl].astype(F32)
            sig = _sigmoid(av)
            sa = av * sig
            g_ref[:, sl] = (sa * bv).astype(BF16)
            dab = (dg * bv * (sig * (1.0 + av * (1.0 - sig)))).astype(BF16)
            dbb = (dg * sa).astype(BF16)
            da_ref[:, sl] = dab
            db_ref[:, sl] = dbb
            dh = dh + _dot(dab, w1_ref[sl, :]) + _dot(dbb, w3_ref[sl, :])
        dmod_ref[g, pl.ds(k0, 1), :] += jnp.sum(dh, axis=0, keepdims=True)
        dmod_ref[g, pl.ds(k0 + 1, 1), :] += jnp.sum(dh * n, axis=0, keepdims=True)
        dn = dh * (1.0 + scale)
        dnw_ref[...] += jnp.sum(dn * xh, axis=0, keepdims=True)
        dxh = dn * nw_ref[...]
        dx_ref[...] = dout_v + r * (dxh - xh * jnp.mean(dxh * xh, axis=-1, keepdims=True))

    td = pl.BlockSpec((tm, d), tiles.row)
    tf = pl.BlockSpec((tm, f), tiles.row)
    lat = pl.BlockSpec((tm, d), tiles.lat_row)
    ta = t + tc
    return _hosted_call(
        body, name, (tiles.n_all,),
        [td, lat] + ([pl.BlockSpec((tm, d), tiles.ctx_row)] if tc else [])
        + [tf, tf, td, _whole(mod.shape), _whole(nw.shape), _whole(w1.shape), _whole(w3.shape), _whole(w2.shape)],
        [lat, td, tf, tf, tf, td, pl.BlockSpec(mod.shape, lambda i: (0, 0, 0)), pl.BlockSpec((1, d), lambda i: (0, 0))],
        [_sds((t, d), F32), _sds((ta, d), BF16), _sds((ta, f), BF16), _sds((ta, f), BF16), _sds((ta, f), BF16),
         _sds((ta, d), BF16), _sds(mod.shape, F32), _sds((1, d), F32)],
        (dout, x_lat) + ((x_ctx,) if tc else ()) + (a, b, y, mod, nw, w1, w3, w2), exch=exch)


def _mm_tn(a, b, rows, name, exch=None):
    m = a.shape[1]
    n = b.shape[1]
    bm = _div_tile(m, 1408, LANES)
    bn = _div_tile(n, 1408, LANES)
    bk = _div_tile(rows, 2304, LANES)
    nk = rows // bk

    def body(a_ref, b_ref, o_ref, acc_ref):
        k = pl.program_id(2)

        @pl.when(k == 0)
        def _():
            acc_ref[...] = jnp.zeros_like(acc_ref)

        acc_ref[...] += _dot_tn(a_ref[...], b_ref[...])

        @pl.when(k == nk - 1)
        def _():
            o_ref[...] = acc_ref[...].astype(BF16)

    (out,), got = _hosted_call(
        body, name, (m // bm, n // bn, nk),
        [pl.BlockSpec((bk, bm), lambda i, j, k: (k, i)), pl.BlockSpec((bk, bn), lambda i, j, k: (k, j))],
        [pl.BlockSpec((bm, bn), lambda i, j, k: (i, j))], [_sds((m, n), BF16)], (a, b),
        scratch=[pltpu.VMEM((bm, bn), F32)], exch=exch)
    return out if exch is None else (out, got)


def _mixin_fwd(xs, mod, nw, wp, s, nb, tm):
    t, d = xs.shape

    def body(x_ref, mod_ref, nw_ref, wp_ref, h_ref, p_ref):
        g = jnp.minimum((pl.program_id(0) * tm) // s, nb)
        shift = mod_ref[g, pl.ds(3, 1), :]
        scale = mod_ref[g, pl.ds(4, 1), :]
        x = x_ref[...]
        r = lax.rsqrt(jnp.mean(x * x, axis=-1, keepdims=True) + EPS)
        hb = ((x * r * nw_ref[...]) * (1.0 + scale) + shift).astype(BF16)
        h_ref[...] = hb
        p_ref[...] = _dot_nt(hb, wp_ref[...]).astype(BF16)

    row = lambda i: (i, 0)
    return pl.pallas_call(
        body, name="mixin_fwd", grid=(t // tm,),
        in_specs=[pl.BlockSpec((tm, d), row), _whole(mod.shape), _whole(nw.shape), _whole(wp.shape)],
        out_specs=[pl.BlockSpec((tm, d), row), pl.BlockSpec((tm, PROJ_COLS), row)],
        out_shape=[_sds((t, d), BF16), _sds((t, PROJ_COLS), BF16)], compiler_params=_params(1),
    )(xs, mod, nw, wp)


def _mixin_bwd(dp0, dpu, dpv, xs, dres, mod, nw, wp, s, nb, tm):
    t_all, d = xs.shape
    nlat = dres.shape[0] // tm

    def body(p0_ref, pu_ref, pv_ref, x_ref, dr_ref, mod_ref, nw_ref, wp_ref, dx_ref, dmod_ref, dnw_ref):
        i = pl.program_id(0)

        @pl.when(i == 0)
        def _():
            dmod_ref[...] = jnp.zeros_like(dmod_ref)
            dnw_ref[...] = jnp.zeros_like(dnw_ref)

        lat = i < nlat
        g = jnp.minimum((i * tm) // s, nb)
        scale = mod_ref[g, pl.ds(4, 1), :]
        dh = _dot(p0_ref[...], wp_ref[0:512, :])
        extra = _dot(pu_ref[...], wp_ref[512:1024, :]) + _dot(pv_ref[...], wp_ref[1024:1536, :])
        dh = dh + jnp.where(lat, extra, 0.0)
        x = x_ref[...]
        r = lax.rsqrt(jnp.mean(x * x, axis=-1, keepdims=True) + EPS)
        xh = x * r
        n = xh * nw_ref[...]
        dmod_ref[g, pl.ds(3, 1), :] += jnp.sum(dh, axis=0, keepdims=True)
        dmod_ref[g, pl.ds(4, 1), :] += jnp.sum(dh * n, axis=0, keepdims=True)
        dn = dh * (1.0 + scale)
        dnw_ref[...] += jnp.sum(dn * xh, axis=0, keepdims=True)
        dxh = dn * nw_ref[...]
        dx_ref[...] = jnp.where(lat, dr_ref[...], 0.0) + r * (dxh - xh * jnp.mean(dxh * xh, axis=-1, keepdims=True))

    row = lambda i: (i, 0)
    lrow = lambda i: (jnp.minimum(i, nlat - 1), 0)
    return pl.pallas_call(
        body, name="mixin_bwd", grid=(t_all // tm,),
        in_specs=[pl.BlockSpec((tm, 512), row), pl.BlockSpec((tm, 512), lrow), pl.BlockSpec((tm, 512), lrow),
                  pl.BlockSpec((tm, d), row), pl.BlockSpec((tm, d), lrow), _whole(mod.shape), _whole(nw.shape),
                  _whole(wp.shape)],
        out_specs=[pl.BlockSpec((tm, d), row), pl.BlockSpec(mod.shape, lambda i: (0, 0, 0)),
                   pl.BlockSpec((1, d), lambda i: (0, 0))],
        out_shape=[_sds((t_all, d), F32), _sds(mod.shape, F32), _sds((1, d), F32)], compiler_params=_params(1),
    )(dp0, dpu, dpv, xs, dres, mod, nw, wp)


def _prep_fwd(proj, row0, nb, s, pos0, sk, key0, into, tabs, wq, wk, wv, kvaw, qaw, qnw, knw, tm, with_q, name):
    nblk = s // tm
    n_into = 0 if into is None else 2

    def body(p_ref, cos_ref, sa_ref, sb_ref, wq_ref, wk_ref, wv_ref, kvaw_ref, qaw_ref, qnw_ref, knw_ref, *rest):
        outs, heads_ref = rest[n_into:-1], rest[-1]
        q_ref, k_ref, v_ref = outs if with_q else (None,) + outs
        cos, sin_a, sin_b = cos_ref[...][None], sa_ref[...][None], sb_ref[...][None]

        def normed_roped(w_ref, src, extra, nw_ref, o_ref, post):
            for h in range(HEADS):
                heads_ref[h] = _dot_nt(src, w_ref[h]) if extra is None else _dot(src, w_ref[h])
            xp = heads_ref[...] if extra is None else heads_ref[...] + extra[None]
            r = lax.rsqrt(jnp.sum(xp * xp, axis=-1, keepdims=True) * (1.0 / QK_HEAD) + EPS)
            o_ref[...] = _rope3(xp * r * (nw_ref[...] * post)[None], cos, sin_a, sin_b).astype(BF16)

        ckv = p_ref[:, 0:128].astype(F32)
        rkv = lax.rsqrt(jnp.mean(ckv * ckv, axis=-1, keepdims=True) + EPS)
        ckvb = (ckv * rkv * kvaw_ref[...]).astype(BF16)
        normed_roped(wk_ref, ckvb, p_ref[:, 128:256].astype(F32), knw_ref, k_ref, 1.0)
        for j in range(HEADS // 2):
            v_ref[j] = _dot(ckvb, wv_ref[j]).astype(BF16)
        if with_q:
            cq = p_ref[:, 256:512].astype(F32)
            rq = lax.rsqrt(jnp.mean(cq * cq, axis=-1, keepdims=True) + EPS)
            normed_roped(wq_ref, (cq * rq * qaw_ref[...]).astype(BF16), None, qnw_ref, q_ref, SOFTMAX_SCALE)

    tab = pl.BlockSpec((tm, HEAD_PAD), lambda i: (pos0 + i % nblk, 0))
    qspec = pl.BlockSpec((None, HEADS, tm, HEAD_PAD), lambda i: (i // nblk, 0, i % nblk, 0))
    kspec = pl.BlockSpec((None, HEADS, tm, HEAD_PAD), lambda i: (i // nblk, 0, key0 + i % nblk, 0))
    vspec = pl.BlockSpec((None, HEADS // 2, tm, HEAD_PAD), lambda i: (i // nblk, 0, key0 + i % nblk, 0))
    qshape = _sds((nb, HEADS, s, HEAD_PAD), BF16)
    kshape = _sds((nb, HEADS, sk, HEAD_PAD), BF16)
    vshape = _sds((nb, HEADS // 2, sk, HEAD_PAD), BF16)
    n_q = 1 if with_q else 0
    return pl.pallas_call(
        body, name=name, grid=(nb * nblk,),
        in_specs=[pl.BlockSpec((tm, 512), lambda i: (row0 + i, 0)), tab, tab, tab, _whole(wq.shape), _whole(wk.shape),
                  _whole(wv.shape), _whole(kvaw.shape), _whole(qaw.shape), _whole(qnw.shape), _whole(knw.shape)]
        + [pl.BlockSpec(memory_space=pl.ANY)] * n_into,
        out_specs=([qspec] if with_q else []) + [kspec, vspec],
        out_shape=([qshape] if with_q else []) + [kshape, vshape],
        scratch_shapes=[pltpu.VMEM((HEADS, tm, HEAD_PAD), F32)],
        input_output_aliases={11: n_q, 12: n_q + 1} if n_into else {}, compiler_params=_params(1),
    )(proj, *tabs, wq, wk, wv, kvaw, qaw, qnw, knw, *(into or ()))


def _prep_bwd(proj, row0, nb, s, pos0, key0, dp_rows, dp_into, tabs, wq, wk, wv, kvaw, qaw, qnw, knw, dq, dk, dv, init, tm,
              name):
    nblk = s // tm
    with_q = dq is not None
    n_init = 0 if init is None else len(init)
    n_into = 0 if dp_into is None else 1

    def body(*refs):
        p_ref, cos_ref, sa_ref, sb_ref, wq_ref, wk_ref, wv_ref, kvaw_ref, qaw_ref, qnw_ref, knw_ref = refs[:11]
        rest = list(refs[11:])
        dq_ref = rest.pop(0) if with_q else None
        dk_ref, dv_ref = rest.pop(0), rest.pop(0)
        init_refs = [rest.pop(0) for _ in range(n_init)]
        if n_into:
            rest.pop(0)
        dp_ref = rest.pop(0)
        if with_q:
            dwq_ref, dqaw_ref, dqnw_ref = rest.pop(0), rest.pop(0), rest.pop(0)
        dwk_ref, dwv_ref, dkvaw_ref, dknw_ref, heads_ref, dhb_ref = rest
        accs = [dwk_ref, dwv_ref, dkvaw_ref, dknw_ref]

        @pl.when(pl.program_id(0) == 0)
        def _():
            for k, acc in enumerate(accs):
                acc[...] = init_refs[k][...] if n_init else jnp.zeros_like(acc)
            if with_q:
                dwq_ref[...] = jnp.zeros_like(dwq_ref)
                dqaw_ref[...] = jnp.zeros_like(dqaw_ref)
                dqnw_ref[...] = jnp.zeros_like(dqnw_ref)

        cos, sin_a, sin_b = cos_ref[...][None], sa_ref[...][None], sb_ref[...][None]
        lane = lax.broadcasted_iota(jnp.int32, (tm, HEAD_PAD), 1)
        rope_lanes = (lane >= QK_NOPE) & (lane < QK_HEAD)

        def heads_bwd(w_ref, src, extra, nw_ref, d_ref, dnw_ref, dw_ref, post):
            w_t = extra is None
            for h in range(HEADS):
                heads_ref[h] = _dot_nt(src, w_ref[h]) if w_t else _dot(src, w_ref[h])
            xp = heads_ref[...] if extra is None else heads_ref[...] + extra[None]
            r = lax.rsqrt(jnp.sum(xp * xp, axis=-1, keepdims=True) * (1.0 / QK_HEAD) + EPS)
            xh = xp * r
            dn = _rope3_t(d_ref[...], cos, sin_a, sin_b)
            dnw_ref[...] += post * jnp.sum(jnp.sum(dn * xh, axis=0), axis=0, keepdims=True)
            dxh = dn * (nw_ref[...] * post)[None]
            dxp = r * (dxh - xh * (jnp.sum(dxh * xh, axis=-1, keepdims=True) * (1.0 / QK_HEAD)))
            dhb_ref[...] = dxp.astype(BF16)
            dsrc = jnp.zeros((tm, src.shape[1]), F32)
            for h in range(HEADS):
                dsrc = dsrc + (_dot(dhb_ref[h], w_ref[h]) if w_t else _dot_nt(dhb_ref[h], w_ref[h]))
                dw_ref[h] += _dot_tn(src, dhb_ref[h])
            return dsrc, jnp.sum(dxp, axis=0)

        ckv = p_ref[:, 0:128].astype(F32)
        rkv = lax.rsqrt(jnp.mean(ckv * ckv, axis=-1, keepdims=True) + EPS)
        ckvh = ckv * rkv
        ckvb = (ckvh * kvaw_ref[...]).astype(BF16)
        dckv, dkp_sum = heads_bwd(wk_ref, ckvb, p_ref[:, 128:256].astype(F32), knw_ref, dk_ref, dknw_ref, dwk_ref,
                                  1.0)
        for j in range(HEADS // 2):
            dvb = dv_ref[j].astype(BF16)
            dckv = dckv + _dot_nt(dvb, wv_ref[j])
            dwv_ref[j] += _dot_tn(ckvb, dvb)
        dkvaw_ref[...] += jnp.sum(dckv * ckvh, axis=0, keepdims=True)
        dch = dckv * kvaw_ref[...]
        dp_ref[:, 0:128] = (rkv * (dch - ckvh * jnp.mean(dch * ckvh, axis=-1, keepdims=True))).astype(BF16)
        dp_ref[:, 128:256] = jnp.where(rope_lanes, dkp_sum, 0.0).astype(BF16)
        if with_q:
            cq = p_ref[:, 256:512].astype(F32)
            rq = lax.rsqrt(jnp.mean(cq * cq, axis=-1, keepdims=True) + EPS)
            cqh = cq * rq
            cqb = (cqh * qaw_ref[...]).astype(BF16)
            dcq, _ = heads_bwd(wq_ref, cqb, None, qnw_ref, dq_ref, dqnw_ref, dwq_ref, SOFTMAX_SCALE)
            dqaw_ref[...] += jnp.sum(dcq * cqh, axis=0, keepdims=True)
            dqc = dcq * qaw_ref[...]
            dp_ref[:, 256:512] = (rq * (dqc - cqh * jnp.mean(dqc * cqh, axis=-1, keepdims=True))).astype(BF16)
        else:
            dp_ref[:, 256:512] = jnp.zeros((tm, Q_LORA), BF16)

    tab = pl.BlockSpec((tm, HEAD_PAD), lambda i: (pos0 + i % nblk, 0))
    qspec = pl.BlockSpec((None, HEADS, tm, HEAD_PAD), lambda i: (i // nblk, 0, i % nblk, 0))
    kspec = pl.BlockSpec((None, HEADS, tm, HEAD_PAD), lambda i: (i // nblk, 0, key0 + i % nblk, 0))
    vspec = pl.BlockSpec((None, HEADS // 2, tm, HEAD_PAD), lambda i: (i // nblk, 0, key0 + i % nblk, 0))

    def acc_spec(shape):
        nd = len(shape)
        return pl.BlockSpec(shape, lambda i: (0,) * nd)

    acc_shapes = [(HEADS, KV_LORA, HEAD_PAD), (HEADS // 2, KV_LORA, HEAD_PAD), (1, KV_LORA), (1, HEAD_PAD)]
    q_shapes = [(HEADS, Q_LORA, HEAD_PAD), (1, Q_LORA), (1, HEAD_PAD)] if with_q else []
    out_shapes = [(dp_rows, 512)] + q_shapes + acc_shapes
    n_before = 11 + (1 if with_q else 0) + 2 + n_init
    return pl.pallas_call(
        body, name=name, grid=(nb * nblk,),
        in_specs=[pl.BlockSpec((tm, 512), lambda i: (row0 + i, 0)), tab, tab, tab, _whole(wq.shape), _whole(wk.shape),
                  _whole(wv.shape), _whole(kvaw.shape), _whole(qaw.shape), _whole(qnw.shape), _whole(knw.shape)]
        + ([qspec] if with_q else []) + [kspec, vspec] + [_whole(a.shape) for a in (init or [])]
        + [pl.BlockSpec(memory_space=pl.ANY)] * n_into,
        out_specs=[pl.BlockSpec((tm, 512), lambda i: (row0 + i, 0))] + [acc_spec(sh) for sh in q_shapes + acc_shapes],
        out_shape=[_sds(out_shapes[0], BF16)] + [_sds(sh, F32) for sh in out_shapes[1:]],
        scratch_shapes=[pltpu.VMEM((HEADS, tm, HEAD_PAD), F32), pltpu.VMEM((HEADS, tm, HEAD_PAD), BF16)],
        input_output_aliases={n_before: 0} if n_into else {}, compiler_params=_params(1),
    )(proj, *tabs, wq, wk, wv, kvaw, qaw, qnw, knw, *([dq] if with_q else []), dk, dv, *(init or []),
      *([dp_into] if n_into else []))


def _attn_fwd(q, k, v, tq, exch=None):
    nb, _, s, _ = q.shape
    sk = k.shape[2]
    nq = s // tq

    def body(q_ref, k_ref, v_ref, o_ref, lse_ref):
        lane = lax.broadcasted_iota(jnp.int32, (tq, HEAD_PAD), 1)
        vv = v_ref[...]
        outs = []
        for hh in range(2):
            sc = _dot_nt(q_ref[hh], k_ref[hh])
            m = jnp.max(sc, axis=-1, keepdims=True)
            p = jnp.exp2(sc - m)
            l = jnp.sum(p, axis=-1, keepdims=True)
            outs.append(_dot(p.astype(BF16), vv) / l)
            lse_ref[hh] = m + jnp.log2(l)
        o_ref[...] = jnp.where(lane < V_HEAD, outs[0], outs[1]).astype(BF16)

    (o, lse), got = _hosted_call(
        body, "attn_fwd", (nb, HEADS // 2, nq),
        [pl.BlockSpec((None, 2, tq, HEAD_PAD), lambda b, j, i: (b, j, i, 0)),
         pl.BlockSpec((None, 2, sk, HEAD_PAD), lambda b, j, i: (b, j, 0, 0)),
         pl.BlockSpec((None, None, sk, HEAD_PAD), lambda b, j, i: (b, j, 0, 0))],
        [pl.BlockSpec((tq, HEAD_PAD), lambda b, j, i: (b * nq + i, j)),
         pl.BlockSpec((None, 2, tq, 1), lambda b, j, i: (b, j, i, 0))],
        [_sds((nb * s, MLA_W), BF16), _sds((nb, HEADS, s, 1), F32)], (q, k, v), exch=exch)
    return o, lse, got


def _attn_bwd(q, k, v, do, o, lse, tq, exch=None):
    nb, _, s, _ = q.shape
    sk = k.shape[2]
    nq = s // tq
    kc = _div_tile(sk, 2304, LANES)

    def body(q_ref, k_ref, v_ref, do_ref, o_ref, lse_ref, dq_ref, dk_ref, dv_ref):
        @pl.when(pl.program_id(2) == 0)
        def _():
            dk_ref[...] = jnp.zeros_like(dk_ref)
            dv_ref[...] = jnp.zeros_like(dv_ref)

        lane = lax.broadcasted_iota(jnp.int32, (tq, HEAD_PAD), 1)
        dov = do_ref[...]
        prod = dov.astype(F32) * o_ref[...].astype(F32)
        for hh in range(2):
            mine = (lane < V_HEAD) if hh == 0 else (lane >= V_HEAD)
            doh = jnp.where(mine, dov, jnp.zeros_like(dov))
            delta = jnp.sum(jnp.where(mine, prod, 0.0), axis=-1, keepdims=True)
            qh = q_ref[hh]
            q_ln2 = (qh.astype(F32) * LN2).astype(BF16)
            lse_h = lse_ref[hh]
            dq = jnp.zeros((tq, HEAD_PAD), F32)
            for c in range(sk // kc):
                rows = slice(c * kc, (c + 1) * kc)
                kv = k_ref[hh, rows, :]
                p = jnp.exp2(_dot_nt(qh, kv) - lse_h)
                dp = _dot_nt(doh, v_ref[rows, :])
                u = (p * (dp - delta)).astype(BF16)
                dq = dq + _dot(u, kv)
                dk_ref[hh, rows, :] += _dot_tn(u, q_ln2)
                dv_ref[rows, :] += _dot_tn(p.astype(BF16), doh)
            dq_ref[hh] = dq * LN2

    qspec = pl.BlockSpec((None, 2, tq, HEAD_PAD), lambda b, j, i: (b, j, i, 0))
    kspec = pl.BlockSpec((None, 2, sk, HEAD_PAD), lambda b, j, i: (b, j, 0, 0))
    vspec = pl.BlockSpec((None, None, sk, HEAD_PAD), lambda b, j, i: (b, j, 0, 0))
    ospec = pl.BlockSpec((tq, HEAD_PAD), lambda b, j, i: (b * nq + i, j))
    return _hosted_call(
        body, "attn_bwd", (nb, HEADS // 2, nq),
        [qspec, kspec, vspec, ospec, ospec, pl.BlockSpec((None, 2, tq, 1), lambda b, j, i: (b, j, i, 0))],
        [qspec, kspec, vspec], [_sds(q.shape, F32), _sds(k.shape, F32), _sds(v.shape, F32)], (q, k, v, do, o, lse),
        exch=exch)


def _group_masks(rows):
    lane = lax.broadcasted_iota(jnp.int32, (rows, GMLP_W), 1)
    return [(lane >= g * GROUP_DIM) & (lane < (g + 1) * GROUP_DIM) for g in range(GROUPS)]


def _gmlp_fwd(proj, t, wcat, bias, vnw, ones, tm):
    def body(u_ref, v_ref, wcat_ref, bias_ref, vnw_ref, ones_ref, o_ref):
        masks = _group_masks(CHUNK)
        gv = _gelu(v_ref[...].astype(F32))
        rv = lax.rsqrt(_group_sum(gv * gv, ones_ref) * (1.0 / GROUP_DIM) + EPS)
        vnb = (gv * rv * vnw_ref[...]).astype(BF16)
        for c in range(tm // CHUNK):
            rows = slice(c * CHUNK, (c + 1) * CHUNK)
            vc = vnb[rows]
            stack = jnp.concatenate([jnp.where(m, vc, jnp.zeros_like(vc)) for m in masks], axis=0)
            sp = _dot(wcat_ref[...], stack) + bias_ref[...]
            o_ref[rows, :] = (_gelu(u_ref[rows, :].astype(F32)) * sp).astype(BF16)

    return pl.pallas_call(
        body, name="gmlp_fwd", grid=(t // tm,),
        in_specs=[pl.BlockSpec((tm, GMLP_W), lambda i: (i, 1)), pl.BlockSpec((tm, GMLP_W), lambda i: (i, 2)),
                  _whole(wcat.shape), _whole(bias.shape), _whole(vnw.shape), _whole(ones.shape)],
        out_specs=pl.BlockSpec((tm, GMLP_W), lambda i: (i, 0)),
        out_shape=_sds((t, GMLP_W), BF16), compiler_params=_params(1),
    )(proj, proj, wcat, bias, vnw, ones)


def _gmlp_bwd(proj, dsg, wcat, wcat_t, bias, vnw, ones, tm):
    t = dsg.shape[0]

    def body(u_ref, v_ref, dsg_ref, wcat_ref, wcatt_ref, bias_ref, vnw_ref, ones_ref,
             du_ref, dv_ref, dws_ref, dbs_ref, dvnw_ref):
        @pl.when(pl.program_id(0) == 0)
        def _():
            dws_ref[...] = jnp.zeros_like(dws_ref)
            dbs_ref[...] = jnp.zeros_like(dbs_ref)
            dvnw_ref[...] = jnp.zeros_like(dvnw_ref)

        masks = _group_masks(CHUNK)
        v = v_ref[...].astype(F32)
        gv = _gelu(v)
        rv = lax.rsqrt(_group_sum(gv * gv, ones_ref) * (1.0 / GROUP_DIM) + EPS)
        xh = gv * rv
        vnb = (xh * vnw_ref[...]).astype(BF16)
        dvn_parts = []
        for c in range(tm // CHUNK):
            rows = slice(c * CHUNK, (c + 1) * CHUNK)
            vc = vnb[rows]
            stack = jnp.concatenate([jnp.where(m, vc, jnp.zeros_like(vc)) for m in masks], axis=0)
            sp = _dot(wcat_ref[...], stack) + bias_ref[...]
            u = u_ref[rows, :].astype(F32)
            dsg_c = dsg_ref[rows, :]
            du_ref[rows, :] = (dsg_c * sp * _gelu_grad(u)).astype(BF16)
            ds = dsg_c * _gelu(u)
            dstack = jnp.concatenate([jnp.where(m, ds, 0.0) for m in masks], axis=0)
            dbs_ref[...] += jnp.broadcast_to(jnp.sum(dstack, axis=-1, keepdims=True), dbs_ref.shape)
            dstb = dstack.astype(BF16)
            dvn_parts.append(_dot(wcatt_ref[...], dstb))
            dws_ref[...] += _dot_nt(dstb, vc)
        dvn = jnp.concatenate(dvn_parts, axis=0) if len(dvn_parts) > 1 else dvn_parts[0]
        dvnw_ref[...] += jnp.sum(dvn * xh, axis=0, keepdims=True)
        dxh = dvn * vnw_ref[...]
        gm = _group_sum(dxh * xh, ones_ref) * (1.0 / GROUP_DIM)
        dv_ref[...] = (rv * (dxh - xh * gm) * _gelu_grad(v)).astype(BF16)

    row = pl.BlockSpec((tm, GMLP_W), lambda i: (i, 0))
    return pl.pallas_call(
        body, name="gmlp_bwd", grid=(t // tm,),
        in_specs=[pl.BlockSpec((tm, GMLP_W), lambda i: (i, 1)), pl.BlockSpec((tm, GMLP_W), lambda i: (i, 2)), row,
                  _whole(wcat.shape), _whole(wcat_t.shape), _whole(bias.shape), _whole(vnw.shape), _whole(ones.shape)],
        out_specs=[row, row, pl.BlockSpec((GROUPS * CHUNK, CHUNK), lambda i: (0, 0)),
                   pl.BlockSpec((GROUPS * CHUNK, CHUNK), lambda i: (0, 0)), pl.BlockSpec((1, GMLP_W), lambda i: (0, 0))],
        out_shape=[_sds((t, GMLP_W), BF16), _sds((t, GMLP_W), BF16), _sds((GROUPS * CHUNK, CHUNK), F32),
                   _sds((GROUPS * CHUNK, CHUNK), F32), _sds((1, GMLP_W), F32)],
        compiler_params=_params(1),
    )(proj, proj, dsg, wcat, wcat_t, bias, vnw, ones)


def _mixout_fwd(o, sg, xs, mod, wout, s, tm):
    t = o.shape[0]
    d = xs.shape[1]

    def body(o_ref, sg_ref, x_ref, mod_ref, w_ref, x2_ref, mix_ref):
        g = (pl.program_id(0) * tm) // s
        gate = mod_ref[g, pl.ds(5, 1), :]
        mix = _dot(o_ref[...], w_ref[0:MLA_W, :]) + _dot(sg_ref[...], w_ref[MLA_W:MLA_W + GMLP_W, :])
        x2_ref[...] = x_ref[...] + gate * mix
        mix_ref[...] = mix.astype(BF16)

    row = lambda i: (i, 0)
    return pl.pallas_call(
        body, name="mixout_fwd", grid=(t // tm,),
        in_specs=[pl.BlockSpec((tm, MLA_W), row), pl.BlockSpec((tm, GMLP_W), row), pl.BlockSpec((tm, d), row),
                  _whole(mod.shape), _whole(wout.shape)],
        out_specs=[pl.BlockSpec((tm, d), row), pl.BlockSpec((tm, d), row)],
        out_shape=[_sds((t, d), F32), _sds((t, d), BF16)], compiler_params=_params(1),
    )(o, sg, xs, mod, wout)


def _mixout_bwd(dx2, mix, mod, wout, s, tm):
    t, d = dx2.shape

    def body(dx_ref, mix_ref, mod_ref, w_ref, dmix_ref, do_ref, dsg_ref, dmod_ref):
        i = pl.program_id(0)

        @pl.when(i == 0)
        def _():
            dmod_ref[...] = jnp.zeros_like(dmod_ref)

        g = (i * tm) // s
        gate = mod_ref[g, pl.ds(5, 1), :]
        dx = dx_ref[...]
        dmod_ref[g, pl.ds(5, 1), :] += jnp.sum(dx * mix_ref[...].astype(F32), axis=0, keepdims=True)
        dmb = (gate * dx).astype(BF16)
        dmix_ref[...] = dmb
        do_ref[...] = _dot_nt(dmb, w_ref[0:MLA_W, :]).astype(BF16)
        dsg_ref[...] = _dot_nt(dmb, w_ref[MLA_W:MLA_W + GMLP_W, :])

    row = lambda i: (i, 0)
    return pl.pallas_call(
        body, name="mixout_bwd", grid=(t // tm,),
        in_specs=[pl.BlockSpec((tm, d), row), pl.BlockSpec((tm, d), row), _whole(mod.shape), _whole(wout.shape)],
        out_specs=[pl.BlockSpec((tm, d), row), pl.BlockSpec((tm, MLA_W), row), pl.BlockSpec((tm, GMLP_W), row),
                   pl.BlockSpec(mod.shape, lambda i: (0, 0, 0))],
        out_shape=[_sds((t, d), BF16), _sds((t, MLA_W), BF16), _sds((t, GMLP_W), F32), _sds(mod.shape, F32)],
        compiler_params=_params(1),
    )(dx2, mix, mod, wout)


def _swap_cores(parts, name):
    n = len(parts)

    def body(*refs):
        srcs, outs, send_sems, recv_sems = refs[:n], refs[n:2 * n], refs[2 * n], refs[2 * n + 1]
        x, y, c = lax.axis_index("x"), lax.axis_index("y"), lax.axis_index("c")
        copies = [pltpu.make_async_remote_copy(
            src_ref=srcs[w], dst_ref=outs[w], send_sem=send_sems.at[w], recv_sem=recv_sems.at[w],
            device_id=(x, y, 1 - c), device_id_type=pl.DeviceIdType.MESH) for w in range(n)]
        for cp in copies:
            cp.start()
        for cp in copies:
            cp.wait()

    any_spec = pl.BlockSpec(memory_space=pl.ANY)
    return pl.pallas_call(
        body, name=name, in_specs=[any_spec] * n, out_specs=[any_spec] * n,
        out_shape=[_sds(p.shape, p.dtype) for p in parts],
        scratch_shapes=[pltpu.SemaphoreType.DMA((n,)), pltpu.SemaphoreType.DMA((n,))],
    )(*parts)


def _row_tile(r, c, mult):
    return _div_tile(r, max(mult, (1 << 16) // c), mult)


def _sum_slots(recv, name):
    _, r, c = recv.shape
    tr = _row_tile(r, c, 16)

    def body(r_ref, o_ref):
        f = lambda k: r_ref[k].astype(F32)
        o_ref[...] = ((f(0) + f(1)) + f(2)) + f(3)

    return pl.pallas_call(
        body, name=name, grid=(r // tr,),
        in_specs=[pl.BlockSpec((N_CHIPS, tr, c), lambda i: (0, i, 0))],
        out_specs=pl.BlockSpec((tr, c), lambda i: (i, 0)),
        out_shape=_sds((r, c), F32), compiler_params=_params(1),
    )(recv)


def _adamw(parts, w, m, v, name, exch=None):
    r, wd = w.shape
    tr = _row_tile(r, wd, 8)
    c1 = 1.0 / (1.0 - ADAM_B1 ** ADAM_STEP)
    c2 = 1.0 / (1.0 - ADAM_B2 ** ADAM_STEP)
    n_p = len(parts)

    def body(*refs):
        p_refs = refs[:n_p]
        w_ref, m_ref, v_ref, g_ref, d_ref, nm_ref, nv_ref = refs[n_p:]
        g = p_refs[0][...]
        for p_ref in p_refs[1:]:
            g = g + p_ref[...]
        nm = ADAM_B1 * m_ref[...] + (1.0 - ADAM_B1) * g
        nv = ADAM_B2 * v_ref[...] + (1.0 - ADAM_B2) * (g * g)
        g_ref[...] = g
        nm_ref[...] = nm
        nv_ref[...] = nv
        d_ref[...] = -ADAM_LR * ((nm * c1) / (jnp.sqrt(nv * c2) + ADAM_EPS) + ADAM_WD * w_ref[...])

    spec = pl.BlockSpec((tr, wd), lambda i: (i, 0))
    return _hosted_call(body, name, (r // tr,), [spec] * (n_p + 3), [spec] * 4, [_sds((r, wd), F32)] * 4,
                        (*parts, w, m, v), exch=exch)


def _all_peers(x, y, c):
    flips = [(dx, dy, dc) for dx in (0, 1) for dy in (0, 1) for dc in (0, 1)][1:]
    return [(1 - x if dx else x, 1 - y if dy else y, 1 - c if dc else c) for dx, dy, dc in flips]


def _first_exchange(shards, cc, w, b):
    n_w = len(shards)
    n = w.shape[1]

    def body(*refs):
        srcs, (cc_ref, w_ref, b_ref) = refs[:n_w], refs[n_w:n_w + 3]
        outs, (all_ref, tab_ref) = refs[n_w + 3:2 * n_w + 3], refs[2 * n_w + 3:2 * n_w + 5]
        (part_ref, ici_send, ici_recv, d2d_send, d2d_recv, local_sems, cc_send, cc_recv, tab_send,
         tab_recv) = refs[2 * n_w + 5:]
        x, y, c = lax.axis_index("x"), lax.axis_index("y"), lax.axis_index("c")
        chip, dev = 2 * x + y, 4 * x + 2 * y + c
        chips = _other_chips(x, y)
        peers = _all_peers(x, y, c)

        def half(wi, which):
            hr = shards[wi].shape[0] // 2
            return pl.ds(pl.multiple_of(which * hr, 16), hr)

        def over_ici(wi, k, arriving):
            px, py = chips[k]
            slot = 2 * px + py if arriving else chip
            return pltpu.make_async_remote_copy(
                src_ref=srcs[wi].at[half(wi, c)], dst_ref=outs[wi].at[slot, half(wi, c)],
                send_sem=ici_send.at[3 * wi + k], recv_sem=ici_recv.at[3 * wi + k], device_id=(px, py, c),
                device_id_type=pl.DeviceIdType.MESH)

        def to_sibling(wi, k, arriving):
            px, py = chips[k]
            rows = half(wi, 1 - c if arriving else c)
            return pltpu.make_async_remote_copy(
                src_ref=outs[wi].at[2 * px + py, rows], dst_ref=outs[wi].at[2 * px + py, rows],
                send_sem=d2d_send.at[3 * wi + k], recv_sem=d2d_recv.at[3 * wi + k], device_id=(x, y, 1 - c),
                device_id_type=pl.DeviceIdType.MESH)

        def cc_copy(k, peer, slot):
            return pltpu.make_async_remote_copy(
                src_ref=cc_ref, dst_ref=all_ref.at[slot], send_sem=cc_send.at[k], recv_sem=cc_recv.at[k],
                device_id=peer, device_id_type=pl.DeviceIdType.MESH)

        def rows_of(px, py):
            return part_ref.at[pl.ds(pl.multiple_of((4 * px + 2 * py + c) * MOD_ROWS, MOD_ROWS), MOD_ROWS)]

        def tab_copy(k, px, py, slot):
            return pltpu.make_async_remote_copy(
                src_ref=rows_of(px, py), dst_ref=tab_ref.at[slot], send_sem=tab_send.at[k], recv_sem=tab_recv.at[k],
                device_id=(px, py, c), device_id_type=pl.DeviceIdType.MESH)

        local = [pltpu.make_async_copy(srcs[wi], outs[wi].at[chip], local_sems.at[wi]) for wi in range(n_w)]
        for cp in local:
            cp.start()
        pairs = [(wi, k) for wi in range(n_w) for k in range(3)]
        for wi, k in pairs:
            over_ici(wi, k, False).start()
        for k, peer in enumerate(peers):
            cc_copy(k, peer, dev).start()
        all_ref[dev] = cc_ref[...]
        for k, (px, py, pc) in enumerate(peers):
            cc_copy(k, (px, py, pc), 4 * px + 2 * py + pc).wait_recv()
        cv = all_ref[...].reshape(8 * MOD_ROWS, cc.shape[1])
        part_ref[...] = _dot((cv * _sigmoid(cv)).astype(BF16), w_ref[...]) + b_ref[...]
        for k, (px, py) in enumerate(chips):
            tab_copy(k, px, py, chip).start()
        tab_ref[chip] = rows_of(x, y)[...]
        for k, (px, py) in enumerate(chips):
            tab_copy(k, px, py, 2 * px + py).wait_recv()
        for wi, k in pairs:
            over_ici(wi, k, True).wait_recv()
            to_sibling(wi, k, False).start()
        for wi, k in pairs:
            to_sibling(wi, k, True).wait_recv()
        for wi, k in pairs:
            over_ici(wi, k, False).wait_send()
            to_sibling(wi, k, False).wait_send()
        for k, peer in enumerate(peers):
            cc_copy(k, peer, dev).wait_send()
        for k, (px, py) in enumerate(chips):
            tab_copy(k, px, py, chip).wait_send()
        for cp in local:
            cp.wait()

    any_spec = pl.BlockSpec(memory_space=pl.ANY)
    vmem = pl.BlockSpec(memory_space=pltpu.VMEM)
    sems3 = pltpu.SemaphoreType.DMA((3 * n_w,))
    got = pl.pallas_call(
        body, name="first_exchange", in_specs=[any_spec] * n_w + [vmem] * 3, out_specs=[any_spec] * n_w + [vmem] * 2,
        out_shape=_exch_shapes("gather", shards) + [_sds((8,) + cc.shape, F32), _sds((N_CHIPS, MOD_ROWS, n), F32)],
        scratch_shapes=[pltpu.VMEM((8 * MOD_ROWS, n), F32), sems3, sems3, sems3, sems3, pltpu.SemaphoreType.DMA((n_w,)),
                        pltpu.SemaphoreType.DMA((7,)), pltpu.SemaphoreType.DMA((7,)), pltpu.SemaphoreType.DMA((3,)),
                        pltpu.SemaphoreType.DMA((3,))],
        compiler_params=pltpu.CompilerParams(vmem_limit_bytes=V7X_VMEM_LIMIT),
    )(*shards, cc, w, b)
    return got[:n_w], got[n_w], got[n_w + 1]


def _ada_bwd_tp(cc_all, dmods, w, ctx_row):
    d, n = w.shape

    def body(cc_ref, m0, m1, m2, m3, w_ref, dw_ref, db_ref, dctx_ref, stage_ref, all_ref, send_sems, recv_sems):
        x, y, c = lax.axis_index("x"), lax.axis_index("y"), lax.axis_index("c")
        me = 4 * x + 2 * y + c
        dsum = m0[...] + m1[...] + m2[...] + m3[...]
        db_ref[...] = jnp.sum(dsum, axis=0, keepdims=True)
        for j in range(N_CHIPS):
            stage_ref[j] = dsum[:, j * n:(j + 1) * n]

        def copy(k, peer, slot):
            px, py, _ = peer
            return pltpu.make_async_remote_copy(
                src_ref=stage_ref.at[2 * px + py], dst_ref=all_ref.at[slot], send_sem=send_sems.at[k],
                recv_sem=recv_sems.at[k], device_id=peer, device_id_type=pl.DeviceIdType.MESH)

        peers = _all_peers(x, y, c)
        for k, peer in enumerate(peers):
            copy(k, peer, me).start()
        all_ref[me] = stage_ref[2 * x + y]
        for k, (px, py, pc) in enumerate(peers):
            copy(k, (px, py, pc), 4 * px + 2 * py + pc).wait_recv()
        for k, peer in enumerate(peers):
            copy(k, peer, me).wait_send()
        cv = cc_ref[...]
        sig = _sigmoid(cv)
        dmb = all_ref[...].reshape(8 * MOD_ROWS, n).astype(BF16)
        dw_ref[...] = _dot_tn((cv * sig).astype(BF16), dmb)
        dsc = _dot_nt(dmb, w_ref[...])
        dctx = dsc[ctx_row:ctx_row + 1, :]
        for dev in range(1, 8):
            dctx = dctx + dsc[dev * MOD_ROWS + ctx_row:dev * MOD_ROWS + ctx_row + 1, :]
        cx = cv[ctx_row:ctx_row + 1, :]
        sx = sig[ctx_row:ctx_row + 1, :]
        dctx_ref[...] = dctx * (sx * (1.0 + cx * (1.0 - sx))) * jnp.where(c == 0, 1.0, 0.0)

    vmem = pl.BlockSpec(memory_space=pltpu.VMEM)
    return pl.pallas_call(
        body, name="ada_bwd_tp", in_specs=[vmem] * 6, out_specs=[vmem] * 3,
        out_shape=[_sds((d, n), F32), _sds((1, N_MOD * d), F32), _sds((1, d), F32)],
        scratch_shapes=[pltpu.VMEM((N_CHIPS, MOD_ROWS, n), F32), pltpu.VMEM((8, MOD_ROWS, n), F32),
                        pltpu.SemaphoreType.DMA((7,)), pltpu.SemaphoreType.DMA((7,))],
        compiler_params=pltpu.CompilerParams(vmem_limit_bytes=V7X_VMEM_LIMIT),
    )(cc_all, *dmods, w)


def _rope_tables(s, ctx):
    pos = np.arange(s, dtype=np.float32)
    inv = (np.float32(ROPE_BASE) ** (-np.arange(0, QK_ROPE // 2, 2, dtype=np.float32) / np.float32(QK_ROPE // 2)))
    ang_r = np.floor(pos / GRID_W)[:, None] * inv
    ang_c = (pos - GRID_W * np.floor(pos / GRID_W))[:, None] * inv
    ang = np.concatenate([ang_r, ang_r, ang_c, ang_c], axis=-1).astype(np.float32)
    cos, sin = np.cos(ang), np.sin(ang)
    half_b = (np.arange(QK_ROPE) // 8) % 2 == 1
    sin_a = np.where(half_b, sin, 0.0)
    sin_b = np.where(half_b, 0.0, -sin)

    def place(tab, fill):
        full = np.full((s + ctx, HEAD_PAD), fill, np.float32)
        full[:s, QK_NOPE:QK_HEAD] = tab
        return jnp.asarray(full)

    return place(cos, 1.0), place(sin_a, 0.0), place(sin_b, 0.0)


def _pad_last(a, n):
    return jnp.pad(a, [(0, 0)] * (a.ndim - 1) + [(0, n - a.shape[-1])])


def _flat_rows(parts, rows, width):
    flat = jnp.concatenate([p.reshape(-1) for p in parts])
    return jnp.pad(flat, (0, rows * width - flat.shape[0])).reshape(rows, width)


def kernel(x, c, ctx, c_ctx, w_ada, b_ada, norm1_w, ffn1_w1, ffn1_w3, ffn1_w2, norm2_w, w_in, q_a_norm_w, w_uq, kv_a_norm_w, w_ukv, q_norm_w, k_norm_w, v_norm_w, w_s, b_s, w_out, norm3_w, ffn2_w1, ffn2_w3, ffn2_w2, loss_target, m_c_ctx, m_w_ada, m_b_ada, m_norm1_w, m_ffn1_w1, m_ffn1_w3, m_ffn1_w2, m_norm2_w, m_w_in, m_q_a_norm_w, m_w_uq, m_kv_a_norm_w, m_w_ukv, m_q_norm_w, m_k_norm_w, m_v_norm_w, m_w_s, m_b_s, m_w_out, m_norm3_w, m_ffn2_w1, m_ffn2_w3, m_ffn2_w2, v_c_ctx, v_w_ada, v_b_ada, v_norm1_w, v_ffn1_w1, v_ffn1_w3, v_ffn1_w2, v_norm2_w, v_w_in, v_q_a_norm_w, v_w_uq, v_kv_a_norm_w, v_w_ukv, v_q_norm_w, v_k_norm_w, v_v_norm_w, v_w_s, v_b_s, v_w_out, v_norm3_w, v_ffn2_w1, v_ffn2_w3, v_ffn2_w2):
    wts = dict(c_ctx=c_ctx, w_ada=w_ada, b_ada=b_ada, norm1_w=norm1_w, ffn1_w1=ffn1_w1, ffn1_w3=ffn1_w3, ffn1_w2=ffn1_w2,
               norm2_w=norm2_w, w_in=w_in, q_a_norm_w=q_a_norm_w, w_uq=w_uq, kv_a_norm_w=kv_a_norm_w, w_ukv=w_ukv,
               q_norm_w=q_norm_w, k_norm_w=k_norm_w, v_norm_w=v_norm_w, w_s=w_s, b_s=b_s, w_out=w_out, norm3_w=norm3_w,
               ffn2_w1=ffn2_w1, ffn2_w3=ffn2_w3, ffn2_w2=ffn2_w2)
    moms = dict(c_ctx=m_c_ctx, w_ada=m_w_ada, b_ada=m_b_ada, norm1_w=m_norm1_w, ffn1_w1=m_ffn1_w1, ffn1_w3=m_ffn1_w3,
                ffn1_w2=m_ffn1_w2, norm2_w=m_norm2_w, w_in=m_w_in, q_a_norm_w=m_q_a_norm_w, w_uq=m_w_uq,
                kv_a_norm_w=m_kv_a_norm_w, w_ukv=m_w_ukv, q_norm_w=m_q_norm_w, k_norm_w=m_k_norm_w, v_norm_w=m_v_norm_w,
                w_s=m_w_s, b_s=m_b_s, w_out=m_w_out, norm3_w=m_norm3_w, ffn2_w1=m_ffn2_w1, ffn2_w3=m_ffn2_w3,
                ffn2_w2=m_ffn2_w2)
    vars_ = dict(c_ctx=v_c_ctx, w_ada=v_w_ada, b_ada=v_b_ada, norm1_w=v_norm1_w, ffn1_w1=v_ffn1_w1, ffn1_w3=v_ffn1_w3,
                 ffn1_w2=v_ffn1_w2, norm2_w=v_norm2_w, w_in=v_w_in, q_a_norm_w=v_q_a_norm_w, w_uq=v_w_uq,
                 kv_a_norm_w=v_kv_a_norm_w, w_ukv=v_w_ukv, q_norm_w=v_q_norm_w, k_norm_w=v_k_norm_w, v_norm_w=v_v_norm_w,
                 w_s=v_w_s, b_s=v_b_s, w_out=v_w_out, norm3_w=v_norm3_w, ffn2_w1=v_ffn2_w1, ffn2_w3=v_ffn2_w3,
                 ffn2_w2=v_ffn2_w2)

    nb, s, d = x.shape
    nctx = ctx.shape[1]
    t, tc = nb * s, nb * nctx
    t_all = t + tc
    sk = s + nctx
    assert nb + 1 <= MOD_ROWS and d % LANES == 0
    tm = _token_tile(s, nctx)

    def held(n, a_):
        return jnp.swapaxes(a_[0], 0, 1) if n in T_WEIGHTS else a_[0]

    def unheld(n, a_):
        return (jnp.swapaxes(a_, 0, 1) if n in T_WEIGHTS else a_)[None]

    shard = {n: held(n, wts[n]).astype(BF16) for n in SHARDED}
    full = {}

    def unshard(names, blocks):
        for n, g4 in zip(names, blocks):
            _, r_, c_ = g4.shape
            if n in ROW_SHARDED or n in T_WEIGHTS:
                full[n] = g4.reshape(N_CHIPS * r_, c_)
            else:
                full[n] = g4.transpose(1, 0, 2).reshape(r_, N_CHIPS * c_)

    def chip_major(n, g_):
        if n in ROW_SHARDED or n in T_WEIGHTS:
            return g_.reshape(N_CHIPS, g_.shape[0] // N_CHIPS, g_.shape[1]).astype(BF16)
        r_, cols = g_.shape
        return g_.reshape(r_, N_CHIPS, cols // N_CHIPS).transpose(1, 0, 2).astype(BF16)

    cc = jnp.concatenate([c, c_ctx[None, :], jnp.zeros((MOD_ROWS - nb - 1, d), F32)], axis=0)
    n_ada = shard["w_ada"].shape[1]
    assert n_ada % LANES == 0
    my_chip = 2 * lax.axis_index("x") + lax.axis_index("y")
    b_cols = lax.dynamic_slice_in_dim(b_ada, my_chip * n_ada, n_ada, axis=1)
    got, cc_all, table = _first_exchange([shard[n] for n in FIRST_WEIGHTS], cc, shard["w_ada"], b_cols)
    unshard(FIRST_WEIGHTS, got)
    cc_all = cc_all.reshape(8 * MOD_ROWS, d)
    mod = table.transpose(1, 0, 2).reshape(MOD_ROWS, N_MOD, d)
    wsb = w_s[0].astype(BF16)
    wcat = wsb.transpose(1, 0, 2).reshape(CHUNK, GROUPS * CHUNK)
    wcat_t = wsb.transpose(2, 0, 1).reshape(CHUNK, GROUPS * CHUNK)
    bias = jnp.repeat(b_s[0].T, GROUP_DIM, axis=1)
    vnw = v_norm_w.reshape(1, GMLP_W)
    lane = jnp.arange(GMLP_W)
    ones = (lane[:, None] // GROUP_DIM == lane[None, :] // GROUP_DIM).astype(BF16)
    qnw = _pad_last(q_norm_w, HEAD_PAD)
    knw = _pad_last(k_norm_w, HEAD_PAD)
    tabs = _rope_tables(s, nctx)

    x_lat, x_ctx = x.reshape(t, d), ctx.reshape(tc, d)
    (xs1, a1, b1, y1), got = _ffn_fwd(x_lat, x_ctx, mod, norm1_w, full["ffn1_w1"], full["ffn1_w3"], full["ffn1_w2"], 0, s,
                                      nb, tm, "ffn1_fwd", exch=("gather", [shard[n] for n in MIX_WEIGHTS]))
    unshard(MIX_WEIGHTS, got)
    wi = full["w_in"]
    wp = jnp.concatenate([wi[0:KV_LORA], jnp.zeros((QK_NOPE, d), BF16), wi[KV_LORA:KV_LORA + QK_ROPE],
                          jnp.zeros((HEAD_PAD - QK_HEAD, d), BF16), wi[KV_LORA + QK_ROPE:]], axis=0)
    wq = jnp.pad(full["w_uq"].reshape(HEADS, QK_HEAD, Q_LORA), ((0, 0), (0, HEAD_PAD - QK_HEAD), (0, 0)))
    wkv = full["w_ukv"].reshape(KV_LORA, HEADS, QK_NOPE + V_HEAD)
    wk = _pad_last(wkv[:, :, :QK_NOPE].transpose(1, 0, 2), HEAD_PAD)
    wv = wkv[:, :, QK_NOPE:].reshape(KV_LORA, HEADS // 2, 2 * V_HEAD).transpose(1, 0, 2)
    h2, proj = _mixin_fwd(xs1, mod, norm2_w, wp, s, nb, tm)
    prep_w = (wq, wk, wv, kv_a_norm_w, q_a_norm_w, qnw, knw)
    q, k_all, v_all = _prep_fwd(proj, 0, nb, s, 0, sk, 0, None, tabs, *prep_w, tm, True, "prep_fwd")
    k_all, v_all = _prep_fwd(proj, t // tm, nb, nctx, s // tm, sk, s // tm, (k_all, v_all), tabs, *prep_w, tm, False,
                             "prep_ctx_fwd")
    tq = _div_tile(s, 512, tm)
    o, lse, got = _attn_fwd(q, k_all, v_all, tq, exch=("gather", [shard[n] for n in LAST_WEIGHTS]))
    unshard(LAST_WEIGHTS, got)
    sg = _gmlp_fwd(proj, t, wcat, bias, vnw, ones, tm)
    x2, mix = _mixout_fwd(o, sg, xs1, mod, full["w_out"], s, tm)
    (dy, a2, b2, y2, loss_part), _ = _ffn_fwd(x2, None, mod, norm3_w, full["ffn2_w1"], full["ffn2_w3"], full["ffn2_w2"], 6,
                                              s, nb, tm, "ffn2_fwd", target=loss_target.reshape(t, d))
    loss = lax.psum(loss_part[0, 0], ("x", "y", "c"))

    grads, cm, recv = {}, {}, {}

    def scatter_of(names):
        return ("scatter", [cm[n] for n in names])

    (dx2, h3, g2, da2, db2, dyb2, dmod_c, grads["norm3_w"]), _ = _ffn_bwd(
        dy, x2, None, a2, b2, y2, mod, norm3_w, full["ffn2_w1"], full["ffn2_w3"], full["ffn2_w2"], 6, s, nb, tm,
        "ffn2_bwd")
    cm["ffn2_w1"] = chip_major("ffn2_w1", _mm_tn(da2, h3, t, "ffn2_dw1"))
    cm["ffn2_w3"] = chip_major("ffn2_w3", _mm_tn(db2, h3, t, "ffn2_dw3"))
    cm["ffn2_w2"] = chip_major("ffn2_w2", _mm_tn(g2, dyb2, t, "ffn2_dw2"))
    dmix, do, dsg, dmod_b = _mixout_bwd(dx2, mix, mod, full["w_out"], s, tm)
    cm["w_out"] = chip_major("w_out", jnp.concatenate([_mm_tn(o, dmix, t, "wout_dw_attn"),
                                                       _mm_tn(sg, dmix, t, "wout_dw_gmlp")], axis=0))
    dpu, dpv, dws, dbs, dvnw = _gmlp_bwd(proj, dsg, wcat, wcat_t, bias, vnw, ones, tm)
    group = LAST_WEIGHTS + ("w_out",)
    (dq, dk, dv), got = _attn_bwd(q, k_all, v_all, do, o, lse, tq, exch=scatter_of(group))
    recv.update(zip(group, got))
    dp0, dwk_c, dwv_c, dkvaw_c, dknw_c = _prep_bwd(
        proj, t // tm, nb, nctx, s // tm, s // tm, t_all, None, tabs, *prep_w, None, dk, dv, None, tm, "prep_ctx_bwd")
    dp0, dwq, dqaw, dqnw, dwk, dwv, dkvaw, dknw = _prep_bwd(
        proj, 0, nb, s, 0, 0, t_all, dp0, tabs, *prep_w, dq, dk, dv, [dwk_c, dwv_c, dkvaw_c, dknw_c], tm, "prep_bwd")
    dxs1, dmod_a, grads["norm2_w"] = _mixin_bwd(dp0, dpu, dpv, xs1, dx2, mod, norm2_w, wp, s, nb, tm)
    dwp = jnp.concatenate([_mm_tn(dp0, h2, t_all, "win_dw_kvq"), _mm_tn(dpu, h2, t, "win_dw_u"),
                           _mm_tn(dpv, h2, t, "win_dw_v")], axis=0)
    cm["w_in"] = chip_major("w_in", jnp.concatenate(
        [dwp[0:KV_LORA], dwp[KV_LORA + QK_NOPE:KV_LORA + QK_HEAD], dwp[256:]], axis=0))
    cm["w_uq"] = chip_major("w_uq", dwq[:, :, :QK_HEAD].transpose(0, 2, 1).reshape(HEADS * QK_HEAD, Q_LORA))
    cm["w_ukv"] = chip_major("w_ukv", jnp.concatenate(
        [dwk[:, :, :QK_NOPE].transpose(1, 0, 2),
         dwv.transpose(1, 0, 2).reshape(KV_LORA, HEADS, V_HEAD)], axis=2).reshape(KV_LORA, HEADS * (QK_NOPE + V_HEAD)))
    (dx_lat, h1, g1, da1, db1, dyb1, dmod_0, grads["norm1_w"]), _ = _ffn_bwd(
        dxs1, x_lat, x_ctx, a1, b1, y1, mod, norm1_w, full["ffn1_w1"], full["ffn1_w3"], full["ffn1_w2"], 0, s, nb, tm,
        "ffn1_bwd")
    dmods = [m_.reshape(MOD_ROWS, N_MOD * d) for m_ in (dmod_0, dmod_a, dmod_b, dmod_c)]
    dw_ada, grads["b_ada"], dctx = _ada_bwd_tp(cc_all, dmods, shard["w_ada"], nb)
    grads["c_ctx"] = dctx[0]
    grads["q_a_norm_w"], grads["kv_a_norm_w"] = dqaw, dkvaw
    grads["q_norm_w"], grads["k_norm_w"] = dqnw[:, :QK_HEAD], dknw[:, :QK_HEAD]
    grads["v_norm_w"], grads["w_s"], grads["b_s"] = dvnw, dws, dbs[:, 0]
    grad_x = dx_lat.reshape(nb, s, d)
    rows_s = _round_up(-(-sum(wts[n].size for n in SMALL) // d), 16)
    cm["small"] = jnp.broadcast_to(_flat_rows([grads[n] for n in SMALL], rows_s, d), (N_CHIPS, rows_s, d))
    group = ("w_in", "w_uq", "w_ukv")
    dw2, got = _mm_tn(g1, dyb1, t_all, "ffn1_dw2", exch=scatter_of(group))
    recv.update(zip(group, got))
    cm["ffn1_w2"] = chip_major("ffn1_w2", dw2)
    dw1, got = _mm_tn(da1, h1, t_all, "ffn1_dw1", exch=scatter_of(("ffn1_w2",)))
    recv["ffn1_w2"] = got[0]
    cm["ffn1_w1"] = chip_major("ffn1_w1", dw1)
    group = ("ffn1_w1", "small")
    dw3, got = _mm_tn(db1, h1, t_all, "ffn1_dw3", exch=scatter_of(group))
    recv.update(zip(group, got))
    cm["ffn1_w3"] = chip_major("ffn1_w3", dw3)
    stepped = {}
    stepped["w_ada"], got = _adamw([dw_ada], wts["w_ada"][0], moms["w_ada"][0], vars_["w_ada"][0], "adamw_w_ada",
                                   exch=scatter_of(("ffn1_w3",)))
    recv["ffn1_w3"] = got[0]

    reduced = tuple(n for n in SHARDED if n != "w_ada") + ("small",)
    part = {n: _sum_slots(recv[n], "sum_" + n) for n in reduced}
    early = LAST_WEIGHTS + ("w_out",)
    late = tuple(n for n in reduced if n not in early)
    sib = dict(zip(early, _swap_cores([part[n] for n in early], "swap_early")))
    sib.update(zip(late, _swap_cores([part[n] for n in late], "swap_late")))
    for n in reduced[:-1]:
        stepped[n], _ = _adamw([part[n], sib[n]], held(n, wts[n]), held(n, moms[n]), held(n, vars_[n]), "adamw_" + n)
    for n in SHARDED:
        stepped[n] = [unheld(n, a_) for a_ in stepped[n]]
    packed, _ = _adamw([part["small"], sib["small"]], _flat_rows([wts[n] for n in SMALL], rows_s, d),
                       _flat_rows([moms[n] for n in SMALL], rows_s, d), _flat_rows([vars_[n] for n in SMALL], rows_s, d),
                       "adamw_small")
    for n in SMALL:
        stepped[n] = []
    for a_ in packed:
        flat = a_.reshape(-1)
        off = 0
        for n in SMALL:
            stepped[n].append(flat[off:off + wts[n].size].reshape(wts[n].shape))
            off += wts[n].size
    return (loss, grad_x, *[stepped[n][0] for n in WEIGHTS], *[stepped[n][1] for n in WEIGHTS],
            *[stepped[n][2] for n in WEIGHTS], *[stepped[n][3] for n in WEIGHTS])
```

```python
import functools
import math

import jax
import jax.numpy as jnp
import numpy as np
from jax import lax
from jax.experimental import pallas as pl
from jax.experimental.pallas import tpu as pltpu

F32 = jnp.float32
BF16 = jnp.bfloat16

EPS = 1e-6
N_MOD = 9
HEADS = 8
QK_NOPE, QK_ROPE, V_HEAD = 64, 32, 64
QK_HEAD = QK_NOPE + QK_ROPE
HEAD_PAD = 128
LN2 = math.log(2.0)
SOFTMAX_SCALE = QK_HEAD ** -0.5 / LN2
Q_LORA, KV_LORA = 256, 128
GROUPS, GROUP_DIM, CHUNK = 8, 64, 128
GMLP_W = GROUPS * GROUP_DIM
MLA_W = HEADS * V_HEAD
IN_COLS = 1440
PROJ_COLS = 1536
GRID_W = 64
ROPE_BASE = 10000.0
MOD_ROWS = 16
ADAM_LR, ADAM_B1, ADAM_B2, ADAM_EPS, ADAM_WD, ADAM_STEP = 0.001, 0.9, 0.999, 1e-08, 0.01, 10
N_CHIPS = 4
LANES = 128
V7X_VMEM_LIMIT = 56 * 1024 * 1024
GELU_C = math.sqrt(2.0 / math.pi)

SHARDED = ("w_ada", "ffn1_w1", "ffn1_w3", "ffn1_w2", "w_in", "w_uq", "w_ukv", "w_out", "ffn2_w1", "ffn2_w3", "ffn2_w2")
ROW_SHARDED = ("ffn1_w2", "w_out", "ffn2_w2")
T_WEIGHTS = ("ffn1_w1", "ffn1_w3", "ffn2_w1", "ffn2_w3", "w_in", "w_uq")
FIRST_WEIGHTS = ("ffn1_w1", "ffn1_w3", "ffn1_w2")
MIX_WEIGHTS = ("w_in", "w_uq", "w_ukv", "w_out")
LAST_WEIGHTS = ("ffn2_w1", "ffn2_w3", "ffn2_w2")
SMALL = ("c_ctx", "b_ada", "norm1_w", "norm2_w", "q_a_norm_w", "kv_a_norm_w", "q_norm_w", "k_norm_w", "v_norm_w",
         "w_s", "b_s", "norm3_w")
WEIGHTS = ("c_ctx", "w_ada", "b_ada", "norm1_w", "ffn1_w1", "ffn1_w3", "ffn1_w2", "norm2_w", "w_in", "q_a_norm_w",
           "w_uq", "kv_a_norm_w", "w_ukv", "q_norm_w", "k_norm_w", "v_norm_w", "w_s", "b_s", "w_out", "norm3_w",
           "ffn2_w1", "ffn2_w3", "ffn2_w2")


def _round_up(n, m):
    return (n + m - 1) // m * m


def _div_tile(n, target, mult):
    best = None
    for t in range(mult, min(n, target) + 1, mult):
        if n % t == 0:
            best = t
    return n if best is None else best


def _dot(a, b):
    return lax.dot_general(a, b, (((1,), (0,)), ((), ())), preferred_element_type=F32)


def _dot_nt(a, b):
    return lax.dot_general(a, b, (((1,), (1,)), ((), ())), preferred_element_type=F32)


def _dot_tn(a, b):
    return lax.dot_general(a, b, (((0,), (0,)), ((), ())), preferred_element_type=F32)


def _sigmoid(x):
    return 1.0 / (1.0 + jnp.exp(-x))


def _gelu(x):
    return 0.5 * x * (1.0 + jnp.tanh(GELU_C * (x + 0.044715 * x * x * x)))


def _gelu_grad(x):
    t = jnp.tanh(GELU_C * (x + 0.044715 * x * x * x))
    return 0.5 * (1.0 + t) + 0.5 * x * (1.0 - t * t) * (GELU_C * (1.0 + 3 * 0.044715 * x * x))


def _rope(x, cos, sin_a, sin_b):
    return x * cos + pltpu.roll(x, 8, 1) * sin_a + pltpu.roll(x, HEAD_PAD - 8, 1) * sin_b


def _rope_t(d, cos, sin_a, sin_b):
    return d * cos + pltpu.roll(d * sin_a, HEAD_PAD - 8, 1) + pltpu.roll(d * sin_b, 8, 1)


def _rope3(x, cos, sin_a, sin_b):
    return x * cos + pltpu.roll(x, 8, 2) * sin_a + pltpu.roll(x, HEAD_PAD - 8, 2) * sin_b


def _rope3_t(d, cos, sin_a, sin_b):
    return d * cos + pltpu.roll(d * sin_a, HEAD_PAD - 8, 2) + pltpu.roll(d * sin_b, 8, 2)


def _group_sum(x, ones_ref):
    hi = x.astype(BF16)
    lo = (x - hi.astype(F32)).astype(BF16)
    return _dot(hi, ones_ref[...]) + _dot(lo, ones_ref[...])


def _params(n_axes):
    return pltpu.CompilerParams(dimension_semantics=("arbitrary",) * n_axes, vmem_limit_bytes=V7X_VMEM_LIMIT)


def _whole(shape):
    nd = len(shape)
    return pl.BlockSpec(shape, lambda *_: (0,) * nd, pipeline_mode=pl.Buffered(1))


def _sds(shape, dtype):
    return jax.ShapeDtypeStruct(shape, dtype)


def _token_tile(s, ctx):
    return _div_tile(math.gcd(s, ctx), 256, CHUNK)


def _other_chips(x, y):
    return [(1 - x, y), (x, 1 - y), (1 - x, 1 - y)]


def _exch_copies(kind, srcs, dsts, send_sems, recv_sems, local_sems):
    x, y, c = lax.axis_index("x"), lax.axis_index("y"), lax.axis_index("c")
    me = 2 * x + y
    local, sends, arrivals = [], [], []
    for w, (src, dst) in enumerate(zip(srcs, dsts)):
        own = src if kind == "gather" else src.at[me]
        local.append(pltpu.make_async_copy(own, dst.at[me], local_sems.at[w]))
        for k, (px, py) in enumerate(_other_chips(x, y)):
            sem = dict(send_sem=send_sems.at[3 * w + k], recv_sem=recv_sems.at[3 * w + k], device_id=(px, py, c),
                       device_id_type=pl.DeviceIdType.MESH)
            out = src if kind == "gather" else src.at[2 * px + py]
            sends.append(pltpu.make_async_remote_copy(src_ref=out, dst_ref=dst.at[me], **sem))
            arrivals.append(pltpu.make_async_remote_copy(src_ref=own, dst_ref=dst.at[2 * px + py], **sem))
    return local, sends, arrivals


def _exch_start(kind, srcs, dsts, sems):
    local, sends, _ = _exch_copies(kind, srcs, dsts, *sems)
    for cp in local + sends:
        cp.start()


def _exch_wait(kind, srcs, dsts, sems):
    local, sends, arrivals = _exch_copies(kind, srcs, dsts, *sems)
    for cp in arrivals:
        cp.wait_recv()
    for cp in sends:
        cp.wait_send()
    for cp in local:
        cp.wait()


def _exch_scratch(n):
    return [pltpu.SemaphoreType.DMA((3 * n,)), pltpu.SemaphoreType.DMA((3 * n,)), pltpu.SemaphoreType.DMA((n,))]


def _exch_shapes(kind, arrays):
    return [_sds((N_CHIPS,) + a.shape if kind == "gather" else a.shape, a.dtype) for a in arrays]


def _hosted_call(body, name, grid, in_specs, out_specs, out_shape, operands, scratch=(), exch=None):
    n_axes = len(grid)
    if exch is None:
        outs = pl.pallas_call(body, name=name, grid=grid, in_specs=list(in_specs), out_specs=list(out_specs),
                              out_shape=list(out_shape), scratch_shapes=list(scratch),
                              compiler_params=_params(n_axes))(*operands)
        return list(outs), []
    kind, arrays = exch
    n_in, n_out, n_sc, n_ex = len(in_specs), len(out_specs), len(scratch), len(arrays)

    def hosted(*refs):
        cin, ein = refs[:n_in], refs[n_in:n_in + n_ex]
        o0 = n_in + n_ex
        cout, eout = refs[o0:o0 + n_out], refs[o0 + n_out:o0 + n_out + n_ex]
        rest = refs[o0 + n_out + n_ex:]
        csc, sems = rest[:n_sc], rest[n_sc:]
        first = functools.reduce(jnp.logical_and, [pl.program_id(a) == 0 for a in range(n_axes)])
        last = functools.reduce(jnp.logical_and, [pl.program_id(a) == grid[a] - 1 for a in range(n_axes)])

        @pl.when(first)
        def _():
            _exch_start(kind, ein, eout, sems)

        body(*cin, *cout, *csc)

        @pl.when(last)
        def _():
            _exch_wait(kind, ein, eout, sems)

    any_spec = pl.BlockSpec(memory_space=pl.ANY)
    outs = pl.pallas_call(
        hosted, name=name, grid=grid, in_specs=list(in_specs) + [any_spec] * n_ex,
        out_specs=list(out_specs) + [any_spec] * n_ex, out_shape=list(out_shape) + _exch_shapes(kind, arrays),
        scratch_shapes=list(scratch) + _exch_scratch(n_ex), compiler_params=_params(n_axes),
    )(*operands, *arrays)
    return list(outs[:n_out]), list(outs[n_out:])


class _TokenTiles:
    def __init__(self, t, tc, tm):
        self.n_lat, self.n_ctx = t // tm, tc // tm
        self.n_all = self.n_lat + self.n_ctx

    def tile(self, i):
        return (i + self.n_lat) % self.n_all if self.n_ctx else i

    def is_lat(self, i):
        return self.tile(i) < self.n_lat

    def row(self, i):
        return (self.tile(i), 0)

    def lat_row(self, i):
        return (jnp.where(self.is_lat(i), self.tile(i), 0), 0) if self.n_ctx else (i, 0)

    def ctx_row(self, i):
        return (jnp.where(self.is_lat(i), self.n_ctx - 1, self.tile(i) - self.n_lat), 0)


def _ffn_fwd(x_lat, x_ctx, mod, nw, w1, w3, w2, k0, s, nb, tm, name, target=None, exch=None):
    t, d = x_lat.shape
    tc = 0 if x_ctx is None else x_ctx.shape[0]
    f = w1.shape[0]
    tiles = _TokenTiles(t, tc, tm)
    n_x = 2 if tc else 1
    n_t = 0 if target is None else 1
    assert not (tc and n_t)

    def body(*refs):
        x_ref = refs[0]
        t_ref = refs[n_x] if n_t else None
        mod_ref, nw_ref, w1_ref, w3_ref, w2_ref, o_ref, a_ref, b_ref, y_ref = refs[n_x + n_t:n_x + n_t + 9]
        i = pl.program_id(0)
        g = jnp.minimum((tiles.tile(i) * tm) // s, nb)
        shift = mod_ref[g, pl.ds(k0, 1), :]
        scale = mod_ref[g, pl.ds(k0 + 1, 1), :]
        gate = mod_ref[g, pl.ds(k0 + 2, 1), :]
        x = jnp.where(tiles.is_lat(i), x_ref[...], refs[1][...]) if tc else x_ref[...]
        r = lax.rsqrt(jnp.mean(x * x, axis=-1, keepdims=True) + EPS)
        hb = ((x * r * nw_ref[...]) * (1.0 + scale) + shift).astype(BF16)
        a = _dot_nt(hb, w1_ref[...])
        b = _dot_nt(hb, w3_ref[...])
        gb = (a * _sigmoid(a) * b).astype(BF16)
        y = _dot(gb, w2_ref[...])
        out = x + (0.5 * gate) * y
        a_ref[...] = a.astype(BF16)
        b_ref[...] = b.astype(BF16)
        y_ref[...] = y.astype(BF16)
        if n_t:
            loss_ref, acc_ref = refs[-2:]

            @pl.when(i == 0)
            def _():
                acc_ref[...] = jnp.zeros_like(acc_ref)

            e = out - t_ref[...]
            o_ref[...] = e * (1.0 / d)
            acc_ref[...] += jnp.sum(e * e, axis=0, keepdims=True)

            @pl.when(i == tiles.n_all - 1)
            def _():
                loss_ref[...] = (0.5 / d) * jnp.sum(acc_ref[...], axis=-1, keepdims=True)
        else:
            o_ref[...] = out

    td = pl.BlockSpec((tm, d), tiles.row)
    tf = pl.BlockSpec((tm, f), tiles.row)
    return _hosted_call(
        body, name, (tiles.n_all,),
        [pl.BlockSpec((tm, d), tiles.lat_row)] + ([pl.BlockSpec((tm, d), tiles.ctx_row)] if tc else []) + [td] * n_t
        + [_whole(mod.shape), _whole(nw.shape), _whole(w1.shape), _whole(w3.shape), _whole(w2.shape)],
        [td, tf, tf, td] + [pl.BlockSpec((1, 1), lambda i: (0, 0))] * n_t,
        [_sds((t + tc, d), F32), _sds((t + tc, f), BF16), _sds((t + tc, f), BF16), _sds((t + tc, d), BF16)]
        + [_sds((1, 1), F32)] * n_t,
        (x_lat,) + ((x_ctx,) if tc else ()) + ((target,) if n_t else ()) + (mod, nw, w1, w3, w2),
        scratch=[pltpu.VMEM((1, d), F32)] * n_t, exch=exch)


def _ffn_bwd(dout, x_lat, x_ctx, a, b, y, mod, nw, w1, w3, w2, k0, s, nb, tm, name, exch=None):
    t, d = x_lat.shape
    tc = 0 if x_ctx is None else x_ctx.shape[0]
    f = w1.shape[0]
    nch = 2 if (f // 2) % LANES == 0 and f % 2 == 0 else 1
    fc = f // nch
    tiles = _TokenTiles(t, tc, tm)
    n_x = 2 if tc else 1

    def body(*refs):
        do_ref, x_ref = refs[0], refs[1]
        (a_ref, b_ref, y_ref, mod_ref, nw_ref, w1_ref, w3_ref, w2_ref,
         dx_ref, h_ref, g_ref, da_ref, db_ref, dy_ref, dmod_ref, dnw_ref) = refs[1 + n_x:]
        i = pl.program_id(0)

        @pl.when(i == 0)
        def _():
            dmod_ref[...] = jnp.zeros_like(dmod_ref)
            dnw_ref[...] = jnp.zeros_like(dnw_ref)

        g = jnp.minimum((tiles.tile(i) * tm) // s, nb)
        shift = mod_ref[g, pl.ds(k0, 1), :]
        scale = mod_ref[g, pl.ds(k0 + 1, 1), :]
        gate = mod_ref[g, pl.ds(k0 + 2, 1), :]
        x = jnp.where(tiles.is_lat(i), x_ref[...], refs[2][...]) if tc else x_ref[...]
        dout_v = do_ref[...]
        r = lax.rsqrt(jnp.mean(x * x, axis=-1, keepdims=True) + EPS)
        xh = x * r
        n = xh * nw_ref[...]
        h_ref[...] = (n * (1.0 + scale) + shift).astype(BF16)
        dyb = ((0.5 * gate) * dout_v).astype(BF16)
        dy_ref[...] = dyb
        dmod_ref[g, pl.ds(k0 + 2, 1), :] += 0.5 * jnp.sum(dout_v * y_ref[...].astype(F32), axis=0, keepdims=True)
        dh = jnp.zeros((tm, d), F32)
        for c in range(nch):
            sl = slice(c * fc, (c + 1) * fc)
            dg = _dot_nt(dyb, w2_ref[sl, :])
            av = a_ref[:, sl].astype(F32)
            bv = b_ref[:, sl].astype(F32)
            sig = _sigmoid(av)
            sa = av * sig
            g_ref[:, sl] = (sa * bv).astype(BF16)
            dab = (dg * bv * (sig * (1.0 + av * (1.0 - sig)))).astype(BF16)
            dbb = (dg * sa).astype(BF16)
            da_ref[:, sl] = dab
            db_ref[:, sl] = dbb
            dh = dh + _dot(dab, w1_ref[sl, :]) + _dot(dbb, w3_ref[sl, :])
        dmod_ref[g, pl.ds(k0, 1), :] += jnp.sum(dh, axis=0, keepdims=True)
        dmod_ref[g, pl.ds(k0 + 1, 1), :] += jnp.sum(dh * n, axis=0, keepdims=True)
        dn = dh * (1.0 + scale)
        dnw_ref[...] += jnp.sum(dn * xh, axis=0, keepdims=True)
        dxh = dn * nw_ref[...]
        dx_ref[...] = dout_v + r * (dxh - xh * jnp.mean(dxh * xh, axis=-1, keepdims=True))

    td = pl.BlockSpec((tm, d), tiles.row)
    tf = pl.BlockSpec((tm, f), tiles.row)
    lat = pl.BlockSpec((tm, d), tiles.lat_row)
    ta = t + tc
    return _hosted_call(
        body, name, (tiles.n_all,),
        [td, lat] + ([pl.BlockSpec((tm, d), tiles.ctx_row)] if tc else [])
        + [tf, tf, td, _whole(mod.shape), _whole(nw.shape), _whole(w1.shape), _whole(w3.shape), _whole(w2.shape)],
        [lat, td, tf, tf, tf, td, pl.BlockSpec(mod.shape, lambda i: (0, 0, 0)), pl.BlockSpec((1, d), lambda i: (0, 0))],
        [_sds((t, d), F32), _sds((ta, d), BF16), _sds((ta, f), BF16), _sds((ta, f), BF16), _sds((ta, f), BF16),
         _sds((ta, d), BF16), _sds(mod.shape, F32), _sds((1, d), F32)],
        (dout, x_lat) + ((x_ctx,) if tc else ()) + (a, b, y, mod, nw, w1, w3, w2), exch=exch)


def _mm_tn(a, b, rows, name, exch=None):
    m = a.shape[1]
    n = b.shape[1]
    bm = _div_tile(m, 1408, LANES)
    bn = _div_tile(n, 1408, LANES)
    bk = _div_tile(rows, 2304, LANES)
    nk = rows // bk

    def body(a_ref, b_ref, o_ref, acc_ref):
        k = pl.program_id(2)

        @pl.when(k == 0)
        def _():
            acc_ref[...] = jnp.zeros_like(acc_ref)

        acc_ref[...] += _dot_tn(a_ref[...], b_ref[...])

        @pl.when(k == nk - 1)
        def _():
            o_ref[...] = acc_ref[...].astype(BF16)

    (out,), got = _hosted_call(
        body, name, (m // bm, n // bn, nk),
        [pl.BlockSpec((bk, bm), lambda i, j, k: (k, i)), pl.BlockSpec((bk, bn), lambda i, j, k: (k, j))],
        [pl.BlockSpec((bm, bn), lambda i, j, k: (i, j))], [_sds((m, n), BF16)], (a, b),
        scratch=[pltpu.VMEM((bm, bn), F32)], exch=exch)
    return out if exch is None else (out, got)


def _mixin_fwd(xs, mod, nw, wp, s, nb, tm):
    t, d = xs.shape

    def body(x_ref, mod_ref, nw_ref, wp_ref, h_ref, p_ref):
        g = jnp.minimum((pl.program_id(0) * tm) // s, nb)
        shift = mod_ref[g, pl.ds(3, 1), :]
        scale = mod_ref[g, pl.ds(4, 1), :]
        x = x_ref[...]
        r = lax.rsqrt(jnp.mean(x * x, axis=-1, keepdims=True) + EPS)
        hb = ((x * r * nw_ref[...]) * (1.0 + scale) + shift).astype(BF16)
        h_ref[...] = hb
        p_ref[...] = _dot_nt(hb, wp_ref[...]).astype(BF16)

    row = lambda i: (i, 0)
    return pl.pallas_call(
        body, name="mixin_fwd", grid=(t // tm,),
        in_specs=[pl.BlockSpec((tm, d), row), _whole(mod.shape), _whole(nw.shape), _whole(wp.shape)],
        out_specs=[pl.BlockSpec((tm, d), row), pl.BlockSpec((tm, PROJ_COLS), row)],
        out_shape=[_sds((t, d), BF16), _sds((t, PROJ_COLS), BF16)], compiler_params=_params(1),
    )(xs, mod, nw, wp)


def _mixin_bwd(dp0, dpu, dpv, xs, dres, mod, nw, wp, s, nb, tm):
    t_all, d = xs.shape
    nlat = dres.shape[0] // tm

    def body(p0_ref, pu_ref, pv_ref, x_ref, dr_ref, mod_ref, nw_ref, wp_ref, dx_ref, dmod_ref, dnw_ref):
        i = pl.program_id(0)

        @pl.when(i == 0)
        def _():
            dmod_ref[...] = jnp.zeros_like(dmod_ref)
            dnw_ref[...] = jnp.zeros_like(dnw_ref)

        lat = i < nlat
        g = jnp.minimum((i * tm) // s, nb)
        scale = mod_ref[g, pl.ds(4, 1), :]
        dh = _dot(p0_ref[...], wp_ref[0:512, :])
        extra = _dot(pu_ref[...], wp_ref[512:1024, :]) + _dot(pv_ref[...], wp_ref[1024:1536, :])
        dh = dh + jnp.where(lat, extra, 0.0)
        x = x_ref[...]
        r = lax.rsqrt(jnp.mean(x * x, axis=-1, keepdims=True) + EPS)
        xh = x * r
        n = xh * nw_ref[...]
        dmod_ref[g, pl.ds(3, 1), :] += jnp.sum(dh, axis=0, keepdims=True)
        dmod_ref[g, pl.ds(4, 1), :] += jnp.sum(dh * n, axis=0, keepdims=True)
        dn = dh * (1.0 + scale)
        dnw_ref[...] += jnp.sum(dn * xh, axis=0, keepdims=True)
        dxh = dn * nw_ref[...]
        dx_ref[...] = jnp.where(lat, dr_ref[...], 0.0) + r * (dxh - xh * jnp.mean(dxh * xh, axis=-1, keepdims=True))

    row = lambda i: (i, 0)
    lrow = lambda i: (jnp.minimum(i, nlat - 1), 0)
    return pl.pallas_call(
        body, name="mixin_bwd", grid=(t_all // tm,),
        in_specs=[pl.BlockSpec((tm, 512), row), pl.BlockSpec((tm, 512), lrow), pl.BlockSpec((tm, 512), lrow),
                  pl.BlockSpec((tm, d), row), pl.BlockSpec((tm, d), lrow), _whole(mod.shape), _whole(nw.shape),
                  _whole(wp.shape)],
        out_specs=[pl.BlockSpec((tm, d), row), pl.BlockSpec(mod.shape, lambda i: (0, 0, 0)),
                   pl.BlockSpec((1, d), lambda i: (0, 0))],
        out_shape=[_sds((t_all, d), F32), _sds(mod.shape, F32), _sds((1, d), F32)], compiler_params=_params(1),
    )(dp0, dpu, dpv, xs, dres, mod, nw, wp)


def _prep_fwd(proj, row0, nb, s, pos0, sk, key0, into, tabs, wq, wk, wv, kvaw, qaw, qnw, knw, tm, with_q, name):
    nblk = s // tm
    n_into = 0 if into is None else 2

    def body(p_ref, cos_ref, sa_ref, sb_ref, wq_ref, wk_ref, wv_ref, kvaw_ref, qaw_ref, qnw_ref, knw_ref, *rest):
        outs, heads_ref = rest[n_into:-1], rest[-1]
        q_ref, k_ref, v_ref = outs if with_q else (None,) + outs
        cos, sin_a, sin_b = cos_ref[...][None], sa_ref[...][None], sb_ref[...][None]

        def normed_roped(w_ref, src, extra, nw_ref, o_ref, post):
            for h in range(HEADS):
                heads_ref[h] = _dot_nt(src, w_ref[h]) if extra is None else _dot(src, w_ref[h])
            xp = heads_ref[...] if extra is None else heads_ref[...] + extra[None]
            r = lax.rsqrt(jnp.sum(xp * xp, axis=-1, keepdims=True) * (1.0 / QK_HEAD) + EPS)
            o_ref[...] = _rope3(xp * r * (nw_ref[...] * post)[None], cos, sin_a, sin_b).astype(BF16)

        ckv = p_ref[:, 0:128].astype(F32)
        rkv = lax.rsqrt(jnp.mean(ckv * ckv, axis=-1, keepdims=True) + EPS)
        ckvb = (ckv * rkv * kvaw_ref[...]).astype(BF16)
        normed_roped(wk_ref, ckvb, p_ref[:, 128:256].astype(F32), knw_ref, k_ref, 1.0)
        for j in range(HEADS // 2):
            v_ref[j] = _dot(ckvb, wv_ref[j]).astype(BF16)
        if with_q:
            cq = p_ref[:, 256:512].astype(F32)
            rq = lax.rsqrt(jnp.mean(cq * cq, axis=-1, keepdims=True) + EPS)
            normed_roped(wq_ref, (cq * rq * qaw_ref[...]).astype(BF16), None, qnw_ref, q_ref, SOFTMAX_SCALE)

    tab = pl.BlockSpec((tm, HEAD_PAD), lambda i: (pos0 + i % nblk, 0))
    qspec = pl.BlockSpec((None, HEADS, tm, HEAD_PAD), lambda i: (i // nblk, 0, i % nblk, 0))
    kspec = pl.BlockSpec((None, HEADS, tm, HEAD_PAD), lambda i: (i // nblk, 0, key0 + i % nblk, 0))
    vspec = pl.BlockSpec((None, HEADS // 2, tm, HEAD_PAD), lambda i: (i // nblk, 0, key0 + i % nblk, 0))
    qshape = _sds((nb, HEADS, s, HEAD_PAD), BF16)
    kshape = _sds((nb, HEADS, sk, HEAD_PAD), BF16)
    vshape = _sds((nb, HEADS // 2, sk, HEAD_PAD), BF16)
    n_q = 1 if with_q else 0
    return pl.pallas_call(
        body, name=name, grid=(nb * nblk,),
        in_specs=[pl.BlockSpec((tm, 512), lambda i: (row0 + i, 0)), tab, tab, tab, _whole(wq.shape), _whole(wk.shape),
                  _whole(wv.shape), _whole(kvaw.shape), _whole(qaw.shape), _whole(qnw.shape), _whole(knw.shape)]
        + [pl.BlockSpec(memory_space=pl.ANY)] * n_into,
        out_specs=([qspec] if with_q else []) + [kspec, vspec],
        out_shape=([qshape] if with_q else []) + [kshape, vshape],
        scratch_shapes=[pltpu.VMEM((HEADS, tm, HEAD_PAD), F32)],
        input_output_aliases={11: n_q, 12: n_q + 1} if n_into else {}, compiler_params=_params(1),
    )(proj, *tabs, wq, wk, wv, kvaw, qaw, qnw, knw, *(into or ()))


def _prep_bwd(proj, row0, nb, s, pos0, key0, dp_rows, dp_into, tabs, wq, wk, wv, kvaw, qaw, qnw, knw, dq, dk, dv, init, tm,
              name):
    nblk = s // tm
    with_q = dq is not None
    n_init = 0 if init is None else len(init)
    n_into = 0 if dp_into is None else 1

    def body(*refs):
        p_ref, cos_ref, sa_ref, sb_ref, wq_ref, wk_ref, wv_ref, kvaw_ref, qaw_ref, qnw_ref, knw_ref = refs[:11]
        rest = list(refs[11:])
        dq_ref = rest.pop(0) if with_q else None
        dk_ref, dv_ref = rest.pop(0), rest.pop(0)
        init_refs = [rest.pop(0) for _ in range(n_init)]
        if n_into:
            rest.pop(0)
        dp_ref = rest.pop(0)
        if with_q:
            dwq_ref, dqaw_ref, dqnw_ref = rest.pop(0), rest.pop(0), rest.pop(0)
        dwk_ref, dwv_ref, dkvaw_ref, dknw_ref, heads_ref, dhb_ref = rest
        accs = [dwk_ref, dwv_ref, dkvaw_ref, dknw_ref]

        @pl.when(pl.program_id(0) == 0)
        def _():
            for k, acc in enumerate(accs):
                acc[...] = init_refs[k][...] if n_init else jnp.zeros_like(acc)
            if with_q:
                dwq_ref[...] = jnp.zeros_like(dwq_ref)
                dqaw_ref[...] = jnp.zeros_like(dqaw_ref)
                dqnw_ref[...] = jnp.zeros_like(dqnw_ref)

        cos, sin_a, sin_b = cos_ref[...][None], sa_ref[...][None], sb_ref[...][None]
        lane = lax.broadcasted_iota(jnp.int32, (tm, HEAD_PAD), 1)
        rope_lanes = (lane >= QK_NOPE) & (lane < QK_HEAD)

        def heads_bwd(w_ref, src, extra, nw_ref, d_ref, dnw_ref, dw_ref, post):
            w_t = extra is None
            for h in range(HEADS):
                heads_ref[h] = _dot_nt(src, w_ref[h]) if w_t else _dot(src, w_ref[h])
            xp = heads_ref[...] if extra is None else heads_ref[...] + extra[None]
            r = lax.rsqrt(jnp.sum(xp * xp, axis=-1, keepdims=True) * (1.0 / QK_HEAD) + EPS)
            xh = xp * r
            dn = _rope3_t(d_ref[...], cos, sin_a, sin_b)
            dnw_ref[...] += post * jnp.sum(jnp.sum(dn * xh, axis=0), axis=0, keepdims=True)
            dxh = dn * (nw_ref[...] * post)[None]
            dxp = r * (dxh - xh * (jnp.sum(dxh * xh, axis=-1, keepdims=True) * (1.0 / QK_HEAD)))
            dhb_ref[...] = dxp.astype(BF16)
            dsrc = jnp.zeros((tm, src.shape[1]), F32)
            for h in range(HEADS):
                dsrc = dsrc + (_dot(dhb_ref[h], w_ref[h]) if w_t else _dot_nt(dhb_ref[h], w_ref[h]))
                dw_ref[h] += _dot_tn(src, dhb_ref[h])
            return dsrc, jnp.sum(dxp, axis=0)

        ckv = p_ref[:, 0:128].astype(F32)
        rkv = lax.rsqrt(jnp.mean(ckv * ckv, axis=-1, keepdims=True) + EPS)
        ckvh = ckv * rkv
        ckvb = (ckvh * kvaw_ref[...]).astype(BF16)
        dckv, dkp_sum = heads_bwd(wk_ref, ckvb, p_ref[:, 128:256].astype(F32), knw_ref, dk_ref, dknw_ref, dwk_ref,
                                  1.0)
        for j in range(HEADS // 2):
            dvb = dv_ref[j].astype(BF16)
            dckv = dckv + _dot_nt(dvb, wv_ref[j])
            dwv_ref[j] += _dot_tn(ckvb, dvb)
        dkvaw_ref[...] += jnp.sum(dckv * ckvh, axis=0, keepdims=True)
        dch = dckv * kvaw_ref[...]
        dp_ref[:, 0:128] = (rkv * (dch - ckvh * jnp.mean(dch * ckvh, axis=-1, keepdims=True))).astype(BF16)
        dp_ref[:, 128:256] = jnp.where(rope_lanes, dkp_sum, 0.0).astype(BF16)
        if with_q:
            cq = p_ref[:, 256:512].astype(F32)
            rq = lax.rsqrt(jnp.mean(cq * cq, axis=-1, keepdims=True) + EPS)
            cqh = cq * rq
            cqb = (cqh * qaw_ref[...]).astype(BF16)
            dcq, _ = heads_bwd(wq_ref, cqb, None, qnw_ref, dq_ref, dqnw_ref, dwq_ref, SOFTMAX_SCALE)
            dqaw_ref[...] += jnp.sum(dcq * cqh, axis=0, keepdims=True)
            dqc = dcq * qaw_ref[...]
            dp_ref[:, 256:512] = (rq * (dqc - cqh * jnp.mean(dqc * cqh, axis=-1, keepdims=True))).astype(BF16)
        else:
            dp_ref[:, 256:512] = jnp.zeros((tm, Q_LORA), BF16)

    tab = pl.BlockSpec((tm, HEAD_PAD), lambda i: (pos0 + i % nblk, 0))
    qspec = pl.BlockSpec((None, HEADS, tm, HEAD_PAD), lambda i: (i // nblk, 0, i % nblk, 0))
    kspec = pl.BlockSpec((None, HEADS, tm, HEAD_PAD), lambda i: (i // nblk, 0, key0 + i % nblk, 0))
    vspec = pl.BlockSpec((None, HEADS // 2, tm, HEAD_PAD), lambda i: (i // nblk, 0, key0 + i % nblk, 0))

    def acc_spec(shape):
        nd = len(shape)
        return pl.BlockSpec(shape, lambda i: (0,) * nd)

    acc_shapes = [(HEADS, KV_LORA, HEAD_PAD), (HEADS // 2, KV_LORA, HEAD_PAD), (1, KV_LORA), (1, HEAD_PAD)]
    q_shapes = [(HEADS, Q_LORA, HEAD_PAD), (1, Q_LORA), (1, HEAD_PAD)] if with_q else []
    out_shapes = [(dp_rows, 512)] + q_shapes + acc_shapes
    n_before = 11 + (1 if with_q else 0) + 2 + n_init
    return pl.pallas_call(
        body, name=name, grid=(nb * nblk,),
        in_specs=[pl.BlockSpec((tm, 512), lambda i: (row0 + i, 0)), tab, tab, tab, _whole(wq.shape), _whole(wk.shape),
                  _whole(wv.shape), _whole(kvaw.shape), _whole(qaw.shape), _whole(qnw.shape), _whole(knw.shape)]
        + ([qspec] if with_q else []) + [kspec, vspec] + [_whole(a.shape) for a in (init or [])]
        + [pl.BlockSpec(memory_space=pl.ANY)] * n_into,
        out_specs=[pl.BlockSpec((tm, 512), lambda i: (row0 + i, 0))] + [acc_spec(sh) for sh in q_shapes + acc_shapes],
        out_shape=[_sds(out_shapes[0], BF16)] + [_sds(sh, F32) for sh in out_shapes[1:]],
        scratch_shapes=[pltpu.VMEM((HEADS, tm, HEAD_PAD), F32), pltpu.VMEM((HEADS, tm, HEAD_PAD), BF16)],
        input_output_aliases={n_before: 0} if n_into else {}, compiler_params=_params(1),
    )(proj, *tabs, wq, wk, wv, kvaw, qaw, qnw, knw, *([dq] if with_q else []), dk, dv, *(init or []),
      *([dp_into] if n_into else []))


def _attn_fwd(q, k, v, tq, exch=None):
    nb, _, s, _ = q.shape
    sk = k.shape[2]
    nq = s // tq

    def body(q_ref, k_ref, v_ref, o_ref, lse_ref, vext_ref):
        @pl.when(pl.program_id(2) == 0)
        def _():
            vext_ref[:, 0:HEAD_PAD] = v_ref[...]
            vext_ref[:, HEAD_PAD:2 * HEAD_PAD] = jnp.ones((sk, HEAD_PAD), BF16)

        lane = lax.broadcasted_iota(jnp.int32, (tq, HEAD_PAD), 1)
        outs = []
        for hh in range(2):
            sc = _dot_nt(q_ref[hh], k_ref[hh])
            m = jnp.max(sc, axis=-1, keepdims=True)
            pv = _dot(jnp.exp2(sc - m).astype(BF16), vext_ref[...])
            l = pv[:, HEAD_PAD:HEAD_PAD + 1]
            outs.append(pv[:, 0:HEAD_PAD] / l)
            lse_ref[hh] = m + jnp.log2(l)
        o_ref[...] = jnp.where(lane < V_HEAD, outs[0], outs[1]).astype(BF16)

    (o, lse), got = _hosted_call(
        body, "attn_fwd", (nb, HEADS // 2, nq),
        [pl.BlockSpec((None, 2, tq, HEAD_PAD), lambda b, j, i: (b, j, i, 0)),
         pl.BlockSpec((None, 2, sk, HEAD_PAD), lambda b, j, i: (b, j, 0, 0)),
         pl.BlockSpec((None, None, sk, HEAD_PAD), lambda b, j, i: (b, j, 0, 0))],
        [pl.BlockSpec((tq, HEAD_PAD), lambda b, j, i: (b * nq + i, j)),
         pl.BlockSpec((None, 2, tq, 1), lambda b, j, i: (b, j, i, 0))],
        [_sds((nb * s, MLA_W), BF16), _sds((nb, HEADS, s, 1), F32)], (q, k, v),
        scratch=[pltpu.VMEM((sk, 2 * HEAD_PAD), BF16)], exch=exch)
    return o, lse, got


def _attn_bwd(q, k, v, do, o, lse, tq, exch=None):
    nb, _, s, _ = q.shape
    sk = k.shape[2]
    nq = s // tq
    kc = _div_tile(sk, 2304, LANES)

    def body(q_ref, k_ref, v_ref, do_ref, o_ref, lse_ref, dq_ref, dk_ref, dv_ref):
        @pl.when(pl.program_id(2) == 0)
        def _():
            dk_ref[...] = jnp.zeros_like(dk_ref)
            dv_ref[...] = jnp.zeros_like(dv_ref)

        lane = lax.broadcasted_iota(jnp.int32, (tq, HEAD_PAD), 1)
        dov = do_ref[...]
        prod = dov.astype(F32) * o_ref[...].astype(F32)
        for hh in range(2):
            mine = (lane < V_HEAD) if hh == 0 else (lane >= V_HEAD)
            doh = jnp.where(mine, dov, jnp.zeros_like(dov))
            delta = jnp.sum(jnp.where(mine, prod, 0.0), axis=-1, keepdims=True)
            qh = q_ref[hh]
            q_ln2 = (qh.astype(F32) * LN2).astype(BF16)
            lse_h = lse_ref[hh]
            dq = jnp.zeros((tq, HEAD_PAD), F32)
            for c in range(sk // kc):
                rows = slice(c * kc, (c + 1) * kc)
                kv = k_ref[hh, rows, :]
                p = jnp.exp2(_dot_nt(qh, kv) - lse_h)
                dp = _dot_nt(doh, v_ref[rows, :])
                u = (p * (dp - delta)).astype(BF16)
                dq = dq + _dot(u, kv)
                dk_ref[hh, rows, :] += _dot_tn(u, q_ln2)
                dv_ref[rows, :] += _dot_tn(p.astype(BF16), doh)
            dq_ref[hh] = dq * LN2

    qspec = pl.BlockSpec((None, 2, tq, HEAD_PAD), lambda b, j, i: (b, j, i, 0))
    kspec = pl.BlockSpec((None, 2, sk, HEAD_PAD), lambda b, j, i: (b, j, 0, 0))
    vspec = pl.BlockSpec((None, None, sk, HEAD_PAD), lambda b, j, i: (b, j, 0, 0))
    ospec = pl.BlockSpec((tq, HEAD_PAD), lambda b, j, i: (b * nq + i, j))
    return _hosted_call(
        body, "attn_bwd", (nb, HEADS // 2, nq),
        [qspec, kspec, vspec, ospec, ospec, pl.BlockSpec((None, 2, tq, 1), lambda b, j, i: (b, j, i, 0))],
        [qspec, kspec, vspec], [_sds(q.shape, F32), _sds(k.shape, F32), _sds(v.shape, F32)], (q, k, v, do, o, lse),
        exch=exch)


def _group_masks(rows):
    lane = lax.broadcasted_iota(jnp.int32, (rows, GMLP_W), 1)
    return [(lane >= g * GROUP_DIM) & (lane < (g + 1) * GROUP_DIM) for g in range(GROUPS)]


def _gmlp_fwd(proj, t, wcat, bias, vnw, ones, tm):
    def body(u_ref, v_ref, wcat_ref, bias_ref, vnw_ref, ones_ref, o_ref):
        masks = _group_masks(CHUNK)
        gv = _gelu(v_ref[...].astype(F32))
        rv = lax.rsqrt(_group_sum(gv * gv, ones_ref) * (1.0 / GROUP_DIM) + EPS)
        vnb = (gv * rv * vnw_ref[...]).astype(BF16)
        for c in range(tm // CHUNK):
            rows = slice(c * CHUNK, (c + 1) * CHUNK)
            vc = vnb[rows]
            stack = jnp.concatenate([jnp.where(m, vc, jnp.zeros_like(vc)) for m in masks], axis=0)
            sp = _dot(wcat_ref[...], stack) + bias_ref[...]
            o_ref[rows, :] = (_gelu(u_ref[rows, :].astype(F32)) * sp).astype(BF16)

    return pl.pallas_call(
        body, name="gmlp_fwd", grid=(t // tm,),
        in_specs=[pl.BlockSpec((tm, GMLP_W), lambda i: (i, 1)), pl.BlockSpec((tm, GMLP_W), lambda i: (i, 2)),
                  _whole(wcat.shape), _whole(bias.shape), _whole(vnw.shape), _whole(ones.shape)],
        out_specs=pl.BlockSpec((tm, GMLP_W), lambda i: (i, 0)),
        out_shape=_sds((t, GMLP_W), BF16), compiler_params=_params(1),
    )(proj, proj, wcat, bias, vnw, ones)


def _gmlp_bwd(proj, dsg, wcat, wcat_t, bias, vnw, ones, tm):
    t = dsg.shape[0]

    def body(u_ref, v_ref, dsg_ref, wcat_ref, wcatt_ref, bias_ref, vnw_ref, ones_ref,
             du_ref, dv_ref, dws_ref, dbs_ref, dvnw_ref):
        @pl.when(pl.program_id(0) == 0)
        def _():
            dws_ref[...] = jnp.zeros_like(dws_ref)
            dbs_ref[...] = jnp.zeros_like(dbs_ref)
            dvnw_ref[...] = jnp.zeros_like(dvnw_ref)

        masks = _group_masks(CHUNK)
        v = v_ref[...].astype(F32)
        gv = _gelu(v)
        rv = lax.rsqrt(_group_sum(gv * gv, ones_ref) * (1.0 / GROUP_DIM) + EPS)
        xh = gv * rv
        vnb = (xh * vnw_ref[...]).astype(BF16)
        dvn_parts = []
        for c in range(tm // CHUNK):
            rows = slice(c * CHUNK, (c + 1) * CHUNK)
            vc = vnb[rows]
            stack = jnp.concatenate([jnp.where(m, vc, jnp.zeros_like(vc)) for m in masks], axis=0)
            sp = _dot(wcat_ref[...], stack) + bias_ref[...]
            u = u_ref[rows, :].astype(F32)
            dsg_c = dsg_ref[rows, :]
            du_ref[rows, :] = (dsg_c * sp * _gelu_grad(u)).astype(BF16)
            ds = dsg_c * _gelu(u)
            dstack = jnp.concatenate([jnp.where(m, ds, 0.0) for m in masks], axis=0)
            dbs_ref[...] += jnp.broadcast_to(jnp.sum(dstack, axis=-1, keepdims=True), dbs_ref.shape)
            dstb = dstack.astype(BF16)
            dvn_parts.append(_dot(wcatt_ref[...], dstb))
            dws_ref[...] += _dot_nt(dstb, vc)
        dvn = jnp.concatenate(dvn_parts, axis=0) if len(dvn_parts) > 1 else dvn_parts[0]
        dvnw_ref[...] += jnp.sum(dvn * xh, axis=0, keepdims=True)
        dxh = dvn * vnw_ref[...]
        gm = _group_sum(dxh * xh, ones_ref) * (1.0 / GROUP_DIM)
        dv_ref[...] = (rv * (dxh - xh * gm) * _gelu_grad(v)).astype(BF16)

    row = pl.BlockSpec((tm, GMLP_W), lambda i: (i, 0))
    return pl.pallas_call(
        body, name="gmlp_bwd", grid=(t // tm,),
        in_specs=[pl.BlockSpec((tm, GMLP_W), lambda i: (i, 1)), pl.BlockSpec((tm, GMLP_W), lambda i: (i, 2)), row,
                  _whole(wcat.shape), _whole(wcat_t.shape), _whole(bias.shape), _whole(vnw.shape), _whole(ones.shape)],
        out_specs=[row, row, pl.BlockSpec((GROUPS * CHUNK, CHUNK), lambda i: (0, 0)),
                   pl.BlockSpec((GROUPS * CHUNK, CHUNK), lambda i: (0, 0)), pl.BlockSpec((1, GMLP_W), lambda i: (0, 0))],
        out_shape=[_sds((t, GMLP_W), BF16), _sds((t, GMLP_W), BF16), _sds((GROUPS * CHUNK, CHUNK), F32),
                   _sds((GROUPS * CHUNK, CHUNK), F32), _sds((1, GMLP_W), F32)],
        compiler_params=_params(1),
    )(proj, proj, dsg, wcat, wcat_t, bias, vnw, ones)


def _mixout_fwd(o, sg, xs, mod, wout, s, tm):
    t = o.shape[0]
    d = xs.shape[1]

    def body(o_ref, sg_ref, x_ref, mod_ref, w_ref, x2_ref, mix_ref):
        g = (pl.program_id(0) * tm) // s
        gate = mod_ref[g, pl.ds(5, 1), :]
        mix = _dot(o_ref[...], w_ref[0:MLA_W, :]) + _dot(sg_ref[...], w_ref[MLA_W:MLA_W + GMLP_W, :])
        x2_ref[...] = x_ref[...] + gate * mix
        mix_ref[...] = mix.astype(BF16)

    row = lambda i: (i, 0)
    return pl.pallas_call(
        body, name="mixout_fwd", grid=(t // tm,),
        in_specs=[pl.BlockSpec((tm, MLA_W), row), pl.BlockSpec((tm, GMLP_W), row), pl.BlockSpec((tm, d), row),
                  _whole(mod.shape), _whole(wout.shape)],
        out_specs=[pl.BlockSpec((tm, d), row), pl.BlockSpec((tm, d), row)],
        out_shape=[_sds((t, d), F32), _sds((t, d), BF16)], compiler_params=_params(1),
    )(o, sg, xs, mod, wout)


def _mixout_bwd(dx2, mix, mod, wout, s, tm):
    t, d = dx2.shape

    def body(dx_ref, mix_ref, mod_ref, w_ref, dmix_ref, do_ref, dsg_ref, dmod_ref):
        i = pl.program_id(0)

        @pl.when(i == 0)
        def _():
            dmod_ref[...] = jnp.zeros_like(dmod_ref)

        g = (i * tm) // s
        gate = mod_ref[g, pl.ds(5, 1), :]
        dx = dx_ref[...]
        dmod_ref[g, pl.ds(5, 1), :] += jnp.sum(dx * mix_ref[...].astype(F32), axis=0, keepdims=True)
        dmb = (gate * dx).astype(BF16)
        dmix_ref[...] = dmb
        do_ref[...] = _dot_nt(dmb, w_ref[0:MLA_W, :]).astype(BF16)
        dsg_ref[...] = _dot_nt(dmb, w_ref[MLA_W:MLA_W + GMLP_W, :])

    row = lambda i: (i, 0)
    return pl.pallas_call(
        body, name="mixout_bwd", grid=(t // tm,),
        in_specs=[pl.BlockSpec((tm, d), row), pl.BlockSpec((tm, d), row), _whole(mod.shape), _whole(wout.shape)],
        out_specs=[pl.BlockSpec((tm, d), row), pl.BlockSpec((tm, MLA_W), row), pl.BlockSpec((tm, GMLP_W), row),
                   pl.BlockSpec(mod.shape, lambda i: (0, 0, 0))],
        out_shape=[_sds((t, d), BF16), _sds((t, MLA_W), BF16), _sds((t, GMLP_W), F32), _sds(mod.shape, F32)],
        compiler_params=_params(1),
    )(dx2, mix, mod, wout)


def _swap_cores(parts, name):
    n = len(parts)

    def body(*refs):
        srcs, outs, send_sems, recv_sems = refs[:n], refs[n:2 * n], refs[2 * n], refs[2 * n + 1]
        x, y, c = lax.axis_index("x"), lax.axis_index("y"), lax.axis_index("c")
        copies = [pltpu.make_async_remote_copy(
            src_ref=srcs[w], dst_ref=outs[w], send_sem=send_sems.at[w], recv_sem=recv_sems.at[w],
            device_id=(x, y, 1 - c), device_id_type=pl.DeviceIdType.MESH) for w in range(n)]
        for cp in copies:
            cp.start()
        for cp in copies:
            cp.wait()

    any_spec = pl.BlockSpec(memory_space=pl.ANY)
    return pl.pallas_call(
        body, name=name, in_specs=[any_spec] * n, out_specs=[any_spec] * n,
        out_shape=[_sds(p.shape, p.dtype) for p in parts],
        scratch_shapes=[pltpu.SemaphoreType.DMA((n,)), pltpu.SemaphoreType.DMA((n,))],
    )(*parts)


def _row_tile(r, c, mult):
    return _div_tile(r, max(mult, (1 << 18) // c), mult)


def _sum_slots(recv, name):
    _, r, c = recv.shape
    tr = _row_tile(r, c, 16)

    def body(r_ref, o_ref):
        f = lambda k: r_ref[k].astype(F32)
        o_ref[...] = ((f(0) + f(1)) + f(2)) + f(3)

    return pl.pallas_call(
        body, name=name, grid=(r // tr,),
        in_specs=[pl.BlockSpec((N_CHIPS, tr, c), lambda i: (0, i, 0))],
        out_specs=pl.BlockSpec((tr, c), lambda i: (i, 0)),
        out_shape=_sds((r, c), F32), compiler_params=_params(1),
    )(recv)


def _adamw(parts, w, m, v, name, exch=None):
    r, wd = w.shape
    tr = _row_tile(r, wd, 8)
    c1 = 1.0 / (1.0 - ADAM_B1 ** ADAM_STEP)
    c2 = 1.0 / (1.0 - ADAM_B2 ** ADAM_STEP)
    n_p = len(parts)

    def body(*refs):
        p_refs = refs[:n_p]
        w_ref, m_ref, v_ref, g_ref, d_ref, nm_ref, nv_ref = refs[n_p:]
        g = p_refs[0][...]
        for p_ref in p_refs[1:]:
            g = g + p_ref[...]
        nm = ADAM_B1 * m_ref[...] + (1.0 - ADAM_B1) * g
        nv = ADAM_B2 * v_ref[...] + (1.0 - ADAM_B2) * (g * g)
        g_ref[...] = g
        nm_ref[...] = nm
        nv_ref[...] = nv
        d_ref[...] = -ADAM_LR * ((nm * c1) / (jnp.sqrt(nv * c2) + ADAM_EPS) + ADAM_WD * w_ref[...])

    spec = pl.BlockSpec((tr, wd), lambda i: (i, 0))
    return _hosted_call(body, name, (r // tr,), [spec] * (n_p + 3), [spec] * 4, [_sds((r, wd), F32)] * 4,
                        (*parts, w, m, v), exch=exch)


def _all_peers(x, y, c):
    flips = [(dx, dy, dc) for dx in (0, 1) for dy in (0, 1) for dc in (0, 1)][1:]
    return [(1 - x if dx else x, 1 - y if dy else y, 1 - c if dc else c) for dx, dy, dc in flips]


def _first_exchange(shards, cc, w, b):
    n_w = len(shards)
    n = w.shape[1]

    def body(*refs):
        srcs, (cc_ref, w_ref, b_ref) = refs[:n_w], refs[n_w:n_w + 3]
        outs, (all_ref, tab_ref) = refs[n_w + 3:2 * n_w + 3], refs[2 * n_w + 3:2 * n_w + 5]
        (part_ref, ici_send, ici_recv, d2d_send, d2d_recv, local_sems, cc_send, cc_recv, tab_send,
         tab_recv) = refs[2 * n_w + 5:]
        x, y, c = lax.axis_index("x"), lax.axis_index("y"), lax.axis_index("c")
        chip, dev = 2 * x + y, 4 * x + 2 * y + c
        chips = _other_chips(x, y)
        peers = _all_peers(x, y, c)

        def half(wi, which):
            hr = shards[wi].shape[0] // 2
            return pl.ds(pl.multiple_of(which * hr, 16), hr)

        def over_ici(wi, k, arriving):
            px, py = chips[k]
            slot = 2 * px + py if arriving else chip
            return pltpu.make_async_remote_copy(
                src_ref=srcs[wi].at[half(wi, c)], dst_ref=outs[wi].at[slot, half(wi, c)],
                send_sem=ici_send.at[3 * wi + k], recv_sem=ici_recv.at[3 * wi + k], device_id=(px, py, c),
                device_id_type=pl.DeviceIdType.MESH)

        def to_sibling(wi, k, arriving):
            px, py = chips[k]
            rows = half(wi, 1 - c if arriving else c)
            return pltpu.make_async_remote_copy(
                src_ref=outs[wi].at[2 * px + py, rows], dst_ref=outs[wi].at[2 * px + py, rows],
                send_sem=d2d_send.at[3 * wi + k], recv_sem=d2d_recv.at[3 * wi + k], device_id=(x, y, 1 - c),
                device_id_type=pl.DeviceIdType.MESH)

        def cc_copy(k, peer, slot):
            return pltpu.make_async_remote_copy(
                src_ref=cc_ref, dst_ref=all_ref.at[slot], send_sem=cc_send.at[k], recv_sem=cc_recv.at[k],
                device_id=peer, device_id_type=pl.DeviceIdType.MESH)

        def rows_of(px, py):
            return part_ref.at[pl.ds(pl.multiple_of((4 * px + 2 * py + c) * MOD_ROWS, MOD_ROWS), MOD_ROWS)]

        def tab_copy(k, px, py, slot):
            return pltpu.make_async_remote_copy(
                src_ref=rows_of(px, py), dst_ref=tab_ref.at[slot], send_sem=tab_send.at[k], recv_sem=tab_recv.at[k],
                device_id=(px, py, c), device_id_type=pl.DeviceIdType.MESH)

        local = [pltpu.make_async_copy(srcs[wi], outs[wi].at[chip], local_sems.at[wi]) for wi in range(n_w)]
        for cp in local:
            cp.start()
        pairs = [(wi, k) for wi in range(n_w) for k in range(3)]
        for wi, k in pairs:
            over_ici(wi, k, False).start()
        for k, peer in enumerate(peers):
            cc_copy(k, peer, dev).start()
        all_ref[dev] = cc_ref[...]
        for k, (px, py, pc) in enumerate(peers):
            cc_copy(k, (px, py, pc), 4 * px + 2 * py + pc).wait_recv()
        cv = all_ref[...].reshape(8 * MOD_ROWS, cc.shape[1])
        part_ref[...] = _dot((cv * _sigmoid(cv)).astype(BF16), w_ref[...]) + b_ref[...]
        for k, (px, py) in enumerate(chips):
            tab_copy(k, px, py, chip).start()
        tab_ref[chip] = rows_of(x, y)[...]
        for k, (px, py) in enumerate(chips):
            tab_copy(k, px, py, 2 * px + py).wait_recv()
        for wi, k in pairs:
            over_ici(wi, k, True).wait_recv()
            to_sibling(wi, k, False).start()
        for wi, k in pairs:
            to_sibling(wi, k, True).wait_recv()
        for wi, k in pairs:
            over_ici(wi, k, False).wait_send()
            to_sibling(wi, k, False).wait_send()
        for k, peer in enumerate(peers):
            cc_copy(k, peer, dev).wait_send()
        for k, (px, py) in enumerate(chips):
            tab_copy(k, px, py, chip).wait_send()
        for cp in local:
            cp.wait()

    any_spec = pl.BlockSpec(memory_space=pl.ANY)
    vmem = pl.BlockSpec(memory_space=pltpu.VMEM)
    sems3 = pltpu.SemaphoreType.DMA((3 * n_w,))
    got = pl.pallas_call(
        body, name="first_exchange", in_specs=[any_spec] * n_w + [vmem] * 3, out_specs=[any_spec] * n_w + [vmem] * 2,
        out_shape=_exch_shapes("gather", shards) + [_sds((8,) + cc.shape, F32), _sds((N_CHIPS, MOD_ROWS, n), F32)],
        scratch_shapes=[pltpu.VMEM((8 * MOD_ROWS, n), F32), sems3, sems3, sems3, sems3, pltpu.SemaphoreType.DMA((n_w,)),
                        pltpu.SemaphoreType.DMA((7,)), pltpu.SemaphoreType.DMA((7,)), pltpu.SemaphoreType.DMA((3,)),
                        pltpu.SemaphoreType.DMA((3,))],
        compiler_params=pltpu.CompilerParams(vmem_limit_bytes=V7X_VMEM_LIMIT),
    )(*shards, cc, w, b)
    return got[:n_w], got[n_w], got[n_w + 1]


def _ada_bwd_tp(cc_all, dmods, w, ctx_row):
    d, n = w.shape

    def body(cc_ref, m0, m1, m2, m3, w_ref, dw_ref, db_ref, dctx_ref, stage_ref, all_ref, send_sems, recv_sems):
        x, y, c = lax.axis_index("x"), lax.axis_index("y"), lax.axis_index("c")
        me = 4 * x + 2 * y + c
        dsum = m0[...] + m1[...] + m2[...] + m3[...]
        db_ref[...] = jnp.sum(dsum, axis=0, keepdims=True)
        for j in range(N_CHIPS):
            stage_ref[j] = dsum[:, j * n:(j + 1) * n]

        def copy(k, peer, slot):
            px, py, _ = peer
            return pltpu.make_async_remote_copy(
                src_ref=stage_ref.at[2 * px + py], dst_ref=all_ref.at[slot], send_sem=send_sems.at[k],
                recv_sem=recv_sems.at[k], device_id=peer, device_id_type=pl.DeviceIdType.MESH)

        peers = _all_peers(x, y, c)
        for k, peer in enumerate(peers):
            copy(k, peer, me).start()
        all_ref[me] = stage_ref[2 * x + y]
        for k, (px, py, pc) in enumerate(peers):
            copy(k, (px, py, pc), 4 * px + 2 * py + pc).wait_recv()
        for k, peer in enumerate(peers):
            copy(k, peer, me).wait_send()
        cv = cc_ref[...]
        sig = _sigmoid(cv)
        dmb = all_ref[...].reshape(8 * MOD_ROWS, n).astype(BF16)
        dw_ref[...] = _dot_tn((cv * sig).astype(BF16), dmb)
        dsc = _dot_nt(dmb, w_ref[...])
        dctx = dsc[ctx_row:ctx_row + 1, :]
        for dev in range(1, 8):
            dctx = dctx + dsc[dev * MOD_ROWS + ctx_row:dev * MOD_ROWS + ctx_row + 1, :]
        cx = cv[ctx_row:ctx_row + 1, :]
        sx = sig[ctx_row:ctx_row + 1, :]
        dctx_ref[...] = dctx * (sx * (1.0 + cx * (1.0 - sx))) * jnp.where(c == 0, 1.0, 0.0)

    vmem = pl.BlockSpec(memory_space=pltpu.VMEM)
    return pl.pallas_call(
        body, name="ada_bwd_tp", in_specs=[vmem] * 6, out_specs=[vmem] * 3,
        out_shape=[_sds((d, n), F32), _sds((1, N_MOD * d), F32), _sds((1, d), F32)],
        scratch_shapes=[pltpu.VMEM((N_CHIPS, MOD_ROWS, n), F32), pltpu.VMEM((8, MOD_ROWS, n), F32),
                        pltpu.SemaphoreType.DMA((7,)), pltpu.SemaphoreType.DMA((7,))],
        compiler_params=pltpu.CompilerParams(vmem_limit_bytes=V7X_VMEM_LIMIT),
    )(cc_all, *dmods, w)


def _rope_tables(s, ctx):
    pos = np.arange(s, dtype=np.float32)
    inv = (np.float32(ROPE_BASE) ** (-np.arange(0, QK_ROPE // 2, 2, dtype=np.float32) / np.float32(QK_ROPE // 2)))
    ang_r = np.floor(pos / GRID_W)[:, None] * inv
    ang_c = (pos - GRID_W * np.floor(pos / GRID_W))[:, None] * inv
    ang = np.concatenate([ang_r, ang_r, ang_c, ang_c], axis=-1).astype(np.float32)
    cos, sin = np.cos(ang), np.sin(ang)
    half_b = (np.arange(QK_ROPE) // 8) % 2 == 1
    sin_a = np.where(half_b, sin, 0.0)
    sin_b = np.where(half_b, 0.0, -sin)

    def place(tab, fill):
        full = np.full((s + ctx, HEAD_PAD), fill, np.float32)
        full[:s, QK_NOPE:QK_HEAD] = tab
        return jnp.asarray(full)

    return place(cos, 1.0), place(sin_a, 0.0), place(sin_b, 0.0)


def _pad_last(a, n):
    return jnp.pad(a, [(0, 0)] * (a.ndim - 1) + [(0, n - a.shape[-1])])


def _flat_rows(parts, rows, width):
    flat = jnp.concatenate([p.reshape(-1) for p in parts])
    return jnp.pad(flat, (0, rows * width - flat.shape[0])).reshape(rows, width)


def kernel(x, c, ctx, c_ctx, w_ada, b_ada, norm1_w, ffn1_w1, ffn1_w3, ffn1_w2, norm2_w, w_in, q_a_norm_w, w_uq, kv_a_norm_w, w_ukv, q_norm_w, k_norm_w, v_norm_w, w_s, b_s, w_out, norm3_w, ffn2_w1, ffn2_w3, ffn2_w2, loss_target, m_c_ctx, m_w_ada, m_b_ada, m_norm1_w, m_ffn1_w1, m_ffn1_w3, m_ffn1_w2, m_norm2_w, m_w_in, m_q_a_norm_w, m_w_uq, m_kv_a_norm_w, m_w_ukv, m_q_norm_w, m_k_norm_w, m_v_norm_w, m_w_s, m_b_s, m_w_out, m_norm3_w, m_ffn2_w1, m_ffn2_w3, m_ffn2_w2, v_c_ctx, v_w_ada, v_b_ada, v_norm1_w, v_ffn1_w1, v_ffn1_w3, v_ffn1_w2, v_norm2_w, v_w_in, v_q_a_norm_w, v_w_uq, v_kv_a_norm_w, v_w_ukv, v_q_norm_w, v_k_norm_w, v_v_norm_w, v_w_s, v_b_s, v_w_out, v_norm3_w, v_ffn2_w1, v_ffn2_w3, v_ffn2_w2):
    wts = dict(c_ctx=c_ctx, w_ada=w_ada, b_ada=b_ada, norm1_w=norm1_w, ffn1_w1=ffn1_w1, ffn1_w3=ffn1_w3, ffn1_w2=ffn1_w2,
               norm2_w=norm2_w, w_in=w_in, q_a_norm_w=q_a_norm_w, w_uq=w_uq, kv_a_norm_w=kv_a_norm_w, w_ukv=w_ukv,
               q_norm_w=q_norm_w, k_norm_w=k_norm_w, v_norm_w=v_norm_w, w_s=w_s, b_s=b_s, w_out=w_out, norm3_w=norm3_w,
               ffn2_w1=ffn2_w1, ffn2_w3=ffn2_w3, ffn2_w2=ffn2_w2)
    moms = dict(c_ctx=m_c_ctx, w_ada=m_w_ada, b_ada=m_b_ada, norm1_w=m_norm1_w, ffn1_w1=m_ffn1_w1, ffn1_w3=m_ffn1_w3,
                ffn1_w2=m_ffn1_w2, norm2_w=m_norm2_w, w_in=m_w_in, q_a_norm_w=m_q_a_norm_w, w_uq=m_w_uq,
                kv_a_norm_w=m_kv_a_norm_w, w_ukv=m_w_ukv, q_norm_w=m_q_norm_w, k_norm_w=m_k_norm_w, v_norm_w=m_v_norm_w,
                w_s=m_w_s, b_s=m_b_s, w_out=m_w_out, norm3_w=m_norm3_w, ffn2_w1=m_ffn2_w1, ffn2_w3=m_ffn2_w3,
                ffn2_w2=m_ffn2_w2)
    vars_ = dict(c_ctx=v_c_ctx, w_ada=v_w_ada, b_ada=v_b_ada, norm1_w=v_norm1_w, ffn1_w1=v_ffn1_w1, ffn1_w3=v_ffn1_w3,
                 ffn1_w2=v_ffn1_w2, norm2_w=v_norm2_w, w_in=v_w_in, q_a_norm_w=v_q_a_norm_w, w_uq=v_w_uq,
                 kv_a_norm_w=v_kv_a_norm_w, w_ukv=v_w_ukv, q_norm_w=v_q_norm_w, k_norm_w=v_k_norm_w, v_norm_w=v_v_norm_w,
                 w_s=v_w_s, b_s=v_b_s, w_out=v_w_out, norm3_w=v_norm3_w, ffn2_w1=v_ffn2_w1, ffn2_w3=v_ffn2_w3,
                 ffn2_w2=v_ffn2_w2)

    nb, s, d = x.shape
    nctx = ctx.shape[1]
    t, tc = nb * s, nb * nctx
    t_all = t + tc
    sk = s + nctx
    assert nb + 1 <= MOD_ROWS and d % LANES == 0
    tm = _token_tile(s, nctx)

    def held(n, a_):
        return jnp.swapaxes(a_[0], 0, 1) if n in T_WEIGHTS else a_[0]

    def unheld(n, a_):
        return (jnp.swapaxes(a_, 0, 1) if n in T_WEIGHTS else a_)[None]

    shard = {n: held(n, wts[n]).astype(BF16) for n in SHARDED}
    full = {}

    def unshard(names, blocks):
        for n, g4 in zip(names, blocks):
            _, r_, c_ = g4.shape
            if n in ROW_SHARDED or n in T_WEIGHTS:
                full[n] = g4.reshape(N_CHIPS * r_, c_)
            else:
                full[n] = g4.transpose(1, 0, 2).reshape(r_, N_CHIPS * c_)

    def chip_major(n, g_):
        if n in ROW_SHARDED or n in T_WEIGHTS:
            return g_.reshape(N_CHIPS, g_.shape[0] // N_CHIPS, g_.shape[1]).astype(BF16)
        r_, cols = g_.shape
        return g_.reshape(r_, N_CHIPS, cols // N_CHIPS).transpose(1, 0, 2).astype(BF16)

    cc = jnp.concatenate([c, c_ctx[None, :], jnp.zeros((MOD_ROWS - nb - 1, d), F32)], axis=0)
    n_ada = shard["w_ada"].shape[1]
    assert n_ada % LANES == 0
    my_chip = 2 * lax.axis_index("x") + lax.axis_index("y")
    b_cols = lax.dynamic_slice_in_dim(b_ada, my_chip * n_ada, n_ada, axis=1)
    got, cc_all, table = _first_exchange([shard[n] for n in FIRST_WEIGHTS], cc, shard["w_ada"], b_cols)
    unshard(FIRST_WEIGHTS, got)
    cc_all = cc_all.reshape(8 * MOD_ROWS, d)
    mod = table.transpose(1, 0, 2).reshape(MOD_ROWS, N_MOD, d)
    wsb = w_s[0].astype(BF16)
    wcat = wsb.transpose(1, 0, 2).reshape(CHUNK, GROUPS * CHUNK)
    wcat_t = wsb.transpose(2, 0, 1).reshape(CHUNK, GROUPS * CHUNK)
    bias = jnp.repeat(b_s[0].T, GROUP_DIM, axis=1)
    vnw = v_norm_w.reshape(1, GMLP_W)
    lane = jnp.arange(GMLP_W)
    ones = (lane[:, None] // GROUP_DIM == lane[None, :] // GROUP_DIM).astype(BF16)
    qnw = _pad_last(q_norm_w, HEAD_PAD)
    knw = _pad_last(k_norm_w, HEAD_PAD)
    tabs = _rope_tables(s, nctx)

    x_lat, x_ctx = x.reshape(t, d), ctx.reshape(tc, d)
    (xs1, a1, b1, y1), got = _ffn_fwd(x_lat, x_ctx, mod, norm1_w, full["ffn1_w1"], full["ffn1_w3"], full["ffn1_w2"], 0, s,
                                      nb, tm, "ffn1_fwd", exch=("gather", [shard[n] for n in MIX_WEIGHTS]))
    unshard(MIX_WEIGHTS, got)
    wi = full["w_in"]
    wp = jnp.concatenate([wi[0:KV_LORA], jnp.zeros((QK_NOPE, d), BF16), wi[KV_LORA:KV_LORA + QK_ROPE],
                          jnp.zeros((HEAD_PAD - QK_HEAD, d), BF16), wi[KV_LORA + QK_ROPE:]], axis=0)
    wq = jnp.pad(full["w_uq"].reshape(HEADS, QK_HEAD, Q_LORA), ((0, 0), (0, HEAD_PAD - QK_HEAD), (0, 0)))
    wkv = full["w_ukv"].reshape(KV_LORA, HEADS, QK_NOPE + V_HEAD)
    wk = _pad_last(wkv[:, :, :QK_NOPE].transpose(1, 0, 2), HEAD_PAD)
    wv = wkv[:, :, QK_NOPE:].reshape(KV_LORA, HEADS // 2, 2 * V_HEAD).transpose(1, 0, 2)
    h2, proj = _mixin_fwd(xs1, mod, norm2_w, wp, s, nb, tm)
    prep_w = (wq, wk, wv, kv_a_norm_w, q_a_norm_w, qnw, knw)
    q, k_all, v_all = _prep_fwd(proj, 0, nb, s, 0, sk, 0, None, tabs, *prep_w, tm, True, "prep_fwd")
    k_all, v_all = _prep_fwd(proj, t // tm, nb, nctx, s // tm, sk, s // tm, (k_all, v_all), tabs, *prep_w, tm, False,
                             "prep_ctx_fwd")
    tq = _div_tile(s, 512, tm)
    o, lse, got = _attn_fwd(q, k_all, v_all, tq, exch=("gather", [shard[n] for n in LAST_WEIGHTS]))
    unshard(LAST_WEIGHTS, got)
    sg = _gmlp_fwd(proj, t, wcat, bias, vnw, ones, tm)
    x2, mix = _mixout_fwd(o, sg, xs1, mod, full["w_out"], s, tm)
    (dy, a2, b2, y2, loss_part), _ = _ffn_fwd(x2, None, mod, norm3_w, full["ffn2_w1"], full["ffn2_w3"], full["ffn2_w2"], 6,
                                              s, nb, tm, "ffn2_fwd", target=loss_target.reshape(t, d))
    loss = lax.psum(loss_part[0, 0], ("x", "y", "c"))

    grads, cm, recv = {}, {}, {}

    def scatter_of(names):
        return ("scatter", [cm[n] for n in names])

    (dx2, h3, g2, da2, db2, dyb2, dmod_c, grads["norm3_w"]), _ = _ffn_bwd(
        dy, x2, None, a2, b2, y2, mod, norm3_w, full["ffn2_w1"], full["ffn2_w3"], full["ffn2_w2"], 6, s, nb, tm,
        "ffn2_bwd")
    cm["ffn2_w1"] = chip_major("ffn2_w1", _mm_tn(da2, h3, t, "ffn2_dw1"))
    cm["ffn2_w3"] = chip_major("ffn2_w3", _mm_tn(db2, h3, t, "ffn2_dw3"))
    cm["ffn2_w2"] = chip_major("ffn2_w2", _mm_tn(g2, dyb2, t, "ffn2_dw2"))
    dmix, do, dsg, dmod_b = _mixout_bwd(dx2, mix, mod, full["w_out"], s, tm)
    cm["w_out"] = chip_major("w_out", jnp.concatenate([_mm_tn(o, dmix, t, "wout_dw_attn"),
                                                       _mm_tn(sg, dmix, t, "wout_dw_gmlp")], axis=0))
    dpu, dpv, dws, dbs, dvnw = _gmlp_bwd(proj, dsg, wcat, wcat_t, bias, vnw, ones, tm)
    group = LAST_WEIGHTS + ("w_out",)
    (dq, dk, dv), got = _attn_bwd(q, k_all, v_all, do, o, lse, tq, exch=scatter_of(group))
    recv.update(zip(group, got))
    dp0, dwk_c, dwv_c, dkvaw_c, dknw_c = _prep_bwd(
        proj, t // tm, nb, nctx, s // tm, s // tm, t_all, None, tabs, *prep_w, None, dk, dv, None, tm, "prep_ctx_bwd")
    dp0, dwq, dqaw, dqnw, dwk, dwv, dkvaw, dknw = _prep_bwd(
        proj, 0, nb, s, 0, 0, t_all, dp0, tabs, *prep_w, dq, dk, dv, [dwk_c, dwv_c, dkvaw_c, dknw_c], tm, "prep_bwd")
    dxs1, dmod_a, grads["norm2_w"] = _mixin_bwd(dp0, dpu, dpv, xs1, dx2, mod, norm2_w, wp, s, nb, tm)
    dwp = jnp.concatenate([_mm_tn(dp0, h2, t_all, "win_dw_kvq"), _mm_tn(dpu, h2, t, "win_dw_u"),
                           _mm_tn(dpv, h2, t, "win_dw_v")], axis=0)
    cm["w_in"] = chip_major("w_in", jnp.concatenate(
        [dwp[0:KV_LORA], dwp[KV_LORA + QK_NOPE:KV_LORA + QK_HEAD], dwp[256:]], axis=0))
    cm["w_uq"] = chip_major("w_uq", dwq[:, :, :QK_HEAD].transpose(0, 2, 1).reshape(HEADS * QK_HEAD, Q_LORA))
    cm["w_ukv"] = chip_major("w_ukv", jnp.concatenate(
        [dwk[:, :, :QK_NOPE].transpose(1, 0, 2),
         dwv.transpose(1, 0, 2).reshape(KV_LORA, HEADS, V_HEAD)], axis=2).reshape(KV_LORA, HEADS * (QK_NOPE + V_HEAD)))
    (dx_lat, h1, g1, da1, db1, dyb1, dmod_0, grads["norm1_w"]), _ = _ffn_bwd(
        dxs1, x_lat, x_ctx, a1, b1, y1, mod, norm1_w, full["ffn1_w1"], full["ffn1_w3"], full["ffn1_w2"], 0, s, nb, tm,
        "ffn1_bwd")
    dmods = [m_.reshape(MOD_ROWS, N_MOD * d) for m_ in (dmod_0, dmod_a, dmod_b, dmod_c)]
    dw_ada, grads["b_ada"], dctx = _ada_bwd_tp(cc_all, dmods, shard["w_ada"], nb)
    grads["c_ctx"] = dctx[0]
    grads["q_a_norm_w"], grads["kv_a_norm_w"] = dqaw, dkvaw
    grads["q_norm_w"], grads["k_norm_w"] = dqnw[:, :QK_HEAD], dknw[:, :QK_HEAD]
    grads["v_norm_w"], grads["w_s"], grads["b_s"] = dvnw, dws, dbs[:, 0]
    grad_x = dx_lat.reshape(nb, s, d)
    rows_s = _round_up(-(-sum(wts[n].size for n in SMALL) // d), 16)
    cm["small"] = jnp.broadcast_to(_flat_rows([grads[n] for n in SMALL], rows_s, d), (N_CHIPS, rows_s, d))
    group = ("w_in", "w_uq", "w_ukv", "small")
    dw2, got = _mm_tn(g1, dyb1, t_all, "ffn1_dw2", exch=scatter_of(group))
    recv.update(zip(group, got))
    cm["ffn1_w2"] = chip_major("ffn1_w2", dw2)
    dw1, got = _mm_tn(da1, h1, t_all, "ffn1_dw1", exch=scatter_of(("ffn1_w2",)))
    recv["ffn1_w2"] = got[0]
    cm["ffn1_w1"] = chip_major("ffn1_w1", dw1)
    dw3, got = _mm_tn(db1, h1, t_all, "ffn1_dw3", exch=scatter_of(("ffn1_w1",)))
    recv["ffn1_w1"] = got[0]
    cm["ffn1_w3"] = chip_major("ffn1_w3", dw3)
    stepped = {}
    stepped["w_ada"], got = _adamw([dw_ada], wts["w_ada"][0], moms["w_ada"][0], vars_["w_ada"][0], "adamw_w_ada",
                                   exch=scatter_of(("ffn1_w3",)))
    recv["ffn1_w3"] = got[0]

    reduced = tuple(n for n in SHARDED if n != "w_ada") + ("small",)
    part = {n: _sum_slots(recv[n], "sum_" + n) for n in reduced}
    early = LAST_WEIGHTS + ("w_out",)
    late = tuple(n for n in reduced if n not in early)
    sib = dict(zip(early, _swap_cores([part[n] for n in early], "swap_early")))
    sib.update(zip(late, _swap_cores([part[n] for n in late], "swap_late")))
    for n in reduced[:-1]:
        stepped[n], _ = _adamw([part[n], sib[n]], held(n, wts[n]), held(n, moms[n]), held(n, vars_[n]), "adamw_" + n)
    for n in SHARDED:
        stepped[n] = [unheld(n, a_) for a_ in stepped[n]]
    packed, _ = _adamw([part["small"], sib["small"]], _flat_rows([wts[n] for n in SMALL], rows_s, d),
                       _flat_rows([moms[n] for n in SMALL], rows_s, d), _flat_rows([vars_[n] for n in SMALL], rows_s, d),
                       "adamw_small")
    for n in SMALL:
        stepped[n] = []
    for a_ in packed:
        flat = a_.reshape(-1)
        off = 0
        for n in SMALL:
            stepped[n].append(flat[off:off + wts[n].size].reshape(wts[n].shape))
            off += wts[n].size
    return (loss, grad_x, *[stepped[n][0] for n in WEIGHTS], *[stepped[n][1] for n in WEIGHTS],
            *[stepped[n][2] for n in WEIGHTS], *[stepped[n][3] for n in WEIGHTS])
```

```python
import functools
import math

import jax
import jax.numpy as jnp
import numpy as np
from jax import lax
from jax.experimental import pallas as pl
from jax.experimental.pallas import tpu as pltpu

F32 = jnp.float32
BF16 = jnp.bfloat16

EPS = 1e-6
N_MOD = 9
HEADS = 8
QK_NOPE, QK_ROPE, V_HEAD = 64, 32, 64
QK_HEAD = QK_NOPE + QK_ROPE
HEAD_PAD = 128
LN2 = math.log(2.0)
SOFTMAX_SCALE = QK_HEAD ** -0.5 / LN2
Q_LORA, KV_LORA = 256, 128
GROUPS, GROUP_DIM, CHUNK = 8, 64, 128
GMLP_W = GROUPS * GROUP_DIM
MLA_W = HEADS * V_HEAD
IN_COLS = 1440
PROJ_COLS = 1536
GRID_W = 64
ROPE_BASE = 10000.0
MOD_ROWS = 16
ADAM_LR, ADAM_B1, ADAM_B2, ADAM_EPS, ADAM_WD, ADAM_STEP = 0.001, 0.9, 0.999, 1e-08, 0.01, 10
N_CHIPS = 4
LANES = 128
V7X_VMEM_LIMIT = 56 * 1024 * 1024
GELU_C = math.sqrt(2.0 / math.pi)

SHARDED = ("w_ada", "ffn1_w1", "ffn1_w3", "ffn1_w2", "w_in", "w_uq", "w_ukv", "w_out", "ffn2_w1", "ffn2_w3", "ffn2_w2")
ROW_SHARDED = ("ffn1_w2", "w_out", "ffn2_w2")
T_WEIGHTS = ("ffn1_w1", "ffn1_w3", "ffn2_w1", "ffn2_w3", "w_in", "w_uq")
FIRST_WEIGHTS = ("ffn1_w1", "ffn1_w3", "ffn1_w2")
MIX_WEIGHTS = ("w_in", "w_uq", "w_ukv", "w_out")
LAST_WEIGHTS = ("ffn2_w1", "ffn2_w3", "ffn2_w2")
SMALL = ("c_ctx", "b_ada", "norm1_w", "norm2_w", "q_a_norm_w", "kv_a_norm_w", "q_norm_w", "k_norm_w", "v_norm_w",
         "w_s", "b_s", "norm3_w")
WEIGHTS = ("c_ctx", "w_ada", "b_ada", "norm1_w", "ffn1_w1", "ffn1_w3", "ffn1_w2", "norm2_w", "w_in", "q_a_norm_w",
           "w_uq", "kv_a_norm_w", "w_ukv", "q_norm_w", "k_norm_w", "v_norm_w", "w_s", "b_s", "w_out", "norm3_w",
           "ffn2_w1", "ffn2_w3", "ffn2_w2")


def _round_up(n, m):
    return (n + m - 1) // m * m


def _div_tile(n, target, mult):
    best = None
    for t in range(mult, min(n, target) + 1, mult):
        if n % t == 0:
            best = t
    return n if best is None else best


def _dot(a, b):
    return lax.dot_general(a, b, (((1,), (0,)), ((), ())), preferred_element_type=F32)


def _dot_nt(a, b):
    return lax.dot_general(a, b, (((1,), (1,)), ((), ())), preferred_element_type=F32)


def _dot_tn(a, b):
    return lax.dot_general(a, b, (((0,), (0,)), ((), ())), preferred_element_type=F32)


def _sigmoid(x):
    return 1.0 / (1.0 + jnp.exp(-x))


def _gelu(x):
    return 0.5 * x * (1.0 + jnp.tanh(GELU_C * (x + 0.044715 * x * x * x)))


def _gelu_grad(x):
    t = jnp.tanh(GELU_C * (x + 0.044715 * x * x * x))
    return 0.5 * (1.0 + t) + 0.5 * x * (1.0 - t * t) * (GELU_C * (1.0 + 3 * 0.044715 * x * x))


def _rope(x, cos, sin_a, sin_b):
    return x * cos + pltpu.roll(x, 8, 1) * sin_a + pltpu.roll(x, HEAD_PAD - 8, 1) * sin_b


def _rope_t(d, cos, sin_a, sin_b):
    return d * cos + pltpu.roll(d * sin_a, HEAD_PAD - 8, 1) + pltpu.roll(d * sin_b, 8, 1)


def _rope3(x, cos, sin_a, sin_b):
    return x * cos + pltpu.roll(x, 8, 2) * sin_a + pltpu.roll(x, HEAD_PAD - 8, 2) * sin_b


def _rope3_t(d, cos, sin_a, sin_b):
    return d * cos + pltpu.roll(d * sin_a, HEAD_PAD - 8, 2) + pltpu.roll(d * sin_b, 8, 2)


def _group_sum(x, ones_ref):
    hi = x.astype(BF16)
    lo = (x - hi.astype(F32)).astype(BF16)
    return _dot(hi, ones_ref[...]) + _dot(lo, ones_ref[...])


def _params(n_axes):
    return pltpu.CompilerParams(dimension_semantics=("arbitrary",) * n_axes, vmem_limit_bytes=V7X_VMEM_LIMIT)


def _whole(shape):
    nd = len(shape)
    return pl.BlockSpec(shape, lambda *_: (0,) * nd, pipeline_mode=pl.Buffered(1))


def _sds(shape, dtype):
    return jax.ShapeDtypeStruct(shape, dtype)


def _token_tile(s, ctx):
    return _div_tile(math.gcd(s, ctx), 256, CHUNK)


def _other_chips(x, y):
    return [(1 - x, y), (x, 1 - y), (1 - x, 1 - y)]


def _exch_copies(kind, srcs, dsts, send_sems, recv_sems, local_sems):
    x, y, c = lax.axis_index("x"), lax.axis_index("y"), lax.axis_index("c")
    me = 2 * x + y
    local, sends, arrivals = [], [], []
    for w, (src, dst) in enumerate(zip(srcs, dsts)):
        own = src if kind == "gather" else src.at[me]
        local.append(pltpu.make_async_copy(own, dst.at[me], local_sems.at[w]))
        for k, (px, py) in enumerate(_other_chips(x, y)):
            sem = dict(send_sem=send_sems.at[3 * w + k], recv_sem=recv_sems.at[3 * w + k], device_id=(px, py, c),
                       device_id_type=pl.DeviceIdType.MESH)
            out = src if kind == "gather" else src.at[2 * px + py]
            sends.append(pltpu.make_async_remote_copy(src_ref=out, dst_ref=dst.at[me], **sem))
            arrivals.append(pltpu.make_async_remote_copy(src_ref=own, dst_ref=dst.at[2 * px + py], **sem))
    return local, sends, arrivals


def _exch_start(kind, srcs, dsts, sems):
    local, sends, _ = _exch_copies(kind, srcs, dsts, *sems)
    for cp in local + sends:
        cp.start()


def _exch_wait(kind, srcs, dsts, sems):
    local, sends, arrivals = _exch_copies(kind, srcs, dsts, *sems)
    for cp in arrivals:
        cp.wait_recv()
    for cp in sends:
        cp.wait_send()
    for cp in local:
        cp.wait()


def _exch_scratch(n):
    return [pltpu.SemaphoreType.DMA((3 * n,)), pltpu.SemaphoreType.DMA((3 * n,)), pltpu.SemaphoreType.DMA((n,))]


def _exch_shapes(kind, arrays):
    return [_sds((N_CHIPS,) + a.shape if kind == "gather" else a.shape, a.dtype) for a in arrays]


def _hosted_call(body, name, grid, in_specs, out_specs, out_shape, operands, scratch=(), exch=None):
    n_axes = len(grid)
    if exch is None:
        outs = pl.pallas_call(body, name=name, grid=grid, in_specs=list(in_specs), out_specs=list(out_specs),
                              out_shape=list(out_shape), scratch_shapes=list(scratch),
                              compiler_params=_params(n_axes))(*operands)
        return list(outs), []
    kind, arrays = exch
    n_in, n_out, n_sc, n_ex = len(in_specs), len(out_specs), len(scratch), len(arrays)

    def hosted(*refs):
        cin, ein = refs[:n_in], refs[n_in:n_in + n_ex]
        o0 = n_in + n_ex
        cout, eout = refs[o0:o0 + n_out], refs[o0 + n_out:o0 + n_out + n_ex]
        rest = refs[o0 + n_out + n_ex:]
        csc, sems = rest[:n_sc], rest[n_sc:]
        first = functools.reduce(jnp.logical_and, [pl.program_id(a) == 0 for a in range(n_axes)])
        last = functools.reduce(jnp.logical_and, [pl.program_id(a) == grid[a] - 1 for a in range(n_axes)])

        @pl.when(first)
        def _():
            _exch_start(kind, ein, eout, sems)

        body(*cin, *cout, *csc)

        @pl.when(last)
        def _():
            _exch_wait(kind, ein, eout, sems)

    any_spec = pl.BlockSpec(memory_space=pl.ANY)
    outs = pl.pallas_call(
        hosted, name=name, grid=grid, in_specs=list(in_specs) + [any_spec] * n_ex,
        out_specs=list(out_specs) + [any_spec] * n_ex, out_shape=list(out_shape) + _exch_shapes(kind, arrays),
        scratch_shapes=list(scratch) + _exch_scratch(n_ex), compiler_params=_params(n_axes),
    )(*operands, *arrays)
    return list(outs[:n_out]), list(outs[n_out:])


class _TokenTiles:
    def __init__(self, t, tc, tm):
        self.n_lat, self.n_ctx = t // tm, tc // tm
        self.n_all = self.n_lat + self.n_ctx

    def tile(self, i):
        return (i + self.n_lat) % self.n_all if self.n_ctx else i

    def is_lat(self, i):
        return self.tile(i) < self.n_lat

    def row(self, i):
        return (self.tile(i), 0)

    def lat_row(self, i):
        return (jnp.where(self.is_lat(i), self.tile(i), 0), 0) if self.n_ctx else (i, 0)

    def ctx_row(self, i):
        return (jnp.where(self.is_lat(i), self.n_ctx - 1, self.tile(i) - self.n_lat), 0)


def _ffn_fwd(x_lat, x_ctx, mod, nw, w1, w3, w2, k0, s, nb, tm, name, target=None, exch=None):
    t, d = x_lat.shape
    tc = 0 if x_ctx is None else x_ctx.shape[0]
    f = w1.shape[0]
    tiles = _TokenTiles(t, tc, tm)
    n_x = 2 if tc else 1
    n_t = 0 if target is None else 1
    assert not (tc and n_t)

    def body(*refs):
        x_ref = refs[0]
        t_ref = refs[n_x] if n_t else None
        mod_ref, nw_ref, w1_ref, w3_ref, w2_ref, o_ref, a_ref, b_ref, y_ref = refs[n_x + n_t:n_x + n_t + 9]
        i = pl.program_id(0)
        g = jnp.minimum((tiles.tile(i) * tm) // s, nb)
        shift = mod_ref[g, pl.ds(k0, 1), :]
        scale = mod_ref[g, pl.ds(k0 + 1, 1), :]
        gate = mod_ref[g, pl.ds(k0 + 2, 1), :]
        x = jnp.where(tiles.is_lat(i), x_ref[...], refs[1][...]) if tc else x_ref[...]
        r = lax.rsqrt(jnp.mean(x * x, axis=-1, keepdims=True) + EPS)
        hb = ((x * r * nw_ref[...]) * (1.0 + scale) + shift).astype(BF16)
        a = _dot_nt(hb, w1_ref[...])
        b = _dot_nt(hb, w3_ref[...])
        gb = (a * _sigmoid(a) * b).astype(BF16)
        y = _dot(gb, w2_ref[...])
        out = x + (0.5 * gate) * y
        a_ref[...] = a.astype(BF16)
        b_ref[...] = b.astype(BF16)
        y_ref[...] = y.astype(BF16)
        if n_t:
            loss_ref, acc_ref = refs[-2:]

            @pl.when(i == 0)
            def _():
                acc_ref[...] = jnp.zeros_like(acc_ref)

            e = out - t_ref[...]
            o_ref[...] = e * (1.0 / d)
            acc_ref[...] += jnp.sum(e * e, axis=0, keepdims=True)

            @pl.when(i == tiles.n_all - 1)
            def _():
                loss_ref[...] = (0.5 / d) * jnp.sum(acc_ref[...], axis=-1, keepdims=True)
        else:
            o_ref[...] = out

    td = pl.BlockSpec((tm, d), tiles.row)
    tf = pl.BlockSpec((tm, f), tiles.row)
    return _hosted_call(
        body, name, (tiles.n_all,),
        [pl.BlockSpec((tm, d), tiles.lat_row)] + ([pl.BlockSpec((tm, d), tiles.ctx_row)] if tc else []) + [td] * n_t
        + [_whole(mod.shape), _whole(nw.shape), _whole(w1.shape), _whole(w3.shape), _whole(w2.shape)],
        [td, tf, tf, td] + [pl.BlockSpec((1, 1), lambda i: (0, 0))] * n_t,
        [_sds((t + tc, d), F32), _sds((t + tc, f), BF16), _sds((t + tc, f), BF16), _sds((t + tc, d), BF16)]
        + [_sds((1, 1), F32)] * n_t,
        (x_lat,) + ((x_ctx,) if tc else ()) + ((target,) if n_t else ()) + (mod, nw, w1, w3, w2),
        scratch=[pltpu.VMEM((1, d), F32)] * n_t, exch=exch)


def _ffn_bwd(dout, x_lat, x_ctx, a, b, y, mod, nw, w1, w3, w2, k0, s, nb, tm, name, exch=None):
    t, d = x_lat.shape
    tc = 0 if x_ctx is None else x_ctx.shape[0]
    f = w1.shape[0]
    nch = 2 if (f // 2) % LANES == 0 and f % 2 == 0 else 1
    fc = f // nch
    tiles = _TokenTiles(t, tc, tm)
    n_x = 2 if tc else 1

    def body(*refs):
        do_ref, x_ref = refs[0], refs[1]
        (a_ref, b_ref, y_ref, mod_ref, nw_ref, w1_ref, w3_ref, w2_ref,
         dx_ref, h_ref, g_ref, da_ref, db_ref, dy_ref, dmod_ref, dnw_ref) = refs[1 + n_x:]
        i = pl.program_id(0)

        @pl.when(i == 0)
        def _():
            dmod_ref[...] = jnp.zeros_like(dmod_ref)
            dnw_ref[...] = jnp.zeros_like(dnw_ref)

        g = jnp.minimum((tiles.tile(i) * tm) // s, nb)
        shift = mod_ref[g, pl.ds(k0, 1), :]
        scale = mod_ref[g, pl.ds(k0 + 1, 1), :]
        gate = mod_ref[g, pl.ds(k0 + 2, 1), :]
        x = jnp.where(tiles.is_lat(i), x_ref[...], refs[2][...]) if tc else x_ref[...]
        dout_v = do_ref[...]
        r = lax.rsqrt(jnp.mean(x * x, axis=-1, keepdims=True) + EPS)
        xh = x * r
        n = xh * nw_ref[...]
        h_ref[...] = (n * (1.0 + scale) + shift).astype(BF16)
        dyb = ((0.5 * gate) * dout_v).astype(BF16)
        dy_ref[...] = dyb
        dmod_ref[g, pl.ds(k0 + 2, 1), :] += 0.5 * jnp.sum(dout_v * y_ref[...].astype(F32), axis=0, keepdims=True)
        dh = jnp.zeros((tm, d), F32)
        for c in range(nch):
            sl = slice(c * fc, (c + 1) * fc)
            dg = _dot_nt(dyb, w2_ref[sl, :])
            av = a_ref[:, sl].astype(F32)
            bv = b_ref[:, sl].astype(F32)
            sig = _sigmoid(av)
            sa = av * sig
            g_ref[:, sl] = (sa * bv).astype(BF16)
            dab = (dg * bv * (sig * (1.0 + av * (1.0 - sig)))).astype(BF16)
            dbb = (dg * sa).astype(BF16)
            da_ref[:, sl] = dab
            db_ref[:, sl] = dbb
            dh = dh + _dot(dab, w1_ref[sl, :]) + _dot(dbb, w3_ref[sl, :])
        dmod_ref[g, pl.ds(k0, 1), :] += jnp.sum(dh, axis=0, keepdims=True)
        dmod_ref[g, pl.ds(k0 + 1, 1), :] += jnp.sum(dh * n, axis=0, keepdims=True)
        dn = dh * (1.0 + scale)
        dnw_ref[...] += jnp.sum(dn * xh, axis=0, keepdims=True)
        dxh = dn * nw_ref[...]
        dx_ref[...] = dout_v + r * (dxh - xh * jnp.mean(dxh * xh, axis=-1, keepdims=True))

    td = pl.BlockSpec((tm, d), tiles.row)
    tf = pl.BlockSpec((tm, f), tiles.row)
    lat = pl.BlockSpec((tm, d), tiles.lat_row)
    ta = t + tc
    return _hosted_call(
        body, name, (tiles.n_all,),
        [td, lat] + ([pl.BlockSpec((tm, d), tiles.ctx_row)] if tc else [])
        + [tf, tf, td, _whole(mod.shape), _whole(nw.shape), _whole(w1.shape), _whole(w3.shape), _whole(w2.shape)],
        [lat, td, tf, tf, tf, td, pl.BlockSpec(mod.shape, lambda i: (0, 0, 0)), pl.BlockSpec((1, d), lambda i: (0, 0))],
        [_sds((t, d), F32), _sds((ta, d), BF16), _sds((ta, f), BF16), _sds((ta, f), BF16), _sds((ta, f), BF16),
         _sds((ta, d), BF16), _sds(mod.shape, F32), _sds((1, d), F32)],
        (dout, x_lat) + ((x_ctx,) if tc else ()) + (a, b, y, mod, nw, w1, w3, w2), exch=exch)


def _mm_tn(a, b, rows, name, exch=None):
    m = a.shape[1]
    n = b.shape[1]
    bm = _div_tile(m, 1408, LANES)
    bn = _div_tile(n, 1408, LANES)
    bk = _div_tile(rows, 2304, LANES)
    nk = rows // bk

    def body(a_ref, b_ref, o_ref, acc_ref):
        k = pl.program_id(2)

        @pl.when(k == 0)
        def _():
            acc_ref[...] = jnp.zeros_like(acc_ref)

        acc_ref[...] += _dot_tn(a_ref[...], b_ref[...])

        @pl.when(k == nk - 1)
        def _():
            o_ref[...] = acc_ref[...].astype(BF16)

    (out,), got = _hosted_call(
        body, name, (m // bm, n // bn, nk),
        [pl.BlockSpec((bk, bm), lambda i, j, k: (k, i)), pl.BlockSpec((bk, bn), lambda i, j, k: (k, j))],
        [pl.BlockSpec((bm, bn), lambda i, j, k: (i, j))], [_sds((m, n), BF16)], (a, b),
        scratch=[pltpu.VMEM((bm, bn), F32)], exch=exch)
    return out if exch is None else (out, got)


def _mixin_fwd(xs, mod, nw, wp, s, nb, tm):
    t, d = xs.shape

    def body(x_ref, mod_ref, nw_ref, wp_ref, h_ref, p_ref):
        g = jnp.minimum((pl.program_id(0) * tm) // s, nb)
        shift = mod_ref[g, pl.ds(3, 1), :]
        scale = mod_ref[g, pl.ds(4, 1), :]
        x = x_ref[...]
        r = lax.rsqrt(jnp.mean(x * x, axis=-1, keepdims=True) + EPS)
        hb = ((x * r * nw_ref[...]) * (1.0 + scale) + shift).astype(BF16)
        h_ref[...] = hb
        p_ref[...] = _dot_nt(hb, wp_ref[...]).astype(BF16)

    row = lambda i: (i, 0)
    return pl.pallas_call(
        body, name="mixin_fwd", grid=(t // tm,),
        in_specs=[pl.BlockSpec((tm, d), row), _whole(mod.shape), _whole(nw.shape), _whole(wp.shape)],
        out_specs=[pl.BlockSpec((tm, d), row), pl.BlockSpec((tm, PROJ_COLS), row)],
        out_shape=[_sds((t, d), BF16), _sds((t, PROJ_COLS), BF16)], compiler_params=_params(1),
    )(xs, mod, nw, wp)


def _mixin_bwd(dp0, dpu, dpv, xs, dres, mod, nw, wp, s, nb, tm):
    t_all, d = xs.shape
    nlat = dres.shape[0] // tm

    def body(p0_ref, pu_ref, pv_ref, x_ref, dr_ref, mod_ref, nw_ref, wp_ref, dx_ref, dmod_ref, dnw_ref):
        i = pl.program_id(0)

        @pl.when(i == 0)
        def _():
            dmod_ref[...] = jnp.zeros_like(dmod_ref)
            dnw_ref[...] = jnp.zeros_like(dnw_ref)

        lat = i < nlat
        g = jnp.minimum((i * tm) // s, nb)
        scale = mod_ref[g, pl.ds(4, 1), :]
        dh = _dot(p0_ref[...], wp_ref[0:512, :])
        extra = _dot(pu_ref[...], wp_ref[512:1024, :]) + _dot(pv_ref[...], wp_ref[1024:1536, :])
        dh = dh + jnp.where(lat, extra, 0.0)
        x = x_ref[...]
        r = lax.rsqrt(jnp.mean(x * x, axis=-1, keepdims=True) + EPS)
        xh = x * r
        n = xh * nw_ref[...]
        dmod_ref[g, pl.ds(3, 1), :] += jnp.sum(dh, axis=0, keepdims=True)
        dmod_ref[g, pl.ds(4, 1), :] += jnp.sum(dh * n, axis=0, keepdims=True)
        dn = dh * (1.0 + scale)
        dnw_ref[...] += jnp.sum(dn * xh, axis=0, keepdims=True)
        dxh = dn * nw_ref[...]
        dx_ref[...] = jnp.where(lat, dr_ref[...], 0.0) + r * (dxh - xh * jnp.mean(dxh * xh, axis=-1, keepdims=True))

    row = lambda i: (i, 0)
    lrow = lambda i: (jnp.minimum(i, nlat - 1), 0)
    return pl.pallas_call(
        body, name="mixin_bwd", grid=(t_all // tm,),
        in_specs=[pl.BlockSpec((tm, 512), row), pl.BlockSpec((tm, 512), lrow), pl.BlockSpec((tm, 512), lrow),
                  pl.BlockSpec((tm, d), row), pl.BlockSpec((tm, d), lrow), _whole(mod.shape), _whole(nw.shape),
                  _whole(wp.shape)],
        out_specs=[pl.BlockSpec((tm, d), row), pl.BlockSpec(mod.shape, lambda i: (0, 0, 0)),
                   pl.BlockSpec((1, d), lambda i: (0, 0))],
        out_shape=[_sds((t_all, d), F32), _sds(mod.shape, F32), _sds((1, d), F32)], compiler_params=_params(1),
    )(dp0, dpu, dpv, xs, dres, mod, nw, wp)


def _prep_fwd(proj, row0, nb, s, pos0, sk, key0, into, tabs, wq, wk, wv, kvaw, qaw, qnw, knw, tm, with_q, name):
    nblk = s // tm
    n_into = 0 if into is None else 2

    def body(p_ref, cos_ref, sa_ref, sb_ref, wq_ref, wk_ref, wv_ref, kvaw_ref, qaw_ref, qnw_ref, knw_ref, *rest):
        outs, heads_ref = rest[n_into:-1], rest[-1]
        q_ref, k_ref, v_ref = outs if with_q else (None,) + outs
        cos, sin_a, sin_b = cos_ref[...][None], sa_ref[...][None], sb_ref[...][None]

        def normed_roped(w_ref, src, extra, nw_ref, o_ref, post):
            for h in range(HEADS):
                heads_ref[h] = _dot_nt(src, w_ref[h]) if extra is None else _dot(src, w_ref[h])
            xp = heads_ref[...] if extra is None else heads_ref[...] + extra[None]
            r = lax.rsqrt(jnp.sum(xp * xp, axis=-1, keepdims=True) * (1.0 / QK_HEAD) + EPS)
            o_ref[...] = _rope3(xp * r * (nw_ref[...] * post)[None], cos, sin_a, sin_b).astype(BF16)

        ckv = p_ref[:, 0:128].astype(F32)
        rkv = lax.rsqrt(jnp.mean(ckv * ckv, axis=-1, keepdims=True) + EPS)
        ckvb = (ckv * rkv * kvaw_ref[...]).astype(BF16)
        normed_roped(wk_ref, ckvb, p_ref[:, 128:256].astype(F32), knw_ref, k_ref, 1.0)
        for j in range(HEADS // 2):
            v_ref[j] = _dot(ckvb, wv_ref[j]).astype(BF16)
        if with_q:
            cq = p_ref[:, 256:512].astype(F32)
            rq = lax.rsqrt(jnp.mean(cq * cq, axis=-1, keepdims=True) + EPS)
            normed_roped(wq_ref, (cq * rq * qaw_ref[...]).astype(BF16), None, qnw_ref, q_ref, SOFTMAX_SCALE)

    tab = pl.BlockSpec((tm, HEAD_PAD), lambda i: (pos0 + i % nblk, 0))
    qspec = pl.BlockSpec((None, HEADS, tm, HEAD_PAD), lambda i: (i // nblk, 0, i % nblk, 0))
    kspec = pl.BlockSpec((None, HEADS, tm, HEAD_PAD), lambda i: (i // nblk, 0, key0 + i % nblk, 0))
    vspec = pl.BlockSpec((None, HEADS // 2, tm, HEAD_PAD), lambda i: (i // nblk, 0, key0 + i % nblk, 0))
    qshape = _sds((nb, HEADS, s, HEAD_PAD), BF16)
    kshape = _sds((nb, HEADS, sk, HEAD_PAD), BF16)
    vshape = _sds((nb, HEADS // 2, sk, HEAD_PAD), BF16)
    n_q = 1 if with_q else 0
    return pl.pallas_call(
        body, name=name, grid=(nb * nblk,),
        in_specs=[pl.BlockSpec((tm, 512), lambda i: (row0 + i, 0)), tab, tab, tab, _whole(wq.shape), _whole(wk.shape),
                  _whole(wv.shape), _whole(kvaw.shape), _whole(qaw.shape), _whole(qnw.shape), _whole(knw.shape)]
        + [pl.BlockSpec(memory_space=pl.ANY)] * n_into,
        out_specs=([qspec] if with_q else []) + [kspec, vspec],
        out_shape=([qshape] if with_q else []) + [kshape, vshape],
        scratch_shapes=[pltpu.VMEM((HEADS, tm, HEAD_PAD), F32)],
        input_output_aliases={11: n_q, 12: n_q + 1} if n_into else {}, compiler_params=_params(1),
    )(proj, *tabs, wq, wk, wv, kvaw, qaw, qnw, knw, *(into or ()))


def _prep_bwd(proj, row0, nb, s, pos0, key0, dp_rows, dp_into, tabs, wq, wk, wv, kvaw, qaw, qnw, knw, dq, dk, dv, init, tm,
              name):
    nblk = s // tm
    with_q = dq is not None
    n_init = 0 if init is None else len(init)
    n_into = 0 if dp_into is None else 1

    def body(*refs):
        p_ref, cos_ref, sa_ref, sb_ref, wq_ref, wk_ref, wv_ref, kvaw_ref, qaw_ref, qnw_ref, knw_ref = refs[:11]
        rest = list(refs[11:])
        dq_ref = rest.pop(0) if with_q else None
        dk_ref, dv_ref = rest.pop(0), rest.pop(0)
        init_refs = [rest.pop(0) for _ in range(n_init)]
        if n_into:
            rest.pop(0)
        dp_ref = rest.pop(0)
        if with_q:
            dwq_ref, dqaw_ref, dqnw_ref = rest.pop(0), rest.pop(0), rest.pop(0)
        dwk_ref, dwv_ref, dkvaw_ref, dknw_ref, heads_ref, dhb_ref, dkr_ref = rest
        accs = [dwk_ref, dwv_ref, dkvaw_ref, dknw_ref]

        @pl.when(pl.program_id(0) == 0)
        def _():
            for k, acc in enumerate(accs):
                acc[...] = init_refs[k][...] if n_init else jnp.zeros_like(acc)
            if with_q:
                dwq_ref[...] = jnp.zeros_like(dwq_ref)
                dqaw_ref[...] = jnp.zeros_like(dqaw_ref)
                dqnw_ref[...] = jnp.zeros_like(dqnw_ref)

        cos, sin_a, sin_b = cos_ref[...][None], sa_ref[...][None], sb_ref[...][None]
        lane = lax.broadcasted_iota(jnp.int32, (tm, HEAD_PAD), 1)
        rope_lanes = (lane >= QK_NOPE) & (lane < QK_HEAD)

        def heads_bwd(w_ref, src, extra, nw_ref, d_ref, dnw_ref, dw_ref, post):
            w_t = extra is None
            for h in range(HEADS):
                heads_ref[h] = _dot_nt(src, w_ref[h]) if w_t else _dot(src, w_ref[h])
            xp = heads_ref[...] if extra is None else heads_ref[...] + extra[None]
            r = lax.rsqrt(jnp.sum(xp * xp, axis=-1, keepdims=True) * (1.0 / QK_HEAD) + EPS)
            xh = xp * r
            dn = _rope3_t(d_ref[...], cos, sin_a, sin_b)
            dnw_ref[...] += post * jnp.sum(jnp.sum(dn * xh, axis=0), axis=0, keepdims=True)
            dxh = dn * (nw_ref[...] * post)[None]
            dxp = r * (dxh - xh * (jnp.sum(dxh * xh, axis=-1, keepdims=True) * (1.0 / QK_HEAD)))
            dhb_ref[...] = dxp.astype(BF16)
            dsrc = jnp.zeros((tm, src.shape[1]), F32)
            for h in range(HEADS):
                dsrc = dsrc + (_dot(dhb_ref[h], w_ref[h]) if w_t else _dot_nt(dhb_ref[h], w_ref[h]))
                dw_ref[h] += _dot_tn(src, dhb_ref[h])
            return dsrc, jnp.sum(dxp, axis=0)

        ckv = p_ref[:, 0:128].astype(F32)
        rkv = lax.rsqrt(jnp.mean(ckv * ckv, axis=-1, keepdims=True) + EPS)
        ckvh = ckv * rkv
        ckvb = (ckvh * kvaw_ref[...]).astype(BF16)
        for h in range(HEADS):
            dkr_ref[h] = dk_ref[h].T
        dckv, dkp_sum = heads_bwd(wk_ref, ckvb, p_ref[:, 128:256].astype(F32), knw_ref, dkr_ref, dknw_ref, dwk_ref,
                                  1.0)
        for j in range(HEADS // 2):
            dvb = dv_ref[j].T.astype(BF16)
            dckv = dckv + _dot_nt(dvb, wv_ref[j])
            dwv_ref[j] += _dot_tn(ckvb, dvb)
        dkvaw_ref[...] += jnp.sum(dckv * ckvh, axis=0, keepdims=True)
        dch = dckv * kvaw_ref[...]
        dp_ref[:, 0:128] = (rkv * (dch - ckvh * jnp.mean(dch * ckvh, axis=-1, keepdims=True))).astype(BF16)
        dp_ref[:, 128:256] = jnp.where(rope_lanes, dkp_sum, 0.0).astype(BF16)
        if with_q:
            cq = p_ref[:, 256:512].astype(F32)
            rq = lax.rsqrt(jnp.mean(cq * cq, axis=-1, keepdims=True) + EPS)
            cqh = cq * rq
            cqb = (cqh * qaw_ref[...]).astype(BF16)
            dcq, _ = heads_bwd(wq_ref, cqb, None, qnw_ref, dq_ref, dqnw_ref, dwq_ref, SOFTMAX_SCALE)
            dqaw_ref[...] += jnp.sum(dcq * cqh, axis=0, keepdims=True)
            dqc = dcq * qaw_ref[...]
            dp_ref[:, 256:512] = (rq * (dqc - cqh * jnp.mean(dqc * cqh, axis=-1, keepdims=True))).astype(BF16)
        else:
            dp_ref[:, 256:512] = jnp.zeros((tm, Q_LORA), BF16)

    tab = pl.BlockSpec((tm, HEAD_PAD), lambda i: (pos0 + i % nblk, 0))
    qspec = pl.BlockSpec((None, HEADS, tm, HEAD_PAD), lambda i: (i // nblk, 0, i % nblk, 0))
    kspec = pl.BlockSpec((None, HEADS, HEAD_PAD, tm), lambda i: (i // nblk, 0, 0, key0 + i % nblk))
    vspec = pl.BlockSpec((None, HEADS // 2, HEAD_PAD, tm), lambda i: (i // nblk, 0, 0, key0 + i % nblk))

    def acc_spec(shape):
        nd = len(shape)
        return pl.BlockSpec(shape, lambda i: (0,) * nd)

    acc_shapes = [(HEADS, KV_LORA, HEAD_PAD), (HEADS // 2, KV_LORA, HEAD_PAD), (1, KV_LORA), (1, HEAD_PAD)]
    q_shapes = [(HEADS, Q_LORA, HEAD_PAD), (1, Q_LORA), (1, HEAD_PAD)] if with_q else []
    out_shapes = [(dp_rows, 512)] + q_shapes + acc_shapes
    n_before = 11 + (1 if with_q else 0) + 2 + n_init
    return pl.pallas_call(
        body, name=name, grid=(nb * nblk,),
        in_specs=[pl.BlockSpec((tm, 512), lambda i: (row0 + i, 0)), tab, tab, tab, _whole(wq.shape), _whole(wk.shape),
                  _whole(wv.shape), _whole(kvaw.shape), _whole(qaw.shape), _whole(qnw.shape), _whole(knw.shape)]
        + ([qspec] if with_q else []) + [kspec, vspec] + [_whole(a.shape) for a in (init or [])]
        + [pl.BlockSpec(memory_space=pl.ANY)] * n_into,
        out_specs=[pl.BlockSpec((tm, 512), lambda i: (row0 + i, 0))] + [acc_spec(sh) for sh in q_shapes + acc_shapes],
        out_shape=[_sds(out_shapes[0], BF16)] + [_sds(sh, F32) for sh in out_shapes[1:]],
        scratch_shapes=[pltpu.VMEM((HEADS, tm, HEAD_PAD), F32), pltpu.VMEM((HEADS, tm, HEAD_PAD), BF16),
                        pltpu.VMEM((HEADS, tm, HEAD_PAD), F32)],
        input_output_aliases={n_before: 0} if n_into else {}, compiler_params=_params(1),
    )(proj, *tabs, wq, wk, wv, kvaw, qaw, qnw, knw, *([dq] if with_q else []), dk, dv, *(init or []),
      *([dp_into] if n_into else []))


def _attn_fwd(q, k, v, tq, exch=None):
    nb, _, s, _ = q.shape
    sk = k.shape[2]
    nq = s // tq

    def body(q_ref, k_ref, v_ref, o_ref, lse_ref, vext_ref):
        @pl.when(pl.program_id(2) == 0)
        def _():
            vext_ref[:, 0:HEAD_PAD] = v_ref[...]
            vext_ref[:, HEAD_PAD:2 * HEAD_PAD] = jnp.ones((sk, HEAD_PAD), BF16)

        lane = lax.broadcasted_iota(jnp.int32, (tq, HEAD_PAD), 1)
        outs = []
        for hh in range(2):
            sc = _dot_nt(q_ref[hh], k_ref[hh])
            m = jnp.max(sc, axis=-1, keepdims=True)
            pv = _dot(jnp.exp2(sc - m).astype(BF16), vext_ref[...])
            l = pv[:, HEAD_PAD:HEAD_PAD + 1]
            outs.append(pv[:, 0:HEAD_PAD] / l)
            lse_ref[hh] = m + jnp.log2(l)
        o_ref[...] = jnp.where(lane < V_HEAD, outs[0], outs[1]).astype(BF16)

    (o, lse), got = _hosted_call(
        body, "attn_fwd", (nb, HEADS // 2, nq),
        [pl.BlockSpec((None, 2, tq, HEAD_PAD), lambda b, j, i: (b, j, i, 0)),
         pl.BlockSpec((None, 2, sk, HEAD_PAD), lambda b, j, i: (b, j, 0, 0)),
         pl.BlockSpec((None, None, sk, HEAD_PAD), lambda b, j, i: (b, j, 0, 0))],
        [pl.BlockSpec((tq, HEAD_PAD), lambda b, j, i: (b * nq + i, j)),
         pl.BlockSpec((None, 2, tq, 1), lambda b, j, i: (b, j, i, 0))],
        [_sds((nb * s, MLA_W), BF16), _sds((nb, HEADS, s, 1), F32)], (q, k, v),
        scratch=[pltpu.VMEM((sk, 2 * HEAD_PAD), BF16)], exch=exch)
    return o, lse, got


def _attn_bwd(q, k, v, do, o, lse, tq, exch=None):
    nb, _, s, _ = q.shape
    sk = k.shape[2]
    nq = s // tq

    def body(q_ref, k_ref, v_ref, do_ref, o_ref, lse_ref, dq_ref, dkt_ref, dvt_ref):
        @pl.when(pl.program_id(2) == 0)
        def _():
            dkt_ref[...] = jnp.zeros_like(dkt_ref)
            dvt_ref[...] = jnp.zeros_like(dvt_ref)

        lane = lax.broadcasted_iota(jnp.int32, (tq, HEAD_PAD), 1)
        dov = do_ref[...]
        prod = dov.astype(F32) * o_ref[...].astype(F32)
        for hh in range(2):
            mine = (lane < V_HEAD) if hh == 0 else (lane >= V_HEAD)
            doh = jnp.where(mine, dov, jnp.zeros_like(dov))
            delta = jnp.sum(jnp.where(mine, prod, 0.0), axis=-1, keepdims=True)
            qh = q_ref[hh]
            q_ln2 = (qh.astype(F32) * LN2).astype(BF16)
            kv = k_ref[hh]
            p = jnp.exp2(_dot_nt(qh, kv) - lse_ref[hh])
            u = (p * (_dot_nt(doh, v_ref[...]) - delta)).astype(BF16)
            dq_ref[hh] = _dot(u, kv) * LN2
            dkt_ref[hh] += _dot_tn(q_ln2, u)
            dvt_ref[...] += _dot_tn(doh, p.astype(BF16))

    qspec = pl.BlockSpec((None, 2, tq, HEAD_PAD), lambda b, j, i: (b, j, i, 0))
    kspec = pl.BlockSpec((None, 2, sk, HEAD_PAD), lambda b, j, i: (b, j, 0, 0))
    vspec = pl.BlockSpec((None, None, sk, HEAD_PAD), lambda b, j, i: (b, j, 0, 0))
    ospec = pl.BlockSpec((tq, HEAD_PAD), lambda b, j, i: (b * nq + i, j))
    return _hosted_call(
        body, "attn_bwd", (nb, HEADS // 2, nq),
        [qspec, kspec, vspec, ospec, ospec, pl.BlockSpec((None, 2, tq, 1), lambda b, j, i: (b, j, i, 0))],
        [qspec, pl.BlockSpec((None, 2, HEAD_PAD, sk), lambda b, j, i: (b, j, 0, 0)),
         pl.BlockSpec((None, None, HEAD_PAD, sk), lambda b, j, i: (b, j, 0, 0))],
        [_sds(q.shape, F32), _sds((nb, HEADS, HEAD_PAD, sk), F32), _sds((nb, HEADS // 2, HEAD_PAD, sk), F32)],
        (q, k, v, do, o, lse), exch=exch)


def _group_masks(rows):
    lane = lax.broadcasted_iota(jnp.int32, (rows, GMLP_W), 1)
    return [(lane >= g * GROUP_DIM) & (lane < (g + 1) * GROUP_DIM) for g in range(GROUPS)]


def _gmlp_fwd(proj, t, wcat, bias, vnw, ones, tm):
    def body(u_ref, v_ref, wcat_ref, bias_ref, vnw_ref, ones_ref, o_ref):
        masks = _group_masks(CHUNK)
        gv = _gelu(v_ref[...].astype(F32))
        rv = lax.rsqrt(_group_sum(gv * gv, ones_ref) * (1.0 / GROUP_DIM) + EPS)
        vnb = (gv * rv * vnw_ref[...]).astype(BF16)
        for c in range(tm // CHUNK):
            rows = slice(c * CHUNK, (c + 1) * CHUNK)
            vc = vnb[rows]
            stack = jnp.concatenate([jnp.where(m, vc, jnp.zeros_like(vc)) for m in masks], axis=0)
            sp = _dot(wcat_ref[...], stack) + bias_ref[...]
            o_ref[rows, :] = (_gelu(u_ref[rows, :].astype(F32)) * sp).astype(BF16)

    return pl.pallas_call(
        body, name="gmlp_fwd", grid=(t // tm,),
        in_specs=[pl.BlockSpec((tm, GMLP_W), lambda i: (i, 1)), pl.BlockSpec((tm, GMLP_W), lambda i: (i, 2)),
                  _whole(wcat.shape), _whole(bias.shape), _whole(vnw.shape), _whole(ones.shape)],
        out_specs=pl.BlockSpec((tm, GMLP_W), lambda i: (i, 0)),
        out_shape=_sds((t, GMLP_W), BF16), compiler_params=_params(1),
    )(proj, proj, wcat, bias, vnw, ones)


def _gmlp_bwd(proj, dsg, wcat, wcat_t, bias, vnw, ones, tm):
    t = dsg.shape[0]

    def body(u_ref, v_ref, dsg_ref, wcat_ref, wcatt_ref, bias_ref, vnw_ref, ones_ref,
             du_ref, dv_ref, dws_ref, dbs_ref, dvnw_ref):
        @pl.when(pl.program_id(0) == 0)
        def _():
            dws_ref[...] = jnp.zeros_like(dws_ref)
            dbs_ref[...] = jnp.zeros_like(dbs_ref)
            dvnw_ref[...] = jnp.zeros_like(dvnw_ref)

        masks = _group_masks(CHUNK)
        v = v_ref[...].astype(F32)
        gv = _gelu(v)
        rv = lax.rsqrt(_group_sum(gv * gv, ones_ref) * (1.0 / GROUP_DIM) + EPS)
        xh = gv * rv
        vnb = (xh * vnw_ref[...]).astype(BF16)
        dvn_parts = []
        for c in range(tm // CHUNK):
            rows = slice(c * CHUNK, (c + 1) * CHUNK)
            vc = vnb[rows]
            stack = jnp.concatenate([jnp.where(m, vc, jnp.zeros_like(vc)) for m in masks], axis=0)
            sp = _dot(wcat_ref[...], stack) + bias_ref[...]
            u = u_ref[rows, :].astype(F32)
            dsg_c = dsg_ref[rows, :]
            du_ref[rows, :] = (dsg_c * sp * _gelu_grad(u)).astype(BF16)
            ds = dsg_c * _gelu(u)
            dstack = jnp.concatenate([jnp.where(m, ds, 0.0) for m in masks], axis=0)
            dbs_ref[...] += jnp.broadcast_to(jnp.sum(dstack, axis=-1, keepdims=True), dbs_ref.shape)
            dstb = dstack.astype(BF16)
            dvn_parts.append(_dot(wcatt_ref[...], dstb))
            dws_ref[...] += _dot_nt(dstb, vc)
        dvn = jnp.concatenate(dvn_parts, axis=0) if len(dvn_parts) > 1 else dvn_parts[0]
        dvnw_ref[...] += jnp.sum(dvn * xh, axis=0, keepdims=True)
        dxh = dvn * vnw_ref[...]
        gm = _group_sum(dxh * xh, ones_ref) * (1.0 / GROUP_DIM)
        dv_ref[...] = (rv * (dxh - xh * gm) * _gelu_grad(v)).astype(BF16)

    row = pl.BlockSpec((tm, GMLP_W), lambda i: (i, 0))
    return pl.pallas_call(
        body, name="gmlp_bwd", grid=(t // tm,),
        in_specs=[pl.BlockSpec((tm, GMLP_W), lambda i: (i, 1)), pl.BlockSpec((tm, GMLP_W), lambda i: (i, 2)), row,
                  _whole(wcat.shape), _whole(wcat_t.shape), _whole(bias.shape), _whole(vnw.shape), _whole(ones.shape)],
        out_specs=[row, row, pl.BlockSpec((GROUPS * CHUNK, CHUNK), lambda i: (0, 0)),
                   pl.BlockSpec((GROUPS * CHUNK, CHUNK), lambda i: (0, 0)), pl.BlockSpec((1, GMLP_W), lambda i: (0, 0))],
        out_shape=[_sds((t, GMLP_W), BF16), _sds((t, GMLP_W), BF16), _sds((GROUPS * CHUNK, CHUNK), F32),
                   _sds((GROUPS * CHUNK, CHUNK), F32), _sds((1, GMLP_W), F32)],
        compiler_params=_params(1),
    )(proj, proj, dsg, wcat, wcat_t, bias, vnw, ones)


def _mixout_fwd(o, sg, xs, mod, wout, s, tm):
    t = o.shape[0]
    d = xs.shape[1]

    def body(o_ref, sg_ref, x_ref, mod_ref, w_ref, x2_ref, mix_ref):
        g = (pl.program_id(0) * tm) // s
        gate = mod_ref[g, pl.ds(5, 1), :]
        mix = _dot(o_ref[...], w_ref[0:MLA_W, :]) + _dot(sg_ref[...], w_ref[MLA_W:MLA_W + GMLP_W, :])
        x2_ref[...] = x_ref[...] + gate * mix
        mix_ref[...] = mix.astype(BF16)

    row = lambda i: (i, 0)
    return pl.pallas_call(
        body, name="mixout_fwd", grid=(t // tm,),
        in_specs=[pl.BlockSpec((tm, MLA_W), row), pl.BlockSpec((tm, GMLP_W), row), pl.BlockSpec((tm, d), row),
                  _whole(mod.shape), _whole(wout.shape)],
        out_specs=[pl.BlockSpec((tm, d), row), pl.BlockSpec((tm, d), row)],
        out_shape=[_sds((t, d), F32), _sds((t, d), BF16)], compiler_params=_params(1),
    )(o, sg, xs, mod, wout)


def _mixout_bwd(dx2, mix, mod, wout, s, tm):
    t, d = dx2.shape

    def body(dx_ref, mix_ref, mod_ref, w_ref, dmix_ref, do_ref, dsg_ref, dmod_ref):
        i = pl.program_id(0)

        @pl.when(i == 0)
        def _():
            dmod_ref[...] = jnp.zeros_like(dmod_ref)

        g = (i * tm) // s
        gate = mod_ref[g, pl.ds(5, 1), :]
        dx = dx_ref[...]
        dmod_ref[g, pl.ds(5, 1), :] += jnp.sum(dx * mix_ref[...].astype(F32), axis=0, keepdims=True)
        dmb = (gate * dx).astype(BF16)
        dmix_ref[...] = dmb
        do_ref[...] = _dot_nt(dmb, w_ref[0:MLA_W, :]).astype(BF16)
        dsg_ref[...] = _dot_nt(dmb, w_ref[MLA_W:MLA_W + GMLP_W, :])

    row = lambda i: (i, 0)
    return pl.pallas_call(
        body, name="mixout_bwd", grid=(t // tm,),
        in_specs=[pl.BlockSpec((tm, d), row), pl.BlockSpec((tm, d), row), _whole(mod.shape), _whole(wout.shape)],
        out_specs=[pl.BlockSpec((tm, d), row), pl.BlockSpec((tm, MLA_W), row), pl.BlockSpec((tm, GMLP_W), row),
                   pl.BlockSpec(mod.shape, lambda i: (0, 0, 0))],
        out_shape=[_sds((t, d), BF16), _sds((t, MLA_W), BF16), _sds((t, GMLP_W), F32), _sds(mod.shape, F32)],
        compiler_params=_params(1),
    )(dx2, mix, mod, wout)


def _swap_cores(parts, name):
    n = len(parts)

    def body(*refs):
        srcs, outs, send_sems, recv_sems = refs[:n], refs[n:2 * n], refs[2 * n], refs[2 * n + 1]
        x, y, c = lax.axis_index("x"), lax.axis_index("y"), lax.axis_index("c")
        copies = [pltpu.make_async_remote_copy(
            src_ref=srcs[w], dst_ref=outs[w], send_sem=send_sems.at[w], recv_sem=recv_sems.at[w],
            device_id=(x, y, 1 - c), device_id_type=pl.DeviceIdType.MESH) for w in range(n)]
        for cp in copies:
            cp.start()
        for cp in copies:
            cp.wait()

    any_spec = pl.BlockSpec(memory_space=pl.ANY)
    return pl.pallas_call(
        body, name=name, in_specs=[any_spec] * n, out_specs=[any_spec] * n,
        out_shape=[_sds(p.shape, p.dtype) for p in parts],
        scratch_shapes=[pltpu.SemaphoreType.DMA((n,)), pltpu.SemaphoreType.DMA((n,))],
    )(*parts)


def _row_tile(r, c, mult):
    return _div_tile(r, max(mult, (1 << 18) // c), mult)


def _sum_slots(recv, name):
    _, r, c = recv.shape
    tr = _row_tile(r, c, 16)

    def body(r_ref, o_ref):
        f = lambda k: r_ref[k].astype(F32)
        o_ref[...] = ((f(0) + f(1)) + f(2)) + f(3)

    return pl.pallas_call(
        body, name=name, grid=(r // tr,),
        in_specs=[pl.BlockSpec((N_CHIPS, tr, c), lambda i: (0, i, 0))],
        out_specs=pl.BlockSpec((tr, c), lambda i: (i, 0)),
        out_shape=_sds((r, c), F32), compiler_params=_params(1),
    )(recv)


def _adamw(parts, w, m, v, name, exch=None):
    r, wd = w.shape
    tr = _row_tile(r, wd, 8)
    c1 = 1.0 / (1.0 - ADAM_B1 ** ADAM_STEP)
    c2 = 1.0 / (1.0 - ADAM_B2 ** ADAM_STEP)
    n_p = len(parts)

    def body(*refs):
        p_refs = refs[:n_p]
        w_ref, m_ref, v_ref, g_ref, d_ref, nm_ref, nv_ref = refs[n_p:]
        g = p_refs[0][...]
        for p_ref in p_refs[1:]:
            g = g + p_ref[...]
        nm = ADAM_B1 * m_ref[...] + (1.0 - ADAM_B1) * g
        nv = ADAM_B2 * v_ref[...] + (1.0 - ADAM_B2) * (g * g)
        g_ref[...] = g
        nm_ref[...] = nm
        nv_ref[...] = nv
        d_ref[...] = -ADAM_LR * ((nm * c1) / (jnp.sqrt(nv * c2) + ADAM_EPS) + ADAM_WD * w_ref[...])

    spec = pl.BlockSpec((tr, wd), lambda i: (i, 0))
    return _hosted_call(body, name, (r // tr,), [spec] * (n_p + 3), [spec] * 4, [_sds((r, wd), F32)] * 4,
                        (*parts, w, m, v), exch=exch)


def _all_peers(x, y, c):
    flips = [(dx, dy, dc) for dx in (0, 1) for dy in (0, 1) for dc in (0, 1)][1:]
    return [(1 - x if dx else x, 1 - y if dy else y, 1 - c if dc else c) for dx, dy, dc in flips]


def _first_exchange(shards, cc, w, b):
    n_w = len(shards)
    n = w.shape[1]

    def body(*refs):
        srcs, (cc_ref, w_ref, b_ref) = refs[:n_w], refs[n_w:n_w + 3]
        outs, (all_ref, tab_ref) = refs[n_w + 3:2 * n_w + 3], refs[2 * n_w + 3:2 * n_w + 5]
        (part_ref, ici_send, ici_recv, d2d_send, d2d_recv, local_sems, cc_send, cc_recv, tab_send,
         tab_recv) = refs[2 * n_w + 5:]
        x, y, c = lax.axis_index("x"), lax.axis_index("y"), lax.axis_index("c")
        chip, dev = 2 * x + y, 4 * x + 2 * y + c
        chips = _other_chips(x, y)
        peers = _all_peers(x, y, c)

        def half(wi, which):
            hr = shards[wi].shape[0] // 2
            return pl.ds(pl.multiple_of(which * hr, 16), hr)

        def over_ici(wi, k, arriving):
            px, py = chips[k]
            slot = 2 * px + py if arriving else chip
            return pltpu.make_async_remote_copy(
                src_ref=srcs[wi].at[half(wi, c)], dst_ref=outs[wi].at[slot, half(wi, c)],
                send_sem=ici_send.at[3 * wi + k], recv_sem=ici_recv.at[3 * wi + k], device_id=(px, py, c),
                device_id_type=pl.DeviceIdType.MESH)

        def to_sibling(wi, k, arriving):
            px, py = chips[k]
            rows = half(wi, 1 - c if arriving else c)
            return pltpu.make_async_remote_copy(
                src_ref=outs[wi].at[2 * px + py, rows], dst_ref=outs[wi].at[2 * px + py, rows],
                send_sem=d2d_send.at[3 * wi + k], recv_sem=d2d_recv.at[3 * wi + k], device_id=(x, y, 1 - c),
                device_id_type=pl.DeviceIdType.MESH)

        def cc_copy(k, peer, slot):
            return pltpu.make_async_remote_copy(
                src_ref=cc_ref, dst_ref=all_ref.at[slot], send_sem=cc_send.at[k], recv_sem=cc_recv.at[k],
                device_id=peer, device_id_type=pl.DeviceIdType.MESH)

        def rows_of(px, py):
            return part_ref.at[pl.ds(pl.multiple_of((4 * px + 2 * py + c) * MOD_ROWS, MOD_ROWS), MOD_ROWS)]

        def tab_copy(k, px, py, slot):
            return pltpu.make_async_remote_copy(
                src_ref=rows_of(px, py), dst_ref=tab_ref.at[slot], send_sem=tab_send.at[k], recv_sem=tab_recv.at[k],
                device_id=(px, py, c), device_id_type=pl.DeviceIdType.MESH)

        local = [pltpu.make_async_copy(srcs[wi], outs[wi].at[chip], local_sems.at[wi]) for wi in range(n_w)]
        for cp in local:
            cp.start()
        pairs = [(wi, k) for wi in range(n_w) for k in range(3)]
        for wi, k in pairs:
            over_ici(wi, k, False).start()
        for k, peer in enumerate(peers):
            cc_copy(k, peer, dev).start()
        all_ref[dev] = cc_ref[...]
        for k, (px, py, pc) in enumerate(peers):
            cc_copy(k, (px, py, pc), 4 * px + 2 * py + pc).wait_recv()
        cv = all_ref[...].reshape(8 * MOD_ROWS, cc.shape[1])
        part_ref[...] = _dot((cv * _sigmoid(cv)).astype(BF16), w_ref[...]) + b_ref[...]
        for k, (px, py) in enumerate(chips):
            tab_copy(k, px, py, chip).start()
        tab_ref[chip] = rows_of(x, y)[...]
        for k, (px, py) in enumerate(chips):
            tab_copy(k, px, py, 2 * px + py).wait_recv()
        for wi, k in pairs:
            over_ici(wi, k, True).wait_recv()
            to_sibling(wi, k, False).start()
        for wi, k in pairs:
            to_sibling(wi, k, True).wait_recv()
        for wi, k in pairs:
            over_ici(wi, k, False).wait_send()
            to_sibling(wi, k, False).wait_send()
        for k, peer in enumerate(peers):
            cc_copy(k, peer, dev).wait_send()
        for k, (px, py) in enumerate(chips):
            tab_copy(k, px, py, chip).wait_send()
        for cp in local:
            cp.wait()

    any_spec = pl.BlockSpec(memory_space=pl.ANY)
    vmem = pl.BlockSpec(memory_space=pltpu.VMEM)
    sems3 = pltpu.SemaphoreType.DMA((3 * n_w,))
    got = pl.pallas_call(
        body, name="first_exchange", in_specs=[any_spec] * n_w + [vmem] * 3, out_specs=[any_spec] * n_w + [vmem] * 2,
        out_shape=_exch_shapes("gather", shards) + [_sds((8,) + cc.shape, F32), _sds((N_CHIPS, MOD_ROWS, n), F32)],
        scratch_shapes=[pltpu.VMEM((8 * MOD_ROWS, n), F32), sems3, sems3, sems3, sems3, pltpu.SemaphoreType.DMA((n_w,)),
                        pltpu.SemaphoreType.DMA((7,)), pltpu.SemaphoreType.DMA((7,)), pltpu.SemaphoreType.DMA((3,)),
                        pltpu.SemaphoreType.DMA((3,))],
        compiler_params=pltpu.CompilerParams(vmem_limit_bytes=V7X_VMEM_LIMIT),
    )(*shards, cc, w, b)
    return got[:n_w], got[n_w], got[n_w + 1]


def _ada_bwd_tp(cc_all, dmods, w, ctx_row):
    d, n = w.shape

    def body(cc_ref, m0, m1, m2, m3, w_ref, dw_ref, db_ref, dctx_ref, stage_ref, all_ref, send_sems, recv_sems):
        x, y, c = lax.axis_index("x"), lax.axis_index("y"), lax.axis_index("c")
        me = 4 * x + 2 * y + c
        dsum = m0[...] + m1[...] + m2[...] + m3[...]
        db_ref[...] = jnp.sum(dsum, axis=0, keepdims=True)
        for j in range(N_CHIPS):
            stage_ref[j] = dsum[:, j * n:(j + 1) * n]

        def copy(k, peer, slot):
            px, py, _ = peer
            return pltpu.make_async_remote_copy(
                src_ref=stage_ref.at[2 * px + py], dst_ref=all_ref.at[slot], send_sem=send_sems.at[k],
                recv_sem=recv_sems.at[k], device_id=peer, device_id_type=pl.DeviceIdType.MESH)

        peers = _all_peers(x, y, c)
        for k, peer in enumerate(peers):
            copy(k, peer, me).start()
        all_ref[me] = stage_ref[2 * x + y]
        for k, (px, py, pc) in enumerate(peers):
            copy(k, (px, py, pc), 4 * px + 2 * py + pc).wait_recv()
        for k, peer in enumerate(peers):
            copy(k, peer, me).wait_send()
        cv = cc_ref[...]
        sig = _sigmoid(cv)
        dmb = all_ref[...].reshape(8 * MOD_ROWS, n).astype(BF16)
        dw_ref[...] = _dot_tn((cv * sig).astype(BF16), dmb)
        dsc = _dot_nt(dmb, w_ref[...])
        dctx = dsc[ctx_row:ctx_row + 1, :]
        for dev in range(1, 8):
            dctx = dctx + dsc[dev * MOD_ROWS + ctx_row:dev * MOD_ROWS + ctx_row + 1, :]
        cx = cv[ctx_row:ctx_row + 1, :]
        sx = sig[ctx_row:ctx_row + 1, :]
        dctx_ref[...] = dctx * (sx * (1.0 + cx * (1.0 - sx))) * jnp.where(c == 0, 1.0, 0.0)

    vmem = pl.BlockSpec(memory_space=pltpu.VMEM)
    return pl.pallas_call(
        body, name="ada_bwd_tp", in_specs=[vmem] * 6, out_specs=[vmem] * 3,
        out_shape=[_sds((d, n), F32), _sds((1, N_MOD * d), F32), _sds((1, d), F32)],
        scratch_shapes=[pltpu.VMEM((N_CHIPS, MOD_ROWS, n), F32), pltpu.VMEM((8, MOD_ROWS, n), F32),
                        pltpu.SemaphoreType.DMA((7,)), pltpu.SemaphoreType.DMA((7,))],
        compiler_params=pltpu.CompilerParams(vmem_limit_bytes=V7X_VMEM_LIMIT),
    )(cc_all, *dmods, w)


def _rope_tables(s, ctx):
    pos = np.arange(s, dtype=np.float32)
    inv = (np.float32(ROPE_BASE) ** (-np.arange(0, QK_ROPE // 2, 2, dtype=np.float32) / np.float32(QK_ROPE // 2)))
    ang_r = np.floor(pos / GRID_W)[:, None] * inv
    ang_c = (pos - GRID_W * np.floor(pos / GRID_W))[:, None] * inv
    ang = np.concatenate([ang_r, ang_r, ang_c, ang_c], axis=-1).astype(np.float32)
    cos, sin = np.cos(ang), np.sin(ang)
    half_b = (np.arange(QK_ROPE) // 8) % 2 == 1
    sin_a = np.where(half_b, sin, 0.0)
    sin_b = np.where(half_b, 0.0, -sin)

    def place(tab, fill):
        full = np.full((s + ctx, HEAD_PAD), fill, np.float32)
        full[:s, QK_NOPE:QK_HEAD] = tab
        return jnp.asarray(full)

    return place(cos, 1.0), place(sin_a, 0.0), place(sin_b, 0.0)


def _pad_last(a, n):
    return jnp.pad(a, [(0, 0)] * (a.ndim - 1) + [(0, n - a.shape[-1])])


def _flat_rows(parts, rows, width):
    flat = jnp.concatenate([p.reshape(-1) for p in parts])
    return jnp.pad(flat, (0, rows * width - flat.shape[0])).reshape(rows, width)


def kernel(x, c, ctx, c_ctx, w_ada, b_ada, norm1_w, ffn1_w1, ffn1_w3, ffn1_w2, norm2_w, w_in, q_a_norm_w, w_uq, kv_a_norm_w, w_ukv, q_norm_w, k_norm_w, v_norm_w, w_s, b_s, w_out, norm3_w, ffn2_w1, ffn2_w3, ffn2_w2, loss_target, m_c_ctx, m_w_ada, m_b_ada, m_norm1_w, m_ffn1_w1, m_ffn1_w3, m_ffn1_w2, m_norm2_w, m_w_in, m_q_a_norm_w, m_w_uq, m_kv_a_norm_w, m_w_ukv, m_q_norm_w, m_k_norm_w, m_v_norm_w, m_w_s, m_b_s, m_w_out, m_norm3_w, m_ffn2_w1, m_ffn2_w3, m_ffn2_w2, v_c_ctx, v_w_ada, v_b_ada, v_norm1_w, v_ffn1_w1, v_ffn1_w3, v_ffn1_w2, v_norm2_w, v_w_in, v_q_a_norm_w, v_w_uq, v_kv_a_norm_w, v_w_ukv, v_q_norm_w, v_k_norm_w, v_v_norm_w, v_w_s, v_b_s, v_w_out, v_norm3_w, v_ffn2_w1, v_ffn2_w3, v_ffn2_w2):
    wts = dict(c_ctx=c_ctx, w_ada=w_ada, b_ada=b_ada, norm1_w=norm1_w, ffn1_w1=ffn1_w1, ffn1_w3=ffn1_w3, ffn1_w2=ffn1_w2,
               norm2_w=norm2_w, w_in=w_in, q_a_norm_w=q_a_norm_w, w_uq=w_uq, kv_a_norm_w=kv_a_norm_w, w_ukv=w_ukv,
               q_norm_w=q_norm_w, k_norm_w=k_norm_w, v_norm_w=v_norm_w, w_s=w_s, b_s=b_s, w_out=w_out, norm3_w=norm3_w,
               ffn2_w1=ffn2_w1, ffn2_w3=ffn2_w3, ffn2_w2=ffn2_w2)
    moms = dict(c_ctx=m_c_ctx, w_ada=m_w_ada, b_ada=m_b_ada, norm1_w=m_norm1_w, ffn1_w1=m_ffn1_w1, ffn1_w3=m_ffn1_w3,
                ffn1_w2=m_ffn1_w2, norm2_w=m_norm2_w, w_in=m_w_in, q_a_norm_w=m_q_a_norm_w, w_uq=m_w_uq,
                kv_a_norm_w=m_kv_a_norm_w, w_ukv=m_w_ukv, q_norm_w=m_q_norm_w, k_norm_w=m_k_norm_w, v_norm_w=m_v_norm_w,
                w_s=m_w_s, b_s=m_b_s, w_out=m_w_out, norm3_w=m_norm3_w, ffn2_w1=m_ffn2_w1, ffn2_w3=m_ffn2_w3,
                ffn2_w2=m_ffn2_w2)
    vars_ = dict(c_ctx=v_c_ctx, w_ada=v_w_ada, b_ada=v_b_ada, norm1_w=v_norm1_w, ffn1_w1=v_ffn1_w1, ffn1_w3=v_ffn1_w3,
                 ffn1_w2=v_ffn1_w2, norm2_w=v_norm2_w, w_in=v_w_in, q_a_norm_w=v_q_a_norm_w, w_uq=v_w_uq,
                 kv_a_norm_w=v_kv_a_norm_w, w_ukv=v_w_ukv, q_norm_w=v_q_norm_w, k_norm_w=v_k_norm_w, v_norm_w=v_v_norm_w,
                 w_s=v_w_s, b_s=v_b_s, w_out=v_w_out, norm3_w=v_norm3_w, ffn2_w1=v_ffn2_w1, ffn2_w3=v_ffn2_w3,
                 ffn2_w2=v_ffn2_w2)

    nb, s, d = x.shape
    nctx = ctx.shape[1]
    t, tc = nb * s, nb * nctx
    t_all = t + tc
    sk = s + nctx
    assert nb + 1 <= MOD_ROWS and d % LANES == 0
    tm = _token_tile(s, nctx)

    def held(n, a_):
        return jnp.swapaxes(a_[0], 0, 1) if n in T_WEIGHTS else a_[0]

    def unheld(n, a_):
        return (jnp.swapaxes(a_, 0, 1) if n in T_WEIGHTS else a_)[None]

    shard = {n: held(n, wts[n]).astype(BF16) for n in SHARDED}
    full = {}

    def unshard(names, blocks):
        for n, g4 in zip(names, blocks):
            _, r_, c_ = g4.shape
            if n in ROW_SHARDED or n in T_WEIGHTS:
                full[n] = g4.reshape(N_CHIPS * r_, c_)
            else:
                full[n] = g4.transpose(1, 0, 2).reshape(r_, N_CHIPS * c_)

    def chip_major(n, g_):
        if n in ROW_SHARDED or n in T_WEIGHTS:
            return g_.reshape(N_CHIPS, g_.shape[0] // N_CHIPS, g_.shape[1]).astype(BF16)
        r_, cols = g_.shape
        return g_.reshape(r_, N_CHIPS, cols // N_CHIPS).transpose(1, 0, 2).astype(BF16)

    cc = jnp.concatenate([c, c_ctx[None, :], jnp.zeros((MOD_ROWS - nb - 1, d), F32)], axis=0)
    n_ada = shard["w_ada"].shape[1]
    assert n_ada % LANES == 0
    my_chip = 2 * lax.axis_index("x") + lax.axis_index("y")
    b_cols = lax.dynamic_slice_in_dim(b_ada, my_chip * n_ada, n_ada, axis=1)
    got, cc_all, table = _first_exchange([shard[n] for n in FIRST_WEIGHTS], cc, shard["w_ada"], b_cols)
    unshard(FIRST_WEIGHTS, got)
    cc_all = cc_all.reshape(8 * MOD_ROWS, d)
    mod = table.transpose(1, 0, 2).reshape(MOD_ROWS, N_MOD, d)
    wsb = w_s[0].astype(BF16)
    wcat = wsb.transpose(1, 0, 2).reshape(CHUNK, GROUPS * CHUNK)
    wcat_t = wsb.transpose(2, 0, 1).reshape(CHUNK, GROUPS * CHUNK)
    bias = jnp.repeat(b_s[0].T, GROUP_DIM, axis=1)
    vnw = v_norm_w.reshape(1, GMLP_W)
    lane = jnp.arange(GMLP_W)
    ones = (lane[:, None] // GROUP_DIM == lane[None, :] // GROUP_DIM).astype(BF16)
    qnw = _pad_last(q_norm_w, HEAD_PAD)
    knw = _pad_last(k_norm_w, HEAD_PAD)
    tabs = _rope_tables(s, nctx)

    x_lat, x_ctx = x.reshape(t, d), ctx.reshape(tc, d)
    (xs1, a1, b1, y1), got = _ffn_fwd(x_lat, x_ctx, mod, norm1_w, full["ffn1_w1"], full["ffn1_w3"], full["ffn1_w2"], 0, s,
                                      nb, tm, "ffn1_fwd", exch=("gather", [shard[n] for n in MIX_WEIGHTS]))
    unshard(MIX_WEIGHTS, got)
    wi = full["w_in"]
    wp = jnp.concatenate([wi[0:KV_LORA], jnp.zeros((QK_NOPE, d), BF16), wi[KV_LORA:KV_LORA + QK_ROPE],
                          jnp.zeros((HEAD_PAD - QK_HEAD, d), BF16), wi[KV_LORA + QK_ROPE:]], axis=0)
    wq = jnp.pad(full["w_uq"].reshape(HEADS, QK_HEAD, Q_LORA), ((0, 0), (0, HEAD_PAD - QK_HEAD), (0, 0)))
    wkv = full["w_ukv"].reshape(KV_LORA, HEADS, QK_NOPE + V_HEAD)
    wk = _pad_last(wkv[:, :, :QK_NOPE].transpose(1, 0, 2), HEAD_PAD)
    wv = wkv[:, :, QK_NOPE:].reshape(KV_LORA, HEADS // 2, 2 * V_HEAD).transpose(1, 0, 2)
    h2, proj = _mixin_fwd(xs1, mod, norm2_w, wp, s, nb, tm)
    prep_w = (wq, wk, wv, kv_a_norm_w, q_a_norm_w, qnw, knw)
    q, k_all, v_all = _prep_fwd(proj, 0, nb, s, 0, sk, 0, None, tabs, *prep_w, tm, True, "prep_fwd")
    k_all, v_all = _prep_fwd(proj, t // tm, nb, nctx, s // tm, sk, s // tm, (k_all, v_all), tabs, *prep_w, tm, False,
                             "prep_ctx_fwd")
    tq = _div_tile(s, 512, tm)
    o, lse, got = _attn_fwd(q, k_all, v_all, tq, exch=("gather", [shard[n] for n in LAST_WEIGHTS]))
    unshard(LAST_WEIGHTS, got)
    sg = _gmlp_fwd(proj, t, wcat, bias, vnw, ones, tm)
    x2, mix = _mixout_fwd(o, sg, xs1, mod, full["w_out"], s, tm)
    (dy, a2, b2, y2, loss_part), _ = _ffn_fwd(x2, None, mod, norm3_w, full["ffn2_w1"], full["ffn2_w3"], full["ffn2_w2"], 6,
                                              s, nb, tm, "ffn2_fwd", target=loss_target.reshape(t, d))
    loss = lax.psum(loss_part[0, 0], ("x", "y", "c"))

    grads, cm, recv = {}, {}, {}

    def scatter_of(names):
        return ("scatter", [cm[n] for n in names])

    (dx2, h3, g2, da2, db2, dyb2, dmod_c, grads["norm3_w"]), _ = _ffn_bwd(
        dy, x2, None, a2, b2, y2, mod, norm3_w, full["ffn2_w1"], full["ffn2_w3"], full["ffn2_w2"], 6, s, nb, tm,
        "ffn2_bwd")
    cm["ffn2_w1"] = chip_major("ffn2_w1", _mm_tn(da2, h3, t, "ffn2_dw1"))
    cm["ffn2_w3"] = chip_major("ffn2_w3", _mm_tn(db2, h3, t, "ffn2_dw3"))
    cm["ffn2_w2"] = chip_major("ffn2_w2", _mm_tn(g2, dyb2, t, "ffn2_dw2"))
    dmix, do, dsg, dmod_b = _mixout_bwd(dx2, mix, mod, full["w_out"], s, tm)
    cm["w_out"] = chip_major("w_out", jnp.concatenate([_mm_tn(o, dmix, t, "wout_dw_attn"),
                                                       _mm_tn(sg, dmix, t, "wout_dw_gmlp")], axis=0))
    dpu, dpv, dws, dbs, dvnw = _gmlp_bwd(proj, dsg, wcat, wcat_t, bias, vnw, ones, tm)
    group = LAST_WEIGHTS + ("w_out",)
    (dq, dk, dv), got = _attn_bwd(q, k_all, v_all, do, o, lse, tq, exch=scatter_of(group))
    recv.update(zip(group, got))
    dp0, dwk_c, dwv_c, dkvaw_c, dknw_c = _prep_bwd(
        proj, t // tm, nb, nctx, s // tm, s // tm, t_all, None, tabs, *prep_w, None, dk, dv, None, tm, "prep_ctx_bwd")
    dp0, dwq, dqaw, dqnw, dwk, dwv, dkvaw, dknw = _prep_bwd(
        proj, 0, nb, s, 0, 0, t_all, dp0, tabs, *prep_w, dq, dk, dv, [dwk_c, dwv_c, dkvaw_c, dknw_c], tm, "prep_bwd")
    dxs1, dmod_a, grads["norm2_w"] = _mixin_bwd(dp0, dpu, dpv, xs1, dx2, mod, norm2_w, wp, s, nb, tm)
    dwp = jnp.concatenate([_mm_tn(dp0, h2, t_all, "win_dw_kvq"), _mm_tn(dpu, h2, t, "win_dw_u"),
                           _mm_tn(dpv, h2, t, "win_dw_v")], axis=0)
    cm["w_in"] = chip_major("w_in", jnp.concatenate(
        [dwp[0:KV_LORA], dwp[KV_LORA + QK_NOPE:KV_LORA + QK_HEAD], dwp[256:]], axis=0))
    cm["w_uq"] = chip_major("w_uq", dwq[:, :, :QK_HEAD].transpose(0, 2, 1).reshape(HEADS * QK_HEAD, Q_LORA))
    cm["w_ukv"] = chip_major("w_ukv", jnp.concatenate(
        [dwk[:, :, :QK_NOPE].transpose(1, 0, 2),
         dwv.transpose(1, 0, 2).reshape(KV_LORA, HEADS, V_HEAD)], axis=2).reshape(KV_LORA, HEADS * (QK_NOPE + V_HEAD)))
    (dx_lat, h1, g1, da1, db1, dyb1, dmod_0, grads["norm1_w"]), _ = _ffn_bwd(
        dxs1, x_lat, x_ctx, a1, b1, y1, mod, norm1_w, full["ffn1_w1"], full["ffn1_w3"], full["ffn1_w2"], 0, s, nb, tm,
        "ffn1_bwd")
    dmods = [m_.reshape(MOD_ROWS, N_MOD * d) for m_ in (dmod_0, dmod_a, dmod_b, dmod_c)]
    dw_ada, grads["b_ada"], dctx = _ada_bwd_tp(cc_all, dmods, shard["w_ada"], nb)
    grads["c_ctx"] = dctx[0]
    grads["q_a_norm_w"], grads["kv_a_norm_w"] = dqaw, dkvaw
    grads["q_norm_w"], grads["k_norm_w"] = dqnw[:, :QK_HEAD], dknw[:, :QK_HEAD]
    grads["v_norm_w"], grads["w_s"], grads["b_s"] = dvnw, dws, dbs[:, 0]
    grad_x = dx_lat.reshape(nb, s, d)
    rows_s = _round_up(-(-sum(wts[n].size for n in SMALL) // d), 16)
    cm["small"] = jnp.broadcast_to(_flat_rows([grads[n] for n in SMALL], rows_s, d), (N_CHIPS, rows_s, d))
    group = ("w_in", "w_uq", "w_ukv", "small")
    dw2, got = _mm_tn(g1, dyb1, t_all, "ffn1_dw2", exch=scatter_of(group))
    recv.update(zip(group, got))
    cm["ffn1_w2"] = chip_major("ffn1_w2", dw2)
    dw1, got = _mm_tn(da1, h1, t_all, "ffn1_dw1", exch=scatter_of(("ffn1_w2",)))
    recv["ffn1_w2"] = got[0]
    cm["ffn1_w1"] = chip_major("ffn1_w1", dw1)
    dw3, got = _mm_tn(db1, h1, t_all, "ffn1_dw3", exch=scatter_of(("ffn1_w1",)))
    recv["ffn1_w1"] = got[0]
    cm["ffn1_w3"] = chip_major("ffn1_w3", dw3)
    stepped = {}
    stepped["w_ada"], got = _adamw([dw_ada], wts["w_ada"][0], moms["w_ada"][0], vars_["w_ada"][0], "adamw_w_ada",
                                   exch=scatter_of(("ffn1_w3",)))
    recv["ffn1_w3"] = got[0]

    reduced = tuple(n for n in SHARDED if n != "w_ada") + ("small",)
    part = {n: _sum_slots(recv[n], "sum_" + n) for n in reduced}
    early = LAST_WEIGHTS + ("w_out",)
    late = tuple(n for n in reduced if n not in early)
    sib = dict(zip(early, _swap_cores([part[n] for n in early], "swap_early")))
    sib.update(zip(late, _swap_cores([part[n] for n in late], "swap_late")))
    for n in reduced[:-1]:
        stepped[n], _ = _adamw([part[n], sib[n]], held(n, wts[n]), held(n, moms[n]), held(n, vars_[n]), "adamw_" + n)
    for n in SHARDED:
        stepped[n] = [unheld(n, a_) for a_ in stepped[n]]
    packed, _ = _adamw([part["small"], sib["small"]], _flat_rows([wts[n] for n in SMALL], rows_s, d),
                       _flat_rows([moms[n] for n in SMALL], rows_s, d), _flat_rows([vars_[n] for n in SMALL], rows_s, d),
                       "adamw_small")
    for n in SMALL:
        stepped[n] = []
    for a_ in packed:
        flat = a_.reshape(-1)
        off = 0
        for n in SMALL:
            stepped[n].append(flat[off:off + wts[n].size].reshape(wts[n].shape))
            off += wts[n].size
    return (loss, grad_x, *[stepped[n][0] for n in WEIGHTS], *[stepped[n][1] for n in WEIGHTS],
            *[stepped[n][2] for n in WEIGHTS], *[stepped[n][3] for n in WEIGHTS])
```

```python
import functools
import math

import jax
import jax.numpy as jnp
import numpy as np
from jax import lax
from jax.experimental import pallas as pl
from jax.experimental.pallas import tpu as pltpu

F32 = jnp.float32
BF16 = jnp.bfloat16

EPS = 1e-6
N_MOD = 9
HEADS = 8
QK_NOPE, QK_ROPE, V_HEAD = 64, 32, 64
QK_HEAD = QK_NOPE + QK_ROPE
HEAD_PAD = 128
LN2 = math.log(2.0)
SOFTMAX_SCALE = QK_HEAD ** -0.5 / LN2
Q_LORA, KV_LORA = 256, 128
GROUPS, GROUP_DIM, CHUNK = 8, 64, 128
GMLP_W = GROUPS * GROUP_DIM
MLA_W = HEADS * V_HEAD
IN_COLS = 1440
PROJ_COLS = 1536
GRID_W = 64
ROPE_BASE = 10000.0
MOD_ROWS = 16
ADAM_LR, ADAM_B1, ADAM_B2, ADAM_EPS, ADAM_WD, ADAM_STEP = 0.001, 0.9, 0.999, 1e-08, 0.01, 10
N_CHIPS = 4
LANES = 128
V7X_VMEM_LIMIT = 56 * 1024 * 1024
GELU_C = math.sqrt(2.0 / math.pi)

SHARDED = ("w_ada", "ffn1_w1", "ffn1_w3", "ffn1_w2", "w_in", "w_uq", "w_ukv", "w_out", "ffn2_w1", "ffn2_w3", "ffn2_w2")
ROW_SHARDED = ("ffn1_w2", "w_out", "ffn2_w2")
T_WEIGHTS = ("ffn1_w1", "ffn1_w3", "ffn2_w1", "ffn2_w3", "w_in", "w_uq")
FIRST_WEIGHTS = ("ffn1_w1", "ffn1_w3", "ffn1_w2")
MIX_WEIGHTS = ("w_in", "w_uq", "w_ukv", "w_out")
LAST_WEIGHTS = ("ffn2_w1", "ffn2_w3", "ffn2_w2")
SMALL = ("c_ctx", "b_ada", "norm1_w", "norm2_w", "q_a_norm_w", "kv_a_norm_w", "q_norm_w", "k_norm_w", "v_norm_w",
         "w_s", "b_s", "norm3_w")
WEIGHTS = ("c_ctx", "w_ada", "b_ada", "norm1_w", "ffn1_w1", "ffn1_w3", "ffn1_w2", "norm2_w", "w_in", "q_a_norm_w",
           "w_uq", "kv_a_norm_w", "w_ukv", "q_norm_w", "k_norm_w", "v_norm_w", "w_s", "b_s", "w_out", "norm3_w",
           "ffn2_w1", "ffn2_w3", "ffn2_w2")


def _round_up(n, m):
    return (n + m - 1) // m * m


def _div_tile(n, target, mult):
    best = None
    for t in range(mult, min(n, target) + 1, mult):
        if n % t == 0:
            best = t
    return n if best is None else best


def _dot(a, b):
    return lax.dot_general(a, b, (((1,), (0,)), ((), ())), preferred_element_type=F32)


def _dot_nt(a, b):
    return lax.dot_general(a, b, (((1,), (1,)), ((), ())), preferred_element_type=F32)


def _dot_tn(a, b):
    return lax.dot_general(a, b, (((0,), (0,)), ((), ())), preferred_element_type=F32)


def _sigmoid(x):
    return 1.0 / (1.0 + jnp.exp(-x))


def _gelu(x):
    return 0.5 * x * (1.0 + jnp.tanh(GELU_C * (x + 0.044715 * x * x * x)))


def _gelu_grad(x):
    t = jnp.tanh(GELU_C * (x + 0.044715 * x * x * x))
    return 0.5 * (1.0 + t) + 0.5 * x * (1.0 - t * t) * (GELU_C * (1.0 + 3 * 0.044715 * x * x))


def _rope3(x, cos, sin_a, sin_b):
    return x * cos + pltpu.roll(x, 8, 2) * sin_a + pltpu.roll(x, HEAD_PAD - 8, 2) * sin_b


def _rope3_t(d, cos, sin_a, sin_b):
    return d * cos + pltpu.roll(d * sin_a, HEAD_PAD - 8, 2) + pltpu.roll(d * sin_b, 8, 2)


def _group_sum(x, ones_ref):
    hi = x.astype(BF16)
    lo = (x - hi.astype(F32)).astype(BF16)
    return _dot(hi, ones_ref[...]) + _dot(lo, ones_ref[...])


def _params(n_axes):
    return pltpu.CompilerParams(dimension_semantics=("arbitrary",) * n_axes, vmem_limit_bytes=V7X_VMEM_LIMIT)


def _whole(shape):
    nd = len(shape)
    return pl.BlockSpec(shape, lambda *_: (0,) * nd, pipeline_mode=pl.Buffered(1))


def _sds(shape, dtype):
    return jax.ShapeDtypeStruct(shape, dtype)


def _token_tile(s, ctx):
    return _div_tile(math.gcd(s, ctx), 256, CHUNK)


def _other_chips(x, y):
    return [(1 - x, y), (x, 1 - y), (1 - x, 1 - y)]


def _exch_copies(kind, srcs, dsts, send_sems, recv_sems, local_sems, with_arrivals):
    x, y, c = lax.axis_index("x"), lax.axis_index("y"), lax.axis_index("c")
    me = 2 * x + y
    local, sends, arrivals = [], [], []
    for w, (src, dst) in enumerate(zip(srcs, dsts)):
        own = src if kind == "gather" else src.at[me]
        local.append(pltpu.make_async_copy(own, dst.at[me], local_sems.at[w]))
        for k, (px, py) in enumerate(_other_chips(x, y)):
            sem = dict(send_sem=send_sems.at[3 * w + k], recv_sem=recv_sems.at[3 * w + k], device_id=(px, py, c),
                       device_id_type=pl.DeviceIdType.MESH)
            out = src if kind == "gather" else src.at[2 * px + py]
            sends.append(pltpu.make_async_remote_copy(src_ref=out, dst_ref=dst.at[me], **sem))
            if with_arrivals:
                arrivals.append(pltpu.make_async_remote_copy(src_ref=own, dst_ref=dst.at[2 * px + py], **sem))
    return local, sends, arrivals


def _exch_start(kind, srcs, dsts, sems):
    local, sends, _ = _exch_copies(kind, srcs, dsts, *sems, with_arrivals=False)
    for cp in local + sends:
        cp.start()


def _exch_wait(kind, srcs, dsts, sems):
    local, sends, arrivals = _exch_copies(kind, srcs, dsts, *sems, with_arrivals=True)
    for cp in arrivals:
        cp.wait_recv()
    for cp in sends:
        cp.wait_send()
    for cp in local:
        cp.wait()


def _exch_scratch(n):
    return [pltpu.SemaphoreType.DMA((3 * n,)), pltpu.SemaphoreType.DMA((3 * n,)), pltpu.SemaphoreType.DMA((n,))]


def _exch_shapes(kind, arrays):
    return [_sds((N_CHIPS,) + a.shape if kind == "gather" else a.shape, a.dtype) for a in arrays]


def _hosted_call(body, name, grid, in_specs, out_specs, out_shape, operands, scratch=(), exch=None):
    n_axes = len(grid)
    if exch is None:
        outs = pl.pallas_call(body, name=name, grid=grid, in_specs=list(in_specs), out_specs=list(out_specs),
                              out_shape=list(out_shape), scratch_shapes=list(scratch),
                              compiler_params=_params(n_axes))(*operands)
        return list(outs), []
    kind, arrays = exch
    n_in, n_out, n_sc, n_ex = len(in_specs), len(out_specs), len(scratch), len(arrays)

    def hosted(*refs):
        cin, ein = refs[:n_in], refs[n_in:n_in + n_ex]
        o0 = n_in + n_ex
        cout, eout = refs[o0:o0 + n_out], refs[o0 + n_out:o0 + n_out + n_ex]
        rest = refs[o0 + n_out + n_ex:]
        csc, sems = rest[:n_sc], rest[n_sc:]
        first = functools.reduce(jnp.logical_and, [pl.program_id(a) == 0 for a in range(n_axes)])
        last = functools.reduce(jnp.logical_and, [pl.program_id(a) == grid[a] - 1 for a in range(n_axes)])

        @pl.when(first)
        def _():
            _exch_start(kind, ein, eout, sems)

        body(*cin, *cout, *csc)

        @pl.when(last)
        def _():
            _exch_wait(kind, ein, eout, sems)

    any_spec = pl.BlockSpec(memory_space=pl.ANY)
    outs = pl.pallas_call(
        hosted, name=name, grid=grid, in_specs=list(in_specs) + [any_spec] * n_ex,
        out_specs=list(out_specs) + [any_spec] * n_ex, out_shape=list(out_shape) + _exch_shapes(kind, arrays),
        scratch_shapes=list(scratch) + _exch_scratch(n_ex), compiler_params=_params(n_axes),
    )(*operands, *arrays)
    return list(outs[:n_out]), list(outs[n_out:])


class _TokenTiles:
    def __init__(self, t, tc, tm):
        self.n_lat, self.n_ctx = t // tm, tc // tm
        self.n_all = self.n_lat + self.n_ctx

    def tile(self, i):
        return (i + self.n_lat) % self.n_all if self.n_ctx else i

    def is_lat(self, i):
        return self.tile(i) < self.n_lat

    def row(self, i):
        return (self.tile(i), 0)

    def lat_row(self, i):
        return (jnp.where(self.is_lat(i), self.tile(i), 0), 0) if self.n_ctx else (i, 0)

    def ctx_row(self, i):
        return (jnp.where(self.is_lat(i), self.n_ctx - 1, self.tile(i) - self.n_lat), 0)


def _ffn_fwd(x_lat, x_ctx, mod, nw, w1, w3, w2, k0, s, nb, tm, name, target=None, exch=None):
    t, d = x_lat.shape
    tc = 0 if x_ctx is None else x_ctx.shape[0]
    f = w1.shape[0]
    tiles = _TokenTiles(t, tc, tm)
    n_x = 2 if tc else 1
    n_t = 0 if target is None else 1
    assert not (tc and n_t)

    def body(*refs):
        x_ref = refs[0]
        t_ref = refs[n_x] if n_t else None
        mod_ref, nw_ref, w1_ref, w3_ref, w2_ref, o_ref, a_ref, b_ref, y_ref = refs[n_x + n_t:n_x + n_t + 9]
        i = pl.program_id(0)
        g = jnp.minimum((tiles.tile(i) * tm) // s, nb)
        shift = mod_ref[g, pl.ds(k0, 1), :]
        scale = mod_ref[g, pl.ds(k0 + 1, 1), :]
        gate = mod_ref[g, pl.ds(k0 + 2, 1), :]
        x = jnp.where(tiles.is_lat(i), x_ref[...], refs[1][...]) if tc else x_ref[...]
        r = lax.rsqrt(jnp.mean(x * x, axis=-1, keepdims=True) + EPS)
        hb = ((x * r * nw_ref[...]) * (1.0 + scale) + shift).astype(BF16)
        a = _dot_nt(hb, w1_ref[...])
        b = _dot_nt(hb, w3_ref[...])
        gb = (a * _sigmoid(a) * b).astype(BF16)
        y = _dot(gb, w2_ref[...])
        out = x + (0.5 * gate) * y
        a_ref[...] = a.astype(BF16)
        b_ref[...] = b.astype(BF16)
        y_ref[...] = y.astype(BF16)
        if n_t:
            loss_ref, acc_ref = refs[-2:]

            @pl.when(i == 0)
            def _():
                acc_ref[...] = jnp.zeros_like(acc_ref)

            e = out - t_ref[...]
            o_ref[...] = e * (1.0 / d)
            acc_ref[...] += jnp.sum(e * e, axis=0, keepdims=True)

            @pl.when(i == tiles.n_all - 1)
            def _():
                loss_ref[...] = (0.5 / d) * jnp.sum(acc_ref[...], axis=-1, keepdims=True)
        else:
            o_ref[...] = out

    td = pl.BlockSpec((tm, d), tiles.row)
    tf = pl.BlockSpec((tm, f), tiles.row)
    return _hosted_call(
        body, name, (tiles.n_all,),
        [pl.BlockSpec((tm, d), tiles.lat_row)] + ([pl.BlockSpec((tm, d), tiles.ctx_row)] if tc else []) + [td] * n_t
        + [_whole(mod.shape), _whole(nw.shape), _whole(w1.shape), _whole(w3.shape), _whole(w2.shape)],
        [td, tf, tf, td] + [pl.BlockSpec((1, 1), lambda i: (0, 0))] * n_t,
        [_sds((t + tc, d), F32), _sds((t + tc, f), BF16), _sds((t + tc, f), BF16), _sds((t + tc, d), BF16)]
        + [_sds((1, 1), F32)] * n_t,
        (x_lat,) + ((x_ctx,) if tc else ()) + ((target,) if n_t else ()) + (mod, nw, w1, w3, w2),
        scratch=[pltpu.VMEM((1, d), F32)] * n_t, exch=exch)


def _ffn_bwd(dout, x_lat, x_ctx, a, b, y, mod, nw, w1, w3, w2, k0, s, nb, tm, name, exch=None):
    t, d = x_lat.shape
    tc = 0 if x_ctx is None else x_ctx.shape[0]
    f = w1.shape[0]
    nch = 2 if (f // 2) % LANES == 0 and f % 2 == 0 else 1
    fc = f // nch
    tiles = _TokenTiles(t, tc, tm)
    n_x = 2 if tc else 1

    def body(*refs):
        do_ref, x_ref = refs[0], refs[1]
        (a_ref, b_ref, y_ref, mod_ref, nw_ref, w1_ref, w3_ref, w2_ref,
         dx_ref, h_ref, g_ref, da_ref, db_ref, dy_ref, dmod_ref, dnw_ref) = refs[1 + n_x:]
        i = pl.program_id(0)

        @pl.when(i == 0)
        def _():
            dmod_ref[...] = jnp.zeros_like(dmod_ref)
            dnw_ref[...] = jnp.zeros_like(dnw_ref)

        g = jnp.minimum((tiles.tile(i) * tm) // s, nb)
        shift = mod_ref[g, pl.ds(k0, 1), :]
        scale = mod_ref[g, pl.ds(k0 + 1, 1), :]
        gate = mod_ref[g, pl.ds(k0 + 2, 1), :]
        x = jnp.where(tiles.is_lat(i), x_ref[...], refs[2][...]) if tc else x_ref[...]
        dout_v = do_ref[...]
        r = lax.rsqrt(jnp.mean(x * x, axis=-1, keepdims=True) + EPS)
        xh = x * r
        n = xh * nw_ref[...]
        h_ref[...] = (n * (1.0 + scale) + shift).astype(BF16)
        dyb = ((0.5 * gate) * dout_v).astype(BF16)
        dy_ref[...] = dyb
        dmod_ref[g, pl.ds(k0 + 2, 1), :] += 0.5 * jnp.sum(dout_v * y_ref[...].astype(F32), axis=0, keepdims=True)
        dh = jnp.zeros((tm, d), F32)
        for c in range(nch):
            sl = slice(c * fc, (c + 1) * fc)
            dg = _dot_nt(dyb, w2_ref[sl, :])
            av = a_ref[:, sl].astype(F32)
            bv = b_ref[:, sl].astype(F32)
            sig = _sigmoid(av)
            sa = av * sig
            g_ref[:, sl] = (sa * bv).astype(BF16)
            dab = (dg * bv * (sig * (1.0 + av * (1.0 - sig)))).astype(BF16)
            dbb = (dg * sa).astype(BF16)
            da_ref[:, sl] = dab
            db_ref[:, sl] = dbb
            dh = dh + _dot(dab, w1_ref[sl, :]) + _dot(dbb, w3_ref[sl, :])
        dmod_ref[g, pl.ds(k0, 1), :] += jnp.sum(dh, axis=0, keepdims=True)
        dmod_ref[g, pl.ds(k0 + 1, 1), :] += jnp.sum(dh * n, axis=0, keepdims=True)
        dn = dh * (1.0 + scale)
        dnw_ref[...] += jnp.sum(dn * xh, axis=0, keepdims=True)
        dxh = dn * nw_ref[...]
        dx_ref[...] = dout_v + r * (dxh - xh * jnp.mean(dxh * xh, axis=-1, keepdims=True))

    td = pl.BlockSpec((tm, d), tiles.row)
    tf = pl.BlockSpec((tm, f), tiles.row)
    lat = pl.BlockSpec((tm, d), tiles.lat_row)
    ta = t + tc
    return _hosted_call(
        body, name, (tiles.n_all,),
        [td, lat] + ([pl.BlockSpec((tm, d), tiles.ctx_row)] if tc else [])
        + [tf, tf, td, _whole(mod.shape), _whole(nw.shape), _whole(w1.shape), _whole(w3.shape), _whole(w2.shape)],
        [lat, td, tf, tf, tf, td, pl.BlockSpec(mod.shape, lambda i: (0, 0, 0)), pl.BlockSpec((1, d), lambda i: (0, 0))],
        [_sds((t, d), F32), _sds((ta, d), BF16), _sds((ta, f), BF16), _sds((ta, f), BF16), _sds((ta, f), BF16),
         _sds((ta, d), BF16), _sds(mod.shape, F32), _sds((1, d), F32)],
        (dout, x_lat) + ((x_ctx,) if tc else ()) + (a, b, y, mod, nw, w1, w3, w2), exch=exch)


def _mm_tn(a, b, rows, name, exch=None):
    m = a.shape[1]
    n = b.shape[1]
    bm = _div_tile(m, 1408, LANES)
    bn = _div_tile(n, 1408, LANES)
    bk = _div_tile(rows, 2304, LANES)
    nk = rows // bk

    def body(a_ref, b_ref, o_ref, acc_ref):
        k = pl.program_id(2)

        @pl.when(k == 0)
        def _():
            acc_ref[...] = jnp.zeros_like(acc_ref)

        acc_ref[...] += _dot_tn(a_ref[...], b_ref[...])

        @pl.when(k == nk - 1)
        def _():
            o_ref[...] = acc_ref[...].astype(BF16)

    (out,), got = _hosted_call(
        body, name, (m // bm, n // bn, nk),
        [pl.BlockSpec((bk, bm), lambda i, j, k: (k, i)), pl.BlockSpec((bk, bn), lambda i, j, k: (k, j))],
        [pl.BlockSpec((bm, bn), lambda i, j, k: (i, j))], [_sds((m, n), BF16)], (a, b),
        scratch=[pltpu.VMEM((bm, bn), F32)], exch=exch)
    return out if exch is None else (out, got)


def _mixin_fwd(xs, mod, nw, wp, s, nb, tm):
    t, d = xs.shape

    def body(x_ref, mod_ref, nw_ref, wp_ref, h_ref, p_ref):
        g = jnp.minimum((pl.program_id(0) * tm) // s, nb)
        shift = mod_ref[g, pl.ds(3, 1), :]
        scale = mod_ref[g, pl.ds(4, 1), :]
        x = x_ref[...]
        r = lax.rsqrt(jnp.mean(x * x, axis=-1, keepdims=True) + EPS)
        hb = ((x * r * nw_ref[...]) * (1.0 + scale) + shift).astype(BF16)
        h_ref[...] = hb
        p_ref[...] = _dot_nt(hb, wp_ref[...]).astype(BF16)

    row = lambda i: (i, 0)
    return pl.pallas_call(
        body, name="mixin_fwd", grid=(t // tm,),
        in_specs=[pl.BlockSpec((tm, d), row), _whole(mod.shape), _whole(nw.shape), _whole(wp.shape)],
        out_specs=[pl.BlockSpec((tm, d), row), pl.BlockSpec((tm, PROJ_COLS), row)],
        out_shape=[_sds((t, d), BF16), _sds((t, PROJ_COLS), BF16)], compiler_params=_params(1),
    )(xs, mod, nw, wp)


def _mixin_bwd(dp0, dpu, dpv, xs, dres, mod, nw, wp, s, nb, tm):
    t_all, d = xs.shape
    nlat = dres.shape[0] // tm

    def body(p0_ref, pu_ref, pv_ref, x_ref, dr_ref, mod_ref, nw_ref, wp_ref, dx_ref, dmod_ref, dnw_ref):
        i = pl.program_id(0)

        @pl.when(i == 0)
        def _():
            dmod_ref[...] = jnp.zeros_like(dmod_ref)
            dnw_ref[...] = jnp.zeros_like(dnw_ref)

        lat = i < nlat
        g = jnp.minimum((i * tm) // s, nb)
        scale = mod_ref[g, pl.ds(4, 1), :]
        dh = _dot(p0_ref[...], wp_ref[0:512, :])
        extra = _dot(pu_ref[...], wp_ref[512:1024, :]) + _dot(pv_ref[...], wp_ref[1024:1536, :])
        dh = dh + jnp.where(lat, extra, 0.0)
        x = x_ref[...]
        r = lax.rsqrt(jnp.mean(x * x, axis=-1, keepdims=True) + EPS)
        xh = x * r
        n = xh * nw_ref[...]
        dmod_ref[g, pl.ds(3, 1), :] += jnp.sum(dh, axis=0, keepdims=True)
        dmod_ref[g, pl.ds(4, 1), :] += jnp.sum(dh * n, axis=0, keepdims=True)
        dn = dh * (1.0 + scale)
        dnw_ref[...] += jnp.sum(dn * xh, axis=0, keepdims=True)
        dxh = dn * nw_ref[...]
        dx_ref[...] = jnp.where(lat, dr_ref[...], 0.0) + r * (dxh - xh * jnp.mean(dxh * xh, axis=-1, keepdims=True))

    row = lambda i: (i, 0)
    lrow = lambda i: (jnp.minimum(i, nlat - 1), 0)
    return pl.pallas_call(
        body, name="mixin_bwd", grid=(t_all // tm,),
        in_specs=[pl.BlockSpec((tm, 512), row), pl.BlockSpec((tm, 512), lrow), pl.BlockSpec((tm, 512), lrow),
                  pl.BlockSpec((tm, d), row), pl.BlockSpec((tm, d), lrow), _whole(mod.shape), _whole(nw.shape),
                  _whole(wp.shape)],
        out_specs=[pl.BlockSpec((tm, d), row), pl.BlockSpec(mod.shape, lambda i: (0, 0, 0)),
                   pl.BlockSpec((1, d), lambda i: (0, 0))],
        out_shape=[_sds((t_all, d), F32), _sds(mod.shape, F32), _sds((1, d), F32)], compiler_params=_params(1),
    )(dp0, dpu, dpv, xs, dres, mod, nw, wp)


def _prep_fwd(proj, row0, nb, s, pos0, sk, key0, into, tabs, wq, wk, wv, kvaw, qaw, qnw, knw, tm, with_q, name):
    nblk = s // tm
    n_into = 0 if into is None else 2

    def body(p_ref, cos_ref, sa_ref, sb_ref, wq_ref, wk_ref, wv_ref, kvaw_ref, qaw_ref, qnw_ref, knw_ref, *rest):
        outs, heads_ref = rest[n_into:-1], rest[-1]
        q_ref, k_ref, v_ref = outs if with_q else (None,) + outs
        cos, sin_a, sin_b = cos_ref[...][None], sa_ref[...][None], sb_ref[...][None]

        def normed_roped(w_ref, src, extra, nw_ref, o_ref, post):
            for h in range(HEADS):
                heads_ref[h] = _dot_nt(src, w_ref[h]) if extra is None else _dot(src, w_ref[h])
            xp = heads_ref[...] if extra is None else heads_ref[...] + extra[None]
            r = lax.rsqrt(jnp.sum(xp * xp, axis=-1, keepdims=True) * (1.0 / QK_HEAD) + EPS)
            o_ref[...] = _rope3(xp * r * (nw_ref[...] * post)[None], cos, sin_a, sin_b).astype(BF16)

        ckv = p_ref[:, 0:128].astype(F32)
        rkv = lax.rsqrt(jnp.mean(ckv * ckv, axis=-1, keepdims=True) + EPS)
        ckvb = (ckv * rkv * kvaw_ref[...]).astype(BF16)
        normed_roped(wk_ref, ckvb, p_ref[:, 128:256].astype(F32), knw_ref, k_ref, 1.0)
        for j in range(HEADS // 2):
            v_ref[j] = _dot(ckvb, wv_ref[j]).astype(BF16)
        if with_q:
            cq = p_ref[:, 256:512].astype(F32)
            rq = lax.rsqrt(jnp.mean(cq * cq, axis=-1, keepdims=True) + EPS)
            normed_roped(wq_ref, (cq * rq * qaw_ref[...]).astype(BF16), None, qnw_ref, q_ref, SOFTMAX_SCALE)

    tab = pl.BlockSpec((tm, HEAD_PAD), lambda i: (pos0 + i % nblk, 0))
    qspec = pl.BlockSpec((None, HEADS, tm, HEAD_PAD), lambda i: (i // nblk, 0, i % nblk, 0))
    kspec = pl.BlockSpec((None, HEADS, tm, HEAD_PAD), lambda i: (i // nblk, 0, key0 + i % nblk, 0))
    vspec = pl.BlockSpec((None, HEADS // 2, tm, HEAD_PAD), lambda i: (i // nblk, 0, key0 + i % nblk, 0))
    qshape = _sds((nb, HEADS, s, HEAD_PAD), BF16)
    kshape = _sds((nb, HEADS, sk, HEAD_PAD), BF16)
    vshape = _sds((nb, HEADS // 2, sk, HEAD_PAD), BF16)
    n_q = 1 if with_q else 0
    return pl.pallas_call(
        body, name=name, grid=(nb * nblk,),
        in_specs=[pl.BlockSpec((tm, 512), lambda i: (row0 + i, 0)), tab, tab, tab, _whole(wq.shape), _whole(wk.shape),
                  _whole(wv.shape), _whole(kvaw.shape), _whole(qaw.shape), _whole(qnw.shape), _whole(knw.shape)]
        + [pl.BlockSpec(memory_space=pl.ANY)] * n_into,
        out_specs=([qspec] if with_q else []) + [kspec, vspec],
        out_shape=([qshape] if with_q else []) + [kshape, vshape],
        scratch_shapes=[pltpu.VMEM((HEADS, tm, HEAD_PAD), F32)],
        input_output_aliases={11: n_q, 12: n_q + 1} if n_into else {}, compiler_params=_params(1),
    )(proj, *tabs, wq, wk, wv, kvaw, qaw, qnw, knw, *(into or ()))


def _prep_bwd(proj, row0, nb, s, pos0, key0, dp_rows, dp_into, tabs, wq, wk, wv, kvaw, qaw, qnw, knw, dq, dk, dv, init, tm,
              name):
    nblk = s // tm
    with_q = dq is not None
    n_init = 0 if init is None else len(init)
    n_into = 0 if dp_into is None else 1

    def body(*refs):
        p_ref, cos_ref, sa_ref, sb_ref, wq_ref, wk_ref, wv_ref, kvaw_ref, qaw_ref, qnw_ref, knw_ref = refs[:11]
        rest = list(refs[11:])
        dq_ref = rest.pop(0) if with_q else None
        dk_ref, dv_ref = rest.pop(0), rest.pop(0)
        init_refs = [rest.pop(0) for _ in range(n_init)]
        if n_into:
            rest.pop(0)
        dp_ref = rest.pop(0)
        if with_q:
            dwq_ref, dqaw_ref, dqnw_ref = rest.pop(0), rest.pop(0), rest.pop(0)
        dwk_ref, dwv_ref, dkvaw_ref, dknw_ref, heads_ref, dhb_ref, dkr_ref = rest
        accs = [dwk_ref, dwv_ref, dkvaw_ref, dknw_ref]

        @pl.when(pl.program_id(0) == 0)
        def _():
            for k, acc in enumerate(accs):
                acc[...] = init_refs[k][...] if n_init else jnp.zeros_like(acc)
            if with_q:
                dwq_ref[...] = jnp.zeros_like(dwq_ref)
                dqaw_ref[...] = jnp.zeros_like(dqaw_ref)
                dqnw_ref[...] = jnp.zeros_like(dqnw_ref)

        cos, sin_a, sin_b = cos_ref[...][None], sa_ref[...][None], sb_ref[...][None]
        lane = lax.broadcasted_iota(jnp.int32, (tm, HEAD_PAD), 1)
        rope_lanes = (lane >= QK_NOPE) & (lane < QK_HEAD)

        def heads_bwd(w_ref, src, extra, nw_ref, d_ref, dnw_ref, dw_ref, post):
            w_t = extra is None
            for h in range(HEADS):
                heads_ref[h] = _dot_nt(src, w_ref[h]) if w_t else _dot(src, w_ref[h])
            xp = heads_ref[...] if extra is None else heads_ref[...] + extra[None]
            r = lax.rsqrt(jnp.sum(xp * xp, axis=-1, keepdims=True) * (1.0 / QK_HEAD) + EPS)
            xh = xp * r
            dn = _rope3_t(d_ref[...], cos, sin_a, sin_b)
            dnw_ref[...] += post * jnp.sum(jnp.sum(dn * xh, axis=0), axis=0, keepdims=True)
            dxh = dn * (nw_ref[...] * post)[None]
            dxp = r * (dxh - xh * (jnp.sum(dxh * xh, axis=-1, keepdims=True) * (1.0 / QK_HEAD)))
            dhb_ref[...] = dxp.astype(BF16)
            dsrc = jnp.zeros((tm, src.shape[1]), F32)
            for h in range(HEADS):
                dsrc = dsrc + (_dot(dhb_ref[h], w_ref[h]) if w_t else _dot_nt(dhb_ref[h], w_ref[h]))
                dw_ref[h] += _dot_tn(src, dhb_ref[h])
            return dsrc, jnp.sum(dxp, axis=0)

        ckv = p_ref[:, 0:128].astype(F32)
        rkv = lax.rsqrt(jnp.mean(ckv * ckv, axis=-1, keepdims=True) + EPS)
        ckvh = ckv * rkv
        ckvb = (ckvh * kvaw_ref[...]).astype(BF16)
        for h in range(HEADS):
            dkr_ref[h] = dk_ref[h].T
        dckv, dkp_sum = heads_bwd(wk_ref, ckvb, p_ref[:, 128:256].astype(F32), knw_ref, dkr_ref, dknw_ref, dwk_ref,
                                  1.0)
        for j in range(HEADS // 2):
            dvb = dv_ref[j].T.astype(BF16)
            dckv = dckv + _dot_nt(dvb, wv_ref[j])
            dwv_ref[j] += _dot_tn(ckvb, dvb)
        dkvaw_ref[...] += jnp.sum(dckv * ckvh, axis=0, keepdims=True)
        dch = dckv * kvaw_ref[...]
        dp_ref[:, 0:128] = (rkv * (dch - ckvh * jnp.mean(dch * ckvh, axis=-1, keepdims=True))).astype(BF16)
        dp_ref[:, 128:256] = jnp.where(rope_lanes, dkp_sum, 0.0).astype(BF16)
        if with_q:
            cq = p_ref[:, 256:512].astype(F32)
            rq = lax.rsqrt(jnp.mean(cq * cq, axis=-1, keepdims=True) + EPS)
            cqh = cq * rq
            cqb = (cqh * qaw_ref[...]).astype(BF16)
            dcq, _ = heads_bwd(wq_ref, cqb, None, qnw_ref, dq_ref, dqnw_ref, dwq_ref, SOFTMAX_SCALE)
            dqaw_ref[...] += jnp.sum(dcq * cqh, axis=0, keepdims=True)
            dqc = dcq * qaw_ref[...]
            dp_ref[:, 256:512] = (rq * (dqc - cqh * jnp.mean(dqc * cqh, axis=-1, keepdims=True))).astype(BF16)
        else:
            dp_ref[:, 256:512] = jnp.zeros((tm, Q_LORA), BF16)

    tab = pl.BlockSpec((tm, HEAD_PAD), lambda i: (pos0 + i % nblk, 0))
    qspec = pl.BlockSpec((None, HEADS, tm, HEAD_PAD), lambda i: (i // nblk, 0, i % nblk, 0))
    kspec = pl.BlockSpec((None, HEADS, HEAD_PAD, tm), lambda i: (i // nblk, 0, 0, key0 + i % nblk))
    vspec = pl.BlockSpec((None, HEADS // 2, HEAD_PAD, tm), lambda i: (i // nblk, 0, 0, key0 + i % nblk))

    def acc_spec(shape):
        nd = len(shape)
        return pl.BlockSpec(shape, lambda i: (0,) * nd)

    acc_shapes = [(HEADS, KV_LORA, HEAD_PAD), (HEADS // 2, KV_LORA, HEAD_PAD), (1, KV_LORA), (1, HEAD_PAD)]
    q_shapes = [(HEADS, Q_LORA, HEAD_PAD), (1, Q_LORA), (1, HEAD_PAD)] if with_q else []
    out_shapes = [(dp_rows, 512)] + q_shapes + acc_shapes
    n_before = 11 + (1 if with_q else 0) + 2 + n_init
    return pl.pallas_call(
        body, name=name, grid=(nb * nblk,),
        in_specs=[pl.BlockSpec((tm, 512), lambda i: (row0 + i, 0)), tab, tab, tab, _whole(wq.shape), _whole(wk.shape),
                  _whole(wv.shape), _whole(kvaw.shape), _whole(qaw.shape), _whole(qnw.shape), _whole(knw.shape)]
        + ([qspec] if with_q else []) + [kspec, vspec] + [_whole(a.shape) for a in (init or [])]
        + [pl.BlockSpec(memory_space=pl.ANY)] * n_into,
        out_specs=[pl.BlockSpec((tm, 512), lambda i: (row0 + i, 0))] + [acc_spec(sh) for sh in q_shapes + acc_shapes],
        out_shape=[_sds(out_shapes[0], BF16)] + [_sds(sh, F32) for sh in out_shapes[1:]],
        scratch_shapes=[pltpu.VMEM((HEADS, tm, HEAD_PAD), F32), pltpu.VMEM((HEADS, tm, HEAD_PAD), BF16),
                        pltpu.VMEM((HEADS, tm, HEAD_PAD), F32)],
        input_output_aliases={n_before: 0} if n_into else {}, compiler_params=_params(1),
    )(proj, *tabs, wq, wk, wv, kvaw, qaw, qnw, knw, *([dq] if with_q else []), dk, dv, *(init or []),
      *([dp_into] if n_into else []))


def _attn_fwd(q, k, v, tq, exch=None):
    nb, _, s, _ = q.shape
    sk = k.shape[2]
    nq = s // tq

    def body(q_ref, k_ref, v_ref, o_ref, lse_ref, vext_ref):
        @pl.when(pl.program_id(2) == 0)
        def _():
            vext_ref[:, 0:HEAD_PAD] = v_ref[...]
            vext_ref[:, HEAD_PAD:2 * HEAD_PAD] = jnp.ones((sk, HEAD_PAD), BF16)

        lane = lax.broadcasted_iota(jnp.int32, (tq, HEAD_PAD), 1)
        outs = []
        for hh in range(2):
            sc = _dot_nt(q_ref[hh], k_ref[hh])
            m = jnp.max(sc, axis=-1, keepdims=True)
            pv = _dot(jnp.exp2(sc - m).astype(BF16), vext_ref[...])
            l = pv[:, HEAD_PAD:HEAD_PAD + 1]
            outs.append(pv[:, 0:HEAD_PAD] / l)
            lse_ref[hh] = m + jnp.log2(l)
        o_ref[...] = jnp.where(lane < V_HEAD, outs[0], outs[1]).astype(BF16)

    (o, lse), got = _hosted_call(
        body, "attn_fwd", (nb, HEADS // 2, nq),
        [pl.BlockSpec((None, 2, tq, HEAD_PAD), lambda b, j, i: (b, j, i, 0)),
         pl.BlockSpec((None, 2, sk, HEAD_PAD), lambda b, j, i: (b, j, 0, 0)),
         pl.BlockSpec((None, None, sk, HEAD_PAD), lambda b, j, i: (b, j, 0, 0))],
        [pl.BlockSpec((tq, HEAD_PAD), lambda b, j, i: (b * nq + i, j)),
         pl.BlockSpec((None, 2, tq, 1), lambda b, j, i: (b, j, i, 0))],
        [_sds((nb * s, MLA_W), BF16), _sds((nb, HEADS, s, 1), F32)], (q, k, v),
        scratch=[pltpu.VMEM((sk, 2 * HEAD_PAD), BF16)], exch=exch)
    return o, lse, got


def _attn_bwd(q, k, v, do, o, lse, tq, exch=None):
    nb, _, s, _ = q.shape
    sk = k.shape[2]
    nq = s // tq

    def body(q_ref, k_ref, v_ref, do_ref, o_ref, lse_ref, dq_ref, dkt_ref, dvt_ref):
        @pl.when(pl.program_id(2) == 0)
        def _():
            dkt_ref[...] = jnp.zeros_like(dkt_ref)
            dvt_ref[...] = jnp.zeros_like(dvt_ref)

        lane = lax.broadcasted_iota(jnp.int32, (tq, HEAD_PAD), 1)
        dov = do_ref[...]
        prod = dov.astype(F32) * o_ref[...].astype(F32)
        for hh in range(2):
            mine = (lane < V_HEAD) if hh == 0 else (lane >= V_HEAD)
            doh = jnp.where(mine, dov, jnp.zeros_like(dov))
            delta = jnp.sum(jnp.where(mine, prod, 0.0), axis=-1, keepdims=True)
            qh = q_ref[hh]
            q_ln2 = (qh.astype(F32) * LN2).astype(BF16)
            kv = k_ref[hh]
            p = jnp.exp2(_dot_nt(qh, kv) - lse_ref[hh])
            u = (p * (_dot_nt(doh, v_ref[...]) - delta)).astype(BF16)
            dq_ref[hh] = _dot(u, kv) * LN2
            dkt_ref[hh] += _dot_tn(q_ln2, u)
            dvt_ref[...] += _dot_tn(doh, p.astype(BF16))

    qspec = pl.BlockSpec((None, 2, tq, HEAD_PAD), lambda b, j, i: (b, j, i, 0))
    kspec = pl.BlockSpec((None, 2, sk, HEAD_PAD), lambda b, j, i: (b, j, 0, 0))
    vspec = pl.BlockSpec((None, None, sk, HEAD_PAD), lambda b, j, i: (b, j, 0, 0))
    ospec = pl.BlockSpec((tq, HEAD_PAD), lambda b, j, i: (b * nq + i, j))
    return _hosted_call(
        body, "attn_bwd", (nb, HEADS // 2, nq),
        [qspec, kspec, vspec, ospec, ospec, pl.BlockSpec((None, 2, tq, 1), lambda b, j, i: (b, j, i, 0))],
        [qspec, pl.BlockSpec((None, 2, HEAD_PAD, sk), lambda b, j, i: (b, j, 0, 0)),
         pl.BlockSpec((None, None, HEAD_PAD, sk), lambda b, j, i: (b, j, 0, 0))],
        [_sds(q.shape, F32), _sds((nb, HEADS, HEAD_PAD, sk), F32), _sds((nb, HEADS // 2, HEAD_PAD, sk), F32)],
        (q, k, v, do, o, lse), exch=exch)


def _group_masks(rows):
    lane = lax.broadcasted_iota(jnp.int32, (rows, GMLP_W), 1)
    return [(lane >= g * GROUP_DIM) & (lane < (g + 1) * GROUP_DIM) for g in range(GROUPS)]


def _gmlp_fwd(proj, t, wcat, bias, vnw, ones, tm):
    def body(u_ref, v_ref, wcat_ref, bias_ref, vnw_ref, ones_ref, o_ref):
        masks = _group_masks(CHUNK)
        gv = _gelu(v_ref[...].astype(F32))
        rv = lax.rsqrt(_group_sum(gv * gv, ones_ref) * (1.0 / GROUP_DIM) + EPS)
        vnb = (gv * rv * vnw_ref[...]).astype(BF16)
        for c in range(tm // CHUNK):
            rows = slice(c * CHUNK, (c + 1) * CHUNK)
            vc = vnb[rows]
            stack = jnp.concatenate([jnp.where(m, vc, jnp.zeros_like(vc)) for m in masks], axis=0)
            sp = _dot(wcat_ref[...], stack) + bias_ref[...]
            o_ref[rows, :] = (_gelu(u_ref[rows, :].astype(F32)) * sp).astype(BF16)

    return pl.pallas_call(
        body, name="gmlp_fwd", grid=(t // tm,),
        in_specs=[pl.BlockSpec((tm, GMLP_W), lambda i: (i, 1)), pl.BlockSpec((tm, GMLP_W), lambda i: (i, 2)),
                  _whole(wcat.shape), _whole(bias.shape), _whole(vnw.shape), _whole(ones.shape)],
        out_specs=pl.BlockSpec((tm, GMLP_W), lambda i: (i, 0)),
        out_shape=_sds((t, GMLP_W), BF16), compiler_params=_params(1),
    )(proj, proj, wcat, bias, vnw, ones)


def _gmlp_bwd(proj, dsg, wcat, wcat_t, bias, vnw, ones, tm):
    t = dsg.shape[0]

    def body(u_ref, v_ref, dsg_ref, wcat_ref, wcatt_ref, bias_ref, vnw_ref, ones_ref,
             du_ref, dv_ref, dws_ref, dbs_ref, dvnw_ref):
        @pl.when(pl.program_id(0) == 0)
        def _():
            dws_ref[...] = jnp.zeros_like(dws_ref)
            dbs_ref[...] = jnp.zeros_like(dbs_ref)
            dvnw_ref[...] = jnp.zeros_like(dvnw_ref)

        masks = _group_masks(CHUNK)
        v = v_ref[...].astype(F32)
        gv = _gelu(v)
        rv = lax.rsqrt(_group_sum(gv * gv, ones_ref) * (1.0 / GROUP_DIM) + EPS)
        xh = gv * rv
        vnb = (xh * vnw_ref[...]).astype(BF16)
        dvn_parts = []
        for c in range(tm // CHUNK):
            rows = slice(c * CHUNK, (c + 1) * CHUNK)
            vc = vnb[rows]
            stack = jnp.concatenate([jnp.where(m, vc, jnp.zeros_like(vc)) for m in masks], axis=0)
            sp = _dot(wcat_ref[...], stack) + bias_ref[...]
            u = u_ref[rows, :].astype(F32)
            dsg_c = dsg_ref[rows, :]
            du_ref[rows, :] = (dsg_c * sp * _gelu_grad(u)).astype(BF16)
            ds = dsg_c * _gelu(u)
            dstack = jnp.concatenate([jnp.where(m, ds, 0.0) for m in masks], axis=0)
            dbs_ref[...] += jnp.broadcast_to(jnp.sum(dstack, axis=-1, keepdims=True), dbs_ref.shape)
            dstb = dstack.astype(BF16)
            dvn_parts.append(_dot(wcatt_ref[...], dstb))
            dws_ref[...] += _dot_nt(dstb, vc)
        dvn = jnp.concatenate(dvn_parts, axis=0) if len(dvn_parts) > 1 else dvn_parts[0]
        dvnw_ref[...] += jnp.sum(dvn * xh, axis=0, keepdims=True)
        dxh = dvn * vnw_ref[...]
        gm = _group_sum(dxh * xh, ones_ref) * (1.0 / GROUP_DIM)
        dv_ref[...] = (rv * (dxh - xh * gm) * _gelu_grad(v)).astype(BF16)

    row = pl.BlockSpec((tm, GMLP_W), lambda i: (i, 0))
    return pl.pallas_call(
        body, name="gmlp_bwd", grid=(t // tm,),
        in_specs=[pl.BlockSpec((tm, GMLP_W), lambda i: (i, 1)), pl.BlockSpec((tm, GMLP_W), lambda i: (i, 2)), row,
                  _whole(wcat.shape), _whole(wcat_t.shape), _whole(bias.shape), _whole(vnw.shape), _whole(ones.shape)],
        out_specs=[row, row, pl.BlockSpec((GROUPS * CHUNK, CHUNK), lambda i: (0, 0)),
                   pl.BlockSpec((GROUPS * CHUNK, CHUNK), lambda i: (0, 0)), pl.BlockSpec((1, GMLP_W), lambda i: (0, 0))],
        out_shape=[_sds((t, GMLP_W), BF16), _sds((t, GMLP_W), BF16), _sds((GROUPS * CHUNK, CHUNK), F32),
                   _sds((GROUPS * CHUNK, CHUNK), F32), _sds((1, GMLP_W), F32)],
        compiler_params=_params(1),
    )(proj, proj, dsg, wcat, wcat_t, bias, vnw, ones)


def _mixout_fwd(o, sg, xs, mod, wout, s, tm):
    t = o.shape[0]
    d = xs.shape[1]

    def body(o_ref, sg_ref, x_ref, mod_ref, w_ref, x2_ref, mix_ref):
        g = (pl.program_id(0) * tm) // s
        gate = mod_ref[g, pl.ds(5, 1), :]
        mix = _dot(o_ref[...], w_ref[0:MLA_W, :]) + _dot(sg_ref[...], w_ref[MLA_W:MLA_W + GMLP_W, :])
        x2_ref[...] = x_ref[...] + gate * mix
        mix_ref[...] = mix.astype(BF16)

    row = lambda i: (i, 0)
    return pl.pallas_call(
        body, name="mixout_fwd", grid=(t // tm,),
        in_specs=[pl.BlockSpec((tm, MLA_W), row), pl.BlockSpec((tm, GMLP_W), row), pl.BlockSpec((tm, d), row),
                  _whole(mod.shape), _whole(wout.shape)],
        out_specs=[pl.BlockSpec((tm, d), row), pl.BlockSpec((tm, d), row)],
        out_shape=[_sds((t, d), F32), _sds((t, d), BF16)], compiler_params=_params(1),
    )(o, sg, xs, mod, wout)


def _mixout_bwd(dx2, mix, mod, wout, s, tm):
    t, d = dx2.shape

    def body(dx_ref, mix_ref, mod_ref, w_ref, dmix_ref, do_ref, dsg_ref, dmod_ref):
        i = pl.program_id(0)

        @pl.when(i == 0)
        def _():
            dmod_ref[...] = jnp.zeros_like(dmod_ref)

        g = (i * tm) // s
        gate = mod_ref[g, pl.ds(5, 1), :]
        dx = dx_ref[...]
        dmod_ref[g, pl.ds(5, 1), :] += jnp.sum(dx * mix_ref[...].astype(F32), axis=0, keepdims=True)
        dmb = (gate * dx).astype(BF16)
        dmix_ref[...] = dmb
        do_ref[...] = _dot_nt(dmb, w_ref[0:MLA_W, :]).astype(BF16)
        dsg_ref[...] = _dot_nt(dmb, w_ref[MLA_W:MLA_W + GMLP_W, :])

    row = lambda i: (i, 0)
    return pl.pallas_call(
        body, name="mixout_bwd", grid=(t // tm,),
        in_specs=[pl.BlockSpec((tm, d), row), pl.BlockSpec((tm, d), row), _whole(mod.shape), _whole(wout.shape)],
        out_specs=[pl.BlockSpec((tm, d), row), pl.BlockSpec((tm, MLA_W), row), pl.BlockSpec((tm, GMLP_W), row),
                   pl.BlockSpec(mod.shape, lambda i: (0, 0, 0))],
        out_shape=[_sds((t, d), BF16), _sds((t, MLA_W), BF16), _sds((t, GMLP_W), F32), _sds(mod.shape, F32)],
        compiler_params=_params(1),
    )(dx2, mix, mod, wout)


def _swap_cores(parts, name):
    n = len(parts)

    def body(*refs):
        srcs, outs, send_sems, recv_sems = refs[:n], refs[n:2 * n], refs[2 * n], refs[2 * n + 1]
        x, y, c = lax.axis_index("x"), lax.axis_index("y"), lax.axis_index("c")
        copies = [pltpu.make_async_remote_copy(
            src_ref=srcs[w], dst_ref=outs[w], send_sem=send_sems.at[w], recv_sem=recv_sems.at[w],
            device_id=(x, y, 1 - c), device_id_type=pl.DeviceIdType.MESH) for w in range(n)]
        for cp in copies:
            cp.start()
        for cp in copies:
            cp.wait()

    any_spec = pl.BlockSpec(memory_space=pl.ANY)
    return pl.pallas_call(
        body, name=name, in_specs=[any_spec] * n, out_specs=[any_spec] * n,
        out_shape=[_sds(p.shape, p.dtype) for p in parts],
        scratch_shapes=[pltpu.SemaphoreType.DMA((n,)), pltpu.SemaphoreType.DMA((n,))],
    )(*parts)


def _row_tile(r, c, mult):
    return _div_tile(r, max(mult, (1 << 18) // c), mult)


def _sum_slots(recv, name):
    _, r, c = recv.shape
    tr = _row_tile(r, c, 16)

    def body(r_ref, o_ref):
        f = lambda k: r_ref[k].astype(F32)
        o_ref[...] = ((f(0) + f(1)) + f(2)) + f(3)

    return pl.pallas_call(
        body, name=name, grid=(r // tr,),
        in_specs=[pl.BlockSpec((N_CHIPS, tr, c), lambda i: (0, i, 0))],
        out_specs=pl.BlockSpec((tr, c), lambda i: (i, 0)),
        out_shape=_sds((r, c), F32), compiler_params=_params(1),
    )(recv)


def _adamw(parts, w, m, v, name, exch=None):
    r, wd = w.shape
    tr = _row_tile(r, wd, 8)
    c1 = 1.0 / (1.0 - ADAM_B1 ** ADAM_STEP)
    c2 = 1.0 / (1.0 - ADAM_B2 ** ADAM_STEP)
    n_p = len(parts)

    def body(*refs):
        p_refs = refs[:n_p]
        w_ref, m_ref, v_ref, g_ref, d_ref, nm_ref, nv_ref = refs[n_p:]
        g = p_refs[0][...]
        for p_ref in p_refs[1:]:
            g = g + p_ref[...]
        nm = ADAM_B1 * m_ref[...] + (1.0 - ADAM_B1) * g
        nv = ADAM_B2 * v_ref[...] + (1.0 - ADAM_B2) * (g * g)
        g_ref[...] = g
        nm_ref[...] = nm
        nv_ref[...] = nv
        d_ref[...] = -ADAM_LR * ((nm * c1) / (jnp.sqrt(nv * c2) + ADAM_EPS) + ADAM_WD * w_ref[...])

    spec = pl.BlockSpec((tr, wd), lambda i: (i, 0))
    return _hosted_call(body, name, (r // tr,), [spec] * (n_p + 3), [spec] * 4, [_sds((r, wd), F32)] * 4,
                        (*parts, w, m, v), exch=exch)


def _all_peers(x, y, c):
    flips = [(dx, dy, dc) for dx in (0, 1) for dy in (0, 1) for dc in (0, 1)][1:]
    return [(1 - x if dx else x, 1 - y if dy else y, 1 - c if dc else c) for dx, dy, dc in flips]


def _first_exchange(shards, cc, w, b):
    n_w = len(shards)
    n = w.shape[1]

    def body(*refs):
        srcs, (cc_ref, w_ref, b_ref) = refs[:n_w], refs[n_w:n_w + 3]
        outs, (all_ref, tab_ref) = refs[n_w + 3:2 * n_w + 3], refs[2 * n_w + 3:2 * n_w + 5]
        (part_ref, ici_send, ici_recv, d2d_send, d2d_recv, local_sems, cc_send, cc_recv, tab_send,
         tab_recv) = refs[2 * n_w + 5:]
        x, y, c = lax.axis_index("x"), lax.axis_index("y"), lax.axis_index("c")
        chip, dev = 2 * x + y, 4 * x + 2 * y + c
        chips = _other_chips(x, y)
        peers = _all_peers(x, y, c)

        def half(wi, which):
            hr = shards[wi].shape[0] // 2
            return pl.ds(pl.multiple_of(which * hr, 16), hr)

        def over_ici(wi, k, arriving):
            px, py = chips[k]
            slot = 2 * px + py if arriving else chip
            return pltpu.make_async_remote_copy(
                src_ref=srcs[wi].at[half(wi, c)], dst_ref=outs[wi].at[slot, half(wi, c)],
                send_sem=ici_send.at[3 * wi + k], recv_sem=ici_recv.at[3 * wi + k], device_id=(px, py, c),
                device_id_type=pl.DeviceIdType.MESH)

        def to_sibling(wi, k, arriving):
            px, py = chips[k]
            rows = half(wi, 1 - c if arriving else c)
            return pltpu.make_async_remote_copy(
                src_ref=outs[wi].at[2 * px + py, rows], dst_ref=outs[wi].at[2 * px + py, rows],
                send_sem=d2d_send.at[3 * wi + k], recv_sem=d2d_recv.at[3 * wi + k], device_id=(x, y, 1 - c),
                device_id_type=pl.DeviceIdType.MESH)

        def cc_copy(k, peer, slot):
            return pltpu.make_async_remote_copy(
                src_ref=cc_ref, dst_ref=all_ref.at[slot], send_sem=cc_send.at[k], recv_sem=cc_recv.at[k],
                device_id=peer, device_id_type=pl.DeviceIdType.MESH)

        def rows_of(px, py):
            return part_ref.at[pl.ds(pl.multiple_of((4 * px + 2 * py + c) * MOD_ROWS, MOD_ROWS), MOD_ROWS)]

        def tab_copy(k, px, py, slot):
            return pltpu.make_async_remote_copy(
                src_ref=rows_of(px, py), dst_ref=tab_ref.at[slot], send_sem=tab_send.at[k], recv_sem=tab_recv.at[k],
                device_id=(px, py, c), device_id_type=pl.DeviceIdType.MESH)

        local = [pltpu.make_async_copy(srcs[wi], outs[wi].at[chip], local_sems.at[wi]) for wi in range(n_w)]
        for cp in local:
            cp.start()
        pairs = [(wi, k) for wi in range(n_w) for k in range(3)]
        for wi, k in pairs:
            over_ici(wi, k, False).start()
        for k, peer in enumerate(peers):
            cc_copy(k, peer, dev).start()
        all_ref[dev] = cc_ref[...]
        for k, (px, py, pc) in enumerate(peers):
            cc_copy(k, (px, py, pc), 4 * px + 2 * py + pc).wait_recv()
        cv = all_ref[...].reshape(8 * MOD_ROWS, cc.shape[1])
        part_ref[...] = _dot((cv * _sigmoid(cv)).astype(BF16), w_ref[...]) + b_ref[...]
        for k, (px, py) in enumerate(chips):
            tab_copy(k, px, py, chip).start()
        tab_ref[chip] = rows_of(x, y)[...]
        for k, (px, py) in enumerate(chips):
            tab_copy(k, px, py, 2 * px + py).wait_recv()
        for wi, k in pairs:
            over_ici(wi, k, True).wait_recv()
            to_sibling(wi, k, False).start()
        for wi, k in pairs:
            to_sibling(wi, k, True).wait_recv()
        for wi, k in pairs:
            over_ici(wi, k, False).wait_send()
            to_sibling(wi, k, False).wait_send()
        for k, peer in enumerate(peers):
            cc_copy(k, peer, dev).wait_send()
        for k, (px, py) in enumerate(chips):
            tab_copy(k, px, py, chip).wait_send()
        for cp in local:
            cp.wait()

    any_spec = pl.BlockSpec(memory_space=pl.ANY)
    vmem = pl.BlockSpec(memory_space=pltpu.VMEM)
    sems3 = pltpu.SemaphoreType.DMA((3 * n_w,))
    got = pl.pallas_call(
        body, name="first_exchange", in_specs=[any_spec] * n_w + [vmem] * 3, out_specs=[any_spec] * n_w + [vmem] * 2,
        out_shape=_exch_shapes("gather", shards) + [_sds((8,) + cc.shape, F32), _sds((N_CHIPS, MOD_ROWS, n), F32)],
        scratch_shapes=[pltpu.VMEM((8 * MOD_ROWS, n), F32), sems3, sems3, sems3, sems3, pltpu.SemaphoreType.DMA((n_w,)),
                        pltpu.SemaphoreType.DMA((7,)), pltpu.SemaphoreType.DMA((7,)), pltpu.SemaphoreType.DMA((3,)),
                        pltpu.SemaphoreType.DMA((3,))],
        compiler_params=pltpu.CompilerParams(vmem_limit_bytes=V7X_VMEM_LIMIT),
    )(*shards, cc, w, b)
    return got[:n_w], got[n_w], got[n_w + 1]


def _ada_bwd_tp(cc_all, dmods, w, ctx_row):
    d, n = w.shape

    def body(cc_ref, m0, m1, m2, m3, w_ref, dw_ref, db_ref, dctx_ref, stage_ref, all_ref, send_sems, recv_sems):
        x, y, c = lax.axis_index("x"), lax.axis_index("y"), lax.axis_index("c")
        me = 4 * x + 2 * y + c
        dsum = m0[...] + m1[...] + m2[...] + m3[...]
        db_ref[...] = jnp.sum(dsum, axis=0, keepdims=True)
        for j in range(N_CHIPS):
            stage_ref[j] = dsum[:, j * n:(j + 1) * n]

        def copy(k, peer, slot):
            px, py, _ = peer
            return pltpu.make_async_remote_copy(
                src_ref=stage_ref.at[2 * px + py], dst_ref=all_ref.at[slot], send_sem=send_sems.at[k],
                recv_sem=recv_sems.at[k], device_id=peer, device_id_type=pl.DeviceIdType.MESH)

        peers = _all_peers(x, y, c)
        for k, peer in enumerate(peers):
            copy(k, peer, me).start()
        all_ref[me] = stage_ref[2 * x + y]
        for k, (px, py, pc) in enumerate(peers):
            copy(k, (px, py, pc), 4 * px + 2 * py + pc).wait_recv()
        for k, peer in enumerate(peers):
            copy(k, peer, me).wait_send()
        cv = cc_ref[...]
        sig = _sigmoid(cv)
        dmb = all_ref[...].reshape(8 * MOD_ROWS, n).astype(BF16)
        dw_ref[...] = _dot_tn((cv * sig).astype(BF16), dmb)
        dsc = _dot_nt(dmb, w_ref[...])
        dctx = dsc[ctx_row:ctx_row + 1, :]
        for dev in range(1, 8):
            dctx = dctx + dsc[dev * MOD_ROWS + ctx_row:dev * MOD_ROWS + ctx_row + 1, :]
        cx = cv[ctx_row:ctx_row + 1, :]
        sx = sig[ctx_row:ctx_row + 1, :]
        dctx_ref[...] = dctx * (sx * (1.0 + cx * (1.0 - sx))) * jnp.where(c == 0, 1.0, 0.0)

    vmem = pl.BlockSpec(memory_space=pltpu.VMEM)
    return pl.pallas_call(
        body, name="ada_bwd_tp", in_specs=[vmem] * 6, out_specs=[vmem] * 3,
        out_shape=[_sds((d, n), F32), _sds((1, N_MOD * d), F32), _sds((1, d), F32)],
        scratch_shapes=[pltpu.VMEM((N_CHIPS, MOD_ROWS, n), F32), pltpu.VMEM((8, MOD_ROWS, n), F32),
                        pltpu.SemaphoreType.DMA((7,)), pltpu.SemaphoreType.DMA((7,))],
        compiler_params=pltpu.CompilerParams(vmem_limit_bytes=V7X_VMEM_LIMIT),
    )(cc_all, *dmods, w)


def _rope_tables(s, ctx):
    pos = np.arange(s, dtype=np.float32)
    inv = (np.float32(ROPE_BASE) ** (-np.arange(0, QK_ROPE // 2, 2, dtype=np.float32) / np.float32(QK_ROPE // 2)))
    ang_r = np.floor(pos / GRID_W)[:, None] * inv
    ang_c = (pos - GRID_W * np.floor(pos / GRID_W))[:, None] * inv
    ang = np.concatenate([ang_r, ang_r, ang_c, ang_c], axis=-1).astype(np.float32)
    cos, sin = np.cos(ang), np.sin(ang)
    half_b = (np.arange(QK_ROPE) // 8) % 2 == 1
    sin_a = np.where(half_b, sin, 0.0)
    sin_b = np.where(half_b, 0.0, -sin)

    def place(tab, fill):
        full = np.full((s + ctx, HEAD_PAD), fill, np.float32)
        full[:s, QK_NOPE:QK_HEAD] = tab
        return jnp.asarray(full)

    return place(cos, 1.0), place(sin_a, 0.0), place(sin_b, 0.0)


def _pad_last(a, n):
    return jnp.pad(a, [(0, 0)] * (a.ndim - 1) + [(0, n - a.shape[-1])])


def _flat_rows(parts, rows, width):
    flat = jnp.concatenate([p.reshape(-1) for p in parts])
    return jnp.pad(flat, (0, rows * width - flat.shape[0])).reshape(rows, width)


def kernel(x, c, ctx, c_ctx, w_ada, b_ada, norm1_w, ffn1_w1, ffn1_w3, ffn1_w2, norm2_w, w_in, q_a_norm_w, w_uq, kv_a_norm_w, w_ukv, q_norm_w, k_norm_w, v_norm_w, w_s, b_s, w_out, norm3_w, ffn2_w1, ffn2_w3, ffn2_w2, loss_target, m_c_ctx, m_w_ada, m_b_ada, m_norm1_w, m_ffn1_w1, m_ffn1_w3, m_ffn1_w2, m_norm2_w, m_w_in, m_q_a_norm_w, m_w_uq, m_kv_a_norm_w, m_w_ukv, m_q_norm_w, m_k_norm_w, m_v_norm_w, m_w_s, m_b_s, m_w_out, m_norm3_w, m_ffn2_w1, m_ffn2_w3, m_ffn2_w2, v_c_ctx, v_w_ada, v_b_ada, v_norm1_w, v_ffn1_w1, v_ffn1_w3, v_ffn1_w2, v_norm2_w, v_w_in, v_q_a_norm_w, v_w_uq, v_kv_a_norm_w, v_w_ukv, v_q_norm_w, v_k_norm_w, v_v_norm_w, v_w_s, v_b_s, v_w_out, v_norm3_w, v_ffn2_w1, v_ffn2_w3, v_ffn2_w2):
    wts = dict(c_ctx=c_ctx, w_ada=w_ada, b_ada=b_ada, norm1_w=norm1_w, ffn1_w1=ffn1_w1, ffn1_w3=ffn1_w3, ffn1_w2=ffn1_w2,
               norm2_w=norm2_w, w_in=w_in, q_a_norm_w=q_a_norm_w, w_uq=w_uq, kv_a_norm_w=kv_a_norm_w, w_ukv=w_ukv,
               q_norm_w=q_norm_w, k_norm_w=k_norm_w, v_norm_w=v_norm_w, w_s=w_s, b_s=b_s, w_out=w_out, norm3_w=norm3_w,
               ffn2_w1=ffn2_w1, ffn2_w3=ffn2_w3, ffn2_w2=ffn2_w2)
    moms = dict(c_ctx=m_c_ctx, w_ada=m_w_ada, b_ada=m_b_ada, norm1_w=m_norm1_w, ffn1_w1=m_ffn1_w1, ffn1_w3=m_ffn1_w3,
                ffn1_w2=m_ffn1_w2, norm2_w=m_norm2_w, w_in=m_w_in, q_a_norm_w=m_q_a_norm_w, w_uq=m_w_uq,
                kv_a_norm_w=m_kv_a_norm_w, w_ukv=m_w_ukv, q_norm_w=m_q_norm_w, k_norm_w=m_k_norm_w, v_norm_w=m_v_norm_w,
                w_s=m_w_s, b_s=m_b_s, w_out=m_w_out, norm3_w=m_norm3_w, ffn2_w1=m_ffn2_w1, ffn2_w3=m_ffn2_w3,
                ffn2_w2=m_ffn2_w2)
    vars_ = dict(c_ctx=v_c_ctx, w_ada=v_w_ada, b_ada=v_b_ada, norm1_w=v_norm1_w, ffn1_w1=v_ffn1_w1, ffn1_w3=v_ffn1_w3,
                 ffn1_w2=v_ffn1_w2, norm2_w=v_norm2_w, w_in=v_w_in, q_a_norm_w=v_q_a_norm_w, w_uq=v_w_uq,
                 kv_a_norm_w=v_kv_a_norm_w, w_ukv=v_w_ukv, q_norm_w=v_q_norm_w, k_norm_w=v_k_norm_w, v_norm_w=v_v_norm_w,
                 w_s=v_w_s, b_s=v_b_s, w_out=v_w_out, norm3_w=v_norm3_w, ffn2_w1=v_ffn2_w1, ffn2_w3=v_ffn2_w3,
                 ffn2_w2=v_ffn2_w2)

    nb, s, d = x.shape
    nctx = ctx.shape[1]
    t, tc = nb * s, nb * nctx
    t_all = t + tc
    sk = s + nctx
    assert nb + 1 <= MOD_ROWS and d % LANES == 0
    tm = _token_tile(s, nctx)

    def held(n, a_):
        return jnp.swapaxes(a_[0], 0, 1) if n in T_WEIGHTS else a_[0]

    def unheld(n, a_):
        return (jnp.swapaxes(a_, 0, 1) if n in T_WEIGHTS else a_)[None]

    shard = {n: held(n, wts[n]).astype(BF16) for n in SHARDED}
    full = {}

    def unshard(names, blocks):
        for n, g4 in zip(names, blocks):
            _, r_, c_ = g4.shape
            if n in ROW_SHARDED or n in T_WEIGHTS:
                full[n] = g4.reshape(N_CHIPS * r_, c_)
            else:
                full[n] = g4.transpose(1, 0, 2).reshape(r_, N_CHIPS * c_)

    def chip_major(n, g_):
        if n in ROW_SHARDED or n in T_WEIGHTS:
            return g_.reshape(N_CHIPS, g_.shape[0] // N_CHIPS, g_.shape[1]).astype(BF16)
        r_, cols = g_.shape
        return g_.reshape(r_, N_CHIPS, cols // N_CHIPS).transpose(1, 0, 2).astype(BF16)

    cc = jnp.concatenate([c, c_ctx[None, :], jnp.zeros((MOD_ROWS - nb - 1, d), F32)], axis=0)
    n_ada = shard["w_ada"].shape[1]
    assert n_ada % LANES == 0
    my_chip = 2 * lax.axis_index("x") + lax.axis_index("y")
    b_cols = lax.dynamic_slice_in_dim(b_ada, my_chip * n_ada, n_ada, axis=1)
    got, cc_all, table = _first_exchange([shard[n] for n in FIRST_WEIGHTS], cc, shard["w_ada"], b_cols)
    unshard(FIRST_WEIGHTS, got)
    cc_all = cc_all.reshape(8 * MOD_ROWS, d)
    mod = table.transpose(1, 0, 2).reshape(MOD_ROWS, N_MOD, d)
    wsb = w_s[0].astype(BF16)
    wcat = wsb.transpose(1, 0, 2).reshape(CHUNK, GROUPS * CHUNK)
    wcat_t = wsb.transpose(2, 0, 1).reshape(CHUNK, GROUPS * CHUNK)
    bias = jnp.repeat(b_s[0].T, GROUP_DIM, axis=1)
    vnw = v_norm_w.reshape(1, GMLP_W)
    lane = jnp.arange(GMLP_W)
    ones = (lane[:, None] // GROUP_DIM == lane[None, :] // GROUP_DIM).astype(BF16)
    qnw = _pad_last(q_norm_w, HEAD_PAD)
    knw = _pad_last(k_norm_w, HEAD_PAD)
    tabs = _rope_tables(s, nctx)

    x_lat, x_ctx = x.reshape(t, d), ctx.reshape(tc, d)
    (xs1, a1, b1, y1), got = _ffn_fwd(x_lat, x_ctx, mod, norm1_w, full["ffn1_w1"], full["ffn1_w3"], full["ffn1_w2"], 0, s,
                                      nb, tm, "ffn1_fwd", exch=("gather", [shard[n] for n in MIX_WEIGHTS]))
    unshard(MIX_WEIGHTS, got)
    wi = full["w_in"]
    wp = jnp.concatenate([wi[0:KV_LORA], jnp.zeros((QK_NOPE, d), BF16), wi[KV_LORA:KV_LORA + QK_ROPE],
                          jnp.zeros((HEAD_PAD - QK_HEAD, d), BF16), wi[KV_LORA + QK_ROPE:]], axis=0)
    wq = jnp.pad(full["w_uq"].reshape(HEADS, QK_HEAD, Q_LORA), ((0, 0), (0, HEAD_PAD - QK_HEAD), (0, 0)))
    wkv = full["w_ukv"].reshape(KV_LORA, HEADS, QK_NOPE + V_HEAD)
    wk = _pad_last(wkv[:, :, :QK_NOPE].transpose(1, 0, 2), HEAD_PAD)
    wv = wkv[:, :, QK_NOPE:].reshape(KV_LORA, HEADS // 2, 2 * V_HEAD).transpose(1, 0, 2)
    h2, proj = _mixin_fwd(xs1, mod, norm2_w, wp, s, nb, tm)
    prep_w = (wq, wk, wv, kv_a_norm_w, q_a_norm_w, qnw, knw)
    q, k_all, v_all = _prep_fwd(proj, 0, nb, s, 0, sk, 0, None, tabs, *prep_w, tm, True, "prep_fwd")
    k_all, v_all = _prep_fwd(proj, t // tm, nb, nctx, s // tm, sk, s // tm, (k_all, v_all), tabs, *prep_w, tm, False,
                             "prep_ctx_fwd")
    tq = _div_tile(s, 512, tm)
    o, lse, got = _attn_fwd(q, k_all, v_all, tq, exch=("gather", [shard[n] for n in LAST_WEIGHTS]))
    unshard(LAST_WEIGHTS, got)
    sg = _gmlp_fwd(proj, t, wcat, bias, vnw, ones, tm)
    x2, mix = _mixout_fwd(o, sg, xs1, mod, full["w_out"], s, tm)
    (dy, a2, b2, y2, loss_part), _ = _ffn_fwd(x2, None, mod, norm3_w, full["ffn2_w1"], full["ffn2_w3"], full["ffn2_w2"], 6,
                                              s, nb, tm, "ffn2_fwd", target=loss_target.reshape(t, d))

    grads, cm, recv = {}, {}, {}

    def scatter_of(names):
        return ("scatter", [cm[n] for n in names])

    (dx2, h3, g2, da2, db2, dyb2, dmod_c, grads["norm3_w"]), _ = _ffn_bwd(
        dy, x2, None, a2, b2, y2, mod, norm3_w, full["ffn2_w1"], full["ffn2_w3"], full["ffn2_w2"], 6, s, nb, tm,
        "ffn2_bwd")
    cm["ffn2_w1"] = chip_major("ffn2_w1", _mm_tn(da2, h3, t, "ffn2_dw1"))
    cm["ffn2_w3"] = chip_major("ffn2_w3", _mm_tn(db2, h3, t, "ffn2_dw3"))
    cm["ffn2_w2"] = chip_major("ffn2_w2", _mm_tn(g2, dyb2, t, "ffn2_dw2"))
    dmix, do, dsg, dmod_b = _mixout_bwd(dx2, mix, mod, full["w_out"], s, tm)
    cm["w_out"] = chip_major("w_out", jnp.concatenate([_mm_tn(o, dmix, t, "wout_dw_attn"),
                                                       _mm_tn(sg, dmix, t, "wout_dw_gmlp")], axis=0))
    dpu, dpv, dws, dbs, dvnw = _gmlp_bwd(proj, dsg, wcat, wcat_t, bias, vnw, ones, tm)
    group = LAST_WEIGHTS + ("w_out",)
    (dq, dk, dv), got = _attn_bwd(q, k_all, v_all, do, o, lse, tq, exch=scatter_of(group))
    recv.update(zip(group, got))
    dp0, dwk_c, dwv_c, dkvaw_c, dknw_c = _prep_bwd(
        proj, t // tm, nb, nctx, s // tm, s // tm, t_all, None, tabs, *prep_w, None, dk, dv, None, tm, "prep_ctx_bwd")
    dp0, dwq, dqaw, dqnw, dwk, dwv, dkvaw, dknw = _prep_bwd(
        proj, 0, nb, s, 0, 0, t_all, dp0, tabs, *prep_w, dq, dk, dv, [dwk_c, dwv_c, dkvaw_c, dknw_c], tm, "prep_bwd")
    dxs1, dmod_a, grads["norm2_w"] = _mixin_bwd(dp0, dpu, dpv, xs1, dx2, mod, norm2_w, wp, s, nb, tm)
    dwp = jnp.concatenate([_mm_tn(dp0, h2, t_all, "win_dw_kvq"), _mm_tn(dpu, h2, t, "win_dw_u"),
                           _mm_tn(dpv, h2, t, "win_dw_v")], axis=0)
    cm["w_in"] = chip_major("w_in", jnp.concatenate(
        [dwp[0:KV_LORA], dwp[KV_LORA + QK_NOPE:KV_LORA + QK_HEAD], dwp[256:]], axis=0))
    cm["w_uq"] = chip_major("w_uq", dwq[:, :, :QK_HEAD].transpose(0, 2, 1).reshape(HEADS * QK_HEAD, Q_LORA))
    cm["w_ukv"] = chip_major("w_ukv", jnp.concatenate(
        [dwk[:, :, :QK_NOPE].transpose(1, 0, 2),
         dwv.transpose(1, 0, 2).reshape(KV_LORA, HEADS, V_HEAD)], axis=2).reshape(KV_LORA, HEADS * (QK_NOPE + V_HEAD)))
    (dx_lat, h1, g1, da1, db1, dyb1, dmod_0, grads["norm1_w"]), _ = _ffn_bwd(
        dxs1, x_lat, x_ctx, a1, b1, y1, mod, norm1_w, full["ffn1_w1"], full["ffn1_w3"], full["ffn1_w2"], 0, s, nb, tm,
        "ffn1_bwd")
    dmods = [m_.reshape(MOD_ROWS, N_MOD * d) for m_ in (dmod_0, dmod_a, dmod_b, dmod_c)]
    dw_ada, grads["b_ada"], dctx = _ada_bwd_tp(cc_all, dmods, shard["w_ada"], nb)
    grads["c_ctx"] = dctx[0]
    grads["q_a_norm_w"], grads["kv_a_norm_w"] = dqaw, dkvaw
    grads["q_norm_w"], grads["k_norm_w"] = dqnw[:, :QK_HEAD], dknw[:, :QK_HEAD]
    grads["v_norm_w"], grads["w_s"], grads["b_s"] = dvnw, dws, dbs[:, 0]
    grad_x = dx_lat.reshape(nb, s, d)
    n_small = sum(wts[n].size for n in SMALL)
    rows_s = _round_up(-(-(n_small + 1) // d), 16)
    cm["small"] = jnp.broadcast_to(_flat_rows([grads[n] for n in SMALL] + [loss_part], rows_s, d), (N_CHIPS, rows_s, d))
    group = ("w_in", "w_uq", "w_ukv", "small")
    dw2, got = _mm_tn(g1, dyb1, t_all, "ffn1_dw2", exch=scatter_of(group))
    recv.update(zip(group, got))
    cm["ffn1_w2"] = chip_major("ffn1_w2", dw2)
    dw1, got = _mm_tn(da1, h1, t_all, "ffn1_dw1", exch=scatter_of(("ffn1_w2",)))
    recv["ffn1_w2"] = got[0]
    cm["ffn1_w1"] = chip_major("ffn1_w1", dw1)
    dw3, got = _mm_tn(db1, h1, t_all, "ffn1_dw3", exch=scatter_of(("ffn1_w1",)))
    recv["ffn1_w1"] = got[0]
    cm["ffn1_w3"] = chip_major("ffn1_w3", dw3)
    stepped = {}
    stepped["w_ada"], got = _adamw([dw_ada], wts["w_ada"][0], moms["w_ada"][0], vars_["w_ada"][0], "adamw_w_ada",
                                   exch=scatter_of(("ffn1_w3",)))
    recv["ffn1_w3"] = got[0]

    reduced = tuple(n for n in SHARDED if n != "w_ada") + ("small",)
    part = {n: _sum_slots(recv[n], "sum_" + n) for n in reduced}
    early = LAST_WEIGHTS + ("w_out",)
    late = tuple(n for n in reduced if n not in early)
    sib = dict(zip(early, _swap_cores([part[n] for n in early], "swap_early")))
    sib.update(zip(late, _swap_cores([part[n] for n in late], "swap_late")))
    for n in reduced[:-1]:
        stepped[n], _ = _adamw([part[n], sib[n]], held(n, wts[n]), held(n, moms[n]), held(n, vars_[n]), "adamw_" + n)
    for n in SHARDED:
        stepped[n] = [unheld(n, a_) for a_ in stepped[n]]
    packed, _ = _adamw([part["small"], sib["small"]], _flat_rows([wts[n] for n in SMALL], rows_s, d),
                       _flat_rows([moms[n] for n in SMALL], rows_s, d), _flat_rows([vars_[n] for n in SMALL], rows_s, d),
                       "adamw_small")
    loss = packed[0].reshape(-1)[n_small]
    for n in SMALL:
        stepped[n] = []
    for a_ in packed:
        flat = a_.reshape(-1)
        off = 0
        for n in SMALL:
            stepped[n].append(flat[off:off + wts[n].size].reshape(wts[n].shape))
            off += wts[n].size
    return (loss, grad_x, *[stepped[n][0] for n in WEIGHTS], *[stepped[n][1] for n in WEIGHTS],
            *[stepped[n][2] for n in WEIGHTS], *[stepped[n][3] for n in WEIGHTS])
```

```python
import functools
import math

import jax
import jax.numpy as jnp
import numpy as np
from jax import lax
from jax.experimental import pallas as pl
from jax.experimental.pallas import tpu as pltpu

F32 = jnp.float32
BF16 = jnp.bfloat16

EPS = 1e-6
N_MOD = 9
HEADS = 8
QK_NOPE, QK_ROPE, V_HEAD = 64, 32, 64
QK_HEAD = QK_NOPE + QK_ROPE
HEAD_PAD = 128
LN2 = math.log(2.0)
SOFTMAX_SCALE = QK_HEAD ** -0.5 / LN2
Q_LORA, KV_LORA = 256, 128
GROUPS, GROUP_DIM, CHUNK = 8, 64, 128
GMLP_W = GROUPS * GROUP_DIM
MLA_W = HEADS * V_HEAD
IN_COLS = 1440
PROJ_COLS = 1536
GRID_W = 64
ROPE_BASE = 10000.0
MOD_ROWS = 16
ADAM_LR, ADAM_B1, ADAM_B2, ADAM_EPS, ADAM_WD, ADAM_STEP = 0.001, 0.9, 0.999, 1e-08, 0.01, 10
N_CHIPS = 4
LANES = 128
V7X_VMEM_LIMIT = 56 * 1024 * 1024
GELU_C = math.sqrt(2.0 / math.pi)

SHARDED = ("w_ada", "ffn1_w1", "ffn1_w3", "ffn1_w2", "w_in", "w_uq", "w_ukv", "w_out", "ffn2_w1", "ffn2_w3", "ffn2_w2")
ROW_SHARDED = ("ffn1_w2", "w_out", "ffn2_w2")
T_WEIGHTS = ("ffn1_w1", "ffn1_w3", "ffn2_w1", "ffn2_w3", "w_in", "w_uq")
FIRST_WEIGHTS = ("ffn1_w1", "ffn1_w3", "ffn1_w2")
MIX_WEIGHTS = ("w_in", "w_uq", "w_ukv", "w_out")
LAST_WEIGHTS = ("ffn2_w1", "ffn2_w3", "ffn2_w2")
SMALL = ("c_ctx", "b_ada", "norm1_w", "norm2_w", "q_a_norm_w", "kv_a_norm_w", "q_norm_w", "k_norm_w", "v_norm_w",
         "w_s", "b_s", "norm3_w")
WEIGHTS = ("c_ctx", "w_ada", "b_ada", "norm1_w", "ffn1_w1", "ffn1_w3", "ffn1_w2", "norm2_w", "w_in", "q_a_norm_w",
           "w_uq", "kv_a_norm_w", "w_ukv", "q_norm_w", "k_norm_w", "v_norm_w", "w_s", "b_s", "w_out", "norm3_w",
           "ffn2_w1", "ffn2_w3", "ffn2_w2")


def _round_up(n, m):
    return (n + m - 1) // m * m


def _div_tile(n, target, mult):
    best = None
    for t in range(mult, min(n, target) + 1, mult):
        if n % t == 0:
            best = t
    return n if best is None else best


def _dot(a, b):
    return lax.dot_general(a, b, (((1,), (0,)), ((), ())), preferred_element_type=F32)


def _dot_nt(a, b):
    return lax.dot_general(a, b, (((1,), (1,)), ((), ())), preferred_element_type=F32)


def _dot_tn(a, b):
    return lax.dot_general(a, b, (((0,), (0,)), ((), ())), preferred_element_type=F32)


def _sigmoid(x):
    return 1.0 / (1.0 + jnp.exp(-x))


def _gelu(x):
    return 0.5 * x * (1.0 + jnp.tanh(GELU_C * (x + 0.044715 * x * x * x)))


def _gelu_grad(x):
    t = jnp.tanh(GELU_C * (x + 0.044715 * x * x * x))
    return 0.5 * (1.0 + t) + 0.5 * x * (1.0 - t * t) * (GELU_C * (1.0 + 3 * 0.044715 * x * x))


def _rope3(x, cos, sin_a, sin_b):
    return x * cos + pltpu.roll(x, 8, 2) * sin_a + pltpu.roll(x, HEAD_PAD - 8, 2) * sin_b


def _rope3_t(d, cos, sin_a, sin_b):
    return d * cos + pltpu.roll(d * sin_a, HEAD_PAD - 8, 2) + pltpu.roll(d * sin_b, 8, 2)


def _group_sum(x, ones_ref):
    hi = x.astype(BF16)
    lo = (x - hi.astype(F32)).astype(BF16)
    return _dot(hi, ones_ref[...]) + _dot(lo, ones_ref[...])


def _params(n_axes):
    return pltpu.CompilerParams(dimension_semantics=("arbitrary",) * n_axes, vmem_limit_bytes=V7X_VMEM_LIMIT)


def _whole(shape):
    nd = len(shape)
    return pl.BlockSpec(shape, lambda *_: (0,) * nd, pipeline_mode=pl.Buffered(1))


def _sds(shape, dtype):
    return jax.ShapeDtypeStruct(shape, dtype)


def _token_tile(s, ctx):
    return _div_tile(math.gcd(s, ctx), 256, CHUNK)


def _other_chips(x, y):
    return [(1 - x, y), (x, 1 - y), (1 - x, 1 - y)]


def _exch_copies(kind, srcs, dsts, send_sems, recv_sems, local_sems, with_arrivals):
    x, y, c = lax.axis_index("x"), lax.axis_index("y"), lax.axis_index("c")
    me = 2 * x + y
    local, sends, arrivals = [], [], []
    for w, (src, dst) in enumerate(zip(srcs, dsts)):
        own = src if kind == "gather" else src.at[me]
        local.append(pltpu.make_async_copy(own, dst.at[me], local_sems.at[w]))
        for k, (px, py) in enumerate(_other_chips(x, y)):
            sem = dict(send_sem=send_sems.at[3 * w + k], recv_sem=recv_sems.at[3 * w + k], device_id=(px, py, c),
                       device_id_type=pl.DeviceIdType.MESH)
            out = src if kind == "gather" else src.at[2 * px + py]
            sends.append(pltpu.make_async_remote_copy(src_ref=out, dst_ref=dst.at[me], **sem))
            if with_arrivals:
                arrivals.append(pltpu.make_async_remote_copy(src_ref=own, dst_ref=dst.at[2 * px + py], **sem))
    return local, sends, arrivals


def _exch_start(kind, srcs, dsts, sems):
    local, sends, _ = _exch_copies(kind, srcs, dsts, *sems, with_arrivals=False)
    for cp in local + sends:
        cp.start()


def _exch_wait(kind, srcs, dsts, sems):
    local, sends, arrivals = _exch_copies(kind, srcs, dsts, *sems, with_arrivals=True)
    for cp in arrivals:
        cp.wait_recv()
    for cp in sends:
        cp.wait_send()
    for cp in local:
        cp.wait()


def _exch_scratch(n):
    return [pltpu.SemaphoreType.DMA((3 * n,)), pltpu.SemaphoreType.DMA((3 * n,)), pltpu.SemaphoreType.DMA((n,))]


def _exch_shapes(kind, arrays):
    return [_sds((N_CHIPS,) + a.shape if kind == "gather" else a.shape, a.dtype) for a in arrays]


def _hosted_call(body, name, grid, in_specs, out_specs, out_shape, operands, scratch=(), exch=None):
    n_axes = len(grid)
    if exch is None:
        outs = pl.pallas_call(body, name=name, grid=grid, in_specs=list(in_specs), out_specs=list(out_specs),
                              out_shape=list(out_shape), scratch_shapes=list(scratch),
                              compiler_params=_params(n_axes))(*operands)
        return list(outs), []
    kind, arrays = exch
    n_in, n_out, n_sc, n_ex = len(in_specs), len(out_specs), len(scratch), len(arrays)

    def hosted(*refs):
        cin, ein = refs[:n_in], refs[n_in:n_in + n_ex]
        o0 = n_in + n_ex
        cout, eout = refs[o0:o0 + n_out], refs[o0 + n_out:o0 + n_out + n_ex]
        rest = refs[o0 + n_out + n_ex:]
        csc, sems = rest[:n_sc], rest[n_sc:]
        first = functools.reduce(jnp.logical_and, [pl.program_id(a) == 0 for a in range(n_axes)])
        last = functools.reduce(jnp.logical_and, [pl.program_id(a) == grid[a] - 1 for a in range(n_axes)])

        @pl.when(first)
        def _():
            _exch_start(kind, ein, eout, sems)

        body(*cin, *cout, *csc)

        @pl.when(last)
        def _():
            _exch_wait(kind, ein, eout, sems)

    any_spec = pl.BlockSpec(memory_space=pl.ANY)
    outs = pl.pallas_call(
        hosted, name=name, grid=grid, in_specs=list(in_specs) + [any_spec] * n_ex,
        out_specs=list(out_specs) + [any_spec] * n_ex, out_shape=list(out_shape) + _exch_shapes(kind, arrays),
        scratch_shapes=list(scratch) + _exch_scratch(n_ex), compiler_params=_params(n_axes),
    )(*operands, *arrays)
    return list(outs[:n_out]), list(outs[n_out:])


class _TokenTiles:
    def __init__(self, t, tc, tm):
        self.n_lat, self.n_ctx = t // tm, tc // tm
        self.n_all = self.n_lat + self.n_ctx

    def tile(self, i):
        return (i + self.n_lat) % self.n_all if self.n_ctx else i

    def is_lat(self, i):
        return self.tile(i) < self.n_lat

    def row(self, i):
        return (self.tile(i), 0)

    def lat_row(self, i):
        return (jnp.where(self.is_lat(i), self.tile(i), 0), 0) if self.n_ctx else (i, 0)

    def ctx_row(self, i):
        return (jnp.where(self.is_lat(i), self.n_ctx - 1, self.tile(i) - self.n_lat), 0)


def _ffn_fwd(x_lat, x_ctx, mod, nw, w1, w3, w2, k0, s, nb, tm, name, target=None, exch=None):
    t, d = x_lat.shape
    tc = 0 if x_ctx is None else x_ctx.shape[0]
    f = w1.shape[0]
    tiles = _TokenTiles(t, tc, tm)
    n_x = 2 if tc else 1
    n_t = 0 if target is None else 1
    assert not (tc and n_t)

    def body(*refs):
        x_ref = refs[0]
        t_ref = refs[n_x] if n_t else None
        mod_ref, nw_ref, w1_ref, w3_ref, w2_ref, o_ref, a_ref, b_ref, y_ref = refs[n_x + n_t:n_x + n_t + 9]
        i = pl.program_id(0)
        g = jnp.minimum((tiles.tile(i) * tm) // s, nb)
        shift = mod_ref[g, pl.ds(k0, 1), :]
        scale = mod_ref[g, pl.ds(k0 + 1, 1), :]
        gate = mod_ref[g, pl.ds(k0 + 2, 1), :]
        x = jnp.where(tiles.is_lat(i), x_ref[...], refs[1][...]) if tc else x_ref[...]
        r = lax.rsqrt(jnp.mean(x * x, axis=-1, keepdims=True) + EPS)
        hb = ((x * r * nw_ref[...]) * (1.0 + scale) + shift).astype(BF16)
        a = _dot_nt(hb, w1_ref[...])
        b = _dot_nt(hb, w3_ref[...])
        gb = (a * _sigmoid(a) * b).astype(BF16)
        y = _dot(gb, w2_ref[...])
        out = x + (0.5 * gate) * y
        a_ref[...] = a.astype(BF16)
        b_ref[...] = b.astype(BF16)
        y_ref[...] = y.astype(BF16)
        if n_t:
            loss_ref, acc_ref = refs[-2:]

            @pl.when(i == 0)
            def _():
                acc_ref[...] = jnp.zeros_like(acc_ref)

            e = out - t_ref[...]
            o_ref[...] = e * (1.0 / d)
            acc_ref[...] += jnp.sum(e * e, axis=0, keepdims=True)

            @pl.when(i == tiles.n_all - 1)
            def _():
                loss_ref[...] = (0.5 / d) * jnp.sum(acc_ref[...], axis=-1, keepdims=True)
        else:
            o_ref[...] = out

    td = pl.BlockSpec((tm, d), tiles.row)
    tf = pl.BlockSpec((tm, f), tiles.row)
    return _hosted_call(
        body, name, (tiles.n_all,),
        [pl.BlockSpec((tm, d), tiles.lat_row)] + ([pl.BlockSpec((tm, d), tiles.ctx_row)] if tc else []) + [td] * n_t
        + [_whole(mod.shape), _whole(nw.shape), _whole(w1.shape), _whole(w3.shape), _whole(w2.shape)],
        [td, tf, tf, td] + [pl.BlockSpec((1, 1), lambda i: (0, 0))] * n_t,
        [_sds((t + tc, d), F32), _sds((t + tc, f), BF16), _sds((t + tc, f), BF16), _sds((t + tc, d), BF16)]
        + [_sds((1, 1), F32)] * n_t,
        (x_lat,) + ((x_ctx,) if tc else ()) + ((target,) if n_t else ()) + (mod, nw, w1, w3, w2),
        scratch=[pltpu.VMEM((1, d), F32)] * n_t, exch=exch)


def _ffn_bwd(dout, x_lat, x_ctx, a, b, y, mod, nw, w1, w3, w2, k0, s, nb, tm, name, exch=None):
    t, d = x_lat.shape
    tc = 0 if x_ctx is None else x_ctx.shape[0]
    f = w1.shape[0]
    nch = 2 if (f // 2) % LANES == 0 and f % 2 == 0 else 1
    fc = f // nch
    tiles = _TokenTiles(t, tc, tm)
    n_x = 2 if tc else 1

    def body(*refs):
        do_ref, x_ref = refs[0], refs[1]
        (a_ref, b_ref, y_ref, mod_ref, nw_ref, w1_ref, w3_ref, w2_ref,
         dx_ref, h_ref, g_ref, da_ref, db_ref, dy_ref, dmod_ref, dnw_ref) = refs[1 + n_x:]
        i = pl.program_id(0)

        @pl.when(i == 0)
        def _():
            dmod_ref[...] = jnp.zeros_like(dmod_ref)
            dnw_ref[...] = jnp.zeros_like(dnw_ref)

        g = jnp.minimum((tiles.tile(i) * tm) // s, nb)
        shift = mod_ref[g, pl.ds(k0, 1), :]
        scale = mod_ref[g, pl.ds(k0 + 1, 1), :]
        gate = mod_ref[g, pl.ds(k0 + 2, 1), :]
        x = jnp.where(tiles.is_lat(i), x_ref[...], refs[2][...]) if tc else x_ref[...]
        dout_v = do_ref[...]
        r = lax.rsqrt(jnp.mean(x * x, axis=-1, keepdims=True) + EPS)
        xh = x * r
        n = xh * nw_ref[...]
        h_ref[...] = (n * (1.0 + scale) + shift).astype(BF16)
        dyb = ((0.5 * gate) * dout_v).astype(BF16)
        dy_ref[...] = dyb
        dmod_ref[g, pl.ds(k0 + 2, 1), :] += 0.5 * jnp.sum(dout_v * y_ref[...].astype(F32), axis=0, keepdims=True)
        dh = jnp.zeros((tm, d), F32)
        for c in range(nch):
            sl = slice(c * fc, (c + 1) * fc)
            dg = _dot_nt(dyb, w2_ref[sl, :])
            av = a_ref[:, sl].astype(F32)
            bv = b_ref[:, sl].astype(F32)
            sig = _sigmoid(av)
            sa = av * sig
            g_ref[:, sl] = (sa * bv).astype(BF16)
            dab = (dg * bv * (sig * (1.0 + av * (1.0 - sig)))).astype(BF16)
            dbb = (dg * sa).astype(BF16)
            da_ref[:, sl] = dab
            db_ref[:, sl] = dbb
            dh = dh + _dot(dab, w1_ref[sl, :]) + _dot(dbb, w3_ref[sl, :])
        dmod_ref[g, pl.ds(k0, 1), :] += jnp.sum(dh, axis=0, keepdims=True)
        dmod_ref[g, pl.ds(k0 + 1, 1), :] += jnp.sum(dh * n, axis=0, keepdims=True)
        dn = dh * (1.0 + scale)
        dnw_ref[...] += jnp.sum(dn * xh, axis=0, keepdims=True)
        dxh = dn * nw_ref[...]
        dx_ref[...] = dout_v + r * (dxh - xh * jnp.mean(dxh * xh, axis=-1, keepdims=True))

    td = pl.BlockSpec((tm, d), tiles.row)
    tf = pl.BlockSpec((tm, f), tiles.row)
    lat = pl.BlockSpec((tm, d), tiles.lat_row)
    ta = t + tc
    return _hosted_call(
        body, name, (tiles.n_all,),
        [td, lat] + ([pl.BlockSpec((tm, d), tiles.ctx_row)] if tc else [])
        + [tf, tf, td, _whole(mod.shape), _whole(nw.shape), _whole(w1.shape), _whole(w3.shape), _whole(w2.shape)],
        [lat, td, tf, tf, tf, td, pl.BlockSpec(mod.shape, lambda i: (0, 0, 0)), pl.BlockSpec((1, d), lambda i: (0, 0))],
        [_sds((t, d), F32), _sds((ta, d), BF16), _sds((ta, f), BF16), _sds((ta, f), BF16), _sds((ta, f), BF16),
         _sds((ta, d), BF16), _sds(mod.shape, F32), _sds((1, d), F32)],
        (dout, x_lat) + ((x_ctx,) if tc else ()) + (a, b, y, mod, nw, w1, w3, w2), exch=exch)


def _mm_tn(a, b, rows, name, exch=None):
    m = a.shape[1]
    n = b.shape[1]
    bm = _div_tile(m, 1408, LANES)
    bn = _div_tile(n, 1408, LANES)
    bk = _div_tile(rows, 2304, LANES)
    nk = rows // bk

    def body(a_ref, b_ref, o_ref, acc_ref):
        k = pl.program_id(2)

        @pl.when(k == 0)
        def _():
            acc_ref[...] = jnp.zeros_like(acc_ref)

        acc_ref[...] += _dot_tn(a_ref[...], b_ref[...])

        @pl.when(k == nk - 1)
        def _():
            o_ref[...] = acc_ref[...].astype(BF16)

    (out,), got = _hosted_call(
        body, name, (m // bm, n // bn, nk),
        [pl.BlockSpec((bk, bm), lambda i, j, k: (k, i)), pl.BlockSpec((bk, bn), lambda i, j, k: (k, j))],
        [pl.BlockSpec((bm, bn), lambda i, j, k: (i, j))], [_sds((m, n), BF16)], (a, b),
        scratch=[pltpu.VMEM((bm, bn), F32)], exch=exch)
    return out if exch is None else (out, got)


def _mixin_fwd(xs, mod, nw, wp, s, nb, tm):
    t, d = xs.shape

    def body(x_ref, mod_ref, nw_ref, wp_ref, h_ref, p_ref):
        g = jnp.minimum((pl.program_id(0) * tm) // s, nb)
        shift = mod_ref[g, pl.ds(3, 1), :]
        scale = mod_ref[g, pl.ds(4, 1), :]
        x = x_ref[...]
        r = lax.rsqrt(jnp.mean(x * x, axis=-1, keepdims=True) + EPS)
        hb = ((x * r * nw_ref[...]) * (1.0 + scale) + shift).astype(BF16)
        h_ref[...] = hb
        p_ref[...] = _dot_nt(hb, wp_ref[...]).astype(BF16)

    row = lambda i: (i, 0)
    return pl.pallas_call(
        body, name="mixin_fwd", grid=(t // tm,),
        in_specs=[pl.BlockSpec((tm, d), row), _whole(mod.shape), _whole(nw.shape), _whole(wp.shape)],
        out_specs=[pl.BlockSpec((tm, d), row), pl.BlockSpec((tm, PROJ_COLS), row)],
        out_shape=[_sds((t, d), BF16), _sds((t, PROJ_COLS), BF16)], compiler_params=_params(1),
    )(xs, mod, nw, wp)


def _mixin_bwd(dp0, dpu, dpv, xs, dres, mod, nw, wp, s, nb, tm):
    t_all, d = xs.shape
    nlat = dres.shape[0] // tm

    def body(p0_ref, pu_ref, pv_ref, x_ref, dr_ref, mod_ref, nw_ref, wp_ref, dx_ref, dmod_ref, dnw_ref):
        i = pl.program_id(0)

        @pl.when(i == 0)
        def _():
            dmod_ref[...] = jnp.zeros_like(dmod_ref)
            dnw_ref[...] = jnp.zeros_like(dnw_ref)

        lat = i < nlat
        g = jnp.minimum((i * tm) // s, nb)
        scale = mod_ref[g, pl.ds(4, 1), :]
        dh = _dot(p0_ref[...], wp_ref[0:512, :])
        extra = _dot(pu_ref[...], wp_ref[512:1024, :]) + _dot(pv_ref[...], wp_ref[1024:1536, :])
        dh = dh + jnp.where(lat, extra, 0.0)
        x = x_ref[...]
        r = lax.rsqrt(jnp.mean(x * x, axis=-1, keepdims=True) + EPS)
        xh = x * r
        n = xh * nw_ref[...]
        dmod_ref[g, pl.ds(3, 1), :] += jnp.sum(dh, axis=0, keepdims=True)
        dmod_ref[g, pl.ds(4, 1), :] += jnp.sum(dh * n, axis=0, keepdims=True)
        dn = dh * (1.0 + scale)
        dnw_ref[...] += jnp.sum(dn * xh, axis=0, keepdims=True)
        dxh = dn * nw_ref[...]
        dx_ref[...] = jnp.where(lat, dr_ref[...], 0.0) + r * (dxh - xh * jnp.mean(dxh * xh, axis=-1, keepdims=True))

    row = lambda i: (i, 0)
    lrow = lambda i: (jnp.minimum(i, nlat - 1), 0)
    return pl.pallas_call(
        body, name="mixin_bwd", grid=(t_all // tm,),
        in_specs=[pl.BlockSpec((tm, 512), row), pl.BlockSpec((tm, 512), lrow), pl.BlockSpec((tm, 512), lrow),
                  pl.BlockSpec((tm, d), row), pl.BlockSpec((tm, d), lrow), _whole(mod.shape), _whole(nw.shape),
                  _whole(wp.shape)],
        out_specs=[pl.BlockSpec((tm, d), row), pl.BlockSpec(mod.shape, lambda i: (0, 0, 0)),
                   pl.BlockSpec((1, d), lambda i: (0, 0))],
        out_shape=[_sds((t_all, d), F32), _sds(mod.shape, F32), _sds((1, d), F32)], compiler_params=_params(1),
    )(dp0, dpu, dpv, xs, dres, mod, nw, wp)


def _prep_fwd(proj, row0, nb, s, pos0, sk, key0, into, tabs, wq, wk, wv, kvaw, qaw, qnw, knw, tm, with_q, name):
    nblk = s // tm
    n_into = 0 if into is None else 2

    def body(p_ref, cos_ref, sa_ref, sb_ref, wq_ref, wk_ref, wv_ref, kvaw_ref, qaw_ref, qnw_ref, knw_ref, *rest):
        outs, heads_ref = rest[n_into:-1], rest[-1]
        q_ref, k_ref, v_ref = outs if with_q else (None,) + outs
        cos, sin_a, sin_b = cos_ref[...][None], sa_ref[...][None], sb_ref[...][None]

        def normed_roped(w_ref, src, extra, nw_ref, o_ref, post):
            for h in range(HEADS):
                heads_ref[h] = _dot_nt(src, w_ref[h]) if extra is None else _dot(src, w_ref[h])
            xp = heads_ref[...] if extra is None else heads_ref[...] + extra[None]
            r = lax.rsqrt(jnp.sum(xp * xp, axis=-1, keepdims=True) * (1.0 / QK_HEAD) + EPS)
            o_ref[...] = _rope3(xp * r * (nw_ref[...] * post)[None], cos, sin_a, sin_b).astype(BF16)

        ckv = p_ref[:, 0:128].astype(F32)
        rkv = lax.rsqrt(jnp.mean(ckv * ckv, axis=-1, keepdims=True) + EPS)
        ckvb = (ckv * rkv * kvaw_ref[...]).astype(BF16)
        normed_roped(wk_ref, ckvb, p_ref[:, 128:256].astype(F32), knw_ref, k_ref, 1.0)
        for j in range(HEADS // 2):
            v_ref[j] = _dot(ckvb, wv_ref[j]).astype(BF16)
        if with_q:
            cq = p_ref[:, 256:512].astype(F32)
            rq = lax.rsqrt(jnp.mean(cq * cq, axis=-1, keepdims=True) + EPS)
            normed_roped(wq_ref, (cq * rq * qaw_ref[...]).astype(BF16), None, qnw_ref, q_ref, SOFTMAX_SCALE)

    tab = pl.BlockSpec((tm, HEAD_PAD), lambda i: (pos0 + i % nblk, 0))
    qspec = pl.BlockSpec((None, HEADS, tm, HEAD_PAD), lambda i: (i // nblk, 0, i % nblk, 0))
    kspec = pl.BlockSpec((None, HEADS, tm, HEAD_PAD), lambda i: (i // nblk, 0, key0 + i % nblk, 0))
    vspec = pl.BlockSpec((None, HEADS // 2, tm, HEAD_PAD), lambda i: (i // nblk, 0, key0 + i % nblk, 0))
    qshape = _sds((nb, HEADS, s, HEAD_PAD), BF16)
    kshape = _sds((nb, HEADS, sk, HEAD_PAD), BF16)
    vshape = _sds((nb, HEADS // 2, sk, HEAD_PAD), BF16)
    n_q = 1 if with_q else 0
    return pl.pallas_call(
        body, name=name, grid=(nb * nblk,),
        in_specs=[pl.BlockSpec((tm, 512), lambda i: (row0 + i, 0)), tab, tab, tab, _whole(wq.shape), _whole(wk.shape),
                  _whole(wv.shape), _whole(kvaw.shape), _whole(qaw.shape), _whole(qnw.shape), _whole(knw.shape)]
        + [pl.BlockSpec(memory_space=pl.ANY)] * n_into,
        out_specs=([qspec] if with_q else []) + [kspec, vspec],
        out_shape=([qshape] if with_q else []) + [kshape, vshape],
        scratch_shapes=[pltpu.VMEM((HEADS, tm, HEAD_PAD), F32)],
        input_output_aliases={11: n_q, 12: n_q + 1} if n_into else {}, compiler_params=_params(1),
    )(proj, *tabs, wq, wk, wv, kvaw, qaw, qnw, knw, *(into or ()))


def _prep_bwd(proj, row0, nb, s, pos0, key0, dp_rows, dp_into, tabs, wq, wk, wv, kvaw, qaw, qnw, knw, dq, dk, dv, init, tm,
              name):
    nblk = s // tm
    with_q = dq is not None
    n_init = 0 if init is None else len(init)
    n_into = 0 if dp_into is None else 1

    def body(*refs):
        p_ref, cos_ref, sa_ref, sb_ref, wq_ref, wk_ref, wv_ref, kvaw_ref, qaw_ref, qnw_ref, knw_ref = refs[:11]
        rest = list(refs[11:])
        dq_ref = rest.pop(0) if with_q else None
        dk_ref, dv_ref = rest.pop(0), rest.pop(0)
        init_refs = [rest.pop(0) for _ in range(n_init)]
        if n_into:
            rest.pop(0)
        dp_ref = rest.pop(0)
        if with_q:
            dwq_ref, dqaw_ref, dqnw_ref = rest.pop(0), rest.pop(0), rest.pop(0)
        dwk_ref, dwv_ref, dkvaw_ref, dknw_ref, heads_ref, dhb_ref, dkr_ref = rest
        accs = [dwk_ref, dwv_ref, dkvaw_ref, dknw_ref]

        @pl.when(pl.program_id(0) == 0)
        def _():
            for k, acc in enumerate(accs):
                acc[...] = init_refs[k][...] if n_init else jnp.zeros_like(acc)
            if with_q:
                dwq_ref[...] = jnp.zeros_like(dwq_ref)
                dqaw_ref[...] = jnp.zeros_like(dqaw_ref)
                dqnw_ref[...] = jnp.zeros_like(dqnw_ref)

        cos, sin_a, sin_b = cos_ref[...][None], sa_ref[...][None], sb_ref[...][None]
        lane = lax.broadcasted_iota(jnp.int32, (tm, HEAD_PAD), 1)
        rope_lanes = (lane >= QK_NOPE) & (lane < QK_HEAD)

        def heads_bwd(w_ref, src, extra, nw_ref, d_ref, dnw_ref, dw_ref, post):
            w_t = extra is None
            for h in range(HEADS):
                heads_ref[h] = _dot_nt(src, w_ref[h]) if w_t else _dot(src, w_ref[h])
            xp = heads_ref[...] if extra is None else heads_ref[...] + extra[None]
            r = lax.rsqrt(jnp.sum(xp * xp, axis=-1, keepdims=True) * (1.0 / QK_HEAD) + EPS)
            xh = xp * r
            dn = _rope3_t(d_ref[...], cos, sin_a, sin_b)
            dnw_ref[...] += post * jnp.sum(jnp.sum(dn * xh, axis=0), axis=0, keepdims=True)
            dxh = dn * (nw_ref[...] * post)[None]
            dxp = r * (dxh - xh * (jnp.sum(dxh * xh, axis=-1, keepdims=True) * (1.0 / QK_HEAD)))
            dhb_ref[...] = dxp.astype(BF16)
            dsrc = jnp.zeros((tm, src.shape[1]), F32)
            for h in range(HEADS):
                dsrc = dsrc + (_dot(dhb_ref[h], w_ref[h]) if w_t else _dot_nt(dhb_ref[h], w_ref[h]))
                dw_ref[h] += _dot_tn(src, dhb_ref[h])
            return dsrc, jnp.sum(dxp, axis=0)

        ckv = p_ref[:, 0:128].astype(F32)
        rkv = lax.rsqrt(jnp.mean(ckv * ckv, axis=-1, keepdims=True) + EPS)
        ckvh = ckv * rkv
        ckvb = (ckvh * kvaw_ref[...]).astype(BF16)
        for h in range(HEADS):
            dkr_ref[h] = dk_ref[h].T
        dckv, dkp_sum = heads_bwd(wk_ref, ckvb, p_ref[:, 128:256].astype(F32), knw_ref, dkr_ref, dknw_ref, dwk_ref,
                                  1.0)
        for j in range(HEADS // 2):
            dvb = dv_ref[j].T.astype(BF16)
            dckv = dckv + _dot_nt(dvb, wv_ref[j])
            dwv_ref[j] += _dot_tn(ckvb, dvb)
        dkvaw_ref[...] += jnp.sum(dckv * ckvh, axis=0, keepdims=True)
        dch = dckv * kvaw_ref[...]
        dp_ref[:, 0:128] = (rkv * (dch - ckvh * jnp.mean(dch * ckvh, axis=-1, keepdims=True))).astype(BF16)
        dp_ref[:, 128:256] = jnp.where(rope_lanes, dkp_sum, 0.0).astype(BF16)
        if with_q:
            cq = p_ref[:, 256:512].astype(F32)
            rq = lax.rsqrt(jnp.mean(cq * cq, axis=-1, keepdims=True) + EPS)
            cqh = cq * rq
            cqb = (cqh * qaw_ref[...]).astype(BF16)
            dcq, _ = heads_bwd(wq_ref, cqb, None, qnw_ref, dq_ref, dqnw_ref, dwq_ref, SOFTMAX_SCALE)
            dqaw_ref[...] += jnp.sum(dcq * cqh, axis=0, keepdims=True)
            dqc = dcq * qaw_ref[...]
            dp_ref[:, 256:512] = (rq * (dqc - cqh * jnp.mean(dqc * cqh, axis=-1, keepdims=True))).astype(BF16)
        else:
            dp_ref[:, 256:512] = jnp.zeros((tm, Q_LORA), BF16)

    tab = pl.BlockSpec((tm, HEAD_PAD), lambda i: (pos0 + i % nblk, 0))
    qspec = pl.BlockSpec((None, HEADS, tm, HEAD_PAD), lambda i: (i // nblk, 0, i % nblk, 0))
    kspec = pl.BlockSpec((None, HEADS, HEAD_PAD, tm), lambda i: (i // nblk, 0, 0, key0 + i % nblk))
    vspec = pl.BlockSpec((None, HEADS // 2, HEAD_PAD, tm), lambda i: (i // nblk, 0, 0, key0 + i % nblk))

    def acc_spec(shape):
        nd = len(shape)
        return pl.BlockSpec(shape, lambda i: (0,) * nd)

    acc_shapes = [(HEADS, KV_LORA, HEAD_PAD), (HEADS // 2, KV_LORA, HEAD_PAD), (1, KV_LORA), (1, HEAD_PAD)]
    q_shapes = [(HEADS, Q_LORA, HEAD_PAD), (1, Q_LORA), (1, HEAD_PAD)] if with_q else []
    out_shapes = [(dp_rows, 512)] + q_shapes + acc_shapes
    n_before = 11 + (1 if with_q else 0) + 2 + n_init
    return pl.pallas_call(
        body, name=name, grid=(nb * nblk,),
        in_specs=[pl.BlockSpec((tm, 512), lambda i: (row0 + i, 0)), tab, tab, tab, _whole(wq.shape), _whole(wk.shape),
                  _whole(wv.shape), _whole(kvaw.shape), _whole(qaw.shape), _whole(qnw.shape), _whole(knw.shape)]
        + ([qspec] if with_q else []) + [kspec, vspec] + [_whole(a.shape) for a in (init or [])]
        + [pl.BlockSpec(memory_space=pl.ANY)] * n_into,
        out_specs=[pl.BlockSpec((tm, 512), lambda i: (row0 + i, 0))] + [acc_spec(sh) for sh in q_shapes + acc_shapes],
        out_shape=[_sds(out_shapes[0], BF16)] + [_sds(sh, F32) for sh in out_shapes[1:]],
        scratch_shapes=[pltpu.VMEM((HEADS, tm, HEAD_PAD), F32), pltpu.VMEM((HEADS, tm, HEAD_PAD), BF16),
                        pltpu.VMEM((HEADS, tm, HEAD_PAD), F32)],
        input_output_aliases={n_before: 0} if n_into else {}, compiler_params=_params(1),
    )(proj, *tabs, wq, wk, wv, kvaw, qaw, qnw, knw, *([dq] if with_q else []), dk, dv, *(init or []),
      *([dp_into] if n_into else []))


def _attn_fwd(q, k, v, tq, exch=None):
    nb, _, s, _ = q.shape
    sk = k.shape[2]
    nq = s // tq

    def body(q_ref, k_ref, v_ref, o_ref, lse_ref, vext_ref):
        @pl.when(pl.program_id(2) == 0)
        def _():
            vext_ref[:, 0:HEAD_PAD] = v_ref[...]
            vext_ref[:, HEAD_PAD:2 * HEAD_PAD] = jnp.ones((sk, HEAD_PAD), BF16)

        lane = lax.broadcasted_iota(jnp.int32, (tq, HEAD_PAD), 1)
        outs = []
        for hh in range(2):
            sc = _dot_nt(q_ref[hh], k_ref[hh])
            m = jnp.max(sc, axis=-1, keepdims=True)
            pv = _dot(jnp.exp2(sc - m).astype(BF16), vext_ref[...])
            l = pv[:, HEAD_PAD:HEAD_PAD + 1]
            outs.append(pv[:, 0:HEAD_PAD] / l)
            lse_ref[hh] = jnp.transpose(jnp.broadcast_to(m + jnp.log2(l), (tq, HEAD_PAD)))[0:1, :]
        o_ref[...] = jnp.where(lane < V_HEAD, outs[0], outs[1]).astype(BF16)

    (o, lse), got = _hosted_call(
        body, "attn_fwd", (nb, HEADS // 2, nq),
        [pl.BlockSpec((None, 2, tq, HEAD_PAD), lambda b, j, i: (b, j, i, 0)),
         pl.BlockSpec((None, 2, sk, HEAD_PAD), lambda b, j, i: (b, j, 0, 0)),
         pl.BlockSpec((None, None, sk, HEAD_PAD), lambda b, j, i: (b, j, 0, 0))],
        [pl.BlockSpec((tq, HEAD_PAD), lambda b, j, i: (b * nq + i, j)),
         pl.BlockSpec((None, 2, 1, tq), lambda b, j, i: (b, j, 0, i))],
        [_sds((nb * s, MLA_W), BF16), _sds((nb, HEADS, 1, s), F32)], (q, k, v),
        scratch=[pltpu.VMEM((sk, 2 * HEAD_PAD), BF16)], exch=exch)
    return o, lse, got


def _attn_bwd(q, k, v, do, o, lse, tq, exch=None):
    nb, _, s, _ = q.shape
    sk = k.shape[2]
    nq = s // tq

    def body(q_ref, k_ref, v_ref, do_ref, o_ref, lse_ref, dq_ref, dkt_ref, dvt_ref):
        @pl.when(pl.program_id(2) == 0)
        def _():
            dkt_ref[...] = jnp.zeros_like(dkt_ref)
            dvt_ref[...] = jnp.zeros_like(dvt_ref)

        lane = lax.broadcasted_iota(jnp.int32, (tq, HEAD_PAD), 1)
        dov = do_ref[...]
        prod = dov.astype(F32) * o_ref[...].astype(F32)
        for hh in range(2):
            mine = (lane < V_HEAD) if hh == 0 else (lane >= V_HEAD)
            doh = jnp.where(mine, dov, jnp.zeros_like(dov))
            delta = jnp.sum(jnp.where(mine, prod, 0.0), axis=-1, keepdims=True)
            qh = q_ref[hh]
            q_ln2 = (qh.astype(F32) * LN2).astype(BF16)
            kv = k_ref[hh]
            lse_h = jnp.transpose(jnp.broadcast_to(lse_ref[hh], (HEAD_PAD, tq)))[:, 0:1]
            p = jnp.exp2(_dot_nt(qh, kv) - lse_h)
            u = (p * (_dot_nt(doh, v_ref[...]) - delta)).astype(BF16)
            dq_ref[hh] = _dot(u, kv) * LN2
            dkt_ref[hh] += _dot_tn(q_ln2, u)
            dvt_ref[...] += _dot_tn(doh, p.astype(BF16))

    qspec = pl.BlockSpec((None, 2, tq, HEAD_PAD), lambda b, j, i: (b, j, i, 0))
    kspec = pl.BlockSpec((None, 2, sk, HEAD_PAD), lambda b, j, i: (b, j, 0, 0))
    vspec = pl.BlockSpec((None, None, sk, HEAD_PAD), lambda b, j, i: (b, j, 0, 0))
    ospec = pl.BlockSpec((tq, HEAD_PAD), lambda b, j, i: (b * nq + i, j))
    return _hosted_call(
        body, "attn_bwd", (nb, HEADS // 2, nq),
        [qspec, kspec, vspec, ospec, ospec, pl.BlockSpec((None, 2, 1, tq), lambda b, j, i: (b, j, 0, i))],
        [qspec, pl.BlockSpec((None, 2, HEAD_PAD, sk), lambda b, j, i: (b, j, 0, 0)),
         pl.BlockSpec((None, None, HEAD_PAD, sk), lambda b, j, i: (b, j, 0, 0))],
        [_sds(q.shape, F32), _sds((nb, HEADS, HEAD_PAD, sk), F32), _sds((nb, HEADS // 2, HEAD_PAD, sk), F32)],
        (q, k, v, do, o, lse), exch=exch)


def _group_masks(rows):
    lane = lax.broadcasted_iota(jnp.int32, (rows, GMLP_W), 1)
    return [(lane >= g * GROUP_DIM) & (lane < (g + 1) * GROUP_DIM) for g in range(GROUPS)]


def _gmlp_fwd(proj, t, wcat, bias, vnw, ones, tm):
    def body(u_ref, v_ref, wcat_ref, bias_ref, vnw_ref, ones_ref, o_ref):
        masks = _group_masks(CHUNK)
        gv = _gelu(v_ref[...].astype(F32))
        rv = lax.rsqrt(_group_sum(gv * gv, ones_ref) * (1.0 / GROUP_DIM) + EPS)
        vnb = (gv * rv * vnw_ref[...]).astype(BF16)
        for c in range(tm // CHUNK):
            rows = slice(c * CHUNK, (c + 1) * CHUNK)
            vc = vnb[rows]
            stack = jnp.concatenate([jnp.where(m, vc, jnp.zeros_like(vc)) for m in masks], axis=0)
            sp = _dot(wcat_ref[...], stack) + bias_ref[...]
            o_ref[rows, :] = (_gelu(u_ref[rows, :].astype(F32)) * sp).astype(BF16)

    return pl.pallas_call(
        body, name="gmlp_fwd", grid=(t // tm,),
        in_specs=[pl.BlockSpec((tm, GMLP_W), lambda i: (i, 1)), pl.BlockSpec((tm, GMLP_W), lambda i: (i, 2)),
                  _whole(wcat.shape), _whole(bias.shape), _whole(vnw.shape), _whole(ones.shape)],
        out_specs=pl.BlockSpec((tm, GMLP_W), lambda i: (i, 0)),
        out_shape=_sds((t, GMLP_W), BF16), compiler_params=_params(1),
    )(proj, proj, wcat, bias, vnw, ones)


def _gmlp_bwd(proj, dsg, wcat, wcat_t, bias, vnw, ones, tm):
    t = dsg.shape[0]

    def body(u_ref, v_ref, dsg_ref, wcat_ref, wcatt_ref, bias_ref, vnw_ref, ones_ref,
             du_ref, dv_ref, dws_ref, dbs_ref, dvnw_ref):
        @pl.when(pl.program_id(0) == 0)
        def _():
            dws_ref[...] = jnp.zeros_like(dws_ref)
            dbs_ref[...] = jnp.zeros_like(dbs_ref)
            dvnw_ref[...] = jnp.zeros_like(dvnw_ref)

        masks = _group_masks(CHUNK)
        v = v_ref[...].astype(F32)
        gv = _gelu(v)
        rv = lax.rsqrt(_group_sum(gv * gv, ones_ref) * (1.0 / GROUP_DIM) + EPS)
        xh = gv * rv
        vnb = (xh * vnw_ref[...]).astype(BF16)
        dvn_parts = []
        for c in range(tm // CHUNK):
            rows = slice(c * CHUNK, (c + 1) * CHUNK)
            vc = vnb[rows]
            stack = jnp.concatenate([jnp.where(m, vc, jnp.zeros_like(vc)) for m in masks], axis=0)
            sp = _dot(wcat_ref[...], stack) + bias_ref[...]
            u = u_ref[rows, :].astype(F32)
            dsg_c = dsg_ref[rows, :]
            du_ref[rows, :] = (dsg_c * sp * _gelu_grad(u)).astype(BF16)
            ds = dsg_c * _gelu(u)
            dstack = jnp.concatenate([jnp.where(m, ds, 0.0) for m in masks], axis=0)
            dbs_ref[...] += jnp.broadcast_to(jnp.sum(dstack, axis=-1, keepdims=True), dbs_ref.shape)
            dstb = dstack.astype(BF16)
            dvn_parts.append(_dot(wcatt_ref[...], dstb))
            dws_ref[...] += _dot_nt(dstb, vc)
        dvn = jnp.concatenate(dvn_parts, axis=0) if len(dvn_parts) > 1 else dvn_parts[0]
        dvnw_ref[...] += jnp.sum(dvn * xh, axis=0, keepdims=True)
        dxh = dvn * vnw_ref[...]
        gm = _group_sum(dxh * xh, ones_ref) * (1.0 / GROUP_DIM)
        dv_ref[...] = (rv * (dxh - xh * gm) * _gelu_grad(v)).astype(BF16)

    row = pl.BlockSpec((tm, GMLP_W), lambda i: (i, 0))
    return pl.pallas_call(
        body, name="gmlp_bwd", grid=(t // tm,),
        in_specs=[pl.BlockSpec((tm, GMLP_W), lambda i: (i, 1)), pl.BlockSpec((tm, GMLP_W), lambda i: (i, 2)), row,
                  _whole(wcat.shape), _whole(wcat_t.shape), _whole(bias.shape), _whole(vnw.shape), _whole(ones.shape)],
        out_specs=[row, row, pl.BlockSpec((GROUPS * CHUNK, CHUNK), lambda i: (0, 0)),
                   pl.BlockSpec((GROUPS * CHUNK, CHUNK), lambda i: (0, 0)), pl.BlockSpec((1, GMLP_W), lambda i: (0, 0))],
        out_shape=[_sds((t, GMLP_W), BF16), _sds((t, GMLP_W), BF16), _sds((GROUPS * CHUNK, CHUNK), F32),
                   _sds((GROUPS * CHUNK, CHUNK), F32), _sds((1, GMLP_W), F32)],
        compiler_params=_params(1),
    )(proj, proj, dsg, wcat, wcat_t, bias, vnw, ones)


def _mixout_fwd(o, sg, xs, mod, wout, s, tm):
    t = o.shape[0]
    d = xs.shape[1]

    def body(o_ref, sg_ref, x_ref, mod_ref, w_ref, x2_ref, mix_ref):
        g = (pl.program_id(0) * tm) // s
        gate = mod_ref[g, pl.ds(5, 1), :]
        mix = _dot(o_ref[...], w_ref[0:MLA_W, :]) + _dot(sg_ref[...], w_ref[MLA_W:MLA_W + GMLP_W, :])
        x2_ref[...] = x_ref[...] + gate * mix
        mix_ref[...] = mix.astype(BF16)

    row = lambda i: (i, 0)
    return pl.pallas_call(
        body, name="mixout_fwd", grid=(t // tm,),
        in_specs=[pl.BlockSpec((tm, MLA_W), row), pl.BlockSpec((tm, GMLP_W), row), pl.BlockSpec((tm, d), row),
                  _whole(mod.shape), _whole(wout.shape)],
        out_specs=[pl.BlockSpec((tm, d), row), pl.BlockSpec((tm, d), row)],
        out_shape=[_sds((t, d), F32), _sds((t, d), BF16)], compiler_params=_params(1),
    )(o, sg, xs, mod, wout)


def _mixout_bwd(dx2, mix, mod, wout, s, tm):
    t, d = dx2.shape

    def body(dx_ref, mix_ref, mod_ref, w_ref, dmix_ref, do_ref, dsg_ref, dmod_ref):
        i = pl.program_id(0)

        @pl.when(i == 0)
        def _():
            dmod_ref[...] = jnp.zeros_like(dmod_ref)

        g = (i * tm) // s
        gate = mod_ref[g, pl.ds(5, 1), :]
        dx = dx_ref[...]
        dmod_ref[g, pl.ds(5, 1), :] += jnp.sum(dx * mix_ref[...].astype(F32), axis=0, keepdims=True)
        dmb = (gate * dx).astype(BF16)
        dmix_ref[...] = dmb
        do_ref[...] = _dot_nt(dmb, w_ref[0:MLA_W, :]).astype(BF16)
        dsg_ref[...] = _dot_nt(dmb, w_ref[MLA_W:MLA_W + GMLP_W, :])

    row = lambda i: (i, 0)
    return pl.pallas_call(
        body, name="mixout_bwd", grid=(t // tm,),
        in_specs=[pl.BlockSpec((tm, d), row), pl.BlockSpec((tm, d), row), _whole(mod.shape), _whole(wout.shape)],
        out_specs=[pl.BlockSpec((tm, d), row), pl.BlockSpec((tm, MLA_W), row), pl.BlockSpec((tm, GMLP_W), row),
                   pl.BlockSpec(mod.shape, lambda i: (0, 0, 0))],
        out_shape=[_sds((t, d), BF16), _sds((t, MLA_W), BF16), _sds((t, GMLP_W), F32), _sds(mod.shape, F32)],
        compiler_params=_params(1),
    )(dx2, mix, mod, wout)


def _swap_cores(parts, name):
    n = len(parts)

    def body(*refs):
        srcs, outs, send_sems, recv_sems = refs[:n], refs[n:2 * n], refs[2 * n], refs[2 * n + 1]
        x, y, c = lax.axis_index("x"), lax.axis_index("y"), lax.axis_index("c")
        copies = [pltpu.make_async_remote_copy(
            src_ref=srcs[w], dst_ref=outs[w], send_sem=send_sems.at[w], recv_sem=recv_sems.at[w],
            device_id=(x, y, 1 - c), device_id_type=pl.DeviceIdType.MESH) for w in range(n)]
        for cp in copies:
            cp.start()
        for cp in copies:
            cp.wait()

    any_spec = pl.BlockSpec(memory_space=pl.ANY)
    return pl.pallas_call(
        body, name=name, in_specs=[any_spec] * n, out_specs=[any_spec] * n,
        out_shape=[_sds(p.shape, p.dtype) for p in parts],
        scratch_shapes=[pltpu.SemaphoreType.DMA((n,)), pltpu.SemaphoreType.DMA((n,))],
    )(*parts)


def _row_tile(r, c, mult):
    return _div_tile(r, max(mult, (1 << 18) // c), mult)


def _sum_slots(recv, name):
    _, r, c = recv.shape
    tr = _row_tile(r, c, 16)

    def body(r_ref, o_ref):
        f = lambda k: r_ref[k].astype(F32)
        o_ref[...] = ((f(0) + f(1)) + f(2)) + f(3)

    return pl.pallas_call(
        body, name=name, grid=(r // tr,),
        in_specs=[pl.BlockSpec((N_CHIPS, tr, c), lambda i: (0, i, 0))],
        out_specs=pl.BlockSpec((tr, c), lambda i: (i, 0)),
        out_shape=_sds((r, c), F32), compiler_params=_params(1),
    )(recv)


def _adamw(parts, w, m, v, name, exch=None):
    r, wd = w.shape
    tr = _row_tile(r, wd, 8)
    c1 = 1.0 / (1.0 - ADAM_B1 ** ADAM_STEP)
    c2 = 1.0 / (1.0 - ADAM_B2 ** ADAM_STEP)
    n_p = len(parts)

    def body(*refs):
        p_refs = refs[:n_p]
        w_ref, m_ref, v_ref, g_ref, d_ref, nm_ref, nv_ref = refs[n_p:]
        g = p_refs[0][...]
        for p_ref in p_refs[1:]:
            g = g + p_ref[...]
        nm = ADAM_B1 * m_ref[...] + (1.0 - ADAM_B1) * g
        nv = ADAM_B2 * v_ref[...] + (1.0 - ADAM_B2) * (g * g)
        g_ref[...] = g
        nm_ref[...] = nm
        nv_ref[...] = nv
        d_ref[...] = -ADAM_LR * ((nm * c1) / (jnp.sqrt(nv * c2) + ADAM_EPS) + ADAM_WD * w_ref[...])

    spec = pl.BlockSpec((tr, wd), lambda i: (i, 0))
    return _hosted_call(body, name, (r // tr,), [spec] * (n_p + 3), [spec] * 4, [_sds((r, wd), F32)] * 4,
                        (*parts, w, m, v), exch=exch)


def _all_peers(x, y, c):
    flips = [(dx, dy, dc) for dx in (0, 1) for dy in (0, 1) for dc in (0, 1)][1:]
    return [(1 - x if dx else x, 1 - y if dy else y, 1 - c if dc else c) for dx, dy, dc in flips]


def _first_exchange(shards, cc, w, b):
    n_w = len(shards)
    n = w.shape[1]

    def body(*refs):
        srcs, (cc_ref, w_ref, b_ref) = refs[:n_w], refs[n_w:n_w + 3]
        outs, (all_ref, tab_ref) = refs[n_w + 3:2 * n_w + 3], refs[2 * n_w + 3:2 * n_w + 5]
        (part_ref, ici_send, ici_recv, d2d_send, d2d_recv, local_sems, cc_send, cc_recv, tab_send,
         tab_recv) = refs[2 * n_w + 5:]
        x, y, c = lax.axis_index("x"), lax.axis_index("y"), lax.axis_index("c")
        chip, dev = 2 * x + y, 4 * x + 2 * y + c
        chips = _other_chips(x, y)
        peers = _all_peers(x, y, c)

        def half(wi, which):
            hr = shards[wi].shape[0] // 2
            return pl.ds(pl.multiple_of(which * hr, 16), hr)

        def over_ici(wi, k, arriving):
            px, py = chips[k]
            slot = 2 * px + py if arriving else chip
            return pltpu.make_async_remote_copy(
                src_ref=srcs[wi].at[half(wi, c)], dst_ref=outs[wi].at[slot, half(wi, c)],
                send_sem=ici_send.at[3 * wi + k], recv_sem=ici_recv.at[3 * wi + k], device_id=(px, py, c),
                device_id_type=pl.DeviceIdType.MESH)

        def to_sibling(wi, k, arriving):
            px, py = chips[k]
            rows = half(wi, 1 - c if arriving else c)
            return pltpu.make_async_remote_copy(
                src_ref=outs[wi].at[2 * px + py, rows], dst_ref=outs[wi].at[2 * px + py, rows],
                send_sem=d2d_send.at[3 * wi + k], recv_sem=d2d_recv.at[3 * wi + k], device_id=(x, y, 1 - c),
                device_id_type=pl.DeviceIdType.MESH)

        def cc_copy(k, peer, slot):
            return pltpu.make_async_remote_copy(
                src_ref=cc_ref, dst_ref=all_ref.at[slot], send_sem=cc_send.at[k], recv_sem=cc_recv.at[k],
                device_id=peer, device_id_type=pl.DeviceIdType.MESH)

        def rows_of(px, py):
            return part_ref.at[pl.ds(pl.multiple_of((4 * px + 2 * py + c) * MOD_ROWS, MOD_ROWS), MOD_ROWS)]

        def tab_copy(k, px, py, slot):
            return pltpu.make_async_remote_copy(
                src_ref=rows_of(px, py), dst_ref=tab_ref.at[slot], send_sem=tab_send.at[k], recv_sem=tab_recv.at[k],
                device_id=(px, py, c), device_id_type=pl.DeviceIdType.MESH)

        local = [pltpu.make_async_copy(srcs[wi], outs[wi].at[chip], local_sems.at[wi]) for wi in range(n_w)]
        for cp in local:
            cp.start()
        pairs = [(wi, k) for wi in range(n_w) for k in range(3)]
        for wi, k in pairs:
            over_ici(wi, k, False).start()
        for k, peer in enumerate(peers):
            cc_copy(k, peer, dev).start()
        all_ref[dev] = cc_ref[...]
        for k, (px, py, pc) in enumerate(peers):
            cc_copy(k, (px, py, pc), 4 * px + 2 * py + pc).wait_recv()
        cv = all_ref[...].reshape(8 * MOD_ROWS, cc.shape[1])
        part_ref[...] = _dot((cv * _sigmoid(cv)).astype(BF16), w_ref[...]) + b_ref[...]
        for k, (px, py) in enumerate(chips):
            tab_copy(k, px, py, chip).start()
        tab_ref[chip] = rows_of(x, y)[...]
        for k, (px, py) in enumerate(chips):
            tab_copy(k, px, py, 2 * px + py).wait_recv()
        for wi, k in pairs:
            over_ici(wi, k, True).wait_recv()
            to_sibling(wi, k, False).start()
        for wi, k in pairs:
            to_sibling(wi, k, True).wait_recv()
        for wi, k in pairs:
            over_ici(wi, k, False).wait_send()
            to_sibling(wi, k, False).wait_send()
        for k, peer in enumerate(peers):
            cc_copy(k, peer, dev).wait_send()
        for k, (px, py) in enumerate(chips):
            tab_copy(k, px, py, chip).wait_send()
        for cp in local:
            cp.wait()

    any_spec = pl.BlockSpec(memory_space=pl.ANY)
    vmem = pl.BlockSpec(memory_space=pltpu.VMEM)
    sems3 = pltpu.SemaphoreType.DMA((3 * n_w,))
    got = pl.pallas_call(
        body, name="first_exchange", in_specs=[any_spec] * n_w + [vmem] * 3, out_specs=[any_spec] * n_w + [vmem] * 2,
        out_shape=_exch_shapes("gather", shards) + [_sds((8,) + cc.shape, F32), _sds((N_CHIPS, MOD_ROWS, n), F32)],
        scratch_shapes=[pltpu.VMEM((8 * MOD_ROWS, n), F32), sems3, sems3, sems3, sems3, pltpu.SemaphoreType.DMA((n_w,)),
                        pltpu.SemaphoreType.DMA((7,)), pltpu.SemaphoreType.DMA((7,)), pltpu.SemaphoreType.DMA((3,)),
                        pltpu.SemaphoreType.DMA((3,))],
        compiler_params=pltpu.CompilerParams(vmem_limit_bytes=V7X_VMEM_LIMIT),
    )(*shards, cc, w, b)
    return got[:n_w], got[n_w], got[n_w + 1]


def _ada_bwd_tp(cc_all, dmods, w, ctx_row):
    d, n = w.shape

    def body(cc_ref, m0, m1, m2, m3, w_ref, dw_ref, db_ref, dctx_ref, stage_ref, all_ref, send_sems, recv_sems):
        x, y, c = lax.axis_index("x"), lax.axis_index("y"), lax.axis_index("c")
        me = 4 * x + 2 * y + c
        dsum = m0[...] + m1[...] + m2[...] + m3[...]
        db_ref[...] = jnp.sum(dsum, axis=0, keepdims=True)
        for j in range(N_CHIPS):
            stage_ref[j] = dsum[:, j * n:(j + 1) * n]

        def copy(k, peer, slot):
            px, py, _ = peer
            return pltpu.make_async_remote_copy(
                src_ref=stage_ref.at[2 * px + py], dst_ref=all_ref.at[slot], send_sem=send_sems.at[k],
                recv_sem=recv_sems.at[k], device_id=peer, device_id_type=pl.DeviceIdType.MESH)

        peers = _all_peers(x, y, c)
        for k, peer in enumerate(peers):
            copy(k, peer, me).start()
        all_ref[me] = stage_ref[2 * x + y]
        for k, (px, py, pc) in enumerate(peers):
            copy(k, (px, py, pc), 4 * px + 2 * py + pc).wait_recv()
        for k, peer in enumerate(peers):
            copy(k, peer, me).wait_send()
        cv = cc_ref[...]
        sig = _sigmoid(cv)
        dmb = all_ref[...].reshape(8 * MOD_ROWS, n).astype(BF16)
        dw_ref[...] = _dot_tn((cv * sig).astype(BF16), dmb)
        dsc = _dot_nt(dmb, w_ref[...])
        dctx = dsc[ctx_row:ctx_row + 1, :]
        for dev in range(1, 8):
            dctx = dctx + dsc[dev * MOD_ROWS + ctx_row:dev * MOD_ROWS + ctx_row + 1, :]
        cx = cv[ctx_row:ctx_row + 1, :]
        sx = sig[ctx_row:ctx_row + 1, :]
        dctx_ref[...] = dctx * (sx * (1.0 + cx * (1.0 - sx))) * jnp.where(c == 0, 1.0, 0.0)

    vmem = pl.BlockSpec(memory_space=pltpu.VMEM)
    return pl.pallas_call(
        body, name="ada_bwd_tp", in_specs=[vmem] * 6, out_specs=[vmem] * 3,
        out_shape=[_sds((d, n), F32), _sds((1, N_MOD * d), F32), _sds((1, d), F32)],
        scratch_shapes=[pltpu.VMEM((N_CHIPS, MOD_ROWS, n), F32), pltpu.VMEM((8, MOD_ROWS, n), F32),
                        pltpu.SemaphoreType.DMA((7,)), pltpu.SemaphoreType.DMA((7,))],
        compiler_params=pltpu.CompilerParams(vmem_limit_bytes=V7X_VMEM_LIMIT),
    )(cc_all, *dmods, w)


def _rope_tables(s, ctx):
    pos = np.arange(s, dtype=np.float32)
    inv = (np.float32(ROPE_BASE) ** (-np.arange(0, QK_ROPE // 2, 2, dtype=np.float32) / np.float32(QK_ROPE // 2)))
    ang_r = np.floor(pos / GRID_W)[:, None] * inv
    ang_c = (pos - GRID_W * np.floor(pos / GRID_W))[:, None] * inv
    ang = np.concatenate([ang_r, ang_r, ang_c, ang_c], axis=-1).astype(np.float32)
    cos, sin = np.cos(ang), np.sin(ang)
    half_b = (np.arange(QK_ROPE) // 8) % 2 == 1
    sin_a = np.where(half_b, sin, 0.0)
    sin_b = np.where(half_b, 0.0, -sin)

    def place(tab, fill):
        full = np.full((s + ctx, HEAD_PAD), fill, np.float32)
        full[:s, QK_NOPE:QK_HEAD] = tab
        return jnp.asarray(full)

    return place(cos, 1.0), place(sin_a, 0.0), place(sin_b, 0.0)


def _pad_last(a, n):
    return jnp.pad(a, [(0, 0)] * (a.ndim - 1) + [(0, n - a.shape[-1])])


def _flat_rows(parts, rows, width):
    flat = jnp.concatenate([p.reshape(-1) for p in parts])
    return jnp.pad(flat, (0, rows * width - flat.shape[0])).reshape(rows, width)


def kernel(x, c, ctx, c_ctx, w_ada, b_ada, norm1_w, ffn1_w1, ffn1_w3, ffn1_w2, norm2_w, w_in, q_a_norm_w, w_uq, kv_a_norm_w, w_ukv, q_norm_w, k_norm_w, v_norm_w, w_s, b_s, w_out, norm3_w, ffn2_w1, ffn2_w3, ffn2_w2, loss_target, m_c_ctx, m_w_ada, m_b_ada, m_norm1_w, m_ffn1_w1, m_ffn1_w3, m_ffn1_w2, m_norm2_w, m_w_in, m_q_a_norm_w, m_w_uq, m_kv_a_norm_w, m_w_ukv, m_q_norm_w, m_k_norm_w, m_v_norm_w, m_w_s, m_b_s, m_w_out, m_norm3_w, m_ffn2_w1, m_ffn2_w3, m_ffn2_w2, v_c_ctx, v_w_ada, v_b_ada, v_norm1_w, v_ffn1_w1, v_ffn1_w3, v_ffn1_w2, v_norm2_w, v_w_in, v_q_a_norm_w, v_w_uq, v_kv_a_norm_w, v_w_ukv, v_q_norm_w, v_k_norm_w, v_v_norm_w, v_w_s, v_b_s, v_w_out, v_norm3_w, v_ffn2_w1, v_ffn2_w3, v_ffn2_w2):
    wts = dict(c_ctx=c_ctx, w_ada=w_ada, b_ada=b_ada, norm1_w=norm1_w, ffn1_w1=ffn1_w1, ffn1_w3=ffn1_w3, ffn1_w2=ffn1_w2,
               norm2_w=norm2_w, w_in=w_in, q_a_norm_w=q_a_norm_w, w_uq=w_uq, kv_a_norm_w=kv_a_norm_w, w_ukv=w_ukv,
               q_norm_w=q_norm_w, k_norm_w=k_norm_w, v_norm_w=v_norm_w, w_s=w_s, b_s=b_s, w_out=w_out, norm3_w=norm3_w,
               ffn2_w1=ffn2_w1, ffn2_w3=ffn2_w3, ffn2_w2=ffn2_w2)
    moms = dict(c_ctx=m_c_ctx, w_ada=m_w_ada, b_ada=m_b_ada, norm1_w=m_norm1_w, ffn1_w1=m_ffn1_w1, ffn1_w3=m_ffn1_w3,
                ffn1_w2=m_ffn1_w2, norm2_w=m_norm2_w, w_in=m_w_in, q_a_norm_w=m_q_a_norm_w, w_uq=m_w_uq,
                kv_a_norm_w=m_kv_a_norm_w, w_ukv=m_w_ukv, q_norm_w=m_q_norm_w, k_norm_w=m_k_norm_w, v_norm_w=m_v_norm_w,
                w_s=m_w_s, b_s=m_b_s, w_out=m_w_out, norm3_w=m_norm3_w, ffn2_w1=m_ffn2_w1, ffn2_w3=m_ffn2_w3,
                ffn2_w2=m_ffn2_w2)
    vars_ = dict(c_ctx=v_c_ctx, w_ada=v_w_ada, b_ada=v_b_ada, norm1_w=v_norm1_w, ffn1_w1=v_ffn1_w1, ffn1_w3=v_ffn1_w3,
                 ffn1_w2=v_ffn1_w2, norm2_w=v_norm2_w, w_in=v_w_in, q_a_norm_w=v_q_a_norm_w, w_uq=v_w_uq,
                 kv_a_norm_w=v_kv_a_norm_w, w_ukv=v_w_ukv, q_norm_w=v_q_norm_w, k_norm_w=v_k_norm_w, v_norm_w=v_v_norm_w,
                 w_s=v_w_s, b_s=v_b_s, w_out=v_w_out, norm3_w=v_norm3_w, ffn2_w1=v_ffn2_w1, ffn2_w3=v_ffn2_w3,
                 ffn2_w2=v_ffn2_w2)

    nb, s, d = x.shape
    nctx = ctx.shape[1]
    t, tc = nb * s, nb * nctx
    t_all = t + tc
    sk = s + nctx
    assert nb + 1 <= MOD_ROWS and d % LANES == 0
    tm = _token_tile(s, nctx)
    tq = _div_tile(s, 512, tm)
    tmx = _div_tile(math.gcd(s, tc), 512, tm)

    def held(n, a_):
        return jnp.swapaxes(a_[0], 0, 1) if n in T_WEIGHTS else a_[0]

    def unheld(n, a_):
        return (jnp.swapaxes(a_, 0, 1) if n in T_WEIGHTS else a_)[None]

    shard = {n: held(n, wts[n]).astype(BF16) for n in SHARDED}
    full = {}

    def unshard(names, blocks):
        for n, g4 in zip(names, blocks):
            _, r_, c_ = g4.shape
            if n in ROW_SHARDED or n in T_WEIGHTS:
                full[n] = g4.reshape(N_CHIPS * r_, c_)
            else:
                full[n] = g4.transpose(1, 0, 2).reshape(r_, N_CHIPS * c_)

    def chip_major(n, g_):
        if n in ROW_SHARDED or n in T_WEIGHTS:
            return g_.reshape(N_CHIPS, g_.shape[0] // N_CHIPS, g_.shape[1]).astype(BF16)
        r_, cols = g_.shape
        return g_.reshape(r_, N_CHIPS, cols // N_CHIPS).transpose(1, 0, 2).astype(BF16)

    cc = jnp.concatenate([c, c_ctx[None, :], jnp.zeros((MOD_ROWS - nb - 1, d), F32)], axis=0)
    n_ada = shard["w_ada"].shape[1]
    assert n_ada % LANES == 0
    my_chip = 2 * lax.axis_index("x") + lax.axis_index("y")
    b_cols = lax.dynamic_slice_in_dim(b_ada, my_chip * n_ada, n_ada, axis=1)
    got, cc_all, table = _first_exchange([shard[n] for n in FIRST_WEIGHTS], cc, shard["w_ada"], b_cols)
    unshard(FIRST_WEIGHTS, got)
    cc_all = cc_all.reshape(8 * MOD_ROWS, d)
    mod = table.transpose(1, 0, 2).reshape(MOD_ROWS, N_MOD, d)
    wsb = w_s[0].astype(BF16)
    wcat = wsb.transpose(1, 0, 2).reshape(CHUNK, GROUPS * CHUNK)
    wcat_t = wsb.transpose(2, 0, 1).reshape(CHUNK, GROUPS * CHUNK)
    bias = jnp.repeat(b_s[0].T, GROUP_DIM, axis=1)
    vnw = v_norm_w.reshape(1, GMLP_W)
    lane = jnp.arange(GMLP_W)
    ones = (lane[:, None] // GROUP_DIM == lane[None, :] // GROUP_DIM).astype(BF16)
    qnw = _pad_last(q_norm_w, HEAD_PAD)
    knw = _pad_last(k_norm_w, HEAD_PAD)
    tabs = _rope_tables(s, nctx)

    x_lat, x_ctx = x.reshape(t, d), ctx.reshape(tc, d)
    (xs1, a1, b1, y1), got = _ffn_fwd(x_lat, x_ctx, mod, norm1_w, full["ffn1_w1"], full["ffn1_w3"], full["ffn1_w2"], 0, s,
                                      nb, tm, "ffn1_fwd", exch=("gather", [shard[n] for n in MIX_WEIGHTS]))
    unshard(MIX_WEIGHTS, got)
    wi = full["w_in"]
    wp = jnp.concatenate([wi[0:KV_LORA], jnp.zeros((QK_NOPE, d), BF16), wi[KV_LORA:KV_LORA + QK_ROPE],
                          jnp.zeros((HEAD_PAD - QK_HEAD, d), BF16), wi[KV_LORA + QK_ROPE:]], axis=0)
    wq = jnp.pad(full["w_uq"].reshape(HEADS, QK_HEAD, Q_LORA), ((0, 0), (0, HEAD_PAD - QK_HEAD), (0, 0)))
    wkv = full["w_ukv"].reshape(KV_LORA, HEADS, QK_NOPE + V_HEAD)
    wk = _pad_last(wkv[:, :, :QK_NOPE].transpose(1, 0, 2), HEAD_PAD)
    wv = wkv[:, :, QK_NOPE:].reshape(KV_LORA, HEADS // 2, 2 * V_HEAD).transpose(1, 0, 2)
    h2, proj = _mixin_fwd(xs1, mod, norm2_w, wp, s, nb, tmx)
    prep_w = (wq, wk, wv, kv_a_norm_w, q_a_norm_w, qnw, knw)
    q, k_all, v_all = _prep_fwd(proj, 0, nb, s, 0, sk, 0, None, tabs, *prep_w, tq, True, "prep_fwd")
    k_all, v_all = _prep_fwd(proj, t // tm, nb, nctx, s // tm, sk, s // tm, (k_all, v_all), tabs, *prep_w, tm, False,
                             "prep_ctx_fwd")
    o, lse, got = _attn_fwd(q, k_all, v_all, tq, exch=("gather", [shard[n] for n in LAST_WEIGHTS]))
    unshard(LAST_WEIGHTS, got)
    sg = _gmlp_fwd(proj, t, wcat, bias, vnw, ones, tq)
    x2, mix = _mixout_fwd(o, sg, xs1, mod, full["w_out"], s, tq)
    (dy, a2, b2, y2, loss_part), _ = _ffn_fwd(x2, None, mod, norm3_w, full["ffn2_w1"], full["ffn2_w3"], full["ffn2_w2"], 6,
                                              s, nb, tm, "ffn2_fwd", target=loss_target.reshape(t, d))

    grads, cm, recv = {}, {}, {}

    def scatter_of(names):
        return ("scatter", [cm[n] for n in names])

    (dx2, h3, g2, da2, db2, dyb2, dmod_c, grads["norm3_w"]), _ = _ffn_bwd(
        dy, x2, None, a2, b2, y2, mod, norm3_w, full["ffn2_w1"], full["ffn2_w3"], full["ffn2_w2"], 6, s, nb, tm,
        "ffn2_bwd")
    cm["ffn2_w1"] = chip_major("ffn2_w1", _mm_tn(da2, h3, t, "ffn2_dw1"))
    cm["ffn2_w3"] = chip_major("ffn2_w3", _mm_tn(db2, h3, t, "ffn2_dw3"))
    cm["ffn2_w2"] = chip_major("ffn2_w2", _mm_tn(g2, dyb2, t, "ffn2_dw2"))
    dmix, do, dsg, dmod_b = _mixout_bwd(dx2, mix, mod, full["w_out"], s, tq)
    cm["w_out"] = chip_major("w_out", jnp.concatenate([_mm_tn(o, dmix, t, "wout_dw_attn"),
                                                       _mm_tn(sg, dmix, t, "wout_dw_gmlp")], axis=0))
    dpu, dpv, dws, dbs, dvnw = _gmlp_bwd(proj, dsg, wcat, wcat_t, bias, vnw, ones, tq)
    group = LAST_WEIGHTS + ("w_out",)
    (dq, dk, dv), got = _attn_bwd(q, k_all, v_all, do, o, lse, tq, exch=scatter_of(group))
    recv.update(zip(group, got))
    dp0, dwk_c, dwv_c, dkvaw_c, dknw_c = _prep_bwd(
        proj, t // tm, nb, nctx, s // tm, s // tm, t_all, None, tabs, *prep_w, None, dk, dv, None, tm, "prep_ctx_bwd")
    dp0, dwq, dqaw, dqnw, dwk, dwv, dkvaw, dknw = _prep_bwd(
        proj, 0, nb, s, 0, 0, t_all, dp0, tabs, *prep_w, dq, dk, dv, [dwk_c, dwv_c, dkvaw_c, dknw_c], tq, "prep_bwd")
    dxs1, dmod_a, grads["norm2_w"] = _mixin_bwd(dp0, dpu, dpv, xs1, dx2, mod, norm2_w, wp, s, nb, tmx)
    dwp = jnp.concatenate([_mm_tn(dp0, h2, t_all, "win_dw_kvq"), _mm_tn(dpu, h2, t, "win_dw_u"),
                           _mm_tn(dpv, h2, t, "win_dw_v")], axis=0)
    cm["w_in"] = chip_major("w_in", jnp.concatenate(
        [dwp[0:KV_LORA], dwp[KV_LORA + QK_NOPE:KV_LORA + QK_HEAD], dwp[256:]], axis=0))
    cm["w_uq"] = chip_major("w_uq", dwq[:, :, :QK_HEAD].transpose(0, 2, 1).reshape(HEADS * QK_HEAD, Q_LORA))
    cm["w_ukv"] = chip_major("w_ukv", jnp.concatenate(
        [dwk[:, :, :QK_NOPE].transpose(1, 0, 2),
         dwv.transpose(1, 0, 2).reshape(KV_LORA, HEADS, V_HEAD)], axis=2).reshape(KV_LORA, HEADS * (QK_NOPE + V_HEAD)))
    (dx_lat, h1, g1, da1, db1, dyb1, dmod_0, grads["norm1_w"]), _ = _ffn_bwd(
        dxs1, x_lat, x_ctx, a1, b1, y1, mod, norm1_w, full["ffn1_w1"], full["ffn1_w3"], full["ffn1_w2"], 0, s, nb, tm,
        "ffn1_bwd")
    dmods = [m_.reshape(MOD_ROWS, N_MOD * d) for m_ in (dmod_0, dmod_a, dmod_b, dmod_c)]
    dw_ada, grads["b_ada"], dctx = _ada_bwd_tp(cc_all, dmods, shard["w_ada"], nb)
    grads["c_ctx"] = dctx[0]
    grads["q_a_norm_w"], grads["kv_a_norm_w"] = dqaw, dkvaw
    grads["q_norm_w"], grads["k_norm_w"] = dqnw[:, :QK_HEAD], dknw[:, :QK_HEAD]
    grads["v_norm_w"], grads["w_s"], grads["b_s"] = dvnw, dws, dbs[:, 0]
    grad_x = dx_lat.reshape(nb, s, d)
    n_small = sum(wts[n].size for n in SMALL)
    rows_s = _round_up(-(-(n_small + 1) // d), 16)
    cm["small"] = jnp.broadcast_to(_flat_rows([grads[n] for n in SMALL] + [loss_part], rows_s, d), (N_CHIPS, rows_s, d))
    group = ("w_in", "w_uq", "w_ukv", "small")
    dw2, got = _mm_tn(g1, dyb1, t_all, "ffn1_dw2", exch=scatter_of(group))
    recv.update(zip(group, got))
    cm["ffn1_w2"] = chip_major("ffn1_w2", dw2)
    dw1, got = _mm_tn(da1, h1, t_all, "ffn1_dw1", exch=scatter_of(("ffn1_w2",)))
    recv["ffn1_w2"] = got[0]
    cm["ffn1_w1"] = chip_major("ffn1_w1", dw1)
    dw3, got = _mm_tn(db1, h1, t_all, "ffn1_dw3", exch=scatter_of(("ffn1_w1",)))
    recv["ffn1_w1"] = got[0]
    cm["ffn1_w3"] = chip_major("ffn1_w3", dw3)
    stepped = {}
    stepped["w_ada"], got = _adamw([dw_ada], wts["w_ada"][0], moms["w_ada"][0], vars_["w_ada"][0], "adamw_w_ada",
                                   exch=scatter_of(("ffn1_w3",)))
    recv["ffn1_w3"] = got[0]

    reduced = tuple(n for n in SHARDED if n != "w_ada") + ("small",)
    part = {n: _sum_slots(recv[n], "sum_" + n) for n in reduced}
    early = LAST_WEIGHTS + ("w_out",)
    late = tuple(n for n in reduced if n not in early)
    sib = dict(zip(early, _swap_cores([part[n] for n in early], "swap_early")))
    sib.update(zip(late, _swap_cores([part[n] for n in late], "swap_late")))
    for n in reduced[:-1]:
        stepped[n], _ = _adamw([part[n], sib[n]], held(n, wts[n]), held(n, moms[n]), held(n, vars_[n]), "adamw_" + n)
    for n in SHARDED:
        stepped[n] = [unheld(n, a_) for a_ in stepped[n]]
    packed, _ = _adamw([part["small"], sib["small"]], _flat_rows([wts[n] for n in SMALL], rows_s, d),
                       _flat_rows([moms[n] for n in SMALL], rows_s, d), _flat_rows([vars_[n] for n in SMALL], rows_s, d),
                       "adamw_small")
    loss = packed[0].reshape(-1)[n_small]
    for n in SMALL:
        stepped[n] = []
    for a_ in packed:
        flat = a_.reshape(-1)
        off = 0
        for n in SMALL:
            stepped[n].append(flat[off:off + wts[n].size].reshape(wts[n].shape))
            off += wts[n].size
    return (loss, grad_x, *[stepped[n][0] for n in WEIGHTS], *[stepped[n][1] for n in WEIGHTS],
            *[stepped[n][2] for n in WEIGHTS], *[stepped[n][3] for n in WEIGHTS])
```

```python
import functools
import math

import jax
import jax.numpy as jnp
import numpy as np
from jax import lax
from jax.experimental import pallas as pl
from jax.experimental.pallas import tpu as pltpu

F32 = jnp.float32
BF16 = jnp.bfloat16

EPS = 1e-6
N_MOD = 9
HEADS = 8
QK_NOPE, QK_ROPE, V_HEAD = 64, 32, 64
QK_HEAD = QK_NOPE + QK_ROPE
HEAD_PAD = 128
LN2 = math.log(2.0)
SOFTMAX_SCALE = QK_HEAD ** -0.5 / LN2
Q_LORA, KV_LORA = 256, 128
GROUPS, GROUP_DIM, CHUNK = 8, 64, 128
GMLP_W = GROUPS * GROUP_DIM
MLA_W = HEADS * V_HEAD
IN_COLS = 1440
PROJ_COLS = 1536
GRID_W = 64
ROPE_BASE = 10000.0
MOD_ROWS = 16
ADAM_LR, ADAM_B1, ADAM_B2, ADAM_EPS, ADAM_WD, ADAM_STEP = 0.001, 0.9, 0.999, 1e-08, 0.01, 10
N_CHIPS = 4
LANES = 128
V7X_VMEM_LIMIT = 56 * 1024 * 1024
GELU_C = math.sqrt(2.0 / math.pi)

SHARDED = ("w_ada", "ffn1_w1", "ffn1_w3", "ffn1_w2", "w_in", "w_uq", "w_ukv", "w_out", "ffn2_w1", "ffn2_w3", "ffn2_w2")
ROW_SHARDED = ("ffn1_w2", "w_out", "ffn2_w2")
T_WEIGHTS = ("ffn1_w1", "ffn1_w3", "ffn2_w1", "ffn2_w3", "w_in", "w_uq")
FIRST_WEIGHTS = ("ffn1_w1", "ffn1_w3", "ffn1_w2")
MIX_WEIGHTS = ("w_in", "w_uq", "w_ukv", "w_out")
LAST_WEIGHTS = ("ffn2_w1", "ffn2_w3", "ffn2_w2")
SMALL = ("c_ctx", "b_ada", "norm1_w", "norm2_w", "q_a_norm_w", "kv_a_norm_w", "q_norm_w", "k_norm_w", "v_norm_w",
         "w_s", "b_s", "norm3_w")
WEIGHTS = ("c_ctx", "w_ada", "b_ada", "norm1_w", "ffn1_w1", "ffn1_w3", "ffn1_w2", "norm2_w", "w_in", "q_a_norm_w",
           "w_uq", "kv_a_norm_w", "w_ukv", "q_norm_w", "k_norm_w", "v_norm_w", "w_s", "b_s", "w_out", "norm3_w",
           "ffn2_w1", "ffn2_w3", "ffn2_w2")


def _round_up(n, m):
    return (n + m - 1) // m * m


def _div_tile(n, target, mult):
    best = None
    for t in range(mult, min(n, target) + 1, mult):
        if n % t == 0:
            best = t
    return n if best is None else best


def _dot(a, b):
    return lax.dot_general(a, b, (((1,), (0,)), ((), ())), preferred_element_type=F32)


def _dot_nt(a, b):
    return lax.dot_general(a, b, (((1,), (1,)), ((), ())), preferred_element_type=F32)


def _dot_tn(a, b):
    return lax.dot_general(a, b, (((0,), (0,)), ((), ())), preferred_element_type=F32)


def _sigmoid(x):
    return 1.0 / (1.0 + jnp.exp(-x))


def _gelu(x):
    return 0.5 * x * (1.0 + jnp.tanh(GELU_C * (x + 0.044715 * x * x * x)))


def _gelu_grad(x):
    t = jnp.tanh(GELU_C * (x + 0.044715 * x * x * x))
    return 0.5 * (1.0 + t) + 0.5 * x * (1.0 - t * t) * (GELU_C * (1.0 + 3 * 0.044715 * x * x))


def _rope3(x, cos, sin_a, sin_b):
    return x * cos + pltpu.roll(x, 8, 2) * sin_a + pltpu.roll(x, HEAD_PAD - 8, 2) * sin_b


def _rope3_t(d, cos, sin_a, sin_b):
    return d * cos + pltpu.roll(d * sin_a, HEAD_PAD - 8, 2) + pltpu.roll(d * sin_b, 8, 2)


def _group_sum(x, ones_ref):
    hi = x.astype(BF16)
    lo = (x - hi.astype(F32)).astype(BF16)
    return _dot(hi, ones_ref[...]) + _dot(lo, ones_ref[...])


def _params(n_axes):
    return pltpu.CompilerParams(dimension_semantics=("arbitrary",) * n_axes, vmem_limit_bytes=V7X_VMEM_LIMIT)


def _whole(shape):
    nd = len(shape)
    return pl.BlockSpec(shape, lambda *_: (0,) * nd, pipeline_mode=pl.Buffered(1))


def _sds(shape, dtype):
    return jax.ShapeDtypeStruct(shape, dtype)


def _token_tile(s, ctx):
    return _div_tile(math.gcd(s, ctx), 256, CHUNK)


def _other_chips(x, y):
    return [(1 - x, y), (x, 1 - y), (1 - x, 1 - y)]


def _exch_copies(kind, srcs, dsts, send_sems, recv_sems, local_sems, with_arrivals):
    x, y, c = lax.axis_index("x"), lax.axis_index("y"), lax.axis_index("c")
    me = 2 * x + y
    local, sends, arrivals = [], [], []
    for w, (src, dst) in enumerate(zip(srcs, dsts)):
        own = src if kind == "gather" else src.at[me]
        local.append(pltpu.make_async_copy(own, dst.at[me], local_sems.at[w]))
        for k, (px, py) in enumerate(_other_chips(x, y)):
            sem = dict(send_sem=send_sems.at[3 * w + k], recv_sem=recv_sems.at[3 * w + k], device_id=(px, py, c),
                       device_id_type=pl.DeviceIdType.MESH)
            out = src if kind == "gather" else src.at[2 * px + py]
            sends.append(pltpu.make_async_remote_copy(src_ref=out, dst_ref=dst.at[me], **sem))
            if with_arrivals:
                arrivals.append(pltpu.make_async_remote_copy(src_ref=own, dst_ref=dst.at[2 * px + py], **sem))
    return local, sends, arrivals


def _exch_start(kind, srcs, dsts, sems):
    local, sends, _ = _exch_copies(kind, srcs, dsts, *sems, with_arrivals=False)
    for cp in local + sends:
        cp.start()


def _exch_wait(kind, srcs, dsts, sems):
    local, sends, arrivals = _exch_copies(kind, srcs, dsts, *sems, with_arrivals=True)
    for cp in arrivals:
        cp.wait_recv()
    for cp in sends:
        cp.wait_send()
    for cp in local:
        cp.wait()


def _exch_scratch(n):
    return [pltpu.SemaphoreType.DMA((3 * n,)), pltpu.SemaphoreType.DMA((3 * n,)), pltpu.SemaphoreType.DMA((n,))]


def _exch_shapes(kind, arrays):
    return [_sds((N_CHIPS,) + a.shape if kind == "gather" else a.shape, a.dtype) for a in arrays]


def _hosted_call(body, name, grid, in_specs, out_specs, out_shape, operands, scratch=(), exch=None):
    n_axes = len(grid)
    if exch is None:
        outs = pl.pallas_call(body, name=name, grid=grid, in_specs=list(in_specs), out_specs=list(out_specs),
                              out_shape=list(out_shape), scratch_shapes=list(scratch),
                              compiler_params=_params(n_axes))(*operands)
        return list(outs), []
    kind, arrays = exch
    n_in, n_out, n_sc, n_ex = len(in_specs), len(out_specs), len(scratch), len(arrays)

    def hosted(*refs):
        cin, ein = refs[:n_in], refs[n_in:n_in + n_ex]
        o0 = n_in + n_ex
        cout, eout = refs[o0:o0 + n_out], refs[o0 + n_out:o0 + n_out + n_ex]
        rest = refs[o0 + n_out + n_ex:]
        csc, sems = rest[:n_sc], rest[n_sc:]
        first = functools.reduce(jnp.logical_and, [pl.program_id(a) == 0 for a in range(n_axes)])
        last = functools.reduce(jnp.logical_and, [pl.program_id(a) == grid[a] - 1 for a in range(n_axes)])

        @pl.when(first)
        def _():
            _exch_start(kind, ein, eout, sems)

        body(*cin, *cout, *csc)

        @pl.when(last)
        def _():
            _exch_wait(kind, ein, eout, sems)

    any_spec = pl.BlockSpec(memory_space=pl.ANY)
    outs = pl.pallas_call(
        hosted, name=name, grid=grid, in_specs=list(in_specs) + [any_spec] * n_ex,
        out_specs=list(out_specs) + [any_spec] * n_ex, out_shape=list(out_shape) + _exch_shapes(kind, arrays),
        scratch_shapes=list(scratch) + _exch_scratch(n_ex), compiler_params=_params(n_axes),
    )(*operands, *arrays)
    return list(outs[:n_out]), list(outs[n_out:])


class _TokenTiles:
    def __init__(self, t, tc, tm):
        self.n_lat, self.n_ctx = t // tm, tc // tm
        self.n_all = self.n_lat + self.n_ctx

    def tile(self, i):
        return (i + self.n_lat) % self.n_all if self.n_ctx else i

    def is_lat(self, i):
        return self.tile(i) < self.n_lat

    def row(self, i):
        return (self.tile(i), 0)

    def lat_row(self, i):
        return (jnp.where(self.is_lat(i), self.tile(i), 0), 0) if self.n_ctx else (i, 0)

    def ctx_row(self, i):
        return (jnp.where(self.is_lat(i), self.n_ctx - 1, self.tile(i) - self.n_lat), 0)


def _ffn_fwd(x_lat, x_ctx, mod, nw, w1, w3, w2, k0, s, nb, tm, name, target=None, exch=None):
    t, d = x_lat.shape
    tc = 0 if x_ctx is None else x_ctx.shape[0]
    f = w1.shape[0]
    tiles = _TokenTiles(t, tc, tm)
    n_x = 2 if tc else 1
    n_t = 0 if target is None else 1
    assert not (tc and n_t)

    def body(*refs):
        x_ref = refs[0]
        t_ref = refs[n_x] if n_t else None
        mod_ref, nw_ref, w1_ref, w3_ref, w2_ref, o_ref, a_ref, b_ref, y_ref = refs[n_x + n_t:n_x + n_t + 9]
        i = pl.program_id(0)
        g = jnp.minimum((tiles.tile(i) * tm) // s, nb)
        shift = mod_ref[g, pl.ds(k0, 1), :]
        scale = mod_ref[g, pl.ds(k0 + 1, 1), :]
        gate = mod_ref[g, pl.ds(k0 + 2, 1), :]
        x = jnp.where(tiles.is_lat(i), x_ref[...], refs[1][...]) if tc else x_ref[...]
        r = lax.rsqrt(jnp.mean(x * x, axis=-1, keepdims=True) + EPS)
        hb = ((x * r * nw_ref[...]) * (1.0 + scale) + shift).astype(BF16)
        a = _dot_nt(hb, w1_ref[...])
        b = _dot_nt(hb, w3_ref[...])
        gb = (a * _sigmoid(a) * b).astype(BF16)
        y = _dot(gb, w2_ref[...])
        out = x + (0.5 * gate) * y
        a_ref[...] = a.astype(BF16)
        b_ref[...] = b.astype(BF16)
        y_ref[...] = y.astype(BF16)
        if n_t:
            loss_ref, acc_ref = refs[-2:]

            @pl.when(i == 0)
            def _():
                acc_ref[...] = jnp.zeros_like(acc_ref)

            e = out - t_ref[...]
            o_ref[...] = e * (1.0 / d)
            acc_ref[...] += jnp.sum(e * e, axis=0, keepdims=True)

            @pl.when(i == tiles.n_all - 1)
            def _():
                loss_ref[...] = (0.5 / d) * jnp.sum(acc_ref[...], axis=-1, keepdims=True)
        else:
            o_ref[...] = out

    td = pl.BlockSpec((tm, d), tiles.row)
    tf = pl.BlockSpec((tm, f), tiles.row)
    return _hosted_call(
        body, name, (tiles.n_all,),
        [pl.BlockSpec((tm, d), tiles.lat_row)] + ([pl.BlockSpec((tm, d), tiles.ctx_row)] if tc else []) + [td] * n_t
        + [_whole(mod.shape), _whole(nw.shape), _whole(w1.shape), _whole(w3.shape), _whole(w2.shape)],
        [td, tf, tf, td] + [pl.BlockSpec((1, 1), lambda i: (0, 0))] * n_t,
        [_sds((t + tc, d), F32), _sds((t + tc, f), BF16), _sds((t + tc, f), BF16), _sds((t + tc, d), BF16)]
        + [_sds((1, 1), F32)] * n_t,
        (x_lat,) + ((x_ctx,) if tc else ()) + ((target,) if n_t else ()) + (mod, nw, w1, w3, w2),
        scratch=[pltpu.VMEM((1, d), F32)] * n_t, exch=exch)


def _ffn_bwd(dout, x_lat, x_ctx, a, b, y, mod, nw, w1, w3, w2, k0, s, nb, tm, name, exch=None):
    t, d = x_lat.shape
    tc = 0 if x_ctx is None else x_ctx.shape[0]
    f = w1.shape[0]
    nch = 2 if (f // 2) % LANES == 0 and f % 2 == 0 else 1
    fc = f // nch
    tiles = _TokenTiles(t, tc, tm)
    n_x = 2 if tc else 1

    def body(*refs):
        do_ref, x_ref = refs[0], refs[1]
        (a_ref, b_ref, y_ref, mod_ref, nw_ref, w1_ref, w3_ref, w2_ref,
         dx_ref, h_ref, g_ref, da_ref, db_ref, dy_ref, dmod_ref, dnw_ref) = refs[1 + n_x:]
        i = pl.program_id(0)

        @pl.when(i == 0)
        def _():
            dmod_ref[...] = jnp.zeros_like(dmod_ref)
            dnw_ref[...] = jnp.zeros_like(dnw_ref)

        g = jnp.minimum((tiles.tile(i) * tm) // s, nb)
        shift = mod_ref[g, pl.ds(k0, 1), :]
        scale = mod_ref[g, pl.ds(k0 + 1, 1), :]
        gate = mod_ref[g, pl.ds(k0 + 2, 1), :]
        x = jnp.where(tiles.is_lat(i), x_ref[...], refs[2][...]) if tc else x_ref[...]
        dout_v = do_ref[...]
        r = lax.rsqrt(jnp.mean(x * x, axis=-1, keepdims=True) + EPS)
        xh = x * r
        n = xh * nw_ref[...]
        h_ref[...] = (n * (1.0 + scale) + shift).astype(BF16)
        dyb = ((0.5 * gate) * dout_v).astype(BF16)
        dy_ref[...] = dyb
        dmod_ref[g, pl.ds(k0 + 2, 1), :] += 0.5 * jnp.sum(dout_v * y_ref[...].astype(F32), axis=0, keepdims=True)
        dh = jnp.zeros((tm, d), F32)
        for c in range(nch):
            sl = slice(c * fc, (c + 1) * fc)
            dg = _dot_nt(dyb, w2_ref[sl, :])
            av = a_ref[:, sl].astype(F32)
            bv = b_ref[:, sl].astype(F32)
            sig = _sigmoid(av)
            sa = av * sig
            g_ref[:, sl] = (sa * bv).astype(BF16)
            dab = (dg * bv * (sig * (1.0 + av * (1.0 - sig)))).astype(BF16)
            dbb = (dg * sa).astype(BF16)
            da_ref[:, sl] = dab
            db_ref[:, sl] = dbb
            dh = dh + _dot(dab, w1_ref[sl, :]) + _dot(dbb, w3_ref[sl, :])
        dmod_ref[g, pl.ds(k0, 1), :] += jnp.sum(dh, axis=0, keepdims=True)
        dmod_ref[g, pl.ds(k0 + 1, 1), :] += jnp.sum(dh * n, axis=0, keepdims=True)
        dn = dh * (1.0 + scale)
        dnw_ref[...] += jnp.sum(dn * xh, axis=0, keepdims=True)
        dxh = dn * nw_ref[...]
        dx_ref[...] = dout_v + r * (dxh - xh * jnp.mean(dxh * xh, axis=-1, keepdims=True))

    td = pl.BlockSpec((tm, d), tiles.row)
    tf = pl.BlockSpec((tm, f), tiles.row)
    lat = pl.BlockSpec((tm, d), tiles.lat_row)
    ta = t + tc
    return _hosted_call(
        body, name, (tiles.n_all,),
        [td, lat] + ([pl.BlockSpec((tm, d), tiles.ctx_row)] if tc else [])
        + [tf, tf, td, _whole(mod.shape), _whole(nw.shape), _whole(w1.shape), _whole(w3.shape), _whole(w2.shape)],
        [lat, td, tf, tf, tf, td, pl.BlockSpec(mod.shape, lambda i: (0, 0, 0)), pl.BlockSpec((1, d), lambda i: (0, 0))],
        [_sds((t, d), F32), _sds((ta, d), BF16), _sds((ta, f), BF16), _sds((ta, f), BF16), _sds((ta, f), BF16),
         _sds((ta, d), BF16), _sds(mod.shape, F32), _sds((1, d), F32)],
        (dout, x_lat) + ((x_ctx,) if tc else ()) + (a, b, y, mod, nw, w1, w3, w2), exch=exch)


def _mm_tn(a, b, rows, name, exch=None):
    m = a.shape[1]
    n = b.shape[1]
    bm = _div_tile(m, 1408, LANES)
    bn = _div_tile(n, 1408, LANES)
    bk = _div_tile(rows, 2304, LANES)
    nk = rows // bk

    def body(a_ref, b_ref, o_ref, acc_ref):
        k = pl.program_id(2)

        @pl.when(k == 0)
        def _():
            acc_ref[...] = jnp.zeros_like(acc_ref)

        acc_ref[...] += _dot_tn(a_ref[...], b_ref[...])

        @pl.when(k == nk - 1)
        def _():
            o_ref[...] = acc_ref[...].astype(BF16)

    (out,), got = _hosted_call(
        body, name, (m // bm, n // bn, nk),
        [pl.BlockSpec((bk, bm), lambda i, j, k: (k, i)), pl.BlockSpec((bk, bn), lambda i, j, k: (k, j))],
        [pl.BlockSpec((bm, bn), lambda i, j, k: (i, j))], [_sds((m, n), BF16)], (a, b),
        scratch=[pltpu.VMEM((bm, bn), F32)], exch=exch)
    return out if exch is None else (out, got)


def _mixin_fwd(xs, mod, nw, wp, s, nb, tm):
    t, d = xs.shape

    def body(x_ref, mod_ref, nw_ref, wp_ref, h_ref, p_ref):
        g = jnp.minimum((pl.program_id(0) * tm) // s, nb)
        shift = mod_ref[g, pl.ds(3, 1), :]
        scale = mod_ref[g, pl.ds(4, 1), :]
        x = x_ref[...]
        r = lax.rsqrt(jnp.mean(x * x, axis=-1, keepdims=True) + EPS)
        hb = ((x * r * nw_ref[...]) * (1.0 + scale) + shift).astype(BF16)
        h_ref[...] = hb
        p_ref[...] = _dot_nt(hb, wp_ref[...]).astype(BF16)

    row = lambda i: (i, 0)
    return pl.pallas_call(
        body, name="mixin_fwd", grid=(t // tm,),
        in_specs=[pl.BlockSpec((tm, d), row), _whole(mod.shape), _whole(nw.shape), _whole(wp.shape)],
        out_specs=[pl.BlockSpec((tm, d), row), pl.BlockSpec((tm, PROJ_COLS), row)],
        out_shape=[_sds((t, d), BF16), _sds((t, PROJ_COLS), BF16)], compiler_params=_params(1),
    )(xs, mod, nw, wp)


def _mixin_bwd(dp0, duv, xs, dres, mod, nw, wp, s, nb, tm):
    t_all, d = xs.shape
    nlat = dres.shape[0] // tm

    def body(p0_ref, uv_ref, x_ref, dr_ref, mod_ref, nw_ref, wp_ref, dx_ref, dmod_ref, dnw_ref):
        i = pl.program_id(0)

        @pl.when(i == 0)
        def _():
            dmod_ref[...] = jnp.zeros_like(dmod_ref)
            dnw_ref[...] = jnp.zeros_like(dnw_ref)

        lat = i < nlat
        g = jnp.minimum((i * tm) // s, nb)
        scale = mod_ref[g, pl.ds(4, 1), :]
        dh = _dot(p0_ref[...], wp_ref[0:512, :])
        extra = _dot(uv_ref[...], wp_ref[512:1536, :])
        dh = dh + jnp.where(lat, extra, 0.0)
        x = x_ref[...]
        r = lax.rsqrt(jnp.mean(x * x, axis=-1, keepdims=True) + EPS)
        xh = x * r
        n = xh * nw_ref[...]
        dmod_ref[g, pl.ds(3, 1), :] += jnp.sum(dh, axis=0, keepdims=True)
        dmod_ref[g, pl.ds(4, 1), :] += jnp.sum(dh * n, axis=0, keepdims=True)
        dn = dh * (1.0 + scale)
        dnw_ref[...] += jnp.sum(dn * xh, axis=0, keepdims=True)
        dxh = dn * nw_ref[...]
        dx_ref[...] = jnp.where(lat, dr_ref[...], 0.0) + r * (dxh - xh * jnp.mean(dxh * xh, axis=-1, keepdims=True))

    row = lambda i: (i, 0)
    lrow = lambda i: (jnp.minimum(i, nlat - 1), 0)
    return pl.pallas_call(
        body, name="mixin_bwd", grid=(t_all // tm,),
        in_specs=[pl.BlockSpec((tm, 512), row), pl.BlockSpec((tm, 1024), lrow), pl.BlockSpec((tm, d), row),
                  pl.BlockSpec((tm, d), lrow), _whole(mod.shape), _whole(nw.shape), _whole(wp.shape)],
        out_specs=[pl.BlockSpec((tm, d), row), pl.BlockSpec(mod.shape, lambda i: (0, 0, 0)),
                   pl.BlockSpec((1, d), lambda i: (0, 0))],
        out_shape=[_sds((t_all, d), F32), _sds(mod.shape, F32), _sds((1, d), F32)], compiler_params=_params(1),
    )(dp0, duv, xs, dres, mod, nw, wp)


def _prep_fwd(proj, row0, nb, s, pos0, sk, key0, into, tabs, wq, wk, wv, kvaw, qaw, qnw, knw, tm, with_q, name):
    nblk = s // tm
    n_into = 0 if into is None else 2

    def body(p_ref, cos_ref, sa_ref, sb_ref, wq_ref, wk_ref, wv_ref, kvaw_ref, qaw_ref, qnw_ref, knw_ref, *rest):
        outs, heads_ref = rest[n_into:-1], rest[-1]
        q_ref, k_ref, v_ref = outs if with_q else (None,) + outs
        cos, sin_a, sin_b = cos_ref[...][None], sa_ref[...][None], sb_ref[...][None]

        def normed_roped(w_ref, src, extra, nw_ref, o_ref, post):
            for h in range(HEADS):
                heads_ref[h] = _dot_nt(src, w_ref[h]) if extra is None else _dot(src, w_ref[h])
            xp = heads_ref[...] if extra is None else heads_ref[...] + extra[None]
            r = lax.rsqrt(jnp.sum(xp * xp, axis=-1, keepdims=True) * (1.0 / QK_HEAD) + EPS)
            o_ref[...] = _rope3(xp * r * (nw_ref[...] * post)[None], cos, sin_a, sin_b).astype(BF16)

        ckv = p_ref[:, 0:128].astype(F32)
        rkv = lax.rsqrt(jnp.mean(ckv * ckv, axis=-1, keepdims=True) + EPS)
        ckvb = (ckv * rkv * kvaw_ref[...]).astype(BF16)
        normed_roped(wk_ref, ckvb, p_ref[:, 128:256].astype(F32), knw_ref, k_ref, 1.0)
        for j in range(HEADS // 2):
            v_ref[j] = _dot(ckvb, wv_ref[j]).astype(BF16)
        if with_q:
            cq = p_ref[:, 256:512].astype(F32)
            rq = lax.rsqrt(jnp.mean(cq * cq, axis=-1, keepdims=True) + EPS)
            normed_roped(wq_ref, (cq * rq * qaw_ref[...]).astype(BF16), None, qnw_ref, q_ref, SOFTMAX_SCALE)

    tab = pl.BlockSpec((tm, HEAD_PAD), lambda i: (pos0 + i % nblk, 0))
    qspec = pl.BlockSpec((None, HEADS, tm, HEAD_PAD), lambda i: (i // nblk, 0, i % nblk, 0))
    kspec = pl.BlockSpec((None, HEADS, tm, HEAD_PAD), lambda i: (i // nblk, 0, key0 + i % nblk, 0))
    vspec = pl.BlockSpec((None, HEADS // 2, tm, HEAD_PAD), lambda i: (i // nblk, 0, key0 + i % nblk, 0))
    qshape = _sds((nb, HEADS, s, HEAD_PAD), BF16)
    kshape = _sds((nb, HEADS, sk, HEAD_PAD), BF16)
    vshape = _sds((nb, HEADS // 2, sk, HEAD_PAD), BF16)
    n_q = 1 if with_q else 0
    return pl.pallas_call(
        body, name=name, grid=(nb * nblk,),
        in_specs=[pl.BlockSpec((tm, 512), lambda i: (row0 + i, 0)), tab, tab, tab, _whole(wq.shape), _whole(wk.shape),
                  _whole(wv.shape), _whole(kvaw.shape), _whole(qaw.shape), _whole(qnw.shape), _whole(knw.shape)]
        + [pl.BlockSpec(memory_space=pl.ANY)] * n_into,
        out_specs=([qspec] if with_q else []) + [kspec, vspec],
        out_shape=([qshape] if with_q else []) + [kshape, vshape],
        scratch_shapes=[pltpu.VMEM((HEADS, tm, HEAD_PAD), F32)],
        input_output_aliases={11: n_q, 12: n_q + 1} if n_into else {}, compiler_params=_params(1),
    )(proj, *tabs, wq, wk, wv, kvaw, qaw, qnw, knw, *(into or ()))


def _prep_bwd(proj, row0, nb, s, pos0, key0, dp_rows, dp_into, tabs, wq, wk, wv, kvaw, qaw, qnw, knw, dq, dk, dv, init, tm,
              name):
    nblk = s // tm
    with_q = dq is not None
    n_init = 0 if init is None else len(init)
    n_into = 0 if dp_into is None else 1

    def body(*refs):
        p_ref, cos_ref, sa_ref, sb_ref, wq_ref, wk_ref, wv_ref, kvaw_ref, qaw_ref, qnw_ref, knw_ref = refs[:11]
        rest = list(refs[11:])
        dq_ref = rest.pop(0) if with_q else None
        dk_ref, dv_ref = rest.pop(0), rest.pop(0)
        init_refs = [rest.pop(0) for _ in range(n_init)]
        if n_into:
            rest.pop(0)
        dp_ref = rest.pop(0)
        if with_q:
            dwq_ref, dqaw_ref, dqnw_ref = rest.pop(0), rest.pop(0), rest.pop(0)
        dwk_ref, dwv_ref, dkvaw_ref, dknw_ref, heads_ref, dhb_ref, dkr_ref = rest
        accs = [dwk_ref, dwv_ref, dkvaw_ref, dknw_ref]

        @pl.when(pl.program_id(0) == 0)
        def _():
            for k, acc in enumerate(accs):
                acc[...] = init_refs[k][...] if n_init else jnp.zeros_like(acc)
            if with_q:
                dwq_ref[...] = jnp.zeros_like(dwq_ref)
                dqaw_ref[...] = jnp.zeros_like(dqaw_ref)
                dqnw_ref[...] = jnp.zeros_like(dqnw_ref)

        cos, sin_a, sin_b = cos_ref[...][None], sa_ref[...][None], sb_ref[...][None]
        lane = lax.broadcasted_iota(jnp.int32, (tm, HEAD_PAD), 1)
        rope_lanes = (lane >= QK_NOPE) & (lane < QK_HEAD)

        def heads_bwd(w_ref, src, extra, nw_ref, d_ref, dnw_ref, dw_ref, post):
            w_t = extra is None
            for h in range(HEADS):
                heads_ref[h] = _dot_nt(src, w_ref[h]) if w_t else _dot(src, w_ref[h])
            xp = heads_ref[...] if extra is None else heads_ref[...] + extra[None]
            r = lax.rsqrt(jnp.sum(xp * xp, axis=-1, keepdims=True) * (1.0 / QK_HEAD) + EPS)
            xh = xp * r
            dn = _rope3_t(d_ref[...], cos, sin_a, sin_b)
            dnw_ref[...] += post * jnp.sum(jnp.sum(dn * xh, axis=0), axis=0, keepdims=True)
            dxh = dn * (nw_ref[...] * post)[None]
            dxp = r * (dxh - xh * (jnp.sum(dxh * xh, axis=-1, keepdims=True) * (1.0 / QK_HEAD)))
            dhb_ref[...] = dxp.astype(BF16)
            dsrc = jnp.zeros((tm, src.shape[1]), F32)
            for h in range(HEADS):
                dsrc = dsrc + (_dot(dhb_ref[h], w_ref[h]) if w_t else _dot_nt(dhb_ref[h], w_ref[h]))
                dw_ref[h] += _dot_tn(src, dhb_ref[h])
            return dsrc, jnp.sum(dxp, axis=0)

        ckv = p_ref[:, 0:128].astype(F32)
        rkv = lax.rsqrt(jnp.mean(ckv * ckv, axis=-1, keepdims=True) + EPS)
        ckvh = ckv * rkv
        ckvb = (ckvh * kvaw_ref[...]).astype(BF16)
        for h in range(HEADS):
            dkr_ref[h] = dk_ref[h].T
        dckv, dkp_sum = heads_bwd(wk_ref, ckvb, p_ref[:, 128:256].astype(F32), knw_ref, dkr_ref, dknw_ref, dwk_ref,
                                  1.0)
        for j in range(HEADS // 2):
            dvb = dv_ref[j].T.astype(BF16)
            dckv = dckv + _dot_nt(dvb, wv_ref[j])
            dwv_ref[j] += _dot_tn(ckvb, dvb)
        dkvaw_ref[...] += jnp.sum(dckv * ckvh, axis=0, keepdims=True)
        dch = dckv * kvaw_ref[...]
        dp_ref[:, 0:128] = (rkv * (dch - ckvh * jnp.mean(dch * ckvh, axis=-1, keepdims=True))).astype(BF16)
        dp_ref[:, 128:256] = jnp.where(rope_lanes, dkp_sum, 0.0).astype(BF16)
        if with_q:
            cq = p_ref[:, 256:512].astype(F32)
            rq = lax.rsqrt(jnp.mean(cq * cq, axis=-1, keepdims=True) + EPS)
            cqh = cq * rq
            cqb = (cqh * qaw_ref[...]).astype(BF16)
            dcq, _ = heads_bwd(wq_ref, cqb, None, qnw_ref, dq_ref, dqnw_ref, dwq_ref, SOFTMAX_SCALE)
            dqaw_ref[...] += jnp.sum(dcq * cqh, axis=0, keepdims=True)
            dqc = dcq * qaw_ref[...]
            dp_ref[:, 256:512] = (rq * (dqc - cqh * jnp.mean(dqc * cqh, axis=-1, keepdims=True))).astype(BF16)
        else:
            dp_ref[:, 256:512] = jnp.zeros((tm, Q_LORA), BF16)

    tab = pl.BlockSpec((tm, HEAD_PAD), lambda i: (pos0 + i % nblk, 0))
    qspec = pl.BlockSpec((None, HEADS, tm, HEAD_PAD), lambda i: (i // nblk, 0, i % nblk, 0))
    kspec = pl.BlockSpec((None, HEADS, HEAD_PAD, tm), lambda i: (i // nblk, 0, 0, key0 + i % nblk))
    vspec = pl.BlockSpec((None, HEADS // 2, HEAD_PAD, tm), lambda i: (i // nblk, 0, 0, key0 + i % nblk))

    def acc_spec(shape):
        nd = len(shape)
        return pl.BlockSpec(shape, lambda i: (0,) * nd)

    acc_shapes = [(HEADS, KV_LORA, HEAD_PAD), (HEADS // 2, KV_LORA, HEAD_PAD), (1, KV_LORA), (1, HEAD_PAD)]
    q_shapes = [(HEADS, Q_LORA, HEAD_PAD), (1, Q_LORA), (1, HEAD_PAD)] if with_q else []
    out_shapes = [(dp_rows, 512)] + q_shapes + acc_shapes
    n_before = 11 + (1 if with_q else 0) + 2 + n_init
    return pl.pallas_call(
        body, name=name, grid=(nb * nblk,),
        in_specs=[pl.BlockSpec((tm, 512), lambda i: (row0 + i, 0)), tab, tab, tab, _whole(wq.shape), _whole(wk.shape),
                  _whole(wv.shape), _whole(kvaw.shape), _whole(qaw.shape), _whole(qnw.shape), _whole(knw.shape)]
        + ([qspec] if with_q else []) + [kspec, vspec] + [_whole(a.shape) for a in (init or [])]
        + [pl.BlockSpec(memory_space=pl.ANY)] * n_into,
        out_specs=[pl.BlockSpec((tm, 512), lambda i: (row0 + i, 0))] + [acc_spec(sh) for sh in q_shapes + acc_shapes],
        out_shape=[_sds(out_shapes[0], BF16)] + [_sds(sh, F32) for sh in out_shapes[1:]],
        scratch_shapes=[pltpu.VMEM((HEADS, tm, HEAD_PAD), F32), pltpu.VMEM((HEADS, tm, HEAD_PAD), BF16),
                        pltpu.VMEM((HEADS, tm, HEAD_PAD), F32)],
        input_output_aliases={n_before: 0} if n_into else {}, compiler_params=_params(1),
    )(proj, *tabs, wq, wk, wv, kvaw, qaw, qnw, knw, *([dq] if with_q else []), dk, dv, *(init or []),
      *([dp_into] if n_into else []))


def _attn_fwd(q, k, v, tq, exch=None):
    nb, _, s, _ = q.shape
    sk = k.shape[2]
    nq = s // tq

    def body(q_ref, k_ref, v_ref, o_ref, lse_ref, vext_ref):
        @pl.when(pl.program_id(2) == 0)
        def _():
            vext_ref[:, 0:HEAD_PAD] = v_ref[...]
            vext_ref[:, HEAD_PAD:2 * HEAD_PAD] = jnp.ones((sk, HEAD_PAD), BF16)

        lane = lax.broadcasted_iota(jnp.int32, (tq, HEAD_PAD), 1)
        outs = []
        for hh in range(2):
            sc = _dot_nt(q_ref[hh], k_ref[hh])
            m = jnp.max(sc, axis=-1, keepdims=True)
            pv = _dot(jnp.exp2(sc - m).astype(BF16), vext_ref[...])
            l = pv[:, HEAD_PAD:HEAD_PAD + 1]
            outs.append(pv[:, 0:HEAD_PAD] / l)
            lse_ref[hh] = m + jnp.log2(l)
        o_ref[...] = jnp.where(lane < V_HEAD, outs[0], outs[1]).astype(BF16)

    (o, lse), got = _hosted_call(
        body, "attn_fwd", (nb, HEADS // 2, nq),
        [pl.BlockSpec((None, 2, tq, HEAD_PAD), lambda b, j, i: (b, j, i, 0)),
         pl.BlockSpec((None, 2, sk, HEAD_PAD), lambda b, j, i: (b, j, 0, 0)),
         pl.BlockSpec((None, None, sk, HEAD_PAD), lambda b, j, i: (b, j, 0, 0))],
        [pl.BlockSpec((tq, HEAD_PAD), lambda b, j, i: (b * nq + i, j)),
         pl.BlockSpec((None, 2, tq, 1), lambda b, j, i: (b, j, i, 0))],
        [_sds((nb * s, MLA_W + GMLP_W), BF16), _sds((nb, HEADS, s, 1), F32)], (q, k, v),
        scratch=[pltpu.VMEM((sk, 2 * HEAD_PAD), BF16)], exch=exch)
    return o, lse, got


def _attn_bwd(q, k, v, do, o, lse, tq, exch=None):
    nb, _, s, _ = q.shape
    sk = k.shape[2]
    nq = s // tq

    def body(q_ref, k_ref, v_ref, do_ref, o_ref, lse_ref, dq_ref, dkt_ref, dvt_ref):
        @pl.when(pl.program_id(2) == 0)
        def _():
            dkt_ref[...] = jnp.zeros_like(dkt_ref)
            dvt_ref[...] = jnp.zeros_like(dvt_ref)

        lane = lax.broadcasted_iota(jnp.int32, (tq, HEAD_PAD), 1)
        dov = do_ref[...]
        prod = dov.astype(F32) * o_ref[...].astype(F32)
        for hh in range(2):
            mine = (lane < V_HEAD) if hh == 0 else (lane >= V_HEAD)
            doh = jnp.where(mine, dov, jnp.zeros_like(dov))
            delta = jnp.sum(jnp.where(mine, prod, 0.0), axis=-1, keepdims=True)
            qh = q_ref[hh]
            q_ln2 = (qh.astype(F32) * LN2).astype(BF16)
            kv = k_ref[hh]
            p = jnp.exp2(_dot_nt(qh, kv) - lse_ref[hh])
            u = (p * (_dot_nt(doh, v_ref[...]) - delta)).astype(BF16)
            dq_ref[hh] = _dot(u, kv) * LN2
            dkt_ref[hh] += _dot_tn(q_ln2, u)
            dvt_ref[...] += _dot_tn(doh, p.astype(BF16))

    qspec = pl.BlockSpec((None, 2, tq, HEAD_PAD), lambda b, j, i: (b, j, i, 0))
    kspec = pl.BlockSpec((None, 2, sk, HEAD_PAD), lambda b, j, i: (b, j, 0, 0))
    vspec = pl.BlockSpec((None, None, sk, HEAD_PAD), lambda b, j, i: (b, j, 0, 0))
    ospec = pl.BlockSpec((tq, HEAD_PAD), lambda b, j, i: (b * nq + i, j))
    return _hosted_call(
        body, "attn_bwd", (nb, HEADS // 2, nq),
        [qspec, kspec, vspec, ospec, ospec, pl.BlockSpec((None, 2, tq, 1), lambda b, j, i: (b, j, i, 0))],
        [qspec, pl.BlockSpec((None, 2, HEAD_PAD, sk), lambda b, j, i: (b, j, 0, 0)),
         pl.BlockSpec((None, None, HEAD_PAD, sk), lambda b, j, i: (b, j, 0, 0))],
        [_sds(q.shape, F32), _sds((nb, HEADS, HEAD_PAD, sk), F32), _sds((nb, HEADS // 2, HEAD_PAD, sk), F32)],
        (q, k, v, do, o, lse), exch=exch)


def _group_masks(rows):
    lane = lax.broadcasted_iota(jnp.int32, (rows, GMLP_W), 1)
    return [(lane >= g * GROUP_DIM) & (lane < (g + 1) * GROUP_DIM) for g in range(GROUPS)]


def _gmlp_fwd(proj, mixcat, wcat, bias, vnw, ones, tm):
    t = mixcat.shape[0]

    def body(u_ref, v_ref, wcat_ref, bias_ref, vnw_ref, ones_ref, _, o_ref):
        masks = _group_masks(CHUNK)
        gv = _gelu(v_ref[...].astype(F32))
        rv = lax.rsqrt(_group_sum(gv * gv, ones_ref) * (1.0 / GROUP_DIM) + EPS)
        vnb = (gv * rv * vnw_ref[...]).astype(BF16)
        for c in range(tm // CHUNK):
            rows = slice(c * CHUNK, (c + 1) * CHUNK)
            vc = vnb[rows]
            stack = jnp.concatenate([jnp.where(m, vc, jnp.zeros_like(vc)) for m in masks], axis=0)
            sp = _dot(wcat_ref[...], stack) + bias_ref[...]
            o_ref[rows, :] = (_gelu(u_ref[rows, :].astype(F32)) * sp).astype(BF16)

    return pl.pallas_call(
        body, name="gmlp_fwd", grid=(t // tm,),
        in_specs=[pl.BlockSpec((tm, GMLP_W), lambda i: (i, 1)), pl.BlockSpec((tm, GMLP_W), lambda i: (i, 2)),
                  _whole(wcat.shape), _whole(bias.shape), _whole(vnw.shape), _whole(ones.shape),
                  pl.BlockSpec(memory_space=pl.ANY)],
        out_specs=pl.BlockSpec((tm, GMLP_W), lambda i: (i, 1)),
        out_shape=_sds(mixcat.shape, BF16), input_output_aliases={6: 0}, compiler_params=_params(1),
    )(proj, proj, wcat, bias, vnw, ones, mixcat)


def _gmlp_bwd(proj, dsg, wcat, wcat_t, bias, vnw, ones, tm):
    t = dsg.shape[0]

    def body(u_ref, v_ref, dsg_ref, wcat_ref, wcatt_ref, bias_ref, vnw_ref, ones_ref,
             duv_ref, dws_ref, dbs_ref, dvnw_ref):
        @pl.when(pl.program_id(0) == 0)
        def _():
            dws_ref[...] = jnp.zeros_like(dws_ref)
            dbs_ref[...] = jnp.zeros_like(dbs_ref)
            dvnw_ref[...] = jnp.zeros_like(dvnw_ref)

        masks = _group_masks(CHUNK)
        v = v_ref[...].astype(F32)
        gv = _gelu(v)
        rv = lax.rsqrt(_group_sum(gv * gv, ones_ref) * (1.0 / GROUP_DIM) + EPS)
        xh = gv * rv
        vnb = (xh * vnw_ref[...]).astype(BF16)
        dvn_parts = []
        for c in range(tm // CHUNK):
            rows = slice(c * CHUNK, (c + 1) * CHUNK)
            vc = vnb[rows]
            stack = jnp.concatenate([jnp.where(m, vc, jnp.zeros_like(vc)) for m in masks], axis=0)
            sp = _dot(wcat_ref[...], stack) + bias_ref[...]
            u = u_ref[rows, :].astype(F32)
            dsg_c = dsg_ref[rows, :]
            duv_ref[rows, 0:GMLP_W] = (dsg_c * sp * _gelu_grad(u)).astype(BF16)
            ds = dsg_c * _gelu(u)
            dstack = jnp.concatenate([jnp.where(m, ds, 0.0) for m in masks], axis=0)
            dbs_ref[...] += jnp.broadcast_to(jnp.sum(dstack, axis=-1, keepdims=True), dbs_ref.shape)
            dstb = dstack.astype(BF16)
            dvn_parts.append(_dot(wcatt_ref[...], dstb))
            dws_ref[...] += _dot_nt(dstb, vc)
        dvn = jnp.concatenate(dvn_parts, axis=0) if len(dvn_parts) > 1 else dvn_parts[0]
        dvnw_ref[...] += jnp.sum(dvn * xh, axis=0, keepdims=True)
        dxh = dvn * vnw_ref[...]
        gm = _group_sum(dxh * xh, ones_ref) * (1.0 / GROUP_DIM)
        duv_ref[:, GMLP_W:2 * GMLP_W] = (rv * (dxh - xh * gm) * _gelu_grad(v)).astype(BF16)

    row = pl.BlockSpec((tm, GMLP_W), lambda i: (i, 0))
    return pl.pallas_call(
        body, name="gmlp_bwd", grid=(t // tm,),
        in_specs=[pl.BlockSpec((tm, GMLP_W), lambda i: (i, 1)), pl.BlockSpec((tm, GMLP_W), lambda i: (i, 2)), row,
                  _whole(wcat.shape), _whole(wcat_t.shape), _whole(bias.shape), _whole(vnw.shape), _whole(ones.shape)],
        out_specs=[pl.BlockSpec((tm, 2 * GMLP_W), lambda i: (i, 0)), pl.BlockSpec((GROUPS * CHUNK, CHUNK), lambda i: (0, 0)),
                   pl.BlockSpec((GROUPS * CHUNK, CHUNK), lambda i: (0, 0)), pl.BlockSpec((1, GMLP_W), lambda i: (0, 0))],
        out_shape=[_sds((t, 2 * GMLP_W), BF16), _sds((GROUPS * CHUNK, CHUNK), F32), _sds((GROUPS * CHUNK, CHUNK), F32),
                   _sds((1, GMLP_W), F32)],
        compiler_params=_params(1),
    )(proj, proj, dsg, wcat, wcat_t, bias, vnw, ones)


def _mixout_fwd(mixcat, xs, mod, wout, s, tm):
    t, width = mixcat.shape
    d = xs.shape[1]

    def body(cat_ref, x_ref, mod_ref, w_ref, x2_ref, mix_ref):
        g = (pl.program_id(0) * tm) // s
        gate = mod_ref[g, pl.ds(5, 1), :]
        mix = _dot(cat_ref[...], w_ref[...])
        x2_ref[...] = x_ref[...] + gate * mix
        mix_ref[...] = mix.astype(BF16)

    row = lambda i: (i, 0)
    return pl.pallas_call(
        body, name="mixout_fwd", grid=(t // tm,),
        in_specs=[pl.BlockSpec((tm, width), row), pl.BlockSpec((tm, d), row), _whole(mod.shape), _whole(wout.shape)],
        out_specs=[pl.BlockSpec((tm, d), row), pl.BlockSpec((tm, d), row)],
        out_shape=[_sds((t, d), F32), _sds((t, d), BF16)], compiler_params=_params(1),
    )(mixcat, xs, mod, wout)


def _mixout_bwd(dx2, mix, mod, wout, s, tm):
    t, d = dx2.shape

    def body(dx_ref, mix_ref, mod_ref, w_ref, dmix_ref, do_ref, dsg_ref, dmod_ref):
        i = pl.program_id(0)

        @pl.when(i == 0)
        def _():
            dmod_ref[...] = jnp.zeros_like(dmod_ref)

        g = (i * tm) // s
        gate = mod_ref[g, pl.ds(5, 1), :]
        dx = dx_ref[...]
        dmod_ref[g, pl.ds(5, 1), :] += jnp.sum(dx * mix_ref[...].astype(F32), axis=0, keepdims=True)
        dmb = (gate * dx).astype(BF16)
        dmix_ref[...] = dmb
        do_ref[...] = _dot_nt(dmb, w_ref[0:MLA_W, :]).astype(BF16)
        dsg_ref[...] = _dot_nt(dmb, w_ref[MLA_W:MLA_W + GMLP_W, :])

    row = lambda i: (i, 0)
    return pl.pallas_call(
        body, name="mixout_bwd", grid=(t // tm,),
        in_specs=[pl.BlockSpec((tm, d), row), pl.BlockSpec((tm, d), row), _whole(mod.shape), _whole(wout.shape)],
        out_specs=[pl.BlockSpec((tm, d), row), pl.BlockSpec((tm, MLA_W), row), pl.BlockSpec((tm, GMLP_W), row),
                   pl.BlockSpec(mod.shape, lambda i: (0, 0, 0))],
        out_shape=[_sds((t, d), BF16), _sds((t, MLA_W), BF16), _sds((t, GMLP_W), F32), _sds(mod.shape, F32)],
        compiler_params=_params(1),
    )(dx2, mix, mod, wout)


def _swap_cores(parts, name):
    n = len(parts)

    def body(*refs):
        srcs, outs, send_sems, recv_sems = refs[:n], refs[n:2 * n], refs[2 * n], refs[2 * n + 1]
        x, y, c = lax.axis_index("x"), lax.axis_index("y"), lax.axis_index("c")
        copies = [pltpu.make_async_remote_copy(
            src_ref=srcs[w], dst_ref=outs[w], send_sem=send_sems.at[w], recv_sem=recv_sems.at[w],
            device_id=(x, y, 1 - c), device_id_type=pl.DeviceIdType.MESH) for w in range(n)]
        for cp in copies:
            cp.start()
        for cp in copies:
            cp.wait()

    any_spec = pl.BlockSpec(memory_space=pl.ANY)
    return pl.pallas_call(
        body, name=name, in_specs=[any_spec] * n, out_specs=[any_spec] * n,
        out_shape=[_sds(p.shape, p.dtype) for p in parts],
        scratch_shapes=[pltpu.SemaphoreType.DMA((n,)), pltpu.SemaphoreType.DMA((n,))],
    )(*parts)


def _row_tile(r, c, mult):
    return _div_tile(r, max(mult, (1 << 18) // c), mult)


def _sum_slots(recv, name):
    _, r, c = recv.shape
    tr = _row_tile(r, c, 16)

    def body(r_ref, o_ref):
        f = lambda k: r_ref[k].astype(F32)
        o_ref[...] = ((f(0) + f(1)) + f(2)) + f(3)

    return pl.pallas_call(
        body, name=name, grid=(r // tr,),
        in_specs=[pl.BlockSpec((N_CHIPS, tr, c), lambda i: (0, i, 0))],
        out_specs=pl.BlockSpec((tr, c), lambda i: (i, 0)),
        out_shape=_sds((r, c), F32), compiler_params=_params(1),
    )(recv)


def _adamw(parts, w, m, v, name, exch=None):
    r, wd = w.shape
    tr = _row_tile(r, wd, 8)
    c1 = 1.0 / (1.0 - ADAM_B1 ** ADAM_STEP)
    c2 = 1.0 / (1.0 - ADAM_B2 ** ADAM_STEP)
    n_p = len(parts)

    def body(*refs):
        p_refs = refs[:n_p]
        w_ref, m_ref, v_ref, g_ref, d_ref, nm_ref, nv_ref = refs[n_p:]
        g = p_refs[0][...]
        for p_ref in p_refs[1:]:
            g = g + p_ref[...]
        nm = ADAM_B1 * m_ref[...] + (1.0 - ADAM_B1) * g
        nv = ADAM_B2 * v_ref[...] + (1.0 - ADAM_B2) * (g * g)
        g_ref[...] = g
        nm_ref[...] = nm
        nv_ref[...] = nv
        d_ref[...] = -ADAM_LR * ((nm * c1) / (jnp.sqrt(nv * c2) + ADAM_EPS) + ADAM_WD * w_ref[...])

    spec = pl.BlockSpec((tr, wd), lambda i: (i, 0))
    return _hosted_call(body, name, (r // tr,), [spec] * (n_p + 3), [spec] * 4, [_sds((r, wd), F32)] * 4,
                        (*parts, w, m, v), exch=exch)


def _all_peers(x, y, c):
    flips = [(dx, dy, dc) for dx in (0, 1) for dy in (0, 1) for dc in (0, 1)][1:]
    return [(1 - x if dx else x, 1 - y if dy else y, 1 - c if dc else c) for dx, dy, dc in flips]


def _first_exchange(shards, cc, w, b):
    n_w = len(shards)
    n = w.shape[1]

    def body(*refs):
        srcs, (cc_ref, w_ref, b_ref) = refs[:n_w], refs[n_w:n_w + 3]
        outs, (all_ref, tab_ref) = refs[n_w + 3:2 * n_w + 3], refs[2 * n_w + 3:2 * n_w + 5]
        (part_ref, ici_send, ici_recv, d2d_send, d2d_recv, local_sems, cc_send, cc_recv, tab_send,
         tab_recv) = refs[2 * n_w + 5:]
        x, y, c = lax.axis_index("x"), lax.axis_index("y"), lax.axis_index("c")
        chip, dev = 2 * x + y, 4 * x + 2 * y + c
        chips = _other_chips(x, y)
        peers = _all_peers(x, y, c)

        def half(wi, which):
            hr = shards[wi].shape[0] // 2
            return pl.ds(pl.multiple_of(which * hr, 16), hr)

        def over_ici(wi, k, arriving):
            px, py = chips[k]
            slot = 2 * px + py if arriving else chip
            return pltpu.make_async_remote_copy(
                src_ref=srcs[wi].at[half(wi, c)], dst_ref=outs[wi].at[slot, half(wi, c)],
                send_sem=ici_send.at[3 * wi + k], recv_sem=ici_recv.at[3 * wi + k], device_id=(px, py, c),
                device_id_type=pl.DeviceIdType.MESH)

        def to_sibling(wi, k, arriving):
            px, py = chips[k]
            rows = half(wi, 1 - c if arriving else c)
            return pltpu.make_async_remote_copy(
                src_ref=outs[wi].at[2 * px + py, rows], dst_ref=outs[wi].at[2 * px + py, rows],
                send_sem=d2d_send.at[3 * wi + k], recv_sem=d2d_recv.at[3 * wi + k], device_id=(x, y, 1 - c),
                device_id_type=pl.DeviceIdType.MESH)

        def cc_copy(k, peer, slot):
            return pltpu.make_async_remote_copy(
                src_ref=cc_ref, dst_ref=all_ref.at[slot], send_sem=cc_send.at[k], recv_sem=cc_recv.at[k],
                device_id=peer, device_id_type=pl.DeviceIdType.MESH)

        def rows_of(px, py):
            return part_ref.at[pl.ds(pl.multiple_of((4 * px + 2 * py + c) * MOD_ROWS, MOD_ROWS), MOD_ROWS)]

        def tab_copy(k, px, py, slot):
            return pltpu.make_async_remote_copy(
                src_ref=rows_of(px, py), dst_ref=tab_ref.at[slot], send_sem=tab_send.at[k], recv_sem=tab_recv.at[k],
                device_id=(px, py, c), device_id_type=pl.DeviceIdType.MESH)

        local = [pltpu.make_async_copy(srcs[wi], outs[wi].at[chip], local_sems.at[wi]) for wi in range(n_w)]
        for cp in local:
            cp.start()
        pairs = [(wi, k) for wi in range(n_w) for k in range(3)]
        for wi, k in pairs:
            over_ici(wi, k, False).start()
        for k, peer in enumerate(peers):
            cc_copy(k, peer, dev).start()
        all_ref[dev] = cc_ref[...]
        for k, (px, py, pc) in enumerate(peers):
            cc_copy(k, (px, py, pc), 4 * px + 2 * py + pc).wait_recv()
        cv = all_ref[...].reshape(8 * MOD_ROWS, cc.shape[1])
        part_ref[...] = _dot((cv * _sigmoid(cv)).astype(BF16), w_ref[...]) + b_ref[...]
        for k, (px, py) in enumerate(chips):
            tab_copy(k, px, py, chip).start()
        tab_ref[chip] = rows_of(x, y)[...]
        for k, (px, py) in enumerate(chips):
            tab_copy(k, px, py, 2 * px + py).wait_recv()
        for wi, k in pairs:
            over_ici(wi, k, True).wait_recv()
            to_sibling(wi, k, False).start()
        for wi, k in pairs:
            to_sibling(wi, k, True).wait_recv()
        for wi, k in pairs:
            over_ici(wi, k, False).wait_send()
            to_sibling(wi, k, False).wait_send()
        for k, peer in enumerate(peers):
            cc_copy(k, peer, dev).wait_send()
        for k, (px, py) in enumerate(chips):
            tab_copy(k, px, py, chip).wait_send()
        for cp in local:
            cp.wait()

    any_spec = pl.BlockSpec(memory_space=pl.ANY)
    vmem = pl.BlockSpec(memory_space=pltpu.VMEM)
    sems3 = pltpu.SemaphoreType.DMA((3 * n_w,))
    got = pl.pallas_call(
        body, name="first_exchange", in_specs=[any_spec] * n_w + [vmem] * 3, out_specs=[any_spec] * n_w + [vmem] * 2,
        out_shape=_exch_shapes("gather", shards) + [_sds((8,) + cc.shape, F32), _sds((N_CHIPS, MOD_ROWS, n), F32)],
        scratch_shapes=[pltpu.VMEM((8 * MOD_ROWS, n), F32), sems3, sems3, sems3, sems3, pltpu.SemaphoreType.DMA((n_w,)),
                        pltpu.SemaphoreType.DMA((7,)), pltpu.SemaphoreType.DMA((7,)), pltpu.SemaphoreType.DMA((3,)),
                        pltpu.SemaphoreType.DMA((3,))],
        compiler_params=pltpu.CompilerParams(vmem_limit_bytes=V7X_VMEM_LIMIT),
    )(*shards, cc, w, b)
    return got[:n_w], got[n_w], got[n_w + 1]


def _ada_bwd_tp(cc_all, dmods, w, ctx_row):
    d, n = w.shape

    def body(cc_ref, m0, m1, m2, m3, w_ref, dw_ref, db_ref, dctx_ref, stage_ref, all_ref, send_sems, recv_sems):
        x, y, c = lax.axis_index("x"), lax.axis_index("y"), lax.axis_index("c")
        me = 4 * x + 2 * y + c
        dsum = m0[...] + m1[...] + m2[...] + m3[...]
        db_ref[...] = jnp.sum(dsum, axis=0, keepdims=True)
        for j in range(N_CHIPS):
            stage_ref[j] = dsum[:, j * n:(j + 1) * n]

        def copy(k, peer, slot):
            px, py, _ = peer
            return pltpu.make_async_remote_copy(
                src_ref=stage_ref.at[2 * px + py], dst_ref=all_ref.at[slot], send_sem=send_sems.at[k],
                recv_sem=recv_sems.at[k], device_id=peer, device_id_type=pl.DeviceIdType.MESH)

        peers = _all_peers(x, y, c)
        for k, peer in enumerate(peers):
            copy(k, peer, me).start()
        all_ref[me] = stage_ref[2 * x + y]
        for k, (px, py, pc) in enumerate(peers):
            copy(k, (px, py, pc), 4 * px + 2 * py + pc).wait_recv()
        for k, peer in enumerate(peers):
            copy(k, peer, me).wait_send()
        cv = cc_ref[...]
        sig = _sigmoid(cv)
        dmb = all_ref[...].reshape(8 * MOD_ROWS, n).astype(BF16)
        dw_ref[...] = _dot_tn((cv * sig).astype(BF16), dmb)
        dsc = _dot_nt(dmb, w_ref[...])
        dctx = dsc[ctx_row:ctx_row + 1, :]
        for dev in range(1, 8):
            dctx = dctx + dsc[dev * MOD_ROWS + ctx_row:dev * MOD_ROWS + ctx_row + 1, :]
        cx = cv[ctx_row:ctx_row + 1, :]
        sx = sig[ctx_row:ctx_row + 1, :]
        dctx_ref[...] = dctx * (sx * (1.0 + cx * (1.0 - sx))) * jnp.where(c == 0, 1.0, 0.0)

    vmem = pl.BlockSpec(memory_space=pltpu.VMEM)
    return pl.pallas_call(
        body, name="ada_bwd_tp", in_specs=[vmem] * 6, out_specs=[vmem] * 3,
        out_shape=[_sds((d, n), F32), _sds((1, N_MOD * d), F32), _sds((1, d), F32)],
        scratch_shapes=[pltpu.VMEM((N_CHIPS, MOD_ROWS, n), F32), pltpu.VMEM((8, MOD_ROWS, n), F32),
                        pltpu.SemaphoreType.DMA((7,)), pltpu.SemaphoreType.DMA((7,))],
        compiler_params=pltpu.CompilerParams(vmem_limit_bytes=V7X_VMEM_LIMIT),
    )(cc_all, *dmods, w)


def _rope_tables(s, ctx):
    pos = np.arange(s, dtype=np.float32)
    inv = (np.float32(ROPE_BASE) ** (-np.arange(0, QK_ROPE // 2, 2, dtype=np.float32) / np.float32(QK_ROPE // 2)))
    ang_r = np.floor(pos / GRID_W)[:, None] * inv
    ang_c = (pos - GRID_W * np.floor(pos / GRID_W))[:, None] * inv
    ang = np.concatenate([ang_r, ang_r, ang_c, ang_c], axis=-1).astype(np.float32)
    cos, sin = np.cos(ang), np.sin(ang)
    half_b = (np.arange(QK_ROPE) // 8) % 2 == 1
    sin_a = np.where(half_b, sin, 0.0)
    sin_b = np.where(half_b, 0.0, -sin)

    def place(tab, fill):
        full = np.full((s + ctx, HEAD_PAD), fill, np.float32)
        full[:s, QK_NOPE:QK_HEAD] = tab
        return jnp.asarray(full)

    return place(cos, 1.0), place(sin_a, 0.0), place(sin_b, 0.0)


def _pad_last(a, n):
    return jnp.pad(a, [(0, 0)] * (a.ndim - 1) + [(0, n - a.shape[-1])])


def _flat_rows(parts, rows, width):
    flat = jnp.concatenate([p.reshape(-1) for p in parts])
    return jnp.pad(flat, (0, rows * width - flat.shape[0])).reshape(rows, width)


def kernel(x, c, ctx, c_ctx, w_ada, b_ada, norm1_w, ffn1_w1, ffn1_w3, ffn1_w2, norm2_w, w_in, q_a_norm_w, w_uq, kv_a_norm_w, w_ukv, q_norm_w, k_norm_w, v_norm_w, w_s, b_s, w_out, norm3_w, ffn2_w1, ffn2_w3, ffn2_w2, loss_target, m_c_ctx, m_w_ada, m_b_ada, m_norm1_w, m_ffn1_w1, m_ffn1_w3, m_ffn1_w2, m_norm2_w, m_w_in, m_q_a_norm_w, m_w_uq, m_kv_a_norm_w, m_w_ukv, m_q_norm_w, m_k_norm_w, m_v_norm_w, m_w_s, m_b_s, m_w_out, m_norm3_w, m_ffn2_w1, m_ffn2_w3, m_ffn2_w2, v_c_ctx, v_w_ada, v_b_ada, v_norm1_w, v_ffn1_w1, v_ffn1_w3, v_ffn1_w2, v_norm2_w, v_w_in, v_q_a_norm_w, v_w_uq, v_kv_a_norm_w, v_w_ukv, v_q_norm_w, v_k_norm_w, v_v_norm_w, v_w_s, v_b_s, v_w_out, v_norm3_w, v_ffn2_w1, v_ffn2_w3, v_ffn2_w2):
    wts = dict(c_ctx=c_ctx, w_ada=w_ada, b_ada=b_ada, norm1_w=norm1_w, ffn1_w1=ffn1_w1, ffn1_w3=ffn1_w3, ffn1_w2=ffn1_w2,
               norm2_w=norm2_w, w_in=w_in, q_a_norm_w=q_a_norm_w, w_uq=w_uq, kv_a_norm_w=kv_a_norm_w, w_ukv=w_ukv,
               q_norm_w=q_norm_w, k_norm_w=k_norm_w, v_norm_w=v_norm_w, w_s=w_s, b_s=b_s, w_out=w_out, norm3_w=norm3_w,
               ffn2_w1=ffn2_w1, ffn2_w3=ffn2_w3, ffn2_w2=ffn2_w2)
    moms = dict(c_ctx=m_c_ctx, w_ada=m_w_ada, b_ada=m_b_ada, norm1_w=m_norm1_w, ffn1_w1=m_ffn1_w1, ffn1_w3=m_ffn1_w3,
                ffn1_w2=m_ffn1_w2, norm2_w=m_norm2_w, w_in=m_w_in, q_a_norm_w=m_q_a_norm_w, w_uq=m_w_uq,
                kv_a_norm_w=m_kv_a_norm_w, w_ukv=m_w_ukv, q_norm_w=m_q_norm_w, k_norm_w=m_k_norm_w, v_norm_w=m_v_norm_w,
                w_s=m_w_s, b_s=m_b_s, w_out=m_w_out, norm3_w=m_norm3_w, ffn2_w1=m_ffn2_w1, ffn2_w3=m_ffn2_w3,
                ffn2_w2=m_ffn2_w2)
    vars_ = dict(c_ctx=v_c_ctx, w_ada=v_w_ada, b_ada=v_b_ada, norm1_w=v_norm1_w, ffn1_w1=v_ffn1_w1, ffn1_w3=v_ffn1_w3,
                 ffn1_w2=v_ffn1_w2, norm2_w=v_norm2_w, w_in=v_w_in, q_a_norm_w=v_q_a_norm_w, w_uq=v_w_uq,
                 kv_a_norm_w=v_kv_a_norm_w, w_ukv=v_w_ukv, q_norm_w=v_q_norm_w, k_norm_w=v_k_norm_w, v_norm_w=v_v_norm_w,
                 w_s=v_w_s, b_s=v_b_s, w_out=v_w_out, norm3_w=v_norm3_w, ffn2_w1=v_ffn2_w1, ffn2_w3=v_ffn2_w3,
                 ffn2_w2=v_ffn2_w2)

    nb, s, d = x.shape
    nctx = ctx.shape[1]
    t, tc = nb * s, nb * nctx
    t_all = t + tc
    sk = s + nctx
    assert nb + 1 <= MOD_ROWS and d % LANES == 0
    tm = _token_tile(s, nctx)
    tq = _div_tile(s, 512, tm)
    tmx = _div_tile(math.gcd(s, tc), 512, tm)

    def held(n, a_):
        return jnp.swapaxes(a_[0], 0, 1) if n in T_WEIGHTS else a_[0]

    def unheld(n, a_):
        return (jnp.swapaxes(a_, 0, 1) if n in T_WEIGHTS else a_)[None]

    shard = {n: held(n, wts[n]).astype(BF16) for n in SHARDED}
    full = {}

    def unshard(names, blocks):
        for n, g4 in zip(names, blocks):
            _, r_, c_ = g4.shape
            if n in ROW_SHARDED or n in T_WEIGHTS:
                full[n] = g4.reshape(N_CHIPS * r_, c_)
            else:
                full[n] = g4.transpose(1, 0, 2).reshape(r_, N_CHIPS * c_)

    def chip_major(n, g_):
        if n in ROW_SHARDED or n in T_WEIGHTS:
            return g_.reshape(N_CHIPS, g_.shape[0] // N_CHIPS, g_.shape[1]).astype(BF16)
        r_, cols = g_.shape
        return g_.reshape(r_, N_CHIPS, cols // N_CHIPS).transpose(1, 0, 2).astype(BF16)

    cc = jnp.concatenate([c, c_ctx[None, :], jnp.zeros((MOD_ROWS - nb - 1, d), F32)], axis=0)
    n_ada = shard["w_ada"].shape[1]
    assert n_ada % LANES == 0
    my_chip = 2 * lax.axis_index("x") + lax.axis_index("y")
    b_cols = lax.dynamic_slice_in_dim(b_ada, my_chip * n_ada, n_ada, axis=1)
    got, cc_all, table = _first_exchange([shard[n] for n in FIRST_WEIGHTS], cc, shard["w_ada"], b_cols)
    unshard(FIRST_WEIGHTS, got)
    cc_all = cc_all.reshape(8 * MOD_ROWS, d)
    mod = table.transpose(1, 0, 2).reshape(MOD_ROWS, N_MOD, d)
    wsb = w_s[0].astype(BF16)
    wcat = wsb.transpose(1, 0, 2).reshape(CHUNK, GROUPS * CHUNK)
    wcat_t = wsb.transpose(2, 0, 1).reshape(CHUNK, GROUPS * CHUNK)
    bias = jnp.repeat(b_s[0].T, GROUP_DIM, axis=1)
    vnw = v_norm_w.reshape(1, GMLP_W)
    lane = jnp.arange(GMLP_W)
    ones = (lane[:, None] // GROUP_DIM == lane[None, :] // GROUP_DIM).astype(BF16)
    qnw = _pad_last(q_norm_w, HEAD_PAD)
    knw = _pad_last(k_norm_w, HEAD_PAD)
    tabs = _rope_tables(s, nctx)

    x_lat, x_ctx = x.reshape(t, d), ctx.reshape(tc, d)
    (xs1, a1, b1, y1), got = _ffn_fwd(x_lat, x_ctx, mod, norm1_w, full["ffn1_w1"], full["ffn1_w3"], full["ffn1_w2"], 0, s,
                                      nb, tm, "ffn1_fwd", exch=("gather", [shard[n] for n in MIX_WEIGHTS]))
    unshard(MIX_WEIGHTS, got)
    wi = full["w_in"]
    wp = jnp.concatenate([wi[0:KV_LORA], jnp.zeros((QK_NOPE, d), BF16), wi[KV_LORA:KV_LORA + QK_ROPE],
                          jnp.zeros((HEAD_PAD - QK_HEAD, d), BF16), wi[KV_LORA + QK_ROPE:]], axis=0)
    wq = jnp.pad(full["w_uq"].reshape(HEADS, QK_HEAD, Q_LORA), ((0, 0), (0, HEAD_PAD - QK_HEAD), (0, 0)))
    wkv = full["w_ukv"].reshape(KV_LORA, HEADS, QK_NOPE + V_HEAD)
    wk = _pad_last(wkv[:, :, :QK_NOPE].transpose(1, 0, 2), HEAD_PAD)
    wv = wkv[:, :, QK_NOPE:].reshape(KV_LORA, HEADS // 2, 2 * V_HEAD).transpose(1, 0, 2)
    h2, proj = _mixin_fwd(xs1, mod, norm2_w, wp, s, nb, tmx)
    prep_w = (wq, wk, wv, kv_a_norm_w, q_a_norm_w, qnw, knw)
    q, k_all, v_all = _prep_fwd(proj, 0, nb, s, 0, sk, 0, None, tabs, *prep_w, tq, True, "prep_fwd")
    k_all, v_all = _prep_fwd(proj, t // tm, nb, nctx, s // tm, sk, s // tm, (k_all, v_all), tabs, *prep_w, tm, False,
                             "prep_ctx_fwd")
    o, lse, got = _attn_fwd(q, k_all, v_all, tq, exch=("gather", [shard[n] for n in LAST_WEIGHTS]))
    unshard(LAST_WEIGHTS, got)
    mixcat = _gmlp_fwd(proj, o, wcat, bias, vnw, ones, tq)
    x2, mix = _mixout_fwd(mixcat, xs1, mod, full["w_out"], s, tq)
    (dy, a2, b2, y2, loss_part), _ = _ffn_fwd(x2, None, mod, norm3_w, full["ffn2_w1"], full["ffn2_w3"], full["ffn2_w2"], 6,
                                              s, nb, tm, "ffn2_fwd", target=loss_target.reshape(t, d))

    grads, cm, recv = {}, {}, {}

    def scatter_of(names):
        return ("scatter", [cm[n] for n in names])

    (dx2, h3, g2, da2, db2, dyb2, dmod_c, grads["norm3_w"]), _ = _ffn_bwd(
        dy, x2, None, a2, b2, y2, mod, norm3_w, full["ffn2_w1"], full["ffn2_w3"], full["ffn2_w2"], 6, s, nb, tm,
        "ffn2_bwd")
    cm["ffn2_w1"] = chip_major("ffn2_w1", _mm_tn(da2, h3, t, "ffn2_dw1"))
    cm["ffn2_w3"] = chip_major("ffn2_w3", _mm_tn(db2, h3, t, "ffn2_dw3"))
    cm["ffn2_w2"] = chip_major("ffn2_w2", _mm_tn(g2, dyb2, t, "ffn2_dw2"))
    dmix, do, dsg, dmod_b = _mixout_bwd(dx2, mix, mod, full["w_out"], s, tq)
    cm["w_out"] = chip_major("w_out", _mm_tn(mixcat, dmix, t, "wout_dw"))
    duv, dws, dbs, dvnw = _gmlp_bwd(proj, dsg, wcat, wcat_t, bias, vnw, ones, tq)
    group = LAST_WEIGHTS + ("w_out",)
    (dq, dk, dv), got = _attn_bwd(q, k_all, v_all, do, mixcat, lse, tq, exch=scatter_of(group))
    recv.update(zip(group, got))
    dp0, dwk_c, dwv_c, dkvaw_c, dknw_c = _prep_bwd(
        proj, t // tm, nb, nctx, s // tm, s // tm, t_all, None, tabs, *prep_w, None, dk, dv, None, tm, "prep_ctx_bwd")
    dp0, dwq, dqaw, dqnw, dwk, dwv, dkvaw, dknw = _prep_bwd(
        proj, 0, nb, s, 0, 0, t_all, dp0, tabs, *prep_w, dq, dk, dv, [dwk_c, dwv_c, dkvaw_c, dknw_c], tq, "prep_bwd")
    dxs1, dmod_a, grads["norm2_w"] = _mixin_bwd(dp0, duv, xs1, dx2, mod, norm2_w, wp, s, nb, tmx)
    dwp = jnp.concatenate([_mm_tn(dp0, h2, t_all, "win_dw_kvq"), _mm_tn(duv, h2, t, "win_dw_uv")], axis=0)
    cm["w_in"] = chip_major("w_in", jnp.concatenate(
        [dwp[0:KV_LORA], dwp[KV_LORA + QK_NOPE:KV_LORA + QK_HEAD], dwp[256:]], axis=0))
    cm["w_uq"] = chip_major("w_uq", dwq[:, :, :QK_HEAD].transpose(0, 2, 1).reshape(HEADS * QK_HEAD, Q_LORA))
    cm["w_ukv"] = chip_major("w_ukv", jnp.concatenate(
        [dwk[:, :, :QK_NOPE].transpose(1, 0, 2),
         dwv.transpose(1, 0, 2).reshape(KV_LORA, HEADS, V_HEAD)], axis=2).reshape(KV_LORA, HEADS * (QK_NOPE + V_HEAD)))
    (dx_lat, h1, g1, da1, db1, dyb1, dmod_0, grads["norm1_w"]), _ = _ffn_bwd(
        dxs1, x_lat, x_ctx, a1, b1, y1, mod, norm1_w, full["ffn1_w1"], full["ffn1_w3"], full["ffn1_w2"], 0, s, nb, tm,
        "ffn1_bwd")
    dmods = [m_.reshape(MOD_ROWS, N_MOD * d) for m_ in (dmod_0, dmod_a, dmod_b, dmod_c)]
    dw_ada, grads["b_ada"], dctx = _ada_bwd_tp(cc_all, dmods, shard["w_ada"], nb)
    grads["c_ctx"] = dctx[0]
    grads["q_a_norm_w"], grads["kv_a_norm_w"] = dqaw, dkvaw
    grads["q_norm_w"], grads["k_norm_w"] = dqnw[:, :QK_HEAD], dknw[:, :QK_HEAD]
    grads["v_norm_w"], grads["w_s"], grads["b_s"] = dvnw, dws, dbs[:, 0]
    grad_x = dx_lat.reshape(nb, s, d)
    n_small = sum(wts[n].size for n in SMALL)
    rows_s = _round_up(-(-(n_small + 1) // d), 16)
    cm["small"] = jnp.broadcast_to(_flat_rows([grads[n] for n in SMALL] + [loss_part], rows_s, d), (N_CHIPS, rows_s, d))
    group = ("w_in", "w_uq", "w_ukv", "small")
    dw2, got = _mm_tn(g1, dyb1, t_all, "ffn1_dw2", exch=scatter_of(group))
    recv.update(zip(group, got))
    cm["ffn1_w2"] = chip_major("ffn1_w2", dw2)
    dw1, got = _mm_tn(da1, h1, t_all, "ffn1_dw1", exch=scatter_of(("ffn1_w2",)))
    recv["ffn1_w2"] = got[0]
    cm["ffn1_w1"] = chip_major("ffn1_w1", dw1)
    dw3, got = _mm_tn(db1, h1, t_all, "ffn1_dw3", exch=scatter_of(("ffn1_w1",)))
    recv["ffn1_w1"] = got[0]
    cm["ffn1_w3"] = chip_major("ffn1_w3", dw3)
    stepped = {}
    stepped["w_ada"], got = _adamw([dw_ada], wts["w_ada"][0], moms["w_ada"][0], vars_["w_ada"][0], "adamw_w_ada",
                                   exch=scatter_of(("ffn1_w3",)))
    recv["ffn1_w3"] = got[0]

    reduced = tuple(n for n in SHARDED if n != "w_ada") + ("small",)
    part = {n: _sum_slots(recv[n], "sum_" + n) for n in reduced}
    early = LAST_WEIGHTS + ("w_out",)
    late = tuple(n for n in reduced if n not in early)
    sib = dict(zip(early, _swap_cores([part[n] for n in early], "swap_early")))
    sib.update(zip(late, _swap_cores([part[n] for n in late], "swap_late")))
    for n in reduced[:-1]:
        stepped[n], _ = _adamw([part[n], sib[n]], held(n, wts[n]), held(n, moms[n]), held(n, vars_[n]), "adamw_" + n)
    for n in SHARDED:
        stepped[n] = [unheld(n, a_) for a_ in stepped[n]]
    packed, _ = _adamw([part["small"], sib["small"]], _flat_rows([wts[n] for n in SMALL], rows_s, d),
                       _flat_rows([moms[n] for n in SMALL], rows_s, d), _flat_rows([vars_[n] for n in SMALL], rows_s, d),
                       "adamw_small")
    loss = packed[0].reshape(-1)[n_small]
    for n in SMALL:
        stepped[n] = []
    for a_ in packed:
        flat = a_.reshape(-1)
        off = 0
        for n in SMALL:
            stepped[n].append(flat[off:off + wts[n].size].reshape(wts[n].shape))
            off += wts[n].size
    return (loss, grad_x, *[stepped[n][0] for n in WEIGHTS], *[stepped[n][1] for n in WEIGHTS],
            *[stepped[n][2] for n in WEIGHTS], *[stepped[n][3] for n in WEIGHTS])
```

```python
import functools
import math

import jax
import jax.numpy as jnp
import numpy as np
from jax import lax
from jax.experimental import pallas as pl
from jax.experimental.pallas import tpu as pltpu

F32 = jnp.float32
BF16 = jnp.bfloat16

EPS = 1e-6
N_MOD = 9
HEADS = 8
QK_NOPE, QK_ROPE, V_HEAD = 64, 32, 64
QK_HEAD = QK_NOPE + QK_ROPE
HEAD_PAD = 128
LN2 = math.log(2.0)
SOFTMAX_SCALE = QK_HEAD ** -0.5 / LN2
Q_LORA, KV_LORA = 256, 128
GROUPS, GROUP_DIM, CHUNK = 8, 64, 128
GMLP_W = GROUPS * GROUP_DIM
MLA_W = HEADS * V_HEAD
IN_COLS = 1440
PROJ_COLS = 1536
GRID_W = 64
ROPE_BASE = 10000.0
MOD_ROWS = 16
ADAM_LR, ADAM_B1, ADAM_B2, ADAM_EPS, ADAM_WD, ADAM_STEP = 0.001, 0.9, 0.999, 1e-08, 0.01, 10
N_CHIPS = 4
LANES = 128
V7X_VMEM_LIMIT = 56 * 1024 * 1024
GELU_C = math.sqrt(2.0 / math.pi)

SHARDED = ("w_ada", "ffn1_w1", "ffn1_w3", "ffn1_w2", "w_in", "w_uq", "w_ukv", "w_out", "ffn2_w1", "ffn2_w3", "ffn2_w2")
ROW_SHARDED = ("ffn1_w2", "w_out", "ffn2_w2")
T_WEIGHTS = ("ffn1_w1", "ffn1_w3", "ffn2_w1", "ffn2_w3", "w_in", "w_uq")
FIRST_WEIGHTS = ("ffn1_w1", "ffn1_w3", "ffn1_w2")
MIX_WEIGHTS = ("w_in", "w_uq", "w_ukv", "w_out")
LAST_WEIGHTS = ("ffn2_w1", "ffn2_w3", "ffn2_w2")
SMALL = ("c_ctx", "b_ada", "norm1_w", "norm2_w", "q_a_norm_w", "kv_a_norm_w", "q_norm_w", "k_norm_w", "v_norm_w",
         "w_s", "b_s", "norm3_w")
WEIGHTS = ("c_ctx", "w_ada", "b_ada", "norm1_w", "ffn1_w1", "ffn1_w3", "ffn1_w2", "norm2_w", "w_in", "q_a_norm_w",
           "w_uq", "kv_a_norm_w", "w_ukv", "q_norm_w", "k_norm_w", "v_norm_w", "w_s", "b_s", "w_out", "norm3_w",
           "ffn2_w1", "ffn2_w3", "ffn2_w2")


def _round_up(n, m):
    return (n + m - 1) // m * m


def _div_tile(n, target, mult):
    best = None
    for t in range(mult, min(n, target) + 1, mult):
        if n % t == 0:
            best = t
    return n if best is None else best


def _dot(a, b):
    return lax.dot_general(a, b, (((1,), (0,)), ((), ())), preferred_element_type=F32)


def _dot_nt(a, b):
    return lax.dot_general(a, b, (((1,), (1,)), ((), ())), preferred_element_type=F32)


def _dot_tn(a, b):
    return lax.dot_general(a, b, (((0,), (0,)), ((), ())), preferred_element_type=F32)


def _sigmoid(x):
    return 1.0 / (1.0 + jnp.exp(-x))


def _gelu(x):
    return 0.5 * x * (1.0 + jnp.tanh(GELU_C * (x + 0.044715 * x * x * x)))


def _gelu_grad(x):
    t = jnp.tanh(GELU_C * (x + 0.044715 * x * x * x))
    return 0.5 * (1.0 + t) + 0.5 * x * (1.0 - t * t) * (GELU_C * (1.0 + 3 * 0.044715 * x * x))


def _rope3(x, cos, sin_a, sin_b):
    return x * cos + pltpu.roll(x, 8, 2) * sin_a + pltpu.roll(x, HEAD_PAD - 8, 2) * sin_b


def _rope3_t(d, cos, sin_a, sin_b):
    return d * cos + pltpu.roll(d * sin_a, HEAD_PAD - 8, 2) + pltpu.roll(d * sin_b, 8, 2)


def _group_sum(x, ones_ref):
    hi = x.astype(BF16)
    lo = (x - hi.astype(F32)).astype(BF16)
    return _dot(hi, ones_ref[...]) + _dot(lo, ones_ref[...])


def _params(n_axes):
    return pltpu.CompilerParams(dimension_semantics=("arbitrary",) * n_axes, vmem_limit_bytes=V7X_VMEM_LIMIT)


def _whole(shape):
    nd = len(shape)
    return pl.BlockSpec(shape, lambda *_: (0,) * nd, pipeline_mode=pl.Buffered(1))


def _sds(shape, dtype):
    return jax.ShapeDtypeStruct(shape, dtype)


def _token_tile(s, ctx):
    return _div_tile(math.gcd(s, ctx), 256, CHUNK)


def _other_chips(x, y):
    return [(1 - x, y), (x, 1 - y), (1 - x, 1 - y)]


def _exch_copies(kind, srcs, dsts, send_sems, recv_sems, local_sems, with_arrivals):
    x, y, c = lax.axis_index("x"), lax.axis_index("y"), lax.axis_index("c")
    me = 2 * x + y
    local, sends, arrivals = [], [], []
    for w, (src, dst) in enumerate(zip(srcs, dsts)):
        own = src if kind == "gather" else src.at[me]
        local.append(pltpu.make_async_copy(own, dst.at[me], local_sems.at[w]))
        for k, (px, py) in enumerate(_other_chips(x, y)):
            sem = dict(send_sem=send_sems.at[3 * w + k], recv_sem=recv_sems.at[3 * w + k], device_id=(px, py, c),
                       device_id_type=pl.DeviceIdType.MESH)
            out = src if kind == "gather" else src.at[2 * px + py]
            sends.append(pltpu.make_async_remote_copy(src_ref=out, dst_ref=dst.at[me], **sem))
            if with_arrivals:
                arrivals.append(pltpu.make_async_remote_copy(src_ref=own, dst_ref=dst.at[2 * px + py], **sem))
    return local, sends, arrivals


def _exch_start(kind, srcs, dsts, sems):
    local, sends, _ = _exch_copies(kind, srcs, dsts, *sems, with_arrivals=False)
    for cp in local + sends:
        cp.start()


def _exch_wait(kind, srcs, dsts, sems):
    local, sends, arrivals = _exch_copies(kind, srcs, dsts, *sems, with_arrivals=True)
    for cp in arrivals:
        cp.wait_recv()
    for cp in sends:
        cp.wait_send()
    for cp in local:
        cp.wait()


def _exch_scratch(n):
    return [pltpu.SemaphoreType.DMA((3 * n,)), pltpu.SemaphoreType.DMA((3 * n,)), pltpu.SemaphoreType.DMA((n,))]


def _exch_shapes(kind, arrays):
    return [_sds((N_CHIPS,) + a.shape if kind == "gather" else a.shape, a.dtype) for a in arrays]


def _hosted_call(body, name, grid, in_specs, out_specs, out_shape, operands, scratch=(), exch=None):
    n_axes = len(grid)
    if exch is None:
        outs = pl.pallas_call(body, name=name, grid=grid, in_specs=list(in_specs), out_specs=list(out_specs),
                              out_shape=list(out_shape), scratch_shapes=list(scratch),
                              compiler_params=_params(n_axes))(*operands)
        return list(outs), []
    kind, arrays = exch
    n_in, n_out, n_sc, n_ex = len(in_specs), len(out_specs), len(scratch), len(arrays)

    def hosted(*refs):
        cin, ein = refs[:n_in], refs[n_in:n_in + n_ex]
        o0 = n_in + n_ex
        cout, eout = refs[o0:o0 + n_out], refs[o0 + n_out:o0 + n_out + n_ex]
        rest = refs[o0 + n_out + n_ex:]
        csc, sems = rest[:n_sc], rest[n_sc:]
        first = functools.reduce(jnp.logical_and, [pl.program_id(a) == 0 for a in range(n_axes)])
        last = functools.reduce(jnp.logical_and, [pl.program_id(a) == grid[a] - 1 for a in range(n_axes)])

        @pl.when(first)
        def _():
            _exch_start(kind, ein, eout, sems)

        body(*cin, *cout, *csc)

        @pl.when(last)
        def _():
            _exch_wait(kind, ein, eout, sems)

    any_spec = pl.BlockSpec(memory_space=pl.ANY)
    outs = pl.pallas_call(
        hosted, name=name, grid=grid, in_specs=list(in_specs) + [any_spec] * n_ex,
        out_specs=list(out_specs) + [any_spec] * n_ex, out_shape=list(out_shape) + _exch_shapes(kind, arrays),
        scratch_shapes=list(scratch) + _exch_scratch(n_ex), compiler_params=_params(n_axes),
    )(*operands, *arrays)
    return list(outs[:n_out]), list(outs[n_out:])


class _TokenTiles:
    def __init__(self, t, tc, tm):
        self.n_lat, self.n_ctx = t // tm, tc // tm
        self.n_all = self.n_lat + self.n_ctx

    def tile(self, i):
        return (i + self.n_lat) % self.n_all if self.n_ctx else i

    def is_lat(self, i):
        return self.tile(i) < self.n_lat

    def row(self, i):
        return (self.tile(i), 0)

    def lat_row(self, i):
        return (jnp.where(self.is_lat(i), self.tile(i), 0), 0) if self.n_ctx else (i, 0)

    def ctx_row(self, i):
        return (jnp.where(self.is_lat(i), self.n_ctx - 1, self.tile(i) - self.n_lat), 0)


def _ffn_fwd(x_lat, x_ctx, mod, nw, w1, w3, w2, k0, s, nb, tm, name, target=None, exch=None):
    t, d = x_lat.shape
    tc = 0 if x_ctx is None else x_ctx.shape[0]
    f = w1.shape[0]
    tiles = _TokenTiles(t, tc, tm)
    n_x = 2 if tc else 1
    n_t = 0 if target is None else 1
    assert not (tc and n_t)

    def body(*refs):
        x_ref = refs[0]
        t_ref = refs[n_x] if n_t else None
        mod_ref, nw_ref, w1_ref, w3_ref, w2_ref, o_ref, a_ref, b_ref, y_ref = refs[n_x + n_t:n_x + n_t + 9]
        i = pl.program_id(0)
        g = jnp.minimum((tiles.tile(i) * tm) // s, nb)
        shift = mod_ref[g, pl.ds(k0, 1), :]
        scale = mod_ref[g, pl.ds(k0 + 1, 1), :]
        gate = mod_ref[g, pl.ds(k0 + 2, 1), :]
        x = jnp.where(tiles.is_lat(i), x_ref[...], refs[1][...]) if tc else x_ref[...]
        r = lax.rsqrt(jnp.mean(x * x, axis=-1, keepdims=True) + EPS)
        hb = ((x * r * nw_ref[...]) * (1.0 + scale) + shift).astype(BF16)
        a = _dot_nt(hb, w1_ref[...])
        b = _dot_nt(hb, w3_ref[...])
        gb = (a * _sigmoid(a) * b).astype(BF16)
        y = _dot(gb, w2_ref[...])
        out = x + (0.5 * gate) * y
        a_ref[...] = a.astype(BF16)
        b_ref[...] = b.astype(BF16)
        y_ref[...] = y.astype(BF16)
        if n_t:
            loss_ref, acc_ref = refs[-2:]

            @pl.when(i == 0)
            def _():
                acc_ref[...] = jnp.zeros_like(acc_ref)

            e = out - t_ref[...]
            o_ref[...] = e * (1.0 / d)
            acc_ref[...] += jnp.sum(e * e, axis=0, keepdims=True)

            @pl.when(i == tiles.n_all - 1)
            def _():
                loss_ref[...] = (0.5 / d) * jnp.sum(acc_ref[...], axis=-1, keepdims=True)
        else:
            o_ref[...] = out

    td = pl.BlockSpec((tm, d), tiles.row)
    tf = pl.BlockSpec((tm, f), tiles.row)
    return _hosted_call(
        body, name, (tiles.n_all,),
        [pl.BlockSpec((tm, d), tiles.lat_row)] + ([pl.BlockSpec((tm, d), tiles.ctx_row)] if tc else []) + [td] * n_t
        + [_whole(mod.shape), _whole(nw.shape), _whole(w1.shape), _whole(w3.shape), _whole(w2.shape)],
        [td, tf, tf, td] + [pl.BlockSpec((1, 1), lambda i: (0, 0))] * n_t,
        [_sds((t + tc, d), F32), _sds((t + tc, f), BF16), _sds((t + tc, f), BF16), _sds((t + tc, d), BF16)]
        + [_sds((1, 1), F32)] * n_t,
        (x_lat,) + ((x_ctx,) if tc else ()) + ((target,) if n_t else ()) + (mod, nw, w1, w3, w2),
        scratch=[pltpu.VMEM((1, d), F32)] * n_t, exch=exch)


def _ffn_bwd(dout, x_lat, x_ctx, a, b, y, mod, nw, w1, w3, w2, k0, s, nb, tm, name, exch=None):
    t, d = x_lat.shape
    tc = 0 if x_ctx is None else x_ctx.shape[0]
    f = w1.shape[0]
    nch = 2 if (f // 2) % LANES == 0 and f % 2 == 0 else 1
    fc = f // nch
    tiles = _TokenTiles(t, tc, tm)
    n_x = 2 if tc else 1

    def body(*refs):
        do_ref, x_ref = refs[0], refs[1]
        (a_ref, b_ref, y_ref, mod_ref, nw_ref, w1_ref, w3_ref, w2_ref,
         dx_ref, h_ref, g_ref, da_ref, db_ref, dy_ref, dmod_ref, dnw_ref) = refs[1 + n_x:]
        i = pl.program_id(0)

        @pl.when(i == 0)
        def _():
            dmod_ref[...] = jnp.zeros_like(dmod_ref)
            dnw_ref[...] = jnp.zeros_like(dnw_ref)

        g = jnp.minimum((tiles.tile(i) * tm) // s, nb)
        shift = mod_ref[g, pl.ds(k0, 1), :]
        scale = mod_ref[g, pl.ds(k0 + 1, 1), :]
        gate = mod_ref[g, pl.ds(k0 + 2, 1), :]
        x = jnp.where(tiles.is_lat(i), x_ref[...], refs[2][...]) if tc else x_ref[...]
        dout_v = do_ref[...]
        r = lax.rsqrt(jnp.mean(x * x, axis=-1, keepdims=True) + EPS)
        xh = x * r
        n = xh * nw_ref[...]
        h_ref[...] = (n * (1.0 + scale) + shift).astype(BF16)
        dyb = ((0.5 * gate) * dout_v).astype(BF16)
        dy_ref[...] = dyb
        dmod_ref[g, pl.ds(k0 + 2, 1), :] += 0.5 * jnp.sum(dout_v * y_ref[...].astype(F32), axis=0, keepdims=True)
        dh = jnp.zeros((tm, d), F32)
        for c in range(nch):
            sl = slice(c * fc, (c + 1) * fc)
            dg = _dot_nt(dyb, w2_ref[sl, :])
            av = a_ref[:, sl].astype(F32)
            bv = b_ref[:, sl].astype(F32)
            sig = _sigmoid(av)
            sa = av * sig
            g_ref[:, sl] = (sa * bv).astype(BF16)
            dab = (dg * bv * (sig * (1.0 + av * (1.0 - sig)))).astype(BF16)
            dbb = (dg * sa).astype(BF16)
            da_ref[:, sl] = dab
            db_ref[:, sl] = dbb
            dh = dh + _dot(dab, w1_ref[sl, :]) + _dot(dbb, w3_ref[sl, :])
        dmod_ref[g, pl.ds(k0, 1), :] += jnp.sum(dh, axis=0, keepdims=True)
        dmod_ref[g, pl.ds(k0 + 1, 1), :] += jnp.sum(dh * n, axis=0, keepdims=True)
        dn = dh * (1.0 + scale)
        dnw_ref[...] += jnp.sum(dn * xh, axis=0, keepdims=True)
        dxh = dn * nw_ref[...]
        dx_ref[...] = dout_v + r * (dxh - xh * jnp.mean(dxh * xh, axis=-1, keepdims=True))

    td = pl.BlockSpec((tm, d), tiles.row)
    tf = pl.BlockSpec((tm, f), tiles.row)
    lat = pl.BlockSpec((tm, d), tiles.lat_row)
    ta = t + tc
    return _hosted_call(
        body, name, (tiles.n_all,),
        [td, lat] + ([pl.BlockSpec((tm, d), tiles.ctx_row)] if tc else [])
        + [tf, tf, td, _whole(mod.shape), _whole(nw.shape), _whole(w1.shape), _whole(w3.shape), _whole(w2.shape)],
        [lat, td, tf, tf, tf, td, pl.BlockSpec(mod.shape, lambda i: (0, 0, 0)), pl.BlockSpec((1, d), lambda i: (0, 0))],
        [_sds((t, d), F32), _sds((ta, d), BF16), _sds((ta, f), BF16), _sds((ta, f), BF16), _sds((ta, f), BF16),
         _sds((ta, d), BF16), _sds(mod.shape, F32), _sds((1, d), F32)],
        (dout, x_lat) + ((x_ctx,) if tc else ()) + (a, b, y, mod, nw, w1, w3, w2), exch=exch)


def _mm_tn(a, b, rows, name, exch=None):
    m = a.shape[1]
    n = b.shape[1]
    bm = _div_tile(m, 1408, LANES)
    bn = _div_tile(n, 1408, LANES)
    bk = _div_tile(rows, 2304, LANES)
    nk = rows // bk

    def body(a_ref, b_ref, o_ref, acc_ref):
        k = pl.program_id(2)

        @pl.when(k == 0)
        def _():
            acc_ref[...] = jnp.zeros_like(acc_ref)

        acc_ref[...] += _dot_tn(a_ref[...], b_ref[...])

        @pl.when(k == nk - 1)
        def _():
            o_ref[...] = acc_ref[...].astype(BF16)

    (out,), got = _hosted_call(
        body, name, (m // bm, n // bn, nk),
        [pl.BlockSpec((bk, bm), lambda i, j, k: (k, i)), pl.BlockSpec((bk, bn), lambda i, j, k: (k, j))],
        [pl.BlockSpec((bm, bn), lambda i, j, k: (i, j))], [_sds((m, n), BF16)], (a, b),
        scratch=[pltpu.VMEM((bm, bn), F32)], exch=exch)
    return out if exch is None else (out, got)


def _mixin_fwd(xs, mod, nw, wp, s, nb, tm):
    t, d = xs.shape

    def body(x_ref, mod_ref, nw_ref, wp_ref, h_ref, p_ref):
        g = jnp.minimum((pl.program_id(0) * tm) // s, nb)
        shift = mod_ref[g, pl.ds(3, 1), :]
        scale = mod_ref[g, pl.ds(4, 1), :]
        x = x_ref[...]
        r = lax.rsqrt(jnp.mean(x * x, axis=-1, keepdims=True) + EPS)
        hb = ((x * r * nw_ref[...]) * (1.0 + scale) + shift).astype(BF16)
        h_ref[...] = hb
        p_ref[...] = _dot_nt(hb, wp_ref[...]).astype(BF16)

    row = lambda i: (i, 0)
    return pl.pallas_call(
        body, name="mixin_fwd", grid=(t // tm,),
        in_specs=[pl.BlockSpec((tm, d), row), _whole(mod.shape), _whole(nw.shape), _whole(wp.shape)],
        out_specs=[pl.BlockSpec((tm, d), row), pl.BlockSpec((tm, PROJ_COLS), row)],
        out_shape=[_sds((t, d), BF16), _sds((t, PROJ_COLS), BF16)], compiler_params=_params(1),
    )(xs, mod, nw, wp)


def _mixin_bwd(dp0, duv, xs, dres, mod, nw, wp, s, nb, tm):
    t_all, d = xs.shape
    nlat = dres.shape[0] // tm

    def body(p0_ref, uv_ref, x_ref, dr_ref, mod_ref, nw_ref, wp_ref, dx_ref, dmod_ref, dnw_ref):
        i = pl.program_id(0)

        @pl.when(i == 0)
        def _():
            dmod_ref[...] = jnp.zeros_like(dmod_ref)
            dnw_ref[...] = jnp.zeros_like(dnw_ref)

        lat = i < nlat
        g = jnp.minimum((i * tm) // s, nb)
        scale = mod_ref[g, pl.ds(4, 1), :]
        dh = _dot(p0_ref[...], wp_ref[0:512, :])
        extra = _dot(uv_ref[...], wp_ref[512:1536, :])
        dh = dh + jnp.where(lat, extra, 0.0)
        x = x_ref[...]
        r = lax.rsqrt(jnp.mean(x * x, axis=-1, keepdims=True) + EPS)
        xh = x * r
        n = xh * nw_ref[...]
        dmod_ref[g, pl.ds(3, 1), :] += jnp.sum(dh, axis=0, keepdims=True)
        dmod_ref[g, pl.ds(4, 1), :] += jnp.sum(dh * n, axis=0, keepdims=True)
        dn = dh * (1.0 + scale)
        dnw_ref[...] += jnp.sum(dn * xh, axis=0, keepdims=True)
        dxh = dn * nw_ref[...]
        dx_ref[...] = jnp.where(lat, dr_ref[...], 0.0) + r * (dxh - xh * jnp.mean(dxh * xh, axis=-1, keepdims=True))

    row = lambda i: (i, 0)
    lrow = lambda i: (jnp.minimum(i, nlat - 1), 0)
    return pl.pallas_call(
        body, name="mixin_bwd", grid=(t_all // tm,),
        in_specs=[pl.BlockSpec((tm, 512), row), pl.BlockSpec((tm, 1024), lrow), pl.BlockSpec((tm, d), row),
                  pl.BlockSpec((tm, d), lrow), _whole(mod.shape), _whole(nw.shape), _whole(wp.shape)],
        out_specs=[pl.BlockSpec((tm, d), row), pl.BlockSpec(mod.shape, lambda i: (0, 0, 0)),
                   pl.BlockSpec((1, d), lambda i: (0, 0))],
        out_shape=[_sds((t_all, d), F32), _sds(mod.shape, F32), _sds((1, d), F32)], compiler_params=_params(1),
    )(dp0, duv, xs, dres, mod, nw, wp)


def _prep_fwd(proj, row0, nb, s, pos0, sk, key0, into, tabs, wq, wk, wv, kvaw, qaw, qnw, knw, tm, with_q, name):
    nblk = s // tm
    n_into = 0 if into is None else 2

    def body(p_ref, cos_ref, sa_ref, sb_ref, wq_ref, wk_ref, wv_ref, kvaw_ref, qaw_ref, qnw_ref, knw_ref, *rest):
        outs, heads_ref = rest[n_into:-1], rest[-1]
        q_ref, k_ref, v_ref = outs if with_q else (None,) + outs
        cos, sin_a, sin_b = cos_ref[...][None], sa_ref[...][None], sb_ref[...][None]

        def normed_roped(w_ref, src, extra, nw_ref, o_ref, post):
            for h in range(HEADS):
                heads_ref[h] = _dot_nt(src, w_ref[h]) if extra is None else _dot(src, w_ref[h])
            xp = heads_ref[...] if extra is None else heads_ref[...] + extra[None]
            r = lax.rsqrt(jnp.sum(xp * xp, axis=-1, keepdims=True) * (1.0 / QK_HEAD) + EPS)
            o_ref[...] = _rope3(xp * r * (nw_ref[...] * post)[None], cos, sin_a, sin_b).astype(BF16)

        ckv = p_ref[:, 0:128].astype(F32)
        rkv = lax.rsqrt(jnp.mean(ckv * ckv, axis=-1, keepdims=True) + EPS)
        ckvb = (ckv * rkv * kvaw_ref[...]).astype(BF16)
        normed_roped(wk_ref, ckvb, p_ref[:, 128:256].astype(F32), knw_ref, k_ref, 1.0)
        for j in range(HEADS // 2):
            v_ref[j] = _dot(ckvb, wv_ref[j]).astype(BF16)
        if with_q:
            cq = p_ref[:, 256:512].astype(F32)
            rq = lax.rsqrt(jnp.mean(cq * cq, axis=-1, keepdims=True) + EPS)
            normed_roped(wq_ref, (cq * rq * qaw_ref[...]).astype(BF16), None, qnw_ref, q_ref, SOFTMAX_SCALE)

    tab = pl.BlockSpec((tm, HEAD_PAD), lambda i: (pos0 + i % nblk, 0))
    qspec = pl.BlockSpec((None, HEADS, tm, HEAD_PAD), lambda i: (i // nblk, 0, i % nblk, 0))
    kspec = pl.BlockSpec((None, HEADS, tm, HEAD_PAD), lambda i: (i // nblk, 0, key0 + i % nblk, 0))
    vspec = pl.BlockSpec((None, HEADS // 2, tm, HEAD_PAD), lambda i: (i // nblk, 0, key0 + i % nblk, 0))
    qshape = _sds((nb, HEADS, s, HEAD_PAD), BF16)
    kshape = _sds((nb, HEADS, sk, HEAD_PAD), BF16)
    vshape = _sds((nb, HEADS // 2, sk, HEAD_PAD), BF16)
    n_q = 1 if with_q else 0
    return pl.pallas_call(
        body, name=name, grid=(nb * nblk,),
        in_specs=[pl.BlockSpec((tm, 512), lambda i: (row0 + i, 0)), tab, tab, tab, _whole(wq.shape), _whole(wk.shape),
                  _whole(wv.shape), _whole(kvaw.shape), _whole(qaw.shape), _whole(qnw.shape), _whole(knw.shape)]
        + [pl.BlockSpec(memory_space=pl.ANY)] * n_into,
        out_specs=([qspec] if with_q else []) + [kspec, vspec],
        out_shape=([qshape] if with_q else []) + [kshape, vshape],
        scratch_shapes=[pltpu.VMEM((HEADS, tm, HEAD_PAD), F32)],
        input_output_aliases={11: n_q, 12: n_q + 1} if n_into else {}, compiler_params=_params(1),
    )(proj, *tabs, wq, wk, wv, kvaw, qaw, qnw, knw, *(into or ()))


def _prep_bwd(proj, row0, nb, s, pos0, key0, dp_rows, dp_into, tabs, wq, wk, wv, kvaw, qaw, qnw, knw, dq, dk, dv, init, tm,
              name):
    nblk = s // tm
    with_q = dq is not None
    n_init = 0 if init is None else len(init)
    n_into = 0 if dp_into is None else 1

    def body(*refs):
        p_ref, cos_ref, sa_ref, sb_ref, wq_ref, wk_ref, wv_ref, kvaw_ref, qaw_ref, qnw_ref, knw_ref = refs[:11]
        rest = list(refs[11:])
        dq_ref = rest.pop(0) if with_q else None
        dk_ref, dv_ref = rest.pop(0), rest.pop(0)
        init_refs = [rest.pop(0) for _ in range(n_init)]
        if n_into:
            rest.pop(0)
        dp_ref = rest.pop(0)
        if with_q:
            dwq_ref, dqaw_ref, dqnw_ref = rest.pop(0), rest.pop(0), rest.pop(0)
        dwk_ref, dwv_ref, dkvaw_ref, dknw_ref, heads_ref, dhb_ref, dkr_ref = rest
        accs = [dwk_ref, dwv_ref, dkvaw_ref, dknw_ref]

        @pl.when(pl.program_id(0) == 0)
        def _():
            for k, acc in enumerate(accs):
                acc[...] = init_refs[k][...] if n_init else jnp.zeros_like(acc)
            if with_q:
                dwq_ref[...] = jnp.zeros_like(dwq_ref)
                dqaw_ref[...] = jnp.zeros_like(dqaw_ref)
                dqnw_ref[...] = jnp.zeros_like(dqnw_ref)

        cos, sin_a, sin_b = cos_ref[...][None], sa_ref[...][None], sb_ref[...][None]
        lane = lax.broadcasted_iota(jnp.int32, (tm, HEAD_PAD), 1)
        rope_lanes = (lane >= QK_NOPE) & (lane < QK_HEAD)

        def heads_bwd(w_ref, src, extra, nw_ref, d_ref, dnw_ref, dw_ref, post):
            w_t = extra is None
            for h in range(HEADS):
                heads_ref[h] = _dot_nt(src, w_ref[h]) if w_t else _dot(src, w_ref[h])
            xp = heads_ref[...] if extra is None else heads_ref[...] + extra[None]
            r = lax.rsqrt(jnp.sum(xp * xp, axis=-1, keepdims=True) * (1.0 / QK_HEAD) + EPS)
            xh = xp * r
            dn = _rope3_t(d_ref[...], cos, sin_a, sin_b)
            dnw_ref[...] += post * jnp.sum(jnp.sum(dn * xh, axis=0), axis=0, keepdims=True)
            dxh = dn * (nw_ref[...] * post)[None]
            dxp = r * (dxh - xh * (jnp.sum(dxh * xh, axis=-1, keepdims=True) * (1.0 / QK_HEAD)))
            dhb_ref[...] = dxp.astype(BF16)
            dsrc = jnp.zeros((tm, src.shape[1]), F32)
            for h in range(HEADS):
                dsrc = dsrc + (_dot(dhb_ref[h], w_ref[h]) if w_t else _dot_nt(dhb_ref[h], w_ref[h]))
                dw_ref[h] += _dot_tn(src, dhb_ref[h])
            return dsrc, jnp.sum(dxp, axis=0)

        ckv = p_ref[:, 0:128].astype(F32)
        rkv = lax.rsqrt(jnp.mean(ckv * ckv, axis=-1, keepdims=True) + EPS)
        ckvh = ckv * rkv
        ckvb = (ckvh * kvaw_ref[...]).astype(BF16)
        for h in range(HEADS):
            dkr_ref[h] = dk_ref[h].T
        dckv, dkp_sum = heads_bwd(wk_ref, ckvb, p_ref[:, 128:256].astype(F32), knw_ref, dkr_ref, dknw_ref, dwk_ref,
                                  1.0)
        for j in range(HEADS // 2):
            dvb = dv_ref[j].T.astype(BF16)
            dckv = dckv + _dot_nt(dvb, wv_ref[j])
            dwv_ref[j] += _dot_tn(ckvb, dvb)
        dkvaw_ref[...] += jnp.sum(dckv * ckvh, axis=0, keepdims=True)
        dch = dckv * kvaw_ref[...]
        dp_ref[:, 0:128] = (rkv * (dch - ckvh * jnp.mean(dch * ckvh, axis=-1, keepdims=True))).astype(BF16)
        dp_ref[:, 128:256] = jnp.where(rope_lanes, dkp_sum, 0.0).astype(BF16)
        if with_q:
            cq = p_ref[:, 256:512].astype(F32)
            rq = lax.rsqrt(jnp.mean(cq * cq, axis=-1, keepdims=True) + EPS)
            cqh = cq * rq
            cqb = (cqh * qaw_ref[...]).astype(BF16)
            dcq, _ = heads_bwd(wq_ref, cqb, None, qnw_ref, dq_ref, dqnw_ref, dwq_ref, SOFTMAX_SCALE)
            dqaw_ref[...] += jnp.sum(dcq * cqh, axis=0, keepdims=True)
            dqc = dcq * qaw_ref[...]
            dp_ref[:, 256:512] = (rq * (dqc - cqh * jnp.mean(dqc * cqh, axis=-1, keepdims=True))).astype(BF16)
        else:
            dp_ref[:, 256:512] = jnp.zeros((tm, Q_LORA), BF16)

    tab = pl.BlockSpec((tm, HEAD_PAD), lambda i: (pos0 + i % nblk, 0))
    qspec = pl.BlockSpec((None, HEADS, tm, HEAD_PAD), lambda i: (i // nblk, 0, i % nblk, 0))
    kspec = pl.BlockSpec((None, HEADS, HEAD_PAD, tm), lambda i: (i // nblk, 0, 0, key0 + i % nblk))
    vspec = pl.BlockSpec((None, HEADS // 2, HEAD_PAD, tm), lambda i: (i // nblk, 0, 0, key0 + i % nblk))

    def acc_spec(shape):
        nd = len(shape)
        return pl.BlockSpec(shape, lambda i: (0,) * nd)

    acc_shapes = [(HEADS, KV_LORA, HEAD_PAD), (HEADS // 2, KV_LORA, HEAD_PAD), (1, KV_LORA), (1, HEAD_PAD)]
    q_shapes = [(HEADS, Q_LORA, HEAD_PAD), (1, Q_LORA), (1, HEAD_PAD)] if with_q else []
    out_shapes = [(dp_rows, 512)] + q_shapes + acc_shapes
    n_before = 11 + (1 if with_q else 0) + 2 + n_init
    return pl.pallas_call(
        body, name=name, grid=(nb * nblk,),
        in_specs=[pl.BlockSpec((tm, 512), lambda i: (row0 + i, 0)), tab, tab, tab, _whole(wq.shape), _whole(wk.shape),
                  _whole(wv.shape), _whole(kvaw.shape), _whole(qaw.shape), _whole(qnw.shape), _whole(knw.shape)]
        + ([qspec] if with_q else []) + [kspec, vspec] + [_whole(a.shape) for a in (init or [])]
        + [pl.BlockSpec(memory_space=pl.ANY)] * n_into,
        out_specs=[pl.BlockSpec((tm, 512), lambda i: (row0 + i, 0))] + [acc_spec(sh) for sh in q_shapes + acc_shapes],
        out_shape=[_sds(out_shapes[0], BF16)] + [_sds(sh, F32) for sh in out_shapes[1:]],
        scratch_shapes=[pltpu.VMEM((HEADS, tm, HEAD_PAD), F32), pltpu.VMEM((HEADS, tm, HEAD_PAD), BF16),
                        pltpu.VMEM((HEADS, tm, HEAD_PAD), F32)],
        input_output_aliases={n_before: 0} if n_into else {}, compiler_params=_params(1),
    )(proj, *tabs, wq, wk, wv, kvaw, qaw, qnw, knw, *([dq] if with_q else []), dk, dv, *(init or []),
      *([dp_into] if n_into else []))


def _attn_fwd(q, k, v, tq, exch=None):
    nb, _, s, _ = q.shape
    sk = k.shape[2]
    nq = s // tq

    def body(q_ref, k_ref, v_ref, o_ref, lse_ref, vext_ref):
        @pl.when(pl.program_id(2) == 0)
        def _():
            vext_ref[:, 0:HEAD_PAD] = v_ref[...]
            vext_ref[:, HEAD_PAD:2 * HEAD_PAD] = jnp.ones((sk, HEAD_PAD), BF16)

        lane = lax.broadcasted_iota(jnp.int32, (tq, HEAD_PAD), 1)
        outs = []
        for hh in range(2):
            sc = _dot_nt(q_ref[hh], k_ref[hh])
            m = jnp.max(sc, axis=-1, keepdims=True)
            pv = _dot(jnp.exp2(sc - m).astype(BF16), vext_ref[...])
            l = pv[:, HEAD_PAD:HEAD_PAD + 1]
            outs.append(pv[:, 0:HEAD_PAD] / l)
            lse_ref[hh] = m + jnp.log2(l)
        o_ref[...] = jnp.where(lane < V_HEAD, outs[0], outs[1]).astype(BF16)

    (o, lse), got = _hosted_call(
        body, "attn_fwd", (nb, HEADS // 2, nq),
        [pl.BlockSpec((None, 2, tq, HEAD_PAD), lambda b, j, i: (b, j, i, 0)),
         pl.BlockSpec((None, 2, sk, HEAD_PAD), lambda b, j, i: (b, j, 0, 0)),
         pl.BlockSpec((None, None, sk, HEAD_PAD), lambda b, j, i: (b, j, 0, 0))],
        [pl.BlockSpec((tq, HEAD_PAD), lambda b, j, i: (b * nq + i, j)),
         pl.BlockSpec((None, 2, tq, 1), lambda b, j, i: (b, j, i, 0))],
        [_sds((nb * s, MLA_W + GMLP_W), BF16), _sds((nb, HEADS, s, 1), F32)], (q, k, v),
        scratch=[pltpu.VMEM((sk, 2 * HEAD_PAD), BF16)], exch=exch)
    return o, lse, got


def _attn_bwd(q, k, v, do, o, lse, tq, exch=None):
    nb, _, s, _ = q.shape
    sk = k.shape[2]
    nq = s // tq

    def body(q_ref, k_ref, v_ref, do_ref, o_ref, lse_ref, dq_ref, dkt_ref, dvt_ref):
        @pl.when(pl.program_id(2) == 0)
        def _():
            dkt_ref[...] = jnp.zeros_like(dkt_ref)
            dvt_ref[...] = jnp.zeros_like(dvt_ref)

        lane = lax.broadcasted_iota(jnp.int32, (tq, HEAD_PAD), 1)
        dov = do_ref[...]
        prod = dov.astype(F32) * o_ref[...].astype(F32)
        for hh in range(2):
            mine = (lane < V_HEAD) if hh == 0 else (lane >= V_HEAD)
            doh = jnp.where(mine, dov, jnp.zeros_like(dov))
            delta = jnp.sum(jnp.where(mine, prod, 0.0), axis=-1, keepdims=True)
            qh = q_ref[hh]
            q_ln2 = (qh.astype(F32) * LN2).astype(BF16)
            kv = k_ref[hh]
            p = jnp.exp2(_dot_nt(qh, kv) - lse_ref[hh])
            u = (p * (_dot_nt(doh, v_ref[...]) - delta)).astype(BF16)
            dq_ref[hh] = _dot(u, kv) * LN2
            dkt_ref[hh] += _dot_tn(q_ln2, u)
            dvt_ref[...] += _dot_tn(doh, p.astype(BF16))

    qspec = pl.BlockSpec((None, 2, tq, HEAD_PAD), lambda b, j, i: (b, j, i, 0))
    kspec = pl.BlockSpec((None, 2, sk, HEAD_PAD), lambda b, j, i: (b, j, 0, 0))
    vspec = pl.BlockSpec((None, None, sk, HEAD_PAD), lambda b, j, i: (b, j, 0, 0))
    ospec = pl.BlockSpec((tq, HEAD_PAD), lambda b, j, i: (b * nq + i, j))
    return _hosted_call(
        body, "attn_bwd", (nb, HEADS // 2, nq),
        [qspec, kspec, vspec, ospec, ospec, pl.BlockSpec((None, 2, tq, 1), lambda b, j, i: (b, j, i, 0))],
        [qspec, pl.BlockSpec((None, 2, HEAD_PAD, sk), lambda b, j, i: (b, j, 0, 0)),
         pl.BlockSpec((None, None, HEAD_PAD, sk), lambda b, j, i: (b, j, 0, 0))],
        [_sds(q.shape, F32), _sds((nb, HEADS, HEAD_PAD, sk), F32), _sds((nb, HEADS // 2, HEAD_PAD, sk), F32)],
        (q, k, v, do, o, lse), exch=exch)


def _group_masks(rows):
    lane = lax.broadcasted_iota(jnp.int32, (rows, GMLP_W), 1)
    return [(lane >= g * GROUP_DIM) & (lane < (g + 1) * GROUP_DIM) for g in range(GROUPS)]


def _gmlp_fwd(proj, mixcat, wcat, bias, vnw, ones, tm):
    t = mixcat.shape[0]

    def body(u_ref, v_ref, wcat_ref, bias_ref, vnw_ref, ones_ref, _, o_ref):
        masks = _group_masks(CHUNK)
        gv = _gelu(v_ref[...].astype(F32))
        rv = lax.rsqrt(_group_sum(gv * gv, ones_ref) * (1.0 / GROUP_DIM) + EPS)
        vnb = (gv * rv * vnw_ref[...]).astype(BF16)
        for c in range(tm // CHUNK):
            rows = slice(c * CHUNK, (c + 1) * CHUNK)
            vc = vnb[rows]
            stack = jnp.concatenate([jnp.where(m, vc, jnp.zeros_like(vc)) for m in masks], axis=0)
            sp = _dot(wcat_ref[...], stack) + bias_ref[...]
            o_ref[rows, :] = (_gelu(u_ref[rows, :].astype(F32)) * sp).astype(BF16)

    return pl.pallas_call(
        body, name="gmlp_fwd", grid=(t // tm,),
        in_specs=[pl.BlockSpec((tm, GMLP_W), lambda i: (i, 1)), pl.BlockSpec((tm, GMLP_W), lambda i: (i, 2)),
                  _whole(wcat.shape), _whole(bias.shape), _whole(vnw.shape), _whole(ones.shape),
                  pl.BlockSpec(memory_space=pl.ANY)],
        out_specs=pl.BlockSpec((tm, GMLP_W), lambda i: (i, 1)),
        out_shape=_sds(mixcat.shape, BF16), input_output_aliases={6: 0}, compiler_params=_params(1),
    )(proj, proj, wcat, bias, vnw, ones, mixcat)


def _gmlp_bwd(proj, dsg, wcat, wcat_t, bias, vnw, ones, tm):
    t = dsg.shape[0]

    def body(u_ref, v_ref, dsg_ref, wcat_ref, wcatt_ref, bias_ref, vnw_ref, ones_ref,
             duv_ref, dws_ref, dbs_ref, dvnw_ref):
        @pl.when(pl.program_id(0) == 0)
        def _():
            dws_ref[...] = jnp.zeros_like(dws_ref)
            dbs_ref[...] = jnp.zeros_like(dbs_ref)
            dvnw_ref[...] = jnp.zeros_like(dvnw_ref)

        masks = _group_masks(CHUNK)
        v = v_ref[...].astype(F32)
        gv = _gelu(v)
        rv = lax.rsqrt(_group_sum(gv * gv, ones_ref) * (1.0 / GROUP_DIM) + EPS)
        xh = gv * rv
        vnb = (xh * vnw_ref[...]).astype(BF16)
        dvn_parts = []
        for c in range(tm // CHUNK):
            rows = slice(c * CHUNK, (c + 1) * CHUNK)
            vc = vnb[rows]
            stack = jnp.concatenate([jnp.where(m, vc, jnp.zeros_like(vc)) for m in masks], axis=0)
            sp = _dot(wcat_ref[...], stack) + bias_ref[...]
            u = u_ref[rows, :].astype(F32)
            dsg_c = dsg_ref[rows, :]
            duv_ref[rows, 0:GMLP_W] = (dsg_c * sp * _gelu_grad(u)).astype(BF16)
            ds = dsg_c * _gelu(u)
            dstack = jnp.concatenate([jnp.where(m, ds, 0.0) for m in masks], axis=0)
            dbs_ref[...] += jnp.broadcast_to(jnp.sum(dstack, axis=-1, keepdims=True), dbs_ref.shape)
            dstb = dstack.astype(BF16)
            dvn_parts.append(_dot(wcatt_ref[...], dstb))
            dws_ref[...] += _dot_nt(dstb, vc)
        dvn = jnp.concatenate(dvn_parts, axis=0) if len(dvn_parts) > 1 else dvn_parts[0]
        dvnw_ref[...] += jnp.sum(dvn * xh, axis=0, keepdims=True)
        dxh = dvn * vnw_ref[...]
        gm = _group_sum(dxh * xh, ones_ref) * (1.0 / GROUP_DIM)
        duv_ref[:, GMLP_W:2 * GMLP_W] = (rv * (dxh - xh * gm) * _gelu_grad(v)).astype(BF16)

    row = pl.BlockSpec((tm, GMLP_W), lambda i: (i, 0))
    return pl.pallas_call(
        body, name="gmlp_bwd", grid=(t // tm,),
        in_specs=[pl.BlockSpec((tm, GMLP_W), lambda i: (i, 1)), pl.BlockSpec((tm, GMLP_W), lambda i: (i, 2)), row,
                  _whole(wcat.shape), _whole(wcat_t.shape), _whole(bias.shape), _whole(vnw.shape), _whole(ones.shape)],
        out_specs=[pl.BlockSpec((tm, 2 * GMLP_W), lambda i: (i, 0)), pl.BlockSpec((GROUPS * CHUNK, CHUNK), lambda i: (0, 0)),
                   pl.BlockSpec((GROUPS * CHUNK, CHUNK), lambda i: (0, 0)), pl.BlockSpec((1, GMLP_W), lambda i: (0, 0))],
        out_shape=[_sds((t, 2 * GMLP_W), BF16), _sds((GROUPS * CHUNK, CHUNK), F32), _sds((GROUPS * CHUNK, CHUNK), F32),
                   _sds((1, GMLP_W), F32)],
        compiler_params=_params(1),
    )(proj, proj, dsg, wcat, wcat_t, bias, vnw, ones)


def _mixout_fwd(mixcat, xs, mod, wout, s, tm):
    t, width = mixcat.shape
    d = xs.shape[1]

    def body(cat_ref, x_ref, mod_ref, w_ref, x2_ref, mix_ref):
        g = (pl.program_id(0) * tm) // s
        gate = mod_ref[g, pl.ds(5, 1), :]
        mix = _dot(cat_ref[...], w_ref[...])
        x2_ref[...] = x_ref[...] + gate * mix
        mix_ref[...] = mix.astype(BF16)

    row = lambda i: (i, 0)
    return pl.pallas_call(
        body, name="mixout_fwd", grid=(t // tm,),
        in_specs=[pl.BlockSpec((tm, width), row), pl.BlockSpec((tm, d), row), _whole(mod.shape), _whole(wout.shape)],
        out_specs=[pl.BlockSpec((tm, d), row), pl.BlockSpec((tm, d), row)],
        out_shape=[_sds((t, d), F32), _sds((t, d), BF16)], compiler_params=_params(1),
    )(mixcat, xs, mod, wout)


def _mixout_bwd(dx2, mix, mod, wout, s, tm):
    t, d = dx2.shape

    def body(dx_ref, mix_ref, mod_ref, w_ref, dmix_ref, do_ref, dsg_ref, dmod_ref):
        i = pl.program_id(0)

        @pl.when(i == 0)
        def _():
            dmod_ref[...] = jnp.zeros_like(dmod_ref)

        g = (i * tm) // s
        gate = mod_ref[g, pl.ds(5, 1), :]
        dx = dx_ref[...]
        dmod_ref[g, pl.ds(5, 1), :] += jnp.sum(dx * mix_ref[...].astype(F32), axis=0, keepdims=True)
        dmb = (gate * dx).astype(BF16)
        dmix_ref[...] = dmb
        do_ref[...] = _dot_nt(dmb, w_ref[0:MLA_W, :]).astype(BF16)
        dsg_ref[...] = _dot_nt(dmb, w_ref[MLA_W:MLA_W + GMLP_W, :])

    row = lambda i: (i, 0)
    return pl.pallas_call(
        body, name="mixout_bwd", grid=(t // tm,),
        in_specs=[pl.BlockSpec((tm, d), row), pl.BlockSpec((tm, d), row), _whole(mod.shape), _whole(wout.shape)],
        out_specs=[pl.BlockSpec((tm, d), row), pl.BlockSpec((tm, MLA_W), row), pl.BlockSpec((tm, GMLP_W), row),
                   pl.BlockSpec(mod.shape, lambda i: (0, 0, 0))],
        out_shape=[_sds((t, d), BF16), _sds((t, MLA_W), BF16), _sds((t, GMLP_W), F32), _sds(mod.shape, F32)],
        compiler_params=_params(1),
    )(dx2, mix, mod, wout)


def _swap_cores(parts, name):
    n = len(parts)

    def body(*refs):
        srcs, outs, send_sems, recv_sems = refs[:n], refs[n:2 * n], refs[2 * n], refs[2 * n + 1]
        x, y, c = lax.axis_index("x"), lax.axis_index("y"), lax.axis_index("c")
        copies = [pltpu.make_async_remote_copy(
            src_ref=srcs[w], dst_ref=outs[w], send_sem=send_sems.at[w], recv_sem=recv_sems.at[w],
            device_id=(x, y, 1 - c), device_id_type=pl.DeviceIdType.MESH) for w in range(n)]
        for cp in copies:
            cp.start()
        for cp in copies:
            cp.wait()

    any_spec = pl.BlockSpec(memory_space=pl.ANY)
    return pl.pallas_call(
        body, name=name, in_specs=[any_spec] * n, out_specs=[any_spec] * n,
        out_shape=[_sds(p.shape, p.dtype) for p in parts],
        scratch_shapes=[pltpu.SemaphoreType.DMA((n,)), pltpu.SemaphoreType.DMA((n,))],
    )(*parts)


def _row_tile(r, c, mult):
    return _div_tile(r, max(mult, (1 << 18) // c), mult)


def _sum_slots(recv, name):
    _, r, c = recv.shape
    tr = _row_tile(r, c, 16)

    def body(r_ref, o_ref):
        f = lambda k: r_ref[k].astype(F32)
        o_ref[...] = ((f(0) + f(1)) + f(2)) + f(3)

    return pl.pallas_call(
        body, name=name, grid=(r // tr,),
        in_specs=[pl.BlockSpec((N_CHIPS, tr, c), lambda i: (0, i, 0))],
        out_specs=pl.BlockSpec((tr, c), lambda i: (i, 0)),
        out_shape=_sds((r, c), F32), compiler_params=_params(1),
    )(recv)


def _adamw(parts, w, m, v, name, exch=None):
    r, wd = w.shape
    tr = _row_tile(r, wd, 8)
    c1 = 1.0 / (1.0 - ADAM_B1 ** ADAM_STEP)
    c2 = 1.0 / (1.0 - ADAM_B2 ** ADAM_STEP)
    n_p = len(parts)

    def body(*refs):
        p_refs = refs[:n_p]
        w_ref, m_ref, v_ref, g_ref, d_ref, nm_ref, nv_ref = refs[n_p:]
        g = p_refs[0][...]
        for p_ref in p_refs[1:]:
            g = g + p_ref[...]
        nm = ADAM_B1 * m_ref[...] + (1.0 - ADAM_B1) * g
        nv = ADAM_B2 * v_ref[...] + (1.0 - ADAM_B2) * (g * g)
        g_ref[...] = g
        nm_ref[...] = nm
        nv_ref[...] = nv
        d_ref[...] = -ADAM_LR * ((nm * c1) / (jnp.sqrt(nv * c2) + ADAM_EPS) + ADAM_WD * w_ref[...])

    spec = pl.BlockSpec((tr, wd), lambda i: (i, 0))
    return _hosted_call(body, name, (r // tr,), [spec] * (n_p + 3), [spec] * 4, [_sds((r, wd), F32)] * 4,
                        (*parts, w, m, v), exch=exch)


def _all_peers(x, y, c):
    flips = [(dx, dy, dc) for dx in (0, 1) for dy in (0, 1) for dc in (0, 1)][1:]
    return [(1 - x if dx else x, 1 - y if dy else y, 1 - c if dc else c) for dx, dy, dc in flips]


def _first_exchange(shards, later, cc, w, b):
    n_w, n_l = len(shards), len(later)
    n = w.shape[1]

    def body(*refs):
        src32, later_in, (cc_ref, w_ref, b_ref) = refs[:n_w], refs[n_w:n_w + n_l], refs[n_w + n_l:n_w + n_l + 3]
        o0 = n_w + n_l + 3
        outs, (all_ref, tab_ref), later_out = refs[o0:o0 + n_w], refs[o0 + n_w:o0 + n_w + 2], refs[o0 + n_w + 2:o0 + n_w + 2 + n_l]
        s0 = o0 + n_w + 2 + n_l
        srcs = refs[s0:s0 + n_w]
        (part_ref, ici_send, ici_recv, d2d_send, d2d_recv, local_sems, cc_send, cc_recv, tab_send,
         tab_recv) = refs[s0 + n_w:]
        for wi in range(n_w):
            srcs[wi][...] = src32[wi][...].astype(BF16)
        x, y, c = lax.axis_index("x"), lax.axis_index("y"), lax.axis_index("c")
        chip, dev = 2 * x + y, 4 * x + 2 * y + c
        chips = _other_chips(x, y)
        peers = _all_peers(x, y, c)

        def half(wi, which):
            hr = shards[wi].shape[0] // 2
            return pl.ds(pl.multiple_of(which * hr, 16), hr)

        def over_ici(wi, k, arriving):
            px, py = chips[k]
            slot = 2 * px + py if arriving else chip
            return pltpu.make_async_remote_copy(
                src_ref=srcs[wi].at[half(wi, c)], dst_ref=outs[wi].at[slot, half(wi, c)],
                send_sem=ici_send.at[3 * wi + k], recv_sem=ici_recv.at[3 * wi + k], device_id=(px, py, c),
                device_id_type=pl.DeviceIdType.MESH)

        def to_sibling(wi, k, arriving):
            px, py = chips[k]
            rows = half(wi, 1 - c if arriving else c)
            return pltpu.make_async_remote_copy(
                src_ref=outs[wi].at[2 * px + py, rows], dst_ref=outs[wi].at[2 * px + py, rows],
                send_sem=d2d_send.at[3 * wi + k], recv_sem=d2d_recv.at[3 * wi + k], device_id=(x, y, 1 - c),
                device_id_type=pl.DeviceIdType.MESH)

        def cc_copy(k, peer, slot):
            return pltpu.make_async_remote_copy(
                src_ref=cc_ref, dst_ref=all_ref.at[slot], send_sem=cc_send.at[k], recv_sem=cc_recv.at[k],
                device_id=peer, device_id_type=pl.DeviceIdType.MESH)

        def rows_of(px, py):
            return part_ref.at[pl.ds(pl.multiple_of((4 * px + 2 * py + c) * MOD_ROWS, MOD_ROWS), MOD_ROWS)]

        def tab_copy(k, px, py, slot):
            return pltpu.make_async_remote_copy(
                src_ref=rows_of(px, py), dst_ref=tab_ref.at[slot], send_sem=tab_send.at[k], recv_sem=tab_recv.at[k],
                device_id=(px, py, c), device_id_type=pl.DeviceIdType.MESH)

        local = [pltpu.make_async_copy(srcs[wi], outs[wi].at[chip], local_sems.at[wi]) for wi in range(n_w)]
        for cp in local:
            cp.start()
        pairs = [(wi, k) for wi in range(n_w) for k in range(3)]
        for wi, k in pairs:
            over_ici(wi, k, False).start()
        for k, peer in enumerate(peers):
            cc_copy(k, peer, dev).start()
        all_ref[dev] = cc_ref[...]
        for k, (px, py, pc) in enumerate(peers):
            cc_copy(k, (px, py, pc), 4 * px + 2 * py + pc).wait_recv()
        cv = all_ref[...].reshape(8 * MOD_ROWS, cc.shape[1])
        part_ref[...] = _dot((cv * _sigmoid(cv)).astype(BF16), w_ref[...]) + b_ref[...]
        for k, (px, py) in enumerate(chips):
            tab_copy(k, px, py, chip).start()
        tab_ref[chip] = rows_of(x, y)[...]
        for k, (px, py) in enumerate(chips):
            tab_copy(k, px, py, 2 * px + py).wait_recv()
        for j in range(n_l):
            later_out[j][...] = later_in[j][...].astype(BF16)
        for wi, k in pairs:
            over_ici(wi, k, True).wait_recv()
            to_sibling(wi, k, False).start()
        for wi, k in pairs:
            to_sibling(wi, k, True).wait_recv()
        for wi, k in pairs:
            over_ici(wi, k, False).wait_send()
            to_sibling(wi, k, False).wait_send()
        for k, peer in enumerate(peers):
            cc_copy(k, peer, dev).wait_send()
        for k, (px, py) in enumerate(chips):
            tab_copy(k, px, py, chip).wait_send()
        for cp in local:
            cp.wait()

    any_spec = pl.BlockSpec(memory_space=pl.ANY)
    vmem = pl.BlockSpec(memory_space=pltpu.VMEM)
    sems3 = pltpu.SemaphoreType.DMA((3 * n_w,))
    got = pl.pallas_call(
        body, name="first_exchange", in_specs=[vmem] * (n_w + n_l + 3),
        out_specs=[any_spec] * n_w + [vmem] * (2 + n_l),
        out_shape=[_sds((N_CHIPS,) + a.shape, BF16) for a in shards]
        + [_sds((8,) + cc.shape, F32), _sds((N_CHIPS, MOD_ROWS, n), F32)] + [_sds(a.shape, BF16) for a in later],
        scratch_shapes=[pltpu.VMEM(a.shape, BF16) for a in shards]
        + [pltpu.VMEM((8 * MOD_ROWS, n), F32), sems3, sems3, sems3, sems3, pltpu.SemaphoreType.DMA((n_w,)),
           pltpu.SemaphoreType.DMA((7,)), pltpu.SemaphoreType.DMA((7,)), pltpu.SemaphoreType.DMA((3,)),
           pltpu.SemaphoreType.DMA((3,))],
        compiler_params=pltpu.CompilerParams(vmem_limit_bytes=V7X_VMEM_LIMIT),
    )(*shards, *later, cc, w, b)
    return got[:n_w], got[n_w], got[n_w + 1], got[n_w + 2:]


def _ada_bwd_tp(cc_all, dmods, w, ctx_row):
    d, n = w.shape

    def body(cc_ref, m0, m1, m2, m3, w_ref, dw_ref, db_ref, dctx_ref, stage_ref, all_ref, send_sems, recv_sems):
        x, y, c = lax.axis_index("x"), lax.axis_index("y"), lax.axis_index("c")
        me = 4 * x + 2 * y + c
        dsum = m0[...] + m1[...] + m2[...] + m3[...]
        db_ref[...] = jnp.sum(dsum, axis=0, keepdims=True)
        for j in range(N_CHIPS):
            stage_ref[j] = dsum[:, j * n:(j + 1) * n]

        def copy(k, peer, slot):
            px, py, _ = peer
            return pltpu.make_async_remote_copy(
                src_ref=stage_ref.at[2 * px + py], dst_ref=all_ref.at[slot], send_sem=send_sems.at[k],
                recv_sem=recv_sems.at[k], device_id=peer, device_id_type=pl.DeviceIdType.MESH)

        peers = _all_peers(x, y, c)
        for k, peer in enumerate(peers):
            copy(k, peer, me).start()
        all_ref[me] = stage_ref[2 * x + y]
        for k, (px, py, pc) in enumerate(peers):
            copy(k, (px, py, pc), 4 * px + 2 * py + pc).wait_recv()
        for k, peer in enumerate(peers):
            copy(k, peer, me).wait_send()
        cv = cc_ref[...]
        sig = _sigmoid(cv)
        dmb = all_ref[...].reshape(8 * MOD_ROWS, n).astype(BF16)
        dw_ref[...] = _dot_tn((cv * sig).astype(BF16), dmb)
        dsc = _dot_nt(dmb, w_ref[...])
        dctx = dsc[ctx_row:ctx_row + 1, :]
        for dev in range(1, 8):
            dctx = dctx + dsc[dev * MOD_ROWS + ctx_row:dev * MOD_ROWS + ctx_row + 1, :]
        cx = cv[ctx_row:ctx_row + 1, :]
        sx = sig[ctx_row:ctx_row + 1, :]
        dctx_ref[...] = dctx * (sx * (1.0 + cx * (1.0 - sx))) * jnp.where(c == 0, 1.0, 0.0)

    vmem = pl.BlockSpec(memory_space=pltpu.VMEM)
    return pl.pallas_call(
        body, name="ada_bwd_tp", in_specs=[vmem] * 6, out_specs=[vmem] * 3,
        out_shape=[_sds((d, n), F32), _sds((1, N_MOD * d), F32), _sds((1, d), F32)],
        scratch_shapes=[pltpu.VMEM((N_CHIPS, MOD_ROWS, n), F32), pltpu.VMEM((8, MOD_ROWS, n), F32),
                        pltpu.SemaphoreType.DMA((7,)), pltpu.SemaphoreType.DMA((7,))],
        compiler_params=pltpu.CompilerParams(vmem_limit_bytes=V7X_VMEM_LIMIT),
    )(cc_all, *dmods, w)


def _rope_tables(s, ctx):
    pos = np.arange(s, dtype=np.float32)
    inv = (np.float32(ROPE_BASE) ** (-np.arange(0, QK_ROPE // 2, 2, dtype=np.float32) / np.float32(QK_ROPE // 2)))
    ang_r = np.floor(pos / GRID_W)[:, None] * inv
    ang_c = (pos - GRID_W * np.floor(pos / GRID_W))[:, None] * inv
    ang = np.concatenate([ang_r, ang_r, ang_c, ang_c], axis=-1).astype(np.float32)
    cos, sin = np.cos(ang), np.sin(ang)
    half_b = (np.arange(QK_ROPE) // 8) % 2 == 1
    sin_a = np.where(half_b, sin, 0.0)
    sin_b = np.where(half_b, 0.0, -sin)

    def place(tab, fill):
        full = np.full((s + ctx, HEAD_PAD), fill, np.float32)
        full[:s, QK_NOPE:QK_HEAD] = tab
        return jnp.asarray(full)

    return place(cos, 1.0), place(sin_a, 0.0), place(sin_b, 0.0)


def _pad_last(a, n):
    return jnp.pad(a, [(0, 0)] * (a.ndim - 1) + [(0, n - a.shape[-1])])


def _flat_rows(parts, rows, width):
    flat = jnp.concatenate([p.reshape(-1) for p in parts])
    return jnp.pad(flat, (0, rows * width - flat.shape[0])).reshape(rows, width)


def kernel(x, c, ctx, c_ctx, w_ada, b_ada, norm1_w, ffn1_w1, ffn1_w3, ffn1_w2, norm2_w, w_in, q_a_norm_w, w_uq, kv_a_norm_w, w_ukv, q_norm_w, k_norm_w, v_norm_w, w_s, b_s, w_out, norm3_w, ffn2_w1, ffn2_w3, ffn2_w2, loss_target, m_c_ctx, m_w_ada, m_b_ada, m_norm1_w, m_ffn1_w1, m_ffn1_w3, m_ffn1_w2, m_norm2_w, m_w_in, m_q_a_norm_w, m_w_uq, m_kv_a_norm_w, m_w_ukv, m_q_norm_w, m_k_norm_w, m_v_norm_w, m_w_s, m_b_s, m_w_out, m_norm3_w, m_ffn2_w1, m_ffn2_w3, m_ffn2_w2, v_c_ctx, v_w_ada, v_b_ada, v_norm1_w, v_ffn1_w1, v_ffn1_w3, v_ffn1_w2, v_norm2_w, v_w_in, v_q_a_norm_w, v_w_uq, v_kv_a_norm_w, v_w_ukv, v_q_norm_w, v_k_norm_w, v_v_norm_w, v_w_s, v_b_s, v_w_out, v_norm3_w, v_ffn2_w1, v_ffn2_w3, v_ffn2_w2):
    wts = dict(c_ctx=c_ctx, w_ada=w_ada, b_ada=b_ada, norm1_w=norm1_w, ffn1_w1=ffn1_w1, ffn1_w3=ffn1_w3, ffn1_w2=ffn1_w2,
               norm2_w=norm2_w, w_in=w_in, q_a_norm_w=q_a_norm_w, w_uq=w_uq, kv_a_norm_w=kv_a_norm_w, w_ukv=w_ukv,
               q_norm_w=q_norm_w, k_norm_w=k_norm_w, v_norm_w=v_norm_w, w_s=w_s, b_s=b_s, w_out=w_out, norm3_w=norm3_w,
               ffn2_w1=ffn2_w1, ffn2_w3=ffn2_w3, ffn2_w2=ffn2_w2)
    moms = dict(c_ctx=m_c_ctx, w_ada=m_w_ada, b_ada=m_b_ada, norm1_w=m_norm1_w, ffn1_w1=m_ffn1_w1, ffn1_w3=m_ffn1_w3,
                ffn1_w2=m_ffn1_w2, norm2_w=m_norm2_w, w_in=m_w_in, q_a_norm_w=m_q_a_norm_w, w_uq=m_w_uq,
                kv_a_norm_w=m_kv_a_norm_w, w_ukv=m_w_ukv, q_norm_w=m_q_norm_w, k_norm_w=m_k_norm_w, v_norm_w=m_v_norm_w,
                w_s=m_w_s, b_s=m_b_s, w_out=m_w_out, norm3_w=m_norm3_w, ffn2_w1=m_ffn2_w1, ffn2_w3=m_ffn2_w3,
                ffn2_w2=m_ffn2_w2)
    vars_ = dict(c_ctx=v_c_ctx, w_ada=v_w_ada, b_ada=v_b_ada, norm1_w=v_norm1_w, ffn1_w1=v_ffn1_w1, ffn1_w3=v_ffn1_w3,
                 ffn1_w2=v_ffn1_w2, norm2_w=v_norm2_w, w_in=v_w_in, q_a_norm_w=v_q_a_norm_w, w_uq=v_w_uq,
                 kv_a_norm_w=v_kv_a_norm_w, w_ukv=v_w_ukv, q_norm_w=v_q_norm_w, k_norm_w=v_k_norm_w, v_norm_w=v_v_norm_w,
                 w_s=v_w_s, b_s=v_b_s, w_out=v_w_out, norm3_w=v_norm3_w, ffn2_w1=v_ffn2_w1, ffn2_w3=v_ffn2_w3,
                 ffn2_w2=v_ffn2_w2)

    nb, s, d = x.shape
    nctx = ctx.shape[1]
    t, tc = nb * s, nb * nctx
    t_all = t + tc
    sk = s + nctx
    assert nb + 1 <= MOD_ROWS and d % LANES == 0
    tm = _token_tile(s, nctx)
    tq = _div_tile(s, 512, tm)
    tmx = _div_tile(math.gcd(s, tc), 512, tm)

    def held(n, a_):
        return jnp.swapaxes(a_[0], 0, 1) if n in T_WEIGHTS else a_[0]

    def unheld(n, a_):
        return (jnp.swapaxes(a_, 0, 1) if n in T_WEIGHTS else a_)[None]

    shard = {"w_ada": w_ada[0].astype(BF16)}
    full = {}

    def unshard(names, blocks):
        for n, g4 in zip(names, blocks):
            _, r_, c_ = g4.shape
            if n in ROW_SHARDED or n in T_WEIGHTS:
                full[n] = g4.reshape(N_CHIPS * r_, c_)
            else:
                full[n] = g4.transpose(1, 0, 2).reshape(r_, N_CHIPS * c_)

    def chip_major(n, g_):
        if n in ROW_SHARDED or n in T_WEIGHTS:
            return g_.reshape(N_CHIPS, g_.shape[0] // N_CHIPS, g_.shape[1]).astype(BF16)
        r_, cols = g_.shape
        return g_.reshape(r_, N_CHIPS, cols // N_CHIPS).transpose(1, 0, 2).astype(BF16)

    cc = jnp.concatenate([c, c_ctx[None, :], jnp.zeros((MOD_ROWS - nb - 1, d), F32)], axis=0)
    n_ada = shard["w_ada"].shape[1]
    assert n_ada % LANES == 0
    my_chip = 2 * lax.axis_index("x") + lax.axis_index("y")
    b_cols = lax.dynamic_slice_in_dim(b_ada, my_chip * n_ada, n_ada, axis=1)
    later = MIX_WEIGHTS + LAST_WEIGHTS
    got, cc_all, table, cast = _first_exchange([held(n, wts[n]) for n in FIRST_WEIGHTS],
                                               [held(n, wts[n]) for n in later], cc, shard["w_ada"], b_cols)
    unshard(FIRST_WEIGHTS, got)
    shard.update(zip(later, cast))
    cc_all = cc_all.reshape(8 * MOD_ROWS, d)
    mod = table.transpose(1, 0, 2).reshape(MOD_ROWS, N_MOD, d)
    wsb = w_s[0].astype(BF16)
    wcat = wsb.transpose(1, 0, 2).reshape(CHUNK, GROUPS * CHUNK)
    wcat_t = wsb.transpose(2, 0, 1).reshape(CHUNK, GROUPS * CHUNK)
    bias = jnp.repeat(b_s[0].T, GROUP_DIM, axis=1)
    vnw = v_norm_w.reshape(1, GMLP_W)
    lane = jnp.arange(GMLP_W)
    ones = (lane[:, None] // GROUP_DIM == lane[None, :] // GROUP_DIM).astype(BF16)
    qnw = _pad_last(q_norm_w, HEAD_PAD)
    knw = _pad_last(k_norm_w, HEAD_PAD)
    tabs = _rope_tables(s, nctx)

    x_lat, x_ctx = x.reshape(t, d), ctx.reshape(tc, d)
    (xs1, a1, b1, y1), got = _ffn_fwd(x_lat, x_ctx, mod, norm1_w, full["ffn1_w1"], full["ffn1_w3"], full["ffn1_w2"], 0, s,
                                      nb, tm, "ffn1_fwd", exch=("gather", [shard[n] for n in MIX_WEIGHTS]))
    unshard(MIX_WEIGHTS, got)
    wi = full["w_in"]
    wp = jnp.concatenate([wi[0:KV_LORA], jnp.zeros((QK_NOPE, d), BF16), wi[KV_LORA:KV_LORA + QK_ROPE],
                          jnp.zeros((HEAD_PAD - QK_HEAD, d), BF16), wi[KV_LORA + QK_ROPE:]], axis=0)
    wq = jnp.pad(full["w_uq"].reshape(HEADS, QK_HEAD, Q_LORA), ((0, 0), (0, HEAD_PAD - QK_HEAD), (0, 0)))
    wkv = full["w_ukv"].reshape(KV_LORA, HEADS, QK_NOPE + V_HEAD)
    wk = _pad_last(wkv[:, :, :QK_NOPE].transpose(1, 0, 2), HEAD_PAD)
    wv = wkv[:, :, QK_NOPE:].reshape(KV_LORA, HEADS // 2, 2 * V_HEAD).transpose(1, 0, 2)
    h2, proj = _mixin_fwd(xs1, mod, norm2_w, wp, s, nb, tmx)
    prep_w = (wq, wk, wv, kv_a_norm_w, q_a_norm_w, qnw, knw)
    q, k_all, v_all = _prep_fwd(proj, 0, nb, s, 0, sk, 0, None, tabs, *prep_w, tq, True, "prep_fwd")
    k_all, v_all = _prep_fwd(proj, t // tm, nb, nctx, s // tm, sk, s // tm, (k_all, v_all), tabs, *prep_w, tm, False,
                             "prep_ctx_fwd")
    o, lse, got = _attn_fwd(q, k_all, v_all, tq, exch=("gather", [shard[n] for n in LAST_WEIGHTS]))
    unshard(LAST_WEIGHTS, got)
    mixcat = _gmlp_fwd(proj, o, wcat, bias, vnw, ones, tq)
    x2, mix = _mixout_fwd(mixcat, xs1, mod, full["w_out"], s, tq)
    (dy, a2, b2, y2, loss_part), _ = _ffn_fwd(x2, None, mod, norm3_w, full["ffn2_w1"], full["ffn2_w3"], full["ffn2_w2"], 6,
                                              s, nb, tm, "ffn2_fwd", target=loss_target.reshape(t, d))

    grads, cm, recv = {}, {}, {}

    def scatter_of(names):
        return ("scatter", [cm[n] for n in names])

    (dx2, h3, g2, da2, db2, dyb2, dmod_c, grads["norm3_w"]), _ = _ffn_bwd(
        dy, x2, None, a2, b2, y2, mod, norm3_w, full["ffn2_w1"], full["ffn2_w3"], full["ffn2_w2"], 6, s, nb, tm,
        "ffn2_bwd")
    cm["ffn2_w1"] = chip_major("ffn2_w1", _mm_tn(da2, h3, t, "ffn2_dw1"))
    cm["ffn2_w3"] = chip_major("ffn2_w3", _mm_tn(db2, h3, t, "ffn2_dw3"))
    cm["ffn2_w2"] = chip_major("ffn2_w2", _mm_tn(g2, dyb2, t, "ffn2_dw2"))
    dmix, do, dsg, dmod_b = _mixout_bwd(dx2, mix, mod, full["w_out"], s, tq)
    cm["w_out"] = chip_major("w_out", _mm_tn(mixcat, dmix, t, "wout_dw"))
    duv, dws, dbs, dvnw = _gmlp_bwd(proj, dsg, wcat, wcat_t, bias, vnw, ones, tq)
    group = LAST_WEIGHTS + ("w_out",)
    (dq, dk, dv), got = _attn_bwd(q, k_all, v_all, do, mixcat, lse, tq, exch=scatter_of(group))
    recv.update(zip(group, got))
    dp0, dwk_c, dwv_c, dkvaw_c, dknw_c = _prep_bwd(
        proj, t // tm, nb, nctx, s // tm, s // tm, t_all, None, tabs, *prep_w, None, dk, dv, None, tm, "prep_ctx_bwd")
    dp0, dwq, dqaw, dqnw, dwk, dwv, dkvaw, dknw = _prep_bwd(
        proj, 0, nb, s, 0, 0, t_all, dp0, tabs, *prep_w, dq, dk, dv, [dwk_c, dwv_c, dkvaw_c, dknw_c], tq, "prep_bwd")
    dxs1, dmod_a, grads["norm2_w"] = _mixin_bwd(dp0, duv, xs1, dx2, mod, norm2_w, wp, s, nb, tmx)
    dwp = jnp.concatenate([_mm_tn(dp0, h2, t_all, "win_dw_kvq"), _mm_tn(duv, h2, t, "win_dw_uv")], axis=0)
    cm["w_in"] = chip_major("w_in", jnp.concatenate(
        [dwp[0:KV_LORA], dwp[KV_LORA + QK_NOPE:KV_LORA + QK_HEAD], dwp[256:]], axis=0))
    cm["w_uq"] = chip_major("w_uq", dwq[:, :, :QK_HEAD].transpose(0, 2, 1).reshape(HEADS * QK_HEAD, Q_LORA))
    cm["w_ukv"] = chip_major("w_ukv", jnp.concatenate(
        [dwk[:, :, :QK_NOPE].transpose(1, 0, 2),
         dwv.transpose(1, 0, 2).reshape(KV_LORA, HEADS, V_HEAD)], axis=2).reshape(KV_LORA, HEADS * (QK_NOPE + V_HEAD)))
    (dx_lat, h1, g1, da1, db1, dyb1, dmod_0, grads["norm1_w"]), _ = _ffn_bwd(
        dxs1, x_lat, x_ctx, a1, b1, y1, mod, norm1_w, full["ffn1_w1"], full["ffn1_w3"], full["ffn1_w2"], 0, s, nb, tm,
        "ffn1_bwd")
    dmods = [m_.reshape(MOD_ROWS, N_MOD * d) for m_ in (dmod_0, dmod_a, dmod_b, dmod_c)]
    dw_ada, grads["b_ada"], dctx = _ada_bwd_tp(cc_all, dmods, shard["w_ada"], nb)
    grads["c_ctx"] = dctx[0]
    grads["q_a_norm_w"], grads["kv_a_norm_w"] = dqaw, dkvaw
    grads["q_norm_w"], grads["k_norm_w"] = dqnw[:, :QK_HEAD], dknw[:, :QK_HEAD]
    grads["v_norm_w"], grads["w_s"], grads["b_s"] = dvnw, dws, dbs[:, 0]
    grad_x = dx_lat.reshape(nb, s, d)
    n_small = sum(wts[n].size for n in SMALL)
    rows_s = _round_up(-(-(n_small + 1) // d), 16)
    cm["small"] = jnp.broadcast_to(_flat_rows([grads[n] for n in SMALL] + [loss_part], rows_s, d), (N_CHIPS, rows_s, d))
    group = ("w_in", "w_uq", "w_ukv", "small")
    dw2, got = _mm_tn(g1, dyb1, t_all, "ffn1_dw2", exch=scatter_of(group))
    recv.update(zip(group, got))
    cm["ffn1_w2"] = chip_major("ffn1_w2", dw2)
    dw1, got = _mm_tn(da1, h1, t_all, "ffn1_dw1", exch=scatter_of(("ffn1_w2",)))
    recv["ffn1_w2"] = got[0]
    cm["ffn1_w1"] = chip_major("ffn1_w1", dw1)
    dw3, got = _mm_tn(db1, h1, t_all, "ffn1_dw3", exch=scatter_of(("ffn1_w1",)))
    recv["ffn1_w1"] = got[0]
    cm["ffn1_w3"] = chip_major("ffn1_w3", dw3)
    stepped = {}
    stepped["w_ada"], got = _adamw([dw_ada], wts["w_ada"][0], moms["w_ada"][0], vars_["w_ada"][0], "adamw_w_ada",
                                   exch=scatter_of(("ffn1_w3",)))
    recv["ffn1_w3"] = got[0]

    reduced = tuple(n for n in SHARDED if n != "w_ada") + ("small",)
    part = {n: _sum_slots(recv[n], "sum_" + n) for n in reduced}
    early = LAST_WEIGHTS + ("w_out",)
    late = tuple(n for n in reduced if n not in early)
    sib = dict(zip(early, _swap_cores([part[n] for n in early], "swap_early")))
    sib.update(zip(late, _swap_cores([part[n] for n in late], "swap_late")))
    for n in reduced[:-1]:
        stepped[n], _ = _adamw([part[n], sib[n]], held(n, wts[n]), held(n, moms[n]), held(n, vars_[n]), "adamw_" + n)
    for n in SHARDED:
        stepped[n] = [unheld(n, a_) for a_ in stepped[n]]
    packed, _ = _adamw([part["small"], sib["small"]], _flat_rows([wts[n] for n in SMALL], rows_s, d),
                       _flat_rows([moms[n] for n in SMALL], rows_s, d), _flat_rows([vars_[n] for n in SMALL], rows_s, d),
                       "adamw_small")
    loss = packed[0].reshape(-1)[n_small]
    for n in SMALL:
        stepped[n] = []
    for a_ in packed:
        flat = a_.reshape(-1)
        off = 0
        for n in SMALL:
            stepped[n].append(flat[off:off + wts[n].size].reshape(wts[n].shape))
            off += wts[n].size
    return (loss, grad_x, *[stepped[n][0] for n in WEIGHTS], *[stepped[n][1] for n in WEIGHTS],
            *[stepped[n][2] for n in WEIGHTS], *[stepped[n][3] for n in WEIGHTS])
```

```python
import functools
import math

import jax
import jax.numpy as jnp
import numpy as np
from jax import lax
from jax.experimental import pallas as pl
from jax.experimental.pallas import tpu as pltpu

F32 = jnp.float32
BF16 = jnp.bfloat16

EPS = 1e-6
N_MOD = 9
HEADS = 8
QK_NOPE, QK_ROPE, V_HEAD = 64, 32, 64
QK_HEAD = QK_NOPE + QK_ROPE
HEAD_PAD = 128
LN2 = math.log(2.0)
SOFTMAX_SCALE = QK_HEAD ** -0.5 / LN2
Q_LORA, KV_LORA = 256, 128
GROUPS, GROUP_DIM, CHUNK = 8, 64, 128
GMLP_W = GROUPS * GROUP_DIM
MLA_W = HEADS * V_HEAD
IN_COLS = 1440
PROJ_COLS = 1536
GRID_W = 64
ROPE_BASE = 10000.0
MOD_ROWS = 16
ADAM_LR, ADAM_B1, ADAM_B2, ADAM_EPS, ADAM_WD, ADAM_STEP = 0.001, 0.9, 0.999, 1e-08, 0.01, 10
N_CHIPS = 4
LANES = 128
V7X_VMEM_LIMIT = 56 * 1024 * 1024
GELU_C = math.sqrt(2.0 / math.pi)

SHARDED = ("w_ada", "ffn1_w1", "ffn1_w3", "ffn1_w2", "w_in", "w_uq", "w_ukv", "w_out", "ffn2_w1", "ffn2_w3", "ffn2_w2")
ROW_SHARDED = ("ffn1_w2", "w_out", "ffn2_w2")
T_WEIGHTS = ("ffn1_w1", "ffn1_w3", "ffn2_w1", "ffn2_w3", "w_in", "w_uq")
FIRST_WEIGHTS = ("ffn1_w1", "ffn1_w3", "ffn1_w2")
MIX_WEIGHTS = ("w_in", "w_uq", "w_ukv", "w_out")
LAST_WEIGHTS = ("ffn2_w1", "ffn2_w3", "ffn2_w2")
SMALL = ("c_ctx", "b_ada", "norm1_w", "norm2_w", "q_a_norm_w", "kv_a_norm_w", "q_norm_w", "k_norm_w", "v_norm_w",
         "w_s", "b_s", "norm3_w")
WEIGHTS = ("c_ctx", "w_ada", "b_ada", "norm1_w", "ffn1_w1", "ffn1_w3", "ffn1_w2", "norm2_w", "w_in", "q_a_norm_w",
           "w_uq", "kv_a_norm_w", "w_ukv", "q_norm_w", "k_norm_w", "v_norm_w", "w_s", "b_s", "w_out", "norm3_w",
           "ffn2_w1", "ffn2_w3", "ffn2_w2")


def _round_up(n, m):
    return (n + m - 1) // m * m


def _div_tile(n, target, mult):
    best = None
    for t in range(mult, min(n, target) + 1, mult):
        if n % t == 0:
            best = t
    return n if best is None else best


def _dot(a, b):
    return lax.dot_general(a, b, (((1,), (0,)), ((), ())), preferred_element_type=F32)


def _dot_nt(a, b):
    return lax.dot_general(a, b, (((1,), (1,)), ((), ())), preferred_element_type=F32)


def _dot_tn(a, b):
    return lax.dot_general(a, b, (((0,), (0,)), ((), ())), preferred_element_type=F32)


def _sigmoid(x):
    return 1.0 / (1.0 + jnp.exp(-x))


def _gelu(x):
    return 0.5 * x * (1.0 + jnp.tanh(GELU_C * (x + 0.044715 * x * x * x)))


def _gelu_grad(x):
    t = jnp.tanh(GELU_C * (x + 0.044715 * x * x * x))
    return 0.5 * (1.0 + t) + 0.5 * x * (1.0 - t * t) * (GELU_C * (1.0 + 3 * 0.044715 * x * x))


def _rope3(x, cos, sin_a, sin_b):
    return x * cos + pltpu.roll(x, 8, 2) * sin_a + pltpu.roll(x, HEAD_PAD - 8, 2) * sin_b


def _rope3_t(d, cos, sin_a, sin_b):
    return d * cos + pltpu.roll(d * sin_a, HEAD_PAD - 8, 2) + pltpu.roll(d * sin_b, 8, 2)


def _group_sum(x, ones_ref):
    hi = x.astype(BF16)
    lo = (x - hi.astype(F32)).astype(BF16)
    return _dot(hi, ones_ref[...]) + _dot(lo, ones_ref[...])


def _params(n_axes):
    return pltpu.CompilerParams(dimension_semantics=("arbitrary",) * n_axes, vmem_limit_bytes=V7X_VMEM_LIMIT)


def _whole(shape):
    nd = len(shape)
    return pl.BlockSpec(shape, lambda *_: (0,) * nd, pipeline_mode=pl.Buffered(1))


def _sds(shape, dtype):
    return jax.ShapeDtypeStruct(shape, dtype)


def _token_tile(s, ctx):
    return _div_tile(math.gcd(s, ctx), 256, CHUNK)


def _other_chips(x, y):
    return [(1 - x, y), (x, 1 - y), (1 - x, 1 - y)]


def _exch_copies(kind, srcs, dsts, send_sems, recv_sems, local_sems, with_arrivals):
    x, y, c = lax.axis_index("x"), lax.axis_index("y"), lax.axis_index("c")
    me = 2 * x + y
    local, sends, arrivals = [], [], []
    for w, (src, dst) in enumerate(zip(srcs, dsts)):
        own = src if kind == "gather" else src.at[me]
        local.append(pltpu.make_async_copy(own, dst.at[me], local_sems.at[w]))
        for k, (px, py) in enumerate(_other_chips(x, y)):
            sem = dict(send_sem=send_sems.at[3 * w + k], recv_sem=recv_sems.at[3 * w + k], device_id=(px, py, c),
                       device_id_type=pl.DeviceIdType.MESH)
            out = src if kind == "gather" else src.at[2 * px + py]
            sends.append(pltpu.make_async_remote_copy(src_ref=out, dst_ref=dst.at[me], **sem))
            if with_arrivals:
                arrivals.append(pltpu.make_async_remote_copy(src_ref=own, dst_ref=dst.at[2 * px + py], **sem))
    return local, sends, arrivals


def _exch_start(kind, srcs, dsts, sems):
    local, sends, _ = _exch_copies(kind, srcs, dsts, *sems, with_arrivals=False)
    for cp in local + sends:
        cp.start()


def _exch_wait(kind, srcs, dsts, sems):
    local, sends, arrivals = _exch_copies(kind, srcs, dsts, *sems, with_arrivals=True)
    for cp in arrivals:
        cp.wait_recv()
    for cp in sends:
        cp.wait_send()
    for cp in local:
        cp.wait()


def _exch_scratch(n):
    return [pltpu.SemaphoreType.DMA((3 * n,)), pltpu.SemaphoreType.DMA((3 * n,)), pltpu.SemaphoreType.DMA((n,))]


def _exch_shapes(kind, arrays):
    return [_sds((N_CHIPS,) + a.shape if kind == "gather" else a.shape, a.dtype) for a in arrays]


def _sibling_copies(srcs, dsts, send_sems, recv_sems):
    x, y, c = lax.axis_index("x"), lax.axis_index("y"), lax.axis_index("c")
    return [pltpu.make_async_remote_copy(
        src_ref=src, dst_ref=dst, send_sem=send_sems.at[w], recv_sem=recv_sems.at[w], device_id=(x, y, 1 - c),
        device_id_type=pl.DeviceIdType.MESH) for w, (src, dst) in enumerate(zip(srcs, dsts))]


def _hosted_call(body, name, grid, in_specs, out_specs, out_shape, operands, scratch=(), exch=None, swap=None):
    n_axes = len(grid)
    if exch is None and swap is None:
        outs = pl.pallas_call(body, name=name, grid=grid, in_specs=list(in_specs), out_specs=list(out_specs),
                              out_shape=list(out_shape), scratch_shapes=list(scratch),
                              compiler_params=_params(n_axes))(*operands)
        return list(outs), []
    kind, arrays = exch if exch is not None else ("scatter", [])
    swaps = list(swap or [])
    n_in, n_out, n_sc, n_ex, n_sw = len(in_specs), len(out_specs), len(scratch), len(arrays), len(swaps)

    def hosted(*refs):
        cin, ein, sin = refs[:n_in], refs[n_in:n_in + n_ex], refs[n_in + n_ex:n_in + n_ex + n_sw]
        o0 = n_in + n_ex + n_sw
        cout, eout, sout = refs[o0:o0 + n_out], refs[o0 + n_out:o0 + n_out + n_ex], refs[o0 + n_out + n_ex:o0 + n_out + n_ex + n_sw]
        rest = refs[o0 + n_out + n_ex + n_sw:]
        csc, sems, swap_sems = rest[:n_sc], rest[n_sc:n_sc + 3], rest[n_sc + 3:]
        first = functools.reduce(jnp.logical_and, [pl.program_id(a) == 0 for a in range(n_axes)])
        last = functools.reduce(jnp.logical_and, [pl.program_id(a) == grid[a] - 1 for a in range(n_axes)])

        @pl.when(first)
        def _():
            if n_ex:
                _exch_start(kind, ein, eout, sems)
            for cp in _sibling_copies(sin, sout, *swap_sems) if n_sw else []:
                cp.start()

        body(*cin, *cout, *csc)

        @pl.when(last)
        def _():
            if n_ex:
                _exch_wait(kind, ein, eout, sems)
            for cp in _sibling_copies(sin, sout, *swap_sems) if n_sw else []:
                cp.wait()

    any_spec = pl.BlockSpec(memory_space=pl.ANY)
    swap_scratch = [pltpu.SemaphoreType.DMA((n_sw,)), pltpu.SemaphoreType.DMA((n_sw,))] if n_sw else []
    outs = pl.pallas_call(
        hosted, name=name, grid=grid, in_specs=list(in_specs) + [any_spec] * (n_ex + n_sw),
        out_specs=list(out_specs) + [any_spec] * (n_ex + n_sw),
        out_shape=list(out_shape) + _exch_shapes(kind, arrays) + [_sds(a.shape, a.dtype) for a in swaps],
        scratch_shapes=list(scratch) + _exch_scratch(max(n_ex, 1)) + swap_scratch, compiler_params=_params(n_axes),
    )(*operands, *arrays, *swaps)
    got = list(outs[n_out:n_out + n_ex])
    return (list(outs[:n_out]), got) if swap is None else (list(outs[:n_out]), got, list(outs[n_out + n_ex:]))


class _TokenTiles:
    def __init__(self, t, tc, tm):
        self.n_lat, self.n_ctx = t // tm, tc // tm
        self.n_all = self.n_lat + self.n_ctx

    def tile(self, i):
        return (i + self.n_lat) % self.n_all if self.n_ctx else i

    def is_lat(self, i):
        return self.tile(i) < self.n_lat

    def row(self, i):
        return (self.tile(i), 0)

    def lat_row(self, i):
        return (jnp.where(self.is_lat(i), self.tile(i), 0), 0) if self.n_ctx else (i, 0)

    def ctx_row(self, i):
        return (jnp.where(self.is_lat(i), self.n_ctx - 1, self.tile(i) - self.n_lat), 0)


def _ffn_fwd(x_lat, x_ctx, mod, nw, w1, w3, w2, k0, s, nb, tm, name, target=None, exch=None):
    t, d = x_lat.shape
    tc = 0 if x_ctx is None else x_ctx.shape[0]
    f = w1.shape[0]
    tiles = _TokenTiles(t, tc, tm)
    n_x = 2 if tc else 1
    n_t = 0 if target is None else 1
    assert not (tc and n_t)

    def body(*refs):
        x_ref = refs[0]
        t_ref = refs[n_x] if n_t else None
        mod_ref, nw_ref, w1_ref, w3_ref, w2_ref, o_ref, a_ref, b_ref, y_ref = refs[n_x + n_t:n_x + n_t + 9]
        i = pl.program_id(0)
        g = jnp.minimum((tiles.tile(i) * tm) // s, nb)
        shift = mod_ref[g, pl.ds(k0, 1), :]
        scale = mod_ref[g, pl.ds(k0 + 1, 1), :]
        gate = mod_ref[g, pl.ds(k0 + 2, 1), :]
        x = jnp.where(tiles.is_lat(i), x_ref[...], refs[1][...]) if tc else x_ref[...]
        r = lax.rsqrt(jnp.mean(x * x, axis=-1, keepdims=True) + EPS)
        hb = ((x * r * nw_ref[...]) * (1.0 + scale) + shift).astype(BF16)
        a = _dot_nt(hb, w1_ref[...])
        b = _dot_nt(hb, w3_ref[...])
        gb = (a * _sigmoid(a) * b).astype(BF16)
        y = _dot(gb, w2_ref[...])
        out = x + (0.5 * gate) * y
        a_ref[...] = a.astype(BF16)
        b_ref[...] = b.astype(BF16)
        y_ref[...] = y.astype(BF16)
        if n_t:
            loss_ref, acc_ref = refs[-2:]

            @pl.when(i == 0)
            def _():
                acc_ref[...] = jnp.zeros_like(acc_ref)

            e = out - t_ref[...]
            o_ref[...] = e * (1.0 / d)
            acc_ref[...] += jnp.sum(e * e, axis=0, keepdims=True)

            @pl.when(i == tiles.n_all - 1)
            def _():
                loss_ref[...] = (0.5 / d) * jnp.sum(acc_ref[...], axis=-1, keepdims=True)
        else:
            o_ref[...] = out

    td = pl.BlockSpec((tm, d), tiles.row)
    tf = pl.BlockSpec((tm, f), tiles.row)
    return _hosted_call(
        body, name, (tiles.n_all,),
        [pl.BlockSpec((tm, d), tiles.lat_row)] + ([pl.BlockSpec((tm, d), tiles.ctx_row)] if tc else []) + [td] * n_t
        + [_whole(mod.shape), _whole(nw.shape), _whole(w1.shape), _whole(w3.shape), _whole(w2.shape)],
        [td, tf, tf, td] + [pl.BlockSpec((1, 1), lambda i: (0, 0))] * n_t,
        [_sds((t + tc, d), F32), _sds((t + tc, f), BF16), _sds((t + tc, f), BF16), _sds((t + tc, d), BF16)]
        + [_sds((1, 1), F32)] * n_t,
        (x_lat,) + ((x_ctx,) if tc else ()) + ((target,) if n_t else ()) + (mod, nw, w1, w3, w2),
        scratch=[pltpu.VMEM((1, d), F32)] * n_t, exch=exch)


def _ffn_bwd(dout, x_lat, x_ctx, a, b, y, mod, nw, w1, w3, w2, k0, s, nb, tm, name, exch=None):
    t, d = x_lat.shape
    tc = 0 if x_ctx is None else x_ctx.shape[0]
    f = w1.shape[0]
    nch = 2 if (f // 2) % LANES == 0 and f % 2 == 0 else 1
    fc = f // nch
    tiles = _TokenTiles(t, tc, tm)
    n_x = 2 if tc else 1

    def body(*refs):
        do_ref, x_ref = refs[0], refs[1]
        (a_ref, b_ref, y_ref, mod_ref, nw_ref, w1_ref, w3_ref, w2_ref,
         dx_ref, h_ref, g_ref, da_ref, db_ref, dy_ref, dmod_ref, dnw_ref) = refs[1 + n_x:]
        i = pl.program_id(0)

        @pl.when(i == 0)
        def _():
            dmod_ref[...] = jnp.zeros_like(dmod_ref)
            dnw_ref[...] = jnp.zeros_like(dnw_ref)

        g = jnp.minimum((tiles.tile(i) * tm) // s, nb)
        shift = mod_ref[g, pl.ds(k0, 1), :]
        scale = mod_ref[g, pl.ds(k0 + 1, 1), :]
        gate = mod_ref[g, pl.ds(k0 + 2, 1), :]
        x = jnp.where(tiles.is_lat(i), x_ref[...], refs[2][...]) if tc else x_ref[...]
        dout_v = do_ref[...]
        r = lax.rsqrt(jnp.mean(x * x, axis=-1, keepdims=True) + EPS)
        xh = x * r
        n = xh * nw_ref[...]
        h_ref[...] = (n * (1.0 + scale) + shift).astype(BF16)
        dyb = ((0.5 * gate) * dout_v).astype(BF16)
        dy_ref[...] = dyb
        dmod_ref[g, pl.ds(k0 + 2, 1), :] += 0.5 * jnp.sum(dout_v * y_ref[...].astype(F32), axis=0, keepdims=True)
        dh = jnp.zeros((tm, d), F32)
        for c in range(nch):
            sl = slice(c * fc, (c + 1) * fc)
            dg = _dot_nt(dyb, w2_ref[sl, :])
            av = a_ref[:, sl].astype(F32)
            bv = b_ref[:, sl].astype(F32)
            sig = _sigmoid(av)
            sa = av * sig
            g_ref[:, sl] = (sa * bv).astype(BF16)
            dab = (dg * bv * (sig * (1.0 + av * (1.0 - sig)))).astype(BF16)
            dbb = (dg * sa).astype(BF16)
            da_ref[:, sl] = dab
            db_ref[:, sl] = dbb
            dh = dh + _dot(dab, w1_ref[sl, :]) + _dot(dbb, w3_ref[sl, :])
        dmod_ref[g, pl.ds(k0, 1), :] += jnp.sum(dh, axis=0, keepdims=True)
        dmod_ref[g, pl.ds(k0 + 1, 1), :] += jnp.sum(dh * n, axis=0, keepdims=True)
        dn = dh * (1.0 + scale)
        dnw_ref[...] += jnp.sum(dn * xh, axis=0, keepdims=True)
        dxh = dn * nw_ref[...]
        dx_ref[...] = dout_v + r * (dxh - xh * jnp.mean(dxh * xh, axis=-1, keepdims=True))

    td = pl.BlockSpec((tm, d), tiles.row)
    tf = pl.BlockSpec((tm, f), tiles.row)
    lat = pl.BlockSpec((tm, d), tiles.lat_row)
    ta = t + tc
    return _hosted_call(
        body, name, (tiles.n_all,),
        [td, lat] + ([pl.BlockSpec((tm, d), tiles.ctx_row)] if tc else [])
        + [tf, tf, td, _whole(mod.shape), _whole(nw.shape), _whole(w1.shape), _whole(w3.shape), _whole(w2.shape)],
        [lat, td, tf, tf, tf, td, pl.BlockSpec(mod.shape, lambda i: (0, 0, 0)), pl.BlockSpec((1, d), lambda i: (0, 0))],
        [_sds((t, d), F32), _sds((ta, d), BF16), _sds((ta, f), BF16), _sds((ta, f), BF16), _sds((ta, f), BF16),
         _sds((ta, d), BF16), _sds(mod.shape, F32), _sds((1, d), F32)],
        (dout, x_lat) + ((x_ctx,) if tc else ()) + (a, b, y, mod, nw, w1, w3, w2), exch=exch)


def _mm_tn(a, b, rows, name, exch=None):
    m = a.shape[1]
    n = b.shape[1]
    bm = _div_tile(m, 1408, LANES)
    bn = _div_tile(n, 1408, LANES)
    bk = _div_tile(rows, 2304, LANES)
    nk = rows // bk

    def body(a_ref, b_ref, o_ref, acc_ref):
        k = pl.program_id(2)

        @pl.when(k == 0)
        def _():
            acc_ref[...] = jnp.zeros_like(acc_ref)

        acc_ref[...] += _dot_tn(a_ref[...], b_ref[...])

        @pl.when(k == nk - 1)
        def _():
            o_ref[...] = acc_ref[...].astype(BF16)

    (out,), got = _hosted_call(
        body, name, (m // bm, n // bn, nk),
        [pl.BlockSpec((bk, bm), lambda i, j, k: (k, i)), pl.BlockSpec((bk, bn), lambda i, j, k: (k, j))],
        [pl.BlockSpec((bm, bn), lambda i, j, k: (i, j))], [_sds((m, n), BF16)], (a, b),
        scratch=[pltpu.VMEM((bm, bn), F32)], exch=exch)
    return out if exch is None else (out, got)


def _mixin_fwd(xs, mod, nw, wp, s, nb, tm):
    t, d = xs.shape

    def body(x_ref, mod_ref, nw_ref, wp_ref, h_ref, p_ref):
        g = jnp.minimum((pl.program_id(0) * tm) // s, nb)
        shift = mod_ref[g, pl.ds(3, 1), :]
        scale = mod_ref[g, pl.ds(4, 1), :]
        x = x_ref[...]
        r = lax.rsqrt(jnp.mean(x * x, axis=-1, keepdims=True) + EPS)
        hb = ((x * r * nw_ref[...]) * (1.0 + scale) + shift).astype(BF16)
        h_ref[...] = hb
        p_ref[...] = _dot_nt(hb, wp_ref[...]).astype(BF16)

    row = lambda i: (i, 0)
    return pl.pallas_call(
        body, name="mixin_fwd", grid=(t // tm,),
        in_specs=[pl.BlockSpec((tm, d), row), _whole(mod.shape), _whole(nw.shape), _whole(wp.shape)],
        out_specs=[pl.BlockSpec((tm, d), row), pl.BlockSpec((tm, PROJ_COLS), row)],
        out_shape=[_sds((t, d), BF16), _sds((t, PROJ_COLS), BF16)], compiler_params=_params(1),
    )(xs, mod, nw, wp)


def _mixin_bwd(dp0, duv, xs, dres, mod, nw, wp, s, nb, tm, swap):
    t_all, d = xs.shape
    nlat = dres.shape[0] // tm

    def body(p0_ref, uv_ref, x_ref, dr_ref, mod_ref, nw_ref, wp_ref, dx_ref, dmod_ref, dnw_ref):
        i = pl.program_id(0)

        @pl.when(i == 0)
        def _():
            dmod_ref[...] = jnp.zeros_like(dmod_ref)
            dnw_ref[...] = jnp.zeros_like(dnw_ref)

        lat = i < nlat
        g = jnp.minimum((i * tm) // s, nb)
        scale = mod_ref[g, pl.ds(4, 1), :]
        dh = _dot(p0_ref[...], wp_ref[0:512, :])
        extra = _dot(uv_ref[...], wp_ref[512:1536, :])
        dh = dh + jnp.where(lat, extra, 0.0)
        x = x_ref[...]
        r = lax.rsqrt(jnp.mean(x * x, axis=-1, keepdims=True) + EPS)
        xh = x * r
        n = xh * nw_ref[...]
        dmod_ref[g, pl.ds(3, 1), :] += jnp.sum(dh, axis=0, keepdims=True)
        dmod_ref[g, pl.ds(4, 1), :] += jnp.sum(dh * n, axis=0, keepdims=True)
        dn = dh * (1.0 + scale)
        dnw_ref[...] += jnp.sum(dn * xh, axis=0, keepdims=True)
        dxh = dn * nw_ref[...]
        dx_ref[...] = jnp.where(lat, dr_ref[...], 0.0) + r * (dxh - xh * jnp.mean(dxh * xh, axis=-1, keepdims=True))

    row = lambda i: (i, 0)
    lrow = lambda i: (jnp.minimum(i, nlat - 1), 0)
    return _hosted_call(
        body, "mixin_bwd", (t_all // tm,),
        [pl.BlockSpec((tm, 512), row), pl.BlockSpec((tm, 1024), lrow), pl.BlockSpec((tm, d), row),
         pl.BlockSpec((tm, d), lrow), _whole(mod.shape), _whole(nw.shape), _whole(wp.shape)],
        [pl.BlockSpec((tm, d), row), pl.BlockSpec(mod.shape, lambda i: (0, 0, 0)), pl.BlockSpec((1, d), lambda i: (0, 0))],
        [_sds((t_all, d), F32), _sds(mod.shape, F32), _sds((1, d), F32)], (dp0, duv, xs, dres, mod, nw, wp), swap=swap)


def _prep_fwd(proj, row0, nb, s, pos0, sk, key0, into, tabs, wq, wk, wv, kvaw, qaw, qnw, knw, tm, with_q, name):
    nblk = s // tm
    n_into = 0 if into is None else 2

    def body(p_ref, cos_ref, sa_ref, sb_ref, wq_ref, wk_ref, wv_ref, kvaw_ref, qaw_ref, qnw_ref, knw_ref, *rest):
        outs, heads_ref = rest[n_into:-1], rest[-1]
        q_ref, k_ref, v_ref = outs if with_q else (None,) + outs
        cos, sin_a, sin_b = cos_ref[...][None], sa_ref[...][None], sb_ref[...][None]

        def normed_roped(w_ref, src, extra, nw_ref, o_ref, post):
            for h in range(HEADS):
                heads_ref[h] = _dot_nt(src, w_ref[h]) if extra is None else _dot(src, w_ref[h])
            xp = heads_ref[...] if extra is None else heads_ref[...] + extra[None]
            r = lax.rsqrt(jnp.sum(xp * xp, axis=-1, keepdims=True) * (1.0 / QK_HEAD) + EPS)
            o_ref[...] = _rope3(xp * r * (nw_ref[...] * post)[None], cos, sin_a, sin_b).astype(BF16)

        ckv = p_ref[:, 0:128].astype(F32)
        rkv = lax.rsqrt(jnp.mean(ckv * ckv, axis=-1, keepdims=True) + EPS)
        ckvb = (ckv * rkv * kvaw_ref[...]).astype(BF16)
        normed_roped(wk_ref, ckvb, p_ref[:, 128:256].astype(F32), knw_ref, k_ref, 1.0)
        for j in range(HEADS // 2):
            v_ref[j] = _dot(ckvb, wv_ref[j]).astype(BF16)
        if with_q:
            cq = p_ref[:, 256:512].astype(F32)
            rq = lax.rsqrt(jnp.mean(cq * cq, axis=-1, keepdims=True) + EPS)
            normed_roped(wq_ref, (cq * rq * qaw_ref[...]).astype(BF16), None, qnw_ref, q_ref, SOFTMAX_SCALE)

    tab = pl.BlockSpec((tm, HEAD_PAD), lambda i: (pos0 + i % nblk, 0))
    qspec = pl.BlockSpec((None, HEADS, tm, HEAD_PAD), lambda i: (i // nblk, 0, i % nblk, 0))
    kspec = pl.BlockSpec((None, HEADS, tm, HEAD_PAD), lambda i: (i // nblk, 0, key0 + i % nblk, 0))
    vspec = pl.BlockSpec((None, HEADS // 2, tm, HEAD_PAD), lambda i: (i // nblk, 0, key0 + i % nblk, 0))
    qshape = _sds((nb, HEADS, s, HEAD_PAD), BF16)
    kshape = _sds((nb, HEADS, sk, HEAD_PAD), BF16)
    vshape = _sds((nb, HEADS // 2, sk, HEAD_PAD), BF16)
    n_q = 1 if with_q else 0
    return pl.pallas_call(
        body, name=name, grid=(nb * nblk,),
        in_specs=[pl.BlockSpec((tm, 512), lambda i: (row0 + i, 0)), tab, tab, tab, _whole(wq.shape), _whole(wk.shape),
                  _whole(wv.shape), _whole(kvaw.shape), _whole(qaw.shape), _whole(qnw.shape), _whole(knw.shape)]
        + [pl.BlockSpec(memory_space=pl.ANY)] * n_into,
        out_specs=([qspec] if with_q else []) + [kspec, vspec],
        out_shape=([qshape] if with_q else []) + [kshape, vshape],
        scratch_shapes=[pltpu.VMEM((HEADS, tm, HEAD_PAD), F32)],
        input_output_aliases={11: n_q, 12: n_q + 1} if n_into else {}, compiler_params=_params(1),
    )(proj, *tabs, wq, wk, wv, kvaw, qaw, qnw, knw, *(into or ()))


def _prep_bwd(proj, row0, nb, s, pos0, key0, dp_rows, dp_into, tabs, wq, wk, wv, kvaw, qaw, qnw, knw, dq, dk, dv, init, tm,
              name):
    nblk = s // tm
    with_q = dq is not None
    n_init = 0 if init is None else len(init)
    n_into = 0 if dp_into is None else 1

    def body(*refs):
        p_ref, cos_ref, sa_ref, sb_ref, wq_ref, wk_ref, wv_ref, kvaw_ref, qaw_ref, qnw_ref, knw_ref = refs[:11]
        rest = list(refs[11:])
        dq_ref = rest.pop(0) if with_q else None
        dk_ref, dv_ref = rest.pop(0), rest.pop(0)
        init_refs = [rest.pop(0) for _ in range(n_init)]
        if n_into:
            rest.pop(0)
        dp_ref = rest.pop(0)
        if with_q:
            dwq_ref, dqaw_ref, dqnw_ref = rest.pop(0), rest.pop(0), rest.pop(0)
        dwk_ref, dwv_ref, dkvaw_ref, dknw_ref, heads_ref, dhb_ref, dkr_ref = rest
        accs = [dwk_ref, dwv_ref, dkvaw_ref, dknw_ref]

        @pl.when(pl.program_id(0) == 0)
        def _():
            for k, acc in enumerate(accs):
                acc[...] = init_refs[k][...] if n_init else jnp.zeros_like(acc)
            if with_q:
                dwq_ref[...] = jnp.zeros_like(dwq_ref)
                dqaw_ref[...] = jnp.zeros_like(dqaw_ref)
                dqnw_ref[...] = jnp.zeros_like(dqnw_ref)

        cos, sin_a, sin_b = cos_ref[...][None], sa_ref[...][None], sb_ref[...][None]
        lane = lax.broadcasted_iota(jnp.int32, (tm, HEAD_PAD), 1)
        rope_lanes = (lane >= QK_NOPE) & (lane < QK_HEAD)

        def heads_bwd(w_ref, src, extra, nw_ref, d_ref, dnw_ref, dw_ref, post):
            w_t = extra is None
            for h in range(HEADS):
                heads_ref[h] = _dot_nt(src, w_ref[h]) if w_t else _dot(src, w_ref[h])
            xp = heads_ref[...] if extra is None else heads_ref[...] + extra[None]
            r = lax.rsqrt(jnp.sum(xp * xp, axis=-1, keepdims=True) * (1.0 / QK_HEAD) + EPS)
            xh = xp * r
            dn = _rope3_t(d_ref[...], cos, sin_a, sin_b)
            dnw_ref[...] += post * jnp.sum(jnp.sum(dn * xh, axis=0), axis=0, keepdims=True)
            dxh = dn * (nw_ref[...] * post)[None]
            dxp = r * (dxh - xh * (jnp.sum(dxh * xh, axis=-1, keepdims=True) * (1.0 / QK_HEAD)))
            dhb_ref[...] = dxp.astype(BF16)
            dsrc = jnp.zeros((tm, src.shape[1]), F32)
            for h in range(HEADS):
                dsrc = dsrc + (_dot(dhb_ref[h], w_ref[h]) if w_t else _dot_nt(dhb_ref[h], w_ref[h]))
                dw_ref[h] += _dot_tn(src, dhb_ref[h])
            return dsrc, jnp.sum(dxp, axis=0)

        ckv = p_ref[:, 0:128].astype(F32)
        rkv = lax.rsqrt(jnp.mean(ckv * ckv, axis=-1, keepdims=True) + EPS)
        ckvh = ckv * rkv
        ckvb = (ckvh * kvaw_ref[...]).astype(BF16)
        for h in range(HEADS):
            dkr_ref[h] = dk_ref[h].T
        dckv, dkp_sum = heads_bwd(wk_ref, ckvb, p_ref[:, 128:256].astype(F32), knw_ref, dkr_ref, dknw_ref, dwk_ref,
                                  1.0)
        for j in range(HEADS // 2):
            dvb = dv_ref[j].T.astype(BF16)
            dckv = dckv + _dot_nt(dvb, wv_ref[j])
            dwv_ref[j] += _dot_tn(ckvb, dvb)
        dkvaw_ref[...] += jnp.sum(dckv * ckvh, axis=0, keepdims=True)
        dch = dckv * kvaw_ref[...]
        dp_ref[:, 0:128] = (rkv * (dch - ckvh * jnp.mean(dch * ckvh, axis=-1, keepdims=True))).astype(BF16)
        dp_ref[:, 128:256] = jnp.where(rope_lanes, dkp_sum, 0.0).astype(BF16)
        if with_q:
            cq = p_ref[:, 256:512].astype(F32)
            rq = lax.rsqrt(jnp.mean(cq * cq, axis=-1, keepdims=True) + EPS)
            cqh = cq * rq
            cqb = (cqh * qaw_ref[...]).astype(BF16)
            dcq, _ = heads_bwd(wq_ref, cqb, None, qnw_ref, dq_ref, dqnw_ref, dwq_ref, SOFTMAX_SCALE)
            dqaw_ref[...] += jnp.sum(dcq * cqh, axis=0, keepdims=True)
            dqc = dcq * qaw_ref[...]
            dp_ref[:, 256:512] = (rq * (dqc - cqh * jnp.mean(dqc * cqh, axis=-1, keepdims=True))).astype(BF16)
        else:
            dp_ref[:, 256:512] = jnp.zeros((tm, Q_LORA), BF16)

    tab = pl.BlockSpec((tm, HEAD_PAD), lambda i: (pos0 + i % nblk, 0))
    qspec = pl.BlockSpec((None, HEADS, tm, HEAD_PAD), lambda i: (i // nblk, 0, i % nblk, 0))
    kspec = pl.BlockSpec((None, HEADS, HEAD_PAD, tm), lambda i: (i // nblk, 0, 0, key0 + i % nblk))
    vspec = pl.BlockSpec((None, HEADS // 2, HEAD_PAD, tm), lambda i: (i // nblk, 0, 0, key0 + i % nblk))

    def acc_spec(shape):
        nd = len(shape)
        return pl.BlockSpec(shape, lambda i: (0,) * nd)

    acc_shapes = [(HEADS, KV_LORA, HEAD_PAD), (HEADS // 2, KV_LORA, HEAD_PAD), (1, KV_LORA), (1, HEAD_PAD)]
    q_shapes = [(HEADS, Q_LORA, HEAD_PAD), (1, Q_LORA), (1, HEAD_PAD)] if with_q else []
    out_shapes = [(dp_rows, 512)] + q_shapes + acc_shapes
    n_before = 11 + (1 if with_q else 0) + 2 + n_init
    return pl.pallas_call(
        body, name=name, grid=(nb * nblk,),
        in_specs=[pl.BlockSpec((tm, 512), lambda i: (row0 + i, 0)), tab, tab, tab, _whole(wq.shape), _whole(wk.shape),
                  _whole(wv.shape), _whole(kvaw.shape), _whole(qaw.shape), _whole(qnw.shape), _whole(knw.shape)]
        + ([qspec] if with_q else []) + [kspec, vspec] + [_whole(a.shape) for a in (init or [])]
        + [pl.BlockSpec(memory_space=pl.ANY)] * n_into,
        out_specs=[pl.BlockSpec((tm, 512), lambda i: (row0 + i, 0))] + [acc_spec(sh) for sh in q_shapes + acc_shapes],
        out_shape=[_sds(out_shapes[0], BF16)] + [_sds(sh, F32) for sh in out_shapes[1:]],
        scratch_shapes=[pltpu.VMEM((HEADS, tm, HEAD_PAD), F32), pltpu.VMEM((HEADS, tm, HEAD_PAD), BF16),
                        pltpu.VMEM((HEADS, tm, HEAD_PAD), F32)],
        input_output_aliases={n_before: 0} if n_into else {}, compiler_params=_params(1),
    )(proj, *tabs, wq, wk, wv, kvaw, qaw, qnw, knw, *([dq] if with_q else []), dk, dv, *(init or []),
      *([dp_into] if n_into else []))


def _attn_fwd(q, k, v, tq, exch=None):
    nb, _, s, _ = q.shape
    sk = k.shape[2]
    nq = s // tq

    def body(q_ref, k_ref, v_ref, o_ref, lse_ref, vext_ref):
        @pl.when(pl.program_id(2) == 0)
        def _():
            vext_ref[:, 0:HEAD_PAD] = v_ref[...]
            vext_ref[:, HEAD_PAD:2 * HEAD_PAD] = jnp.ones((sk, HEAD_PAD), BF16)

        lane = lax.broadcasted_iota(jnp.int32, (tq, HEAD_PAD), 1)
        outs = []
        for hh in range(2):
            sc = _dot_nt(q_ref[hh], k_ref[hh])
            m = jnp.max(sc, axis=-1, keepdims=True)
            pv = _dot(jnp.exp2(sc - m).astype(BF16), vext_ref[...])
            l = pv[:, HEAD_PAD:HEAD_PAD + 1]
            outs.append(pv[:, 0:HEAD_PAD] / l)
            lse_ref[hh] = m + jnp.log2(l)
        o_ref[...] = jnp.where(lane < V_HEAD, outs[0], outs[1]).astype(BF16)

    (o, lse), got = _hosted_call(
        body, "attn_fwd", (nb, HEADS // 2, nq),
        [pl.BlockSpec((None, 2, tq, HEAD_PAD), lambda b, j, i: (b, j, i, 0)),
         pl.BlockSpec((None, 2, sk, HEAD_PAD), lambda b, j, i: (b, j, 0, 0)),
         pl.BlockSpec((None, None, sk, HEAD_PAD), lambda b, j, i: (b, j, 0, 0))],
        [pl.BlockSpec((tq, HEAD_PAD), lambda b, j, i: (b * nq + i, j)),
         pl.BlockSpec((None, 2, tq, 1), lambda b, j, i: (b, j, i, 0))],
        [_sds((nb * s, MLA_W + GMLP_W), BF16), _sds((nb, HEADS, s, 1), F32)], (q, k, v),
        scratch=[pltpu.VMEM((sk, 2 * HEAD_PAD), BF16)], exch=exch)
    return o, lse, got


def _attn_bwd(q, k, v, do, o, lse, tq, exch=None):
    nb, _, s, _ = q.shape
    sk = k.shape[2]
    nq = s // tq

    def body(q_ref, k_ref, v_ref, do_ref, o_ref, lse_ref, dq_ref, dkt_ref, dvt_ref):
        @pl.when(pl.program_id(2) == 0)
        def _():
            dkt_ref[...] = jnp.zeros_like(dkt_ref)
            dvt_ref[...] = jnp.zeros_like(dvt_ref)

        lane = lax.broadcasted_iota(jnp.int32, (tq, HEAD_PAD), 1)
        dov = do_ref[...]
        prod = dov.astype(F32) * o_ref[...].astype(F32)
        for hh in range(2):
            mine = (lane < V_HEAD) if hh == 0 else (lane >= V_HEAD)
            doh = jnp.where(mine, dov, jnp.zeros_like(dov))
            delta = jnp.sum(jnp.where(mine, prod, 0.0), axis=-1, keepdims=True)
            qh = q_ref[hh]
            q_ln2 = (qh.astype(F32) * LN2).astype(BF16)
            kv = k_ref[hh]
            p = jnp.exp2(_dot_nt(qh, kv) - lse_ref[hh])
            u = (p * (_dot_nt(doh, v_ref[...]) - delta)).astype(BF16)
            dq_ref[hh] = _dot(u, kv) * LN2
            dkt_ref[hh] += _dot_tn(q_ln2, u)
            dvt_ref[...] += _dot_tn(doh, p.astype(BF16))

    qspec = pl.BlockSpec((None, 2, tq, HEAD_PAD), lambda b, j, i: (b, j, i, 0))
    kspec = pl.BlockSpec((None, 2, sk, HEAD_PAD), lambda b, j, i: (b, j, 0, 0))
    vspec = pl.BlockSpec((None, None, sk, HEAD_PAD), lambda b, j, i: (b, j, 0, 0))
    ospec = pl.BlockSpec((tq, HEAD_PAD), lambda b, j, i: (b * nq + i, j))
    return _hosted_call(
        body, "attn_bwd", (nb, HEADS // 2, nq),
        [qspec, kspec, vspec, ospec, ospec, pl.BlockSpec((None, 2, tq, 1), lambda b, j, i: (b, j, i, 0))],
        [qspec, pl.BlockSpec((None, 2, HEAD_PAD, sk), lambda b, j, i: (b, j, 0, 0)),
         pl.BlockSpec((None, None, HEAD_PAD, sk), lambda b, j, i: (b, j, 0, 0))],
        [_sds(q.shape, F32), _sds((nb, HEADS, HEAD_PAD, sk), F32), _sds((nb, HEADS // 2, HEAD_PAD, sk), F32)],
        (q, k, v, do, o, lse), exch=exch)


def _group_masks(rows):
    lane = lax.broadcasted_iota(jnp.int32, (rows, GMLP_W), 1)
    return [(lane >= g * GROUP_DIM) & (lane < (g + 1) * GROUP_DIM) for g in range(GROUPS)]


def _gmlp_fwd(proj, mixcat, wcat, bias, vnw, ones, tm):
    t = mixcat.shape[0]

    def body(u_ref, v_ref, wcat_ref, bias_ref, vnw_ref, ones_ref, _, o_ref):
        masks = _group_masks(CHUNK)
        gv = _gelu(v_ref[...].astype(F32))
        rv = lax.rsqrt(_group_sum(gv * gv, ones_ref) * (1.0 / GROUP_DIM) + EPS)
        vnb = (gv * rv * vnw_ref[...]).astype(BF16)
        for c in range(tm // CHUNK):
            rows = slice(c * CHUNK, (c + 1) * CHUNK)
            vc = vnb[rows]
            stack = jnp.concatenate([jnp.where(m, vc, jnp.zeros_like(vc)) for m in masks], axis=0)
            sp = _dot(wcat_ref[...], stack) + bias_ref[...]
            o_ref[rows, :] = (_gelu(u_ref[rows, :].astype(F32)) * sp).astype(BF16)

    return pl.pallas_call(
        body, name="gmlp_fwd", grid=(t // tm,),
        in_specs=[pl.BlockSpec((tm, GMLP_W), lambda i: (i, 1)), pl.BlockSpec((tm, GMLP_W), lambda i: (i, 2)),
                  _whole(wcat.shape), _whole(bias.shape), _whole(vnw.shape), _whole(ones.shape),
                  pl.BlockSpec(memory_space=pl.ANY)],
        out_specs=pl.BlockSpec((tm, GMLP_W), lambda i: (i, 1)),
        out_shape=_sds(mixcat.shape, BF16), input_output_aliases={6: 0}, compiler_params=_params(1),
    )(proj, proj, wcat, bias, vnw, ones, mixcat)


def _gmlp_bwd(proj, dsg, wcat, wcat_t, bias, vnw, ones, tm):
    t = dsg.shape[0]

    def body(u_ref, v_ref, dsg_ref, wcat_ref, wcatt_ref, bias_ref, vnw_ref, ones_ref,
             duv_ref, dws_ref, dbs_ref, dvnw_ref):
        @pl.when(pl.program_id(0) == 0)
        def _():
            dws_ref[...] = jnp.zeros_like(dws_ref)
            dbs_ref[...] = jnp.zeros_like(dbs_ref)
            dvnw_ref[...] = jnp.zeros_like(dvnw_ref)

        masks = _group_masks(CHUNK)
        v = v_ref[...].astype(F32)
        gv = _gelu(v)
        rv = lax.rsqrt(_group_sum(gv * gv, ones_ref) * (1.0 / GROUP_DIM) + EPS)
        xh = gv * rv
        vnb = (xh * vnw_ref[...]).astype(BF16)
        dvn_parts = []
        for c in range(tm // CHUNK):
            rows = slice(c * CHUNK, (c + 1) * CHUNK)
            vc = vnb[rows]
            stack = jnp.concatenate([jnp.where(m, vc, jnp.zeros_like(vc)) for m in masks], axis=0)
            sp = _dot(wcat_ref[...], stack) + bias_ref[...]
            u = u_ref[rows, :].astype(F32)
            dsg_c = dsg_ref[rows, :]
            duv_ref[rows, 0:GMLP_W] = (dsg_c * sp * _gelu_grad(u)).astype(BF16)
            ds = dsg_c * _gelu(u)
            dstack = jnp.concatenate([jnp.where(m, ds, 0.0) for m in masks], axis=0)
            dbs_ref[...] += jnp.broadcast_to(jnp.sum(dstack, axis=-1, keepdims=True), dbs_ref.shape)
            dstb = dstack.astype(BF16)
            dvn_parts.append(_dot(wcatt_ref[...], dstb))
            dws_ref[...] += _dot_nt(dstb, vc)
        dvn = jnp.concatenate(dvn_parts, axis=0) if len(dvn_parts) > 1 else dvn_parts[0]
        dvnw_ref[...] += jnp.sum(dvn * xh, axis=0, keepdims=True)
        dxh = dvn * vnw_ref[...]
        gm = _group_sum(dxh * xh, ones_ref) * (1.0 / GROUP_DIM)
        duv_ref[:, GMLP_W:2 * GMLP_W] = (rv * (dxh - xh * gm) * _gelu_grad(v)).astype(BF16)

    row = pl.BlockSpec((tm, GMLP_W), lambda i: (i, 0))
    return pl.pallas_call(
        body, name="gmlp_bwd", grid=(t // tm,),
        in_specs=[pl.BlockSpec((tm, GMLP_W), lambda i: (i, 1)), pl.BlockSpec((tm, GMLP_W), lambda i: (i, 2)), row,
                  _whole(wcat.shape), _whole(wcat_t.shape), _whole(bias.shape), _whole(vnw.shape), _whole(ones.shape)],
        out_specs=[pl.BlockSpec((tm, 2 * GMLP_W), lambda i: (i, 0)), pl.BlockSpec((GROUPS * CHUNK, CHUNK), lambda i: (0, 0)),
                   pl.BlockSpec((GROUPS * CHUNK, CHUNK), lambda i: (0, 0)), pl.BlockSpec((1, GMLP_W), lambda i: (0, 0))],
        out_shape=[_sds((t, 2 * GMLP_W), BF16), _sds((GROUPS * CHUNK, CHUNK), F32), _sds((GROUPS * CHUNK, CHUNK), F32),
                   _sds((1, GMLP_W), F32)],
        compiler_params=_params(1),
    )(proj, proj, dsg, wcat, wcat_t, bias, vnw, ones)


def _mixout_fwd(mixcat, xs, mod, wout, s, tm):
    t, width = mixcat.shape
    d = xs.shape[1]

    def body(cat_ref, x_ref, mod_ref, w_ref, x2_ref, mix_ref):
        g = (pl.program_id(0) * tm) // s
        gate = mod_ref[g, pl.ds(5, 1), :]
        mix = _dot(cat_ref[...], w_ref[...])
        x2_ref[...] = x_ref[...] + gate * mix
        mix_ref[...] = mix.astype(BF16)

    row = lambda i: (i, 0)
    return pl.pallas_call(
        body, name="mixout_fwd", grid=(t // tm,),
        in_specs=[pl.BlockSpec((tm, width), row), pl.BlockSpec((tm, d), row), _whole(mod.shape), _whole(wout.shape)],
        out_specs=[pl.BlockSpec((tm, d), row), pl.BlockSpec((tm, d), row)],
        out_shape=[_sds((t, d), F32), _sds((t, d), BF16)], compiler_params=_params(1),
    )(mixcat, xs, mod, wout)


def _mixout_bwd(dx2, mix, mod, wout, s, tm):
    t, d = dx2.shape

    def body(dx_ref, mix_ref, mod_ref, w_ref, dmix_ref, do_ref, dsg_ref, dmod_ref):
        i = pl.program_id(0)

        @pl.when(i == 0)
        def _():
            dmod_ref[...] = jnp.zeros_like(dmod_ref)

        g = (i * tm) // s
        gate = mod_ref[g, pl.ds(5, 1), :]
        dx = dx_ref[...]
        dmod_ref[g, pl.ds(5, 1), :] += jnp.sum(dx * mix_ref[...].astype(F32), axis=0, keepdims=True)
        dmb = (gate * dx).astype(BF16)
        dmix_ref[...] = dmb
        do_ref[...] = _dot_nt(dmb, w_ref[0:MLA_W, :]).astype(BF16)
        dsg_ref[...] = _dot_nt(dmb, w_ref[MLA_W:MLA_W + GMLP_W, :])

    row = lambda i: (i, 0)
    return pl.pallas_call(
        body, name="mixout_bwd", grid=(t // tm,),
        in_specs=[pl.BlockSpec((tm, d), row), pl.BlockSpec((tm, d), row), _whole(mod.shape), _whole(wout.shape)],
        out_specs=[pl.BlockSpec((tm, d), row), pl.BlockSpec((tm, MLA_W), row), pl.BlockSpec((tm, GMLP_W), row),
                   pl.BlockSpec(mod.shape, lambda i: (0, 0, 0))],
        out_shape=[_sds((t, d), BF16), _sds((t, MLA_W), BF16), _sds((t, GMLP_W), F32), _sds(mod.shape, F32)],
        compiler_params=_params(1),
    )(dx2, mix, mod, wout)


def _swap_cores(parts, name):
    n = len(parts)

    def body(*refs):
        srcs, outs, send_sems, recv_sems = refs[:n], refs[n:2 * n], refs[2 * n], refs[2 * n + 1]
        x, y, c = lax.axis_index("x"), lax.axis_index("y"), lax.axis_index("c")
        copies = [pltpu.make_async_remote_copy(
            src_ref=srcs[w], dst_ref=outs[w], send_sem=send_sems.at[w], recv_sem=recv_sems.at[w],
            device_id=(x, y, 1 - c), device_id_type=pl.DeviceIdType.MESH) for w in range(n)]
        for cp in copies:
            cp.start()
        for cp in copies:
            cp.wait()

    any_spec = pl.BlockSpec(memory_space=pl.ANY)
    return pl.pallas_call(
        body, name=name, in_specs=[any_spec] * n, out_specs=[any_spec] * n,
        out_shape=[_sds(p.shape, p.dtype) for p in parts],
        scratch_shapes=[pltpu.SemaphoreType.DMA((n,)), pltpu.SemaphoreType.DMA((n,))],
    )(*parts)


def _row_tile(r, c, mult):
    return _div_tile(r, max(mult, (1 << 18) // c), mult)


def _sum_slots(recv, name):
    _, r, c = recv.shape
    tr = _row_tile(r, c, 16)

    def body(r_ref, o_ref):
        f = lambda k: r_ref[k].astype(F32)
        o_ref[...] = ((f(0) + f(1)) + f(2)) + f(3)

    return pl.pallas_call(
        body, name=name, grid=(r // tr,),
        in_specs=[pl.BlockSpec((N_CHIPS, tr, c), lambda i: (0, i, 0))],
        out_specs=pl.BlockSpec((tr, c), lambda i: (i, 0)),
        out_shape=_sds((r, c), F32), compiler_params=_params(1),
    )(recv)


def _adamw(parts, w, m, v, name, exch=None, swap=None):
    r, wd = w.shape
    tr = _row_tile(r, wd, 8)
    c1 = 1.0 / (1.0 - ADAM_B1 ** ADAM_STEP)
    c2 = 1.0 / (1.0 - ADAM_B2 ** ADAM_STEP)
    n_p = len(parts)

    def body(*refs):
        p_refs = refs[:n_p]
        w_ref, m_ref, v_ref, g_ref, d_ref, nm_ref, nv_ref = refs[n_p:]
        g = p_refs[0][...]
        for p_ref in p_refs[1:]:
            g = g + p_ref[...]
        nm = ADAM_B1 * m_ref[...] + (1.0 - ADAM_B1) * g
        nv = ADAM_B2 * v_ref[...] + (1.0 - ADAM_B2) * (g * g)
        g_ref[...] = g
        nm_ref[...] = nm
        nv_ref[...] = nv
        d_ref[...] = -ADAM_LR * ((nm * c1) / (jnp.sqrt(nv * c2) + ADAM_EPS) + ADAM_WD * w_ref[...])

    spec = pl.BlockSpec((tr, wd), lambda i: (i, 0))
    return _hosted_call(body, name, (r // tr,), [spec] * (n_p + 3), [spec] * 4, [_sds((r, wd), F32)] * 4,
                        (*parts, w, m, v), exch=exch, swap=swap)


def _all_peers(x, y, c):
    flips = [(dx, dy, dc) for dx in (0, 1) for dy in (0, 1) for dc in (0, 1)][1:]
    return [(1 - x if dx else x, 1 - y if dy else y, 1 - c if dc else c) for dx, dy, dc in flips]


def _first_exchange(shards, later, cc, w, b):
    n_w, n_l = len(shards), len(later)
    n = w.shape[1]

    def body(*refs):
        src32, later_in, (cc_ref, w_ref, b_ref) = refs[:n_w], refs[n_w:n_w + n_l], refs[n_w + n_l:n_w + n_l + 3]
        o0 = n_w + n_l + 3
        outs, (all_ref, tab_ref), later_out = refs[o0:o0 + n_w], refs[o0 + n_w:o0 + n_w + 2], refs[o0 + n_w + 2:o0 + n_w + 2 + n_l]
        s0 = o0 + n_w + 2 + n_l
        srcs = refs[s0:s0 + n_w]
        (part_ref, ici_send, ici_recv, d2d_send, d2d_recv, local_sems, cc_send, cc_recv, tab_send,
         tab_recv) = refs[s0 + n_w:]
        for wi in range(n_w):
            srcs[wi][...] = src32[wi][...].astype(BF16)
        x, y, c = lax.axis_index("x"), lax.axis_index("y"), lax.axis_index("c")
        chip, dev = 2 * x + y, 4 * x + 2 * y + c
        chips = _other_chips(x, y)
        peers = _all_peers(x, y, c)

        def half(wi, which):
            hr = shards[wi].shape[0] // 2
            return pl.ds(pl.multiple_of(which * hr, 16), hr)

        def over_ici(wi, k, arriving):
            px, py = chips[k]
            slot = 2 * px + py if arriving else chip
            return pltpu.make_async_remote_copy(
                src_ref=srcs[wi].at[half(wi, c)], dst_ref=outs[wi].at[slot, half(wi, c)],
                send_sem=ici_send.at[3 * wi + k], recv_sem=ici_recv.at[3 * wi + k], device_id=(px, py, c),
                device_id_type=pl.DeviceIdType.MESH)

        def to_sibling(wi, k, arriving):
            px, py = chips[k]
            rows = half(wi, 1 - c if arriving else c)
            return pltpu.make_async_remote_copy(
                src_ref=outs[wi].at[2 * px + py, rows], dst_ref=outs[wi].at[2 * px + py, rows],
                send_sem=d2d_send.at[3 * wi + k], recv_sem=d2d_recv.at[3 * wi + k], device_id=(x, y, 1 - c),
                device_id_type=pl.DeviceIdType.MESH)

        def cc_copy(k, peer, slot):
            return pltpu.make_async_remote_copy(
                src_ref=cc_ref, dst_ref=all_ref.at[slot], send_sem=cc_send.at[k], recv_sem=cc_recv.at[k],
                device_id=peer, device_id_type=pl.DeviceIdType.MESH)

        def rows_of(px, py):
            return part_ref.at[pl.ds(pl.multiple_of((4 * px + 2 * py + c) * MOD_ROWS, MOD_ROWS), MOD_ROWS)]

        def tab_copy(k, px, py, slot):
            return pltpu.make_async_remote_copy(
                src_ref=rows_of(px, py), dst_ref=tab_ref.at[slot], send_sem=tab_send.at[k], recv_sem=tab_recv.at[k],
                device_id=(px, py, c), device_id_type=pl.DeviceIdType.MESH)

        local = [pltpu.make_async_copy(srcs[wi], outs[wi].at[chip], local_sems.at[wi]) for wi in range(n_w)]
        for cp in local:
            cp.start()
        pairs = [(wi, k) for wi in range(n_w) for k in range(3)]
        for wi, k in pairs:
            over_ici(wi, k, False).start()
        for k, peer in enumerate(peers):
            cc_copy(k, peer, dev).start()
        all_ref[dev] = cc_ref[...]
        for k, (px, py, pc) in enumerate(peers):
            cc_copy(k, (px, py, pc), 4 * px + 2 * py + pc).wait_recv()
        cv = all_ref[...].reshape(8 * MOD_ROWS, cc.shape[1])
        part_ref[...] = _dot((cv * _sigmoid(cv)).astype(BF16), w_ref[...]) + b_ref[...]
        for k, (px, py) in enumerate(chips):
            tab_copy(k, px, py, chip).start()
        tab_ref[chip] = rows_of(x, y)[...]
        for k, (px, py) in enumerate(chips):
            tab_copy(k, px, py, 2 * px + py).wait_recv()
        for j in range(n_l):
            later_out[j][...] = later_in[j][...].astype(BF16)
        for wi, k in pairs:
            over_ici(wi, k, True).wait_recv()
            to_sibling(wi, k, False).start()
        for wi, k in pairs:
            to_sibling(wi, k, True).wait_recv()
        for wi, k in pairs:
            over_ici(wi, k, False).wait_send()
            to_sibling(wi, k, False).wait_send()
        for k, peer in enumerate(peers):
            cc_copy(k, peer, dev).wait_send()
        for k, (px, py) in enumerate(chips):
            tab_copy(k, px, py, chip).wait_send()
        for cp in local:
            cp.wait()

    any_spec = pl.BlockSpec(memory_space=pl.ANY)
    vmem = pl.BlockSpec(memory_space=pltpu.VMEM)
    sems3 = pltpu.SemaphoreType.DMA((3 * n_w,))
    got = pl.pallas_call(
        body, name="first_exchange", in_specs=[vmem] * (n_w + n_l + 3),
        out_specs=[any_spec] * n_w + [vmem] * (2 + n_l),
        out_shape=[_sds((N_CHIPS,) + a.shape, BF16) for a in shards]
        + [_sds((8,) + cc.shape, F32), _sds((N_CHIPS, MOD_ROWS, n), F32)] + [_sds(a.shape, BF16) for a in later],
        scratch_shapes=[pltpu.VMEM(a.shape, BF16) for a in shards]
        + [pltpu.VMEM((8 * MOD_ROWS, n), F32), sems3, sems3, sems3, sems3, pltpu.SemaphoreType.DMA((n_w,)),
           pltpu.SemaphoreType.DMA((7,)), pltpu.SemaphoreType.DMA((7,)), pltpu.SemaphoreType.DMA((3,)),
           pltpu.SemaphoreType.DMA((3,))],
        compiler_params=pltpu.CompilerParams(vmem_limit_bytes=V7X_VMEM_LIMIT),
    )(*shards, *later, cc, w, b)
    return got[:n_w], got[n_w], got[n_w + 1], got[n_w + 2:]


def _ada_bwd_tp(cc_all, dmods, w, ctx_row):
    d, n = w.shape

    def body(cc_ref, m0, m1, m2, m3, w_ref, dw_ref, db_ref, dctx_ref, stage_ref, all_ref, send_sems, recv_sems):
        x, y, c = lax.axis_index("x"), lax.axis_index("y"), lax.axis_index("c")
        me = 4 * x + 2 * y + c
        dsum = m0[...] + m1[...] + m2[...] + m3[...]
        db_ref[...] = jnp.sum(dsum, axis=0, keepdims=True)
        for j in range(N_CHIPS):
            stage_ref[j] = dsum[:, j * n:(j + 1) * n]

        def copy(k, peer, slot):
            px, py, _ = peer
            return pltpu.make_async_remote_copy(
                src_ref=stage_ref.at[2 * px + py], dst_ref=all_ref.at[slot], send_sem=send_sems.at[k],
                recv_sem=recv_sems.at[k], device_id=peer, device_id_type=pl.DeviceIdType.MESH)

        peers = _all_peers(x, y, c)
        for k, peer in enumerate(peers):
            copy(k, peer, me).start()
        all_ref[me] = stage_ref[2 * x + y]
        for k, (px, py, pc) in enumerate(peers):
            copy(k, (px, py, pc), 4 * px + 2 * py + pc).wait_recv()
        for k, peer in enumerate(peers):
            copy(k, peer, me).wait_send()
        cv = cc_ref[...]
        sig = _sigmoid(cv)
        dmb = all_ref[...].reshape(8 * MOD_ROWS, n).astype(BF16)
        dw_ref[...] = _dot_tn((cv * sig).astype(BF16), dmb)
        dsc = _dot_nt(dmb, w_ref[...])
        dctx = dsc[ctx_row:ctx_row + 1, :]
        for dev in range(1, 8):
            dctx = dctx + dsc[dev * MOD_ROWS + ctx_row:dev * MOD_ROWS + ctx_row + 1, :]
        cx = cv[ctx_row:ctx_row + 1, :]
        sx = sig[ctx_row:ctx_row + 1, :]
        dctx_ref[...] = dctx * (sx * (1.0 + cx * (1.0 - sx))) * jnp.where(c == 0, 1.0, 0.0)

    vmem = pl.BlockSpec(memory_space=pltpu.VMEM)
    return pl.pallas_call(
        body, name="ada_bwd_tp", in_specs=[vmem] * 6, out_specs=[vmem] * 3,
        out_shape=[_sds((d, n), F32), _sds((1, N_MOD * d), F32), _sds((1, d), F32)],
        scratch_shapes=[pltpu.VMEM((N_CHIPS, MOD_ROWS, n), F32), pltpu.VMEM((8, MOD_ROWS, n), F32),
                        pltpu.SemaphoreType.DMA((7,)), pltpu.SemaphoreType.DMA((7,))],
        compiler_params=pltpu.CompilerParams(vmem_limit_bytes=V7X_VMEM_LIMIT),
    )(cc_all, *dmods, w)


def _rope_tables(s, ctx):
    pos = np.arange(s, dtype=np.float32)
    inv = (np.float32(ROPE_BASE) ** (-np.arange(0, QK_ROPE // 2, 2, dtype=np.float32) / np.float32(QK_ROPE // 2)))
    ang_r = np.floor(pos / GRID_W)[:, None] * inv
    ang_c = (pos - GRID_W * np.floor(pos / GRID_W))[:, None] * inv
    ang = np.concatenate([ang_r, ang_r, ang_c, ang_c], axis=-1).astype(np.float32)
    cos, sin = np.cos(ang), np.sin(ang)
    half_b = (np.arange(QK_ROPE) // 8) % 2 == 1
    sin_a = np.where(half_b, sin, 0.0)
    sin_b = np.where(half_b, 0.0, -sin)

    def place(tab, fill):
        full = np.full((s + ctx, HEAD_PAD), fill, np.float32)
        full[:s, QK_NOPE:QK_HEAD] = tab
        return jnp.asarray(full)

    return place(cos, 1.0), place(sin_a, 0.0), place(sin_b, 0.0)


def _pad_last(a, n):
    return jnp.pad(a, [(0, 0)] * (a.ndim - 1) + [(0, n - a.shape[-1])])


def _flat_rows(parts, rows, width):
    flat = jnp.concatenate([p.reshape(-1) for p in parts])
    return jnp.pad(flat, (0, rows * width - flat.shape[0])).reshape(rows, width)


def kernel(x, c, ctx, c_ctx, w_ada, b_ada, norm1_w, ffn1_w1, ffn1_w3, ffn1_w2, norm2_w, w_in, q_a_norm_w, w_uq, kv_a_norm_w, w_ukv, q_norm_w, k_norm_w, v_norm_w, w_s, b_s, w_out, norm3_w, ffn2_w1, ffn2_w3, ffn2_w2, loss_target, m_c_ctx, m_w_ada, m_b_ada, m_norm1_w, m_ffn1_w1, m_ffn1_w3, m_ffn1_w2, m_norm2_w, m_w_in, m_q_a_norm_w, m_w_uq, m_kv_a_norm_w, m_w_ukv, m_q_norm_w, m_k_norm_w, m_v_norm_w, m_w_s, m_b_s, m_w_out, m_norm3_w, m_ffn2_w1, m_ffn2_w3, m_ffn2_w2, v_c_ctx, v_w_ada, v_b_ada, v_norm1_w, v_ffn1_w1, v_ffn1_w3, v_ffn1_w2, v_norm2_w, v_w_in, v_q_a_norm_w, v_w_uq, v_kv_a_norm_w, v_w_ukv, v_q_norm_w, v_k_norm_w, v_v_norm_w, v_w_s, v_b_s, v_w_out, v_norm3_w, v_ffn2_w1, v_ffn2_w3, v_ffn2_w2):
    wts = dict(c_ctx=c_ctx, w_ada=w_ada, b_ada=b_ada, norm1_w=norm1_w, ffn1_w1=ffn1_w1, ffn1_w3=ffn1_w3, ffn1_w2=ffn1_w2,
               norm2_w=norm2_w, w_in=w_in, q_a_norm_w=q_a_norm_w, w_uq=w_uq, kv_a_norm_w=kv_a_norm_w, w_ukv=w_ukv,
               q_norm_w=q_norm_w, k_norm_w=k_norm_w, v_norm_w=v_norm_w, w_s=w_s, b_s=b_s, w_out=w_out, norm3_w=norm3_w,
               ffn2_w1=ffn2_w1, ffn2_w3=ffn2_w3, ffn2_w2=ffn2_w2)
    moms = dict(c_ctx=m_c_ctx, w_ada=m_w_ada, b_ada=m_b_ada, norm1_w=m_norm1_w, ffn1_w1=m_ffn1_w1, ffn1_w3=m_ffn1_w3,
                ffn1_w2=m_ffn1_w2, norm2_w=m_norm2_w, w_in=m_w_in, q_a_norm_w=m_q_a_norm_w, w_uq=m_w_uq,
                kv_a_norm_w=m_kv_a_norm_w, w_ukv=m_w_ukv, q_norm_w=m_q_norm_w, k_norm_w=m_k_norm_w, v_norm_w=m_v_norm_w,
                w_s=m_w_s, b_s=m_b_s, w_out=m_w_out, norm3_w=m_norm3_w, ffn2_w1=m_ffn2_w1, ffn2_w3=m_ffn2_w3,
                ffn2_w2=m_ffn2_w2)
    vars_ = dict(c_ctx=v_c_ctx, w_ada=v_w_ada, b_ada=v_b_ada, norm1_w=v_norm1_w, ffn1_w1=v_ffn1_w1, ffn1_w3=v_ffn1_w3,
                 ffn1_w2=v_ffn1_w2, norm2_w=v_norm2_w, w_in=v_w_in, q_a_norm_w=v_q_a_norm_w, w_uq=v_w_uq,
                 kv_a_norm_w=v_kv_a_norm_w, w_ukv=v_w_ukv, q_norm_w=v_q_norm_w, k_norm_w=v_k_norm_w, v_norm_w=v_v_norm_w,
                 w_s=v_w_s, b_s=v_b_s, w_out=v_w_out, norm3_w=v_norm3_w, ffn2_w1=v_ffn2_w1, ffn2_w3=v_ffn2_w3,
                 ffn2_w2=v_ffn2_w2)

    nb, s, d = x.shape
    nctx = ctx.shape[1]
    t, tc = nb * s, nb * nctx
    t_all = t + tc
    sk = s + nctx
    assert nb + 1 <= MOD_ROWS and d % LANES == 0
    tm = _token_tile(s, nctx)
    tq = _div_tile(s, 512, tm)
    tmx = _div_tile(math.gcd(s, tc), 512, tm)

    def held(n, a_):
        return jnp.swapaxes(a_[0], 0, 1) if n in T_WEIGHTS else a_[0]

    def unheld(n, a_):
        return (jnp.swapaxes(a_, 0, 1) if n in T_WEIGHTS else a_)[None]

    shard = {"w_ada": w_ada[0].astype(BF16)}
    full = {}

    def unshard(names, blocks):
        for n, g4 in zip(names, blocks):
            _, r_, c_ = g4.shape
            if n in ROW_SHARDED or n in T_WEIGHTS:
                full[n] = g4.reshape(N_CHIPS * r_, c_)
            else:
                full[n] = g4.transpose(1, 0, 2).reshape(r_, N_CHIPS * c_)

    def chip_major(n, g_):
        if n in ROW_SHARDED or n in T_WEIGHTS:
            return g_.reshape(N_CHIPS, g_.shape[0] // N_CHIPS, g_.shape[1]).astype(BF16)
        r_, cols = g_.shape
        return g_.reshape(r_, N_CHIPS, cols // N_CHIPS).transpose(1, 0, 2).astype(BF16)

    cc = jnp.concatenate([c, c_ctx[None, :], jnp.zeros((MOD_ROWS - nb - 1, d), F32)], axis=0)
    n_ada = shard["w_ada"].shape[1]
    assert n_ada % LANES == 0
    my_chip = 2 * lax.axis_index("x") + lax.axis_index("y")
    b_cols = lax.dynamic_slice_in_dim(b_ada, my_chip * n_ada, n_ada, axis=1)
    later = MIX_WEIGHTS + LAST_WEIGHTS
    got, cc_all, table, cast = _first_exchange([held(n, wts[n]) for n in FIRST_WEIGHTS],
                                               [held(n, wts[n]) for n in later], cc, shard["w_ada"], b_cols)
    unshard(FIRST_WEIGHTS, got)
    shard.update(zip(later, cast))
    cc_all = cc_all.reshape(8 * MOD_ROWS, d)
    mod = table.transpose(1, 0, 2).reshape(MOD_ROWS, N_MOD, d)
    wsb = w_s[0].astype(BF16)
    wcat = wsb.transpose(1, 0, 2).reshape(CHUNK, GROUPS * CHUNK)
    wcat_t = wsb.transpose(2, 0, 1).reshape(CHUNK, GROUPS * CHUNK)
    bias = jnp.repeat(b_s[0].T, GROUP_DIM, axis=1)
    vnw = v_norm_w.reshape(1, GMLP_W)
    lane = jnp.arange(GMLP_W)
    ones = (lane[:, None] // GROUP_DIM == lane[None, :] // GROUP_DIM).astype(BF16)
    qnw = _pad_last(q_norm_w, HEAD_PAD)
    knw = _pad_last(k_norm_w, HEAD_PAD)
    tabs = _rope_tables(s, nctx)

    x_lat, x_ctx = x.reshape(t, d), ctx.reshape(tc, d)
    (xs1, a1, b1, y1), got = _ffn_fwd(x_lat, x_ctx, mod, norm1_w, full["ffn1_w1"], full["ffn1_w3"], full["ffn1_w2"], 0, s,
                                      nb, tm, "ffn1_fwd", exch=("gather", [shard[n] for n in MIX_WEIGHTS]))
    unshard(MIX_WEIGHTS, got)
    wi = full["w_in"]
    wp = jnp.concatenate([wi[0:KV_LORA], jnp.zeros((QK_NOPE, d), BF16), wi[KV_LORA:KV_LORA + QK_ROPE],
                          jnp.zeros((HEAD_PAD - QK_HEAD, d), BF16), wi[KV_LORA + QK_ROPE:]], axis=0)
    wq = jnp.pad(full["w_uq"].reshape(HEADS, QK_HEAD, Q_LORA), ((0, 0), (0, HEAD_PAD - QK_HEAD), (0, 0)))
    wkv = full["w_ukv"].reshape(KV_LORA, HEADS, QK_NOPE + V_HEAD)
    wk = _pad_last(wkv[:, :, :QK_NOPE].transpose(1, 0, 2), HEAD_PAD)
    wv = wkv[:, :, QK_NOPE:].reshape(KV_LORA, HEADS // 2, 2 * V_HEAD).transpose(1, 0, 2)
    h2, proj = _mixin_fwd(xs1, mod, norm2_w, wp, s, nb, tmx)
    prep_w = (wq, wk, wv, kv_a_norm_w, q_a_norm_w, qnw, knw)
    q, k_all, v_all = _prep_fwd(proj, 0, nb, s, 0, sk, 0, None, tabs, *prep_w, tq, True, "prep_fwd")
    k_all, v_all = _prep_fwd(proj, t // tm, nb, nctx, s // tm, sk, s // tm, (k_all, v_all), tabs, *prep_w, tm, False,
                             "prep_ctx_fwd")
    o, lse, got = _attn_fwd(q, k_all, v_all, tq, exch=("gather", [shard[n] for n in LAST_WEIGHTS]))
    unshard(LAST_WEIGHTS, got)
    mixcat = _gmlp_fwd(proj, o, wcat, bias, vnw, ones, tq)
    x2, mix = _mixout_fwd(mixcat, xs1, mod, full["w_out"], s, tq)
    (dy, a2, b2, y2, loss_part), _ = _ffn_fwd(x2, None, mod, norm3_w, full["ffn2_w1"], full["ffn2_w3"], full["ffn2_w2"], 6,
                                              s, nb, tm, "ffn2_fwd", target=loss_target.reshape(t, d))

    grads, cm, recv = {}, {}, {}

    def scatter_of(names):
        return ("scatter", [cm[n] for n in names])

    (dx2, h3, g2, da2, db2, dyb2, dmod_c, grads["norm3_w"]), _ = _ffn_bwd(
        dy, x2, None, a2, b2, y2, mod, norm3_w, full["ffn2_w1"], full["ffn2_w3"], full["ffn2_w2"], 6, s, nb, tm,
        "ffn2_bwd")
    cm["ffn2_w1"] = chip_major("ffn2_w1", _mm_tn(da2, h3, t, "ffn2_dw1"))
    cm["ffn2_w3"] = chip_major("ffn2_w3", _mm_tn(db2, h3, t, "ffn2_dw3"))
    cm["ffn2_w2"] = chip_major("ffn2_w2", _mm_tn(g2, dyb2, t, "ffn2_dw2"))
    dmix, do, dsg, dmod_b = _mixout_bwd(dx2, mix, mod, full["w_out"], s, tq)
    cm["w_out"] = chip_major("w_out", _mm_tn(mixcat, dmix, t, "wout_dw"))
    duv, dws, dbs, dvnw = _gmlp_bwd(proj, dsg, wcat, wcat_t, bias, vnw, ones, tq)
    group = LAST_WEIGHTS + ("w_out",)
    (dq, dk, dv), got = _attn_bwd(q, k_all, v_all, do, mixcat, lse, tq, exch=scatter_of(group))
    recv.update(zip(group, got))
    dp0, dwk_c, dwv_c, dkvaw_c, dknw_c = _prep_bwd(
        proj, t // tm, nb, nctx, s // tm, s // tm, t_all, None, tabs, *prep_w, None, dk, dv, None, tm, "prep_ctx_bwd")
    dp0, dwq, dqaw, dqnw, dwk, dwv, dkvaw, dknw = _prep_bwd(
        proj, 0, nb, s, 0, 0, t_all, dp0, tabs, *prep_w, dq, dk, dv, [dwk_c, dwv_c, dkvaw_c, dknw_c], tq, "prep_bwd")
    part, sib = {}, {}
    early = LAST_WEIGHTS + ("w_out",)
    for n in early:
        part[n] = _sum_slots(recv[n], "sum_" + n)
    (dxs1, dmod_a, grads["norm2_w"]), _, got = _mixin_bwd(dp0, duv, xs1, dx2, mod, norm2_w, wp, s, nb, tmx,
                                                          [part[n] for n in early])
    sib.update(zip(early, got))
    dwp = jnp.concatenate([_mm_tn(dp0, h2, t_all, "win_dw_kvq"), _mm_tn(duv, h2, t, "win_dw_uv")], axis=0)
    cm["w_in"] = chip_major("w_in", jnp.concatenate(
        [dwp[0:KV_LORA], dwp[KV_LORA + QK_NOPE:KV_LORA + QK_HEAD], dwp[256:]], axis=0))
    cm["w_uq"] = chip_major("w_uq", dwq[:, :, :QK_HEAD].transpose(0, 2, 1).reshape(HEADS * QK_HEAD, Q_LORA))
    cm["w_ukv"] = chip_major("w_ukv", jnp.concatenate(
        [dwk[:, :, :QK_NOPE].transpose(1, 0, 2),
         dwv.transpose(1, 0, 2).reshape(KV_LORA, HEADS, V_HEAD)], axis=2).reshape(KV_LORA, HEADS * (QK_NOPE + V_HEAD)))
    (dx_lat, h1, g1, da1, db1, dyb1, dmod_0, grads["norm1_w"]), _ = _ffn_bwd(
        dxs1, x_lat, x_ctx, a1, b1, y1, mod, norm1_w, full["ffn1_w1"], full["ffn1_w3"], full["ffn1_w2"], 0, s, nb, tm,
        "ffn1_bwd")
    dmods = [m_.reshape(MOD_ROWS, N_MOD * d) for m_ in (dmod_0, dmod_a, dmod_b, dmod_c)]
    dw_ada, grads["b_ada"], dctx = _ada_bwd_tp(cc_all, dmods, shard["w_ada"], nb)
    grads["c_ctx"] = dctx[0]
    grads["q_a_norm_w"], grads["kv_a_norm_w"] = dqaw, dkvaw
    grads["q_norm_w"], grads["k_norm_w"] = dqnw[:, :QK_HEAD], dknw[:, :QK_HEAD]
    grads["v_norm_w"], grads["w_s"], grads["b_s"] = dvnw, dws, dbs[:, 0]
    grad_x = dx_lat.reshape(nb, s, d)
    n_small = sum(wts[n].size for n in SMALL)
    rows_s = _round_up(-(-(n_small + 1) // d), 16)
    cm["small"] = jnp.broadcast_to(_flat_rows([grads[n] for n in SMALL] + [loss_part], rows_s, d), (N_CHIPS, rows_s, d))
    group = ("w_in", "w_uq", "w_ukv", "small")
    dw2, got = _mm_tn(g1, dyb1, t_all, "ffn1_dw2", exch=scatter_of(group))
    recv.update(zip(group, got))
    cm["ffn1_w2"] = chip_major("ffn1_w2", dw2)
    dw1, got = _mm_tn(da1, h1, t_all, "ffn1_dw1", exch=scatter_of(("ffn1_w2",)))
    recv["ffn1_w2"] = got[0]
    cm["ffn1_w1"] = chip_major("ffn1_w1", dw1)
    dw3, got = _mm_tn(db1, h1, t_all, "ffn1_dw3", exch=scatter_of(("ffn1_w1",)))
    recv["ffn1_w1"] = got[0]
    cm["ffn1_w3"] = chip_major("ffn1_w3", dw3)
    stepped = {}
    reduced = tuple(n for n in SHARDED if n != "w_ada") + ("small",)
    late = tuple(n for n in reduced if n not in early and n != "ffn1_w3")
    for n in late:
        part[n] = _sum_slots(recv[n], "sum_" + n)
    stepped["w_ada"], got, got_sib = _adamw([dw_ada], wts["w_ada"][0], moms["w_ada"][0], vars_["w_ada"][0],
                                            "adamw_w_ada", exch=scatter_of(("ffn1_w3",)), swap=[part[n] for n in late])
    sib.update(zip(late, got_sib))
    part["ffn1_w3"] = _sum_slots(got[0], "sum_ffn1_w3")
    sib["ffn1_w3"] = _swap_cores([part["ffn1_w3"]], "swap_last")[0]
    for n in reduced[:-1]:
        stepped[n], _ = _adamw([part[n], sib[n]], held(n, wts[n]), held(n, moms[n]), held(n, vars_[n]), "adamw_" + n)
    for n in SHARDED:
        stepped[n] = [unheld(n, a_) for a_ in stepped[n]]
    packed, _ = _adamw([part["small"], sib["small"]], _flat_rows([wts[n] for n in SMALL], rows_s, d),
                       _flat_rows([moms[n] for n in SMALL], rows_s, d), _flat_rows([vars_[n] for n in SMALL], rows_s, d),
                       "adamw_small")
    loss = packed[0].reshape(-1)[n_small]
    for n in SMALL:
        stepped[n] = []
    for a_ in packed:
        flat = a_.reshape(-1)
        off = 0
        for n in SMALL:
            stepped[n].append(flat[off:off + wts[n].size].reshape(wts[n].shape))
            off += wts[n].size
    return (loss, grad_x, *[stepped[n][0] for n in WEIGHTS], *[stepped[n][1] for n in WEIGHTS],
            *[stepped[n][2] for n in WEIGHTS], *[stepped[n][3] for n in WEIGHTS])
```

```python
import functools
import math

import jax
import jax.numpy as jnp
import numpy as np
from jax import lax
from jax.experimental import pallas as pl
from jax.experimental.pallas import tpu as pltpu

F32 = jnp.float32
BF16 = jnp.bfloat16

EPS = 1e-6
N_MOD = 9
HEADS = 8
QK_NOPE, QK_ROPE, V_HEAD = 64, 32, 64
QK_HEAD = QK_NOPE + QK_ROPE
HEAD_PAD = 128
LN2 = math.log(2.0)
SOFTMAX_SCALE = QK_HEAD ** -0.5 / LN2
Q_LORA, KV_LORA = 256, 128
GROUPS, GROUP_DIM, CHUNK = 8, 64, 128
GMLP_W = GROUPS * GROUP_DIM
MLA_W = HEADS * V_HEAD
IN_COLS = 1440
PROJ_COLS = 1536
GRID_W = 64
ROPE_BASE = 10000.0
MOD_ROWS = 16
ADAM_LR, ADAM_B1, ADAM_B2, ADAM_EPS, ADAM_WD, ADAM_STEP = 0.001, 0.9, 0.999, 1e-08, 0.01, 10
N_CHIPS = 4
LANES = 128
V7X_VMEM_LIMIT = 56 * 1024 * 1024
GELU_C = math.sqrt(2.0 / math.pi)

SHARDED = ("w_ada", "ffn1_w1", "ffn1_w3", "ffn1_w2", "w_in", "w_uq", "w_ukv", "w_out", "ffn2_w1", "ffn2_w3", "ffn2_w2")
ROW_SHARDED = ("ffn1_w2", "w_out", "ffn2_w2")
T_WEIGHTS = ("ffn1_w1", "ffn1_w3", "ffn2_w1", "ffn2_w3", "w_in", "w_uq")
FIRST_WEIGHTS = ("ffn1_w1", "ffn1_w3")
MIX_WEIGHTS = ("w_in", "w_uq", "w_ukv")
LAST_WEIGHTS = ("ffn2_w1", "ffn2_w3", "ffn2_w2")
SMALL = ("c_ctx", "b_ada", "norm1_w", "norm2_w", "q_a_norm_w", "kv_a_norm_w", "q_norm_w", "k_norm_w", "v_norm_w",
         "w_s", "b_s", "norm3_w")
WEIGHTS = ("c_ctx", "w_ada", "b_ada", "norm1_w", "ffn1_w1", "ffn1_w3", "ffn1_w2", "norm2_w", "w_in", "q_a_norm_w",
           "w_uq", "kv_a_norm_w", "w_ukv", "q_norm_w", "k_norm_w", "v_norm_w", "w_s", "b_s", "w_out", "norm3_w",
           "ffn2_w1", "ffn2_w3", "ffn2_w2")


def _round_up(n, m):
    return (n + m - 1) // m * m


def _div_tile(n, target, mult):
    best = None
    for t in range(mult, min(n, target) + 1, mult):
        if n % t == 0:
            best = t
    return n if best is None else best


def _dot(a, b):
    return lax.dot_general(a, b, (((1,), (0,)), ((), ())), preferred_element_type=F32)


def _dot_nt(a, b):
    return lax.dot_general(a, b, (((1,), (1,)), ((), ())), preferred_element_type=F32)


def _dot_tn(a, b):
    return lax.dot_general(a, b, (((0,), (0,)), ((), ())), preferred_element_type=F32)


def _sigmoid(x):
    return 1.0 / (1.0 + jnp.exp(-x))


def _gelu(x):
    return 0.5 * x * (1.0 + jnp.tanh(GELU_C * (x + 0.044715 * x * x * x)))


def _gelu_grad(x):
    t = jnp.tanh(GELU_C * (x + 0.044715 * x * x * x))
    return 0.5 * (1.0 + t) + 0.5 * x * (1.0 - t * t) * (GELU_C * (1.0 + 3 * 0.044715 * x * x))


def _rope3(x, cos, sin_a, sin_b):
    return x * cos + pltpu.roll(x, 8, 2) * sin_a + pltpu.roll(x, HEAD_PAD - 8, 2) * sin_b


def _rope3_t(d, cos, sin_a, sin_b):
    return d * cos + pltpu.roll(d * sin_a, HEAD_PAD - 8, 2) + pltpu.roll(d * sin_b, 8, 2)


def _group_sum(x, ones_ref):
    hi = x.astype(BF16)
    lo = (x - hi.astype(F32)).astype(BF16)
    return _dot(hi, ones_ref[...]) + _dot(lo, ones_ref[...])


def _params(n_axes):
    return pltpu.CompilerParams(dimension_semantics=("arbitrary",) * n_axes, vmem_limit_bytes=V7X_VMEM_LIMIT)


def _whole(shape):
    nd = len(shape)
    return pl.BlockSpec(shape, lambda *_: (0,) * nd, pipeline_mode=pl.Buffered(1))


def _sds(shape, dtype):
    return jax.ShapeDtypeStruct(shape, dtype)


def _token_tile(s, ctx):
    return _div_tile(math.gcd(s, ctx), 256, CHUNK)


def _other_chips(x, y):
    return [(1 - x, y), (x, 1 - y), (1 - x, 1 - y)]


def _exch_copies(kind, srcs, dsts, send_sems, recv_sems, local_sems, with_arrivals):
    x, y, c = lax.axis_index("x"), lax.axis_index("y"), lax.axis_index("c")
    me = 2 * x + y
    local, sends, arrivals = [], [], []
    for w, (src, dst) in enumerate(zip(srcs, dsts)):
        own = src if kind == "gather" else src.at[me]
        local.append(pltpu.make_async_copy(own, dst.at[me], local_sems.at[w]))
        for k, (px, py) in enumerate(_other_chips(x, y)):
            sem = dict(send_sem=send_sems.at[3 * w + k], recv_sem=recv_sems.at[3 * w + k], device_id=(px, py, c),
                       device_id_type=pl.DeviceIdType.MESH)
            out = src if kind == "gather" else src.at[2 * px + py]
            sends.append(pltpu.make_async_remote_copy(src_ref=out, dst_ref=dst.at[me], **sem))
            if with_arrivals:
                arrivals.append(pltpu.make_async_remote_copy(src_ref=own, dst_ref=dst.at[2 * px + py], **sem))
    return local, sends, arrivals


def _exch_start(kind, srcs, dsts, sems):
    local, sends, _ = _exch_copies(kind, srcs, dsts, *sems, with_arrivals=False)
    for cp in local + sends:
        cp.start()


def _exch_wait(kind, srcs, dsts, sems):
    local, sends, arrivals = _exch_copies(kind, srcs, dsts, *sems, with_arrivals=True)
    for cp in arrivals:
        cp.wait_recv()
    for cp in sends:
        cp.wait_send()
    for cp in local:
        cp.wait()


def _exch_scratch(n):
    return [pltpu.SemaphoreType.DMA((3 * n,)), pltpu.SemaphoreType.DMA((3 * n,)), pltpu.SemaphoreType.DMA((n,))]


def _exch_shapes(kind, arrays):
    return [_sds((N_CHIPS,) + a.shape if kind == "gather" else a.shape, a.dtype) for a in arrays]


def _sibling_copies(srcs, dsts, send_sems, recv_sems):
    x, y, c = lax.axis_index("x"), lax.axis_index("y"), lax.axis_index("c")
    return [pltpu.make_async_remote_copy(
        src_ref=src, dst_ref=dst, send_sem=send_sems.at[w], recv_sem=recv_sems.at[w], device_id=(x, y, 1 - c),
        device_id_type=pl.DeviceIdType.MESH) for w, (src, dst) in enumerate(zip(srcs, dsts))]


def _hosted_call(body, name, grid, in_specs, out_specs, out_shape, operands, scratch=(), exch=None, swap=None):
    n_axes = len(grid)
    if exch is None and swap is None:
        outs = pl.pallas_call(body, name=name, grid=grid, in_specs=list(in_specs), out_specs=list(out_specs),
                              out_shape=list(out_shape), scratch_shapes=list(scratch),
                              compiler_params=_params(n_axes))(*operands)
        return list(outs), []
    kind, arrays = exch if exch is not None else ("scatter", [])
    swaps = list(swap or [])
    n_in, n_out, n_sc, n_ex, n_sw = len(in_specs), len(out_specs), len(scratch), len(arrays), len(swaps)

    def hosted(*refs):
        cin, ein, sin = refs[:n_in], refs[n_in:n_in + n_ex], refs[n_in + n_ex:n_in + n_ex + n_sw]
        o0 = n_in + n_ex + n_sw
        cout, eout, sout = refs[o0:o0 + n_out], refs[o0 + n_out:o0 + n_out + n_ex], refs[o0 + n_out + n_ex:o0 + n_out + n_ex + n_sw]
        rest = refs[o0 + n_out + n_ex + n_sw:]
        csc, sems, swap_sems = rest[:n_sc], rest[n_sc:n_sc + 3], rest[n_sc + 3:]
        first = functools.reduce(jnp.logical_and, [pl.program_id(a) == 0 for a in range(n_axes)])
        last = functools.reduce(jnp.logical_and, [pl.program_id(a) == grid[a] - 1 for a in range(n_axes)])

        @pl.when(first)
        def _():
            if n_ex:
                _exch_start(kind, ein, eout, sems)
            for cp in _sibling_copies(sin, sout, *swap_sems) if n_sw else []:
                cp.start()

        body(*cin, *cout, *csc)

        @pl.when(last)
        def _():
            if n_ex:
                _exch_wait(kind, ein, eout, sems)
            for cp in _sibling_copies(sin, sout, *swap_sems) if n_sw else []:
                cp.wait()

    any_spec = pl.BlockSpec(memory_space=pl.ANY)
    swap_scratch = [pltpu.SemaphoreType.DMA((n_sw,)), pltpu.SemaphoreType.DMA((n_sw,))] if n_sw else []
    outs = pl.pallas_call(
        hosted, name=name, grid=grid, in_specs=list(in_specs) + [any_spec] * (n_ex + n_sw),
        out_specs=list(out_specs) + [any_spec] * (n_ex + n_sw),
        out_shape=list(out_shape) + _exch_shapes(kind, arrays) + [_sds(a.shape, a.dtype) for a in swaps],
        scratch_shapes=list(scratch) + _exch_scratch(max(n_ex, 1)) + swap_scratch, compiler_params=_params(n_axes),
    )(*operands, *arrays, *swaps)
    got = list(outs[n_out:n_out + n_ex])
    return (list(outs[:n_out]), got) if swap is None else (list(outs[:n_out]), got, list(outs[n_out + n_ex:]))


class _TokenTiles:
    def __init__(self, t, tc, tm):
        self.n_lat, self.n_ctx = t // tm, tc // tm
        self.n_all = self.n_lat + self.n_ctx

    def tile(self, i):
        return (i + self.n_lat) % self.n_all if self.n_ctx else i

    def is_lat(self, i):
        return self.tile(i) < self.n_lat

    def row(self, i):
        return (self.tile(i), 0)

    def lat_row(self, i):
        return (jnp.where(self.is_lat(i), self.tile(i), 0), 0) if self.n_ctx else (i, 0)

    def ctx_row(self, i):
        return (jnp.where(self.is_lat(i), self.n_ctx - 1, self.tile(i) - self.n_lat), 0)


def _ffn_fwd(x_lat, x_ctx, mod, nw, w1, w3, w2, k0, s, nb, tm, name, target=None, exch=None):
    t, d = x_lat.shape
    tc = 0 if x_ctx is None else x_ctx.shape[0]
    f = w1.shape[0]
    tiles = _TokenTiles(t, tc, tm)
    n_x = 2 if tc else 1
    n_t = 0 if target is None else 1
    assert not (tc and n_t)

    def body(*refs):
        x_ref = refs[0]
        t_ref = refs[n_x] if n_t else None
        mod_ref, nw_ref, w1_ref, w3_ref, w2_ref, o_ref, a_ref, b_ref, y_ref = refs[n_x + n_t:n_x + n_t + 9]
        i = pl.program_id(0)
        g = jnp.minimum((tiles.tile(i) * tm) // s, nb)
        shift = mod_ref[g, pl.ds(k0, 1), :]
        scale = mod_ref[g, pl.ds(k0 + 1, 1), :]
        gate = mod_ref[g, pl.ds(k0 + 2, 1), :]
        x = jnp.where(tiles.is_lat(i), x_ref[...], refs[1][...]) if tc else x_ref[...]
        r = lax.rsqrt(jnp.mean(x * x, axis=-1, keepdims=True) + EPS)
        hb = ((x * r * nw_ref[...]) * (1.0 + scale) + shift).astype(BF16)
        a = _dot_nt(hb, w1_ref[...])
        b = _dot_nt(hb, w3_ref[...])
        gb = (a * _sigmoid(a) * b).astype(BF16)
        y = _dot(gb, w2_ref[...])
        out = x + (0.5 * gate) * y
        a_ref[...] = a.astype(BF16)
        b_ref[...] = b.astype(BF16)
        y_ref[...] = y.astype(BF16)
        if n_t:
            loss_ref, acc_ref = refs[-2:]

            @pl.when(i == 0)
            def _():
                acc_ref[...] = jnp.zeros_like(acc_ref)

            e = out - t_ref[...]
            o_ref[...] = e * (1.0 / d)
            acc_ref[...] += jnp.sum(e * e, axis=0, keepdims=True)

            @pl.when(i == tiles.n_all - 1)
            def _():
                loss_ref[...] = (0.5 / d) * jnp.sum(acc_ref[...], axis=-1, keepdims=True)
        else:
            o_ref[...] = out

    td = pl.BlockSpec((tm, d), tiles.row)
    tf = pl.BlockSpec((tm, f), tiles.row)
    return _hosted_call(
        body, name, (tiles.n_all,),
        [pl.BlockSpec((tm, d), tiles.lat_row)] + ([pl.BlockSpec((tm, d), tiles.ctx_row)] if tc else []) + [td] * n_t
        + [_whole(mod.shape), _whole(nw.shape), _whole(w1.shape), _whole(w3.shape), _whole(w2.shape)],
        [td, tf, tf, td] + [pl.BlockSpec((1, 1), lambda i: (0, 0))] * n_t,
        [_sds((t + tc, d), F32), _sds((t + tc, f), BF16), _sds((t + tc, f), BF16), _sds((t + tc, d), BF16)]
        + [_sds((1, 1), F32)] * n_t,
        (x_lat,) + ((x_ctx,) if tc else ()) + ((target,) if n_t else ()) + (mod, nw, w1, w3, w2),
        scratch=[pltpu.VMEM((1, d), F32)] * n_t, exch=exch)


def _ffn_pre(x_lat, x_ctx, mod, nw, w1, w3, k0, s, nb, tm, name, exch=None):
    t, d = x_lat.shape
    tc = x_ctx.shape[0]
    f = w1.shape[0]
    tiles = _TokenTiles(t, tc, tm)

    def body(x_ref, c_ref, mod_ref, nw_ref, w1_ref, w3_ref, a_ref, b_ref):
        i = pl.program_id(0)
        g = jnp.minimum((tiles.tile(i) * tm) // s, nb)
        shift = mod_ref[g, pl.ds(k0, 1), :]
        scale = mod_ref[g, pl.ds(k0 + 1, 1), :]
        x = jnp.where(tiles.is_lat(i), x_ref[...], c_ref[...])
        r = lax.rsqrt(jnp.mean(x * x, axis=-1, keepdims=True) + EPS)
        hb = ((x * r * nw_ref[...]) * (1.0 + scale) + shift).astype(BF16)
        a_ref[...] = _dot_nt(hb, w1_ref[...]).astype(BF16)
        b_ref[...] = _dot_nt(hb, w3_ref[...]).astype(BF16)

    tf = pl.BlockSpec((tm, f), tiles.row)
    return _hosted_call(
        body, name, (tiles.n_all,),
        [pl.BlockSpec((tm, d), tiles.lat_row), pl.BlockSpec((tm, d), tiles.ctx_row), _whole(mod.shape), _whole(nw.shape),
         _whole(w1.shape), _whole(w3.shape)],
        [tf, tf], [_sds((t + tc, f), BF16)] * 2, (x_lat, x_ctx, mod, nw, w1, w3), exch=exch)


def _ffn_post(x_lat, x_ctx, a, b, mod, w2, k0, s, nb, tm, name, exch=None):
    t, d = x_lat.shape
    tc = x_ctx.shape[0]
    f = w2.shape[0]
    tiles = _TokenTiles(t, tc, tm)

    def body(x_ref, c_ref, a_ref, b_ref, mod_ref, w2_ref, o_ref, y_ref):
        i = pl.program_id(0)
        g = jnp.minimum((tiles.tile(i) * tm) // s, nb)
        gate = mod_ref[g, pl.ds(k0 + 2, 1), :]
        x = jnp.where(tiles.is_lat(i), x_ref[...], c_ref[...])
        av = a_ref[...].astype(F32)
        y = _dot((av * _sigmoid(av) * b_ref[...].astype(F32)).astype(BF16), w2_ref[...])
        o_ref[...] = x + (0.5 * gate) * y
        y_ref[...] = y.astype(BF16)

    td = pl.BlockSpec((tm, d), tiles.row)
    tf = pl.BlockSpec((tm, f), tiles.row)
    return _hosted_call(
        body, name, (tiles.n_all,),
        [pl.BlockSpec((tm, d), tiles.lat_row), pl.BlockSpec((tm, d), tiles.ctx_row), tf, tf, _whole(mod.shape),
         _whole(w2.shape)],
        [td, td], [_sds((t + tc, d), F32), _sds((t + tc, d), BF16)], (x_lat, x_ctx, a, b, mod, w2), exch=exch)


def _ffn_bwd(dout, x_lat, x_ctx, a, b, y, mod, nw, w1, w3, w2, k0, s, nb, tm, name, exch=None):
    t, d = x_lat.shape
    tc = 0 if x_ctx is None else x_ctx.shape[0]
    f = w1.shape[0]
    nch = 2 if (f // 2) % LANES == 0 and f % 2 == 0 else 1
    fc = f // nch
    tiles = _TokenTiles(t, tc, tm)
    n_x = 2 if tc else 1

    def body(*refs):
        do_ref, x_ref = refs[0], refs[1]
        (a_ref, b_ref, y_ref, mod_ref, nw_ref, w1_ref, w3_ref, w2_ref,
         dx_ref, h_ref, g_ref, da_ref, db_ref, dy_ref, dmod_ref, dnw_ref) = refs[1 + n_x:]
        i = pl.program_id(0)

        @pl.when(i == 0)
        def _():
            dmod_ref[...] = jnp.zeros_like(dmod_ref)
            dnw_ref[...] = jnp.zeros_like(dnw_ref)

        g = jnp.minimum((tiles.tile(i) * tm) // s, nb)
        shift = mod_ref[g, pl.ds(k0, 1), :]
        scale = mod_ref[g, pl.ds(k0 + 1, 1), :]
        gate = mod_ref[g, pl.ds(k0 + 2, 1), :]
        x = jnp.where(tiles.is_lat(i), x_ref[...], refs[2][...]) if tc else x_ref[...]
        dout_v = do_ref[...]
        r = lax.rsqrt(jnp.mean(x * x, axis=-1, keepdims=True) + EPS)
        xh = x * r
        n = xh * nw_ref[...]
        h_ref[...] = (n * (1.0 + scale) + shift).astype(BF16)
        dyb = ((0.5 * gate) * dout_v).astype(BF16)
        dy_ref[...] = dyb
        dmod_ref[g, pl.ds(k0 + 2, 1), :] += 0.5 * jnp.sum(dout_v * y_ref[...].astype(F32), axis=0, keepdims=True)
        dh = jnp.zeros((tm, d), F32)
        for c in range(nch):
            sl = slice(c * fc, (c + 1) * fc)
            dg = _dot_nt(dyb, w2_ref[sl, :])
            av = a_ref[:, sl].astype(F32)
            bv = b_ref[:, sl].astype(F32)
            sig = _sigmoid(av)
            sa = av * sig
            g_ref[:, sl] = (sa * bv).astype(BF16)
            dab = (dg * bv * (sig * (1.0 + av * (1.0 - sig)))).astype(BF16)
            dbb = (dg * sa).astype(BF16)
            da_ref[:, sl] = dab
            db_ref[:, sl] = dbb
            dh = dh + _dot(dab, w1_ref[sl, :]) + _dot(dbb, w3_ref[sl, :])
        dmod_ref[g, pl.ds(k0, 1), :] += jnp.sum(dh, axis=0, keepdims=True)
        dmod_ref[g, pl.ds(k0 + 1, 1), :] += jnp.sum(dh * n, axis=0, keepdims=True)
        dn = dh * (1.0 + scale)
        dnw_ref[...] += jnp.sum(dn * xh, axis=0, keepdims=True)
        dxh = dn * nw_ref[...]
        dx_ref[...] = dout_v + r * (dxh - xh * jnp.mean(dxh * xh, axis=-1, keepdims=True))

    td = pl.BlockSpec((tm, d), tiles.row)
    tf = pl.BlockSpec((tm, f), tiles.row)
    lat = pl.BlockSpec((tm, d), tiles.lat_row)
    ta = t + tc
    return _hosted_call(
        body, name, (tiles.n_all,),
        [td, lat] + ([pl.BlockSpec((tm, d), tiles.ctx_row)] if tc else [])
        + [tf, tf, td, _whole(mod.shape), _whole(nw.shape), _whole(w1.shape), _whole(w3.shape), _whole(w2.shape)],
        [lat, td, tf, tf, tf, td, pl.BlockSpec(mod.shape, lambda i: (0, 0, 0)), pl.BlockSpec((1, d), lambda i: (0, 0))],
        [_sds((t, d), F32), _sds((ta, d), BF16), _sds((ta, f), BF16), _sds((ta, f), BF16), _sds((ta, f), BF16),
         _sds((ta, d), BF16), _sds(mod.shape, F32), _sds((1, d), F32)],
        (dout, x_lat) + ((x_ctx,) if tc else ()) + (a, b, y, mod, nw, w1, w3, w2), exch=exch)


def _mm_tn(a, b, rows, name, exch=None):
    m = a.shape[1]
    n = b.shape[1]
    bm = _div_tile(m, 1408, LANES)
    bn = _div_tile(n, 1408, LANES)
    bk = _div_tile(rows, 2304, LANES)
    nk = rows // bk

    def body(a_ref, b_ref, o_ref, acc_ref):
        k = pl.program_id(2)

        @pl.when(k == 0)
        def _():
            acc_ref[...] = jnp.zeros_like(acc_ref)

        acc_ref[...] += _dot_tn(a_ref[...], b_ref[...])

        @pl.when(k == nk - 1)
        def _():
            o_ref[...] = acc_ref[...].astype(BF16)

    (out,), got = _hosted_call(
        body, name, (m // bm, n // bn, nk),
        [pl.BlockSpec((bk, bm), lambda i, j, k: (k, i)), pl.BlockSpec((bk, bn), lambda i, j, k: (k, j))],
        [pl.BlockSpec((bm, bn), lambda i, j, k: (i, j))], [_sds((m, n), BF16)], (a, b),
        scratch=[pltpu.VMEM((bm, bn), F32)], exch=exch)
    return out if exch is None else (out, got)


def _mixin_fwd(xs, mod, nw, wp, s, nb, tm):
    t, d = xs.shape

    def body(x_ref, mod_ref, nw_ref, wp_ref, h_ref, p_ref):
        g = jnp.minimum((pl.program_id(0) * tm) // s, nb)
        shift = mod_ref[g, pl.ds(3, 1), :]
        scale = mod_ref[g, pl.ds(4, 1), :]
        x = x_ref[...]
        r = lax.rsqrt(jnp.mean(x * x, axis=-1, keepdims=True) + EPS)
        hb = ((x * r * nw_ref[...]) * (1.0 + scale) + shift).astype(BF16)
        h_ref[...] = hb
        p_ref[...] = _dot_nt(hb, wp_ref[...]).astype(BF16)

    row = lambda i: (i, 0)
    return pl.pallas_call(
        body, name="mixin_fwd", grid=(t // tm,),
        in_specs=[pl.BlockSpec((tm, d), row), _whole(mod.shape), _whole(nw.shape), _whole(wp.shape)],
        out_specs=[pl.BlockSpec((tm, d), row), pl.BlockSpec((tm, PROJ_COLS), row)],
        out_shape=[_sds((t, d), BF16), _sds((t, PROJ_COLS), BF16)], compiler_params=_params(1),
    )(xs, mod, nw, wp)


def _mixin_bwd(dp0, duv, xs, dres, mod, nw, wp, s, nb, tm, swap):
    t_all, d = xs.shape
    nlat = dres.shape[0] // tm

    def body(p0_ref, uv_ref, x_ref, dr_ref, mod_ref, nw_ref, wp_ref, dx_ref, dmod_ref, dnw_ref):
        i = pl.program_id(0)

        @pl.when(i == 0)
        def _():
            dmod_ref[...] = jnp.zeros_like(dmod_ref)
            dnw_ref[...] = jnp.zeros_like(dnw_ref)

        lat = i < nlat
        g = jnp.minimum((i * tm) // s, nb)
        scale = mod_ref[g, pl.ds(4, 1), :]
        dh = _dot(p0_ref[...], wp_ref[0:512, :])
        extra = _dot(uv_ref[...], wp_ref[512:1536, :])
        dh = dh + jnp.where(lat, extra, 0.0)
        x = x_ref[...]
        r = lax.rsqrt(jnp.mean(x * x, axis=-1, keepdims=True) + EPS)
        xh = x * r
        n = xh * nw_ref[...]
        dmod_ref[g, pl.ds(3, 1), :] += jnp.sum(dh, axis=0, keepdims=True)
        dmod_ref[g, pl.ds(4, 1), :] += jnp.sum(dh * n, axis=0, keepdims=True)
        dn = dh * (1.0 + scale)
        dnw_ref[...] += jnp.sum(dn * xh, axis=0, keepdims=True)
        dxh = dn * nw_ref[...]
        dx_ref[...] = jnp.where(lat, dr_ref[...], 0.0) + r * (dxh - xh * jnp.mean(dxh * xh, axis=-1, keepdims=True))

    row = lambda i: (i, 0)
    lrow = lambda i: (jnp.minimum(i, nlat - 1), 0)
    return _hosted_call(
        body, "mixin_bwd", (t_all // tm,),
        [pl.BlockSpec((tm, 512), row), pl.BlockSpec((tm, 1024), lrow), pl.BlockSpec((tm, d), row),
         pl.BlockSpec((tm, d), lrow), _whole(mod.shape), _whole(nw.shape), _whole(wp.shape)],
        [pl.BlockSpec((tm, d), row), pl.BlockSpec(mod.shape, lambda i: (0, 0, 0)), pl.BlockSpec((1, d), lambda i: (0, 0))],
        [_sds((t_all, d), F32), _sds(mod.shape, F32), _sds((1, d), F32)], (dp0, duv, xs, dres, mod, nw, wp), swap=swap)


def _prep_fwd(proj, row0, nb, s, pos0, sk, key0, into, tabs, wq, wk, wv, kvaw, qaw, qnw, knw, tm, with_q, name):
    nblk = s // tm
    n_into = 0 if into is None else 2

    def body(p_ref, cos_ref, sa_ref, sb_ref, wq_ref, wk_ref, wv_ref, kvaw_ref, qaw_ref, qnw_ref, knw_ref, *rest):
        outs, heads_ref = rest[n_into:-1], rest[-1]
        q_ref, k_ref, v_ref = outs if with_q else (None,) + outs
        cos, sin_a, sin_b = cos_ref[...][None], sa_ref[...][None], sb_ref[...][None]

        def normed_roped(w_ref, src, extra, nw_ref, o_ref, post):
            for h in range(HEADS):
                heads_ref[h] = _dot_nt(src, w_ref[h]) if extra is None else _dot(src, w_ref[h])
            xp = heads_ref[...] if extra is None else heads_ref[...] + extra[None]
            r = lax.rsqrt(jnp.sum(xp * xp, axis=-1, keepdims=True) * (1.0 / QK_HEAD) + EPS)
            o_ref[...] = _rope3(xp * r * (nw_ref[...] * post)[None], cos, sin_a, sin_b).astype(BF16)

        ckv = p_ref[:, 0:128].astype(F32)
        rkv = lax.rsqrt(jnp.mean(ckv * ckv, axis=-1, keepdims=True) + EPS)
        ckvb = (ckv * rkv * kvaw_ref[...]).astype(BF16)
        normed_roped(wk_ref, ckvb, p_ref[:, 128:256].astype(F32), knw_ref, k_ref, 1.0)
        for j in range(HEADS // 2):
            v_ref[j] = _dot(ckvb, wv_ref[j]).astype(BF16)
        if with_q:
            cq = p_ref[:, 256:512].astype(F32)
            rq = lax.rsqrt(jnp.mean(cq * cq, axis=-1, keepdims=True) + EPS)
            normed_roped(wq_ref, (cq * rq * qaw_ref[...]).astype(BF16), None, qnw_ref, q_ref, SOFTMAX_SCALE)

    tab = pl.BlockSpec((tm, HEAD_PAD), lambda i: (pos0 + i % nblk, 0))
    qspec = pl.BlockSpec((None, HEADS, tm, HEAD_PAD), lambda i: (i // nblk, 0, i % nblk, 0))
    kspec = pl.BlockSpec((None, HEADS, tm, HEAD_PAD), lambda i: (i // nblk, 0, key0 + i % nblk, 0))
    vspec = pl.BlockSpec((None, HEADS // 2, tm, HEAD_PAD), lambda i: (i // nblk, 0, key0 + i % nblk, 0))
    qshape = _sds((nb, HEADS, s, HEAD_PAD), BF16)
    kshape = _sds((nb, HEADS, sk, HEAD_PAD), BF16)
    vshape = _sds((nb, HEADS // 2, sk, HEAD_PAD), BF16)
    n_q = 1 if with_q else 0
    return pl.pallas_call(
        body, name=name, grid=(nb * nblk,),
        in_specs=[pl.BlockSpec((tm, 512), lambda i: (row0 + i, 0)), tab, tab, tab, _whole(wq.shape), _whole(wk.shape),
                  _whole(wv.shape), _whole(kvaw.shape), _whole(qaw.shape), _whole(qnw.shape), _whole(knw.shape)]
        + [pl.BlockSpec(memory_space=pl.ANY)] * n_into,
        out_specs=([qspec] if with_q else []) + [kspec, vspec],
        out_shape=([qshape] if with_q else []) + [kshape, vshape],
        scratch_shapes=[pltpu.VMEM((HEADS, tm, HEAD_PAD), F32)],
        input_output_aliases={11: n_q, 12: n_q + 1} if n_into else {}, compiler_params=_params(1),
    )(proj, *tabs, wq, wk, wv, kvaw, qaw, qnw, knw, *(into or ()))


def _prep_bwd(proj, row0, nb, s, pos0, key0, dp_rows, dp_into, tabs, wq, wk, wv, kvaw, qaw, qnw, knw, dq, dk, dv, init, tm,
              name):
    nblk = s // tm
    with_q = dq is not None
    n_init = 0 if init is None else len(init)
    n_into = 0 if dp_into is None else 1

    def body(*refs):
        p_ref, cos_ref, sa_ref, sb_ref, wq_ref, wk_ref, wv_ref, kvaw_ref, qaw_ref, qnw_ref, knw_ref = refs[:11]
        rest = list(refs[11:])
        dq_ref = rest.pop(0) if with_q else None
        dk_ref, dv_ref = rest.pop(0), rest.pop(0)
        init_refs = [rest.pop(0) for _ in range(n_init)]
        if n_into:
            rest.pop(0)
        dp_ref = rest.pop(0)
        if with_q:
            dwq_ref, dqaw_ref, dqnw_ref = rest.pop(0), rest.pop(0), rest.pop(0)
        dwk_ref, dwv_ref, dkvaw_ref, dknw_ref, heads_ref, dhb_ref, dkr_ref = rest
        accs = [dwk_ref, dwv_ref, dkvaw_ref, dknw_ref]

        @pl.when(pl.program_id(0) == 0)
        def _():
            for k, acc in enumerate(accs):
                acc[...] = init_refs[k][...] if n_init else jnp.zeros_like(acc)
            if with_q:
                dwq_ref[...] = jnp.zeros_like(dwq_ref)
                dqaw_ref[...] = jnp.zeros_like(dqaw_ref)
                dqnw_ref[...] = jnp.zeros_like(dqnw_ref)

        cos, sin_a, sin_b = cos_ref[...][None], sa_ref[...][None], sb_ref[...][None]
        lane = lax.broadcasted_iota(jnp.int32, (tm, HEAD_PAD), 1)
        rope_lanes = (lane >= QK_NOPE) & (lane < QK_HEAD)

        def heads_bwd(w_ref, src, extra, nw_ref, d_ref, dnw_ref, dw_ref, post):
            w_t = extra is None
            for h in range(HEADS):
                heads_ref[h] = _dot_nt(src, w_ref[h]) if w_t else _dot(src, w_ref[h])
            xp = heads_ref[...] if extra is None else heads_ref[...] + extra[None]
            r = lax.rsqrt(jnp.sum(xp * xp, axis=-1, keepdims=True) * (1.0 / QK_HEAD) + EPS)
            xh = xp * r
            dn = _rope3_t(d_ref[...], cos, sin_a, sin_b)
            dnw_ref[...] += post * jnp.sum(jnp.sum(dn * xh, axis=0), axis=0, keepdims=True)
            dxh = dn * (nw_ref[...] * post)[None]
            dxp = r * (dxh - xh * (jnp.sum(dxh * xh, axis=-1, keepdims=True) * (1.0 / QK_HEAD)))
            dhb_ref[...] = dxp.astype(BF16)
            dsrc = jnp.zeros((tm, src.shape[1]), F32)
            for h in range(HEADS):
                dsrc = dsrc + (_dot(dhb_ref[h], w_ref[h]) if w_t else _dot_nt(dhb_ref[h], w_ref[h]))
                dw_ref[h] += _dot_tn(src, dhb_ref[h])
            return dsrc, jnp.sum(dxp, axis=0)

        ckv = p_ref[:, 0:128].astype(F32)
        rkv = lax.rsqrt(jnp.mean(ckv * ckv, axis=-1, keepdims=True) + EPS)
        ckvh = ckv * rkv
        ckvb = (ckvh * kvaw_ref[...]).astype(BF16)
        for h in range(HEADS):
            dkr_ref[h] = dk_ref[h].T
        dckv, dkp_sum = heads_bwd(wk_ref, ckvb, p_ref[:, 128:256].astype(F32), knw_ref, dkr_ref, dknw_ref, dwk_ref,
                                  1.0)
        for j in range(HEADS // 2):
            dvb = dv_ref[j].T.astype(BF16)
            dckv = dckv + _dot_nt(dvb, wv_ref[j])
            dwv_ref[j] += _dot_tn(ckvb, dvb)
        dkvaw_ref[...] += jnp.sum(dckv * ckvh, axis=0, keepdims=True)
        dch = dckv * kvaw_ref[...]
        dp_ref[:, 0:128] = (rkv * (dch - ckvh * jnp.mean(dch * ckvh, axis=-1, keepdims=True))).astype(BF16)
        dp_ref[:, 128:256] = jnp.where(rope_lanes, dkp_sum, 0.0).astype(BF16)
        if with_q:
            cq = p_ref[:, 256:512].astype(F32)
            rq = lax.rsqrt(jnp.mean(cq * cq, axis=-1, keepdims=True) + EPS)
            cqh = cq * rq
            cqb = (cqh * qaw_ref[...]).astype(BF16)
            dcq, _ = heads_bwd(wq_ref, cqb, None, qnw_ref, dq_ref, dqnw_ref, dwq_ref, SOFTMAX_SCALE)
            dqaw_ref[...] += jnp.sum(dcq * cqh, axis=0, keepdims=True)
            dqc = dcq * qaw_ref[...]
            dp_ref[:, 256:512] = (rq * (dqc - cqh * jnp.mean(dqc * cqh, axis=-1, keepdims=True))).astype(BF16)
        else:
            dp_ref[:, 256:512] = jnp.zeros((tm, Q_LORA), BF16)

    tab = pl.BlockSpec((tm, HEAD_PAD), lambda i: (pos0 + i % nblk, 0))
    qspec = pl.BlockSpec((None, HEADS, tm, HEAD_PAD), lambda i: (i // nblk, 0, i % nblk, 0))
    kspec = pl.BlockSpec((None, HEADS, HEAD_PAD, tm), lambda i: (i // nblk, 0, 0, key0 + i % nblk))
    vspec = pl.BlockSpec((None, HEADS // 2, HEAD_PAD, tm), lambda i: (i // nblk, 0, 0, key0 + i % nblk))

    def acc_spec(shape):
        nd = len(shape)
        return pl.BlockSpec(shape, lambda i: (0,) * nd)

    acc_shapes = [(HEADS, KV_LORA, HEAD_PAD), (HEADS // 2, KV_LORA, HEAD_PAD), (1, KV_LORA), (1, HEAD_PAD)]
    q_shapes = [(HEADS, Q_LORA, HEAD_PAD), (1, Q_LORA), (1, HEAD_PAD)] if with_q else []
    out_shapes = [(dp_rows, 512)] + q_shapes + acc_shapes
    n_before = 11 + (1 if with_q else 0) + 2 + n_init
    return pl.pallas_call(
        body, name=name, grid=(nb * nblk,),
        in_specs=[pl.BlockSpec((tm, 512), lambda i: (row0 + i, 0)), tab, tab, tab, _whole(wq.shape), _whole(wk.shape),
                  _whole(wv.shape), _whole(kvaw.shape), _whole(qaw.shape), _whole(qnw.shape), _whole(knw.shape)]
        + ([qspec] if with_q else []) + [kspec, vspec] + [_whole(a.shape) for a in (init or [])]
        + [pl.BlockSpec(memory_space=pl.ANY)] * n_into,
        out_specs=[pl.BlockSpec((tm, 512), lambda i: (row0 + i, 0))] + [acc_spec(sh) for sh in q_shapes + acc_shapes],
        out_shape=[_sds(out_shapes[0], BF16)] + [_sds(sh, F32) for sh in out_shapes[1:]],
        scratch_shapes=[pltpu.VMEM((HEADS, tm, HEAD_PAD), F32), pltpu.VMEM((HEADS, tm, HEAD_PAD), BF16),
                        pltpu.VMEM((HEADS, tm, HEAD_PAD), F32)],
        input_output_aliases={n_before: 0} if n_into else {}, compiler_params=_params(1),
    )(proj, *tabs, wq, wk, wv, kvaw, qaw, qnw, knw, *([dq] if with_q else []), dk, dv, *(init or []),
      *([dp_into] if n_into else []))


def _attn_fwd(q, k, v, tq, exch=None):
    nb, _, s, _ = q.shape
    sk = k.shape[2]
    nq = s // tq

    def body(q_ref, k_ref, v_ref, o_ref, lse_ref, vext_ref):
        @pl.when(pl.program_id(2) == 0)
        def _():
            vext_ref[:, 0:HEAD_PAD] = v_ref[...]
            vext_ref[:, HEAD_PAD:2 * HEAD_PAD] = jnp.ones((sk, HEAD_PAD), BF16)

        lane = lax.broadcasted_iota(jnp.int32, (tq, HEAD_PAD), 1)
        outs = []
        for hh in range(2):
            sc = _dot_nt(q_ref[hh], k_ref[hh])
            m = jnp.max(sc, axis=-1, keepdims=True)
            pv = _dot(jnp.exp2(sc - m).astype(BF16), vext_ref[...])
            l = pv[:, HEAD_PAD:HEAD_PAD + 1]
            outs.append(pv[:, 0:HEAD_PAD] / l)
            lse_ref[hh] = m + jnp.log2(l)
        o_ref[...] = jnp.where(lane < V_HEAD, outs[0], outs[1]).astype(BF16)

    (o, lse), got = _hosted_call(
        body, "attn_fwd", (nb, HEADS // 2, nq),
        [pl.BlockSpec((None, 2, tq, HEAD_PAD), lambda b, j, i: (b, j, i, 0)),
         pl.BlockSpec((None, 2, sk, HEAD_PAD), lambda b, j, i: (b, j, 0, 0)),
         pl.BlockSpec((None, None, sk, HEAD_PAD), lambda b, j, i: (b, j, 0, 0))],
        [pl.BlockSpec((tq, HEAD_PAD), lambda b, j, i: (b * nq + i, j)),
         pl.BlockSpec((None, 2, tq, 1), lambda b, j, i: (b, j, i, 0))],
        [_sds((nb * s, MLA_W + GMLP_W), BF16), _sds((nb, HEADS, s, 1), F32)], (q, k, v),
        scratch=[pltpu.VMEM((sk, 2 * HEAD_PAD), BF16)], exch=exch)
    return o, lse, got


def _attn_bwd(q, k, v, do, o, lse, tq, exch=None):
    nb, _, s, _ = q.shape
    sk = k.shape[2]
    nq = s // tq

    def body(q_ref, k_ref, v_ref, do_ref, o_ref, lse_ref, dq_ref, dkt_ref, dvt_ref):
        @pl.when(pl.program_id(2) == 0)
        def _():
            dkt_ref[...] = jnp.zeros_like(dkt_ref)
            dvt_ref[...] = jnp.zeros_like(dvt_ref)

        lane = lax.broadcasted_iota(jnp.int32, (tq, HEAD_PAD), 1)
        dov = do_ref[...]
        prod = dov.astype(F32) * o_ref[...].astype(F32)
        for hh in range(2):
            mine = (lane < V_HEAD) if hh == 0 else (lane >= V_HEAD)
            doh = jnp.where(mine, dov, jnp.zeros_like(dov))
            delta = jnp.sum(jnp.where(mine, prod, 0.0), axis=-1, keepdims=True)
            qh = q_ref[hh]
            q_ln2 = (qh.astype(F32) * LN2).astype(BF16)
            kv = k_ref[hh]
            p = jnp.exp2(_dot_nt(qh, kv) - lse_ref[hh])
            u = (p * (_dot_nt(doh, v_ref[...]) - delta)).astype(BF16)
            dq_ref[hh] = _dot(u, kv) * LN2
            dkt_ref[hh] += _dot_tn(q_ln2, u)
            dvt_ref[...] += _dot_tn(doh, p.astype(BF16))

    qspec = pl.BlockSpec((None, 2, tq, HEAD_PAD), lambda b, j, i: (b, j, i, 0))
    kspec = pl.BlockSpec((None, 2, sk, HEAD_PAD), lambda b, j, i: (b, j, 0, 0))
    vspec = pl.BlockSpec((None, None, sk, HEAD_PAD), lambda b, j, i: (b, j, 0, 0))
    ospec = pl.BlockSpec((tq, HEAD_PAD), lambda b, j, i: (b * nq + i, j))
    return _hosted_call(
        body, "attn_bwd", (nb, HEADS // 2, nq),
        [qspec, kspec, vspec, ospec, ospec, pl.BlockSpec((None, 2, tq, 1), lambda b, j, i: (b, j, i, 0))],
        [qspec, pl.BlockSpec((None, 2, HEAD_PAD, sk), lambda b, j, i: (b, j, 0, 0)),
         pl.BlockSpec((None, None, HEAD_PAD, sk), lambda b, j, i: (b, j, 0, 0))],
        [_sds(q.shape, F32), _sds((nb, HEADS, HEAD_PAD, sk), F32), _sds((nb, HEADS // 2, HEAD_PAD, sk), F32)],
        (q, k, v, do, o, lse), exch=exch)


def _group_masks(rows):
    lane = lax.broadcasted_iota(jnp.int32, (rows, GMLP_W), 1)
    return [(lane >= g * GROUP_DIM) & (lane < (g + 1) * GROUP_DIM) for g in range(GROUPS)]


def _gmlp_fwd(proj, mixcat, wcat, bias, vnw, ones, tm):
    t = mixcat.shape[0]

    def body(u_ref, v_ref, wcat_ref, bias_ref, vnw_ref, ones_ref, _, o_ref):
        masks = _group_masks(CHUNK)
        gv = _gelu(v_ref[...].astype(F32))
        rv = lax.rsqrt(_group_sum(gv * gv, ones_ref) * (1.0 / GROUP_DIM) + EPS)
        vnb = (gv * rv * vnw_ref[...]).astype(BF16)
        for c in range(tm // CHUNK):
            rows = slice(c * CHUNK, (c + 1) * CHUNK)
            vc = vnb[rows]
            stack = jnp.concatenate([jnp.where(m, vc, jnp.zeros_like(vc)) for m in masks], axis=0)
            sp = _dot(wcat_ref[...], stack) + bias_ref[...]
            o_ref[rows, :] = (_gelu(u_ref[rows, :].astype(F32)) * sp).astype(BF16)

    return pl.pallas_call(
        body, name="gmlp_fwd", grid=(t // tm,),
        in_specs=[pl.BlockSpec((tm, GMLP_W), lambda i: (i, 1)), pl.BlockSpec((tm, GMLP_W), lambda i: (i, 2)),
                  _whole(wcat.shape), _whole(bias.shape), _whole(vnw.shape), _whole(ones.shape),
                  pl.BlockSpec(memory_space=pl.ANY)],
        out_specs=pl.BlockSpec((tm, GMLP_W), lambda i: (i, 1)),
        out_shape=_sds(mixcat.shape, BF16), input_output_aliases={6: 0}, compiler_params=_params(1),
    )(proj, proj, wcat, bias, vnw, ones, mixcat)


def _gmlp_bwd(proj, dsg, wcat, wcat_t, bias, vnw, ones, tm):
    t = dsg.shape[0]

    def body(u_ref, v_ref, dsg_ref, wcat_ref, wcatt_ref, bias_ref, vnw_ref, ones_ref,
             duv_ref, dws_ref, dbs_ref, dvnw_ref):
        @pl.when(pl.program_id(0) == 0)
        def _():
            dws_ref[...] = jnp.zeros_like(dws_ref)
            dbs_ref[...] = jnp.zeros_like(dbs_ref)
            dvnw_ref[...] = jnp.zeros_like(dvnw_ref)

        masks = _group_masks(CHUNK)
        v = v_ref[...].astype(F32)
        gv = _gelu(v)
        rv = lax.rsqrt(_group_sum(gv * gv, ones_ref) * (1.0 / GROUP_DIM) + EPS)
        xh = gv * rv
        vnb = (xh * vnw_ref[...]).astype(BF16)
        dvn_parts = []
        for c in range(tm // CHUNK):
            rows = slice(c * CHUNK, (c + 1) * CHUNK)
            vc = vnb[rows]
            stack = jnp.concatenate([jnp.where(m, vc, jnp.zeros_like(vc)) for m in masks], axis=0)
            sp = _dot(wcat_ref[...], stack) + bias_ref[...]
            u = u_ref[rows, :].astype(F32)
            dsg_c = dsg_ref[rows, :]
            duv_ref[rows, 0:GMLP_W] = (dsg_c * sp * _gelu_grad(u)).astype(BF16)
            ds = dsg_c * _gelu(u)
            dstack = jnp.concatenate([jnp.where(m, ds, 0.0) for m in masks], axis=0)
            dbs_ref[...] += jnp.broadcast_to(jnp.sum(dstack, axis=-1, keepdims=True), dbs_ref.shape)
            dstb = dstack.astype(BF16)
            dvn_parts.append(_dot(wcatt_ref[...], dstb))
            dws_ref[...] += _dot_nt(dstb, vc)
        dvn = jnp.concatenate(dvn_parts, axis=0) if len(dvn_parts) > 1 else dvn_parts[0]
        dvnw_ref[...] += jnp.sum(dvn * xh, axis=0, keepdims=True)
        dxh = dvn * vnw_ref[...]
        gm = _group_sum(dxh * xh, ones_ref) * (1.0 / GROUP_DIM)
        duv_ref[:, GMLP_W:2 * GMLP_W] = (rv * (dxh - xh * gm) * _gelu_grad(v)).astype(BF16)

    row = pl.BlockSpec((tm, GMLP_W), lambda i: (i, 0))
    return pl.pallas_call(
        body, name="gmlp_bwd", grid=(t // tm,),
        in_specs=[pl.BlockSpec((tm, GMLP_W), lambda i: (i, 1)), pl.BlockSpec((tm, GMLP_W), lambda i: (i, 2)), row,
                  _whole(wcat.shape), _whole(wcat_t.shape), _whole(bias.shape), _whole(vnw.shape), _whole(ones.shape)],
        out_specs=[pl.BlockSpec((tm, 2 * GMLP_W), lambda i: (i, 0)), pl.BlockSpec((GROUPS * CHUNK, CHUNK), lambda i: (0, 0)),
                   pl.BlockSpec((GROUPS * CHUNK, CHUNK), lambda i: (0, 0)), pl.BlockSpec((1, GMLP_W), lambda i: (0, 0))],
        out_shape=[_sds((t, 2 * GMLP_W), BF16), _sds((GROUPS * CHUNK, CHUNK), F32), _sds((GROUPS * CHUNK, CHUNK), F32),
                   _sds((1, GMLP_W), F32)],
        compiler_params=_params(1),
    )(proj, proj, dsg, wcat, wcat_t, bias, vnw, ones)


def _mixout_fwd(mixcat, xs, mod, wout, s, tm):
    t, width = mixcat.shape
    d = xs.shape[1]

    def body(cat_ref, x_ref, mod_ref, w_ref, x2_ref, mix_ref):
        g = (pl.program_id(0) * tm) // s
        gate = mod_ref[g, pl.ds(5, 1), :]
        mix = _dot(cat_ref[...], w_ref[...])
        x2_ref[...] = x_ref[...] + gate * mix
        mix_ref[...] = mix.astype(BF16)

    row = lambda i: (i, 0)
    return pl.pallas_call(
        body, name="mixout_fwd", grid=(t // tm,),
        in_specs=[pl.BlockSpec((tm, width), row), pl.BlockSpec((tm, d), row), _whole(mod.shape), _whole(wout.shape)],
        out_specs=[pl.BlockSpec((tm, d), row), pl.BlockSpec((tm, d), row)],
        out_shape=[_sds((t, d), F32), _sds((t, d), BF16)], compiler_params=_params(1),
    )(mixcat, xs, mod, wout)


def _mixout_bwd(dx2, mix, mod, wout, s, tm):
    t, d = dx2.shape

    def body(dx_ref, mix_ref, mod_ref, w_ref, dmix_ref, do_ref, dsg_ref, dmod_ref):
        i = pl.program_id(0)

        @pl.when(i == 0)
        def _():
            dmod_ref[...] = jnp.zeros_like(dmod_ref)

        g = (i * tm) // s
        gate = mod_ref[g, pl.ds(5, 1), :]
        dx = dx_ref[...]
        dmod_ref[g, pl.ds(5, 1), :] += jnp.sum(dx * mix_ref[...].astype(F32), axis=0, keepdims=True)
        dmb = (gate * dx).astype(BF16)
        dmix_ref[...] = dmb
        do_ref[...] = _dot_nt(dmb, w_ref[0:MLA_W, :]).astype(BF16)
        dsg_ref[...] = _dot_nt(dmb, w_ref[MLA_W:MLA_W + GMLP_W, :])

    row = lambda i: (i, 0)
    return pl.pallas_call(
        body, name="mixout_bwd", grid=(t // tm,),
        in_specs=[pl.BlockSpec((tm, d), row), pl.BlockSpec((tm, d), row), _whole(mod.shape), _whole(wout.shape)],
        out_specs=[pl.BlockSpec((tm, d), row), pl.BlockSpec((tm, MLA_W), row), pl.BlockSpec((tm, GMLP_W), row),
                   pl.BlockSpec(mod.shape, lambda i: (0, 0, 0))],
        out_shape=[_sds((t, d), BF16), _sds((t, MLA_W), BF16), _sds((t, GMLP_W), F32), _sds(mod.shape, F32)],
        compiler_params=_params(1),
    )(dx2, mix, mod, wout)


def _swap_cores(parts, name):
    n = len(parts)

    def body(*refs):
        srcs, outs, send_sems, recv_sems = refs[:n], refs[n:2 * n], refs[2 * n], refs[2 * n + 1]
        x, y, c = lax.axis_index("x"), lax.axis_index("y"), lax.axis_index("c")
        copies = [pltpu.make_async_remote_copy(
            src_ref=srcs[w], dst_ref=outs[w], send_sem=send_sems.at[w], recv_sem=recv_sems.at[w],
            device_id=(x, y, 1 - c), device_id_type=pl.DeviceIdType.MESH) for w in range(n)]
        for cp in copies:
            cp.start()
        for cp in copies:
            cp.wait()

    any_spec = pl.BlockSpec(memory_space=pl.ANY)
    return pl.pallas_call(
        body, name=name, in_specs=[any_spec] * n, out_specs=[any_spec] * n,
        out_shape=[_sds(p.shape, p.dtype) for p in parts],
        scratch_shapes=[pltpu.SemaphoreType.DMA((n,)), pltpu.SemaphoreType.DMA((n,))],
    )(*parts)


def _row_tile(r, c, mult):
    return _div_tile(r, max(mult, (1 << 18) // c), mult)


def _sum_slots(recv, name):
    _, r, c = recv.shape
    tr = _row_tile(r, c, 16)

    def body(r_ref, o_ref):
        f = lambda k: r_ref[k].astype(F32)
        o_ref[...] = ((f(0) + f(1)) + f(2)) + f(3)

    return pl.pallas_call(
        body, name=name, grid=(r // tr,),
        in_specs=[pl.BlockSpec((N_CHIPS, tr, c), lambda i: (0, i, 0))],
        out_specs=pl.BlockSpec((tr, c), lambda i: (i, 0)),
        out_shape=_sds((r, c), F32), compiler_params=_params(1),
    )(recv)


def _adamw(parts, w, m, v, name, exch=None, swap=None):
    r, wd = w.shape
    tr = _row_tile(r, wd, 8)
    c1 = 1.0 / (1.0 - ADAM_B1 ** ADAM_STEP)
    c2 = 1.0 / (1.0 - ADAM_B2 ** ADAM_STEP)
    n_p = len(parts)

    def body(*refs):
        p_refs = refs[:n_p]
        w_ref, m_ref, v_ref, g_ref, d_ref, nm_ref, nv_ref = refs[n_p:]
        g = p_refs[0][...]
        for p_ref in p_refs[1:]:
            g = g + p_ref[...]
        nm = ADAM_B1 * m_ref[...] + (1.0 - ADAM_B1) * g
        nv = ADAM_B2 * v_ref[...] + (1.0 - ADAM_B2) * (g * g)
        g_ref[...] = g
        nm_ref[...] = nm
        nv_ref[...] = nv
        d_ref[...] = -ADAM_LR * ((nm * c1) / (jnp.sqrt(nv * c2) + ADAM_EPS) + ADAM_WD * w_ref[...])

    spec = pl.BlockSpec((tr, wd), lambda i: (i, 0))
    return _hosted_call(body, name, (r // tr,), [spec] * (n_p + 3), [spec] * 4, [_sds((r, wd), F32)] * 4,
                        (*parts, w, m, v), exch=exch, swap=swap)


def _all_peers(x, y, c):
    flips = [(dx, dy, dc) for dx in (0, 1) for dy in (0, 1) for dc in (0, 1)][1:]
    return [(1 - x if dx else x, 1 - y if dy else y, 1 - c if dc else c) for dx, dy, dc in flips]


def _first_exchange(shards, later, cc, w, b):
    n_w, n_l = len(shards), len(later)
    n = w.shape[1]

    def body(*refs):
        src32, later_in, (cc_ref, w_ref, b_ref) = refs[:n_w], refs[n_w:n_w + n_l], refs[n_w + n_l:n_w + n_l + 3]
        o0 = n_w + n_l + 3
        outs, (all_ref, tab_ref), later_out = refs[o0:o0 + n_w], refs[o0 + n_w:o0 + n_w + 2], refs[o0 + n_w + 2:o0 + n_w + 2 + n_l]
        s0 = o0 + n_w + 2 + n_l
        srcs = refs[s0:s0 + n_w]
        (part_ref, ici_send, ici_recv, d2d_send, d2d_recv, local_sems, cc_send, cc_recv, tab_send,
         tab_recv) = refs[s0 + n_w:]
        for wi in range(n_w):
            srcs[wi][...] = src32[wi][...].astype(BF16)
        x, y, c = lax.axis_index("x"), lax.axis_index("y"), lax.axis_index("c")
        chip, dev = 2 * x + y, 4 * x + 2 * y + c
        chips = _other_chips(x, y)
        peers = _all_peers(x, y, c)

        def half(wi, which):
            hr = shards[wi].shape[0] // 2
            return pl.ds(pl.multiple_of(which * hr, 16), hr)

        def over_ici(wi, k, arriving):
            px, py = chips[k]
            slot = 2 * px + py if arriving else chip
            return pltpu.make_async_remote_copy(
                src_ref=srcs[wi].at[half(wi, c)], dst_ref=outs[wi].at[slot, half(wi, c)],
                send_sem=ici_send.at[3 * wi + k], recv_sem=ici_recv.at[3 * wi + k], device_id=(px, py, c),
                device_id_type=pl.DeviceIdType.MESH)

        def to_sibling(wi, k, arriving):
            px, py = chips[k]
            rows = half(wi, 1 - c if arriving else c)
            return pltpu.make_async_remote_copy(
                src_ref=outs[wi].at[2 * px + py, rows], dst_ref=outs[wi].at[2 * px + py, rows],
                send_sem=d2d_send.at[3 * wi + k], recv_sem=d2d_recv.at[3 * wi + k], device_id=(x, y, 1 - c),
                device_id_type=pl.DeviceIdType.MESH)

        def cc_copy(k, peer, slot):
            return pltpu.make_async_remote_copy(
                src_ref=cc_ref, dst_ref=all_ref.at[slot], send_sem=cc_send.at[k], recv_sem=cc_recv.at[k],
                device_id=peer, device_id_type=pl.DeviceIdType.MESH)

        def rows_of(px, py):
            return part_ref.at[pl.ds(pl.multiple_of((4 * px + 2 * py + c) * MOD_ROWS, MOD_ROWS), MOD_ROWS)]

        def tab_copy(k, px, py, slot):
            return pltpu.make_async_remote_copy(
                src_ref=rows_of(px, py), dst_ref=tab_ref.at[slot], send_sem=tab_send.at[k], recv_sem=tab_recv.at[k],
                device_id=(px, py, c), device_id_type=pl.DeviceIdType.MESH)

        local = [pltpu.make_async_copy(srcs[wi], outs[wi].at[chip], local_sems.at[wi]) for wi in range(n_w)]
        for cp in local:
            cp.start()
        pairs = [(wi, k) for wi in range(n_w) for k in range(3)]
        for wi, k in pairs:
            over_ici(wi, k, False).start()
        for k, peer in enumerate(peers):
            cc_copy(k, peer, dev).start()
        all_ref[dev] = cc_ref[...]
        for k, (px, py, pc) in enumerate(peers):
            cc_copy(k, (px, py, pc), 4 * px + 2 * py + pc).wait_recv()
        cv = all_ref[...].reshape(8 * MOD_ROWS, cc.shape[1])
        part_ref[...] = _dot((cv * _sigmoid(cv)).astype(BF16), w_ref[...]) + b_ref[...]
        for k, (px, py) in enumerate(chips):
            tab_copy(k, px, py, chip).start()
        tab_ref[chip] = rows_of(x, y)[...]
        for k, (px, py) in enumerate(chips):
            tab_copy(k, px, py, 2 * px + py).wait_recv()
        for j in range(n_l):
            later_out[j][...] = later_in[j][...].astype(BF16)
        for wi, k in pairs:
            over_ici(wi, k, True).wait_recv()
            to_sibling(wi, k, False).start()
        for wi, k in pairs:
            to_sibling(wi, k, True).wait_recv()
        for wi, k in pairs:
            over_ici(wi, k, False).wait_send()
            to_sibling(wi, k, False).wait_send()
        for k, peer in enumerate(peers):
            cc_copy(k, peer, dev).wait_send()
        for k, (px, py) in enumerate(chips):
            tab_copy(k, px, py, chip).wait_send()
        for cp in local:
            cp.wait()

    any_spec = pl.BlockSpec(memory_space=pl.ANY)
    vmem = pl.BlockSpec(memory_space=pltpu.VMEM)
    sems3 = pltpu.SemaphoreType.DMA((3 * n_w,))
    got = pl.pallas_call(
        body, name="first_exchange", in_specs=[vmem] * (n_w + n_l + 3),
        out_specs=[any_spec] * n_w + [vmem] * (2 + n_l),
        out_shape=[_sds((N_CHIPS,) + a.shape, BF16) for a in shards]
        + [_sds((8,) + cc.shape, F32), _sds((N_CHIPS, MOD_ROWS, n), F32)] + [_sds(a.shape, BF16) for a in later],
        scratch_shapes=[pltpu.VMEM(a.shape, BF16) for a in shards]
        + [pltpu.VMEM((8 * MOD_ROWS, n), F32), sems3, sems3, sems3, sems3, pltpu.SemaphoreType.DMA((n_w,)),
           pltpu.SemaphoreType.DMA((7,)), pltpu.SemaphoreType.DMA((7,)), pltpu.SemaphoreType.DMA((3,)),
           pltpu.SemaphoreType.DMA((3,))],
        compiler_params=pltpu.CompilerParams(vmem_limit_bytes=V7X_VMEM_LIMIT),
    )(*shards, *later, cc, w, b)
    return got[:n_w], got[n_w], got[n_w + 1], got[n_w + 2:]


def _ada_bwd_tp(cc_all, dmods, w, ctx_row):
    d, n = w.shape

    def body(cc_ref, m0, m1, m2, m3, w_ref, dw_ref, db_ref, dctx_ref, stage_ref, all_ref, send_sems, recv_sems):
        x, y, c = lax.axis_index("x"), lax.axis_index("y"), lax.axis_index("c")
        me = 4 * x + 2 * y + c
        dsum = m0[...] + m1[...] + m2[...] + m3[...]
        db_ref[...] = jnp.sum(dsum, axis=0, keepdims=True)
        for j in range(N_CHIPS):
            stage_ref[j] = dsum[:, j * n:(j + 1) * n]

        def copy(k, peer, slot):
            px, py, _ = peer
            return pltpu.make_async_remote_copy(
                src_ref=stage_ref.at[2 * px + py], dst_ref=all_ref.at[slot], send_sem=send_sems.at[k],
                recv_sem=recv_sems.at[k], device_id=peer, device_id_type=pl.DeviceIdType.MESH)

        peers = _all_peers(x, y, c)
        for k, peer in enumerate(peers):
            copy(k, peer, me).start()
        all_ref[me] = stage_ref[2 * x + y]
        for k, (px, py, pc) in enumerate(peers):
            copy(k, (px, py, pc), 4 * px + 2 * py + pc).wait_recv()
        for k, peer in enumerate(peers):
            copy(k, peer, me).wait_send()
        cv = cc_ref[...]
        sig = _sigmoid(cv)
        dmb = all_ref[...].reshape(8 * MOD_ROWS, n).astype(BF16)
        dw_ref[...] = _dot_tn((cv * sig).astype(BF16), dmb)
        dsc = _dot_nt(dmb, w_ref[...])
        dctx = dsc[ctx_row:ctx_row + 1, :]
        for dev in range(1, 8):
            dctx = dctx + dsc[dev * MOD_ROWS + ctx_row:dev * MOD_ROWS + ctx_row + 1, :]
        cx = cv[ctx_row:ctx_row + 1, :]
        sx = sig[ctx_row:ctx_row + 1, :]
        dctx_ref[...] = dctx * (sx * (1.0 + cx * (1.0 - sx))) * jnp.where(c == 0, 1.0, 0.0)

    vmem = pl.BlockSpec(memory_space=pltpu.VMEM)
    return pl.pallas_call(
        body, name="ada_bwd_tp", in_specs=[vmem] * 6, out_specs=[vmem] * 3,
        out_shape=[_sds((d, n), F32), _sds((1, N_MOD * d), F32), _sds((1, d), F32)],
        scratch_shapes=[pltpu.VMEM((N_CHIPS, MOD_ROWS, n), F32), pltpu.VMEM((8, MOD_ROWS, n), F32),
                        pltpu.SemaphoreType.DMA((7,)), pltpu.SemaphoreType.DMA((7,))],
        compiler_params=pltpu.CompilerParams(vmem_limit_bytes=V7X_VMEM_LIMIT),
    )(cc_all, *dmods, w)


def _rope_tables(s, ctx):
    pos = np.arange(s, dtype=np.float32)
    inv = (np.float32(ROPE_BASE) ** (-np.arange(0, QK_ROPE // 2, 2, dtype=np.float32) / np.float32(QK_ROPE // 2)))
    ang_r = np.floor(pos / GRID_W)[:, None] * inv
    ang_c = (pos - GRID_W * np.floor(pos / GRID_W))[:, None] * inv
    ang = np.concatenate([ang_r, ang_r, ang_c, ang_c], axis=-1).astype(np.float32)
    cos, sin = np.cos(ang), np.sin(ang)
    half_b = (np.arange(QK_ROPE) // 8) % 2 == 1
    sin_a = np.where(half_b, sin, 0.0)
    sin_b = np.where(half_b, 0.0, -sin)

    def place(tab, fill):
        full = np.full((s + ctx, HEAD_PAD), fill, np.float32)
        full[:s, QK_NOPE:QK_HEAD] = tab
        return jnp.asarray(full)

    return place(cos, 1.0), place(sin_a, 0.0), place(sin_b, 0.0)


def _pad_last(a, n):
    return jnp.pad(a, [(0, 0)] * (a.ndim - 1) + [(0, n - a.shape[-1])])


def _flat_rows(parts, rows, width):
    flat = jnp.concatenate([p.reshape(-1) for p in parts])
    return jnp.pad(flat, (0, rows * width - flat.shape[0])).reshape(rows, width)


def kernel(x, c, ctx, c_ctx, w_ada, b_ada, norm1_w, ffn1_w1, ffn1_w3, ffn1_w2, norm2_w, w_in, q_a_norm_w, w_uq, kv_a_norm_w, w_ukv, q_norm_w, k_norm_w, v_norm_w, w_s, b_s, w_out, norm3_w, ffn2_w1, ffn2_w3, ffn2_w2, loss_target, m_c_ctx, m_w_ada, m_b_ada, m_norm1_w, m_ffn1_w1, m_ffn1_w3, m_ffn1_w2, m_norm2_w, m_w_in, m_q_a_norm_w, m_w_uq, m_kv_a_norm_w, m_w_ukv, m_q_norm_w, m_k_norm_w, m_v_norm_w, m_w_s, m_b_s, m_w_out, m_norm3_w, m_ffn2_w1, m_ffn2_w3, m_ffn2_w2, v_c_ctx, v_w_ada, v_b_ada, v_norm1_w, v_ffn1_w1, v_ffn1_w3, v_ffn1_w2, v_norm2_w, v_w_in, v_q_a_norm_w, v_w_uq, v_kv_a_norm_w, v_w_ukv, v_q_norm_w, v_k_norm_w, v_v_norm_w, v_w_s, v_b_s, v_w_out, v_norm3_w, v_ffn2_w1, v_ffn2_w3, v_ffn2_w2):
    wts = dict(c_ctx=c_ctx, w_ada=w_ada, b_ada=b_ada, norm1_w=norm1_w, ffn1_w1=ffn1_w1, ffn1_w3=ffn1_w3, ffn1_w2=ffn1_w2,
               norm2_w=norm2_w, w_in=w_in, q_a_norm_w=q_a_norm_w, w_uq=w_uq, kv_a_norm_w=kv_a_norm_w, w_ukv=w_ukv,
               q_norm_w=q_norm_w, k_norm_w=k_norm_w, v_norm_w=v_norm_w, w_s=w_s, b_s=b_s, w_out=w_out, norm3_w=norm3_w,
               ffn2_w1=ffn2_w1, ffn2_w3=ffn2_w3, ffn2_w2=ffn2_w2)
    moms = dict(c_ctx=m_c_ctx, w_ada=m_w_ada, b_ada=m_b_ada, norm1_w=m_norm1_w, ffn1_w1=m_ffn1_w1, ffn1_w3=m_ffn1_w3,
                ffn1_w2=m_ffn1_w2, norm2_w=m_norm2_w, w_in=m_w_in, q_a_norm_w=m_q_a_norm_w, w_uq=m_w_uq,
                kv_a_norm_w=m_kv_a_norm_w, w_ukv=m_w_ukv, q_norm_w=m_q_norm_w, k_norm_w=m_k_norm_w, v_norm_w=m_v_norm_w,
                w_s=m_w_s, b_s=m_b_s, w_out=m_w_out, norm3_w=m_norm3_w, ffn2_w1=m_ffn2_w1, ffn2_w3=m_ffn2_w3,
                ffn2_w2=m_ffn2_w2)
    vars_ = dict(c_ctx=v_c_ctx, w_ada=v_w_ada, b_ada=v_b_ada, norm1_w=v_norm1_w, ffn1_w1=v_ffn1_w1, ffn1_w3=v_ffn1_w3,
                 ffn1_w2=v_ffn1_w2, norm2_w=v_norm2_w, w_in=v_w_in, q_a_norm_w=v_q_a_norm_w, w_uq=v_w_uq,
                 kv_a_norm_w=v_kv_a_norm_w, w_ukv=v_w_ukv, q_norm_w=v_q_norm_w, k_norm_w=v_k_norm_w, v_norm_w=v_v_norm_w,
                 w_s=v_w_s, b_s=v_b_s, w_out=v_w_out, norm3_w=v_norm3_w, ffn2_w1=v_ffn2_w1, ffn2_w3=v_ffn2_w3,
                 ffn2_w2=v_ffn2_w2)

    nb, s, d = x.shape
    nctx = ctx.shape[1]
    t, tc = nb * s, nb * nctx
    t_all = t + tc
    sk = s + nctx
    assert nb + 1 <= MOD_ROWS and d % LANES == 0
    tm = _token_tile(s, nctx)
    tq = _div_tile(s, 512, tm)
    tmx = _div_tile(math.gcd(s, tc), 512, tm)

    def held(n, a_):
        return jnp.swapaxes(a_[0], 0, 1) if n in T_WEIGHTS else a_[0]

    def unheld(n, a_):
        return (jnp.swapaxes(a_, 0, 1) if n in T_WEIGHTS else a_)[None]

    shard = {"w_ada": w_ada[0].astype(BF16)}
    full = {}

    def unshard(names, blocks):
        for n, g4 in zip(names, blocks):
            _, r_, c_ = g4.shape
            if n in ROW_SHARDED or n in T_WEIGHTS:
                full[n] = g4.reshape(N_CHIPS * r_, c_)
            else:
                full[n] = g4.transpose(1, 0, 2).reshape(r_, N_CHIPS * c_)

    def chip_major(n, g_):
        if n in ROW_SHARDED or n in T_WEIGHTS:
            return g_.reshape(N_CHIPS, g_.shape[0] // N_CHIPS, g_.shape[1]).astype(BF16)
        r_, cols = g_.shape
        return g_.reshape(r_, N_CHIPS, cols // N_CHIPS).transpose(1, 0, 2).astype(BF16)

    cc = jnp.concatenate([c, c_ctx[None, :], jnp.zeros((MOD_ROWS - nb - 1, d), F32)], axis=0)
    n_ada = shard["w_ada"].shape[1]
    assert n_ada % LANES == 0
    my_chip = 2 * lax.axis_index("x") + lax.axis_index("y")
    b_cols = lax.dynamic_slice_in_dim(b_ada, my_chip * n_ada, n_ada, axis=1)
    later = ("ffn1_w2",) + MIX_WEIGHTS + LAST_WEIGHTS + ("w_out",)
    got, cc_all, table, cast = _first_exchange([held(n, wts[n]) for n in FIRST_WEIGHTS],
                                               [held(n, wts[n]) for n in later], cc, shard["w_ada"], b_cols)
    unshard(FIRST_WEIGHTS, got)
    shard.update(zip(later, cast))
    cc_all = cc_all.reshape(8 * MOD_ROWS, d)
    mod = table.transpose(1, 0, 2).reshape(MOD_ROWS, N_MOD, d)
    wsb = w_s[0].astype(BF16)
    wcat = wsb.transpose(1, 0, 2).reshape(CHUNK, GROUPS * CHUNK)
    wcat_t = wsb.transpose(2, 0, 1).reshape(CHUNK, GROUPS * CHUNK)
    bias = jnp.repeat(b_s[0].T, GROUP_DIM, axis=1)
    vnw = v_norm_w.reshape(1, GMLP_W)
    lane = jnp.arange(GMLP_W)
    ones = (lane[:, None] // GROUP_DIM == lane[None, :] // GROUP_DIM).astype(BF16)
    qnw = _pad_last(q_norm_w, HEAD_PAD)
    knw = _pad_last(k_norm_w, HEAD_PAD)
    tabs = _rope_tables(s, nctx)

    x_lat, x_ctx = x.reshape(t, d), ctx.reshape(tc, d)
    (a1, b1), got = _ffn_pre(x_lat, x_ctx, mod, norm1_w, full["ffn1_w1"], full["ffn1_w3"], 0, s, nb, tm, "ffn1_pre",
                             exch=("gather", [shard["ffn1_w2"]]))
    unshard(("ffn1_w2",), got)
    (xs1, y1), got = _ffn_post(x_lat, x_ctx, a1, b1, mod, full["ffn1_w2"], 0, s, nb, tm, "ffn1_post",
                               exch=("gather", [shard[n] for n in MIX_WEIGHTS]))
    unshard(MIX_WEIGHTS, got)
    wi = full["w_in"]
    wp = jnp.concatenate([wi[0:KV_LORA], jnp.zeros((QK_NOPE, d), BF16), wi[KV_LORA:KV_LORA + QK_ROPE],
                          jnp.zeros((HEAD_PAD - QK_HEAD, d), BF16), wi[KV_LORA + QK_ROPE:]], axis=0)
    wq = jnp.pad(full["w_uq"].reshape(HEADS, QK_HEAD, Q_LORA), ((0, 0), (0, HEAD_PAD - QK_HEAD), (0, 0)))
    wkv = full["w_ukv"].reshape(KV_LORA, HEADS, QK_NOPE + V_HEAD)
    wk = _pad_last(wkv[:, :, :QK_NOPE].transpose(1, 0, 2), HEAD_PAD)
    wv = wkv[:, :, QK_NOPE:].reshape(KV_LORA, HEADS // 2, 2 * V_HEAD).transpose(1, 0, 2)
    h2, proj = _mixin_fwd(xs1, mod, norm2_w, wp, s, nb, tmx)
    prep_w = (wq, wk, wv, kv_a_norm_w, q_a_norm_w, qnw, knw)
    q, k_all, v_all = _prep_fwd(proj, 0, nb, s, 0, sk, 0, None, tabs, *prep_w, tq, True, "prep_fwd")
    k_all, v_all = _prep_fwd(proj, t // tm, nb, nctx, s // tm, sk, s // tm, (k_all, v_all), tabs, *prep_w, tm, False,
                             "prep_ctx_fwd")
    o, lse, got = _attn_fwd(q, k_all, v_all, tq, exch=("gather", [shard[n] for n in LAST_WEIGHTS + ("w_out",)]))
    unshard(LAST_WEIGHTS + ("w_out",), got)
    mixcat = _gmlp_fwd(proj, o, wcat, bias, vnw, ones, tq)
    x2, mix = _mixout_fwd(mixcat, xs1, mod, full["w_out"], s, tq)
    (dy, a2, b2, y2, loss_part), _ = _ffn_fwd(x2, None, mod, norm3_w, full["ffn2_w1"], full["ffn2_w3"], full["ffn2_w2"], 6,
                                              s, nb, tm, "ffn2_fwd", target=loss_target.reshape(t, d))

    grads, cm, recv = {}, {}, {}

    def scatter_of(names):
        return ("scatter", [cm[n] for n in names])

    (dx2, h3, g2, da2, db2, dyb2, dmod_c, grads["norm3_w"]), _ = _ffn_bwd(
        dy, x2, None, a2, b2, y2, mod, norm3_w, full["ffn2_w1"], full["ffn2_w3"], full["ffn2_w2"], 6, s, nb, tm,
        "ffn2_bwd")
    cm["ffn2_w1"] = chip_major("ffn2_w1", _mm_tn(da2, h3, t, "ffn2_dw1"))
    cm["ffn2_w3"] = chip_major("ffn2_w3", _mm_tn(db2, h3, t, "ffn2_dw3"))
    cm["ffn2_w2"] = chip_major("ffn2_w2", _mm_tn(g2, dyb2, t, "ffn2_dw2"))
    dmix, do, dsg, dmod_b = _mixout_bwd(dx2, mix, mod, full["w_out"], s, tq)
    cm["w_out"] = chip_major("w_out", _mm_tn(mixcat, dmix, t, "wout_dw"))
    duv, dws, dbs, dvnw = _gmlp_bwd(proj, dsg, wcat, wcat_t, bias, vnw, ones, tq)
    group = LAST_WEIGHTS + ("w_out",)
    (dq, dk, dv), got = _attn_bwd(q, k_all, v_all, do, mixcat, lse, tq, exch=scatter_of(group))
    recv.update(zip(group, got))
    dp0, dwk_c, dwv_c, dkvaw_c, dknw_c = _prep_bwd(
        proj, t // tm, nb, nctx, s // tm, s // tm, t_all, None, tabs, *prep_w, None, dk, dv, None, tm, "prep_ctx_bwd")
    dp0, dwq, dqaw, dqnw, dwk, dwv, dkvaw, dknw = _prep_bwd(
        proj, 0, nb, s, 0, 0, t_all, dp0, tabs, *prep_w, dq, dk, dv, [dwk_c, dwv_c, dkvaw_c, dknw_c], tq, "prep_bwd")
    part, sib = {}, {}
    early = LAST_WEIGHTS + ("w_out",)
    for n in early:
        part[n] = _sum_slots(recv[n], "sum_" + n)
    (dxs1, dmod_a, grads["norm2_w"]), _, got = _mixin_bwd(dp0, duv, xs1, dx2, mod, norm2_w, wp, s, nb, tmx,
                                                          [part[n] for n in early])
    sib.update(zip(early, got))
    dwp = jnp.concatenate([_mm_tn(dp0, h2, t_all, "win_dw_kvq"), _mm_tn(duv, h2, t, "win_dw_uv")], axis=0)
    cm["w_in"] = chip_major("w_in", jnp.concatenate(
        [dwp[0:KV_LORA], dwp[KV_LORA + QK_NOPE:KV_LORA + QK_HEAD], dwp[256:]], axis=0))
    cm["w_uq"] = chip_major("w_uq", dwq[:, :, :QK_HEAD].transpose(0, 2, 1).reshape(HEADS * QK_HEAD, Q_LORA))
    cm["w_ukv"] = chip_major("w_ukv", jnp.concatenate(
        [dwk[:, :, :QK_NOPE].transpose(1, 0, 2),
         dwv.transpose(1, 0, 2).reshape(KV_LORA, HEADS, V_HEAD)], axis=2).reshape(KV_LORA, HEADS * (QK_NOPE + V_HEAD)))
    (dx_lat, h1, g1, da1, db1, dyb1, dmod_0, grads["norm1_w"]), _ = _ffn_bwd(
        dxs1, x_lat, x_ctx, a1, b1, y1, mod, norm1_w, full["ffn1_w1"], full["ffn1_w3"], full["ffn1_w2"], 0, s, nb, tm,
        "ffn1_bwd")
    dmods = [m_.reshape(MOD_ROWS, N_MOD * d) for m_ in (dmod_0, dmod_a, dmod_b, dmod_c)]
    dw_ada, grads["b_ada"], dctx = _ada_bwd_tp(cc_all, dmods, shard["w_ada"], nb)
    grads["c_ctx"] = dctx[0]
    grads["q_a_norm_w"], grads["kv_a_norm_w"] = dqaw, dkvaw
    grads["q_norm_w"], grads["k_norm_w"] = dqnw[:, :QK_HEAD], dknw[:, :QK_HEAD]
    grads["v_norm_w"], grads["w_s"], grads["b_s"] = dvnw, dws, dbs[:, 0]
    grad_x = dx_lat.reshape(nb, s, d)
    n_small = sum(wts[n].size for n in SMALL)
    rows_s = _round_up(-(-(n_small + 1) // d), 16)
    cm["small"] = jnp.broadcast_to(_flat_rows([grads[n] for n in SMALL] + [loss_part], rows_s, d), (N_CHIPS, rows_s, d))
    group = ("w_in", "w_uq", "w_ukv", "small")
    dw2, got = _mm_tn(g1, dyb1, t_all, "ffn1_dw2", exch=scatter_of(group))
    recv.update(zip(group, got))
    cm["ffn1_w2"] = chip_major("ffn1_w2", dw2)
    dw1, got = _mm_tn(da1, h1, t_all, "ffn1_dw1", exch=scatter_of(("ffn1_w2",)))
    recv["ffn1_w2"] = got[0]
    cm["ffn1_w1"] = chip_major("ffn1_w1", dw1)
    dw3, got = _mm_tn(db1, h1, t_all, "ffn1_dw3", exch=scatter_of(("ffn1_w1",)))
    recv["ffn1_w1"] = got[0]
    cm["ffn1_w3"] = chip_major("ffn1_w3", dw3)
    stepped = {}
    reduced = tuple(n for n in SHARDED if n != "w_ada") + ("small",)
    late = tuple(n for n in reduced if n not in early and n != "ffn1_w3")
    for n in late:
        part[n] = _sum_slots(recv[n], "sum_" + n)
    stepped["w_ada"], got, got_sib = _adamw([dw_ada], wts["w_ada"][0], moms["w_ada"][0], vars_["w_ada"][0],
                                            "adamw_w_ada", exch=scatter_of(("ffn1_w3",)), swap=[part[n] for n in late])
    sib.update(zip(late, got_sib))
    part["ffn1_w3"] = _sum_slots(got[0], "sum_ffn1_w3")
    sib["ffn1_w3"] = _swap_cores([part["ffn1_w3"]], "swap_last")[0]
    for n in reduced[:-1]:
        stepped[n], _ = _adamw([part[n], sib[n]], held(n, wts[n]), held(n, moms[n]), held(n, vars_[n]), "adamw_" + n)
    for n in SHARDED:
        stepped[n] = [unheld(n, a_) for a_ in stepped[n]]
    packed, _ = _adamw([part["small"], sib["small"]], _flat_rows([wts[n] for n in SMALL], rows_s, d),
                       _flat_rows([moms[n] for n in SMALL], rows_s, d), _flat_rows([vars_[n] for n in SMALL], rows_s, d),
                       "adamw_small")
    loss = packed[0].reshape(-1)[n_small]
    for n in SMALL:
        stepped[n] = []
    for a_ in packed:
        flat = a_.reshape(-1)
        off = 0
        for n in SMALL:
            stepped[n].append(flat[off:off + wts[n].size].reshape(wts[n].shape))
            off += wts[n].size
    return (loss, grad_x, *[stepped[n][0] for n in WEIGHTS], *[stepped[n][1] for n in WEIGHTS],
            *[stepped[n][2] for n in WEIGHTS], *[stepped[n][3] for n in WEIGHTS])
```

```python
import functools
import math

import jax
import jax.numpy as jnp
import numpy as np
from jax import lax
from jax.experimental import pallas as pl
from jax.experimental.pallas import tpu as pltpu

F32 = jnp.float32
BF16 = jnp.bfloat16

EPS = 1e-6
N_MOD = 9
HEADS = 8
QK_NOPE, QK_ROPE, V_HEAD = 64, 32, 64
QK_HEAD = QK_NOPE + QK_ROPE
HEAD_PAD = 128
LN2 = math.log(2.0)
SOFTMAX_SCALE = QK_HEAD ** -0.5 / LN2
Q_LORA, KV_LORA = 256, 128
GROUPS, GROUP_DIM, CHUNK = 8, 64, 128
GMLP_W = GROUPS * GROUP_DIM
MLA_W = HEADS * V_HEAD
IN_COLS = 1440
PROJ_COLS = 1536
GRID_W = 64
ROPE_BASE = 10000.0
MOD_ROWS = 16
ADAM_LR, ADAM_B1, ADAM_B2, ADAM_EPS, ADAM_WD, ADAM_STEP = 0.001, 0.9, 0.999, 1e-08, 0.01, 10
N_CHIPS = 4
LANES = 128
V7X_VMEM_LIMIT = 56 * 1024 * 1024
GELU_C = math.sqrt(2.0 / math.pi)

SHARDED = ("w_ada", "ffn1_w1", "ffn1_w3", "ffn1_w2", "w_in", "w_uq", "w_ukv", "w_out", "ffn2_w1", "ffn2_w3", "ffn2_w2")
ROW_SHARDED = ("ffn1_w2", "w_out", "ffn2_w2")
T_WEIGHTS = ("ffn1_w1", "ffn1_w3", "ffn2_w1", "ffn2_w3", "w_in", "w_uq")
FIRST_WEIGHTS = ("ffn1_w1", "ffn1_w3", "ffn1_w2")
MIX_WEIGHTS = ("w_in", "w_uq", "w_ukv", "w_out")
LAST_WEIGHTS = ("ffn2_w1", "ffn2_w3", "ffn2_w2")
SMALL = ("c_ctx", "b_ada", "norm1_w", "norm2_w", "q_a_norm_w", "kv_a_norm_w", "q_norm_w", "k_norm_w", "v_norm_w",
         "w_s", "b_s", "norm3_w")
WEIGHTS = ("c_ctx", "w_ada", "b_ada", "norm1_w", "ffn1_w1", "ffn1_w3", "ffn1_w2", "norm2_w", "w_in", "q_a_norm_w",
           "w_uq", "kv_a_norm_w", "w_ukv", "q_norm_w", "k_norm_w", "v_norm_w", "w_s", "b_s", "w_out", "norm3_w",
           "ffn2_w1", "ffn2_w3", "ffn2_w2")


def _round_up(n, m):
    return (n + m - 1) // m * m


def _div_tile(n, target, mult):
    best = None
    for t in range(mult, min(n, target) + 1, mult):
        if n % t == 0:
            best = t
    return n if best is None else best


def _dot(a, b):
    return lax.dot_general(a, b, (((1,), (0,)), ((), ())), preferred_element_type=F32)


def _dot_nt(a, b):
    return lax.dot_general(a, b, (((1,), (1,)), ((), ())), preferred_element_type=F32)


def _dot_tn(a, b):
    return lax.dot_general(a, b, (((0,), (0,)), ((), ())), preferred_element_type=F32)


def _sigmoid(x):
    return 1.0 / (1.0 + jnp.exp(-x))


def _gelu(x):
    return 0.5 * x * (1.0 + jnp.tanh(GELU_C * (x + 0.044715 * x * x * x)))


def _gelu_grad(x):
    t = jnp.tanh(GELU_C * (x + 0.044715 * x * x * x))
    return 0.5 * (1.0 + t) + 0.5 * x * (1.0 - t * t) * (GELU_C * (1.0 + 3 * 0.044715 * x * x))


def _rope3(x, cos, sin_a, sin_b):
    return x * cos + pltpu.roll(x, 8, 2) * sin_a + pltpu.roll(x, HEAD_PAD - 8, 2) * sin_b


def _rope3_t(d, cos, sin_a, sin_b):
    return d * cos + pltpu.roll(d * sin_a, HEAD_PAD - 8, 2) + pltpu.roll(d * sin_b, 8, 2)


def _group_sum(x, ones_ref):
    hi = x.astype(BF16)
    lo = (x - hi.astype(F32)).astype(BF16)
    return _dot(hi, ones_ref[...]) + _dot(lo, ones_ref[...])


def _params(n_axes):
    return pltpu.CompilerParams(dimension_semantics=("arbitrary",) * n_axes, vmem_limit_bytes=V7X_VMEM_LIMIT)


def _whole(shape):
    nd = len(shape)
    return pl.BlockSpec(shape, lambda *_: (0,) * nd, pipeline_mode=pl.Buffered(1))


def _sds(shape, dtype):
    return jax.ShapeDtypeStruct(shape, dtype)


def _token_tile(s, ctx):
    return _div_tile(math.gcd(s, ctx), 256, CHUNK)


def _other_chips(x, y):
    return [(1 - x, y), (x, 1 - y), (1 - x, 1 - y)]


def _exch_copies(kind, srcs, dsts, send_sems, recv_sems, local_sems, with_arrivals):
    x, y, c = lax.axis_index("x"), lax.axis_index("y"), lax.axis_index("c")
    me = 2 * x + y
    local, sends, arrivals = [], [], []
    for w, (src, dst) in enumerate(zip(srcs, dsts)):
        own = src if kind == "gather" else src.at[me]
        local.append(pltpu.make_async_copy(own, dst.at[me], local_sems.at[w]))
        for k, (px, py) in enumerate(_other_chips(x, y)):
            sem = dict(send_sem=send_sems.at[3 * w + k], recv_sem=recv_sems.at[3 * w + k], device_id=(px, py, c),
                       device_id_type=pl.DeviceIdType.MESH)
            out = src if kind == "gather" else src.at[2 * px + py]
            sends.append(pltpu.make_async_remote_copy(src_ref=out, dst_ref=dst.at[me], **sem))
            if with_arrivals:
                arrivals.append(pltpu.make_async_remote_copy(src_ref=own, dst_ref=dst.at[2 * px + py], **sem))
    return local, sends, arrivals


def _exch_start(kind, srcs, dsts, sems):
    local, sends, _ = _exch_copies(kind, srcs, dsts, *sems, with_arrivals=False)
    for cp in local + sends:
        cp.start()


def _exch_wait(kind, srcs, dsts, sems):
    local, sends, arrivals = _exch_copies(kind, srcs, dsts, *sems, with_arrivals=True)
    for cp in arrivals:
        cp.wait_recv()
    for cp in sends:
        cp.wait_send()
    for cp in local:
        cp.wait()


def _exch_scratch(n):
    return [pltpu.SemaphoreType.DMA((3 * n,)), pltpu.SemaphoreType.DMA((3 * n,)), pltpu.SemaphoreType.DMA((n,))]


def _exch_shapes(kind, arrays):
    return [_sds((N_CHIPS,) + a.shape if kind == "gather" else a.shape, a.dtype) for a in arrays]


def _sibling_copies(srcs, dsts, send_sems, recv_sems):
    x, y, c = lax.axis_index("x"), lax.axis_index("y"), lax.axis_index("c")
    return [pltpu.make_async_remote_copy(
        src_ref=src, dst_ref=dst, send_sem=send_sems.at[w], recv_sem=recv_sems.at[w], device_id=(x, y, 1 - c),
        device_id_type=pl.DeviceIdType.MESH) for w, (src, dst) in enumerate(zip(srcs, dsts))]


def _hosted_call(body, name, grid, in_specs, out_specs, out_shape, operands, scratch=(), exch=None, swap=None):
    n_axes = len(grid)
    if exch is None and swap is None:
        outs = pl.pallas_call(body, name=name, grid=grid, in_specs=list(in_specs), out_specs=list(out_specs),
                              out_shape=list(out_shape), scratch_shapes=list(scratch),
                              compiler_params=_params(n_axes))(*operands)
        return list(outs), []
    kind, arrays = exch if exch is not None else ("scatter", [])
    swaps = list(swap or [])
    n_in, n_out, n_sc, n_ex, n_sw = len(in_specs), len(out_specs), len(scratch), len(arrays), len(swaps)

    def hosted(*refs):
        cin, ein, sin = refs[:n_in], refs[n_in:n_in + n_ex], refs[n_in + n_ex:n_in + n_ex + n_sw]
        o0 = n_in + n_ex + n_sw
        cout, eout, sout = refs[o0:o0 + n_out], refs[o0 + n_out:o0 + n_out + n_ex], refs[o0 + n_out + n_ex:o0 + n_out + n_ex + n_sw]
        rest = refs[o0 + n_out + n_ex + n_sw:]
        csc, sems, swap_sems = rest[:n_sc], rest[n_sc:n_sc + 3], rest[n_sc + 3:]
        first = functools.reduce(jnp.logical_and, [pl.program_id(a) == 0 for a in range(n_axes)])
        last = functools.reduce(jnp.logical_and, [pl.program_id(a) == grid[a] - 1 for a in range(n_axes)])

        @pl.when(first)
        def _():
            if n_ex:
                _exch_start(kind, ein, eout, sems)
            for cp in _sibling_copies(sin, sout, *swap_sems) if n_sw else []:
                cp.start()

        body(*cin, *cout, *csc)

        @pl.when(last)
        def _():
            if n_ex:
                _exch_wait(kind, ein, eout, sems)
            for cp in _sibling_copies(sin, sout, *swap_sems) if n_sw else []:
                cp.wait()

    any_spec = pl.BlockSpec(memory_space=pl.ANY)
    swap_scratch = [pltpu.SemaphoreType.DMA((n_sw,)), pltpu.SemaphoreType.DMA((n_sw,))] if n_sw else []
    outs = pl.pallas_call(
        hosted, name=name, grid=grid, in_specs=list(in_specs) + [any_spec] * (n_ex + n_sw),
        out_specs=list(out_specs) + [any_spec] * (n_ex + n_sw),
        out_shape=list(out_shape) + _exch_shapes(kind, arrays) + [_sds(a.shape, a.dtype) for a in swaps],
        scratch_shapes=list(scratch) + _exch_scratch(max(n_ex, 1)) + swap_scratch, compiler_params=_params(n_axes),
    )(*operands, *arrays, *swaps)
    got = list(outs[n_out:n_out + n_ex])
    return (list(outs[:n_out]), got) if swap is None else (list(outs[:n_out]), got, list(outs[n_out + n_ex:]))


class _TokenTiles:
    def __init__(self, t, tc, tm):
        self.n_lat, self.n_ctx = t // tm, tc // tm
        self.n_all = self.n_lat + self.n_ctx

    def tile(self, i):
        return (i + self.n_lat) % self.n_all if self.n_ctx else i

    def is_lat(self, i):
        return self.tile(i) < self.n_lat

    def row(self, i):
        return (self.tile(i), 0)

    def lat_row(self, i):
        return (jnp.where(self.is_lat(i), self.tile(i), 0), 0) if self.n_ctx else (i, 0)

    def ctx_row(self, i):
        return (jnp.where(self.is_lat(i), self.n_ctx - 1, self.tile(i) - self.n_lat), 0)


def _ffn_fwd(x_lat, x_ctx, mod, nw, w1, w3, w2, k0, s, nb, tm, name, target=None, exch=None):
    t, d = x_lat.shape
    tc = 0 if x_ctx is None else x_ctx.shape[0]
    f = w1.shape[0]
    tiles = _TokenTiles(t, tc, tm)
    n_x = 2 if tc else 1
    n_t = 0 if target is None else 1
    assert not (tc and n_t)

    def body(*refs):
        x_ref = refs[0]
        t_ref = refs[n_x] if n_t else None
        mod_ref, nw_ref, w1_ref, w3_ref, w2_ref, o_ref, a_ref, b_ref, y_ref = refs[n_x + n_t:n_x + n_t + 9]
        i = pl.program_id(0)
        g = jnp.minimum((tiles.tile(i) * tm) // s, nb)
        shift = mod_ref[g, pl.ds(k0, 1), :]
        scale = mod_ref[g, pl.ds(k0 + 1, 1), :]
        gate = mod_ref[g, pl.ds(k0 + 2, 1), :]
        x = jnp.where(tiles.is_lat(i), x_ref[...], refs[1][...]) if tc else x_ref[...]
        r = lax.rsqrt(jnp.mean(x * x, axis=-1, keepdims=True) + EPS)
        hb = ((x * r * nw_ref[...]) * (1.0 + scale) + shift).astype(BF16)
        a = _dot_nt(hb, w1_ref[...])
        b = _dot_nt(hb, w3_ref[...])
        gb = (a * _sigmoid(a) * b).astype(BF16)
        y = _dot(gb, w2_ref[...])
        out = x + (0.5 * gate) * y
        a_ref[...] = a.astype(BF16)
        b_ref[...] = b.astype(BF16)
        y_ref[...] = y.astype(BF16)
        if n_t:
            loss_ref, acc_ref = refs[-2:]

            @pl.when(i == 0)
            def _():
                acc_ref[...] = jnp.zeros_like(acc_ref)

            e = out - t_ref[...]
            o_ref[...] = e * (1.0 / d)
            acc_ref[...] += jnp.sum(e * e, axis=0, keepdims=True)

            @pl.when(i == tiles.n_all - 1)
            def _():
                loss_ref[...] = (0.5 / d) * jnp.sum(acc_ref[...], axis=-1, keepdims=True)
        else:
            o_ref[...] = out

    td = pl.BlockSpec((tm, d), tiles.row)
    tf = pl.BlockSpec((tm, f), tiles.row)
    return _hosted_call(
        body, name, (tiles.n_all,),
        [pl.BlockSpec((tm, d), tiles.lat_row)] + ([pl.BlockSpec((tm, d), tiles.ctx_row)] if tc else []) + [td] * n_t
        + [_whole(mod.shape), _whole(nw.shape), _whole(w1.shape), _whole(w3.shape), _whole(w2.shape)],
        [td, tf, tf, td] + [pl.BlockSpec((1, 1), lambda i: (0, 0))] * n_t,
        [_sds((t + tc, d), F32), _sds((t + tc, f), BF16), _sds((t + tc, f), BF16), _sds((t + tc, d), BF16)]
        + [_sds((1, 1), F32)] * n_t,
        (x_lat,) + ((x_ctx,) if tc else ()) + ((target,) if n_t else ()) + (mod, nw, w1, w3, w2),
        scratch=[pltpu.VMEM((1, d), F32)] * n_t, exch=exch)


def _ffn_bwd(dout, x_lat, x_ctx, a, b, y, mod, nw, w1, w3, w2, k0, s, nb, tm, name, exch=None):
    t, d = x_lat.shape
    tc = 0 if x_ctx is None else x_ctx.shape[0]
    f = w1.shape[0]
    nch = 2 if (f // 2) % LANES == 0 and f % 2 == 0 else 1
    fc = f // nch
    tiles = _TokenTiles(t, tc, tm)
    n_x = 2 if tc else 1

    def body(*refs):
        do_ref, x_ref = refs[0], refs[1]
        (a_ref, b_ref, y_ref, mod_ref, nw_ref, w1_ref, w3_ref, w2_ref,
         dx_ref, h_ref, g_ref, da_ref, db_ref, dy_ref, dmod_ref, dnw_ref) = refs[1 + n_x:]
        i = pl.program_id(0)

        @pl.when(i == 0)
        def _():
            dmod_ref[...] = jnp.zeros_like(dmod_ref)
            dnw_ref[...] = jnp.zeros_like(dnw_ref)

        g = jnp.minimum((tiles.tile(i) * tm) // s, nb)
        shift = mod_ref[g, pl.ds(k0, 1), :]
        scale = mod_ref[g, pl.ds(k0 + 1, 1), :]
        gate = mod_ref[g, pl.ds(k0 + 2, 1), :]
        x = jnp.where(tiles.is_lat(i), x_ref[...], refs[2][...]) if tc else x_ref[...]
        dout_v = do_ref[...]
        r = lax.rsqrt(jnp.mean(x * x, axis=-1, keepdims=True) + EPS)
        xh = x * r
        n = xh * nw_ref[...]
        h_ref[...] = (n * (1.0 + scale) + shift).astype(BF16)
        dyb = ((0.5 * gate) * dout_v).astype(BF16)
        dy_ref[...] = dyb
        dmod_ref[g, pl.ds(k0 + 2, 1), :] += 0.5 * jnp.sum(dout_v * y_ref[...].astype(F32), axis=0, keepdims=True)
        dh = jnp.zeros((tm, d), F32)
        for c in range(nch):
            sl = slice(c * fc, (c + 1) * fc)
            dg = _dot_nt(dyb, w2_ref[sl, :])
            av = a_ref[:, sl].astype(F32)
            bv = b_ref[:, sl].astype(F32)
            sig = _sigmoid(av)
            sa = av * sig
            g_ref[:, sl] = (sa * bv).astype(BF16)
            dab = (dg * bv * (sig * (1.0 + av * (1.0 - sig)))).astype(BF16)
            dbb = (dg * sa).astype(BF16)
            da_ref[:, sl] = dab
            db_ref[:, sl] = dbb
            dh = dh + _dot(dab, w1_ref[sl, :]) + _dot(dbb, w3_ref[sl, :])
        dmod_ref[g, pl.ds(k0, 1), :] += jnp.sum(dh, axis=0, keepdims=True)
        dmod_ref[g, pl.ds(k0 + 1, 1), :] += jnp.sum(dh * n, axis=0, keepdims=True)
        dn = dh * (1.0 + scale)
        dnw_ref[...] += jnp.sum(dn * xh, axis=0, keepdims=True)
        dxh = dn * nw_ref[...]
        dx_ref[...] = dout_v + r * (dxh - xh * jnp.mean(dxh * xh, axis=-1, keepdims=True))

    td = pl.BlockSpec((tm, d), tiles.row)
    tf = pl.BlockSpec((tm, f), tiles.row)
    lat = pl.BlockSpec((tm, d), tiles.lat_row)
    ta = t + tc
    return _hosted_call(
        body, name, (tiles.n_all,),
        [td, lat] + ([pl.BlockSpec((tm, d), tiles.ctx_row)] if tc else [])
        + [tf, tf, td, _whole(mod.shape), _whole(nw.shape), _whole(w1.shape), _whole(w3.shape), _whole(w2.shape)],
        [lat, td, tf, tf, tf, td, pl.BlockSpec(mod.shape, lambda i: (0, 0, 0)), pl.BlockSpec((1, d), lambda i: (0, 0))],
        [_sds((t, d), F32), _sds((ta, d), BF16), _sds((ta, f), BF16), _sds((ta, f), BF16), _sds((ta, f), BF16),
         _sds((ta, d), BF16), _sds(mod.shape, F32), _sds((1, d), F32)],
        (dout, x_lat) + ((x_ctx,) if tc else ()) + (a, b, y, mod, nw, w1, w3, w2), exch=exch)


def _mm_tn(a, b, rows, name, exch=None):
    m = a.shape[1]
    n = b.shape[1]
    bm = _div_tile(m, 1408, LANES)
    bn = _div_tile(n, 1408, LANES)
    bk = _div_tile(rows, 2304, LANES)
    nk = rows // bk

    def body(a_ref, b_ref, o_ref, acc_ref):
        k = pl.program_id(2)

        @pl.when(k == 0)
        def _():
            acc_ref[...] = jnp.zeros_like(acc_ref)

        acc_ref[...] += _dot_tn(a_ref[...], b_ref[...])

        @pl.when(k == nk - 1)
        def _():
            o_ref[...] = acc_ref[...].astype(BF16)

    (out,), got = _hosted_call(
        body, name, (m // bm, n // bn, nk),
        [pl.BlockSpec((bk, bm), lambda i, j, k: (k, i)), pl.BlockSpec((bk, bn), lambda i, j, k: (k, j))],
        [pl.BlockSpec((bm, bn), lambda i, j, k: (i, j))], [_sds((m, n), BF16)], (a, b),
        scratch=[pltpu.VMEM((bm, bn), F32)], exch=exch)
    return out if exch is None else (out, got)


def _mixin_fwd(xs, mod, nw, wp, s, nb, tm):
    t, d = xs.shape

    def body(x_ref, mod_ref, nw_ref, wp_ref, h_ref, p_ref):
        g = jnp.minimum((pl.program_id(0) * tm) // s, nb)
        shift = mod_ref[g, pl.ds(3, 1), :]
        scale = mod_ref[g, pl.ds(4, 1), :]
        x = x_ref[...]
        r = lax.rsqrt(jnp.mean(x * x, axis=-1, keepdims=True) + EPS)
        hb = ((x * r * nw_ref[...]) * (1.0 + scale) + shift).astype(BF16)
        h_ref[...] = hb
        p_ref[...] = _dot_nt(hb, wp_ref[...]).astype(BF16)

    row = lambda i: (i, 0)
    return pl.pallas_call(
        body, name="mixin_fwd", grid=(t // tm,),
        in_specs=[pl.BlockSpec((tm, d), row), _whole(mod.shape), _whole(nw.shape), _whole(wp.shape)],
        out_specs=[pl.BlockSpec((tm, d), row), pl.BlockSpec((tm, PROJ_COLS), row)],
        out_shape=[_sds((t, d), BF16), _sds((t, PROJ_COLS), BF16)], compiler_params=_params(1),
    )(xs, mod, nw, wp)


def _mixin_bwd(dp0, duv, xs, dres, mod, nw, wp, s, nb, tm, swap):
    t_all, d = xs.shape
    nlat = dres.shape[0] // tm

    def body(p0_ref, uv_ref, x_ref, dr_ref, mod_ref, nw_ref, wp_ref, dx_ref, dmod_ref, dnw_ref):
        i = pl.program_id(0)

        @pl.when(i == 0)
        def _():
            dmod_ref[...] = jnp.zeros_like(dmod_ref)
            dnw_ref[...] = jnp.zeros_like(dnw_ref)

        lat = i < nlat
        g = jnp.minimum((i * tm) // s, nb)
        scale = mod_ref[g, pl.ds(4, 1), :]
        dh = _dot(p0_ref[...], wp_ref[0:512, :])
        extra = _dot(uv_ref[...], wp_ref[512:1536, :])
        dh = dh + jnp.where(lat, extra, 0.0)
        x = x_ref[...]
        r = lax.rsqrt(jnp.mean(x * x, axis=-1, keepdims=True) + EPS)
        xh = x * r
        n = xh * nw_ref[...]
        dmod_ref[g, pl.ds(3, 1), :] += jnp.sum(dh, axis=0, keepdims=True)
        dmod_ref[g, pl.ds(4, 1), :] += jnp.sum(dh * n, axis=0, keepdims=True)
        dn = dh * (1.0 + scale)
        dnw_ref[...] += jnp.sum(dn * xh, axis=0, keepdims=True)
        dxh = dn * nw_ref[...]
        dx_ref[...] = jnp.where(lat, dr_ref[...], 0.0) + r * (dxh - xh * jnp.mean(dxh * xh, axis=-1, keepdims=True))

    row = lambda i: (i, 0)
    lrow = lambda i: (jnp.minimum(i, nlat - 1), 0)
    return _hosted_call(
        body, "mixin_bwd", (t_all // tm,),
        [pl.BlockSpec((tm, 512), row), pl.BlockSpec((tm, 1024), lrow), pl.BlockSpec((tm, d), row),
         pl.BlockSpec((tm, d), lrow), _whole(mod.shape), _whole(nw.shape), _whole(wp.shape)],
        [pl.BlockSpec((tm, d), row), pl.BlockSpec(mod.shape, lambda i: (0, 0, 0)), pl.BlockSpec((1, d), lambda i: (0, 0))],
        [_sds((t_all, d), F32), _sds(mod.shape, F32), _sds((1, d), F32)], (dp0, duv, xs, dres, mod, nw, wp), swap=swap)


def _prep_fwd(proj, row0, nb, s, pos0, sk, key0, into, tabs, wq, wk, wv, kvaw, qaw, qnw, knw, tm, with_q, name):
    nblk = s // tm
    n_into = 0 if into is None else 2

    def body(p_ref, cos_ref, sa_ref, sb_ref, wq_ref, wk_ref, wv_ref, kvaw_ref, qaw_ref, qnw_ref, knw_ref, *rest):
        outs, heads_ref = rest[n_into:-1], rest[-1]
        q_ref, k_ref, v_ref = outs if with_q else (None,) + outs
        cos, sin_a, sin_b = cos_ref[...][None], sa_ref[...][None], sb_ref[...][None]

        def normed_roped(w_ref, src, extra, nw_ref, o_ref, post):
            for h in range(HEADS):
                heads_ref[h] = _dot_nt(src, w_ref[h]) if extra is None else _dot(src, w_ref[h])
            xp = heads_ref[...] if extra is None else heads_ref[...] + extra[None]
            r = lax.rsqrt(jnp.sum(xp * xp, axis=-1, keepdims=True) * (1.0 / QK_HEAD) + EPS)
            o_ref[...] = _rope3(xp * r * (nw_ref[...] * post)[None], cos, sin_a, sin_b).astype(BF16)

        ckv = p_ref[:, 0:128].astype(F32)
        rkv = lax.rsqrt(jnp.mean(ckv * ckv, axis=-1, keepdims=True) + EPS)
        ckvb = (ckv * rkv * kvaw_ref[...]).astype(BF16)
        normed_roped(wk_ref, ckvb, p_ref[:, 128:256].astype(F32), knw_ref, k_ref, 1.0)
        for j in range(HEADS // 2):
            v_ref[j] = _dot(ckvb, wv_ref[j]).astype(BF16)
        if with_q:
            cq = p_ref[:, 256:512].astype(F32)
            rq = lax.rsqrt(jnp.mean(cq * cq, axis=-1, keepdims=True) + EPS)
            normed_roped(wq_ref, (cq * rq * qaw_ref[...]).astype(BF16), None, qnw_ref, q_ref, SOFTMAX_SCALE)

    tab = pl.BlockSpec((tm, HEAD_PAD), lambda i: (pos0 + i % nblk, 0))
    qspec = pl.BlockSpec((None, HEADS, tm, HEAD_PAD), lambda i: (i // nblk, 0, i % nblk, 0))
    kspec = pl.BlockSpec((None, HEADS, tm, HEAD_PAD), lambda i: (i // nblk, 0, key0 + i % nblk, 0))
    vspec = pl.BlockSpec((None, HEADS // 2, tm, HEAD_PAD), lambda i: (i // nblk, 0, key0 + i % nblk, 0))
    qshape = _sds((nb, HEADS, s, HEAD_PAD), BF16)
    kshape = _sds((nb, HEADS, sk, HEAD_PAD), BF16)
    vshape = _sds((nb, HEADS // 2, sk, HEAD_PAD), BF16)
    n_q = 1 if with_q else 0
    return pl.pallas_call(
        body, name=name, grid=(nb * nblk,),
        in_specs=[pl.BlockSpec((tm, 512), lambda i: (row0 + i, 0)), tab, tab, tab, _whole(wq.shape), _whole(wk.shape),
                  _whole(wv.shape), _whole(kvaw.shape), _whole(qaw.shape), _whole(qnw.shape), _whole(knw.shape)]
        + [pl.BlockSpec(memory_space=pl.ANY)] * n_into,
        out_specs=([qspec] if with_q else []) + [kspec, vspec],
        out_shape=([qshape] if with_q else []) + [kshape, vshape],
        scratch_shapes=[pltpu.VMEM((HEADS, tm, HEAD_PAD), F32)],
        input_output_aliases={11: n_q, 12: n_q + 1} if n_into else {}, compiler_params=_params(1),
    )(proj, *tabs, wq, wk, wv, kvaw, qaw, qnw, knw, *(into or ()))


def _prep_bwd(proj, row0, nb, s, pos0, key0, dp_rows, dp_into, tabs, wq, wk, wv, kvaw, qaw, qnw, knw, dq, dk, dv, init, tm,
              name):
    nblk = s // tm
    with_q = dq is not None
    n_init = 0 if init is None else len(init)
    n_into = 0 if dp_into is None else 1

    def body(*refs):
        p_ref, cos_ref, sa_ref, sb_ref, wq_ref, wk_ref, wv_ref, kvaw_ref, qaw_ref, qnw_ref, knw_ref = refs[:11]
        rest = list(refs[11:])
        dq_ref = rest.pop(0) if with_q else None
        dk_ref, dv_ref = rest.pop(0), rest.pop(0)
        init_refs = [rest.pop(0) for _ in range(n_init)]
        if n_into:
            rest.pop(0)
        dp_ref = rest.pop(0)
        if with_q:
            dwq_ref, dqaw_ref, dqnw_ref = rest.pop(0), rest.pop(0), rest.pop(0)
        dwk_ref, dwv_ref, dkvaw_ref, dknw_ref, heads_ref, dhb_ref, dkr_ref = rest
        accs = [dwk_ref, dwv_ref, dkvaw_ref, dknw_ref]

        @pl.when(pl.program_id(0) == 0)
        def _():
            for k, acc in enumerate(accs):
                acc[...] = init_refs[k][...] if n_init else jnp.zeros_like(acc)
            if with_q:
                dwq_ref[...] = jnp.zeros_like(dwq_ref)
                dqaw_ref[...] = jnp.zeros_like(dqaw_ref)
                dqnw_ref[...] = jnp.zeros_like(dqnw_ref)

        cos, sin_a, sin_b = cos_ref[...][None], sa_ref[...][None], sb_ref[...][None]
        lane = lax.broadcasted_iota(jnp.int32, (tm, HEAD_PAD), 1)
        rope_lanes = (lane >= QK_NOPE) & (lane < QK_HEAD)

        def heads_bwd(w_ref, src, extra, nw_ref, d_ref, dnw_ref, dw_ref, post):
            w_t = extra is None
            for h in range(HEADS):
                heads_ref[h] = _dot_nt(src, w_ref[h]) if w_t else _dot(src, w_ref[h])
            xp = heads_ref[...] if extra is None else heads_ref[...] + extra[None]
            r = lax.rsqrt(jnp.sum(xp * xp, axis=-1, keepdims=True) * (1.0 / QK_HEAD) + EPS)
            xh = xp * r
            dn = _rope3_t(d_ref[...], cos, sin_a, sin_b)
            dnw_ref[...] += post * jnp.sum(jnp.sum(dn * xh, axis=0), axis=0, keepdims=True)
            dxh = dn * (nw_ref[...] * post)[None]
            dxp = r * (dxh - xh * (jnp.sum(dxh * xh, axis=-1, keepdims=True) * (1.0 / QK_HEAD)))
            dhb_ref[...] = dxp.astype(BF16)
            dsrc = jnp.zeros((tm, src.shape[1]), F32)
            for h in range(HEADS):
                dsrc = dsrc + (_dot(dhb_ref[h], w_ref[h]) if w_t else _dot_nt(dhb_ref[h], w_ref[h]))
                dw_ref[h] += _dot_tn(src, dhb_ref[h])
            return dsrc, jnp.sum(dxp, axis=0)

        ckv = p_ref[:, 0:128].astype(F32)
        rkv = lax.rsqrt(jnp.mean(ckv * ckv, axis=-1, keepdims=True) + EPS)
        ckvh = ckv * rkv
        ckvb = (ckvh * kvaw_ref[...]).astype(BF16)
        for h in range(HEADS):
            dkr_ref[h] = dk_ref[h].T
        dckv, dkp_sum = heads_bwd(wk_ref, ckvb, p_ref[:, 128:256].astype(F32), knw_ref, dkr_ref, dknw_ref, dwk_ref,
                                  1.0)
        for j in range(HEADS // 2):
            dvb = dv_ref[j].T.astype(BF16)
            dckv = dckv + _dot_nt(dvb, wv_ref[j])
            dwv_ref[j] += _dot_tn(ckvb, dvb)
        dkvaw_ref[...] += jnp.sum(dckv * ckvh, axis=0, keepdims=True)
        dch = dckv * kvaw_ref[...]
        dp_ref[:, 0:128] = (rkv * (dch - ckvh * jnp.mean(dch * ckvh, axis=-1, keepdims=True))).astype(BF16)
        dp_ref[:, 128:256] = jnp.where(rope_lanes, dkp_sum, 0.0).astype(BF16)
        if with_q:
            cq = p_ref[:, 256:512].astype(F32)
            rq = lax.rsqrt(jnp.mean(cq * cq, axis=-1, keepdims=True) + EPS)
            cqh = cq * rq
            cqb = (cqh * qaw_ref[...]).astype(BF16)
            dcq, _ = heads_bwd(wq_ref, cqb, None, qnw_ref, dq_ref, dqnw_ref, dwq_ref, SOFTMAX_SCALE)
            dqaw_ref[...] += jnp.sum(dcq * cqh, axis=0, keepdims=True)
            dqc = dcq * qaw_ref[...]
            dp_ref[:, 256:512] = (rq * (dqc - cqh * jnp.mean(dqc * cqh, axis=-1, keepdims=True))).astype(BF16)
        else:
            dp_ref[:, 256:512] = jnp.zeros((tm, Q_LORA), BF16)

    tab = pl.BlockSpec((tm, HEAD_PAD), lambda i: (pos0 + i % nblk, 0))
    qspec = pl.BlockSpec((None, HEADS, tm, HEAD_PAD), lambda i: (i // nblk, 0, i % nblk, 0))
    kspec = pl.BlockSpec((None, HEADS, HEAD_PAD, tm), lambda i: (i // nblk, 0, 0, key0 + i % nblk))
    vspec = pl.BlockSpec((None, HEADS // 2, HEAD_PAD, tm), lambda i: (i // nblk, 0, 0, key0 + i % nblk))

    def acc_spec(shape):
        nd = len(shape)
        return pl.BlockSpec(shape, lambda i: (0,) * nd)

    acc_shapes = [(HEADS, KV_LORA, HEAD_PAD), (HEADS // 2, KV_LORA, HEAD_PAD), (1, KV_LORA), (1, HEAD_PAD)]
    q_shapes = [(HEADS, Q_LORA, HEAD_PAD), (1, Q_LORA), (1, HEAD_PAD)] if with_q else []
    out_shapes = [(dp_rows, 512)] + q_shapes + acc_shapes
    n_before = 11 + (1 if with_q else 0) + 2 + n_init
    return pl.pallas_call(
        body, name=name, grid=(nb * nblk,),
        in_specs=[pl.BlockSpec((tm, 512), lambda i: (row0 + i, 0)), tab, tab, tab, _whole(wq.shape), _whole(wk.shape),
                  _whole(wv.shape), _whole(kvaw.shape), _whole(qaw.shape), _whole(qnw.shape), _whole(knw.shape)]
        + ([qspec] if with_q else []) + [kspec, vspec] + [_whole(a.shape) for a in (init or [])]
        + [pl.BlockSpec(memory_space=pl.ANY)] * n_into,
        out_specs=[pl.BlockSpec((tm, 512), lambda i: (row0 + i, 0))] + [acc_spec(sh) for sh in q_shapes + acc_shapes],
        out_shape=[_sds(out_shapes[0], BF16)] + [_sds(sh, F32) for sh in out_shapes[1:]],
        scratch_shapes=[pltpu.VMEM((HEADS, tm, HEAD_PAD), F32), pltpu.VMEM((HEADS, tm, HEAD_PAD), BF16),
                        pltpu.VMEM((HEADS, tm, HEAD_PAD), F32)],
        input_output_aliases={n_before: 0} if n_into else {}, compiler_params=_params(1),
    )(proj, *tabs, wq, wk, wv, kvaw, qaw, qnw, knw, *([dq] if with_q else []), dk, dv, *(init or []),
      *([dp_into] if n_into else []))


def _attn_fwd(q, k, v, tq, exch=None):
    nb, _, s, _ = q.shape
    sk = k.shape[2]
    nq = s // tq

    def body(q_ref, k_ref, v_ref, o_ref, lse_ref, vext_ref):
        @pl.when(pl.program_id(2) == 0)
        def _():
            vext_ref[:, 0:HEAD_PAD] = v_ref[...]
            vext_ref[:, HEAD_PAD:2 * HEAD_PAD] = jnp.ones((sk, HEAD_PAD), BF16)

        lane = lax.broadcasted_iota(jnp.int32, (tq, HEAD_PAD), 1)
        outs = []
        for hh in range(2):
            sc = _dot_nt(q_ref[hh], k_ref[hh])
            m = jnp.max(sc, axis=-1, keepdims=True)
            pv = _dot(jnp.exp2(sc - m).astype(BF16), vext_ref[...])
            l = pv[:, HEAD_PAD:HEAD_PAD + 1]
            outs.append(pv[:, 0:HEAD_PAD] / l)
            lse_ref[hh] = m + jnp.log2(l)
        o_ref[...] = jnp.where(lane < V_HEAD, outs[0], outs[1]).astype(BF16)

    (o, lse), got = _hosted_call(
        body, "attn_fwd", (nb, HEADS // 2, nq),
        [pl.BlockSpec((None, 2, tq, HEAD_PAD), lambda b, j, i: (b, j, i, 0)),
         pl.BlockSpec((None, 2, sk, HEAD_PAD), lambda b, j, i: (b, j, 0, 0)),
         pl.BlockSpec((None, None, sk, HEAD_PAD), lambda b, j, i: (b, j, 0, 0))],
        [pl.BlockSpec((tq, HEAD_PAD), lambda b, j, i: (b * nq + i, j)),
         pl.BlockSpec((None, 2, tq, 1), lambda b, j, i: (b, j, i, 0))],
        [_sds((nb * s, MLA_W + GMLP_W), BF16), _sds((nb, HEADS, s, 1), F32)], (q, k, v),
        scratch=[pltpu.VMEM((sk, 2 * HEAD_PAD), BF16)], exch=exch)
    return o, lse, got


def _attn_bwd(q, k, v, do, o, lse, tq, exch=None):
    nb, _, s, _ = q.shape
    sk = k.shape[2]
    nq = s // tq

    def body(q_ref, k_ref, v_ref, do_ref, o_ref, lse_ref, dq_ref, dkt_ref, dvt_ref):
        @pl.when(pl.program_id(2) == 0)
        def _():
            dkt_ref[...] = jnp.zeros_like(dkt_ref)
            dvt_ref[...] = jnp.zeros_like(dvt_ref)

        lane = lax.broadcasted_iota(jnp.int32, (tq, HEAD_PAD), 1)
        dov = do_ref[...]
        prod = dov.astype(F32) * o_ref[...].astype(F32)
        for hh in range(2):
            mine = (lane < V_HEAD) if hh == 0 else (lane >= V_HEAD)
            doh = jnp.where(mine, dov, jnp.zeros_like(dov))
            delta = jnp.sum(jnp.where(mine, prod, 0.0), axis=-1, keepdims=True)
            qh = q_ref[hh]
            q_ln2 = (qh.astype(F32) * LN2).astype(BF16)
            kv = k_ref[hh]
            p = jnp.exp2(_dot_nt(qh, kv) - lse_ref[hh])
            u = (p * (_dot_nt(doh, v_ref[...]) - delta)).astype(BF16)
            dq_ref[hh] = _dot(u, kv) * LN2
            dkt_ref[hh] += _dot_tn(q_ln2, u)
            dvt_ref[...] += _dot_tn(doh, p.astype(BF16))

    qspec = pl.BlockSpec((None, 2, tq, HEAD_PAD), lambda b, j, i: (b, j, i, 0))
    kspec = pl.BlockSpec((None, 2, sk, HEAD_PAD), lambda b, j, i: (b, j, 0, 0))
    vspec = pl.BlockSpec((None, None, sk, HEAD_PAD), lambda b, j, i: (b, j, 0, 0))
    ospec = pl.BlockSpec((tq, HEAD_PAD), lambda b, j, i: (b * nq + i, j))
    return _hosted_call(
        body, "attn_bwd", (nb, HEADS // 2, nq),
        [qspec, kspec, vspec, ospec, ospec, pl.BlockSpec((None, 2, tq, 1), lambda b, j, i: (b, j, i, 0))],
        [qspec, pl.BlockSpec((None, 2, HEAD_PAD, sk), lambda b, j, i: (b, j, 0, 0)),
         pl.BlockSpec((None, None, HEAD_PAD, sk), lambda b, j, i: (b, j, 0, 0))],
        [_sds(q.shape, F32), _sds((nb, HEADS, HEAD_PAD, sk), F32), _sds((nb, HEADS // 2, HEAD_PAD, sk), F32)],
        (q, k, v, do, o, lse), exch=exch)


def _group_masks(rows):
    lane = lax.broadcasted_iota(jnp.int32, (rows, GMLP_W), 1)
    return [(lane >= g * GROUP_DIM) & (lane < (g + 1) * GROUP_DIM) for g in range(GROUPS)]


def _gmlp_fwd(proj, mixcat, wcat, bias, vnw, ones, tm):
    t = mixcat.shape[0]

    def body(u_ref, v_ref, wcat_ref, bias_ref, vnw_ref, ones_ref, _, o_ref):
        masks = _group_masks(CHUNK)
        gv = _gelu(v_ref[...].astype(F32))
        rv = lax.rsqrt(_group_sum(gv * gv, ones_ref) * (1.0 / GROUP_DIM) + EPS)
        vnb = (gv * rv * vnw_ref[...]).astype(BF16)
        for c in range(tm // CHUNK):
            rows = slice(c * CHUNK, (c + 1) * CHUNK)
            vc = vnb[rows]
            stack = jnp.concatenate([jnp.where(m, vc, jnp.zeros_like(vc)) for m in masks], axis=0)
            sp = _dot(wcat_ref[...], stack) + bias_ref[...]
            o_ref[rows, :] = (_gelu(u_ref[rows, :].astype(F32)) * sp).astype(BF16)

    return pl.pallas_call(
        body, name="gmlp_fwd", grid=(t // tm,),
        in_specs=[pl.BlockSpec((tm, GMLP_W), lambda i: (i, 1)), pl.BlockSpec((tm, GMLP_W), lambda i: (i, 2)),
                  _whole(wcat.shape), _whole(bias.shape), _whole(vnw.shape), _whole(ones.shape),
                  pl.BlockSpec(memory_space=pl.ANY)],
        out_specs=pl.BlockSpec((tm, GMLP_W), lambda i: (i, 1)),
        out_shape=_sds(mixcat.shape, BF16), input_output_aliases={6: 0}, compiler_params=_params(1),
    )(proj, proj, wcat, bias, vnw, ones, mixcat)


def _gmlp_bwd(proj, dsg, wcat, wcat_t, bias, vnw, ones, tm):
    t = dsg.shape[0]

    def body(u_ref, v_ref, dsg_ref, wcat_ref, wcatt_ref, bias_ref, vnw_ref, ones_ref,
             duv_ref, dws_ref, dbs_ref, dvnw_ref):
        @pl.when(pl.program_id(0) == 0)
        def _():
            dws_ref[...] = jnp.zeros_like(dws_ref)
            dbs_ref[...] = jnp.zeros_like(dbs_ref)
            dvnw_ref[...] = jnp.zeros_like(dvnw_ref)

        masks = _group_masks(CHUNK)
        v = v_ref[...].astype(F32)
        gv = _gelu(v)
        rv = lax.rsqrt(_group_sum(gv * gv, ones_ref) * (1.0 / GROUP_DIM) + EPS)
        xh = gv * rv
        vnb = (xh * vnw_ref[...]).astype(BF16)
        dvn_parts = []
        for c in range(tm // CHUNK):
            rows = slice(c * CHUNK, (c + 1) * CHUNK)
            vc = vnb[rows]
            stack = jnp.concatenate([jnp.where(m, vc, jnp.zeros_like(vc)) for m in masks], axis=0)
            sp = _dot(wcat_ref[...], stack) + bias_ref[...]
            u = u_ref[rows, :].astype(F32)
            dsg_c = dsg_ref[rows, :]
            duv_ref[rows, 0:GMLP_W] = (dsg_c * sp * _gelu_grad(u)).astype(BF16)
            ds = dsg_c * _gelu(u)
            dstack = jnp.concatenate([jnp.where(m, ds, 0.0) for m in masks], axis=0)
            dbs_ref[...] += jnp.broadcast_to(jnp.sum(dstack, axis=-1, keepdims=True), dbs_ref.shape)
            dstb = dstack.astype(BF16)
            dvn_parts.append(_dot(wcatt_ref[...], dstb))
            dws_ref[...] += _dot_nt(dstb, vc)
        dvn = jnp.concatenate(dvn_parts, axis=0) if len(dvn_parts) > 1 else dvn_parts[0]
        dvnw_ref[...] += jnp.sum(dvn * xh, axis=0, keepdims=True)
        dxh = dvn * vnw_ref[...]
        gm = _group_sum(dxh * xh, ones_ref) * (1.0 / GROUP_DIM)
        duv_ref[:, GMLP_W:2 * GMLP_W] = (rv * (dxh - xh * gm) * _gelu_grad(v)).astype(BF16)

    row = pl.BlockSpec((tm, GMLP_W), lambda i: (i, 0))
    return pl.pallas_call(
        body, name="gmlp_bwd", grid=(t // tm,),
        in_specs=[pl.BlockSpec((tm, GMLP_W), lambda i: (i, 1)), pl.BlockSpec((tm, GMLP_W), lambda i: (i, 2)), row,
                  _whole(wcat.shape), _whole(wcat_t.shape), _whole(bias.shape), _whole(vnw.shape), _whole(ones.shape)],
        out_specs=[pl.BlockSpec((tm, 2 * GMLP_W), lambda i: (i, 0)), pl.BlockSpec((GROUPS * CHUNK, CHUNK), lambda i: (0, 0)),
                   pl.BlockSpec((GROUPS * CHUNK, CHUNK), lambda i: (0, 0)), pl.BlockSpec((1, GMLP_W), lambda i: (0, 0))],
        out_shape=[_sds((t, 2 * GMLP_W), BF16), _sds((GROUPS * CHUNK, CHUNK), F32), _sds((GROUPS * CHUNK, CHUNK), F32),
                   _sds((1, GMLP_W), F32)],
        compiler_params=_params(1),
    )(proj, proj, dsg, wcat, wcat_t, bias, vnw, ones)


def _mixout_fwd(mixcat, xs, mod, wout, s, tm):
    t, width = mixcat.shape
    d = xs.shape[1]

    def body(cat_ref, x_ref, mod_ref, w_ref, x2_ref, mix_ref):
        g = (pl.program_id(0) * tm) // s
        gate = mod_ref[g, pl.ds(5, 1), :]
        mix = _dot(cat_ref[...], w_ref[...])
        x2_ref[...] = x_ref[...] + gate * mix
        mix_ref[...] = mix.astype(BF16)

    row = lambda i: (i, 0)
    return pl.pallas_call(
        body, name="mixout_fwd", grid=(t // tm,),
        in_specs=[pl.BlockSpec((tm, width), row), pl.BlockSpec((tm, d), row), _whole(mod.shape), _whole(wout.shape)],
        out_specs=[pl.BlockSpec((tm, d), row), pl.BlockSpec((tm, d), row)],
        out_shape=[_sds((t, d), F32), _sds((t, d), BF16)], compiler_params=_params(1),
    )(mixcat, xs, mod, wout)


def _mixout_bwd(dx2, mix, mod, wout, s, tm):
    t, d = dx2.shape

    def body(dx_ref, mix_ref, mod_ref, w_ref, dmix_ref, do_ref, dsg_ref, dmod_ref):
        i = pl.program_id(0)

        @pl.when(i == 0)
        def _():
            dmod_ref[...] = jnp.zeros_like(dmod_ref)

        g = (i * tm) // s
        gate = mod_ref[g, pl.ds(5, 1), :]
        dx = dx_ref[...]
        dmod_ref[g, pl.ds(5, 1), :] += jnp.sum(dx * mix_ref[...].astype(F32), axis=0, keepdims=True)
        dmb = (gate * dx).astype(BF16)
        dmix_ref[...] = dmb
        do_ref[...] = _dot_nt(dmb, w_ref[0:MLA_W, :]).astype(BF16)
        dsg_ref[...] = _dot_nt(dmb, w_ref[MLA_W:MLA_W + GMLP_W, :])

    row = lambda i: (i, 0)
    return pl.pallas_call(
        body, name="mixout_bwd", grid=(t // tm,),
        in_specs=[pl.BlockSpec((tm, d), row), pl.BlockSpec((tm, d), row), _whole(mod.shape), _whole(wout.shape)],
        out_specs=[pl.BlockSpec((tm, d), row), pl.BlockSpec((tm, MLA_W), row), pl.BlockSpec((tm, GMLP_W), row),
                   pl.BlockSpec(mod.shape, lambda i: (0, 0, 0))],
        out_shape=[_sds((t, d), BF16), _sds((t, MLA_W), BF16), _sds((t, GMLP_W), F32), _sds(mod.shape, F32)],
        compiler_params=_params(1),
    )(dx2, mix, mod, wout)


def _swap_cores(parts, name):
    n = len(parts)

    def body(*refs):
        srcs, outs, send_sems, recv_sems = refs[:n], refs[n:2 * n], refs[2 * n], refs[2 * n + 1]
        x, y, c = lax.axis_index("x"), lax.axis_index("y"), lax.axis_index("c")
        copies = [pltpu.make_async_remote_copy(
            src_ref=srcs[w], dst_ref=outs[w], send_sem=send_sems.at[w], recv_sem=recv_sems.at[w],
            device_id=(x, y, 1 - c), device_id_type=pl.DeviceIdType.MESH) for w in range(n)]
        for cp in copies:
            cp.start()
        for cp in copies:
            cp.wait()

    any_spec = pl.BlockSpec(memory_space=pl.ANY)
    return pl.pallas_call(
        body, name=name, in_specs=[any_spec] * n, out_specs=[any_spec] * n,
        out_shape=[_sds(p.shape, p.dtype) for p in parts],
        scratch_shapes=[pltpu.SemaphoreType.DMA((n,)), pltpu.SemaphoreType.DMA((n,))],
    )(*parts)


def _row_tile(r, c, mult):
    return _div_tile(r, max(mult, (1 << 18) // c), mult)


def _sum_slots(recv, name):
    _, r, c = recv.shape
    tr = _row_tile(r, c, 16)

    def body(r_ref, o_ref):
        f = lambda k: r_ref[k].astype(F32)
        o_ref[...] = ((f(0) + f(1)) + f(2)) + f(3)

    return pl.pallas_call(
        body, name=name, grid=(r // tr,),
        in_specs=[pl.BlockSpec((N_CHIPS, tr, c), lambda i: (0, i, 0))],
        out_specs=pl.BlockSpec((tr, c), lambda i: (i, 0)),
        out_shape=_sds((r, c), F32), compiler_params=_params(1),
    )(recv)


def _adamw(parts, w, m, v, name, exch=None, swap=None):
    r, wd = w.shape
    tr = _row_tile(r, wd, 8)
    c1 = 1.0 / (1.0 - ADAM_B1 ** ADAM_STEP)
    c2 = 1.0 / (1.0 - ADAM_B2 ** ADAM_STEP)
    n_p = len(parts)

    def body(*refs):
        p_refs = refs[:n_p]
        w_ref, m_ref, v_ref, g_ref, d_ref, nm_ref, nv_ref = refs[n_p:]
        g = p_refs[0][...]
        for p_ref in p_refs[1:]:
            g = g + p_ref[...]
        nm = ADAM_B1 * m_ref[...] + (1.0 - ADAM_B1) * g
        nv = ADAM_B2 * v_ref[...] + (1.0 - ADAM_B2) * (g * g)
        g_ref[...] = g
        nm_ref[...] = nm
        nv_ref[...] = nv
        d_ref[...] = -ADAM_LR * ((nm * c1) / (jnp.sqrt(nv * c2) + ADAM_EPS) + ADAM_WD * w_ref[...])

    spec = pl.BlockSpec((tr, wd), lambda i: (i, 0))
    return _hosted_call(body, name, (r // tr,), [spec] * (n_p + 3), [spec] * 4, [_sds((r, wd), F32)] * 4,
                        (*parts, w, m, v), exch=exch, swap=swap)


def _all_peers(x, y, c):
    flips = [(dx, dy, dc) for dx in (0, 1) for dy in (0, 1) for dc in (0, 1)][1:]
    return [(1 - x if dx else x, 1 - y if dy else y, 1 - c if dc else c) for dx, dy, dc in flips]


def _first_exchange(shards, later, cc, w, b):
    n_w, n_l = len(shards), len(later)
    n = w.shape[1]

    def body(*refs):
        src32, later_in, (cc_ref, w_ref, b_ref) = refs[:n_w], refs[n_w:n_w + n_l], refs[n_w + n_l:n_w + n_l + 3]
        o0 = n_w + n_l + 3
        outs, (all_ref, tab_ref), later_out = refs[o0:o0 + n_w], refs[o0 + n_w:o0 + n_w + 2], refs[o0 + n_w + 2:o0 + n_w + 2 + n_l]
        s0 = o0 + n_w + 2 + n_l
        srcs = refs[s0:s0 + n_w]
        (part_ref, ici_send, ici_recv, d2d_send, d2d_recv, local_sems, cc_send, cc_recv, tab_send,
         tab_recv) = refs[s0 + n_w:]
        for wi in range(n_w):
            srcs[wi][...] = src32[wi][...].astype(BF16)
        x, y, c = lax.axis_index("x"), lax.axis_index("y"), lax.axis_index("c")
        chip, dev = 2 * x + y, 4 * x + 2 * y + c
        chips = _other_chips(x, y)
        peers = _all_peers(x, y, c)

        def half(wi, which):
            hr = shards[wi].shape[0] // 2
            return pl.ds(pl.multiple_of(which * hr, 16), hr)

        def over_ici(wi, k, arriving):
            px, py = chips[k]
            slot = 2 * px + py if arriving else chip
            return pltpu.make_async_remote_copy(
                src_ref=srcs[wi].at[half(wi, c)], dst_ref=outs[wi].at[slot, half(wi, c)],
                send_sem=ici_send.at[3 * wi + k], recv_sem=ici_recv.at[3 * wi + k], device_id=(px, py, c),
                device_id_type=pl.DeviceIdType.MESH)

        def to_sibling(wi, k, arriving):
            px, py = chips[k]
            rows = half(wi, 1 - c if arriving else c)
            return pltpu.make_async_remote_copy(
                src_ref=outs[wi].at[2 * px + py, rows], dst_ref=outs[wi].at[2 * px + py, rows],
                send_sem=d2d_send.at[3 * wi + k], recv_sem=d2d_recv.at[3 * wi + k], device_id=(x, y, 1 - c),
                device_id_type=pl.DeviceIdType.MESH)

        def cc_copy(k, peer, slot):
            return pltpu.make_async_remote_copy(
                src_ref=cc_ref, dst_ref=all_ref.at[slot], send_sem=cc_send.at[k], recv_sem=cc_recv.at[k],
                device_id=peer, device_id_type=pl.DeviceIdType.MESH)

        def rows_of(px, py):
            return part_ref.at[pl.ds(pl.multiple_of((4 * px + 2 * py + c) * MOD_ROWS, MOD_ROWS), MOD_ROWS)]

        def tab_copy(k, px, py, slot):
            return pltpu.make_async_remote_copy(
                src_ref=rows_of(px, py), dst_ref=tab_ref.at[slot], send_sem=tab_send.at[k], recv_sem=tab_recv.at[k],
                device_id=(px, py, c), device_id_type=pl.DeviceIdType.MESH)

        local = [pltpu.make_async_copy(srcs[wi], outs[wi].at[chip], local_sems.at[wi]) for wi in range(n_w)]
        for cp in local:
            cp.start()
        pairs = [(wi, k) for wi in range(n_w) for k in range(3)]
        for wi, k in pairs:
            over_ici(wi, k, False).start()
        for k, peer in enumerate(peers):
            cc_copy(k, peer, dev).start()
        all_ref[dev] = cc_ref[...]
        for k, (px, py, pc) in enumerate(peers):
            cc_copy(k, (px, py, pc), 4 * px + 2 * py + pc).wait_recv()
        cv = all_ref[...].reshape(8 * MOD_ROWS, cc.shape[1])
        part_ref[...] = _dot((cv * _sigmoid(cv)).astype(BF16), w_ref[...]) + b_ref[...]
        for k, (px, py) in enumerate(chips):
            tab_copy(k, px, py, chip).start()
        tab_ref[chip] = rows_of(x, y)[...]
        for k, (px, py) in enumerate(chips):
            tab_copy(k, px, py, 2 * px + py).wait_recv()
        for j in range(n_l):
            later_out[j][...] = later_in[j][...].astype(BF16)
        for wi, k in pairs:
            over_ici(wi, k, True).wait_recv()
            to_sibling(wi, k, False).start()
        for wi, k in pairs:
            to_sibling(wi, k, True).wait_recv()
        for wi, k in pairs:
            over_ici(wi, k, False).wait_send()
            to_sibling(wi, k, False).wait_send()
        for k, peer in enumerate(peers):
            cc_copy(k, peer, dev).wait_send()
        for k, (px, py) in enumerate(chips):
            tab_copy(k, px, py, chip).wait_send()
        for cp in local:
            cp.wait()

    any_spec = pl.BlockSpec(memory_space=pl.ANY)
    vmem = pl.BlockSpec(memory_space=pltpu.VMEM)
    sems3 = pltpu.SemaphoreType.DMA((3 * n_w,))
    got = pl.pallas_call(
        body, name="first_exchange", in_specs=[vmem] * (n_w + n_l + 3),
        out_specs=[any_spec] * n_w + [vmem] * (2 + n_l),
        out_shape=[_sds((N_CHIPS,) + a.shape, BF16) for a in shards]
        + [_sds((8,) + cc.shape, F32), _sds((N_CHIPS, MOD_ROWS, n), F32)] + [_sds(a.shape, BF16) for a in later],
        scratch_shapes=[pltpu.VMEM(a.shape, BF16) for a in shards]
        + [pltpu.VMEM((8 * MOD_ROWS, n), F32), sems3, sems3, sems3, sems3, pltpu.SemaphoreType.DMA((n_w,)),
           pltpu.SemaphoreType.DMA((7,)), pltpu.SemaphoreType.DMA((7,)), pltpu.SemaphoreType.DMA((3,)),
           pltpu.SemaphoreType.DMA((3,))],
        compiler_params=pltpu.CompilerParams(vmem_limit_bytes=V7X_VMEM_LIMIT),
    )(*shards, *later, cc, w, b)
    return got[:n_w], got[n_w], got[n_w + 1], got[n_w + 2:]


def _ada_bwd_tp(cc_all, dmods, w, ctx_row):
    d, n = w.shape

    def body(cc_ref, m0, m1, m2, m3, w_ref, dw_ref, db_ref, dctx_ref, stage_ref, all_ref, send_sems, recv_sems):
        x, y, c = lax.axis_index("x"), lax.axis_index("y"), lax.axis_index("c")
        me = 4 * x + 2 * y + c
        dsum = m0[...] + m1[...] + m2[...] + m3[...]
        db_ref[...] = jnp.sum(dsum, axis=0, keepdims=True)
        for j in range(N_CHIPS):
            stage_ref[j] = dsum[:, j * n:(j + 1) * n]

        def copy(k, peer, slot):
            px, py, _ = peer
            return pltpu.make_async_remote_copy(
                src_ref=stage_ref.at[2 * px + py], dst_ref=all_ref.at[slot], send_sem=send_sems.at[k],
                recv_sem=recv_sems.at[k], device_id=peer, device_id_type=pl.DeviceIdType.MESH)

        peers = _all_peers(x, y, c)
        for k, peer in enumerate(peers):
            copy(k, peer, me).start()
        all_ref[me] = stage_ref[2 * x + y]
        for k, (px, py, pc) in enumerate(peers):
            copy(k, (px, py, pc), 4 * px + 2 * py + pc).wait_recv()
        for k, peer in enumerate(peers):
            copy(k, peer, me).wait_send()
        cv = cc_ref[...]
        sig = _sigmoid(cv)
        dmb = all_ref[...].reshape(8 * MOD_ROWS, n).astype(BF16)
        dw_ref[...] = _dot_tn((cv * sig).astype(BF16), dmb)
        dsc = _dot_nt(dmb, w_ref[...])
        dctx = dsc[ctx_row:ctx_row + 1, :]
        for dev in range(1, 8):
            dctx = dctx + dsc[dev * MOD_ROWS + ctx_row:dev * MOD_ROWS + ctx_row + 1, :]
        cx = cv[ctx_row:ctx_row + 1, :]
        sx = sig[ctx_row:ctx_row + 1, :]
        dctx_ref[...] = dctx * (sx * (1.0 + cx * (1.0 - sx))) * jnp.where(c == 0, 1.0, 0.0)

    vmem = pl.BlockSpec(memory_space=pltpu.VMEM)
    return pl.pallas_call(
        body, name="ada_bwd_tp", in_specs=[vmem] * 6, out_specs=[vmem] * 3,
        out_shape=[_sds((d, n), F32), _sds((1, N_MOD * d), F32), _sds((1, d), F32)],
        scratch_shapes=[pltpu.VMEM((N_CHIPS, MOD_ROWS, n), F32), pltpu.VMEM((8, MOD_ROWS, n), F32),
                        pltpu.SemaphoreType.DMA((7,)), pltpu.SemaphoreType.DMA((7,))],
        compiler_params=pltpu.CompilerParams(vmem_limit_bytes=V7X_VMEM_LIMIT),
    )(cc_all, *dmods, w)


def _rope_tables(s, ctx):
    pos = np.arange(s, dtype=np.float32)
    inv = (np.float32(ROPE_BASE) ** (-np.arange(0, QK_ROPE // 2, 2, dtype=np.float32) / np.float32(QK_ROPE // 2)))
    ang_r = np.floor(pos / GRID_W)[:, None] * inv
    ang_c = (pos - GRID_W * np.floor(pos / GRID_W))[:, None] * inv
    ang = np.concatenate([ang_r, ang_r, ang_c, ang_c], axis=-1).astype(np.float32)
    cos, sin = np.cos(ang), np.sin(ang)
    half_b = (np.arange(QK_ROPE) // 8) % 2 == 1
    sin_a = np.where(half_b, sin, 0.0)
    sin_b = np.where(half_b, 0.0, -sin)

    def place(tab, fill):
        full = np.full((s + ctx, HEAD_PAD), fill, np.float32)
        full[:s, QK_NOPE:QK_HEAD] = tab
        return jnp.asarray(full)

    return place(cos, 1.0), place(sin_a, 0.0), place(sin_b, 0.0)


def _pad_last(a, n):
    return jnp.pad(a, [(0, 0)] * (a.ndim - 1) + [(0, n - a.shape[-1])])


def _flat_rows(parts, rows, width):
    flat = jnp.concatenate([p.reshape(-1) for p in parts])
    return jnp.pad(flat, (0, rows * width - flat.shape[0])).reshape(rows, width)


def kernel(x, c, ctx, c_ctx, w_ada, b_ada, norm1_w, ffn1_w1, ffn1_w3, ffn1_w2, norm2_w, w_in, q_a_norm_w, w_uq, kv_a_norm_w, w_ukv, q_norm_w, k_norm_w, v_norm_w, w_s, b_s, w_out, norm3_w, ffn2_w1, ffn2_w3, ffn2_w2, loss_target, m_c_ctx, m_w_ada, m_b_ada, m_norm1_w, m_ffn1_w1, m_ffn1_w3, m_ffn1_w2, m_norm2_w, m_w_in, m_q_a_norm_w, m_w_uq, m_kv_a_norm_w, m_w_ukv, m_q_norm_w, m_k_norm_w, m_v_norm_w, m_w_s, m_b_s, m_w_out, m_norm3_w, m_ffn2_w1, m_ffn2_w3, m_ffn2_w2, v_c_ctx, v_w_ada, v_b_ada, v_norm1_w, v_ffn1_w1, v_ffn1_w3, v_ffn1_w2, v_norm2_w, v_w_in, v_q_a_norm_w, v_w_uq, v_kv_a_norm_w, v_w_ukv, v_q_norm_w, v_k_norm_w, v_v_norm_w, v_w_s, v_b_s, v_w_out, v_norm3_w, v_ffn2_w1, v_ffn2_w3, v_ffn2_w2):
    wts = dict(c_ctx=c_ctx, w_ada=w_ada, b_ada=b_ada, norm1_w=norm1_w, ffn1_w1=ffn1_w1, ffn1_w3=ffn1_w3, ffn1_w2=ffn1_w2,
               norm2_w=norm2_w, w_in=w_in, q_a_norm_w=q_a_norm_w, w_uq=w_uq, kv_a_norm_w=kv_a_norm_w, w_ukv=w_ukv,
               q_norm_w=q_norm_w, k_norm_w=k_norm_w, v_norm_w=v_norm_w, w_s=w_s, b_s=b_s, w_out=w_out, norm3_w=norm3_w,
               ffn2_w1=ffn2_w1, ffn2_w3=ffn2_w3, ffn2_w2=ffn2_w2)
    moms = dict(c_ctx=m_c_ctx, w_ada=m_w_ada, b_ada=m_b_ada, norm1_w=m_norm1_w, ffn1_w1=m_ffn1_w1, ffn1_w3=m_ffn1_w3,
                ffn1_w2=m_ffn1_w2, norm2_w=m_norm2_w, w_in=m_w_in, q_a_norm_w=m_q_a_norm_w, w_uq=m_w_uq,
                kv_a_norm_w=m_kv_a_norm_w, w_ukv=m_w_ukv, q_norm_w=m_q_norm_w, k_norm_w=m_k_norm_w, v_norm_w=m_v_norm_w,
                w_s=m_w_s, b_s=m_b_s, w_out=m_w_out, norm3_w=m_norm3_w, ffn2_w1=m_ffn2_w1, ffn2_w3=m_ffn2_w3,
                ffn2_w2=m_ffn2_w2)
    vars_ = dict(c_ctx=v_c_ctx, w_ada=v_w_ada, b_ada=v_b_ada, norm1_w=v_norm1_w, ffn1_w1=v_ffn1_w1, ffn1_w3=v_ffn1_w3,
                 ffn1_w2=v_ffn1_w2, norm2_w=v_norm2_w, w_in=v_w_in, q_a_norm_w=v_q_a_norm_w, w_uq=v_w_uq,
                 kv_a_norm_w=v_kv_a_norm_w, w_ukv=v_w_ukv, q_norm_w=v_q_norm_w, k_norm_w=v_k_norm_w, v_norm_w=v_v_norm_w,
                 w_s=v_w_s, b_s=v_b_s, w_out=v_w_out, norm3_w=v_norm3_w, ffn2_w1=v_ffn2_w1, ffn2_w3=v_ffn2_w3,
                 ffn2_w2=v_ffn2_w2)

    nb, s, d = x.shape
    nctx = ctx.shape[1]
    t, tc = nb * s, nb * nctx
    t_all = t + tc
    sk = s + nctx
    assert nb + 1 <= MOD_ROWS and d % LANES == 0
    tm = _token_tile(s, nctx)
    tq = _div_tile(s, 512, tm)
    tmx = _div_tile(math.gcd(s, tc), 1024, tm)
    tmo = _div_tile(s, 1024, tm)

    def held(n, a_):
        return jnp.swapaxes(a_[0], 0, 1) if n in T_WEIGHTS else a_[0]

    def unheld(n, a_):
        return (jnp.swapaxes(a_, 0, 1) if n in T_WEIGHTS else a_)[None]

    shard = {"w_ada": w_ada[0].astype(BF16)}
    full = {}

    def unshard(names, blocks):
        for n, g4 in zip(names, blocks):
            _, r_, c_ = g4.shape
            if n in ROW_SHARDED or n in T_WEIGHTS:
                full[n] = g4.reshape(N_CHIPS * r_, c_)
            else:
                full[n] = g4.transpose(1, 0, 2).reshape(r_, N_CHIPS * c_)

    def chip_major(n, g_):
        if n in ROW_SHARDED or n in T_WEIGHTS:
            return g_.reshape(N_CHIPS, g_.shape[0] // N_CHIPS, g_.shape[1]).astype(BF16)
        r_, cols = g_.shape
        return g_.reshape(r_, N_CHIPS, cols // N_CHIPS).transpose(1, 0, 2).astype(BF16)

    cc = jnp.concatenate([c, c_ctx[None, :], jnp.zeros((MOD_ROWS - nb - 1, d), F32)], axis=0)
    n_ada = shard["w_ada"].shape[1]
    assert n_ada % LANES == 0
    my_chip = 2 * lax.axis_index("x") + lax.axis_index("y")
    b_cols = lax.dynamic_slice_in_dim(b_ada, my_chip * n_ada, n_ada, axis=1)
    later = MIX_WEIGHTS + LAST_WEIGHTS
    got, cc_all, table, cast = _first_exchange([held(n, wts[n]) for n in FIRST_WEIGHTS],
                                               [held(n, wts[n]) for n in later], cc, shard["w_ada"], b_cols)
    unshard(FIRST_WEIGHTS, got)
    shard.update(zip(later, cast))
    cc_all = cc_all.reshape(8 * MOD_ROWS, d)
    mod = table.transpose(1, 0, 2).reshape(MOD_ROWS, N_MOD, d)
    wsb = w_s[0].astype(BF16)
    wcat = wsb.transpose(1, 0, 2).reshape(CHUNK, GROUPS * CHUNK)
    wcat_t = wsb.transpose(2, 0, 1).reshape(CHUNK, GROUPS * CHUNK)
    bias = jnp.repeat(b_s[0].T, GROUP_DIM, axis=1)
    vnw = v_norm_w.reshape(1, GMLP_W)
    lane = jnp.arange(GMLP_W)
    ones = (lane[:, None] // GROUP_DIM == lane[None, :] // GROUP_DIM).astype(BF16)
    qnw = _pad_last(q_norm_w, HEAD_PAD)
    knw = _pad_last(k_norm_w, HEAD_PAD)
    tabs = _rope_tables(s, nctx)

    x_lat, x_ctx = x.reshape(t, d), ctx.reshape(tc, d)
    (xs1, a1, b1, y1), got = _ffn_fwd(x_lat, x_ctx, mod, norm1_w, full["ffn1_w1"], full["ffn1_w3"], full["ffn1_w2"], 0, s,
                                      nb, tm, "ffn1_fwd", exch=("gather", [shard[n] for n in MIX_WEIGHTS]))
    unshard(MIX_WEIGHTS, got)
    wi = full["w_in"]
    wp = jnp.concatenate([wi[0:KV_LORA], jnp.zeros((QK_NOPE, d), BF16), wi[KV_LORA:KV_LORA + QK_ROPE],
                          jnp.zeros((HEAD_PAD - QK_HEAD, d), BF16), wi[KV_LORA + QK_ROPE:]], axis=0)
    wq = jnp.pad(full["w_uq"].reshape(HEADS, QK_HEAD, Q_LORA), ((0, 0), (0, HEAD_PAD - QK_HEAD), (0, 0)))
    wkv = full["w_ukv"].reshape(KV_LORA, HEADS, QK_NOPE + V_HEAD)
    wk = _pad_last(wkv[:, :, :QK_NOPE].transpose(1, 0, 2), HEAD_PAD)
    wv = wkv[:, :, QK_NOPE:].reshape(KV_LORA, HEADS // 2, 2 * V_HEAD).transpose(1, 0, 2)
    h2, proj = _mixin_fwd(xs1, mod, norm2_w, wp, s, nb, tmx)
    prep_w = (wq, wk, wv, kv_a_norm_w, q_a_norm_w, qnw, knw)
    q, k_all, v_all = _prep_fwd(proj, 0, nb, s, 0, sk, 0, None, tabs, *prep_w, tmo, True, "prep_fwd")
    k_all, v_all = _prep_fwd(proj, t // tm, nb, nctx, s // tm, sk, s // tm, (k_all, v_all), tabs, *prep_w, tm, False,
                             "prep_ctx_fwd")
    o, lse, got = _attn_fwd(q, k_all, v_all, tq, exch=("gather", [shard[n] for n in LAST_WEIGHTS]))
    unshard(LAST_WEIGHTS, got)
    mixcat = _gmlp_fwd(proj, o, wcat, bias, vnw, ones, tq)
    x2, mix = _mixout_fwd(mixcat, xs1, mod, full["w_out"], s, tmo)
    (dy, a2, b2, y2, loss_part), _ = _ffn_fwd(x2, None, mod, norm3_w, full["ffn2_w1"], full["ffn2_w3"], full["ffn2_w2"], 6,
                                              s, nb, tm, "ffn2_fwd", target=loss_target.reshape(t, d))

    grads, cm, recv = {}, {}, {}

    def scatter_of(names):
        return ("scatter", [cm[n] for n in names])

    (dx2, h3, g2, da2, db2, dyb2, dmod_c, grads["norm3_w"]), _ = _ffn_bwd(
        dy, x2, None, a2, b2, y2, mod, norm3_w, full["ffn2_w1"], full["ffn2_w3"], full["ffn2_w2"], 6, s, nb, tm,
        "ffn2_bwd")
    cm["ffn2_w1"] = chip_major("ffn2_w1", _mm_tn(da2, h3, t, "ffn2_dw1"))
    cm["ffn2_w3"] = chip_major("ffn2_w3", _mm_tn(db2, h3, t, "ffn2_dw3"))
    cm["ffn2_w2"] = chip_major("ffn2_w2", _mm_tn(g2, dyb2, t, "ffn2_dw2"))
    dmix, do, dsg, dmod_b = _mixout_bwd(dx2, mix, mod, full["w_out"], s, tmo)
    cm["w_out"] = chip_major("w_out", _mm_tn(mixcat, dmix, t, "wout_dw"))
    duv, dws, dbs, dvnw = _gmlp_bwd(proj, dsg, wcat, wcat_t, bias, vnw, ones, tq)
    group = LAST_WEIGHTS + ("w_out",)
    (dq, dk, dv), got = _attn_bwd(q, k_all, v_all, do, mixcat, lse, tq, exch=scatter_of(group))
    recv.update(zip(group, got))
    dp0, dwk_c, dwv_c, dkvaw_c, dknw_c = _prep_bwd(
        proj, t // tm, nb, nctx, s // tm, s // tm, t_all, None, tabs, *prep_w, None, dk, dv, None, tm, "prep_ctx_bwd")
    dp0, dwq, dqaw, dqnw, dwk, dwv, dkvaw, dknw = _prep_bwd(
        proj, 0, nb, s, 0, 0, t_all, dp0, tabs, *prep_w, dq, dk, dv, [dwk_c, dwv_c, dkvaw_c, dknw_c], tq, "prep_bwd")
    part, sib = {}, {}
    early = LAST_WEIGHTS + ("w_out",)
    for n in early:
        part[n] = _sum_slots(recv[n], "sum_" + n)
    (dxs1, dmod_a, grads["norm2_w"]), _, got = _mixin_bwd(dp0, duv, xs1, dx2, mod, norm2_w, wp, s, nb, tmx,
                                                          [part[n] for n in early])
    sib.update(zip(early, got))
    dwp = jnp.concatenate([_mm_tn(dp0, h2, t_all, "win_dw_kvq"), _mm_tn(duv, h2, t, "win_dw_uv")], axis=0)
    cm["w_in"] = chip_major("w_in", jnp.concatenate(
        [dwp[0:KV_LORA], dwp[KV_LORA + QK_NOPE:KV_LORA + QK_HEAD], dwp[256:]], axis=0))
    cm["w_uq"] = chip_major("w_uq", dwq[:, :, :QK_HEAD].transpose(0, 2, 1).reshape(HEADS * QK_HEAD, Q_LORA))
    cm["w_ukv"] = chip_major("w_ukv", jnp.concatenate(
        [dwk[:, :, :QK_NOPE].transpose(1, 0, 2),
         dwv.transpose(1, 0, 2).reshape(KV_LORA, HEADS, V_HEAD)], axis=2).reshape(KV_LORA, HEADS * (QK_NOPE + V_HEAD)))
    (dx_lat, h1, g1, da1, db1, dyb1, dmod_0, grads["norm1_w"]), _ = _ffn_bwd(
        dxs1, x_lat, x_ctx, a1, b1, y1, mod, norm1_w, full["ffn1_w1"], full["ffn1_w3"], full["ffn1_w2"], 0, s, nb, tm,
        "ffn1_bwd")
    dmods = [m_.reshape(MOD_ROWS, N_MOD * d) for m_ in (dmod_0, dmod_a, dmod_b, dmod_c)]
    dw_ada, grads["b_ada"], dctx = _ada_bwd_tp(cc_all, dmods, shard["w_ada"], nb)
    grads["c_ctx"] = dctx[0]
    grads["q_a_norm_w"], grads["kv_a_norm_w"] = dqaw, dkvaw
    grads["q_norm_w"], grads["k_norm_w"] = dqnw[:, :QK_HEAD], dknw[:, :QK_HEAD]
    grads["v_norm_w"], grads["w_s"], grads["b_s"] = dvnw, dws, dbs[:, 0]
    grad_x = dx_lat.reshape(nb, s, d)
    n_small = sum(wts[n].size for n in SMALL)
    rows_s = _round_up(-(-(n_small + 1) // d), 16)
    cm["small"] = jnp.broadcast_to(_flat_rows([grads[n] for n in SMALL] + [loss_part], rows_s, d), (N_CHIPS, rows_s, d))
    group = ("w_in", "w_uq", "w_ukv", "small")
    dw2, got = _mm_tn(g1, dyb1, t_all, "ffn1_dw2", exch=scatter_of(group))
    recv.update(zip(group, got))
    cm["ffn1_w2"] = chip_major("ffn1_w2", dw2)
    dw1, got = _mm_tn(da1, h1, t_all, "ffn1_dw1", exch=scatter_of(("ffn1_w2",)))
    recv["ffn1_w2"] = got[0]
    cm["ffn1_w1"] = chip_major("ffn1_w1", dw1)
    dw3, got = _mm_tn(db1, h1, t_all, "ffn1_dw3", exch=scatter_of(("ffn1_w1",)))
    recv["ffn1_w1"] = got[0]
    cm["ffn1_w3"] = chip_major("ffn1_w3", dw3)
    stepped = {}
    reduced = tuple(n for n in SHARDED if n != "w_ada") + ("small",)
    late = tuple(n for n in reduced if n not in early and n != "ffn1_w3")
    for n in late:
        part[n] = _sum_slots(recv[n], "sum_" + n)
    stepped["w_ada"], got, got_sib = _adamw([dw_ada], wts["w_ada"][0], moms["w_ada"][0], vars_["w_ada"][0],
                                            "adamw_w_ada", exch=scatter_of(("ffn1_w3",)), swap=[part[n] for n in late])
    sib.update(zip(late, got_sib))
    part["ffn1_w3"] = _sum_slots(got[0], "sum_ffn1_w3")
    sib["ffn1_w3"] = _swap_cores([part["ffn1_w3"]], "swap_last")[0]
    for n in reduced[:-1]:
        stepped[n], _ = _adamw([part[n], sib[n]], held(n, wts[n]), held(n, moms[n]), held(n, vars_[n]), "adamw_" + n)
    for n in SHARDED:
        stepped[n] = [unheld(n, a_) for a_ in stepped[n]]
    packed, _ = _adamw([part["small"], sib["small"]], _flat_rows([wts[n] for n in SMALL], rows_s, d),
                       _flat_rows([moms[n] for n in SMALL], rows_s, d), _flat_rows([vars_[n] for n in SMALL], rows_s, d),
                       "adamw_small")
    loss = packed[0].reshape(-1)[n_small]
    for n in SMALL:
        stepped[n] = []
    for a_ in packed:
        flat = a_.reshape(-1)
        off = 0
        for n in SMALL:
            stepped[n].append(flat[off:off + wts[n].size].reshape(wts[n].shape))
            off += wts[n].size
    return (loss, grad_x, *[stepped[n][0] for n in WEIGHTS], *[stepped[n][1] for n in WEIGHTS],
            *[stepped[n][2] for n in WEIGHTS], *[stepped[n][3] for n in WEIGHTS])
```

```python
import functools
import math

import jax
import jax.numpy as jnp
import numpy as np
from jax import lax
from jax.experimental import pallas as pl
from jax.experimental.pallas import tpu as pltpu

F32 = jnp.float32
BF16 = jnp.bfloat16

EPS = 1e-6
N_MOD = 9
HEADS = 8
QK_NOPE, QK_ROPE, V_HEAD = 64, 32, 64
QK_HEAD = QK_NOPE + QK_ROPE
HEAD_PAD = 128
LN2 = math.log(2.0)
SOFTMAX_SCALE = QK_HEAD ** -0.5 / LN2
Q_LORA, KV_LORA = 256, 128
GROUPS, GROUP_DIM, CHUNK = 8, 64, 128
GMLP_W = GROUPS * GROUP_DIM
MLA_W = HEADS * V_HEAD
IN_COLS = 1440
PROJ_COLS = 1536
GRID_W = 64
ROPE_BASE = 10000.0
MOD_ROWS = 16
ADAM_LR, ADAM_B1, ADAM_B2, ADAM_EPS, ADAM_WD, ADAM_STEP = 0.001, 0.9, 0.999, 1e-08, 0.01, 10
N_CHIPS = 4
LANES = 128
V7X_VMEM_LIMIT = 56 * 1024 * 1024
GELU_C = math.sqrt(2.0 / math.pi)

SHARDED = ("w_ada", "ffn1_w1", "ffn1_w3", "ffn1_w2", "w_in", "w_uq", "w_ukv", "w_out", "ffn2_w1", "ffn2_w3", "ffn2_w2")
ROW_SHARDED = ("ffn1_w2", "w_out", "ffn2_w2")
T_WEIGHTS = ("ffn1_w1", "ffn1_w3", "ffn2_w1", "ffn2_w3", "w_in", "w_uq")
FIRST_WEIGHTS = ("ffn1_w1", "ffn1_w3", "ffn1_w2")
MIX_WEIGHTS = ("w_in", "w_uq", "w_ukv", "w_out")
LAST_WEIGHTS = ("ffn2_w1", "ffn2_w3", "ffn2_w2")
SMALL = ("c_ctx", "b_ada", "norm1_w", "norm2_w", "q_a_norm_w", "kv_a_norm_w", "q_norm_w", "k_norm_w", "v_norm_w",
         "w_s", "b_s", "norm3_w")
WEIGHTS = ("c_ctx", "w_ada", "b_ada", "norm1_w", "ffn1_w1", "ffn1_w3", "ffn1_w2", "norm2_w", "w_in", "q_a_norm_w",
           "w_uq", "kv_a_norm_w", "w_ukv", "q_norm_w", "k_norm_w", "v_norm_w", "w_s", "b_s", "w_out", "norm3_w",
           "ffn2_w1", "ffn2_w3", "ffn2_w2")


def _round_up(n, m):
    return (n + m - 1) // m * m


def _div_tile(n, target, mult):
    best = None
    for t in range(mult, min(n, target) + 1, mult):
        if n % t == 0:
            best = t
    return n if best is None else best


def _dot(a, b):
    return lax.dot_general(a, b, (((1,), (0,)), ((), ())), preferred_element_type=F32)


def _dot_nt(a, b):
    return lax.dot_general(a, b, (((1,), (1,)), ((), ())), preferred_element_type=F32)


def _dot_tn(a, b):
    return lax.dot_general(a, b, (((0,), (0,)), ((), ())), preferred_element_type=F32)


def _sigmoid(x):
    return 1.0 / (1.0 + jnp.exp(-x))


def _gelu(x):
    return 0.5 * x * (1.0 + jnp.tanh(GELU_C * (x + 0.044715 * x * x * x)))


def _gelu_grad(x):
    t = jnp.tanh(GELU_C * (x + 0.044715 * x * x * x))
    return 0.5 * (1.0 + t) + 0.5 * x * (1.0 - t * t) * (GELU_C * (1.0 + 3 * 0.044715 * x * x))


def _rope3(x, cos, sin_a, sin_b):
    return x * cos + pltpu.roll(x, 8, 2) * sin_a + pltpu.roll(x, HEAD_PAD - 8, 2) * sin_b


def _rope3_t(d, cos, sin_a, sin_b):
    return d * cos + pltpu.roll(d * sin_a, HEAD_PAD - 8, 2) + pltpu.roll(d * sin_b, 8, 2)


def _group_sum(x, ones_ref):
    hi = x.astype(BF16)
    lo = (x - hi.astype(F32)).astype(BF16)
    return _dot(hi, ones_ref[...]) + _dot(lo, ones_ref[...])


def _params(n_axes):
    return pltpu.CompilerParams(dimension_semantics=("arbitrary",) * n_axes, vmem_limit_bytes=V7X_VMEM_LIMIT)


def _whole(shape):
    nd = len(shape)
    return pl.BlockSpec(shape, lambda *_: (0,) * nd, pipeline_mode=pl.Buffered(1))


def _sds(shape, dtype):
    return jax.ShapeDtypeStruct(shape, dtype)


def _token_tile(s, ctx):
    return _div_tile(math.gcd(s, ctx), 256, CHUNK)


def _other_chips(x, y):
    return [(1 - x, y), (x, 1 - y), (1 - x, 1 - y)]


def _exch_copies(kind, srcs, dsts, send_sems, recv_sems, local_sems, with_arrivals):
    x, y, c = lax.axis_index("x"), lax.axis_index("y"), lax.axis_index("c")
    me = 2 * x + y
    local, sends, arrivals = [], [], []
    for w, (src, dst) in enumerate(zip(srcs, dsts)):
        own = src if kind == "gather" else src.at[me]
        local.append(pltpu.make_async_copy(own, dst.at[me], local_sems.at[w]))
        for k, (px, py) in enumerate(_other_chips(x, y)):
            sem = dict(send_sem=send_sems.at[3 * w + k], recv_sem=recv_sems.at[3 * w + k], device_id=(px, py, c),
                       device_id_type=pl.DeviceIdType.MESH)
            out = src if kind == "gather" else src.at[2 * px + py]
            sends.append(pltpu.make_async_remote_copy(src_ref=out, dst_ref=dst.at[me], **sem))
            if with_arrivals:
                arrivals.append(pltpu.make_async_remote_copy(src_ref=own, dst_ref=dst.at[2 * px + py], **sem))
    return local, sends, arrivals


def _exch_start(kind, srcs, dsts, sems):
    local, sends, _ = _exch_copies(kind, srcs, dsts, *sems, with_arrivals=False)
    for cp in local + sends:
        cp.start()


def _exch_wait(kind, srcs, dsts, sems):
    local, sends, arrivals = _exch_copies(kind, srcs, dsts, *sems, with_arrivals=True)
    for cp in arrivals:
        cp.wait_recv()
    for cp in sends:
        cp.wait_send()
    for cp in local:
        cp.wait()


def _exch_scratch(n):
    return [pltpu.SemaphoreType.DMA((3 * n,)), pltpu.SemaphoreType.DMA((3 * n,)), pltpu.SemaphoreType.DMA((n,))]


def _exch_shapes(kind, arrays):
    return [_sds((N_CHIPS,) + a.shape if kind == "gather" else a.shape, a.dtype) for a in arrays]


def _sibling_copies(srcs, dsts, send_sems, recv_sems):
    x, y, c = lax.axis_index("x"), lax.axis_index("y"), lax.axis_index("c")
    return [pltpu.make_async_remote_copy(
        src_ref=src, dst_ref=dst, send_sem=send_sems.at[w], recv_sem=recv_sems.at[w], device_id=(x, y, 1 - c),
        device_id_type=pl.DeviceIdType.MESH) for w, (src, dst) in enumerate(zip(srcs, dsts))]


def _hosted_call(body, name, grid, in_specs, out_specs, out_shape, operands, scratch=(), exch=None, swap=None):
    n_axes = len(grid)
    if exch is None and swap is None:
        outs = pl.pallas_call(body, name=name, grid=grid, in_specs=list(in_specs), out_specs=list(out_specs),
                              out_shape=list(out_shape), scratch_shapes=list(scratch),
                              compiler_params=_params(n_axes))(*operands)
        return list(outs), []
    kind, arrays = exch if exch is not None else ("scatter", [])
    swaps = list(swap or [])
    n_in, n_out, n_sc, n_ex, n_sw = len(in_specs), len(out_specs), len(scratch), len(arrays), len(swaps)

    def hosted(*refs):
        cin, ein, sin = refs[:n_in], refs[n_in:n_in + n_ex], refs[n_in + n_ex:n_in + n_ex + n_sw]
        o0 = n_in + n_ex + n_sw
        cout, eout, sout = refs[o0:o0 + n_out], refs[o0 + n_out:o0 + n_out + n_ex], refs[o0 + n_out + n_ex:o0 + n_out + n_ex + n_sw]
        rest = refs[o0 + n_out + n_ex + n_sw:]
        csc, sems, swap_sems = rest[:n_sc], rest[n_sc:n_sc + 3], rest[n_sc + 3:]
        first = functools.reduce(jnp.logical_and, [pl.program_id(a) == 0 for a in range(n_axes)])
        last = functools.reduce(jnp.logical_and, [pl.program_id(a) == grid[a] - 1 for a in range(n_axes)])

        @pl.when(first)
        def _():
            if n_ex:
                _exch_start(kind, ein, eout, sems)
            for cp in _sibling_copies(sin, sout, *swap_sems) if n_sw else []:
                cp.start()

        body(*cin, *cout, *csc)

        @pl.when(last)
        def _():
            if n_ex:
                _exch_wait(kind, ein, eout, sems)
            for cp in _sibling_copies(sin, sout, *swap_sems) if n_sw else []:
                cp.wait()

    any_spec = pl.BlockSpec(memory_space=pl.ANY)
    swap_scratch = [pltpu.SemaphoreType.DMA((n_sw,)), pltpu.SemaphoreType.DMA((n_sw,))] if n_sw else []
    outs = pl.pallas_call(
        hosted, name=name, grid=grid, in_specs=list(in_specs) + [any_spec] * (n_ex + n_sw),
        out_specs=list(out_specs) + [any_spec] * (n_ex + n_sw),
        out_shape=list(out_shape) + _exch_shapes(kind, arrays) + [_sds(a.shape, a.dtype) for a in swaps],
        scratch_shapes=list(scratch) + _exch_scratch(max(n_ex, 1)) + swap_scratch, compiler_params=_params(n_axes),
    )(*operands, *arrays, *swaps)
    got = list(outs[n_out:n_out + n_ex])
    return (list(outs[:n_out]), got) if swap is None else (list(outs[:n_out]), got, list(outs[n_out + n_ex:]))


class _TokenTiles:
    def __init__(self, t, tc, tm):
        self.n_lat, self.n_ctx = t // tm, tc // tm
        self.n_all = self.n_lat + self.n_ctx

    def tile(self, i):
        return (i + self.n_lat) % self.n_all if self.n_ctx else i

    def is_lat(self, i):
        return self.tile(i) < self.n_lat

    def row(self, i):
        return (self.tile(i), 0)

    def lat_row(self, i):
        return (jnp.where(self.is_lat(i), self.tile(i), 0), 0) if self.n_ctx else (i, 0)

    def ctx_row(self, i):
        return (jnp.where(self.is_lat(i), self.n_ctx - 1, self.tile(i) - self.n_lat), 0)


def _ffn_fwd(x_lat, x_ctx, mod, nw, w1, w3, w2, k0, s, nb, tm, name, target=None, exch=None):
    t, d = x_lat.shape
    tc = 0 if x_ctx is None else x_ctx.shape[0]
    f = w1.shape[0]
    tiles = _TokenTiles(t, tc, tm)
    n_x = 2 if tc else 1
    n_t = 0 if target is None else 1
    assert not (tc and n_t)

    def body(*refs):
        x_ref = refs[0]
        t_ref = refs[n_x] if n_t else None
        mod_ref, nw_ref, w1_ref, w3_ref, w2_ref, o_ref, a_ref, b_ref, y_ref = refs[n_x + n_t:n_x + n_t + 9]
        i = pl.program_id(0)
        g = jnp.minimum((tiles.tile(i) * tm) // s, nb)
        shift = mod_ref[g, pl.ds(k0, 1), :]
        scale = mod_ref[g, pl.ds(k0 + 1, 1), :]
        gate = mod_ref[g, pl.ds(k0 + 2, 1), :]
        x = jnp.where(tiles.is_lat(i), x_ref[...], refs[1][...]) if tc else x_ref[...]
        r = lax.rsqrt(jnp.mean(x * x, axis=-1, keepdims=True) + EPS)
        hb = ((x * r * nw_ref[...]) * (1.0 + scale) + shift).astype(BF16)
        a = _dot_nt(hb, w1_ref[...])
        b = _dot_nt(hb, w3_ref[...])
        gb = (a * _sigmoid(a) * b).astype(BF16)
        y = _dot(gb, w2_ref[...])
        out = x + (0.5 * gate) * y
        a_ref[...] = a.astype(BF16)
        b_ref[...] = b.astype(BF16)
        y_ref[...] = y.astype(BF16)
        if n_t:
            loss_ref, acc_ref = refs[-2:]

            @pl.when(i == 0)
            def _():
                acc_ref[...] = jnp.zeros_like(acc_ref)

            e = out - t_ref[...]
            o_ref[...] = e * (1.0 / d)
            acc_ref[...] += jnp.sum(e * e, axis=0, keepdims=True)

            @pl.when(i == tiles.n_all - 1)
            def _():
                loss_ref[...] = (0.5 / d) * jnp.sum(acc_ref[...], axis=-1, keepdims=True)
        else:
            o_ref[...] = out

    td = pl.BlockSpec((tm, d), tiles.row)
    tf = pl.BlockSpec((tm, f), tiles.row)
    return _hosted_call(
        body, name, (tiles.n_all,),
        [pl.BlockSpec((tm, d), tiles.lat_row)] + ([pl.BlockSpec((tm, d), tiles.ctx_row)] if tc else []) + [td] * n_t
        + [_whole(mod.shape), _whole(nw.shape), _whole(w1.shape), _whole(w3.shape), _whole(w2.shape)],
        [td, tf, tf, td] + [pl.BlockSpec((1, 1), lambda i: (0, 0))] * n_t,
        [_sds((t + tc, d), F32), _sds((t + tc, f), BF16), _sds((t + tc, f), BF16), _sds((t + tc, d), BF16)]
        + [_sds((1, 1), F32)] * n_t,
        (x_lat,) + ((x_ctx,) if tc else ()) + ((target,) if n_t else ()) + (mod, nw, w1, w3, w2),
        scratch=[pltpu.VMEM((1, d), F32)] * n_t, exch=exch)


def _ffn_bwd(dout, x_lat, x_ctx, a, b, y, mod, nw, w1, w3, w2, k0, s, nb, tm, name, exch=None):
    t, d = x_lat.shape
    tc = 0 if x_ctx is None else x_ctx.shape[0]
    f = w1.shape[0]
    nch = 2 if (f // 2) % LANES == 0 and f % 2 == 0 else 1
    fc = f // nch
    tiles = _TokenTiles(t, tc, tm)
    n_x = 2 if tc else 1

    def body(*refs):
        do_ref, x_ref = refs[0], refs[1]
        (a_ref, b_ref, y_ref, mod_ref, nw_ref, w1_ref, w3_ref, w2_ref,
         dx_ref, h_ref, g_ref, da_ref, db_ref, dy_ref, dmod_ref, dnw_ref) = refs[1 + n_x:]
        i = pl.program_id(0)

        @pl.when(i == 0)
        def _():
            dmod_ref[...] = jnp.zeros_like(dmod_ref)
            dnw_ref[...] = jnp.zeros_like(dnw_ref)

        g = jnp.minimum((tiles.tile(i) * tm) // s, nb)
        shift = mod_ref[g, pl.ds(k0, 1), :]
        scale = mod_ref[g, pl.ds(k0 + 1, 1), :]
        gate = mod_ref[g, pl.ds(k0 + 2, 1), :]
        x = jnp.where(tiles.is_lat(i), x_ref[...], refs[2][...]) if tc else x_ref[...]
        dout_v = do_ref[...]
        r = lax.rsqrt(jnp.mean(x * x, axis=-1, keepdims=True) + EPS)
        xh = x * r
        n = xh * nw_ref[...]
        h_ref[...] = (n * (1.0 + scale) + shift).astype(BF16)
        dyb = ((0.5 * gate) * dout_v).astype(BF16)
        dy_ref[...] = dyb
        dmod_ref[g, pl.ds(k0 + 2, 1), :] += 0.5 * jnp.sum(dout_v * y_ref[...].astype(F32), axis=0, keepdims=True)
        dh = jnp.zeros((tm, d), F32)
        for c in range(nch):
            sl = slice(c * fc, (c + 1) * fc)
            dg = _dot_nt(dyb, w2_ref[sl, :])
            av = a_ref[:, sl].astype(F32)
            bv = b_ref[:, sl].astype(F32)
            sig = _sigmoid(av)
            sa = av * sig
            g_ref[:, sl] = (sa * bv).astype(BF16)
            dab = (dg * bv * (sig * (1.0 + av * (1.0 - sig)))).astype(BF16)
            dbb = (dg * sa).astype(BF16)
            da_ref[:, sl] = dab
            db_ref[:, sl] = dbb
            dh = dh + _dot(dab, w1_ref[sl, :]) + _dot(dbb, w3_ref[sl, :])
        dmod_ref[g, pl.ds(k0, 1), :] += jnp.sum(dh, axis=0, keepdims=True)
        dmod_ref[g, pl.ds(k0 + 1, 1), :] += jnp.sum(dh * n, axis=0, keepdims=True)
        dn = dh * (1.0 + scale)
        dnw_ref[...] += jnp.sum(dn * xh, axis=0, keepdims=True)
        dxh = dn * nw_ref[...]
        dx_ref[...] = dout_v + r * (dxh - xh * jnp.mean(dxh * xh, axis=-1, keepdims=True))

    td = pl.BlockSpec((tm, d), tiles.row)
    tf = pl.BlockSpec((tm, f), tiles.row)
    lat = pl.BlockSpec((tm, d), tiles.lat_row)
    ta = t + tc
    return _hosted_call(
        body, name, (tiles.n_all,),
        [td, lat] + ([pl.BlockSpec((tm, d), tiles.ctx_row)] if tc else [])
        + [tf, tf, td, _whole(mod.shape), _whole(nw.shape), _whole(w1.shape), _whole(w3.shape), _whole(w2.shape)],
        [lat, td, tf, tf, tf, td, pl.BlockSpec(mod.shape, lambda i: (0, 0, 0)), pl.BlockSpec((1, d), lambda i: (0, 0))],
        [_sds((t, d), F32), _sds((ta, d), BF16), _sds((ta, f), BF16), _sds((ta, f), BF16), _sds((ta, f), BF16),
         _sds((ta, d), BF16), _sds(mod.shape, F32), _sds((1, d), F32)],
        (dout, x_lat) + ((x_ctx,) if tc else ()) + (a, b, y, mod, nw, w1, w3, w2), exch=exch)


def _mm_tn(a, b, rows, name, exch=None):
    m = a.shape[1]
    n = b.shape[1]
    bm = _div_tile(m, 1408, LANES)
    bn = _div_tile(n, 1408, LANES)
    bk = _div_tile(rows, 2304, LANES)
    nk = rows // bk

    def body(a_ref, b_ref, o_ref, acc_ref):
        k = pl.program_id(2)

        @pl.when(k == 0)
        def _():
            acc_ref[...] = jnp.zeros_like(acc_ref)

        acc_ref[...] += _dot_tn(a_ref[...], b_ref[...])

        @pl.when(k == nk - 1)
        def _():
            o_ref[...] = acc_ref[...].astype(BF16)

    (out,), got = _hosted_call(
        body, name, (m // bm, n // bn, nk),
        [pl.BlockSpec((bk, bm), lambda i, j, k: (k, i)), pl.BlockSpec((bk, bn), lambda i, j, k: (k, j))],
        [pl.BlockSpec((bm, bn), lambda i, j, k: (i, j))], [_sds((m, n), BF16)], (a, b),
        scratch=[pltpu.VMEM((bm, bn), F32)], exch=exch)
    return out if exch is None else (out, got)


def _mixin_fwd(xs, mod, nw, wp, s, nb, tm):
    t, d = xs.shape

    def body(x_ref, mod_ref, nw_ref, wp_ref, h_ref, p_ref):
        g = jnp.minimum((pl.program_id(0) * tm) // s, nb)
        shift = mod_ref[g, pl.ds(3, 1), :]
        scale = mod_ref[g, pl.ds(4, 1), :]
        x = x_ref[...]
        r = lax.rsqrt(jnp.mean(x * x, axis=-1, keepdims=True) + EPS)
        hb = ((x * r * nw_ref[...]) * (1.0 + scale) + shift).astype(BF16)
        h_ref[...] = hb
        p_ref[...] = _dot_nt(hb, wp_ref[...]).astype(BF16)

    row = lambda i: (i, 0)
    return pl.pallas_call(
        body, name="mixin_fwd", grid=(t // tm,),
        in_specs=[pl.BlockSpec((tm, d), row), _whole(mod.shape), _whole(nw.shape), _whole(wp.shape)],
        out_specs=[pl.BlockSpec((tm, d), row), pl.BlockSpec((tm, PROJ_COLS), row)],
        out_shape=[_sds((t, d), BF16), _sds((t, PROJ_COLS), BF16)], compiler_params=_params(1),
    )(xs, mod, nw, wp)


def _mixin_bwd(dp0, duv, xs, dres, mod, nw, wp, s, nb, tm, swap):
    t_all, d = xs.shape
    nlat = dres.shape[0] // tm

    def body(p0_ref, uv_ref, x_ref, dr_ref, mod_ref, nw_ref, wp_ref, dx_ref, dmod_ref, dnw_ref):
        i = pl.program_id(0)

        @pl.when(i == 0)
        def _():
            dmod_ref[...] = jnp.zeros_like(dmod_ref)
            dnw_ref[...] = jnp.zeros_like(dnw_ref)

        lat = i < nlat
        g = jnp.minimum((i * tm) // s, nb)
        scale = mod_ref[g, pl.ds(4, 1), :]
        dh = _dot(p0_ref[...], wp_ref[0:512, :])
        extra = _dot(uv_ref[...], wp_ref[512:1536, :])
        dh = dh + jnp.where(lat, extra, 0.0)
        x = x_ref[...]
        r = lax.rsqrt(jnp.mean(x * x, axis=-1, keepdims=True) + EPS)
        xh = x * r
        n = xh * nw_ref[...]
        dmod_ref[g, pl.ds(3, 1), :] += jnp.sum(dh, axis=0, keepdims=True)
        dmod_ref[g, pl.ds(4, 1), :] += jnp.sum(dh * n, axis=0, keepdims=True)
        dn = dh * (1.0 + scale)
        dnw_ref[...] += jnp.sum(dn * xh, axis=0, keepdims=True)
        dxh = dn * nw_ref[...]
        dx_ref[...] = jnp.where(lat, dr_ref[...], 0.0) + r * (dxh - xh * jnp.mean(dxh * xh, axis=-1, keepdims=True))

    row = lambda i: (i, 0)
    lrow = lambda i: (jnp.minimum(i, nlat - 1), 0)
    return _hosted_call(
        body, "mixin_bwd", (t_all // tm,),
        [pl.BlockSpec((tm, 512), row), pl.BlockSpec((tm, 1024), lrow), pl.BlockSpec((tm, d), row),
         pl.BlockSpec((tm, d), lrow), _whole(mod.shape), _whole(nw.shape), _whole(wp.shape)],
        [pl.BlockSpec((tm, d), row), pl.BlockSpec(mod.shape, lambda i: (0, 0, 0)), pl.BlockSpec((1, d), lambda i: (0, 0))],
        [_sds((t_all, d), F32), _sds(mod.shape, F32), _sds((1, d), F32)], (dp0, duv, xs, dres, mod, nw, wp), swap=swap)


def _prep_fwd(proj, row0, nb, s, pos0, sk, key0, into, tabs, wq, wk, wv, kvaw, qaw, qnw, knw, tm, with_q, name):
    nblk = s // tm
    n_into = 0 if into is None else 2

    def body(p_ref, cos_ref, sa_ref, sb_ref, wq_ref, wk_ref, wv_ref, kvaw_ref, qaw_ref, qnw_ref, knw_ref, *rest):
        outs, heads_ref = rest[n_into:-1], rest[-1]
        q_ref, k_ref, v_ref = outs if with_q else (None,) + outs
        cos, sin_a, sin_b = cos_ref[...][None], sa_ref[...][None], sb_ref[...][None]

        def normed_roped(w_ref, src, extra, nw_ref, o_ref, post):
            for h in range(HEADS):
                heads_ref[h] = _dot_nt(src, w_ref[h]) if extra is None else _dot(src, w_ref[h])
            xp = heads_ref[...] if extra is None else heads_ref[...] + extra[None]
            r = lax.rsqrt(jnp.sum(xp * xp, axis=-1, keepdims=True) * (1.0 / QK_HEAD) + EPS)
            o_ref[...] = _rope3(xp * r * (nw_ref[...] * post)[None], cos, sin_a, sin_b).astype(BF16)

        ckv = p_ref[:, 0:128].astype(F32)
        rkv = lax.rsqrt(jnp.mean(ckv * ckv, axis=-1, keepdims=True) + EPS)
        ckvb = (ckv * rkv * kvaw_ref[...]).astype(BF16)
        normed_roped(wk_ref, ckvb, p_ref[:, 128:256].astype(F32), knw_ref, k_ref, 1.0)
        for j in range(HEADS // 2):
            v_ref[j] = _dot(ckvb, wv_ref[j]).astype(BF16)
        if with_q:
            cq = p_ref[:, 256:512].astype(F32)
            rq = lax.rsqrt(jnp.mean(cq * cq, axis=-1, keepdims=True) + EPS)
            normed_roped(wq_ref, (cq * rq * qaw_ref[...]).astype(BF16), None, qnw_ref, q_ref, SOFTMAX_SCALE)

    tab = pl.BlockSpec((tm, HEAD_PAD), lambda i: (pos0 + i % nblk, 0))
    qspec = pl.BlockSpec((None, HEADS, tm, HEAD_PAD), lambda i: (i // nblk, 0, i % nblk, 0))
    kspec = pl.BlockSpec((None, HEADS, tm, HEAD_PAD), lambda i: (i // nblk, 0, key0 + i % nblk, 0))
    vspec = pl.BlockSpec((None, HEADS // 2, tm, HEAD_PAD), lambda i: (i // nblk, 0, key0 + i % nblk, 0))
    qshape = _sds((nb, HEADS, s, HEAD_PAD), BF16)
    kshape = _sds((nb, HEADS, sk, HEAD_PAD), BF16)
    vshape = _sds((nb, HEADS // 2, sk, HEAD_PAD), BF16)
    n_q = 1 if with_q else 0
    return pl.pallas_call(
        body, name=name, grid=(nb * nblk,),
        in_specs=[pl.BlockSpec((tm, 512), lambda i: (row0 + i, 0)), tab, tab, tab, _whole(wq.shape), _whole(wk.shape),
                  _whole(wv.shape), _whole(kvaw.shape), _whole(qaw.shape), _whole(qnw.shape), _whole(knw.shape)]
        + [pl.BlockSpec(memory_space=pl.ANY)] * n_into,
        out_specs=([qspec] if with_q else []) + [kspec, vspec],
        out_shape=([qshape] if with_q else []) + [kshape, vshape],
        scratch_shapes=[pltpu.VMEM((HEADS, tm, HEAD_PAD), F32)],
        input_output_aliases={11: n_q, 12: n_q + 1} if n_into else {}, compiler_params=_params(1),
    )(proj, *tabs, wq, wk, wv, kvaw, qaw, qnw, knw, *(into or ()))


def _prep_bwd(proj, row0, nb, s, pos0, key0, dp_rows, dp_into, tabs, wq, wk, wv, kvaw, qaw, qnw, knw, dq, dk, dv, init, tm,
              name):
    nblk = s // tm
    with_q = dq is not None
    n_init = 0 if init is None else len(init)
    n_into = 0 if dp_into is None else 1

    def body(*refs):
        p_ref, cos_ref, sa_ref, sb_ref, wq_ref, wk_ref, wv_ref, kvaw_ref, qaw_ref, qnw_ref, knw_ref = refs[:11]
        rest = list(refs[11:])
        dq_ref = rest.pop(0) if with_q else None
        dk_ref, dv_ref = rest.pop(0), rest.pop(0)
        init_refs = [rest.pop(0) for _ in range(n_init)]
        if n_into:
            rest.pop(0)
        dp_ref = rest.pop(0)
        if with_q:
            dwq_ref, dqaw_ref, dqnw_ref = rest.pop(0), rest.pop(0), rest.pop(0)
        dwk_ref, dwv_ref, dkvaw_ref, dknw_ref, heads_ref, dhb_ref, dkr_ref = rest
        accs = [dwk_ref, dwv_ref, dkvaw_ref, dknw_ref]

        @pl.when(pl.program_id(0) == 0)
        def _():
            for k, acc in enumerate(accs):
                acc[...] = init_refs[k][...] if n_init else jnp.zeros_like(acc)
            if with_q:
                dwq_ref[...] = jnp.zeros_like(dwq_ref)
                dqaw_ref[...] = jnp.zeros_like(dqaw_ref)
                dqnw_ref[...] = jnp.zeros_like(dqnw_ref)

        cos, sin_a, sin_b = cos_ref[...][None], sa_ref[...][None], sb_ref[...][None]
        lane = lax.broadcasted_iota(jnp.int32, (tm, HEAD_PAD), 1)
        rope_lanes = (lane >= QK_NOPE) & (lane < QK_HEAD)

        def heads_bwd(w_ref, src, extra, nw_ref, d_ref, dnw_ref, dw_ref, post):
            w_t = extra is None
            for h in range(HEADS):
                heads_ref[h] = _dot_nt(src, w_ref[h]) if w_t else _dot(src, w_ref[h])
            xp = heads_ref[...] if extra is None else heads_ref[...] + extra[None]
            r = lax.rsqrt(jnp.sum(xp * xp, axis=-1, keepdims=True) * (1.0 / QK_HEAD) + EPS)
            xh = xp * r
            dn = _rope3_t(d_ref[...].astype(F32), cos, sin_a, sin_b)
            dnw_ref[...] += post * jnp.sum(jnp.sum(dn * xh, axis=0), axis=0, keepdims=True)
            dxh = dn * (nw_ref[...] * post)[None]
            dxp = r * (dxh - xh * (jnp.sum(dxh * xh, axis=-1, keepdims=True) * (1.0 / QK_HEAD)))
            dhb_ref[...] = dxp.astype(BF16)
            dsrc = jnp.zeros((tm, src.shape[1]), F32)
            for h in range(HEADS):
                dsrc = dsrc + (_dot(dhb_ref[h], w_ref[h]) if w_t else _dot_nt(dhb_ref[h], w_ref[h]))
                dw_ref[h] += _dot_tn(src, dhb_ref[h])
            return dsrc, jnp.sum(dxp, axis=0)

        ckv = p_ref[:, 0:128].astype(F32)
        rkv = lax.rsqrt(jnp.mean(ckv * ckv, axis=-1, keepdims=True) + EPS)
        ckvh = ckv * rkv
        ckvb = (ckvh * kvaw_ref[...]).astype(BF16)
        for h in range(HEADS):
            dkr_ref[h] = dk_ref[h].astype(F32).T
        dckv, dkp_sum = heads_bwd(wk_ref, ckvb, p_ref[:, 128:256].astype(F32), knw_ref, dkr_ref, dknw_ref, dwk_ref,
                                  1.0)
        for j in range(HEADS // 2):
            dvb = dv_ref[j].astype(F32).T.astype(BF16)
            dckv = dckv + _dot_nt(dvb, wv_ref[j])
            dwv_ref[j] += _dot_tn(ckvb, dvb)
        dkvaw_ref[...] += jnp.sum(dckv * ckvh, axis=0, keepdims=True)
        dch = dckv * kvaw_ref[...]
        dp_ref[:, 0:128] = (rkv * (dch - ckvh * jnp.mean(dch * ckvh, axis=-1, keepdims=True))).astype(BF16)
        dp_ref[:, 128:256] = jnp.where(rope_lanes, dkp_sum, 0.0).astype(BF16)
        if with_q:
            cq = p_ref[:, 256:512].astype(F32)
            rq = lax.rsqrt(jnp.mean(cq * cq, axis=-1, keepdims=True) + EPS)
            cqh = cq * rq
            cqb = (cqh * qaw_ref[...]).astype(BF16)
            dcq, _ = heads_bwd(wq_ref, cqb, None, qnw_ref, dq_ref, dqnw_ref, dwq_ref, SOFTMAX_SCALE)
            dqaw_ref[...] += jnp.sum(dcq * cqh, axis=0, keepdims=True)
            dqc = dcq * qaw_ref[...]
            dp_ref[:, 256:512] = (rq * (dqc - cqh * jnp.mean(dqc * cqh, axis=-1, keepdims=True))).astype(BF16)
        else:
            dp_ref[:, 256:512] = jnp.zeros((tm, Q_LORA), BF16)

    tab = pl.BlockSpec((tm, HEAD_PAD), lambda i: (pos0 + i % nblk, 0))
    qspec = pl.BlockSpec((None, HEADS, tm, HEAD_PAD), lambda i: (i // nblk, 0, i % nblk, 0))
    kspec = pl.BlockSpec((None, HEADS, HEAD_PAD, tm), lambda i: (i // nblk, 0, 0, key0 + i % nblk))
    vspec = pl.BlockSpec((None, HEADS // 2, HEAD_PAD, tm), lambda i: (i // nblk, 0, 0, key0 + i % nblk))

    def acc_spec(shape):
        nd = len(shape)
        return pl.BlockSpec(shape, lambda i: (0,) * nd)

    acc_shapes = [(HEADS, KV_LORA, HEAD_PAD), (HEADS // 2, KV_LORA, HEAD_PAD), (1, KV_LORA), (1, HEAD_PAD)]
    q_shapes = [(HEADS, Q_LORA, HEAD_PAD), (1, Q_LORA), (1, HEAD_PAD)] if with_q else []
    out_shapes = [(dp_rows, 512)] + q_shapes + acc_shapes
    n_before = 11 + (1 if with_q else 0) + 2 + n_init
    return pl.pallas_call(
        body, name=name, grid=(nb * nblk,),
        in_specs=[pl.BlockSpec((tm, 512), lambda i: (row0 + i, 0)), tab, tab, tab, _whole(wq.shape), _whole(wk.shape),
                  _whole(wv.shape), _whole(kvaw.shape), _whole(qaw.shape), _whole(qnw.shape), _whole(knw.shape)]
        + ([qspec] if with_q else []) + [kspec, vspec] + [_whole(a.shape) for a in (init or [])]
        + [pl.BlockSpec(memory_space=pl.ANY)] * n_into,
        out_specs=[pl.BlockSpec((tm, 512), lambda i: (row0 + i, 0))] + [acc_spec(sh) for sh in q_shapes + acc_shapes],
        out_shape=[_sds(out_shapes[0], BF16)] + [_sds(sh, F32) for sh in out_shapes[1:]],
        scratch_shapes=[pltpu.VMEM((HEADS, tm, HEAD_PAD), F32), pltpu.VMEM((HEADS, tm, HEAD_PAD), BF16),
                        pltpu.VMEM((HEADS, tm, HEAD_PAD), F32)],
        input_output_aliases={n_before: 0} if n_into else {}, compiler_params=_params(1),
    )(proj, *tabs, wq, wk, wv, kvaw, qaw, qnw, knw, *([dq] if with_q else []), dk, dv, *(init or []),
      *([dp_into] if n_into else []))


def _attn_fwd(q, k, v, tq, exch=None):
    nb, _, s, _ = q.shape
    sk = k.shape[2]
    nq = s // tq

    def body(q_ref, k_ref, v_ref, o_ref, lse_ref, vext_ref):
        @pl.when(pl.program_id(2) == 0)
        def _():
            vext_ref[:, 0:HEAD_PAD] = v_ref[...]
            vext_ref[:, HEAD_PAD:2 * HEAD_PAD] = jnp.ones((sk, HEAD_PAD), BF16)

        lane = lax.broadcasted_iota(jnp.int32, (tq, HEAD_PAD), 1)
        outs = []
        for hh in range(2):
            sc = _dot_nt(q_ref[hh], k_ref[hh])
            m = jnp.max(sc, axis=-1, keepdims=True)
            pv = _dot(jnp.exp2(sc - m).astype(BF16), vext_ref[...])
            l = pv[:, HEAD_PAD:HEAD_PAD + 1]
            outs.append(pv[:, 0:HEAD_PAD] / l)
            lse_ref[hh] = m + jnp.log2(l)
        o_ref[...] = jnp.where(lane < V_HEAD, outs[0], outs[1]).astype(BF16)

    (o, lse), got = _hosted_call(
        body, "attn_fwd", (nb, HEADS // 2, nq),
        [pl.BlockSpec((None, 2, tq, HEAD_PAD), lambda b, j, i: (b, j, i, 0)),
         pl.BlockSpec((None, 2, sk, HEAD_PAD), lambda b, j, i: (b, j, 0, 0)),
         pl.BlockSpec((None, None, sk, HEAD_PAD), lambda b, j, i: (b, j, 0, 0))],
        [pl.BlockSpec((tq, HEAD_PAD), lambda b, j, i: (b * nq + i, j)),
         pl.BlockSpec((None, 2, tq, 1), lambda b, j, i: (b, j, i, 0))],
        [_sds((nb * s, MLA_W + GMLP_W), BF16), _sds((nb, HEADS, s, 1), F32)], (q, k, v),
        scratch=[pltpu.VMEM((sk, 2 * HEAD_PAD), BF16)], exch=exch)
    return o, lse, got


def _attn_bwd(q, k, v, do, o, lse, tq, exch=None):
    nb, _, s, _ = q.shape
    sk = k.shape[2]
    nq = s // tq

    def body(q_ref, k_ref, v_ref, do_ref, o_ref, lse_ref, dq_ref, dk_out, dv_out, dkt_ref, dvt_ref):
        @pl.when(pl.program_id(2) == 0)
        def _():
            dkt_ref[...] = jnp.zeros_like(dkt_ref)
            dvt_ref[...] = jnp.zeros_like(dvt_ref)

        lane = lax.broadcasted_iota(jnp.int32, (tq, HEAD_PAD), 1)
        dov = do_ref[...]
        prod = dov.astype(F32) * o_ref[...].astype(F32)
        for hh in range(2):
            mine = (lane < V_HEAD) if hh == 0 else (lane >= V_HEAD)
            doh = jnp.where(mine, dov, jnp.zeros_like(dov))
            delta = jnp.sum(jnp.where(mine, prod, 0.0), axis=-1, keepdims=True)
            qh = q_ref[hh]
            q_ln2 = (qh.astype(F32) * LN2).astype(BF16)
            kv = k_ref[hh]
            p = jnp.exp2(_dot_nt(qh, kv) - lse_ref[hh])
            u = (p * (_dot_nt(doh, v_ref[...]) - delta)).astype(BF16)
            dq_ref[hh] = (_dot(u, kv) * LN2).astype(BF16)
            dkt_ref[hh] += _dot_tn(q_ln2, u)
            dvt_ref[...] += _dot_tn(doh, p.astype(BF16))

        @pl.when(pl.program_id(2) == nq - 1)
        def _():
            dk_out[...] = dkt_ref[...].astype(BF16)
            dv_out[...] = dvt_ref[...].astype(BF16)

    qspec = pl.BlockSpec((None, 2, tq, HEAD_PAD), lambda b, j, i: (b, j, i, 0))
    kspec = pl.BlockSpec((None, 2, sk, HEAD_PAD), lambda b, j, i: (b, j, 0, 0))
    vspec = pl.BlockSpec((None, None, sk, HEAD_PAD), lambda b, j, i: (b, j, 0, 0))
    ospec = pl.BlockSpec((tq, HEAD_PAD), lambda b, j, i: (b * nq + i, j))
    return _hosted_call(
        body, "attn_bwd", (nb, HEADS // 2, nq),
        [qspec, kspec, vspec, ospec, ospec, pl.BlockSpec((None, 2, tq, 1), lambda b, j, i: (b, j, i, 0))],
        [qspec, pl.BlockSpec((None, 2, HEAD_PAD, sk), lambda b, j, i: (b, j, 0, 0)),
         pl.BlockSpec((None, None, HEAD_PAD, sk), lambda b, j, i: (b, j, 0, 0))],
        [_sds(q.shape, BF16), _sds((nb, HEADS, HEAD_PAD, sk), BF16), _sds((nb, HEADS // 2, HEAD_PAD, sk), BF16)],
        (q, k, v, do, o, lse), scratch=[pltpu.VMEM((2, HEAD_PAD, sk), F32), pltpu.VMEM((HEAD_PAD, sk), F32)], exch=exch)


def _group_masks(rows):
    lane = lax.broadcasted_iota(jnp.int32, (rows, GMLP_W), 1)
    return [(lane >= g * GROUP_DIM) & (lane < (g + 1) * GROUP_DIM) for g in range(GROUPS)]


def _gmlp_fwd(proj, mixcat, wcat, bias, vnw, ones, tm):
    t = mixcat.shape[0]

    def body(u_ref, v_ref, wcat_ref, bias_ref, vnw_ref, ones_ref, _, o_ref):
        masks = _group_masks(CHUNK)
        gv = _gelu(v_ref[...].astype(F32))
        rv = lax.rsqrt(_group_sum(gv * gv, ones_ref) * (1.0 / GROUP_DIM) + EPS)
        vnb = (gv * rv * vnw_ref[...]).astype(BF16)
        for c in range(tm // CHUNK):
            rows = slice(c * CHUNK, (c + 1) * CHUNK)
            vc = vnb[rows]
            stack = jnp.concatenate([jnp.where(m, vc, jnp.zeros_like(vc)) for m in masks], axis=0)
            sp = _dot(wcat_ref[...], stack) + bias_ref[...]
            o_ref[rows, :] = (_gelu(u_ref[rows, :].astype(F32)) * sp).astype(BF16)

    return pl.pallas_call(
        body, name="gmlp_fwd", grid=(t // tm,),
        in_specs=[pl.BlockSpec((tm, GMLP_W), lambda i: (i, 1)), pl.BlockSpec((tm, GMLP_W), lambda i: (i, 2)),
                  _whole(wcat.shape), _whole(bias.shape), _whole(vnw.shape), _whole(ones.shape),
                  pl.BlockSpec(memory_space=pl.ANY)],
        out_specs=pl.BlockSpec((tm, GMLP_W), lambda i: (i, 1)),
        out_shape=_sds(mixcat.shape, BF16), input_output_aliases={6: 0}, compiler_params=_params(1),
    )(proj, proj, wcat, bias, vnw, ones, mixcat)


def _gmlp_bwd(proj, dsg, wcat, wcat_t, bias, vnw, ones, tm):
    t = dsg.shape[0]

    def body(u_ref, v_ref, dsg_ref, wcat_ref, wcatt_ref, bias_ref, vnw_ref, ones_ref,
             duv_ref, dws_ref, dbs_ref, dvnw_ref):
        @pl.when(pl.program_id(0) == 0)
        def _():
            dws_ref[...] = jnp.zeros_like(dws_ref)
            dbs_ref[...] = jnp.zeros_like(dbs_ref)
            dvnw_ref[...] = jnp.zeros_like(dvnw_ref)

        masks = _group_masks(CHUNK)
        v = v_ref[...].astype(F32)
        gv = _gelu(v)
        rv = lax.rsqrt(_group_sum(gv * gv, ones_ref) * (1.0 / GROUP_DIM) + EPS)
        xh = gv * rv
        vnb = (xh * vnw_ref[...]).astype(BF16)
        dvn_parts = []
        for c in range(tm // CHUNK):
            rows = slice(c * CHUNK, (c + 1) * CHUNK)
            vc = vnb[rows]
            stack = jnp.concatenate([jnp.where(m, vc, jnp.zeros_like(vc)) for m in masks], axis=0)
            sp = _dot(wcat_ref[...], stack) + bias_ref[...]
            u = u_ref[rows, :].astype(F32)
            dsg_c = dsg_ref[rows, :]
            duv_ref[rows, 0:GMLP_W] = (dsg_c * sp * _gelu_grad(u)).astype(BF16)
            ds = dsg_c * _gelu(u)
            dstack = jnp.concatenate([jnp.where(m, ds, 0.0) for m in masks], axis=0)
            dbs_ref[...] += jnp.broadcast_to(jnp.sum(dstack, axis=-1, keepdims=True), dbs_ref.shape)
            dstb = dstack.astype(BF16)
            dvn_parts.append(_dot(wcatt_ref[...], dstb))
            dws_ref[...] += _dot_nt(dstb, vc)
        dvn = jnp.concatenate(dvn_parts, axis=0) if len(dvn_parts) > 1 else dvn_parts[0]
        dvnw_ref[...] += jnp.sum(dvn * xh, axis=0, keepdims=True)
        dxh = dvn * vnw_ref[...]
        gm = _group_sum(dxh * xh, ones_ref) * (1.0 / GROUP_DIM)
        duv_ref[:, GMLP_W:2 * GMLP_W] = (rv * (dxh - xh * gm) * _gelu_grad(v)).astype(BF16)

    row = pl.BlockSpec((tm, GMLP_W), lambda i: (i, 0))
    return pl.pallas_call(
        body, name="gmlp_bwd", grid=(t // tm,),
        in_specs=[pl.BlockSpec((tm, GMLP_W), lambda i: (i, 1)), pl.BlockSpec((tm, GMLP_W), lambda i: (i, 2)), row,
                  _whole(wcat.shape), _whole(wcat_t.shape), _whole(bias.shape), _whole(vnw.shape), _whole(ones.shape)],
        out_specs=[pl.BlockSpec((tm, 2 * GMLP_W), lambda i: (i, 0)), pl.BlockSpec((GROUPS * CHUNK, CHUNK), lambda i: (0, 0)),
                   pl.BlockSpec((GROUPS * CHUNK, CHUNK), lambda i: (0, 0)), pl.BlockSpec((1, GMLP_W), lambda i: (0, 0))],
        out_shape=[_sds((t, 2 * GMLP_W), BF16), _sds((GROUPS * CHUNK, CHUNK), F32), _sds((GROUPS * CHUNK, CHUNK), F32),
                   _sds((1, GMLP_W), F32)],
        compiler_params=_params(1),
    )(proj, proj, dsg, wcat, wcat_t, bias, vnw, ones)


def _mixout_fwd(mixcat, xs, mod, wout, s, tm):
    t, width = mixcat.shape
    d = xs.shape[1]

    def body(cat_ref, x_ref, mod_ref, w_ref, x2_ref, mix_ref):
        g = (pl.program_id(0) * tm) // s
        gate = mod_ref[g, pl.ds(5, 1), :]
        mix = _dot(cat_ref[...], w_ref[...])
        x2_ref[...] = x_ref[...] + gate * mix
        mix_ref[...] = mix.astype(BF16)

    row = lambda i: (i, 0)
    return pl.pallas_call(
        body, name="mixout_fwd", grid=(t // tm,),
        in_specs=[pl.BlockSpec((tm, width), row), pl.BlockSpec((tm, d), row), _whole(mod.shape), _whole(wout.shape)],
        out_specs=[pl.BlockSpec((tm, d), row), pl.BlockSpec((tm, d), row)],
        out_shape=[_sds((t, d), F32), _sds((t, d), BF16)], compiler_params=_params(1),
    )(mixcat, xs, mod, wout)


def _mixout_bwd(dx2, mix, mod, wout, s, tm):
    t, d = dx2.shape

    def body(dx_ref, mix_ref, mod_ref, w_ref, dmix_ref, do_ref, dsg_ref, dmod_ref):
        i = pl.program_id(0)

        @pl.when(i == 0)
        def _():
            dmod_ref[...] = jnp.zeros_like(dmod_ref)

        g = (i * tm) // s
        gate = mod_ref[g, pl.ds(5, 1), :]
        dx = dx_ref[...]
        dmod_ref[g, pl.ds(5, 1), :] += jnp.sum(dx * mix_ref[...].astype(F32), axis=0, keepdims=True)
        dmb = (gate * dx).astype(BF16)
        dmix_ref[...] = dmb
        do_ref[...] = _dot_nt(dmb, w_ref[0:MLA_W, :]).astype(BF16)
        dsg_ref[...] = _dot_nt(dmb, w_ref[MLA_W:MLA_W + GMLP_W, :])

    row = lambda i: (i, 0)
    return pl.pallas_call(
        body, name="mixout_bwd", grid=(t // tm,),
        in_specs=[pl.BlockSpec((tm, d), row), pl.BlockSpec((tm, d), row), _whole(mod.shape), _whole(wout.shape)],
        out_specs=[pl.BlockSpec((tm, d), row), pl.BlockSpec((tm, MLA_W), row), pl.BlockSpec((tm, GMLP_W), row),
                   pl.BlockSpec(mod.shape, lambda i: (0, 0, 0))],
        out_shape=[_sds((t, d), BF16), _sds((t, MLA_W), BF16), _sds((t, GMLP_W), F32), _sds(mod.shape, F32)],
        compiler_params=_params(1),
    )(dx2, mix, mod, wout)


def _swap_cores(parts, name):
    n = len(parts)

    def body(*refs):
        srcs, outs, send_sems, recv_sems = refs[:n], refs[n:2 * n], refs[2 * n], refs[2 * n + 1]
        x, y, c = lax.axis_index("x"), lax.axis_index("y"), lax.axis_index("c")
        copies = [pltpu.make_async_remote_copy(
            src_ref=srcs[w], dst_ref=outs[w], send_sem=send_sems.at[w], recv_sem=recv_sems.at[w],
            device_id=(x, y, 1 - c), device_id_type=pl.DeviceIdType.MESH) for w in range(n)]
        for cp in copies:
            cp.start()
        for cp in copies:
            cp.wait()

    any_spec = pl.BlockSpec(memory_space=pl.ANY)
    return pl.pallas_call(
        body, name=name, in_specs=[any_spec] * n, out_specs=[any_spec] * n,
        out_shape=[_sds(p.shape, p.dtype) for p in parts],
        scratch_shapes=[pltpu.SemaphoreType.DMA((n,)), pltpu.SemaphoreType.DMA((n,))],
    )(*parts)


def _row_tile(r, c, mult):
    return _div_tile(r, max(mult, (1 << 18) // c), mult)


def _sum_slots(recv, name):
    _, r, c = recv.shape
    tr = _row_tile(r, c, 16)

    def body(r_ref, o_ref):
        f = lambda k: r_ref[k].astype(F32)
        o_ref[...] = ((f(0) + f(1)) + f(2)) + f(3)

    return pl.pallas_call(
        body, name=name, grid=(r // tr,),
        in_specs=[pl.BlockSpec((N_CHIPS, tr, c), lambda i: (0, i, 0))],
        out_specs=pl.BlockSpec((tr, c), lambda i: (i, 0)),
        out_shape=_sds((r, c), F32), compiler_params=_params(1),
    )(recv)


def _adamw(parts, w, m, v, name, exch=None, swap=None):
    r, wd = w.shape
    tr = _row_tile(r, wd, 8)
    c1 = 1.0 / (1.0 - ADAM_B1 ** ADAM_STEP)
    c2 = 1.0 / (1.0 - ADAM_B2 ** ADAM_STEP)
    n_p = len(parts)

    def body(*refs):
        p_refs = refs[:n_p]
        w_ref, m_ref, v_ref, g_ref, d_ref, nm_ref, nv_ref = refs[n_p:]
        g = p_refs[0][...]
        for p_ref in p_refs[1:]:
            g = g + p_ref[...]
        nm = ADAM_B1 * m_ref[...] + (1.0 - ADAM_B1) * g
        nv = ADAM_B2 * v_ref[...] + (1.0 - ADAM_B2) * (g * g)
        g_ref[...] = g
        nm_ref[...] = nm
        nv_ref[...] = nv
        d_ref[...] = -ADAM_LR * ((nm * c1) / (jnp.sqrt(nv * c2) + ADAM_EPS) + ADAM_WD * w_ref[...])

    spec = pl.BlockSpec((tr, wd), lambda i: (i, 0))
    return _hosted_call(body, name, (r // tr,), [spec] * (n_p + 3), [spec] * 4, [_sds((r, wd), F32)] * 4,
                        (*parts, w, m, v), exch=exch, swap=swap)


def _all_peers(x, y, c):
    flips = [(dx, dy, dc) for dx in (0, 1) for dy in (0, 1) for dc in (0, 1)][1:]
    return [(1 - x if dx else x, 1 - y if dy else y, 1 - c if dc else c) for dx, dy, dc in flips]


def _first_exchange(shards, later, cc, w, b):
    n_w, n_l = len(shards), len(later)
    n = w.shape[1]

    def body(*refs):
        src32, later_in, (cc_ref, w_ref, b_ref) = refs[:n_w], refs[n_w:n_w + n_l], refs[n_w + n_l:n_w + n_l + 3]
        o0 = n_w + n_l + 3
        outs, (all_ref, tab_ref), later_out = refs[o0:o0 + n_w], refs[o0 + n_w:o0 + n_w + 2], refs[o0 + n_w + 2:o0 + n_w + 2 + n_l]
        s0 = o0 + n_w + 2 + n_l
        srcs = refs[s0:s0 + n_w]
        (part_ref, ici_send, ici_recv, d2d_send, d2d_recv, local_sems, cc_send, cc_recv, tab_send,
         tab_recv) = refs[s0 + n_w:]
        for wi in range(n_w):
            srcs[wi][...] = src32[wi][...].astype(BF16)
        x, y, c = lax.axis_index("x"), lax.axis_index("y"), lax.axis_index("c")
        chip, dev = 2 * x + y, 4 * x + 2 * y + c
        chips = _other_chips(x, y)
        peers = _all_peers(x, y, c)

        def half(wi, which):
            hr = shards[wi].shape[0] // 2
            return pl.ds(pl.multiple_of(which * hr, 16), hr)

        def over_ici(wi, k, arriving):
            px, py = chips[k]
            slot = 2 * px + py if arriving else chip
            return pltpu.make_async_remote_copy(
                src_ref=srcs[wi].at[half(wi, c)], dst_ref=outs[wi].at[slot, half(wi, c)],
                send_sem=ici_send.at[3 * wi + k], recv_sem=ici_recv.at[3 * wi + k], device_id=(px, py, c),
                device_id_type=pl.DeviceIdType.MESH)

        def to_sibling(wi, k, arriving):
            px, py = chips[k]
            rows = half(wi, 1 - c if arriving else c)
            return pltpu.make_async_remote_copy(
                src_ref=outs[wi].at[2 * px + py, rows], dst_ref=outs[wi].at[2 * px + py, rows],
                send_sem=d2d_send.at[3 * wi + k], recv_sem=d2d_recv.at[3 * wi + k], device_id=(x, y, 1 - c),
                device_id_type=pl.DeviceIdType.MESH)

        def cc_copy(k, peer, slot):
            return pltpu.make_async_remote_copy(
                src_ref=cc_ref, dst_ref=all_ref.at[slot], send_sem=cc_send.at[k], recv_sem=cc_recv.at[k],
                device_id=peer, device_id_type=pl.DeviceIdType.MESH)

        def rows_of(px, py):
            return part_ref.at[pl.ds(pl.multiple_of((4 * px + 2 * py + c) * MOD_ROWS, MOD_ROWS), MOD_ROWS)]

        def tab_copy(k, px, py, slot):
            return pltpu.make_async_remote_copy(
                src_ref=rows_of(px, py), dst_ref=tab_ref.at[slot], send_sem=tab_send.at[k], recv_sem=tab_recv.at[k],
                device_id=(px, py, c), device_id_type=pl.DeviceIdType.MESH)

        local = [pltpu.make_async_copy(srcs[wi], outs[wi].at[chip], local_sems.at[wi]) for wi in range(n_w)]
        for cp in local:
            cp.start()
        pairs = [(wi, k) for wi in range(n_w) for k in range(3)]
        for wi, k in pairs:
            over_ici(wi, k, False).start()
        for k, peer in enumerate(peers):
            cc_copy(k, peer, dev).start()
        all_ref[dev] = cc_ref[...]
        for k, (px, py, pc) in enumerate(peers):
            cc_copy(k, (px, py, pc), 4 * px + 2 * py + pc).wait_recv()
        cv = all_ref[...].reshape(8 * MOD_ROWS, cc.shape[1])
        part_ref[...] = _dot((cv * _sigmoid(cv)).astype(BF16), w_ref[...]) + b_ref[...]
        for k, (px, py) in enumerate(chips):
            tab_copy(k, px, py, chip).start()
        tab_ref[chip] = rows_of(x, y)[...]
        for k, (px, py) in enumerate(chips):
            tab_copy(k, px, py, 2 * px + py).wait_recv()
        for j in range(n_l):
            later_out[j][...] = later_in[j][...].astype(BF16)
        for wi, k in pairs:
            over_ici(wi, k, True).wait_recv()
            to_sibling(wi, k, False).start()
        for wi, k in pairs:
            to_sibling(wi, k, True).wait_recv()
        for wi, k in pairs:
            over_ici(wi, k, False).wait_send()
            to_sibling(wi, k, False).wait_send()
        for k, peer in enumerate(peers):
            cc_copy(k, peer, dev).wait_send()
        for k, (px, py) in enumerate(chips):
            tab_copy(k, px, py, chip).wait_send()
        for cp in local:
            cp.wait()

    any_spec = pl.BlockSpec(memory_space=pl.ANY)
    vmem = pl.BlockSpec(memory_space=pltpu.VMEM)
    sems3 = pltpu.SemaphoreType.DMA((3 * n_w,))
    got = pl.pallas_call(
        body, name="first_exchange", in_specs=[vmem] * (n_w + n_l + 3),
        out_specs=[any_spec] * n_w + [vmem] * (2 + n_l),
        out_shape=[_sds((N_CHIPS,) + a.shape, BF16) for a in shards]
        + [_sds((8,) + cc.shape, F32), _sds((N_CHIPS, MOD_ROWS, n), F32)] + [_sds(a.shape, BF16) for a in later],
        scratch_shapes=[pltpu.VMEM(a.shape, BF16) for a in shards]
        + [pltpu.VMEM((8 * MOD_ROWS, n), F32), sems3, sems3, sems3, sems3, pltpu.SemaphoreType.DMA((n_w,)),
           pltpu.SemaphoreType.DMA((7,)), pltpu.SemaphoreType.DMA((7,)), pltpu.SemaphoreType.DMA((3,)),
           pltpu.SemaphoreType.DMA((3,))],
        compiler_params=pltpu.CompilerParams(vmem_limit_bytes=V7X_VMEM_LIMIT),
    )(*shards, *later, cc, w, b)
    return got[:n_w], got[n_w], got[n_w + 1], got[n_w + 2:]


def _ada_bwd_tp(cc_all, dmods, w, ctx_row):
    d, n = w.shape

    def body(cc_ref, m0, m1, m2, m3, w_ref, dw_ref, db_ref, dctx_ref, stage_ref, all_ref, send_sems, recv_sems):
        x, y, c = lax.axis_index("x"), lax.axis_index("y"), lax.axis_index("c")
        me = 4 * x + 2 * y + c
        dsum = m0[...] + m1[...] + m2[...] + m3[...]
        db_ref[...] = jnp.sum(dsum, axis=0, keepdims=True)
        for j in range(N_CHIPS):
            stage_ref[j] = dsum[:, j * n:(j + 1) * n]

        def copy(k, peer, slot):
            px, py, _ = peer
            return pltpu.make_async_remote_copy(
                src_ref=stage_ref.at[2 * px + py], dst_ref=all_ref.at[slot], send_sem=send_sems.at[k],
                recv_sem=recv_sems.at[k], device_id=peer, device_id_type=pl.DeviceIdType.MESH)

        peers = _all_peers(x, y, c)
        for k, peer in enumerate(peers):
            copy(k, peer, me).start()
        all_ref[me] = stage_ref[2 * x + y]
        for k, (px, py, pc) in enumerate(peers):
            copy(k, (px, py, pc), 4 * px + 2 * py + pc).wait_recv()
        for k, peer in enumerate(peers):
            copy(k, peer, me).wait_send()
        cv = cc_ref[...]
        sig = _sigmoid(cv)
        dmb = all_ref[...].reshape(8 * MOD_ROWS, n).astype(BF16)
        dw_ref[...] = _dot_tn((cv * sig).astype(BF16), dmb)
        dsc = _dot_nt(dmb, w_ref[...])
        dctx = dsc[ctx_row:ctx_row + 1, :]
        for dev in range(1, 8):
            dctx = dctx + dsc[dev * MOD_ROWS + ctx_row:dev * MOD_ROWS + ctx_row + 1, :]
        cx = cv[ctx_row:ctx_row + 1, :]
        sx = sig[ctx_row:ctx_row + 1, :]
        dctx_ref[...] = dctx * (sx * (1.0 + cx * (1.0 - sx))) * jnp.where(c == 0, 1.0, 0.0)

    vmem = pl.BlockSpec(memory_space=pltpu.VMEM)
    return pl.pallas_call(
        body, name="ada_bwd_tp", in_specs=[vmem] * 6, out_specs=[vmem] * 3,
        out_shape=[_sds((d, n), F32), _sds((1, N_MOD * d), F32), _sds((1, d), F32)],
        scratch_shapes=[pltpu.VMEM((N_CHIPS, MOD_ROWS, n), F32), pltpu.VMEM((8, MOD_ROWS, n), F32),
                        pltpu.SemaphoreType.DMA((7,)), pltpu.SemaphoreType.DMA((7,))],
        compiler_params=pltpu.CompilerParams(vmem_limit_bytes=V7X_VMEM_LIMIT),
    )(cc_all, *dmods, w)


def _rope_tables(s, ctx):
    pos = np.arange(s, dtype=np.float32)
    inv = (np.float32(ROPE_BASE) ** (-np.arange(0, QK_ROPE // 2, 2, dtype=np.float32) / np.float32(QK_ROPE // 2)))
    ang_r = np.floor(pos / GRID_W)[:, None] * inv
    ang_c = (pos - GRID_W * np.floor(pos / GRID_W))[:, None] * inv
    ang = np.concatenate([ang_r, ang_r, ang_c, ang_c], axis=-1).astype(np.float32)
    cos, sin = np.cos(ang), np.sin(ang)
    half_b = (np.arange(QK_ROPE) // 8) % 2 == 1
    sin_a = np.where(half_b, sin, 0.0)
    sin_b = np.where(half_b, 0.0, -sin)

    def place(tab, fill):
        full = np.full((s + ctx, HEAD_PAD), fill, np.float32)
        full[:s, QK_NOPE:QK_HEAD] = tab
        return jnp.asarray(full)

    return place(cos, 1.0), place(sin_a, 0.0), place(sin_b, 0.0)


def _pad_last(a, n):
    return jnp.pad(a, [(0, 0)] * (a.ndim - 1) + [(0, n - a.shape[-1])])


def _flat_rows(parts, rows, width):
    flat = jnp.concatenate([p.reshape(-1) for p in parts])
    return jnp.pad(flat, (0, rows * width - flat.shape[0])).reshape(rows, width)


def kernel(x, c, ctx, c_ctx, w_ada, b_ada, norm1_w, ffn1_w1, ffn1_w3, ffn1_w2, norm2_w, w_in, q_a_norm_w, w_uq, kv_a_norm_w, w_ukv, q_norm_w, k_norm_w, v_norm_w, w_s, b_s, w_out, norm3_w, ffn2_w1, ffn2_w3, ffn2_w2, loss_target, m_c_ctx, m_w_ada, m_b_ada, m_norm1_w, m_ffn1_w1, m_ffn1_w3, m_ffn1_w2, m_norm2_w, m_w_in, m_q_a_norm_w, m_w_uq, m_kv_a_norm_w, m_w_ukv, m_q_norm_w, m_k_norm_w, m_v_norm_w, m_w_s, m_b_s, m_w_out, m_norm3_w, m_ffn2_w1, m_ffn2_w3, m_ffn2_w2, v_c_ctx, v_w_ada, v_b_ada, v_norm1_w, v_ffn1_w1, v_ffn1_w3, v_ffn1_w2, v_norm2_w, v_w_in, v_q_a_norm_w, v_w_uq, v_kv_a_norm_w, v_w_ukv, v_q_norm_w, v_k_norm_w, v_v_norm_w, v_w_s, v_b_s, v_w_out, v_norm3_w, v_ffn2_w1, v_ffn2_w3, v_ffn2_w2):
    wts = dict(c_ctx=c_ctx, w_ada=w_ada, b_ada=b_ada, norm1_w=norm1_w, ffn1_w1=ffn1_w1, ffn1_w3=ffn1_w3, ffn1_w2=ffn1_w2,
               norm2_w=norm2_w, w_in=w_in, q_a_norm_w=q_a_norm_w, w_uq=w_uq, kv_a_norm_w=kv_a_norm_w, w_ukv=w_ukv,
               q_norm_w=q_norm_w, k_norm_w=k_norm_w, v_norm_w=v_norm_w, w_s=w_s, b_s=b_s, w_out=w_out, norm3_w=norm3_w,
               ffn2_w1=ffn2_w1, ffn2_w3=ffn2_w3, ffn2_w2=ffn2_w2)
    moms = dict(c_ctx=m_c_ctx, w_ada=m_w_ada, b_ada=m_b_ada, norm1_w=m_norm1_w, ffn1_w1=m_ffn1_w1, ffn1_w3=m_ffn1_w3,
                ffn1_w2=m_ffn1_w2, norm2_w=m_norm2_w, w_in=m_w_in, q_a_norm_w=m_q_a_norm_w, w_uq=m_w_uq,
                kv_a_norm_w=m_kv_a_norm_w, w_ukv=m_w_ukv, q_norm_w=m_q_norm_w, k_norm_w=m_k_norm_w, v_norm_w=m_v_norm_w,
                w_s=m_w_s, b_s=m_b_s, w_out=m_w_out, norm3_w=m_norm3_w, ffn2_w1=m_ffn2_w1, ffn2_w3=m_ffn2_w3,
                ffn2_w2=m_ffn2_w2)
    vars_ = dict(c_ctx=v_c_ctx, w_ada=v_w_ada, b_ada=v_b_ada, norm1_w=v_norm1_w, ffn1_w1=v_ffn1_w1, ffn1_w3=v_ffn1_w3,
                 ffn1_w2=v_ffn1_w2, norm2_w=v_norm2_w, w_in=v_w_in, q_a_norm_w=v_q_a_norm_w, w_uq=v_w_uq,
                 kv_a_norm_w=v_kv_a_norm_w, w_ukv=v_w_ukv, q_norm_w=v_q_norm_w, k_norm_w=v_k_norm_w, v_norm_w=v_v_norm_w,
                 w_s=v_w_s, b_s=v_b_s, w_out=v_w_out, norm3_w=v_norm3_w, ffn2_w1=v_ffn2_w1, ffn2_w3=v_ffn2_w3,
                 ffn2_w2=v_ffn2_w2)

    nb, s, d = x.shape
    nctx = ctx.shape[1]
    t, tc = nb * s, nb * nctx
    t_all = t + tc
    sk = s + nctx
    assert nb + 1 <= MOD_ROWS and d % LANES == 0
    tm = _token_tile(s, nctx)
    tq = _div_tile(s, 512, tm)
    tmx = _div_tile(math.gcd(s, tc), 1024, tm)
    tmo = _div_tile(s, 1024, tm)

    def held(n, a_):
        return jnp.swapaxes(a_[0], 0, 1) if n in T_WEIGHTS else a_[0]

    def unheld(n, a_):
        return (jnp.swapaxes(a_, 0, 1) if n in T_WEIGHTS else a_)[None]

    shard = {"w_ada": w_ada[0].astype(BF16)}
    full = {}

    def unshard(names, blocks):
        for n, g4 in zip(names, blocks):
            _, r_, c_ = g4.shape
            if n in ROW_SHARDED or n in T_WEIGHTS:
                full[n] = g4.reshape(N_CHIPS * r_, c_)
            else:
                full[n] = g4.transpose(1, 0, 2).reshape(r_, N_CHIPS * c_)

    def chip_major(n, g_):
        if n in ROW_SHARDED or n in T_WEIGHTS:
            return g_.reshape(N_CHIPS, g_.shape[0] // N_CHIPS, g_.shape[1]).astype(BF16)
        r_, cols = g_.shape
        return g_.reshape(r_, N_CHIPS, cols // N_CHIPS).transpose(1, 0, 2).astype(BF16)

    cc = jnp.concatenate([c, c_ctx[None, :], jnp.zeros((MOD_ROWS - nb - 1, d), F32)], axis=0)
    n_ada = shard["w_ada"].shape[1]
    assert n_ada % LANES == 0
    my_chip = 2 * lax.axis_index("x") + lax.axis_index("y")
    b_cols = lax.dynamic_slice_in_dim(b_ada, my_chip * n_ada, n_ada, axis=1)
    later = MIX_WEIGHTS + LAST_WEIGHTS
    got, cc_all, table, cast = _first_exchange([held(n, wts[n]) for n in FIRST_WEIGHTS],
                                               [held(n, wts[n]) for n in later], cc, shard["w_ada"], b_cols)
    unshard(FIRST_WEIGHTS, got)
    shard.update(zip(later, cast))
    cc_all = cc_all.reshape(8 * MOD_ROWS, d)
    mod = table.transpose(1, 0, 2).reshape(MOD_ROWS, N_MOD, d)
    wsb = w_s[0].astype(BF16)
    wcat = wsb.transpose(1, 0, 2).reshape(CHUNK, GROUPS * CHUNK)
    wcat_t = wsb.transpose(2, 0, 1).reshape(CHUNK, GROUPS * CHUNK)
    bias = jnp.repeat(b_s[0].T, GROUP_DIM, axis=1)
    vnw = v_norm_w.reshape(1, GMLP_W)
    lane = jnp.arange(GMLP_W)
    ones = (lane[:, None] // GROUP_DIM == lane[None, :] // GROUP_DIM).astype(BF16)
    qnw = _pad_last(q_norm_w, HEAD_PAD)
    knw = _pad_last(k_norm_w, HEAD_PAD)
    tabs = _rope_tables(s, nctx)

    x_lat, x_ctx = x.reshape(t, d), ctx.reshape(tc, d)
    (xs1, a1, b1, y1), got = _ffn_fwd(x_lat, x_ctx, mod, norm1_w, full["ffn1_w1"], full["ffn1_w3"], full["ffn1_w2"], 0, s,
                                      nb, tm, "ffn1_fwd", exch=("gather", [shard[n] for n in MIX_WEIGHTS]))
    unshard(MIX_WEIGHTS, got)
    wi = full["w_in"]
    wp = jnp.concatenate([wi[0:KV_LORA], jnp.zeros((QK_NOPE, d), BF16), wi[KV_LORA:KV_LORA + QK_ROPE],
                          jnp.zeros((HEAD_PAD - QK_HEAD, d), BF16), wi[KV_LORA + QK_ROPE:]], axis=0)
    wq = jnp.pad(full["w_uq"].reshape(HEADS, QK_HEAD, Q_LORA), ((0, 0), (0, HEAD_PAD - QK_HEAD), (0, 0)))
    wkv = full["w_ukv"].reshape(KV_LORA, HEADS, QK_NOPE + V_HEAD)
    wk = _pad_last(wkv[:, :, :QK_NOPE].transpose(1, 0, 2), HEAD_PAD)
    wv = wkv[:, :, QK_NOPE:].reshape(KV_LORA, HEADS // 2, 2 * V_HEAD).transpose(1, 0, 2)
    h2, proj = _mixin_fwd(xs1, mod, norm2_w, wp, s, nb, tmx)
    prep_w = (wq, wk, wv, kv_a_norm_w, q_a_norm_w, qnw, knw)
    q, k_all, v_all = _prep_fwd(proj, 0, nb, s, 0, sk, 0, None, tabs, *prep_w, tmo, True, "prep_fwd")
    k_all, v_all = _prep_fwd(proj, t // tm, nb, nctx, s // tm, sk, s // tm, (k_all, v_all), tabs, *prep_w, tm, False,
                             "prep_ctx_fwd")
    o, lse, got = _attn_fwd(q, k_all, v_all, tq, exch=("gather", [shard[n] for n in LAST_WEIGHTS]))
    unshard(LAST_WEIGHTS, got)
    mixcat = _gmlp_fwd(proj, o, wcat, bias, vnw, ones, tq)
    x2, mix = _mixout_fwd(mixcat, xs1, mod, full["w_out"], s, tmo)
    (dy, a2, b2, y2, loss_part), _ = _ffn_fwd(x2, None, mod, norm3_w, full["ffn2_w1"], full["ffn2_w3"], full["ffn2_w2"], 6,
                                              s, nb, tm, "ffn2_fwd", target=loss_target.reshape(t, d))

    grads, cm, recv = {}, {}, {}

    def scatter_of(names):
        return ("scatter", [cm[n] for n in names])

    (dx2, h3, g2, da2, db2, dyb2, dmod_c, grads["norm3_w"]), _ = _ffn_bwd(
        dy, x2, None, a2, b2, y2, mod, norm3_w, full["ffn2_w1"], full["ffn2_w3"], full["ffn2_w2"], 6, s, nb, tm,
        "ffn2_bwd")
    cm["ffn2_w1"] = chip_major("ffn2_w1", _mm_tn(da2, h3, t, "ffn2_dw1"))
    cm["ffn2_w3"] = chip_major("ffn2_w3", _mm_tn(db2, h3, t, "ffn2_dw3"))
    cm["ffn2_w2"] = chip_major("ffn2_w2", _mm_tn(g2, dyb2, t, "ffn2_dw2"))
    dmix, do, dsg, dmod_b = _mixout_bwd(dx2, mix, mod, full["w_out"], s, tmo)
    cm["w_out"] = chip_major("w_out", _mm_tn(mixcat, dmix, t, "wout_dw"))
    duv, dws, dbs, dvnw = _gmlp_bwd(proj, dsg, wcat, wcat_t, bias, vnw, ones, tq)
    group = LAST_WEIGHTS + ("w_out",)
    (dq, dk, dv), got = _attn_bwd(q, k_all, v_all, do, mixcat, lse, tq, exch=scatter_of(group))
    recv.update(zip(group, got))
    dp0, dwk_c, dwv_c, dkvaw_c, dknw_c = _prep_bwd(
        proj, t // tm, nb, nctx, s // tm, s // tm, t_all, None, tabs, *prep_w, None, dk, dv, None, tm, "prep_ctx_bwd")
    dp0, dwq, dqaw, dqnw, dwk, dwv, dkvaw, dknw = _prep_bwd(
        proj, 0, nb, s, 0, 0, t_all, dp0, tabs, *prep_w, dq, dk, dv, [dwk_c, dwv_c, dkvaw_c, dknw_c], tq, "prep_bwd")
    part, sib = {}, {}
    early = LAST_WEIGHTS + ("w_out",)
    for n in early:
        part[n] = _sum_slots(recv[n], "sum_" + n)
    (dxs1, dmod_a, grads["norm2_w"]), _, got = _mixin_bwd(dp0, duv, xs1, dx2, mod, norm2_w, wp, s, nb, tmx,
                                                          [part[n] for n in early])
    sib.update(zip(early, got))
    dwp = jnp.concatenate([_mm_tn(dp0, h2, t_all, "win_dw_kvq"), _mm_tn(duv, h2, t, "win_dw_uv")], axis=0)
    cm["w_in"] = chip_major("w_in", jnp.concatenate(
        [dwp[0:KV_LORA], dwp[KV_LORA + QK_NOPE:KV_LORA + QK_HEAD], dwp[256:]], axis=0))
    cm["w_uq"] = chip_major("w_uq", dwq[:, :, :QK_HEAD].transpose(0, 2, 1).reshape(HEADS * QK_HEAD, Q_LORA))
    cm["w_ukv"] = chip_major("w_ukv", jnp.concatenate(
        [dwk[:, :, :QK_NOPE].transpose(1, 0, 2),
         dwv.transpose(1, 0, 2).reshape(KV_LORA, HEADS, V_HEAD)], axis=2).reshape(KV_LORA, HEADS * (QK_NOPE + V_HEAD)))
    (dx_lat, h1, g1, da1, db1, dyb1, dmod_0, grads["norm1_w"]), _ = _ffn_bwd(
        dxs1, x_lat, x_ctx, a1, b1, y1, mod, norm1_w, full["ffn1_w1"], full["ffn1_w3"], full["ffn1_w2"], 0, s, nb, tm,
        "ffn1_bwd")
    dmods = [m_.reshape(MOD_ROWS, N_MOD * d) for m_ in (dmod_0, dmod_a, dmod_b, dmod_c)]
    dw_ada, grads["b_ada"], dctx = _ada_bwd_tp(cc_all, dmods, shard["w_ada"], nb)
    grads["c_ctx"] = dctx[0]
    grads["q_a_norm_w"], grads["kv_a_norm_w"] = dqaw, dkvaw
    grads["q_norm_w"], grads["k_norm_w"] = dqnw[:, :QK_HEAD], dknw[:, :QK_HEAD]
    grads["v_norm_w"], grads["w_s"], grads["b_s"] = dvnw, dws, dbs[:, 0]
    grad_x = dx_lat.reshape(nb, s, d)
    n_small = sum(wts[n].size for n in SMALL)
    rows_s = _round_up(-(-(n_small + 1) // d), 16)
    cm["small"] = jnp.broadcast_to(_flat_rows([grads[n] for n in SMALL] + [loss_part], rows_s, d), (N_CHIPS, rows_s, d))
    group = ("w_in", "w_uq", "w_ukv", "small")
    dw2, got = _mm_tn(g1, dyb1, t_all, "ffn1_dw2", exch=scatter_of(group))
    recv.update(zip(group, got))
    cm["ffn1_w2"] = chip_major("ffn1_w2", dw2)
    dw1, got = _mm_tn(da1, h1, t_all, "ffn1_dw1", exch=scatter_of(("ffn1_w2",)))
    recv["ffn1_w2"] = got[0]
    cm["ffn1_w1"] = chip_major("ffn1_w1", dw1)
    dw3, got = _mm_tn(db1, h1, t_all, "ffn1_dw3", exch=scatter_of(("ffn1_w1",)))
    recv["ffn1_w1"] = got[0]
    cm["ffn1_w3"] = chip_major("ffn1_w3", dw3)
    stepped = {}
    reduced = tuple(n for n in SHARDED if n != "w_ada") + ("small",)
    late = tuple(n for n in reduced if n not in early and n != "ffn1_w3")
    for n in late:
        part[n] = _sum_slots(recv[n], "sum_" + n)
    stepped["w_ada"], got, got_sib = _adamw([dw_ada], wts["w_ada"][0], moms["w_ada"][0], vars_["w_ada"][0],
                                            "adamw_w_ada", exch=scatter_of(("ffn1_w3",)), swap=[part[n] for n in late])
    sib.update(zip(late, got_sib))
    part["ffn1_w3"] = _sum_slots(got[0], "sum_ffn1_w3")
    sib["ffn1_w3"] = _swap_cores([part["ffn1_w3"]], "swap_last")[0]
    for n in reduced[:-1]:
        stepped[n], _ = _adamw([part[n], sib[n]], held(n, wts[n]), held(n, moms[n]), held(n, vars_[n]), "adamw_" + n)
    for n in SHARDED:
        stepped[n] = [unheld(n, a_) for a_ in stepped[n]]
    packed, _ = _adamw([part["small"], sib["small"]], _flat_rows([wts[n] for n in SMALL], rows_s, d),
                       _flat_rows([moms[n] for n in SMALL], rows_s, d), _flat_rows([vars_[n] for n in SMALL], rows_s, d),
                       "adamw_small")
    loss = packed[0].reshape(-1)[n_small]
    for n in SMALL:
        stepped[n] = []
    for a_ in packed:
        flat = a_.reshape(-1)
        off = 0
        for n in SMALL:
            stepped[n].append(flat[off:off + wts[n].size].reshape(wts[n].shape))
            off += wts[n].size
    return (loss, grad_x, *[stepped[n][0] for n in WEIGHTS], *[stepped[n][1] for n in WEIGHTS],
            *[stepped[n][2] for n in WEIGHTS], *[stepped[n][3] for n in WEIGHTS])
```

```python
import functools
import math

import jax
import jax.numpy as jnp
import numpy as np
from jax import lax
from jax.experimental import pallas as pl
from jax.experimental.pallas import tpu as pltpu

F32 = jnp.float32
BF16 = jnp.bfloat16

EPS = 1e-6
N_MOD = 9
HEADS = 8
QK_NOPE, QK_ROPE, V_HEAD = 64, 32, 64
QK_HEAD = QK_NOPE + QK_ROPE
HEAD_PAD = 128
LN2 = math.log(2.0)
SOFTMAX_SCALE = QK_HEAD ** -0.5 / LN2
Q_LORA, KV_LORA = 256, 128
GROUPS, GROUP_DIM, CHUNK = 8, 64, 128
GMLP_W = GROUPS * GROUP_DIM
MLA_W = HEADS * V_HEAD
IN_COLS = 1440
PROJ_COLS = 1536
GRID_W = 64
ROPE_BASE = 10000.0
MOD_ROWS = 16
ADAM_LR, ADAM_B1, ADAM_B2, ADAM_EPS, ADAM_WD, ADAM_STEP = 0.001, 0.9, 0.999, 1e-08, 0.01, 10
N_CHIPS = 4
LANES = 128
V7X_VMEM_LIMIT = 56 * 1024 * 1024
GELU_C = math.sqrt(2.0 / math.pi)

SHARDED = ("w_ada", "ffn1_w1", "ffn1_w3", "ffn1_w2", "w_in", "w_uq", "w_ukv", "w_out", "ffn2_w1", "ffn2_w3", "ffn2_w2")
ROW_SHARDED = ("ffn1_w2", "w_out", "ffn2_w2")
T_WEIGHTS = ("ffn1_w1", "ffn1_w3", "ffn2_w1", "ffn2_w3", "w_in", "w_uq")
FIRST_WEIGHTS = ("ffn1_w1", "ffn1_w3", "ffn1_w2")
MIX_WEIGHTS = ("w_in", "w_uq", "w_ukv", "w_out")
LAST_WEIGHTS = ("ffn2_w1", "ffn2_w3", "ffn2_w2")
SMALL = ("c_ctx", "b_ada", "norm1_w", "norm2_w", "q_a_norm_w", "kv_a_norm_w", "q_norm_w", "k_norm_w", "v_norm_w",
         "w_s", "b_s", "norm3_w")
WEIGHTS = ("c_ctx", "w_ada", "b_ada", "norm1_w", "ffn1_w1", "ffn1_w3", "ffn1_w2", "norm2_w", "w_in", "q_a_norm_w",
           "w_uq", "kv_a_norm_w", "w_ukv", "q_norm_w", "k_norm_w", "v_norm_w", "w_s", "b_s", "w_out", "norm3_w",
           "ffn2_w1", "ffn2_w3", "ffn2_w2")


def _round_up(n, m):
    return (n + m - 1) // m * m


def _div_tile(n, target, mult):
    best = None
    for t in range(mult, min(n, target) + 1, mult):
        if n % t == 0:
            best = t
    return n if best is None else best


def _dot(a, b):
    return lax.dot_general(a, b, (((1,), (0,)), ((), ())), preferred_element_type=F32)


def _dot_nt(a, b):
    return lax.dot_general(a, b, (((1,), (1,)), ((), ())), preferred_element_type=F32)


def _dot_tn(a, b):
    return lax.dot_general(a, b, (((0,), (0,)), ((), ())), preferred_element_type=F32)


def _sigmoid(x):
    return 1.0 / (1.0 + jnp.exp(-x))


def _gelu(x):
    return 0.5 * x * (1.0 + jnp.tanh(GELU_C * (x + 0.044715 * x * x * x)))


def _gelu_grad(x):
    t = jnp.tanh(GELU_C * (x + 0.044715 * x * x * x))
    return 0.5 * (1.0 + t) + 0.5 * x * (1.0 - t * t) * (GELU_C * (1.0 + 3 * 0.044715 * x * x))


def _rope3(x, cos, sin_a, sin_b):
    return x * cos + pltpu.roll(x, 8, 2) * sin_a + pltpu.roll(x, HEAD_PAD - 8, 2) * sin_b


def _rope3_t(d, cos, sin_a, sin_b):
    return d * cos + pltpu.roll(d * sin_a, HEAD_PAD - 8, 2) + pltpu.roll(d * sin_b, 8, 2)


def _group_sum(x, ones_ref):
    hi = x.astype(BF16)
    lo = (x - hi.astype(F32)).astype(BF16)
    return _dot(hi, ones_ref[...]) + _dot(lo, ones_ref[...])


def _params(n_axes):
    return pltpu.CompilerParams(dimension_semantics=("arbitrary",) * n_axes, vmem_limit_bytes=V7X_VMEM_LIMIT)


def _whole(shape):
    nd = len(shape)
    return pl.BlockSpec(shape, lambda *_: (0,) * nd, pipeline_mode=pl.Buffered(1))


def _sds(shape, dtype):
    return jax.ShapeDtypeStruct(shape, dtype)


def _token_tile(s, ctx):
    return _div_tile(math.gcd(s, ctx), 256, CHUNK)


def _other_chips(x, y):
    return [(1 - x, y), (x, 1 - y), (1 - x, 1 - y)]


def _exch_copies(kind, srcs, dsts, send_sems, recv_sems, local_sems, with_arrivals):
    x, y, c = lax.axis_index("x"), lax.axis_index("y"), lax.axis_index("c")
    me = 2 * x + y
    local, sends, arrivals = [], [], []
    for w, (src, dst) in enumerate(zip(srcs, dsts)):
        own = src if kind == "gather" else src.at[me]
        local.append(pltpu.make_async_copy(own, dst.at[me], local_sems.at[w]))
        for k, (px, py) in enumerate(_other_chips(x, y)):
            sem = dict(send_sem=send_sems.at[3 * w + k], recv_sem=recv_sems.at[3 * w + k], device_id=(px, py, c),
                       device_id_type=pl.DeviceIdType.MESH)
            out = src if kind == "gather" else src.at[2 * px + py]
            sends.append(pltpu.make_async_remote_copy(src_ref=out, dst_ref=dst.at[me], **sem))
            if with_arrivals:
                arrivals.append(pltpu.make_async_remote_copy(src_ref=own, dst_ref=dst.at[2 * px + py], **sem))
    return local, sends, arrivals


def _exch_start(kind, srcs, dsts, sems):
    local, sends, _ = _exch_copies(kind, srcs, dsts, *sems, with_arrivals=False)
    for cp in local + sends:
        cp.start()


def _exch_wait(kind, srcs, dsts, sems):
    local, sends, arrivals = _exch_copies(kind, srcs, dsts, *sems, with_arrivals=True)
    for cp in arrivals:
        cp.wait_recv()
    for cp in sends:
        cp.wait_send()
    for cp in local:
        cp.wait()


def _exch_scratch(n):
    return [pltpu.SemaphoreType.DMA((3 * n,)), pltpu.SemaphoreType.DMA((3 * n,)), pltpu.SemaphoreType.DMA((n,))]


def _exch_shapes(kind, arrays):
    return [_sds((N_CHIPS,) + a.shape if kind == "gather" else a.shape, a.dtype) for a in arrays]


def _sibling_copies(srcs, dsts, send_sems, recv_sems):
    x, y, c = lax.axis_index("x"), lax.axis_index("y"), lax.axis_index("c")
    return [pltpu.make_async_remote_copy(
        src_ref=src, dst_ref=dst, send_sem=send_sems.at[w], recv_sem=recv_sems.at[w], device_id=(x, y, 1 - c),
        device_id_type=pl.DeviceIdType.MESH) for w, (src, dst) in enumerate(zip(srcs, dsts))]


def _hosted_call(body, name, grid, in_specs, out_specs, out_shape, operands, scratch=(), exch=None, swap=None):
    n_axes = len(grid)
    if exch is None and swap is None:
        outs = pl.pallas_call(body, name=name, grid=grid, in_specs=list(in_specs), out_specs=list(out_specs),
                              out_shape=list(out_shape), scratch_shapes=list(scratch),
                              compiler_params=_params(n_axes))(*operands)
        return list(outs), []
    kind, arrays = exch if exch is not None else ("scatter", [])
    swaps = list(swap or [])
    n_in, n_out, n_sc, n_ex, n_sw = len(in_specs), len(out_specs), len(scratch), len(arrays), len(swaps)

    def hosted(*refs):
        cin, ein, sin = refs[:n_in], refs[n_in:n_in + n_ex], refs[n_in + n_ex:n_in + n_ex + n_sw]
        o0 = n_in + n_ex + n_sw
        cout, eout, sout = refs[o0:o0 + n_out], refs[o0 + n_out:o0 + n_out + n_ex], refs[o0 + n_out + n_ex:o0 + n_out + n_ex + n_sw]
        rest = refs[o0 + n_out + n_ex + n_sw:]
        csc, sems, swap_sems = rest[:n_sc], rest[n_sc:n_sc + 3], rest[n_sc + 3:]
        first = functools.reduce(jnp.logical_and, [pl.program_id(a) == 0 for a in range(n_axes)])
        last = functools.reduce(jnp.logical_and, [pl.program_id(a) == grid[a] - 1 for a in range(n_axes)])

        @pl.when(first)
        def _():
            if n_ex:
                _exch_start(kind, ein, eout, sems)
            for cp in _sibling_copies(sin, sout, *swap_sems) if n_sw else []:
                cp.start()

        body(*cin, *cout, *csc)

        @pl.when(last)
        def _():
            if n_ex:
                _exch_wait(kind, ein, eout, sems)
            for cp in _sibling_copies(sin, sout, *swap_sems) if n_sw else []:
                cp.wait()

    any_spec = pl.BlockSpec(memory_space=pl.ANY)
    swap_scratch = [pltpu.SemaphoreType.DMA((n_sw,)), pltpu.SemaphoreType.DMA((n_sw,))] if n_sw else []
    outs = pl.pallas_call(
        hosted, name=name, grid=grid, in_specs=list(in_specs) + [any_spec] * (n_ex + n_sw),
        out_specs=list(out_specs) + [any_spec] * (n_ex + n_sw),
        out_shape=list(out_shape) + _exch_shapes(kind, arrays) + [_sds(a.shape, a.dtype) for a in swaps],
        scratch_shapes=list(scratch) + _exch_scratch(max(n_ex, 1)) + swap_scratch, compiler_params=_params(n_axes),
    )(*operands, *arrays, *swaps)
    got = list(outs[n_out:n_out + n_ex])
    return (list(outs[:n_out]), got) if swap is None else (list(outs[:n_out]), got, list(outs[n_out + n_ex:]))


class _TokenTiles:
    def __init__(self, t, tc, tm):
        self.n_lat, self.n_ctx = t // tm, tc // tm
        self.n_all = self.n_lat + self.n_ctx

    def tile(self, i):
        return (i + self.n_lat) % self.n_all if self.n_ctx else i

    def is_lat(self, i):
        return self.tile(i) < self.n_lat

    def row(self, i):
        return (self.tile(i), 0)

    def lat_row(self, i):
        return (jnp.where(self.is_lat(i), self.tile(i), 0), 0) if self.n_ctx else (i, 0)

    def ctx_row(self, i):
        return (jnp.where(self.is_lat(i), self.n_ctx - 1, self.tile(i) - self.n_lat), 0)


def _ffn_fwd(x_lat, x_ctx, mod, nw, w1, w3, w2, k0, s, nb, tm, name, target=None, exch=None):
    t, d = x_lat.shape
    tc = 0 if x_ctx is None else x_ctx.shape[0]
    f = w1.shape[0]
    tiles = _TokenTiles(t, tc, tm)
    n_x = 2 if tc else 1
    n_t = 0 if target is None else 1
    assert not (tc and n_t)

    def body(*refs):
        x_ref = refs[0]
        t_ref = refs[n_x] if n_t else None
        mod_ref, nw_ref, w1_ref, w3_ref, w2_ref, o_ref, a_ref, b_ref, y_ref = refs[n_x + n_t:n_x + n_t + 9]
        i = pl.program_id(0)
        g = jnp.minimum((tiles.tile(i) * tm) // s, nb)
        shift = mod_ref[g, pl.ds(k0, 1), :]
        scale = mod_ref[g, pl.ds(k0 + 1, 1), :]
        gate = mod_ref[g, pl.ds(k0 + 2, 1), :]
        x = jnp.where(tiles.is_lat(i), x_ref[...], refs[1][...]) if tc else x_ref[...]
        r = lax.rsqrt(jnp.mean(x * x, axis=-1, keepdims=True) + EPS)
        hb = ((x * r * nw_ref[...]) * (1.0 + scale) + shift).astype(BF16)
        a = _dot_nt(hb, w1_ref[...])
        b = _dot_nt(hb, w3_ref[...])
        gb = (a * _sigmoid(a) * b).astype(BF16)
        y = _dot(gb, w2_ref[...])
        out = x + (0.5 * gate) * y
        a_ref[...] = a.astype(BF16)
        b_ref[...] = b.astype(BF16)
        y_ref[...] = y.astype(BF16)
        if n_t:
            loss_ref, acc_ref = refs[-2:]

            @pl.when(i == 0)
            def _():
                acc_ref[...] = jnp.zeros_like(acc_ref)

            e = out - t_ref[...]
            o_ref[...] = e * (1.0 / d)
            acc_ref[...] += jnp.sum(e * e, axis=0, keepdims=True)

            @pl.when(i == tiles.n_all - 1)
            def _():
                loss_ref[...] = (0.5 / d) * jnp.sum(acc_ref[...], axis=-1, keepdims=True)
        else:
            o_ref[...] = out

    td = pl.BlockSpec((tm, d), tiles.row)
    tf = pl.BlockSpec((tm, f), tiles.row)
    return _hosted_call(
        body, name, (tiles.n_all,),
        [pl.BlockSpec((tm, d), tiles.lat_row)] + ([pl.BlockSpec((tm, d), tiles.ctx_row)] if tc else []) + [td] * n_t
        + [_whole(mod.shape), _whole(nw.shape), _whole(w1.shape), _whole(w3.shape), _whole(w2.shape)],
        [td, tf, tf, td] + [pl.BlockSpec((1, 1), lambda i: (0, 0))] * n_t,
        [_sds((t + tc, d), F32), _sds((t + tc, f), BF16), _sds((t + tc, f), BF16), _sds((t + tc, d), BF16)]
        + [_sds((1, 1), F32)] * n_t,
        (x_lat,) + ((x_ctx,) if tc else ()) + ((target,) if n_t else ()) + (mod, nw, w1, w3, w2),
        scratch=[pltpu.VMEM((1, d), F32)] * n_t, exch=exch)


def _ffn_bwd(dout, x_lat, x_ctx, a, b, y, mod, nw, w1, w3, w2, k0, s, nb, tm, name, exch=None):
    t, d = x_lat.shape
    tc = 0 if x_ctx is None else x_ctx.shape[0]
    f = w1.shape[0]
    nch = 2 if (f // 2) % LANES == 0 and f % 2 == 0 else 1
    fc = f // nch
    tiles = _TokenTiles(t, tc, tm)
    n_x = 2 if tc else 1

    def body(*refs):
        do_ref, x_ref = refs[0], refs[1]
        (a_ref, b_ref, y_ref, mod_ref, nw_ref, w1_ref, w3_ref, w2_ref,
         dx_ref, h_ref, g_ref, da_ref, db_ref, dy_ref, dmod_ref, dnw_ref) = refs[1 + n_x:]
        i = pl.program_id(0)

        @pl.when(i == 0)
        def _():
            dmod_ref[...] = jnp.zeros_like(dmod_ref)
            dnw_ref[...] = jnp.zeros_like(dnw_ref)

        g = jnp.minimum((tiles.tile(i) * tm) // s, nb)
        shift = mod_ref[g, pl.ds(k0, 1), :]
        scale = mod_ref[g, pl.ds(k0 + 1, 1), :]
        gate = mod_ref[g, pl.ds(k0 + 2, 1), :]
        x = jnp.where(tiles.is_lat(i), x_ref[...], refs[2][...]) if tc else x_ref[...]
        dout_v = do_ref[...]
        r = lax.rsqrt(jnp.mean(x * x, axis=-1, keepdims=True) + EPS)
        xh = x * r
        n = xh * nw_ref[...]
        h_ref[...] = (n * (1.0 + scale) + shift).astype(BF16)
        dyb = ((0.5 * gate) * dout_v).astype(BF16)
        dy_ref[...] = dyb
        dmod_ref[g, pl.ds(k0 + 2, 1), :] += 0.5 * jnp.sum(dout_v * y_ref[...].astype(F32), axis=0, keepdims=True)
        dh = jnp.zeros((tm, d), F32)
        for c in range(nch):
            sl = slice(c * fc, (c + 1) * fc)
            dg = _dot_nt(dyb, w2_ref[sl, :])
            av = a_ref[:, sl].astype(F32)
            bv = b_ref[:, sl].astype(F32)
            sig = _sigmoid(av)
            sa = av * sig
            g_ref[:, sl] = (sa * bv).astype(BF16)
            dab = (dg * bv * (sig * (1.0 + av * (1.0 - sig)))).astype(BF16)
            dbb = (dg * sa).astype(BF16)
            da_ref[:, sl] = dab
            db_ref[:, sl] = dbb
            dh = dh + _dot(dab, w1_ref[sl, :]) + _dot(dbb, w3_ref[sl, :])
        dmod_ref[g, pl.ds(k0, 1), :] += jnp.sum(dh, axis=0, keepdims=True)
        dmod_ref[g, pl.ds(k0 + 1, 1), :] += jnp.sum(dh * n, axis=0, keepdims=True)
        dn = dh * (1.0 + scale)
        dnw_ref[...] += jnp.sum(dn * xh, axis=0, keepdims=True)
        dxh = dn * nw_ref[...]
        dx_ref[...] = dout_v + r * (dxh - xh * jnp.mean(dxh * xh, axis=-1, keepdims=True))

    td = pl.BlockSpec((tm, d), tiles.row)
    tf = pl.BlockSpec((tm, f), tiles.row)
    lat = pl.BlockSpec((tm, d), tiles.lat_row)
    ta = t + tc
    return _hosted_call(
        body, name, (tiles.n_all,),
        [td, lat] + ([pl.BlockSpec((tm, d), tiles.ctx_row)] if tc else [])
        + [tf, tf, td, _whole(mod.shape), _whole(nw.shape), _whole(w1.shape), _whole(w3.shape), _whole(w2.shape)],
        [lat, td, tf, tf, tf, td, pl.BlockSpec(mod.shape, lambda i: (0, 0, 0)), pl.BlockSpec((1, d), lambda i: (0, 0))],
        [_sds((t, d), F32), _sds((ta, d), BF16), _sds((ta, f), BF16), _sds((ta, f), BF16), _sds((ta, f), BF16),
         _sds((ta, d), BF16), _sds(mod.shape, F32), _sds((1, d), F32)],
        (dout, x_lat) + ((x_ctx,) if tc else ()) + (a, b, y, mod, nw, w1, w3, w2), exch=exch)


def _mm_tn(a, b, rows, name, exch=None):
    m = a.shape[1]
    n = b.shape[1]
    bm = _div_tile(m, 1408, LANES)
    bn = _div_tile(n, 1408, LANES)
    bk = _div_tile(rows, 2304, LANES)
    nk = rows // bk

    def body(a_ref, b_ref, o_ref, acc_ref):
        k = pl.program_id(2)

        @pl.when(k == 0)
        def _():
            acc_ref[...] = jnp.zeros_like(acc_ref)

        acc_ref[...] += _dot_tn(a_ref[...], b_ref[...])

        @pl.when(k == nk - 1)
        def _():
            o_ref[...] = acc_ref[...].astype(BF16)

    (out,), got = _hosted_call(
        body, name, (m // bm, n // bn, nk),
        [pl.BlockSpec((bk, bm), lambda i, j, k: (k, i)), pl.BlockSpec((bk, bn), lambda i, j, k: (k, j))],
        [pl.BlockSpec((bm, bn), lambda i, j, k: (i, j))], [_sds((m, n), BF16)], (a, b),
        scratch=[pltpu.VMEM((bm, bn), F32)], exch=exch)
    return out if exch is None else (out, got)


def _mixin_fwd(xs, mod, nw, wp, s, nb, tm):
    t, d = xs.shape

    def body(x_ref, mod_ref, nw_ref, wp_ref, h_ref, p_ref):
        g = jnp.minimum((pl.program_id(0) * tm) // s, nb)
        shift = mod_ref[g, pl.ds(3, 1), :]
        scale = mod_ref[g, pl.ds(4, 1), :]
        x = x_ref[...]
        r = lax.rsqrt(jnp.mean(x * x, axis=-1, keepdims=True) + EPS)
        hb = ((x * r * nw_ref[...]) * (1.0 + scale) + shift).astype(BF16)
        h_ref[...] = hb
        p_ref[...] = _dot_nt(hb, wp_ref[...]).astype(BF16)

    row = lambda i: (i, 0)
    return pl.pallas_call(
        body, name="mixin_fwd", grid=(t // tm,),
        in_specs=[pl.BlockSpec((tm, d), row), _whole(mod.shape), _whole(nw.shape), _whole(wp.shape)],
        out_specs=[pl.BlockSpec((tm, d), row), pl.BlockSpec((tm, PROJ_COLS), row)],
        out_shape=[_sds((t, d), BF16), _sds((t, PROJ_COLS), BF16)], compiler_params=_params(1),
    )(xs, mod, nw, wp)


def _mixin_bwd(dp0, duv, xs, dres, mod, nw, wp, s, nb, tm, swap):
    t_all, d = xs.shape
    nlat = dres.shape[0] // tm

    def body(p0_ref, uv_ref, x_ref, dr_ref, mod_ref, nw_ref, wp_ref, dx_ref, dmod_ref, dnw_ref):
        i = pl.program_id(0)

        @pl.when(i == 0)
        def _():
            dmod_ref[...] = jnp.zeros_like(dmod_ref)
            dnw_ref[...] = jnp.zeros_like(dnw_ref)

        lat = i < nlat
        g = jnp.minimum((i * tm) // s, nb)
        scale = mod_ref[g, pl.ds(4, 1), :]
        dh = _dot(p0_ref[...], wp_ref[0:512, :])
        extra = _dot(uv_ref[...], wp_ref[512:1536, :])
        dh = dh + jnp.where(lat, extra, 0.0)
        x = x_ref[...]
        r = lax.rsqrt(jnp.mean(x * x, axis=-1, keepdims=True) + EPS)
        xh = x * r
        n = xh * nw_ref[...]
        dmod_ref[g, pl.ds(3, 1), :] += jnp.sum(dh, axis=0, keepdims=True)
        dmod_ref[g, pl.ds(4, 1), :] += jnp.sum(dh * n, axis=0, keepdims=True)
        dn = dh * (1.0 + scale)
        dnw_ref[...] += jnp.sum(dn * xh, axis=0, keepdims=True)
        dxh = dn * nw_ref[...]
        dx_ref[...] = jnp.where(lat, dr_ref[...], 0.0) + r * (dxh - xh * jnp.mean(dxh * xh, axis=-1, keepdims=True))

    row = lambda i: (i, 0)
    lrow = lambda i: (jnp.minimum(i, nlat - 1), 0)
    return _hosted_call(
        body, "mixin_bwd", (t_all // tm,),
        [pl.BlockSpec((tm, 512), row), pl.BlockSpec((tm, 1024), lrow), pl.BlockSpec((tm, d), row),
         pl.BlockSpec((tm, d), lrow), _whole(mod.shape), _whole(nw.shape), _whole(wp.shape)],
        [pl.BlockSpec((tm, d), row), pl.BlockSpec(mod.shape, lambda i: (0, 0, 0)), pl.BlockSpec((1, d), lambda i: (0, 0))],
        [_sds((t_all, d), F32), _sds(mod.shape, F32), _sds((1, d), F32)], (dp0, duv, xs, dres, mod, nw, wp), swap=swap)


def _prep_fwd(proj, row0, nb, s, pos0, sk, key0, into, tabs, wq, wk, wv, kvaw, qaw, qnw, knw, tm, with_q, name):
    nblk = s // tm
    n_into = 0 if into is None else 2

    def body(p_ref, cos_ref, sa_ref, sb_ref, wq_ref, wk_ref, wv_ref, kvaw_ref, qaw_ref, qnw_ref, knw_ref, *rest):
        outs, heads_ref = rest[n_into:-1], rest[-1]
        q_ref, k_ref, v_ref = outs if with_q else (None,) + outs
        cos, sin_a, sin_b = cos_ref[...][None], sa_ref[...][None], sb_ref[...][None]

        def normed_roped(w_ref, src, extra, nw_ref, o_ref, post):
            for h in range(HEADS):
                heads_ref[h] = _dot_nt(src, w_ref[h]) if extra is None else _dot(src, w_ref[h])
            xp = heads_ref[...] if extra is None else heads_ref[...] + extra[None]
            r = lax.rsqrt(jnp.sum(xp * xp, axis=-1, keepdims=True) * (1.0 / QK_HEAD) + EPS)
            o_ref[...] = _rope3(xp * r * (nw_ref[...] * post)[None], cos, sin_a, sin_b).astype(BF16)

        ckv = p_ref[:, 0:128].astype(F32)
        rkv = lax.rsqrt(jnp.mean(ckv * ckv, axis=-1, keepdims=True) + EPS)
        ckvb = (ckv * rkv * kvaw_ref[...]).astype(BF16)
        normed_roped(wk_ref, ckvb, p_ref[:, 128:256].astype(F32), knw_ref, k_ref, 1.0)
        for j in range(HEADS // 2):
            v_ref[j] = _dot(ckvb, wv_ref[j]).astype(BF16)
        if with_q:
            cq = p_ref[:, 256:512].astype(F32)
            rq = lax.rsqrt(jnp.mean(cq * cq, axis=-1, keepdims=True) + EPS)
            normed_roped(wq_ref, (cq * rq * qaw_ref[...]).astype(BF16), None, qnw_ref, q_ref, SOFTMAX_SCALE)

    tab = pl.BlockSpec((tm, HEAD_PAD), lambda i: (pos0 + i % nblk, 0))
    qspec = pl.BlockSpec((None, HEADS, tm, HEAD_PAD), lambda i: (i // nblk, 0, i % nblk, 0))
    kspec = pl.BlockSpec((None, HEADS, tm, HEAD_PAD), lambda i: (i // nblk, 0, key0 + i % nblk, 0))
    vspec = pl.BlockSpec((None, HEADS // 2, tm, HEAD_PAD), lambda i: (i // nblk, 0, key0 + i % nblk, 0))
    qshape = _sds((nb, HEADS, s, HEAD_PAD), BF16)
    kshape = _sds((nb, HEADS, sk, HEAD_PAD), BF16)
    vshape = _sds((nb, HEADS // 2, sk, HEAD_PAD), BF16)
    n_q = 1 if with_q else 0
    return pl.pallas_call(
        body, name=name, grid=(nb * nblk,),
        in_specs=[pl.BlockSpec((tm, 512), lambda i: (row0 + i, 0)), tab, tab, tab, _whole(wq.shape), _whole(wk.shape),
                  _whole(wv.shape), _whole(kvaw.shape), _whole(qaw.shape), _whole(qnw.shape), _whole(knw.shape)]
        + [pl.BlockSpec(memory_space=pl.ANY)] * n_into,
        out_specs=([qspec] if with_q else []) + [kspec, vspec],
        out_shape=([qshape] if with_q else []) + [kshape, vshape],
        scratch_shapes=[pltpu.VMEM((HEADS, tm, HEAD_PAD), F32)],
        input_output_aliases={11: n_q, 12: n_q + 1} if n_into else {}, compiler_params=_params(1),
    )(proj, *tabs, wq, wk, wv, kvaw, qaw, qnw, knw, *(into or ()))


def _prep_bwd(proj, row0, nb, s, pos0, key0, dp_rows, dp_into, tabs, wq, wk, wv, kvaw, qaw, qnw, knw, dq, dk, dv, init, tm,
              name):
    nblk = s // tm
    with_q = dq is not None
    n_init = 0 if init is None else len(init)
    n_into = 0 if dp_into is None else 1

    def body(*refs):
        p_ref, cos_ref, sa_ref, sb_ref, wq_ref, wk_ref, wv_ref, kvaw_ref, qaw_ref, qnw_ref, knw_ref = refs[:11]
        rest = list(refs[11:])
        dq_ref = rest.pop(0) if with_q else None
        dk_ref, dv_ref = rest.pop(0), rest.pop(0)
        init_refs = [rest.pop(0) for _ in range(n_init)]
        if n_into:
            rest.pop(0)
        dp_ref = rest.pop(0)
        if with_q:
            dwq_ref, dqaw_ref, dqnw_ref = rest.pop(0), rest.pop(0), rest.pop(0)
        dwk_ref, dwv_ref, dkvaw_ref, dknw_ref, heads_ref, dhb_ref, dkr_ref = rest
        accs = [dwk_ref, dwv_ref, dkvaw_ref, dknw_ref]

        @pl.when(pl.program_id(0) == 0)
        def _():
            for k, acc in enumerate(accs):
                acc[...] = init_refs[k][...] if n_init else jnp.zeros_like(acc)
            if with_q:
                dwq_ref[...] = jnp.zeros_like(dwq_ref)
                dqaw_ref[...] = jnp.zeros_like(dqaw_ref)
                dqnw_ref[...] = jnp.zeros_like(dqnw_ref)

        cos, sin_a, sin_b = cos_ref[...][None], sa_ref[...][None], sb_ref[...][None]
        lane = lax.broadcasted_iota(jnp.int32, (tm, HEAD_PAD), 1)
        rope_lanes = (lane >= QK_NOPE) & (lane < QK_HEAD)

        def heads_bwd(w_ref, src, extra, nw_ref, d_ref, dnw_ref, dw_ref, post):
            w_t = extra is None
            for h in range(HEADS):
                heads_ref[h] = _dot_nt(src, w_ref[h]) if w_t else _dot(src, w_ref[h])
            xp = heads_ref[...] if extra is None else heads_ref[...] + extra[None]
            r = lax.rsqrt(jnp.sum(xp * xp, axis=-1, keepdims=True) * (1.0 / QK_HEAD) + EPS)
            xh = xp * r
            dn = _rope3_t(d_ref[...].astype(F32), cos, sin_a, sin_b)
            dnw_ref[...] += post * jnp.sum(jnp.sum(dn * xh, axis=0), axis=0, keepdims=True)
            dxh = dn * (nw_ref[...] * post)[None]
            dxp = r * (dxh - xh * (jnp.sum(dxh * xh, axis=-1, keepdims=True) * (1.0 / QK_HEAD)))
            dhb_ref[...] = dxp.astype(BF16)
            dsrc = jnp.zeros((tm, src.shape[1]), F32)
            for h in range(HEADS):
                dsrc = dsrc + (_dot(dhb_ref[h], w_ref[h]) if w_t else _dot_nt(dhb_ref[h], w_ref[h]))
                dw_ref[h] += _dot_tn(src, dhb_ref[h])
            return dsrc, jnp.sum(dxp, axis=0)

        ckv = p_ref[:, 0:128].astype(F32)
        rkv = lax.rsqrt(jnp.mean(ckv * ckv, axis=-1, keepdims=True) + EPS)
        ckvh = ckv * rkv
        ckvb = (ckvh * kvaw_ref[...]).astype(BF16)
        for h in range(HEADS):
            dkr_ref[h] = dk_ref[h].astype(F32).T
        dckv, dkp_sum = heads_bwd(wk_ref, ckvb, p_ref[:, 128:256].astype(F32), knw_ref, dkr_ref, dknw_ref, dwk_ref,
                                  1.0)
        for j in range(HEADS // 2):
            dvb = dv_ref[j].astype(F32).T.astype(BF16)
            dckv = dckv + _dot_nt(dvb, wv_ref[j])
            dwv_ref[j] += _dot_tn(ckvb, dvb)
        dkvaw_ref[...] += jnp.sum(dckv * ckvh, axis=0, keepdims=True)
        dch = dckv * kvaw_ref[...]
        dp_ref[:, 0:128] = (rkv * (dch - ckvh * jnp.mean(dch * ckvh, axis=-1, keepdims=True))).astype(BF16)
        dp_ref[:, 128:256] = jnp.where(rope_lanes, dkp_sum, 0.0).astype(BF16)
        if with_q:
            cq = p_ref[:, 256:512].astype(F32)
            rq = lax.rsqrt(jnp.mean(cq * cq, axis=-1, keepdims=True) + EPS)
            cqh = cq * rq
            cqb = (cqh * qaw_ref[...]).astype(BF16)
            dcq, _ = heads_bwd(wq_ref, cqb, None, qnw_ref, dq_ref, dqnw_ref, dwq_ref, SOFTMAX_SCALE)
            dqaw_ref[...] += jnp.sum(dcq * cqh, axis=0, keepdims=True)
            dqc = dcq * qaw_ref[...]
            dp_ref[:, 256:512] = (rq * (dqc - cqh * jnp.mean(dqc * cqh, axis=-1, keepdims=True))).astype(BF16)
        else:
            dp_ref[:, 256:512] = jnp.zeros((tm, Q_LORA), BF16)

    tab = pl.BlockSpec((tm, HEAD_PAD), lambda i: (pos0 + i % nblk, 0))
    qspec = pl.BlockSpec((None, HEADS, tm, HEAD_PAD), lambda i: (i // nblk, 0, i % nblk, 0))
    kspec = pl.BlockSpec((None, HEADS, HEAD_PAD, tm), lambda i: (i // nblk, 0, 0, key0 + i % nblk))
    vspec = pl.BlockSpec((None, HEADS // 2, HEAD_PAD, tm), lambda i: (i // nblk, 0, 0, key0 + i % nblk))

    def acc_spec(shape):
        nd = len(shape)
        return pl.BlockSpec(shape, lambda i: (0,) * nd)

    acc_shapes = [(HEADS, KV_LORA, HEAD_PAD), (HEADS // 2, KV_LORA, HEAD_PAD), (1, KV_LORA), (1, HEAD_PAD)]
    q_shapes = [(HEADS, Q_LORA, HEAD_PAD), (1, Q_LORA), (1, HEAD_PAD)] if with_q else []
    out_shapes = [(dp_rows, 512)] + q_shapes + acc_shapes
    n_before = 11 + (1 if with_q else 0) + 2 + n_init
    return pl.pallas_call(
        body, name=name, grid=(nb * nblk,),
        in_specs=[pl.BlockSpec((tm, 512), lambda i: (row0 + i, 0)), tab, tab, tab, _whole(wq.shape), _whole(wk.shape),
                  _whole(wv.shape), _whole(kvaw.shape), _whole(qaw.shape), _whole(qnw.shape), _whole(knw.shape)]
        + ([qspec] if with_q else []) + [kspec, vspec] + [_whole(a.shape) for a in (init or [])]
        + [pl.BlockSpec(memory_space=pl.ANY)] * n_into,
        out_specs=[pl.BlockSpec((tm, 512), lambda i: (row0 + i, 0))] + [acc_spec(sh) for sh in q_shapes + acc_shapes],
        out_shape=[_sds(out_shapes[0], BF16)] + [_sds(sh, F32) for sh in out_shapes[1:]],
        scratch_shapes=[pltpu.VMEM((HEADS, tm, HEAD_PAD), F32), pltpu.VMEM((HEADS, tm, HEAD_PAD), BF16),
                        pltpu.VMEM((HEADS, tm, HEAD_PAD), F32)],
        input_output_aliases={n_before: 0} if n_into else {}, compiler_params=_params(1),
    )(proj, *tabs, wq, wk, wv, kvaw, qaw, qnw, knw, *([dq] if with_q else []), dk, dv, *(init or []),
      *([dp_into] if n_into else []))


def _attn_fwd(q, k, v, tq, exch=None):
    nb, _, s, _ = q.shape
    sk = k.shape[2]
    nq = s // tq

    def body(q_ref, k_ref, v_ref, o_ref, lse_ref, vext_ref):
        @pl.when(pl.program_id(2) == 0)
        def _():
            vext_ref[:, 0:HEAD_PAD] = v_ref[...]
            vext_ref[:, HEAD_PAD:2 * HEAD_PAD] = jnp.ones((sk, HEAD_PAD), BF16)

        lane = lax.broadcasted_iota(jnp.int32, (tq, HEAD_PAD), 1)
        outs = []
        for hh in range(2):
            sc = _dot_nt(q_ref[hh], k_ref[hh])
            m = jnp.max(sc, axis=-1, keepdims=True)
            pv = _dot(jnp.exp2(sc - m).astype(BF16), vext_ref[...])
            l = pv[:, HEAD_PAD:HEAD_PAD + 1]
            outs.append(pv[:, 0:HEAD_PAD] / l)
            lse_ref[hh] = m + jnp.log2(l)
        o_ref[...] = jnp.where(lane < V_HEAD, outs[0], outs[1]).astype(BF16)

    (o, lse), got = _hosted_call(
        body, "attn_fwd", (nb, HEADS // 2, nq),
        [pl.BlockSpec((None, 2, tq, HEAD_PAD), lambda b, j, i: (b, j, i, 0)),
         pl.BlockSpec((None, 2, sk, HEAD_PAD), lambda b, j, i: (b, j, 0, 0)),
         pl.BlockSpec((None, None, sk, HEAD_PAD), lambda b, j, i: (b, j, 0, 0))],
        [pl.BlockSpec((tq, HEAD_PAD), lambda b, j, i: (b * nq + i, j)),
         pl.BlockSpec((None, 2, tq, 1), lambda b, j, i: (b, j, i, 0))],
        [_sds((nb * s, MLA_W + GMLP_W), BF16), _sds((nb, HEADS, s, 1), F32)], (q, k, v),
        scratch=[pltpu.VMEM((sk, 2 * HEAD_PAD), BF16)], exch=exch)
    return o, lse, got


def _attn_bwd(q, k, v, do, o, lse, tq, exch=None):
    nb, _, s, _ = q.shape
    sk = k.shape[2]
    nq = s // tq

    def body(q_ref, k_ref, v_ref, do_ref, o_ref, lse_ref, dq_ref, dk_out, dv_out, dkt_ref, dvt_ref):
        @pl.when(pl.program_id(2) == 0)
        def _():
            dkt_ref[...] = jnp.zeros_like(dkt_ref)
            dvt_ref[...] = jnp.zeros_like(dvt_ref)

        lane = lax.broadcasted_iota(jnp.int32, (tq, HEAD_PAD), 1)
        dov = do_ref[...]
        prod = dov.astype(F32) * o_ref[...].astype(F32)
        for hh in range(2):
            mine = (lane < V_HEAD) if hh == 0 else (lane >= V_HEAD)
            doh = jnp.where(mine, dov, jnp.zeros_like(dov))
            delta = jnp.sum(jnp.where(mine, prod, 0.0), axis=-1, keepdims=True)
            qh = q_ref[hh]
            q_ln2 = (qh.astype(F32) * LN2).astype(BF16)
            kv = k_ref[hh]
            p = jnp.exp2(_dot_nt(qh, kv) - lse_ref[hh])
            u = (p * (_dot_nt(doh, v_ref[...]) - delta)).astype(BF16)
            dq_ref[hh] = (_dot(u, kv) * LN2).astype(BF16)
            dkt_ref[hh] += _dot_tn(q_ln2, u)
            dvt_ref[...] += _dot_tn(doh, p.astype(BF16))

        @pl.when(pl.program_id(2) == nq - 1)
        def _():
            dk_out[...] = dkt_ref[...].astype(BF16)
            dv_out[...] = dvt_ref[...].astype(BF16)

    qspec = pl.BlockSpec((None, 2, tq, HEAD_PAD), lambda b, j, i: (b, j, i, 0))
    kspec = pl.BlockSpec((None, 2, sk, HEAD_PAD), lambda b, j, i: (b, j, 0, 0))
    vspec = pl.BlockSpec((None, None, sk, HEAD_PAD), lambda b, j, i: (b, j, 0, 0))
    ospec = pl.BlockSpec((tq, HEAD_PAD), lambda b, j, i: (b * nq + i, j))
    return _hosted_call(
        body, "attn_bwd", (nb, HEADS // 2, nq),
        [qspec, kspec, vspec, ospec, ospec, pl.BlockSpec((None, 2, tq, 1), lambda b, j, i: (b, j, i, 0))],
        [qspec, pl.BlockSpec((None, 2, HEAD_PAD, sk), lambda b, j, i: (b, j, 0, 0)),
         pl.BlockSpec((None, None, HEAD_PAD, sk), lambda b, j, i: (b, j, 0, 0))],
        [_sds(q.shape, BF16), _sds((nb, HEADS, HEAD_PAD, sk), BF16), _sds((nb, HEADS // 2, HEAD_PAD, sk), BF16)],
        (q, k, v, do, o, lse), scratch=[pltpu.VMEM((2, HEAD_PAD, sk), F32), pltpu.VMEM((HEAD_PAD, sk), F32)], exch=exch)


def _group_masks(rows):
    lane = lax.broadcasted_iota(jnp.int32, (rows, GMLP_W), 1)
    return [(lane >= g * GROUP_DIM) & (lane < (g + 1) * GROUP_DIM) for g in range(GROUPS)]


def _gmlp_fwd(proj, mixcat, wcat, bias, vnw, ones, tm):
    t = mixcat.shape[0]

    def body(u_ref, v_ref, wcat_ref, bias_ref, vnw_ref, ones_ref, _, o_ref):
        masks = _group_masks(CHUNK)
        gv = _gelu(v_ref[...].astype(F32))
        rv = lax.rsqrt(_group_sum(gv * gv, ones_ref) * (1.0 / GROUP_DIM) + EPS)
        vnb = (gv * rv * vnw_ref[...]).astype(BF16)
        for c in range(tm // CHUNK):
            rows = slice(c * CHUNK, (c + 1) * CHUNK)
            vc = vnb[rows]
            stack = jnp.concatenate([jnp.where(m, vc, jnp.zeros_like(vc)) for m in masks], axis=0)
            sp = _dot(wcat_ref[...], stack) + bias_ref[...]
            o_ref[rows, :] = (_gelu(u_ref[rows, :].astype(F32)) * sp).astype(BF16)

    return pl.pallas_call(
        body, name="gmlp_fwd", grid=(t // tm,),
        in_specs=[pl.BlockSpec((tm, GMLP_W), lambda i: (i, 1)), pl.BlockSpec((tm, GMLP_W), lambda i: (i, 2)),
                  _whole(wcat.shape), _whole(bias.shape), _whole(vnw.shape), _whole(ones.shape),
                  pl.BlockSpec(memory_space=pl.ANY)],
        out_specs=pl.BlockSpec((tm, GMLP_W), lambda i: (i, 1)),
        out_shape=_sds(mixcat.shape, BF16), input_output_aliases={6: 0}, compiler_params=_params(1),
    )(proj, proj, wcat, bias, vnw, ones, mixcat)


def _gmlp_bwd(proj, dsg, wcat, wcat_t, bias, vnw, ones, tm):
    t = dsg.shape[0]

    def body(u_ref, v_ref, dsg_ref, wcat_ref, wcatt_ref, bias_ref, vnw_ref, ones_ref,
             duv_ref, dws_ref, dbs_ref, dvnw_ref):
        @pl.when(pl.program_id(0) == 0)
        def _():
            dws_ref[...] = jnp.zeros_like(dws_ref)
            dbs_ref[...] = jnp.zeros_like(dbs_ref)
            dvnw_ref[...] = jnp.zeros_like(dvnw_ref)

        masks = _group_masks(CHUNK)
        v = v_ref[...].astype(F32)
        gv = _gelu(v)
        rv = lax.rsqrt(_group_sum(gv * gv, ones_ref) * (1.0 / GROUP_DIM) + EPS)
        xh = gv * rv
        vnb = (xh * vnw_ref[...]).astype(BF16)
        dvn_parts = []
        for c in range(tm // CHUNK):
            rows = slice(c * CHUNK, (c + 1) * CHUNK)
            vc = vnb[rows]
            stack = jnp.concatenate([jnp.where(m, vc, jnp.zeros_like(vc)) for m in masks], axis=0)
            sp = _dot(wcat_ref[...], stack) + bias_ref[...]
            u = u_ref[rows, :].astype(F32)
            dsg_c = dsg_ref[rows, :].astype(F32)
            duv_ref[rows, 0:GMLP_W] = (dsg_c * sp * _gelu_grad(u)).astype(BF16)
            ds = dsg_c * _gelu(u)
            dstack = jnp.concatenate([jnp.where(m, ds, 0.0) for m in masks], axis=0)
            dbs_ref[...] += jnp.broadcast_to(jnp.sum(dstack, axis=-1, keepdims=True), dbs_ref.shape)
            dstb = dstack.astype(BF16)
            dvn_parts.append(_dot(wcatt_ref[...], dstb))
            dws_ref[...] += _dot_nt(dstb, vc)
        dvn = jnp.concatenate(dvn_parts, axis=0) if len(dvn_parts) > 1 else dvn_parts[0]
        dvnw_ref[...] += jnp.sum(dvn * xh, axis=0, keepdims=True)
        dxh = dvn * vnw_ref[...]
        gm = _group_sum(dxh * xh, ones_ref) * (1.0 / GROUP_DIM)
        duv_ref[:, GMLP_W:2 * GMLP_W] = (rv * (dxh - xh * gm) * _gelu_grad(v)).astype(BF16)

    row = pl.BlockSpec((tm, GMLP_W), lambda i: (i, 0))
    return pl.pallas_call(
        body, name="gmlp_bwd", grid=(t // tm,),
        in_specs=[pl.BlockSpec((tm, GMLP_W), lambda i: (i, 1)), pl.BlockSpec((tm, GMLP_W), lambda i: (i, 2)), row,
                  _whole(wcat.shape), _whole(wcat_t.shape), _whole(bias.shape), _whole(vnw.shape), _whole(ones.shape)],
        out_specs=[pl.BlockSpec((tm, 2 * GMLP_W), lambda i: (i, 0)), pl.BlockSpec((GROUPS * CHUNK, CHUNK), lambda i: (0, 0)),
                   pl.BlockSpec((GROUPS * CHUNK, CHUNK), lambda i: (0, 0)), pl.BlockSpec((1, GMLP_W), lambda i: (0, 0))],
        out_shape=[_sds((t, 2 * GMLP_W), BF16), _sds((GROUPS * CHUNK, CHUNK), F32), _sds((GROUPS * CHUNK, CHUNK), F32),
                   _sds((1, GMLP_W), F32)],
        compiler_params=_params(1),
    )(proj, proj, dsg, wcat, wcat_t, bias, vnw, ones)


def _mixout_fwd(mixcat, xs, mod, wout, s, tm):
    t, width = mixcat.shape
    d = xs.shape[1]

    def body(cat_ref, x_ref, mod_ref, w_ref, x2_ref, mix_ref):
        g = (pl.program_id(0) * tm) // s
        gate = mod_ref[g, pl.ds(5, 1), :]
        mix = _dot(cat_ref[...], w_ref[...])
        x2_ref[...] = x_ref[...] + gate * mix
        mix_ref[...] = mix.astype(BF16)

    row = lambda i: (i, 0)
    return pl.pallas_call(
        body, name="mixout_fwd", grid=(t // tm,),
        in_specs=[pl.BlockSpec((tm, width), row), pl.BlockSpec((tm, d), row), _whole(mod.shape), _whole(wout.shape)],
        out_specs=[pl.BlockSpec((tm, d), row), pl.BlockSpec((tm, d), row)],
        out_shape=[_sds((t, d), F32), _sds((t, d), BF16)], compiler_params=_params(1),
    )(mixcat, xs, mod, wout)


def _mixout_bwd(dx2, mix, mod, wout, s, tm):
    t, d = dx2.shape

    def body(dx_ref, mix_ref, mod_ref, w_ref, dmix_ref, do_ref, dsg_ref, dmod_ref):
        i = pl.program_id(0)

        @pl.when(i == 0)
        def _():
            dmod_ref[...] = jnp.zeros_like(dmod_ref)

        g = (i * tm) // s
        gate = mod_ref[g, pl.ds(5, 1), :]
        dx = dx_ref[...]
        dmod_ref[g, pl.ds(5, 1), :] += jnp.sum(dx * mix_ref[...].astype(F32), axis=0, keepdims=True)
        dmb = (gate * dx).astype(BF16)
        dmix_ref[...] = dmb
        do_ref[...] = _dot_nt(dmb, w_ref[0:MLA_W, :]).astype(BF16)
        dsg_ref[...] = _dot_nt(dmb, w_ref[MLA_W:MLA_W + GMLP_W, :]).astype(BF16)

    row = lambda i: (i, 0)
    return pl.pallas_call(
        body, name="mixout_bwd", grid=(t // tm,),
        in_specs=[pl.BlockSpec((tm, d), row), pl.BlockSpec((tm, d), row), _whole(mod.shape), _whole(wout.shape)],
        out_specs=[pl.BlockSpec((tm, d), row), pl.BlockSpec((tm, MLA_W), row), pl.BlockSpec((tm, GMLP_W), row),
                   pl.BlockSpec(mod.shape, lambda i: (0, 0, 0))],
        out_shape=[_sds((t, d), BF16), _sds((t, MLA_W), BF16), _sds((t, GMLP_W), BF16), _sds(mod.shape, F32)],
        compiler_params=_params(1),
    )(dx2, mix, mod, wout)


def _swap_cores(parts, name):
    n = len(parts)

    def body(*refs):
        srcs, outs, send_sems, recv_sems = refs[:n], refs[n:2 * n], refs[2 * n], refs[2 * n + 1]
        x, y, c = lax.axis_index("x"), lax.axis_index("y"), lax.axis_index("c")
        copies = [pltpu.make_async_remote_copy(
            src_ref=srcs[w], dst_ref=outs[w], send_sem=send_sems.at[w], recv_sem=recv_sems.at[w],
            device_id=(x, y, 1 - c), device_id_type=pl.DeviceIdType.MESH) for w in range(n)]
        for cp in copies:
            cp.start()
        for cp in copies:
            cp.wait()

    any_spec = pl.BlockSpec(memory_space=pl.ANY)
    return pl.pallas_call(
        body, name=name, in_specs=[any_spec] * n, out_specs=[any_spec] * n,
        out_shape=[_sds(p.shape, p.dtype) for p in parts],
        scratch_shapes=[pltpu.SemaphoreType.DMA((n,)), pltpu.SemaphoreType.DMA((n,))],
    )(*parts)


def _row_tile(r, c, mult):
    return _div_tile(r, max(mult, (1 << 18) // c), mult)


def _sum_slots(recv, name):
    _, r, c = recv.shape
    tr = _row_tile(r, c, 16)

    def body(r_ref, o_ref):
        f = lambda k: r_ref[k].astype(F32)
        o_ref[...] = ((f(0) + f(1)) + f(2)) + f(3)

    return pl.pallas_call(
        body, name=name, grid=(r // tr,),
        in_specs=[pl.BlockSpec((N_CHIPS, tr, c), lambda i: (0, i, 0))],
        out_specs=pl.BlockSpec((tr, c), lambda i: (i, 0)),
        out_shape=_sds((r, c), F32), compiler_params=_params(1),
    )(recv)


def _adamw(parts, w, m, v, name, exch=None, swap=None):
    r, wd = w.shape
    tr = _row_tile(r, wd, 8)
    c1 = 1.0 / (1.0 - ADAM_B1 ** ADAM_STEP)
    c2 = 1.0 / (1.0 - ADAM_B2 ** ADAM_STEP)
    n_p = len(parts)

    def body(*refs):
        p_refs = refs[:n_p]
        w_ref, m_ref, v_ref, g_ref, d_ref, nm_ref, nv_ref = refs[n_p:]
        g = p_refs[0][...]
        for p_ref in p_refs[1:]:
            g = g + p_ref[...]
        nm = ADAM_B1 * m_ref[...] + (1.0 - ADAM_B1) * g
        nv = ADAM_B2 * v_ref[...] + (1.0 - ADAM_B2) * (g * g)
        g_ref[...] = g
        nm_ref[...] = nm
        nv_ref[...] = nv
        d_ref[...] = -ADAM_LR * ((nm * c1) / (jnp.sqrt(nv * c2) + ADAM_EPS) + ADAM_WD * w_ref[...])

    spec = pl.BlockSpec((tr, wd), lambda i: (i, 0))
    return _hosted_call(body, name, (r // tr,), [spec] * (n_p + 3), [spec] * 4, [_sds((r, wd), F32)] * 4,
                        (*parts, w, m, v), exch=exch, swap=swap)


def _all_peers(x, y, c):
    flips = [(dx, dy, dc) for dx in (0, 1) for dy in (0, 1) for dc in (0, 1)][1:]
    return [(1 - x if dx else x, 1 - y if dy else y, 1 - c if dc else c) for dx, dy, dc in flips]


def _first_exchange(shards, later, cc, w, b):
    n_w, n_l = len(shards), len(later)
    n = w.shape[1]

    def body(*refs):
        src32, later_in, (cc_ref, w_ref, b_ref) = refs[:n_w], refs[n_w:n_w + n_l], refs[n_w + n_l:n_w + n_l + 3]
        o0 = n_w + n_l + 3
        outs, (all_ref, tab_ref), later_out = refs[o0:o0 + n_w], refs[o0 + n_w:o0 + n_w + 2], refs[o0 + n_w + 2:o0 + n_w + 2 + n_l]
        s0 = o0 + n_w + 2 + n_l
        srcs = refs[s0:s0 + n_w]
        (part_ref, ici_send, ici_recv, d2d_send, d2d_recv, local_sems, cc_send, cc_recv, tab_send,
         tab_recv) = refs[s0 + n_w:]
        for wi in range(n_w):
            srcs[wi][...] = src32[wi][...].astype(BF16)
        x, y, c = lax.axis_index("x"), lax.axis_index("y"), lax.axis_index("c")
        chip, dev = 2 * x + y, 4 * x + 2 * y + c
        chips = _other_chips(x, y)
        peers = _all_peers(x, y, c)

        def half(wi, which):
            hr = shards[wi].shape[0] // 2
            return pl.ds(pl.multiple_of(which * hr, 16), hr)

        def over_ici(wi, k, arriving):
            px, py = chips[k]
            slot = 2 * px + py if arriving else chip
            return pltpu.make_async_remote_copy(
                src_ref=srcs[wi].at[half(wi, c)], dst_ref=outs[wi].at[slot, half(wi, c)],
                send_sem=ici_send.at[3 * wi + k], recv_sem=ici_recv.at[3 * wi + k], device_id=(px, py, c),
                device_id_type=pl.DeviceIdType.MESH)

        def to_sibling(wi, k, arriving):
            px, py = chips[k]
            rows = half(wi, 1 - c if arriving else c)
            return pltpu.make_async_remote_copy(
                src_ref=outs[wi].at[2 * px + py, rows], dst_ref=outs[wi].at[2 * px + py, rows],
                send_sem=d2d_send.at[3 * wi + k], recv_sem=d2d_recv.at[3 * wi + k], device_id=(x, y, 1 - c),
                device_id_type=pl.DeviceIdType.MESH)

        def cc_copy(k, peer, slot):
            return pltpu.make_async_remote_copy(
                src_ref=cc_ref, dst_ref=all_ref.at[slot], send_sem=cc_send.at[k], recv_sem=cc_recv.at[k],
                device_id=peer, device_id_type=pl.DeviceIdType.MESH)

        def rows_of(px, py):
            return part_ref.at[pl.ds(pl.multiple_of((4 * px + 2 * py + c) * MOD_ROWS, MOD_ROWS), MOD_ROWS)]

        def tab_copy(k, px, py, slot):
            return pltpu.make_async_remote_copy(
                src_ref=rows_of(px, py), dst_ref=tab_ref.at[slot], send_sem=tab_send.at[k], recv_sem=tab_recv.at[k],
                device_id=(px, py, c), device_id_type=pl.DeviceIdType.MESH)

        local = [pltpu.make_async_copy(srcs[wi], outs[wi].at[chip], local_sems.at[wi]) for wi in range(n_w)]
        for cp in local:
            cp.start()
        pairs = [(wi, k) for wi in range(n_w) for k in range(3)]
        for wi, k in pairs:
            over_ici(wi, k, False).start()
        for k, peer in enumerate(peers):
            cc_copy(k, peer, dev).start()
        all_ref[dev] = cc_ref[...]
        for k, (px, py, pc) in enumerate(peers):
            cc_copy(k, (px, py, pc), 4 * px + 2 * py + pc).wait_recv()
        cv = all_ref[...].reshape(8 * MOD_ROWS, cc.shape[1])
        part_ref[...] = _dot((cv * _sigmoid(cv)).astype(BF16), w_ref[...]) + b_ref[...]
        for k, (px, py) in enumerate(chips):
            tab_copy(k, px, py, chip).start()
        tab_ref[chip] = rows_of(x, y)[...]
        for k, (px, py) in enumerate(chips):
            tab_copy(k, px, py, 2 * px + py).wait_recv()
        for j in range(n_l):
            later_out[j][...] = later_in[j][...].astype(BF16)
        for wi, k in pairs:
            over_ici(wi, k, True).wait_recv()
            to_sibling(wi, k, False).start()
        for wi, k in pairs:
            to_sibling(wi, k, True).wait_recv()
        for wi, k in pairs:
            over_ici(wi, k, False).wait_send()
            to_sibling(wi, k, False).wait_send()
        for k, peer in enumerate(peers):
            cc_copy(k, peer, dev).wait_send()
        for k, (px, py) in enumerate(chips):
            tab_copy(k, px, py, chip).wait_send()
        for cp in local:
            cp.wait()

    any_spec = pl.BlockSpec(memory_space=pl.ANY)
    vmem = pl.BlockSpec(memory_space=pltpu.VMEM)
    sems3 = pltpu.SemaphoreType.DMA((3 * n_w,))
    got = pl.pallas_call(
        body, name="first_exchange", in_specs=[vmem] * (n_w + n_l + 3),
        out_specs=[any_spec] * n_w + [vmem] * (2 + n_l),
        out_shape=[_sds((N_CHIPS,) + a.shape, BF16) for a in shards]
        + [_sds((8,) + cc.shape, F32), _sds((N_CHIPS, MOD_ROWS, n), F32)] + [_sds(a.shape, BF16) for a in later],
        scratch_shapes=[pltpu.VMEM(a.shape, BF16) for a in shards]
        + [pltpu.VMEM((8 * MOD_ROWS, n), F32), sems3, sems3, sems3, sems3, pltpu.SemaphoreType.DMA((n_w,)),
           pltpu.SemaphoreType.DMA((7,)), pltpu.SemaphoreType.DMA((7,)), pltpu.SemaphoreType.DMA((3,)),
           pltpu.SemaphoreType.DMA((3,))],
        compiler_params=pltpu.CompilerParams(vmem_limit_bytes=V7X_VMEM_LIMIT),
    )(*shards, *later, cc, w, b)
    return got[:n_w], got[n_w], got[n_w + 1], got[n_w + 2:]


def _ada_bwd_tp(cc_all, dmods, w, ctx_row):
    d, n = w.shape

    def body(cc_ref, m0, m1, m2, m3, w_ref, dw_ref, db_ref, dctx_ref, stage_ref, all_ref, send_sems, recv_sems):
        x, y, c = lax.axis_index("x"), lax.axis_index("y"), lax.axis_index("c")
        me = 4 * x + 2 * y + c
        dsum = m0[...] + m1[...] + m2[...] + m3[...]
        db_ref[...] = jnp.sum(dsum, axis=0, keepdims=True)
        for j in range(N_CHIPS):
            stage_ref[j] = dsum[:, j * n:(j + 1) * n]

        def copy(k, peer, slot):
            px, py, _ = peer
            return pltpu.make_async_remote_copy(
                src_ref=stage_ref.at[2 * px + py], dst_ref=all_ref.at[slot], send_sem=send_sems.at[k],
                recv_sem=recv_sems.at[k], device_id=peer, device_id_type=pl.DeviceIdType.MESH)

        peers = _all_peers(x, y, c)
        for k, peer in enumerate(peers):
            copy(k, peer, me).start()
        all_ref[me] = stage_ref[2 * x + y]
        for k, (px, py, pc) in enumerate(peers):
            copy(k, (px, py, pc), 4 * px + 2 * py + pc).wait_recv()
        for k, peer in enumerate(peers):
            copy(k, peer, me).wait_send()
        cv = cc_ref[...]
        sig = _sigmoid(cv)
        dmb = all_ref[...].reshape(8 * MOD_ROWS, n).astype(BF16)
        dw_ref[...] = _dot_tn((cv * sig).astype(BF16), dmb)
        dsc = _dot_nt(dmb, w_ref[...])
        dctx = dsc[ctx_row:ctx_row + 1, :]
        for dev in range(1, 8):
            dctx = dctx + dsc[dev * MOD_ROWS + ctx_row:dev * MOD_ROWS + ctx_row + 1, :]
        cx = cv[ctx_row:ctx_row + 1, :]
        sx = sig[ctx_row:ctx_row + 1, :]
        dctx_ref[...] = dctx * (sx * (1.0 + cx * (1.0 - sx))) * jnp.where(c == 0, 1.0, 0.0)

    vmem = pl.BlockSpec(memory_space=pltpu.VMEM)
    return pl.pallas_call(
        body, name="ada_bwd_tp", in_specs=[vmem] * 6, out_specs=[vmem] * 3,
        out_shape=[_sds((d, n), F32), _sds((1, N_MOD * d), F32), _sds((1, d), F32)],
        scratch_shapes=[pltpu.VMEM((N_CHIPS, MOD_ROWS, n), F32), pltpu.VMEM((8, MOD_ROWS, n), F32),
                        pltpu.SemaphoreType.DMA((7,)), pltpu.SemaphoreType.DMA((7,))],
        compiler_params=pltpu.CompilerParams(vmem_limit_bytes=V7X_VMEM_LIMIT),
    )(cc_all, *dmods, w)


def _rope_tables(s, ctx):
    pos = np.arange(s, dtype=np.float32)
    inv = (np.float32(ROPE_BASE) ** (-np.arange(0, QK_ROPE // 2, 2, dtype=np.float32) / np.float32(QK_ROPE // 2)))
    ang_r = np.floor(pos / GRID_W)[:, None] * inv
    ang_c = (pos - GRID_W * np.floor(pos / GRID_W))[:, None] * inv
    ang = np.concatenate([ang_r, ang_r, ang_c, ang_c], axis=-1).astype(np.float32)
    cos, sin = np.cos(ang), np.sin(ang)
    half_b = (np.arange(QK_ROPE) // 8) % 2 == 1
    sin_a = np.where(half_b, sin, 0.0)
    sin_b = np.where(half_b, 0.0, -sin)

    def place(tab, fill):
        full = np.full((s + ctx, HEAD_PAD), fill, np.float32)
        full[:s, QK_NOPE:QK_HEAD] = tab
        return jnp.asarray(full)

    return place(cos, 1.0), place(sin_a, 0.0), place(sin_b, 0.0)


def _pad_last(a, n):
    return jnp.pad(a, [(0, 0)] * (a.ndim - 1) + [(0, n - a.shape[-1])])


def _flat_rows(parts, rows, width):
    flat = jnp.concatenate([p.reshape(-1) for p in parts])
    return jnp.pad(flat, (0, rows * width - flat.shape[0])).reshape(rows, width)


def kernel(x, c, ctx, c_ctx, w_ada, b_ada, norm1_w, ffn1_w1, ffn1_w3, ffn1_w2, norm2_w, w_in, q_a_norm_w, w_uq, kv_a_norm_w, w_ukv, q_norm_w, k_norm_w, v_norm_w, w_s, b_s, w_out, norm3_w, ffn2_w1, ffn2_w3, ffn2_w2, loss_target, m_c_ctx, m_w_ada, m_b_ada, m_norm1_w, m_ffn1_w1, m_ffn1_w3, m_ffn1_w2, m_norm2_w, m_w_in, m_q_a_norm_w, m_w_uq, m_kv_a_norm_w, m_w_ukv, m_q_norm_w, m_k_norm_w, m_v_norm_w, m_w_s, m_b_s, m_w_out, m_norm3_w, m_ffn2_w1, m_ffn2_w3, m_ffn2_w2, v_c_ctx, v_w_ada, v_b_ada, v_norm1_w, v_ffn1_w1, v_ffn1_w3, v_ffn1_w2, v_norm2_w, v_w_in, v_q_a_norm_w, v_w_uq, v_kv_a_norm_w, v_w_ukv, v_q_norm_w, v_k_norm_w, v_v_norm_w, v_w_s, v_b_s, v_w_out, v_norm3_w, v_ffn2_w1, v_ffn2_w3, v_ffn2_w2):
    wts = dict(c_ctx=c_ctx, w_ada=w_ada, b_ada=b_ada, norm1_w=norm1_w, ffn1_w1=ffn1_w1, ffn1_w3=ffn1_w3, ffn1_w2=ffn1_w2,
               norm2_w=norm2_w, w_in=w_in, q_a_norm_w=q_a_norm_w, w_uq=w_uq, kv_a_norm_w=kv_a_norm_w, w_ukv=w_ukv,
               q_norm_w=q_norm_w, k_norm_w=k_norm_w, v_norm_w=v_norm_w, w_s=w_s, b_s=b_s, w_out=w_out, norm3_w=norm3_w,
               ffn2_w1=ffn2_w1, ffn2_w3=ffn2_w3, ffn2_w2=ffn2_w2)
    moms = dict(c_ctx=m_c_ctx, w_ada=m_w_ada, b_ada=m_b_ada, norm1_w=m_norm1_w, ffn1_w1=m_ffn1_w1, ffn1_w3=m_ffn1_w3,
                ffn1_w2=m_ffn1_w2, norm2_w=m_norm2_w, w_in=m_w_in, q_a_norm_w=m_q_a_norm_w, w_uq=m_w_uq,
                kv_a_norm_w=m_kv_a_norm_w, w_ukv=m_w_ukv, q_norm_w=m_q_norm_w, k_norm_w=m_k_norm_w, v_norm_w=m_v_norm_w,
                w_s=m_w_s, b_s=m_b_s, w_out=m_w_out, norm3_w=m_norm3_w, ffn2_w1=m_ffn2_w1, ffn2_w3=m_ffn2_w3,
                ffn2_w2=m_ffn2_w2)
    vars_ = dict(c_ctx=v_c_ctx, w_ada=v_w_ada, b_ada=v_b_ada, norm1_w=v_norm1_w, ffn1_w1=v_ffn1_w1, ffn1_w3=v_ffn1_w3,
                 ffn1_w2=v_ffn1_w2, norm2_w=v_norm2_w, w_in=v_w_in, q_a_norm_w=v_q_a_norm_w, w_uq=v_w_uq,
                 kv_a_norm_w=v_kv_a_norm_w, w_ukv=v_w_ukv, q_norm_w=v_q_norm_w, k_norm_w=v_k_norm_w, v_norm_w=v_v_norm_w,
                 w_s=v_w_s, b_s=v_b_s, w_out=v_w_out, norm3_w=v_norm3_w, ffn2_w1=v_ffn2_w1, ffn2_w3=v_ffn2_w3,
                 ffn2_w2=v_ffn2_w2)

    nb, s, d = x.shape
    nctx = ctx.shape[1]
    t, tc = nb * s, nb * nctx
    t_all = t + tc
    sk = s + nctx
    assert nb + 1 <= MOD_ROWS and d % LANES == 0
    tm = _token_tile(s, nctx)
    tq = _div_tile(s, 512, tm)
    tmx = _div_tile(math.gcd(s, tc), 1024, tm)
    tmo = _div_tile(s, 1024, tm)

    def held(n, a_):
        return jnp.swapaxes(a_[0], 0, 1) if n in T_WEIGHTS else a_[0]

    def unheld(n, a_):
        return (jnp.swapaxes(a_, 0, 1) if n in T_WEIGHTS else a_)[None]

    shard = {"w_ada": w_ada[0].astype(BF16)}
    full = {}

    def unshard(names, blocks):
        for n, g4 in zip(names, blocks):
            _, r_, c_ = g4.shape
            if n in ROW_SHARDED or n in T_WEIGHTS:
                full[n] = g4.reshape(N_CHIPS * r_, c_)
            else:
                full[n] = g4.transpose(1, 0, 2).reshape(r_, N_CHIPS * c_)

    def chip_major(n, g_):
        if n in ROW_SHARDED or n in T_WEIGHTS:
            return g_.reshape(N_CHIPS, g_.shape[0] // N_CHIPS, g_.shape[1]).astype(BF16)
        r_, cols = g_.shape
        return g_.reshape(r_, N_CHIPS, cols // N_CHIPS).transpose(1, 0, 2).astype(BF16)

    cc = jnp.concatenate([c, c_ctx[None, :], jnp.zeros((MOD_ROWS - nb - 1, d), F32)], axis=0)
    n_ada = shard["w_ada"].shape[1]
    assert n_ada % LANES == 0
    my_chip = 2 * lax.axis_index("x") + lax.axis_index("y")
    b_cols = lax.dynamic_slice_in_dim(b_ada, my_chip * n_ada, n_ada, axis=1)
    later = MIX_WEIGHTS + LAST_WEIGHTS
    got, cc_all, table, cast = _first_exchange([held(n, wts[n]) for n in FIRST_WEIGHTS],
                                               [held(n, wts[n]) for n in later], cc, shard["w_ada"], b_cols)
    unshard(FIRST_WEIGHTS, got)
    shard.update(zip(later, cast))
    cc_all = cc_all.reshape(8 * MOD_ROWS, d)
    mod = table.transpose(1, 0, 2).reshape(MOD_ROWS, N_MOD, d)
    wsb = w_s[0].astype(BF16)
    wcat = wsb.transpose(1, 0, 2).reshape(CHUNK, GROUPS * CHUNK)
    wcat_t = wsb.transpose(2, 0, 1).reshape(CHUNK, GROUPS * CHUNK)
    bias = jnp.repeat(b_s[0].T, GROUP_DIM, axis=1)
    vnw = v_norm_w.reshape(1, GMLP_W)
    lane = jnp.arange(GMLP_W)
    ones = (lane[:, None] // GROUP_DIM == lane[None, :] // GROUP_DIM).astype(BF16)
    qnw = _pad_last(q_norm_w, HEAD_PAD)
    knw = _pad_last(k_norm_w, HEAD_PAD)
    tabs = _rope_tables(s, nctx)

    x_lat, x_ctx = x.reshape(t, d), ctx.reshape(tc, d)
    (xs1, a1, b1, y1), got = _ffn_fwd(x_lat, x_ctx, mod, norm1_w, full["ffn1_w1"], full["ffn1_w3"], full["ffn1_w2"], 0, s,
                                      nb, tm, "ffn1_fwd", exch=("gather", [shard[n] for n in MIX_WEIGHTS]))
    unshard(MIX_WEIGHTS, got)
    wi = full["w_in"]
    wp = jnp.concatenate([wi[0:KV_LORA], jnp.zeros((QK_NOPE, d), BF16), wi[KV_LORA:KV_LORA + QK_ROPE],
                          jnp.zeros((HEAD_PAD - QK_HEAD, d), BF16), wi[KV_LORA + QK_ROPE:]], axis=0)
    wq = jnp.pad(full["w_uq"].reshape(HEADS, QK_HEAD, Q_LORA), ((0, 0), (0, HEAD_PAD - QK_HEAD), (0, 0)))
    wkv = full["w_ukv"].reshape(KV_LORA, HEADS, QK_NOPE + V_HEAD)
    wk = _pad_last(wkv[:, :, :QK_NOPE].transpose(1, 0, 2), HEAD_PAD)
    wv = wkv[:, :, QK_NOPE:].reshape(KV_LORA, HEADS // 2, 2 * V_HEAD).transpose(1, 0, 2)
    h2, proj = _mixin_fwd(xs1, mod, norm2_w, wp, s, nb, tmx)
    prep_w = (wq, wk, wv, kv_a_norm_w, q_a_norm_w, qnw, knw)
    q, k_all, v_all = _prep_fwd(proj, 0, nb, s, 0, sk, 0, None, tabs, *prep_w, tmo, True, "prep_fwd")
    k_all, v_all = _prep_fwd(proj, t // tm, nb, nctx, s // tm, sk, s // tm, (k_all, v_all), tabs, *prep_w, tm, False,
                             "prep_ctx_fwd")
    o, lse, got = _attn_fwd(q, k_all, v_all, tq, exch=("gather", [shard[n] for n in LAST_WEIGHTS]))
    unshard(LAST_WEIGHTS, got)
    mixcat = _gmlp_fwd(proj, o, wcat, bias, vnw, ones, tq)
    x2, mix = _mixout_fwd(mixcat, xs1, mod, full["w_out"], s, tmo)
    (dy, a2, b2, y2, loss_part), _ = _ffn_fwd(x2, None, mod, norm3_w, full["ffn2_w1"], full["ffn2_w3"], full["ffn2_w2"], 6,
                                              s, nb, tm, "ffn2_fwd", target=loss_target.reshape(t, d))

    grads, cm, recv = {}, {}, {}

    def scatter_of(names):
        return ("scatter", [cm[n] for n in names])

    (dx2, h3, g2, da2, db2, dyb2, dmod_c, grads["norm3_w"]), _ = _ffn_bwd(
        dy, x2, None, a2, b2, y2, mod, norm3_w, full["ffn2_w1"], full["ffn2_w3"], full["ffn2_w2"], 6, s, nb, tm,
        "ffn2_bwd")
    cm["ffn2_w1"] = chip_major("ffn2_w1", _mm_tn(da2, h3, t, "ffn2_dw1"))
    cm["ffn2_w3"] = chip_major("ffn2_w3", _mm_tn(db2, h3, t, "ffn2_dw3"))
    cm["ffn2_w2"] = chip_major("ffn2_w2", _mm_tn(g2, dyb2, t, "ffn2_dw2"))
    dmix, do, dsg, dmod_b = _mixout_bwd(dx2, mix, mod, full["w_out"], s, tmo)
    cm["w_out"] = chip_major("w_out", _mm_tn(mixcat, dmix, t, "wout_dw"))
    duv, dws, dbs, dvnw = _gmlp_bwd(proj, dsg, wcat, wcat_t, bias, vnw, ones, tq)
    group = LAST_WEIGHTS + ("w_out",)
    (dq, dk, dv), got = _attn_bwd(q, k_all, v_all, do, mixcat, lse, tq, exch=scatter_of(group))
    recv.update(zip(group, got))
    dp0, dwk_c, dwv_c, dkvaw_c, dknw_c = _prep_bwd(
        proj, t // tm, nb, nctx, s // tm, s // tm, t_all, None, tabs, *prep_w, None, dk, dv, None, tm, "prep_ctx_bwd")
    dp0, dwq, dqaw, dqnw, dwk, dwv, dkvaw, dknw = _prep_bwd(
        proj, 0, nb, s, 0, 0, t_all, dp0, tabs, *prep_w, dq, dk, dv, [dwk_c, dwv_c, dkvaw_c, dknw_c], tq, "prep_bwd")
    part, sib = {}, {}
    early = LAST_WEIGHTS + ("w_out",)
    for n in early:
        part[n] = _sum_slots(recv[n], "sum_" + n)
    (dxs1, dmod_a, grads["norm2_w"]), _, got = _mixin_bwd(dp0, duv, xs1, dx2, mod, norm2_w, wp, s, nb, tmx,
                                                          [part[n] for n in early])
    sib.update(zip(early, got))
    dwp = jnp.concatenate([_mm_tn(dp0, h2, t_all, "win_dw_kvq"), _mm_tn(duv, h2, t, "win_dw_uv")], axis=0)
    cm["w_in"] = chip_major("w_in", jnp.concatenate(
        [dwp[0:KV_LORA], dwp[KV_LORA + QK_NOPE:KV_LORA + QK_HEAD], dwp[256:]], axis=0))
    cm["w_uq"] = chip_major("w_uq", dwq[:, :, :QK_HEAD].transpose(0, 2, 1).reshape(HEADS * QK_HEAD, Q_LORA))
    cm["w_ukv"] = chip_major("w_ukv", jnp.concatenate(
        [dwk[:, :, :QK_NOPE].transpose(1, 0, 2),
         dwv.transpose(1, 0, 2).reshape(KV_LORA, HEADS, V_HEAD)], axis=2).reshape(KV_LORA, HEADS * (QK_NOPE + V_HEAD)))
    (dx_lat, h1, g1, da1, db1, dyb1, dmod_0, grads["norm1_w"]), _ = _ffn_bwd(
        dxs1, x_lat, x_ctx, a1, b1, y1, mod, norm1_w, full["ffn1_w1"], full["ffn1_w3"], full["ffn1_w2"], 0, s, nb, tm,
        "ffn1_bwd")
    dmods = [m_.reshape(MOD_ROWS, N_MOD * d) for m_ in (dmod_0, dmod_a, dmod_b, dmod_c)]
    dw_ada, grads["b_ada"], dctx = _ada_bwd_tp(cc_all, dmods, shard["w_ada"], nb)
    grads["c_ctx"] = dctx[0]
    grads["q_a_norm_w"], grads["kv_a_norm_w"] = dqaw, dkvaw
    grads["q_norm_w"], grads["k_norm_w"] = dqnw[:, :QK_HEAD], dknw[:, :QK_HEAD]
    grads["v_norm_w"], grads["w_s"], grads["b_s"] = dvnw, dws, dbs[:, 0]
    grad_x = dx_lat.reshape(nb, s, d)
    n_small = sum(wts[n].size for n in SMALL)
    rows_s = _round_up(-(-(n_small + 1) // d), 16)
    cm["small"] = jnp.broadcast_to(_flat_rows([grads[n] for n in SMALL] + [loss_part], rows_s, d), (N_CHIPS, rows_s, d))
    group = ("w_in", "w_uq", "w_ukv", "small")
    dw2, got = _mm_tn(g1, dyb1, t_all, "ffn1_dw2", exch=scatter_of(group))
    recv.update(zip(group, got))
    cm["ffn1_w2"] = chip_major("ffn1_w2", dw2)
    dw1, got = _mm_tn(da1, h1, t_all, "ffn1_dw1", exch=scatter_of(("ffn1_w2",)))
    recv["ffn1_w2"] = got[0]
    cm["ffn1_w1"] = chip_major("ffn1_w1", dw1)
    dw3, got = _mm_tn(db1, h1, t_all, "ffn1_dw3", exch=scatter_of(("ffn1_w1",)))
    recv["ffn1_w1"] = got[0]
    cm["ffn1_w3"] = chip_major("ffn1_w3", dw3)
    stepped = {}
    reduced = tuple(n for n in SHARDED if n != "w_ada") + ("small",)
    late = tuple(n for n in reduced if n not in early and n != "ffn1_w3")
    for n in late:
        part[n] = _sum_slots(recv[n], "sum_" + n)
    stepped["w_ada"], got, got_sib = _adamw([dw_ada], wts["w_ada"][0], moms["w_ada"][0], vars_["w_ada"][0],
                                            "adamw_w_ada", exch=scatter_of(("ffn1_w3",)), swap=[part[n] for n in late])
    sib.update(zip(late, got_sib))
    part["ffn1_w3"] = _sum_slots(got[0], "sum_ffn1_w3")
    sib["ffn1_w3"] = _swap_cores([part["ffn1_w3"]], "swap_last")[0]
    for n in reduced[:-1]:
        stepped[n], _ = _adamw([part[n], sib[n]], held(n, wts[n]), held(n, moms[n]), held(n, vars_[n]), "adamw_" + n)
    for n in SHARDED:
        stepped[n] = [unheld(n, a_) for a_ in stepped[n]]
    packed, _ = _adamw([part["small"], sib["small"]], _flat_rows([wts[n] for n in SMALL], rows_s, d),
                       _flat_rows([moms[n] for n in SMALL], rows_s, d), _flat_rows([vars_[n] for n in SMALL], rows_s, d),
                       "adamw_small")
    loss = packed[0].reshape(-1)[n_small]
    for n in SMALL:
        stepped[n] = []
    for a_ in packed:
        flat = a_.reshape(-1)
        off = 0
        for n in SMALL:
            stepped[n].append(flat[off:off + wts[n].size].reshape(wts[n].shape))
            off += wts[n].size
    return (loss, grad_x, *[stepped[n][0] for n in WEIGHTS], *[stepped[n][1] for n in WEIGHTS],
            *[stepped[n][2] for n in WEIGHTS], *[stepped[n][3] for n in WEIGHTS])
```

```python
import functools
import math

import jax
import jax.numpy as jnp
import numpy as np
from jax import lax
from jax.experimental import pallas as pl
from jax.experimental.pallas import tpu as pltpu

F32 = jnp.float32
BF16 = jnp.bfloat16

EPS = 1e-6
N_MOD = 9
HEADS = 8
QK_NOPE, QK_ROPE, V_HEAD = 64, 32, 64
QK_HEAD = QK_NOPE + QK_ROPE
HEAD_PAD = 128
LN2 = math.log(2.0)
SOFTMAX_SCALE = QK_HEAD ** -0.5 / LN2
Q_LORA, KV_LORA = 256, 128
GROUPS, GROUP_DIM, CHUNK = 8, 64, 128
GMLP_W = GROUPS * GROUP_DIM
MLA_W = HEADS * V_HEAD
IN_COLS = 1440
PROJ_COLS = 1536
GRID_W = 64
ROPE_BASE = 10000.0
MOD_ROWS = 16
ADAM_LR, ADAM_B1, ADAM_B2, ADAM_EPS, ADAM_WD, ADAM_STEP = 0.001, 0.9, 0.999, 1e-08, 0.01, 10
N_CHIPS = 4
LANES = 128
V7X_VMEM_LIMIT = 56 * 1024 * 1024
GELU_C = math.sqrt(2.0 / math.pi)

SHARDED = ("w_ada", "ffn1_w1", "ffn1_w3", "ffn1_w2", "w_in", "w_uq", "w_ukv", "w_out", "ffn2_w1", "ffn2_w3", "ffn2_w2")
ROW_SHARDED = ("ffn1_w2", "w_out", "ffn2_w2")
T_WEIGHTS = ("ffn1_w1", "ffn1_w3", "ffn2_w1", "ffn2_w3", "w_in", "w_uq")
FIRST_WEIGHTS = ("ffn1_w1", "ffn1_w3", "ffn1_w2")
MIX_WEIGHTS = ("w_in", "w_uq", "w_ukv", "w_out")
LAST_WEIGHTS = ("ffn2_w1", "ffn2_w3", "ffn2_w2")
SMALL = ("c_ctx", "b_ada", "norm1_w", "norm2_w", "q_a_norm_w", "kv_a_norm_w", "q_norm_w", "k_norm_w", "v_norm_w",
         "w_s", "b_s", "norm3_w")
WEIGHTS = ("c_ctx", "w_ada", "b_ada", "norm1_w", "ffn1_w1", "ffn1_w3", "ffn1_w2", "norm2_w", "w_in", "q_a_norm_w",
           "w_uq", "kv_a_norm_w", "w_ukv", "q_norm_w", "k_norm_w", "v_norm_w", "w_s", "b_s", "w_out", "norm3_w",
           "ffn2_w1", "ffn2_w3", "ffn2_w2")


def _round_up(n, m):
    return (n + m - 1) // m * m


def _div_tile(n, target, mult):
    best = None
    for t in range(mult, min(n, target) + 1, mult):
        if n % t == 0:
            best = t
    return n if best is None else best


def _dot(a, b):
    return lax.dot_general(a, b, (((1,), (0,)), ((), ())), preferred_element_type=F32)


def _dot_nt(a, b):
    return lax.dot_general(a, b, (((1,), (1,)), ((), ())), preferred_element_type=F32)


def _dot_tn(a, b):
    return lax.dot_general(a, b, (((0,), (0,)), ((), ())), preferred_element_type=F32)


def _sigmoid(x):
    return 1.0 / (1.0 + jnp.exp(-x))


def _gelu(x):
    return 0.5 * x * (1.0 + jnp.tanh(GELU_C * (x + 0.044715 * x * x * x)))


def _gelu_grad(x):
    t = jnp.tanh(GELU_C * (x + 0.044715 * x * x * x))
    return 0.5 * (1.0 + t) + 0.5 * x * (1.0 - t * t) * (GELU_C * (1.0 + 3 * 0.044715 * x * x))


def _rope3(x, cos, sin_a, sin_b):
    return x * cos + pltpu.roll(x, 8, 2) * sin_a + pltpu.roll(x, HEAD_PAD - 8, 2) * sin_b


def _rope3_t(d, cos, sin_a, sin_b):
    return d * cos + pltpu.roll(d * sin_a, HEAD_PAD - 8, 2) + pltpu.roll(d * sin_b, 8, 2)


def _group_sum(x, ones_ref):
    hi = x.astype(BF16)
    lo = (x - hi.astype(F32)).astype(BF16)
    return _dot(hi, ones_ref[...]) + _dot(lo, ones_ref[...])


def _params(n_axes):
    return pltpu.CompilerParams(dimension_semantics=("arbitrary",) * n_axes, vmem_limit_bytes=V7X_VMEM_LIMIT)


def _whole(shape):
    nd = len(shape)
    return pl.BlockSpec(shape, lambda *_: (0,) * nd, pipeline_mode=pl.Buffered(1))


def _sds(shape, dtype):
    return jax.ShapeDtypeStruct(shape, dtype)


def _token_tile(s, ctx):
    return _div_tile(math.gcd(s, ctx), 256, CHUNK)


def _other_chips(x, y):
    return [(1 - x, y), (x, 1 - y), (1 - x, 1 - y)]


def _exch_copies(kind, srcs, dsts, send_sems, recv_sems, local_sems, with_arrivals):
    x, y, c = lax.axis_index("x"), lax.axis_index("y"), lax.axis_index("c")
    me = 2 * x + y
    local, sends, arrivals = [], [], []
    for w, (src, dst) in enumerate(zip(srcs, dsts)):
        own = src if kind == "gather" else src.at[me]
        local.append(pltpu.make_async_copy(own, dst.at[me], local_sems.at[w]))
        for k, (px, py) in enumerate(_other_chips(x, y)):
            sem = dict(send_sem=send_sems.at[3 * w + k], recv_sem=recv_sems.at[3 * w + k], device_id=(px, py, c),
                       device_id_type=pl.DeviceIdType.MESH)
            out = src if kind == "gather" else src.at[2 * px + py]
            sends.append(pltpu.make_async_remote_copy(src_ref=out, dst_ref=dst.at[me], **sem))
            if with_arrivals:
                arrivals.append(pltpu.make_async_remote_copy(src_ref=own, dst_ref=dst.at[2 * px + py], **sem))
    return local, sends, arrivals


def _exch_start(kind, srcs, dsts, sems):
    local, sends, _ = _exch_copies(kind, srcs, dsts, *sems, with_arrivals=False)
    for cp in local + sends:
        cp.start()


def _exch_wait(kind, srcs, dsts, sems):
    local, sends, arrivals = _exch_copies(kind, srcs, dsts, *sems, with_arrivals=True)
    for cp in arrivals:
        cp.wait_recv()
    for cp in sends:
        cp.wait_send()
    for cp in local:
        cp.wait()


def _exch_scratch(n):
    return [pltpu.SemaphoreType.DMA((3 * n,)), pltpu.SemaphoreType.DMA((3 * n,)), pltpu.SemaphoreType.DMA((n,))]


def _exch_shapes(kind, arrays):
    return [_sds((N_CHIPS,) + a.shape if kind == "gather" else a.shape, a.dtype) for a in arrays]


def _sibling_copies(srcs, dsts, send_sems, recv_sems):
    x, y, c = lax.axis_index("x"), lax.axis_index("y"), lax.axis_index("c")
    return [pltpu.make_async_remote_copy(
        src_ref=src, dst_ref=dst, send_sem=send_sems.at[w], recv_sem=recv_sems.at[w], device_id=(x, y, 1 - c),
        device_id_type=pl.DeviceIdType.MESH) for w, (src, dst) in enumerate(zip(srcs, dsts))]


def _hosted_call(body, name, grid, in_specs, out_specs, out_shape, operands, scratch=(), exch=None, swap=None):
    n_axes = len(grid)
    if exch is None and swap is None:
        outs = pl.pallas_call(body, name=name, grid=grid, in_specs=list(in_specs), out_specs=list(out_specs),
                              out_shape=list(out_shape), scratch_shapes=list(scratch),
                              compiler_params=_params(n_axes))(*operands)
        return list(outs), []
    kind, arrays = exch if exch is not None else ("scatter", [])
    swaps = list(swap or [])
    n_in, n_out, n_sc, n_ex, n_sw = len(in_specs), len(out_specs), len(scratch), len(arrays), len(swaps)

    def hosted(*refs):
        cin, ein, sin = refs[:n_in], refs[n_in:n_in + n_ex], refs[n_in + n_ex:n_in + n_ex + n_sw]
        o0 = n_in + n_ex + n_sw
        cout, eout, sout = refs[o0:o0 + n_out], refs[o0 + n_out:o0 + n_out + n_ex], refs[o0 + n_out + n_ex:o0 + n_out + n_ex + n_sw]
        rest = refs[o0 + n_out + n_ex + n_sw:]
        csc, sems, swap_sems = rest[:n_sc], rest[n_sc:n_sc + 3], rest[n_sc + 3:]
        first = functools.reduce(jnp.logical_and, [pl.program_id(a) == 0 for a in range(n_axes)])
        last = functools.reduce(jnp.logical_and, [pl.program_id(a) == grid[a] - 1 for a in range(n_axes)])

        @pl.when(first)
        def _():
            if n_ex:
                _exch_start(kind, ein, eout, sems)
            for cp in _sibling_copies(sin, sout, *swap_sems) if n_sw else []:
                cp.start()

        body(*cin, *cout, *csc)

        @pl.when(last)
        def _():
            if n_ex:
                _exch_wait(kind, ein, eout, sems)
            for cp in _sibling_copies(sin, sout, *swap_sems) if n_sw else []:
                cp.wait()

    any_spec = pl.BlockSpec(memory_space=pl.ANY)
    swap_scratch = [pltpu.SemaphoreType.DMA((n_sw,)), pltpu.SemaphoreType.DMA((n_sw,))] if n_sw else []
    outs = pl.pallas_call(
        hosted, name=name, grid=grid, in_specs=list(in_specs) + [any_spec] * (n_ex + n_sw),
        out_specs=list(out_specs) + [any_spec] * (n_ex + n_sw),
        out_shape=list(out_shape) + _exch_shapes(kind, arrays) + [_sds(a.shape, a.dtype) for a in swaps],
        scratch_shapes=list(scratch) + _exch_scratch(max(n_ex, 1)) + swap_scratch, compiler_params=_params(n_axes),
    )(*operands, *arrays, *swaps)
    got = list(outs[n_out:n_out + n_ex])
    return (list(outs[:n_out]), got) if swap is None else (list(outs[:n_out]), got, list(outs[n_out + n_ex:]))


class _TokenTiles:
    def __init__(self, t, tc, tm):
        self.n_lat, self.n_ctx = t // tm, tc // tm
        self.n_all = self.n_lat + self.n_ctx

    def tile(self, i):
        return (i + self.n_lat) % self.n_all if self.n_ctx else i

    def is_lat(self, i):
        return self.tile(i) < self.n_lat

    def row(self, i):
        return (self.tile(i), 0)

    def lat_row(self, i):
        return (jnp.where(self.is_lat(i), self.tile(i), 0), 0) if self.n_ctx else (i, 0)

    def ctx_row(self, i):
        return (jnp.where(self.is_lat(i), self.n_ctx - 1, self.tile(i) - self.n_lat), 0)


def _ffn_fwd(x_lat, x_ctx, mod, nw, w1, w3, w2, k0, s, nb, tm, name, target=None, exch=None):
    t, d = x_lat.shape
    tc = 0 if x_ctx is None else x_ctx.shape[0]
    f = w1.shape[0]
    tiles = _TokenTiles(t, tc, tm)
    n_x = 2 if tc else 1
    n_t = 0 if target is None else 1
    assert not (tc and n_t)

    def body(*refs):
        x_ref = refs[0]
        t_ref = refs[n_x] if n_t else None
        mod_ref, nw_ref, w1_ref, w3_ref, w2_ref, o_ref, a_ref, b_ref, y_ref = refs[n_x + n_t:n_x + n_t + 9]
        i = pl.program_id(0)
        g = jnp.minimum((tiles.tile(i) * tm) // s, nb)
        shift = mod_ref[g, pl.ds(k0, 1), :]
        scale = mod_ref[g, pl.ds(k0 + 1, 1), :]
        gate = mod_ref[g, pl.ds(k0 + 2, 1), :]
        x = jnp.where(tiles.is_lat(i), x_ref[...], refs[1][...]) if tc else x_ref[...]
        r = lax.rsqrt(jnp.mean(x * x, axis=-1, keepdims=True) + EPS)
        hb = ((x * r * nw_ref[...]) * (1.0 + scale) + shift).astype(BF16)
        a = _dot_nt(hb, w1_ref[...])
        b = _dot_nt(hb, w3_ref[...])
        gb = (a * _sigmoid(a) * b).astype(BF16)
        y = _dot(gb, w2_ref[...])
        out = x + (0.5 * gate) * y
        a_ref[...] = a.astype(BF16)
        b_ref[...] = b.astype(BF16)
        y_ref[...] = y.astype(BF16)
        if n_t:
            loss_ref, acc_ref = refs[-2:]

            @pl.when(i == 0)
            def _():
                acc_ref[...] = jnp.zeros_like(acc_ref)

            e = out - t_ref[...]
            o_ref[...] = e * (1.0 / d)
            acc_ref[...] += jnp.sum(e * e, axis=0, keepdims=True)

            @pl.when(i == tiles.n_all - 1)
            def _():
                loss_ref[...] = (0.5 / d) * jnp.sum(acc_ref[...], axis=-1, keepdims=True)
        else:
            o_ref[...] = out

    td = pl.BlockSpec((tm, d), tiles.row)
    tf = pl.BlockSpec((tm, f), tiles.row)
    return _hosted_call(
        body, name, (tiles.n_all,),
        [pl.BlockSpec((tm, d), tiles.lat_row)] + ([pl.BlockSpec((tm, d), tiles.ctx_row)] if tc else []) + [td] * n_t
        + [_whole(mod.shape), _whole(nw.shape), _whole(w1.shape), _whole(w3.shape), _whole(w2.shape)],
        [td, tf, tf, td] + [pl.BlockSpec((1, 1), lambda i: (0, 0))] * n_t,
        [_sds((t + tc, d), F32), _sds((t + tc, f), BF16), _sds((t + tc, f), BF16), _sds((t + tc, d), BF16)]
        + [_sds((1, 1), F32)] * n_t,
        (x_lat,) + ((x_ctx,) if tc else ()) + ((target,) if n_t else ()) + (mod, nw, w1, w3, w2),
        scratch=[pltpu.VMEM((1, d), F32)] * n_t, exch=exch)


def _ffn_bwd(dout, x_lat, x_ctx, a, b, y, mod, nw, w1, w3, w2, k0, s, nb, tm, name, exch=None):
    t, d = x_lat.shape
    tc = 0 if x_ctx is None else x_ctx.shape[0]
    f = w1.shape[0]
    nch = 2 if (f // 2) % LANES == 0 and f % 2 == 0 else 1
    fc = f // nch
    tiles = _TokenTiles(t, tc, tm)
    n_x = 2 if tc else 1

    def body(*refs):
        do_ref, x_ref = refs[0], refs[1]
        (a_ref, b_ref, y_ref, mod_ref, nw_ref, w1_ref, w3_ref, w2_ref,
         dx_ref, h_ref, g_ref, da_ref, db_ref, dy_ref, dmod_ref, dnw_ref) = refs[1 + n_x:]
        i = pl.program_id(0)

        @pl.when(i == 0)
        def _():
            dmod_ref[...] = jnp.zeros_like(dmod_ref)
            dnw_ref[...] = jnp.zeros_like(dnw_ref)

        g = jnp.minimum((tiles.tile(i) * tm) // s, nb)
        shift = mod_ref[g, pl.ds(k0, 1), :]
        scale = mod_ref[g, pl.ds(k0 + 1, 1), :]
        gate = mod_ref[g, pl.ds(k0 + 2, 1), :]
        x = jnp.where(tiles.is_lat(i), x_ref[...], refs[2][...]) if tc else x_ref[...]
        dout_v = do_ref[...]
        r = lax.rsqrt(jnp.mean(x * x, axis=-1, keepdims=True) + EPS)
        xh = x * r
        n = xh * nw_ref[...]
        h_ref[...] = (n * (1.0 + scale) + shift).astype(BF16)
        dyb = ((0.5 * gate) * dout_v).astype(BF16)
        dy_ref[...] = dyb
        dmod_ref[g, pl.ds(k0 + 2, 1), :] += 0.5 * jnp.sum(dout_v * y_ref[...].astype(F32), axis=0, keepdims=True)
        dh = jnp.zeros((tm, d), F32)
        for c in range(nch):
            sl = slice(c * fc, (c + 1) * fc)
            dg = _dot_nt(dyb, w2_ref[sl, :])
            av = a_ref[:, sl].astype(F32)
            bv = b_ref[:, sl].astype(F32)
            sig = _sigmoid(av)
            sa = av * sig
            g_ref[:, sl] = (sa * bv).astype(BF16)
            dab = (dg * bv * (sig * (1.0 + av * (1.0 - sig)))).astype(BF16)
            dbb = (dg * sa).astype(BF16)
            da_ref[:, sl] = dab
            db_ref[:, sl] = dbb
            dh = dh + _dot(dab, w1_ref[sl, :]) + _dot(dbb, w3_ref[sl, :])
        dmod_ref[g, pl.ds(k0, 1), :] += jnp.sum(dh, axis=0, keepdims=True)
        dmod_ref[g, pl.ds(k0 + 1, 1), :] += jnp.sum(dh * n, axis=0, keepdims=True)
        dn = dh * (1.0 + scale)
        dnw_ref[...] += jnp.sum(dn * xh, axis=0, keepdims=True)
        dxh = dn * nw_ref[...]
        dx_ref[...] = dout_v + r * (dxh - xh * jnp.mean(dxh * xh, axis=-1, keepdims=True))

    td = pl.BlockSpec((tm, d), tiles.row)
    tf = pl.BlockSpec((tm, f), tiles.row)
    lat = pl.BlockSpec((tm, d), tiles.lat_row)
    ta = t + tc
    return _hosted_call(
        body, name, (tiles.n_all,),
        [td, lat] + ([pl.BlockSpec((tm, d), tiles.ctx_row)] if tc else [])
        + [tf, tf, td, _whole(mod.shape), _whole(nw.shape), _whole(w1.shape), _whole(w3.shape), _whole(w2.shape)],
        [lat, td, tf, tf, tf, td, pl.BlockSpec(mod.shape, lambda i: (0, 0, 0)), pl.BlockSpec((1, d), lambda i: (0, 0))],
        [_sds((t, d), F32), _sds((ta, d), BF16), _sds((ta, f), BF16), _sds((ta, f), BF16), _sds((ta, f), BF16),
         _sds((ta, d), BF16), _sds(mod.shape, F32), _sds((1, d), F32)],
        (dout, x_lat) + ((x_ctx,) if tc else ()) + (a, b, y, mod, nw, w1, w3, w2), exch=exch)


def _mm_tn(a, b, rows, name, exch=None):
    m = a.shape[1]
    n = b.shape[1]
    bm = _div_tile(m, 1408, LANES)
    bn = _div_tile(n, 1408, LANES)
    bk = _div_tile(rows, 2304, LANES)
    nk = rows // bk

    def body(a_ref, b_ref, o_ref, acc_ref):
        k = pl.program_id(2)

        @pl.when(k == 0)
        def _():
            acc_ref[...] = jnp.zeros_like(acc_ref)

        acc_ref[...] += _dot_tn(a_ref[...], b_ref[...])

        @pl.when(k == nk - 1)
        def _():
            o_ref[...] = acc_ref[...].astype(BF16)

    (out,), got = _hosted_call(
        body, name, (m // bm, n // bn, nk),
        [pl.BlockSpec((bk, bm), lambda i, j, k: (k, i)), pl.BlockSpec((bk, bn), lambda i, j, k: (k, j))],
        [pl.BlockSpec((bm, bn), lambda i, j, k: (i, j))], [_sds((m, n), BF16)], (a, b),
        scratch=[pltpu.VMEM((bm, bn), F32)], exch=exch)
    return out if exch is None else (out, got)


def _mixin_fwd(xs, mod, nw, wp, s, nb, tm):
    t, d = xs.shape

    def body(x_ref, mod_ref, nw_ref, wp_ref, h_ref, p_ref):
        g = jnp.minimum((pl.program_id(0) * tm) // s, nb)
        shift = mod_ref[g, pl.ds(3, 1), :]
        scale = mod_ref[g, pl.ds(4, 1), :]
        x = x_ref[...]
        r = lax.rsqrt(jnp.mean(x * x, axis=-1, keepdims=True) + EPS)
        hb = ((x * r * nw_ref[...]) * (1.0 + scale) + shift).astype(BF16)
        h_ref[...] = hb
        p_ref[...] = _dot_nt(hb, wp_ref[...]).astype(BF16)

    row = lambda i: (i, 0)
    return pl.pallas_call(
        body, name="mixin_fwd", grid=(t // tm,),
        in_specs=[pl.BlockSpec((tm, d), row), _whole(mod.shape), _whole(nw.shape), _whole(wp.shape)],
        out_specs=[pl.BlockSpec((tm, d), row), pl.BlockSpec((tm, PROJ_COLS), row)],
        out_shape=[_sds((t, d), BF16), _sds((t, PROJ_COLS), BF16)], compiler_params=_params(1),
    )(xs, mod, nw, wp)


def _mixin_bwd(dp0, duv, xs, dres, mod, nw, wp, s, nb, tm, swap):
    t_all, d = xs.shape
    nlat = dres.shape[0] // tm

    def body(p0_ref, uv_ref, x_ref, dr_ref, mod_ref, nw_ref, wp_ref, dx_ref, dmod_ref, dnw_ref):
        i = pl.program_id(0)

        @pl.when(i == 0)
        def _():
            dmod_ref[...] = jnp.zeros_like(dmod_ref)
            dnw_ref[...] = jnp.zeros_like(dnw_ref)

        lat = i < nlat
        g = jnp.minimum((i * tm) // s, nb)
        scale = mod_ref[g, pl.ds(4, 1), :]
        dh = _dot(p0_ref[...], wp_ref[0:512, :])
        extra = _dot(uv_ref[...], wp_ref[512:1536, :])
        dh = dh + jnp.where(lat, extra, 0.0)
        x = x_ref[...]
        r = lax.rsqrt(jnp.mean(x * x, axis=-1, keepdims=True) + EPS)
        xh = x * r
        n = xh * nw_ref[...]
        dmod_ref[g, pl.ds(3, 1), :] += jnp.sum(dh, axis=0, keepdims=True)
        dmod_ref[g, pl.ds(4, 1), :] += jnp.sum(dh * n, axis=0, keepdims=True)
        dn = dh * (1.0 + scale)
        dnw_ref[...] += jnp.sum(dn * xh, axis=0, keepdims=True)
        dxh = dn * nw_ref[...]
        dx_ref[...] = jnp.where(lat, dr_ref[...], 0.0) + r * (dxh - xh * jnp.mean(dxh * xh, axis=-1, keepdims=True))

    row = lambda i: (i, 0)
    lrow = lambda i: (jnp.minimum(i, nlat - 1), 0)
    return _hosted_call(
        body, "mixin_bwd", (t_all // tm,),
        [pl.BlockSpec((tm, 512), row), pl.BlockSpec((tm, 1024), lrow), pl.BlockSpec((tm, d), row),
         pl.BlockSpec((tm, d), lrow), _whole(mod.shape), _whole(nw.shape), _whole(wp.shape)],
        [pl.BlockSpec((tm, d), row), pl.BlockSpec(mod.shape, lambda i: (0, 0, 0)), pl.BlockSpec((1, d), lambda i: (0, 0))],
        [_sds((t_all, d), F32), _sds(mod.shape, F32), _sds((1, d), F32)], (dp0, duv, xs, dres, mod, nw, wp), swap=swap)


def _prep_fwd(proj, row0, nb, s, pos0, sk, key0, into, tabs, wq, wk, wv, kvaw, qaw, qnw, knw, tm, with_q, name):
    nblk = s // tm
    n_into = 0 if into is None else 2

    def body(p_ref, cos_ref, sa_ref, sb_ref, wq_ref, wk_ref, wv_ref, kvaw_ref, qaw_ref, qnw_ref, knw_ref, *rest):
        outs, heads_ref = rest[n_into:-1], rest[-1]
        q_ref, k_ref, v_ref = outs if with_q else (None,) + outs
        cos, sin_a, sin_b = cos_ref[...][None], sa_ref[...][None], sb_ref[...][None]

        def normed_roped(w_ref, src, extra, nw_ref, o_ref, post):
            for h in range(HEADS):
                heads_ref[h] = _dot_nt(src, w_ref[h]) if extra is None else _dot(src, w_ref[h])
            xp = heads_ref[...] if extra is None else heads_ref[...] + extra[None]
            r = lax.rsqrt(jnp.sum(xp * xp, axis=-1, keepdims=True) * (1.0 / QK_HEAD) + EPS)
            o_ref[...] = _rope3(xp * r * (nw_ref[...] * post)[None], cos, sin_a, sin_b).astype(BF16)

        ckv = p_ref[:, 0:128].astype(F32)
        rkv = lax.rsqrt(jnp.mean(ckv * ckv, axis=-1, keepdims=True) + EPS)
        ckvb = (ckv * rkv * kvaw_ref[...]).astype(BF16)
        normed_roped(wk_ref, ckvb, p_ref[:, 128:256].astype(F32), knw_ref, k_ref, 1.0)
        for j in range(HEADS // 2):
            v_ref[j] = _dot(ckvb, wv_ref[j]).astype(BF16)
        if with_q:
            cq = p_ref[:, 256:512].astype(F32)
            rq = lax.rsqrt(jnp.mean(cq * cq, axis=-1, keepdims=True) + EPS)
            normed_roped(wq_ref, (cq * rq * qaw_ref[...]).astype(BF16), None, qnw_ref, q_ref, SOFTMAX_SCALE)

    tab = pl.BlockSpec((tm, HEAD_PAD), lambda i: (pos0 + i % nblk, 0))
    qspec = pl.BlockSpec((None, HEADS, tm, HEAD_PAD), lambda i: (i // nblk, 0, i % nblk, 0))
    kspec = pl.BlockSpec((None, HEADS, tm, HEAD_PAD), lambda i: (i // nblk, 0, key0 + i % nblk, 0))
    vspec = pl.BlockSpec((None, HEADS // 2, tm, HEAD_PAD), lambda i: (i // nblk, 0, key0 + i % nblk, 0))
    qshape = _sds((nb, HEADS, s, HEAD_PAD), BF16)
    kshape = _sds((nb, HEADS, sk, HEAD_PAD), BF16)
    vshape = _sds((nb, HEADS // 2, sk, HEAD_PAD), BF16)
    n_q = 1 if with_q else 0
    return pl.pallas_call(
        body, name=name, grid=(nb * nblk,),
        in_specs=[pl.BlockSpec((tm, 512), lambda i: (row0 + i, 0)), tab, tab, tab, _whole(wq.shape), _whole(wk.shape),
                  _whole(wv.shape), _whole(kvaw.shape), _whole(qaw.shape), _whole(qnw.shape), _whole(knw.shape)]
        + [pl.BlockSpec(memory_space=pl.ANY)] * n_into,
        out_specs=([qspec] if with_q else []) + [kspec, vspec],
        out_shape=([qshape] if with_q else []) + [kshape, vshape],
        scratch_shapes=[pltpu.VMEM((HEADS, tm, HEAD_PAD), F32)],
        input_output_aliases={11: n_q, 12: n_q + 1} if n_into else {}, compiler_params=_params(1),
    )(proj, *tabs, wq, wk, wv, kvaw, qaw, qnw, knw, *(into or ()))


def _prep_bwd(proj, row0, nb, s, pos0, key0, dp_rows, dp_into, tabs, wq, wk, wv, kvaw, qaw, qnw, knw, dq, dk, dv, init, tm,
              name):
    nblk = s // tm
    with_q = dq is not None
    n_init = 0 if init is None else len(init)
    n_into = 0 if dp_into is None else 1

    def body(*refs):
        p_ref, cos_ref, sa_ref, sb_ref, wq_ref, wk_ref, wv_ref, kvaw_ref, qaw_ref, qnw_ref, knw_ref = refs[:11]
        rest = list(refs[11:])
        dq_ref = rest.pop(0) if with_q else None
        dk_ref, dv_ref = rest.pop(0), rest.pop(0)
        init_refs = [rest.pop(0) for _ in range(n_init)]
        if n_into:
            rest.pop(0)
        dp_ref = rest.pop(0)
        if with_q:
            dwq_ref, dqaw_ref, dqnw_ref = rest.pop(0), rest.pop(0), rest.pop(0)
        dwk_ref, dwv_ref, dkvaw_ref, dknw_ref, heads_ref, dhb_ref, dkr_ref = rest
        accs = [dwk_ref, dwv_ref, dkvaw_ref, dknw_ref]

        @pl.when(pl.program_id(0) == 0)
        def _():
            for k, acc in enumerate(accs):
                acc[...] = init_refs[k][...] if n_init else jnp.zeros_like(acc)
            if with_q:
                dwq_ref[...] = jnp.zeros_like(dwq_ref)
                dqaw_ref[...] = jnp.zeros_like(dqaw_ref)
                dqnw_ref[...] = jnp.zeros_like(dqnw_ref)

        cos, sin_a, sin_b = cos_ref[...][None], sa_ref[...][None], sb_ref[...][None]
        lane = lax.broadcasted_iota(jnp.int32, (tm, HEAD_PAD), 1)
        rope_lanes = (lane >= QK_NOPE) & (lane < QK_HEAD)

        def heads_bwd(w_ref, src, extra, nw_ref, d_ref, dnw_ref, dw_ref, post):
            w_t = extra is None
            for h in range(HEADS):
                heads_ref[h] = _dot_nt(src, w_ref[h]) if w_t else _dot(src, w_ref[h])
            xp = heads_ref[...] if extra is None else heads_ref[...] + extra[None]
            r = lax.rsqrt(jnp.sum(xp * xp, axis=-1, keepdims=True) * (1.0 / QK_HEAD) + EPS)
            xh = xp * r
            dn = _rope3_t(d_ref[...].astype(F32), cos, sin_a, sin_b)
            dnw_ref[...] += post * jnp.sum(jnp.sum(dn * xh, axis=0), axis=0, keepdims=True)
            dxh = dn * (nw_ref[...] * post)[None]
            dxp = r * (dxh - xh * (jnp.sum(dxh * xh, axis=-1, keepdims=True) * (1.0 / QK_HEAD)))
            dhb_ref[...] = dxp.astype(BF16)
            dsrc = jnp.zeros((tm, src.shape[1]), F32)
            for h in range(HEADS):
                dsrc = dsrc + (_dot(dhb_ref[h], w_ref[h]) if w_t else _dot_nt(dhb_ref[h], w_ref[h]))
                dw_ref[h] += _dot_tn(src, dhb_ref[h])
            return dsrc, jnp.sum(dxp, axis=0)

        ckv = p_ref[:, 0:128].astype(F32)
        rkv = lax.rsqrt(jnp.mean(ckv * ckv, axis=-1, keepdims=True) + EPS)
        ckvh = ckv * rkv
        ckvb = (ckvh * kvaw_ref[...]).astype(BF16)
        for h in range(HEADS):
            dkr_ref[h] = dk_ref[h].astype(F32).T
        dckv, dkp_sum = heads_bwd(wk_ref, ckvb, p_ref[:, 128:256].astype(F32), knw_ref, dkr_ref, dknw_ref, dwk_ref,
                                  1.0)
        for j in range(HEADS // 2):
            dvb = dv_ref[j].astype(F32).T.astype(BF16)
            dckv = dckv + _dot_nt(dvb, wv_ref[j])
            dwv_ref[j] += _dot_tn(ckvb, dvb)
        dkvaw_ref[...] += jnp.sum(dckv * ckvh, axis=0, keepdims=True)
        dch = dckv * kvaw_ref[...]
        dp_ref[:, 0:128] = (rkv * (dch - ckvh * jnp.mean(dch * ckvh, axis=-1, keepdims=True))).astype(BF16)
        dp_ref[:, 128:256] = jnp.where(rope_lanes, dkp_sum, 0.0).astype(BF16)
        if with_q:
            cq = p_ref[:, 256:512].astype(F32)
            rq = lax.rsqrt(jnp.mean(cq * cq, axis=-1, keepdims=True) + EPS)
            cqh = cq * rq
            cqb = (cqh * qaw_ref[...]).astype(BF16)
            dcq, _ = heads_bwd(wq_ref, cqb, None, qnw_ref, dq_ref, dqnw_ref, dwq_ref, SOFTMAX_SCALE)
            dqaw_ref[...] += jnp.sum(dcq * cqh, axis=0, keepdims=True)
            dqc = dcq * qaw_ref[...]
            dp_ref[:, 256:512] = (rq * (dqc - cqh * jnp.mean(dqc * cqh, axis=-1, keepdims=True))).astype(BF16)
        else:
            dp_ref[:, 256:512] = jnp.zeros((tm, Q_LORA), BF16)

    tab = pl.BlockSpec((tm, HEAD_PAD), lambda i: (pos0 + i % nblk, 0))
    qspec = pl.BlockSpec((None, HEADS, tm, HEAD_PAD), lambda i: (i // nblk, 0, i % nblk, 0))
    kspec = pl.BlockSpec((None, HEADS, HEAD_PAD, tm), lambda i: (i // nblk, 0, 0, key0 + i % nblk))
    vspec = pl.BlockSpec((None, HEADS // 2, HEAD_PAD, tm), lambda i: (i // nblk, 0, 0, key0 + i % nblk))

    def acc_spec(shape):
        nd = len(shape)
        return pl.BlockSpec(shape, lambda i: (0,) * nd)

    acc_shapes = [(HEADS, KV_LORA, HEAD_PAD), (HEADS // 2, KV_LORA, HEAD_PAD), (1, KV_LORA), (1, HEAD_PAD)]
    q_shapes = [(HEADS, Q_LORA, HEAD_PAD), (1, Q_LORA), (1, HEAD_PAD)] if with_q else []
    out_shapes = [(dp_rows, 512)] + q_shapes + acc_shapes
    n_before = 11 + (1 if with_q else 0) + 2 + n_init
    return pl.pallas_call(
        body, name=name, grid=(nb * nblk,),
        in_specs=[pl.BlockSpec((tm, 512), lambda i: (row0 + i, 0)), tab, tab, tab, _whole(wq.shape), _whole(wk.shape),
                  _whole(wv.shape), _whole(kvaw.shape), _whole(qaw.shape), _whole(qnw.shape), _whole(knw.shape)]
        + ([qspec] if with_q else []) + [kspec, vspec] + [_whole(a.shape) for a in (init or [])]
        + [pl.BlockSpec(memory_space=pl.ANY)] * n_into,
        out_specs=[pl.BlockSpec((tm, 512), lambda i: (row0 + i, 0))] + [acc_spec(sh) for sh in q_shapes + acc_shapes],
        out_shape=[_sds(out_shapes[0], BF16)] + [_sds(sh, F32) for sh in out_shapes[1:]],
        scratch_shapes=[pltpu.VMEM((HEADS, tm, HEAD_PAD), F32), pltpu.VMEM((HEADS, tm, HEAD_PAD), BF16),
                        pltpu.VMEM((HEADS, tm, HEAD_PAD), F32)],
        input_output_aliases={n_before: 0} if n_into else {}, compiler_params=_params(1),
    )(proj, *tabs, wq, wk, wv, kvaw, qaw, qnw, knw, *([dq] if with_q else []), dk, dv, *(init or []),
      *([dp_into] if n_into else []))


def _attn_fwd(q, k, v, tq, exch=None):
    nb, _, s, _ = q.shape
    sk = k.shape[2]
    nq = s // tq

    def body(q_ref, k_ref, v_ref, o_ref, lse_ref, vext_ref):
        @pl.when(pl.program_id(2) == 0)
        def _():
            vext_ref[:, 0:HEAD_PAD] = v_ref[...]
            vext_ref[:, HEAD_PAD:2 * HEAD_PAD] = jnp.ones((sk, HEAD_PAD), BF16)

        lane = lax.broadcasted_iota(jnp.int32, (tq, HEAD_PAD), 1)
        outs = []
        for hh in range(2):
            sc = _dot_nt(q_ref[hh], k_ref[hh])
            m = jnp.max(sc, axis=-1, keepdims=True)
            pv = _dot(jnp.exp2(sc - m).astype(BF16), vext_ref[...])
            l = pv[:, HEAD_PAD:HEAD_PAD + 1]
            outs.append(pv[:, 0:HEAD_PAD] / l)
            lse_ref[hh] = m + jnp.log2(l)
        o_ref[...] = jnp.where(lane < V_HEAD, outs[0], outs[1]).astype(BF16)

    (o, lse), got = _hosted_call(
        body, "attn_fwd", (nb, HEADS // 2, nq),
        [pl.BlockSpec((None, 2, tq, HEAD_PAD), lambda b, j, i: (b, j, i, 0)),
         pl.BlockSpec((None, 2, sk, HEAD_PAD), lambda b, j, i: (b, j, 0, 0)),
         pl.BlockSpec((None, None, sk, HEAD_PAD), lambda b, j, i: (b, j, 0, 0))],
        [pl.BlockSpec((tq, HEAD_PAD), lambda b, j, i: (b * nq + i, j)),
         pl.BlockSpec((None, 2, tq, 1), lambda b, j, i: (b, j, i, 0))],
        [_sds((nb * s, MLA_W + GMLP_W), BF16), _sds((nb, HEADS, s, 1), F32)], (q, k, v),
        scratch=[pltpu.VMEM((sk, 2 * HEAD_PAD), BF16)], exch=exch)
    return o, lse, got


def _attn_bwd(q, k, v, do, o, lse, tq, exch=None):
    nb, _, s, _ = q.shape
    sk = k.shape[2]
    nq = s // tq

    def body(q_ref, k_ref, v_ref, do_ref, o_ref, lse_ref, dq_ref, dk_out, dv_out, dkt_ref, dvt_ref):
        @pl.when(pl.program_id(2) == 0)
        def _():
            dkt_ref[...] = jnp.zeros_like(dkt_ref)
            dvt_ref[...] = jnp.zeros_like(dvt_ref)

        lane = lax.broadcasted_iota(jnp.int32, (tq, HEAD_PAD), 1)
        dov = do_ref[...]
        prod = dov.astype(F32) * o_ref[...].astype(F32)
        for hh in range(2):
            mine = (lane < V_HEAD) if hh == 0 else (lane >= V_HEAD)
            doh = jnp.where(mine, dov, jnp.zeros_like(dov))
            delta = jnp.sum(jnp.where(mine, prod, 0.0), axis=-1, keepdims=True)
            qh = q_ref[hh]
            q_ln2 = (qh.astype(F32) * LN2).astype(BF16)
            kv = k_ref[hh]
            p = jnp.exp2(_dot_nt(qh, kv) - lse_ref[hh])
            u = (p * (_dot_nt(doh, v_ref[...]) - delta)).astype(BF16)
            dq_ref[hh] = (_dot(u, kv) * LN2).astype(BF16)
            dkt_ref[hh] += _dot_tn(q_ln2, u)
            dvt_ref[...] += _dot_tn(doh, p.astype(BF16))

        @pl.when(pl.program_id(2) == nq - 1)
        def _():
            dk_out[...] = dkt_ref[...].astype(BF16)
            dv_out[...] = dvt_ref[...].astype(BF16)

    qspec = pl.BlockSpec((None, 2, tq, HEAD_PAD), lambda b, j, i: (b, j, i, 0))
    kspec = pl.BlockSpec((None, 2, sk, HEAD_PAD), lambda b, j, i: (b, j, 0, 0))
    vspec = pl.BlockSpec((None, None, sk, HEAD_PAD), lambda b, j, i: (b, j, 0, 0))
    ospec = pl.BlockSpec((tq, HEAD_PAD), lambda b, j, i: (b * nq + i, j))
    return _hosted_call(
        body, "attn_bwd", (nb, HEADS // 2, nq),
        [qspec, kspec, vspec, ospec, ospec, pl.BlockSpec((None, 2, tq, 1), lambda b, j, i: (b, j, i, 0))],
        [qspec, pl.BlockSpec((None, 2, HEAD_PAD, sk), lambda b, j, i: (b, j, 0, 0)),
         pl.BlockSpec((None, None, HEAD_PAD, sk), lambda b, j, i: (b, j, 0, 0))],
        [_sds(q.shape, BF16), _sds((nb, HEADS, HEAD_PAD, sk), BF16), _sds((nb, HEADS // 2, HEAD_PAD, sk), BF16)],
        (q, k, v, do, o, lse), scratch=[pltpu.VMEM((2, HEAD_PAD, sk), F32), pltpu.VMEM((HEAD_PAD, sk), F32)], exch=exch)


def _group_masks(rows):
    lane = lax.broadcasted_iota(jnp.int32, (rows, GMLP_W), 1)
    return [(lane >= g * GROUP_DIM) & (lane < (g + 1) * GROUP_DIM) for g in range(GROUPS)]


def _gmlp_fwd(proj, mixcat, wcat, bias, vnw, ones, tm):
    t = mixcat.shape[0]

    def body(u_ref, v_ref, wcat_ref, bias_ref, vnw_ref, ones_ref, _, o_ref):
        masks = _group_masks(CHUNK)
        gv = _gelu(v_ref[...].astype(F32))
        rv = lax.rsqrt(_group_sum(gv * gv, ones_ref) * (1.0 / GROUP_DIM) + EPS)
        vnb = (gv * rv * vnw_ref[...]).astype(BF16)
        for c in range(tm // CHUNK):
            rows = slice(c * CHUNK, (c + 1) * CHUNK)
            vc = vnb[rows]
            stack = jnp.concatenate([jnp.where(m, vc, jnp.zeros_like(vc)) for m in masks], axis=0)
            sp = _dot(wcat_ref[...], stack) + bias_ref[...]
            o_ref[rows, :] = (_gelu(u_ref[rows, :].astype(F32)) * sp).astype(BF16)

    return pl.pallas_call(
        body, name="gmlp_fwd", grid=(t // tm,),
        in_specs=[pl.BlockSpec((tm, GMLP_W), lambda i: (i, 1)), pl.BlockSpec((tm, GMLP_W), lambda i: (i, 2)),
                  _whole(wcat.shape), _whole(bias.shape), _whole(vnw.shape), _whole(ones.shape),
                  pl.BlockSpec(memory_space=pl.ANY)],
        out_specs=pl.BlockSpec((tm, GMLP_W), lambda i: (i, 1)),
        out_shape=_sds(mixcat.shape, BF16), input_output_aliases={6: 0}, compiler_params=_params(1),
    )(proj, proj, wcat, bias, vnw, ones, mixcat)


def _gmlp_bwd(proj, dsg, wcat, wcat_t, bias, vnw, ones, tm):
    t = dsg.shape[0]

    def body(u_ref, v_ref, dsg_ref, wcat_ref, wcatt_ref, bias_ref, vnw_ref, ones_ref,
             duv_ref, dws_ref, dbs_ref, dvnw_ref):
        @pl.when(pl.program_id(0) == 0)
        def _():
            dws_ref[...] = jnp.zeros_like(dws_ref)
            dbs_ref[...] = jnp.zeros_like(dbs_ref)
            dvnw_ref[...] = jnp.zeros_like(dvnw_ref)

        masks = _group_masks(CHUNK)
        v = v_ref[...].astype(F32)
        gv = _gelu(v)
        rv = lax.rsqrt(_group_sum(gv * gv, ones_ref) * (1.0 / GROUP_DIM) + EPS)
        xh = gv * rv
        vnb = (xh * vnw_ref[...]).astype(BF16)
        dvn_parts = []
        for c in range(tm // CHUNK):
            rows = slice(c * CHUNK, (c + 1) * CHUNK)
            vc = vnb[rows]
            stack = jnp.concatenate([jnp.where(m, vc, jnp.zeros_like(vc)) for m in masks], axis=0)
            sp = _dot(wcat_ref[...], stack) + bias_ref[...]
            u = u_ref[rows, :].astype(F32)
            dsg_c = dsg_ref[rows, :].astype(F32)
            duv_ref[rows, 0:GMLP_W] = (dsg_c * sp * _gelu_grad(u)).astype(BF16)
            ds = dsg_c * _gelu(u)
            dstack = jnp.concatenate([jnp.where(m, ds, 0.0) for m in masks], axis=0)
            dbs_ref[...] += jnp.broadcast_to(jnp.sum(dstack, axis=-1, keepdims=True), dbs_ref.shape)
            dstb = dstack.astype(BF16)
            dvn_parts.append(_dot(wcatt_ref[...], dstb))
            dws_ref[...] += _dot_nt(dstb, vc)
        dvn = jnp.concatenate(dvn_parts, axis=0) if len(dvn_parts) > 1 else dvn_parts[0]
        dvnw_ref[...] += jnp.sum(dvn * xh, axis=0, keepdims=True)
        dxh = dvn * vnw_ref[...]
        gm = _group_sum(dxh * xh, ones_ref) * (1.0 / GROUP_DIM)
        duv_ref[:, GMLP_W:2 * GMLP_W] = (rv * (dxh - xh * gm) * _gelu_grad(v)).astype(BF16)

    row = pl.BlockSpec((tm, GMLP_W), lambda i: (i, 0))
    return pl.pallas_call(
        body, name="gmlp_bwd", grid=(t // tm,),
        in_specs=[pl.BlockSpec((tm, GMLP_W), lambda i: (i, 1)), pl.BlockSpec((tm, GMLP_W), lambda i: (i, 2)), row,
                  _whole(wcat.shape), _whole(wcat_t.shape), _whole(bias.shape), _whole(vnw.shape), _whole(ones.shape)],
        out_specs=[pl.BlockSpec((tm, 2 * GMLP_W), lambda i: (i, 0)), pl.BlockSpec((GROUPS * CHUNK, CHUNK), lambda i: (0, 0)),
                   pl.BlockSpec((GROUPS * CHUNK, CHUNK), lambda i: (0, 0)), pl.BlockSpec((1, GMLP_W), lambda i: (0, 0))],
        out_shape=[_sds((t, 2 * GMLP_W), BF16), _sds((GROUPS * CHUNK, CHUNK), F32), _sds((GROUPS * CHUNK, CHUNK), F32),
                   _sds((1, GMLP_W), F32)],
        compiler_params=_params(1),
    )(proj, proj, dsg, wcat, wcat_t, bias, vnw, ones)


def _mixout_fwd(mixcat, xs, mod, wout, s, tm):
    t, width = mixcat.shape
    d = xs.shape[1]

    def body(cat_ref, x_ref, mod_ref, w_ref, x2_ref, mix_ref):
        g = (pl.program_id(0) * tm) // s
        gate = mod_ref[g, pl.ds(5, 1), :]
        mix = _dot(cat_ref[...], w_ref[...])
        x2_ref[...] = x_ref[...] + gate * mix
        mix_ref[...] = mix.astype(BF16)

    row = lambda i: (i, 0)
    return pl.pallas_call(
        body, name="mixout_fwd", grid=(t // tm,),
        in_specs=[pl.BlockSpec((tm, width), row), pl.BlockSpec((tm, d), row), _whole(mod.shape), _whole(wout.shape)],
        out_specs=[pl.BlockSpec((tm, d), row), pl.BlockSpec((tm, d), row)],
        out_shape=[_sds((t, d), F32), _sds((t, d), BF16)], compiler_params=_params(1),
    )(mixcat, xs, mod, wout)


def _mixout_bwd(dx2, mix, mod, wout, s, tm):
    t, d = dx2.shape

    def body(dx_ref, mix_ref, mod_ref, w_ref, dmix_ref, do_ref, dsg_ref, dmod_ref):
        i = pl.program_id(0)

        @pl.when(i == 0)
        def _():
            dmod_ref[...] = jnp.zeros_like(dmod_ref)

        g = (i * tm) // s
        gate = mod_ref[g, pl.ds(5, 1), :]
        dx = dx_ref[...]
        dmod_ref[g, pl.ds(5, 1), :] += jnp.sum(dx * mix_ref[...].astype(F32), axis=0, keepdims=True)
        dmb = (gate * dx).astype(BF16)
        dmix_ref[...] = dmb
        do_ref[...] = _dot_nt(dmb, w_ref[0:MLA_W, :]).astype(BF16)
        dsg_ref[...] = _dot_nt(dmb, w_ref[MLA_W:MLA_W + GMLP_W, :]).astype(BF16)

    row = lambda i: (i, 0)
    return pl.pallas_call(
        body, name="mixout_bwd", grid=(t // tm,),
        in_specs=[pl.BlockSpec((tm, d), row), pl.BlockSpec((tm, d), row), _whole(mod.shape), _whole(wout.shape)],
        out_specs=[pl.BlockSpec((tm, d), row), pl.BlockSpec((tm, MLA_W), row), pl.BlockSpec((tm, GMLP_W), row),
                   pl.BlockSpec(mod.shape, lambda i: (0, 0, 0))],
        out_shape=[_sds((t, d), BF16), _sds((t, MLA_W), BF16), _sds((t, GMLP_W), BF16), _sds(mod.shape, F32)],
        compiler_params=_params(1),
    )(dx2, mix, mod, wout)


def _swap_cores(parts, name):
    n = len(parts)

    def body(*refs):
        srcs, outs, send_sems, recv_sems = refs[:n], refs[n:2 * n], refs[2 * n], refs[2 * n + 1]
        x, y, c = lax.axis_index("x"), lax.axis_index("y"), lax.axis_index("c")
        copies = [pltpu.make_async_remote_copy(
            src_ref=srcs[w], dst_ref=outs[w], send_sem=send_sems.at[w], recv_sem=recv_sems.at[w],
            device_id=(x, y, 1 - c), device_id_type=pl.DeviceIdType.MESH) for w in range(n)]
        for cp in copies:
            cp.start()
        for cp in copies:
            cp.wait()

    any_spec = pl.BlockSpec(memory_space=pl.ANY)
    return pl.pallas_call(
        body, name=name, in_specs=[any_spec] * n, out_specs=[any_spec] * n,
        out_shape=[_sds(p.shape, p.dtype) for p in parts],
        scratch_shapes=[pltpu.SemaphoreType.DMA((n,)), pltpu.SemaphoreType.DMA((n,))],
    )(*parts)


def _row_tile(r, c, mult):
    return _div_tile(r, max(mult, (1 << 18) // c), mult)


def _sum_slots(recv, name):
    _, r, c = recv.shape
    tr = _row_tile(r, c, 16)

    def body(r_ref, o_ref):
        f = lambda k: r_ref[k].astype(F32)
        o_ref[...] = ((f(0) + f(1)) + f(2)) + f(3)

    return pl.pallas_call(
        body, name=name, grid=(r // tr,),
        in_specs=[pl.BlockSpec((N_CHIPS, tr, c), lambda i: (0, i, 0))],
        out_specs=pl.BlockSpec((tr, c), lambda i: (i, 0)),
        out_shape=_sds((r, c), F32), compiler_params=_params(1),
    )(recv)


def _adamw(parts, w, m, v, name, exch=None, swap=None):
    r, wd = w.shape
    tr = _row_tile(r, wd, 8)
    c1 = 1.0 / (1.0 - ADAM_B1 ** ADAM_STEP)
    c2 = 1.0 / (1.0 - ADAM_B2 ** ADAM_STEP)
    n_p = len(parts)

    def body(*refs):
        p_refs = refs[:n_p]
        w_ref, m_ref, v_ref, g_ref, d_ref, nm_ref, nv_ref = refs[n_p:]
        g = p_refs[0][...]
        for p_ref in p_refs[1:]:
            g = g + p_ref[...]
        nm = ADAM_B1 * m_ref[...] + (1.0 - ADAM_B1) * g
        nv = ADAM_B2 * v_ref[...] + (1.0 - ADAM_B2) * (g * g)
        g_ref[...] = g
        nm_ref[...] = nm
        nv_ref[...] = nv
        d_ref[...] = -ADAM_LR * ((nm * c1) / (jnp.sqrt(nv * c2) + ADAM_EPS) + ADAM_WD * w_ref[...])

    spec = pl.BlockSpec((tr, wd), lambda i: (i, 0))
    return _hosted_call(body, name, (r // tr,), [spec] * (n_p + 3), [spec] * 4, [_sds((r, wd), F32)] * 4,
                        (*parts, w, m, v), exch=exch, swap=swap)


def _all_peers(x, y, c):
    flips = [(dx, dy, dc) for dx in (0, 1) for dy in (0, 1) for dc in (0, 1)][1:]
    return [(1 - x if dx else x, 1 - y if dy else y, 1 - c if dc else c) for dx, dy, dc in flips]


def _first_exchange(shards, later, cc, w, b):
    n_w, n_l = len(shards), len(later)
    n = w.shape[1]

    def body(*refs):
        src32, later_in, (cc_ref, w_ref, b_ref) = refs[:n_w], refs[n_w:n_w + n_l], refs[n_w + n_l:n_w + n_l + 3]
        o0 = n_w + n_l + 3
        outs, (all_ref, tab_ref), later_out = refs[o0:o0 + n_w], refs[o0 + n_w:o0 + n_w + 2], refs[o0 + n_w + 2:o0 + n_w + 2 + n_l]
        s0 = o0 + n_w + 2 + n_l
        srcs = refs[s0:s0 + n_w]
        (part_ref, ici_send, ici_recv, d2d_send, d2d_recv, local_sems, cc_send, cc_recv, tab_send,
         tab_recv) = refs[s0 + n_w:]
        for wi in range(n_w):
            srcs[wi][...] = src32[wi][...].astype(BF16)
        x, y, c = lax.axis_index("x"), lax.axis_index("y"), lax.axis_index("c")
        chip, dev = 2 * x + y, 4 * x + 2 * y + c
        chips = _other_chips(x, y)
        peers = _all_peers(x, y, c)

        def half(wi, which):
            hr = shards[wi].shape[0] // 2
            return pl.ds(pl.multiple_of(which * hr, 16), hr)

        def over_ici(wi, k, arriving):
            px, py = chips[k]
            slot = 2 * px + py if arriving else chip
            return pltpu.make_async_remote_copy(
                src_ref=srcs[wi].at[half(wi, c)], dst_ref=outs[wi].at[slot, half(wi, c)],
                send_sem=ici_send.at[3 * wi + k], recv_sem=ici_recv.at[3 * wi + k], device_id=(px, py, c),
                device_id_type=pl.DeviceIdType.MESH)

        def to_sibling(wi, k, arriving):
            px, py = chips[k]
            rows = half(wi, 1 - c if arriving else c)
            return pltpu.make_async_remote_copy(
                src_ref=outs[wi].at[2 * px + py, rows], dst_ref=outs[wi].at[2 * px + py, rows],
                send_sem=d2d_send.at[3 * wi + k], recv_sem=d2d_recv.at[3 * wi + k], device_id=(x, y, 1 - c),
                device_id_type=pl.DeviceIdType.MESH)

        def cc_copy(k, peer, slot):
            return pltpu.make_async_remote_copy(
                src_ref=cc_ref, dst_ref=all_ref.at[slot], send_sem=cc_send.at[k], recv_sem=cc_recv.at[k],
                device_id=peer, device_id_type=pl.DeviceIdType.MESH)

        def rows_of(px, py):
            return part_ref.at[pl.ds(pl.multiple_of((4 * px + 2 * py + c) * MOD_ROWS, MOD_ROWS), MOD_ROWS)]

        def tab_copy(k, px, py, slot):
            return pltpu.make_async_remote_copy(
                src_ref=rows_of(px, py), dst_ref=tab_ref.at[slot], send_sem=tab_send.at[k], recv_sem=tab_recv.at[k],
                device_id=(px, py, c), device_id_type=pl.DeviceIdType.MESH)

        local = [pltpu.make_async_copy(srcs[wi], outs[wi].at[chip], local_sems.at[wi]) for wi in range(n_w)]
        for cp in local:
            cp.start()
        pairs = [(wi, k) for wi in range(n_w) for k in range(3)]
        for wi, k in pairs:
            over_ici(wi, k, False).start()
        for k, peer in enumerate(peers):
            cc_copy(k, peer, dev).start()
        all_ref[dev] = cc_ref[...]
        for k, (px, py, pc) in enumerate(peers):
            cc_copy(k, (px, py, pc), 4 * px + 2 * py + pc).wait_recv()
        cv = all_ref[...].reshape(8 * MOD_ROWS, cc.shape[1])
        part_ref[...] = _dot((cv * _sigmoid(cv)).astype(BF16), w_ref[...]) + b_ref[...]
        for k, (px, py) in enumerate(chips):
            tab_copy(k, px, py, chip).start()
        tab_ref[chip] = rows_of(x, y)[...]
        for k, (px, py) in enumerate(chips):
            tab_copy(k, px, py, 2 * px + py).wait_recv()
        for j in range(n_l):
            later_out[j][...] = later_in[j][...].astype(BF16)
        for wi, k in pairs:
            over_ici(wi, k, True).wait_recv()
            to_sibling(wi, k, False).start()
        for wi, k in pairs:
            to_sibling(wi, k, True).wait_recv()
        for wi, k in pairs:
            over_ici(wi, k, False).wait_send()
            to_sibling(wi, k, False).wait_send()
        for k, peer in enumerate(peers):
            cc_copy(k, peer, dev).wait_send()
        for k, (px, py) in enumerate(chips):
            tab_copy(k, px, py, chip).wait_send()
        for cp in local:
            cp.wait()

    any_spec = pl.BlockSpec(memory_space=pl.ANY)
    vmem = pl.BlockSpec(memory_space=pltpu.VMEM)
    sems3 = pltpu.SemaphoreType.DMA((3 * n_w,))
    got = pl.pallas_call(
        body, name="first_exchange", in_specs=[vmem] * (n_w + n_l + 3),
        out_specs=[any_spec] * n_w + [vmem] * (2 + n_l),
        out_shape=[_sds((N_CHIPS,) + a.shape, BF16) for a in shards]
        + [_sds((8,) + cc.shape, F32), _sds((N_CHIPS, MOD_ROWS, n), F32)] + [_sds(a.shape, BF16) for a in later],
        scratch_shapes=[pltpu.VMEM(a.shape, BF16) for a in shards]
        + [pltpu.VMEM((8 * MOD_ROWS, n), F32), sems3, sems3, sems3, sems3, pltpu.SemaphoreType.DMA((n_w,)),
           pltpu.SemaphoreType.DMA((7,)), pltpu.SemaphoreType.DMA((7,)), pltpu.SemaphoreType.DMA((3,)),
           pltpu.SemaphoreType.DMA((3,))],
        compiler_params=pltpu.CompilerParams(vmem_limit_bytes=V7X_VMEM_LIMIT),
    )(*shards, *later, cc, w, b)
    return got[:n_w], got[n_w], got[n_w + 1], got[n_w + 2:]


def _ada_bwd_tp(cc_all, dmods, w, ctx_row):
    d, n = w.shape

    def body(cc_ref, m0, m1, m2, m3, w_ref, dw_ref, db_ref, dctx_ref, stage_ref, all_ref, send_sems, recv_sems):
        x, y, c = lax.axis_index("x"), lax.axis_index("y"), lax.axis_index("c")
        me = 4 * x + 2 * y + c
        dsum = m0[...] + m1[...] + m2[...] + m3[...]
        db_ref[...] = jnp.sum(dsum, axis=0, keepdims=True)
        for j in range(N_CHIPS):
            stage_ref[j] = dsum[:, j * n:(j + 1) * n]

        def copy(k, peer, slot):
            px, py, _ = peer
            return pltpu.make_async_remote_copy(
                src_ref=stage_ref.at[2 * px + py], dst_ref=all_ref.at[slot], send_sem=send_sems.at[k],
                recv_sem=recv_sems.at[k], device_id=peer, device_id_type=pl.DeviceIdType.MESH)

        peers = _all_peers(x, y, c)
        for k, peer in enumerate(peers):
            copy(k, peer, me).start()
        all_ref[me] = stage_ref[2 * x + y]
        for k, (px, py, pc) in enumerate(peers):
            copy(k, (px, py, pc), 4 * px + 2 * py + pc).wait_recv()
        for k, peer in enumerate(peers):
            copy(k, peer, me).wait_send()
        cv = cc_ref[...]
        sig = _sigmoid(cv)
        dmb = all_ref[...].reshape(8 * MOD_ROWS, n).astype(BF16)
        dw_ref[...] = _dot_tn((cv * sig).astype(BF16), dmb)
        dsc = _dot_nt(dmb, w_ref[...])
        dctx = dsc[ctx_row:ctx_row + 1, :]
        for dev in range(1, 8):
            dctx = dctx + dsc[dev * MOD_ROWS + ctx_row:dev * MOD_ROWS + ctx_row + 1, :]
        cx = cv[ctx_row:ctx_row + 1, :]
        sx = sig[ctx_row:ctx_row + 1, :]
        dctx_ref[...] = dctx * (sx * (1.0 + cx * (1.0 - sx))) * jnp.where(c == 0, 1.0, 0.0)

    vmem = pl.BlockSpec(memory_space=pltpu.VMEM)
    return pl.pallas_call(
        body, name="ada_bwd_tp", in_specs=[vmem] * 6, out_specs=[vmem] * 3,
        out_shape=[_sds((d, n), F32), _sds((1, N_MOD * d), F32), _sds((1, d), F32)],
        scratch_shapes=[pltpu.VMEM((N_CHIPS, MOD_ROWS, n), F32), pltpu.VMEM((8, MOD_ROWS, n), F32),
                        pltpu.SemaphoreType.DMA((7,)), pltpu.SemaphoreType.DMA((7,))],
        compiler_params=pltpu.CompilerParams(vmem_limit_bytes=V7X_VMEM_LIMIT),
    )(cc_all, *dmods, w)


def _rope_tables(s, ctx):
    pos = np.arange(s, dtype=np.float32)
    inv = (np.float32(ROPE_BASE) ** (-np.arange(0, QK_ROPE // 2, 2, dtype=np.float32) / np.float32(QK_ROPE // 2)))
    ang_r = np.floor(pos / GRID_W)[:, None] * inv
    ang_c = (pos - GRID_W * np.floor(pos / GRID_W))[:, None] * inv
    ang = np.concatenate([ang_r, ang_r, ang_c, ang_c], axis=-1).astype(np.float32)
    cos, sin = np.cos(ang), np.sin(ang)
    half_b = (np.arange(QK_ROPE) // 8) % 2 == 1
    sin_a = np.where(half_b, sin, 0.0)
    sin_b = np.where(half_b, 0.0, -sin)

    def place(tab, fill):
        full = np.full((s + ctx, HEAD_PAD), fill, np.float32)
        full[:s, QK_NOPE:QK_HEAD] = tab
        return jnp.asarray(full)

    return place(cos, 1.0), place(sin_a, 0.0), place(sin_b, 0.0)


def _pad_last(a, n):
    return jnp.pad(a, [(0, 0)] * (a.ndim - 1) + [(0, n - a.shape[-1])])


def _flat_rows(parts, rows, width):
    flat = jnp.concatenate([p.reshape(-1) for p in parts])
    return jnp.pad(flat, (0, rows * width - flat.shape[0])).reshape(rows, width)


def kernel(x, c, ctx, c_ctx, w_ada, b_ada, norm1_w, ffn1_w1, ffn1_w3, ffn1_w2, norm2_w, w_in, q_a_norm_w, w_uq, kv_a_norm_w, w_ukv, q_norm_w, k_norm_w, v_norm_w, w_s, b_s, w_out, norm3_w, ffn2_w1, ffn2_w3, ffn2_w2, loss_target, m_c_ctx, m_w_ada, m_b_ada, m_norm1_w, m_ffn1_w1, m_ffn1_w3, m_ffn1_w2, m_norm2_w, m_w_in, m_q_a_norm_w, m_w_uq, m_kv_a_norm_w, m_w_ukv, m_q_norm_w, m_k_norm_w, m_v_norm_w, m_w_s, m_b_s, m_w_out, m_norm3_w, m_ffn2_w1, m_ffn2_w3, m_ffn2_w2, v_c_ctx, v_w_ada, v_b_ada, v_norm1_w, v_ffn1_w1, v_ffn1_w3, v_ffn1_w2, v_norm2_w, v_w_in, v_q_a_norm_w, v_w_uq, v_kv_a_norm_w, v_w_ukv, v_q_norm_w, v_k_norm_w, v_v_norm_w, v_w_s, v_b_s, v_w_out, v_norm3_w, v_ffn2_w1, v_ffn2_w3, v_ffn2_w2):
    wts = dict(c_ctx=c_ctx, w_ada=w_ada, b_ada=b_ada, norm1_w=norm1_w, ffn1_w1=ffn1_w1, ffn1_w3=ffn1_w3, ffn1_w2=ffn1_w2,
               norm2_w=norm2_w, w_in=w_in, q_a_norm_w=q_a_norm_w, w_uq=w_uq, kv_a_norm_w=kv_a_norm_w, w_ukv=w_ukv,
               q_norm_w=q_norm_w, k_norm_w=k_norm_w, v_norm_w=v_norm_w, w_s=w_s, b_s=b_s, w_out=w_out, norm3_w=norm3_w,
               ffn2_w1=ffn2_w1, ffn2_w3=ffn2_w3, ffn2_w2=ffn2_w2)
    moms = dict(c_ctx=m_c_ctx, w_ada=m_w_ada, b_ada=m_b_ada, norm1_w=m_norm1_w, ffn1_w1=m_ffn1_w1, ffn1_w3=m_ffn1_w3,
                ffn1_w2=m_ffn1_w2, norm2_w=m_norm2_w, w_in=m_w_in, q_a_norm_w=m_q_a_norm_w, w_uq=m_w_uq,
                kv_a_norm_w=m_kv_a_norm_w, w_ukv=m_w_ukv, q_norm_w=m_q_norm_w, k_norm_w=m_k_norm_w, v_norm_w=m_v_norm_w,
                w_s=m_w_s, b_s=m_b_s, w_out=m_w_out, norm3_w=m_norm3_w, ffn2_w1=m_ffn2_w1, ffn2_w3=m_ffn2_w3,
                ffn2_w2=m_ffn2_w2)
    vars_ = dict(c_ctx=v_c_ctx, w_ada=v_w_ada, b_ada=v_b_ada, norm1_w=v_norm1_w, ffn1_w1=v_ffn1_w1, ffn1_w3=v_ffn1_w3,
                 ffn1_w2=v_ffn1_w2, norm2_w=v_norm2_w, w_in=v_w_in, q_a_norm_w=v_q_a_norm_w, w_uq=v_w_uq,
                 kv_a_norm_w=v_kv_a_norm_w, w_ukv=v_w_ukv, q_norm_w=v_q_norm_w, k_norm_w=v_k_norm_w, v_norm_w=v_v_norm_w,
                 w_s=v_w_s, b_s=v_b_s, w_out=v_w_out, norm3_w=v_norm3_w, ffn2_w1=v_ffn2_w1, ffn2_w3=v_ffn2_w3,
                 ffn2_w2=v_ffn2_w2)

    nb, s, d = x.shape
    nctx = ctx.shape[1]
    t, tc = nb * s, nb * nctx
    t_all = t + tc
    sk = s + nctx
    assert nb + 1 <= MOD_ROWS and d % LANES == 0
    tm = _token_tile(s, nctx)
    tq = _div_tile(s, 512, tm)
    tmx = _div_tile(math.gcd(s, tc), 1024, tm)
    tmo = _div_tile(s, 1024, tm)

    def held(n, a_):
        return jnp.swapaxes(a_[0], 0, 1) if n in T_WEIGHTS else a_[0]

    def unheld(n, a_):
        return (jnp.swapaxes(a_, 0, 1) if n in T_WEIGHTS else a_)[None]

    shard = {"w_ada": w_ada[0].astype(BF16)}
    full = {}

    def unshard(names, blocks):
        for n, g4 in zip(names, blocks):
            _, r_, c_ = g4.shape
            if n in ROW_SHARDED or n in T_WEIGHTS:
                full[n] = g4.reshape(N_CHIPS * r_, c_)
            else:
                full[n] = g4.transpose(1, 0, 2).reshape(r_, N_CHIPS * c_)

    def chip_major(n, g_):
        if n in ROW_SHARDED or n in T_WEIGHTS:
            return g_.reshape(N_CHIPS, g_.shape[0] // N_CHIPS, g_.shape[1]).astype(BF16)
        r_, cols = g_.shape
        return g_.reshape(r_, N_CHIPS, cols // N_CHIPS).transpose(1, 0, 2).astype(BF16)

    cc = jnp.concatenate([c, c_ctx[None, :], jnp.zeros((MOD_ROWS - nb - 1, d), F32)], axis=0)
    n_ada = shard["w_ada"].shape[1]
    assert n_ada % LANES == 0
    my_chip = 2 * lax.axis_index("x") + lax.axis_index("y")
    b_cols = lax.dynamic_slice_in_dim(b_ada, my_chip * n_ada, n_ada, axis=1)
    later = MIX_WEIGHTS + LAST_WEIGHTS
    got, cc_all, table, cast = _first_exchange([held(n, wts[n]) for n in FIRST_WEIGHTS],
                                               [held(n, wts[n]) for n in later], cc, shard["w_ada"], b_cols)
    unshard(FIRST_WEIGHTS, got)
    shard.update(zip(later, cast))
    cc_all = cc_all.reshape(8 * MOD_ROWS, d)
    mod = table.transpose(1, 0, 2).reshape(MOD_ROWS, N_MOD, d)
    wsb = w_s[0].astype(BF16)
    wcat = wsb.transpose(1, 0, 2).reshape(CHUNK, GROUPS * CHUNK)
    wcat_t = wsb.transpose(2, 0, 1).reshape(CHUNK, GROUPS * CHUNK)
    bias = jnp.repeat(b_s[0].T, GROUP_DIM, axis=1)
    vnw = v_norm_w.reshape(1, GMLP_W)
    lane = jnp.arange(GMLP_W)
    ones = (lane[:, None] // GROUP_DIM == lane[None, :] // GROUP_DIM).astype(BF16)
    qnw = _pad_last(q_norm_w, HEAD_PAD)
    knw = _pad_last(k_norm_w, HEAD_PAD)
    tabs = _rope_tables(s, nctx)

    x_lat, x_ctx = x.reshape(t, d), ctx.reshape(tc, d)
    (xs1, a1, b1, y1), got = _ffn_fwd(x_lat, x_ctx, mod, norm1_w, full["ffn1_w1"], full["ffn1_w3"], full["ffn1_w2"], 0, s,
                                      nb, tm, "ffn1_fwd", exch=("gather", [shard[n] for n in MIX_WEIGHTS]))
    unshard(MIX_WEIGHTS, got)
    wi = full["w_in"]
    wp = jnp.concatenate([wi[0:KV_LORA], jnp.zeros((QK_NOPE, d), BF16), wi[KV_LORA:KV_LORA + QK_ROPE],
                          jnp.zeros((HEAD_PAD - QK_HEAD, d), BF16), wi[KV_LORA + QK_ROPE:]], axis=0)
    wq = jnp.pad(full["w_uq"].reshape(HEADS, QK_HEAD, Q_LORA), ((0, 0), (0, HEAD_PAD - QK_HEAD), (0, 0)))
    wkv = full["w_ukv"].reshape(KV_LORA, HEADS, QK_NOPE + V_HEAD)
    wk = _pad_last(wkv[:, :, :QK_NOPE].transpose(1, 0, 2), HEAD_PAD)
    wv = wkv[:, :, QK_NOPE:].reshape(KV_LORA, HEADS // 2, 2 * V_HEAD).transpose(1, 0, 2)
    h2, proj = _mixin_fwd(xs1, mod, norm2_w, wp, s, nb, tmx)
    prep_w = (wq, wk, wv, kv_a_norm_w, q_a_norm_w, qnw, knw)
    q, k_all, v_all = _prep_fwd(proj, 0, nb, s, 0, sk, 0, None, tabs, *prep_w, tmo, True, "prep_fwd")
    k_all, v_all = _prep_fwd(proj, t // tm, nb, nctx, s // tm, sk, s // tm, (k_all, v_all), tabs, *prep_w, tm, False,
                             "prep_ctx_fwd")
    o, lse, got = _attn_fwd(q, k_all, v_all, tmo, exch=("gather", [shard[n] for n in LAST_WEIGHTS]))
    unshard(LAST_WEIGHTS, got)
    mixcat = _gmlp_fwd(proj, o, wcat, bias, vnw, ones, tq)
    x2, mix = _mixout_fwd(mixcat, xs1, mod, full["w_out"], s, tmo)
    (dy, a2, b2, y2, loss_part), _ = _ffn_fwd(x2, None, mod, norm3_w, full["ffn2_w1"], full["ffn2_w3"], full["ffn2_w2"], 6,
                                              s, nb, tm, "ffn2_fwd", target=loss_target.reshape(t, d))

    grads, cm, recv = {}, {}, {}

    def scatter_of(names):
        return ("scatter", [cm[n] for n in names])

    (dx2, h3, g2, da2, db2, dyb2, dmod_c, grads["norm3_w"]), _ = _ffn_bwd(
        dy, x2, None, a2, b2, y2, mod, norm3_w, full["ffn2_w1"], full["ffn2_w3"], full["ffn2_w2"], 6, s, nb, tm,
        "ffn2_bwd")
    cm["ffn2_w1"] = chip_major("ffn2_w1", _mm_tn(da2, h3, t, "ffn2_dw1"))
    cm["ffn2_w3"] = chip_major("ffn2_w3", _mm_tn(db2, h3, t, "ffn2_dw3"))
    cm["ffn2_w2"] = chip_major("ffn2_w2", _mm_tn(g2, dyb2, t, "ffn2_dw2"))
    dmix, do, dsg, dmod_b = _mixout_bwd(dx2, mix, mod, full["w_out"], s, tmo)
    cm["w_out"] = chip_major("w_out", _mm_tn(mixcat, dmix, t, "wout_dw"))
    duv, dws, dbs, dvnw = _gmlp_bwd(proj, dsg, wcat, wcat_t, bias, vnw, ones, tq)
    group = LAST_WEIGHTS + ("w_out",)
    (dq, dk, dv), got = _attn_bwd(q, k_all, v_all, do, mixcat, lse, tmo, exch=scatter_of(group))
    recv.update(zip(group, got))
    dp0, dwk_c, dwv_c, dkvaw_c, dknw_c = _prep_bwd(
        proj, t // tm, nb, nctx, s // tm, s // tm, t_all, None, tabs, *prep_w, None, dk, dv, None, tm, "prep_ctx_bwd")
    dp0, dwq, dqaw, dqnw, dwk, dwv, dkvaw, dknw = _prep_bwd(
        proj, 0, nb, s, 0, 0, t_all, dp0, tabs, *prep_w, dq, dk, dv, [dwk_c, dwv_c, dkvaw_c, dknw_c], tq, "prep_bwd")
    part, sib = {}, {}
    early = LAST_WEIGHTS + ("w_out",)
    for n in early:
        part[n] = _sum_slots(recv[n], "sum_" + n)
    (dxs1, dmod_a, grads["norm2_w"]), _, got = _mixin_bwd(dp0, duv, xs1, dx2, mod, norm2_w, wp, s, nb, tmx,
                                                          [part[n] for n in early])
    sib.update(zip(early, got))
    dwp = jnp.concatenate([_mm_tn(dp0, h2, t_all, "win_dw_kvq"), _mm_tn(duv, h2, t, "win_dw_uv")], axis=0)
    cm["w_in"] = chip_major("w_in", jnp.concatenate(
        [dwp[0:KV_LORA], dwp[KV_LORA + QK_NOPE:KV_LORA + QK_HEAD], dwp[256:]], axis=0))
    cm["w_uq"] = chip_major("w_uq", dwq[:, :, :QK_HEAD].transpose(0, 2, 1).reshape(HEADS * QK_HEAD, Q_LORA))
    cm["w_ukv"] = chip_major("w_ukv", jnp.concatenate(
        [dwk[:, :, :QK_NOPE].transpose(1, 0, 2),
         dwv.transpose(1, 0, 2).reshape(KV_LORA, HEADS, V_HEAD)], axis=2).reshape(KV_LORA, HEADS * (QK_NOPE + V_HEAD)))
    (dx_lat, h1, g1, da1, db1, dyb1, dmod_0, grads["norm1_w"]), _ = _ffn_bwd(
        dxs1, x_lat, x_ctx, a1, b1, y1, mod, norm1_w, full["ffn1_w1"], full["ffn1_w3"], full["ffn1_w2"], 0, s, nb, tm,
        "ffn1_bwd")
    dmods = [m_.reshape(MOD_ROWS, N_MOD * d) for m_ in (dmod_0, dmod_a, dmod_b, dmod_c)]
    dw_ada, grads["b_ada"], dctx = _ada_bwd_tp(cc_all, dmods, shard["w_ada"], nb)
    grads["c_ctx"] = dctx[0]
    grads["q_a_norm_w"], grads["kv_a_norm_w"] = dqaw, dkvaw
    grads["q_norm_w"], grads["k_norm_w"] = dqnw[:, :QK_HEAD], dknw[:, :QK_HEAD]
    grads["v_norm_w"], grads["w_s"], grads["b_s"] = dvnw, dws, dbs[:, 0]
    grad_x = dx_lat.reshape(nb, s, d)
    n_small = sum(wts[n].size for n in SMALL)
    rows_s = _round_up(-(-(n_small + 1) // d), 16)
    cm["small"] = jnp.broadcast_to(_flat_rows([grads[n] for n in SMALL] + [loss_part], rows_s, d), (N_CHIPS, rows_s, d))
    group = ("w_in", "w_uq", "w_ukv", "small")
    dw2, got = _mm_tn(g1, dyb1, t_all, "ffn1_dw2", exch=scatter_of(group))
    recv.update(zip(group, got))
    cm["ffn1_w2"] = chip_major("ffn1_w2", dw2)
    dw1, got = _mm_tn(da1, h1, t_all, "ffn1_dw1", exch=scatter_of(("ffn1_w2",)))
    recv["ffn1_w2"] = got[0]
    cm["ffn1_w1"] = chip_major("ffn1_w1", dw1)
    dw3, got = _mm_tn(db1, h1, t_all, "ffn1_dw3", exch=scatter_of(("ffn1_w1",)))
    recv["ffn1_w1"] = got[0]
    cm["ffn1_w3"] = chip_major("ffn1_w3", dw3)
    stepped = {}
    reduced = tuple(n for n in SHARDED if n != "w_ada") + ("small",)
    late = tuple(n for n in reduced if n not in early and n != "ffn1_w3")
    for n in late:
        part[n] = _sum_slots(recv[n], "sum_" + n)
    stepped["w_ada"], got, got_sib = _adamw([dw_ada], wts["w_ada"][0], moms["w_ada"][0], vars_["w_ada"][0],
                                            "adamw_w_ada", exch=scatter_of(("ffn1_w3",)), swap=[part[n] for n in late])
    sib.update(zip(late, got_sib))
    part["ffn1_w3"] = _sum_slots(got[0], "sum_ffn1_w3")
    sib["ffn1_w3"] = _swap_cores([part["ffn1_w3"]], "swap_last")[0]
    for n in reduced[:-1]:
        stepped[n], _ = _adamw([part[n], sib[n]], held(n, wts[n]), held(n, moms[n]), held(n, vars_[n]), "adamw_" + n)
    for n in SHARDED:
        stepped[n] = [unheld(n, a_) for a_ in stepped[n]]
    packed, _ = _adamw([part["small"], sib["small"]], _flat_rows([wts[n] for n in SMALL], rows_s, d),
                       _flat_rows([moms[n] for n in SMALL], rows_s, d), _flat_rows([vars_[n] for n in SMALL], rows_s, d),
                       "adamw_small")
    loss = packed[0].reshape(-1)[n_small]
    for n in SMALL:
        stepped[n] = []
    for a_ in packed:
        flat = a_.reshape(-1)
        off = 0
        for n in SMALL:
            stepped[n].append(flat[off:off + wts[n].size].reshape(wts[n].shape))
            off += wts[n].size
    return (loss, grad_x, *[stepped[n][0] for n in WEIGHTS], *[stepped[n][1] for n in WEIGHTS],
            *[stepped[n][2] for n in WEIGHTS], *[stepped[n][3] for n in WEIGHTS])
```

```python
import functools
import math

import jax
import jax.numpy as jnp
import numpy as np
from jax import lax
from jax.experimental import pallas as pl
from jax.experimental.pallas import tpu as pltpu

F32 = jnp.float32
BF16 = jnp.bfloat16

EPS = 1e-6
N_MOD = 9
HEADS = 8
QK_NOPE, QK_ROPE, V_HEAD = 64, 32, 64
QK_HEAD = QK_NOPE + QK_ROPE
HEAD_PAD = 128
LN2 = math.log(2.0)
SOFTMAX_SCALE = QK_HEAD ** -0.5 / LN2
Q_LORA, KV_LORA = 256, 128
GROUPS, GROUP_DIM, CHUNK = 8, 64, 128
GMLP_W = GROUPS * GROUP_DIM
MLA_W = HEADS * V_HEAD
IN_COLS = 1440
PROJ_COLS = 1536
GRID_W = 64
ROPE_BASE = 10000.0
MOD_ROWS = 16
ADAM_LR, ADAM_B1, ADAM_B2, ADAM_EPS, ADAM_WD, ADAM_STEP = 0.001, 0.9, 0.999, 1e-08, 0.01, 10
N_CHIPS = 4
LANES = 128
V7X_VMEM_LIMIT = 56 * 1024 * 1024
GELU_C = math.sqrt(2.0 / math.pi)

SHARDED = ("w_ada", "ffn1_w1", "ffn1_w3", "ffn1_w2", "w_in", "w_uq", "w_ukv", "w_out", "ffn2_w1", "ffn2_w3", "ffn2_w2")
ROW_SHARDED = ("ffn1_w2", "w_out", "ffn2_w2")
T_WEIGHTS = ("ffn1_w1", "ffn1_w3", "ffn2_w1", "ffn2_w3", "w_in", "w_uq")
FIRST_WEIGHTS = ("ffn1_w1", "ffn1_w3", "ffn1_w2")
MIX_WEIGHTS = ("w_in", "w_uq", "w_ukv", "w_out")
LAST_WEIGHTS = ("ffn2_w1", "ffn2_w3", "ffn2_w2")
SMALL = ("c_ctx", "b_ada", "norm1_w", "norm2_w", "q_a_norm_w", "kv_a_norm_w", "q_norm_w", "k_norm_w", "v_norm_w",
         "w_s", "b_s", "norm3_w")
WEIGHTS = ("c_ctx", "w_ada", "b_ada", "norm1_w", "ffn1_w1", "ffn1_w3", "ffn1_w2", "norm2_w", "w_in", "q_a_norm_w",
           "w_uq", "kv_a_norm_w", "w_ukv", "q_norm_w", "k_norm_w", "v_norm_w", "w_s", "b_s", "w_out", "norm3_w",
           "ffn2_w1", "ffn2_w3", "ffn2_w2")


def _round_up(n, m):
    return (n + m - 1) // m * m


def _div_tile(n, target, mult):
    best = None
    for t in range(mult, min(n, target) + 1, mult):
        if n % t == 0:
            best = t
    return n if best is None else best


def _dot(a, b):
    return lax.dot_general(a, b, (((1,), (0,)), ((), ())), preferred_element_type=F32)


def _dot_nt(a, b):
    return lax.dot_general(a, b, (((1,), (1,)), ((), ())), preferred_element_type=F32)


def _dot_tn(a, b):
    return lax.dot_general(a, b, (((0,), (0,)), ((), ())), preferred_element_type=F32)


def _sigmoid(x):
    return 1.0 / (1.0 + jnp.exp(-x))


def _gelu(x):
    return 0.5 * x * (1.0 + jnp.tanh(GELU_C * (x + 0.044715 * x * x * x)))


def _gelu_grad(x):
    t = jnp.tanh(GELU_C * (x + 0.044715 * x * x * x))
    return 0.5 * (1.0 + t) + 0.5 * x * (1.0 - t * t) * (GELU_C * (1.0 + 3 * 0.044715 * x * x))


def _rope3(x, cos, sin_a, sin_b):
    return x * cos + pltpu.roll(x, 8, 2) * sin_a + pltpu.roll(x, HEAD_PAD - 8, 2) * sin_b


def _rope3_t(d, cos, sin_a, sin_b):
    return d * cos + pltpu.roll(d * sin_a, HEAD_PAD - 8, 2) + pltpu.roll(d * sin_b, 8, 2)


def _group_sum(x, ones_ref):
    hi = x.astype(BF16)
    lo = (x - hi.astype(F32)).astype(BF16)
    return _dot(hi, ones_ref[...]) + _dot(lo, ones_ref[...])


def _params(n_axes):
    return pltpu.CompilerParams(dimension_semantics=("arbitrary",) * n_axes, vmem_limit_bytes=V7X_VMEM_LIMIT)


def _whole(shape):
    nd = len(shape)
    return pl.BlockSpec(shape, lambda *_: (0,) * nd, pipeline_mode=pl.Buffered(1))


def _sds(shape, dtype):
    return jax.ShapeDtypeStruct(shape, dtype)


def _token_tile(s, ctx):
    return _div_tile(math.gcd(s, ctx), 256, CHUNK)


def _other_chips(x, y):
    return [(1 - x, y), (x, 1 - y), (1 - x, 1 - y)]


def _exch_copies(kind, srcs, dsts, send_sems, recv_sems, local_sems, with_arrivals):
    x, y, c = lax.axis_index("x"), lax.axis_index("y"), lax.axis_index("c")
    me = 2 * x + y
    local, sends, arrivals = [], [], []
    for w, (src, dst) in enumerate(zip(srcs, dsts)):
        own = src if kind == "gather" else src.at[me]
        local.append(pltpu.make_async_copy(own, dst.at[me], local_sems.at[w]))
        for k, (px, py) in enumerate(_other_chips(x, y)):
            sem = dict(send_sem=send_sems.at[3 * w + k], recv_sem=recv_sems.at[3 * w + k], device_id=(px, py, c),
                       device_id_type=pl.DeviceIdType.MESH)
            out = src if kind == "gather" else src.at[2 * px + py]
            sends.append(pltpu.make_async_remote_copy(src_ref=out, dst_ref=dst.at[me], **sem))
            if with_arrivals:
                arrivals.append(pltpu.make_async_remote_copy(src_ref=own, dst_ref=dst.at[2 * px + py], **sem))
    return local, sends, arrivals


def _exch_start(kind, srcs, dsts, sems):
    local, sends, _ = _exch_copies(kind, srcs, dsts, *sems, with_arrivals=False)
    for cp in local + sends:
        cp.start()


def _exch_wait(kind, srcs, dsts, sems):
    local, sends, arrivals = _exch_copies(kind, srcs, dsts, *sems, with_arrivals=True)
    for cp in arrivals:
        cp.wait_recv()
    for cp in sends:
        cp.wait_send()
    for cp in local:
        cp.wait()


def _exch_scratch(n):
    return [pltpu.SemaphoreType.DMA((3 * n,)), pltpu.SemaphoreType.DMA((3 * n,)), pltpu.SemaphoreType.DMA((n,))]


def _exch_shapes(kind, arrays):
    return [_sds((N_CHIPS,) + a.shape if kind == "gather" else a.shape, a.dtype) for a in arrays]


def _sibling_copies(srcs, dsts, send_sems, recv_sems):
    x, y, c = lax.axis_index("x"), lax.axis_index("y"), lax.axis_index("c")
    return [pltpu.make_async_remote_copy(
        src_ref=src, dst_ref=dst, send_sem=send_sems.at[w], recv_sem=recv_sems.at[w], device_id=(x, y, 1 - c),
        device_id_type=pl.DeviceIdType.MESH) for w, (src, dst) in enumerate(zip(srcs, dsts))]


def _hosted_call(body, name, grid, in_specs, out_specs, out_shape, operands, scratch=(), exch=None, swap=None):
    n_axes = len(grid)
    if exch is None and swap is None:
        outs = pl.pallas_call(body, name=name, grid=grid, in_specs=list(in_specs), out_specs=list(out_specs),
                              out_shape=list(out_shape), scratch_shapes=list(scratch),
                              compiler_params=_params(n_axes))(*operands)
        return list(outs), []
    kind, arrays = exch if exch is not None else ("scatter", [])
    swaps = list(swap or [])
    n_in, n_out, n_sc, n_ex, n_sw = len(in_specs), len(out_specs), len(scratch), len(arrays), len(swaps)

    def hosted(*refs):
        cin, ein, sin = refs[:n_in], refs[n_in:n_in + n_ex], refs[n_in + n_ex:n_in + n_ex + n_sw]
        o0 = n_in + n_ex + n_sw
        cout, eout, sout = refs[o0:o0 + n_out], refs[o0 + n_out:o0 + n_out + n_ex], refs[o0 + n_out + n_ex:o0 + n_out + n_ex + n_sw]
        rest = refs[o0 + n_out + n_ex + n_sw:]
        csc, sems, swap_sems = rest[:n_sc], rest[n_sc:n_sc + 3], rest[n_sc + 3:]
        first = functools.reduce(jnp.logical_and, [pl.program_id(a) == 0 for a in range(n_axes)])
        last = functools.reduce(jnp.logical_and, [pl.program_id(a) == grid[a] - 1 for a in range(n_axes)])

        @pl.when(first)
        def _():
            if n_ex:
                _exch_start(kind, ein, eout, sems)
            for cp in _sibling_copies(sin, sout, *swap_sems) if n_sw else []:
                cp.start()

        body(*cin, *cout, *csc)

        @pl.when(last)
        def _():
            if n_ex:
                _exch_wait(kind, ein, eout, sems)
            for cp in _sibling_copies(sin, sout, *swap_sems) if n_sw else []:
                cp.wait()

    any_spec = pl.BlockSpec(memory_space=pl.ANY)
    swap_scratch = [pltpu.SemaphoreType.DMA((n_sw,)), pltpu.SemaphoreType.DMA((n_sw,))] if n_sw else []
    outs = pl.pallas_call(
        hosted, name=name, grid=grid, in_specs=list(in_specs) + [any_spec] * (n_ex + n_sw),
        out_specs=list(out_specs) + [any_spec] * (n_ex + n_sw),
        out_shape=list(out_shape) + _exch_shapes(kind, arrays) + [_sds(a.shape, a.dtype) for a in swaps],
        scratch_shapes=list(scratch) + _exch_scratch(max(n_ex, 1)) + swap_scratch, compiler_params=_params(n_axes),
    )(*operands, *arrays, *swaps)
    got = list(outs[n_out:n_out + n_ex])
    return (list(outs[:n_out]), got) if swap is None else (list(outs[:n_out]), got, list(outs[n_out + n_ex:]))


class _TokenTiles:
    def __init__(self, t, tc, tm):
        self.n_lat, self.n_ctx = t // tm, tc // tm
        self.n_all = self.n_lat + self.n_ctx

    def tile(self, i):
        return (i + self.n_lat) % self.n_all if self.n_ctx else i

    def is_lat(self, i):
        return self.tile(i) < self.n_lat

    def row(self, i):
        return (self.tile(i), 0)

    def lat_row(self, i):
        return (jnp.where(self.is_lat(i), self.tile(i), 0), 0) if self.n_ctx else (i, 0)

    def ctx_row(self, i):
        return (jnp.where(self.is_lat(i), self.n_ctx - 1, self.tile(i) - self.n_lat), 0)


def _ffn_fwd(x_lat, x_ctx, mod, nw, w1, w3, w2, k0, s, nb, tm, name, target=None, exch=None):
    t, d = x_lat.shape
    tc = 0 if x_ctx is None else x_ctx.shape[0]
    f = w1.shape[0]
    tiles = _TokenTiles(t, tc, tm)
    n_x = 2 if tc else 1
    n_t = 0 if target is None else 1
    assert not (tc and n_t)

    def body(*refs):
        x_ref = refs[0]
        t_ref = refs[n_x] if n_t else None
        mod_ref, nw_ref, w1_ref, w3_ref, w2_ref, o_ref, a_ref, b_ref, y_ref = refs[n_x + n_t:n_x + n_t + 9]
        i = pl.program_id(0)
        g = jnp.minimum((tiles.tile(i) * tm) // s, nb)
        shift = mod_ref[g, pl.ds(k0, 1), :]
        scale = mod_ref[g, pl.ds(k0 + 1, 1), :]
        gate = mod_ref[g, pl.ds(k0 + 2, 1), :]
        x = jnp.where(tiles.is_lat(i), x_ref[...], refs[1][...]) if tc else x_ref[...]
        r = lax.rsqrt(jnp.mean(x * x, axis=-1, keepdims=True) + EPS)
        hb = ((x * r * nw_ref[...]) * (1.0 + scale) + shift).astype(BF16)
        a = _dot_nt(hb, w1_ref[...])
        b = _dot_nt(hb, w3_ref[...])
        gb = (a * _sigmoid(a) * b).astype(BF16)
        y = _dot(gb, w2_ref[...])
        out = x + (0.5 * gate) * y
        a_ref[...] = a.astype(BF16)
        b_ref[...] = b.astype(BF16)
        y_ref[...] = y.astype(BF16)
        if n_t:
            loss_ref, acc_ref = refs[-2:]

            @pl.when(i == 0)
            def _():
                acc_ref[...] = jnp.zeros_like(acc_ref)

            e = out - t_ref[...]
            o_ref[...] = e * (1.0 / d)
            acc_ref[...] += jnp.sum(e * e, axis=0, keepdims=True)

            @pl.when(i == tiles.n_all - 1)
            def _():
                loss_ref[...] = (0.5 / d) * jnp.sum(acc_ref[...], axis=-1, keepdims=True)
        else:
            o_ref[...] = out

    td = pl.BlockSpec((tm, d), tiles.row)
    tf = pl.BlockSpec((tm, f), tiles.row)
    return _hosted_call(
        body, name, (tiles.n_all,),
        [pl.BlockSpec((tm, d), tiles.lat_row)] + ([pl.BlockSpec((tm, d), tiles.ctx_row)] if tc else []) + [td] * n_t
        + [_whole(mod.shape), _whole(nw.shape), _whole(w1.shape), _whole(w3.shape), _whole(w2.shape)],
        [td, tf, tf, td] + [pl.BlockSpec((1, 1), lambda i: (0, 0))] * n_t,
        [_sds((t + tc, d), F32), _sds((t + tc, f), BF16), _sds((t + tc, f), BF16), _sds((t + tc, d), BF16)]
        + [_sds((1, 1), F32)] * n_t,
        (x_lat,) + ((x_ctx,) if tc else ()) + ((target,) if n_t else ()) + (mod, nw, w1, w3, w2),
        scratch=[pltpu.VMEM((1, d), F32)] * n_t, exch=exch)


def _ffn_bwd(dout, x_lat, x_ctx, a, b, y, mod, nw, w1, w3, w2, k0, s, nb, tm, name, exch=None):
    t, d = x_lat.shape
    tc = 0 if x_ctx is None else x_ctx.shape[0]
    f = w1.shape[0]
    nch = 2 if (f // 2) % LANES == 0 and f % 2 == 0 else 1
    fc = f // nch
    tiles = _TokenTiles(t, tc, tm)
    n_x = 2 if tc else 1

    def body(*refs):
        do_ref, x_ref = refs[0], refs[1]
        (a_ref, b_ref, y_ref, mod_ref, nw_ref, w1_ref, w3_ref, w2_ref,
         dx_ref, h_ref, g_ref, da_ref, db_ref, dy_ref, dmod_ref, dnw_ref) = refs[1 + n_x:]
        i = pl.program_id(0)

        @pl.when(i == 0)
        def _():
            dmod_ref[...] = jnp.zeros_like(dmod_ref)
            dnw_ref[...] = jnp.zeros_like(dnw_ref)

        g = jnp.minimum((tiles.tile(i) * tm) // s, nb)
        shift = mod_ref[g, pl.ds(k0, 1), :]
        scale = mod_ref[g, pl.ds(k0 + 1, 1), :]
        gate = mod_ref[g, pl.ds(k0 + 2, 1), :]
        x = jnp.where(tiles.is_lat(i), x_ref[...], refs[2][...]) if tc else x_ref[...]
        dout_v = do_ref[...]
        r = lax.rsqrt(jnp.mean(x * x, axis=-1, keepdims=True) + EPS)
        xh = x * r
        n = xh * nw_ref[...]
        h_ref[...] = (n * (1.0 + scale) + shift).astype(BF16)
        dyb = ((0.5 * gate) * dout_v).astype(BF16)
        dy_ref[...] = dyb
        dmod_ref[g, pl.ds(k0 + 2, 1), :] += 0.5 * jnp.sum(dout_v * y_ref[...].astype(F32), axis=0, keepdims=True)
        dh = jnp.zeros((tm, d), F32)
        for c in range(nch):
            sl = slice(c * fc, (c + 1) * fc)
            dg = _dot_nt(dyb, w2_ref[sl, :])
            av = a_ref[:, sl].astype(F32)
            bv = b_ref[:, sl].astype(F32)
            sig = _sigmoid(av)
            sa = av * sig
            g_ref[:, sl] = (sa * bv).astype(BF16)
            dab = (dg * bv * (sig * (1.0 + av * (1.0 - sig)))).astype(BF16)
            dbb = (dg * sa).astype(BF16)
            da_ref[:, sl] = dab
            db_ref[:, sl] = dbb
            dh = dh + _dot(dab, w1_ref[sl, :]) + _dot(dbb, w3_ref[sl, :])
        dmod_ref[g, pl.ds(k0, 1), :] += jnp.sum(dh, axis=0, keepdims=True)
        dmod_ref[g, pl.ds(k0 + 1, 1), :] += jnp.sum(dh * n, axis=0, keepdims=True)
        dn = dh * (1.0 + scale)
        dnw_ref[...] += jnp.sum(dn * xh, axis=0, keepdims=True)
        dxh = dn * nw_ref[...]
        dx_ref[...] = dout_v + r * (dxh - xh * jnp.mean(dxh * xh, axis=-1, keepdims=True))

    td = pl.BlockSpec((tm, d), tiles.row)
    tf = pl.BlockSpec((tm, f), tiles.row)
    lat = pl.BlockSpec((tm, d), tiles.lat_row)
    ta = t + tc
    return _hosted_call(
        body, name, (tiles.n_all,),
        [td, lat] + ([pl.BlockSpec((tm, d), tiles.ctx_row)] if tc else [])
        + [tf, tf, td, _whole(mod.shape), _whole(nw.shape), _whole(w1.shape), _whole(w3.shape), _whole(w2.shape)],
        [lat, td, tf, tf, tf, td, pl.BlockSpec(mod.shape, lambda i: (0, 0, 0)), pl.BlockSpec((1, d), lambda i: (0, 0))],
        [_sds((t, d), F32), _sds((ta, d), BF16), _sds((ta, f), BF16), _sds((ta, f), BF16), _sds((ta, f), BF16),
         _sds((ta, d), BF16), _sds(mod.shape, F32), _sds((1, d), F32)],
        (dout, x_lat) + ((x_ctx,) if tc else ()) + (a, b, y, mod, nw, w1, w3, w2), exch=exch)


def _mm_tn(a, b, rows, name, exch=None):
    m = a.shape[1]
    n = b.shape[1]
    bm = _div_tile(m, 1408, LANES)
    bn = _div_tile(n, 1408, LANES)
    bk = _div_tile(rows, 2304, LANES)
    nk = rows // bk

    def body(a_ref, b_ref, o_ref, acc_ref):
        k = pl.program_id(2)

        @pl.when(k == 0)
        def _():
            acc_ref[...] = jnp.zeros_like(acc_ref)

        acc_ref[...] += _dot_tn(a_ref[...], b_ref[...])

        @pl.when(k == nk - 1)
        def _():
            o_ref[...] = acc_ref[...].astype(BF16)

    (out,), got = _hosted_call(
        body, name, (m // bm, n // bn, nk),
        [pl.BlockSpec((bk, bm), lambda i, j, k: (k, i)), pl.BlockSpec((bk, bn), lambda i, j, k: (k, j))],
        [pl.BlockSpec((bm, bn), lambda i, j, k: (i, j))], [_sds((m, n), BF16)], (a, b),
        scratch=[pltpu.VMEM((bm, bn), F32)], exch=exch)
    return out if exch is None else (out, got)


def _mixin_fwd(xs, mod, nw, wp, s, nb, tm):
    t, d = xs.shape

    def body(x_ref, mod_ref, nw_ref, wp_ref, h_ref, p_ref):
        g = jnp.minimum((pl.program_id(0) * tm) // s, nb)
        shift = mod_ref[g, pl.ds(3, 1), :]
        scale = mod_ref[g, pl.ds(4, 1), :]
        x = x_ref[...]
        r = lax.rsqrt(jnp.mean(x * x, axis=-1, keepdims=True) + EPS)
        hb = ((x * r * nw_ref[...]) * (1.0 + scale) + shift).astype(BF16)
        h_ref[...] = hb
        p_ref[...] = _dot_nt(hb, wp_ref[...]).astype(BF16)

    row = lambda i: (i, 0)
    return pl.pallas_call(
        body, name="mixin_fwd", grid=(t // tm,),
        in_specs=[pl.BlockSpec((tm, d), row), _whole(mod.shape), _whole(nw.shape), _whole(wp.shape)],
        out_specs=[pl.BlockSpec((tm, d), row), pl.BlockSpec((tm, PROJ_COLS), row)],
        out_shape=[_sds((t, d), BF16), _sds((t, PROJ_COLS), BF16)], compiler_params=_params(1),
    )(xs, mod, nw, wp)


def _mixin_bwd(dp0, duv, xs, dres, mod, nw, wp, s, nb, tm, swap):
    t_all, d = xs.shape
    nlat = dres.shape[0] // tm

    def body(p0_ref, uv_ref, x_ref, dr_ref, mod_ref, nw_ref, wp_ref, dx_ref, dmod_ref, dnw_ref):
        i = pl.program_id(0)

        @pl.when(i == 0)
        def _():
            dmod_ref[...] = jnp.zeros_like(dmod_ref)
            dnw_ref[...] = jnp.zeros_like(dnw_ref)

        lat = i < nlat
        g = jnp.minimum((i * tm) // s, nb)
        scale = mod_ref[g, pl.ds(4, 1), :]
        dh = _dot(p0_ref[...], wp_ref[0:512, :])
        extra = _dot(uv_ref[...], wp_ref[512:1536, :])
        dh = dh + jnp.where(lat, extra, 0.0)
        x = x_ref[...]
        r = lax.rsqrt(jnp.mean(x * x, axis=-1, keepdims=True) + EPS)
        xh = x * r
        n = xh * nw_ref[...]
        dmod_ref[g, pl.ds(3, 1), :] += jnp.sum(dh, axis=0, keepdims=True)
        dmod_ref[g, pl.ds(4, 1), :] += jnp.sum(dh * n, axis=0, keepdims=True)
        dn = dh * (1.0 + scale)
        dnw_ref[...] += jnp.sum(dn * xh, axis=0, keepdims=True)
        dxh = dn * nw_ref[...]
        dx_ref[...] = jnp.where(lat, dr_ref[...], 0.0) + r * (dxh - xh * jnp.mean(dxh * xh, axis=-1, keepdims=True))

    row = lambda i: (i, 0)
    lrow = lambda i: (jnp.minimum(i, nlat - 1), 0)
    return _hosted_call(
        body, "mixin_bwd", (t_all // tm,),
        [pl.BlockSpec((tm, 512), row), pl.BlockSpec((tm, 1024), lrow), pl.BlockSpec((tm, d), row),
         pl.BlockSpec((tm, d), lrow), _whole(mod.shape), _whole(nw.shape), _whole(wp.shape)],
        [pl.BlockSpec((tm, d), row), pl.BlockSpec(mod.shape, lambda i: (0, 0, 0)), pl.BlockSpec((1, d), lambda i: (0, 0))],
        [_sds((t_all, d), F32), _sds(mod.shape, F32), _sds((1, d), F32)], (dp0, duv, xs, dres, mod, nw, wp), swap=swap)


def _prep_fwd(proj, row0, nb, s, pos0, sk, key0, into, tabs, wq, wk, wv, kvaw, qaw, qnw, knw, tm, with_q, name):
    nblk = s // tm
    n_into = 0 if into is None else 2

    def body(p_ref, cos_ref, sa_ref, sb_ref, wq_ref, wk_ref, wv_ref, kvaw_ref, qaw_ref, qnw_ref, knw_ref, *rest):
        outs, heads_ref = rest[n_into:-1], rest[-1]
        q_ref, k_ref, v_ref = outs if with_q else (None,) + outs
        cos, sin_a, sin_b = cos_ref[...][None], sa_ref[...][None], sb_ref[...][None]

        def normed_roped(w_ref, src, extra, nw_ref, o_ref, post):
            for h in range(HEADS):
                heads_ref[h] = _dot_nt(src, w_ref[h]) if extra is None else _dot(src, w_ref[h])
            xp = heads_ref[...] if extra is None else heads_ref[...] + extra[None]
            r = lax.rsqrt(jnp.sum(xp * xp, axis=-1, keepdims=True) * (1.0 / QK_HEAD) + EPS)
            o_ref[...] = _rope3(xp * r * (nw_ref[...] * post)[None], cos, sin_a, sin_b).astype(BF16)

        ckv = p_ref[:, 0:128].astype(F32)
        rkv = lax.rsqrt(jnp.mean(ckv * ckv, axis=-1, keepdims=True) + EPS)
        ckvb = (ckv * rkv * kvaw_ref[...]).astype(BF16)
        normed_roped(wk_ref, ckvb, p_ref[:, 128:256].astype(F32), knw_ref, k_ref, 1.0)
        for j in range(HEADS // 2):
            v_ref[j] = _dot(ckvb, wv_ref[j]).astype(BF16)
        if with_q:
            cq = p_ref[:, 256:512].astype(F32)
            rq = lax.rsqrt(jnp.mean(cq * cq, axis=-1, keepdims=True) + EPS)
            normed_roped(wq_ref, (cq * rq * qaw_ref[...]).astype(BF16), None, qnw_ref, q_ref, SOFTMAX_SCALE)

    tab = pl.BlockSpec((tm, HEAD_PAD), lambda i: (pos0 + i % nblk, 0))
    qspec = pl.BlockSpec((None, HEADS, tm, HEAD_PAD), lambda i: (i // nblk, 0, i % nblk, 0))
    kspec = pl.BlockSpec((None, HEADS, tm, HEAD_PAD), lambda i: (i // nblk, 0, key0 + i % nblk, 0))
    vspec = pl.BlockSpec((None, HEADS // 2, tm, HEAD_PAD), lambda i: (i // nblk, 0, key0 + i % nblk, 0))
    qshape = _sds((nb, HEADS, s, HEAD_PAD), BF16)
    kshape = _sds((nb, HEADS, sk, HEAD_PAD), BF16)
    vshape = _sds((nb, HEADS // 2, sk, HEAD_PAD), BF16)
    n_q = 1 if with_q else 0
    return pl.pallas_call(
        body, name=name, grid=(nb * nblk,),
        in_specs=[pl.BlockSpec((tm, 512), lambda i: (row0 + i, 0)), tab, tab, tab, _whole(wq.shape), _whole(wk.shape),
                  _whole(wv.shape), _whole(kvaw.shape), _whole(qaw.shape), _whole(qnw.shape), _whole(knw.shape)]
        + [pl.BlockSpec(memory_space=pl.ANY)] * n_into,
        out_specs=([qspec] if with_q else []) + [kspec, vspec],
        out_shape=([qshape] if with_q else []) + [kshape, vshape],
        scratch_shapes=[pltpu.VMEM((HEADS, tm, HEAD_PAD), F32)],
        input_output_aliases={11: n_q, 12: n_q + 1} if n_into else {}, compiler_params=_params(1),
    )(proj, *tabs, wq, wk, wv, kvaw, qaw, qnw, knw, *(into or ()))


def _prep_bwd(proj, row0, nb, s, pos0, key0, dp_rows, dp_into, tabs, wq, wk, wv, kvaw, qaw, qnw, knw, dq, dk, dv, init, tm,
              name):
    nblk = s // tm
    with_q = dq is not None
    n_init = 0 if init is None else len(init)
    n_into = 0 if dp_into is None else 1

    def body(*refs):
        p_ref, cos_ref, sa_ref, sb_ref, wq_ref, wk_ref, wv_ref, kvaw_ref, qaw_ref, qnw_ref, knw_ref = refs[:11]
        rest = list(refs[11:])
        dq_ref = rest.pop(0) if with_q else None
        dk_ref, dv_ref = rest.pop(0), rest.pop(0)
        init_refs = [rest.pop(0) for _ in range(n_init)]
        if n_into:
            rest.pop(0)
        dp_ref = rest.pop(0)
        if with_q:
            dwq_ref, dqaw_ref, dqnw_ref = rest.pop(0), rest.pop(0), rest.pop(0)
        dwk_ref, dwv_ref, dkvaw_ref, dknw_ref, heads_ref, dhb_ref, dkr_ref = rest
        accs = [dwk_ref, dwv_ref, dkvaw_ref, dknw_ref]

        @pl.when(pl.program_id(0) == 0)
        def _():
            for k, acc in enumerate(accs):
                acc[...] = init_refs[k][...] if n_init else jnp.zeros_like(acc)
            if with_q:
                dwq_ref[...] = jnp.zeros_like(dwq_ref)
                dqaw_ref[...] = jnp.zeros_like(dqaw_ref)
                dqnw_ref[...] = jnp.zeros_like(dqnw_ref)

        cos, sin_a, sin_b = cos_ref[...][None], sa_ref[...][None], sb_ref[...][None]
        lane = lax.broadcasted_iota(jnp.int32, (tm, HEAD_PAD), 1)
        rope_lanes = (lane >= QK_NOPE) & (lane < QK_HEAD)

        def heads_bwd(w_ref, src, extra, nw_ref, d_ref, dnw_ref, dw_ref, post):
            w_t = extra is None
            for h in range(HEADS):
                heads_ref[h] = _dot_nt(src, w_ref[h]) if w_t else _dot(src, w_ref[h])
            xp = heads_ref[...] if extra is None else heads_ref[...] + extra[None]
            r = lax.rsqrt(jnp.sum(xp * xp, axis=-1, keepdims=True) * (1.0 / QK_HEAD) + EPS)
            xh = xp * r
            dn = _rope3_t(d_ref[...].astype(F32), cos, sin_a, sin_b)
            dnw_ref[...] += post * jnp.sum(jnp.sum(dn * xh, axis=0), axis=0, keepdims=True)
            dxh = dn * (nw_ref[...] * post)[None]
            dxp = r * (dxh - xh * (jnp.sum(dxh * xh, axis=-1, keepdims=True) * (1.0 / QK_HEAD)))
            dhb_ref[...] = dxp.astype(BF16)
            dsrc = jnp.zeros((tm, src.shape[1]), F32)
            for h in range(HEADS):
                dsrc = dsrc + (_dot(dhb_ref[h], w_ref[h]) if w_t else _dot_nt(dhb_ref[h], w_ref[h]))
                dw_ref[h] += _dot_tn(src, dhb_ref[h])
            return dsrc, jnp.sum(dxp, axis=0)

        ckv = p_ref[:, 0:128].astype(F32)
        rkv = lax.rsqrt(jnp.mean(ckv * ckv, axis=-1, keepdims=True) + EPS)
        ckvh = ckv * rkv
        ckvb = (ckvh * kvaw_ref[...]).astype(BF16)
        for h in range(HEADS):
            dkr_ref[h] = dk_ref[h].astype(F32).T
        dckv, dkp_sum = heads_bwd(wk_ref, ckvb, p_ref[:, 128:256].astype(F32), knw_ref, dkr_ref, dknw_ref, dwk_ref,
                                  1.0)
        for j in range(HEADS // 2):
            dvb = dv_ref[j].astype(F32).T.astype(BF16)
            dckv = dckv + _dot_nt(dvb, wv_ref[j])
            dwv_ref[j] += _dot_tn(ckvb, dvb)
        dkvaw_ref[...] += jnp.sum(dckv * ckvh, axis=0, keepdims=True)
        dch = dckv * kvaw_ref[...]
        dp_ref[:, 0:128] = (rkv * (dch - ckvh * jnp.mean(dch * ckvh, axis=-1, keepdims=True))).astype(BF16)
        dp_ref[:, 128:256] = jnp.where(rope_lanes, dkp_sum, 0.0).astype(BF16)
        if with_q:
            cq = p_ref[:, 256:512].astype(F32)
            rq = lax.rsqrt(jnp.mean(cq * cq, axis=-1, keepdims=True) + EPS)
            cqh = cq * rq
            cqb = (cqh * qaw_ref[...]).astype(BF16)
            dcq, _ = heads_bwd(wq_ref, cqb, None, qnw_ref, dq_ref, dqnw_ref, dwq_ref, SOFTMAX_SCALE)
            dqaw_ref[...] += jnp.sum(dcq * cqh, axis=0, keepdims=True)
            dqc = dcq * qaw_ref[...]
            dp_ref[:, 256:512] = (rq * (dqc - cqh * jnp.mean(dqc * cqh, axis=-1, keepdims=True))).astype(BF16)
        else:
            dp_ref[:, 256:512] = jnp.zeros((tm, Q_LORA), BF16)

    tab = pl.BlockSpec((tm, HEAD_PAD), lambda i: (pos0 + i % nblk, 0))
    qspec = pl.BlockSpec((None, HEADS, tm, HEAD_PAD), lambda i: (i // nblk, 0, i % nblk, 0))
    kspec = pl.BlockSpec((None, HEADS, HEAD_PAD, tm), lambda i: (i // nblk, 0, 0, key0 + i % nblk))
    vspec = pl.BlockSpec((None, HEADS // 2, HEAD_PAD, tm), lambda i: (i // nblk, 0, 0, key0 + i % nblk))

    def acc_spec(shape):
        nd = len(shape)
        return pl.BlockSpec(shape, lambda i: (0,) * nd)

    acc_shapes = [(HEADS, KV_LORA, HEAD_PAD), (HEADS // 2, KV_LORA, HEAD_PAD), (1, KV_LORA), (1, HEAD_PAD)]
    q_shapes = [(HEADS, Q_LORA, HEAD_PAD), (1, Q_LORA), (1, HEAD_PAD)] if with_q else []
    out_shapes = [(dp_rows, 512)] + q_shapes + acc_shapes
    n_before = 11 + (1 if with_q else 0) + 2 + n_init
    return pl.pallas_call(
        body, name=name, grid=(nb * nblk,),
        in_specs=[pl.BlockSpec((tm, 512), lambda i: (row0 + i, 0)), tab, tab, tab, _whole(wq.shape), _whole(wk.shape),
                  _whole(wv.shape), _whole(kvaw.shape), _whole(qaw.shape), _whole(qnw.shape), _whole(knw.shape)]
        + ([qspec] if with_q else []) + [kspec, vspec] + [_whole(a.shape) for a in (init or [])]
        + [pl.BlockSpec(memory_space=pl.ANY)] * n_into,
        out_specs=[pl.BlockSpec((tm, 512), lambda i: (row0 + i, 0))] + [acc_spec(sh) for sh in q_shapes + acc_shapes],
        out_shape=[_sds(out_shapes[0], BF16)] + [_sds(sh, F32) for sh in out_shapes[1:]],
        scratch_shapes=[pltpu.VMEM((HEADS, tm, HEAD_PAD), F32), pltpu.VMEM((HEADS, tm, HEAD_PAD), BF16),
                        pltpu.VMEM((HEADS, tm, HEAD_PAD), F32)],
        input_output_aliases={n_before: 0} if n_into else {}, compiler_params=_params(1),
    )(proj, *tabs, wq, wk, wv, kvaw, qaw, qnw, knw, *([dq] if with_q else []), dk, dv, *(init or []),
      *([dp_into] if n_into else []))


def _attn_fwd(q, k, v, tq, exch=None):
    nb, _, s, _ = q.shape
    sk = k.shape[2]
    nq = s // tq

    def body(q_ref, k_ref, v_ref, o_ref, lse_ref, vext_ref):
        @pl.when(pl.program_id(2) == 0)
        def _():
            vext_ref[:, 0:HEAD_PAD] = v_ref[...]
            vext_ref[:, HEAD_PAD:2 * HEAD_PAD] = jnp.ones((sk, HEAD_PAD), BF16)

        lane = lax.broadcasted_iota(jnp.int32, (tq, HEAD_PAD), 1)
        outs = []
        for hh in range(2):
            sc = _dot_nt(q_ref[hh], k_ref[hh])
            m = jnp.max(sc, axis=-1, keepdims=True)
            pv = _dot(jnp.exp2(sc - m).astype(BF16), vext_ref[...])
            l = pv[:, HEAD_PAD:HEAD_PAD + 1]
            outs.append(pv[:, 0:HEAD_PAD] / l)
            lse_ref[hh] = m + jnp.log2(l)
        o_ref[...] = jnp.where(lane < V_HEAD, outs[0], outs[1]).astype(BF16)

    (o, lse), got = _hosted_call(
        body, "attn_fwd", (nb, HEADS // 2, nq),
        [pl.BlockSpec((None, 2, tq, HEAD_PAD), lambda b, j, i: (b, j, i, 0)),
         pl.BlockSpec((None, 2, sk, HEAD_PAD), lambda b, j, i: (b, j, 0, 0)),
         pl.BlockSpec((None, None, sk, HEAD_PAD), lambda b, j, i: (b, j, 0, 0))],
        [pl.BlockSpec((tq, HEAD_PAD), lambda b, j, i: (b * nq + i, j)),
         pl.BlockSpec((None, 2, tq, 1), lambda b, j, i: (b, j, i, 0))],
        [_sds((nb * s, MLA_W + GMLP_W), BF16), _sds((nb, HEADS, s, 1), F32)], (q, k, v),
        scratch=[pltpu.VMEM((sk, 2 * HEAD_PAD), BF16)], exch=exch)
    return o, lse, got


def _attn_bwd(q, k, v, do, o, lse, tq, exch=None):
    nb, _, s, _ = q.shape
    sk = k.shape[2]
    nq = s // tq

    def body(q_ref, k_ref, v_ref, do_ref, o_ref, lse_ref, dq_ref, dk_out, dv_out, dkt_ref, dvt_ref):
        @pl.when(pl.program_id(2) == 0)
        def _():
            dkt_ref[...] = jnp.zeros_like(dkt_ref)
            dvt_ref[...] = jnp.zeros_like(dvt_ref)

        lane = lax.broadcasted_iota(jnp.int32, (tq, HEAD_PAD), 1)
        dov = do_ref[...]
        prod = dov.astype(F32) * o_ref[...].astype(F32)
        for hh in range(2):
            mine = (lane < V_HEAD) if hh == 0 else (lane >= V_HEAD)
            doh = jnp.where(mine, dov, jnp.zeros_like(dov))
            delta = jnp.sum(jnp.where(mine, prod, 0.0), axis=-1, keepdims=True)
            qh = q_ref[hh]
            q_ln2 = (qh.astype(F32) * LN2).astype(BF16)
            kv = k_ref[hh]
            p = jnp.exp2(_dot_nt(qh, kv) - lse_ref[hh])
            u = (p * (_dot_nt(doh, v_ref[...]) - delta)).astype(BF16)
            dq_ref[hh] = (_dot(u, kv) * LN2).astype(BF16)
            dkt_ref[hh] += _dot_tn(q_ln2, u)
            dvt_ref[...] += _dot_tn(doh, p.astype(BF16))

        @pl.when(pl.program_id(2) == nq - 1)
        def _():
            dk_out[...] = dkt_ref[...].astype(BF16)
            dv_out[...] = dvt_ref[...].astype(BF16)

    qspec = pl.BlockSpec((None, 2, tq, HEAD_PAD), lambda b, j, i: (b, j, i, 0))
    kspec = pl.BlockSpec((None, 2, sk, HEAD_PAD), lambda b, j, i: (b, j, 0, 0))
    vspec = pl.BlockSpec((None, None, sk, HEAD_PAD), lambda b, j, i: (b, j, 0, 0))
    ospec = pl.BlockSpec((tq, HEAD_PAD), lambda b, j, i: (b * nq + i, j))
    return _hosted_call(
        body, "attn_bwd", (nb, HEADS // 2, nq),
        [qspec, kspec, vspec, ospec, ospec, pl.BlockSpec((None, 2, tq, 1), lambda b, j, i: (b, j, i, 0))],
        [qspec, pl.BlockSpec((None, 2, HEAD_PAD, sk), lambda b, j, i: (b, j, 0, 0)),
         pl.BlockSpec((None, None, HEAD_PAD, sk), lambda b, j, i: (b, j, 0, 0))],
        [_sds(q.shape, BF16), _sds((nb, HEADS, HEAD_PAD, sk), BF16), _sds((nb, HEADS // 2, HEAD_PAD, sk), BF16)],
        (q, k, v, do, o, lse), scratch=[pltpu.VMEM((2, HEAD_PAD, sk), F32), pltpu.VMEM((HEAD_PAD, sk), F32)], exch=exch)


def _group_masks(rows):
    lane = lax.broadcasted_iota(jnp.int32, (rows, GMLP_W), 1)
    return [(lane >= g * GROUP_DIM) & (lane < (g + 1) * GROUP_DIM) for g in range(GROUPS)]


def _gmlp_fwd(proj, mixcat, wcat, bias, vnw, ones, tm):
    t = mixcat.shape[0]

    def body(u_ref, v_ref, wcat_ref, bias_ref, vnw_ref, ones_ref, _, o_ref):
        masks = _group_masks(CHUNK)
        gv = _gelu(v_ref[...].astype(F32))
        rv = lax.rsqrt(_group_sum(gv * gv, ones_ref) * (1.0 / GROUP_DIM) + EPS)
        vnb = (gv * rv * vnw_ref[...]).astype(BF16)
        for c in range(tm // CHUNK):
            rows = slice(c * CHUNK, (c + 1) * CHUNK)
            vc = vnb[rows]
            stack = jnp.concatenate([jnp.where(m, vc, jnp.zeros_like(vc)) for m in masks], axis=0)
            sp = _dot(wcat_ref[...], stack) + bias_ref[...]
            o_ref[rows, :] = (_gelu(u_ref[rows, :].astype(F32)) * sp).astype(BF16)

    return pl.pallas_call(
        body, name="gmlp_fwd", grid=(t // tm,),
        in_specs=[pl.BlockSpec((tm, GMLP_W), lambda i: (i, 1)), pl.BlockSpec((tm, GMLP_W), lambda i: (i, 2)),
                  _whole(wcat.shape), _whole(bias.shape), _whole(vnw.shape), _whole(ones.shape),
                  pl.BlockSpec(memory_space=pl.ANY)],
        out_specs=pl.BlockSpec((tm, GMLP_W), lambda i: (i, 1)),
        out_shape=_sds(mixcat.shape, BF16), input_output_aliases={6: 0}, compiler_params=_params(1),
    )(proj, proj, wcat, bias, vnw, ones, mixcat)


def _gmlp_bwd(proj, dsg, wcat, wcat_t, bias, vnw, ones, tm):
    t = dsg.shape[0]

    def body(u_ref, v_ref, dsg_ref, wcat_ref, wcatt_ref, bias_ref, vnw_ref, ones_ref,
             duv_ref, dws_ref, dbs_ref, dvnw_ref):
        @pl.when(pl.program_id(0) == 0)
        def _():
            dws_ref[...] = jnp.zeros_like(dws_ref)
            dbs_ref[...] = jnp.zeros_like(dbs_ref)
            dvnw_ref[...] = jnp.zeros_like(dvnw_ref)

        masks = _group_masks(CHUNK)
        v = v_ref[...].astype(F32)
        gv = _gelu(v)
        rv = lax.rsqrt(_group_sum(gv * gv, ones_ref) * (1.0 / GROUP_DIM) + EPS)
        xh = gv * rv
        vnb = (xh * vnw_ref[...]).astype(BF16)
        dvn_parts = []
        for c in range(tm // CHUNK):
            rows = slice(c * CHUNK, (c + 1) * CHUNK)
            vc = vnb[rows]
            stack = jnp.concatenate([jnp.where(m, vc, jnp.zeros_like(vc)) for m in masks], axis=0)
            sp = _dot(wcat_ref[...], stack) + bias_ref[...]
            u = u_ref[rows, :].astype(F32)
            dsg_c = dsg_ref[rows, :].astype(F32)
            duv_ref[rows, 0:GMLP_W] = (dsg_c * sp * _gelu_grad(u)).astype(BF16)
            ds = dsg_c * _gelu(u)
            dstack = jnp.concatenate([jnp.where(m, ds, 0.0) for m in masks], axis=0)
            dbs_ref[...] += jnp.broadcast_to(jnp.sum(dstack, axis=-1, keepdims=True), dbs_ref.shape)
            dstb = dstack.astype(BF16)
            dvn_parts.append(_dot(wcatt_ref[...], dstb))
            dws_ref[...] += _dot_nt(dstb, vc)
        dvn = jnp.concatenate(dvn_parts, axis=0) if len(dvn_parts) > 1 else dvn_parts[0]
        dvnw_ref[...] += jnp.sum(dvn * xh, axis=0, keepdims=True)
        dxh = dvn * vnw_ref[...]
        gm = _group_sum(dxh * xh, ones_ref) * (1.0 / GROUP_DIM)
        duv_ref[:, GMLP_W:2 * GMLP_W] = (rv * (dxh - xh * gm) * _gelu_grad(v)).astype(BF16)

    row = pl.BlockSpec((tm, GMLP_W), lambda i: (i, 0))
    return pl.pallas_call(
        body, name="gmlp_bwd", grid=(t // tm,),
        in_specs=[pl.BlockSpec((tm, GMLP_W), lambda i: (i, 1)), pl.BlockSpec((tm, GMLP_W), lambda i: (i, 2)), row,
                  _whole(wcat.shape), _whole(wcat_t.shape), _whole(bias.shape), _whole(vnw.shape), _whole(ones.shape)],
        out_specs=[pl.BlockSpec((tm, 2 * GMLP_W), lambda i: (i, 0)), pl.BlockSpec((GROUPS * CHUNK, CHUNK), lambda i: (0, 0)),
                   pl.BlockSpec((GROUPS * CHUNK, CHUNK), lambda i: (0, 0)), pl.BlockSpec((1, GMLP_W), lambda i: (0, 0))],
        out_shape=[_sds((t, 2 * GMLP_W), BF16), _sds((GROUPS * CHUNK, CHUNK), F32), _sds((GROUPS * CHUNK, CHUNK), F32),
                   _sds((1, GMLP_W), F32)],
        compiler_params=_params(1),
    )(proj, proj, dsg, wcat, wcat_t, bias, vnw, ones)


def _mixout_fwd(mixcat, xs, mod, wout, s, tm):
    t, width = mixcat.shape
    d = xs.shape[1]

    def body(cat_ref, x_ref, mod_ref, w_ref, x2_ref, mix_ref):
        g = (pl.program_id(0) * tm) // s
        gate = mod_ref[g, pl.ds(5, 1), :]
        mix = _dot(cat_ref[...], w_ref[...])
        x2_ref[...] = x_ref[...] + gate * mix
        mix_ref[...] = mix.astype(BF16)

    row = lambda i: (i, 0)
    return pl.pallas_call(
        body, name="mixout_fwd", grid=(t // tm,),
        in_specs=[pl.BlockSpec((tm, width), row), pl.BlockSpec((tm, d), row), _whole(mod.shape), _whole(wout.shape)],
        out_specs=[pl.BlockSpec((tm, d), row), pl.BlockSpec((tm, d), row)],
        out_shape=[_sds((t, d), F32), _sds((t, d), BF16)], compiler_params=_params(1),
    )(mixcat, xs, mod, wout)


def _mixout_bwd(dx2, mix, mod, wout, s, tm):
    t, d = dx2.shape

    def body(dx_ref, mix_ref, mod_ref, w_ref, dmix_ref, do_ref, dsg_ref, dmod_ref):
        i = pl.program_id(0)

        @pl.when(i == 0)
        def _():
            dmod_ref[...] = jnp.zeros_like(dmod_ref)

        g = (i * tm) // s
        gate = mod_ref[g, pl.ds(5, 1), :]
        dx = dx_ref[...]
        dmod_ref[g, pl.ds(5, 1), :] += jnp.sum(dx * mix_ref[...].astype(F32), axis=0, keepdims=True)
        dmb = (gate * dx).astype(BF16)
        dmix_ref[...] = dmb
        do_ref[...] = _dot_nt(dmb, w_ref[0:MLA_W, :]).astype(BF16)
        dsg_ref[...] = _dot_nt(dmb, w_ref[MLA_W:MLA_W + GMLP_W, :]).astype(BF16)

    row = lambda i: (i, 0)
    return pl.pallas_call(
        body, name="mixout_bwd", grid=(t // tm,),
        in_specs=[pl.BlockSpec((tm, d), row), pl.BlockSpec((tm, d), row), _whole(mod.shape), _whole(wout.shape)],
        out_specs=[pl.BlockSpec((tm, d), row), pl.BlockSpec((tm, MLA_W), row), pl.BlockSpec((tm, GMLP_W), row),
                   pl.BlockSpec(mod.shape, lambda i: (0, 0, 0))],
        out_shape=[_sds((t, d), BF16), _sds((t, MLA_W), BF16), _sds((t, GMLP_W), BF16), _sds(mod.shape, F32)],
        compiler_params=_params(1),
    )(dx2, mix, mod, wout)


def _swap_cores(parts, name):
    n = len(parts)

    def body(*refs):
        srcs, outs, send_sems, recv_sems = refs[:n], refs[n:2 * n], refs[2 * n], refs[2 * n + 1]
        x, y, c = lax.axis_index("x"), lax.axis_index("y"), lax.axis_index("c")
        copies = [pltpu.make_async_remote_copy(
            src_ref=srcs[w], dst_ref=outs[w], send_sem=send_sems.at[w], recv_sem=recv_sems.at[w],
            device_id=(x, y, 1 - c), device_id_type=pl.DeviceIdType.MESH) for w in range(n)]
        for cp in copies:
            cp.start()
        for cp in copies:
            cp.wait()

    any_spec = pl.BlockSpec(memory_space=pl.ANY)
    return pl.pallas_call(
        body, name=name, in_specs=[any_spec] * n, out_specs=[any_spec] * n,
        out_shape=[_sds(p.shape, p.dtype) for p in parts],
        scratch_shapes=[pltpu.SemaphoreType.DMA((n,)), pltpu.SemaphoreType.DMA((n,))],
    )(*parts)


def _row_tile(r, c, mult):
    return _div_tile(r, max(mult, (1 << 18) // c), mult)


def _sum_slots(recv, name):
    _, r, c = recv.shape
    tr = _row_tile(r, c, 16)

    def body(r_ref, o_ref):
        f = lambda k: r_ref[k].astype(F32)
        o_ref[...] = ((f(0) + f(1)) + f(2)) + f(3)

    return pl.pallas_call(
        body, name=name, grid=(r // tr,),
        in_specs=[pl.BlockSpec((N_CHIPS, tr, c), lambda i: (0, i, 0))],
        out_specs=pl.BlockSpec((tr, c), lambda i: (i, 0)),
        out_shape=_sds((r, c), F32), compiler_params=_params(1),
    )(recv)


def _adamw(parts, w, m, v, name, exch=None, swap=None):
    r, wd = w.shape
    tr = _row_tile(r, wd, 8)
    c1 = 1.0 / (1.0 - ADAM_B1 ** ADAM_STEP)
    c2 = 1.0 / (1.0 - ADAM_B2 ** ADAM_STEP)
    n_p = len(parts)

    def body(*refs):
        p_refs = refs[:n_p]
        w_ref, m_ref, v_ref, g_ref, d_ref, nm_ref, nv_ref = refs[n_p:]
        g = p_refs[0][...]
        for p_ref in p_refs[1:]:
            g = g + p_ref[...]
        nm = ADAM_B1 * m_ref[...] + (1.0 - ADAM_B1) * g
        nv = ADAM_B2 * v_ref[...] + (1.0 - ADAM_B2) * (g * g)
        g_ref[...] = g
        nm_ref[...] = nm
        nv_ref[...] = nv
        d_ref[...] = -ADAM_LR * ((nm * c1) / (jnp.sqrt(nv * c2) + ADAM_EPS) + ADAM_WD * w_ref[...])

    spec = pl.BlockSpec((tr, wd), lambda i: (i, 0))
    return _hosted_call(body, name, (r // tr,), [spec] * (n_p + 3), [spec] * 4, [_sds((r, wd), F32)] * 4,
                        (*parts, w, m, v), exch=exch, swap=swap)


def _all_peers(x, y, c):
    flips = [(dx, dy, dc) for dx in (0, 1) for dy in (0, 1) for dc in (0, 1)][1:]
    return [(1 - x if dx else x, 1 - y if dy else y, 1 - c if dc else c) for dx, dy, dc in flips]


def _first_exchange(shards, later, cc, w, b):
    n_w, n_l = len(shards), len(later)
    n = w.shape[1]

    def body(*refs):
        src32, later_in, (cc_ref, w_ref, b_ref) = refs[:n_w], refs[n_w:n_w + n_l], refs[n_w + n_l:n_w + n_l + 3]
        o0 = n_w + n_l + 3
        outs, (all_ref, tab_ref), later_out = refs[o0:o0 + n_w], refs[o0 + n_w:o0 + n_w + 2], refs[o0 + n_w + 2:o0 + n_w + 2 + n_l]
        s0 = o0 + n_w + 2 + n_l
        srcs = refs[s0:s0 + n_w]
        (part_ref, ici_send, ici_recv, d2d_send, d2d_recv, local_sems, cc_send, cc_recv, tab_send,
         tab_recv) = refs[s0 + n_w:]
        for wi in range(n_w):
            srcs[wi][...] = src32[wi][...].astype(BF16)
        x, y, c = lax.axis_index("x"), lax.axis_index("y"), lax.axis_index("c")
        chip, dev = 2 * x + y, 4 * x + 2 * y + c
        chips = _other_chips(x, y)
        peers = _all_peers(x, y, c)

        def half(wi, which):
            hr = shards[wi].shape[0] // 2
            return pl.ds(pl.multiple_of(which * hr, 16), hr)

        def over_ici(wi, k, arriving):
            px, py = chips[k]
            slot = 2 * px + py if arriving else chip
            return pltpu.make_async_remote_copy(
                src_ref=srcs[wi].at[half(wi, c)], dst_ref=outs[wi].at[slot, half(wi, c)],
                send_sem=ici_send.at[3 * wi + k], recv_sem=ici_recv.at[3 * wi + k], device_id=(px, py, c),
                device_id_type=pl.DeviceIdType.MESH)

        def to_sibling(wi, k, arriving):
            px, py = chips[k]
            rows = half(wi, 1 - c if arriving else c)
            return pltpu.make_async_remote_copy(
                src_ref=outs[wi].at[2 * px + py, rows], dst_ref=outs[wi].at[2 * px + py, rows],
                send_sem=d2d_send.at[3 * wi + k], recv_sem=d2d_recv.at[3 * wi + k], device_id=(x, y, 1 - c),
                device_id_type=pl.DeviceIdType.MESH)

        def cc_copy(k, peer, slot):
            return pltpu.make_async_remote_copy(
                src_ref=cc_ref, dst_ref=all_ref.at[slot], send_sem=cc_send.at[k], recv_sem=cc_recv.at[k],
                device_id=peer, device_id_type=pl.DeviceIdType.MESH)

        def rows_of(px, py):
            return part_ref.at[pl.ds(pl.multiple_of((4 * px + 2 * py + c) * MOD_ROWS, MOD_ROWS), MOD_ROWS)]

        def tab_copy(k, px, py, slot):
            return pltpu.make_async_remote_copy(
                src_ref=rows_of(px, py), dst_ref=tab_ref.at[slot], send_sem=tab_send.at[k], recv_sem=tab_recv.at[k],
                device_id=(px, py, c), device_id_type=pl.DeviceIdType.MESH)

        local = [pltpu.make_async_copy(srcs[wi], outs[wi].at[chip], local_sems.at[wi]) for wi in range(n_w)]
        for cp in local:
            cp.start()
        pairs = [(wi, k) for wi in range(n_w) for k in range(3)]
        for wi, k in pairs:
            over_ici(wi, k, False).start()
        for k, peer in enumerate(peers):
            cc_copy(k, peer, dev).start()
        all_ref[dev] = cc_ref[...]
        for k, (px, py, pc) in enumerate(peers):
            cc_copy(k, (px, py, pc), 4 * px + 2 * py + pc).wait_recv()
        cv = all_ref[...].reshape(8 * MOD_ROWS, cc.shape[1])
        part_ref[...] = _dot((cv * _sigmoid(cv)).astype(BF16), w_ref[...]) + b_ref[...]
        for k, (px, py) in enumerate(chips):
            tab_copy(k, px, py, chip).start()
        tab_ref[chip] = rows_of(x, y)[...]
        for k, (px, py) in enumerate(chips):
            tab_copy(k, px, py, 2 * px + py).wait_recv()
        for j in range(n_l):
            later_out[j][...] = later_in[j][...].astype(BF16)
        for wi, k in pairs:
            over_ici(wi, k, True).wait_recv()
            to_sibling(wi, k, False).start()
        for wi, k in pairs:
            to_sibling(wi, k, True).wait_recv()
        for wi, k in pairs:
            over_ici(wi, k, False).wait_send()
            to_sibling(wi, k, False).wait_send()
        for k, peer in enumerate(peers):
            cc_copy(k, peer, dev).wait_send()
        for k, (px, py) in enumerate(chips):
            tab_copy(k, px, py, chip).wait_send()
        for cp in local:
            cp.wait()

    any_spec = pl.BlockSpec(memory_space=pl.ANY)
    vmem = pl.BlockSpec(memory_space=pltpu.VMEM)
    sems3 = pltpu.SemaphoreType.DMA((3 * n_w,))
    got = pl.pallas_call(
        body, name="first_exchange", in_specs=[vmem] * (n_w + n_l + 3),
        out_specs=[any_spec] * n_w + [vmem] * (2 + n_l),
        out_shape=[_sds((N_CHIPS,) + a.shape, BF16) for a in shards]
        + [_sds((8,) + cc.shape, F32), _sds((N_CHIPS, MOD_ROWS, n), F32)] + [_sds(a.shape, BF16) for a in later],
        scratch_shapes=[pltpu.VMEM(a.shape, BF16) for a in shards]
        + [pltpu.VMEM((8 * MOD_ROWS, n), F32), sems3, sems3, sems3, sems3, pltpu.SemaphoreType.DMA((n_w,)),
           pltpu.SemaphoreType.DMA((7,)), pltpu.SemaphoreType.DMA((7,)), pltpu.SemaphoreType.DMA((3,)),
           pltpu.SemaphoreType.DMA((3,))],
        compiler_params=pltpu.CompilerParams(vmem_limit_bytes=V7X_VMEM_LIMIT),
    )(*shards, *later, cc, w, b)
    return got[:n_w], got[n_w], got[n_w + 1], got[n_w + 2:]


def _ada_bwd_tp(cc_all, dmods, w, ctx_row):
    d, n = w.shape

    def body(cc_ref, m0, m1, m2, m3, w_ref, dw_ref, db_ref, dctx_ref, stage_ref, all_ref, send_sems, recv_sems):
        x, y, c = lax.axis_index("x"), lax.axis_index("y"), lax.axis_index("c")
        me = 4 * x + 2 * y + c
        dsum = m0[...] + m1[...] + m2[...] + m3[...]
        db_ref[...] = jnp.sum(dsum, axis=0, keepdims=True)
        for j in range(N_CHIPS):
            stage_ref[j] = dsum[:, j * n:(j + 1) * n]

        def copy(k, peer, slot):
            px, py, _ = peer
            return pltpu.make_async_remote_copy(
                src_ref=stage_ref.at[2 * px + py], dst_ref=all_ref.at[slot], send_sem=send_sems.at[k],
                recv_sem=recv_sems.at[k], device_id=peer, device_id_type=pl.DeviceIdType.MESH)

        peers = _all_peers(x, y, c)
        for k, peer in enumerate(peers):
            copy(k, peer, me).start()
        all_ref[me] = stage_ref[2 * x + y]
        for k, (px, py, pc) in enumerate(peers):
            copy(k, (px, py, pc), 4 * px + 2 * py + pc).wait_recv()
        for k, peer in enumerate(peers):
            copy(k, peer, me).wait_send()
        cv = cc_ref[...]
        sig = _sigmoid(cv)
        dmb = all_ref[...].reshape(8 * MOD_ROWS, n).astype(BF16)
        dw_ref[...] = _dot_tn((cv * sig).astype(BF16), dmb)
        dsc = _dot_nt(dmb, w_ref[...])
        dctx = dsc[ctx_row:ctx_row + 1, :]
        for dev in range(1, 8):
            dctx = dctx + dsc[dev * MOD_ROWS + ctx_row:dev * MOD_ROWS + ctx_row + 1, :]
        cx = cv[ctx_row:ctx_row + 1, :]
        sx = sig[ctx_row:ctx_row + 1, :]
        dctx_ref[...] = dctx * (sx * (1.0 + cx * (1.0 - sx))) * jnp.where(c == 0, 1.0, 0.0)

    vmem = pl.BlockSpec(memory_space=pltpu.VMEM)
    return pl.pallas_call(
        body, name="ada_bwd_tp", in_specs=[vmem] * 6, out_specs=[vmem] * 3,
        out_shape=[_sds((d, n), F32), _sds((1, N_MOD * d), F32), _sds((1, d), F32)],
        scratch_shapes=[pltpu.VMEM((N_CHIPS, MOD_ROWS, n), F32), pltpu.VMEM((8, MOD_ROWS, n), F32),
                        pltpu.SemaphoreType.DMA((7,)), pltpu.SemaphoreType.DMA((7,))],
        compiler_params=pltpu.CompilerParams(vmem_limit_bytes=V7X_VMEM_LIMIT),
    )(cc_all, *dmods, w)


def _rope_tables(s, ctx):
    pos = np.arange(s, dtype=np.float32)
    inv = (np.float32(ROPE_BASE) ** (-np.arange(0, QK_ROPE // 2, 2, dtype=np.float32) / np.float32(QK_ROPE // 2)))
    ang_r = np.floor(pos / GRID_W)[:, None] * inv
    ang_c = (pos - GRID_W * np.floor(pos / GRID_W))[:, None] * inv
    ang = np.concatenate([ang_r, ang_r, ang_c, ang_c], axis=-1).astype(np.float32)
    cos, sin = np.cos(ang), np.sin(ang)
    half_b = (np.arange(QK_ROPE) // 8) % 2 == 1
    sin_a = np.where(half_b, sin, 0.0)
    sin_b = np.where(half_b, 0.0, -sin)

    def place(tab, fill):
        full = np.full((s + ctx, HEAD_PAD), fill, np.float32)
        full[:s, QK_NOPE:QK_HEAD] = tab
        return jnp.asarray(full)

    return place(cos, 1.0), place(sin_a, 0.0), place(sin_b, 0.0)


def _pad_last(a, n):
    return jnp.pad(a, [(0, 0)] * (a.ndim - 1) + [(0, n - a.shape[-1])])


def _flat_rows(parts, rows, width):
    flat = jnp.concatenate([p.reshape(-1) for p in parts])
    return jnp.pad(flat, (0, rows * width - flat.shape[0])).reshape(rows, width)


def kernel(x, c, ctx, c_ctx, w_ada, b_ada, norm1_w, ffn1_w1, ffn1_w3, ffn1_w2, norm2_w, w_in, q_a_norm_w, w_uq, kv_a_norm_w, w_ukv, q_norm_w, k_norm_w, v_norm_w, w_s, b_s, w_out, norm3_w, ffn2_w1, ffn2_w3, ffn2_w2, loss_target, m_c_ctx, m_w_ada, m_b_ada, m_norm1_w, m_ffn1_w1, m_ffn1_w3, m_ffn1_w2, m_norm2_w, m_w_in, m_q_a_norm_w, m_w_uq, m_kv_a_norm_w, m_w_ukv, m_q_norm_w, m_k_norm_w, m_v_norm_w, m_w_s, m_b_s, m_w_out, m_norm3_w, m_ffn2_w1, m_ffn2_w3, m_ffn2_w2, v_c_ctx, v_w_ada, v_b_ada, v_norm1_w, v_ffn1_w1, v_ffn1_w3, v_ffn1_w2, v_norm2_w, v_w_in, v_q_a_norm_w, v_w_uq, v_kv_a_norm_w, v_w_ukv, v_q_norm_w, v_k_norm_w, v_v_norm_w, v_w_s, v_b_s, v_w_out, v_norm3_w, v_ffn2_w1, v_ffn2_w3, v_ffn2_w2):
    wts = dict(c_ctx=c_ctx, w_ada=w_ada, b_ada=b_ada, norm1_w=norm1_w, ffn1_w1=ffn1_w1, ffn1_w3=ffn1_w3, ffn1_w2=ffn1_w2,
               norm2_w=norm2_w, w_in=w_in, q_a_norm_w=q_a_norm_w, w_uq=w_uq, kv_a_norm_w=kv_a_norm_w, w_ukv=w_ukv,
               q_norm_w=q_norm_w, k_norm_w=k_norm_w, v_norm_w=v_norm_w, w_s=w_s, b_s=b_s, w_out=w_out, norm3_w=norm3_w,
               ffn2_w1=ffn2_w1, ffn2_w3=ffn2_w3, ffn2_w2=ffn2_w2)
    moms = dict(c_ctx=m_c_ctx, w_ada=m_w_ada, b_ada=m_b_ada, norm1_w=m_norm1_w, ffn1_w1=m_ffn1_w1, ffn1_w3=m_ffn1_w3,
                ffn1_w2=m_ffn1_w2, norm2_w=m_norm2_w, w_in=m_w_in, q_a_norm_w=m_q_a_norm_w, w_uq=m_w_uq,
                kv_a_norm_w=m_kv_a_norm_w, w_ukv=m_w_ukv, q_norm_w=m_q_norm_w, k_norm_w=m_k_norm_w, v_norm_w=m_v_norm_w,
                w_s=m_w_s, b_s=m_b_s, w_out=m_w_out, norm3_w=m_norm3_w, ffn2_w1=m_ffn2_w1, ffn2_w3=m_ffn2_w3,
                ffn2_w2=m_ffn2_w2)
    vars_ = dict(c_ctx=v_c_ctx, w_ada=v_w_ada, b_ada=v_b_ada, norm1_w=v_norm1_w, ffn1_w1=v_ffn1_w1, ffn1_w3=v_ffn1_w3,
                 ffn1_w2=v_ffn1_w2, norm2_w=v_norm2_w, w_in=v_w_in, q_a_norm_w=v_q_a_norm_w, w_uq=v_w_uq,
                 kv_a_norm_w=v_kv_a_norm_w, w_ukv=v_w_ukv, q_norm_w=v_q_norm_w, k_norm_w=v_k_norm_w, v_norm_w=v_v_norm_w,
                 w_s=v_w_s, b_s=v_b_s, w_out=v_w_out, norm3_w=v_norm3_w, ffn2_w1=v_ffn2_w1, ffn2_w3=v_ffn2_w3,
                 ffn2_w2=v_ffn2_w2)

    nb, s, d = x.shape
    nctx = ctx.shape[1]
    t, tc = nb * s, nb * nctx
    t_all = t + tc
    sk = s + nctx
    assert nb + 1 <= MOD_ROWS and d % LANES == 0
    tm = _token_tile(s, nctx)
    tq = _div_tile(s, 512, tm)
    tmx = _div_tile(math.gcd(s, tc), 1024, tm)
    tmo = _div_tile(s, 1024, tm)

    def held(n, a_):
        return jnp.swapaxes(a_[0], 0, 1) if n in T_WEIGHTS else a_[0]

    def unheld(n, a_):
        return (jnp.swapaxes(a_, 0, 1) if n in T_WEIGHTS else a_)[None]

    shard = {"w_ada": w_ada[0].astype(BF16)}
    full = {}

    def unshard(names, blocks):
        for n, g4 in zip(names, blocks):
            _, r_, c_ = g4.shape
            if n in ROW_SHARDED or n in T_WEIGHTS:
                full[n] = g4.reshape(N_CHIPS * r_, c_)
            else:
                full[n] = g4.transpose(1, 0, 2).reshape(r_, N_CHIPS * c_)

    def chip_major(n, g_):
        if n in ROW_SHARDED or n in T_WEIGHTS:
            return g_.reshape(N_CHIPS, g_.shape[0] // N_CHIPS, g_.shape[1]).astype(BF16)
        r_, cols = g_.shape
        return g_.reshape(r_, N_CHIPS, cols // N_CHIPS).transpose(1, 0, 2).astype(BF16)

    cc = jnp.concatenate([c, c_ctx[None, :], jnp.zeros((MOD_ROWS - nb - 1, d), F32)], axis=0)
    n_ada = shard["w_ada"].shape[1]
    assert n_ada % LANES == 0
    my_chip = 2 * lax.axis_index("x") + lax.axis_index("y")
    b_cols = lax.dynamic_slice_in_dim(b_ada, my_chip * n_ada, n_ada, axis=1)
    later = MIX_WEIGHTS + LAST_WEIGHTS
    got, cc_all, table, cast = _first_exchange([held(n, wts[n]) for n in FIRST_WEIGHTS],
                                               [held(n, wts[n]) for n in later], cc, shard["w_ada"], b_cols)
    unshard(FIRST_WEIGHTS, got)
    shard.update(zip(later, cast))
    cc_all = cc_all.reshape(8 * MOD_ROWS, d)
    mod = table.transpose(1, 0, 2).reshape(MOD_ROWS, N_MOD, d)
    wsb = w_s[0].astype(BF16)
    wcat = wsb.transpose(1, 0, 2).reshape(CHUNK, GROUPS * CHUNK)
    wcat_t = wsb.transpose(2, 0, 1).reshape(CHUNK, GROUPS * CHUNK)
    bias = jnp.repeat(b_s[0].T, GROUP_DIM, axis=1)
    vnw = v_norm_w.reshape(1, GMLP_W)
    lane = jnp.arange(GMLP_W)
    ones = (lane[:, None] // GROUP_DIM == lane[None, :] // GROUP_DIM).astype(BF16)
    qnw = _pad_last(q_norm_w, HEAD_PAD)
    knw = _pad_last(k_norm_w, HEAD_PAD)
    tabs = _rope_tables(s, nctx)

    x_lat, x_ctx = x.reshape(t, d), ctx.reshape(tc, d)
    (xs1, a1, b1, y1), got = _ffn_fwd(x_lat, x_ctx, mod, norm1_w, full["ffn1_w1"], full["ffn1_w3"], full["ffn1_w2"], 0, s,
                                      nb, tm, "ffn1_fwd", exch=("gather", [shard[n] for n in MIX_WEIGHTS]))
    unshard(MIX_WEIGHTS, got)
    wi = full["w_in"]
    wp = jnp.concatenate([wi[0:KV_LORA], jnp.zeros((QK_NOPE, d), BF16), wi[KV_LORA:KV_LORA + QK_ROPE],
                          jnp.zeros((HEAD_PAD - QK_HEAD, d), BF16), wi[KV_LORA + QK_ROPE:]], axis=0)
    wq = jnp.pad(full["w_uq"].reshape(HEADS, QK_HEAD, Q_LORA), ((0, 0), (0, HEAD_PAD - QK_HEAD), (0, 0)))
    wkv = full["w_ukv"].reshape(KV_LORA, HEADS, QK_NOPE + V_HEAD)
    wk = _pad_last(wkv[:, :, :QK_NOPE].transpose(1, 0, 2), HEAD_PAD)
    wv = wkv[:, :, QK_NOPE:].reshape(KV_LORA, HEADS // 2, 2 * V_HEAD).transpose(1, 0, 2)
    h2, proj = _mixin_fwd(xs1, mod, norm2_w, wp, s, nb, tmx)
    prep_w = (wq, wk, wv, kv_a_norm_w, q_a_norm_w, qnw, knw)
    q, k_all, v_all = _prep_fwd(proj, 0, nb, s, 0, sk, 0, None, tabs, *prep_w, tmo, True, "prep_fwd")
    k_all, v_all = _prep_fwd(proj, t // tm, nb, nctx, s // tm, sk, s // tm, (k_all, v_all), tabs, *prep_w, tm, False,
                             "prep_ctx_fwd")
    o, lse, got = _attn_fwd(q, k_all, v_all, _div_tile(s, 2048, tm), exch=("gather", [shard[n] for n in LAST_WEIGHTS]))
    unshard(LAST_WEIGHTS, got)
    mixcat = _gmlp_fwd(proj, o, wcat, bias, vnw, ones, tmo)
    x2, mix = _mixout_fwd(mixcat, xs1, mod, full["w_out"], s, tmo)
    (dy, a2, b2, y2, loss_part), _ = _ffn_fwd(x2, None, mod, norm3_w, full["ffn2_w1"], full["ffn2_w3"], full["ffn2_w2"], 6,
                                              s, nb, tm, "ffn2_fwd", target=loss_target.reshape(t, d))

    grads, cm, recv = {}, {}, {}

    def scatter_of(names):
        return ("scatter", [cm[n] for n in names])

    (dx2, h3, g2, da2, db2, dyb2, dmod_c, grads["norm3_w"]), _ = _ffn_bwd(
        dy, x2, None, a2, b2, y2, mod, norm3_w, full["ffn2_w1"], full["ffn2_w3"], full["ffn2_w2"], 6, s, nb, tm,
        "ffn2_bwd")
    cm["ffn2_w1"] = chip_major("ffn2_w1", _mm_tn(da2, h3, t, "ffn2_dw1"))
    cm["ffn2_w3"] = chip_major("ffn2_w3", _mm_tn(db2, h3, t, "ffn2_dw3"))
    cm["ffn2_w2"] = chip_major("ffn2_w2", _mm_tn(g2, dyb2, t, "ffn2_dw2"))
    dmix, do, dsg, dmod_b = _mixout_bwd(dx2, mix, mod, full["w_out"], s, tmo)
    cm["w_out"] = chip_major("w_out", _mm_tn(mixcat, dmix, t, "wout_dw"))
    duv, dws, dbs, dvnw = _gmlp_bwd(proj, dsg, wcat, wcat_t, bias, vnw, ones, tmo)
    group = LAST_WEIGHTS + ("w_out",)
    (dq, dk, dv), got = _attn_bwd(q, k_all, v_all, do, mixcat, lse, tmo, exch=scatter_of(group))
    recv.update(zip(group, got))
    dp0, dwk_c, dwv_c, dkvaw_c, dknw_c = _prep_bwd(
        proj, t // tm, nb, nctx, s // tm, s // tm, t_all, None, tabs, *prep_w, None, dk, dv, None, tm, "prep_ctx_bwd")
    dp0, dwq, dqaw, dqnw, dwk, dwv, dkvaw, dknw = _prep_bwd(
        proj, 0, nb, s, 0, 0, t_all, dp0, tabs, *prep_w, dq, dk, dv, [dwk_c, dwv_c, dkvaw_c, dknw_c], tq, "prep_bwd")
    part, sib = {}, {}
    early = LAST_WEIGHTS + ("w_out",)
    for n in early:
        part[n] = _sum_slots(recv[n], "sum_" + n)
    (dxs1, dmod_a, grads["norm2_w"]), _, got = _mixin_bwd(dp0, duv, xs1, dx2, mod, norm2_w, wp, s, nb, tmx,
                                                          [part[n] for n in early])
    sib.update(zip(early, got))
    dwp = jnp.concatenate([_mm_tn(dp0, h2, t_all, "win_dw_kvq"), _mm_tn(duv, h2, t, "win_dw_uv")], axis=0)
    cm["w_in"] = chip_major("w_in", jnp.concatenate(
        [dwp[0:KV_LORA], dwp[KV_LORA + QK_NOPE:KV_LORA + QK_HEAD], dwp[256:]], axis=0))
    cm["w_uq"] = chip_major("w_uq", dwq[:, :, :QK_HEAD].transpose(0, 2, 1).reshape(HEADS * QK_HEAD, Q_LORA))
    cm["w_ukv"] = chip_major("w_ukv", jnp.concatenate(
        [dwk[:, :, :QK_NOPE].transpose(1, 0, 2),
         dwv.transpose(1, 0, 2).reshape(KV_LORA, HEADS, V_HEAD)], axis=2).reshape(KV_LORA, HEADS * (QK_NOPE + V_HEAD)))
    (dx_lat, h1, g1, da1, db1, dyb1, dmod_0, grads["norm1_w"]), _ = _ffn_bwd(
        dxs1, x_lat, x_ctx, a1, b1, y1, mod, norm1_w, full["ffn1_w1"], full["ffn1_w3"], full["ffn1_w2"], 0, s, nb, tm,
        "ffn1_bwd")
    dmods = [m_.reshape(MOD_ROWS, N_MOD * d) for m_ in (dmod_0, dmod_a, dmod_b, dmod_c)]
    dw_ada, grads["b_ada"], dctx = _ada_bwd_tp(cc_all, dmods, shard["w_ada"], nb)
    grads["c_ctx"] = dctx[0]
    grads["q_a_norm_w"], grads["kv_a_norm_w"] = dqaw, dkvaw
    grads["q_norm_w"], grads["k_norm_w"] = dqnw[:, :QK_HEAD], dknw[:, :QK_HEAD]
    grads["v_norm_w"], grads["w_s"], grads["b_s"] = dvnw, dws, dbs[:, 0]
    grad_x = dx_lat.reshape(nb, s, d)
    n_small = sum(wts[n].size for n in SMALL)
    rows_s = _round_up(-(-(n_small + 1) // d), 16)
    cm["small"] = jnp.broadcast_to(_flat_rows([grads[n] for n in SMALL] + [loss_part], rows_s, d), (N_CHIPS, rows_s, d))
    group = ("w_in", "w_uq", "w_ukv", "small")
    dw2, got = _mm_tn(g1, dyb1, t_all, "ffn1_dw2", exch=scatter_of(group))
    recv.update(zip(group, got))
    cm["ffn1_w2"] = chip_major("ffn1_w2", dw2)
    dw1, got = _mm_tn(da1, h1, t_all, "ffn1_dw1", exch=scatter_of(("ffn1_w2",)))
    recv["ffn1_w2"] = got[0]
    cm["ffn1_w1"] = chip_major("ffn1_w1", dw1)
    dw3, got = _mm_tn(db1, h1, t_all, "ffn1_dw3", exch=scatter_of(("ffn1_w1",)))
    recv["ffn1_w1"] = got[0]
    cm["ffn1_w3"] = chip_major("ffn1_w3", dw3)
    stepped = {}
    reduced = tuple(n for n in SHARDED if n != "w_ada") + ("small",)
    late = tuple(n for n in reduced if n not in early and n != "ffn1_w3")
    for n in late:
        part[n] = _sum_slots(recv[n], "sum_" + n)
    stepped["w_ada"], got, got_sib = _adamw([dw_ada], wts["w_ada"][0], moms["w_ada"][0], vars_["w_ada"][0],
                                            "adamw_w_ada", exch=scatter_of(("ffn1_w3",)), swap=[part[n] for n in late])
    sib.update(zip(late, got_sib))
    part["ffn1_w3"] = _sum_slots(got[0], "sum_ffn1_w3")
    sib["ffn1_w3"] = _swap_cores([part["ffn1_w3"]], "swap_last")[0]
    for n in reduced[:-1]:
        stepped[n], _ = _adamw([part[n], sib[n]], held(n, wts[n]), held(n, moms[n]), held(n, vars_[n]), "adamw_" + n)
    for n in SHARDED:
        stepped[n] = [unheld(n, a_) for a_ in stepped[n]]
    packed, _ = _adamw([part["small"], sib["small"]], _flat_rows([wts[n] for n in SMALL], rows_s, d),
                       _flat_rows([moms[n] for n in SMALL], rows_s, d), _flat_rows([vars_[n] for n in SMALL], rows_s, d),
                       "adamw_small")
    loss = packed[0].reshape(-1)[n_small]
    for n in SMALL:
        stepped[n] = []
    for a_ in packed:
        flat = a_.reshape(-1)
        off = 0
        for n in SMALL:
            stepped[n].append(flat[off:off + wts[n].size].reshape(wts[n].shape))
            off += wts[n].size
    return (loss, grad_x, *[stepped[n][0] for n in WEIGHTS], *[stepped[n][1] for n in WEIGHTS],
            *[stepped[n][2] for n in WEIGHTS], *[stepped[n][3] for n in WEIGHTS])
```

```python
import functools
import math

import jax
import jax.numpy as jnp
import numpy as np
from jax import lax
from jax.experimental import pallas as pl
from jax.experimental.pallas import tpu as pltpu

F32 = jnp.float32
BF16 = jnp.bfloat16

EPS = 1e-6
N_MOD = 9
HEADS = 8
QK_NOPE, QK_ROPE, V_HEAD = 64, 32, 64
QK_HEAD = QK_NOPE + QK_ROPE
HEAD_PAD = 128
LN2 = math.log(2.0)
SOFTMAX_SCALE = QK_HEAD ** -0.5 / LN2
Q_LORA, KV_LORA = 256, 128
GROUPS, GROUP_DIM, CHUNK = 8, 64, 128
GMLP_W = GROUPS * GROUP_DIM
MLA_W = HEADS * V_HEAD
IN_COLS = 1440
PROJ_COLS = 1536
GRID_W = 64
ROPE_BASE = 10000.0
MOD_ROWS = 16
ADAM_LR, ADAM_B1, ADAM_B2, ADAM_EPS, ADAM_WD, ADAM_STEP = 0.001, 0.9, 0.999, 1e-08, 0.01, 10
N_CHIPS = 4
LANES = 128
V7X_VMEM_LIMIT = 56 * 1024 * 1024
GELU_C = math.sqrt(2.0 / math.pi)

SHARDED = ("w_ada", "ffn1_w1", "ffn1_w3", "ffn1_w2", "w_in", "w_uq", "w_ukv", "w_out", "ffn2_w1", "ffn2_w3", "ffn2_w2")
ROW_SHARDED = ("ffn1_w2", "w_out", "ffn2_w2")
T_WEIGHTS = ("ffn1_w1", "ffn1_w3", "ffn2_w1", "ffn2_w3", "w_in", "w_uq")
FIRST_WEIGHTS = ("ffn1_w1", "ffn1_w3", "ffn1_w2")
MIX_WEIGHTS = ("w_in", "w_uq", "w_ukv", "w_out")
LAST_WEIGHTS = ("ffn2_w1", "ffn2_w3", "ffn2_w2")
SMALL = ("c_ctx", "b_ada", "norm1_w", "norm2_w", "q_a_norm_w", "kv_a_norm_w", "q_norm_w", "k_norm_w", "v_norm_w",
         "w_s", "b_s", "norm3_w")
WEIGHTS = ("c_ctx", "w_ada", "b_ada", "norm1_w", "ffn1_w1", "ffn1_w3", "ffn1_w2", "norm2_w", "w_in", "q_a_norm_w",
           "w_uq", "kv_a_norm_w", "w_ukv", "q_norm_w", "k_norm_w", "v_norm_w", "w_s", "b_s", "w_out", "norm3_w",
           "ffn2_w1", "ffn2_w3", "ffn2_w2")


def _round_up(n, m):
    return (n + m - 1) // m * m


def _div_tile(n, target, mult):
    best = None
    for t in range(mult, min(n, target) + 1, mult):
        if n % t == 0:
            best = t
    return n if best is None else best


def _dot(a, b):
    return lax.dot_general(a, b, (((1,), (0,)), ((), ())), preferred_element_type=F32)


def _dot_nt(a, b):
    return lax.dot_general(a, b, (((1,), (1,)), ((), ())), preferred_element_type=F32)


def _dot_tn(a, b):
    return lax.dot_general(a, b, (((0,), (0,)), ((), ())), preferred_element_type=F32)


def _sigmoid(x):
    return 1.0 / (1.0 + jnp.exp(-x))


def _gelu(x):
    return 0.5 * x * (1.0 + jnp.tanh(GELU_C * (x + 0.044715 * x * x * x)))


def _gelu_grad(x):
    t = jnp.tanh(GELU_C * (x + 0.044715 * x * x * x))
    return 0.5 * (1.0 + t) + 0.5 * x * (1.0 - t * t) * (GELU_C * (1.0 + 3 * 0.044715 * x * x))


def _rope3(x, cos, sin_a, sin_b):
    return x * cos + pltpu.roll(x, 8, 2) * sin_a + pltpu.roll(x, HEAD_PAD - 8, 2) * sin_b


def _rope3_t(d, cos, sin_a, sin_b):
    return d * cos + pltpu.roll(d * sin_a, HEAD_PAD - 8, 2) + pltpu.roll(d * sin_b, 8, 2)


def _group_sum(x, ones_ref):
    hi = x.astype(BF16)
    lo = (x - hi.astype(F32)).astype(BF16)
    return _dot(hi, ones_ref[...]) + _dot(lo, ones_ref[...])


def _params(n_axes):
    return pltpu.CompilerParams(dimension_semantics=("arbitrary",) * n_axes, vmem_limit_bytes=V7X_VMEM_LIMIT)


def _whole(shape):
    nd = len(shape)
    return pl.BlockSpec(shape, lambda *_: (0,) * nd, pipeline_mode=pl.Buffered(1))


def _sds(shape, dtype):
    return jax.ShapeDtypeStruct(shape, dtype)


def _token_tile(s, ctx):
    return _div_tile(math.gcd(s, ctx), 256, CHUNK)


def _other_chips(x, y):
    return [(1 - x, y), (x, 1 - y), (1 - x, 1 - y)]


def _exch_copies(kind, srcs, dsts, send_sems, recv_sems, local_sems, with_arrivals):
    x, y, c = lax.axis_index("x"), lax.axis_index("y"), lax.axis_index("c")
    me = 2 * x + y
    local, sends, arrivals = [], [], []
    for w, (src, dst) in enumerate(zip(srcs, dsts)):
        own = src if kind == "gather" else src.at[me]
        local.append(pltpu.make_async_copy(own, dst.at[me], local_sems.at[w]))
        for k, (px, py) in enumerate(_other_chips(x, y)):
            sem = dict(send_sem=send_sems.at[3 * w + k], recv_sem=recv_sems.at[3 * w + k], device_id=(px, py, c),
                       device_id_type=pl.DeviceIdType.MESH)
            out = src if kind == "gather" else src.at[2 * px + py]
            sends.append(pltpu.make_async_remote_copy(src_ref=out, dst_ref=dst.at[me], **sem))
            if with_arrivals:
                arrivals.append(pltpu.make_async_remote_copy(src_ref=own, dst_ref=dst.at[2 * px + py], **sem))
    return local, sends, arrivals


def _exch_start(kind, srcs, dsts, sems):
    local, sends, _ = _exch_copies(kind, srcs, dsts, *sems, with_arrivals=False)
    for cp in local + sends:
        cp.start()


def _exch_wait(kind, srcs, dsts, sems):
    local, sends, arrivals = _exch_copies(kind, srcs, dsts, *sems, with_arrivals=True)
    for cp in arrivals:
        cp.wait_recv()
    for cp in sends:
        cp.wait_send()
    for cp in local:
        cp.wait()


def _exch_scratch(n):
    return [pltpu.SemaphoreType.DMA((3 * n,)), pltpu.SemaphoreType.DMA((3 * n,)), pltpu.SemaphoreType.DMA((n,))]


def _exch_shapes(kind, arrays):
    return [_sds((N_CHIPS,) + a.shape if kind == "gather" else a.shape, a.dtype) for a in arrays]


def _sibling_copies(srcs, dsts, send_sems, recv_sems):
    x, y, c = lax.axis_index("x"), lax.axis_index("y"), lax.axis_index("c")
    return [pltpu.make_async_remote_copy(
        src_ref=src, dst_ref=dst, send_sem=send_sems.at[w], recv_sem=recv_sems.at[w], device_id=(x, y, 1 - c),
        device_id_type=pl.DeviceIdType.MESH) for w, (src, dst) in enumerate(zip(srcs, dsts))]


def _hosted_call(body, name, grid, in_specs, out_specs, out_shape, operands, scratch=(), exch=None, swap=None):
    n_axes = len(grid)
    if exch is None and swap is None:
        outs = pl.pallas_call(body, name=name, grid=grid, in_specs=list(in_specs), out_specs=list(out_specs),
                              out_shape=list(out_shape), scratch_shapes=list(scratch),
                              compiler_params=_params(n_axes))(*operands)
        return list(outs), []
    kind, arrays = exch if exch is not None else ("scatter", [])
    swaps = list(swap or [])
    n_in, n_out, n_sc, n_ex, n_sw = len(in_specs), len(out_specs), len(scratch), len(arrays), len(swaps)

    def hosted(*refs):
        cin, ein, sin = refs[:n_in], refs[n_in:n_in + n_ex], refs[n_in + n_ex:n_in + n_ex + n_sw]
        o0 = n_in + n_ex + n_sw
        cout, eout, sout = refs[o0:o0 + n_out], refs[o0 + n_out:o0 + n_out + n_ex], refs[o0 + n_out + n_ex:o0 + n_out + n_ex + n_sw]
        rest = refs[o0 + n_out + n_ex + n_sw:]
        csc, sems, swap_sems = rest[:n_sc], rest[n_sc:n_sc + 3], rest[n_sc + 3:]
        first = functools.reduce(jnp.logical_and, [pl.program_id(a) == 0 for a in range(n_axes)])
        last = functools.reduce(jnp.logical_and, [pl.program_id(a) == grid[a] - 1 for a in range(n_axes)])

        @pl.when(first)
        def _():
            if n_ex:
                _exch_start(kind, ein, eout, sems)
            for cp in _sibling_copies(sin, sout, *swap_sems) if n_sw else []:
                cp.start()

        body(*cin, *cout, *csc)

        @pl.when(last)
        def _():
            if n_ex:
                _exch_wait(kind, ein, eout, sems)
            for cp in _sibling_copies(sin, sout, *swap_sems) if n_sw else []:
                cp.wait()

    any_spec = pl.BlockSpec(memory_space=pl.ANY)
    swap_scratch = [pltpu.SemaphoreType.DMA((n_sw,)), pltpu.SemaphoreType.DMA((n_sw,))] if n_sw else []
    outs = pl.pallas_call(
        hosted, name=name, grid=grid, in_specs=list(in_specs) + [any_spec] * (n_ex + n_sw),
        out_specs=list(out_specs) + [any_spec] * (n_ex + n_sw),
        out_shape=list(out_shape) + _exch_shapes(kind, arrays) + [_sds(a.shape, a.dtype) for a in swaps],
        scratch_shapes=list(scratch) + _exch_scratch(max(n_ex, 1)) + swap_scratch, compiler_params=_params(n_axes),
    )(*operands, *arrays, *swaps)
    got = list(outs[n_out:n_out + n_ex])
    return (list(outs[:n_out]), got) if swap is None else (list(outs[:n_out]), got, list(outs[n_out + n_ex:]))


class _TokenTiles:
    def __init__(self, t, tc, tm):
        self.n_lat, self.n_ctx = t // tm, tc // tm
        self.n_all = self.n_lat + self.n_ctx

    def tile(self, i):
        return (i + self.n_lat) % self.n_all if self.n_ctx else i

    def is_lat(self, i):
        return self.tile(i) < self.n_lat

    def row(self, i):
        return (self.tile(i), 0)

    def lat_row(self, i):
        return (jnp.where(self.is_lat(i), self.tile(i), 0), 0) if self.n_ctx else (i, 0)

    def ctx_row(self, i):
        return (jnp.where(self.is_lat(i), self.n_ctx - 1, self.tile(i) - self.n_lat), 0)


def _ffn_fwd(x_lat, x_ctx, mod, nw, w1, w3, w2, k0, s, nb, tm, name, target=None, exch=None):
    t, d = x_lat.shape
    tc = 0 if x_ctx is None else x_ctx.shape[0]
    f = w1.shape[0]
    tiles = _TokenTiles(t, tc, tm)
    n_x = 2 if tc else 1
    n_t = 0 if target is None else 1
    assert not (tc and n_t)

    def body(*refs):
        x_ref = refs[0]
        t_ref = refs[n_x] if n_t else None
        mod_ref, nw_ref, w1_ref, w3_ref, w2_ref, o_ref, a_ref, b_ref, y_ref = refs[n_x + n_t:n_x + n_t + 9]
        i = pl.program_id(0)
        g = jnp.minimum((tiles.tile(i) * tm) // s, nb)
        shift = mod_ref[g, pl.ds(k0, 1), :]
        scale = mod_ref[g, pl.ds(k0 + 1, 1), :]
        gate = mod_ref[g, pl.ds(k0 + 2, 1), :]
        x = jnp.where(tiles.is_lat(i), x_ref[...], refs[1][...]) if tc else x_ref[...]
        r = lax.rsqrt(jnp.mean(x * x, axis=-1, keepdims=True) + EPS)
        hb = ((x * r * nw_ref[...]) * (1.0 + scale) + shift).astype(BF16)
        a = _dot_nt(hb, w1_ref[...])
        b = _dot_nt(hb, w3_ref[...])
        gb = (a * _sigmoid(a) * b).astype(BF16)
        y = _dot(gb, w2_ref[...])
        out = x + (0.5 * gate) * y
        a_ref[...] = a.astype(BF16)
        b_ref[...] = b.astype(BF16)
        y_ref[...] = y.astype(BF16)
        if n_t:
            loss_ref, acc_ref = refs[-2:]

            @pl.when(i == 0)
            def _():
                acc_ref[...] = jnp.zeros_like(acc_ref)

            e = out - t_ref[...]
            o_ref[...] = e * (1.0 / d)
            acc_ref[...] += jnp.sum(e * e, axis=0, keepdims=True)

            @pl.when(i == tiles.n_all - 1)
            def _():
                loss_ref[...] = (0.5 / d) * jnp.sum(acc_ref[...], axis=-1, keepdims=True)
        else:
            o_ref[...] = out

    td = pl.BlockSpec((tm, d), tiles.row)
    tf = pl.BlockSpec((tm, f), tiles.row)
    return _hosted_call(
        body, name, (tiles.n_all,),
        [pl.BlockSpec((tm, d), tiles.lat_row)] + ([pl.BlockSpec((tm, d), tiles.ctx_row)] if tc else []) + [td] * n_t
        + [_whole(mod.shape), _whole(nw.shape), _whole(w1.shape), _whole(w3.shape), _whole(w2.shape)],
        [td, tf, tf, td] + [pl.BlockSpec((1, 1), lambda i: (0, 0))] * n_t,
        [_sds((t + tc, d), F32), _sds((t + tc, f), BF16), _sds((t + tc, f), BF16), _sds((t + tc, d), BF16)]
        + [_sds((1, 1), F32)] * n_t,
        (x_lat,) + ((x_ctx,) if tc else ()) + ((target,) if n_t else ()) + (mod, nw, w1, w3, w2),
        scratch=[pltpu.VMEM((1, d), F32)] * n_t, exch=exch)


def _ffn_bwd(dout, x_lat, x_ctx, a, b, y, mod, nw, w1, w3, w2, k0, s, nb, tm, name, exch=None):
    t, d = x_lat.shape
    tc = 0 if x_ctx is None else x_ctx.shape[0]
    f = w1.shape[0]
    nch = 2 if (f // 2) % LANES == 0 and f % 2 == 0 else 1
    fc = f // nch
    tiles = _TokenTiles(t, tc, tm)
    n_x = 2 if tc else 1

    def body(*refs):
        do_ref, x_ref = refs[0], refs[1]
        (a_ref, b_ref, y_ref, mod_ref, nw_ref, w1_ref, w3_ref, w2_ref,
         dx_ref, h_ref, g_ref, da_ref, db_ref, dy_ref, dmod_ref, dnw_ref) = refs[1 + n_x:]
        i = pl.program_id(0)

        @pl.when(i == 0)
        def _():
            dmod_ref[...] = jnp.zeros_like(dmod_ref)
            dnw_ref[...] = jnp.zeros_like(dnw_ref)

        g = jnp.minimum((tiles.tile(i) * tm) // s, nb)
        shift = mod_ref[g, pl.ds(k0, 1), :]
        scale = mod_ref[g, pl.ds(k0 + 1, 1), :]
        gate = mod_ref[g, pl.ds(k0 + 2, 1), :]
        x = jnp.where(tiles.is_lat(i), x_ref[...], refs[2][...]) if tc else x_ref[...]
        dout_v = do_ref[...]
        r = lax.rsqrt(jnp.mean(x * x, axis=-1, keepdims=True) + EPS)
        xh = x * r
        n = xh * nw_ref[...]
        h_ref[...] = (n * (1.0 + scale) + shift).astype(BF16)
        dyb = ((0.5 * gate) * dout_v).astype(BF16)
        dy_ref[...] = dyb
        dmod_ref[g, pl.ds(k0 + 2, 1), :] += 0.5 * jnp.sum(dout_v * y_ref[...].astype(F32), axis=0, keepdims=True)
        dh = jnp.zeros((tm, d), F32)
        for c in range(nch):
            sl = slice(c * fc, (c + 1) * fc)
            dg = _dot_nt(dyb, w2_ref[sl, :])
            av = a_ref[:, sl].astype(F32)
            bv = b_ref[:, sl].astype(F32)
            sig = _sigmoid(av)
            sa = av * sig
            g_ref[:, sl] = (sa * bv).astype(BF16)
            dab = (dg * bv * (sig * (1.0 + av * (1.0 - sig)))).astype(BF16)
            dbb = (dg * sa).astype(BF16)
            da_ref[:, sl] = dab
            db_ref[:, sl] = dbb
            dh = dh + _dot(dab, w1_ref[sl, :]) + _dot(dbb, w3_ref[sl, :])
        dmod_ref[g, pl.ds(k0, 1), :] += jnp.sum(dh, axis=0, keepdims=True)
        dmod_ref[g, pl.ds(k0 + 1, 1), :] += jnp.sum(dh * n, axis=0, keepdims=True)
        dn = dh * (1.0 + scale)
        dnw_ref[...] += jnp.sum(dn * xh, axis=0, keepdims=True)
        dxh = dn * nw_ref[...]
        dx_ref[...] = dout_v + r * (dxh - xh * jnp.mean(dxh * xh, axis=-1, keepdims=True))

    td = pl.BlockSpec((tm, d), tiles.row)
    tf = pl.BlockSpec((tm, f), tiles.row)
    lat = pl.BlockSpec((tm, d), tiles.lat_row)
    ta = t + tc
    return _hosted_call(
        body, name, (tiles.n_all,),
        [td, lat] + ([pl.BlockSpec((tm, d), tiles.ctx_row)] if tc else [])
        + [tf, tf, td, _whole(mod.shape), _whole(nw.shape), _whole(w1.shape), _whole(w3.shape), _whole(w2.shape)],
        [lat, td, tf, tf, tf, td, pl.BlockSpec(mod.shape, lambda i: (0, 0, 0)), pl.BlockSpec((1, d), lambda i: (0, 0))],
        [_sds((t, d), F32), _sds((ta, d), BF16), _sds((ta, f), BF16), _sds((ta, f), BF16), _sds((ta, f), BF16),
         _sds((ta, d), BF16), _sds(mod.shape, F32), _sds((1, d), F32)],
        (dout, x_lat) + ((x_ctx,) if tc else ()) + (a, b, y, mod, nw, w1, w3, w2), exch=exch)


def _mm_tn(a, b, rows, name, exch=None):
    m = a.shape[1]
    n = b.shape[1]
    bm = _div_tile(m, 1408, LANES)
    bn = _div_tile(n, 1408, LANES)
    bk = _div_tile(rows, 2304, LANES)
    nk = rows // bk

    def body(a_ref, b_ref, o_ref, acc_ref):
        k = pl.program_id(2)

        @pl.when(k == 0)
        def _():
            acc_ref[...] = jnp.zeros_like(acc_ref)

        acc_ref[...] += _dot_tn(a_ref[...], b_ref[...])

        @pl.when(k == nk - 1)
        def _():
            o_ref[...] = acc_ref[...].astype(BF16)

    (out,), got = _hosted_call(
        body, name, (m // bm, n // bn, nk),
        [pl.BlockSpec((bk, bm), lambda i, j, k: (k, i)), pl.BlockSpec((bk, bn), lambda i, j, k: (k, j))],
        [pl.BlockSpec((bm, bn), lambda i, j, k: (i, j))], [_sds((m, n), BF16)], (a, b),
        scratch=[pltpu.VMEM((bm, bn), F32)], exch=exch)
    return out if exch is None else (out, got)


def _mixin_fwd(xs, mod, nw, wp, s, nb, tm):
    t, d = xs.shape

    def body(x_ref, mod_ref, nw_ref, wp_ref, h_ref, p_ref):
        g = jnp.minimum((pl.program_id(0) * tm) // s, nb)
        shift = mod_ref[g, pl.ds(3, 1), :]
        scale = mod_ref[g, pl.ds(4, 1), :]
        x = x_ref[...]
        r = lax.rsqrt(jnp.mean(x * x, axis=-1, keepdims=True) + EPS)
        hb = ((x * r * nw_ref[...]) * (1.0 + scale) + shift).astype(BF16)
        h_ref[...] = hb
        p_ref[...] = _dot_nt(hb, wp_ref[...]).astype(BF16)

    row = lambda i: (i, 0)
    return pl.pallas_call(
        body, name="mixin_fwd", grid=(t // tm,),
        in_specs=[pl.BlockSpec((tm, d), row), _whole(mod.shape), _whole(nw.shape), _whole(wp.shape)],
        out_specs=[pl.BlockSpec((tm, d), row), pl.BlockSpec((tm, PROJ_COLS), row)],
        out_shape=[_sds((t, d), BF16), _sds((t, PROJ_COLS), BF16)], compiler_params=_params(1),
    )(xs, mod, nw, wp)


def _mixin_bwd(dp0, duv, xs, dres, mod, nw, wp, s, nb, tm, swap):
    t_all, d = xs.shape
    nlat = dres.shape[0] // tm

    def body(p0_ref, uv_ref, x_ref, dr_ref, mod_ref, nw_ref, wp_ref, dx_ref, dmod_ref, dnw_ref):
        i = pl.program_id(0)

        @pl.when(i == 0)
        def _():
            dmod_ref[...] = jnp.zeros_like(dmod_ref)
            dnw_ref[...] = jnp.zeros_like(dnw_ref)

        lat = i < nlat
        g = jnp.minimum((i * tm) // s, nb)
        scale = mod_ref[g, pl.ds(4, 1), :]
        dh = _dot(p0_ref[...], wp_ref[0:512, :])
        extra = _dot(uv_ref[...], wp_ref[512:1536, :])
        dh = dh + jnp.where(lat, extra, 0.0)
        x = x_ref[...]
        r = lax.rsqrt(jnp.mean(x * x, axis=-1, keepdims=True) + EPS)
        xh = x * r
        n = xh * nw_ref[...]
        dmod_ref[g, pl.ds(3, 1), :] += jnp.sum(dh, axis=0, keepdims=True)
        dmod_ref[g, pl.ds(4, 1), :] += jnp.sum(dh * n, axis=0, keepdims=True)
        dn = dh * (1.0 + scale)
        dnw_ref[...] += jnp.sum(dn * xh, axis=0, keepdims=True)
        dxh = dn * nw_ref[...]
        dx_ref[...] = jnp.where(lat, dr_ref[...], 0.0) + r * (dxh - xh * jnp.mean(dxh * xh, axis=-1, keepdims=True))

    row = lambda i: (i, 0)
    lrow = lambda i: (jnp.minimum(i, nlat - 1), 0)
    return _hosted_call(
        body, "mixin_bwd", (t_all // tm,),
        [pl.BlockSpec((tm, 512), row), pl.BlockSpec((tm, 1024), lrow), pl.BlockSpec((tm, d), row),
         pl.BlockSpec((tm, d), lrow), _whole(mod.shape), _whole(nw.shape), _whole(wp.shape)],
        [pl.BlockSpec((tm, d), row), pl.BlockSpec(mod.shape, lambda i: (0, 0, 0)), pl.BlockSpec((1, d), lambda i: (0, 0))],
        [_sds((t_all, d), F32), _sds(mod.shape, F32), _sds((1, d), F32)], (dp0, duv, xs, dres, mod, nw, wp), swap=swap)


def _prep_fwd(proj, row0, nb, s, pos0, sk, key0, into, tabs, wq, wk, wv, kvaw, qaw, qnw, knw, tm, with_q, name):
    nblk = s // tm
    n_into = 0 if into is None else 2

    def body(p_ref, cos_ref, sa_ref, sb_ref, wq_ref, wk_ref, wv_ref, kvaw_ref, qaw_ref, qnw_ref, knw_ref, *rest):
        outs, heads_ref = rest[n_into:-1], rest[-1]
        q_ref, k_ref, v_ref = outs if with_q else (None,) + outs
        cos, sin_a, sin_b = cos_ref[...][None], sa_ref[...][None], sb_ref[...][None]

        def normed_roped(w_ref, src, extra, nw_ref, o_ref, post):
            for h in range(HEADS):
                heads_ref[h] = _dot_nt(src, w_ref[h]) if extra is None else _dot(src, w_ref[h])
            xp = heads_ref[...] if extra is None else heads_ref[...] + extra[None]
            r = lax.rsqrt(jnp.sum(xp * xp, axis=-1, keepdims=True) * (1.0 / QK_HEAD) + EPS)
            o_ref[...] = _rope3(xp * r * (nw_ref[...] * post)[None], cos, sin_a, sin_b).astype(BF16)

        ckv = p_ref[:, 0:128].astype(F32)
        rkv = lax.rsqrt(jnp.mean(ckv * ckv, axis=-1, keepdims=True) + EPS)
        ckvb = (ckv * rkv * kvaw_ref[...]).astype(BF16)
        normed_roped(wk_ref, ckvb, p_ref[:, 128:256].astype(F32), knw_ref, k_ref, 1.0)
        for j in range(HEADS // 2):
            v_ref[j] = _dot(ckvb, wv_ref[j]).astype(BF16)
        if with_q:
            cq = p_ref[:, 256:512].astype(F32)
            rq = lax.rsqrt(jnp.mean(cq * cq, axis=-1, keepdims=True) + EPS)
            normed_roped(wq_ref, (cq * rq * qaw_ref[...]).astype(BF16), None, qnw_ref, q_ref, SOFTMAX_SCALE)

    tab = pl.BlockSpec((tm, HEAD_PAD), lambda i: (pos0 + i % nblk, 0))
    qspec = pl.BlockSpec((None, HEADS, tm, HEAD_PAD), lambda i: (i // nblk, 0, i % nblk, 0))
    kspec = pl.BlockSpec((None, HEADS, tm, HEAD_PAD), lambda i: (i // nblk, 0, key0 + i % nblk, 0))
    vspec = pl.BlockSpec((None, HEADS // 2, tm, HEAD_PAD), lambda i: (i // nblk, 0, key0 + i % nblk, 0))
    qshape = _sds((nb, HEADS, s, HEAD_PAD), BF16)
    kshape = _sds((nb, HEADS, sk, HEAD_PAD), BF16)
    vshape = _sds((nb, HEADS // 2, sk, HEAD_PAD), BF16)
    n_q = 1 if with_q else 0
    return pl.pallas_call(
        body, name=name, grid=(nb * nblk,),
        in_specs=[pl.BlockSpec((tm, 512), lambda i: (row0 + i, 0)), tab, tab, tab, _whole(wq.shape), _whole(wk.shape),
                  _whole(wv.shape), _whole(kvaw.shape), _whole(qaw.shape), _whole(qnw.shape), _whole(knw.shape)]
        + [pl.BlockSpec(memory_space=pl.ANY)] * n_into,
        out_specs=([qspec] if with_q else []) + [kspec, vspec],
        out_shape=([qshape] if with_q else []) + [kshape, vshape],
        scratch_shapes=[pltpu.VMEM((HEADS, tm, HEAD_PAD), F32)],
        input_output_aliases={11: n_q, 12: n_q + 1} if n_into else {}, compiler_params=_params(1),
    )(proj, *tabs, wq, wk, wv, kvaw, qaw, qnw, knw, *(into or ()))


def _prep_bwd(proj, row0, nb, s, pos0, key0, dp_rows, dp_into, tabs, wq, wk, wv, kvaw, qaw, qnw, knw, dq, dk, dv, init, tm,
              name):
    nblk = s // tm
    with_q = dq is not None
    n_init = 0 if init is None else len(init)
    n_into = 0 if dp_into is None else 1

    def body(*refs):
        p_ref, cos_ref, sa_ref, sb_ref, wq_ref, wk_ref, wv_ref, kvaw_ref, qaw_ref, qnw_ref, knw_ref = refs[:11]
        rest = list(refs[11:])
        dq_ref = rest.pop(0) if with_q else None
        dk_ref, dv_ref = rest.pop(0), rest.pop(0)
        init_refs = [rest.pop(0) for _ in range(n_init)]
        if n_into:
            rest.pop(0)
        dp_ref = rest.pop(0)
        if with_q:
            dwq_ref, dqaw_ref, dqnw_ref = rest.pop(0), rest.pop(0), rest.pop(0)
        dwk_ref, dwv_ref, dkvaw_ref, dknw_ref, heads_ref, dhb_ref, dkr_ref = rest
        accs = [dwk_ref, dwv_ref, dkvaw_ref, dknw_ref]

        @pl.when(pl.program_id(0) == 0)
        def _():
            for k, acc in enumerate(accs):
                acc[...] = init_refs[k][...] if n_init else jnp.zeros_like(acc)
            if with_q:
                dwq_ref[...] = jnp.zeros_like(dwq_ref)
                dqaw_ref[...] = jnp.zeros_like(dqaw_ref)
                dqnw_ref[...] = jnp.zeros_like(dqnw_ref)

        cos, sin_a, sin_b = cos_ref[...][None], sa_ref[...][None], sb_ref[...][None]
        lane = lax.broadcasted_iota(jnp.int32, (tm, HEAD_PAD), 1)
        rope_lanes = (lane >= QK_NOPE) & (lane < QK_HEAD)

        def heads_bwd(w_ref, src, extra, nw_ref, d_ref, dnw_ref, dw_ref, post):
            w_t = extra is None
            for h in range(HEADS):
                heads_ref[h] = _dot_nt(src, w_ref[h]) if w_t else _dot(src, w_ref[h])
            xp = heads_ref[...] if extra is None else heads_ref[...] + extra[None]
            r = lax.rsqrt(jnp.sum(xp * xp, axis=-1, keepdims=True) * (1.0 / QK_HEAD) + EPS)
            xh = xp * r
            dn = _rope3_t(d_ref[...].astype(F32), cos, sin_a, sin_b)
            dnw_ref[...] += post * jnp.sum(jnp.sum(dn * xh, axis=0), axis=0, keepdims=True)
            dxh = dn * (nw_ref[...] * post)[None]
            dxp = r * (dxh - xh * (jnp.sum(dxh * xh, axis=-1, keepdims=True) * (1.0 / QK_HEAD)))
            dhb_ref[...] = dxp.astype(BF16)
            dsrc = jnp.zeros((tm, src.shape[1]), F32)
            for h in range(HEADS):
                dsrc = dsrc + (_dot(dhb_ref[h], w_ref[h]) if w_t else _dot_nt(dhb_ref[h], w_ref[h]))
                dw_ref[h] += _dot_tn(src, dhb_ref[h])
            return dsrc, jnp.sum(dxp, axis=0)

        ckv = p_ref[:, 0:128].astype(F32)
        rkv = lax.rsqrt(jnp.mean(ckv * ckv, axis=-1, keepdims=True) + EPS)
        ckvh = ckv * rkv
        ckvb = (ckvh * kvaw_ref[...]).astype(BF16)
        for h in range(HEADS):
            dkr_ref[h] = dk_ref[h].astype(F32).T
        dckv, dkp_sum = heads_bwd(wk_ref, ckvb, p_ref[:, 128:256].astype(F32), knw_ref, dkr_ref, dknw_ref, dwk_ref,
                                  1.0)
        for j in range(HEADS // 2):
            dvb = dv_ref[j].astype(F32).T.astype(BF16)
            dckv = dckv + _dot_nt(dvb, wv_ref[j])
            dwv_ref[j] += _dot_tn(ckvb, dvb)
        dkvaw_ref[...] += jnp.sum(dckv * ckvh, axis=0, keepdims=True)
        dch = dckv * kvaw_ref[...]
        dp_ref[:, 0:128] = (rkv * (dch - ckvh * jnp.mean(dch * ckvh, axis=-1, keepdims=True))).astype(BF16)
        dp_ref[:, 128:256] = jnp.where(rope_lanes, dkp_sum, 0.0).astype(BF16)
        if with_q:
            cq = p_ref[:, 256:512].astype(F32)
            rq = lax.rsqrt(jnp.mean(cq * cq, axis=-1, keepdims=True) + EPS)
            cqh = cq * rq
            cqb = (cqh * qaw_ref[...]).astype(BF16)
            dcq, _ = heads_bwd(wq_ref, cqb, None, qnw_ref, dq_ref, dqnw_ref, dwq_ref, SOFTMAX_SCALE)
            dqaw_ref[...] += jnp.sum(dcq * cqh, axis=0, keepdims=True)
            dqc = dcq * qaw_ref[...]
            dp_ref[:, 256:512] = (rq * (dqc - cqh * jnp.mean(dqc * cqh, axis=-1, keepdims=True))).astype(BF16)
        else:
            dp_ref[:, 256:512] = jnp.zeros((tm, Q_LORA), BF16)

    tab = pl.BlockSpec((tm, HEAD_PAD), lambda i: (pos0 + i % nblk, 0))
    qspec = pl.BlockSpec((None, HEADS, tm, HEAD_PAD), lambda i: (i // nblk, 0, i % nblk, 0))
    kspec = pl.BlockSpec((None, HEADS, HEAD_PAD, tm), lambda i: (i // nblk, 0, 0, key0 + i % nblk))
    vspec = pl.BlockSpec((None, HEADS // 2, HEAD_PAD, tm), lambda i: (i // nblk, 0, 0, key0 + i % nblk))

    def acc_spec(shape):
        nd = len(shape)
        return pl.BlockSpec(shape, lambda i: (0,) * nd)

    acc_shapes = [(HEADS, KV_LORA, HEAD_PAD), (HEADS // 2, KV_LORA, HEAD_PAD), (1, KV_LORA), (1, HEAD_PAD)]
    q_shapes = [(HEADS, Q_LORA, HEAD_PAD), (1, Q_LORA), (1, HEAD_PAD)] if with_q else []
    out_shapes = [(dp_rows, 512)] + q_shapes + acc_shapes
    n_before = 11 + (1 if with_q else 0) + 2 + n_init
    return pl.pallas_call(
        body, name=name, grid=(nb * nblk,),
        in_specs=[pl.BlockSpec((tm, 512), lambda i: (row0 + i, 0)), tab, tab, tab, _whole(wq.shape), _whole(wk.shape),
                  _whole(wv.shape), _whole(kvaw.shape), _whole(qaw.shape), _whole(qnw.shape), _whole(knw.shape)]
        + ([qspec] if with_q else []) + [kspec, vspec] + [_whole(a.shape) for a in (init or [])]
        + [pl.BlockSpec(memory_space=pl.ANY)] * n_into,
        out_specs=[pl.BlockSpec((tm, 512), lambda i: (row0 + i, 0))] + [acc_spec(sh) for sh in q_shapes + acc_shapes],
        out_shape=[_sds(out_shapes[0], BF16)] + [_sds(sh, F32) for sh in out_shapes[1:]],
        scratch_shapes=[pltpu.VMEM((HEADS, tm, HEAD_PAD), F32), pltpu.VMEM((HEADS, tm, HEAD_PAD), BF16),
                        pltpu.VMEM((HEADS, tm, HEAD_PAD), F32)],
        input_output_aliases={n_before: 0} if n_into else {}, compiler_params=_params(1),
    )(proj, *tabs, wq, wk, wv, kvaw, qaw, qnw, knw, *([dq] if with_q else []), dk, dv, *(init or []),
      *([dp_into] if n_into else []))


def _attn_fwd(q, k, v, tq, exch=None):
    nb, _, s, _ = q.shape
    sk = k.shape[2]
    nq = s // tq

    def body(q_ref, k_ref, v_ref, o_ref, lse_ref, vext_ref):
        @pl.when(pl.program_id(2) == 0)
        def _():
            vext_ref[:, 0:HEAD_PAD] = v_ref[...]
            vext_ref[:, HEAD_PAD:2 * HEAD_PAD] = jnp.ones((sk, HEAD_PAD), BF16)

        lane = lax.broadcasted_iota(jnp.int32, (tq, HEAD_PAD), 1)
        outs = []
        for hh in range(2):
            sc = _dot_nt(q_ref[hh], k_ref[hh])
            m = jnp.max(sc, axis=-1, keepdims=True)
            pv = _dot(jnp.exp2(sc - m).astype(BF16), vext_ref[...])
            l = pv[:, HEAD_PAD:HEAD_PAD + 1]
            outs.append(pv[:, 0:HEAD_PAD] / l)
            lse_ref[hh] = m + jnp.log2(l)
        o_ref[...] = jnp.where(lane < V_HEAD, outs[0], outs[1]).astype(BF16)

    (o, lse), got = _hosted_call(
        body, "attn_fwd", (nb, HEADS // 2, nq),
        [pl.BlockSpec((None, 2, tq, HEAD_PAD), lambda b, j, i: (b, j, i, 0)),
         pl.BlockSpec((None, 2, sk, HEAD_PAD), lambda b, j, i: (b, j, 0, 0)),
         pl.BlockSpec((None, None, sk, HEAD_PAD), lambda b, j, i: (b, j, 0, 0))],
        [pl.BlockSpec((tq, HEAD_PAD), lambda b, j, i: (b * nq + i, j)),
         pl.BlockSpec((None, 2, tq, 1), lambda b, j, i: (b, j, i, 0))],
        [_sds((nb * s, MLA_W + GMLP_W), BF16), _sds((nb, HEADS, s, 1), F32)], (q, k, v),
        scratch=[pltpu.VMEM((sk, 2 * HEAD_PAD), BF16)], exch=exch)
    return o, lse, got


def _attn_bwd(q, k, v, do, o, lse, tq, exch=None):
    nb, _, s, _ = q.shape
    sk = k.shape[2]
    nq = s // tq

    def body(q_ref, k_ref, v_ref, do_ref, o_ref, lse_ref, dq_ref, dk_out, dv_out, dkt_ref, dvt_ref):
        @pl.when(pl.program_id(2) == 0)
        def _():
            dkt_ref[...] = jnp.zeros_like(dkt_ref)
            dvt_ref[...] = jnp.zeros_like(dvt_ref)

        lane = lax.broadcasted_iota(jnp.int32, (tq, HEAD_PAD), 1)
        dov = do_ref[...]
        prod = dov.astype(F32) * o_ref[...].astype(F32)
        for hh in range(2):
            mine = (lane < V_HEAD) if hh == 0 else (lane >= V_HEAD)
            doh = jnp.where(mine, dov, jnp.zeros_like(dov))
            delta = jnp.sum(jnp.where(mine, prod, 0.0), axis=-1, keepdims=True)
            qh = q_ref[hh]
            q_ln2 = (qh.astype(F32) * LN2).astype(BF16)
            kv = k_ref[hh]
            p = jnp.exp2(_dot_nt(qh, kv) - lse_ref[hh])
            u = (p * (_dot_nt(doh, v_ref[...]) - delta)).astype(BF16)
            dq_ref[hh] = (_dot(u, kv) * LN2).astype(BF16)
            dkt_ref[hh] += _dot_tn(q_ln2, u)
            dvt_ref[...] += _dot_tn(doh, p.astype(BF16))

        @pl.when(pl.program_id(2) == nq - 1)
        def _():
            dk_out[...] = dkt_ref[...].astype(BF16)
            dv_out[...] = dvt_ref[...].astype(BF16)

    qspec = pl.BlockSpec((None, 2, tq, HEAD_PAD), lambda b, j, i: (b, j, i, 0))
    kspec = pl.BlockSpec((None, 2, sk, HEAD_PAD), lambda b, j, i: (b, j, 0, 0))
    vspec = pl.BlockSpec((None, None, sk, HEAD_PAD), lambda b, j, i: (b, j, 0, 0))
    ospec = pl.BlockSpec((tq, HEAD_PAD), lambda b, j, i: (b * nq + i, j))
    return _hosted_call(
        body, "attn_bwd", (nb, HEADS // 2, nq),
        [qspec, kspec, vspec, ospec, ospec, pl.BlockSpec((None, 2, tq, 1), lambda b, j, i: (b, j, i, 0))],
        [qspec, pl.BlockSpec((None, 2, HEAD_PAD, sk), lambda b, j, i: (b, j, 0, 0)),
         pl.BlockSpec((None, None, HEAD_PAD, sk), lambda b, j, i: (b, j, 0, 0))],
        [_sds(q.shape, BF16), _sds((nb, HEADS, HEAD_PAD, sk), BF16), _sds((nb, HEADS // 2, HEAD_PAD, sk), BF16)],
        (q, k, v, do, o, lse), scratch=[pltpu.VMEM((2, HEAD_PAD, sk), F32), pltpu.VMEM((HEAD_PAD, sk), F32)], exch=exch)


def _group_masks(rows):
    lane = lax.broadcasted_iota(jnp.int32, (rows, GMLP_W), 1)
    return [(lane >= g * GROUP_DIM) & (lane < (g + 1) * GROUP_DIM) for g in range(GROUPS)]


def _gmlp_fwd(proj, mixcat, wcat, bias, vnw, ones, tm):
    t = mixcat.shape[0]

    def body(u_ref, v_ref, wcat_ref, bias_ref, vnw_ref, ones_ref, _, o_ref):
        masks = _group_masks(CHUNK)
        gv = _gelu(v_ref[...].astype(F32))
        rv = lax.rsqrt(_group_sum(gv * gv, ones_ref) * (1.0 / GROUP_DIM) + EPS)
        vnb = (gv * rv * vnw_ref[...]).astype(BF16)
        for c in range(tm // CHUNK):
            rows = slice(c * CHUNK, (c + 1) * CHUNK)
            vc = vnb[rows]
            stack = jnp.concatenate([jnp.where(m, vc, jnp.zeros_like(vc)) for m in masks], axis=0)
            sp = _dot(wcat_ref[...], stack) + bias_ref[...]
            o_ref[rows, :] = (_gelu(u_ref[rows, :].astype(F32)) * sp).astype(BF16)

    return pl.pallas_call(
        body, name="gmlp_fwd", grid=(t // tm,),
        in_specs=[pl.BlockSpec((tm, GMLP_W), lambda i: (i, 1)), pl.BlockSpec((tm, GMLP_W), lambda i: (i, 2)),
                  _whole(wcat.shape), _whole(bias.shape), _whole(vnw.shape), _whole(ones.shape),
                  pl.BlockSpec(memory_space=pl.ANY)],
        out_specs=pl.BlockSpec((tm, GMLP_W), lambda i: (i, 1)),
        out_shape=_sds(mixcat.shape, BF16), input_output_aliases={6: 0}, compiler_params=_params(1),
    )(proj, proj, wcat, bias, vnw, ones, mixcat)


def _gmlp_bwd(proj, dsg, wcat, wcat_t, bias, vnw, ones, tm):
    t = dsg.shape[0]

    def body(u_ref, v_ref, dsg_ref, wcat_ref, wcatt_ref, bias_ref, vnw_ref, ones_ref,
             duv_ref, dws_ref, dbs_ref, dvnw_ref):
        @pl.when(pl.program_id(0) == 0)
        def _():
            dws_ref[...] = jnp.zeros_like(dws_ref)
            dbs_ref[...] = jnp.zeros_like(dbs_ref)
            dvnw_ref[...] = jnp.zeros_like(dvnw_ref)

        masks = _group_masks(CHUNK)
        v = v_ref[...].astype(F32)
        gv = _gelu(v)
        rv = lax.rsqrt(_group_sum(gv * gv, ones_ref) * (1.0 / GROUP_DIM) + EPS)
        xh = gv * rv
        vnb = (xh * vnw_ref[...]).astype(BF16)
        dvn_parts = []
        for c in range(tm // CHUNK):
            rows = slice(c * CHUNK, (c + 1) * CHUNK)
            vc = vnb[rows]
            stack = jnp.concatenate([jnp.where(m, vc, jnp.zeros_like(vc)) for m in masks], axis=0)
            sp = _dot(wcat_ref[...], stack) + bias_ref[...]
            u = u_ref[rows, :].astype(F32)
            dsg_c = dsg_ref[rows, :].astype(F32)
            duv_ref[rows, 0:GMLP_W] = (dsg_c * sp * _gelu_grad(u)).astype(BF16)
            ds = dsg_c * _gelu(u)
            dstack = jnp.concatenate([jnp.where(m, ds, 0.0) for m in masks], axis=0)
            dbs_ref[...] += jnp.broadcast_to(jnp.sum(dstack, axis=-1, keepdims=True), dbs_ref.shape)
            dstb = dstack.astype(BF16)
            dvn_parts.append(_dot(wcatt_ref[...], dstb))
            dws_ref[...] += _dot_nt(dstb, vc)
        dvn = jnp.concatenate(dvn_parts, axis=0) if len(dvn_parts) > 1 else dvn_parts[0]
        dvnw_ref[...] += jnp.sum(dvn * xh, axis=0, keepdims=True)
        dxh = dvn * vnw_ref[...]
        gm = _group_sum(dxh * xh, ones_ref) * (1.0 / GROUP_DIM)
        duv_ref[:, GMLP_W:2 * GMLP_W] = (rv * (dxh - xh * gm) * _gelu_grad(v)).astype(BF16)

    row = pl.BlockSpec((tm, GMLP_W), lambda i: (i, 0))
    return pl.pallas_call(
        body, name="gmlp_bwd", grid=(t // tm,),
        in_specs=[pl.BlockSpec((tm, GMLP_W), lambda i: (i, 1)), pl.BlockSpec((tm, GMLP_W), lambda i: (i, 2)), row,
                  _whole(wcat.shape), _whole(wcat_t.shape), _whole(bias.shape), _whole(vnw.shape), _whole(ones.shape)],
        out_specs=[pl.BlockSpec((tm, 2 * GMLP_W), lambda i: (i, 0)), pl.BlockSpec((GROUPS * CHUNK, CHUNK), lambda i: (0, 0)),
                   pl.BlockSpec((GROUPS * CHUNK, CHUNK), lambda i: (0, 0)), pl.BlockSpec((1, GMLP_W), lambda i: (0, 0))],
        out_shape=[_sds((t, 2 * GMLP_W), BF16), _sds((GROUPS * CHUNK, CHUNK), F32), _sds((GROUPS * CHUNK, CHUNK), F32),
                   _sds((1, GMLP_W), F32)],
        compiler_params=_params(1),
    )(proj, proj, dsg, wcat, wcat_t, bias, vnw, ones)


def _mixout_fwd(mixcat, xs, mod, wout, s, tm):
    t, width = mixcat.shape
    d = xs.shape[1]

    def body(cat_ref, x_ref, mod_ref, w_ref, x2_ref, mix_ref):
        g = (pl.program_id(0) * tm) // s
        gate = mod_ref[g, pl.ds(5, 1), :]
        mix = _dot(cat_ref[...], w_ref[...])
        x2_ref[...] = x_ref[...] + gate * mix
        mix_ref[...] = mix.astype(BF16)

    row = lambda i: (i, 0)
    return pl.pallas_call(
        body, name="mixout_fwd", grid=(t // tm,),
        in_specs=[pl.BlockSpec((tm, width), row), pl.BlockSpec((tm, d), row), _whole(mod.shape), _whole(wout.shape)],
        out_specs=[pl.BlockSpec((tm, d), row), pl.BlockSpec((tm, d), row)],
        out_shape=[_sds((t, d), F32), _sds((t, d), BF16)], compiler_params=_params(1),
    )(mixcat, xs, mod, wout)


def _mixout_bwd(dx2, mix, mod, wout, s, tm):
    t, d = dx2.shape

    def body(dx_ref, mix_ref, mod_ref, w_ref, dmix_ref, do_ref, dsg_ref, dmod_ref):
        i = pl.program_id(0)

        @pl.when(i == 0)
        def _():
            dmod_ref[...] = jnp.zeros_like(dmod_ref)

        g = (i * tm) // s
        gate = mod_ref[g, pl.ds(5, 1), :]
        dx = dx_ref[...]
        dmod_ref[g, pl.ds(5, 1), :] += jnp.sum(dx * mix_ref[...].astype(F32), axis=0, keepdims=True)
        dmb = (gate * dx).astype(BF16)
        dmix_ref[...] = dmb
        do_ref[...] = _dot_nt(dmb, w_ref[0:MLA_W, :]).astype(BF16)
        dsg_ref[...] = _dot_nt(dmb, w_ref[MLA_W:MLA_W + GMLP_W, :]).astype(BF16)

    row = lambda i: (i, 0)
    return pl.pallas_call(
        body, name="mixout_bwd", grid=(t // tm,),
        in_specs=[pl.BlockSpec((tm, d), row), pl.BlockSpec((tm, d), row), _whole(mod.shape), _whole(wout.shape)],
        out_specs=[pl.BlockSpec((tm, d), row), pl.BlockSpec((tm, MLA_W), row), pl.BlockSpec((tm, GMLP_W), row),
                   pl.BlockSpec(mod.shape, lambda i: (0, 0, 0))],
        out_shape=[_sds((t, d), BF16), _sds((t, MLA_W), BF16), _sds((t, GMLP_W), BF16), _sds(mod.shape, F32)],
        compiler_params=_params(1),
    )(dx2, mix, mod, wout)


def _swap_cores(parts, name):
    n = len(parts)

    def body(*refs):
        srcs, outs, send_sems, recv_sems = refs[:n], refs[n:2 * n], refs[2 * n], refs[2 * n + 1]
        x, y, c = lax.axis_index("x"), lax.axis_index("y"), lax.axis_index("c")
        copies = [pltpu.make_async_remote_copy(
            src_ref=srcs[w], dst_ref=outs[w], send_sem=send_sems.at[w], recv_sem=recv_sems.at[w],
            device_id=(x, y, 1 - c), device_id_type=pl.DeviceIdType.MESH) for w in range(n)]
        for cp in copies:
            cp.start()
        for cp in copies:
            cp.wait()

    any_spec = pl.BlockSpec(memory_space=pl.ANY)
    return pl.pallas_call(
        body, name=name, in_specs=[any_spec] * n, out_specs=[any_spec] * n,
        out_shape=[_sds(p.shape, p.dtype) for p in parts],
        scratch_shapes=[pltpu.SemaphoreType.DMA((n,)), pltpu.SemaphoreType.DMA((n,))],
    )(*parts)


def _row_tile(r, c, mult):
    return _div_tile(r, max(mult, (1 << 18) // c), mult)


def _sum_slots(recv, name):
    return _sum_slots_many([recv], name)[0]


def _sum_slots_many(recvs, name):
    n = len(recvs)
    _, r, c = recvs[0].shape
    assert all(a.shape == recvs[0].shape for a in recvs)
    tr = _row_tile(r, c * n, 16)

    def body(*refs):
        for r_ref, o_ref in zip(refs[:n], refs[n:]):
            f = lambda k: r_ref[k].astype(F32)
            o_ref[...] = ((f(0) + f(1)) + f(2)) + f(3)

    return pl.pallas_call(
        body, name=name, grid=(r // tr,),
        in_specs=[pl.BlockSpec((N_CHIPS, tr, c), lambda i: (0, i, 0))] * n,
        out_specs=[pl.BlockSpec((tr, c), lambda i: (i, 0))] * n,
        out_shape=[_sds((r, c), F32)] * n, compiler_params=_params(1),
    )(*recvs)


def _adamw(parts, w, m, v, name, exch=None, swap=None):
    r, wd = w.shape
    tr = _row_tile(r, wd, 8)
    c1 = 1.0 / (1.0 - ADAM_B1 ** ADAM_STEP)
    c2 = 1.0 / (1.0 - ADAM_B2 ** ADAM_STEP)
    n_p = len(parts)

    def body(*refs):
        p_refs = refs[:n_p]
        w_ref, m_ref, v_ref, g_ref, d_ref, nm_ref, nv_ref = refs[n_p:]
        g = p_refs[0][...]
        for p_ref in p_refs[1:]:
            g = g + p_ref[...]
        nm = ADAM_B1 * m_ref[...] + (1.0 - ADAM_B1) * g
        nv = ADAM_B2 * v_ref[...] + (1.0 - ADAM_B2) * (g * g)
        g_ref[...] = g
        nm_ref[...] = nm
        nv_ref[...] = nv
        d_ref[...] = -ADAM_LR * ((nm * c1) / (jnp.sqrt(nv * c2) + ADAM_EPS) + ADAM_WD * w_ref[...])

    spec = pl.BlockSpec((tr, wd), lambda i: (i, 0))
    return _hosted_call(body, name, (r // tr,), [spec] * (n_p + 3), [spec] * 4, [_sds((r, wd), F32)] * 4,
                        (*parts, w, m, v), exch=exch, swap=swap)


def _all_peers(x, y, c):
    flips = [(dx, dy, dc) for dx in (0, 1) for dy in (0, 1) for dc in (0, 1)][1:]
    return [(1 - x if dx else x, 1 - y if dy else y, 1 - c if dc else c) for dx, dy, dc in flips]


def _first_exchange(shards, later, cc, w, b):
    n_w, n_l = len(shards), len(later)
    n = w.shape[1]

    def body(*refs):
        src32, later_in, (cc_ref, w_ref, b_ref) = refs[:n_w], refs[n_w:n_w + n_l], refs[n_w + n_l:n_w + n_l + 3]
        o0 = n_w + n_l + 3
        outs, (all_ref, tab_ref), later_out = refs[o0:o0 + n_w], refs[o0 + n_w:o0 + n_w + 2], refs[o0 + n_w + 2:o0 + n_w + 2 + n_l]
        s0 = o0 + n_w + 2 + n_l
        srcs = refs[s0:s0 + n_w]
        (part_ref, ici_send, ici_recv, d2d_send, d2d_recv, local_sems, cc_send, cc_recv, tab_send,
         tab_recv) = refs[s0 + n_w:]
        for wi in range(n_w):
            srcs[wi][...] = src32[wi][...].astype(BF16)
        x, y, c = lax.axis_index("x"), lax.axis_index("y"), lax.axis_index("c")
        chip, dev = 2 * x + y, 4 * x + 2 * y + c
        chips = _other_chips(x, y)
        peers = _all_peers(x, y, c)

        def half(wi, which):
            hr = shards[wi].shape[0] // 2
            return pl.ds(pl.multiple_of(which * hr, 16), hr)

        def over_ici(wi, k, arriving):
            px, py = chips[k]
            slot = 2 * px + py if arriving else chip
            return pltpu.make_async_remote_copy(
                src_ref=srcs[wi].at[half(wi, c)], dst_ref=outs[wi].at[slot, half(wi, c)],
                send_sem=ici_send.at[3 * wi + k], recv_sem=ici_recv.at[3 * wi + k], device_id=(px, py, c),
                device_id_type=pl.DeviceIdType.MESH)

        def to_sibling(wi, k, arriving):
            px, py = chips[k]
            rows = half(wi, 1 - c if arriving else c)
            return pltpu.make_async_remote_copy(
                src_ref=outs[wi].at[2 * px + py, rows], dst_ref=outs[wi].at[2 * px + py, rows],
                send_sem=d2d_send.at[3 * wi + k], recv_sem=d2d_recv.at[3 * wi + k], device_id=(x, y, 1 - c),
                device_id_type=pl.DeviceIdType.MESH)

        def cc_copy(k, peer, slot):
            return pltpu.make_async_remote_copy(
                src_ref=cc_ref, dst_ref=all_ref.at[slot], send_sem=cc_send.at[k], recv_sem=cc_recv.at[k],
                device_id=peer, device_id_type=pl.DeviceIdType.MESH)

        def rows_of(px, py):
            return part_ref.at[pl.ds(pl.multiple_of((4 * px + 2 * py + c) * MOD_ROWS, MOD_ROWS), MOD_ROWS)]

        def tab_copy(k, px, py, slot):
            return pltpu.make_async_remote_copy(
                src_ref=rows_of(px, py), dst_ref=tab_ref.at[slot], send_sem=tab_send.at[k], recv_sem=tab_recv.at[k],
                device_id=(px, py, c), device_id_type=pl.DeviceIdType.MESH)

        local = [pltpu.make_async_copy(srcs[wi], outs[wi].at[chip], local_sems.at[wi]) for wi in range(n_w)]
        for cp in local:
            cp.start()
        pairs = [(wi, k) for wi in range(n_w) for k in range(3)]
        for wi, k in pairs:
            over_ici(wi, k, False).start()
        for k, peer in enumerate(peers):
            cc_copy(k, peer, dev).start()
        all_ref[dev] = cc_ref[...]
        for k, (px, py, pc) in enumerate(peers):
            cc_copy(k, (px, py, pc), 4 * px + 2 * py + pc).wait_recv()
        cv = all_ref[...].reshape(8 * MOD_ROWS, cc.shape[1])
        part_ref[...] = _dot((cv * _sigmoid(cv)).astype(BF16), w_ref[...]) + b_ref[...]
        for k, (px, py) in enumerate(chips):
            tab_copy(k, px, py, chip).start()
        tab_ref[chip] = rows_of(x, y)[...]
        for k, (px, py) in enumerate(chips):
            tab_copy(k, px, py, 2 * px + py).wait_recv()
        for j in range(n_l):
            later_out[j][...] = later_in[j][...].astype(BF16)
        for wi, k in pairs:
            over_ici(wi, k, True).wait_recv()
            to_sibling(wi, k, False).start()
        for wi, k in pairs:
            to_sibling(wi, k, True).wait_recv()
        for wi, k in pairs:
            over_ici(wi, k, False).wait_send()
            to_sibling(wi, k, False).wait_send()
        for k, peer in enumerate(peers):
            cc_copy(k, peer, dev).wait_send()
        for k, (px, py) in enumerate(chips):
            tab_copy(k, px, py, chip).wait_send()
        for cp in local:
            cp.wait()

    any_spec = pl.BlockSpec(memory_space=pl.ANY)
    vmem = pl.BlockSpec(memory_space=pltpu.VMEM)
    sems3 = pltpu.SemaphoreType.DMA((3 * n_w,))
    got = pl.pallas_call(
        body, name="first_exchange", in_specs=[vmem] * (n_w + n_l + 3),
        out_specs=[any_spec] * n_w + [vmem] * (2 + n_l),
        out_shape=[_sds((N_CHIPS,) + a.shape, BF16) for a in shards]
        + [_sds((8,) + cc.shape, F32), _sds((N_CHIPS, MOD_ROWS, n), F32)] + [_sds(a.shape, BF16) for a in later],
        scratch_shapes=[pltpu.VMEM(a.shape, BF16) for a in shards]
        + [pltpu.VMEM((8 * MOD_ROWS, n), F32), sems3, sems3, sems3, sems3, pltpu.SemaphoreType.DMA((n_w,)),
           pltpu.SemaphoreType.DMA((7,)), pltpu.SemaphoreType.DMA((7,)), pltpu.SemaphoreType.DMA((3,)),
           pltpu.SemaphoreType.DMA((3,))],
        compiler_params=pltpu.CompilerParams(vmem_limit_bytes=V7X_VMEM_LIMIT),
    )(*shards, *later, cc, w, b)
    return got[:n_w], got[n_w], got[n_w + 1], got[n_w + 2:]


def _ada_bwd_tp(cc_all, dmods, w, ctx_row):
    d, n = w.shape

    def body(cc_ref, m0, m1, m2, m3, w_ref, dw_ref, db_ref, dctx_ref, stage_ref, all_ref, send_sems, recv_sems):
        x, y, c = lax.axis_index("x"), lax.axis_index("y"), lax.axis_index("c")
        me = 4 * x + 2 * y + c
        dsum = m0[...] + m1[...] + m2[...] + m3[...]
        db_ref[...] = jnp.sum(dsum, axis=0, keepdims=True)
        for j in range(N_CHIPS):
            stage_ref[j] = dsum[:, j * n:(j + 1) * n]

        def copy(k, peer, slot):
            px, py, _ = peer
            return pltpu.make_async_remote_copy(
                src_ref=stage_ref.at[2 * px + py], dst_ref=all_ref.at[slot], send_sem=send_sems.at[k],
                recv_sem=recv_sems.at[k], device_id=peer, device_id_type=pl.DeviceIdType.MESH)

        peers = _all_peers(x, y, c)
        for k, peer in enumerate(peers):
            copy(k, peer, me).start()
        all_ref[me] = stage_ref[2 * x + y]
        for k, (px, py, pc) in enumerate(peers):
            copy(k, (px, py, pc), 4 * px + 2 * py + pc).wait_recv()
        for k, peer in enumerate(peers):
            copy(k, peer, me).wait_send()
        cv = cc_ref[...]
        sig = _sigmoid(cv)
        dmb = all_ref[...].reshape(8 * MOD_ROWS, n).astype(BF16)
        dw_ref[...] = _dot_tn((cv * sig).astype(BF16), dmb)
        dsc = _dot_nt(dmb, w_ref[...])
        dctx = dsc[ctx_row:ctx_row + 1, :]
        for dev in range(1, 8):
            dctx = dctx + dsc[dev * MOD_ROWS + ctx_row:dev * MOD_ROWS + ctx_row + 1, :]
        cx = cv[ctx_row:ctx_row + 1, :]
        sx = sig[ctx_row:ctx_row + 1, :]
        dctx_ref[...] = dctx * (sx * (1.0 + cx * (1.0 - sx))) * jnp.where(c == 0, 1.0, 0.0)

    vmem = pl.BlockSpec(memory_space=pltpu.VMEM)
    return pl.pallas_call(
        body, name="ada_bwd_tp", in_specs=[vmem] * 6, out_specs=[vmem] * 3,
        out_shape=[_sds((d, n), F32), _sds((1, N_MOD * d), F32), _sds((1, d), F32)],
        scratch_shapes=[pltpu.VMEM((N_CHIPS, MOD_ROWS, n), F32), pltpu.VMEM((8, MOD_ROWS, n), F32),
                        pltpu.SemaphoreType.DMA((7,)), pltpu.SemaphoreType.DMA((7,))],
        compiler_params=pltpu.CompilerParams(vmem_limit_bytes=V7X_VMEM_LIMIT),
    )(cc_all, *dmods, w)


def _rope_tables(s, ctx):
    pos = np.arange(s, dtype=np.float32)
    inv = (np.float32(ROPE_BASE) ** (-np.arange(0, QK_ROPE // 2, 2, dtype=np.float32) / np.float32(QK_ROPE // 2)))
    ang_r = np.floor(pos / GRID_W)[:, None] * inv
    ang_c = (pos - GRID_W * np.floor(pos / GRID_W))[:, None] * inv
    ang = np.concatenate([ang_r, ang_r, ang_c, ang_c], axis=-1).astype(np.float32)
    cos, sin = np.cos(ang), np.sin(ang)
    half_b = (np.arange(QK_ROPE) // 8) % 2 == 1
    sin_a = np.where(half_b, sin, 0.0)
    sin_b = np.where(half_b, 0.0, -sin)

    def place(tab, fill):
        full = np.full((s + ctx, HEAD_PAD), fill, np.float32)
        full[:s, QK_NOPE:QK_HEAD] = tab
        return jnp.asarray(full)

    return place(cos, 1.0), place(sin_a, 0.0), place(sin_b, 0.0)


def _pad_last(a, n):
    return jnp.pad(a, [(0, 0)] * (a.ndim - 1) + [(0, n - a.shape[-1])])


def _flat_rows(parts, rows, width):
    flat = jnp.concatenate([p.reshape(-1) for p in parts])
    return jnp.pad(flat, (0, rows * width - flat.shape[0])).reshape(rows, width)


def kernel(x, c, ctx, c_ctx, w_ada, b_ada, norm1_w, ffn1_w1, ffn1_w3, ffn1_w2, norm2_w, w_in, q_a_norm_w, w_uq, kv_a_norm_w, w_ukv, q_norm_w, k_norm_w, v_norm_w, w_s, b_s, w_out, norm3_w, ffn2_w1, ffn2_w3, ffn2_w2, loss_target, m_c_ctx, m_w_ada, m_b_ada, m_norm1_w, m_ffn1_w1, m_ffn1_w3, m_ffn1_w2, m_norm2_w, m_w_in, m_q_a_norm_w, m_w_uq, m_kv_a_norm_w, m_w_ukv, m_q_norm_w, m_k_norm_w, m_v_norm_w, m_w_s, m_b_s, m_w_out, m_norm3_w, m_ffn2_w1, m_ffn2_w3, m_ffn2_w2, v_c_ctx, v_w_ada, v_b_ada, v_norm1_w, v_ffn1_w1, v_ffn1_w3, v_ffn1_w2, v_norm2_w, v_w_in, v_q_a_norm_w, v_w_uq, v_kv_a_norm_w, v_w_ukv, v_q_norm_w, v_k_norm_w, v_v_norm_w, v_w_s, v_b_s, v_w_out, v_norm3_w, v_ffn2_w1, v_ffn2_w3, v_ffn2_w2):
    wts = dict(c_ctx=c_ctx, w_ada=w_ada, b_ada=b_ada, norm1_w=norm1_w, ffn1_w1=ffn1_w1, ffn1_w3=ffn1_w3, ffn1_w2=ffn1_w2,
               norm2_w=norm2_w, w_in=w_in, q_a_norm_w=q_a_norm_w, w_uq=w_uq, kv_a_norm_w=kv_a_norm_w, w_ukv=w_ukv,
               q_norm_w=q_norm_w, k_norm_w=k_norm_w, v_norm_w=v_norm_w, w_s=w_s, b_s=b_s, w_out=w_out, norm3_w=norm3_w,
               ffn2_w1=ffn2_w1, ffn2_w3=ffn2_w3, ffn2_w2=ffn2_w2)
    moms = dict(c_ctx=m_c_ctx, w_ada=m_w_ada, b_ada=m_b_ada, norm1_w=m_norm1_w, ffn1_w1=m_ffn1_w1, ffn1_w3=m_ffn1_w3,
                ffn1_w2=m_ffn1_w2, norm2_w=m_norm2_w, w_in=m_w_in, q_a_norm_w=m_q_a_norm_w, w_uq=m_w_uq,
                kv_a_norm_w=m_kv_a_norm_w, w_ukv=m_w_ukv, q_norm_w=m_q_norm_w, k_norm_w=m_k_norm_w, v_norm_w=m_v_norm_w,
                w_s=m_w_s, b_s=m_b_s, w_out=m_w_out, norm3_w=m_norm3_w, ffn2_w1=m_ffn2_w1, ffn2_w3=m_ffn2_w3,
                ffn2_w2=m_ffn2_w2)
    vars_ = dict(c_ctx=v_c_ctx, w_ada=v_w_ada, b_ada=v_b_ada, norm1_w=v_norm1_w, ffn1_w1=v_ffn1_w1, ffn1_w3=v_ffn1_w3,
                 ffn1_w2=v_ffn1_w2, norm2_w=v_norm2_w, w_in=v_w_in, q_a_norm_w=v_q_a_norm_w, w_uq=v_w_uq,
                 kv_a_norm_w=v_kv_a_norm_w, w_ukv=v_w_ukv, q_norm_w=v_q_norm_w, k_norm_w=v_k_norm_w, v_norm_w=v_v_norm_w,
                 w_s=v_w_s, b_s=v_b_s, w_out=v_w_out, norm3_w=v_norm3_w, ffn2_w1=v_ffn2_w1, ffn2_w3=v_ffn2_w3,
                 ffn2_w2=v_ffn2_w2)

    nb, s, d = x.shape
    nctx = ctx.shape[1]
    t, tc = nb * s, nb * nctx
    t_all = t + tc
    sk = s + nctx
    assert nb + 1 <= MOD_ROWS and d % LANES == 0
    tm = _token_tile(s, nctx)
    tq = _div_tile(s, 512, tm)
    tmx = _div_tile(math.gcd(s, tc), 1024, tm)
    tmo = _div_tile(s, 1024, tm)

    def held(n, a_):
        return jnp.swapaxes(a_[0], 0, 1) if n in T_WEIGHTS else a_[0]

    def unheld(n, a_):
        return (jnp.swapaxes(a_, 0, 1) if n in T_WEIGHTS else a_)[None]

    shard = {"w_ada": w_ada[0].astype(BF16)}
    full = {}

    def unshard(names, blocks):
        for n, g4 in zip(names, blocks):
            _, r_, c_ = g4.shape
            if n in ROW_SHARDED or n in T_WEIGHTS:
                full[n] = g4.reshape(N_CHIPS * r_, c_)
            else:
                full[n] = g4.transpose(1, 0, 2).reshape(r_, N_CHIPS * c_)

    def chip_major(n, g_):
        if n in ROW_SHARDED or n in T_WEIGHTS:
            return g_.reshape(N_CHIPS, g_.shape[0] // N_CHIPS, g_.shape[1]).astype(BF16)
        r_, cols = g_.shape
        return g_.reshape(r_, N_CHIPS, cols // N_CHIPS).transpose(1, 0, 2).astype(BF16)

    cc = jnp.concatenate([c, c_ctx[None, :], jnp.zeros((MOD_ROWS - nb - 1, d), F32)], axis=0)
    n_ada = shard["w_ada"].shape[1]
    assert n_ada % LANES == 0
    my_chip = 2 * lax.axis_index("x") + lax.axis_index("y")
    b_cols = lax.dynamic_slice_in_dim(b_ada, my_chip * n_ada, n_ada, axis=1)
    later = MIX_WEIGHTS + LAST_WEIGHTS
    got, cc_all, table, cast = _first_exchange([held(n, wts[n]) for n in FIRST_WEIGHTS],
                                               [held(n, wts[n]) for n in later], cc, shard["w_ada"], b_cols)
    unshard(FIRST_WEIGHTS, got)
    shard.update(zip(later, cast))
    cc_all = cc_all.reshape(8 * MOD_ROWS, d)
    mod = table.transpose(1, 0, 2).reshape(MOD_ROWS, N_MOD, d)
    wsb = w_s[0].astype(BF16)
    wcat = wsb.transpose(1, 0, 2).reshape(CHUNK, GROUPS * CHUNK)
    wcat_t = wsb.transpose(2, 0, 1).reshape(CHUNK, GROUPS * CHUNK)
    bias = jnp.repeat(b_s[0].T, GROUP_DIM, axis=1)
    vnw = v_norm_w.reshape(1, GMLP_W)
    lane = jnp.arange(GMLP_W)
    ones = (lane[:, None] // GROUP_DIM == lane[None, :] // GROUP_DIM).astype(BF16)
    qnw = _pad_last(q_norm_w, HEAD_PAD)
    knw = _pad_last(k_norm_w, HEAD_PAD)
    tabs = _rope_tables(s, nctx)

    x_lat, x_ctx = x.reshape(t, d), ctx.reshape(tc, d)
    (xs1, a1, b1, y1), got = _ffn_fwd(x_lat, x_ctx, mod, norm1_w, full["ffn1_w1"], full["ffn1_w3"], full["ffn1_w2"], 0, s,
                                      nb, tm, "ffn1_fwd", exch=("gather", [shard[n] for n in MIX_WEIGHTS]))
    unshard(MIX_WEIGHTS, got)
    wi = full["w_in"]
    wp = jnp.concatenate([wi[0:KV_LORA], jnp.zeros((QK_NOPE, d), BF16), wi[KV_LORA:KV_LORA + QK_ROPE],
                          jnp.zeros((HEAD_PAD - QK_HEAD, d), BF16), wi[KV_LORA + QK_ROPE:]], axis=0)
    wq = jnp.pad(full["w_uq"].reshape(HEADS, QK_HEAD, Q_LORA), ((0, 0), (0, HEAD_PAD - QK_HEAD), (0, 0)))
    wkv = full["w_ukv"].reshape(KV_LORA, HEADS, QK_NOPE + V_HEAD)
    wk = _pad_last(wkv[:, :, :QK_NOPE].transpose(1, 0, 2), HEAD_PAD)
    wv = wkv[:, :, QK_NOPE:].reshape(KV_LORA, HEADS // 2, 2 * V_HEAD).transpose(1, 0, 2)
    h2, proj = _mixin_fwd(xs1, mod, norm2_w, wp, s, nb, tmx)
    prep_w = (wq, wk, wv, kv_a_norm_w, q_a_norm_w, qnw, knw)
    q, k_all, v_all = _prep_fwd(proj, 0, nb, s, 0, sk, 0, None, tabs, *prep_w, tmo, True, "prep_fwd")
    k_all, v_all = _prep_fwd(proj, t // tm, nb, nctx, s // tm, sk, s // tm, (k_all, v_all), tabs, *prep_w, tm, False,
                             "prep_ctx_fwd")
    o, lse, got = _attn_fwd(q, k_all, v_all, _div_tile(s, 2048, tm), exch=("gather", [shard[n] for n in LAST_WEIGHTS]))
    unshard(LAST_WEIGHTS, got)
    mixcat = _gmlp_fwd(proj, o, wcat, bias, vnw, ones, tmo)
    x2, mix = _mixout_fwd(mixcat, xs1, mod, full["w_out"], s, tmo)
    (dy, a2, b2, y2, loss_part), _ = _ffn_fwd(x2, None, mod, norm3_w, full["ffn2_w1"], full["ffn2_w3"], full["ffn2_w2"], 6,
                                              s, nb, tm, "ffn2_fwd", target=loss_target.reshape(t, d))

    grads, cm, recv = {}, {}, {}

    def scatter_of(names):
        return ("scatter", [cm[n] for n in names])

    (dx2, h3, g2, da2, db2, dyb2, dmod_c, grads["norm3_w"]), _ = _ffn_bwd(
        dy, x2, None, a2, b2, y2, mod, norm3_w, full["ffn2_w1"], full["ffn2_w3"], full["ffn2_w2"], 6, s, nb, tm,
        "ffn2_bwd")
    cm["ffn2_w1"] = chip_major("ffn2_w1", _mm_tn(da2, h3, t, "ffn2_dw1"))
    cm["ffn2_w3"] = chip_major("ffn2_w3", _mm_tn(db2, h3, t, "ffn2_dw3"))
    cm["ffn2_w2"] = chip_major("ffn2_w2", _mm_tn(g2, dyb2, t, "ffn2_dw2"))
    dmix, do, dsg, dmod_b = _mixout_bwd(dx2, mix, mod, full["w_out"], s, tmo)
    cm["w_out"] = chip_major("w_out", _mm_tn(mixcat, dmix, t, "wout_dw"))
    duv, dws, dbs, dvnw = _gmlp_bwd(proj, dsg, wcat, wcat_t, bias, vnw, ones, tmo)
    group = LAST_WEIGHTS + ("w_out",)
    (dq, dk, dv), got = _attn_bwd(q, k_all, v_all, do, mixcat, lse, tmo, exch=scatter_of(group))
    recv.update(zip(group, got))
    dp0, dwk_c, dwv_c, dkvaw_c, dknw_c = _prep_bwd(
        proj, t // tm, nb, nctx, s // tm, s // tm, t_all, None, tabs, *prep_w, None, dk, dv, None, tm, "prep_ctx_bwd")
    dp0, dwq, dqaw, dqnw, dwk, dwv, dkvaw, dknw = _prep_bwd(
        proj, 0, nb, s, 0, 0, t_all, dp0, tabs, *prep_w, dq, dk, dv, [dwk_c, dwv_c, dkvaw_c, dknw_c], tq, "prep_bwd")
    part, sib = {}, {}
    early = LAST_WEIGHTS + ("w_out",)
    def sum_group(names, tag):
        shapes = sorted({recv[n].shape for n in names})
        for k, shape in enumerate(shapes):
            same = [n for n in names if recv[n].shape == shape]
            part.update(zip(same, _sum_slots_many([recv[n] for n in same], f"sum_{tag}{k}")))

    sum_group(early, "early")
    (dxs1, dmod_a, grads["norm2_w"]), _, got = _mixin_bwd(dp0, duv, xs1, dx2, mod, norm2_w, wp, s, nb, tmx,
                                                          [part[n] for n in early])
    sib.update(zip(early, got))
    dwp = jnp.concatenate([_mm_tn(dp0, h2, t_all, "win_dw_kvq"), _mm_tn(duv, h2, t, "win_dw_uv")], axis=0)
    cm["w_in"] = chip_major("w_in", jnp.concatenate(
        [dwp[0:KV_LORA], dwp[KV_LORA + QK_NOPE:KV_LORA + QK_HEAD], dwp[256:]], axis=0))
    cm["w_uq"] = chip_major("w_uq", dwq[:, :, :QK_HEAD].transpose(0, 2, 1).reshape(HEADS * QK_HEAD, Q_LORA))
    cm["w_ukv"] = chip_major("w_ukv", jnp.concatenate(
        [dwk[:, :, :QK_NOPE].transpose(1, 0, 2),
         dwv.transpose(1, 0, 2).reshape(KV_LORA, HEADS, V_HEAD)], axis=2).reshape(KV_LORA, HEADS * (QK_NOPE + V_HEAD)))
    (dx_lat, h1, g1, da1, db1, dyb1, dmod_0, grads["norm1_w"]), _ = _ffn_bwd(
        dxs1, x_lat, x_ctx, a1, b1, y1, mod, norm1_w, full["ffn1_w1"], full["ffn1_w3"], full["ffn1_w2"], 0, s, nb, tm,
        "ffn1_bwd")
    dmods = [m_.reshape(MOD_ROWS, N_MOD * d) for m_ in (dmod_0, dmod_a, dmod_b, dmod_c)]
    dw_ada, grads["b_ada"], dctx = _ada_bwd_tp(cc_all, dmods, shard["w_ada"], nb)
    grads["c_ctx"] = dctx[0]
    grads["q_a_norm_w"], grads["kv_a_norm_w"] = dqaw, dkvaw
    grads["q_norm_w"], grads["k_norm_w"] = dqnw[:, :QK_HEAD], dknw[:, :QK_HEAD]
    grads["v_norm_w"], grads["w_s"], grads["b_s"] = dvnw, dws, dbs[:, 0]
    grad_x = dx_lat.reshape(nb, s, d)
    n_small = sum(wts[n].size for n in SMALL)
    rows_s = _round_up(-(-(n_small + 1) // d), 16)
    cm["small"] = jnp.broadcast_to(_flat_rows([grads[n] for n in SMALL] + [loss_part], rows_s, d), (N_CHIPS, rows_s, d))
    group = ("w_in", "w_uq", "w_ukv", "small")
    dw2, got = _mm_tn(g1, dyb1, t_all, "ffn1_dw2", exch=scatter_of(group))
    recv.update(zip(group, got))
    cm["ffn1_w2"] = chip_major("ffn1_w2", dw2)
    dw1, got = _mm_tn(da1, h1, t_all, "ffn1_dw1", exch=scatter_of(("ffn1_w2",)))
    recv["ffn1_w2"] = got[0]
    cm["ffn1_w1"] = chip_major("ffn1_w1", dw1)
    dw3, got = _mm_tn(db1, h1, t_all, "ffn1_dw3", exch=scatter_of(("ffn1_w1",)))
    recv["ffn1_w1"] = got[0]
    cm["ffn1_w3"] = chip_major("ffn1_w3", dw3)
    stepped = {}
    reduced = tuple(n for n in SHARDED if n != "w_ada") + ("small",)
    late = tuple(n for n in reduced if n not in early and n != "ffn1_w3")
    sum_group(late, "late")
    stepped["w_ada"], got, got_sib = _adamw([dw_ada], wts["w_ada"][0], moms["w_ada"][0], vars_["w_ada"][0],
                                            "adamw_w_ada", exch=scatter_of(("ffn1_w3",)), swap=[part[n] for n in late])
    sib.update(zip(late, got_sib))
    part["ffn1_w3"] = _sum_slots(got[0], "sum_ffn1_w3")
    sib["ffn1_w3"] = _swap_cores([part["ffn1_w3"]], "swap_last")[0]
    for n in reduced[:-1]:
        stepped[n], _ = _adamw([part[n], sib[n]], held(n, wts[n]), held(n, moms[n]), held(n, vars_[n]), "adamw_" + n)
    for n in SHARDED:
        stepped[n] = [unheld(n, a_) for a_ in stepped[n]]
    packed, _ = _adamw([part["small"], sib["small"]], _flat_rows([wts[n] for n in SMALL], rows_s, d),
                       _flat_rows([moms[n] for n in SMALL], rows_s, d), _flat_rows([vars_[n] for n in SMALL], rows_s, d),
                       "adamw_small")
    loss = packed[0].reshape(-1)[n_small]
    for n in SMALL:
        stepped[n] = []
    for a_ in packed:
        flat = a_.reshape(-1)
        off = 0
        for n in SMALL:
            stepped[n].append(flat[off:off + wts[n].size].reshape(wts[n].shape))
            off += wts[n].size
    return (loss, grad_x, *[stepped[n][0] for n in WEIGHTS], *[stepped[n][1] for n in WEIGHTS],
            *[stepped[n][2] for n in WEIGHTS], *[stepped[n][3] for n in WEIGHTS])
```
